```python
import jax, jax.numpy as jnp
from jax import lax
import numpy as np

D_MODEL = 2048
BATCH = 8
SEQ = 4096
DEPTH = 1

MLA_HEADS = 8
QK_NOPE_DIM = 128
QK_ROPE_DIM = 64
V_HEAD_DIM = 128
Q_LORA_RANK = 512
KV_LORA_RANK = 256
ROPE_THETA = 10000.0
Q_BLOCK = 128
DIL_PATTERNS = ((128, 1), (512, 4), (2048, 16))
DIL_GROUPS = 3
DIL_HEADS_PER_GROUP = 4
DIL_HEADS = DIL_GROUPS * DIL_HEADS_PER_GROUP
DIL_HEAD_DIM = 128
DIL_BLOCK = 128
ALIBI_MAX_BIAS = 8.0
D_FF = 5504
CONV_WIDTH = 3
NORM_EPS = 1e-6

MLA_Q_DIM = MLA_HEADS * (QK_NOPE_DIM + QK_ROPE_DIM)
MLA_KV_DIM = MLA_HEADS * (QK_NOPE_DIM + V_HEAD_DIM)
DIL_QKV_DIM = DIL_HEADS * DIL_HEAD_DIM
DIL_OUT_DIM = DIL_HEADS_PER_GROUP * DIL_HEAD_DIM
IN_SPLITS = (Q_LORA_RANK, KV_LORA_RANK, QK_ROPE_DIM, DIL_QKV_DIM, DIL_QKV_DIM, DIL_QKV_DIM, D_MODEL, D_MODEL)
D_IN = Q_LORA_RANK + KV_LORA_RANK + QK_ROPE_DIM + 3 * DIL_QKV_DIM + 2 * D_MODEL

kernel_name = 'hybrid_mla_dilated_convffn'


def rmsnorm(x, g):
    xf = x.astype(jnp.float32)
    y = xf * lax.rsqrt(jnp.mean(xf * xf, axis=-1, keepdims=True) + NORM_EPS)
    return (y * g.astype(jnp.float32)).astype(x.dtype)


def rope(x, cos, sin):
    half = x.shape[-1] // 2
    xf = x.astype(jnp.float32)
    x1, x2 = xf[..., :half], xf[..., half:]
    return jnp.concatenate([x1 * cos - x2 * sin, x2 * cos + x1 * sin], axis=-1).astype(x.dtype)


def mla_attention(c_q, c_kv, k_pe_raw, q_norm_g, w_uq, kv_norm_g, w_ukv):
    B, S, _ = c_q.shape
    q = (rmsnorm(c_q, q_norm_g) @ w_uq).reshape(B, S, MLA_HEADS, QK_NOPE_DIM + QK_ROPE_DIM)
    kv = (rmsnorm(c_kv, kv_norm_g) @ w_ukv).reshape(B, S, MLA_HEADS, QK_NOPE_DIM + V_HEAD_DIM)
    q_nope, q_pe = q[..., :QK_NOPE_DIM], q[..., QK_NOPE_DIM:]
    k_nope, v = kv[..., :QK_NOPE_DIM], kv[..., QK_NOPE_DIM:]
    pos = jnp.arange(S, dtype=jnp.float32)
    inv_freq = ROPE_THETA ** (-jnp.arange(0, QK_ROPE_DIM, 2, dtype=jnp.float32) / QK_ROPE_DIM)
    ang = pos[:, None] * inv_freq[None, :]
    cos, sin = jnp.cos(ang), jnp.sin(ang)
    q_pe = rope(q_pe, cos[:, None, :], sin[:, None, :])
    k_pe = rope(k_pe_raw, cos, sin)
    scale = (QK_NOPE_DIM + QK_ROPE_DIM) ** -0.5
    nb = S // Q_BLOCK
    qn_b = q_nope.reshape(B, nb, Q_BLOCK, MLA_HEADS, QK_NOPE_DIM).transpose(1, 0, 2, 3, 4)
    qp_b = q_pe.reshape(B, nb, Q_BLOCK, MLA_HEADS, QK_ROPE_DIM).transpose(1, 0, 2, 3, 4)
    kpos = jnp.arange(S)

    def one_block(args):
        qn, qp, i = args
        s = (jnp.einsum('bqhd,bkhd->bhqk', qn, k_nope).astype(jnp.float32)
             + jnp.einsum('bqhr,bkr->bhqk', qp, k_pe).astype(jnp.float32)) * scale
        qpos = i * Q_BLOCK + jnp.arange(Q_BLOCK)
        s = jnp.where(kpos[None, :] <= qpos[:, None], s, -jnp.inf)
        p = jax.nn.softmax(s, axis=-1).astype(v.dtype)
        return jnp.einsum('bhqk,bkhd->bqhd', p, v)

    o = lax.map(one_block, (qn_b, qp_b, jnp.arange(nb)))
    return o.transpose(1, 0, 2, 3, 4).reshape(B, S, MLA_HEADS * V_HEAD_DIM)


def dilated_group(q, k, v, window, dil, slopes):
    B, S, H, D = q.shape
    w_sub = window // dil
    L = S // dil
    nb = -(-L // DIL_BLOCK)
    Lp = nb * DIL_BLOCK

    def to_blocks(t):
        t = t.reshape(B, L, dil, H, D).transpose(0, 2, 1, 3, 4)
        t = jnp.pad(t, ((0, 0), (0, 0), (0, Lp - L), (0, 0), (0, 0)))
        return t.reshape(B, dil, nb, DIL_BLOCK, H, D)

    def with_prev(t):
        prev = jnp.pad(t, ((0, 0), (0, 0), (1, 0), (0, 0), (0, 0), (0, 0)))[:, :, :-1]
        return jnp.concatenate([prev, t], axis=3)

    qb = to_blocks(q)
    kk = with_prev(to_blocks(k))
    vv = with_prev(to_blocks(v))
    s = jnp.einsum('brnqhd,brnkhd->brnhqk', qb, kk).astype(jnp.float32) * (D ** -0.5)
    p_idx = jnp.arange(DIL_BLOCK)
    k_idx = jnp.arange(2 * DIL_BLOCK)
    j = p_idx[:, None] + DIL_BLOCK - k_idx[None, :]
    valid = (j >= 0) & (j <= w_sub)
    first = jnp.arange(nb) == 0
    valid = valid[None] & ~(first[:, None, None] & (k_idx < DIL_BLOCK)[None, None, :])
    alibi = -slopes.astype(jnp.float32)[:, None, None] * (dil * j).astype(jnp.float32)[None]
    s = jnp.where(valid[None, None, :, None], s + alibi[None, None, None], -jnp.inf)
    lse = jax.nn.logsumexp(s, axis=-1)
    p = jnp.exp(s - lse[..., None]).astype(v.dtype)
    o = jnp.einsum('brnhqk,brnkhd->brnqhd', p, vv)

    def from_blocks(t):
        t = t.reshape((B, dil, Lp) + t.shape[4:])[:, :, :L]
        t = jnp.moveaxis(t, 1, 2)
        return t.reshape((B, S) + t.shape[3:])

    return from_blocks(o), from_blocks(lse.transpose(0, 1, 2, 4, 3))


def dilated_attention(dq, dk, dv):
    B, S, _ = dq.shape
    shp = (B, S, DIL_GROUPS, DIL_HEADS_PER_GROUP, DIL_HEAD_DIM)
    q, k, v = dq.reshape(shp), dk.reshape(shp), dv.reshape(shp)
    slopes = 2.0 ** (-ALIBI_MAX_BIAS * jnp.arange(1, DIL_HEADS + 1, dtype=jnp.float32) / DIL_HEADS)
    slopes = slopes.reshape(DIL_GROUPS, DIL_HEADS_PER_GROUP)
    outs, lses = [], []
    for g, (window, dil) in enumerate(DIL_PATTERNS):
        o_g, l_g = dilated_group(q[:, :, g], k[:, :, g], v[:, :, g], window, dil, slopes[g])
        outs.append(o_g)
        lses.append(l_g)
    o = jnp.stack(outs, axis=0)
    wts = jax.nn.softmax(jnp.stack(lses, axis=0), axis=0)
    out = jnp.sum(wts[..., None] * o.astype(jnp.float32), axis=0).astype(dq.dtype)
    return out.reshape(B, S, DIL_OUT_DIM)


def causal_dwconv(u, w, b):
    S = u.shape[1]
    upad = jnp.pad(u, ((0, 0), (CONV_WIDTH - 1, 0), (0, 0)))
    out = b
    for t in range(CONV_WIDTH):
        out = out + w[t] * upad[:, t:t + S]
    return out


def _fwd_setup_inputs(seed: int = 0) -> dict:
    key = jax.random.key(seed)
    ks = jax.random.split(key, 17)

    def w(k, shape, fan_in):
        return jax.random.normal(k, shape, jnp.float32) * (fan_in ** -0.5)

    def gain(k, shape):
        return 1.0 + 0.02 * jax.random.normal(k, shape, jnp.float32)

    return {
        'x': jax.random.normal(ks[0], (BATCH, SEQ, D_MODEL), jnp.float32),
        'attn_norm_g': gain(ks[1], (DEPTH, D_MODEL)),
        'w_in': w(ks[2], (DEPTH, D_MODEL, D_IN), D_MODEL),
        'b_gate': 0.02 * jax.random.normal(ks[3], (DEPTH, 2 * D_MODEL), jnp.float32),
        'q_norm_g': gain(ks[4], (DEPTH, Q_LORA_RANK)),
        'w_uq': w(ks[5], (DEPTH, Q_LORA_RANK, MLA_Q_DIM), Q_LORA_RANK),
        'kv_norm_g': gain(ks[6], (DEPTH, KV_LORA_RANK)),
        'w_ukv': w(ks[7], (DEPTH, KV_LORA_RANK, MLA_KV_DIM), KV_LORA_RANK),
        'w_o_mla': w(ks[8], (DEPTH, MLA_HEADS * V_HEAD_DIM, D_MODEL), MLA_HEADS * V_HEAD_DIM),
        'w_o_dil': w(ks[9], (DEPTH, DIL_OUT_DIM, D_MODEL), DIL_OUT_DIM),
        'w_out': w(ks[10], (DEPTH, D_MODEL, D_MODEL), D_MODEL),
        'ffn_norm_g': gain(ks[11], (DEPTH, D_MODEL)),
        'w_up': w(ks[12], (DEPTH, D_MODEL, 2 * D_FF), D_MODEL),
        'conv_w': w(ks[13], (DEPTH, CONV_WIDTH, 2 * D_FF), CONV_WIDTH),
        'conv_b': 0.02 * jax.random.normal(ks[14], (DEPTH, 2 * D_FF), jnp.float32),
        'w_down': w(ks[15], (DEPTH, D_FF, D_MODEL), D_FF),
        'final_norm_g': gain(ks[16], (D_MODEL,)),
    }


def _fwd_reference(x, attn_norm_g, w_in, b_gate, q_norm_g, w_uq, kv_norm_g, w_ukv, w_o_mla, w_o_dil,
              w_out, ffn_norm_g, w_up, conv_w, conv_b, w_down, final_norm_g):
    split_at = [int(c) for c in np.cumsum(IN_SPLITS)[:-1]]
    for l in range(DEPTH):
        h = rmsnorm(x, attn_norm_g[l])
        proj = h @ w_in[l]
        c_q, c_kv, k_pe, dq, dk, dv, ga, gb = jnp.split(proj, split_at, axis=-1)
        gate_a = jax.nn.sigmoid(ga + b_gate[l, :D_MODEL])
        gate_b = jax.nn.sigmoid(gb + b_gate[l, D_MODEL:])
        o_a = mla_attention(c_q, c_kv, k_pe, q_norm_g[l], w_uq[l], kv_norm_g[l], w_ukv[l]) @ w_o_mla[l]
        o_b = dilated_attention(dq, dk, dv) @ w_o_dil[l]
        x = x + (gate_a * o_a + gate_b * o_b) @ w_out[l]
        h2 = rmsnorm(x, ffn_norm_g[l])
        u = causal_dwconv(h2 @ w_up[l], conv_w[l], conv_b[l])
        up, gate = u[..., :D_FF], u[..., D_FF:]
        x = x + (jax.nn.silu(gate) * up) @ w_down[l]
    return rmsnorm(x, final_norm_g)


import jax as _jax
import jax.numpy as _jnp

TWIN_FORMAT = 'train_step'
FWD_PARAMS = ['x', 'attn_norm_g', 'w_in', 'b_gate', 'q_norm_g', 'w_uq', 'kv_norm_g', 'w_ukv', 'w_o_mla', 'w_o_dil', 'w_out', 'ffn_norm_g', 'w_up', 'conv_w', 'conv_b', 'w_down', 'final_norm_g']
TWIN_WEIGHTS = ['attn_norm_g', 'w_in', 'b_gate', 'q_norm_g', 'w_uq', 'kv_norm_g', 'w_ukv', 'w_o_mla', 'w_o_dil', 'w_out', 'ffn_norm_g', 'w_up', 'conv_w', 'conv_b', 'w_down', 'final_norm_g']
TWIN_DIFF_INPUT = 'x'
TWIN_INPUTS = ['x', 'attn_norm_g', 'w_in', 'b_gate', 'q_norm_g', 'w_uq', 'kv_norm_g', 'w_ukv', 'w_o_mla', 'w_o_dil', 'w_out', 'ffn_norm_g', 'w_up', 'conv_w', 'conv_b', 'w_down', 'final_norm_g', 'loss_target', 'm_attn_norm_g', 'm_w_in', 'm_b_gate', 'm_q_norm_g', 'm_w_uq', 'm_kv_norm_g', 'm_w_ukv', 'm_w_o_mla', 'm_w_o_dil', 'm_w_out', 'm_ffn_norm_g', 'm_w_up', 'm_conv_w', 'm_conv_b', 'm_w_down', 'm_final_norm_g', 'v_attn_norm_g', 'v_w_in', 'v_b_gate', 'v_q_norm_g', 'v_w_uq', 'v_kv_norm_g', 'v_w_ukv', 'v_w_o_mla', 'v_w_o_dil', 'v_w_out', 'v_ffn_norm_g', 'v_w_up', 'v_conv_w', 'v_conv_b', 'v_w_down', 'v_final_norm_g']
TWIN_OUTPUTS = ['loss', 'grad_x', 'grad_attn_norm_g', 'grad_w_in', 'grad_b_gate', 'grad_q_norm_g', 'grad_w_uq', 'grad_kv_norm_g', 'grad_w_ukv', 'grad_w_o_mla', 'grad_w_o_dil', 'grad_w_out', 'grad_ffn_norm_g', 'grad_w_up', 'grad_conv_w', 'grad_conv_b', 'grad_w_down', 'grad_final_norm_g', 'delta_attn_norm_g', 'delta_w_in', 'delta_b_gate', 'delta_q_norm_g', 'delta_w_uq', 'delta_kv_norm_g', 'delta_w_ukv', 'delta_w_o_mla', 'delta_w_o_dil', 'delta_w_out', 'delta_ffn_norm_g', 'delta_w_up', 'delta_conv_w', 'delta_conv_b', 'delta_w_down', 'delta_final_norm_g', 'new_m_attn_norm_g', 'new_m_w_in', 'new_m_b_gate', 'new_m_q_norm_g', 'new_m_w_uq', 'new_m_kv_norm_g', 'new_m_w_ukv', 'new_m_w_o_mla', 'new_m_w_o_dil', 'new_m_w_out', 'new_m_ffn_norm_g', 'new_m_w_up', 'new_m_conv_w', 'new_m_conv_b', 'new_m_w_down', 'new_m_final_norm_g', 'new_v_attn_norm_g', 'new_v_w_in', 'new_v_b_gate', 'new_v_q_norm_g', 'new_v_w_uq', 'new_v_kv_norm_g', 'new_v_w_ukv', 'new_v_w_o_mla', 'new_v_w_o_dil', 'new_v_w_out', 'new_v_ffn_norm_g', 'new_v_w_up', 'new_v_conv_w', 'new_v_conv_b', 'new_v_w_down', 'new_v_final_norm_g']
TWIN_LEAF_KINDS = {'loss': 'loss', 'grad_x': 'grad_x', 'grad_attn_norm_g': 'grad_w', 'grad_w_in': 'grad_w', 'grad_b_gate': 'grad_w', 'grad_q_norm_g': 'grad_w', 'grad_w_uq': 'grad_w', 'grad_kv_norm_g': 'grad_w', 'grad_w_ukv': 'grad_w', 'grad_w_o_mla': 'grad_w', 'grad_w_o_dil': 'grad_w', 'grad_w_out': 'grad_w', 'grad_ffn_norm_g': 'grad_w', 'grad_w_up': 'grad_w', 'grad_conv_w': 'grad_w', 'grad_conv_b': 'grad_w', 'grad_w_down': 'grad_w', 'grad_final_norm_g': 'grad_w', 'delta_attn_norm_g': 'delta_w', 'delta_w_in': 'delta_w', 'delta_b_gate': 'delta_w', 'delta_q_norm_g': 'delta_w', 'delta_w_uq': 'delta_w', 'delta_kv_norm_g': 'delta_w', 'delta_w_ukv': 'delta_w', 'delta_w_o_mla': 'delta_w', 'delta_w_o_dil': 'delta_w', 'delta_w_out': 'delta_w', 'delta_ffn_norm_g': 'delta_w', 'delta_w_up': 'delta_w', 'delta_conv_w': 'delta_w', 'delta_conv_b': 'delta_w', 'delta_w_down': 'delta_w', 'delta_final_norm_g': 'delta_w', 'new_m_attn_norm_g': 'new_m', 'new_m_w_in': 'new_m', 'new_m_b_gate': 'new_m', 'new_m_q_norm_g': 'new_m', 'new_m_w_uq': 'new_m', 'new_m_kv_norm_g': 'new_m', 'new_m_w_ukv': 'new_m', 'new_m_w_o_mla': 'new_m', 'new_m_w_o_dil': 'new_m', 'new_m_w_out': 'new_m', 'new_m_ffn_norm_g': 'new_m', 'new_m_w_up': 'new_m', 'new_m_conv_w': 'new_m', 'new_m_conv_b': 'new_m', 'new_m_w_down': 'new_m', 'new_m_final_norm_g': 'new_m', 'new_v_attn_norm_g': 'new_v', 'new_v_w_in': 'new_v', 'new_v_b_gate': 'new_v', 'new_v_q_norm_g': 'new_v', 'new_v_w_uq': 'new_v', 'new_v_kv_norm_g': 'new_v', 'new_v_w_ukv': 'new_v', 'new_v_w_o_mla': 'new_v', 'new_v_w_o_dil': 'new_v', 'new_v_w_out': 'new_v', 'new_v_ffn_norm_g': 'new_v', 'new_v_w_up': 'new_v', 'new_v_conv_w': 'new_v', 'new_v_conv_b': 'new_v', 'new_v_w_down': 'new_v', 'new_v_final_norm_g': 'new_v'}


def _forward(args):
    return _fwd_reference(*[args[k] for k in FWD_PARAMS])


def _output_shape():
    def fwd():
        inp = _fwd_setup_inputs(0)
        return _fwd_reference(*[inp[k] for k in FWD_PARAMS])
    out = _jax.eval_shape(fwd)
    return out.shape, out.dtype

N_MICROBATCH = 1
ADAM_LR = 0.001
ADAM_B1 = 0.9
ADAM_B2 = 0.999
ADAM_EPS = 1e-08
ADAM_WD = 0.01
ADAM_STEP = 10
PER_EXAMPLE_BATCH_AXIS = {'x': 0, 'loss_target': 0}
SHARED_INPUTS = []
_WEIGHT_DTYPES = {'attn_norm_g': _jnp.float32, 'w_in': _jnp.float32, 'b_gate': _jnp.float32, 'q_norm_g': _jnp.float32, 'w_uq': _jnp.float32, 'kv_norm_g': _jnp.float32, 'w_ukv': _jnp.float32, 'w_o_mla': _jnp.float32, 'w_o_dil': _jnp.float32, 'w_out': _jnp.float32, 'ffn_norm_g': _jnp.float32, 'w_up': _jnp.float32, 'conv_w': _jnp.float32, 'conv_b': _jnp.float32, 'w_down': _jnp.float32, 'final_norm_g': _jnp.float32}
MOMENT_SCALE = {'attn_norm_g': 3.294880e-02, 'w_in': 1.528351e-02, 'b_gate': 5.728125e-03, 'q_norm_g': 1.896300e-02, 'w_uq': 1.096352e-02, 'kv_norm_g': 4.249861e-02, 'w_ukv': 1.393742e-02, 'w_o_mla': 1.140400e-02, 'w_o_dil': 1.736454e-02, 'w_out': 2.079611e-02, 'ffn_norm_g': 6.597127e-02, 'w_up': 2.882690e-02, 'conv_w': 2.833868e-02, 'conv_b': 2.893748e-02, 'w_down': 4.652450e-02, 'final_norm_g': 1.600399e+01}


def _to_microbatches(a, axis):
    t = _jnp.moveaxis(a, axis, 0)
    t = t.reshape((N_MICROBATCH, t.shape[0] // N_MICROBATCH) + t.shape[1:])
    return _jnp.moveaxis(t, 1, axis + 1)


def setup_inputs(seed: int = 0) -> dict:
    inp = _fwd_setup_inputs(seed)
    key = _jax.random.fold_in(_jax.random.key(seed), 7919)
    shape, _ = _output_shape()
    out = dict(inp)
    out["loss_target"] = _jax.random.normal(_jax.random.fold_in(key, 0), shape, _jnp.float32)
    for i, name in enumerate(TWIN_WEIGHTS):
        w = inp[name].astype(_jnp.float32)
        if MOMENT_SCALE is None:
            s = _jnp.sqrt(_jnp.mean(_jnp.square(w)) + 1e-30)
        else:
            s = MOMENT_SCALE[name]
        km, kv = _jax.random.split(_jax.random.fold_in(key, i + 1))
        out[name] = w
        out["m_" + name] = s * _jax.random.normal(km, w.shape, _jnp.float32)
        out["v_" + name] = (s * s) * _jax.random.uniform(kv, w.shape, _jnp.float32, 0.5, 1.5)
    if N_MICROBATCH > 1:
        for name, axis in PER_EXAMPLE_BATCH_AXIS.items():
            out[name] = _to_microbatches(out[name], axis)
    return {'x': out['x'], 'attn_norm_g': out['attn_norm_g'], 'w_in': out['w_in'], 'b_gate': out['b_gate'], 'q_norm_g': out['q_norm_g'], 'w_uq': out['w_uq'], 'kv_norm_g': out['kv_norm_g'], 'w_ukv': out['w_ukv'], 'w_o_mla': out['w_o_mla'], 'w_o_dil': out['w_o_dil'], 'w_out': out['w_out'], 'ffn_norm_g': out['ffn_norm_g'], 'w_up': out['w_up'], 'conv_w': out['conv_w'], 'conv_b': out['conv_b'], 'w_down': out['w_down'], 'final_norm_g': out['final_norm_g'], 'loss_target': out['loss_target'], 'm_attn_norm_g': out['m_attn_norm_g'], 'm_w_in': out['m_w_in'], 'm_b_gate': out['m_b_gate'], 'm_q_norm_g': out['m_q_norm_g'], 'm_w_uq': out['m_w_uq'], 'm_kv_norm_g': out['m_kv_norm_g'], 'm_w_ukv': out['m_w_ukv'], 'm_w_o_mla': out['m_w_o_mla'], 'm_w_o_dil': out['m_w_o_dil'], 'm_w_out': out['m_w_out'], 'm_ffn_norm_g': out['m_ffn_norm_g'], 'm_w_up': out['m_w_up'], 'm_conv_w': out['m_conv_w'], 'm_conv_b': out['m_conv_b'], 'm_w_down': out['m_w_down'], 'm_final_norm_g': out['m_final_norm_g'], 'v_attn_norm_g': out['v_attn_norm_g'], 'v_w_in': out['v_w_in'], 'v_b_gate': out['v_b_gate'], 'v_q_norm_g': out['v_q_norm_g'], 'v_w_uq': out['v_w_uq'], 'v_kv_norm_g': out['v_kv_norm_g'], 'v_w_ukv': out['v_w_ukv'], 'v_w_o_mla': out['v_w_o_mla'], 'v_w_o_dil': out['v_w_o_dil'], 'v_w_out': out['v_w_out'], 'v_ffn_norm_g': out['v_ffn_norm_g'], 'v_w_up': out['v_w_up'], 'v_conv_w': out['v_conv_w'], 'v_conv_b': out['v_conv_b'], 'v_w_down': out['v_w_down'], 'v_final_norm_g': out['v_final_norm_g']}


def _loss(weights, diff, rest, loss_target):
    with _jax.named_scope("forward"):
        args = {**rest, TWIN_DIFF_INPUT: diff, **{k: w.astype(_WEIGHT_DTYPES[k]) for k, w in weights.items()}}
        y = _forward(args)
    with _jax.named_scope("loss_head"):
        err = _jnp.square(y.astype(_jnp.float32) - loss_target)
        return 0.5 * _jnp.sum(_jnp.mean(err, axis=-1)) if err.ndim else 0.5 * err


def _adamw(w, g, m, v):
    m = ADAM_B1 * m + (1.0 - ADAM_B1) * g
    v = ADAM_B2 * v + (1.0 - ADAM_B2) * _jnp.square(g)
    m_hat = m / (1.0 - ADAM_B1 ** ADAM_STEP)
    v_hat = v / (1.0 - ADAM_B2 ** ADAM_STEP)
    delta = -ADAM_LR * (m_hat / (_jnp.sqrt(v_hat) + ADAM_EPS) + ADAM_WD * w)
    return delta, m, v


def reference(x, attn_norm_g, w_in, b_gate, q_norm_g, w_uq, kv_norm_g, w_ukv, w_o_mla, w_o_dil, w_out, ffn_norm_g, w_up, conv_w, conv_b, w_down, final_norm_g, loss_target, m_attn_norm_g, m_w_in, m_b_gate, m_q_norm_g, m_w_uq, m_kv_norm_g, m_w_ukv, m_w_o_mla, m_w_o_dil, m_w_out, m_ffn_norm_g, m_w_up, m_conv_w, m_conv_b, m_w_down, m_final_norm_g, v_attn_norm_g, v_w_in, v_b_gate, v_q_norm_g, v_w_uq, v_kv_norm_g, v_w_ukv, v_w_o_mla, v_w_o_dil, v_w_out, v_ffn_norm_g, v_w_up, v_conv_w, v_conv_b, v_w_down, v_final_norm_g):
    given = dict(x=x, attn_norm_g=attn_norm_g, w_in=w_in, b_gate=b_gate, q_norm_g=q_norm_g, w_uq=w_uq, kv_norm_g=kv_norm_g, w_ukv=w_ukv, w_o_mla=w_o_mla, w_o_dil=w_o_dil, w_out=w_out, ffn_norm_g=ffn_norm_g, w_up=w_up, conv_w=conv_w, conv_b=conv_b, w_down=w_down, final_norm_g=final_norm_g, loss_target=loss_target, m_attn_norm_g=m_attn_norm_g, m_w_in=m_w_in, m_b_gate=m_b_gate, m_q_norm_g=m_q_norm_g, m_w_uq=m_w_uq, m_kv_norm_g=m_kv_norm_g, m_w_ukv=m_w_ukv, m_w_o_mla=m_w_o_mla, m_w_o_dil=m_w_o_dil, m_w_out=m_w_out, m_ffn_norm_g=m_ffn_norm_g, m_w_up=m_w_up, m_conv_w=m_conv_w, m_conv_b=m_conv_b, m_w_down=m_w_down, m_final_norm_g=m_final_norm_g, v_attn_norm_g=v_attn_norm_g, v_w_in=v_w_in, v_b_gate=v_b_gate, v_q_norm_g=v_q_norm_g, v_w_uq=v_w_uq, v_kv_norm_g=v_kv_norm_g, v_w_ukv=v_w_ukv, v_w_o_mla=v_w_o_mla, v_w_o_dil=v_w_o_dil, v_w_out=v_w_out, v_ffn_norm_g=v_ffn_norm_g, v_w_up=v_w_up, v_conv_w=v_conv_w, v_conv_b=v_conv_b, v_w_down=v_w_down, v_final_norm_g=v_final_norm_g)
    weights = {n: given[n] for n in TWIN_WEIGHTS}
    shared = {n: given[n] for n in SHARED_INPUTS}
    per_example = {n: given[n] for n in ['x']}
    grad_fn = _jax.value_and_grad(_loss, argnums=(0, 1))

    def one_microbatch(ex, loss_target):
        ex = dict(ex)
        diff = ex.pop(TWIN_DIFF_INPUT)
        return grad_fn(weights, diff, {**shared, **ex}, loss_target)

    if N_MICROBATCH == 1:
        loss, (grad_w, grad_x) = one_microbatch(per_example, given["loss_target"])
    else:
        def body(carry, xs):
            loss_sum, grad_sum = carry
            l_k, (gw_k, gx_k) = one_microbatch(xs[0], xs[1])
            with _jax.named_scope("update"):
                return (loss_sum + l_k, _jax.tree.map(_jnp.add, grad_sum, gw_k)), gx_k

        init = (_jnp.zeros((), _jnp.float32), _jax.tree.map(_jnp.zeros_like, weights))
        (loss, grad_w), grad_x = _jax.lax.scan(body, init, (per_example, given["loss_target"]))
    with _jax.named_scope("update"):
        delta_w, new_m, new_v = {}, {}, {}
        for n in TWIN_WEIGHTS:
            delta_w[n], new_m[n], new_v[n] = _adamw(weights[n], grad_w[n], given["m_" + n], given["v_" + n])
    return (loss, grad_x, *[grad_w[n] for n in TWIN_WEIGHTS], *[delta_w[n] for n in TWIN_WEIGHTS],
            *[new_m[n] for n in TWIN_WEIGHTS], *[new_v[n] for n in TWIN_WEIGHTS])
```

```python
import functools
import math

import numpy as np
import jax
import jax.numpy as jnp
from jax import lax
from jax.experimental import pallas as pl
from jax.experimental.pallas import tpu as pltpu

F32 = jnp.float32
BF = jnp.bfloat16
MESH = pl.DeviceIdType.MESH

D_MODEL = 2048
MLA_HEADS = 8
QK_NOPE = 128
QK_ROPE = 64
Q_RANK = 512
KV_RANK = 256
ROPE_THETA = 10000.0
DIL_PATTERNS = ((128, 1), (512, 4), (2048, 16))
DIL_GROUPS = 3
DIL_HPG = 4
DIL_HEADS = 12
HEAD = 128
DIL_BLOCK = 128
ALIBI_MAX_BIAS = 8.0
NORM_EPS = 1e-6
N_CHIPS = 4
ADAM_LR = 0.001
ADAM_B1 = 0.9
ADAM_B2 = 0.999
ADAM_EPS = 1e-08
ADAM_WD = 0.01
ADAM_STEP = 10

LANE = 128
VMEM_LIMIT = 56 * 1024 * 1024
MLA_SCALE = (QK_NOPE + QK_ROPE) ** -0.5
DIL_SCALE = HEAD ** -0.5


def _params(*sem):
    return pltpu.CompilerParams(dimension_semantics=sem, vmem_limit_bytes=VMEM_LIMIT)


def _tile(n, pref):
    t = (pref // LANE) * LANE
    while t >= LANE:
        if n % t == 0:
            return t
        t -= LANE
    return n


NN = (((1,), (0,)), ((), ()))
NT = (((1,), (1,)), ((), ()))
TN = (((0,), (0,)), ((), ()))


def _mm_call(name, a, b, add, *, grid, a_spec, b_spec, add_spec, o_spec, o_shape, o_dtype, acc_shape, dims, nk):
    nax = len(grid)

    def body(*refs):
        if add is None:
            a_ref, b_ref, o_ref = refs[:3]
            c_ref = None
            scr = refs[3:]
        else:
            a_ref, b_ref, c_ref, o_ref = refs[:4]
            scr = refs[4:]
        prod = lax.dot_general(a_ref[...].astype(BF), b_ref[...].astype(BF), dims, preferred_element_type=F32)
        if nk == 1:
            if c_ref is not None:
                prod = prod + c_ref[...]
            o_ref[...] = prod.astype(o_ref.dtype)
        else:
            acc = scr[0]
            k = pl.program_id(nax - 1)

            @pl.when(k == 0)
            def _():
                if c_ref is not None:
                    acc[...] = prod + c_ref[...]
                else:
                    acc[...] = prod

            @pl.when(k > 0)
            def _():
                acc[...] += prod

            @pl.when(k == nk - 1)
            def _():
                o_ref[...] = acc[...].astype(o_ref.dtype)

    ins = [a, b] + ([] if add is None else [add])
    specs = [a_spec, b_spec] + ([] if add is None else [add_spec])
    sem = ("parallel",) * (nax - 1) + ("arbitrary",)
    return pl.pallas_call(
        body, name=name, grid=grid, in_specs=specs, out_specs=o_spec,
        out_shape=jax.ShapeDtypeStruct(o_shape, o_dtype),
        scratch_shapes=[] if nk == 1 else [pltpu.VMEM(acc_shape, F32)],
        compiler_params=_params(*sem),
    )(*ins)


def _mm_nn(name, a, b, *, add=None, o_dtype=F32):
    M, K = a.shape
    sharded = b.ndim == 3
    Ns = b.shape[-1]
    N = Ns * (b.shape[0] if sharded else 1)
    tm, tn, tk = _tile(M, 512), _tile(Ns, 512), _tile(K, 2048)
    per = Ns // tn
    nk = K // tk
    if sharded:
        b_spec = pl.BlockSpec((None, tk, tn), lambda i, j, k: (j // per, k, j % per))
    else:
        b_spec = pl.BlockSpec((tk, tn), lambda i, j, k: (k, j))
    return _mm_call(
        name, a, b, add, grid=(M // tm, N // tn, nk),
        a_spec=pl.BlockSpec((tm, tk), lambda i, j, k: (i, k)), b_spec=b_spec,
        add_spec=pl.BlockSpec((tm, tn), lambda i, j, k: (i, j)),
        o_spec=pl.BlockSpec((tm, tn), lambda i, j, k: (i, j)),
        o_shape=(M, N), o_dtype=o_dtype, acc_shape=(tm, tn), dims=NN, nk=nk)


def _mm_nt(name, a, b, *, add=None, o_dtype=F32):
    M, K = a.shape
    sharded = b.ndim == 3
    N, Ks = b.shape[-2], b.shape[-1]
    tm, tn, tk = _tile(M, 512), _tile(N, 512), _tile(Ks, 2048)
    per = Ks // tk
    nk = K // tk
    if sharded:
        b_spec = pl.BlockSpec((None, tn, tk), lambda i, j, k: (k // per, j, k % per))
    else:
        b_spec = pl.BlockSpec((tn, tk), lambda i, j, k: (j, k))
    return _mm_call(
        name, a, b, add, grid=(M // tm, N // tn, nk),
        a_spec=pl.BlockSpec((tm, tk), lambda i, j, k: (i, k)), b_spec=b_spec,
        add_spec=pl.BlockSpec((tm, tn), lambda i, j, k: (i, j)),
        o_spec=pl.BlockSpec((tm, tn), lambda i, j, k: (i, j)),
        o_shape=(M, N), o_dtype=o_dtype, acc_shape=(tm, tn), dims=NT, nk=nk)


def _mm_tn(name, a, b, *, shards=1, o_dtype=BF):
    S, M = a.shape
    N = b.shape[1]
    Ns = N // shards
    tm, tn, tk = _tile(M, 1024), _tile(Ns, 1024), _tile(S, 512)
    per = Ns // tn
    nk = S // tk
    if shards > 1:
        o_spec = pl.BlockSpec((None, tm, tn), lambda i, j, k: (j // per, i, j % per))
        o_shape = (shards, M, Ns)
    else:
        o_spec = pl.BlockSpec((tm, tn), lambda i, j, k: (i, j))
        o_shape = (M, N)
    return _mm_call(
        name, a, b, None, grid=(M // tm, N // tn, nk),
        a_spec=pl.BlockSpec((tk, tm), lambda i, j, k: (k, i)),
        b_spec=pl.BlockSpec((tk, tn), lambda i, j, k: (k, j)),
        add_spec=None, o_spec=o_spec, o_shape=o_shape, o_dtype=o_dtype, acc_shape=(tm, tn), dims=TN, nk=nk)


def _up_fwd(h2, w_up):
    S, D = h2.shape
    G, _, C = w_up.shape
    tm = _tile(S, 256)
    return _mm_call(
        "up_fwd", h2, w_up, None, grid=(G, S // tm, 1),
        a_spec=pl.BlockSpec((tm, D), lambda g, i, k: (i, 0)),
        b_spec=pl.BlockSpec((None, D, C), lambda g, i, k: (g, 0, 0)),
        add_spec=None, o_spec=pl.BlockSpec((None, tm, C), lambda g, i, k: (g, i, 0)),
        o_shape=(G, S, C), o_dtype=F32, acc_shape=None, dims=NN, nk=1)


def _up_dgrad(du_pre, w_up):
    G, S, C = du_pre.shape
    D = w_up.shape[1]
    tm, tn = _tile(S, 1024), _tile(D, 1024)
    return _mm_call(
        "up_dgrad", du_pre, w_up, None, grid=(S // tm, D // tn, G),
        a_spec=pl.BlockSpec((None, tm, C), lambda i, j, g: (g, i, 0)),
        b_spec=pl.BlockSpec((None, tn, C), lambda i, j, g: (g, j, 0)),
        add_spec=None, o_spec=pl.BlockSpec((tm, tn), lambda i, j, g: (i, j)),
        o_shape=(S, D), o_dtype=F32, acc_shape=(tm, tn), dims=NT, nk=G)


def _up_wgrad(h2, du_pre):
    G, S, C = du_pre.shape
    D = h2.shape[1]
    tm, tk = _tile(D, 512), _tile(S, 512)
    return _mm_call(
        "up_wgrad", h2, du_pre, None, grid=(G, D // tm, S // tk),
        a_spec=pl.BlockSpec((tk, tm), lambda g, i, k: (k, i)),
        b_spec=pl.BlockSpec((None, tk, C), lambda g, i, k: (g, k, 0)),
        add_spec=None, o_spec=pl.BlockSpec((None, tm, C), lambda g, i, k: (g, i, 0)),
        o_shape=(G, D, C), o_dtype=BF, acc_shape=(tm, C), dims=TN, nk=S // tk)


def _down_fwd(act, w_down, x1):
    G, S, C = act.shape
    D = w_down.shape[2]
    tm, tn = _tile(S, 1024), _tile(D, 1024)
    return _mm_call(
        "down_fwd", act, w_down, x1, grid=(S // tm, D // tn, G),
        a_spec=pl.BlockSpec((None, tm, C), lambda i, j, g: (g, i, 0)),
        b_spec=pl.BlockSpec((None, C, tn), lambda i, j, g: (g, 0, j)),
        add_spec=pl.BlockSpec((tm, tn), lambda i, j, g: (i, j)),
        o_spec=pl.BlockSpec((tm, tn), lambda i, j, g: (i, j)),
        o_shape=(S, D), o_dtype=F32, acc_shape=(tm, tn), dims=NN, nk=G)


def _down_dgrad(dx2, w_down):
    S, D = dx2.shape
    G, C, _ = w_down.shape
    tm = _tile(S, 256)
    return _mm_call(
        "down_dgrad", dx2, w_down, None, grid=(G, S // tm, 1),
        a_spec=pl.BlockSpec((tm, D), lambda g, i, k: (i, 0)),
        b_spec=pl.BlockSpec((None, C, D), lambda g, i, k: (g, 0, 0)),
        add_spec=None, o_spec=pl.BlockSpec((None, tm, C), lambda g, i, k: (g, i, 0)),
        o_shape=(G, S, C), o_dtype=F32, acc_shape=None, dims=NT, nk=1)


def _down_wgrad(act, dx2):
    G, S, C = act.shape
    D = dx2.shape[1]
    tn, tk = _tile(D, 512), _tile(S, 512)
    return _mm_call(
        "down_wgrad", act, dx2, None, grid=(G, D // tn, S // tk),
        a_spec=pl.BlockSpec((None, tk, C), lambda g, j, k: (g, k, 0)),
        b_spec=pl.BlockSpec((tk, tn), lambda g, j, k: (k, j)),
        add_spec=None, o_spec=pl.BlockSpec((None, C, tn), lambda g, j, k: (g, 0, j)),
        o_shape=(G, C, D), o_dtype=BF, acc_shape=(C, tn), dims=TN, nk=S // tk)


def _row(ts, c):
    return pl.BlockSpec((ts, c), lambda i: (i, 0))


def _bcast(r, c):
    return pl.BlockSpec((r, c), lambda i: (0, 0))


def _accumulate(i, ref, val):
    @pl.when(i == 0)
    def _():
        ref[...] = val

    @pl.when(i > 0)
    def _():
        ref[...] += val


def _rstd(xv):
    return lax.rsqrt(jnp.mean(xv * xv, axis=-1, keepdims=True) + NORM_EPS)


def _rmsnorm_fwd(name, x, g):
    S, D = x.shape
    ts = _tile(S, 512)

    def body(x_ref, g_ref, o_ref):
        xv = x_ref[...]
        o_ref[...] = (xv * _rstd(xv) * g_ref[...]).astype(o_ref.dtype)

    return pl.pallas_call(
        body, name=name, grid=(S // ts,), in_specs=[_row(ts, D), _bcast(1, D)], out_specs=_row(ts, D),
        out_shape=jax.ShapeDtypeStruct((S, D), BF), compiler_params=_params("parallel"))(x, g)


def _norm_bwd_rows(dy, xv, g):
    r = _rstd(xv)
    xh = xv * r
    dxh = dy * g
    dx = r * (dxh - xh * jnp.mean(dxh * xh, axis=-1, keepdims=True))
    return dx, jnp.sum(dy * xh, axis=0, keepdims=True)


def _rmsnorm_bwd(name, dy, x, g, res):
    S, D = x.shape
    ts = _tile(S, 512)

    def body(dy_ref, x_ref, g_ref, res_ref, dx_ref, dg_ref):
        dx, dg = _norm_bwd_rows(dy_ref[...], x_ref[...], g_ref[...])
        dx_ref[...] = dx + res_ref[...]
        _accumulate(pl.program_id(0), dg_ref, dg)

    return pl.pallas_call(
        body, name=name, grid=(S // ts,),
        in_specs=[_row(ts, D), _row(ts, D), _bcast(1, D), _row(ts, D)],
        out_specs=[_row(ts, D), _bcast(1, D)],
        out_shape=[jax.ShapeDtypeStruct((S, D), F32), jax.ShapeDtypeStruct((1, D), F32)],
        compiler_params=_params("arbitrary"))(dy, x, g, res)


def _rope_tables(S):
    half = QK_ROPE // 2
    pos = jnp.arange(S, dtype=F32)
    inv_freq = ROPE_THETA ** (-jnp.arange(0, QK_ROPE, 2, dtype=F32) / QK_ROPE)
    ang = pos[:, None] * inv_freq[None, :]
    cos, sin = jnp.cos(ang), jnp.sin(ang)
    z = jnp.zeros((S, half), F32)
    return jnp.concatenate([cos, z, cos, z], axis=1), jnp.concatenate([-sin, z, sin, z], axis=1)


def _rope_lanes(x, cos, sin_signed, inverse):
    if inverse:
        return x * cos + pltpu.roll(x * sin_signed, LANE // 2, 1)
    return x * cos + pltpu.roll(x, LANE // 2, 1) * sin_signed


def _rope(name, x, cos, sin_signed, inverse):
    S, W = x.shape
    ts = _tile(S, 512)

    def body(x_ref, c_ref, s_ref, o_ref):
        c, s = c_ref[...], s_ref[...]
        for h in range(W // LANE):
            sl = slice(h * LANE, (h + 1) * LANE)
            o_ref[:, sl] = _rope_lanes(x_ref[:, sl], c, s, inverse).astype(o_ref.dtype)

    return pl.pallas_call(
        body, name=name, grid=(S // ts,), in_specs=[_row(ts, W), _row(ts, LANE), _row(ts, LANE)],
        out_specs=_row(ts, W), out_shape=jax.ShapeDtypeStruct((S, W), BF),
        compiler_params=_params("parallel"))(x, cos, sin_signed)


LAT_W = 1024
_CQ = slice(0, Q_RANK)
_CKV = slice(Q_RANK, Q_RANK + KV_RANK)
_KPE = slice(Q_RANK + KV_RANK, Q_RANK + KV_RANK + LANE)


def _mla_prep(lat, qg, kvg, cos, sin_signed):
    S = lat.shape[0]
    ts = _tile(S, 512)

    def body(lat_ref, qg_ref, kvg_ref, c_ref, s_ref, qn_ref, kvn_ref, kpe_ref):
        cq = lat_ref[:, _CQ]
        qn_ref[...] = (cq * _rstd(cq) * qg_ref[...]).astype(BF)
        ckv = lat_ref[:, _CKV]
        kvn_ref[...] = (ckv * _rstd(ckv) * kvg_ref[...]).astype(BF)
        kpe_ref[...] = _rope_lanes(lat_ref[:, _KPE], c_ref[...], s_ref[...], False).astype(BF)

    return pl.pallas_call(
        body, name="mla_prep", grid=(S // ts,),
        in_specs=[_row(ts, LAT_W), _bcast(1, Q_RANK), _bcast(1, KV_RANK), _row(ts, LANE), _row(ts, LANE)],
        out_specs=[_row(ts, Q_RANK), _row(ts, KV_RANK), _row(ts, LANE)],
        out_shape=[jax.ShapeDtypeStruct((S, Q_RANK), BF), jax.ShapeDtypeStruct((S, KV_RANK), BF),
                   jax.ShapeDtypeStruct((S, LANE), BF)],
        compiler_params=_params("parallel"))(lat, qg, kvg, cos, sin_signed)


def _mla_prep_bwd(lat, qg, kvg, cos, sin_signed, d_qn, d_kvn, d_kpe):
    S = lat.shape[0]
    ts = _tile(S, 512)

    def body(lat_ref, qg_ref, kvg_ref, c_ref, s_ref, dqn_ref, dkvn_ref, dkpe_ref, dlat_ref, dqg_ref, dkvg_ref):
        i = pl.program_id(0)
        dcq, dqg = _norm_bwd_rows(dqn_ref[...], lat_ref[:, _CQ], qg_ref[...])
        dckv, dkvg = _norm_bwd_rows(dkvn_ref[...], lat_ref[:, _CKV], kvg_ref[...])
        dlat_ref[:, _CQ] = dcq.astype(BF)
        dlat_ref[:, _CKV] = dckv.astype(BF)
        dlat_ref[:, _KPE] = _rope_lanes(dkpe_ref[...], c_ref[...], s_ref[...], True).astype(BF)
        dlat_ref[:, _KPE.stop:] = jnp.zeros((ts, LAT_W - _KPE.stop), BF)
        _accumulate(i, dqg_ref, dqg)
        _accumulate(i, dkvg_ref, dkvg)

    return pl.pallas_call(
        body, name="mla_prep_bwd", grid=(S // ts,),
        in_specs=[_row(ts, LAT_W), _bcast(1, Q_RANK), _bcast(1, KV_RANK), _row(ts, LANE), _row(ts, LANE),
                  _row(ts, Q_RANK), _row(ts, KV_RANK), _row(ts, LANE)],
        out_specs=[_row(ts, LAT_W), _bcast(1, Q_RANK), _bcast(1, KV_RANK)],
        out_shape=[jax.ShapeDtypeStruct((S, LAT_W), BF), jax.ShapeDtypeStruct((1, Q_RANK), F32),
                   jax.ShapeDtypeStruct((1, KV_RANK), F32)],
        compiler_params=_params("arbitrary"))(lat, qg, kvg, cos, sin_signed, d_qn, d_kvn, d_kpe)


def _sigmoid(z):
    return 1.0 / (1.0 + jnp.exp(-z))


def _merge_fwd(gpre, b_gate, o_a, o_b):
    S, D = o_a.shape
    ts = _tile(S, 256)

    def body(g_ref, b_ref, oa_ref, ob_ref, m_ref):
        ga = _sigmoid(g_ref[:, :D] + b_ref[:, :D])
        gb = _sigmoid(g_ref[:, D:] + b_ref[:, D:])
        m_ref[...] = (ga * oa_ref[...] + gb * ob_ref[...]).astype(BF)

    return pl.pallas_call(
        body, name="merge_fwd", grid=(S // ts,),
        in_specs=[_row(ts, 2 * D), _bcast(1, 2 * D), _row(ts, D), _row(ts, D)], out_specs=_row(ts, D),
        out_shape=jax.ShapeDtypeStruct((S, D), BF), compiler_params=_params("parallel"))(gpre, b_gate, o_a, o_b)


def _merge_bwd(d_merge, gpre, b_gate, o_a, o_b):
    S, D = o_a.shape
    ts = _tile(S, 256)

    def body(dm_ref, g_ref, b_ref, oa_ref, ob_ref, doa_ref, dob_ref, dg_ref, db_ref):
        dm = dm_ref[...]
        ga = _sigmoid(g_ref[:, :D] + b_ref[:, :D])
        gb = _sigmoid(g_ref[:, D:] + b_ref[:, D:])
        doa_ref[...] = (dm * ga).astype(BF)
        dob_ref[...] = (dm * gb).astype(BF)
        dga = dm * oa_ref[...] * ga * (1.0 - ga)
        dgb = dm * ob_ref[...] * gb * (1.0 - gb)
        dg_ref[:, :D] = dga.astype(BF)
        dg_ref[:, D:] = dgb.astype(BF)
        i = pl.program_id(0)
        part = jnp.concatenate([jnp.sum(dga, axis=0, keepdims=True), jnp.sum(dgb, axis=0, keepdims=True)], axis=1)
        _accumulate(i, db_ref, part)

    return pl.pallas_call(
        body, name="merge_bwd", grid=(S // ts,),
        in_specs=[_row(ts, D), _row(ts, 2 * D), _bcast(1, 2 * D), _row(ts, D), _row(ts, D)],
        out_specs=[_row(ts, D), _row(ts, D), _row(ts, 2 * D), _bcast(1, 2 * D)],
        out_shape=[jax.ShapeDtypeStruct((S, D), BF), jax.ShapeDtypeStruct((S, D), BF),
                   jax.ShapeDtypeStruct((S, 2 * D), BF), jax.ShapeDtypeStruct((1, 2 * D), F32)],
        compiler_params=_params("arbitrary"))(d_merge, gpre, b_gate, o_a, o_b)


def _final_loss(x2, tgt, gf):
    S, D = x2.shape
    ts = _tile(S, 512)

    def body(x_ref, t_ref, g_ref, dx_ref, dg_ref, loss_ref):
        i = pl.program_id(0)
        xv = x_ref[...]
        g = g_ref[...]
        y = xv * _rstd(xv) * g
        err = y - t_ref[...]
        dx, dg = _norm_bwd_rows(err * (1.0 / D), xv, g)
        dx_ref[...] = dx
        _accumulate(i, dg_ref, dg)
        part = 0.5 * jnp.sum(jnp.mean(err * err, axis=-1, keepdims=True), axis=0, keepdims=True)
        _accumulate(i, loss_ref, jnp.broadcast_to(part, (8, LANE)))

    return pl.pallas_call(
        body, name="final_loss", grid=(S // ts,),
        in_specs=[_row(ts, D), _row(ts, D), _bcast(1, D)],
        out_specs=[_row(ts, D), _bcast(1, D), _bcast(8, LANE)],
        out_shape=[jax.ShapeDtypeStruct((S, D), F32), jax.ShapeDtypeStruct((1, D), F32),
                   jax.ShapeDtypeStruct((8, LANE), F32)],
        compiler_params=_params("arbitrary"))(x2, tgt, gf)


HALO = 8


def _shift_down(cur, prev, k, rows):
    out = pltpu.roll(cur, k, 0)
    for j in range(k):
        out = jnp.where(rows == j, prev[HALO - k + j:HALO - k + j + 1, :], out)
    return out


def _shift_up(cur, nxt, k, rows, ts):
    out = pltpu.roll(cur, ts - k, 0)
    for j in range(k):
        out = jnp.where(rows == ts - k + j, nxt[j:j + 1, :], out)
    return out


def _conv_rows(cur, prev, w, b, rows):
    return b + w[0:1, :] * _shift_down(cur, prev, 2, rows) + w[1:2, :] * _shift_down(cur, prev, 1, rows) + w[2:3, :] * cur


def _conv_specs(ts, C, shard_of):
    nh = ts // HALO
    cur = pl.BlockSpec((None, ts, C), lambda g, i: (shard_of(g), i, 0))
    prev = pl.BlockSpec((None, HALO, C), lambda g, i: (shard_of(g), jnp.maximum(i * nh - 1, 0), 0))
    return cur, prev


def _ffn_act(u_pre, conv_w, conv_b):
    G4, S, C = u_pre.shape
    G = G4 // 2
    ts = _tile(S, 256)

    def body(up_ref, upp_ref, gt_ref, gtp_ref, wu_ref, wg_ref, bu_ref, bg_ref, act_ref):
        first = pl.program_id(1) == 0
        rows = lax.broadcasted_iota(jnp.int32, (ts, C), 0)
        pu = jnp.where(first, 0.0, upp_ref[...])
        pg = jnp.where(first, 0.0, gtp_ref[...])
        up = _conv_rows(up_ref[...], pu, wu_ref[...], bu_ref[...], rows)
        gate = _conv_rows(gt_ref[...], pg, wg_ref[...], bg_ref[...], rows)
        act_ref[...] = (gate * _sigmoid(gate) * up).astype(BF)

    cur_u, prev_u = _conv_specs(ts, C, lambda g: g)
    cur_g, prev_g = _conv_specs(ts, C, lambda g: g + G)
    w_u = pl.BlockSpec((None, 3, C), lambda g, i: (g, 0, 0))
    w_g = pl.BlockSpec((None, 3, C), lambda g, i: (g + G, 0, 0))
    b_u = pl.BlockSpec((None, 1, C), lambda g, i: (g, 0, 0))
    b_g = pl.BlockSpec((None, 1, C), lambda g, i: (g + G, 0, 0))
    return pl.pallas_call(
        body, name="ffn_act", grid=(G, S // ts),
        in_specs=[cur_u, prev_u, cur_g, prev_g, w_u, w_g, b_u, b_g],
        out_specs=pl.BlockSpec((None, ts, C), lambda g, i: (g, i, 0)),
        out_shape=jax.ShapeDtypeStruct((G, S, C), BF),
        compiler_params=_params("parallel", "parallel"))(u_pre, u_pre, u_pre, u_pre, conv_w, conv_w, conv_b, conv_b)


def _ffn_act_bwd(u_pre, conv_w, conv_b, d_act):
    G4, S, C = u_pre.shape
    G = G4 // 2
    ts = _tile(S, 256)

    def body(up_ref, upp_ref, gt_ref, gtp_ref, wu_ref, wg_ref, bu_ref, bg_ref, da_ref, dup_ref, dgt_ref):
        first = pl.program_id(1) == 0
        rows = lax.broadcasted_iota(jnp.int32, (ts, C), 0)
        pu = jnp.where(first, 0.0, upp_ref[...])
        pg = jnp.where(first, 0.0, gtp_ref[...])
        up = _conv_rows(up_ref[...], pu, wu_ref[...], bu_ref[...], rows)
        gate = _conv_rows(gt_ref[...], pg, wg_ref[...], bg_ref[...], rows)
        sg = _sigmoid(gate)
        da = da_ref[...]
        dup_ref[...] = da * (gate * sg)
        dgt_ref[...] = da * up * (sg * (1.0 + gate * (1.0 - sg)))

    cur_u, prev_u = _conv_specs(ts, C, lambda g: g)
    cur_g, prev_g = _conv_specs(ts, C, lambda g: g + G)
    w_u = pl.BlockSpec((None, 3, C), lambda g, i: (g, 0, 0))
    w_g = pl.BlockSpec((None, 3, C), lambda g, i: (g + G, 0, 0))
    b_u = pl.BlockSpec((None, 1, C), lambda g, i: (g, 0, 0))
    b_g = pl.BlockSpec((None, 1, C), lambda g, i: (g + G, 0, 0))
    blk = pl.BlockSpec((None, ts, C), lambda g, i: (g, i, 0))
    d_up, d_gate = pl.pallas_call(
        body, name="ffn_act_bwd", grid=(G, S // ts),
        in_specs=[cur_u, prev_u, cur_g, prev_g, w_u, w_g, b_u, b_g, blk],
        out_specs=[blk, blk],
        out_shape=[jax.ShapeDtypeStruct((G, S, C), F32), jax.ShapeDtypeStruct((G, S, C), F32)],
        compiler_params=_params("parallel", "parallel"))(
            u_pre, u_pre, u_pre, u_pre, conv_w, conv_w, conv_b, conv_b, d_act)
    return jnp.concatenate([d_up, d_gate], axis=0)


def _conv_bwd(du, u_pre, conv_w):
    G4, S, C = du.shape
    ts = _tile(S, 256)
    nh = ts // HALO
    last_halo = S // HALO - 1

    def body(du_ref, dun_ref, u_ref, up_ref, w_ref, dpre_ref, dw_ref, db_ref):
        i = pl.program_id(1)
        rows = lax.broadcasted_iota(jnp.int32, (ts, C), 0)
        du_c = du_ref[...]
        nxt = jnp.where(i == pl.num_programs(1) - 1, 0.0, dun_ref[...])
        prev = jnp.where(i == 0, 0.0, up_ref[...])
        w = w_ref[...]
        dpre = w[2:3, :] * du_c + w[1:2, :] * _shift_up(du_c, nxt, 1, rows, ts) + w[0:1, :] * _shift_up(du_c, nxt, 2, rows, ts)
        dpre_ref[...] = dpre.astype(BF)
        u_c = u_ref[...]
        dw = jnp.concatenate([
            jnp.sum(du_c * _shift_down(u_c, prev, 2, rows), axis=0, keepdims=True),
            jnp.sum(du_c * _shift_down(u_c, prev, 1, rows), axis=0, keepdims=True),
            jnp.sum(du_c * u_c, axis=0, keepdims=True)], axis=0)
        _accumulate(i, dw_ref, dw)
        _accumulate(i, db_ref, jnp.sum(du_c, axis=0, keepdims=True))

    cur = pl.BlockSpec((None, ts, C), lambda g, i: (g, i, 0))
    nxt = pl.BlockSpec((None, HALO, C), lambda g, i: (g, jnp.minimum((i + 1) * nh, last_halo), 0))
    prev = pl.BlockSpec((None, HALO, C), lambda g, i: (g, jnp.maximum(i * nh - 1, 0), 0))
    return pl.pallas_call(
        body, name="conv_bwd", grid=(G4, S // ts),
        in_specs=[cur, nxt, cur, prev, pl.BlockSpec((None, 3, C), lambda g, i: (g, 0, 0))],
        out_specs=[cur, pl.BlockSpec((None, 3, C), lambda g, i: (g, 0, 0)), pl.BlockSpec((None, 1, C), lambda g, i: (g, 0, 0))],
        out_shape=[jax.ShapeDtypeStruct((G4, S, C), BF), jax.ShapeDtypeStruct((G4, 3, C), F32),
                   jax.ShapeDtypeStruct((G4, 1, C), F32)],
        compiler_params=_params("parallel", "arbitrary"))(du, du, u_pre, u_pre, conv_w)


MLA_TQ = 512
MLA_TK = 512


def _mla_scores(qn_ref, qp_ref, kn_ref, kpe_ref, i, j):
    q = jnp.concatenate([qn_ref[...], qp_ref[...]], axis=1)
    k = jnp.concatenate([kn_ref[...], kpe_ref[...]], axis=1)
    s = lax.dot_general(q, k, NT, preferred_element_type=F32) * MLA_SCALE
    row = i * MLA_TQ + lax.broadcasted_iota(jnp.int32, s.shape, 0)
    col = j * MLA_TK + lax.broadcasted_iota(jnp.int32, s.shape, 1)
    return q, k, s, col <= row


def _mla_fwd(qn, qp, kn, kpe, v):
    S = qn.shape[0]
    nq, nk = S // MLA_TQ, S // MLA_TK

    def body(qn_ref, qp_ref, kn_ref, kpe_ref, v_ref, o_ref, lse_ref, m_scr, l_scr, acc_scr):
        i, j = pl.program_id(1), pl.program_id(2)

        @pl.when(j == 0)
        def _():
            m_scr[...] = jnp.full(m_scr.shape, -jnp.inf, F32)
            l_scr[...] = jnp.zeros(l_scr.shape, F32)
            acc_scr[...] = jnp.zeros(acc_scr.shape, F32)

        @pl.when(j <= i)
        def _():
            _, _, s, ok = _mla_scores(qn_ref, qp_ref, kn_ref, kpe_ref, i, j)
            s = jnp.where(ok, s, -jnp.inf)
            m_prev = m_scr[...]
            m_new = jnp.maximum(m_prev, jnp.max(s, axis=1, keepdims=True))
            p = jnp.exp(s - m_new)
            alpha = jnp.exp(m_prev - m_new)
            l_scr[...] = alpha * l_scr[...] + jnp.sum(p, axis=1, keepdims=True)
            acc_scr[...] = alpha * acc_scr[...] + lax.dot_general(p.astype(BF), v_ref[...], NN, preferred_element_type=F32)
            m_scr[...] = m_new

        @pl.when(j == i)
        def _():
            o_ref[...] = (acc_scr[...] / l_scr[...]).astype(BF)
            lse_ref[...] = jnp.broadcast_to(m_scr[...] + jnp.log(l_scr[...]), lse_ref.shape)

    qspec = pl.BlockSpec((MLA_TQ, HEAD), lambda h, i, j: (i, h))
    kspec = pl.BlockSpec((MLA_TK, HEAD), lambda h, i, j: (jnp.minimum(j, i), h))
    kpespec = pl.BlockSpec((MLA_TK, HEAD), lambda h, i, j: (jnp.minimum(j, i), 0))
    return pl.pallas_call(
        body, name="mla_fwd", grid=(MLA_HEADS, nq, nk),
        in_specs=[qspec, qspec, kspec, kpespec, kspec],
        out_specs=[qspec, pl.BlockSpec((None, MLA_TQ, LANE), lambda h, i, j: (h, i, 0))],
        out_shape=[jax.ShapeDtypeStruct((S, MLA_HEADS * HEAD), BF), jax.ShapeDtypeStruct((MLA_HEADS, S, LANE), F32)],
        scratch_shapes=[pltpu.VMEM((MLA_TQ, 1), F32), pltpu.VMEM((MLA_TQ, 1), F32), pltpu.VMEM((MLA_TQ, HEAD), F32)],
        compiler_params=_params("parallel", "parallel", "arbitrary"))(qn, qp, kn, kpe, v)


def _mla_p_ds(qn_ref, qp_ref, kn_ref, kpe_ref, v_ref, do_ref, o_ref, lse_ref, i, j):
    q, k, s, ok = _mla_scores(qn_ref, qp_ref, kn_ref, kpe_ref, i, j)
    p = jnp.exp(jnp.where(ok, s, -jnp.inf) - lse_ref[:, 0:1])
    do = do_ref[...]
    delta = jnp.sum(do.astype(F32) * o_ref[...].astype(F32), axis=1, keepdims=True)
    dp = lax.dot_general(do, v_ref[...], NT, preferred_element_type=F32)
    ds = p * (dp - delta) * MLA_SCALE
    return q, k, p, ds, do


def _mla_bwd_dq(qn, qp, kn, kpe, v, do, o, lse):
    S = qn.shape[0]
    nq, nk = S // MLA_TQ, S // MLA_TK

    def body(qn_ref, qp_ref, kn_ref, kpe_ref, v_ref, do_ref, o_ref, lse_ref, dqn_ref, dqp_ref, acc):
        i, j = pl.program_id(1), pl.program_id(2)

        @pl.when(j == 0)
        def _():
            acc[...] = jnp.zeros(acc.shape, F32)

        @pl.when(j <= i)
        def _():
            _, k, _, ds, _ = _mla_p_ds(qn_ref, qp_ref, kn_ref, kpe_ref, v_ref, do_ref, o_ref, lse_ref, i, j)
            acc[...] += lax.dot_general(ds.astype(BF), k, NN, preferred_element_type=F32)

        @pl.when(j == i)
        def _():
            dqn_ref[...] = acc[:, :HEAD].astype(BF)
            dqp_ref[...] = acc[:, HEAD:]

    qspec = pl.BlockSpec((MLA_TQ, HEAD), lambda h, i, j: (i, h))
    kspec = pl.BlockSpec((MLA_TK, HEAD), lambda h, i, j: (jnp.minimum(j, i), h))
    kpespec = pl.BlockSpec((MLA_TK, HEAD), lambda h, i, j: (jnp.minimum(j, i), 0))
    lsespec = pl.BlockSpec((None, MLA_TQ, LANE), lambda h, i, j: (h, i, 0))
    return pl.pallas_call(
        body, name="mla_bwd_dq", grid=(MLA_HEADS, nq, nk),
        in_specs=[qspec, qspec, kspec, kpespec, kspec, qspec, qspec, lsespec],
        out_specs=[qspec, qspec],
        out_shape=[jax.ShapeDtypeStruct((S, MLA_HEADS * HEAD), BF), jax.ShapeDtypeStruct((S, MLA_HEADS * HEAD), F32)],
        scratch_shapes=[pltpu.VMEM((MLA_TQ, 2 * HEAD), F32)],
        compiler_params=_params("parallel", "parallel", "arbitrary"))(qn, qp, kn, kpe, v, do, o, lse)


def _mla_bwd_dkv(qn, qp, kn, kpe, v, do, o, lse):
    S = qn.shape[0]
    nq, nk = S // MLA_TQ, S // MLA_TK

    def body(qn_ref, qp_ref, kn_ref, kpe_ref, v_ref, do_ref, o_ref, lse_ref, dkn_ref, dv_ref, dkpe_ref, dk_acc, dv_acc, dkpe_acc):
        j, h, i = pl.program_id(0), pl.program_id(1), pl.program_id(2)

        @pl.when(i == 0)
        def _():
            dk_acc[...] = jnp.zeros(dk_acc.shape, F32)
            dv_acc[...] = jnp.zeros(dv_acc.shape, F32)

        @pl.when((i == 0) & (h == 0))
        def _():
            dkpe_acc[...] = jnp.zeros(dkpe_acc.shape, F32)

        @pl.when(i >= j)
        def _():
            q, _, p, ds, do = _mla_p_ds(qn_ref, qp_ref, kn_ref, kpe_ref, v_ref, do_ref, o_ref, lse_ref, i, j)
            dv_acc[...] += lax.dot_general(p.astype(BF), do, TN, preferred_element_type=F32)
            dk_acc[...] += lax.dot_general(ds.astype(BF), q, TN, preferred_element_type=F32)

        @pl.when(i == nq - 1)
        def _():
            dkn_ref[...] = dk_acc[:, :HEAD].astype(BF)
            dv_ref[...] = dv_acc[...].astype(BF)
            dkpe_acc[...] += dk_acc[:, HEAD:]

        @pl.when((i == nq - 1) & (h == MLA_HEADS - 1))
        def _():
            dkpe_ref[...] = dkpe_acc[...]

    qspec = pl.BlockSpec((MLA_TQ, HEAD), lambda j, h, i: (jnp.maximum(i, j), h))
    kspec = pl.BlockSpec((MLA_TK, HEAD), lambda j, h, i: (j, h))
    kpespec = pl.BlockSpec((MLA_TK, HEAD), lambda j, h, i: (j, 0))
    lsespec = pl.BlockSpec((None, MLA_TQ, LANE), lambda j, h, i: (h, jnp.maximum(i, j), 0))
    return pl.pallas_call(
        body, name="mla_bwd_dkv", grid=(nk, MLA_HEADS, nq),
        in_specs=[qspec, qspec, kspec, kpespec, kspec, qspec, qspec, lsespec],
        out_specs=[kspec, kspec, kpespec],
        out_shape=[jax.ShapeDtypeStruct((S, MLA_HEADS * HEAD), BF), jax.ShapeDtypeStruct((S, MLA_HEADS * HEAD), BF),
                   jax.ShapeDtypeStruct((S, HEAD), F32)],
        scratch_shapes=[pltpu.VMEM((MLA_TK, 2 * HEAD), F32), pltpu.VMEM((MLA_TK, HEAD), F32), pltpu.VMEM((MLA_TK, HEAD), F32)],
        compiler_params=_params("parallel", "arbitrary", "arbitrary"))(qn, qp, kn, kpe, v, do, o, lse)


DIL_W = 3 * DIL_HPG * HEAD
DIL_O = DIL_HPG * HEAD


def _dil_slopes(g):
    return [2.0 ** (-ALIBI_MAX_BIAS * (g * DIL_HPG + hh + 1) / DIL_HEADS) for hh in range(DIL_HPG)]


def _dil_bias(dil):
    p = lax.broadcasted_iota(jnp.int32, (DIL_BLOCK, DIL_BLOCK), 0)
    kk = lax.broadcasted_iota(jnp.int32, (DIL_BLOCK, DIL_BLOCK), 1)
    jc = p - kk
    dist_c = (dil * jc).astype(F32)
    dist_p = (dil * (jc + DIL_BLOCK)).astype(F32)
    return jc >= 0, jc <= 0, dist_c, dist_p


def _dil_head(blk, hh):
    q = blk[:, hh * HEAD:(hh + 1) * HEAD]
    k = blk[:, DIL_O + hh * HEAD:DIL_O + (hh + 1) * HEAD]
    v = blk[:, 2 * DIL_O + hh * HEAD:2 * DIL_O + (hh + 1) * HEAD]
    return q, k, v


def _dil_s(q, k, slope, dist, ok):
    s = lax.dot_general(q, k, NT, preferred_element_type=F32) * DIL_SCALE - slope * dist
    return jnp.where(ok, s, -jnp.inf)


def _dil_view(a, dil):
    S, W = a.shape
    return a.reshape(S // dil, dil * W)


def _dil_fwd(qkv, g):
    _, dil = DIL_PATTERNS[g]
    S = qkv.shape[0]
    L = S // dil
    nb = L // DIL_BLOCK
    slopes = _dil_slopes(g)

    def body(cur_ref, prev_ref, o_ref, lse_ref):
        n = pl.program_id(1)
        ok_c, ok_p, dist_c, dist_p = _dil_bias(dil)
        ok_p = ok_p & (n > 0)
        cur, prev = cur_ref[...], prev_ref[...]
        for hh in range(DIL_HPG):
            q, kc, vc = _dil_head(cur, hh)
            _, kp, vp = _dil_head(prev, hh)
            sc = _dil_s(q, kc, slopes[hh], dist_c, ok_c)
            sp = _dil_s(q, kp, slopes[hh], dist_p, ok_p)
            m = jnp.maximum(jnp.max(sc, axis=1, keepdims=True), jnp.max(sp, axis=1, keepdims=True))
            pc, pp = jnp.exp(sc - m), jnp.exp(sp - m)
            l = jnp.sum(pc, axis=1, keepdims=True) + jnp.sum(pp, axis=1, keepdims=True)
            o = (lax.dot_general(pc.astype(BF), vc, NN, preferred_element_type=F32)
                 + lax.dot_general(pp.astype(BF), vp, NN, preferred_element_type=F32)) / l
            sl = slice(hh * HEAD, (hh + 1) * HEAD)
            o_ref[:, sl] = o
            lse_ref[:, sl] = jnp.broadcast_to(m + jnp.log(l), (DIL_BLOCK, HEAD))

    ospec = pl.BlockSpec((DIL_BLOCK, DIL_O), lambda r, n: (n, r))
    o, lse = pl.pallas_call(
        body, name=f"dil_fwd{g}", grid=(dil, nb),
        in_specs=[pl.BlockSpec((DIL_BLOCK, DIL_W), lambda r, n: (n, r)),
                  pl.BlockSpec((DIL_BLOCK, DIL_W), lambda r, n: (jnp.maximum(n - 1, 0), r))],
        out_specs=[ospec, ospec],
        out_shape=[jax.ShapeDtypeStruct((L, dil * DIL_O), F32), jax.ShapeDtypeStruct((L, dil * DIL_O), F32)],
        compiler_params=_params("parallel", "parallel"))(_dil_view(qkv, dil), _dil_view(qkv, dil))
    return o.reshape(S, DIL_O), lse.reshape(S, DIL_O)


def _dil_combine(os_, lses):
    S = os_[0].shape[0]
    ts = _tile(S, 512)

    def body(o0, o1, o2, l0, l1, l2, out_ref, lse_ref):
        a, b, c = l0[...], l1[...], l2[...]
        m = jnp.maximum(jnp.maximum(a, b), c)
        ea, eb, ec = jnp.exp(a - m), jnp.exp(b - m), jnp.exp(c - m)
        tot = ea + eb + ec
        out_ref[...] = ((ea * o0[...] + eb * o1[...] + ec * o2[...]) / tot).astype(BF)
        lse_ref[...] = m + jnp.log(tot)

    return pl.pallas_call(
        body, name="dil_combine", grid=(S // ts,), in_specs=[_row(ts, DIL_O)] * 6,
        out_specs=[_row(ts, DIL_O), _row(ts, DIL_O)],
        out_shape=[jax.ShapeDtypeStruct((S, DIL_O), BF), jax.ShapeDtypeStruct((S, DIL_O), F32)],
        compiler_params=_params("parallel"))(*os_, *lses)


def _dil_delta(do, out):
    S = do.shape[0]
    ts = _tile(S, 512)

    def body(do_ref, o_ref, d_ref):
        for hh in range(DIL_HPG):
            sl = slice(hh * HEAD, (hh + 1) * HEAD)
            d = jnp.sum(do_ref[:, sl].astype(F32) * o_ref[:, sl].astype(F32), axis=1, keepdims=True)
            d_ref[:, sl] = jnp.broadcast_to(d, (ts, HEAD))

    return pl.pallas_call(
        body, name="dil_delta", grid=(S // ts,), in_specs=[_row(ts, DIL_O)] * 2, out_specs=_row(ts, DIL_O),
        out_shape=jax.ShapeDtypeStruct((S, DIL_O), F32), compiler_params=_params("parallel"))(do, out)


def _dil_bwd(qkv, do, lse, delta, g):
    _, dil = DIL_PATTERNS[g]
    S = qkv.shape[0]
    L = S // dil
    nb = L // DIL_BLOCK
    slopes = _dil_slopes(g)

    def pair(q, k, v, do_h, lse_h, delta_h, slope, dist, ok):
        s = _dil_s(q, k, slope, dist, ok)
        p = jnp.exp(s - lse_h)
        dp = lax.dot_general(do_h, v, NT, preferred_element_type=F32)
        ds = (p * (dp - delta_h) * DIL_SCALE).astype(BF)
        return p.astype(BF), ds

    def body(cur_ref, prev_ref, next_ref, doc_ref, don_ref, lsec_ref, lsen_ref, dlc_ref, dln_ref, out_ref):
        n = pl.program_id(1)
        ok_c, ok_p0, dist_c, dist_p = _dil_bias(dil)
        ok_a = ok_p0 & (n > 0)
        ok_n = ok_p0 & (n < nb - 1)
        cur, prev, nxt = cur_ref[...], prev_ref[...], next_ref[...]
        for hh in range(DIL_HPG):
            sl = slice(hh * HEAD, (hh + 1) * HEAD)
            q, kc, vc = _dil_head(cur, hh)
            _, kp, vp = _dil_head(prev, hh)
            qn, _, _ = _dil_head(nxt, hh)
            do_c, do_n = doc_ref[:, sl], don_ref[:, sl]
            lse_c, lse_n = lsec_ref[:, sl][:, 0:1], lsen_ref[:, sl][:, 0:1]
            dl_c, dl_n = dlc_ref[:, sl][:, 0:1], dln_ref[:, sl][:, 0:1]
            _, ds_a = pair(q, kp, vp, do_c, lse_c, dl_c, slopes[hh], dist_p, ok_a)
            p_b, ds_b = pair(q, kc, vc, do_c, lse_c, dl_c, slopes[hh], dist_c, ok_c)
            p_n, ds_n = pair(qn, kc, vc, do_n, lse_n, dl_n, slopes[hh], dist_p, ok_n)
            dq = (lax.dot_general(ds_a, kp, NN, preferred_element_type=F32)
                  + lax.dot_general(ds_b, kc, NN, preferred_element_type=F32))
            dk = (lax.dot_general(ds_b, q, TN, preferred_element_type=F32)
                  + lax.dot_general(ds_n, qn, TN, preferred_element_type=F32))
            dv = (lax.dot_general(p_b, do_c, TN, preferred_element_type=F32)
                  + lax.dot_general(p_n, do_n, TN, preferred_element_type=F32))
            out_ref[:, sl] = dq.astype(BF)
            out_ref[:, DIL_O + hh * HEAD:DIL_O + (hh + 1) * HEAD] = dk.astype(BF)
            out_ref[:, 2 * DIL_O + hh * HEAD:2 * DIL_O + (hh + 1) * HEAD] = dv.astype(BF)

    cur_w = pl.BlockSpec((DIL_BLOCK, DIL_W), lambda r, n: (n, r))
    prev_w = pl.BlockSpec((DIL_BLOCK, DIL_W), lambda r, n: (jnp.maximum(n - 1, 0), r))
    next_w = pl.BlockSpec((DIL_BLOCK, DIL_W), lambda r, n: (jnp.minimum(n + 1, nb - 1), r))
    cur_o = pl.BlockSpec((DIL_BLOCK, DIL_O), lambda r, n: (n, r))
    next_o = pl.BlockSpec((DIL_BLOCK, DIL_O), lambda r, n: (jnp.minimum(n + 1, nb - 1), r))
    qv, dov, lsev, dlv = _dil_view(qkv, dil), _dil_view(do, dil), _dil_view(lse, dil), _dil_view(delta, dil)
    out = pl.pallas_call(
        body, name=f"dil_bwd{g}", grid=(dil, nb),
        in_specs=[cur_w, prev_w, next_w, cur_o, next_o, cur_o, next_o, cur_o, next_o],
        out_specs=cur_w, out_shape=jax.ShapeDtypeStruct((L, dil * DIL_W), BF),
        compiler_params=_params("parallel", "parallel"))(qv, qv, qv, dov, dov, lsev, lsev, dlv, dlv)
    return out.reshape(S, DIL_W)


def _adamw(name, w, g, m, v):
    R, C = w.shape
    tr = R if R <= 512 else _tile_rows(R, 256)

    def body(w_ref, g_ref, m_ref, v_ref, d_ref, nm_ref, nv_ref):
        gv = g_ref[...]
        nm = ADAM_B1 * m_ref[...] + (1.0 - ADAM_B1) * gv
        nv = ADAM_B2 * v_ref[...] + (1.0 - ADAM_B2) * (gv * gv)
        m_hat = nm / (1.0 - ADAM_B1 ** ADAM_STEP)
        v_hat = nv / (1.0 - ADAM_B2 ** ADAM_STEP)
        d_ref[...] = -ADAM_LR * (m_hat / (jnp.sqrt(v_hat) + ADAM_EPS) + ADAM_WD * w_ref[...])
        nm_ref[...] = nm
        nv_ref[...] = nv

    spec = pl.BlockSpec((tr, C), lambda i: (i, 0))
    shp = jax.ShapeDtypeStruct((R, C), F32)
    return pl.pallas_call(
        body, name=name, grid=(R // tr,), in_specs=[spec] * 4, out_specs=[spec] * 3, out_shape=[shp] * 3,
        compiler_params=_params("parallel"))(w, g, m, v)


def _tile_rows(n, pref):
    t = (pref // 8) * 8
    while t >= 8:
        if n % t == 0:
            return t
        t -= 8
    return n


def _add_n(name, xs, o_dtype):
    R, C = xs[0].shape
    tr = _tile_rows(R, 256)
    n = len(xs)

    def body(*refs):
        acc = refs[0][...].astype(F32)
        for r in refs[1:n]:
            acc = acc + r[...].astype(F32)
        refs[n][...] = acc.astype(o_dtype)

    spec = pl.BlockSpec((tr, C), lambda i: (i, 0))
    return pl.pallas_call(
        body, name=name, grid=(R // tr,), in_specs=[spec] * n, out_specs=spec,
        out_shape=jax.ShapeDtypeStruct((R, C), o_dtype), compiler_params=_params("parallel"))(*xs)


ANY = pl.BlockSpec(memory_space=pl.ANY)


def _place():
    x, y, c = lax.axis_index("x"), lax.axis_index("y"), lax.axis_index("c")
    chips = [(1 - x, y), (x, 1 - y), (1 - x, 1 - y)]
    chip_idx = [2 * cx + cy for cx, cy in chips]
    return x, y, c, 2 * x + y, chips, chip_idx


def _rcopy(src, dst, ssem, rsem, dev):
    return pltpu.make_async_remote_copy(src_ref=src, dst_ref=dst, send_sem=ssem, recv_sem=rsem,
                                        device_id=dev, device_id_type=MESH)


def _allgather_shards(shards, split):
    n = len(shards)

    def body(*refs):
        ins, outs = refs[:n], refs[n:2 * n]
        ssem, rsem, lsem = refs[2 * n:]
        x, y, c, me, chips, chip_idx = _place()
        sib = (x, y, 1 - c)
        local = [pltpu.make_async_copy(ins[i], outs[i].at[me], lsem.at[i]) for i in range(n)]
        for cp in local:
            cp.start()
        started = []

        def rows(i, hc):
            if not split[i]:
                return pl.ds(0, shards[i].shape[0])
            h = shards[i].shape[0] // 2
            return pl.ds(hc * h, h)

        for i in range(n):
            for k in range(3):
                cp = _rcopy(ins[i].at[rows(i, c)], outs[i].at[me, rows(i, c)], ssem.at[i, k], rsem.at[i, k], (*chips[k], c))
                cp.start()
                started.append(cp)
        for i in range(n):
            for k in range(3):
                landed = outs[i].at[chip_idx[k], rows(i, c)]
                _rcopy(landed, landed, ssem.at[i, k], rsem.at[i, k], (*chips[k], c)).wait_recv()
                if split[i]:
                    cp = _rcopy(landed, landed, ssem.at[i, 3 + k], rsem.at[i, 3 + k], sib)
                    cp.start()
                    started.append(cp)
        for i in range(n):
            if split[i]:
                for k in range(3):
                    other = outs[i].at[chip_idx[k], rows(i, 1 - c)]
                    _rcopy(other, other, ssem.at[i, 3 + k], rsem.at[i, 3 + k], sib).wait_recv()
        for cp in started:
            cp.wait_send()
        for cp in local:
            cp.wait()

    return pl.pallas_call(
        body, name="allgather_weights", in_specs=[ANY] * n, out_specs=[ANY] * n,
        out_shape=[jax.ShapeDtypeStruct((N_CHIPS,) + s.shape, s.dtype) for s in shards],
        scratch_shapes=[pltpu.SemaphoreType.DMA((n, 6)), pltpu.SemaphoreType.DMA((n, 6)), pltpu.SemaphoreType.DMA((n,))],
    )(*shards)


def _pair_exchange_halves(gs):
    n = len(gs)

    def half_shape(g):
        return (g.shape[0], g.shape[1] // 2, g.shape[2])

    def body(*refs):
        ins, mine, theirs = refs[:n], refs[n:2 * n], refs[2 * n:3 * n]
        ssem, rsem, lsem = refs[3 * n:]
        x, y, c, _, _, _ = _place()
        sib = (x, y, 1 - c)
        cps = []
        for i in range(n):
            h = gs[i].shape[1] // 2
            loc = pltpu.make_async_copy(ins[i].at[:, pl.ds(c * h, h)], mine[i], lsem.at[i])
            loc.start()
            rem = _rcopy(ins[i].at[:, pl.ds((1 - c) * h, h)], theirs[i], ssem.at[i], rsem.at[i], sib)
            rem.start()
            cps.append((loc, rem))
        for loc, rem in cps:
            rem.wait()
            loc.wait()

    shapes = [jax.ShapeDtypeStruct(half_shape(g), g.dtype) for g in gs]
    outs = pl.pallas_call(
        body, name="grad_pair_exchange", in_specs=[ANY] * n, out_specs=[ANY] * (2 * n), out_shape=shapes + shapes,
        scratch_shapes=[pltpu.SemaphoreType.DMA((n,)), pltpu.SemaphoreType.DMA((n,)), pltpu.SemaphoreType.DMA((n,))],
    )(*gs)
    return outs[:n], outs[n:]


def _chip_exchange(hs):
    n = len(hs)

    def body(*refs):
        ins, outs = refs[:n], refs[n:2 * n]
        ssem, rsem, lsem = refs[2 * n:]
        x, y, c, me, chips, chip_idx = _place()
        cps = []
        for i in range(n):
            loc = pltpu.make_async_copy(ins[i].at[me], outs[i].at[3], lsem.at[i])
            loc.start()
            cps.append(loc)
            for k in range(3):
                cp = _rcopy(ins[i].at[chip_idx[k]], outs[i].at[k], ssem.at[i, k], rsem.at[i, k], (*chips[k], c))
                cp.start()
                cps.append(cp)
        for cp in cps:
            cp.wait()

    return pl.pallas_call(
        body, name="grad_chip_exchange", in_specs=[ANY] * n, out_specs=[ANY] * n,
        out_shape=[jax.ShapeDtypeStruct(h.shape, h.dtype) for h in hs],
        scratch_shapes=[pltpu.SemaphoreType.DMA((n, 3)), pltpu.SemaphoreType.DMA((n, 3)), pltpu.SemaphoreType.DMA((n,))],
    )(*hs)


def _pair_join_halves(fs):
    n = len(fs)

    def body(*refs):
        ins, outs = refs[:n], refs[n:2 * n]
        ssem, rsem, lsem = refs[2 * n:]
        x, y, c, _, _, _ = _place()
        sib = (x, y, 1 - c)
        cps = []
        for i in range(n):
            h = fs[i].shape[0]
            loc = pltpu.make_async_copy(ins[i], outs[i].at[pl.ds(c * h, h)], lsem.at[i])
            loc.start()
            rem = _rcopy(ins[i], outs[i].at[pl.ds(c * h, h)], ssem.at[i], rsem.at[i], sib)
            rem.start()
            cps.append((loc, rem, h))
        for i, (loc, rem, h) in enumerate(cps):
            rem.wait_send()
            got = outs[i].at[pl.ds((1 - c) * h, h)]
            _rcopy(got, got, ssem.at[i], rsem.at[i], sib).wait_recv()
            loc.wait()

    return pl.pallas_call(
        body, name="grad_pair_join", in_specs=[ANY] * n, out_specs=[ANY] * n,
        out_shape=[jax.ShapeDtypeStruct((2 * f.shape[0], f.shape[1]), f.dtype) for f in fs],
        scratch_shapes=[pltpu.SemaphoreType.DMA((n,)), pltpu.SemaphoreType.DMA((n,)), pltpu.SemaphoreType.DMA((n,))],
    )(*fs)


def _allreduce_small(v):
    R, K = v.shape
    ndev = 8

    def body(v_ref, o_ref, land, ssem, rsem):
        x, y, c = lax.axis_index("x"), lax.axis_index("y"), lax.axis_index("c")
        me = 4 * x + 2 * y + c
        land[me] = v_ref[...]
        cps = []
        for r in range(1, ndev):
            fx, fy, fc = (r >> 2) & 1, (r >> 1) & 1, r & 1
            peer = (x ^ fx, y ^ fy, c ^ fc)
            cp = _rcopy(v_ref, land.at[me], ssem.at[r - 1], rsem.at[r - 1], peer)
            cp.start()
            cps.append((cp, 4 * peer[0] + 2 * peer[1] + peer[2], r))
        for cp, src, r in cps:
            cp.wait_send()
            _rcopy(v_ref, land.at[src], ssem.at[r - 1], rsem.at[r - 1], (x, y, c)).wait_recv()
        acc = land[0]
        for d in range(1, ndev):
            acc = acc + land[d]
        o_ref[...] = acc

    vm = pl.BlockSpec(memory_space=pltpu.VMEM)
    return pl.pallas_call(
        body, name="allreduce_small", in_specs=[vm], out_specs=vm, out_shape=jax.ShapeDtypeStruct((R, K), F32),
        scratch_shapes=[pltpu.VMEM((ndev, R, K), F32), pltpu.SemaphoreType.DMA((ndev - 1,)), pltpu.SemaphoreType.DMA((ndev - 1,))],
    )(v)


IN_SPLITS = (Q_RANK, KV_RANK, QK_ROPE, DIL_HEADS * HEAD, DIL_HEADS * HEAD, DIL_HEADS * HEAD, D_MODEL, D_MODEL)
IN_OFF = tuple(int(v) for v in np.cumsum((0,) + IN_SPLITS))


def _unshard_cols(g):
    G, K, Ns = g.shape
    return g.transpose(1, 0, 2).reshape(K, G * Ns)


def _shard_cols(w):
    K, N = w.shape
    return w.reshape(K, N_CHIPS, N // N_CHIPS).transpose(1, 0, 2)


def _rope_pad(w):
    half = QK_ROPE // 2
    z = jnp.zeros(w.shape[:-1] + (half,), w.dtype)
    return jnp.concatenate([w[..., :half], z, w[..., half:], z], axis=-1)


def _rope_unpad(w):
    half = QK_ROPE // 2
    return jnp.concatenate([w[..., :half], w[..., 2 * half:3 * half]], axis=-1)


def _split_w_in(w_in_g):
    w = _unshard_cols(w_in_g)
    K = w.shape[0]
    p = [w[:, IN_OFF[i]:IN_OFF[i + 1]] for i in range(8)]
    w_lat = jnp.concatenate([p[0], p[1], _rope_pad(p[2]), jnp.zeros((K, LAT_W - _KPE.stop), w.dtype)], axis=1)
    w_dil = [jnp.concatenate([p[3 + t][:, g * DIL_O:(g + 1) * DIL_O] for t in range(3)], axis=1) for g in range(DIL_GROUPS)]
    w_gate = jnp.concatenate([p[6], p[7]], axis=1)
    return w_lat, w_dil, w_gate


def _merge_dw_in(dw_lat, dw_dil, dw_gate):
    parts = [dw_lat[:, _CQ], dw_lat[:, _CKV], _rope_unpad(dw_lat[:, _KPE])]
    for t in range(3):
        parts += [dw_dil[g][:, t * DIL_O:(t + 1) * DIL_O] for g in range(DIL_GROUPS)]
    parts.append(dw_gate)
    return _shard_cols(jnp.concatenate(parts, axis=1))


def _split_w_uq(w_uq_g):
    w = _unshard_cols(w_uq_g)
    K = w.shape[0]
    w = w.reshape(K, MLA_HEADS, QK_NOPE + QK_ROPE)
    return w[:, :, :QK_NOPE].reshape(K, MLA_HEADS * HEAD), _rope_pad(w[:, :, QK_NOPE:]).reshape(K, MLA_HEADS * HEAD)


def _merge_dw_uq(dw_n, dw_p):
    K = dw_n.shape[0]
    w = jnp.concatenate([dw_n.reshape(K, MLA_HEADS, HEAD), _rope_unpad(dw_p.reshape(K, MLA_HEADS, HEAD))], axis=-1)
    return _shard_cols(w.reshape(K, MLA_HEADS * (QK_NOPE + QK_ROPE)))


def _split_w_ukv(w_ukv_g):
    w = _unshard_cols(w_ukv_g)
    K = w.shape[0]
    w = w.reshape(K, MLA_HEADS, 2 * HEAD)
    return w[:, :, :HEAD].reshape(K, MLA_HEADS * HEAD), w[:, :, HEAD:].reshape(K, MLA_HEADS * HEAD)


def _merge_dw_ukv(dw_k, dw_v):
    K = dw_k.shape[0]
    w = jnp.concatenate([dw_k.reshape(K, MLA_HEADS, HEAD), dw_v.reshape(K, MLA_HEADS, HEAD)], axis=-1)
    return _shard_cols(w.reshape(K, MLA_HEADS * 2 * HEAD))


def _local_step(x, tgt, W):
    S, D = x.shape
    cos, sin_s = _rope_tables(S)
    w_lat, w_dil, w_gate = _split_w_in(W["w_in"])
    w_uqn, w_uqp = _split_w_uq(W["w_uq"])
    w_k, w_v = _split_w_ukv(W["w_ukv"])
    w_o_mla, w_o_dil = W["w_o_mla"], W["w_o_dil"]
    w_out = W["w_out"].reshape(D, D)
    w_up = W["w_up"]
    G4, _, C = w_up.shape
    w_down = W["w_down"].reshape(G4 // 2, C, D)
    conv_w = W["conv_w"]
    conv_b = W["conv_b"].reshape(G4, 1, C)

    h = _rmsnorm_fwd("attn_norm", x, W["attn_norm_g"])
    lat = _mm_nn("proj_lat", h, w_lat)
    qkv = [_mm_nn(f"proj_dil{g}", h, w_dil[g], o_dtype=BF) for g in range(DIL_GROUPS)]
    gpre = _mm_nn("proj_gate", h, w_gate)
    qn_, kvn, kpe = _mla_prep(lat, W["q_norm_g"], W["kv_norm_g"], cos, sin_s)
    q_nope = _mm_nn("q_nope", qn_, w_uqn, o_dtype=BF)
    q_pe = _rope("q_rope", _mm_nn("q_pe", qn_, w_uqp), cos, sin_s, False)
    k_nope = _mm_nn("k_nope", kvn, w_k, o_dtype=BF)
    v_mla = _mm_nn("v_mla", kvn, w_v, o_dtype=BF)
    attn_a, lse_a = _mla_fwd(q_nope, q_pe, k_nope, kpe, v_mla)
    dil = [_dil_fwd(qkv[g], g) for g in range(DIL_GROUPS)]
    attn_b, lse_b = _dil_combine([o for o, _ in dil], [l for _, l in dil])
    o_a = _mm_nn("o_mla", attn_a, w_o_mla)
    o_b = _mm_nn("o_dil", attn_b, w_o_dil)
    merge = _merge_fwd(gpre, W["b_gate"], o_a, o_b)
    x1 = _mm_nn("out_proj", merge, w_out, add=x)
    h2 = _rmsnorm_fwd("ffn_norm", x1, W["ffn_norm_g"])
    u_pre = _up_fwd(h2, w_up)
    act = _ffn_act(u_pre, conv_w, conv_b)
    x2 = _down_fwd(act, w_down, x1)
    dx2, d_final_g, loss8 = _final_loss(x2, tgt, W["final_norm_g"])

    d_act = _down_dgrad(dx2, w_down)
    dw_down = _down_wgrad(act, dx2)
    du = _ffn_act_bwd(u_pre, conv_w, conv_b, d_act)
    du_pre, d_conv_w, d_conv_b = _conv_bwd(du, u_pre, conv_w)
    dh2 = _up_dgrad(du_pre, w_up)
    dw_up = _up_wgrad(h2, du_pre)
    dx1, d_ffn_g = _rmsnorm_bwd("ffn_norm_bwd", dh2, x1, W["ffn_norm_g"], dx2)
    d_merge = _mm_nt("out_proj_dgrad", dx1, w_out)
    dw_out = _mm_tn("out_proj_wgrad", merge, dx1)
    d_oa, d_ob, d_gpre, d_b_gate = _merge_bwd(d_merge, gpre, W["b_gate"], o_a, o_b)
    d_attn_a = _mm_nt("o_mla_dgrad", d_oa, w_o_mla, o_dtype=BF)
    dw_o_mla = _mm_tn("o_mla_wgrad", attn_a, d_oa, shards=N_CHIPS)
    d_attn_b = _mm_nt("o_dil_dgrad", d_ob, w_o_dil, o_dtype=BF)
    dw_o_dil = _mm_tn("o_dil_wgrad", attn_b, d_ob, shards=N_CHIPS)
    delta_b = _dil_delta(d_attn_b, attn_b)
    d_qkv = [_dil_bwd(qkv[g], d_attn_b, lse_b, delta_b, g) for g in range(DIL_GROUPS)]
    dq_nope, dq_pe_rot = _mla_bwd_dq(q_nope, q_pe, k_nope, kpe, v_mla, d_attn_a, attn_a, lse_a)
    dk_nope, dv_mla, dkpe_rot = _mla_bwd_dkv(q_nope, q_pe, k_nope, kpe, v_mla, d_attn_a, attn_a, lse_a)
    dq_pe = _rope("q_rope_bwd", dq_pe_rot, cos, sin_s, True)
    d_qn = _mm_nt("q_pe_dgrad", dq_pe, w_uqp, add=_mm_nt("q_nope_dgrad", dq_nope, w_uqn))
    d_kvn = _mm_nt("v_dgrad", dv_mla, w_v, add=_mm_nt("k_nope_dgrad", dk_nope, w_k))
    dw_uq = _merge_dw_uq(_mm_tn("q_nope_wgrad", qn_, dq_nope), _mm_tn("q_pe_wgrad", qn_, dq_pe))
    dw_ukv = _merge_dw_ukv(_mm_tn("k_nope_wgrad", kvn, dk_nope), _mm_tn("v_wgrad", kvn, dv_mla))
    d_lat, d_q_g, d_kv_g = _mla_prep_bwd(lat, W["q_norm_g"], W["kv_norm_g"], cos, sin_s, d_qn, d_kvn, dkpe_rot)
    dh = _mm_nt("proj_lat_dgrad", d_lat, w_lat)
    for g in range(DIL_GROUPS):
        dh = _mm_nt(f"proj_dil{g}_dgrad", d_qkv[g], w_dil[g], add=dh)
    dh = _mm_nt("proj_gate_dgrad", d_gpre, w_gate, add=dh)
    dw_in = _merge_dw_in(_mm_tn("proj_lat_wgrad", h, d_lat),
                         [_mm_tn(f"proj_dil{g}_wgrad", h, d_qkv[g]) for g in range(DIL_GROUPS)],
                         _mm_tn("proj_gate_wgrad", h, d_gpre))
    grad_x, d_attn_g = _rmsnorm_bwd("attn_norm_bwd", dh, x, W["attn_norm_g"], dx1)

    big = {"w_in": dw_in, "w_uq": dw_uq, "w_ukv": dw_ukv, "w_o_mla": dw_o_mla, "w_o_dil": dw_o_dil,
           "w_out": dw_out.reshape(N_CHIPS, D // N_CHIPS, D), "w_up": dw_up,
           "w_down": dw_down.reshape(N_CHIPS, (G4 // 2) * C // N_CHIPS, D)}
    small = {"attn_norm_g": d_attn_g, "b_gate": d_b_gate, "q_norm_g": d_q_g, "kv_norm_g": d_kv_g,
             "ffn_norm_g": d_ffn_g, "conv_w": d_conv_w, "conv_b": d_conv_b.reshape(1, G4 * C),
             "final_norm_g": d_final_g}
    return loss8[0, 0], grad_x, big, small


BIG = ("w_in", "w_uq", "w_ukv", "w_o_mla", "w_o_dil", "w_out", "w_up", "w_down")
SMALL = ("attn_norm_g", "b_gate", "q_norm_g", "kv_norm_g", "ffn_norm_g", "conv_w", "conv_b", "final_norm_g")
WEIGHTS = ("attn_norm_g", "w_in", "b_gate", "q_norm_g", "w_uq", "kv_norm_g", "w_ukv", "w_o_mla", "w_o_dil",
           "w_out", "ffn_norm_g", "w_up", "conv_w", "conv_b", "w_down", "final_norm_g")
SMALL_ROWS = 8


def _reduce_big(big):
    gs = [big[n] for n in BIG]
    mine, theirs = _pair_exchange_halves(gs)
    hs = []
    for n, a, b in zip(BIG, mine, theirs):
        G, Rh, C = a.shape
        hs.append(_add_n(f"pair_sum_{n}", [a.reshape(G * Rh, C), b.reshape(G * Rh, C)], BF).reshape(G, Rh, C))
    got = _chip_exchange(hs)
    fs = [_add_n(f"chip_sum_{n}", [r[3], r[0], r[1], r[2]], F32) for n, r in zip(BIG, got)]
    return dict(zip(BIG, _pair_join_halves(fs)))


def _reduce_small(small):
    flat = [small[n].reshape(-1) for n in SMALL]
    sizes = [f.shape[0] for f in flat]
    total = sum(sizes)
    width = -(-total // (SMALL_ROWS * LANE)) * LANE
    packed = jnp.concatenate(flat + [jnp.zeros((SMALL_ROWS * width - total,), F32)]).reshape(SMALL_ROWS, width)
    red = _allreduce_small(packed).reshape(-1)
    out, off = {}, 0
    for n, s in zip(SMALL, sizes):
        out[n] = red[off:off + s]
        off += s
    return out


def kernel(x, attn_norm_g, w_in, b_gate, q_norm_g, w_uq, kv_norm_g, w_ukv, w_o_mla, w_o_dil, w_out, ffn_norm_g, w_up, conv_w, conv_b, w_down, final_norm_g, loss_target, m_attn_norm_g, m_w_in, m_b_gate, m_q_norm_g, m_w_uq, m_kv_norm_g, m_w_ukv, m_w_o_mla, m_w_o_dil, m_w_out, m_ffn_norm_g, m_w_up, m_conv_w, m_conv_b, m_w_down, m_final_norm_g, v_attn_norm_g, v_w_in, v_b_gate, v_q_norm_g, v_w_uq, v_kv_norm_g, v_w_ukv, v_w_o_mla, v_w_o_dil, v_w_out, v_ffn_norm_g, v_w_up, v_conv_w, v_conv_b, v_w_down, v_final_norm_g):
    given = dict(attn_norm_g=attn_norm_g, w_in=w_in, b_gate=b_gate, q_norm_g=q_norm_g, w_uq=w_uq, kv_norm_g=kv_norm_g,
                 w_ukv=w_ukv, w_o_mla=w_o_mla, w_o_dil=w_o_dil, w_out=w_out, ffn_norm_g=ffn_norm_g, w_up=w_up,
                 conv_w=conv_w, conv_b=conv_b, w_down=w_down, final_norm_g=final_norm_g)
    moments_m = dict(attn_norm_g=m_attn_norm_g, w_in=m_w_in, b_gate=m_b_gate, q_norm_g=m_q_norm_g, w_uq=m_w_uq,
                     kv_norm_g=m_kv_norm_g, w_ukv=m_w_ukv, w_o_mla=m_w_o_mla, w_o_dil=m_w_o_dil, w_out=m_w_out,
                     ffn_norm_g=m_ffn_norm_g, w_up=m_w_up, conv_w=m_conv_w, conv_b=m_conv_b, w_down=m_w_down,
                     final_norm_g=m_final_norm_g)
    moments_v = dict(attn_norm_g=v_attn_norm_g, w_in=v_w_in, b_gate=v_b_gate, q_norm_g=v_q_norm_g, w_uq=v_w_uq,
                     kv_norm_g=v_kv_norm_g, w_ukv=v_w_ukv, w_o_mla=v_w_o_mla, w_o_dil=v_w_o_dil, w_out=v_w_out,
                     ffn_norm_g=v_ffn_norm_g, w_up=v_w_up, conv_w=v_conv_w, conv_b=v_conv_b, w_down=v_w_down,
                     final_norm_g=v_final_norm_g)

    shards = [given[n][0].astype(BF) for n in BIG] + [given["conv_w"][0]]
    gathered = _allgather_shards(shards, [True] * len(BIG) + [False])
    W = dict(zip(BIG + ("conv_w",), gathered))
    for n in ("attn_norm_g", "b_gate", "q_norm_g", "kv_norm_g", "ffn_norm_g", "conv_b"):
        W[n] = given[n]
    W["final_norm_g"] = given["final_norm_g"].reshape(1, -1)

    loss_part, grad_x, big, small = _local_step(x[0], loss_target[0], W)
    loss = lax.psum(loss_part, ("x", "y", "c"))

    g_big = _reduce_big(big)
    g_small = _reduce_small(small)
    chip = 2 * lax.axis_index("x") + lax.axis_index("y")
    grads = {}
    for n in BIG:
        grads[n] = g_big[n].reshape(given[n].shape)
    for n in SMALL:
        if n == "conv_w":
            full = g_small[n].reshape(N_CHIPS, 3, -1)
            grads[n] = lax.dynamic_index_in_dim(full, chip, 0, keepdims=True)
        else:
            grads[n] = g_small[n].reshape(given[n].shape)

    delta, new_m, new_v = {}, {}, {}
    for n in WEIGHTS:
        shp = given[n].shape
        two_d = (-1, shp[-1]) if len(shp) > 1 else (1, -1)
        d, nm, nv = _adamw(f"adamw_{n}", given[n].reshape(two_d), grads[n].reshape(two_d),
                           moments_m[n].reshape(two_d), moments_v[n].reshape(two_d))
        delta[n], new_m[n], new_v[n] = d.reshape(shp), nm.reshape(shp), nv.reshape(shp)

    return (loss, grad_x[None], *[grads[n] for n in WEIGHTS], *[delta[n] for n in WEIGHTS],
            *[new_m[n] for n in WEIGHTS], *[new_v[n] for n in WEIGHTS])
```

```python
import functools
import math

import numpy as np
import jax
import jax.numpy as jnp
from jax import lax
from jax.experimental import pallas as pl
from jax.experimental.pallas import tpu as pltpu

F32 = jnp.float32
BF = jnp.bfloat16
MESH = pl.DeviceIdType.MESH

D_MODEL = 2048
MLA_HEADS = 8
QK_NOPE = 128
QK_ROPE = 64
Q_RANK = 512
KV_RANK = 256
ROPE_THETA = 10000.0
DIL_PATTERNS = ((128, 1), (512, 4), (2048, 16))
DIL_GROUPS = 3
DIL_HPG = 4
DIL_HEADS = 12
HEAD = 128
DIL_BLOCK = 128
ALIBI_MAX_BIAS = 8.0
NORM_EPS = 1e-6
N_CHIPS = 4
ADAM_LR = 0.001
ADAM_B1 = 0.9
ADAM_B2 = 0.999
ADAM_EPS = 1e-08
ADAM_WD = 0.01
ADAM_STEP = 10

LANE = 128
VMEM_LIMIT = 56 * 1024 * 1024
MLA_SCALE = (QK_NOPE + QK_ROPE) ** -0.5
DIL_SCALE = HEAD ** -0.5


def _params(*sem):
    return pltpu.CompilerParams(dimension_semantics=sem, vmem_limit_bytes=VMEM_LIMIT)


def _tile(n, pref):
    t = (pref // LANE) * LANE
    while t >= LANE:
        if n % t == 0:
            return t
        t -= LANE
    return n


NN = (((1,), (0,)), ((), ()))
NT = (((1,), (1,)), ((), ()))
TN = (((0,), (0,)), ((), ()))


def _mm_call(name, a, b, add, *, grid, a_spec, b_spec, add_spec, o_spec, o_shape, o_dtype, acc_shape, dims, nk):
    nax = len(grid)

    def body(*refs):
        if add is None:
            a_ref, b_ref, o_ref = refs[:3]
            c_ref = None
            scr = refs[3:]
        else:
            a_ref, b_ref, c_ref, o_ref = refs[:4]
            scr = refs[4:]
        prod = lax.dot_general(a_ref[...].astype(BF), b_ref[...].astype(BF), dims, preferred_element_type=F32)
        if nk == 1:
            if c_ref is not None:
                prod = prod + c_ref[...]
            o_ref[...] = prod.astype(o_ref.dtype)
        else:
            acc = scr[0]
            k = pl.program_id(nax - 1)

            @pl.when(k == 0)
            def _():
                if c_ref is not None:
                    acc[...] = prod + c_ref[...]
                else:
                    acc[...] = prod

            @pl.when(k > 0)
            def _():
                acc[...] += prod

            @pl.when(k == nk - 1)
            def _():
                o_ref[...] = acc[...].astype(o_ref.dtype)

    ins = [a, b] + ([] if add is None else [add])
    specs = [a_spec, b_spec] + ([] if add is None else [add_spec])
    sem = ("parallel",) * (nax - 1) + ("arbitrary",)
    return pl.pallas_call(
        body, name=name, grid=grid, in_specs=specs, out_specs=o_spec,
        out_shape=jax.ShapeDtypeStruct(o_shape, o_dtype),
        scratch_shapes=[] if nk == 1 else [pltpu.VMEM(acc_shape, F32)],
        compiler_params=_params(*sem),
    )(*ins)


def _mm_nn(name, a, b, *, add=None, o_dtype=F32):
    M, K = a.shape
    sharded = b.ndim == 3
    Ns = b.shape[-1]
    N = Ns * (b.shape[0] if sharded else 1)
    tm, tn, tk = _tile(M, 512), _tile(Ns, 512), _tile(K, 2048)
    per = Ns // tn
    nk = K // tk
    if sharded:
        b_spec = pl.BlockSpec((None, tk, tn), lambda i, j, k: (j // per, k, j % per))
    else:
        b_spec = pl.BlockSpec((tk, tn), lambda i, j, k: (k, j))
    return _mm_call(
        name, a, b, add, grid=(M // tm, N // tn, nk),
        a_spec=pl.BlockSpec((tm, tk), lambda i, j, k: (i, k)), b_spec=b_spec,
        add_spec=pl.BlockSpec((tm, tn), lambda i, j, k: (i, j)),
        o_spec=pl.BlockSpec((tm, tn), lambda i, j, k: (i, j)),
        o_shape=(M, N), o_dtype=o_dtype, acc_shape=(tm, tn), dims=NN, nk=nk)


def _mm_nt(name, a, b, *, add=None, o_dtype=F32):
    M, K = a.shape
    sharded = b.ndim == 3
    N, Ks = b.shape[-2], b.shape[-1]
    tm, tn, tk = _tile(M, 512), _tile(N, 512), _tile(Ks, 2048)
    per = Ks // tk
    nk = K // tk
    if sharded:
        b_spec = pl.BlockSpec((None, tn, tk), lambda i, j, k: (k // per, j, k % per))
    else:
        b_spec = pl.BlockSpec((tn, tk), lambda i, j, k: (j, k))
    return _mm_call(
        name, a, b, add, grid=(M // tm, N // tn, nk),
        a_spec=pl.BlockSpec((tm, tk), lambda i, j, k: (i, k)), b_spec=b_spec,
        add_spec=pl.BlockSpec((tm, tn), lambda i, j, k: (i, j)),
        o_spec=pl.BlockSpec((tm, tn), lambda i, j, k: (i, j)),
        o_shape=(M, N), o_dtype=o_dtype, acc_shape=(tm, tn), dims=NT, nk=nk)


def _mm_tn(name, a, b, *, shards=1, o_dtype=BF):
    S, M = a.shape
    N = b.shape[1]
    Ns = N // shards
    tm, tn, tk = _tile(M, 1024), _tile(Ns, 1024), _tile(S, 512)
    per = Ns // tn
    nk = S // tk
    if shards > 1:
        o_spec = pl.BlockSpec((None, tm, tn), lambda i, j, k: (j // per, i, j % per))
        o_shape = (shards, M, Ns)
    else:
        o_spec = pl.BlockSpec((tm, tn), lambda i, j, k: (i, j))
        o_shape = (M, N)
    return _mm_call(
        name, a, b, None, grid=(M // tm, N // tn, nk),
        a_spec=pl.BlockSpec((tk, tm), lambda i, j, k: (k, i)),
        b_spec=pl.BlockSpec((tk, tn), lambda i, j, k: (k, j)),
        add_spec=None, o_spec=o_spec, o_shape=o_shape, o_dtype=o_dtype, acc_shape=(tm, tn), dims=TN, nk=nk)


def _up_fwd(h2, w_up):
    S, D = h2.shape
    G, _, C = w_up.shape
    tm = _tile(S, 256)
    return _mm_call(
        "up_fwd", h2, w_up, None, grid=(G, S // tm, 1),
        a_spec=pl.BlockSpec((tm, D), lambda g, i, k: (i, 0)),
        b_spec=pl.BlockSpec((None, D, C), lambda g, i, k: (g, 0, 0)),
        add_spec=None, o_spec=pl.BlockSpec((None, tm, C), lambda g, i, k: (g, i, 0)),
        o_shape=(G, S, C), o_dtype=F32, acc_shape=None, dims=NN, nk=1)


def _up_dgrad(du_pre, w_up):
    G, S, C = du_pre.shape
    D = w_up.shape[1]
    tm, tn = _tile(S, 1024), _tile(D, 1024)
    return _mm_call(
        "up_dgrad", du_pre, w_up, None, grid=(S // tm, D // tn, G),
        a_spec=pl.BlockSpec((None, tm, C), lambda i, j, g: (g, i, 0)),
        b_spec=pl.BlockSpec((None, tn, C), lambda i, j, g: (g, j, 0)),
        add_spec=None, o_spec=pl.BlockSpec((tm, tn), lambda i, j, g: (i, j)),
        o_shape=(S, D), o_dtype=F32, acc_shape=(tm, tn), dims=NT, nk=G)


def _up_wgrad(h2, du_pre):
    G, S, C = du_pre.shape
    D = h2.shape[1]
    tm, tk = _tile(D, 512), _tile(S, 512)
    return _mm_call(
        "up_wgrad", h2, du_pre, None, grid=(G, D // tm, S // tk),
        a_spec=pl.BlockSpec((tk, tm), lambda g, i, k: (k, i)),
        b_spec=pl.BlockSpec((None, tk, C), lambda g, i, k: (g, k, 0)),
        add_spec=None, o_spec=pl.BlockSpec((None, tm, C), lambda g, i, k: (g, i, 0)),
        o_shape=(G, D, C), o_dtype=BF, acc_shape=(tm, C), dims=TN, nk=S // tk)


def _down_fwd(act, w_down, x1):
    G, S, C = act.shape
    D = w_down.shape[2]
    tm, tn = _tile(S, 1024), _tile(D, 1024)
    return _mm_call(
        "down_fwd", act, w_down, x1, grid=(S // tm, D // tn, G),
        a_spec=pl.BlockSpec((None, tm, C), lambda i, j, g: (g, i, 0)),
        b_spec=pl.BlockSpec((None, C, tn), lambda i, j, g: (g, 0, j)),
        add_spec=pl.BlockSpec((tm, tn), lambda i, j, g: (i, j)),
        o_spec=pl.BlockSpec((tm, tn), lambda i, j, g: (i, j)),
        o_shape=(S, D), o_dtype=F32, acc_shape=(tm, tn), dims=NN, nk=G)


def _down_dgrad(dx2, w_down):
    S, D = dx2.shape
    G, C, _ = w_down.shape
    tm = _tile(S, 256)
    return _mm_call(
        "down_dgrad", dx2, w_down, None, grid=(G, S // tm, 1),
        a_spec=pl.BlockSpec((tm, D), lambda g, i, k: (i, 0)),
        b_spec=pl.BlockSpec((None, C, D), lambda g, i, k: (g, 0, 0)),
        add_spec=None, o_spec=pl.BlockSpec((None, tm, C), lambda g, i, k: (g, i, 0)),
        o_shape=(G, S, C), o_dtype=F32, acc_shape=None, dims=NT, nk=1)


def _down_wgrad(act, dx2):
    G, S, C = act.shape
    D = dx2.shape[1]
    tn, tk = _tile(D, 512), _tile(S, 512)
    return _mm_call(
        "down_wgrad", act, dx2, None, grid=(G, D // tn, S // tk),
        a_spec=pl.BlockSpec((None, tk, C), lambda g, j, k: (g, k, 0)),
        b_spec=pl.BlockSpec((tk, tn), lambda g, j, k: (k, j)),
        add_spec=None, o_spec=pl.BlockSpec((None, C, tn), lambda g, j, k: (g, 0, j)),
        o_shape=(G, C, D), o_dtype=BF, acc_shape=(C, tn), dims=TN, nk=S // tk)


def _row(ts, c):
    return pl.BlockSpec((ts, c), lambda i: (i, 0))


def _bcast(r, c):
    return pl.BlockSpec((r, c), lambda i: (0, 0))


def _accumulate(i, ref, val):
    @pl.when(i == 0)
    def _():
        ref[...] = val

    @pl.when(i > 0)
    def _():
        ref[...] += val


def _rstd(xv):
    return lax.rsqrt(jnp.mean(xv * xv, axis=-1, keepdims=True) + NORM_EPS)


def _rmsnorm_fwd(name, x, g):
    S, D = x.shape
    ts = _tile(S, 512)

    def body(x_ref, g_ref, o_ref):
        xv = x_ref[...]
        o_ref[...] = (xv * _rstd(xv) * g_ref[...]).astype(o_ref.dtype)

    return pl.pallas_call(
        body, name=name, grid=(S // ts,), in_specs=[_row(ts, D), _bcast(1, D)], out_specs=_row(ts, D),
        out_shape=jax.ShapeDtypeStruct((S, D), BF), compiler_params=_params("parallel"))(x, g)


def _norm_bwd_rows(dy, xv, g):
    r = _rstd(xv)
    xh = xv * r
    dxh = dy * g
    dx = r * (dxh - xh * jnp.mean(dxh * xh, axis=-1, keepdims=True))
    return dx, jnp.sum(dy * xh, axis=0, keepdims=True)


def _rmsnorm_bwd(name, dy, x, g, res):
    S, D = x.shape
    ts = _tile(S, 512)

    def body(dy_ref, x_ref, g_ref, res_ref, dx_ref, dg_ref):
        dx, dg = _norm_bwd_rows(dy_ref[...], x_ref[...], g_ref[...])
        dx_ref[...] = dx + res_ref[...]
        _accumulate(pl.program_id(0), dg_ref, dg)

    return pl.pallas_call(
        body, name=name, grid=(S // ts,),
        in_specs=[_row(ts, D), _row(ts, D), _bcast(1, D), _row(ts, D)],
        out_specs=[_row(ts, D), _bcast(1, D)],
        out_shape=[jax.ShapeDtypeStruct((S, D), F32), jax.ShapeDtypeStruct((1, D), F32)],
        compiler_params=_params("arbitrary"))(dy, x, g, res)


def _rope_tables(S):
    half = QK_ROPE // 2
    pos = jnp.arange(S, dtype=F32)
    inv_freq = ROPE_THETA ** (-jnp.arange(0, QK_ROPE, 2, dtype=F32) / QK_ROPE)
    ang = pos[:, None] * inv_freq[None, :]
    cos, sin = jnp.cos(ang), jnp.sin(ang)
    z = jnp.zeros((S, half), F32)
    return jnp.concatenate([cos, z, cos, z], axis=1), jnp.concatenate([-sin, z, sin, z], axis=1)


def _rope_lanes(x, cos, sin_signed, inverse):
    if inverse:
        return x * cos + pltpu.roll(x * sin_signed, LANE // 2, 1)
    return x * cos + pltpu.roll(x, LANE // 2, 1) * sin_signed


def _rope(name, x, cos, sin_signed, inverse):
    S, W = x.shape
    ts = _tile(S, 512)

    def body(x_ref, c_ref, s_ref, o_ref):
        c, s = c_ref[...], s_ref[...]
        for h in range(W // LANE):
            sl = slice(h * LANE, (h + 1) * LANE)
            o_ref[:, sl] = _rope_lanes(x_ref[:, sl], c, s, inverse).astype(o_ref.dtype)

    return pl.pallas_call(
        body, name=name, grid=(S // ts,), in_specs=[_row(ts, W), _row(ts, LANE), _row(ts, LANE)],
        out_specs=_row(ts, W), out_shape=jax.ShapeDtypeStruct((S, W), BF),
        compiler_params=_params("parallel"))(x, cos, sin_signed)


LAT_W = 1024
_CQ = slice(0, Q_RANK)
_CKV = slice(Q_RANK, Q_RANK + KV_RANK)
_KPE = slice(Q_RANK + KV_RANK, Q_RANK + KV_RANK + LANE)


def _mla_prep(lat, qg, kvg, cos, sin_signed):
    S = lat.shape[0]
    ts = _tile(S, 512)

    def body(lat_ref, qg_ref, kvg_ref, c_ref, s_ref, qn_ref, kvn_ref, kpe_ref):
        cq = lat_ref[:, _CQ]
        qn_ref[...] = (cq * _rstd(cq) * qg_ref[...]).astype(BF)
        ckv = lat_ref[:, _CKV]
        kvn_ref[...] = (ckv * _rstd(ckv) * kvg_ref[...]).astype(BF)
        kpe_ref[...] = _rope_lanes(lat_ref[:, _KPE], c_ref[...], s_ref[...], False).astype(BF)

    return pl.pallas_call(
        body, name="mla_prep", grid=(S // ts,),
        in_specs=[_row(ts, LAT_W), _bcast(1, Q_RANK), _bcast(1, KV_RANK), _row(ts, LANE), _row(ts, LANE)],
        out_specs=[_row(ts, Q_RANK), _row(ts, KV_RANK), _row(ts, LANE)],
        out_shape=[jax.ShapeDtypeStruct((S, Q_RANK), BF), jax.ShapeDtypeStruct((S, KV_RANK), BF),
                   jax.ShapeDtypeStruct((S, LANE), BF)],
        compiler_params=_params("parallel"))(lat, qg, kvg, cos, sin_signed)


def _mla_prep_bwd(lat, qg, kvg, cos, sin_signed, d_qn, d_kvn, d_kpe):
    S = lat.shape[0]
    ts = _tile(S, 512)

    def body(lat_ref, qg_ref, kvg_ref, c_ref, s_ref, dqn_ref, dkvn_ref, dkpe_ref, dlat_ref, dqg_ref, dkvg_ref):
        i = pl.program_id(0)
        dcq, dqg = _norm_bwd_rows(dqn_ref[...], lat_ref[:, _CQ], qg_ref[...])
        dckv, dkvg = _norm_bwd_rows(dkvn_ref[...], lat_ref[:, _CKV], kvg_ref[...])
        dlat_ref[:, _CQ] = dcq.astype(BF)
        dlat_ref[:, _CKV] = dckv.astype(BF)
        dlat_ref[:, _KPE] = _rope_lanes(dkpe_ref[...], c_ref[...], s_ref[...], True).astype(BF)
        dlat_ref[:, _KPE.stop:] = jnp.zeros((ts, LAT_W - _KPE.stop), BF)
        _accumulate(i, dqg_ref, dqg)
        _accumulate(i, dkvg_ref, dkvg)

    return pl.pallas_call(
        body, name="mla_prep_bwd", grid=(S // ts,),
        in_specs=[_row(ts, LAT_W), _bcast(1, Q_RANK), _bcast(1, KV_RANK), _row(ts, LANE), _row(ts, LANE),
                  _row(ts, Q_RANK), _row(ts, KV_RANK), _row(ts, LANE)],
        out_specs=[_row(ts, LAT_W), _bcast(1, Q_RANK), _bcast(1, KV_RANK)],
        out_shape=[jax.ShapeDtypeStruct((S, LAT_W), BF), jax.ShapeDtypeStruct((1, Q_RANK), F32),
                   jax.ShapeDtypeStruct((1, KV_RANK), F32)],
        compiler_params=_params("arbitrary"))(lat, qg, kvg, cos, sin_signed, d_qn, d_kvn, d_kpe)


def _sigmoid(z):
    return 1.0 / (1.0 + jnp.exp(-z))


def _merge_fwd(gpre, b_gate, o_a, o_b):
    S, D = o_a.shape
    ts = _tile(S, 256)

    def body(g_ref, b_ref, oa_ref, ob_ref, m_ref):
        ga = _sigmoid(g_ref[:, :D] + b_ref[:, :D])
        gb = _sigmoid(g_ref[:, D:] + b_ref[:, D:])
        m_ref[...] = (ga * oa_ref[...] + gb * ob_ref[...]).astype(BF)

    return pl.pallas_call(
        body, name="merge_fwd", grid=(S // ts,),
        in_specs=[_row(ts, 2 * D), _bcast(1, 2 * D), _row(ts, D), _row(ts, D)], out_specs=_row(ts, D),
        out_shape=jax.ShapeDtypeStruct((S, D), BF), compiler_params=_params("parallel"))(gpre, b_gate, o_a, o_b)


def _merge_bwd(d_merge, gpre, b_gate, o_a, o_b):
    S, D = o_a.shape
    ts = _tile(S, 256)

    def body(dm_ref, g_ref, b_ref, oa_ref, ob_ref, doa_ref, dob_ref, dg_ref, db_ref):
        dm = dm_ref[...]
        ga = _sigmoid(g_ref[:, :D] + b_ref[:, :D])
        gb = _sigmoid(g_ref[:, D:] + b_ref[:, D:])
        doa_ref[...] = (dm * ga).astype(BF)
        dob_ref[...] = (dm * gb).astype(BF)
        dga = dm * oa_ref[...] * ga * (1.0 - ga)
        dgb = dm * ob_ref[...] * gb * (1.0 - gb)
        dg_ref[:, :D] = dga.astype(BF)
        dg_ref[:, D:] = dgb.astype(BF)
        i = pl.program_id(0)
        part = jnp.concatenate([jnp.sum(dga, axis=0, keepdims=True), jnp.sum(dgb, axis=0, keepdims=True)], axis=1)
        _accumulate(i, db_ref, part)

    return pl.pallas_call(
        body, name="merge_bwd", grid=(S // ts,),
        in_specs=[_row(ts, D), _row(ts, 2 * D), _bcast(1, 2 * D), _row(ts, D), _row(ts, D)],
        out_specs=[_row(ts, D), _row(ts, D), _row(ts, 2 * D), _bcast(1, 2 * D)],
        out_shape=[jax.ShapeDtypeStruct((S, D), BF), jax.ShapeDtypeStruct((S, D), BF),
                   jax.ShapeDtypeStruct((S, 2 * D), BF), jax.ShapeDtypeStruct((1, 2 * D), F32)],
        compiler_params=_params("arbitrary"))(d_merge, gpre, b_gate, o_a, o_b)


def _final_loss(x2, tgt, gf):
    S, D = x2.shape
    ts = _tile(S, 512)

    def body(x_ref, t_ref, g_ref, dx_ref, dg_ref, loss_ref):
        i = pl.program_id(0)
        xv = x_ref[...]
        g = g_ref[...]
        y = xv * _rstd(xv) * g
        err = y - t_ref[...]
        dx, dg = _norm_bwd_rows(err * (1.0 / D), xv, g)
        dx_ref[...] = dx
        _accumulate(i, dg_ref, dg)
        part = 0.5 * jnp.sum(jnp.mean(err * err, axis=-1, keepdims=True), axis=0, keepdims=True)
        _accumulate(i, loss_ref, jnp.broadcast_to(part, (8, LANE)))

    return pl.pallas_call(
        body, name="final_loss", grid=(S // ts,),
        in_specs=[_row(ts, D), _row(ts, D), _bcast(1, D)],
        out_specs=[_row(ts, D), _bcast(1, D), _bcast(8, LANE)],
        out_shape=[jax.ShapeDtypeStruct((S, D), F32), jax.ShapeDtypeStruct((1, D), F32),
                   jax.ShapeDtypeStruct((8, LANE), F32)],
        compiler_params=_params("arbitrary"))(x2, tgt, gf)


HALO = 8


def _shift_down(cur, prev, k, rows):
    out = pltpu.roll(cur, k, 0)
    for j in range(k):
        out = jnp.where(rows == j, prev[HALO - k + j:HALO - k + j + 1, :], out)
    return out


def _shift_up(cur, nxt, k, rows, ts):
    out = pltpu.roll(cur, ts - k, 0)
    for j in range(k):
        out = jnp.where(rows == ts - k + j, nxt[j:j + 1, :], out)
    return out


def _conv_rows(cur, prev, w, b, rows):
    return b + w[0:1, :] * _shift_down(cur, prev, 2, rows) + w[1:2, :] * _shift_down(cur, prev, 1, rows) + w[2:3, :] * cur


def _conv_specs(ts, C, shard_of):
    nh = ts // HALO
    cur = pl.BlockSpec((None, ts, C), lambda g, i: (shard_of(g), i, 0))
    prev = pl.BlockSpec((None, HALO, C), lambda g, i: (shard_of(g), jnp.maximum(i * nh - 1, 0), 0))
    return cur, prev


def _ffn_act(u_pre, conv_w, conv_b):
    G4, S, C = u_pre.shape
    G = G4 // 2
    ts = _tile(S, 256)

    def body(up_ref, upp_ref, gt_ref, gtp_ref, wu_ref, wg_ref, bu_ref, bg_ref, act_ref):
        first = pl.program_id(1) == 0
        rows = lax.broadcasted_iota(jnp.int32, (ts, C), 0)
        pu = jnp.where(first, 0.0, upp_ref[...])
        pg = jnp.where(first, 0.0, gtp_ref[...])
        up = _conv_rows(up_ref[...], pu, wu_ref[...], bu_ref[...], rows)
        gate = _conv_rows(gt_ref[...], pg, wg_ref[...], bg_ref[...], rows)
        act_ref[...] = (gate * _sigmoid(gate) * up).astype(BF)

    cur_u, prev_u = _conv_specs(ts, C, lambda g: g)
    cur_g, prev_g = _conv_specs(ts, C, lambda g: g + G)
    w_u = pl.BlockSpec((None, 3, C), lambda g, i: (g, 0, 0))
    w_g = pl.BlockSpec((None, 3, C), lambda g, i: (g + G, 0, 0))
    b_u = pl.BlockSpec((None, 1, C), lambda g, i: (g, 0, 0))
    b_g = pl.BlockSpec((None, 1, C), lambda g, i: (g + G, 0, 0))
    return pl.pallas_call(
        body, name="ffn_act", grid=(G, S // ts),
        in_specs=[cur_u, prev_u, cur_g, prev_g, w_u, w_g, b_u, b_g],
        out_specs=pl.BlockSpec((None, ts, C), lambda g, i: (g, i, 0)),
        out_shape=jax.ShapeDtypeStruct((G, S, C), BF),
        compiler_params=_params("parallel", "parallel"))(u_pre, u_pre, u_pre, u_pre, conv_w, conv_w, conv_b, conv_b)


def _ffn_act_bwd(u_pre, conv_w, conv_b, d_act):
    G4, S, C = u_pre.shape
    G = G4 // 2
    ts = _tile(S, 256)

    def body(up_ref, upp_ref, gt_ref, gtp_ref, wu_ref, wg_ref, bu_ref, bg_ref, da_ref, dup_ref, dgt_ref):
        first = pl.program_id(1) == 0
        rows = lax.broadcasted_iota(jnp.int32, (ts, C), 0)
        pu = jnp.where(first, 0.0, upp_ref[...])
        pg = jnp.where(first, 0.0, gtp_ref[...])
        up = _conv_rows(up_ref[...], pu, wu_ref[...], bu_ref[...], rows)
        gate = _conv_rows(gt_ref[...], pg, wg_ref[...], bg_ref[...], rows)
        sg = _sigmoid(gate)
        da = da_ref[...]
        dup_ref[...] = da * (gate * sg)
        dgt_ref[...] = da * up * (sg * (1.0 + gate * (1.0 - sg)))

    cur_u, prev_u = _conv_specs(ts, C, lambda g: g)
    cur_g, prev_g = _conv_specs(ts, C, lambda g: g + G)
    w_u = pl.BlockSpec((None, 3, C), lambda g, i: (g, 0, 0))
    w_g = pl.BlockSpec((None, 3, C), lambda g, i: (g + G, 0, 0))
    b_u = pl.BlockSpec((None, 1, C), lambda g, i: (g, 0, 0))
    b_g = pl.BlockSpec((None, 1, C), lambda g, i: (g + G, 0, 0))
    blk = pl.BlockSpec((None, ts, C), lambda g, i: (g, i, 0))
    d_up, d_gate = pl.pallas_call(
        body, name="ffn_act_bwd", grid=(G, S // ts),
        in_specs=[cur_u, prev_u, cur_g, prev_g, w_u, w_g, b_u, b_g, blk],
        out_specs=[blk, blk],
        out_shape=[jax.ShapeDtypeStruct((G, S, C), F32), jax.ShapeDtypeStruct((G, S, C), F32)],
        compiler_params=_params("parallel", "parallel"))(
            u_pre, u_pre, u_pre, u_pre, conv_w, conv_w, conv_b, conv_b, d_act)
    return jnp.concatenate([d_up, d_gate], axis=0)


def _conv_bwd(du, u_pre, conv_w):
    G4, S, C = du.shape
    ts = _tile(S, 256)
    nh = ts // HALO
    last_halo = S // HALO - 1

    def body(du_ref, dun_ref, u_ref, up_ref, w_ref, dpre_ref, dw_ref, db_ref):
        i = pl.program_id(1)
        rows = lax.broadcasted_iota(jnp.int32, (ts, C), 0)
        du_c = du_ref[...]
        nxt = jnp.where(i == pl.num_programs(1) - 1, 0.0, dun_ref[...])
        prev = jnp.where(i == 0, 0.0, up_ref[...])
        w = w_ref[...]
        dpre = w[2:3, :] * du_c + w[1:2, :] * _shift_up(du_c, nxt, 1, rows, ts) + w[0:1, :] * _shift_up(du_c, nxt, 2, rows, ts)
        dpre_ref[...] = dpre.astype(BF)
        u_c = u_ref[...]
        dw = jnp.concatenate([
            jnp.sum(du_c * _shift_down(u_c, prev, 2, rows), axis=0, keepdims=True),
            jnp.sum(du_c * _shift_down(u_c, prev, 1, rows), axis=0, keepdims=True),
            jnp.sum(du_c * u_c, axis=0, keepdims=True)], axis=0)
        _accumulate(i, dw_ref, dw)
        _accumulate(i, db_ref, jnp.sum(du_c, axis=0, keepdims=True))

    cur = pl.BlockSpec((None, ts, C), lambda g, i: (g, i, 0))
    nxt = pl.BlockSpec((None, HALO, C), lambda g, i: (g, jnp.minimum((i + 1) * nh, last_halo), 0))
    prev = pl.BlockSpec((None, HALO, C), lambda g, i: (g, jnp.maximum(i * nh - 1, 0), 0))
    return pl.pallas_call(
        body, name="conv_bwd", grid=(G4, S // ts),
        in_specs=[cur, nxt, cur, prev, pl.BlockSpec((None, 3, C), lambda g, i: (g, 0, 0))],
        out_specs=[cur, pl.BlockSpec((None, 3, C), lambda g, i: (g, 0, 0)), pl.BlockSpec((None, 1, C), lambda g, i: (g, 0, 0))],
        out_shape=[jax.ShapeDtypeStruct((G4, S, C), BF), jax.ShapeDtypeStruct((G4, 3, C), F32),
                   jax.ShapeDtypeStruct((G4, 1, C), F32)],
        compiler_params=_params("parallel", "arbitrary"))(du, du, u_pre, u_pre, conv_w)


MLA_TQ = 512
MLA_TK = 512


def _mla_scores(qn_ref, qp_ref, kn_ref, kpe_ref, i, j):
    q = jnp.concatenate([qn_ref[...], qp_ref[...]], axis=1)
    k = jnp.concatenate([kn_ref[...], kpe_ref[...]], axis=1)
    s = lax.dot_general(q, k, NT, preferred_element_type=F32) * MLA_SCALE
    row = i * MLA_TQ + lax.broadcasted_iota(jnp.int32, s.shape, 0)
    col = j * MLA_TK + lax.broadcasted_iota(jnp.int32, s.shape, 1)
    return q, k, s, col <= row


def _mla_fwd(qn, qp, kn, kpe, v):
    S = qn.shape[0]
    nq, nk = S // MLA_TQ, S // MLA_TK

    def body(qn_ref, qp_ref, kn_ref, kpe_ref, v_ref, o_ref, lse_ref, m_scr, l_scr, acc_scr):
        i, j = pl.program_id(1), pl.program_id(2)

        @pl.when(j == 0)
        def _():
            m_scr[...] = jnp.full(m_scr.shape, -jnp.inf, F32)
            l_scr[...] = jnp.zeros(l_scr.shape, F32)
            acc_scr[...] = jnp.zeros(acc_scr.shape, F32)

        @pl.when(j <= i)
        def _():
            _, _, s, ok = _mla_scores(qn_ref, qp_ref, kn_ref, kpe_ref, i, j)
            s = jnp.where(ok, s, -jnp.inf)
            m_prev = m_scr[...]
            m_new = jnp.maximum(m_prev, jnp.max(s, axis=1, keepdims=True))
            p = jnp.exp(s - m_new)
            alpha = jnp.exp(m_prev - m_new)
            l_scr[...] = alpha * l_scr[...] + jnp.sum(p, axis=1, keepdims=True)
            acc_scr[...] = alpha * acc_scr[...] + lax.dot_general(p.astype(BF), v_ref[...], NN, preferred_element_type=F32)
            m_scr[...] = m_new

        @pl.when(j == i)
        def _():
            o_ref[...] = (acc_scr[...] / l_scr[...]).astype(BF)
            lse_ref[...] = jnp.broadcast_to(m_scr[...] + jnp.log(l_scr[...]), lse_ref.shape)

    qspec = pl.BlockSpec((MLA_TQ, HEAD), lambda h, i, j: (i, h))
    kspec = pl.BlockSpec((MLA_TK, HEAD), lambda h, i, j: (jnp.minimum(j, i), h))
    kpespec = pl.BlockSpec((MLA_TK, HEAD), lambda h, i, j: (jnp.minimum(j, i), 0))
    return pl.pallas_call(
        body, name="mla_fwd", grid=(MLA_HEADS, nq, nk),
        in_specs=[qspec, qspec, kspec, kpespec, kspec],
        out_specs=[qspec, pl.BlockSpec((None, MLA_TQ, LANE), lambda h, i, j: (h, i, 0))],
        out_shape=[jax.ShapeDtypeStruct((S, MLA_HEADS * HEAD), BF), jax.ShapeDtypeStruct((MLA_HEADS, S, LANE), F32)],
        scratch_shapes=[pltpu.VMEM((MLA_TQ, 1), F32), pltpu.VMEM((MLA_TQ, 1), F32), pltpu.VMEM((MLA_TQ, HEAD), F32)],
        compiler_params=_params("parallel", "parallel", "arbitrary"))(qn, qp, kn, kpe, v)


def _mla_p_ds(qn_ref, qp_ref, kn_ref, kpe_ref, v_ref, do_ref, o_ref, lse_ref, i, j):
    q, k, s, ok = _mla_scores(qn_ref, qp_ref, kn_ref, kpe_ref, i, j)
    p = jnp.exp(jnp.where(ok, s, -jnp.inf) - lse_ref[:, 0:1])
    do = do_ref[...]
    delta = jnp.sum(do.astype(F32) * o_ref[...].astype(F32), axis=1, keepdims=True)
    dp = lax.dot_general(do, v_ref[...], NT, preferred_element_type=F32)
    ds = p * (dp - delta) * MLA_SCALE
    return q, k, p, ds, do


def _mla_bwd_dq(qn, qp, kn, kpe, v, do, o, lse):
    S = qn.shape[0]
    nq, nk = S // MLA_TQ, S // MLA_TK

    def body(qn_ref, qp_ref, kn_ref, kpe_ref, v_ref, do_ref, o_ref, lse_ref, dqn_ref, dqp_ref, acc):
        i, j = pl.program_id(1), pl.program_id(2)

        @pl.when(j == 0)
        def _():
            acc[...] = jnp.zeros(acc.shape, F32)

        @pl.when(j <= i)
        def _():
            _, k, _, ds, _ = _mla_p_ds(qn_ref, qp_ref, kn_ref, kpe_ref, v_ref, do_ref, o_ref, lse_ref, i, j)
            acc[...] += lax.dot_general(ds.astype(BF), k, NN, preferred_element_type=F32)

        @pl.when(j == i)
        def _():
            dqn_ref[...] = acc[:, :HEAD].astype(BF)
            dqp_ref[...] = acc[:, HEAD:]

    qspec = pl.BlockSpec((MLA_TQ, HEAD), lambda h, i, j: (i, h))
    kspec = pl.BlockSpec((MLA_TK, HEAD), lambda h, i, j: (jnp.minimum(j, i), h))
    kpespec = pl.BlockSpec((MLA_TK, HEAD), lambda h, i, j: (jnp.minimum(j, i), 0))
    lsespec = pl.BlockSpec((None, MLA_TQ, LANE), lambda h, i, j: (h, i, 0))
    return pl.pallas_call(
        body, name="mla_bwd_dq", grid=(MLA_HEADS, nq, nk),
        in_specs=[qspec, qspec, kspec, kpespec, kspec, qspec, qspec, lsespec],
        out_specs=[qspec, qspec],
        out_shape=[jax.ShapeDtypeStruct((S, MLA_HEADS * HEAD), BF), jax.ShapeDtypeStruct((S, MLA_HEADS * HEAD), F32)],
        scratch_shapes=[pltpu.VMEM((MLA_TQ, 2 * HEAD), F32)],
        compiler_params=_params("parallel", "parallel", "arbitrary"))(qn, qp, kn, kpe, v, do, o, lse)


def _mla_bwd_dkv(qn, qp, kn, kpe, v, do, o, lse):
    S = qn.shape[0]
    nq, nk = S // MLA_TQ, S // MLA_TK

    def body(qn_ref, qp_ref, kn_ref, kpe_ref, v_ref, do_ref, o_ref, lse_ref, dkn_ref, dv_ref, dkpe_ref, dk_acc, dv_acc, dkpe_acc):
        j, h, i = pl.program_id(0), pl.program_id(1), pl.program_id(2)

        @pl.when(i == 0)
        def _():
            dk_acc[...] = jnp.zeros(dk_acc.shape, F32)
            dv_acc[...] = jnp.zeros(dv_acc.shape, F32)

        @pl.when((i == 0) & (h == 0))
        def _():
            dkpe_acc[...] = jnp.zeros(dkpe_acc.shape, F32)

        @pl.when(i >= j)
        def _():
            q, _, p, ds, do = _mla_p_ds(qn_ref, qp_ref, kn_ref, kpe_ref, v_ref, do_ref, o_ref, lse_ref, i, j)
            dv_acc[...] += lax.dot_general(p.astype(BF), do, TN, preferred_element_type=F32)
            dk_acc[...] += lax.dot_general(ds.astype(BF), q, TN, preferred_element_type=F32)

        @pl.when(i == nq - 1)
        def _():
            dkn_ref[...] = dk_acc[:, :HEAD].astype(BF)
            dv_ref[...] = dv_acc[...].astype(BF)
            dkpe_acc[...] += dk_acc[:, HEAD:]

        @pl.when((i == nq - 1) & (h == MLA_HEADS - 1))
        def _():
            dkpe_ref[...] = dkpe_acc[...]

    qspec = pl.BlockSpec((MLA_TQ, HEAD), lambda j, h, i: (jnp.maximum(i, j), h))
    kspec = pl.BlockSpec((MLA_TK, HEAD), lambda j, h, i: (j, h))
    kpespec = pl.BlockSpec((MLA_TK, HEAD), lambda j, h, i: (j, 0))
    lsespec = pl.BlockSpec((None, MLA_TQ, LANE), lambda j, h, i: (h, jnp.maximum(i, j), 0))
    return pl.pallas_call(
        body, name="mla_bwd_dkv", grid=(nk, MLA_HEADS, nq),
        in_specs=[qspec, qspec, kspec, kpespec, kspec, qspec, qspec, lsespec],
        out_specs=[kspec, kspec, kpespec],
        out_shape=[jax.ShapeDtypeStruct((S, MLA_HEADS * HEAD), BF), jax.ShapeDtypeStruct((S, MLA_HEADS * HEAD), BF),
                   jax.ShapeDtypeStruct((S, HEAD), F32)],
        scratch_shapes=[pltpu.VMEM((MLA_TK, 2 * HEAD), F32), pltpu.VMEM((MLA_TK, HEAD), F32), pltpu.VMEM((MLA_TK, HEAD), F32)],
        compiler_params=_params("parallel", "arbitrary", "arbitrary"))(qn, qp, kn, kpe, v, do, o, lse)


DIL_W = 3 * DIL_HPG * HEAD
DIL_O = DIL_HPG * HEAD


def _dil_slopes(g):
    return [2.0 ** (-ALIBI_MAX_BIAS * (g * DIL_HPG + hh + 1) / DIL_HEADS) for hh in range(DIL_HPG)]


def _dil_bias(dil):
    p = lax.broadcasted_iota(jnp.int32, (DIL_BLOCK, DIL_BLOCK), 0)
    kk = lax.broadcasted_iota(jnp.int32, (DIL_BLOCK, DIL_BLOCK), 1)
    jc = p - kk
    dist_c = (dil * jc).astype(F32)
    dist_p = (dil * (jc + DIL_BLOCK)).astype(F32)
    return jc >= 0, jc <= 0, dist_c, dist_p


def _dil_head(blk, hh):
    q = blk[:, hh * HEAD:(hh + 1) * HEAD]
    k = blk[:, DIL_O + hh * HEAD:DIL_O + (hh + 1) * HEAD]
    v = blk[:, 2 * DIL_O + hh * HEAD:2 * DIL_O + (hh + 1) * HEAD]
    return q, k, v


def _dil_s(q, k, slope, dist, ok):
    s = lax.dot_general(q, k, NT, preferred_element_type=F32) * DIL_SCALE - slope * dist
    return jnp.where(ok, s, -jnp.inf)


def _dil_view(a, dil):
    S, W = a.shape
    return a.reshape(S // dil, dil * W)


def _dil_fwd(qkv, g):
    _, dil = DIL_PATTERNS[g]
    S = qkv.shape[0]
    L = S // dil
    nb = L // DIL_BLOCK
    slopes = _dil_slopes(g)

    def body(cur_ref, prev_ref, o_ref, lse_ref):
        n = pl.program_id(1)
        ok_c, ok_p, dist_c, dist_p = _dil_bias(dil)
        ok_p = ok_p & (n > 0)
        cur, prev = cur_ref[...], prev_ref[...]
        for hh in range(DIL_HPG):
            q, kc, vc = _dil_head(cur, hh)
            _, kp, vp = _dil_head(prev, hh)
            sc = _dil_s(q, kc, slopes[hh], dist_c, ok_c)
            sp = _dil_s(q, kp, slopes[hh], dist_p, ok_p)
            m = jnp.maximum(jnp.max(sc, axis=1, keepdims=True), jnp.max(sp, axis=1, keepdims=True))
            pc, pp = jnp.exp(sc - m), jnp.exp(sp - m)
            l = jnp.sum(pc, axis=1, keepdims=True) + jnp.sum(pp, axis=1, keepdims=True)
            o = (lax.dot_general(pc.astype(BF), vc, NN, preferred_element_type=F32)
                 + lax.dot_general(pp.astype(BF), vp, NN, preferred_element_type=F32)) / l
            sl = slice(hh * HEAD, (hh + 1) * HEAD)
            o_ref[:, sl] = o
            lse_ref[:, sl] = jnp.broadcast_to(m + jnp.log(l), (DIL_BLOCK, HEAD))

    ospec = pl.BlockSpec((DIL_BLOCK, DIL_O), lambda r, n: (n, r))
    o, lse = pl.pallas_call(
        body, name=f"dil_fwd{g}", grid=(dil, nb),
        in_specs=[pl.BlockSpec((DIL_BLOCK, DIL_W), lambda r, n: (n, r)),
                  pl.BlockSpec((DIL_BLOCK, DIL_W), lambda r, n: (jnp.maximum(n - 1, 0), r))],
        out_specs=[ospec, ospec],
        out_shape=[jax.ShapeDtypeStruct((L, dil * DIL_O), F32), jax.ShapeDtypeStruct((L, dil * DIL_O), F32)],
        compiler_params=_params("parallel", "parallel"))(_dil_view(qkv, dil), _dil_view(qkv, dil))
    return o.reshape(S, DIL_O), lse.reshape(S, DIL_O)


def _dil_combine(os_, lses):
    S = os_[0].shape[0]
    ts = _tile(S, 512)

    def body(o0, o1, o2, l0, l1, l2, out_ref, lse_ref):
        a, b, c = l0[...], l1[...], l2[...]
        m = jnp.maximum(jnp.maximum(a, b), c)
        ea, eb, ec = jnp.exp(a - m), jnp.exp(b - m), jnp.exp(c - m)
        tot = ea + eb + ec
        out_ref[...] = ((ea * o0[...] + eb * o1[...] + ec * o2[...]) / tot).astype(BF)
        lse_ref[...] = m + jnp.log(tot)

    return pl.pallas_call(
        body, name="dil_combine", grid=(S // ts,), in_specs=[_row(ts, DIL_O)] * 6,
        out_specs=[_row(ts, DIL_O), _row(ts, DIL_O)],
        out_shape=[jax.ShapeDtypeStruct((S, DIL_O), BF), jax.ShapeDtypeStruct((S, DIL_O), F32)],
        compiler_params=_params("parallel"))(*os_, *lses)


def _dil_delta(do, out):
    S = do.shape[0]
    ts = _tile(S, 512)

    def body(do_ref, o_ref, d_ref):
        for hh in range(DIL_HPG):
            sl = slice(hh * HEAD, (hh + 1) * HEAD)
            d = jnp.sum(do_ref[:, sl].astype(F32) * o_ref[:, sl].astype(F32), axis=1, keepdims=True)
            d_ref[:, sl] = jnp.broadcast_to(d, (ts, HEAD))

    return pl.pallas_call(
        body, name="dil_delta", grid=(S // ts,), in_specs=[_row(ts, DIL_O)] * 2, out_specs=_row(ts, DIL_O),
        out_shape=jax.ShapeDtypeStruct((S, DIL_O), F32), compiler_params=_params("parallel"))(do, out)


def _dil_bwd(qkv, do, lse, delta, g):
    _, dil = DIL_PATTERNS[g]
    S = qkv.shape[0]
    L = S // dil
    nb = L // DIL_BLOCK
    slopes = _dil_slopes(g)

    def pair(q, k, v, do_h, lse_h, delta_h, slope, dist, ok):
        s = _dil_s(q, k, slope, dist, ok)
        p = jnp.exp(s - lse_h)
        dp = lax.dot_general(do_h, v, NT, preferred_element_type=F32)
        ds = (p * (dp - delta_h) * DIL_SCALE).astype(BF)
        return p.astype(BF), ds

    def body(cur_ref, prev_ref, next_ref, doc_ref, don_ref, lsec_ref, lsen_ref, dlc_ref, dln_ref, out_ref):
        n = pl.program_id(1)
        ok_c, ok_p0, dist_c, dist_p = _dil_bias(dil)
        ok_a = ok_p0 & (n > 0)
        ok_n = ok_p0 & (n < nb - 1)
        cur, prev, nxt = cur_ref[...], prev_ref[...], next_ref[...]
        for hh in range(DIL_HPG):
            sl = slice(hh * HEAD, (hh + 1) * HEAD)
            q, kc, vc = _dil_head(cur, hh)
            _, kp, vp = _dil_head(prev, hh)
            qn, _, _ = _dil_head(nxt, hh)
            do_c, do_n = doc_ref[:, sl], don_ref[:, sl]
            lse_c, lse_n = lsec_ref[:, sl][:, 0:1], lsen_ref[:, sl][:, 0:1]
            dl_c, dl_n = dlc_ref[:, sl][:, 0:1], dln_ref[:, sl][:, 0:1]
            _, ds_a = pair(q, kp, vp, do_c, lse_c, dl_c, slopes[hh], dist_p, ok_a)
            p_b, ds_b = pair(q, kc, vc, do_c, lse_c, dl_c, slopes[hh], dist_c, ok_c)
            p_n, ds_n = pair(qn, kc, vc, do_n, lse_n, dl_n, slopes[hh], dist_p, ok_n)
            dq = (lax.dot_general(ds_a, kp, NN, preferred_element_type=F32)
                  + lax.dot_general(ds_b, kc, NN, preferred_element_type=F32))
            dk = (lax.dot_general(ds_b, q, TN, preferred_element_type=F32)
                  + lax.dot_general(ds_n, qn, TN, preferred_element_type=F32))
            dv = (lax.dot_general(p_b, do_c, TN, preferred_element_type=F32)
                  + lax.dot_general(p_n, do_n, TN, preferred_element_type=F32))
            out_ref[:, sl] = dq.astype(BF)
            out_ref[:, DIL_O + hh * HEAD:DIL_O + (hh + 1) * HEAD] = dk.astype(BF)
            out_ref[:, 2 * DIL_O + hh * HEAD:2 * DIL_O + (hh + 1) * HEAD] = dv.astype(BF)

    cur_w = pl.BlockSpec((DIL_BLOCK, DIL_W), lambda r, n: (n, r))
    prev_w = pl.BlockSpec((DIL_BLOCK, DIL_W), lambda r, n: (jnp.maximum(n - 1, 0), r))
    next_w = pl.BlockSpec((DIL_BLOCK, DIL_W), lambda r, n: (jnp.minimum(n + 1, nb - 1), r))
    cur_o = pl.BlockSpec((DIL_BLOCK, DIL_O), lambda r, n: (n, r))
    next_o = pl.BlockSpec((DIL_BLOCK, DIL_O), lambda r, n: (jnp.minimum(n + 1, nb - 1), r))
    qv, dov, lsev, dlv = _dil_view(qkv, dil), _dil_view(do, dil), _dil_view(lse, dil), _dil_view(delta, dil)
    out = pl.pallas_call(
        body, name=f"dil_bwd{g}", grid=(dil, nb),
        in_specs=[cur_w, prev_w, next_w, cur_o, next_o, cur_o, next_o, cur_o, next_o],
        out_specs=cur_w, out_shape=jax.ShapeDtypeStruct((L, dil * DIL_W), BF),
        compiler_params=_params("parallel", "parallel"))(qv, qv, qv, dov, dov, lsev, lsev, dlv, dlv)
    return out.reshape(S, DIL_W)


def _adamw(name, w, g, m, v):
    R, C = w.shape
    tr = R if R <= 512 else _tile_rows(R, 256)

    def body(w_ref, g_ref, m_ref, v_ref, d_ref, nm_ref, nv_ref):
        gv = g_ref[...]
        nm = ADAM_B1 * m_ref[...] + (1.0 - ADAM_B1) * gv
        nv = ADAM_B2 * v_ref[...] + (1.0 - ADAM_B2) * (gv * gv)
        m_hat = nm / (1.0 - ADAM_B1 ** ADAM_STEP)
        v_hat = nv / (1.0 - ADAM_B2 ** ADAM_STEP)
        d_ref[...] = -ADAM_LR * (m_hat / (jnp.sqrt(v_hat) + ADAM_EPS) + ADAM_WD * w_ref[...])
        nm_ref[...] = nm
        nv_ref[...] = nv

    spec = pl.BlockSpec((tr, C), lambda i: (i, 0))
    shp = jax.ShapeDtypeStruct((R, C), F32)
    return pl.pallas_call(
        body, name=name, grid=(R // tr,), in_specs=[spec] * 4, out_specs=[spec] * 3, out_shape=[shp] * 3,
        compiler_params=_params("parallel"))(w, g, m, v)


def _tile_rows(n, pref, mult=8):
    t = (pref // mult) * mult
    while t >= mult:
        if n % t == 0:
            return t
        t -= mult
    return n


ANY = pl.BlockSpec(memory_space=pl.ANY)


def _place():
    x, y, c = lax.axis_index("x"), lax.axis_index("y"), lax.axis_index("c")
    chips = [(1 - x, y), (x, 1 - y), (1 - x, 1 - y)]
    chip_idx = [2 * cx + cy for cx, cy in chips]
    return x, y, c, 2 * x + y, chips, chip_idx


def _rcopy(src, dst, ssem, rsem, dev):
    return pltpu.make_async_remote_copy(src_ref=src, dst_ref=dst, send_sem=ssem, recv_sem=rsem,
                                        device_id=dev, device_id_type=MESH)


def _allgather_shards(shards, split):
    n = len(shards)

    def body(*refs):
        ins, outs = refs[:n], refs[n:2 * n]
        ssem, rsem, lsem = refs[2 * n:]
        x, y, c, me, chips, chip_idx = _place()
        sib = (x, y, 1 - c)
        local = [pltpu.make_async_copy(ins[i], outs[i].at[me], lsem.at[i]) for i in range(n)]
        for cp in local:
            cp.start()
        started = []

        def rows(i, hc):
            if not split[i]:
                return pl.ds(0, shards[i].shape[0])
            h = shards[i].shape[0] // 2
            return pl.ds(hc * h, h)

        for i in range(n):
            for k in range(3):
                cp = _rcopy(ins[i].at[rows(i, c)], outs[i].at[me, rows(i, c)], ssem.at[i, k], rsem.at[i, k], (*chips[k], c))
                cp.start()
                started.append(cp)
        for i in range(n):
            for k in range(3):
                landed = outs[i].at[chip_idx[k], rows(i, c)]
                _rcopy(landed, landed, ssem.at[i, k], rsem.at[i, k], (*chips[k], c)).wait_recv()
                if split[i]:
                    cp = _rcopy(landed, landed, ssem.at[i, 3 + k], rsem.at[i, 3 + k], sib)
                    cp.start()
                    started.append(cp)
        for i in range(n):
            if split[i]:
                for k in range(3):
                    other = outs[i].at[chip_idx[k], rows(i, 1 - c)]
                    _rcopy(other, other, ssem.at[i, 3 + k], rsem.at[i, 3 + k], sib).wait_recv()
        for cp in started:
            cp.wait_send()
        for cp in local:
            cp.wait()

    return pl.pallas_call(
        body, name="allgather_weights", in_specs=[ANY] * n, out_specs=[ANY] * n,
        out_shape=[jax.ShapeDtypeStruct((N_CHIPS,) + s.shape, s.dtype) for s in shards],
        scratch_shapes=[pltpu.SemaphoreType.DMA((n, 6)), pltpu.SemaphoreType.DMA((n, 6)), pltpu.SemaphoreType.DMA((n,))],
    )(*shards)


EXCHANGE_CHUNK_BYTES = 3 * 512 * 1024


def _half_geometry(R, C, axis):
    Rp, Cp = (R // 2, C) if axis == 0 else (R, C // 2)
    tr = _tile_rows(Rp, max(16, EXCHANGE_CHUNK_BYTES // (2 * Cp)), 16)
    return Rp, Cp, tr, Rp // tr


def _pair_sum(name, g, axis):
    G, R, C = g.shape
    Rp, Cp, tr, nb = _half_geometry(R, C, axis)
    steps = G * nb

    def half_block(s, b, h):
        return (s, h * nb + b, 0) if axis == 0 else (s, b, h)

    def body(c_ref, keep_ref, give_ref, out_ref, land, ssem, rsem, credit):
        x, y, c = lax.axis_index("x"), lax.axis_index("y"), lax.axis_index("c")
        sib = (x, y, 1 - c)
        t = pl.program_id(0) * nb + pl.program_id(1)
        slot = t % 2

        @pl.when(t >= 2)
        def _():
            pl.semaphore_wait(credit, 1)

        cp = _rcopy(give_ref.at[0], land.at[slot], ssem.at[slot], rsem.at[slot], sib)
        cp.start()
        cp.wait_recv()
        out_ref[...] = (keep_ref[...].astype(F32) + land[slot].astype(F32)).astype(BF)

        @pl.when(t + 2 < steps)
        def _():
            pl.semaphore_signal(credit, 1, device_id=sib, device_id_type=MESH)

        cp.wait_send()

    blk = (None, tr, Cp)
    grid_spec = pltpu.PrefetchScalarGridSpec(
        num_scalar_prefetch=1, grid=(G, nb),
        in_specs=[pl.BlockSpec(blk, lambda s, b, c_ref: half_block(s, b, c_ref[0])),
                  pl.BlockSpec((1, tr, Cp), lambda s, b, c_ref: half_block(s, b, 1 - c_ref[0]))],
        out_specs=pl.BlockSpec(blk, lambda s, b, c_ref: (s, b, 0)),
        scratch_shapes=[pltpu.VMEM((2, tr, Cp), BF), pltpu.SemaphoreType.DMA((2,)), pltpu.SemaphoreType.DMA((2,)),
                        pltpu.SemaphoreType.REGULAR])
    c_arr = lax.axis_index("c").astype(jnp.int32).reshape(1)
    return pl.pallas_call(
        body, name=name, grid_spec=grid_spec, out_shape=jax.ShapeDtypeStruct((G, Rp, Cp), BF),
        compiler_params=_params("arbitrary", "arbitrary"))(c_arr, g, g)


def _chip_sum(name, h):
    G, Rp, Cp = h.shape
    tr = _tile_rows(Rp, max(16, EXCHANGE_CHUNK_BYTES // (2 * Cp)), 16)
    nb = Rp // tr

    def body(idx_ref, keep_ref, g0_ref, g1_ref, g2_ref, out_ref, land, ssem, rsem, credit):
        x, y, c, _, chips, _ = _place()
        b = pl.program_id(0)
        slot = b % 2
        gives = (g0_ref, g1_ref, g2_ref)

        @pl.when(b >= 2)
        def _():
            for k in range(3):
                pl.semaphore_wait(credit.at[k], 1)

        cps = [_rcopy(gives[k].at[0], land.at[slot, k], ssem.at[slot, k], rsem.at[slot, k], (*chips[k], c)) for k in range(3)]
        for cp in cps:
            cp.start()
        for cp in cps:
            cp.wait_recv()
        acc = keep_ref[...].astype(F32)
        for k in range(3):
            acc = acc + land[slot, k].astype(F32)
        out_ref[...] = acc

        @pl.when(b + 2 < nb)
        def _():
            for k in range(3):
                pl.semaphore_signal(credit.at[k], 1, device_id=(*chips[k], c), device_id_type=MESH)

        for cp in cps:
            cp.wait_send()

    blk = (None, tr, Cp)
    grid_spec = pltpu.PrefetchScalarGridSpec(
        num_scalar_prefetch=1, grid=(nb,),
        in_specs=[pl.BlockSpec(blk if k == 0 else (1, tr, Cp), functools.partial(lambda b, idx_ref, k: (idx_ref[k], b, 0), k=k))
                  for k in range(4)],
        out_specs=pl.BlockSpec((tr, Cp), lambda b, idx_ref: (b, 0)),
        scratch_shapes=[pltpu.VMEM((2, 3, tr, Cp), BF), pltpu.SemaphoreType.DMA((2, 3)), pltpu.SemaphoreType.DMA((2, 3)),
                        pltpu.SemaphoreType.REGULAR((3,))])
    x, y = lax.axis_index("x"), lax.axis_index("y")
    idx = jnp.stack([2 * x + y, 2 * (1 - x) + y, 2 * x + (1 - y), 2 * (1 - x) + (1 - y)]).astype(jnp.int32)
    return pl.pallas_call(
        body, name=name, grid_spec=grid_spec, out_shape=jax.ShapeDtypeStruct((Rp, Cp), F32),
        compiler_params=_params("arbitrary"))(idx, h, h, h, h)


def _pair_join(name, f, axis):
    Rp, Cp = f.shape
    R, C = (2 * Rp, Cp) if axis == 0 else (Rp, 2 * Cp)
    tr = _tile_rows(Rp, max(8, EXCHANGE_CHUNK_BYTES // (4 * Cp)), 8)
    nb = Rp // tr

    def body(f_ref, full, ssem, rsem, lsem):
        x, y, c = lax.axis_index("x"), lax.axis_index("y"), lax.axis_index("c")
        sib = (x, y, 1 - c)
        b = pl.program_id(0)

        def place(h, r0, rows):
            if axis == 0:
                return full.at[pl.ds(pl.multiple_of(h * Rp + r0, 8), rows), :]
            return full.at[pl.ds(pl.multiple_of(r0, 8), rows), pl.ds(pl.multiple_of(h * Cp, LANE), Cp)]

        mine = place(c, b * tr, tr)
        loc = pltpu.make_async_copy(f_ref, mine, lsem)
        rem = _rcopy(f_ref, mine, ssem, rsem, sib)
        loc.start()
        rem.start()
        loc.wait()
        rem.wait_send()

        @pl.when(b == nb - 1)
        def _():
            theirs = place(1 - c, 0, Rp)
            _rcopy(theirs, theirs, ssem, rsem, sib).wait_recv()

    return pl.pallas_call(
        body, name=name, grid=(nb,), in_specs=[pl.BlockSpec((tr, Cp), lambda b: (b, 0))], out_specs=ANY,
        out_shape=jax.ShapeDtypeStruct((R, C), F32),
        scratch_shapes=[pltpu.SemaphoreType.DMA, pltpu.SemaphoreType.DMA, pltpu.SemaphoreType.DMA],
        compiler_params=_params("arbitrary"))(f)


def _allreduce_small(v):
    R, K = v.shape
    ndev = 8

    def body(v_ref, o_ref, land, ssem, rsem):
        x, y, c = lax.axis_index("x"), lax.axis_index("y"), lax.axis_index("c")
        me = 4 * x + 2 * y + c
        land[me] = v_ref[...]
        cps = []
        for r in range(1, ndev):
            fx, fy, fc = (r >> 2) & 1, (r >> 1) & 1, r & 1
            peer = (x ^ fx, y ^ fy, c ^ fc)
            cp = _rcopy(v_ref, land.at[me], ssem.at[r - 1], rsem.at[r - 1], peer)
            cp.start()
            cps.append((cp, 4 * peer[0] + 2 * peer[1] + peer[2], r))
        for cp, src, r in cps:
            cp.wait_send()
            _rcopy(v_ref, land.at[src], ssem.at[r - 1], rsem.at[r - 1], (x, y, c)).wait_recv()
        acc = land[0]
        for d in range(1, ndev):
            acc = acc + land[d]
        o_ref[...] = acc

    vm = pl.BlockSpec(memory_space=pltpu.VMEM)
    return pl.pallas_call(
        body, name="allreduce_small", in_specs=[vm], out_specs=vm, out_shape=jax.ShapeDtypeStruct((R, K), F32),
        scratch_shapes=[pltpu.VMEM((ndev, R, K), F32), pltpu.SemaphoreType.DMA((ndev - 1,)), pltpu.SemaphoreType.DMA((ndev - 1,))],
    )(v)


IN_SPLITS = (Q_RANK, KV_RANK, QK_ROPE, DIL_HEADS * HEAD, DIL_HEADS * HEAD, DIL_HEADS * HEAD, D_MODEL, D_MODEL)
IN_OFF = tuple(int(v) for v in np.cumsum((0,) + IN_SPLITS))


def _unshard_cols(g):
    G, K, Ns = g.shape
    return g.transpose(1, 0, 2).reshape(K, G * Ns)


def _shard_cols(w):
    K, N = w.shape
    return w.reshape(K, N_CHIPS, N // N_CHIPS).transpose(1, 0, 2)


def _rope_pad(w):
    half = QK_ROPE // 2
    z = jnp.zeros(w.shape[:-1] + (half,), w.dtype)
    return jnp.concatenate([w[..., :half], z, w[..., half:], z], axis=-1)


def _rope_unpad(w):
    half = QK_ROPE // 2
    return jnp.concatenate([w[..., :half], w[..., 2 * half:3 * half]], axis=-1)


def _split_w_in(w_in_g):
    w = _unshard_cols(w_in_g)
    K = w.shape[0]
    p = [w[:, IN_OFF[i]:IN_OFF[i + 1]] for i in range(8)]
    w_lat = jnp.concatenate([p[0], p[1], _rope_pad(p[2]), jnp.zeros((K, LAT_W - _KPE.stop), w.dtype)], axis=1)
    w_dil = [jnp.concatenate([p[3 + t][:, g * DIL_O:(g + 1) * DIL_O] for t in range(3)], axis=1) for g in range(DIL_GROUPS)]
    w_gate = jnp.concatenate([p[6], p[7]], axis=1)
    return w_lat, w_dil, w_gate


def _merge_dw_in(dw_lat, dw_dil, dw_gate):
    parts = [dw_lat[:, _CQ], dw_lat[:, _CKV], _rope_unpad(dw_lat[:, _KPE])]
    for t in range(3):
        parts += [dw_dil[g][:, t * DIL_O:(t + 1) * DIL_O] for g in range(DIL_GROUPS)]
    parts.append(dw_gate)
    return _shard_cols(jnp.concatenate(parts, axis=1))


def _split_w_uq(w_uq_g):
    w = _unshard_cols(w_uq_g)
    K = w.shape[0]
    w = w.reshape(K, MLA_HEADS, QK_NOPE + QK_ROPE)
    return w[:, :, :QK_NOPE].reshape(K, MLA_HEADS * HEAD), _rope_pad(w[:, :, QK_NOPE:]).reshape(K, MLA_HEADS * HEAD)


def _merge_dw_uq(dw_n, dw_p):
    K = dw_n.shape[0]
    w = jnp.concatenate([dw_n.reshape(K, MLA_HEADS, HEAD), _rope_unpad(dw_p.reshape(K, MLA_HEADS, HEAD))], axis=-1)
    return _shard_cols(w.reshape(K, MLA_HEADS * (QK_NOPE + QK_ROPE)))


def _split_w_ukv(w_ukv_g):
    w = _unshard_cols(w_ukv_g)
    K = w.shape[0]
    w = w.reshape(K, MLA_HEADS, 2 * HEAD)
    return w[:, :, :HEAD].reshape(K, MLA_HEADS * HEAD), w[:, :, HEAD:].reshape(K, MLA_HEADS * HEAD)


def _merge_dw_ukv(dw_k, dw_v):
    K = dw_k.shape[0]
    w = jnp.concatenate([dw_k.reshape(K, MLA_HEADS, HEAD), dw_v.reshape(K, MLA_HEADS, HEAD)], axis=-1)
    return _shard_cols(w.reshape(K, MLA_HEADS * 2 * HEAD))


def _local_step(x, tgt, W):
    S, D = x.shape
    cos, sin_s = _rope_tables(S)
    w_lat, w_dil, w_gate = _split_w_in(W["w_in"])
    w_uqn, w_uqp = _split_w_uq(W["w_uq"])
    w_k, w_v = _split_w_ukv(W["w_ukv"])
    w_o_mla, w_o_dil = W["w_o_mla"], W["w_o_dil"]
    w_out = W["w_out"].reshape(D, D)
    w_up = W["w_up"]
    G4, _, C = w_up.shape
    w_down = W["w_down"].reshape(G4 // 2, C, D)
    conv_w = W["conv_w"]
    conv_b = W["conv_b"].reshape(G4, 1, C)

    h = _rmsnorm_fwd("attn_norm", x, W["attn_norm_g"])
    lat = _mm_nn("proj_lat", h, w_lat)
    qkv = [_mm_nn(f"proj_dil{g}", h, w_dil[g], o_dtype=BF) for g in range(DIL_GROUPS)]
    gpre = _mm_nn("proj_gate", h, w_gate)
    qn_, kvn, kpe = _mla_prep(lat, W["q_norm_g"], W["kv_norm_g"], cos, sin_s)
    q_nope = _mm_nn("q_nope", qn_, w_uqn, o_dtype=BF)
    q_pe = _rope("q_rope", _mm_nn("q_pe", qn_, w_uqp), cos, sin_s, False)
    k_nope = _mm_nn("k_nope", kvn, w_k, o_dtype=BF)
    v_mla = _mm_nn("v_mla", kvn, w_v, o_dtype=BF)
    attn_a, lse_a = _mla_fwd(q_nope, q_pe, k_nope, kpe, v_mla)
    dil = [_dil_fwd(qkv[g], g) for g in range(DIL_GROUPS)]
    attn_b, lse_b = _dil_combine([o for o, _ in dil], [l for _, l in dil])
    o_a = _mm_nn("o_mla", attn_a, w_o_mla)
    o_b = _mm_nn("o_dil", attn_b, w_o_dil)
    merge = _merge_fwd(gpre, W["b_gate"], o_a, o_b)
    x1 = _mm_nn("out_proj", merge, w_out, add=x)
    h2 = _rmsnorm_fwd("ffn_norm", x1, W["ffn_norm_g"])
    u_pre = _up_fwd(h2, w_up)
    act = _ffn_act(u_pre, conv_w, conv_b)
    x2 = _down_fwd(act, w_down, x1)
    dx2, d_final_g, loss8 = _final_loss(x2, tgt, W["final_norm_g"])

    d_act = _down_dgrad(dx2, w_down)
    dw_down = _down_wgrad(act, dx2)
    du = _ffn_act_bwd(u_pre, conv_w, conv_b, d_act)
    du_pre, d_conv_w, d_conv_b = _conv_bwd(du, u_pre, conv_w)
    dh2 = _up_dgrad(du_pre, w_up)
    dw_up = _up_wgrad(h2, du_pre)
    dx1, d_ffn_g = _rmsnorm_bwd("ffn_norm_bwd", dh2, x1, W["ffn_norm_g"], dx2)
    d_merge = _mm_nt("out_proj_dgrad", dx1, w_out)
    dw_out = _mm_tn("out_proj_wgrad", merge, dx1)
    d_oa, d_ob, d_gpre, d_b_gate = _merge_bwd(d_merge, gpre, W["b_gate"], o_a, o_b)
    d_attn_a = _mm_nt("o_mla_dgrad", d_oa, w_o_mla, o_dtype=BF)
    dw_o_mla = _mm_tn("o_mla_wgrad", attn_a, d_oa, shards=N_CHIPS)
    d_attn_b = _mm_nt("o_dil_dgrad", d_ob, w_o_dil, o_dtype=BF)
    dw_o_dil = _mm_tn("o_dil_wgrad", attn_b, d_ob, shards=N_CHIPS)
    delta_b = _dil_delta(d_attn_b, attn_b)
    d_qkv = [_dil_bwd(qkv[g], d_attn_b, lse_b, delta_b, g) for g in range(DIL_GROUPS)]
    dq_nope, dq_pe_rot = _mla_bwd_dq(q_nope, q_pe, k_nope, kpe, v_mla, d_attn_a, attn_a, lse_a)
    dk_nope, dv_mla, dkpe_rot = _mla_bwd_dkv(q_nope, q_pe, k_nope, kpe, v_mla, d_attn_a, attn_a, lse_a)
    dq_pe = _rope("q_rope_bwd", dq_pe_rot, cos, sin_s, True)
    d_qn = _mm_nt("q_pe_dgrad", dq_pe, w_uqp, add=_mm_nt("q_nope_dgrad", dq_nope, w_uqn))
    d_kvn = _mm_nt("v_dgrad", dv_mla, w_v, add=_mm_nt("k_nope_dgrad", dk_nope, w_k))
    dw_uq = _merge_dw_uq(_mm_tn("q_nope_wgrad", qn_, dq_nope), _mm_tn("q_pe_wgrad", qn_, dq_pe))
    dw_ukv = _merge_dw_ukv(_mm_tn("k_nope_wgrad", kvn, dk_nope), _mm_tn("v_wgrad", kvn, dv_mla))
    d_lat, d_q_g, d_kv_g = _mla_prep_bwd(lat, W["q_norm_g"], W["kv_norm_g"], cos, sin_s, d_qn, d_kvn, dkpe_rot)
    dh = _mm_nt("proj_lat_dgrad", d_lat, w_lat)
    for g in range(DIL_GROUPS):
        dh = _mm_nt(f"proj_dil{g}_dgrad", d_qkv[g], w_dil[g], add=dh)
    dh = _mm_nt("proj_gate_dgrad", d_gpre, w_gate, add=dh)
    dw_in = _merge_dw_in(_mm_tn("proj_lat_wgrad", h, d_lat),
                         [_mm_tn(f"proj_dil{g}_wgrad", h, d_qkv[g]) for g in range(DIL_GROUPS)],
                         _mm_tn("proj_gate_wgrad", h, d_gpre))
    grad_x, d_attn_g = _rmsnorm_bwd("attn_norm_bwd", dh, x, W["attn_norm_g"], dx1)

    big = {"w_in": dw_in, "w_uq": dw_uq, "w_ukv": dw_ukv, "w_o_mla": dw_o_mla, "w_o_dil": dw_o_dil,
           "w_out": dw_out.reshape(N_CHIPS, D // N_CHIPS, D), "w_up": dw_up,
           "w_down": dw_down.reshape(N_CHIPS, (G4 // 2) * C // N_CHIPS, D)}
    small = {"attn_norm_g": d_attn_g, "b_gate": d_b_gate, "q_norm_g": d_q_g, "kv_norm_g": d_kv_g,
             "ffn_norm_g": d_ffn_g, "conv_w": d_conv_w, "conv_b": d_conv_b.reshape(1, G4 * C),
             "final_norm_g": d_final_g}
    return loss8[0, 0], grad_x, big, small


BIG = ("w_in", "w_uq", "w_ukv", "w_o_mla", "w_o_dil", "w_out", "w_up", "w_down")
SMALL = ("attn_norm_g", "b_gate", "q_norm_g", "kv_norm_g", "ffn_norm_g", "conv_w", "conv_b", "final_norm_g")
WEIGHTS = ("attn_norm_g", "w_in", "b_gate", "q_norm_g", "w_uq", "kv_norm_g", "w_ukv", "w_o_mla", "w_o_dil",
           "w_out", "ffn_norm_g", "w_up", "conv_w", "conv_b", "w_down", "final_norm_g")
SMALL_ROWS = 8
HALF_AXIS = {"w_down": 1}


def _reduce_big(big):
    out = {}
    for n in BIG:
        axis = HALF_AXIS.get(n, 0)
        half = _chip_sum(f"chip_sum_{n}", _pair_sum(f"pair_sum_{n}", big[n], axis))
        out[n] = _pair_join(f"pair_join_{n}", half, axis)
    return out


def _reduce_small(small):
    flat = [small[n].reshape(-1) for n in SMALL]
    sizes = [f.shape[0] for f in flat]
    total = sum(sizes)
    width = -(-total // (SMALL_ROWS * LANE)) * LANE
    packed = jnp.concatenate(flat + [jnp.zeros((SMALL_ROWS * width - total,), F32)]).reshape(SMALL_ROWS, width)
    red = _allreduce_small(packed).reshape(-1)
    out, off = {}, 0
    for n, s in zip(SMALL, sizes):
        out[n] = red[off:off + s]
        off += s
    return out


def kernel(x, attn_norm_g, w_in, b_gate, q_norm_g, w_uq, kv_norm_g, w_ukv, w_o_mla, w_o_dil, w_out, ffn_norm_g, w_up, conv_w, conv_b, w_down, final_norm_g, loss_target, m_attn_norm_g, m_w_in, m_b_gate, m_q_norm_g, m_w_uq, m_kv_norm_g, m_w_ukv, m_w_o_mla, m_w_o_dil, m_w_out, m_ffn_norm_g, m_w_up, m_conv_w, m_conv_b, m_w_down, m_final_norm_g, v_attn_norm_g, v_w_in, v_b_gate, v_q_norm_g, v_w_uq, v_kv_norm_g, v_w_ukv, v_w_o_mla, v_w_o_dil, v_w_out, v_ffn_norm_g, v_w_up, v_conv_w, v_conv_b, v_w_down, v_final_norm_g):
    given = dict(attn_norm_g=attn_norm_g, w_in=w_in, b_gate=b_gate, q_norm_g=q_norm_g, w_uq=w_uq, kv_norm_g=kv_norm_g,
                 w_ukv=w_ukv, w_o_mla=w_o_mla, w_o_dil=w_o_dil, w_out=w_out, ffn_norm_g=ffn_norm_g, w_up=w_up,
                 conv_w=conv_w, conv_b=conv_b, w_down=w_down, final_norm_g=final_norm_g)
    moments_m = dict(attn_norm_g=m_attn_norm_g, w_in=m_w_in, b_gate=m_b_gate, q_norm_g=m_q_norm_g, w_uq=m_w_uq,
                     kv_norm_g=m_kv_norm_g, w_ukv=m_w_ukv, w_o_mla=m_w_o_mla, w_o_dil=m_w_o_dil, w_out=m_w_out,
                     ffn_norm_g=m_ffn_norm_g, w_up=m_w_up, conv_w=m_conv_w, conv_b=m_conv_b, w_down=m_w_down,
                     final_norm_g=m_final_norm_g)
    moments_v = dict(attn_norm_g=v_attn_norm_g, w_in=v_w_in, b_gate=v_b_gate, q_norm_g=v_q_norm_g, w_uq=v_w_uq,
                     kv_norm_g=v_kv_norm_g, w_ukv=v_w_ukv, w_o_mla=v_w_o_mla, w_o_dil=v_w_o_dil, w_out=v_w_out,
                     ffn_norm_g=v_ffn_norm_g, w_up=v_w_up, conv_w=v_conv_w, conv_b=v_conv_b, w_down=v_w_down,
                     final_norm_g=v_final_norm_g)

    shards = [given[n][0].astype(BF) for n in BIG] + [given["conv_w"][0]]
    gathered = _allgather_shards(shards, [True] * len(BIG) + [False])
    W = dict(zip(BIG + ("conv_w",), gathered))
    for n in ("attn_norm_g", "b_gate", "q_norm_g", "kv_norm_g", "ffn_norm_g", "conv_b"):
        W[n] = given[n]
    W["final_norm_g"] = given["final_norm_g"].reshape(1, -1)

    loss_part, grad_x, big, small = _local_step(x[0], loss_target[0], W)
    loss = lax.psum(loss_part, ("x", "y", "c"))

    g_big = _reduce_big(big)
    g_small = _reduce_small(small)
    chip = 2 * lax.axis_index("x") + lax.axis_index("y")
    grads = {}
    for n in BIG:
        grads[n] = g_big[n].reshape(given[n].shape)
    for n in SMALL:
        if n == "conv_w":
            full = g_small[n].reshape(N_CHIPS, 3, -1)
            grads[n] = lax.dynamic_index_in_dim(full, chip, 0, keepdims=True)
        else:
            grads[n] = g_small[n].reshape(given[n].shape)

    delta, new_m, new_v = {}, {}, {}
    for n in WEIGHTS:
        shp = given[n].shape
        two_d = (-1, shp[-1]) if len(shp) > 1 else (1, -1)
        d, nm, nv = _adamw(f"adamw_{n}", given[n].reshape(two_d), grads[n].reshape(two_d),
                           moments_m[n].reshape(two_d), moments_v[n].reshape(two_d))
        delta[n], new_m[n], new_v[n] = d.reshape(shp), nm.reshape(shp), nv.reshape(shp)

    return (loss, grad_x[None], *[grads[n] for n in WEIGHTS], *[delta[n] for n in WEIGHTS],
            *[new_m[n] for n in WEIGHTS], *[new_v[n] for n in WEIGHTS])
```

```python
import functools
import math

import numpy as np
import jax
import jax.numpy as jnp
from jax import lax
from jax.experimental import pallas as pl
from jax.experimental.pallas import tpu as pltpu

F32 = jnp.float32
BF = jnp.bfloat16
MESH = pl.DeviceIdType.MESH

D_MODEL = 2048
MLA_HEADS = 8
QK_NOPE = 128
QK_ROPE = 64
Q_RANK = 512
KV_RANK = 256
ROPE_THETA = 10000.0
DIL_PATTERNS = ((128, 1), (512, 4), (2048, 16))
DIL_GROUPS = 3
DIL_HPG = 4
DIL_HEADS = 12
HEAD = 128
DIL_BLOCK = 128
ALIBI_MAX_BIAS = 8.0
NORM_EPS = 1e-6
N_CHIPS = 4
ADAM_LR = 0.001
ADAM_B1 = 0.9
ADAM_B2 = 0.999
ADAM_EPS = 1e-08
ADAM_WD = 0.01
ADAM_STEP = 10

LANE = 128
VMEM_LIMIT = 56 * 1024 * 1024
MLA_SCALE = (QK_NOPE + QK_ROPE) ** -0.5
DIL_SCALE = HEAD ** -0.5


def _params(*sem):
    return pltpu.CompilerParams(dimension_semantics=sem, vmem_limit_bytes=VMEM_LIMIT)


def _tile(n, pref):
    t = (pref // LANE) * LANE
    while t >= LANE:
        if n % t == 0:
            return t
        t -= LANE
    return n


NN = (((1,), (0,)), ((), ()))
NT = (((1,), (1,)), ((), ()))
TN = (((0,), (0,)), ((), ()))


def _mm_call(name, a, b, add, *, grid, a_spec, b_spec, add_spec, o_spec, o_shape, o_dtype, acc_shape, dims, nk):
    nax = len(grid)

    def body(*refs):
        if add is None:
            a_ref, b_ref, o_ref = refs[:3]
            c_ref = None
            scr = refs[3:]
        else:
            a_ref, b_ref, c_ref, o_ref = refs[:4]
            scr = refs[4:]
        prod = lax.dot_general(a_ref[...].astype(BF), b_ref[...].astype(BF), dims, preferred_element_type=F32)
        if nk == 1:
            if c_ref is not None:
                prod = prod + c_ref[...]
            o_ref[...] = prod.astype(o_ref.dtype)
        else:
            acc = scr[0]
            k = pl.program_id(nax - 1)

            @pl.when(k == 0)
            def _():
                if c_ref is not None:
                    acc[...] = prod + c_ref[...]
                else:
                    acc[...] = prod

            @pl.when(k > 0)
            def _():
                acc[...] += prod

            @pl.when(k == nk - 1)
            def _():
                o_ref[...] = acc[...].astype(o_ref.dtype)

    ins = [a, b] + ([] if add is None else [add])
    specs = [a_spec, b_spec] + ([] if add is None else [add_spec])
    sem = ("parallel",) * (nax - 1) + ("arbitrary",)
    return pl.pallas_call(
        body, name=name, grid=grid, in_specs=specs, out_specs=o_spec,
        out_shape=jax.ShapeDtypeStruct(o_shape, o_dtype),
        scratch_shapes=[] if nk == 1 else [pltpu.VMEM(acc_shape, F32)],
        compiler_params=_params(*sem),
    )(*ins)


def _mm_nn(name, a, b, *, add=None, o_dtype=F32):
    M, K = a.shape
    sharded = b.ndim == 3
    Ns = b.shape[-1]
    N = Ns * (b.shape[0] if sharded else 1)
    tm, tn, tk = _tile(M, 512), _tile(Ns, 512), _tile(K, 2048)
    per = Ns // tn
    nk = K // tk
    if sharded:
        b_spec = pl.BlockSpec((None, tk, tn), lambda i, j, k: (j // per, k, j % per))
    else:
        b_spec = pl.BlockSpec((tk, tn), lambda i, j, k: (k, j))
    return _mm_call(
        name, a, b, add, grid=(M // tm, N // tn, nk),
        a_spec=pl.BlockSpec((tm, tk), lambda i, j, k: (i, k)), b_spec=b_spec,
        add_spec=pl.BlockSpec((tm, tn), lambda i, j, k: (i, j)),
        o_spec=pl.BlockSpec((tm, tn), lambda i, j, k: (i, j)),
        o_shape=(M, N), o_dtype=o_dtype, acc_shape=(tm, tn), dims=NN, nk=nk)


def _mm_nt(name, a, b, *, add=None, o_dtype=F32):
    M, K = a.shape
    sharded = b.ndim == 3
    N, Ks = b.shape[-2], b.shape[-1]
    tm, tn, tk = _tile(M, 512), _tile(N, 512), _tile(Ks, 2048)
    per = Ks // tk
    nk = K // tk
    if sharded:
        b_spec = pl.BlockSpec((None, tn, tk), lambda i, j, k: (k // per, j, k % per))
    else:
        b_spec = pl.BlockSpec((tn, tk), lambda i, j, k: (j, k))
    return _mm_call(
        name, a, b, add, grid=(M // tm, N // tn, nk),
        a_spec=pl.BlockSpec((tm, tk), lambda i, j, k: (i, k)), b_spec=b_spec,
        add_spec=pl.BlockSpec((tm, tn), lambda i, j, k: (i, j)),
        o_spec=pl.BlockSpec((tm, tn), lambda i, j, k: (i, j)),
        o_shape=(M, N), o_dtype=o_dtype, acc_shape=(tm, tn), dims=NT, nk=nk)


def _mm_tn(name, a, b, *, shards=1, o_dtype=BF):
    S, M = a.shape
    N = b.shape[1]
    Ns = N // shards
    tm, tn, tk = _tile(M, 1024), _tile(Ns, 1024), _tile(S, 512)
    per = Ns // tn
    nk = S // tk
    if shards > 1:
        o_spec = pl.BlockSpec((None, tm, tn), lambda i, j, k: (j // per, i, j % per))
        o_shape = (shards, M, Ns)
    else:
        o_spec = pl.BlockSpec((tm, tn), lambda i, j, k: (i, j))
        o_shape = (M, N)
    return _mm_call(
        name, a, b, None, grid=(M // tm, N // tn, nk),
        a_spec=pl.BlockSpec((tk, tm), lambda i, j, k: (k, i)),
        b_spec=pl.BlockSpec((tk, tn), lambda i, j, k: (k, j)),
        add_spec=None, o_spec=o_spec, o_shape=o_shape, o_dtype=o_dtype, acc_shape=(tm, tn), dims=TN, nk=nk)


def _up_fwd(h2, w_up):
    S, D = h2.shape
    G, _, C = w_up.shape
    tm = _tile(S, 256)
    return _mm_call(
        "up_fwd", h2, w_up, None, grid=(G, S // tm, 1),
        a_spec=pl.BlockSpec((tm, D), lambda g, i, k: (i, 0)),
        b_spec=pl.BlockSpec((None, D, C), lambda g, i, k: (g, 0, 0)),
        add_spec=None, o_spec=pl.BlockSpec((None, tm, C), lambda g, i, k: (g, i, 0)),
        o_shape=(G, S, C), o_dtype=F32, acc_shape=None, dims=NN, nk=1)


def _up_dgrad(du_pre, w_up):
    G, S, C = du_pre.shape
    D = w_up.shape[1]
    tm, tn = _tile(S, 1024), _tile(D, 1024)
    return _mm_call(
        "up_dgrad", du_pre, w_up, None, grid=(S // tm, D // tn, G),
        a_spec=pl.BlockSpec((None, tm, C), lambda i, j, g: (g, i, 0)),
        b_spec=pl.BlockSpec((None, tn, C), lambda i, j, g: (g, j, 0)),
        add_spec=None, o_spec=pl.BlockSpec((tm, tn), lambda i, j, g: (i, j)),
        o_shape=(S, D), o_dtype=F32, acc_shape=(tm, tn), dims=NT, nk=G)


def _up_wgrad(h2, du_pre):
    G, S, C = du_pre.shape
    D = h2.shape[1]
    tm, tk = _tile(D, 512), _tile(S, 512)
    return _mm_call(
        "up_wgrad", h2, du_pre, None, grid=(G, D // tm, S // tk),
        a_spec=pl.BlockSpec((tk, tm), lambda g, i, k: (k, i)),
        b_spec=pl.BlockSpec((None, tk, C), lambda g, i, k: (g, k, 0)),
        add_spec=None, o_spec=pl.BlockSpec((None, tm, C), lambda g, i, k: (g, i, 0)),
        o_shape=(G, D, C), o_dtype=BF, acc_shape=(tm, C), dims=TN, nk=S // tk)


def _down_fwd(act, w_down, x1):
    G, S, C = act.shape
    D = w_down.shape[2]
    tm, tn = _tile(S, 1024), _tile(D, 1024)
    return _mm_call(
        "down_fwd", act, w_down, x1, grid=(S // tm, D // tn, G),
        a_spec=pl.BlockSpec((None, tm, C), lambda i, j, g: (g, i, 0)),
        b_spec=pl.BlockSpec((None, C, tn), lambda i, j, g: (g, 0, j)),
        add_spec=pl.BlockSpec((tm, tn), lambda i, j, g: (i, j)),
        o_spec=pl.BlockSpec((tm, tn), lambda i, j, g: (i, j)),
        o_shape=(S, D), o_dtype=F32, acc_shape=(tm, tn), dims=NN, nk=G)


def _down_dgrad(dx2, w_down):
    S, D = dx2.shape
    G, C, _ = w_down.shape
    tm = _tile(S, 256)
    return _mm_call(
        "down_dgrad", dx2, w_down, None, grid=(G, S // tm, 1),
        a_spec=pl.BlockSpec((tm, D), lambda g, i, k: (i, 0)),
        b_spec=pl.BlockSpec((None, C, D), lambda g, i, k: (g, 0, 0)),
        add_spec=None, o_spec=pl.BlockSpec((None, tm, C), lambda g, i, k: (g, i, 0)),
        o_shape=(G, S, C), o_dtype=F32, acc_shape=None, dims=NT, nk=1)


def _down_wgrad(act, dx2):
    G, S, C = act.shape
    D = dx2.shape[1]
    tn, tk = _tile(D, 512), _tile(S, 512)
    return _mm_call(
        "down_wgrad", act, dx2, None, grid=(G, D // tn, S // tk),
        a_spec=pl.BlockSpec((None, tk, C), lambda g, j, k: (g, k, 0)),
        b_spec=pl.BlockSpec((tk, tn), lambda g, j, k: (k, j)),
        add_spec=None, o_spec=pl.BlockSpec((None, C, tn), lambda g, j, k: (g, 0, j)),
        o_shape=(G, C, D), o_dtype=BF, acc_shape=(C, tn), dims=TN, nk=S // tk)


def _row(ts, c):
    return pl.BlockSpec((ts, c), lambda i: (i, 0))


def _bcast(r, c):
    return pl.BlockSpec((r, c), lambda i: (0, 0))


def _accumulate(i, ref, val):
    @pl.when(i == 0)
    def _():
        ref[...] = val

    @pl.when(i > 0)
    def _():
        ref[...] += val


def _rstd(xv):
    return lax.rsqrt(jnp.mean(xv * xv, axis=-1, keepdims=True) + NORM_EPS)


def _rmsnorm_fwd(name, x, g):
    S, D = x.shape
    ts = _tile(S, 512)

    def body(x_ref, g_ref, o_ref):
        xv = x_ref[...]
        o_ref[...] = (xv * _rstd(xv) * g_ref[...]).astype(o_ref.dtype)

    return pl.pallas_call(
        body, name=name, grid=(S // ts,), in_specs=[_row(ts, D), _bcast(1, D)], out_specs=_row(ts, D),
        out_shape=jax.ShapeDtypeStruct((S, D), BF), compiler_params=_params("parallel"))(x, g)


def _norm_bwd_rows(dy, xv, g):
    r = _rstd(xv)
    xh = xv * r
    dxh = dy * g
    dx = r * (dxh - xh * jnp.mean(dxh * xh, axis=-1, keepdims=True))
    return dx, jnp.sum(dy * xh, axis=0, keepdims=True)


def _rmsnorm_bwd(name, dy, x, g, res):
    S, D = x.shape
    ts = _tile(S, 512)

    def body(dy_ref, x_ref, g_ref, res_ref, dx_ref, dg_ref):
        dx, dg = _norm_bwd_rows(dy_ref[...], x_ref[...], g_ref[...])
        dx_ref[...] = dx + res_ref[...]
        _accumulate(pl.program_id(0), dg_ref, dg)

    return pl.pallas_call(
        body, name=name, grid=(S // ts,),
        in_specs=[_row(ts, D), _row(ts, D), _bcast(1, D), _row(ts, D)],
        out_specs=[_row(ts, D), _bcast(1, D)],
        out_shape=[jax.ShapeDtypeStruct((S, D), F32), jax.ShapeDtypeStruct((1, D), F32)],
        compiler_params=_params("arbitrary"))(dy, x, g, res)


def _rope_tables(S):
    half = QK_ROPE // 2
    pos = jnp.arange(S, dtype=F32)
    inv_freq = ROPE_THETA ** (-jnp.arange(0, QK_ROPE, 2, dtype=F32) / QK_ROPE)
    ang = pos[:, None] * inv_freq[None, :]
    cos, sin = jnp.cos(ang), jnp.sin(ang)
    z = jnp.zeros((S, half), F32)
    return jnp.concatenate([cos, z, cos, z], axis=1), jnp.concatenate([-sin, z, sin, z], axis=1)


def _rope_lanes(x, cos, sin_signed, inverse):
    if inverse:
        return x * cos + pltpu.roll(x * sin_signed, LANE // 2, 1)
    return x * cos + pltpu.roll(x, LANE // 2, 1) * sin_signed


def _rope(name, x, cos, sin_signed, inverse):
    S, W = x.shape
    ts = _tile(S, 512)

    def body(x_ref, c_ref, s_ref, o_ref):
        c, s = c_ref[...], s_ref[...]
        for h in range(W // LANE):
            sl = slice(h * LANE, (h + 1) * LANE)
            o_ref[:, sl] = _rope_lanes(x_ref[:, sl], c, s, inverse).astype(o_ref.dtype)

    return pl.pallas_call(
        body, name=name, grid=(S // ts,), in_specs=[_row(ts, W), _row(ts, LANE), _row(ts, LANE)],
        out_specs=_row(ts, W), out_shape=jax.ShapeDtypeStruct((S, W), BF),
        compiler_params=_params("parallel"))(x, cos, sin_signed)


LAT_W = 1024
_CQ = slice(0, Q_RANK)
_CKV = slice(Q_RANK, Q_RANK + KV_RANK)
_KPE = slice(Q_RANK + KV_RANK, Q_RANK + KV_RANK + LANE)


def _mla_prep(lat, qg, kvg, cos, sin_signed):
    S = lat.shape[0]
    ts = _tile(S, 512)

    def body(lat_ref, qg_ref, kvg_ref, c_ref, s_ref, qn_ref, kvn_ref, kpe_ref):
        cq = lat_ref[:, _CQ]
        qn_ref[...] = (cq * _rstd(cq) * qg_ref[...]).astype(BF)
        ckv = lat_ref[:, _CKV]
        kvn_ref[...] = (ckv * _rstd(ckv) * kvg_ref[...]).astype(BF)
        kpe_ref[...] = _rope_lanes(lat_ref[:, _KPE], c_ref[...], s_ref[...], False).astype(BF)

    return pl.pallas_call(
        body, name="mla_prep", grid=(S // ts,),
        in_specs=[_row(ts, LAT_W), _bcast(1, Q_RANK), _bcast(1, KV_RANK), _row(ts, LANE), _row(ts, LANE)],
        out_specs=[_row(ts, Q_RANK), _row(ts, KV_RANK), _row(ts, LANE)],
        out_shape=[jax.ShapeDtypeStruct((S, Q_RANK), BF), jax.ShapeDtypeStruct((S, KV_RANK), BF),
                   jax.ShapeDtypeStruct((S, LANE), BF)],
        compiler_params=_params("parallel"))(lat, qg, kvg, cos, sin_signed)


def _mla_prep_bwd(lat, qg, kvg, cos, sin_signed, d_qn, d_kvn, d_kpe):
    S = lat.shape[0]
    ts = _tile(S, 512)

    def body(lat_ref, qg_ref, kvg_ref, c_ref, s_ref, dqn_ref, dkvn_ref, dkpe_ref, dlat_ref, dqg_ref, dkvg_ref):
        i = pl.program_id(0)
        dcq, dqg = _norm_bwd_rows(dqn_ref[...], lat_ref[:, _CQ], qg_ref[...])
        dckv, dkvg = _norm_bwd_rows(dkvn_ref[...], lat_ref[:, _CKV], kvg_ref[...])
        dlat_ref[:, _CQ] = dcq.astype(BF)
        dlat_ref[:, _CKV] = dckv.astype(BF)
        dlat_ref[:, _KPE] = _rope_lanes(dkpe_ref[...], c_ref[...], s_ref[...], True).astype(BF)
        dlat_ref[:, _KPE.stop:] = jnp.zeros((ts, LAT_W - _KPE.stop), BF)
        _accumulate(i, dqg_ref, dqg)
        _accumulate(i, dkvg_ref, dkvg)

    return pl.pallas_call(
        body, name="mla_prep_bwd", grid=(S // ts,),
        in_specs=[_row(ts, LAT_W), _bcast(1, Q_RANK), _bcast(1, KV_RANK), _row(ts, LANE), _row(ts, LANE),
                  _row(ts, Q_RANK), _row(ts, KV_RANK), _row(ts, LANE)],
        out_specs=[_row(ts, LAT_W), _bcast(1, Q_RANK), _bcast(1, KV_RANK)],
        out_shape=[jax.ShapeDtypeStruct((S, LAT_W), BF), jax.ShapeDtypeStruct((1, Q_RANK), F32),
                   jax.ShapeDtypeStruct((1, KV_RANK), F32)],
        compiler_params=_params("arbitrary"))(lat, qg, kvg, cos, sin_signed, d_qn, d_kvn, d_kpe)


def _sigmoid(z):
    return 1.0 / (1.0 + jnp.exp(-z))


def _merge_fwd(gpre, b_gate, o_a, o_b):
    S, D = o_a.shape
    ts = _tile(S, 256)

    def body(g_ref, b_ref, oa_ref, ob_ref, m_ref):
        ga = _sigmoid(g_ref[:, :D] + b_ref[:, :D])
        gb = _sigmoid(g_ref[:, D:] + b_ref[:, D:])
        m_ref[...] = (ga * oa_ref[...] + gb * ob_ref[...]).astype(BF)

    return pl.pallas_call(
        body, name="merge_fwd", grid=(S // ts,),
        in_specs=[_row(ts, 2 * D), _bcast(1, 2 * D), _row(ts, D), _row(ts, D)], out_specs=_row(ts, D),
        out_shape=jax.ShapeDtypeStruct((S, D), BF), compiler_params=_params("parallel"))(gpre, b_gate, o_a, o_b)


def _merge_bwd(d_merge, gpre, b_gate, o_a, o_b):
    S, D = o_a.shape
    ts = _tile(S, 256)

    def body(dm_ref, g_ref, b_ref, oa_ref, ob_ref, doa_ref, dob_ref, dg_ref, db_ref):
        dm = dm_ref[...]
        ga = _sigmoid(g_ref[:, :D] + b_ref[:, :D])
        gb = _sigmoid(g_ref[:, D:] + b_ref[:, D:])
        doa_ref[...] = (dm * ga).astype(BF)
        dob_ref[...] = (dm * gb).astype(BF)
        dga = dm * oa_ref[...] * ga * (1.0 - ga)
        dgb = dm * ob_ref[...] * gb * (1.0 - gb)
        dg_ref[:, :D] = dga.astype(BF)
        dg_ref[:, D:] = dgb.astype(BF)
        i = pl.program_id(0)
        part = jnp.concatenate([jnp.sum(dga, axis=0, keepdims=True), jnp.sum(dgb, axis=0, keepdims=True)], axis=1)
        _accumulate(i, db_ref, part)

    return pl.pallas_call(
        body, name="merge_bwd", grid=(S // ts,),
        in_specs=[_row(ts, D), _row(ts, 2 * D), _bcast(1, 2 * D), _row(ts, D), _row(ts, D)],
        out_specs=[_row(ts, D), _row(ts, D), _row(ts, 2 * D), _bcast(1, 2 * D)],
        out_shape=[jax.ShapeDtypeStruct((S, D), BF), jax.ShapeDtypeStruct((S, D), BF),
                   jax.ShapeDtypeStruct((S, 2 * D), BF), jax.ShapeDtypeStruct((1, 2 * D), F32)],
        compiler_params=_params("arbitrary"))(d_merge, gpre, b_gate, o_a, o_b)


def _final_loss(x2, tgt, gf):
    S, D = x2.shape
    ts = _tile(S, 512)

    def body(x_ref, t_ref, g_ref, dx_ref, dg_ref, loss_ref):
        i = pl.program_id(0)
        xv = x_ref[...]
        g = g_ref[...]
        y = xv * _rstd(xv) * g
        err = y - t_ref[...]
        dx, dg = _norm_bwd_rows(err * (1.0 / D), xv, g)
        dx_ref[...] = dx
        _accumulate(i, dg_ref, dg)
        part = 0.5 * jnp.sum(jnp.mean(err * err, axis=-1, keepdims=True), axis=0, keepdims=True)
        _accumulate(i, loss_ref, jnp.broadcast_to(part, (8, LANE)))

    return pl.pallas_call(
        body, name="final_loss", grid=(S // ts,),
        in_specs=[_row(ts, D), _row(ts, D), _bcast(1, D)],
        out_specs=[_row(ts, D), _bcast(1, D), _bcast(8, LANE)],
        out_shape=[jax.ShapeDtypeStruct((S, D), F32), jax.ShapeDtypeStruct((1, D), F32),
                   jax.ShapeDtypeStruct((8, LANE), F32)],
        compiler_params=_params("arbitrary"))(x2, tgt, gf)


HALO = 8


def _shift_down(cur, prev, k, rows):
    out = pltpu.roll(cur, k, 0)
    for j in range(k):
        out = jnp.where(rows == j, prev[HALO - k + j:HALO - k + j + 1, :], out)
    return out


def _shift_up(cur, nxt, k, rows, ts):
    out = pltpu.roll(cur, ts - k, 0)
    for j in range(k):
        out = jnp.where(rows == ts - k + j, nxt[j:j + 1, :], out)
    return out


def _conv_rows(cur, prev, w, b, rows):
    return b + w[0:1, :] * _shift_down(cur, prev, 2, rows) + w[1:2, :] * _shift_down(cur, prev, 1, rows) + w[2:3, :] * cur


def _conv_specs(ts, C, shard_of):
    nh = ts // HALO
    cur = pl.BlockSpec((None, ts, C), lambda g, i: (shard_of(g), i, 0))
    prev = pl.BlockSpec((None, HALO, C), lambda g, i: (shard_of(g), jnp.maximum(i * nh - 1, 0), 0))
    return cur, prev


def _ffn_act(u_pre, conv_w, conv_b):
    G4, S, C = u_pre.shape
    G = G4 // 2
    ts = _tile(S, 256)

    def body(up_ref, upp_ref, gt_ref, gtp_ref, wu_ref, wg_ref, bu_ref, bg_ref, act_ref):
        first = pl.program_id(1) == 0
        rows = lax.broadcasted_iota(jnp.int32, (ts, C), 0)
        pu = jnp.where(first, 0.0, upp_ref[...])
        pg = jnp.where(first, 0.0, gtp_ref[...])
        up = _conv_rows(up_ref[...], pu, wu_ref[...], bu_ref[...], rows)
        gate = _conv_rows(gt_ref[...], pg, wg_ref[...], bg_ref[...], rows)
        act_ref[...] = (gate * _sigmoid(gate) * up).astype(BF)

    cur_u, prev_u = _conv_specs(ts, C, lambda g: g)
    cur_g, prev_g = _conv_specs(ts, C, lambda g: g + G)
    w_u = pl.BlockSpec((None, 3, C), lambda g, i: (g, 0, 0))
    w_g = pl.BlockSpec((None, 3, C), lambda g, i: (g + G, 0, 0))
    b_u = pl.BlockSpec((None, 1, C), lambda g, i: (g, 0, 0))
    b_g = pl.BlockSpec((None, 1, C), lambda g, i: (g + G, 0, 0))
    return pl.pallas_call(
        body, name="ffn_act", grid=(G, S // ts),
        in_specs=[cur_u, prev_u, cur_g, prev_g, w_u, w_g, b_u, b_g],
        out_specs=pl.BlockSpec((None, ts, C), lambda g, i: (g, i, 0)),
        out_shape=jax.ShapeDtypeStruct((G, S, C), BF),
        compiler_params=_params("parallel", "parallel"))(u_pre, u_pre, u_pre, u_pre, conv_w, conv_w, conv_b, conv_b)


def _ffn_act_bwd(u_pre, conv_w, conv_b, d_act):
    G4, S, C = u_pre.shape
    G = G4 // 2
    ts = _tile(S, 256)

    def body(up_ref, upp_ref, gt_ref, gtp_ref, wu_ref, wg_ref, bu_ref, bg_ref, da_ref, dup_ref, dgt_ref):
        first = pl.program_id(1) == 0
        rows = lax.broadcasted_iota(jnp.int32, (ts, C), 0)
        pu = jnp.where(first, 0.0, upp_ref[...])
        pg = jnp.where(first, 0.0, gtp_ref[...])
        up = _conv_rows(up_ref[...], pu, wu_ref[...], bu_ref[...], rows)
        gate = _conv_rows(gt_ref[...], pg, wg_ref[...], bg_ref[...], rows)
        sg = _sigmoid(gate)
        da = da_ref[...]
        dup_ref[...] = da * (gate * sg)
        dgt_ref[...] = da * up * (sg * (1.0 + gate * (1.0 - sg)))

    cur_u, prev_u = _conv_specs(ts, C, lambda g: g)
    cur_g, prev_g = _conv_specs(ts, C, lambda g: g + G)
    w_u = pl.BlockSpec((None, 3, C), lambda g, i: (g, 0, 0))
    w_g = pl.BlockSpec((None, 3, C), lambda g, i: (g + G, 0, 0))
    b_u = pl.BlockSpec((None, 1, C), lambda g, i: (g, 0, 0))
    b_g = pl.BlockSpec((None, 1, C), lambda g, i: (g + G, 0, 0))
    blk = pl.BlockSpec((None, ts, C), lambda g, i: (g, i, 0))
    d_up, d_gate = pl.pallas_call(
        body, name="ffn_act_bwd", grid=(G, S // ts),
        in_specs=[cur_u, prev_u, cur_g, prev_g, w_u, w_g, b_u, b_g, blk],
        out_specs=[blk, blk],
        out_shape=[jax.ShapeDtypeStruct((G, S, C), F32), jax.ShapeDtypeStruct((G, S, C), F32)],
        compiler_params=_params("parallel", "parallel"))(
            u_pre, u_pre, u_pre, u_pre, conv_w, conv_w, conv_b, conv_b, d_act)
    return jnp.concatenate([d_up, d_gate], axis=0)


def _conv_bwd(du, u_pre, conv_w):
    G4, S, C = du.shape
    ts = _tile(S, 256)
    nh = ts // HALO
    last_halo = S // HALO - 1

    def body(du_ref, dun_ref, u_ref, up_ref, w_ref, dpre_ref, dw_ref, db_ref):
        i = pl.program_id(1)
        rows = lax.broadcasted_iota(jnp.int32, (ts, C), 0)
        du_c = du_ref[...]
        nxt = jnp.where(i == pl.num_programs(1) - 1, 0.0, dun_ref[...])
        prev = jnp.where(i == 0, 0.0, up_ref[...])
        w = w_ref[...]
        dpre = w[2:3, :] * du_c + w[1:2, :] * _shift_up(du_c, nxt, 1, rows, ts) + w[0:1, :] * _shift_up(du_c, nxt, 2, rows, ts)
        dpre_ref[...] = dpre.astype(BF)
        u_c = u_ref[...]
        dw = jnp.concatenate([
            jnp.sum(du_c * _shift_down(u_c, prev, 2, rows), axis=0, keepdims=True),
            jnp.sum(du_c * _shift_down(u_c, prev, 1, rows), axis=0, keepdims=True),
            jnp.sum(du_c * u_c, axis=0, keepdims=True)], axis=0)
        _accumulate(i, dw_ref, dw)
        _accumulate(i, db_ref, jnp.sum(du_c, axis=0, keepdims=True))

    cur = pl.BlockSpec((None, ts, C), lambda g, i: (g, i, 0))
    nxt = pl.BlockSpec((None, HALO, C), lambda g, i: (g, jnp.minimum((i + 1) * nh, last_halo), 0))
    prev = pl.BlockSpec((None, HALO, C), lambda g, i: (g, jnp.maximum(i * nh - 1, 0), 0))
    return pl.pallas_call(
        body, name="conv_bwd", grid=(G4, S // ts),
        in_specs=[cur, nxt, cur, prev, pl.BlockSpec((None, 3, C), lambda g, i: (g, 0, 0))],
        out_specs=[cur, pl.BlockSpec((None, 3, C), lambda g, i: (g, 0, 0)), pl.BlockSpec((None, 1, C), lambda g, i: (g, 0, 0))],
        out_shape=[jax.ShapeDtypeStruct((G4, S, C), BF), jax.ShapeDtypeStruct((G4, 3, C), F32),
                   jax.ShapeDtypeStruct((G4, 1, C), F32)],
        compiler_params=_params("parallel", "arbitrary"))(du, du, u_pre, u_pre, conv_w)


MLA_TQ = 512
MLA_TK = 512


def _mla_scores(qn_ref, qp_ref, kn_ref, kpe_ref, i, j):
    q = jnp.concatenate([qn_ref[...], qp_ref[...]], axis=1)
    k = jnp.concatenate([kn_ref[...], kpe_ref[...]], axis=1)
    s = lax.dot_general(q, k, NT, preferred_element_type=F32) * MLA_SCALE
    row = i * MLA_TQ + lax.broadcasted_iota(jnp.int32, s.shape, 0)
    col = j * MLA_TK + lax.broadcasted_iota(jnp.int32, s.shape, 1)
    return q, k, s, col <= row


def _mla_fwd(qn, qp, kn, kpe, v):
    S = qn.shape[0]
    nq, nk = S // MLA_TQ, S // MLA_TK

    def body(qn_ref, qp_ref, kn_ref, kpe_ref, v_ref, o_ref, lse_ref, m_scr, l_scr, acc_scr):
        i, j = pl.program_id(1), pl.program_id(2)

        @pl.when(j == 0)
        def _():
            m_scr[...] = jnp.full(m_scr.shape, -jnp.inf, F32)
            l_scr[...] = jnp.zeros(l_scr.shape, F32)
            acc_scr[...] = jnp.zeros(acc_scr.shape, F32)

        @pl.when(j <= i)
        def _():
            _, _, s, ok = _mla_scores(qn_ref, qp_ref, kn_ref, kpe_ref, i, j)
            s = jnp.where(ok, s, -jnp.inf)
            m_prev = m_scr[...]
            m_new = jnp.maximum(m_prev, jnp.max(s, axis=1, keepdims=True))
            p = jnp.exp(s - m_new)
            alpha = jnp.exp(m_prev - m_new)
            l_scr[...] = alpha * l_scr[...] + jnp.sum(p, axis=1, keepdims=True)
            acc_scr[...] = alpha * acc_scr[...] + lax.dot_general(p.astype(BF), v_ref[...], NN, preferred_element_type=F32)
            m_scr[...] = m_new

        @pl.when(j == i)
        def _():
            o_ref[...] = (acc_scr[...] / l_scr[...]).astype(BF)
            lse_ref[...] = jnp.broadcast_to(m_scr[...] + jnp.log(l_scr[...]), lse_ref.shape)

    qspec = pl.BlockSpec((MLA_TQ, HEAD), lambda h, i, j: (i, h))
    kspec = pl.BlockSpec((MLA_TK, HEAD), lambda h, i, j: (jnp.minimum(j, i), h))
    kpespec = pl.BlockSpec((MLA_TK, HEAD), lambda h, i, j: (jnp.minimum(j, i), 0))
    return pl.pallas_call(
        body, name="mla_fwd", grid=(MLA_HEADS, nq, nk),
        in_specs=[qspec, qspec, kspec, kpespec, kspec],
        out_specs=[qspec, pl.BlockSpec((None, MLA_TQ, LANE), lambda h, i, j: (h, i, 0))],
        out_shape=[jax.ShapeDtypeStruct((S, MLA_HEADS * HEAD), BF), jax.ShapeDtypeStruct((MLA_HEADS, S, LANE), F32)],
        scratch_shapes=[pltpu.VMEM((MLA_TQ, 1), F32), pltpu.VMEM((MLA_TQ, 1), F32), pltpu.VMEM((MLA_TQ, HEAD), F32)],
        compiler_params=_params("parallel", "parallel", "arbitrary"))(qn, qp, kn, kpe, v)


def _mla_p_ds(qn_ref, qp_ref, kn_ref, kpe_ref, v_ref, do_ref, o_ref, lse_ref, i, j):
    q, k, s, ok = _mla_scores(qn_ref, qp_ref, kn_ref, kpe_ref, i, j)
    p = jnp.exp(jnp.where(ok, s, -jnp.inf) - lse_ref[:, 0:1])
    do = do_ref[...]
    delta = jnp.sum(do.astype(F32) * o_ref[...].astype(F32), axis=1, keepdims=True)
    dp = lax.dot_general(do, v_ref[...], NT, preferred_element_type=F32)
    ds = p * (dp - delta) * MLA_SCALE
    return q, k, p, ds, do


def _mla_bwd_dq(qn, qp, kn, kpe, v, do, o, lse):
    S = qn.shape[0]
    nq, nk = S // MLA_TQ, S // MLA_TK

    def body(qn_ref, qp_ref, kn_ref, kpe_ref, v_ref, do_ref, o_ref, lse_ref, dqn_ref, dqp_ref, acc):
        i, j = pl.program_id(1), pl.program_id(2)

        @pl.when(j == 0)
        def _():
            acc[...] = jnp.zeros(acc.shape, F32)

        @pl.when(j <= i)
        def _():
            _, k, _, ds, _ = _mla_p_ds(qn_ref, qp_ref, kn_ref, kpe_ref, v_ref, do_ref, o_ref, lse_ref, i, j)
            acc[...] += lax.dot_general(ds.astype(BF), k, NN, preferred_element_type=F32)

        @pl.when(j == i)
        def _():
            dqn_ref[...] = acc[:, :HEAD].astype(BF)
            dqp_ref[...] = acc[:, HEAD:]

    qspec = pl.BlockSpec((MLA_TQ, HEAD), lambda h, i, j: (i, h))
    kspec = pl.BlockSpec((MLA_TK, HEAD), lambda h, i, j: (jnp.minimum(j, i), h))
    kpespec = pl.BlockSpec((MLA_TK, HEAD), lambda h, i, j: (jnp.minimum(j, i), 0))
    lsespec = pl.BlockSpec((None, MLA_TQ, LANE), lambda h, i, j: (h, i, 0))
    return pl.pallas_call(
        body, name="mla_bwd_dq", grid=(MLA_HEADS, nq, nk),
        in_specs=[qspec, qspec, kspec, kpespec, kspec, qspec, qspec, lsespec],
        out_specs=[qspec, qspec],
        out_shape=[jax.ShapeDtypeStruct((S, MLA_HEADS * HEAD), BF), jax.ShapeDtypeStruct((S, MLA_HEADS * HEAD), F32)],
        scratch_shapes=[pltpu.VMEM((MLA_TQ, 2 * HEAD), F32)],
        compiler_params=_params("parallel", "parallel", "arbitrary"))(qn, qp, kn, kpe, v, do, o, lse)


def _mla_bwd_dkv(qn, qp, kn, kpe, v, do, o, lse):
    S = qn.shape[0]
    nq, nk = S // MLA_TQ, S // MLA_TK

    def body(qn_ref, qp_ref, kn_ref, kpe_ref, v_ref, do_ref, o_ref, lse_ref, dkn_ref, dv_ref, dkpe_ref, dk_acc, dv_acc, dkpe_acc):
        j, h, i = pl.program_id(0), pl.program_id(1), pl.program_id(2)

        @pl.when(i == 0)
        def _():
            dk_acc[...] = jnp.zeros(dk_acc.shape, F32)
            dv_acc[...] = jnp.zeros(dv_acc.shape, F32)

        @pl.when((i == 0) & (h == 0))
        def _():
            dkpe_acc[...] = jnp.zeros(dkpe_acc.shape, F32)

        @pl.when(i >= j)
        def _():
            q, _, p, ds, do = _mla_p_ds(qn_ref, qp_ref, kn_ref, kpe_ref, v_ref, do_ref, o_ref, lse_ref, i, j)
            dv_acc[...] += lax.dot_general(p.astype(BF), do, TN, preferred_element_type=F32)
            dk_acc[...] += lax.dot_general(ds.astype(BF), q, TN, preferred_element_type=F32)

        @pl.when(i == nq - 1)
        def _():
            dkn_ref[...] = dk_acc[:, :HEAD].astype(BF)
            dv_ref[...] = dv_acc[...].astype(BF)
            dkpe_acc[...] += dk_acc[:, HEAD:]

        @pl.when((i == nq - 1) & (h == MLA_HEADS - 1))
        def _():
            dkpe_ref[...] = dkpe_acc[...]

    qspec = pl.BlockSpec((MLA_TQ, HEAD), lambda j, h, i: (jnp.maximum(i, j), h))
    kspec = pl.BlockSpec((MLA_TK, HEAD), lambda j, h, i: (j, h))
    kpespec = pl.BlockSpec((MLA_TK, HEAD), lambda j, h, i: (j, 0))
    lsespec = pl.BlockSpec((None, MLA_TQ, LANE), lambda j, h, i: (h, jnp.maximum(i, j), 0))
    return pl.pallas_call(
        body, name="mla_bwd_dkv", grid=(nk, MLA_HEADS, nq),
        in_specs=[qspec, qspec, kspec, kpespec, kspec, qspec, qspec, lsespec],
        out_specs=[kspec, kspec, kpespec],
        out_shape=[jax.ShapeDtypeStruct((S, MLA_HEADS * HEAD), BF), jax.ShapeDtypeStruct((S, MLA_HEADS * HEAD), BF),
                   jax.ShapeDtypeStruct((S, HEAD), F32)],
        scratch_shapes=[pltpu.VMEM((MLA_TK, 2 * HEAD), F32), pltpu.VMEM((MLA_TK, HEAD), F32), pltpu.VMEM((MLA_TK, HEAD), F32)],
        compiler_params=_params("parallel", "arbitrary", "arbitrary"))(qn, qp, kn, kpe, v, do, o, lse)


DIL_W = 3 * DIL_HPG * HEAD
DIL_O = DIL_HPG * HEAD


def _dil_slopes(g):
    return [2.0 ** (-ALIBI_MAX_BIAS * (g * DIL_HPG + hh + 1) / DIL_HEADS) for hh in range(DIL_HPG)]


def _dil_bias(dil):
    p = lax.broadcasted_iota(jnp.int32, (DIL_BLOCK, DIL_BLOCK), 0)
    kk = lax.broadcasted_iota(jnp.int32, (DIL_BLOCK, DIL_BLOCK), 1)
    jc = p - kk
    dist_c = (dil * jc).astype(F32)
    dist_p = (dil * (jc + DIL_BLOCK)).astype(F32)
    return jc >= 0, jc <= 0, dist_c, dist_p


def _dil_head(blk, hh):
    q = blk[:, hh * HEAD:(hh + 1) * HEAD]
    k = blk[:, DIL_O + hh * HEAD:DIL_O + (hh + 1) * HEAD]
    v = blk[:, 2 * DIL_O + hh * HEAD:2 * DIL_O + (hh + 1) * HEAD]
    return q, k, v


def _dil_s(q, k, slope, dist, ok):
    s = lax.dot_general(q, k, NT, preferred_element_type=F32) * DIL_SCALE - slope * dist
    return jnp.where(ok, s, -jnp.inf)


def _dil_view(a, dil):
    S, W = a.shape
    return a.reshape(S // dil, dil * W)


def _dil_fwd(qkv, g):
    _, dil = DIL_PATTERNS[g]
    S = qkv.shape[0]
    L = S // dil
    nb = L // DIL_BLOCK
    slopes = _dil_slopes(g)

    def body(cur_ref, prev_ref, o_ref, lse_ref):
        n = pl.program_id(1)
        ok_c, ok_p, dist_c, dist_p = _dil_bias(dil)
        ok_p = ok_p & (n > 0)
        cur, prev = cur_ref[...], prev_ref[...]
        for hh in range(DIL_HPG):
            q, kc, vc = _dil_head(cur, hh)
            _, kp, vp = _dil_head(prev, hh)
            sc = _dil_s(q, kc, slopes[hh], dist_c, ok_c)
            sp = _dil_s(q, kp, slopes[hh], dist_p, ok_p)
            m = jnp.maximum(jnp.max(sc, axis=1, keepdims=True), jnp.max(sp, axis=1, keepdims=True))
            pc, pp = jnp.exp(sc - m), jnp.exp(sp - m)
            l = jnp.sum(pc, axis=1, keepdims=True) + jnp.sum(pp, axis=1, keepdims=True)
            o = (lax.dot_general(pc.astype(BF), vc, NN, preferred_element_type=F32)
                 + lax.dot_general(pp.astype(BF), vp, NN, preferred_element_type=F32)) / l
            sl = slice(hh * HEAD, (hh + 1) * HEAD)
            o_ref[:, sl] = o
            lse_ref[:, sl] = jnp.broadcast_to(m + jnp.log(l), (DIL_BLOCK, HEAD))

    ospec = pl.BlockSpec((DIL_BLOCK, DIL_O), lambda r, n: (n, r))
    o, lse = pl.pallas_call(
        body, name=f"dil_fwd{g}", grid=(dil, nb),
        in_specs=[pl.BlockSpec((DIL_BLOCK, DIL_W), lambda r, n: (n, r)),
                  pl.BlockSpec((DIL_BLOCK, DIL_W), lambda r, n: (jnp.maximum(n - 1, 0), r))],
        out_specs=[ospec, ospec],
        out_shape=[jax.ShapeDtypeStruct((L, dil * DIL_O), F32), jax.ShapeDtypeStruct((L, dil * DIL_O), F32)],
        compiler_params=_params("parallel", "parallel"))(_dil_view(qkv, dil), _dil_view(qkv, dil))
    return o.reshape(S, DIL_O), lse.reshape(S, DIL_O)


def _dil_combine(os_, lses):
    S = os_[0].shape[0]
    ts = _tile(S, 512)

    def body(o0, o1, o2, l0, l1, l2, out_ref, lse_ref):
        a, b, c = l0[...], l1[...], l2[...]
        m = jnp.maximum(jnp.maximum(a, b), c)
        ea, eb, ec = jnp.exp(a - m), jnp.exp(b - m), jnp.exp(c - m)
        tot = ea + eb + ec
        out_ref[...] = ((ea * o0[...] + eb * o1[...] + ec * o2[...]) / tot).astype(BF)
        lse_ref[...] = m + jnp.log(tot)

    return pl.pallas_call(
        body, name="dil_combine", grid=(S // ts,), in_specs=[_row(ts, DIL_O)] * 6,
        out_specs=[_row(ts, DIL_O), _row(ts, DIL_O)],
        out_shape=[jax.ShapeDtypeStruct((S, DIL_O), BF), jax.ShapeDtypeStruct((S, DIL_O), F32)],
        compiler_params=_params("parallel"))(*os_, *lses)


def _dil_delta(do, out):
    S = do.shape[0]
    ts = _tile(S, 512)

    def body(do_ref, o_ref, d_ref):
        for hh in range(DIL_HPG):
            sl = slice(hh * HEAD, (hh + 1) * HEAD)
            d = jnp.sum(do_ref[:, sl].astype(F32) * o_ref[:, sl].astype(F32), axis=1, keepdims=True)
            d_ref[:, sl] = jnp.broadcast_to(d, (ts, HEAD))

    return pl.pallas_call(
        body, name="dil_delta", grid=(S // ts,), in_specs=[_row(ts, DIL_O)] * 2, out_specs=_row(ts, DIL_O),
        out_shape=jax.ShapeDtypeStruct((S, DIL_O), F32), compiler_params=_params("parallel"))(do, out)


def _dil_bwd(qkv, do, lse, delta, g):
    _, dil = DIL_PATTERNS[g]
    S = qkv.shape[0]
    L = S // dil
    nb = L // DIL_BLOCK
    slopes = _dil_slopes(g)

    def pair(q, k, v, do_h, lse_h, delta_h, slope, dist, ok):
        s = _dil_s(q, k, slope, dist, ok)
        p = jnp.exp(s - lse_h)
        dp = lax.dot_general(do_h, v, NT, preferred_element_type=F32)
        ds = (p * (dp - delta_h) * DIL_SCALE).astype(BF)
        return p.astype(BF), ds

    def body(cur_ref, prev_ref, next_ref, doc_ref, don_ref, lsec_ref, lsen_ref, dlc_ref, dln_ref, out_ref):
        n = pl.program_id(1)
        ok_c, ok_p0, dist_c, dist_p = _dil_bias(dil)
        ok_a = ok_p0 & (n > 0)
        ok_n = ok_p0 & (n < nb - 1)
        cur, prev, nxt = cur_ref[...], prev_ref[...], next_ref[...]
        for hh in range(DIL_HPG):
            sl = slice(hh * HEAD, (hh + 1) * HEAD)
            q, kc, vc = _dil_head(cur, hh)
            _, kp, vp = _dil_head(prev, hh)
            qn, _, _ = _dil_head(nxt, hh)
            do_c, do_n = doc_ref[:, sl], don_ref[:, sl]
            lse_c, lse_n = lsec_ref[:, sl][:, 0:1], lsen_ref[:, sl][:, 0:1]
            dl_c, dl_n = dlc_ref[:, sl][:, 0:1], dln_ref[:, sl][:, 0:1]
            _, ds_a = pair(q, kp, vp, do_c, lse_c, dl_c, slopes[hh], dist_p, ok_a)
            p_b, ds_b = pair(q, kc, vc, do_c, lse_c, dl_c, slopes[hh], dist_c, ok_c)
            p_n, ds_n = pair(qn, kc, vc, do_n, lse_n, dl_n, slopes[hh], dist_p, ok_n)
            dq = (lax.dot_general(ds_a, kp, NN, preferred_element_type=F32)
                  + lax.dot_general(ds_b, kc, NN, preferred_element_type=F32))
            dk = (lax.dot_general(ds_b, q, TN, preferred_element_type=F32)
                  + lax.dot_general(ds_n, qn, TN, preferred_element_type=F32))
            dv = (lax.dot_general(p_b, do_c, TN, preferred_element_type=F32)
                  + lax.dot_general(p_n, do_n, TN, preferred_element_type=F32))
            out_ref[:, sl] = dq.astype(BF)
            out_ref[:, DIL_O + hh * HEAD:DIL_O + (hh + 1) * HEAD] = dk.astype(BF)
            out_ref[:, 2 * DIL_O + hh * HEAD:2 * DIL_O + (hh + 1) * HEAD] = dv.astype(BF)

    cur_w = pl.BlockSpec((DIL_BLOCK, DIL_W), lambda r, n: (n, r))
    prev_w = pl.BlockSpec((DIL_BLOCK, DIL_W), lambda r, n: (jnp.maximum(n - 1, 0), r))
    next_w = pl.BlockSpec((DIL_BLOCK, DIL_W), lambda r, n: (jnp.minimum(n + 1, nb - 1), r))
    cur_o = pl.BlockSpec((DIL_BLOCK, DIL_O), lambda r, n: (n, r))
    next_o = pl.BlockSpec((DIL_BLOCK, DIL_O), lambda r, n: (jnp.minimum(n + 1, nb - 1), r))
    qv, dov, lsev, dlv = _dil_view(qkv, dil), _dil_view(do, dil), _dil_view(lse, dil), _dil_view(delta, dil)
    out = pl.pallas_call(
        body, name=f"dil_bwd{g}", grid=(dil, nb),
        in_specs=[cur_w, prev_w, next_w, cur_o, next_o, cur_o, next_o, cur_o, next_o],
        out_specs=cur_w, out_shape=jax.ShapeDtypeStruct((L, dil * DIL_W), BF),
        compiler_params=_params("parallel", "parallel"))(qv, qv, qv, dov, dov, lsev, lsev, dlv, dlv)
    return out.reshape(S, DIL_W)


def _adamw(name, w, g, m, v):
    R, C = w.shape
    tr = R if R <= 512 else _tile_rows(R, 256)

    def body(w_ref, g_ref, m_ref, v_ref, d_ref, nm_ref, nv_ref):
        gv = g_ref[...]
        nm = ADAM_B1 * m_ref[...] + (1.0 - ADAM_B1) * gv
        nv = ADAM_B2 * v_ref[...] + (1.0 - ADAM_B2) * (gv * gv)
        m_hat = nm / (1.0 - ADAM_B1 ** ADAM_STEP)
        v_hat = nv / (1.0 - ADAM_B2 ** ADAM_STEP)
        d_ref[...] = -ADAM_LR * (m_hat / (jnp.sqrt(v_hat) + ADAM_EPS) + ADAM_WD * w_ref[...])
        nm_ref[...] = nm
        nv_ref[...] = nv

    spec = pl.BlockSpec((tr, C), lambda i: (i, 0))
    shp = jax.ShapeDtypeStruct((R, C), F32)
    return pl.pallas_call(
        body, name=name, grid=(R // tr,), in_specs=[spec] * 4, out_specs=[spec] * 3, out_shape=[shp] * 3,
        compiler_params=_params("parallel"))(w, g, m, v)


def _tile_rows(n, pref, mult=8):
    t = (pref // mult) * mult
    while t >= mult:
        if n % t == 0:
            return t
        t -= mult
    return n


ANY = pl.BlockSpec(memory_space=pl.ANY)


def _place():
    x, y, c = lax.axis_index("x"), lax.axis_index("y"), lax.axis_index("c")
    chips = [(1 - x, y), (x, 1 - y), (1 - x, 1 - y)]
    chip_idx = [2 * cx + cy for cx, cy in chips]
    return x, y, c, 2 * x + y, chips, chip_idx


def _rcopy(src, dst, ssem, rsem, dev):
    return pltpu.make_async_remote_copy(src_ref=src, dst_ref=dst, send_sem=ssem, recv_sem=rsem,
                                        device_id=dev, device_id_type=MESH)


HBM = pl.BlockSpec(memory_space=pltpu.HBM)
SEM = pl.BlockSpec(memory_space=pltpu.SEMAPHORE)
EFFECT = pltpu.SideEffectType.DATAFLOW_SIDE_EFFECTING


def _split_copies(kind, srcs, lands, ssem, rsem):
    _, _, c, me, chips, chip_idx = _place()
    cps = []
    for i in range(len(srcs)):
        for k in range(3):
            if kind == "gather":
                src, dst = srcs[i], lands[i].at[me]
            else:
                src, dst = srcs[i].at[chip_idx[k]], lands[i].at[k]
            cps.append(_rcopy(src, dst, ssem.at[3 * i + k], rsem.at[3 * i + k], (*chips[k], c)))
    return cps


def _exchange_start(name, kind, srcs, lands, groups):
    n, ng = len(srcs), len(groups)

    def body(*refs):
        src_refs, land_refs = refs[:n], refs[n:2 * n]
        sems = refs[2 * n:2 * n + 2 * ng]
        token = refs[-1]
        for gi, grp in enumerate(groups):
            cps = _split_copies(kind, [src_refs[i] for i in grp], [land_refs[i] for i in grp], sems[2 * gi], sems[2 * gi + 1])
            for cp in cps:
                cp.start()
        token[...] = jnp.zeros_like(token)

    arrays = list(srcs) + list(lands)
    out_shape = []
    for grp in groups:
        out_shape += [pltpu.SemaphoreType.DMA((3 * len(grp),)), pltpu.SemaphoreType.DMA((3 * len(grp),))]
    out_shape += [pltpu.HBM(a.shape, a.dtype) for a in arrays] + [jax.ShapeDtypeStruct((8, LANE), F32)]
    outs = pl.pallas_call(
        body, name=name, out_shape=out_shape, in_specs=[HBM] * (2 * n),
        out_specs=[SEM] * (2 * ng) + [HBM] * (2 * n) + [pl.BlockSpec(memory_space=pltpu.VMEM)],
        input_output_aliases={i: 2 * ng + i for i in range(2 * n)},
        compiler_params=pltpu.CompilerParams(has_side_effects=EFFECT),
    )(*[pltpu.with_memory_space_constraint(a, pltpu.HBM) for a in arrays])
    sems = [(outs[2 * gi], outs[2 * gi + 1]) for gi in range(ng)]
    thru = outs[2 * ng:2 * ng + 2 * n]
    return sems, thru[:n], thru[n:], outs[-1]


def _exchange_wait(name, kind, srcs, lands, sems, after):
    n = len(srcs)

    def body(*refs):
        cps = _split_copies(kind, refs[:n], refs[n:2 * n], refs[2 * n], refs[2 * n + 1])
        for cp in cps:
            cp.wait_send()
            cp.wait_recv()

    arrays = list(srcs) + list(lands)
    outs = pl.pallas_call(
        body, name=name, out_shape=[pltpu.HBM(a.shape, a.dtype) for a in arrays],
        in_specs=[HBM] * (2 * n) + [SEM, SEM, ANY], out_specs=[HBM] * (2 * n),
        input_output_aliases={i: i for i in range(2 * n)},
        compiler_params=pltpu.CompilerParams(has_side_effects=EFFECT),
    )(*arrays, sems[0], sems[1], after)
    return outs[:n], outs[n:]


EXCHANGE_CHUNK_BYTES = 3 * 512 * 1024


def _half_geometry(R, C, axis):
    Rp, Cp = (R // 2, C) if axis == 0 else (R, C // 2)
    tr = _tile_rows(Rp, max(16, EXCHANGE_CHUNK_BYTES // (2 * Cp)), 16)
    return Rp, Cp, tr, Rp // tr


def _pair_sum(name, g, axis):
    G, R, C = g.shape
    Rp, Cp, tr, nb = _half_geometry(R, C, axis)
    steps = G * nb

    def half_block(s, b, h):
        return (s, h * nb + b, 0) if axis == 0 else (s, b, h)

    def body(c_ref, keep_ref, give_ref, out_ref, land, ssem, rsem, credit):
        x, y, c = lax.axis_index("x"), lax.axis_index("y"), lax.axis_index("c")
        sib = (x, y, 1 - c)
        t = pl.program_id(0) * nb + pl.program_id(1)
        slot = t % 2

        @pl.when(t >= 2)
        def _():
            pl.semaphore_wait(credit, 1)

        cp = _rcopy(give_ref.at[0], land.at[slot], ssem.at[slot], rsem.at[slot], sib)
        cp.start()
        cp.wait_recv()
        out_ref[...] = (keep_ref[...].astype(F32) + land[slot].astype(F32)).astype(BF)

        @pl.when(t + 2 < steps)
        def _():
            pl.semaphore_signal(credit, 1, device_id=sib, device_id_type=MESH)

        cp.wait_send()

    blk = (None, tr, Cp)
    grid_spec = pltpu.PrefetchScalarGridSpec(
        num_scalar_prefetch=1, grid=(G, nb),
        in_specs=[pl.BlockSpec(blk, lambda s, b, c_ref: half_block(s, b, c_ref[0])),
                  pl.BlockSpec((1, tr, Cp), lambda s, b, c_ref: half_block(s, b, 1 - c_ref[0]))],
        out_specs=pl.BlockSpec(blk, lambda s, b, c_ref: (s, b, 0)),
        scratch_shapes=[pltpu.VMEM((2, tr, Cp), BF), pltpu.SemaphoreType.DMA((2,)), pltpu.SemaphoreType.DMA((2,)),
                        pltpu.SemaphoreType.REGULAR])
    c_arr = lax.axis_index("c").astype(jnp.int32).reshape(1)
    return pl.pallas_call(
        body, name=name, grid_spec=grid_spec, out_shape=jax.ShapeDtypeStruct((G, Rp, Cp), BF),
        compiler_params=_params("arbitrary", "arbitrary"))(c_arr, g, g)


def _chip_total(name, h, landed):
    G, Rp, Cp = h.shape
    tr = _tile_rows(Rp, max(16, EXCHANGE_CHUNK_BYTES // (2 * Cp)), 16)

    def body(me_ref, own_ref, l0_ref, l1_ref, l2_ref, out_ref):
        acc = own_ref[...].astype(F32)
        for r in (l0_ref, l1_ref, l2_ref):
            acc = acc + r[...].astype(F32)
        out_ref[...] = acc

    blk = (None, tr, Cp)
    grid_spec = pltpu.PrefetchScalarGridSpec(
        num_scalar_prefetch=1, grid=(Rp // tr,),
        in_specs=[pl.BlockSpec(blk, lambda b, me_ref: (me_ref[0], b, 0))]
        + [pl.BlockSpec(blk, functools.partial(lambda b, me_ref, k: (k, b, 0), k=k)) for k in range(3)],
        out_specs=pl.BlockSpec((tr, Cp), lambda b, me_ref: (b, 0)))
    me = (2 * lax.axis_index("x") + lax.axis_index("y")).astype(jnp.int32).reshape(1)
    return pl.pallas_call(
        body, name=name, grid_spec=grid_spec, out_shape=jax.ShapeDtypeStruct((Rp, Cp), F32),
        compiler_params=_params("parallel"))(me, h, landed, landed, landed)


def _pair_join(name, f, axis):
    Rp, Cp = f.shape
    R, C = (2 * Rp, Cp) if axis == 0 else (Rp, 2 * Cp)
    tr = _tile_rows(Rp, max(8, EXCHANGE_CHUNK_BYTES // (4 * Cp)), 8)
    nb = Rp // tr

    def body(f_ref, full, ssem, rsem, lsem):
        x, y, c = lax.axis_index("x"), lax.axis_index("y"), lax.axis_index("c")
        sib = (x, y, 1 - c)
        b = pl.program_id(0)

        def place(h, r0, rows):
            if axis == 0:
                return full.at[pl.ds(pl.multiple_of(h * Rp + r0, 8), rows), :]
            return full.at[pl.ds(pl.multiple_of(r0, 8), rows), pl.ds(pl.multiple_of(h * Cp, LANE), Cp)]

        mine = place(c, b * tr, tr)
        loc = pltpu.make_async_copy(f_ref, mine, lsem)
        rem = _rcopy(f_ref, mine, ssem, rsem, sib)
        loc.start()
        rem.start()
        loc.wait()
        rem.wait_send()

        @pl.when(b == nb - 1)
        def _():
            theirs = place(1 - c, 0, Rp)
            _rcopy(theirs, theirs, ssem, rsem, sib).wait_recv()

    return pl.pallas_call(
        body, name=name, grid=(nb,), in_specs=[pl.BlockSpec((tr, Cp), lambda b: (b, 0))], out_specs=ANY,
        out_shape=jax.ShapeDtypeStruct((R, C), F32),
        scratch_shapes=[pltpu.SemaphoreType.DMA, pltpu.SemaphoreType.DMA, pltpu.SemaphoreType.DMA],
        compiler_params=_params("arbitrary"))(f)


def _allreduce_small(v):
    R, K = v.shape
    ndev = 8

    def body(v_ref, o_ref, land, ssem, rsem):
        x, y, c = lax.axis_index("x"), lax.axis_index("y"), lax.axis_index("c")
        me = 4 * x + 2 * y + c
        land[me] = v_ref[...]
        cps = []
        for r in range(1, ndev):
            fx, fy, fc = (r >> 2) & 1, (r >> 1) & 1, r & 1
            peer = (x ^ fx, y ^ fy, c ^ fc)
            cp = _rcopy(v_ref, land.at[me], ssem.at[r - 1], rsem.at[r - 1], peer)
            cp.start()
            cps.append((cp, 4 * peer[0] + 2 * peer[1] + peer[2], r))
        for cp, src, r in cps:
            cp.wait_send()
            _rcopy(v_ref, land.at[src], ssem.at[r - 1], rsem.at[r - 1], (x, y, c)).wait_recv()
        acc = land[0]
        for d in range(1, ndev):
            acc = acc + land[d]
        o_ref[...] = acc

    vm = pl.BlockSpec(memory_space=pltpu.VMEM)
    return pl.pallas_call(
        body, name="allreduce_small", in_specs=[vm], out_specs=vm, out_shape=jax.ShapeDtypeStruct((R, K), F32),
        scratch_shapes=[pltpu.VMEM((ndev, R, K), F32), pltpu.SemaphoreType.DMA((ndev - 1,)), pltpu.SemaphoreType.DMA((ndev - 1,))],
    )(v)


IN_SPLITS = (Q_RANK, KV_RANK, QK_ROPE, DIL_HEADS * HEAD, DIL_HEADS * HEAD, DIL_HEADS * HEAD, D_MODEL, D_MODEL)
IN_OFF = tuple(int(v) for v in np.cumsum((0,) + IN_SPLITS))


def _unshard_cols(g):
    G, K, Ns = g.shape
    return g.transpose(1, 0, 2).reshape(K, G * Ns)


def _shard_cols(w):
    K, N = w.shape
    return w.reshape(K, N_CHIPS, N // N_CHIPS).transpose(1, 0, 2)


def _rope_pad(w):
    half = QK_ROPE // 2
    z = jnp.zeros(w.shape[:-1] + (half,), w.dtype)
    return jnp.concatenate([w[..., :half], z, w[..., half:], z], axis=-1)


def _rope_unpad(w):
    half = QK_ROPE // 2
    return jnp.concatenate([w[..., :half], w[..., 2 * half:3 * half]], axis=-1)


def _split_w_in(w_in_g):
    w = _unshard_cols(w_in_g)
    K = w.shape[0]
    p = [w[:, IN_OFF[i]:IN_OFF[i + 1]] for i in range(8)]
    w_lat = jnp.concatenate([p[0], p[1], _rope_pad(p[2]), jnp.zeros((K, LAT_W - _KPE.stop), w.dtype)], axis=1)
    w_dil = [jnp.concatenate([p[3 + t][:, g * DIL_O:(g + 1) * DIL_O] for t in range(3)], axis=1) for g in range(DIL_GROUPS)]
    w_gate = jnp.concatenate([p[6], p[7]], axis=1)
    return w_lat, w_dil, w_gate


def _merge_dw_in(dw_lat, dw_dil, dw_gate):
    parts = [dw_lat[:, _CQ], dw_lat[:, _CKV], _rope_unpad(dw_lat[:, _KPE])]
    for t in range(3):
        parts += [dw_dil[g][:, t * DIL_O:(t + 1) * DIL_O] for g in range(DIL_GROUPS)]
    parts.append(dw_gate)
    return _shard_cols(jnp.concatenate(parts, axis=1))


def _split_w_uq(w_uq_g):
    w = _unshard_cols(w_uq_g)
    K = w.shape[0]
    w = w.reshape(K, MLA_HEADS, QK_NOPE + QK_ROPE)
    return w[:, :, :QK_NOPE].reshape(K, MLA_HEADS * HEAD), _rope_pad(w[:, :, QK_NOPE:]).reshape(K, MLA_HEADS * HEAD)


def _merge_dw_uq(dw_n, dw_p):
    K = dw_n.shape[0]
    w = jnp.concatenate([dw_n.reshape(K, MLA_HEADS, HEAD), _rope_unpad(dw_p.reshape(K, MLA_HEADS, HEAD))], axis=-1)
    return _shard_cols(w.reshape(K, MLA_HEADS * (QK_NOPE + QK_ROPE)))


def _split_w_ukv(w_ukv_g):
    w = _unshard_cols(w_ukv_g)
    K = w.shape[0]
    w = w.reshape(K, MLA_HEADS, 2 * HEAD)
    return w[:, :, :HEAD].reshape(K, MLA_HEADS * HEAD), w[:, :, HEAD:].reshape(K, MLA_HEADS * HEAD)


def _merge_dw_ukv(dw_k, dw_v):
    K = dw_k.shape[0]
    w = jnp.concatenate([dw_k.reshape(K, MLA_HEADS, HEAD), dw_v.reshape(K, MLA_HEADS, HEAD)], axis=-1)
    return _shard_cols(w.reshape(K, MLA_HEADS * 2 * HEAD))


GATHER_GROUPS = (("w_in",), ("w_uq", "w_ukv", "w_o_mla", "w_o_dil", "w_out"), ("w_up", "w_down", "conv_w"))
REDUCE_GROUPS = (("w_down", "w_up"), ("w_out", "w_o_mla", "w_o_dil"), ("w_uq", "w_ukv", "w_in"))


def _local_step(x, tgt, W, fetch, emit):
    S, D = x.shape
    cos, sin_s = _rope_tables(S)
    w_lat, w_dil, w_gate = _split_w_in(fetch(0, x)["w_in"])

    h = _rmsnorm_fwd("attn_norm", x, W["attn_norm_g"])
    lat = _mm_nn("proj_lat", h, w_lat)
    qkv = [_mm_nn(f"proj_dil{g}", h, w_dil[g], o_dtype=BF) for g in range(DIL_GROUPS)]
    gpre = _mm_nn("proj_gate", h, w_gate)
    WB = fetch(1, gpre)
    w_uqn, w_uqp = _split_w_uq(WB["w_uq"])
    w_k, w_v = _split_w_ukv(WB["w_ukv"])
    w_o_mla, w_o_dil = WB["w_o_mla"], WB["w_o_dil"]
    w_out = WB["w_out"].reshape(D, D)
    qn_, kvn, kpe = _mla_prep(lat, W["q_norm_g"], W["kv_norm_g"], cos, sin_s)
    q_nope = _mm_nn("q_nope", qn_, w_uqn, o_dtype=BF)
    q_pe = _rope("q_rope", _mm_nn("q_pe", qn_, w_uqp), cos, sin_s, False)
    k_nope = _mm_nn("k_nope", kvn, w_k, o_dtype=BF)
    v_mla = _mm_nn("v_mla", kvn, w_v, o_dtype=BF)
    attn_a, lse_a = _mla_fwd(q_nope, q_pe, k_nope, kpe, v_mla)
    dil = [_dil_fwd(qkv[g], g) for g in range(DIL_GROUPS)]
    attn_b, lse_b = _dil_combine([o for o, _ in dil], [l for _, l in dil])
    o_a = _mm_nn("o_mla", attn_a, w_o_mla)
    o_b = _mm_nn("o_dil", attn_b, w_o_dil)
    merge = _merge_fwd(gpre, W["b_gate"], o_a, o_b)
    x1 = _mm_nn("out_proj", merge, w_out, add=x)
    WC = fetch(2, merge)
    w_up = WC["w_up"]
    G4, _, C = w_up.shape
    w_down = WC["w_down"].reshape(G4 // 2, C, D)
    conv_w = WC["conv_w"]
    conv_b = W["conv_b"].reshape(G4, 1, C)
    h2 = _rmsnorm_fwd("ffn_norm", x1, W["ffn_norm_g"])
    u_pre = _up_fwd(h2, w_up)
    act = _ffn_act(u_pre, conv_w, conv_b)
    x2 = _down_fwd(act, w_down, x1)
    dx2, d_final_g, loss8 = _final_loss(x2, tgt, W["final_norm_g"])

    d_act = _down_dgrad(dx2, w_down)
    dw_down = _down_wgrad(act, dx2)
    du = _ffn_act_bwd(u_pre, conv_w, conv_b, d_act)
    du_pre, d_conv_w, d_conv_b = _conv_bwd(du, u_pre, conv_w)
    dh2 = _up_dgrad(du_pre, w_up)
    dw_up = _up_wgrad(h2, du_pre)
    zero = emit(0, {"w_down": dw_down.reshape(N_CHIPS, (G4 // 2) * C // N_CHIPS, D), "w_up": dw_up})
    dx1, d_ffn_g = _rmsnorm_bwd("ffn_norm_bwd", dh2, x1, W["ffn_norm_g"] + zero, dx2)
    d_merge = _mm_nt("out_proj_dgrad", dx1, w_out)
    dw_out = _mm_tn("out_proj_wgrad", merge, dx1)
    d_oa, d_ob, d_gpre, d_b_gate = _merge_bwd(d_merge, gpre, W["b_gate"], o_a, o_b)
    d_attn_a = _mm_nt("o_mla_dgrad", d_oa, w_o_mla, o_dtype=BF)
    dw_o_mla = _mm_tn("o_mla_wgrad", attn_a, d_oa, shards=N_CHIPS)
    d_attn_b = _mm_nt("o_dil_dgrad", d_ob, w_o_dil, o_dtype=BF)
    dw_o_dil = _mm_tn("o_dil_wgrad", attn_b, d_ob, shards=N_CHIPS)
    zero = emit(1, {"w_out": dw_out.reshape(N_CHIPS, D // N_CHIPS, D), "w_o_mla": dw_o_mla, "w_o_dil": dw_o_dil})
    q_norm_g = W["q_norm_g"] + zero
    delta_b = _dil_delta(d_attn_b, attn_b)
    d_qkv = [_dil_bwd(qkv[g], d_attn_b, lse_b, delta_b, g) for g in range(DIL_GROUPS)]
    dq_nope, dq_pe_rot = _mla_bwd_dq(q_nope, q_pe, k_nope, kpe, v_mla, d_attn_a, attn_a, lse_a)
    dk_nope, dv_mla, dkpe_rot = _mla_bwd_dkv(q_nope, q_pe, k_nope, kpe, v_mla, d_attn_a, attn_a, lse_a)
    dq_pe = _rope("q_rope_bwd", dq_pe_rot, cos, sin_s, True)
    d_qn = _mm_nt("q_pe_dgrad", dq_pe, w_uqp, add=_mm_nt("q_nope_dgrad", dq_nope, w_uqn))
    d_kvn = _mm_nt("v_dgrad", dv_mla, w_v, add=_mm_nt("k_nope_dgrad", dk_nope, w_k))
    dw_uq = _merge_dw_uq(_mm_tn("q_nope_wgrad", qn_, dq_nope), _mm_tn("q_pe_wgrad", qn_, dq_pe))
    dw_ukv = _merge_dw_ukv(_mm_tn("k_nope_wgrad", kvn, dk_nope), _mm_tn("v_wgrad", kvn, dv_mla))
    d_lat, d_q_g, d_kv_g = _mla_prep_bwd(lat, q_norm_g, W["kv_norm_g"], cos, sin_s, d_qn, d_kvn, dkpe_rot)
    dh = _mm_nt("proj_lat_dgrad", d_lat, w_lat)
    for g in range(DIL_GROUPS):
        dh = _mm_nt(f"proj_dil{g}_dgrad", d_qkv[g], w_dil[g], add=dh)
    dh = _mm_nt("proj_gate_dgrad", d_gpre, w_gate, add=dh)
    dw_in = _merge_dw_in(_mm_tn("proj_lat_wgrad", h, d_lat),
                         [_mm_tn(f"proj_dil{g}_wgrad", h, d_qkv[g]) for g in range(DIL_GROUPS)],
                         _mm_tn("proj_gate_wgrad", h, d_gpre))
    emit(2, {"w_uq": dw_uq, "w_ukv": dw_ukv, "w_in": dw_in})
    grad_x, d_attn_g = _rmsnorm_bwd("attn_norm_bwd", dh, x, W["attn_norm_g"], dx1)

    small = {"attn_norm_g": d_attn_g, "b_gate": d_b_gate, "q_norm_g": d_q_g, "kv_norm_g": d_kv_g,
             "ffn_norm_g": d_ffn_g, "conv_w": d_conv_w, "conv_b": d_conv_b.reshape(1, G4 * C),
             "final_norm_g": d_final_g}
    return loss8[0, 0], grad_x, small


BIG = ("w_in", "w_uq", "w_ukv", "w_o_mla", "w_o_dil", "w_out", "w_up", "w_down")
SMALL = ("attn_norm_g", "b_gate", "q_norm_g", "kv_norm_g", "ffn_norm_g", "conv_w", "conv_b", "final_norm_g")
WEIGHTS = ("attn_norm_g", "w_in", "b_gate", "q_norm_g", "w_uq", "kv_norm_g", "w_ukv", "w_o_mla", "w_o_dil",
           "w_out", "ffn_norm_g", "w_up", "conv_w", "conv_b", "w_down", "final_norm_g")
SMALL_ROWS = 8
HALF_AXIS = {"w_down": 1}


def _gather_start(shards):
    names = [n for grp in GATHER_GROUPS for n in grp]
    chip = 2 * lax.axis_index("x") + lax.axis_index("y")
    srcs = [shards[n] for n in names]
    lands = [lax.dynamic_update_slice(lax.empty((N_CHIPS,) + s.shape, s.dtype), s[None], (chip, 0, 0)) for s in srcs]
    groups, at = [], 0
    for grp in GATHER_GROUPS:
        groups.append(list(range(at, at + len(grp))))
        at += len(grp)
    sems, srcs, lands, token = _exchange_start("gather_start", "gather", srcs, lands, groups)

    def fetch(i, after):
        idx = groups[i]
        _, got = _exchange_wait(f"gather_wait{i}", "gather", [srcs[j] for j in idx], [lands[j] for j in idx], sems[i], after)
        return dict(zip(GATHER_GROUPS[i], got))

    return fetch, token[0, 0]


def _reduce_start(i, grads):
    names = REDUCE_GROUPS[i]
    hs = [_pair_sum(f"pair_sum_{n}", grads[n], HALF_AXIS.get(n, 0)) for n in names]
    lands = [lax.empty((3,) + h.shape[1:], h.dtype) for h in hs]
    sems, hs, lands, token = _exchange_start(f"reduce_start{i}", "scatter", hs, lands, [list(range(len(names)))])
    return (sems[0], hs, lands), token[0, 0]


def _reduce_finish(i, pending, after):
    sems, hs, lands = pending
    hs, lands = _exchange_wait(f"reduce_wait{i}", "scatter", hs, lands, sems, after)
    out = {}
    for n, h, landed in zip(REDUCE_GROUPS[i], hs, lands):
        out[n] = _pair_join(f"pair_join_{n}", _chip_total(f"chip_total_{n}", h, landed), HALF_AXIS.get(n, 0))
    return out


def _reduce_small(small):
    flat = [small[n].reshape(-1) for n in SMALL]
    sizes = [f.shape[0] for f in flat]
    total = sum(sizes)
    width = -(-total // (SMALL_ROWS * LANE)) * LANE
    packed = jnp.concatenate(flat + [jnp.zeros((SMALL_ROWS * width - total,), F32)]).reshape(SMALL_ROWS, width)
    red = _allreduce_small(packed).reshape(-1)
    out, off = {}, 0
    for n, s in zip(SMALL, sizes):
        out[n] = red[off:off + s]
        off += s
    return out


def kernel(x, attn_norm_g, w_in, b_gate, q_norm_g, w_uq, kv_norm_g, w_ukv, w_o_mla, w_o_dil, w_out, ffn_norm_g, w_up, conv_w, conv_b, w_down, final_norm_g, loss_target, m_attn_norm_g, m_w_in, m_b_gate, m_q_norm_g, m_w_uq, m_kv_norm_g, m_w_ukv, m_w_o_mla, m_w_o_dil, m_w_out, m_ffn_norm_g, m_w_up, m_conv_w, m_conv_b, m_w_down, m_final_norm_g, v_attn_norm_g, v_w_in, v_b_gate, v_q_norm_g, v_w_uq, v_kv_norm_g, v_w_ukv, v_w_o_mla, v_w_o_dil, v_w_out, v_ffn_norm_g, v_w_up, v_conv_w, v_conv_b, v_w_down, v_final_norm_g):
    given = dict(attn_norm_g=attn_norm_g, w_in=w_in, b_gate=b_gate, q_norm_g=q_norm_g, w_uq=w_uq, kv_norm_g=kv_norm_g,
                 w_ukv=w_ukv, w_o_mla=w_o_mla, w_o_dil=w_o_dil, w_out=w_out, ffn_norm_g=ffn_norm_g, w_up=w_up,
                 conv_w=conv_w, conv_b=conv_b, w_down=w_down, final_norm_g=final_norm_g)
    moments_m = dict(attn_norm_g=m_attn_norm_g, w_in=m_w_in, b_gate=m_b_gate, q_norm_g=m_q_norm_g, w_uq=m_w_uq,
                     kv_norm_g=m_kv_norm_g, w_ukv=m_w_ukv, w_o_mla=m_w_o_mla, w_o_dil=m_w_o_dil, w_out=m_w_out,
                     ffn_norm_g=m_ffn_norm_g, w_up=m_w_up, conv_w=m_conv_w, conv_b=m_conv_b, w_down=m_w_down,
                     final_norm_g=m_final_norm_g)
    moments_v = dict(attn_norm_g=v_attn_norm_g, w_in=v_w_in, b_gate=v_b_gate, q_norm_g=v_q_norm_g, w_uq=v_w_uq,
                     kv_norm_g=v_kv_norm_g, w_ukv=v_w_ukv, w_o_mla=v_w_o_mla, w_o_dil=v_w_o_dil, w_out=v_w_out,
                     ffn_norm_g=v_ffn_norm_g, w_up=v_w_up, conv_w=v_conv_w, conv_b=v_conv_b, w_down=v_w_down,
                     final_norm_g=v_final_norm_g)

    shards = {n: given[n][0].astype(BF) for n in BIG}
    shards["conv_w"] = given["conv_w"][0]
    fetch, zero = _gather_start(shards)
    W = {n: given[n] for n in ("b_gate", "q_norm_g", "kv_norm_g", "ffn_norm_g", "conv_b")}
    W["attn_norm_g"] = given["attn_norm_g"] + zero
    W["final_norm_g"] = given["final_norm_g"].reshape(1, -1)

    pending = {}

    def emit(i, grads):
        pending[i], token = _reduce_start(i, grads)
        return token

    loss_part, grad_x, small = _local_step(x[0], loss_target[0], W, fetch, emit)
    loss = lax.psum(loss_part, ("x", "y", "c"))
    grads, delta, new_m, new_v = {}, {}, {}, {}

    def adamw(n, g):
        shp = given[n].shape
        two_d = (-1, shp[-1]) if len(shp) > 1 else (1, -1)
        d, nm, nv = _adamw(f"adamw_{n}", given[n].reshape(two_d), g.reshape(two_d),
                           moments_m[n].reshape(two_d), moments_v[n].reshape(two_d))
        grads[n], delta[n], new_m[n], new_v[n] = g.reshape(shp), d.reshape(shp), nm.reshape(shp), nv.reshape(shp)

    after = grad_x
    for i in range(len(REDUCE_GROUPS)):
        for n, g in _reduce_finish(i, pending[i], after).items():
            adamw(n, g)
        after = delta[REDUCE_GROUPS[i][-1]]
    g_small = _reduce_small(small)
    chip = 2 * lax.axis_index("x") + lax.axis_index("y")
    for n in SMALL:
        if n == "conv_w":
            full = g_small[n].reshape(N_CHIPS, 3, -1)
            adamw(n, lax.dynamic_index_in_dim(full, chip, 0, keepdims=True))
        else:
            adamw(n, g_small[n])

    return (loss, grad_x[None], *[grads[n] for n in WEIGHTS], *[delta[n] for n in WEIGHTS],
            *[new_m[n] for n in WEIGHTS], *[new_v[n] for n in WEIGHTS])
```

```python
import functools
import math

import numpy as np
import jax
import jax.numpy as jnp
from jax import lax
from jax.experimental import pallas as pl
from jax.experimental.pallas import tpu as pltpu

F32 = jnp.float32
BF = jnp.bfloat16
MESH = pl.DeviceIdType.MESH

D_MODEL = 2048
MLA_HEADS = 8
QK_NOPE = 128
QK_ROPE = 64
Q_RANK = 512
KV_RANK = 256
ROPE_THETA = 10000.0
DIL_PATTERNS = ((128, 1), (512, 4), (2048, 16))
DIL_GROUPS = 3
DIL_HPG = 4
DIL_HEADS = 12
HEAD = 128
DIL_BLOCK = 128
ALIBI_MAX_BIAS = 8.0
NORM_EPS = 1e-6
N_CHIPS = 4
ADAM_LR = 0.001
ADAM_B1 = 0.9
ADAM_B2 = 0.999
ADAM_EPS = 1e-08
ADAM_WD = 0.01
ADAM_STEP = 10

LANE = 128
VMEM_LIMIT = 56 * 1024 * 1024
MLA_SCALE = (QK_NOPE + QK_ROPE) ** -0.5
DIL_SCALE = HEAD ** -0.5


def _params(*sem):
    return pltpu.CompilerParams(dimension_semantics=sem, vmem_limit_bytes=VMEM_LIMIT)


def _tile(n, pref):
    t = (pref // LANE) * LANE
    while t >= LANE:
        if n % t == 0:
            return t
        t -= LANE
    return n


NN = (((1,), (0,)), ((), ()))
NT = (((1,), (1,)), ((), ()))
TN = (((0,), (0,)), ((), ()))


def _mm_call(name, a, b, add, *, grid, a_spec, b_spec, add_spec, o_spec, o_shape, o_dtype, acc_shape, dims, nk):
    nax = len(grid)

    def body(*refs):
        if add is None:
            a_ref, b_ref, o_ref = refs[:3]
            c_ref = None
            scr = refs[3:]
        else:
            a_ref, b_ref, c_ref, o_ref = refs[:4]
            scr = refs[4:]
        prod = lax.dot_general(a_ref[...].astype(BF), b_ref[...].astype(BF), dims, preferred_element_type=F32)
        if nk == 1:
            if c_ref is not None:
                prod = prod + c_ref[...]
            o_ref[...] = prod.astype(o_ref.dtype)
        else:
            acc = scr[0]
            k = pl.program_id(nax - 1)

            @pl.when(k == 0)
            def _():
                if c_ref is not None:
                    acc[...] = prod + c_ref[...]
                else:
                    acc[...] = prod

            @pl.when(k > 0)
            def _():
                acc[...] += prod

            @pl.when(k == nk - 1)
            def _():
                o_ref[...] = acc[...].astype(o_ref.dtype)

    ins = [a, b] + ([] if add is None else [add])
    specs = [a_spec, b_spec] + ([] if add is None else [add_spec])
    sem = ("parallel",) * (nax - 1) + ("arbitrary",)
    return pl.pallas_call(
        body, name=name, grid=grid, in_specs=specs, out_specs=o_spec,
        out_shape=jax.ShapeDtypeStruct(o_shape, o_dtype),
        scratch_shapes=[] if nk == 1 else [pltpu.VMEM(acc_shape, F32)],
        compiler_params=_params(*sem),
    )(*ins)


def _mm_nn(name, a, b, *, add=None, o_dtype=F32):
    M, K = a.shape
    sharded = b.ndim == 3
    Ns = b.shape[-1]
    N = Ns * (b.shape[0] if sharded else 1)
    tm, tn, tk = _tile(M, 512), _tile(Ns, 512), _tile(K, 2048)
    per = Ns // tn
    nk = K // tk
    if sharded:
        b_spec = pl.BlockSpec((None, tk, tn), lambda i, j, k: (j // per, k, j % per))
    else:
        b_spec = pl.BlockSpec((tk, tn), lambda i, j, k: (k, j))
    return _mm_call(
        name, a, b, add, grid=(M // tm, N // tn, nk),
        a_spec=pl.BlockSpec((tm, tk), lambda i, j, k: (i, k)), b_spec=b_spec,
        add_spec=pl.BlockSpec((tm, tn), lambda i, j, k: (i, j)),
        o_spec=pl.BlockSpec((tm, tn), lambda i, j, k: (i, j)),
        o_shape=(M, N), o_dtype=o_dtype, acc_shape=(tm, tn), dims=NN, nk=nk)


def _mm_nt(name, a, b, *, add=None, o_dtype=F32):
    M, K = a.shape
    sharded = b.ndim == 3
    N, Ks = b.shape[-2], b.shape[-1]
    tm, tn, tk = _tile(M, 512), _tile(N, 512), _tile(Ks, 2048)
    per = Ks // tk
    nk = K // tk
    if sharded:
        b_spec = pl.BlockSpec((None, tn, tk), lambda i, j, k: (k // per, j, k % per))
    else:
        b_spec = pl.BlockSpec((tn, tk), lambda i, j, k: (j, k))
    return _mm_call(
        name, a, b, add, grid=(M // tm, N // tn, nk),
        a_spec=pl.BlockSpec((tm, tk), lambda i, j, k: (i, k)), b_spec=b_spec,
        add_spec=pl.BlockSpec((tm, tn), lambda i, j, k: (i, j)),
        o_spec=pl.BlockSpec((tm, tn), lambda i, j, k: (i, j)),
        o_shape=(M, N), o_dtype=o_dtype, acc_shape=(tm, tn), dims=NT, nk=nk)


def _mm_tn(name, a, b, *, shards=1, o_dtype=BF):
    S, M = a.shape
    N = b.shape[1]
    Ns = N // shards
    tm, tn, tk = _tile(M, 1024), _tile(Ns, 1024), _tile(S, 512)
    per = Ns // tn
    nk = S // tk
    if shards > 1:
        o_spec = pl.BlockSpec((None, tm, tn), lambda i, j, k: (j // per, i, j % per))
        o_shape = (shards, M, Ns)
    else:
        o_spec = pl.BlockSpec((tm, tn), lambda i, j, k: (i, j))
        o_shape = (M, N)
    return _mm_call(
        name, a, b, None, grid=(M // tm, N // tn, nk),
        a_spec=pl.BlockSpec((tk, tm), lambda i, j, k: (k, i)),
        b_spec=pl.BlockSpec((tk, tn), lambda i, j, k: (k, j)),
        add_spec=None, o_spec=o_spec, o_shape=o_shape, o_dtype=o_dtype, acc_shape=(tm, tn), dims=TN, nk=nk)


def _up_fwd(h2, w_up):
    S, D = h2.shape
    G, _, C = w_up.shape
    tm = _tile(S, 256)
    return _mm_call(
        "up_fwd", h2, w_up, None, grid=(G, S // tm, 1),
        a_spec=pl.BlockSpec((tm, D), lambda g, i, k: (i, 0)),
        b_spec=pl.BlockSpec((None, D, C), lambda g, i, k: (g, 0, 0)),
        add_spec=None, o_spec=pl.BlockSpec((None, tm, C), lambda g, i, k: (g, i, 0)),
        o_shape=(G, S, C), o_dtype=F32, acc_shape=None, dims=NN, nk=1)


def _up_dgrad(du_pre, w_up):
    G, S, C = du_pre.shape
    D = w_up.shape[1]
    tm, tn = _tile(S, 1024), _tile(D, 1024)
    return _mm_call(
        "up_dgrad", du_pre, w_up, None, grid=(S // tm, D // tn, G),
        a_spec=pl.BlockSpec((None, tm, C), lambda i, j, g: (g, i, 0)),
        b_spec=pl.BlockSpec((None, tn, C), lambda i, j, g: (g, j, 0)),
        add_spec=None, o_spec=pl.BlockSpec((tm, tn), lambda i, j, g: (i, j)),
        o_shape=(S, D), o_dtype=F32, acc_shape=(tm, tn), dims=NT, nk=G)


def _up_wgrad(h2, du_pre):
    G, S, C = du_pre.shape
    D = h2.shape[1]
    tm, tk = _tile(D, 512), _tile(S, 512)
    return _mm_call(
        "up_wgrad", h2, du_pre, None, grid=(G, D // tm, S // tk),
        a_spec=pl.BlockSpec((tk, tm), lambda g, i, k: (k, i)),
        b_spec=pl.BlockSpec((None, tk, C), lambda g, i, k: (g, k, 0)),
        add_spec=None, o_spec=pl.BlockSpec((None, tm, C), lambda g, i, k: (g, i, 0)),
        o_shape=(G, D, C), o_dtype=BF, acc_shape=(tm, C), dims=TN, nk=S // tk)


def _down_fwd(act, w_down, x1):
    G, S, C = act.shape
    D = w_down.shape[2]
    tm, tn = _tile(S, 1024), _tile(D, 1024)
    return _mm_call(
        "down_fwd", act, w_down, x1, grid=(S // tm, D // tn, G),
        a_spec=pl.BlockSpec((None, tm, C), lambda i, j, g: (g, i, 0)),
        b_spec=pl.BlockSpec((None, C, tn), lambda i, j, g: (g, 0, j)),
        add_spec=pl.BlockSpec((tm, tn), lambda i, j, g: (i, j)),
        o_spec=pl.BlockSpec((tm, tn), lambda i, j, g: (i, j)),
        o_shape=(S, D), o_dtype=F32, acc_shape=(tm, tn), dims=NN, nk=G)


def _down_dgrad(dx2, w_down):
    S, D = dx2.shape
    G, C, _ = w_down.shape
    tm = _tile(S, 256)
    return _mm_call(
        "down_dgrad", dx2, w_down, None, grid=(G, S // tm, 1),
        a_spec=pl.BlockSpec((tm, D), lambda g, i, k: (i, 0)),
        b_spec=pl.BlockSpec((None, C, D), lambda g, i, k: (g, 0, 0)),
        add_spec=None, o_spec=pl.BlockSpec((None, tm, C), lambda g, i, k: (g, i, 0)),
        o_shape=(G, S, C), o_dtype=F32, acc_shape=None, dims=NT, nk=1)


def _down_wgrad(act, dx2):
    G, S, C = act.shape
    D = dx2.shape[1]
    tn, tk = _tile(D, 512), _tile(S, 512)
    return _mm_call(
        "down_wgrad", act, dx2, None, grid=(G, D // tn, S // tk),
        a_spec=pl.BlockSpec((None, tk, C), lambda g, j, k: (g, k, 0)),
        b_spec=pl.BlockSpec((tk, tn), lambda g, j, k: (k, j)),
        add_spec=None, o_spec=pl.BlockSpec((None, C, tn), lambda g, j, k: (g, 0, j)),
        o_shape=(G, C, D), o_dtype=BF, acc_shape=(C, tn), dims=TN, nk=S // tk)


def _row(ts, c):
    return pl.BlockSpec((ts, c), lambda i: (i, 0))


def _bcast(r, c):
    return pl.BlockSpec((r, c), lambda i: (0, 0))


def _accumulate(i, ref, val):
    @pl.when(i == 0)
    def _():
        ref[...] = val

    @pl.when(i > 0)
    def _():
        ref[...] += val


def _rstd(xv):
    return lax.rsqrt(jnp.mean(xv * xv, axis=-1, keepdims=True) + NORM_EPS)


def _rmsnorm_fwd(name, x, g):
    S, D = x.shape
    ts = _tile(S, 512)

    def body(x_ref, g_ref, o_ref):
        xv = x_ref[...]
        o_ref[...] = (xv * _rstd(xv) * g_ref[...]).astype(o_ref.dtype)

    return pl.pallas_call(
        body, name=name, grid=(S // ts,), in_specs=[_row(ts, D), _bcast(1, D)], out_specs=_row(ts, D),
        out_shape=jax.ShapeDtypeStruct((S, D), BF), compiler_params=_params("parallel"))(x, g)


def _norm_bwd_rows(dy, xv, g):
    r = _rstd(xv)
    xh = xv * r
    dxh = dy * g
    dx = r * (dxh - xh * jnp.mean(dxh * xh, axis=-1, keepdims=True))
    return dx, jnp.sum(dy * xh, axis=0, keepdims=True)


def _rmsnorm_bwd(name, dy, x, g, res):
    S, D = x.shape
    ts = _tile(S, 512)

    def body(dy_ref, x_ref, g_ref, res_ref, dx_ref, dg_ref):
        dx, dg = _norm_bwd_rows(dy_ref[...], x_ref[...], g_ref[...])
        dx_ref[...] = dx + res_ref[...]
        _accumulate(pl.program_id(0), dg_ref, dg)

    return pl.pallas_call(
        body, name=name, grid=(S // ts,),
        in_specs=[_row(ts, D), _row(ts, D), _bcast(1, D), _row(ts, D)],
        out_specs=[_row(ts, D), _bcast(1, D)],
        out_shape=[jax.ShapeDtypeStruct((S, D), F32), jax.ShapeDtypeStruct((1, D), F32)],
        compiler_params=_params("arbitrary"))(dy, x, g, res)


def _rope_tables(S):
    half = QK_ROPE // 2
    pos = jnp.arange(S, dtype=F32)
    inv_freq = ROPE_THETA ** (-jnp.arange(0, QK_ROPE, 2, dtype=F32) / QK_ROPE)
    ang = pos[:, None] * inv_freq[None, :]
    cos, sin = jnp.cos(ang), jnp.sin(ang)
    z = jnp.zeros((S, half), F32)
    return jnp.concatenate([cos, z, cos, z], axis=1), jnp.concatenate([-sin, z, sin, z], axis=1)


def _rope_lanes(x, cos, sin_signed, inverse):
    if inverse:
        return x * cos + pltpu.roll(x * sin_signed, LANE // 2, 1)
    return x * cos + pltpu.roll(x, LANE // 2, 1) * sin_signed


def _rope(name, x, cos, sin_signed, inverse):
    S, W = x.shape
    ts = _tile(S, 512)

    def body(x_ref, c_ref, s_ref, o_ref):
        c, s = c_ref[...], s_ref[...]
        for h in range(W // LANE):
            sl = slice(h * LANE, (h + 1) * LANE)
            o_ref[:, sl] = _rope_lanes(x_ref[:, sl], c, s, inverse).astype(o_ref.dtype)

    return pl.pallas_call(
        body, name=name, grid=(S // ts,), in_specs=[_row(ts, W), _row(ts, LANE), _row(ts, LANE)],
        out_specs=_row(ts, W), out_shape=jax.ShapeDtypeStruct((S, W), BF),
        compiler_params=_params("parallel"))(x, cos, sin_signed)


LAT_W = 1024
_CQ = slice(0, Q_RANK)
_CKV = slice(Q_RANK, Q_RANK + KV_RANK)
_KPE = slice(Q_RANK + KV_RANK, Q_RANK + KV_RANK + LANE)


def _mla_prep(lat, qg, kvg, cos, sin_signed):
    S = lat.shape[0]
    ts = _tile(S, 512)

    def body(lat_ref, qg_ref, kvg_ref, c_ref, s_ref, qn_ref, kvn_ref, kpe_ref):
        cq = lat_ref[:, _CQ]
        qn_ref[...] = (cq * _rstd(cq) * qg_ref[...]).astype(BF)
        ckv = lat_ref[:, _CKV]
        kvn_ref[...] = (ckv * _rstd(ckv) * kvg_ref[...]).astype(BF)
        kpe_ref[...] = _rope_lanes(lat_ref[:, _KPE], c_ref[...], s_ref[...], False).astype(BF)

    return pl.pallas_call(
        body, name="mla_prep", grid=(S // ts,),
        in_specs=[_row(ts, LAT_W), _bcast(1, Q_RANK), _bcast(1, KV_RANK), _row(ts, LANE), _row(ts, LANE)],
        out_specs=[_row(ts, Q_RANK), _row(ts, KV_RANK), _row(ts, LANE)],
        out_shape=[jax.ShapeDtypeStruct((S, Q_RANK), BF), jax.ShapeDtypeStruct((S, KV_RANK), BF),
                   jax.ShapeDtypeStruct((S, LANE), BF)],
        compiler_params=_params("parallel"))(lat, qg, kvg, cos, sin_signed)


def _mla_prep_bwd(lat, qg, kvg, cos, sin_signed, d_qn, d_kvn, d_kpe):
    S = lat.shape[0]
    ts = _tile(S, 512)

    def body(lat_ref, qg_ref, kvg_ref, c_ref, s_ref, dqn_ref, dkvn_ref, dkpe_ref, dlat_ref, dqg_ref, dkvg_ref):
        i = pl.program_id(0)
        dcq, dqg = _norm_bwd_rows(dqn_ref[...], lat_ref[:, _CQ], qg_ref[...])
        dckv, dkvg = _norm_bwd_rows(dkvn_ref[...], lat_ref[:, _CKV], kvg_ref[...])
        dlat_ref[:, _CQ] = dcq.astype(BF)
        dlat_ref[:, _CKV] = dckv.astype(BF)
        dkpe = dkpe_ref[0]
        for g in range(1, d_kpe.shape[0]):
            dkpe = dkpe + dkpe_ref[g]
        dlat_ref[:, _KPE] = _rope_lanes(dkpe, c_ref[...], s_ref[...], True).astype(BF)
        dlat_ref[:, _KPE.stop:] = jnp.zeros((ts, LAT_W - _KPE.stop), BF)
        _accumulate(i, dqg_ref, dqg)
        _accumulate(i, dkvg_ref, dkvg)

    return pl.pallas_call(
        body, name="mla_prep_bwd", grid=(S // ts,),
        in_specs=[_row(ts, LAT_W), _bcast(1, Q_RANK), _bcast(1, KV_RANK), _row(ts, LANE), _row(ts, LANE),
                  _row(ts, Q_RANK), _row(ts, KV_RANK), pl.BlockSpec((d_kpe.shape[0], ts, LANE), lambda i: (0, i, 0))],
        out_specs=[_row(ts, LAT_W), _bcast(1, Q_RANK), _bcast(1, KV_RANK)],
        out_shape=[jax.ShapeDtypeStruct((S, LAT_W), BF), jax.ShapeDtypeStruct((1, Q_RANK), F32),
                   jax.ShapeDtypeStruct((1, KV_RANK), F32)],
        compiler_params=_params("arbitrary"))(lat, qg, kvg, cos, sin_signed, d_qn, d_kvn, d_kpe)


def _sigmoid(z):
    return 1.0 / (1.0 + jnp.exp(-z))


def _merge_fwd(gpre, b_gate, o_a, o_b):
    S, D = o_a.shape
    ts = _tile(S, 256)

    def body(g_ref, b_ref, oa_ref, ob_ref, m_ref):
        ga = _sigmoid(g_ref[:, :D] + b_ref[:, :D])
        gb = _sigmoid(g_ref[:, D:] + b_ref[:, D:])
        m_ref[...] = (ga * oa_ref[...] + gb * ob_ref[...]).astype(BF)

    return pl.pallas_call(
        body, name="merge_fwd", grid=(S // ts,),
        in_specs=[_row(ts, 2 * D), _bcast(1, 2 * D), _row(ts, D), _row(ts, D)], out_specs=_row(ts, D),
        out_shape=jax.ShapeDtypeStruct((S, D), BF), compiler_params=_params("parallel"))(gpre, b_gate, o_a, o_b)


def _merge_bwd(d_merge, gpre, b_gate, o_a, o_b):
    S, D = o_a.shape
    ts = _tile(S, 256)

    def body(dm_ref, g_ref, b_ref, oa_ref, ob_ref, doa_ref, dob_ref, dg_ref, db_ref):
        dm = dm_ref[...]
        ga = _sigmoid(g_ref[:, :D] + b_ref[:, :D])
        gb = _sigmoid(g_ref[:, D:] + b_ref[:, D:])
        doa_ref[...] = (dm * ga).astype(BF)
        dob_ref[...] = (dm * gb).astype(BF)
        dga = dm * oa_ref[...] * ga * (1.0 - ga)
        dgb = dm * ob_ref[...] * gb * (1.0 - gb)
        dg_ref[:, :D] = dga.astype(BF)
        dg_ref[:, D:] = dgb.astype(BF)
        i = pl.program_id(0)
        part = jnp.concatenate([jnp.sum(dga, axis=0, keepdims=True), jnp.sum(dgb, axis=0, keepdims=True)], axis=1)
        _accumulate(i, db_ref, part)

    return pl.pallas_call(
        body, name="merge_bwd", grid=(S // ts,),
        in_specs=[_row(ts, D), _row(ts, 2 * D), _bcast(1, 2 * D), _row(ts, D), _row(ts, D)],
        out_specs=[_row(ts, D), _row(ts, D), _row(ts, 2 * D), _bcast(1, 2 * D)],
        out_shape=[jax.ShapeDtypeStruct((S, D), BF), jax.ShapeDtypeStruct((S, D), BF),
                   jax.ShapeDtypeStruct((S, 2 * D), BF), jax.ShapeDtypeStruct((1, 2 * D), F32)],
        compiler_params=_params("arbitrary"))(d_merge, gpre, b_gate, o_a, o_b)


def _final_loss(x2, tgt, gf):
    S, D = x2.shape
    ts = _tile(S, 512)

    def body(x_ref, t_ref, g_ref, dx_ref, dg_ref, loss_ref):
        i = pl.program_id(0)
        xv = x_ref[...]
        g = g_ref[...]
        y = xv * _rstd(xv) * g
        err = y - t_ref[...]
        dx, dg = _norm_bwd_rows(err * (1.0 / D), xv, g)
        dx_ref[...] = dx
        _accumulate(i, dg_ref, dg)
        part = 0.5 * jnp.sum(jnp.mean(err * err, axis=-1, keepdims=True), axis=0, keepdims=True)
        _accumulate(i, loss_ref, jnp.broadcast_to(part, (8, LANE)))

    return pl.pallas_call(
        body, name="final_loss", grid=(S // ts,),
        in_specs=[_row(ts, D), _row(ts, D), _bcast(1, D)],
        out_specs=[_row(ts, D), _bcast(1, D), _bcast(8, LANE)],
        out_shape=[jax.ShapeDtypeStruct((S, D), F32), jax.ShapeDtypeStruct((1, D), F32),
                   jax.ShapeDtypeStruct((8, LANE), F32)],
        compiler_params=_params("arbitrary"))(x2, tgt, gf)


HALO = 8


def _shift_down(cur, prev, k, rows):
    out = pltpu.roll(cur, k, 0)
    for j in range(k):
        out = jnp.where(rows == j, prev[HALO - k + j:HALO - k + j + 1, :], out)
    return out


def _shift_up(cur, nxt, k, rows, ts):
    out = pltpu.roll(cur, ts - k, 0)
    for j in range(k):
        out = jnp.where(rows == ts - k + j, nxt[j:j + 1, :], out)
    return out


def _conv_rows(cur, prev, w, b, rows):
    return b + w[0:1, :] * _shift_down(cur, prev, 2, rows) + w[1:2, :] * _shift_down(cur, prev, 1, rows) + w[2:3, :] * cur


def _conv_specs(ts, C, shard_of):
    nh = ts // HALO
    cur = pl.BlockSpec((None, ts, C), lambda g, i: (shard_of(g), i, 0))
    prev = pl.BlockSpec((None, HALO, C), lambda g, i: (shard_of(g), jnp.maximum(i * nh - 1, 0), 0))
    return cur, prev


def _ffn_act(u_pre, conv_w, conv_b):
    G4, S, C = u_pre.shape
    G = G4 // 2
    ts = _tile(S, 256)

    def body(up_ref, upp_ref, gt_ref, gtp_ref, wu_ref, wg_ref, bu_ref, bg_ref, act_ref):
        first = pl.program_id(1) == 0
        rows = lax.broadcasted_iota(jnp.int32, (ts, C), 0)
        pu = jnp.where(first, 0.0, upp_ref[...])
        pg = jnp.where(first, 0.0, gtp_ref[...])
        up = _conv_rows(up_ref[...], pu, wu_ref[...], bu_ref[...], rows)
        gate = _conv_rows(gt_ref[...], pg, wg_ref[...], bg_ref[...], rows)
        act_ref[...] = (gate * _sigmoid(gate) * up).astype(BF)

    cur_u, prev_u = _conv_specs(ts, C, lambda g: g)
    cur_g, prev_g = _conv_specs(ts, C, lambda g: g + G)
    w_u = pl.BlockSpec((None, 3, C), lambda g, i: (g, 0, 0))
    w_g = pl.BlockSpec((None, 3, C), lambda g, i: (g + G, 0, 0))
    b_u = pl.BlockSpec((None, 1, C), lambda g, i: (g, 0, 0))
    b_g = pl.BlockSpec((None, 1, C), lambda g, i: (g + G, 0, 0))
    return pl.pallas_call(
        body, name="ffn_act", grid=(G, S // ts),
        in_specs=[cur_u, prev_u, cur_g, prev_g, w_u, w_g, b_u, b_g],
        out_specs=pl.BlockSpec((None, ts, C), lambda g, i: (g, i, 0)),
        out_shape=jax.ShapeDtypeStruct((G, S, C), BF),
        compiler_params=_params("parallel", "parallel"))(u_pre, u_pre, u_pre, u_pre, conv_w, conv_w, conv_b, conv_b)


def _ffn_act_bwd(u_pre, conv_w, conv_b, d_act):
    G4, S, C = u_pre.shape
    G = G4 // 2
    ts = _tile(S, 256)

    def body(up_ref, upp_ref, gt_ref, gtp_ref, wu_ref, wg_ref, bu_ref, bg_ref, da_ref, du_ref):
        first = pl.program_id(1) == 0
        rows = lax.broadcasted_iota(jnp.int32, (ts, C), 0)
        pu = jnp.where(first, 0.0, upp_ref[...])
        pg = jnp.where(first, 0.0, gtp_ref[...])
        up = _conv_rows(up_ref[...], pu, wu_ref[...], bu_ref[...], rows)
        gate = _conv_rows(gt_ref[...], pg, wg_ref[...], bg_ref[...], rows)
        sg = _sigmoid(gate)
        da = da_ref[...]
        du_ref[0] = da * (gate * sg)
        du_ref[1] = da * up * (sg * (1.0 + gate * (1.0 - sg)))

    cur_u, prev_u = _conv_specs(ts, C, lambda g: g)
    cur_g, prev_g = _conv_specs(ts, C, lambda g: g + G)
    w_u = pl.BlockSpec((None, 3, C), lambda g, i: (g, 0, 0))
    w_g = pl.BlockSpec((None, 3, C), lambda g, i: (g + G, 0, 0))
    b_u = pl.BlockSpec((None, 1, C), lambda g, i: (g, 0, 0))
    b_g = pl.BlockSpec((None, 1, C), lambda g, i: (g + G, 0, 0))
    blk = pl.BlockSpec((None, ts, C), lambda g, i: (g, i, 0))
    du = pl.pallas_call(
        body, name="ffn_act_bwd", grid=(G, S // ts),
        in_specs=[cur_u, prev_u, cur_g, prev_g, w_u, w_g, b_u, b_g, blk],
        out_specs=pl.BlockSpec((2, None, ts, C), lambda g, i: (0, g, i, 0)),
        out_shape=jax.ShapeDtypeStruct((2, G, S, C), F32),
        compiler_params=_params("parallel", "parallel"))(
            u_pre, u_pre, u_pre, u_pre, conv_w, conv_w, conv_b, conv_b, d_act)
    return du.reshape(G4, S, C)


def _conv_bwd(du, u_pre, conv_w):
    G4, S, C = du.shape
    ts = _tile(S, 256)
    nh = ts // HALO
    last_halo = S // HALO - 1

    def body(du_ref, dun_ref, u_ref, up_ref, w_ref, dpre_ref, dw_ref, db_ref):
        i = pl.program_id(1)
        rows = lax.broadcasted_iota(jnp.int32, (ts, C), 0)
        du_c = du_ref[...]
        nxt = jnp.where(i == pl.num_programs(1) - 1, 0.0, dun_ref[...])
        prev = jnp.where(i == 0, 0.0, up_ref[...])
        w = w_ref[...]
        dpre = w[2:3, :] * du_c + w[1:2, :] * _shift_up(du_c, nxt, 1, rows, ts) + w[0:1, :] * _shift_up(du_c, nxt, 2, rows, ts)
        dpre_ref[...] = dpre.astype(BF)
        u_c = u_ref[...]
        dw = jnp.concatenate([
            jnp.sum(du_c * _shift_down(u_c, prev, 2, rows), axis=0, keepdims=True),
            jnp.sum(du_c * _shift_down(u_c, prev, 1, rows), axis=0, keepdims=True),
            jnp.sum(du_c * u_c, axis=0, keepdims=True)], axis=0)
        _accumulate(i, dw_ref, dw)
        _accumulate(i, db_ref, jnp.sum(du_c, axis=0, keepdims=True))

    cur = pl.BlockSpec((None, ts, C), lambda g, i: (g, i, 0))
    nxt = pl.BlockSpec((None, HALO, C), lambda g, i: (g, jnp.minimum((i + 1) * nh, last_halo), 0))
    prev = pl.BlockSpec((None, HALO, C), lambda g, i: (g, jnp.maximum(i * nh - 1, 0), 0))
    return pl.pallas_call(
        body, name="conv_bwd", grid=(G4, S // ts),
        in_specs=[cur, nxt, cur, prev, pl.BlockSpec((None, 3, C), lambda g, i: (g, 0, 0))],
        out_specs=[cur, pl.BlockSpec((None, 3, C), lambda g, i: (g, 0, 0)), pl.BlockSpec((None, 1, C), lambda g, i: (g, 0, 0))],
        out_shape=[jax.ShapeDtypeStruct((G4, S, C), BF), jax.ShapeDtypeStruct((G4, 3, C), F32),
                   jax.ShapeDtypeStruct((G4, 1, C), F32)],
        compiler_params=_params("parallel", "arbitrary"))(du, du, u_pre, u_pre, conv_w)


MLA_T = 512
MLA_HB = 4
MLA_HW = MLA_HB * HEAD


def _mla_pairs(n, by_row):
    if by_row:
        pairs = [(i, j) for i in range(n) for j in range(i + 1)]
    else:
        pairs = [(i, j) for j in range(n) for i in range(j, n)]
    return jnp.asarray([p[0] for p in pairs], jnp.int32), jnp.asarray([p[1] for p in pairs], jnp.int32)


def _mla_specs(S):
    q = pl.BlockSpec((MLA_T, MLA_HW), lambda g, t, it, jt: (it[t], g))
    k = pl.BlockSpec((MLA_T, MLA_HW), lambda g, t, it, jt: (jt[t], g))
    kpe = pl.BlockSpec((MLA_T, HEAD), lambda g, t, it, jt: (jt[t], 0))
    lse = pl.BlockSpec((MLA_HB, MLA_T, LANE), lambda g, t, it, jt: (g, it[t], 0))
    return q, k, kpe, lse


def _mla_head(ref, hh):
    return ref[:, hh * HEAD:(hh + 1) * HEAD]


def _mla_scores(qn_ref, qp_ref, kn_ref, kpe, hh, ok):
    q = jnp.concatenate([_mla_head(qn_ref, hh), _mla_head(qp_ref, hh)], axis=1)
    k = jnp.concatenate([_mla_head(kn_ref, hh), kpe], axis=1)
    s = lax.dot_general(q, k, NT, preferred_element_type=F32) * MLA_SCALE
    return q, k, jnp.where(ok, s, -jnp.inf)


def _mla_causal(i, j):
    row = i * MLA_T + lax.broadcasted_iota(jnp.int32, (MLA_T, MLA_T), 0)
    col = j * MLA_T + lax.broadcasted_iota(jnp.int32, (MLA_T, MLA_T), 1)
    return col <= row


def _mla_fwd(qn, qp, kn, kpe, v):
    S = qn.shape[0]
    it, jt = _mla_pairs(S // MLA_T, True)

    def body(it_ref, jt_ref, qn_ref, qp_ref, kn_ref, kpe_ref, v_ref, o_ref, lse_ref, m_scr, acc_scr):
        t = pl.program_id(1)
        i, j = it_ref[t], jt_ref[t]

        @pl.when(j == 0)
        def _():
            m_scr[...] = jnp.full(m_scr.shape, -jnp.inf, F32)
            acc_scr[...] = jnp.zeros(acc_scr.shape, F32)

        ok = _mla_causal(i, j)
        kpe_v = kpe_ref[...]
        ones = jnp.ones((MLA_T, HEAD), BF)
        state = [(m_scr[hh], acc_scr[hh]) for hh in range(MLA_HB)]
        new = []
        for hh in range(MLA_HB):
            m_prev, acc = state[hh]
            _, _, s = _mla_scores(qn_ref, qp_ref, kn_ref, kpe_v, hh, ok)
            m_new = jnp.maximum(m_prev, jnp.max(s, axis=1, keepdims=True))
            p = jnp.exp(s - m_new).astype(BF)
            v1 = jnp.concatenate([_mla_head(v_ref, hh), ones], axis=1)
            new.append((m_new, jnp.exp(m_prev - m_new) * acc + lax.dot_general(p, v1, NN, preferred_element_type=F32)))
        for hh in range(MLA_HB):
            m_scr[hh], acc_scr[hh] = new[hh]

        @pl.when(j == i)
        def _():
            for hh in range(MLA_HB):
                l = acc_scr[hh, :, HEAD:]
                o_ref[:, hh * HEAD:(hh + 1) * HEAD] = (acc_scr[hh, :, :HEAD] / l).astype(BF)
                lse_ref[hh] = m_scr[hh] + jnp.log(l)

    qspec, kspec, kpespec, lsespec = _mla_specs(S)
    grid_spec = pltpu.PrefetchScalarGridSpec(
        num_scalar_prefetch=2, grid=(MLA_HEADS // MLA_HB, it.shape[0]),
        in_specs=[qspec, qspec, kspec, kpespec, kspec], out_specs=[qspec, lsespec],
        scratch_shapes=[pltpu.VMEM((MLA_HB, MLA_T, 1), F32), pltpu.VMEM((MLA_HB, MLA_T, 2 * HEAD), F32)])
    return pl.pallas_call(
        body, name="mla_fwd", grid_spec=grid_spec,
        out_shape=[jax.ShapeDtypeStruct((S, MLA_HEADS * HEAD), BF), jax.ShapeDtypeStruct((MLA_HEADS, S, LANE), F32)],
        compiler_params=_params("parallel", "arbitrary"))(it, jt, qn, qp, kn, kpe, v)


def _mla_p_ds(qn_ref, qp_ref, kn_ref, kpe, v_ref, do_ref, o_ref, lse_ref, hh, ok):
    q, k, s = _mla_scores(qn_ref, qp_ref, kn_ref, kpe, hh, ok)
    p = jnp.exp(s - lse_ref[hh][:, 0:1])
    do = _mla_head(do_ref, hh)
    delta = jnp.sum(do.astype(F32) * _mla_head(o_ref, hh).astype(F32), axis=1, keepdims=True)
    dp = lax.dot_general(do, _mla_head(v_ref, hh), NT, preferred_element_type=F32)
    ds = p * (dp - delta) * MLA_SCALE
    return q, k, p, ds, do


def _mla_bwd_dq(qn, qp, kn, kpe, v, do, o, lse):
    S = qn.shape[0]
    it, jt = _mla_pairs(S // MLA_T, True)

    def body(it_ref, jt_ref, qn_ref, qp_ref, kn_ref, kpe_ref, v_ref, do_ref, o_ref, lse_ref, dqn_ref, dqp_ref, acc):
        t = pl.program_id(1)
        i, j = it_ref[t], jt_ref[t]

        @pl.when(j == 0)
        def _():
            acc[...] = jnp.zeros(acc.shape, F32)

        ok = _mla_causal(i, j)
        kpe_v = kpe_ref[...]
        old = [acc[hh] for hh in range(MLA_HB)]
        for hh in range(MLA_HB):
            _, k, _, ds, _ = _mla_p_ds(qn_ref, qp_ref, kn_ref, kpe_v, v_ref, do_ref, o_ref, lse_ref, hh, ok)
            old[hh] = old[hh] + lax.dot_general(ds.astype(BF), k, NN, preferred_element_type=F32)
        for hh in range(MLA_HB):
            acc[hh] = old[hh]

        @pl.when(j == i)
        def _():
            for hh in range(MLA_HB):
                dqn_ref[:, hh * HEAD:(hh + 1) * HEAD] = acc[hh, :, :HEAD].astype(BF)
                dqp_ref[:, hh * HEAD:(hh + 1) * HEAD] = acc[hh, :, HEAD:]

    qspec, kspec, kpespec, lsespec = _mla_specs(S)
    grid_spec = pltpu.PrefetchScalarGridSpec(
        num_scalar_prefetch=2, grid=(MLA_HEADS // MLA_HB, it.shape[0]),
        in_specs=[qspec, qspec, kspec, kpespec, kspec, qspec, qspec, lsespec], out_specs=[qspec, qspec],
        scratch_shapes=[pltpu.VMEM((MLA_HB, MLA_T, 2 * HEAD), F32)])
    return pl.pallas_call(
        body, name="mla_bwd_dq", grid_spec=grid_spec,
        out_shape=[jax.ShapeDtypeStruct((S, MLA_HEADS * HEAD), BF), jax.ShapeDtypeStruct((S, MLA_HEADS * HEAD), F32)],
        compiler_params=_params("parallel", "arbitrary"))(it, jt, qn, qp, kn, kpe, v, do, o, lse)


def _mla_bwd_dkv(qn, qp, kn, kpe, v, do, o, lse):
    S = qn.shape[0]
    nq = S // MLA_T
    it, jt = _mla_pairs(nq, False)

    def body(it_ref, jt_ref, qn_ref, qp_ref, kn_ref, kpe_ref, v_ref, do_ref, o_ref, lse_ref, dkn_ref, dv_ref, dkpe_ref,
             dk_acc, dv_acc):
        t = pl.program_id(1)
        i, j = it_ref[t], jt_ref[t]

        @pl.when(i == j)
        def _():
            dk_acc[...] = jnp.zeros(dk_acc.shape, F32)
            dv_acc[...] = jnp.zeros(dv_acc.shape, F32)

        ok = _mla_causal(i, j)
        kpe_v = kpe_ref[...]
        for hh in range(MLA_HB):
            q, _, p, ds, do = _mla_p_ds(qn_ref, qp_ref, kn_ref, kpe_v, v_ref, do_ref, o_ref, lse_ref, hh, ok)
            dv_acc[hh] += lax.dot_general(p.astype(BF), do, TN, preferred_element_type=F32)
            dk_acc[hh] += lax.dot_general(ds.astype(BF), q, TN, preferred_element_type=F32)

        @pl.when(i == nq - 1)
        def _():
            dkpe = dk_acc[0, :, HEAD:]
            for hh in range(MLA_HB):
                dkn_ref[:, hh * HEAD:(hh + 1) * HEAD] = dk_acc[hh, :, :HEAD].astype(BF)
                dv_ref[:, hh * HEAD:(hh + 1) * HEAD] = dv_acc[hh].astype(BF)
                if hh:
                    dkpe = dkpe + dk_acc[hh, :, HEAD:]
            dkpe_ref[...] = dkpe

    qspec, kspec, kpespec, lsespec = _mla_specs(S)
    dkpespec = pl.BlockSpec((None, MLA_T, HEAD), lambda g, t, it, jt: (g, jt[t], 0))
    grid_spec = pltpu.PrefetchScalarGridSpec(
        num_scalar_prefetch=2, grid=(MLA_HEADS // MLA_HB, it.shape[0]),
        in_specs=[qspec, qspec, kspec, kpespec, kspec, qspec, qspec, lsespec], out_specs=[kspec, kspec, dkpespec],
        scratch_shapes=[pltpu.VMEM((MLA_HB, MLA_T, 2 * HEAD), F32), pltpu.VMEM((MLA_HB, MLA_T, HEAD), F32)])
    return pl.pallas_call(
        body, name="mla_bwd_dkv", grid_spec=grid_spec,
        out_shape=[jax.ShapeDtypeStruct((S, MLA_HEADS * HEAD), BF), jax.ShapeDtypeStruct((S, MLA_HEADS * HEAD), BF),
                   jax.ShapeDtypeStruct((MLA_HEADS // MLA_HB, S, HEAD), F32)],
        compiler_params=_params("parallel", "arbitrary"))(it, jt, qn, qp, kn, kpe, v, do, o, lse)


DIL_W = 3 * DIL_HPG * HEAD
DIL_O = DIL_HPG * HEAD


def _dil_slopes(g):
    return [2.0 ** (-ALIBI_MAX_BIAS * (g * DIL_HPG + hh + 1) / DIL_HEADS) for hh in range(DIL_HPG)]


def _dil_bias(dil):
    p = lax.broadcasted_iota(jnp.int32, (DIL_BLOCK, DIL_BLOCK), 0)
    kk = lax.broadcasted_iota(jnp.int32, (DIL_BLOCK, DIL_BLOCK), 1)
    jc = p - kk
    dist_c = (dil * jc).astype(F32)
    dist_p = (dil * (jc + DIL_BLOCK)).astype(F32)
    return jc >= 0, jc <= 0, dist_c, dist_p


def _dil_head(blk, hh):
    q = blk[:, hh * HEAD:(hh + 1) * HEAD]
    k = blk[:, DIL_O + hh * HEAD:DIL_O + (hh + 1) * HEAD]
    v = blk[:, 2 * DIL_O + hh * HEAD:2 * DIL_O + (hh + 1) * HEAD]
    return q, k, v


def _dil_s(q, k, slope, dist, ok):
    s = lax.dot_general(q, k, NT, preferred_element_type=F32) * DIL_SCALE - slope * dist
    return jnp.where(ok, s, -jnp.inf)


def _dil_view(a, dil):
    S, W = a.shape
    return a.reshape(S // dil, dil * W)


def _dil_fwd(qkv, g):
    _, dil = DIL_PATTERNS[g]
    S = qkv.shape[0]
    L = S // dil
    nb = L // DIL_BLOCK
    slopes = _dil_slopes(g)

    def body(cur_ref, prev_ref, o_ref, lse_ref):
        n = pl.program_id(1)
        ok_c, ok_p, dist_c, dist_p = _dil_bias(dil)
        ok_p = ok_p & (n > 0)
        cur, prev = cur_ref[...], prev_ref[...]
        for hh in range(DIL_HPG):
            q, kc, vc = _dil_head(cur, hh)
            _, kp, vp = _dil_head(prev, hh)
            sc = _dil_s(q, kc, slopes[hh], dist_c, ok_c)
            sp = _dil_s(q, kp, slopes[hh], dist_p, ok_p)
            m = jnp.maximum(jnp.max(sc, axis=1, keepdims=True), jnp.max(sp, axis=1, keepdims=True))
            pc, pp = jnp.exp(sc - m), jnp.exp(sp - m)
            l = jnp.sum(pc, axis=1, keepdims=True) + jnp.sum(pp, axis=1, keepdims=True)
            o = (lax.dot_general(pc.astype(BF), vc, NN, preferred_element_type=F32)
                 + lax.dot_general(pp.astype(BF), vp, NN, preferred_element_type=F32)) / l
            sl = slice(hh * HEAD, (hh + 1) * HEAD)
            o_ref[:, sl] = o
            lse_ref[:, sl] = jnp.broadcast_to(m + jnp.log(l), (DIL_BLOCK, HEAD))

    ospec = pl.BlockSpec((DIL_BLOCK, DIL_O), lambda r, n: (n, r))
    o, lse = pl.pallas_call(
        body, name=f"dil_fwd{g}", grid=(dil, nb),
        in_specs=[pl.BlockSpec((DIL_BLOCK, DIL_W), lambda r, n: (n, r)),
                  pl.BlockSpec((DIL_BLOCK, DIL_W), lambda r, n: (jnp.maximum(n - 1, 0), r))],
        out_specs=[ospec, ospec],
        out_shape=[jax.ShapeDtypeStruct((L, dil * DIL_O), F32), jax.ShapeDtypeStruct((L, dil * DIL_O), F32)],
        compiler_params=_params("parallel", "parallel"))(_dil_view(qkv, dil), _dil_view(qkv, dil))
    return o.reshape(S, DIL_O), lse.reshape(S, DIL_O)


def _dil_combine(os_, lses):
    S = os_[0].shape[0]
    ts = _tile(S, 512)

    def body(o0, o1, o2, l0, l1, l2, out_ref, lse_ref):
        a, b, c = l0[...], l1[...], l2[...]
        m = jnp.maximum(jnp.maximum(a, b), c)
        ea, eb, ec = jnp.exp(a - m), jnp.exp(b - m), jnp.exp(c - m)
        tot = ea + eb + ec
        out_ref[...] = ((ea * o0[...] + eb * o1[...] + ec * o2[...]) / tot).astype(BF)
        lse_ref[...] = m + jnp.log(tot)

    return pl.pallas_call(
        body, name="dil_combine", grid=(S // ts,), in_specs=[_row(ts, DIL_O)] * 6,
        out_specs=[_row(ts, DIL_O), _row(ts, DIL_O)],
        out_shape=[jax.ShapeDtypeStruct((S, DIL_O), BF), jax.ShapeDtypeStruct((S, DIL_O), F32)],
        compiler_params=_params("parallel"))(*os_, *lses)


def _dil_delta(do, out):
    S = do.shape[0]
    ts = _tile(S, 512)

    def body(do_ref, o_ref, d_ref):
        for hh in range(DIL_HPG):
            sl = slice(hh * HEAD, (hh + 1) * HEAD)
            d = jnp.sum(do_ref[:, sl].astype(F32) * o_ref[:, sl].astype(F32), axis=1, keepdims=True)
            d_ref[:, sl] = jnp.broadcast_to(d, (ts, HEAD))

    return pl.pallas_call(
        body, name="dil_delta", grid=(S // ts,), in_specs=[_row(ts, DIL_O)] * 2, out_specs=_row(ts, DIL_O),
        out_shape=jax.ShapeDtypeStruct((S, DIL_O), F32), compiler_params=_params("parallel"))(do, out)


def _dil_bwd(qkv, do, lse, delta, g):
    _, dil = DIL_PATTERNS[g]
    S = qkv.shape[0]
    L = S // dil
    nb = L // DIL_BLOCK
    slopes = _dil_slopes(g)

    def pair(q, k, v, do_h, lse_h, delta_h, slope, dist, ok):
        s = _dil_s(q, k, slope, dist, ok)
        p = jnp.exp(s - lse_h)
        dp = lax.dot_general(do_h, v, NT, preferred_element_type=F32)
        ds = (p * (dp - delta_h) * DIL_SCALE).astype(BF)
        return p.astype(BF), ds

    def body(cur_ref, prev_ref, next_ref, doc_ref, don_ref, lsec_ref, lsen_ref, dlc_ref, dln_ref, out_ref):
        n = pl.program_id(1)
        ok_c, ok_p0, dist_c, dist_p = _dil_bias(dil)
        ok_a = ok_p0 & (n > 0)
        ok_n = ok_p0 & (n < nb - 1)
        cur, prev, nxt = cur_ref[...], prev_ref[...], next_ref[...]
        for hh in range(DIL_HPG):
            sl = slice(hh * HEAD, (hh + 1) * HEAD)
            q, kc, vc = _dil_head(cur, hh)
            _, kp, vp = _dil_head(prev, hh)
            qn, _, _ = _dil_head(nxt, hh)
            do_c, do_n = doc_ref[:, sl], don_ref[:, sl]
            lse_c, lse_n = lsec_ref[:, sl][:, 0:1], lsen_ref[:, sl][:, 0:1]
            dl_c, dl_n = dlc_ref[:, sl][:, 0:1], dln_ref[:, sl][:, 0:1]
            _, ds_a = pair(q, kp, vp, do_c, lse_c, dl_c, slopes[hh], dist_p, ok_a)
            p_b, ds_b = pair(q, kc, vc, do_c, lse_c, dl_c, slopes[hh], dist_c, ok_c)
            p_n, ds_n = pair(qn, kc, vc, do_n, lse_n, dl_n, slopes[hh], dist_p, ok_n)
            dq = (lax.dot_general(ds_a, kp, NN, preferred_element_type=F32)
                  + lax.dot_general(ds_b, kc, NN, preferred_element_type=F32))
            dk = (lax.dot_general(ds_b, q, TN, preferred_element_type=F32)
                  + lax.dot_general(ds_n, qn, TN, preferred_element_type=F32))
            dv = (lax.dot_general(p_b, do_c, TN, preferred_element_type=F32)
                  + lax.dot_general(p_n, do_n, TN, preferred_element_type=F32))
            out_ref[:, sl] = dq.astype(BF)
            out_ref[:, DIL_O + hh * HEAD:DIL_O + (hh + 1) * HEAD] = dk.astype(BF)
            out_ref[:, 2 * DIL_O + hh * HEAD:2 * DIL_O + (hh + 1) * HEAD] = dv.astype(BF)

    cur_w = pl.BlockSpec((DIL_BLOCK, DIL_W), lambda r, n: (n, r))
    prev_w = pl.BlockSpec((DIL_BLOCK, DIL_W), lambda r, n: (jnp.maximum(n - 1, 0), r))
    next_w = pl.BlockSpec((DIL_BLOCK, DIL_W), lambda r, n: (jnp.minimum(n + 1, nb - 1), r))
    cur_o = pl.BlockSpec((DIL_BLOCK, DIL_O), lambda r, n: (n, r))
    next_o = pl.BlockSpec((DIL_BLOCK, DIL_O), lambda r, n: (jnp.minimum(n + 1, nb - 1), r))
    qv, dov, lsev, dlv = _dil_view(qkv, dil), _dil_view(do, dil), _dil_view(lse, dil), _dil_view(delta, dil)
    out = pl.pallas_call(
        body, name=f"dil_bwd{g}", grid=(dil, nb),
        in_specs=[cur_w, prev_w, next_w, cur_o, next_o, cur_o, next_o, cur_o, next_o],
        out_specs=cur_w, out_shape=jax.ShapeDtypeStruct((L, dil * DIL_W), BF),
        compiler_params=_params("parallel", "parallel"))(qv, qv, qv, dov, dov, lsev, lsev, dlv, dlv)
    return out.reshape(S, DIL_W)


def _adamw(name, w, g, m, v):
    R, C = w.shape
    tr = R if R <= 512 else _tile_rows(R, 256)

    def body(w_ref, g_ref, m_ref, v_ref, d_ref, nm_ref, nv_ref):
        gv = g_ref[...]
        nm = ADAM_B1 * m_ref[...] + (1.0 - ADAM_B1) * gv
        nv = ADAM_B2 * v_ref[...] + (1.0 - ADAM_B2) * (gv * gv)
        m_hat = nm / (1.0 - ADAM_B1 ** ADAM_STEP)
        v_hat = nv / (1.0 - ADAM_B2 ** ADAM_STEP)
        d_ref[...] = -ADAM_LR * (m_hat / (jnp.sqrt(v_hat) + ADAM_EPS) + ADAM_WD * w_ref[...])
        nm_ref[...] = nm
        nv_ref[...] = nv

    spec = pl.BlockSpec((tr, C), lambda i: (i, 0))
    shp = jax.ShapeDtypeStruct((R, C), F32)
    return pl.pallas_call(
        body, name=name, grid=(R // tr,), in_specs=[spec] * 4, out_specs=[spec] * 3, out_shape=[shp] * 3,
        compiler_params=_params("parallel"))(w, g, m, v)


def _tile_rows(n, pref, mult=8):
    t = (pref // mult) * mult
    while t >= mult:
        if n % t == 0:
            return t
        t -= mult
    return n


ANY = pl.BlockSpec(memory_space=pl.ANY)


def _place():
    x, y, c = lax.axis_index("x"), lax.axis_index("y"), lax.axis_index("c")
    chips = [(1 - x, y), (x, 1 - y), (1 - x, 1 - y)]
    chip_idx = [2 * cx + cy for cx, cy in chips]
    return x, y, c, 2 * x + y, chips, chip_idx


def _rcopy(src, dst, ssem, rsem, dev):
    return pltpu.make_async_remote_copy(src_ref=src, dst_ref=dst, send_sem=ssem, recv_sem=rsem,
                                        device_id=dev, device_id_type=MESH)


HBM = pl.BlockSpec(memory_space=pltpu.HBM)
SEM = pl.BlockSpec(memory_space=pltpu.SEMAPHORE)
EFFECT = pltpu.SideEffectType.DATAFLOW_SIDE_EFFECTING


def _split_copies(kind, srcs, lands, ssem, rsem):
    _, _, c, me, chips, chip_idx = _place()
    cps = []
    for i in range(len(srcs)):
        for k in range(3):
            if kind == "gather":
                src, dst = srcs[i], lands[i].at[me]
            else:
                src, dst = srcs[i].at[chip_idx[k]], lands[i].at[k]
            cps.append(_rcopy(src, dst, ssem.at[3 * i + k], rsem.at[3 * i + k], (*chips[k], c)))
    return cps


def _exchange_start(name, kind, srcs, lands, groups):
    n, ng = len(srcs), len(groups)

    def body(*refs):
        src_refs, land_refs = refs[:n], refs[n:2 * n]
        sems = refs[2 * n:2 * n + 2 * ng]
        token = refs[-1]
        for gi, grp in enumerate(groups):
            cps = _split_copies(kind, [src_refs[i] for i in grp], [land_refs[i] for i in grp], sems[2 * gi], sems[2 * gi + 1])
            for cp in cps:
                cp.start()
        token[...] = jnp.zeros_like(token)

    arrays = list(srcs) + list(lands)
    out_shape = []
    for grp in groups:
        out_shape += [pltpu.SemaphoreType.DMA((3 * len(grp),)), pltpu.SemaphoreType.DMA((3 * len(grp),))]
    out_shape += [pltpu.HBM(a.shape, a.dtype) for a in arrays] + [jax.ShapeDtypeStruct((8, LANE), F32)]
    outs = pl.pallas_call(
        body, name=name, out_shape=out_shape, in_specs=[HBM] * (2 * n),
        out_specs=[SEM] * (2 * ng) + [HBM] * (2 * n) + [pl.BlockSpec(memory_space=pltpu.VMEM)],
        input_output_aliases={i: 2 * ng + i for i in range(2 * n)},
        compiler_params=pltpu.CompilerParams(has_side_effects=EFFECT),
    )(*[pltpu.with_memory_space_constraint(a, pltpu.HBM) for a in arrays])
    sems = [(outs[2 * gi], outs[2 * gi + 1]) for gi in range(ng)]
    thru = outs[2 * ng:2 * ng + 2 * n]
    return sems, thru[:n], thru[n:], outs[-1]


def _exchange_wait(name, kind, srcs, lands, sems, after):
    n = len(srcs)

    def body(*refs):
        cps = _split_copies(kind, refs[:n], refs[n:2 * n], refs[2 * n], refs[2 * n + 1])
        for cp in cps:
            cp.wait_send()
            cp.wait_recv()

    arrays = list(srcs) + list(lands)
    outs = pl.pallas_call(
        body, name=name, out_shape=[pltpu.HBM(a.shape, a.dtype) for a in arrays],
        in_specs=[HBM] * (2 * n) + [SEM, SEM, ANY], out_specs=[HBM] * (2 * n),
        input_output_aliases={i: i for i in range(2 * n)},
        compiler_params=pltpu.CompilerParams(has_side_effects=EFFECT),
    )(*arrays, sems[0], sems[1], after)
    return outs[:n], outs[n:]


EXCHANGE_CHUNK_BYTES = 3 * 512 * 1024


def _half_geometry(R, C, axis):
    Rp, Cp = (R // 2, C) if axis == 0 else (R, C // 2)
    tr = _tile_rows(Rp, max(16, EXCHANGE_CHUNK_BYTES // (2 * Cp)), 16)
    return Rp, Cp, tr, Rp // tr


def _pair_sum(name, g, axis):
    G, R, C = g.shape
    Rp, Cp, tr, nb = _half_geometry(R, C, axis)
    steps = G * nb

    def half_block(s, b, h):
        return (s, h * nb + b, 0) if axis == 0 else (s, b, h)

    def body(c_ref, keep_ref, give_ref, out_ref, land, ssem, rsem, credit):
        x, y, c = lax.axis_index("x"), lax.axis_index("y"), lax.axis_index("c")
        sib = (x, y, 1 - c)
        t = pl.program_id(0) * nb + pl.program_id(1)
        slot = t % 2

        @pl.when(t >= 2)
        def _():
            pl.semaphore_wait(credit, 1)

        cp = _rcopy(give_ref.at[0], land.at[slot], ssem.at[slot], rsem.at[slot], sib)
        cp.start()
        cp.wait_recv()
        out_ref[...] = (keep_ref[...].astype(F32) + land[slot].astype(F32)).astype(BF)

        @pl.when(t + 2 < steps)
        def _():
            pl.semaphore_signal(credit, 1, device_id=sib, device_id_type=MESH)

        cp.wait_send()

    blk = (None, tr, Cp)
    grid_spec = pltpu.PrefetchScalarGridSpec(
        num_scalar_prefetch=1, grid=(G, nb),
        in_specs=[pl.BlockSpec(blk, lambda s, b, c_ref: half_block(s, b, c_ref[0])),
                  pl.BlockSpec((1, tr, Cp), lambda s, b, c_ref: half_block(s, b, 1 - c_ref[0]))],
        out_specs=pl.BlockSpec(blk, lambda s, b, c_ref: (s, b, 0)),
        scratch_shapes=[pltpu.VMEM((2, tr, Cp), BF), pltpu.SemaphoreType.DMA((2,)), pltpu.SemaphoreType.DMA((2,)),
                        pltpu.SemaphoreType.REGULAR])
    c_arr = lax.axis_index("c").astype(jnp.int32).reshape(1)
    return pl.pallas_call(
        body, name=name, grid_spec=grid_spec, out_shape=jax.ShapeDtypeStruct((G, Rp, Cp), BF),
        compiler_params=_params("arbitrary", "arbitrary"))(c_arr, g, g)


def _chip_total(name, h, landed):
    G, Rp, Cp = h.shape
    tr = _tile_rows(Rp, max(16, EXCHANGE_CHUNK_BYTES // (2 * Cp)), 16)

    def body(me_ref, own_ref, l0_ref, l1_ref, l2_ref, out_ref):
        acc = own_ref[...].astype(F32)
        for r in (l0_ref, l1_ref, l2_ref):
            acc = acc + r[...].astype(F32)
        out_ref[...] = acc

    blk = (None, tr, Cp)
    grid_spec = pltpu.PrefetchScalarGridSpec(
        num_scalar_prefetch=1, grid=(Rp // tr,),
        in_specs=[pl.BlockSpec(blk, lambda b, me_ref: (me_ref[0], b, 0))]
        + [pl.BlockSpec(blk, functools.partial(lambda b, me_ref, k: (k, b, 0), k=k)) for k in range(3)],
        out_specs=pl.BlockSpec((tr, Cp), lambda b, me_ref: (b, 0)))
    me = (2 * lax.axis_index("x") + lax.axis_index("y")).astype(jnp.int32).reshape(1)
    return pl.pallas_call(
        body, name=name, grid_spec=grid_spec, out_shape=jax.ShapeDtypeStruct((Rp, Cp), F32),
        compiler_params=_params("parallel"))(me, h, landed, landed, landed)


def _pair_join(name, f, axis):
    Rp, Cp = f.shape
    R, C = (2 * Rp, Cp) if axis == 0 else (Rp, 2 * Cp)
    tr = _tile_rows(Rp, max(8, EXCHANGE_CHUNK_BYTES // (4 * Cp)), 8)
    nb = Rp // tr

    def body(f_ref, full, ssem, rsem, lsem):
        x, y, c = lax.axis_index("x"), lax.axis_index("y"), lax.axis_index("c")
        sib = (x, y, 1 - c)
        b = pl.program_id(0)

        def place(h, r0, rows):
            if axis == 0:
                return full.at[pl.ds(pl.multiple_of(h * Rp + r0, 8), rows), :]
            return full.at[pl.ds(pl.multiple_of(r0, 8), rows), pl.ds(pl.multiple_of(h * Cp, LANE), Cp)]

        mine = place(c, b * tr, tr)
        loc = pltpu.make_async_copy(f_ref, mine, lsem)
        rem = _rcopy(f_ref, mine, ssem, rsem, sib)
        loc.start()
        rem.start()
        loc.wait()
        rem.wait_send()

        @pl.when(b == nb - 1)
        def _():
            theirs = place(1 - c, 0, Rp)
            _rcopy(theirs, theirs, ssem, rsem, sib).wait_recv()

    return pl.pallas_call(
        body, name=name, grid=(nb,), in_specs=[pl.BlockSpec((tr, Cp), lambda b: (b, 0))], out_specs=ANY,
        out_shape=jax.ShapeDtypeStruct((R, C), F32),
        scratch_shapes=[pltpu.SemaphoreType.DMA, pltpu.SemaphoreType.DMA, pltpu.SemaphoreType.DMA],
        compiler_params=_params("arbitrary"))(f)


def _allreduce_small(v):
    R, K = v.shape
    ndev = 8

    def body(v_ref, o_ref, land, ssem, rsem):
        x, y, c = lax.axis_index("x"), lax.axis_index("y"), lax.axis_index("c")
        me = 4 * x + 2 * y + c
        land[me] = v_ref[...]
        cps = []
        for r in range(1, ndev):
            fx, fy, fc = (r >> 2) & 1, (r >> 1) & 1, r & 1
            peer = (x ^ fx, y ^ fy, c ^ fc)
            cp = _rcopy(v_ref, land.at[me], ssem.at[r - 1], rsem.at[r - 1], peer)
            cp.start()
            cps.append((cp, 4 * peer[0] + 2 * peer[1] + peer[2], r))
        for cp, src, r in cps:
            cp.wait_send()
            _rcopy(v_ref, land.at[src], ssem.at[r - 1], rsem.at[r - 1], (x, y, c)).wait_recv()
        acc = land[0]
        for d in range(1, ndev):
            acc = acc + land[d]
        o_ref[...] = acc

    vm = pl.BlockSpec(memory_space=pltpu.VMEM)
    return pl.pallas_call(
        body, name="allreduce_small", in_specs=[vm], out_specs=vm, out_shape=jax.ShapeDtypeStruct((R, K), F32),
        scratch_shapes=[pltpu.VMEM((ndev, R, K), F32), pltpu.SemaphoreType.DMA((ndev - 1,)), pltpu.SemaphoreType.DMA((ndev - 1,))],
    )(v)


IN_SPLITS = (Q_RANK, KV_RANK, QK_ROPE, DIL_HEADS * HEAD, DIL_HEADS * HEAD, DIL_HEADS * HEAD, D_MODEL, D_MODEL)
IN_OFF = tuple(int(v) for v in np.cumsum((0,) + IN_SPLITS))


def _unshard_cols(g):
    G, K, Ns = g.shape
    return g.transpose(1, 0, 2).reshape(K, G * Ns)


def _shard_cols(w):
    K, N = w.shape
    return w.reshape(K, N_CHIPS, N // N_CHIPS).transpose(1, 0, 2)


def _rope_pad(w):
    half = QK_ROPE // 2
    z = jnp.zeros(w.shape[:-1] + (half,), w.dtype)
    return jnp.concatenate([w[..., :half], z, w[..., half:], z], axis=-1)


def _rope_unpad(w):
    half = QK_ROPE // 2
    return jnp.concatenate([w[..., :half], w[..., 2 * half:3 * half]], axis=-1)


def _split_w_in(w_in_g):
    w = _unshard_cols(w_in_g)
    K = w.shape[0]
    p = [w[:, IN_OFF[i]:IN_OFF[i + 1]] for i in range(8)]
    w_lat = jnp.concatenate([p[0], p[1], _rope_pad(p[2]), jnp.zeros((K, LAT_W - _KPE.stop), w.dtype)], axis=1)
    w_dil = [jnp.concatenate([p[3 + t][:, g * DIL_O:(g + 1) * DIL_O] for t in range(3)], axis=1) for g in range(DIL_GROUPS)]
    w_gate = jnp.concatenate([p[6], p[7]], axis=1)
    return w_lat, w_dil, w_gate


def _merge_dw_in(dw_lat, dw_dil, dw_gate):
    parts = [dw_lat[:, _CQ], dw_lat[:, _CKV], _rope_unpad(dw_lat[:, _KPE])]
    for t in range(3):
        parts += [dw_dil[g][:, t * DIL_O:(t + 1) * DIL_O] for g in range(DIL_GROUPS)]
    parts.append(dw_gate)
    return _shard_cols(jnp.concatenate(parts, axis=1))


def _split_w_uq(w_uq_g):
    w = _unshard_cols(w_uq_g)
    K = w.shape[0]
    w = w.reshape(K, MLA_HEADS, QK_NOPE + QK_ROPE)
    return w[:, :, :QK_NOPE].reshape(K, MLA_HEADS * HEAD), _rope_pad(w[:, :, QK_NOPE:]).reshape(K, MLA_HEADS * HEAD)


def _merge_dw_uq(dw_n, dw_p):
    K = dw_n.shape[0]
    w = jnp.concatenate([dw_n.reshape(K, MLA_HEADS, HEAD), _rope_unpad(dw_p.reshape(K, MLA_HEADS, HEAD))], axis=-1)
    return _shard_cols(w.reshape(K, MLA_HEADS * (QK_NOPE + QK_ROPE)))


def _split_w_ukv(w_ukv_g):
    w = _unshard_cols(w_ukv_g)
    K = w.shape[0]
    w = w.reshape(K, MLA_HEADS, 2 * HEAD)
    return w[:, :, :HEAD].reshape(K, MLA_HEADS * HEAD), w[:, :, HEAD:].reshape(K, MLA_HEADS * HEAD)


def _merge_dw_ukv(dw_k, dw_v):
    K = dw_k.shape[0]
    w = jnp.concatenate([dw_k.reshape(K, MLA_HEADS, HEAD), dw_v.reshape(K, MLA_HEADS, HEAD)], axis=-1)
    return _shard_cols(w.reshape(K, MLA_HEADS * 2 * HEAD))


GATHER_GROUPS = (("w_in",), ("w_uq", "w_ukv", "w_o_mla", "w_o_dil", "w_out"), ("w_up", "w_down", "conv_w"))
REDUCE_GROUPS = (("w_down", "w_up"), ("w_out", "w_o_mla", "w_o_dil"), ("w_uq", "w_ukv", "w_in"))


def _local_step(x, tgt, W, fetch, emit):
    S, D = x.shape
    cos, sin_s = _rope_tables(S)
    w_lat, w_dil, w_gate = _split_w_in(fetch(0, x)["w_in"])

    h = _rmsnorm_fwd("attn_norm", x, W["attn_norm_g"])
    lat = _mm_nn("proj_lat", h, w_lat)
    qkv = [_mm_nn(f"proj_dil{g}", h, w_dil[g], o_dtype=BF) for g in range(DIL_GROUPS)]
    gpre = _mm_nn("proj_gate", h, w_gate)
    WB = fetch(1, gpre)
    w_uqn, w_uqp = _split_w_uq(WB["w_uq"])
    w_k, w_v = _split_w_ukv(WB["w_ukv"])
    w_o_mla, w_o_dil = WB["w_o_mla"], WB["w_o_dil"]
    w_out = WB["w_out"].reshape(D, D)
    qn_, kvn, kpe = _mla_prep(lat, W["q_norm_g"], W["kv_norm_g"], cos, sin_s)
    q_nope = _mm_nn("q_nope", qn_, w_uqn, o_dtype=BF)
    q_pe = _rope("q_rope", _mm_nn("q_pe", qn_, w_uqp), cos, sin_s, False)
    k_nope = _mm_nn("k_nope", kvn, w_k, o_dtype=BF)
    v_mla = _mm_nn("v_mla", kvn, w_v, o_dtype=BF)
    attn_a, lse_a = _mla_fwd(q_nope, q_pe, k_nope, kpe, v_mla)
    dil = [_dil_fwd(qkv[g], g) for g in range(DIL_GROUPS)]
    attn_b, lse_b = _dil_combine([o for o, _ in dil], [l for _, l in dil])
    o_a = _mm_nn("o_mla", attn_a, w_o_mla)
    o_b = _mm_nn("o_dil", attn_b, w_o_dil)
    merge = _merge_fwd(gpre, W["b_gate"], o_a, o_b)
    x1 = _mm_nn("out_proj", merge, w_out, add=x)
    WC = fetch(2, merge)
    w_up = WC["w_up"]
    G4, _, C = w_up.shape
    w_down = WC["w_down"].reshape(G4 // 2, C, D)
    conv_w = WC["conv_w"]
    conv_b = W["conv_b"].reshape(G4, 1, C)
    h2 = _rmsnorm_fwd("ffn_norm", x1, W["ffn_norm_g"])
    u_pre = _up_fwd(h2, w_up)
    act = _ffn_act(u_pre, conv_w, conv_b)
    x2 = _down_fwd(act, w_down, x1)
    dx2, d_final_g, loss8 = _final_loss(x2, tgt, W["final_norm_g"])

    d_act = _down_dgrad(dx2, w_down)
    dw_down = _down_wgrad(act, dx2)
    du = _ffn_act_bwd(u_pre, conv_w, conv_b, d_act)
    du_pre, d_conv_w, d_conv_b = _conv_bwd(du, u_pre, conv_w)
    dh2 = _up_dgrad(du_pre, w_up)
    dw_up = _up_wgrad(h2, du_pre)
    zero = emit(0, {"w_down": dw_down.reshape(N_CHIPS, (G4 // 2) * C // N_CHIPS, D), "w_up": dw_up})
    dx1, d_ffn_g = _rmsnorm_bwd("ffn_norm_bwd", dh2, x1, W["ffn_norm_g"] + zero, dx2)
    d_merge = _mm_nt("out_proj_dgrad", dx1, w_out)
    dw_out = _mm_tn("out_proj_wgrad", merge, dx1)
    d_oa, d_ob, d_gpre, d_b_gate = _merge_bwd(d_merge, gpre, W["b_gate"], o_a, o_b)
    d_attn_a = _mm_nt("o_mla_dgrad", d_oa, w_o_mla, o_dtype=BF)
    dw_o_mla = _mm_tn("o_mla_wgrad", attn_a, d_oa, shards=N_CHIPS)
    d_attn_b = _mm_nt("o_dil_dgrad", d_ob, w_o_dil, o_dtype=BF)
    dw_o_dil = _mm_tn("o_dil_wgrad", attn_b, d_ob, shards=N_CHIPS)
    zero = emit(1, {"w_out": dw_out.reshape(N_CHIPS, D // N_CHIPS, D), "w_o_mla": dw_o_mla, "w_o_dil": dw_o_dil})
    q_norm_g = W["q_norm_g"] + zero
    delta_b = _dil_delta(d_attn_b, attn_b)
    d_qkv = [_dil_bwd(qkv[g], d_attn_b, lse_b, delta_b, g) for g in range(DIL_GROUPS)]
    dq_nope, dq_pe_rot = _mla_bwd_dq(q_nope, q_pe, k_nope, kpe, v_mla, d_attn_a, attn_a, lse_a)
    dk_nope, dv_mla, dkpe_rot = _mla_bwd_dkv(q_nope, q_pe, k_nope, kpe, v_mla, d_attn_a, attn_a, lse_a)
    dq_pe = _rope("q_rope_bwd", dq_pe_rot, cos, sin_s, True)
    d_qn = _mm_nt("q_pe_dgrad", dq_pe, w_uqp, add=_mm_nt("q_nope_dgrad", dq_nope, w_uqn))
    d_kvn = _mm_nt("v_dgrad", dv_mla, w_v, add=_mm_nt("k_nope_dgrad", dk_nope, w_k))
    dw_uq = _merge_dw_uq(_mm_tn("q_nope_wgrad", qn_, dq_nope), _mm_tn("q_pe_wgrad", qn_, dq_pe))
    dw_ukv = _merge_dw_ukv(_mm_tn("k_nope_wgrad", kvn, dk_nope), _mm_tn("v_wgrad", kvn, dv_mla))
    d_lat, d_q_g, d_kv_g = _mla_prep_bwd(lat, q_norm_g, W["kv_norm_g"], cos, sin_s, d_qn, d_kvn, dkpe_rot)
    dw_in = _merge_dw_in(_mm_tn("proj_lat_wgrad", h, d_lat),
                         [_mm_tn(f"proj_dil{g}_wgrad", h, d_qkv[g]) for g in range(DIL_GROUPS)],
                         _mm_tn("proj_gate_wgrad", h, d_gpre))
    zero = emit(2, {"w_uq": dw_uq, "w_ukv": dw_ukv, "w_in": dw_in})
    dh = _mm_nt("proj_lat_dgrad", d_lat, w_lat + zero.astype(BF))
    for g in range(DIL_GROUPS):
        dh = _mm_nt(f"proj_dil{g}_dgrad", d_qkv[g], w_dil[g], add=dh)
    dh = _mm_nt("proj_gate_dgrad", d_gpre, w_gate, add=dh)
    grad_x, d_attn_g = _rmsnorm_bwd("attn_norm_bwd", dh, x, W["attn_norm_g"], dx1)

    small = {"attn_norm_g": d_attn_g, "b_gate": d_b_gate, "q_norm_g": d_q_g, "kv_norm_g": d_kv_g,
             "ffn_norm_g": d_ffn_g, "conv_w": d_conv_w, "conv_b": d_conv_b.reshape(1, G4 * C),
             "final_norm_g": d_final_g}
    return loss8[0, 0], grad_x, small


BIG = ("w_in", "w_uq", "w_ukv", "w_o_mla", "w_o_dil", "w_out", "w_up", "w_down")
SMALL = ("attn_norm_g", "b_gate", "q_norm_g", "kv_norm_g", "ffn_norm_g", "conv_w", "conv_b", "final_norm_g")
WEIGHTS = ("attn_norm_g", "w_in", "b_gate", "q_norm_g", "w_uq", "kv_norm_g", "w_ukv", "w_o_mla", "w_o_dil",
           "w_out", "ffn_norm_g", "w_up", "conv_w", "conv_b", "w_down", "final_norm_g")
SMALL_ROWS = 8
HALF_AXIS = {"w_down": 1}


def _gather_start(shards):
    names = [n for grp in GATHER_GROUPS for n in grp]
    chip = 2 * lax.axis_index("x") + lax.axis_index("y")
    srcs = [shards[n] for n in names]
    lands = [lax.dynamic_update_slice(lax.empty((N_CHIPS,) + s.shape, s.dtype), s[None], (chip, 0, 0)) for s in srcs]
    groups, at = [], 0
    for grp in GATHER_GROUPS:
        groups.append(list(range(at, at + len(grp))))
        at += len(grp)
    sems, srcs, lands, token = _exchange_start("gather_start", "gather", srcs, lands, groups)

    def fetch(i, after):
        idx = groups[i]
        _, got = _exchange_wait(f"gather_wait{i}", "gather", [srcs[j] for j in idx], [lands[j] for j in idx], sems[i], after)
        return dict(zip(GATHER_GROUPS[i], got))

    return fetch, token[0, 0]


def _reduce_start(i, grads):
    names = REDUCE_GROUPS[i]
    hs = [_pair_sum(f"pair_sum_{n}", grads[n], HALF_AXIS.get(n, 0)) for n in names]
    lands = [lax.empty((3,) + h.shape[1:], h.dtype) for h in hs]
    sems, hs, lands, token = _exchange_start(f"reduce_start{i}", "scatter", hs, lands, [list(range(len(names)))])
    return (sems[0], hs, lands), token[0, 0]


def _reduce_finish(i, pending, after):
    sems, hs, lands = pending
    hs, lands = _exchange_wait(f"reduce_wait{i}", "scatter", hs, lands, sems, after)
    out = {}
    for n, h, landed in zip(REDUCE_GROUPS[i], hs, lands):
        out[n] = _pair_join(f"pair_join_{n}", _chip_total(f"chip_total_{n}", h, landed), HALF_AXIS.get(n, 0))
    return out


def _reduce_small(small):
    flat = [small[n].reshape(-1) for n in SMALL]
    sizes = [f.shape[0] for f in flat]
    total = sum(sizes)
    width = -(-total // (SMALL_ROWS * LANE)) * LANE
    packed = jnp.concatenate(flat + [jnp.zeros((SMALL_ROWS * width - total,), F32)]).reshape(SMALL_ROWS, width)
    red = _allreduce_small(packed).reshape(-1)
    out, off = {}, 0
    for n, s in zip(SMALL, sizes):
        out[n] = red[off:off + s]
        off += s
    return out


def kernel(x, attn_norm_g, w_in, b_gate, q_norm_g, w_uq, kv_norm_g, w_ukv, w_o_mla, w_o_dil, w_out, ffn_norm_g, w_up, conv_w, conv_b, w_down, final_norm_g, loss_target, m_attn_norm_g, m_w_in, m_b_gate, m_q_norm_g, m_w_uq, m_kv_norm_g, m_w_ukv, m_w_o_mla, m_w_o_dil, m_w_out, m_ffn_norm_g, m_w_up, m_conv_w, m_conv_b, m_w_down, m_final_norm_g, v_attn_norm_g, v_w_in, v_b_gate, v_q_norm_g, v_w_uq, v_kv_norm_g, v_w_ukv, v_w_o_mla, v_w_o_dil, v_w_out, v_ffn_norm_g, v_w_up, v_conv_w, v_conv_b, v_w_down, v_final_norm_g):
    given = dict(attn_norm_g=attn_norm_g, w_in=w_in, b_gate=b_gate, q_norm_g=q_norm_g, w_uq=w_uq, kv_norm_g=kv_norm_g,
                 w_ukv=w_ukv, w_o_mla=w_o_mla, w_o_dil=w_o_dil, w_out=w_out, ffn_norm_g=ffn_norm_g, w_up=w_up,
                 conv_w=conv_w, conv_b=conv_b, w_down=w_down, final_norm_g=final_norm_g)
    moments_m = dict(attn_norm_g=m_attn_norm_g, w_in=m_w_in, b_gate=m_b_gate, q_norm_g=m_q_norm_g, w_uq=m_w_uq,
                     kv_norm_g=m_kv_norm_g, w_ukv=m_w_ukv, w_o_mla=m_w_o_mla, w_o_dil=m_w_o_dil, w_out=m_w_out,
                     ffn_norm_g=m_ffn_norm_g, w_up=m_w_up, conv_w=m_conv_w, conv_b=m_conv_b, w_down=m_w_down,
                     final_norm_g=m_final_norm_g)
    moments_v = dict(attn_norm_g=v_attn_norm_g, w_in=v_w_in, b_gate=v_b_gate, q_norm_g=v_q_norm_g, w_uq=v_w_uq,
                     kv_norm_g=v_kv_norm_g, w_ukv=v_w_ukv, w_o_mla=v_w_o_mla, w_o_dil=v_w_o_dil, w_out=v_w_out,
                     ffn_norm_g=v_ffn_norm_g, w_up=v_w_up, conv_w=v_conv_w, conv_b=v_conv_b, w_down=v_w_down,
                     final_norm_g=v_final_norm_g)

    shards = {n: given[n][0].astype(BF) for n in BIG}
    shards["conv_w"] = given["conv_w"][0]
    fetch, zero = _gather_start(shards)
    W = {n: given[n] for n in ("b_gate", "q_norm_g", "kv_norm_g", "ffn_norm_g", "conv_b")}
    W["attn_norm_g"] = given["attn_norm_g"] + zero
    W["final_norm_g"] = given["final_norm_g"].reshape(1, -1)

    pending = {}

    def emit(i, grads):
        pending[i], token = _reduce_start(i, grads)
        return token

    loss_part, grad_x, small = _local_step(x[0], loss_target[0], W, fetch, emit)
    loss = lax.psum(loss_part, ("x", "y", "c"))
    grads, delta, new_m, new_v = {}, {}, {}, {}

    def adamw(n, g):
        shp = given[n].shape
        two_d = (-1, shp[-1]) if len(shp) > 1 else (1, -1)
        d, nm, nv = _adamw(f"adamw_{n}", given[n].reshape(two_d), g.reshape(two_d),
                           moments_m[n].reshape(two_d), moments_v[n].reshape(two_d))
        grads[n], delta[n], new_m[n], new_v[n] = g.reshape(shp), d.reshape(shp), nm.reshape(shp), nv.reshape(shp)

    after = grad_x
    for i in range(len(REDUCE_GROUPS)):
        for n, g in _reduce_finish(i, pending[i], after).items():
            adamw(n, g)
        after = delta[REDUCE_GROUPS[i][-1]]
    g_small = _reduce_small(small)
    chip = 2 * lax.axis_index("x") + lax.axis_index("y")
    for n in SMALL:
        if n == "conv_w":
            full = g_small[n].reshape(N_CHIPS, 3, -1)
            adamw(n, lax.dynamic_index_in_dim(full, chip, 0, keepdims=True))
        else:
            adamw(n, g_small[n])

    return (loss, grad_x[None], *[grads[n] for n in WEIGHTS], *[delta[n] for n in WEIGHTS],
            *[new_m[n] for n in WEIGHTS], *[new_v[n] for n in WEIGHTS])
```

```python
import functools
import math

import numpy as np
import jax
import jax.numpy as jnp
from jax import lax
from jax.experimental import pallas as pl
from jax.experimental.pallas import tpu as pltpu

F32 = jnp.float32
BF = jnp.bfloat16
MESH = pl.DeviceIdType.MESH

D_MODEL = 2048
MLA_HEADS = 8
QK_NOPE = 128
QK_ROPE = 64
Q_RANK = 512
KV_RANK = 256
ROPE_THETA = 10000.0
DIL_PATTERNS = ((128, 1), (512, 4), (2048, 16))
DIL_GROUPS = 3
DIL_HPG = 4
DIL_HEADS = 12
HEAD = 128
DIL_BLOCK = 128
ALIBI_MAX_BIAS = 8.0
NORM_EPS = 1e-6
N_CHIPS = 4
ADAM_LR = 0.001
ADAM_B1 = 0.9
ADAM_B2 = 0.999
ADAM_EPS = 1e-08
ADAM_WD = 0.01
ADAM_STEP = 10

LANE = 128
VMEM_LIMIT = 56 * 1024 * 1024
MLA_SCALE = (QK_NOPE + QK_ROPE) ** -0.5
DIL_SCALE = HEAD ** -0.5


def _params(*sem):
    return pltpu.CompilerParams(dimension_semantics=sem, vmem_limit_bytes=VMEM_LIMIT)


def _tile(n, pref):
    t = (pref // LANE) * LANE
    while t >= LANE:
        if n % t == 0:
            return t
        t -= LANE
    return n


NN = (((1,), (0,)), ((), ()))
NT = (((1,), (1,)), ((), ()))
TN = (((0,), (0,)), ((), ()))


def _mm_call(name, a, b, add, *, grid, a_spec, b_spec, add_spec, o_spec, o_shape, o_dtype, acc_shape, dims, nk):
    nax = len(grid)

    def body(*refs):
        if add is None:
            a_ref, b_ref, o_ref = refs[:3]
            c_ref = None
            scr = refs[3:]
        else:
            a_ref, b_ref, c_ref, o_ref = refs[:4]
            scr = refs[4:]
        prod = lax.dot_general(a_ref[...].astype(BF), b_ref[...].astype(BF), dims, preferred_element_type=F32)
        if nk == 1:
            if c_ref is not None:
                prod = prod + c_ref[...]
            o_ref[...] = prod.astype(o_ref.dtype)
        else:
            acc = scr[0]
            k = pl.program_id(nax - 1)

            @pl.when(k == 0)
            def _():
                if c_ref is not None:
                    acc[...] = prod + c_ref[...]
                else:
                    acc[...] = prod

            @pl.when(k > 0)
            def _():
                acc[...] += prod

            @pl.when(k == nk - 1)
            def _():
                o_ref[...] = acc[...].astype(o_ref.dtype)

    ins = [a, b] + ([] if add is None else [add])
    specs = [a_spec, b_spec] + ([] if add is None else [add_spec])
    sem = ("parallel",) * (nax - 1) + ("arbitrary",)
    return pl.pallas_call(
        body, name=name, grid=grid, in_specs=specs, out_specs=o_spec,
        out_shape=jax.ShapeDtypeStruct(o_shape, o_dtype),
        scratch_shapes=[] if nk == 1 else [pltpu.VMEM(acc_shape, F32)],
        compiler_params=_params(*sem),
    )(*ins)


def _mm_nn(name, a, b, *, add=None, o_dtype=F32):
    M, K = a.shape
    sharded = b.ndim == 3
    Ns = b.shape[-1]
    N = Ns * (b.shape[0] if sharded else 1)
    tm, tn, tk = _tile(M, 1024), _tile(Ns, 1024), _tile(K, 2048)
    per = Ns // tn
    nk = K // tk
    if sharded:
        b_spec = pl.BlockSpec((None, tk, tn), lambda i, j, k: (j // per, k, j % per))
    else:
        b_spec = pl.BlockSpec((tk, tn), lambda i, j, k: (k, j))
    return _mm_call(
        name, a, b, add, grid=(M // tm, N // tn, nk),
        a_spec=pl.BlockSpec((tm, tk), lambda i, j, k: (i, k)), b_spec=b_spec,
        add_spec=pl.BlockSpec((tm, tn), lambda i, j, k: (i, j)),
        o_spec=pl.BlockSpec((tm, tn), lambda i, j, k: (i, j)),
        o_shape=(M, N), o_dtype=o_dtype, acc_shape=(tm, tn), dims=NN, nk=nk)


def _mm_nt(name, a, b, *, add=None, o_dtype=F32):
    M, K = a.shape
    sharded = b.ndim == 3
    N, Ks = b.shape[-2], b.shape[-1]
    tm, tn, tk = _tile(M, 1024), _tile(N, 1024), _tile(Ks, 2048)
    per = Ks // tk
    nk = K // tk
    if sharded:
        b_spec = pl.BlockSpec((None, tn, tk), lambda i, j, k: (k // per, j, k % per))
    else:
        b_spec = pl.BlockSpec((tn, tk), lambda i, j, k: (j, k))
    return _mm_call(
        name, a, b, add, grid=(M // tm, N // tn, nk),
        a_spec=pl.BlockSpec((tm, tk), lambda i, j, k: (i, k)), b_spec=b_spec,
        add_spec=pl.BlockSpec((tm, tn), lambda i, j, k: (i, j)),
        o_spec=pl.BlockSpec((tm, tn), lambda i, j, k: (i, j)),
        o_shape=(M, N), o_dtype=o_dtype, acc_shape=(tm, tn), dims=NT, nk=nk)


def _mm_tn(name, a, b, *, shards=1, o_dtype=BF):
    S, M = a.shape
    N = b.shape[1]
    Ns = N // shards
    tm, tn, tk = _tile(M, 1024), _tile(Ns, 1024), _tile(S, 2048)
    per = Ns // tn
    nk = S // tk
    if shards > 1:
        o_spec = pl.BlockSpec((None, tm, tn), lambda i, j, k: (j // per, i, j % per))
        o_shape = (shards, M, Ns)
    else:
        o_spec = pl.BlockSpec((tm, tn), lambda i, j, k: (i, j))
        o_shape = (M, N)
    return _mm_call(
        name, a, b, None, grid=(M // tm, N // tn, nk),
        a_spec=pl.BlockSpec((tk, tm), lambda i, j, k: (k, i)),
        b_spec=pl.BlockSpec((tk, tn), lambda i, j, k: (k, j)),
        add_spec=None, o_spec=o_spec, o_shape=o_shape, o_dtype=o_dtype, acc_shape=(tm, tn), dims=TN, nk=nk)


def _up_fwd(h2, w_up):
    S, D = h2.shape
    G, _, C = w_up.shape
    tm = _tile(S, 512)
    return _mm_call(
        "up_fwd", h2, w_up, None, grid=(G, S // tm, 1),
        a_spec=pl.BlockSpec((tm, D), lambda g, i, k: (i, 0)),
        b_spec=pl.BlockSpec((None, D, C), lambda g, i, k: (g, 0, 0)),
        add_spec=None, o_spec=pl.BlockSpec((None, tm, C), lambda g, i, k: (g, i, 0)),
        o_shape=(G, S, C), o_dtype=BF, acc_shape=None, dims=NN, nk=1)


def _up_dgrad(du_pre, w_up):
    G, S, C = du_pre.shape
    D = w_up.shape[1]
    tm, tn = _tile(S, 1024), _tile(D, 1024)
    return _mm_call(
        "up_dgrad", du_pre, w_up, None, grid=(S // tm, D // tn, G),
        a_spec=pl.BlockSpec((None, tm, C), lambda i, j, g: (g, i, 0)),
        b_spec=pl.BlockSpec((None, tn, C), lambda i, j, g: (g, j, 0)),
        add_spec=None, o_spec=pl.BlockSpec((tm, tn), lambda i, j, g: (i, j)),
        o_shape=(S, D), o_dtype=F32, acc_shape=(tm, tn), dims=NT, nk=G)


def _up_wgrad(h2, du_pre):
    G, S, C = du_pre.shape
    D = h2.shape[1]
    tm, tk = _tile(D, 512), _tile(S, 2048)
    return _mm_call(
        "up_wgrad", h2, du_pre, None, grid=(G, D // tm, S // tk),
        a_spec=pl.BlockSpec((tk, tm), lambda g, i, k: (k, i)),
        b_spec=pl.BlockSpec((None, tk, C), lambda g, i, k: (g, k, 0)),
        add_spec=None, o_spec=pl.BlockSpec((None, tm, C), lambda g, i, k: (g, i, 0)),
        o_shape=(G, D, C), o_dtype=BF, acc_shape=(tm, C), dims=TN, nk=S // tk)


def _down_fwd(act, w_down, x1):
    G, S, C = act.shape
    D = w_down.shape[2]
    tm, tn = _tile(S, 1024), _tile(D, 1024)
    return _mm_call(
        "down_fwd", act, w_down, x1, grid=(S // tm, D // tn, G),
        a_spec=pl.BlockSpec((None, tm, C), lambda i, j, g: (g, i, 0)),
        b_spec=pl.BlockSpec((None, C, tn), lambda i, j, g: (g, 0, j)),
        add_spec=pl.BlockSpec((tm, tn), lambda i, j, g: (i, j)),
        o_spec=pl.BlockSpec((tm, tn), lambda i, j, g: (i, j)),
        o_shape=(S, D), o_dtype=F32, acc_shape=(tm, tn), dims=NN, nk=G)


def _down_dgrad(dx2, w_down):
    S, D = dx2.shape
    G, C, _ = w_down.shape
    tm = _tile(S, 512)
    return _mm_call(
        "down_dgrad", dx2, w_down, None, grid=(G, S // tm, 1),
        a_spec=pl.BlockSpec((tm, D), lambda g, i, k: (i, 0)),
        b_spec=pl.BlockSpec((None, C, D), lambda g, i, k: (g, 0, 0)),
        add_spec=None, o_spec=pl.BlockSpec((None, tm, C), lambda g, i, k: (g, i, 0)),
        o_shape=(G, S, C), o_dtype=BF, acc_shape=None, dims=NT, nk=1)


def _down_wgrad(act, dx2):
    G, S, C = act.shape
    D = dx2.shape[1]
    tn, tk = _tile(D, 512), _tile(S, 1024)
    return _mm_call(
        "down_wgrad", act, dx2, None, grid=(G, D // tn, S // tk),
        a_spec=pl.BlockSpec((None, tk, C), lambda g, j, k: (g, k, 0)),
        b_spec=pl.BlockSpec((tk, tn), lambda g, j, k: (k, j)),
        add_spec=None, o_spec=pl.BlockSpec((None, C, tn), lambda g, j, k: (g, 0, j)),
        o_shape=(G, C, D), o_dtype=BF, acc_shape=(C, tn), dims=TN, nk=S // tk)


def _row(ts, c):
    return pl.BlockSpec((ts, c), lambda i: (i, 0))


def _bcast(r, c):
    return pl.BlockSpec((r, c), lambda i: (0, 0))


def _accumulate(i, ref, val):
    @pl.when(i == 0)
    def _():
        ref[...] = val

    @pl.when(i > 0)
    def _():
        ref[...] += val


def _rstd(xv):
    return lax.rsqrt(jnp.mean(xv * xv, axis=-1, keepdims=True) + NORM_EPS)


def _rmsnorm_fwd(name, x, g):
    S, D = x.shape
    ts = _tile(S, 512)

    def body(x_ref, g_ref, o_ref):
        xv = x_ref[...]
        o_ref[...] = (xv * _rstd(xv) * g_ref[...]).astype(o_ref.dtype)

    return pl.pallas_call(
        body, name=name, grid=(S // ts,), in_specs=[_row(ts, D), _bcast(1, D)], out_specs=_row(ts, D),
        out_shape=jax.ShapeDtypeStruct((S, D), BF), compiler_params=_params("parallel"))(x, g)


def _norm_bwd_rows(dy, xv, g):
    r = _rstd(xv)
    xh = xv * r
    dxh = dy * g
    dx = r * (dxh - xh * jnp.mean(dxh * xh, axis=-1, keepdims=True))
    return dx, jnp.sum(dy * xh, axis=0, keepdims=True)


def _rmsnorm_bwd(name, dy, x, g, res):
    S, D = x.shape
    ts = _tile(S, 512)

    def body(dy_ref, x_ref, g_ref, res_ref, dx_ref, dg_ref):
        dx, dg = _norm_bwd_rows(dy_ref[...], x_ref[...], g_ref[...])
        dx_ref[...] = dx + res_ref[...]
        _accumulate(pl.program_id(0), dg_ref, dg)

    return pl.pallas_call(
        body, name=name, grid=(S // ts,),
        in_specs=[_row(ts, D), _row(ts, D), _bcast(1, D), _row(ts, D)],
        out_specs=[_row(ts, D), _bcast(1, D)],
        out_shape=[jax.ShapeDtypeStruct((S, D), F32), jax.ShapeDtypeStruct((1, D), F32)],
        compiler_params=_params("arbitrary"))(dy, x, g, res)


def _rope_tables(S):
    half = QK_ROPE // 2
    pos = jnp.arange(S, dtype=F32)
    inv_freq = ROPE_THETA ** (-jnp.arange(0, QK_ROPE, 2, dtype=F32) / QK_ROPE)
    ang = pos[:, None] * inv_freq[None, :]
    cos, sin = jnp.cos(ang), jnp.sin(ang)
    z = jnp.zeros((S, half), F32)
    return jnp.concatenate([cos, z, cos, z], axis=1), jnp.concatenate([-sin, z, sin, z], axis=1)


def _rope_lanes(x, cos, sin_signed, inverse):
    if inverse:
        return x * cos + pltpu.roll(x * sin_signed, LANE // 2, 1)
    return x * cos + pltpu.roll(x, LANE // 2, 1) * sin_signed


def _rope(name, x, cos, sin_signed, inverse):
    S, W = x.shape
    ts = _tile(S, 512)

    def body(x_ref, c_ref, s_ref, o_ref):
        c, s = c_ref[...], s_ref[...]
        for h in range(W // LANE):
            sl = slice(h * LANE, (h + 1) * LANE)
            o_ref[:, sl] = _rope_lanes(x_ref[:, sl], c, s, inverse).astype(o_ref.dtype)

    return pl.pallas_call(
        body, name=name, grid=(S // ts,), in_specs=[_row(ts, W), _row(ts, LANE), _row(ts, LANE)],
        out_specs=_row(ts, W), out_shape=jax.ShapeDtypeStruct((S, W), BF),
        compiler_params=_params("parallel"))(x, cos, sin_signed)


LAT_W = 1024
_CQ = slice(0, Q_RANK)
_CKV = slice(Q_RANK, Q_RANK + KV_RANK)
_KPE = slice(Q_RANK + KV_RANK, Q_RANK + KV_RANK + LANE)


def _mla_prep(lat, qg, kvg, cos, sin_signed):
    S = lat.shape[0]
    ts = _tile(S, 512)

    def body(lat_ref, qg_ref, kvg_ref, c_ref, s_ref, qn_ref, kvn_ref, kpe_ref):
        cq = lat_ref[:, _CQ]
        qn_ref[...] = (cq * _rstd(cq) * qg_ref[...]).astype(BF)
        ckv = lat_ref[:, _CKV]
        kvn_ref[...] = (ckv * _rstd(ckv) * kvg_ref[...]).astype(BF)
        kpe_ref[...] = _rope_lanes(lat_ref[:, _KPE], c_ref[...], s_ref[...], False).astype(BF)

    return pl.pallas_call(
        body, name="mla_prep", grid=(S // ts,),
        in_specs=[_row(ts, LAT_W), _bcast(1, Q_RANK), _bcast(1, KV_RANK), _row(ts, LANE), _row(ts, LANE)],
        out_specs=[_row(ts, Q_RANK), _row(ts, KV_RANK), _row(ts, LANE)],
        out_shape=[jax.ShapeDtypeStruct((S, Q_RANK), BF), jax.ShapeDtypeStruct((S, KV_RANK), BF),
                   jax.ShapeDtypeStruct((S, LANE), BF)],
        compiler_params=_params("parallel"))(lat, qg, kvg, cos, sin_signed)


def _mla_prep_bwd(lat, qg, kvg, cos, sin_signed, d_qn, d_kvn, d_kpe):
    S = lat.shape[0]
    ts = _tile(S, 512)

    def body(lat_ref, qg_ref, kvg_ref, c_ref, s_ref, dqn_ref, dkvn_ref, dkpe_ref, dlat_ref, dqg_ref, dkvg_ref):
        i = pl.program_id(0)
        dcq, dqg = _norm_bwd_rows(dqn_ref[...], lat_ref[:, _CQ], qg_ref[...])
        dckv, dkvg = _norm_bwd_rows(dkvn_ref[...], lat_ref[:, _CKV], kvg_ref[...])
        dlat_ref[:, _CQ] = dcq.astype(BF)
        dlat_ref[:, _CKV] = dckv.astype(BF)
        dkpe = dkpe_ref[0]
        for g in range(1, d_kpe.shape[0]):
            dkpe = dkpe + dkpe_ref[g]
        dlat_ref[:, _KPE] = _rope_lanes(dkpe, c_ref[...], s_ref[...], True).astype(BF)
        dlat_ref[:, _KPE.stop:] = jnp.zeros((ts, LAT_W - _KPE.stop), BF)
        _accumulate(i, dqg_ref, dqg)
        _accumulate(i, dkvg_ref, dkvg)

    return pl.pallas_call(
        body, name="mla_prep_bwd", grid=(S // ts,),
        in_specs=[_row(ts, LAT_W), _bcast(1, Q_RANK), _bcast(1, KV_RANK), _row(ts, LANE), _row(ts, LANE),
                  _row(ts, Q_RANK), _row(ts, KV_RANK), pl.BlockSpec((d_kpe.shape[0], ts, LANE), lambda i: (0, i, 0))],
        out_specs=[_row(ts, LAT_W), _bcast(1, Q_RANK), _bcast(1, KV_RANK)],
        out_shape=[jax.ShapeDtypeStruct((S, LAT_W), BF), jax.ShapeDtypeStruct((1, Q_RANK), F32),
                   jax.ShapeDtypeStruct((1, KV_RANK), F32)],
        compiler_params=_params("arbitrary"))(lat, qg, kvg, cos, sin_signed, d_qn, d_kvn, d_kpe)


def _sigmoid(z):
    return 1.0 / (1.0 + jnp.exp(-z))


def _merge_fwd(gpre, b_gate, o_a, o_b):
    S, D = o_a.shape
    ts = _tile(S, 256)

    def body(g_ref, b_ref, oa_ref, ob_ref, m_ref):
        ga = _sigmoid(g_ref[:, :D] + b_ref[:, :D])
        gb = _sigmoid(g_ref[:, D:] + b_ref[:, D:])
        m_ref[...] = (ga * oa_ref[...] + gb * ob_ref[...]).astype(BF)

    return pl.pallas_call(
        body, name="merge_fwd", grid=(S // ts,),
        in_specs=[_row(ts, 2 * D), _bcast(1, 2 * D), _row(ts, D), _row(ts, D)], out_specs=_row(ts, D),
        out_shape=jax.ShapeDtypeStruct((S, D), BF), compiler_params=_params("parallel"))(gpre, b_gate, o_a, o_b)


def _merge_bwd(d_merge, gpre, b_gate, o_a, o_b):
    S, D = o_a.shape
    ts = _tile(S, 256)

    def body(dm_ref, g_ref, b_ref, oa_ref, ob_ref, doa_ref, dob_ref, dg_ref, db_ref):
        dm = dm_ref[...]
        ga = _sigmoid(g_ref[:, :D] + b_ref[:, :D])
        gb = _sigmoid(g_ref[:, D:] + b_ref[:, D:])
        doa_ref[...] = (dm * ga).astype(BF)
        dob_ref[...] = (dm * gb).astype(BF)
        dga = dm * oa_ref[...] * ga * (1.0 - ga)
        dgb = dm * ob_ref[...] * gb * (1.0 - gb)
        dg_ref[:, :D] = dga.astype(BF)
        dg_ref[:, D:] = dgb.astype(BF)
        i = pl.program_id(0)
        part = jnp.concatenate([jnp.sum(dga, axis=0, keepdims=True), jnp.sum(dgb, axis=0, keepdims=True)], axis=1)
        _accumulate(i, db_ref, part)

    return pl.pallas_call(
        body, name="merge_bwd", grid=(S // ts,),
        in_specs=[_row(ts, D), _row(ts, 2 * D), _bcast(1, 2 * D), _row(ts, D), _row(ts, D)],
        out_specs=[_row(ts, D), _row(ts, D), _row(ts, 2 * D), _bcast(1, 2 * D)],
        out_shape=[jax.ShapeDtypeStruct((S, D), BF), jax.ShapeDtypeStruct((S, D), BF),
                   jax.ShapeDtypeStruct((S, 2 * D), BF), jax.ShapeDtypeStruct((1, 2 * D), F32)],
        compiler_params=_params("arbitrary"))(d_merge, gpre, b_gate, o_a, o_b)


def _final_loss(x2, tgt, gf):
    S, D = x2.shape
    ts = _tile(S, 512)

    def body(x_ref, t_ref, g_ref, dx_ref, dg_ref, loss_ref):
        i = pl.program_id(0)
        xv = x_ref[...]
        g = g_ref[...]
        y = xv * _rstd(xv) * g
        err = y - t_ref[...]
        dx, dg = _norm_bwd_rows(err * (1.0 / D), xv, g)
        dx_ref[...] = dx
        _accumulate(i, dg_ref, dg)
        part = 0.5 * jnp.sum(jnp.mean(err * err, axis=-1, keepdims=True), axis=0, keepdims=True)
        _accumulate(i, loss_ref, jnp.broadcast_to(part, (8, LANE)))

    return pl.pallas_call(
        body, name="final_loss", grid=(S // ts,),
        in_specs=[_row(ts, D), _row(ts, D), _bcast(1, D)],
        out_specs=[_row(ts, D), _bcast(1, D), _bcast(8, LANE)],
        out_shape=[jax.ShapeDtypeStruct((S, D), F32), jax.ShapeDtypeStruct((1, D), F32),
                   jax.ShapeDtypeStruct((8, LANE), F32)],
        compiler_params=_params("arbitrary"))(x2, tgt, gf)


HALO = 16


def _shift_down(cur, prev, k, rows):
    out = pltpu.roll(cur, k, 0)
    for j in range(k):
        out = jnp.where(rows == j, prev[HALO - k + j:HALO - k + j + 1, :], out)
    return out


def _shift_up(cur, nxt, k, rows, ts):
    out = pltpu.roll(cur, ts - k, 0)
    for j in range(k):
        out = jnp.where(rows == ts - k + j, nxt[j:j + 1, :], out)
    return out


def _conv_rows(cur, prev, w, b, rows):
    return b + w[0:1, :] * _shift_down(cur, prev, 2, rows) + w[1:2, :] * _shift_down(cur, prev, 1, rows) + w[2:3, :] * cur


def _conv_specs(ts, C, shard_of):
    nh = ts // HALO
    cur = pl.BlockSpec((None, ts, C), lambda g, i: (shard_of(g), i, 0))
    prev = pl.BlockSpec((None, HALO, C), lambda g, i: (shard_of(g), jnp.maximum(i * nh - 1, 0), 0))
    return cur, prev


def _ffn_act(u_pre, conv_w, conv_b):
    G4, S, C = u_pre.shape
    G = G4 // 2
    ts = _tile(S, 256)

    def body(up_ref, upp_ref, gt_ref, gtp_ref, wu_ref, wg_ref, bu_ref, bg_ref, act_ref):
        first = pl.program_id(1) == 0
        rows = lax.broadcasted_iota(jnp.int32, (ts, C), 0)
        pu = jnp.where(first, 0.0, upp_ref[...].astype(F32))
        pg = jnp.where(first, 0.0, gtp_ref[...].astype(F32))
        up = _conv_rows(up_ref[...].astype(F32), pu, wu_ref[...], bu_ref[...], rows)
        gate = _conv_rows(gt_ref[...].astype(F32), pg, wg_ref[...], bg_ref[...], rows)
        act_ref[...] = (gate * _sigmoid(gate) * up).astype(BF)

    cur_u, prev_u = _conv_specs(ts, C, lambda g: g)
    cur_g, prev_g = _conv_specs(ts, C, lambda g: g + G)
    w_u = pl.BlockSpec((None, 3, C), lambda g, i: (g, 0, 0))
    w_g = pl.BlockSpec((None, 3, C), lambda g, i: (g + G, 0, 0))
    b_u = pl.BlockSpec((None, 1, C), lambda g, i: (g, 0, 0))
    b_g = pl.BlockSpec((None, 1, C), lambda g, i: (g + G, 0, 0))
    return pl.pallas_call(
        body, name="ffn_act", grid=(G, S // ts),
        in_specs=[cur_u, prev_u, cur_g, prev_g, w_u, w_g, b_u, b_g],
        out_specs=pl.BlockSpec((None, ts, C), lambda g, i: (g, i, 0)),
        out_shape=jax.ShapeDtypeStruct((G, S, C), BF),
        compiler_params=_params("parallel", "parallel"))(u_pre, u_pre, u_pre, u_pre, conv_w, conv_w, conv_b, conv_b)


def _ffn_act_bwd(u_pre, conv_w, conv_b, d_act):
    G4, S, C = u_pre.shape
    G = G4 // 2
    ts = _tile(S, 256)

    def body(up_ref, upp_ref, gt_ref, gtp_ref, wu_ref, wg_ref, bu_ref, bg_ref, da_ref, du_ref):
        first = pl.program_id(1) == 0
        rows = lax.broadcasted_iota(jnp.int32, (ts, C), 0)
        pu = jnp.where(first, 0.0, upp_ref[...].astype(F32))
        pg = jnp.where(first, 0.0, gtp_ref[...].astype(F32))
        up = _conv_rows(up_ref[...].astype(F32), pu, wu_ref[...], bu_ref[...], rows)
        gate = _conv_rows(gt_ref[...].astype(F32), pg, wg_ref[...], bg_ref[...], rows)
        sg = _sigmoid(gate)
        da = da_ref[...].astype(F32)
        du_ref[0] = (da * (gate * sg)).astype(BF)
        du_ref[1] = (da * up * (sg * (1.0 + gate * (1.0 - sg)))).astype(BF)

    cur_u, prev_u = _conv_specs(ts, C, lambda g: g)
    cur_g, prev_g = _conv_specs(ts, C, lambda g: g + G)
    w_u = pl.BlockSpec((None, 3, C), lambda g, i: (g, 0, 0))
    w_g = pl.BlockSpec((None, 3, C), lambda g, i: (g + G, 0, 0))
    b_u = pl.BlockSpec((None, 1, C), lambda g, i: (g, 0, 0))
    b_g = pl.BlockSpec((None, 1, C), lambda g, i: (g + G, 0, 0))
    blk = pl.BlockSpec((None, ts, C), lambda g, i: (g, i, 0))
    du = pl.pallas_call(
        body, name="ffn_act_bwd", grid=(G, S // ts),
        in_specs=[cur_u, prev_u, cur_g, prev_g, w_u, w_g, b_u, b_g, blk],
        out_specs=pl.BlockSpec((2, None, ts, C), lambda g, i: (0, g, i, 0)),
        out_shape=jax.ShapeDtypeStruct((2, G, S, C), BF),
        compiler_params=_params("parallel", "parallel"))(
            u_pre, u_pre, u_pre, u_pre, conv_w, conv_w, conv_b, conv_b, d_act)
    return du.reshape(G4, S, C)


def _conv_bwd(du, u_pre, conv_w):
    G4, S, C = du.shape
    ts = _tile(S, 256)
    nh = ts // HALO
    last_halo = S // HALO - 1

    def body(du_ref, dun_ref, u_ref, up_ref, w_ref, dpre_ref, dw_ref, db_ref):
        i = pl.program_id(1)
        rows = lax.broadcasted_iota(jnp.int32, (ts, C), 0)
        du_c = du_ref[...].astype(F32)
        nxt = jnp.where(i == pl.num_programs(1) - 1, 0.0, dun_ref[...].astype(F32))
        prev = jnp.where(i == 0, 0.0, up_ref[...].astype(F32))
        w = w_ref[...]
        dpre = w[2:3, :] * du_c + w[1:2, :] * _shift_up(du_c, nxt, 1, rows, ts) + w[0:1, :] * _shift_up(du_c, nxt, 2, rows, ts)
        dpre_ref[...] = dpre.astype(BF)
        u_c = u_ref[...].astype(F32)
        dw = jnp.concatenate([
            jnp.sum(du_c * _shift_down(u_c, prev, 2, rows), axis=0, keepdims=True),
            jnp.sum(du_c * _shift_down(u_c, prev, 1, rows), axis=0, keepdims=True),
            jnp.sum(du_c * u_c, axis=0, keepdims=True)], axis=0)
        _accumulate(i, dw_ref, dw)
        _accumulate(i, db_ref, jnp.sum(du_c, axis=0, keepdims=True))

    cur = pl.BlockSpec((None, ts, C), lambda g, i: (g, i, 0))
    nxt = pl.BlockSpec((None, HALO, C), lambda g, i: (g, jnp.minimum((i + 1) * nh, last_halo), 0))
    prev = pl.BlockSpec((None, HALO, C), lambda g, i: (g, jnp.maximum(i * nh - 1, 0), 0))
    return pl.pallas_call(
        body, name="conv_bwd", grid=(G4, S // ts),
        in_specs=[cur, nxt, cur, prev, pl.BlockSpec((None, 3, C), lambda g, i: (g, 0, 0))],
        out_specs=[cur, pl.BlockSpec((None, 3, C), lambda g, i: (g, 0, 0)), pl.BlockSpec((None, 1, C), lambda g, i: (g, 0, 0))],
        out_shape=[jax.ShapeDtypeStruct((G4, S, C), BF), jax.ShapeDtypeStruct((G4, 3, C), F32),
                   jax.ShapeDtypeStruct((G4, 1, C), F32)],
        compiler_params=_params("parallel", "arbitrary"))(du, du, u_pre, u_pre, conv_w)


MLA_T = 512
MLA_HB = 4
MLA_HW = MLA_HB * HEAD


def _mla_pairs(n, by_row):
    if by_row:
        pairs = [(i, j) for i in range(n) for j in range(i + 1)]
    else:
        pairs = [(i, j) for j in range(n) for i in range(j, n)]
    return jnp.asarray([p[0] for p in pairs], jnp.int32), jnp.asarray([p[1] for p in pairs], jnp.int32)


def _mla_specs(S):
    q = pl.BlockSpec((MLA_T, MLA_HW), lambda g, t, it, jt: (it[t], g))
    k = pl.BlockSpec((MLA_T, MLA_HW), lambda g, t, it, jt: (jt[t], g))
    kpe = pl.BlockSpec((MLA_T, HEAD), lambda g, t, it, jt: (jt[t], 0))
    lse = pl.BlockSpec((MLA_HB, MLA_T, LANE), lambda g, t, it, jt: (g, it[t], 0))
    return q, k, kpe, lse


def _mla_head(ref, hh):
    return ref[:, hh * HEAD:(hh + 1) * HEAD]


def _mla_scores(qn_ref, qp_ref, kn_ref, kpe, hh, ok):
    q = jnp.concatenate([_mla_head(qn_ref, hh), _mla_head(qp_ref, hh)], axis=1)
    k = jnp.concatenate([_mla_head(kn_ref, hh), kpe], axis=1)
    s = lax.dot_general(q, k, NT, preferred_element_type=F32) * MLA_SCALE
    return q, k, jnp.where(ok, s, -jnp.inf)


def _mla_causal(i, j):
    row = i * MLA_T + lax.broadcasted_iota(jnp.int32, (MLA_T, MLA_T), 0)
    col = j * MLA_T + lax.broadcasted_iota(jnp.int32, (MLA_T, MLA_T), 1)
    return col <= row


def _mla_fwd(qn, qp, kn, kpe, v):
    S = qn.shape[0]
    it, jt = _mla_pairs(S // MLA_T, True)

    def body(it_ref, jt_ref, qn_ref, qp_ref, kn_ref, kpe_ref, v_ref, o_ref, lse_ref, m_scr, acc_scr):
        t = pl.program_id(1)
        i, j = it_ref[t], jt_ref[t]

        @pl.when(j == 0)
        def _():
            m_scr[...] = jnp.full(m_scr.shape, -jnp.inf, F32)
            acc_scr[...] = jnp.zeros(acc_scr.shape, F32)

        ok = _mla_causal(i, j)
        kpe_v = kpe_ref[...]
        ones = jnp.ones((MLA_T, HEAD), BF)
        state = [(m_scr[hh], acc_scr[hh]) for hh in range(MLA_HB)]
        new = []
        for hh in range(MLA_HB):
            m_prev, acc = state[hh]
            _, _, s = _mla_scores(qn_ref, qp_ref, kn_ref, kpe_v, hh, ok)
            m_new = jnp.maximum(m_prev, jnp.max(s, axis=1, keepdims=True))
            p = jnp.exp(s - m_new).astype(BF)
            v1 = jnp.concatenate([_mla_head(v_ref, hh), ones], axis=1)
            new.append((m_new, jnp.exp(m_prev - m_new) * acc + lax.dot_general(p, v1, NN, preferred_element_type=F32)))
        for hh in range(MLA_HB):
            m_scr[hh], acc_scr[hh] = new[hh]

        @pl.when(j == i)
        def _():
            for hh in range(MLA_HB):
                l = acc_scr[hh, :, HEAD:]
                o_ref[:, hh * HEAD:(hh + 1) * HEAD] = (acc_scr[hh, :, :HEAD] / l).astype(BF)
                lse_ref[hh] = m_scr[hh] + jnp.log(l)

    qspec, kspec, kpespec, lsespec = _mla_specs(S)
    grid_spec = pltpu.PrefetchScalarGridSpec(
        num_scalar_prefetch=2, grid=(MLA_HEADS // MLA_HB, it.shape[0]),
        in_specs=[qspec, qspec, kspec, kpespec, kspec], out_specs=[qspec, lsespec],
        scratch_shapes=[pltpu.VMEM((MLA_HB, MLA_T, 1), F32), pltpu.VMEM((MLA_HB, MLA_T, 2 * HEAD), F32)])
    return pl.pallas_call(
        body, name="mla_fwd", grid_spec=grid_spec,
        out_shape=[jax.ShapeDtypeStruct((S, MLA_HEADS * HEAD), BF), jax.ShapeDtypeStruct((MLA_HEADS, S, LANE), F32)],
        compiler_params=_params("parallel", "arbitrary"))(it, jt, qn, qp, kn, kpe, v)


def _mla_p_ds(qn_ref, qp_ref, kn_ref, kpe, v_ref, do_ref, o_ref, lse_ref, hh, ok):
    q, k, s = _mla_scores(qn_ref, qp_ref, kn_ref, kpe, hh, ok)
    p = jnp.exp(s - lse_ref[hh][:, 0:1])
    do = _mla_head(do_ref, hh)
    delta = jnp.sum(do.astype(F32) * _mla_head(o_ref, hh).astype(F32), axis=1, keepdims=True)
    dp = lax.dot_general(do, _mla_head(v_ref, hh), NT, preferred_element_type=F32)
    ds = p * (dp - delta) * MLA_SCALE
    return q, k, p, ds, do


def _mla_bwd_dq(qn, qp, kn, kpe, v, do, o, lse):
    S = qn.shape[0]
    it, jt = _mla_pairs(S // MLA_T, True)

    def body(it_ref, jt_ref, qn_ref, qp_ref, kn_ref, kpe_ref, v_ref, do_ref, o_ref, lse_ref, dqn_ref, dqp_ref, acc):
        t = pl.program_id(1)
        i, j = it_ref[t], jt_ref[t]

        @pl.when(j == 0)
        def _():
            acc[...] = jnp.zeros(acc.shape, F32)

        ok = _mla_causal(i, j)
        kpe_v = kpe_ref[...]
        old = [acc[hh] for hh in range(MLA_HB)]
        for hh in range(MLA_HB):
            _, k, _, ds, _ = _mla_p_ds(qn_ref, qp_ref, kn_ref, kpe_v, v_ref, do_ref, o_ref, lse_ref, hh, ok)
            old[hh] = old[hh] + lax.dot_general(ds.astype(BF), k, NN, preferred_element_type=F32)
        for hh in range(MLA_HB):
            acc[hh] = old[hh]

        @pl.when(j == i)
        def _():
            for hh in range(MLA_HB):
                dqn_ref[:, hh * HEAD:(hh + 1) * HEAD] = acc[hh, :, :HEAD].astype(BF)
                dqp_ref[:, hh * HEAD:(hh + 1) * HEAD] = acc[hh, :, HEAD:]

    qspec, kspec, kpespec, lsespec = _mla_specs(S)
    grid_spec = pltpu.PrefetchScalarGridSpec(
        num_scalar_prefetch=2, grid=(MLA_HEADS // MLA_HB, it.shape[0]),
        in_specs=[qspec, qspec, kspec, kpespec, kspec, qspec, qspec, lsespec], out_specs=[qspec, qspec],
        scratch_shapes=[pltpu.VMEM((MLA_HB, MLA_T, 2 * HEAD), F32)])
    return pl.pallas_call(
        body, name="mla_bwd_dq", grid_spec=grid_spec,
        out_shape=[jax.ShapeDtypeStruct((S, MLA_HEADS * HEAD), BF), jax.ShapeDtypeStruct((S, MLA_HEADS * HEAD), F32)],
        compiler_params=_params("parallel", "arbitrary"))(it, jt, qn, qp, kn, kpe, v, do, o, lse)


def _mla_bwd_dkv(qn, qp, kn, kpe, v, do, o, lse):
    S = qn.shape[0]
    nq = S // MLA_T
    it, jt = _mla_pairs(nq, False)

    def body(it_ref, jt_ref, qn_ref, qp_ref, kn_ref, kpe_ref, v_ref, do_ref, o_ref, lse_ref, dkn_ref, dv_ref, dkpe_ref,
             dk_acc, dv_acc):
        t = pl.program_id(1)
        i, j = it_ref[t], jt_ref[t]

        @pl.when(i == j)
        def _():
            dk_acc[...] = jnp.zeros(dk_acc.shape, F32)
            dv_acc[...] = jnp.zeros(dv_acc.shape, F32)

        ok = _mla_causal(i, j)
        kpe_v = kpe_ref[...]
        for hh in range(MLA_HB):
            q, _, p, ds, do = _mla_p_ds(qn_ref, qp_ref, kn_ref, kpe_v, v_ref, do_ref, o_ref, lse_ref, hh, ok)
            dv_acc[hh] += lax.dot_general(p.astype(BF), do, TN, preferred_element_type=F32)
            dk_acc[hh] += lax.dot_general(ds.astype(BF), q, TN, preferred_element_type=F32)

        @pl.when(i == nq - 1)
        def _():
            dkpe = dk_acc[0, :, HEAD:]
            for hh in range(MLA_HB):
                dkn_ref[:, hh * HEAD:(hh + 1) * HEAD] = dk_acc[hh, :, :HEAD].astype(BF)
                dv_ref[:, hh * HEAD:(hh + 1) * HEAD] = dv_acc[hh].astype(BF)
                if hh:
                    dkpe = dkpe + dk_acc[hh, :, HEAD:]
            dkpe_ref[...] = dkpe

    qspec, kspec, kpespec, lsespec = _mla_specs(S)
    dkpespec = pl.BlockSpec((None, MLA_T, HEAD), lambda g, t, it, jt: (g, jt[t], 0))
    grid_spec = pltpu.PrefetchScalarGridSpec(
        num_scalar_prefetch=2, grid=(MLA_HEADS // MLA_HB, it.shape[0]),
        in_specs=[qspec, qspec, kspec, kpespec, kspec, qspec, qspec, lsespec], out_specs=[kspec, kspec, dkpespec],
        scratch_shapes=[pltpu.VMEM((MLA_HB, MLA_T, 2 * HEAD), F32), pltpu.VMEM((MLA_HB, MLA_T, HEAD), F32)])
    return pl.pallas_call(
        body, name="mla_bwd_dkv", grid_spec=grid_spec,
        out_shape=[jax.ShapeDtypeStruct((S, MLA_HEADS * HEAD), BF), jax.ShapeDtypeStruct((S, MLA_HEADS * HEAD), BF),
                   jax.ShapeDtypeStruct((MLA_HEADS // MLA_HB, S, HEAD), F32)],
        compiler_params=_params("parallel", "arbitrary"))(it, jt, qn, qp, kn, kpe, v, do, o, lse)


DIL_W = 3 * DIL_HPG * HEAD
DIL_O = DIL_HPG * HEAD


def _dil_slopes(g):
    return [2.0 ** (-ALIBI_MAX_BIAS * (g * DIL_HPG + hh + 1) / DIL_HEADS) for hh in range(DIL_HPG)]


def _dil_bias(dil):
    p = lax.broadcasted_iota(jnp.int32, (DIL_BLOCK, DIL_BLOCK), 0)
    kk = lax.broadcasted_iota(jnp.int32, (DIL_BLOCK, DIL_BLOCK), 1)
    jc = p - kk
    dist_c = (dil * jc).astype(F32)
    dist_p = (dil * (jc + DIL_BLOCK)).astype(F32)
    return jc >= 0, jc <= 0, dist_c, dist_p


def _dil_head(blk, hh):
    q = blk[:, hh * HEAD:(hh + 1) * HEAD]
    k = blk[:, DIL_O + hh * HEAD:DIL_O + (hh + 1) * HEAD]
    v = blk[:, 2 * DIL_O + hh * HEAD:2 * DIL_O + (hh + 1) * HEAD]
    return q, k, v


def _dil_s(q, k, slope, dist, ok):
    s = lax.dot_general(q, k, NT, preferred_element_type=F32) * DIL_SCALE - slope * dist
    return jnp.where(ok, s, -jnp.inf)


def _dil_view(a, dil):
    S, W = a.shape
    return a.reshape(S // dil, dil * W)


def _dil_fwd(qkv, g):
    _, dil = DIL_PATTERNS[g]
    S = qkv.shape[0]
    L = S // dil
    nb = L // DIL_BLOCK
    slopes = _dil_slopes(g)

    def body(cur_ref, prev_ref, o_ref, lse_ref):
        n = pl.program_id(1)
        ok_c, ok_p, dist_c, dist_p = _dil_bias(dil)
        ok_p = ok_p & (n > 0)
        cur, prev = cur_ref[...], prev_ref[...]
        for hh in range(DIL_HPG):
            q, kc, vc = _dil_head(cur, hh)
            _, kp, vp = _dil_head(prev, hh)
            sc = _dil_s(q, kc, slopes[hh], dist_c, ok_c)
            sp = _dil_s(q, kp, slopes[hh], dist_p, ok_p)
            m = jnp.maximum(jnp.max(sc, axis=1, keepdims=True), jnp.max(sp, axis=1, keepdims=True))
            pc, pp = jnp.exp(sc - m), jnp.exp(sp - m)
            l = jnp.sum(pc, axis=1, keepdims=True) + jnp.sum(pp, axis=1, keepdims=True)
            o = (lax.dot_general(pc.astype(BF), vc, NN, preferred_element_type=F32)
                 + lax.dot_general(pp.astype(BF), vp, NN, preferred_element_type=F32)) / l
            sl = slice(hh * HEAD, (hh + 1) * HEAD)
            o_ref[:, sl] = o
            lse_ref[:, sl] = jnp.broadcast_to(m + jnp.log(l), (DIL_BLOCK, HEAD))

    ospec = pl.BlockSpec((DIL_BLOCK, DIL_O), lambda r, n: (n, r))
    o, lse = pl.pallas_call(
        body, name=f"dil_fwd{g}", grid=(dil, nb),
        in_specs=[pl.BlockSpec((DIL_BLOCK, DIL_W), lambda r, n: (n, r)),
                  pl.BlockSpec((DIL_BLOCK, DIL_W), lambda r, n: (jnp.maximum(n - 1, 0), r))],
        out_specs=[ospec, ospec],
        out_shape=[jax.ShapeDtypeStruct((L, dil * DIL_O), F32), jax.ShapeDtypeStruct((L, dil * DIL_O), F32)],
        compiler_params=_params("parallel", "parallel"))(_dil_view(qkv, dil), _dil_view(qkv, dil))
    return o.reshape(S, DIL_O), lse.reshape(S, DIL_O)


def _dil_combine(os_, lses):
    S = os_[0].shape[0]
    ts = _tile(S, 512)

    def body(o0, o1, o2, l0, l1, l2, out_ref, lse_ref):
        a, b, c = l0[...], l1[...], l2[...]
        m = jnp.maximum(jnp.maximum(a, b), c)
        ea, eb, ec = jnp.exp(a - m), jnp.exp(b - m), jnp.exp(c - m)
        tot = ea + eb + ec
        out_ref[...] = ((ea * o0[...] + eb * o1[...] + ec * o2[...]) / tot).astype(BF)
        lse_ref[...] = m + jnp.log(tot)

    return pl.pallas_call(
        body, name="dil_combine", grid=(S // ts,), in_specs=[_row(ts, DIL_O)] * 6,
        out_specs=[_row(ts, DIL_O), _row(ts, DIL_O)],
        out_shape=[jax.ShapeDtypeStruct((S, DIL_O), BF), jax.ShapeDtypeStruct((S, DIL_O), F32)],
        compiler_params=_params("parallel"))(*os_, *lses)


def _dil_delta(do, out):
    S = do.shape[0]
    ts = _tile(S, 512)

    def body(do_ref, o_ref, d_ref):
        for hh in range(DIL_HPG):
            sl = slice(hh * HEAD, (hh + 1) * HEAD)
            d = jnp.sum(do_ref[:, sl].astype(F32) * o_ref[:, sl].astype(F32), axis=1, keepdims=True)
            d_ref[:, sl] = jnp.broadcast_to(d, (ts, HEAD))

    return pl.pallas_call(
        body, name="dil_delta", grid=(S // ts,), in_specs=[_row(ts, DIL_O)] * 2, out_specs=_row(ts, DIL_O),
        out_shape=jax.ShapeDtypeStruct((S, DIL_O), F32), compiler_params=_params("parallel"))(do, out)


def _dil_bwd(qkv, do, lse, delta, g):
    _, dil = DIL_PATTERNS[g]
    S = qkv.shape[0]
    L = S // dil
    nb = L // DIL_BLOCK
    slopes = _dil_slopes(g)

    def pair(q, k, v, do_h, lse_h, delta_h, slope, dist, ok):
        s = _dil_s(q, k, slope, dist, ok)
        p = jnp.exp(s - lse_h)
        dp = lax.dot_general(do_h, v, NT, preferred_element_type=F32)
        ds = (p * (dp - delta_h) * DIL_SCALE).astype(BF)
        return p.astype(BF), ds

    def body(cur_ref, prev_ref, next_ref, doc_ref, don_ref, lsec_ref, lsen_ref, dlc_ref, dln_ref, out_ref):
        n = pl.program_id(1)
        ok_c, ok_p0, dist_c, dist_p = _dil_bias(dil)
        ok_a = ok_p0 & (n > 0)
        ok_n = ok_p0 & (n < nb - 1)
        cur, prev, nxt = cur_ref[...], prev_ref[...], next_ref[...]
        for hh in range(DIL_HPG):
            sl = slice(hh * HEAD, (hh + 1) * HEAD)
            q, kc, vc = _dil_head(cur, hh)
            _, kp, vp = _dil_head(prev, hh)
            qn, _, _ = _dil_head(nxt, hh)
            do_c, do_n = doc_ref[:, sl], don_ref[:, sl]
            lse_c, lse_n = lsec_ref[:, sl][:, 0:1], lsen_ref[:, sl][:, 0:1]
            dl_c, dl_n = dlc_ref[:, sl][:, 0:1], dln_ref[:, sl][:, 0:1]
            _, ds_a = pair(q, kp, vp, do_c, lse_c, dl_c, slopes[hh], dist_p, ok_a)
            p_b, ds_b = pair(q, kc, vc, do_c, lse_c, dl_c, slopes[hh], dist_c, ok_c)
            p_n, ds_n = pair(qn, kc, vc, do_n, lse_n, dl_n, slopes[hh], dist_p, ok_n)
            dq = (lax.dot_general(ds_a, kp, NN, preferred_element_type=F32)
                  + lax.dot_general(ds_b, kc, NN, preferred_element_type=F32))
            dk = (lax.dot_general(ds_b, q, TN, preferred_element_type=F32)
                  + lax.dot_general(ds_n, qn, TN, preferred_element_type=F32))
            dv = (lax.dot_general(p_b, do_c, TN, preferred_element_type=F32)
                  + lax.dot_general(p_n, do_n, TN, preferred_element_type=F32))
            out_ref[:, sl] = dq.astype(BF)
            out_ref[:, DIL_O + hh * HEAD:DIL_O + (hh + 1) * HEAD] = dk.astype(BF)
            out_ref[:, 2 * DIL_O + hh * HEAD:2 * DIL_O + (hh + 1) * HEAD] = dv.astype(BF)

    cur_w = pl.BlockSpec((DIL_BLOCK, DIL_W), lambda r, n: (n, r))
    prev_w = pl.BlockSpec((DIL_BLOCK, DIL_W), lambda r, n: (jnp.maximum(n - 1, 0), r))
    next_w = pl.BlockSpec((DIL_BLOCK, DIL_W), lambda r, n: (jnp.minimum(n + 1, nb - 1), r))
    cur_o = pl.BlockSpec((DIL_BLOCK, DIL_O), lambda r, n: (n, r))
    next_o = pl.BlockSpec((DIL_BLOCK, DIL_O), lambda r, n: (jnp.minimum(n + 1, nb - 1), r))
    qv, dov, lsev, dlv = _dil_view(qkv, dil), _dil_view(do, dil), _dil_view(lse, dil), _dil_view(delta, dil)
    out = pl.pallas_call(
        body, name=f"dil_bwd{g}", grid=(dil, nb),
        in_specs=[cur_w, prev_w, next_w, cur_o, next_o, cur_o, next_o, cur_o, next_o],
        out_specs=cur_w, out_shape=jax.ShapeDtypeStruct((L, dil * DIL_W), BF),
        compiler_params=_params("parallel", "parallel"))(qv, qv, qv, dov, dov, lsev, lsev, dlv, dlv)
    return out.reshape(S, DIL_W)


def _adamw(name, w, g, m, v):
    R, C = w.shape
    tr = R if R <= 512 else _tile_rows(R, 256)

    def body(w_ref, g_ref, m_ref, v_ref, d_ref, nm_ref, nv_ref):
        gv = g_ref[...]
        nm = ADAM_B1 * m_ref[...] + (1.0 - ADAM_B1) * gv
        nv = ADAM_B2 * v_ref[...] + (1.0 - ADAM_B2) * (gv * gv)
        m_hat = nm / (1.0 - ADAM_B1 ** ADAM_STEP)
        v_hat = nv / (1.0 - ADAM_B2 ** ADAM_STEP)
        d_ref[...] = -ADAM_LR * (m_hat / (jnp.sqrt(v_hat) + ADAM_EPS) + ADAM_WD * w_ref[...])
        nm_ref[...] = nm
        nv_ref[...] = nv

    spec = pl.BlockSpec((tr, C), lambda i: (i, 0))
    shp = jax.ShapeDtypeStruct((R, C), F32)
    return pl.pallas_call(
        body, name=name, grid=(R // tr,), in_specs=[spec] * 4, out_specs=[spec] * 3, out_shape=[shp] * 3,
        compiler_params=_params("parallel"))(w, g, m, v)


def _tile_rows(n, pref, mult=8):
    t = (pref // mult) * mult
    while t >= mult:
        if n % t == 0:
            return t
        t -= mult
    return n


ANY = pl.BlockSpec(memory_space=pl.ANY)


def _place():
    x, y, c = lax.axis_index("x"), lax.axis_index("y"), lax.axis_index("c")
    chips = [(1 - x, y), (x, 1 - y), (1 - x, 1 - y)]
    chip_idx = [2 * cx + cy for cx, cy in chips]
    return x, y, c, 2 * x + y, chips, chip_idx


def _rcopy(src, dst, ssem, rsem, dev):
    return pltpu.make_async_remote_copy(src_ref=src, dst_ref=dst, send_sem=ssem, recv_sem=rsem,
                                        device_id=dev, device_id_type=MESH)


HBM = pl.BlockSpec(memory_space=pltpu.HBM)
SEM = pl.BlockSpec(memory_space=pltpu.SEMAPHORE)
EFFECT = pltpu.SideEffectType.DATAFLOW_SIDE_EFFECTING


def _split_copies(kind, srcs, lands, ssem, rsem):
    _, _, c, me, chips, chip_idx = _place()
    cps = []
    for i in range(len(srcs)):
        for k in range(3):
            if kind == "gather":
                src, dst = srcs[i], lands[i].at[me]
            else:
                src, dst = srcs[i].at[chip_idx[k]], lands[i].at[k]
            cps.append(_rcopy(src, dst, ssem.at[3 * i + k], rsem.at[3 * i + k], (*chips[k], c)))
    return cps


def _exchange_start(name, kind, srcs, lands, groups):
    n, ng = len(srcs), len(groups)

    def body(*refs):
        src_refs, land_refs = refs[:n], refs[n:2 * n]
        sems = refs[2 * n:2 * n + 2 * ng]
        token = refs[-1]
        for gi, grp in enumerate(groups):
            cps = _split_copies(kind, [src_refs[i] for i in grp], [land_refs[i] for i in grp], sems[2 * gi], sems[2 * gi + 1])
            for cp in cps:
                cp.start()
        token[...] = jnp.zeros_like(token)

    arrays = list(srcs) + list(lands)
    out_shape = []
    for grp in groups:
        out_shape += [pltpu.SemaphoreType.DMA((3 * len(grp),)), pltpu.SemaphoreType.DMA((3 * len(grp),))]
    out_shape += [pltpu.HBM(a.shape, a.dtype) for a in arrays] + [jax.ShapeDtypeStruct((8, LANE), F32)]
    outs = pl.pallas_call(
        body, name=name, out_shape=out_shape, in_specs=[HBM] * (2 * n),
        out_specs=[SEM] * (2 * ng) + [HBM] * (2 * n) + [pl.BlockSpec(memory_space=pltpu.VMEM)],
        input_output_aliases={i: 2 * ng + i for i in range(2 * n)},
        compiler_params=pltpu.CompilerParams(has_side_effects=EFFECT),
    )(*[pltpu.with_memory_space_constraint(a, pltpu.HBM) for a in arrays])
    sems = [(outs[2 * gi], outs[2 * gi + 1]) for gi in range(ng)]
    thru = outs[2 * ng:2 * ng + 2 * n]
    return sems, thru[:n], thru[n:], outs[-1]


def _exchange_wait(name, kind, srcs, lands, sems, after):
    n = len(srcs)

    def body(*refs):
        cps = _split_copies(kind, refs[:n], refs[n:2 * n], refs[2 * n], refs[2 * n + 1])
        for cp in cps:
            cp.wait_send()
            cp.wait_recv()

    arrays = list(srcs) + list(lands)
    outs = pl.pallas_call(
        body, name=name, out_shape=[pltpu.HBM(a.shape, a.dtype) for a in arrays],
        in_specs=[HBM] * (2 * n) + [SEM, SEM, ANY], out_specs=[HBM] * (2 * n),
        input_output_aliases={i: i for i in range(2 * n)},
        compiler_params=pltpu.CompilerParams(has_side_effects=EFFECT),
    )(*arrays, sems[0], sems[1], after)
    return outs[:n], outs[n:]


EXCHANGE_CHUNK_BYTES = 3 * 512 * 1024


def _half_geometry(R, C, axis):
    Rp, Cp = (R // 2, C) if axis == 0 else (R, C // 2)
    tr = _tile_rows(Rp, max(16, EXCHANGE_CHUNK_BYTES // (2 * Cp)), 16)
    return Rp, Cp, tr, Rp // tr


def _pair_sum(name, g, axis):
    G, R, C = g.shape
    Rp, Cp, tr, nb = _half_geometry(R, C, axis)
    steps = G * nb

    def half_block(s, b, h):
        return (s, h * nb + b, 0) if axis == 0 else (s, b, h)

    def body(c_ref, keep_ref, give_ref, out_ref, land, ssem, rsem, credit):
        x, y, c = lax.axis_index("x"), lax.axis_index("y"), lax.axis_index("c")
        sib = (x, y, 1 - c)
        t = pl.program_id(0) * nb + pl.program_id(1)
        slot = t % 2

        @pl.when(t >= 2)
        def _():
            pl.semaphore_wait(credit, 1)

        cp = _rcopy(give_ref.at[0], land.at[slot], ssem.at[slot], rsem.at[slot], sib)
        cp.start()
        cp.wait_recv()
        out_ref[...] = (keep_ref[...].astype(F32) + land[slot].astype(F32)).astype(BF)

        @pl.when(t + 2 < steps)
        def _():
            pl.semaphore_signal(credit, 1, device_id=sib, device_id_type=MESH)

        cp.wait_send()

    blk = (None, tr, Cp)
    grid_spec = pltpu.PrefetchScalarGridSpec(
        num_scalar_prefetch=1, grid=(G, nb),
        in_specs=[pl.BlockSpec(blk, lambda s, b, c_ref: half_block(s, b, c_ref[0])),
                  pl.BlockSpec((1, tr, Cp), lambda s, b, c_ref: half_block(s, b, 1 - c_ref[0]))],
        out_specs=pl.BlockSpec(blk, lambda s, b, c_ref: (s, b, 0)),
        scratch_shapes=[pltpu.VMEM((2, tr, Cp), BF), pltpu.SemaphoreType.DMA((2,)), pltpu.SemaphoreType.DMA((2,)),
                        pltpu.SemaphoreType.REGULAR])
    c_arr = lax.axis_index("c").astype(jnp.int32).reshape(1)
    return pl.pallas_call(
        body, name=name, grid_spec=grid_spec, out_shape=jax.ShapeDtypeStruct((G, Rp, Cp), BF),
        compiler_params=_params("arbitrary", "arbitrary"))(c_arr, g, g)


def _chip_total(name, h, landed):
    G, Rp, Cp = h.shape
    tr = _tile_rows(Rp, max(16, EXCHANGE_CHUNK_BYTES // (2 * Cp)), 16)

    def body(me_ref, own_ref, l0_ref, l1_ref, l2_ref, out_ref):
        acc = own_ref[...].astype(F32)
        for r in (l0_ref, l1_ref, l2_ref):
            acc = acc + r[...].astype(F32)
        out_ref[...] = acc

    blk = (None, tr, Cp)
    grid_spec = pltpu.PrefetchScalarGridSpec(
        num_scalar_prefetch=1, grid=(Rp // tr,),
        in_specs=[pl.BlockSpec(blk, lambda b, me_ref: (me_ref[0], b, 0))]
        + [pl.BlockSpec(blk, functools.partial(lambda b, me_ref, k: (k, b, 0), k=k)) for k in range(3)],
        out_specs=pl.BlockSpec((tr, Cp), lambda b, me_ref: (b, 0)))
    me = (2 * lax.axis_index("x") + lax.axis_index("y")).astype(jnp.int32).reshape(1)
    return pl.pallas_call(
        body, name=name, grid_spec=grid_spec, out_shape=jax.ShapeDtypeStruct((Rp, Cp), F32),
        compiler_params=_params("parallel"))(me, h, landed, landed, landed)


def _pair_join(name, f, axis):
    Rp, Cp = f.shape
    R, C = (2 * Rp, Cp) if axis == 0 else (Rp, 2 * Cp)
    tr = _tile_rows(Rp, max(8, EXCHANGE_CHUNK_BYTES // (4 * Cp)), 8)
    nb = Rp // tr

    def body(f_ref, full, ssem, rsem, lsem):
        x, y, c = lax.axis_index("x"), lax.axis_index("y"), lax.axis_index("c")
        sib = (x, y, 1 - c)
        b = pl.program_id(0)

        def place(h, r0, rows):
            if axis == 0:
                return full.at[pl.ds(pl.multiple_of(h * Rp + r0, 8), rows), :]
            return full.at[pl.ds(pl.multiple_of(r0, 8), rows), pl.ds(pl.multiple_of(h * Cp, LANE), Cp)]

        mine = place(c, b * tr, tr)
        loc = pltpu.make_async_copy(f_ref, mine, lsem)
        rem = _rcopy(f_ref, mine, ssem, rsem, sib)
        loc.start()
        rem.start()
        loc.wait()
        rem.wait_send()

        @pl.when(b == nb - 1)
        def _():
            theirs = place(1 - c, 0, Rp)
            _rcopy(theirs, theirs, ssem, rsem, sib).wait_recv()

    return pl.pallas_call(
        body, name=name, grid=(nb,), in_specs=[pl.BlockSpec((tr, Cp), lambda b: (b, 0))], out_specs=ANY,
        out_shape=jax.ShapeDtypeStruct((R, C), F32),
        scratch_shapes=[pltpu.SemaphoreType.DMA, pltpu.SemaphoreType.DMA, pltpu.SemaphoreType.DMA],
        compiler_params=_params("arbitrary"))(f)


def _allreduce_small(v):
    R, K = v.shape
    ndev = 8

    def body(v_ref, o_ref, land, ssem, rsem):
        x, y, c = lax.axis_index("x"), lax.axis_index("y"), lax.axis_index("c")
        me = 4 * x + 2 * y + c
        land[me] = v_ref[...]
        cps = []
        for r in range(1, ndev):
            fx, fy, fc = (r >> 2) & 1, (r >> 1) & 1, r & 1
            peer = (x ^ fx, y ^ fy, c ^ fc)
            cp = _rcopy(v_ref, land.at[me], ssem.at[r - 1], rsem.at[r - 1], peer)
            cp.start()
            cps.append((cp, 4 * peer[0] + 2 * peer[1] + peer[2], r))
        for cp, src, r in cps:
            cp.wait_send()
            _rcopy(v_ref, land.at[src], ssem.at[r - 1], rsem.at[r - 1], (x, y, c)).wait_recv()
        acc = land[0]
        for d in range(1, ndev):
            acc = acc + land[d]
        o_ref[...] = acc

    vm = pl.BlockSpec(memory_space=pltpu.VMEM)
    return pl.pallas_call(
        body, name="allreduce_small", in_specs=[vm], out_specs=vm, out_shape=jax.ShapeDtypeStruct((R, K), F32),
        scratch_shapes=[pltpu.VMEM((ndev, R, K), F32), pltpu.SemaphoreType.DMA((ndev - 1,)), pltpu.SemaphoreType.DMA((ndev - 1,))],
    )(v)


IN_SPLITS = (Q_RANK, KV_RANK, QK_ROPE, DIL_HEADS * HEAD, DIL_HEADS * HEAD, DIL_HEADS * HEAD, D_MODEL, D_MODEL)
IN_OFF = tuple(int(v) for v in np.cumsum((0,) + IN_SPLITS))


def _unshard_cols(g):
    G, K, Ns = g.shape
    return g.transpose(1, 0, 2).reshape(K, G * Ns)


def _shard_cols(w):
    K, N = w.shape
    return w.reshape(K, N_CHIPS, N // N_CHIPS).transpose(1, 0, 2)


def _rope_pad(w):
    half = QK_ROPE // 2
    z = jnp.zeros(w.shape[:-1] + (half,), w.dtype)
    return jnp.concatenate([w[..., :half], z, w[..., half:], z], axis=-1)


def _rope_unpad(w):
    half = QK_ROPE // 2
    return jnp.concatenate([w[..., :half], w[..., 2 * half:3 * half]], axis=-1)


def _split_w_in(w_in_g):
    w = _unshard_cols(w_in_g)
    K = w.shape[0]
    p = [w[:, IN_OFF[i]:IN_OFF[i + 1]] for i in range(8)]
    w_lat = jnp.concatenate([p[0], p[1], _rope_pad(p[2]), jnp.zeros((K, LAT_W - _KPE.stop), w.dtype)], axis=1)
    w_dil = [jnp.concatenate([p[3 + t][:, g * DIL_O:(g + 1) * DIL_O] for t in range(3)], axis=1) for g in range(DIL_GROUPS)]
    w_gate = jnp.concatenate([p[6], p[7]], axis=1)
    return w_lat, w_dil, w_gate


def _merge_dw_in(dw_lat, dw_dil, dw_gate):
    parts = [dw_lat[:, _CQ], dw_lat[:, _CKV], _rope_unpad(dw_lat[:, _KPE])]
    for t in range(3):
        parts += [dw_dil[g][:, t * DIL_O:(t + 1) * DIL_O] for g in range(DIL_GROUPS)]
    parts.append(dw_gate)
    return _shard_cols(jnp.concatenate(parts, axis=1))


def _split_w_uq(w_uq_g):
    w = _unshard_cols(w_uq_g)
    K = w.shape[0]
    w = w.reshape(K, MLA_HEADS, QK_NOPE + QK_ROPE)
    return w[:, :, :QK_NOPE].reshape(K, MLA_HEADS * HEAD), _rope_pad(w[:, :, QK_NOPE:]).reshape(K, MLA_HEADS * HEAD)


def _merge_dw_uq(dw_n, dw_p):
    K = dw_n.shape[0]
    w = jnp.concatenate([dw_n.reshape(K, MLA_HEADS, HEAD), _rope_unpad(dw_p.reshape(K, MLA_HEADS, HEAD))], axis=-1)
    return _shard_cols(w.reshape(K, MLA_HEADS * (QK_NOPE + QK_ROPE)))


def _split_w_ukv(w_ukv_g):
    w = _unshard_cols(w_ukv_g)
    K = w.shape[0]
    w = w.reshape(K, MLA_HEADS, 2 * HEAD)
    return w[:, :, :HEAD].reshape(K, MLA_HEADS * HEAD), w[:, :, HEAD:].reshape(K, MLA_HEADS * HEAD)


def _merge_dw_ukv(dw_k, dw_v):
    K = dw_k.shape[0]
    w = jnp.concatenate([dw_k.reshape(K, MLA_HEADS, HEAD), dw_v.reshape(K, MLA_HEADS, HEAD)], axis=-1)
    return _shard_cols(w.reshape(K, MLA_HEADS * 2 * HEAD))


GATHER_GROUPS = (("w_in",), ("w_uq", "w_ukv", "w_o_mla", "w_o_dil", "w_out"), ("w_up", "w_down", "conv_w"))
REDUCE_GROUPS = (("w_down", "w_up"), ("w_out", "w_o_mla", "w_o_dil"), ("w_uq", "w_ukv", "w_in"))


def _local_step(x, tgt, W, fetch, emit):
    S, D = x.shape
    cos, sin_s = _rope_tables(S)
    w_lat, w_dil, w_gate = _split_w_in(fetch(0, x)["w_in"])

    h = _rmsnorm_fwd("attn_norm", x, W["attn_norm_g"])
    lat = _mm_nn("proj_lat", h, w_lat)
    qkv = [_mm_nn(f"proj_dil{g}", h, w_dil[g], o_dtype=BF) for g in range(DIL_GROUPS)]
    gpre = _mm_nn("proj_gate", h, w_gate)
    WB = fetch(1, gpre)
    w_uqn, w_uqp = _split_w_uq(WB["w_uq"])
    w_k, w_v = _split_w_ukv(WB["w_ukv"])
    w_o_mla, w_o_dil = WB["w_o_mla"], WB["w_o_dil"]
    w_out = WB["w_out"].reshape(D, D)
    qn_, kvn, kpe = _mla_prep(lat, W["q_norm_g"], W["kv_norm_g"], cos, sin_s)
    q_nope = _mm_nn("q_nope", qn_, w_uqn, o_dtype=BF)
    q_pe = _rope("q_rope", _mm_nn("q_pe", qn_, w_uqp), cos, sin_s, False)
    k_nope = _mm_nn("k_nope", kvn, w_k, o_dtype=BF)
    v_mla = _mm_nn("v_mla", kvn, w_v, o_dtype=BF)
    attn_a, lse_a = _mla_fwd(q_nope, q_pe, k_nope, kpe, v_mla)
    dil = [_dil_fwd(qkv[g], g) for g in range(DIL_GROUPS)]
    attn_b, lse_b = _dil_combine([o for o, _ in dil], [l for _, l in dil])
    o_a = _mm_nn("o_mla", attn_a, w_o_mla)
    o_b = _mm_nn("o_dil", attn_b, w_o_dil)
    merge = _merge_fwd(gpre, W["b_gate"], o_a, o_b)
    x1 = _mm_nn("out_proj", merge, w_out, add=x)
    WC = fetch(2, merge)
    w_up = WC["w_up"]
    G4, _, C = w_up.shape
    w_down = WC["w_down"].reshape(G4 // 2, C, D)
    conv_w = WC["conv_w"]
    conv_b = W["conv_b"].reshape(G4, 1, C)
    h2 = _rmsnorm_fwd("ffn_norm", x1, W["ffn_norm_g"])
    u_pre = _up_fwd(h2, w_up)
    act = _ffn_act(u_pre, conv_w, conv_b)
    x2 = _down_fwd(act, w_down, x1)
    dx2, d_final_g, loss8 = _final_loss(x2, tgt, W["final_norm_g"])

    d_act = _down_dgrad(dx2, w_down)
    dw_down = _down_wgrad(act, dx2)
    du = _ffn_act_bwd(u_pre, conv_w, conv_b, d_act)
    du_pre, d_conv_w, d_conv_b = _conv_bwd(du, u_pre, conv_w)
    dh2 = _up_dgrad(du_pre, w_up)
    dw_up = _up_wgrad(h2, du_pre)
    zero = emit(0, {"w_down": dw_down.reshape(N_CHIPS, (G4 // 2) * C // N_CHIPS, D), "w_up": dw_up})
    dx1, d_ffn_g = _rmsnorm_bwd("ffn_norm_bwd", dh2, x1, W["ffn_norm_g"] + zero, dx2)
    d_merge = _mm_nt("out_proj_dgrad", dx1, w_out)
    dw_out = _mm_tn("out_proj_wgrad", merge, dx1)
    d_oa, d_ob, d_gpre, d_b_gate = _merge_bwd(d_merge, gpre, W["b_gate"], o_a, o_b)
    d_attn_a = _mm_nt("o_mla_dgrad", d_oa, w_o_mla, o_dtype=BF)
    dw_o_mla = _mm_tn("o_mla_wgrad", attn_a, d_oa, shards=N_CHIPS)
    d_attn_b = _mm_nt("o_dil_dgrad", d_ob, w_o_dil, o_dtype=BF)
    dw_o_dil = _mm_tn("o_dil_wgrad", attn_b, d_ob, shards=N_CHIPS)
    zero = emit(1, {"w_out": dw_out.reshape(N_CHIPS, D // N_CHIPS, D), "w_o_mla": dw_o_mla, "w_o_dil": dw_o_dil})
    q_norm_g = W["q_norm_g"] + zero
    delta_b = _dil_delta(d_attn_b, attn_b)
    d_qkv = [_dil_bwd(qkv[g], d_attn_b, lse_b, delta_b, g) for g in range(DIL_GROUPS)]
    dq_nope, dq_pe_rot = _mla_bwd_dq(q_nope, q_pe, k_nope, kpe, v_mla, d_attn_a, attn_a, lse_a)
    dk_nope, dv_mla, dkpe_rot = _mla_bwd_dkv(q_nope, q_pe, k_nope, kpe, v_mla, d_attn_a, attn_a, lse_a)
    dq_pe = _rope("q_rope_bwd", dq_pe_rot, cos, sin_s, True)
    d_qn = _mm_nt("q_pe_dgrad", dq_pe, w_uqp, add=_mm_nt("q_nope_dgrad", dq_nope, w_uqn))
    d_kvn = _mm_nt("v_dgrad", dv_mla, w_v, add=_mm_nt("k_nope_dgrad", dk_nope, w_k))
    dw_uq = _merge_dw_uq(_mm_tn("q_nope_wgrad", qn_, dq_nope), _mm_tn("q_pe_wgrad", qn_, dq_pe))
    dw_ukv = _merge_dw_ukv(_mm_tn("k_nope_wgrad", kvn, dk_nope), _mm_tn("v_wgrad", kvn, dv_mla))
    d_lat, d_q_g, d_kv_g = _mla_prep_bwd(lat, q_norm_g, W["kv_norm_g"], cos, sin_s, d_qn, d_kvn, dkpe_rot)
    dw_in = _merge_dw_in(_mm_tn("proj_lat_wgrad", h, d_lat),
                         [_mm_tn(f"proj_dil{g}_wgrad", h, d_qkv[g]) for g in range(DIL_GROUPS)],
                         _mm_tn("proj_gate_wgrad", h, d_gpre))
    zero = emit(2, {"w_uq": dw_uq, "w_ukv": dw_ukv, "w_in": dw_in})
    dh = _mm_nt("proj_lat_dgrad", d_lat, w_lat + zero.astype(BF))
    for g in range(DIL_GROUPS):
        dh = _mm_nt(f"proj_dil{g}_dgrad", d_qkv[g], w_dil[g], add=dh)
    dh = _mm_nt("proj_gate_dgrad", d_gpre, w_gate, add=dh)
    grad_x, d_attn_g = _rmsnorm_bwd("attn_norm_bwd", dh, x, W["attn_norm_g"], dx1)

    small = {"attn_norm_g": d_attn_g, "b_gate": d_b_gate, "q_norm_g": d_q_g, "kv_norm_g": d_kv_g,
             "ffn_norm_g": d_ffn_g, "conv_w": d_conv_w, "conv_b": d_conv_b.reshape(1, G4 * C),
             "final_norm_g": d_final_g}
    return loss8[0, 0], grad_x, small


BIG = ("w_in", "w_uq", "w_ukv", "w_o_mla", "w_o_dil", "w_out", "w_up", "w_down")
SMALL = ("attn_norm_g", "b_gate", "q_norm_g", "kv_norm_g", "ffn_norm_g", "conv_w", "conv_b", "final_norm_g")
WEIGHTS = ("attn_norm_g", "w_in", "b_gate", "q_norm_g", "w_uq", "kv_norm_g", "w_ukv", "w_o_mla", "w_o_dil",
           "w_out", "ffn_norm_g", "w_up", "conv_w", "conv_b", "w_down", "final_norm_g")
SMALL_ROWS = 8
HALF_AXIS = {"w_down": 1}


def _gather_start(shards):
    names = [n for grp in GATHER_GROUPS for n in grp]
    chip = 2 * lax.axis_index("x") + lax.axis_index("y")
    srcs = [shards[n] for n in names]
    lands = [lax.dynamic_update_slice(lax.empty((N_CHIPS,) + s.shape, s.dtype), s[None], (chip, 0, 0)) for s in srcs]
    groups, at = [], 0
    for grp in GATHER_GROUPS:
        groups.append(list(range(at, at + len(grp))))
        at += len(grp)
    sems, srcs, lands, token = _exchange_start("gather_start", "gather", srcs, lands, groups)

    def fetch(i, after):
        idx = groups[i]
        _, got = _exchange_wait(f"gather_wait{i}", "gather", [srcs[j] for j in idx], [lands[j] for j in idx], sems[i], after)
        return dict(zip(GATHER_GROUPS[i], got))

    return fetch, token[0, 0]


def _reduce_start(i, grads):
    names = REDUCE_GROUPS[i]
    hs = [_pair_sum(f"pair_sum_{n}", grads[n], HALF_AXIS.get(n, 0)) for n in names]
    lands = [lax.empty((3,) + h.shape[1:], h.dtype) for h in hs]
    sems, hs, lands, token = _exchange_start(f"reduce_start{i}", "scatter", hs, lands, [list(range(len(names)))])
    return (sems[0], hs, lands), token[0, 0]


def _reduce_finish(i, pending, after):
    sems, hs, lands = pending
    hs, lands = _exchange_wait(f"reduce_wait{i}", "scatter", hs, lands, sems, after)
    out = {}
    for n, h, landed in zip(REDUCE_GROUPS[i], hs, lands):
        out[n] = _pair_join(f"pair_join_{n}", _chip_total(f"chip_total_{n}", h, landed), HALF_AXIS.get(n, 0))
    return out


def _reduce_small(small):
    flat = [small[n].reshape(-1) for n in SMALL]
    sizes = [f.shape[0] for f in flat]
    total = sum(sizes)
    width = -(-total // (SMALL_ROWS * LANE)) * LANE
    packed = jnp.concatenate(flat + [jnp.zeros((SMALL_ROWS * width - total,), F32)]).reshape(SMALL_ROWS, width)
    red = _allreduce_small(packed).reshape(-1)
    out, off = {}, 0
    for n, s in zip(SMALL, sizes):
        out[n] = red[off:off + s]
        off += s
    return out


def kernel(x, attn_norm_g, w_in, b_gate, q_norm_g, w_uq, kv_norm_g, w_ukv, w_o_mla, w_o_dil, w_out, ffn_norm_g, w_up, conv_w, conv_b, w_down, final_norm_g, loss_target, m_attn_norm_g, m_w_in, m_b_gate, m_q_norm_g, m_w_uq, m_kv_norm_g, m_w_ukv, m_w_o_mla, m_w_o_dil, m_w_out, m_ffn_norm_g, m_w_up, m_conv_w, m_conv_b, m_w_down, m_final_norm_g, v_attn_norm_g, v_w_in, v_b_gate, v_q_norm_g, v_w_uq, v_kv_norm_g, v_w_ukv, v_w_o_mla, v_w_o_dil, v_w_out, v_ffn_norm_g, v_w_up, v_conv_w, v_conv_b, v_w_down, v_final_norm_g):
    given = dict(attn_norm_g=attn_norm_g, w_in=w_in, b_gate=b_gate, q_norm_g=q_norm_g, w_uq=w_uq, kv_norm_g=kv_norm_g,
                 w_ukv=w_ukv, w_o_mla=w_o_mla, w_o_dil=w_o_dil, w_out=w_out, ffn_norm_g=ffn_norm_g, w_up=w_up,
                 conv_w=conv_w, conv_b=conv_b, w_down=w_down, final_norm_g=final_norm_g)
    moments_m = dict(attn_norm_g=m_attn_norm_g, w_in=m_w_in, b_gate=m_b_gate, q_norm_g=m_q_norm_g, w_uq=m_w_uq,
                     kv_norm_g=m_kv_norm_g, w_ukv=m_w_ukv, w_o_mla=m_w_o_mla, w_o_dil=m_w_o_dil, w_out=m_w_out,
                     ffn_norm_g=m_ffn_norm_g, w_up=m_w_up, conv_w=m_conv_w, conv_b=m_conv_b, w_down=m_w_down,
                     final_norm_g=m_final_norm_g)
    moments_v = dict(attn_norm_g=v_attn_norm_g, w_in=v_w_in, b_gate=v_b_gate, q_norm_g=v_q_norm_g, w_uq=v_w_uq,
                     kv_norm_g=v_kv_norm_g, w_ukv=v_w_ukv, w_o_mla=v_w_o_mla, w_o_dil=v_w_o_dil, w_out=v_w_out,
                     ffn_norm_g=v_ffn_norm_g, w_up=v_w_up, conv_w=v_conv_w, conv_b=v_conv_b, w_down=v_w_down,
                     final_norm_g=v_final_norm_g)

    shards = {n: given[n][0].astype(BF) for n in BIG}
    shards["conv_w"] = given["conv_w"][0]
    fetch, zero = _gather_start(shards)
    W = {n: given[n] for n in ("b_gate", "q_norm_g", "kv_norm_g", "ffn_norm_g", "conv_b")}
    W["attn_norm_g"] = given["attn_norm_g"] + zero
    W["final_norm_g"] = given["final_norm_g"].reshape(1, -1)

    pending = {}

    def emit(i, grads):
        pending[i], token = _reduce_start(i, grads)
        return token

    loss_part, grad_x, small = _local_step(x[0], loss_target[0], W, fetch, emit)
    loss = lax.psum(loss_part, ("x", "y", "c"))
    grads, delta, new_m, new_v = {}, {}, {}, {}

    def adamw(n, g):
        shp = given[n].shape
        two_d = (-1, shp[-1]) if len(shp) > 1 else (1, -1)
        d, nm, nv = _adamw(f"adamw_{n}", given[n].reshape(two_d), g.reshape(two_d),
                           moments_m[n].reshape(two_d), moments_v[n].reshape(two_d))
        grads[n], delta[n], new_m[n], new_v[n] = g.reshape(shp), d.reshape(shp), nm.reshape(shp), nv.reshape(shp)

    after = grad_x
    for i in range(len(REDUCE_GROUPS)):
        for n, g in _reduce_finish(i, pending[i], after).items():
            adamw(n, g)
        after = delta[REDUCE_GROUPS[i][-1]]
    g_small = _reduce_small(small)
    chip = 2 * lax.axis_index("x") + lax.axis_index("y")
    for n in SMALL:
        if n == "conv_w":
            full = g_small[n].reshape(N_CHIPS, 3, -1)
            adamw(n, lax.dynamic_index_in_dim(full, chip, 0, keepdims=True))
        else:
            adamw(n, g_small[n])

    return (loss, grad_x[None], *[grads[n] for n in WEIGHTS], *[delta[n] for n in WEIGHTS],
            *[new_m[n] for n in WEIGHTS], *[new_v[n] for n in WEIGHTS])
```

```python
import functools
import math

import numpy as np
import jax
import jax.numpy as jnp
from jax import lax
from jax.experimental import pallas as pl
from jax.experimental.pallas import tpu as pltpu

F32 = jnp.float32
BF = jnp.bfloat16
MESH = pl.DeviceIdType.MESH

D_MODEL = 2048
MLA_HEADS = 8
QK_NOPE = 128
QK_ROPE = 64
Q_RANK = 512
KV_RANK = 256
ROPE_THETA = 10000.0
DIL_PATTERNS = ((128, 1), (512, 4), (2048, 16))
DIL_GROUPS = 3
DIL_HPG = 4
DIL_HEADS = 12
HEAD = 128
DIL_BLOCK = 128
ALIBI_MAX_BIAS = 8.0
NORM_EPS = 1e-6
N_CHIPS = 4
ADAM_LR = 0.001
ADAM_B1 = 0.9
ADAM_B2 = 0.999
ADAM_EPS = 1e-08
ADAM_WD = 0.01
ADAM_STEP = 10

LANE = 128
VMEM_LIMIT = 56 * 1024 * 1024
MLA_SCALE = (QK_NOPE + QK_ROPE) ** -0.5
DIL_SCALE = HEAD ** -0.5


def _params(*sem):
    return pltpu.CompilerParams(dimension_semantics=sem, vmem_limit_bytes=VMEM_LIMIT)


def _tile(n, pref):
    t = (pref // LANE) * LANE
    while t >= LANE:
        if n % t == 0:
            return t
        t -= LANE
    return n


NN = (((1,), (0,)), ((), ()))
NT = (((1,), (1,)), ((), ()))
TN = (((0,), (0,)), ((), ()))


def _mm_call(name, a, b, add, *, grid, a_spec, b_spec, add_spec, o_spec, o_shape, o_dtype, acc_shape, dims, nk):
    nax = len(grid)

    def body(*refs):
        if add is None:
            a_ref, b_ref, o_ref = refs[:3]
            c_ref = None
            scr = refs[3:]
        else:
            a_ref, b_ref, c_ref, o_ref = refs[:4]
            scr = refs[4:]
        prod = lax.dot_general(a_ref[...].astype(BF), b_ref[...].astype(BF), dims, preferred_element_type=F32)
        if nk == 1:
            if c_ref is not None:
                prod = prod + c_ref[...]
            o_ref[...] = prod.astype(o_ref.dtype)
        else:
            acc = scr[0]
            k = pl.program_id(nax - 1)

            @pl.when(k == 0)
            def _():
                if c_ref is not None:
                    acc[...] = prod + c_ref[...]
                else:
                    acc[...] = prod

            @pl.when(k > 0)
            def _():
                acc[...] += prod

            @pl.when(k == nk - 1)
            def _():
                o_ref[...] = acc[...].astype(o_ref.dtype)

    ins = [a, b] + ([] if add is None else [add])
    specs = [a_spec, b_spec] + ([] if add is None else [add_spec])
    sem = ("parallel",) * (nax - 1) + ("arbitrary",)
    return pl.pallas_call(
        body, name=name, grid=grid, in_specs=specs, out_specs=o_spec,
        out_shape=jax.ShapeDtypeStruct(o_shape, o_dtype),
        scratch_shapes=[] if nk == 1 else [pltpu.VMEM(acc_shape, F32)],
        compiler_params=_params(*sem),
    )(*ins)


def _mm_nn(name, a, b, *, add=None, o_dtype=F32):
    M, K = a.shape
    sharded = b.ndim == 3
    Ns = b.shape[-1]
    N = Ns * (b.shape[0] if sharded else 1)
    tm, tn, tk = _tile(M, 1024), _tile(Ns, 1024), _tile(K, 2048)
    per = Ns // tn
    nk = K // tk
    if sharded:
        b_spec = pl.BlockSpec((None, tk, tn), lambda i, j, k: (j // per, k, j % per))
    else:
        b_spec = pl.BlockSpec((tk, tn), lambda i, j, k: (k, j))
    return _mm_call(
        name, a, b, add, grid=(M // tm, N // tn, nk),
        a_spec=pl.BlockSpec((tm, tk), lambda i, j, k: (i, k)), b_spec=b_spec,
        add_spec=pl.BlockSpec((tm, tn), lambda i, j, k: (i, j)),
        o_spec=pl.BlockSpec((tm, tn), lambda i, j, k: (i, j)),
        o_shape=(M, N), o_dtype=o_dtype, acc_shape=(tm, tn), dims=NN, nk=nk)


def _mm_nt(name, a, b, *, add=None, o_dtype=F32):
    M, K = a.shape
    sharded = b.ndim == 3
    N, Ks = b.shape[-2], b.shape[-1]
    tm, tn, tk = _tile(M, 1024), _tile(N, 1024), _tile(Ks, 2048)
    per = Ks // tk
    nk = K // tk
    if sharded:
        b_spec = pl.BlockSpec((None, tn, tk), lambda i, j, k: (k // per, j, k % per))
    else:
        b_spec = pl.BlockSpec((tn, tk), lambda i, j, k: (j, k))
    return _mm_call(
        name, a, b, add, grid=(M // tm, N // tn, nk),
        a_spec=pl.BlockSpec((tm, tk), lambda i, j, k: (i, k)), b_spec=b_spec,
        add_spec=pl.BlockSpec((tm, tn), lambda i, j, k: (i, j)),
        o_spec=pl.BlockSpec((tm, tn), lambda i, j, k: (i, j)),
        o_shape=(M, N), o_dtype=o_dtype, acc_shape=(tm, tn), dims=NT, nk=nk)


def _mm_tn(name, a, b, *, shards=1, o_dtype=BF):
    S, M = a.shape
    N = b.shape[1]
    Ns = N // shards
    tm, tn, tk = _tile(M, 1024), _tile(Ns, 1024), _tile(S, 2048)
    per = Ns // tn
    nk = S // tk
    if shards > 1:
        o_spec = pl.BlockSpec((None, tm, tn), lambda i, j, k: (j // per, i, j % per))
        o_shape = (shards, M, Ns)
    else:
        o_spec = pl.BlockSpec((tm, tn), lambda i, j, k: (i, j))
        o_shape = (M, N)
    return _mm_call(
        name, a, b, None, grid=(M // tm, N // tn, nk),
        a_spec=pl.BlockSpec((tk, tm), lambda i, j, k: (k, i)),
        b_spec=pl.BlockSpec((tk, tn), lambda i, j, k: (k, j)),
        add_spec=None, o_spec=o_spec, o_shape=o_shape, o_dtype=o_dtype, acc_shape=(tm, tn), dims=TN, nk=nk)


def _up_fwd(h2, w_up):
    S, D = h2.shape
    G, _, C = w_up.shape
    tm = _tile(S, 512)
    return _mm_call(
        "up_fwd", h2, w_up, None, grid=(G, S // tm, 1),
        a_spec=pl.BlockSpec((tm, D), lambda g, i, k: (i, 0)),
        b_spec=pl.BlockSpec((None, D, C), lambda g, i, k: (g, 0, 0)),
        add_spec=None, o_spec=pl.BlockSpec((None, tm, C), lambda g, i, k: (g, i, 0)),
        o_shape=(G, S, C), o_dtype=BF, acc_shape=None, dims=NN, nk=1)


def _up_dgrad(du_pre, w_up):
    G, S, C = du_pre.shape
    D = w_up.shape[1]
    tm, tn = _tile(S, 1024), _tile(D, 1024)
    return _mm_call(
        "up_dgrad", du_pre, w_up, None, grid=(S // tm, D // tn, G),
        a_spec=pl.BlockSpec((None, tm, C), lambda i, j, g: (g, i, 0)),
        b_spec=pl.BlockSpec((None, tn, C), lambda i, j, g: (g, j, 0)),
        add_spec=None, o_spec=pl.BlockSpec((tm, tn), lambda i, j, g: (i, j)),
        o_shape=(S, D), o_dtype=F32, acc_shape=(tm, tn), dims=NT, nk=G)


def _up_wgrad(h2, du_pre):
    G, S, C = du_pre.shape
    D = h2.shape[1]
    tm, tk = _tile(D, 512), _tile(S, 2048)
    return _mm_call(
        "up_wgrad", h2, du_pre, None, grid=(G, D // tm, S // tk),
        a_spec=pl.BlockSpec((tk, tm), lambda g, i, k: (k, i)),
        b_spec=pl.BlockSpec((None, tk, C), lambda g, i, k: (g, k, 0)),
        add_spec=None, o_spec=pl.BlockSpec((None, tm, C), lambda g, i, k: (g, i, 0)),
        o_shape=(G, D, C), o_dtype=BF, acc_shape=(tm, C), dims=TN, nk=S // tk)


def _down_fwd(act, w_down, x1):
    G, S, C = act.shape
    D = w_down.shape[2]
    tm, tn = _tile(S, 1024), _tile(D, 1024)
    return _mm_call(
        "down_fwd", act, w_down, x1, grid=(S // tm, D // tn, G),
        a_spec=pl.BlockSpec((None, tm, C), lambda i, j, g: (g, i, 0)),
        b_spec=pl.BlockSpec((None, C, tn), lambda i, j, g: (g, 0, j)),
        add_spec=pl.BlockSpec((tm, tn), lambda i, j, g: (i, j)),
        o_spec=pl.BlockSpec((tm, tn), lambda i, j, g: (i, j)),
        o_shape=(S, D), o_dtype=F32, acc_shape=(tm, tn), dims=NN, nk=G)


def _down_dgrad(dx2, w_down):
    S, D = dx2.shape
    G, C, _ = w_down.shape
    tm = _tile(S, 512)
    return _mm_call(
        "down_dgrad", dx2, w_down, None, grid=(G, S // tm, 1),
        a_spec=pl.BlockSpec((tm, D), lambda g, i, k: (i, 0)),
        b_spec=pl.BlockSpec((None, C, D), lambda g, i, k: (g, 0, 0)),
        add_spec=None, o_spec=pl.BlockSpec((None, tm, C), lambda g, i, k: (g, i, 0)),
        o_shape=(G, S, C), o_dtype=BF, acc_shape=None, dims=NT, nk=1)


def _down_wgrad(act, dx2):
    G, S, C = act.shape
    D = dx2.shape[1]
    tn, tk = _tile(D, 512), _tile(S, 1024)
    return _mm_call(
        "down_wgrad", act, dx2, None, grid=(G, D // tn, S // tk),
        a_spec=pl.BlockSpec((None, tk, C), lambda g, j, k: (g, k, 0)),
        b_spec=pl.BlockSpec((tk, tn), lambda g, j, k: (k, j)),
        add_spec=None, o_spec=pl.BlockSpec((None, C, tn), lambda g, j, k: (g, 0, j)),
        o_shape=(G, C, D), o_dtype=BF, acc_shape=(C, tn), dims=TN, nk=S // tk)


def _row(ts, c):
    return pl.BlockSpec((ts, c), lambda i: (i, 0))


def _bcast(r, c):
    return pl.BlockSpec((r, c), lambda i: (0, 0))


def _accumulate(i, ref, val):
    @pl.when(i == 0)
    def _():
        ref[...] = val

    @pl.when(i > 0)
    def _():
        ref[...] += val


def _rstd(xv):
    return lax.rsqrt(jnp.mean(xv * xv, axis=-1, keepdims=True) + NORM_EPS)


def _rmsnorm_fwd(name, x, g):
    S, D = x.shape
    ts = _tile(S, 512)

    def body(x_ref, g_ref, o_ref):
        xv = x_ref[...]
        o_ref[...] = (xv * _rstd(xv) * g_ref[...]).astype(o_ref.dtype)

    return pl.pallas_call(
        body, name=name, grid=(S // ts,), in_specs=[_row(ts, D), _bcast(1, D)], out_specs=_row(ts, D),
        out_shape=jax.ShapeDtypeStruct((S, D), BF), compiler_params=_params("parallel"))(x, g)


def _norm_bwd_rows(dy, xv, g):
    r = _rstd(xv)
    xh = xv * r
    dxh = dy * g
    dx = r * (dxh - xh * jnp.mean(dxh * xh, axis=-1, keepdims=True))
    return dx, jnp.sum(dy * xh, axis=0, keepdims=True)


def _rmsnorm_bwd(name, dy, x, g, res):
    S, D = x.shape
    ts = _tile(S, 512)

    def body(dy_ref, x_ref, g_ref, res_ref, dx_ref, dg_ref):
        dx, dg = _norm_bwd_rows(dy_ref[...], x_ref[...], g_ref[...])
        dx_ref[...] = dx + res_ref[...]
        _accumulate(pl.program_id(0), dg_ref, dg)

    return pl.pallas_call(
        body, name=name, grid=(S // ts,),
        in_specs=[_row(ts, D), _row(ts, D), _bcast(1, D), _row(ts, D)],
        out_specs=[_row(ts, D), _bcast(1, D)],
        out_shape=[jax.ShapeDtypeStruct((S, D), F32), jax.ShapeDtypeStruct((1, D), F32)],
        compiler_params=_params("arbitrary"))(dy, x, g, res)


def _rope_tables(S):
    half = QK_ROPE // 2
    pos = jnp.arange(S, dtype=F32)
    inv_freq = ROPE_THETA ** (-jnp.arange(0, QK_ROPE, 2, dtype=F32) / QK_ROPE)
    ang = pos[:, None] * inv_freq[None, :]
    cos, sin = jnp.cos(ang), jnp.sin(ang)
    z = jnp.zeros((S, half), F32)
    return jnp.concatenate([cos, z, cos, z], axis=1), jnp.concatenate([-sin, z, sin, z], axis=1)


def _rope_lanes(x, cos, sin_signed, inverse):
    if inverse:
        return x * cos + pltpu.roll(x * sin_signed, LANE // 2, 1)
    return x * cos + pltpu.roll(x, LANE // 2, 1) * sin_signed


def _rope(name, x, cos, sin_signed, inverse):
    S, W = x.shape
    ts = _tile(S, 512)

    def body(x_ref, c_ref, s_ref, o_ref):
        c, s = c_ref[...], s_ref[...]
        for h in range(W // LANE):
            sl = slice(h * LANE, (h + 1) * LANE)
            o_ref[:, sl] = _rope_lanes(x_ref[:, sl], c, s, inverse).astype(o_ref.dtype)

    return pl.pallas_call(
        body, name=name, grid=(S // ts,), in_specs=[_row(ts, W), _row(ts, LANE), _row(ts, LANE)],
        out_specs=_row(ts, W), out_shape=jax.ShapeDtypeStruct((S, W), BF),
        compiler_params=_params("parallel"))(x, cos, sin_signed)


LAT_W = 1024
_CQ = slice(0, Q_RANK)
_CKV = slice(Q_RANK, Q_RANK + KV_RANK)
_KPE = slice(Q_RANK + KV_RANK, Q_RANK + KV_RANK + LANE)


def _mla_prep(lat, qg, kvg, cos, sin_signed):
    S = lat.shape[0]
    ts = _tile(S, 512)

    def body(lat_ref, qg_ref, kvg_ref, c_ref, s_ref, qn_ref, kvn_ref, kpe_ref):
        cq = lat_ref[:, _CQ]
        qn_ref[...] = (cq * _rstd(cq) * qg_ref[...]).astype(BF)
        ckv = lat_ref[:, _CKV]
        kvn_ref[...] = (ckv * _rstd(ckv) * kvg_ref[...]).astype(BF)
        kpe_ref[...] = _rope_lanes(lat_ref[:, _KPE], c_ref[...], s_ref[...], False).astype(BF)

    return pl.pallas_call(
        body, name="mla_prep", grid=(S // ts,),
        in_specs=[_row(ts, LAT_W), _bcast(1, Q_RANK), _bcast(1, KV_RANK), _row(ts, LANE), _row(ts, LANE)],
        out_specs=[_row(ts, Q_RANK), _row(ts, KV_RANK), _row(ts, LANE)],
        out_shape=[jax.ShapeDtypeStruct((S, Q_RANK), BF), jax.ShapeDtypeStruct((S, KV_RANK), BF),
                   jax.ShapeDtypeStruct((S, LANE), BF)],
        compiler_params=_params("parallel"))(lat, qg, kvg, cos, sin_signed)


def _mla_prep_bwd(lat, qg, kvg, cos, sin_signed, d_qn, d_kvn, d_kpe):
    S = lat.shape[0]
    ts = _tile(S, 512)

    def body(lat_ref, qg_ref, kvg_ref, c_ref, s_ref, dqn_ref, dkvn_ref, dkpe_ref, dlat_ref, dqg_ref, dkvg_ref):
        i = pl.program_id(0)
        dcq, dqg = _norm_bwd_rows(dqn_ref[...], lat_ref[:, _CQ], qg_ref[...])
        dckv, dkvg = _norm_bwd_rows(dkvn_ref[...], lat_ref[:, _CKV], kvg_ref[...])
        dlat_ref[:, _CQ] = dcq.astype(BF)
        dlat_ref[:, _CKV] = dckv.astype(BF)
        dkpe = dkpe_ref[0]
        for g in range(1, d_kpe.shape[0]):
            dkpe = dkpe + dkpe_ref[g]
        dlat_ref[:, _KPE] = _rope_lanes(dkpe, c_ref[...], s_ref[...], True).astype(BF)
        dlat_ref[:, _KPE.stop:] = jnp.zeros((ts, LAT_W - _KPE.stop), BF)
        _accumulate(i, dqg_ref, dqg)
        _accumulate(i, dkvg_ref, dkvg)

    return pl.pallas_call(
        body, name="mla_prep_bwd", grid=(S // ts,),
        in_specs=[_row(ts, LAT_W), _bcast(1, Q_RANK), _bcast(1, KV_RANK), _row(ts, LANE), _row(ts, LANE),
                  _row(ts, Q_RANK), _row(ts, KV_RANK), pl.BlockSpec((d_kpe.shape[0], ts, LANE), lambda i: (0, i, 0))],
        out_specs=[_row(ts, LAT_W), _bcast(1, Q_RANK), _bcast(1, KV_RANK)],
        out_shape=[jax.ShapeDtypeStruct((S, LAT_W), BF), jax.ShapeDtypeStruct((1, Q_RANK), F32),
                   jax.ShapeDtypeStruct((1, KV_RANK), F32)],
        compiler_params=_params("arbitrary"))(lat, qg, kvg, cos, sin_signed, d_qn, d_kvn, d_kpe)


def _sigmoid(z):
    return 1.0 / (1.0 + jnp.exp(-z))


def _merge_fwd(gpre, b_gate, o_a, o_b):
    S, D = o_a.shape
    ts = _tile(S, 256)

    def body(g_ref, b_ref, oa_ref, ob_ref, m_ref):
        ga = _sigmoid(g_ref[:, :D] + b_ref[:, :D])
        gb = _sigmoid(g_ref[:, D:] + b_ref[:, D:])
        m_ref[...] = (ga * oa_ref[...] + gb * ob_ref[...]).astype(BF)

    return pl.pallas_call(
        body, name="merge_fwd", grid=(S // ts,),
        in_specs=[_row(ts, 2 * D), _bcast(1, 2 * D), _row(ts, D), _row(ts, D)], out_specs=_row(ts, D),
        out_shape=jax.ShapeDtypeStruct((S, D), BF), compiler_params=_params("parallel"))(gpre, b_gate, o_a, o_b)


def _merge_bwd(d_merge, gpre, b_gate, o_a, o_b):
    S, D = o_a.shape
    ts = _tile(S, 256)

    def body(dm_ref, g_ref, b_ref, oa_ref, ob_ref, doa_ref, dob_ref, dg_ref, db_ref):
        dm = dm_ref[...]
        ga = _sigmoid(g_ref[:, :D] + b_ref[:, :D])
        gb = _sigmoid(g_ref[:, D:] + b_ref[:, D:])
        doa_ref[...] = (dm * ga).astype(BF)
        dob_ref[...] = (dm * gb).astype(BF)
        dga = dm * oa_ref[...] * ga * (1.0 - ga)
        dgb = dm * ob_ref[...] * gb * (1.0 - gb)
        dg_ref[:, :D] = dga.astype(BF)
        dg_ref[:, D:] = dgb.astype(BF)
        i = pl.program_id(0)
        part = jnp.concatenate([jnp.sum(dga, axis=0, keepdims=True), jnp.sum(dgb, axis=0, keepdims=True)], axis=1)
        _accumulate(i, db_ref, part)

    return pl.pallas_call(
        body, name="merge_bwd", grid=(S // ts,),
        in_specs=[_row(ts, D), _row(ts, 2 * D), _bcast(1, 2 * D), _row(ts, D), _row(ts, D)],
        out_specs=[_row(ts, D), _row(ts, D), _row(ts, 2 * D), _bcast(1, 2 * D)],
        out_shape=[jax.ShapeDtypeStruct((S, D), BF), jax.ShapeDtypeStruct((S, D), BF),
                   jax.ShapeDtypeStruct((S, 2 * D), BF), jax.ShapeDtypeStruct((1, 2 * D), F32)],
        compiler_params=_params("arbitrary"))(d_merge, gpre, b_gate, o_a, o_b)


def _final_loss(x2, tgt, gf):
    S, D = x2.shape
    ts = _tile(S, 512)

    def body(x_ref, t_ref, g_ref, dx_ref, dg_ref, loss_ref):
        i = pl.program_id(0)
        xv = x_ref[...]
        g = g_ref[...]
        y = xv * _rstd(xv) * g
        err = y - t_ref[...]
        dx, dg = _norm_bwd_rows(err * (1.0 / D), xv, g)
        dx_ref[...] = dx
        _accumulate(i, dg_ref, dg)
        part = 0.5 * jnp.sum(jnp.mean(err * err, axis=-1, keepdims=True), axis=0, keepdims=True)
        _accumulate(i, loss_ref, jnp.broadcast_to(part, (8, LANE)))

    return pl.pallas_call(
        body, name="final_loss", grid=(S // ts,),
        in_specs=[_row(ts, D), _row(ts, D), _bcast(1, D)],
        out_specs=[_row(ts, D), _bcast(1, D), _bcast(8, LANE)],
        out_shape=[jax.ShapeDtypeStruct((S, D), F32), jax.ShapeDtypeStruct((1, D), F32),
                   jax.ShapeDtypeStruct((8, LANE), F32)],
        compiler_params=_params("arbitrary"))(x2, tgt, gf)


HALO = 16


SUB = 8


def _shift_down(cur, prev, k, rows):
    out = pltpu.roll(cur, k, 0)
    head = out[:SUB]
    for j in range(k):
        head = jnp.where(rows == j, prev[HALO - k + j:HALO - k + j + 1, :], head)
    return jnp.concatenate([head, out[SUB:]], axis=0)


def _shift_up(cur, nxt, k, rows, ts):
    out = pltpu.roll(cur, ts - k, 0)
    tail = out[ts - SUB:]
    for j in range(k):
        tail = jnp.where(rows == SUB - k + j, nxt[j:j + 1, :], tail)
    return jnp.concatenate([out[:ts - SUB], tail], axis=0)


def _conv_rows(cur, prev, w, b, rows):
    return b + w[0:1, :] * _shift_down(cur, prev, 2, rows) + w[1:2, :] * _shift_down(cur, prev, 1, rows) + w[2:3, :] * cur


def _conv_specs(ts, C, shard_of):
    nh = ts // HALO
    cur = pl.BlockSpec((None, ts, C), lambda g, i: (shard_of(g), i, 0))
    prev = pl.BlockSpec((None, HALO, C), lambda g, i: (shard_of(g), jnp.maximum(i * nh - 1, 0), 0))
    return cur, prev


def _ffn_act(u_pre, conv_w, conv_b):
    G4, S, C = u_pre.shape
    G = G4 // 2
    ts = _tile(S, 256)

    def body(up_ref, upp_ref, gt_ref, gtp_ref, wu_ref, wg_ref, bu_ref, bg_ref, act_ref):
        first = pl.program_id(1) == 0
        rows = lax.broadcasted_iota(jnp.int32, (SUB, C), 0)
        pu = jnp.where(first, 0.0, upp_ref[...].astype(F32))
        pg = jnp.where(first, 0.0, gtp_ref[...].astype(F32))
        up = _conv_rows(up_ref[...].astype(F32), pu, wu_ref[...], bu_ref[...], rows)
        gate = _conv_rows(gt_ref[...].astype(F32), pg, wg_ref[...], bg_ref[...], rows)
        act_ref[...] = (gate * _sigmoid(gate) * up).astype(BF)

    cur_u, prev_u = _conv_specs(ts, C, lambda g: g)
    cur_g, prev_g = _conv_specs(ts, C, lambda g: g + G)
    w_u = pl.BlockSpec((None, 3, C), lambda g, i: (g, 0, 0))
    w_g = pl.BlockSpec((None, 3, C), lambda g, i: (g + G, 0, 0))
    b_u = pl.BlockSpec((None, 1, C), lambda g, i: (g, 0, 0))
    b_g = pl.BlockSpec((None, 1, C), lambda g, i: (g + G, 0, 0))
    return pl.pallas_call(
        body, name="ffn_act", grid=(G, S // ts),
        in_specs=[cur_u, prev_u, cur_g, prev_g, w_u, w_g, b_u, b_g],
        out_specs=pl.BlockSpec((None, ts, C), lambda g, i: (g, i, 0)),
        out_shape=jax.ShapeDtypeStruct((G, S, C), BF),
        compiler_params=_params("parallel", "parallel"))(u_pre, u_pre, u_pre, u_pre, conv_w, conv_w, conv_b, conv_b)


def _ffn_act_bwd(u_pre, conv_w, conv_b, d_act):
    G4, S, C = u_pre.shape
    G = G4 // 2
    ts = _tile(S, 256)

    def body(up_ref, upp_ref, gt_ref, gtp_ref, wu_ref, wg_ref, bu_ref, bg_ref, da_ref, du_ref):
        first = pl.program_id(1) == 0
        rows = lax.broadcasted_iota(jnp.int32, (SUB, C), 0)
        pu = jnp.where(first, 0.0, upp_ref[...].astype(F32))
        pg = jnp.where(first, 0.0, gtp_ref[...].astype(F32))
        up = _conv_rows(up_ref[...].astype(F32), pu, wu_ref[...], bu_ref[...], rows)
        gate = _conv_rows(gt_ref[...].astype(F32), pg, wg_ref[...], bg_ref[...], rows)
        sg = _sigmoid(gate)
        da = da_ref[...].astype(F32)
        du_ref[0] = (da * (gate * sg)).astype(BF)
        du_ref[1] = (da * up * (sg * (1.0 + gate * (1.0 - sg)))).astype(BF)

    cur_u, prev_u = _conv_specs(ts, C, lambda g: g)
    cur_g, prev_g = _conv_specs(ts, C, lambda g: g + G)
    w_u = pl.BlockSpec((None, 3, C), lambda g, i: (g, 0, 0))
    w_g = pl.BlockSpec((None, 3, C), lambda g, i: (g + G, 0, 0))
    b_u = pl.BlockSpec((None, 1, C), lambda g, i: (g, 0, 0))
    b_g = pl.BlockSpec((None, 1, C), lambda g, i: (g + G, 0, 0))
    blk = pl.BlockSpec((None, ts, C), lambda g, i: (g, i, 0))
    du = pl.pallas_call(
        body, name="ffn_act_bwd", grid=(G, S // ts),
        in_specs=[cur_u, prev_u, cur_g, prev_g, w_u, w_g, b_u, b_g, blk],
        out_specs=pl.BlockSpec((2, None, ts, C), lambda g, i: (0, g, i, 0)),
        out_shape=jax.ShapeDtypeStruct((2, G, S, C), BF),
        compiler_params=_params("parallel", "parallel"))(
            u_pre, u_pre, u_pre, u_pre, conv_w, conv_w, conv_b, conv_b, d_act)
    return du.reshape(G4, S, C)


def _conv_bwd(du, u_pre, conv_w):
    G4, S, C = du.shape
    ts = _tile(S, 256)
    nh = ts // HALO
    last_halo = S // HALO - 1

    def body(du_ref, dun_ref, u_ref, up_ref, w_ref, dpre_ref, dw_ref, db_ref):
        i = pl.program_id(1)
        rows = lax.broadcasted_iota(jnp.int32, (SUB, C), 0)
        du_c = du_ref[...].astype(F32)
        nxt = jnp.where(i == pl.num_programs(1) - 1, 0.0, dun_ref[...].astype(F32))
        prev = jnp.where(i == 0, 0.0, up_ref[...].astype(F32))
        w = w_ref[...]
        dpre = w[2:3, :] * du_c + w[1:2, :] * _shift_up(du_c, nxt, 1, rows, ts) + w[0:1, :] * _shift_up(du_c, nxt, 2, rows, ts)
        dpre_ref[...] = dpre.astype(BF)
        u_c = u_ref[...].astype(F32)
        dw = jnp.concatenate([
            jnp.sum(du_c * _shift_down(u_c, prev, 2, rows), axis=0, keepdims=True),
            jnp.sum(du_c * _shift_down(u_c, prev, 1, rows), axis=0, keepdims=True),
            jnp.sum(du_c * u_c, axis=0, keepdims=True)], axis=0)
        _accumulate(i, dw_ref, dw)
        _accumulate(i, db_ref, jnp.sum(du_c, axis=0, keepdims=True))

    cur = pl.BlockSpec((None, ts, C), lambda g, i: (g, i, 0))
    nxt = pl.BlockSpec((None, HALO, C), lambda g, i: (g, jnp.minimum((i + 1) * nh, last_halo), 0))
    prev = pl.BlockSpec((None, HALO, C), lambda g, i: (g, jnp.maximum(i * nh - 1, 0), 0))
    return pl.pallas_call(
        body, name="conv_bwd", grid=(G4, S // ts),
        in_specs=[cur, nxt, cur, prev, pl.BlockSpec((None, 3, C), lambda g, i: (g, 0, 0))],
        out_specs=[cur, pl.BlockSpec((None, 3, C), lambda g, i: (g, 0, 0)), pl.BlockSpec((None, 1, C), lambda g, i: (g, 0, 0))],
        out_shape=[jax.ShapeDtypeStruct((G4, S, C), BF), jax.ShapeDtypeStruct((G4, 3, C), F32),
                   jax.ShapeDtypeStruct((G4, 1, C), F32)],
        compiler_params=_params("parallel", "arbitrary"))(du, du, u_pre, u_pre, conv_w)


MLA_T = 512
MLA_HB = 4
MLA_HW = MLA_HB * HEAD


def _mla_pairs(n, by_row):
    if by_row:
        pairs = [(i, j) for i in range(n) for j in range(i + 1)]
    else:
        pairs = [(i, j) for j in range(n) for i in range(j, n)]
    return jnp.asarray([p[0] for p in pairs], jnp.int32), jnp.asarray([p[1] for p in pairs], jnp.int32)


def _mla_specs(S):
    q = pl.BlockSpec((MLA_T, MLA_HW), lambda g, t, it, jt: (it[t], g))
    k = pl.BlockSpec((MLA_T, MLA_HW), lambda g, t, it, jt: (jt[t], g))
    kpe = pl.BlockSpec((MLA_T, HEAD), lambda g, t, it, jt: (jt[t], 0))
    lse = pl.BlockSpec((MLA_HB, MLA_T, LANE), lambda g, t, it, jt: (g, it[t], 0))
    return q, k, kpe, lse


def _mla_head(ref, hh):
    return ref[:, hh * HEAD:(hh + 1) * HEAD]


def _mla_scores(qn_ref, qp_ref, kn_ref, kpe, hh, ok):
    q = jnp.concatenate([_mla_head(qn_ref, hh), _mla_head(qp_ref, hh)], axis=1)
    k = jnp.concatenate([_mla_head(kn_ref, hh), kpe], axis=1)
    s = lax.dot_general(q, k, NT, preferred_element_type=F32) * MLA_SCALE
    return q, k, jnp.where(ok, s, -jnp.inf)


def _mla_causal(i, j):
    row = i * MLA_T + lax.broadcasted_iota(jnp.int32, (MLA_T, MLA_T), 0)
    col = j * MLA_T + lax.broadcasted_iota(jnp.int32, (MLA_T, MLA_T), 1)
    return col <= row


def _mla_fwd(qn, qp, kn, kpe, v):
    S = qn.shape[0]
    it, jt = _mla_pairs(S // MLA_T, True)

    def body(it_ref, jt_ref, qn_ref, qp_ref, kn_ref, kpe_ref, v_ref, o_ref, lse_ref, m_scr, acc_scr):
        t = pl.program_id(1)
        i, j = it_ref[t], jt_ref[t]

        @pl.when(j == 0)
        def _():
            m_scr[...] = jnp.full(m_scr.shape, -jnp.inf, F32)
            acc_scr[...] = jnp.zeros(acc_scr.shape, F32)

        ok = _mla_causal(i, j)
        kpe_v = kpe_ref[...]
        ones = jnp.ones((MLA_T, HEAD), BF)
        state = [(m_scr[hh], acc_scr[hh]) for hh in range(MLA_HB)]
        new = []
        for hh in range(MLA_HB):
            m_prev, acc = state[hh]
            _, _, s = _mla_scores(qn_ref, qp_ref, kn_ref, kpe_v, hh, ok)
            m_new = jnp.maximum(m_prev, jnp.max(s, axis=1, keepdims=True))
            p = jnp.exp(s - m_new).astype(BF)
            v1 = jnp.concatenate([_mla_head(v_ref, hh), ones], axis=1)
            new.append((m_new, jnp.exp(m_prev - m_new) * acc + lax.dot_general(p, v1, NN, preferred_element_type=F32)))
        for hh in range(MLA_HB):
            m_scr[hh], acc_scr[hh] = new[hh]

        @pl.when(j == i)
        def _():
            for hh in range(MLA_HB):
                l = acc_scr[hh, :, HEAD:]
                o_ref[:, hh * HEAD:(hh + 1) * HEAD] = (acc_scr[hh, :, :HEAD] / l).astype(BF)
                lse_ref[hh] = m_scr[hh] + jnp.log(l)

    qspec, kspec, kpespec, lsespec = _mla_specs(S)
    grid_spec = pltpu.PrefetchScalarGridSpec(
        num_scalar_prefetch=2, grid=(MLA_HEADS // MLA_HB, it.shape[0]),
        in_specs=[qspec, qspec, kspec, kpespec, kspec], out_specs=[qspec, lsespec],
        scratch_shapes=[pltpu.VMEM((MLA_HB, MLA_T, 1), F32), pltpu.VMEM((MLA_HB, MLA_T, 2 * HEAD), F32)])
    return pl.pallas_call(
        body, name="mla_fwd", grid_spec=grid_spec,
        out_shape=[jax.ShapeDtypeStruct((S, MLA_HEADS * HEAD), BF), jax.ShapeDtypeStruct((MLA_HEADS, S, LANE), F32)],
        compiler_params=_params("parallel", "arbitrary"))(it, jt, qn, qp, kn, kpe, v)


def _mla_p_ds(qn_ref, qp_ref, kn_ref, kpe, v_ref, do_ref, o_ref, lse_ref, hh, ok):
    q, k, s = _mla_scores(qn_ref, qp_ref, kn_ref, kpe, hh, ok)
    p = jnp.exp(s - lse_ref[hh][:, 0:1])
    do = _mla_head(do_ref, hh)
    delta = jnp.sum(do.astype(F32) * _mla_head(o_ref, hh).astype(F32), axis=1, keepdims=True)
    dp = lax.dot_general(do, _mla_head(v_ref, hh), NT, preferred_element_type=F32)
    ds = p * (dp - delta) * MLA_SCALE
    return q, k, p, ds, do


def _mla_bwd_dq(qn, qp, kn, kpe, v, do, o, lse):
    S = qn.shape[0]
    it, jt = _mla_pairs(S // MLA_T, True)

    def body(it_ref, jt_ref, qn_ref, qp_ref, kn_ref, kpe_ref, v_ref, do_ref, o_ref, lse_ref, dqn_ref, dqp_ref, acc):
        t = pl.program_id(1)
        i, j = it_ref[t], jt_ref[t]

        @pl.when(j == 0)
        def _():
            acc[...] = jnp.zeros(acc.shape, F32)

        ok = _mla_causal(i, j)
        kpe_v = kpe_ref[...]
        old = [acc[hh] for hh in range(MLA_HB)]
        for hh in range(MLA_HB):
            _, k, _, ds, _ = _mla_p_ds(qn_ref, qp_ref, kn_ref, kpe_v, v_ref, do_ref, o_ref, lse_ref, hh, ok)
            old[hh] = old[hh] + lax.dot_general(ds.astype(BF), k, NN, preferred_element_type=F32)
        for hh in range(MLA_HB):
            acc[hh] = old[hh]

        @pl.when(j == i)
        def _():
            for hh in range(MLA_HB):
                dqn_ref[:, hh * HEAD:(hh + 1) * HEAD] = acc[hh, :, :HEAD].astype(BF)
                dqp_ref[:, hh * HEAD:(hh + 1) * HEAD] = acc[hh, :, HEAD:]

    qspec, kspec, kpespec, lsespec = _mla_specs(S)
    grid_spec = pltpu.PrefetchScalarGridSpec(
        num_scalar_prefetch=2, grid=(MLA_HEADS // MLA_HB, it.shape[0]),
        in_specs=[qspec, qspec, kspec, kpespec, kspec, qspec, qspec, lsespec], out_specs=[qspec, qspec],
        scratch_shapes=[pltpu.VMEM((MLA_HB, MLA_T, 2 * HEAD), F32)])
    return pl.pallas_call(
        body, name="mla_bwd_dq", grid_spec=grid_spec,
        out_shape=[jax.ShapeDtypeStruct((S, MLA_HEADS * HEAD), BF), jax.ShapeDtypeStruct((S, MLA_HEADS * HEAD), F32)],
        compiler_params=_params("parallel", "arbitrary"))(it, jt, qn, qp, kn, kpe, v, do, o, lse)


def _mla_bwd_dkv(qn, qp, kn, kpe, v, do, o, lse):
    S = qn.shape[0]
    nq = S // MLA_T
    it, jt = _mla_pairs(nq, False)

    def body(it_ref, jt_ref, qn_ref, qp_ref, kn_ref, kpe_ref, v_ref, do_ref, o_ref, lse_ref, dkn_ref, dv_ref, dkpe_ref,
             dk_acc, dv_acc):
        t = pl.program_id(1)
        i, j = it_ref[t], jt_ref[t]

        @pl.when(i == j)
        def _():
            dk_acc[...] = jnp.zeros(dk_acc.shape, F32)
            dv_acc[...] = jnp.zeros(dv_acc.shape, F32)

        ok = _mla_causal(i, j)
        kpe_v = kpe_ref[...]
        for hh in range(MLA_HB):
            q, _, p, ds, do = _mla_p_ds(qn_ref, qp_ref, kn_ref, kpe_v, v_ref, do_ref, o_ref, lse_ref, hh, ok)
            dv_acc[hh] += lax.dot_general(p.astype(BF), do, TN, preferred_element_type=F32)
            dk_acc[hh] += lax.dot_general(ds.astype(BF), q, TN, preferred_element_type=F32)

        @pl.when(i == nq - 1)
        def _():
            dkpe = dk_acc[0, :, HEAD:]
            for hh in range(MLA_HB):
                dkn_ref[:, hh * HEAD:(hh + 1) * HEAD] = dk_acc[hh, :, :HEAD].astype(BF)
                dv_ref[:, hh * HEAD:(hh + 1) * HEAD] = dv_acc[hh].astype(BF)
                if hh:
                    dkpe = dkpe + dk_acc[hh, :, HEAD:]
            dkpe_ref[...] = dkpe

    qspec, kspec, kpespec, lsespec = _mla_specs(S)
    dkpespec = pl.BlockSpec((None, MLA_T, HEAD), lambda g, t, it, jt: (g, jt[t], 0))
    grid_spec = pltpu.PrefetchScalarGridSpec(
        num_scalar_prefetch=2, grid=(MLA_HEADS // MLA_HB, it.shape[0]),
        in_specs=[qspec, qspec, kspec, kpespec, kspec, qspec, qspec, lsespec], out_specs=[kspec, kspec, dkpespec],
        scratch_shapes=[pltpu.VMEM((MLA_HB, MLA_T, 2 * HEAD), F32), pltpu.VMEM((MLA_HB, MLA_T, HEAD), F32)])
    return pl.pallas_call(
        body, name="mla_bwd_dkv", grid_spec=grid_spec,
        out_shape=[jax.ShapeDtypeStruct((S, MLA_HEADS * HEAD), BF), jax.ShapeDtypeStruct((S, MLA_HEADS * HEAD), BF),
                   jax.ShapeDtypeStruct((MLA_HEADS // MLA_HB, S, HEAD), F32)],
        compiler_params=_params("parallel", "arbitrary"))(it, jt, qn, qp, kn, kpe, v, do, o, lse)


DIL_W = 3 * DIL_HPG * HEAD
DIL_O = DIL_HPG * HEAD


def _dil_slopes(g):
    return [2.0 ** (-ALIBI_MAX_BIAS * (g * DIL_HPG + hh + 1) / DIL_HEADS) for hh in range(DIL_HPG)]


def _dil_bias(dil):
    p = lax.broadcasted_iota(jnp.int32, (DIL_BLOCK, DIL_BLOCK), 0)
    kk = lax.broadcasted_iota(jnp.int32, (DIL_BLOCK, DIL_BLOCK), 1)
    jc = p - kk
    dist_c = (dil * jc).astype(F32)
    dist_p = (dil * (jc + DIL_BLOCK)).astype(F32)
    return jc >= 0, jc <= 0, dist_c, dist_p


def _dil_head(blk, hh):
    q = blk[:, hh * HEAD:(hh + 1) * HEAD]
    k = blk[:, DIL_O + hh * HEAD:DIL_O + (hh + 1) * HEAD]
    v = blk[:, 2 * DIL_O + hh * HEAD:2 * DIL_O + (hh + 1) * HEAD]
    return q, k, v


def _dil_s(q, k, slope, dist, ok):
    s = lax.dot_general(q, k, NT, preferred_element_type=F32) * DIL_SCALE - slope * dist
    return jnp.where(ok, s, -jnp.inf)


def _dil_view(a, dil):
    S, W = a.shape
    return a.reshape(S // dil, dil * W)


def _dil_fwd(qkv, g):
    _, dil = DIL_PATTERNS[g]
    S = qkv.shape[0]
    L = S // dil
    nb = L // DIL_BLOCK
    slopes = _dil_slopes(g)

    def body(cur_ref, prev_ref, o_ref, lse_ref):
        n = pl.program_id(1)
        ok_c, ok_p, dist_c, dist_p = _dil_bias(dil)
        ok_p = ok_p & (n > 0)
        cur, prev = cur_ref[...], prev_ref[...]
        for hh in range(DIL_HPG):
            q, kc, vc = _dil_head(cur, hh)
            _, kp, vp = _dil_head(prev, hh)
            sc = _dil_s(q, kc, slopes[hh], dist_c, ok_c)
            sp = _dil_s(q, kp, slopes[hh], dist_p, ok_p)
            m = jnp.maximum(jnp.max(sc, axis=1, keepdims=True), jnp.max(sp, axis=1, keepdims=True))
            pc, pp = jnp.exp(sc - m), jnp.exp(sp - m)
            l = jnp.sum(pc, axis=1, keepdims=True) + jnp.sum(pp, axis=1, keepdims=True)
            o = (lax.dot_general(pc.astype(BF), vc, NN, preferred_element_type=F32)
                 + lax.dot_general(pp.astype(BF), vp, NN, preferred_element_type=F32)) / l
            sl = slice(hh * HEAD, (hh + 1) * HEAD)
            o_ref[:, sl] = o
            lse_ref[:, sl] = jnp.broadcast_to(m + jnp.log(l), (DIL_BLOCK, HEAD))

    ospec = pl.BlockSpec((DIL_BLOCK, DIL_O), lambda r, n: (n, r))
    o, lse = pl.pallas_call(
        body, name=f"dil_fwd{g}", grid=(dil, nb),
        in_specs=[pl.BlockSpec((DIL_BLOCK, DIL_W), lambda r, n: (n, r)),
                  pl.BlockSpec((DIL_BLOCK, DIL_W), lambda r, n: (jnp.maximum(n - 1, 0), r))],
        out_specs=[ospec, ospec],
        out_shape=[jax.ShapeDtypeStruct((L, dil * DIL_O), F32), jax.ShapeDtypeStruct((L, dil * DIL_O), F32)],
        compiler_params=_params("parallel", "parallel"))(_dil_view(qkv, dil), _dil_view(qkv, dil))
    return o.reshape(S, DIL_O), lse.reshape(S, DIL_O)


def _dil_combine(os_, lses):
    S = os_[0].shape[0]
    ts = _tile(S, 512)

    def body(o0, o1, o2, l0, l1, l2, out_ref, lse_ref):
        a, b, c = l0[...], l1[...], l2[...]
        m = jnp.maximum(jnp.maximum(a, b), c)
        ea, eb, ec = jnp.exp(a - m), jnp.exp(b - m), jnp.exp(c - m)
        tot = ea + eb + ec
        out_ref[...] = ((ea * o0[...] + eb * o1[...] + ec * o2[...]) / tot).astype(BF)
        lse_ref[...] = m + jnp.log(tot)

    return pl.pallas_call(
        body, name="dil_combine", grid=(S // ts,), in_specs=[_row(ts, DIL_O)] * 6,
        out_specs=[_row(ts, DIL_O), _row(ts, DIL_O)],
        out_shape=[jax.ShapeDtypeStruct((S, DIL_O), BF), jax.ShapeDtypeStruct((S, DIL_O), F32)],
        compiler_params=_params("parallel"))(*os_, *lses)


def _dil_delta(do, out):
    S = do.shape[0]
    ts = _tile(S, 512)

    def body(do_ref, o_ref, d_ref):
        for hh in range(DIL_HPG):
            sl = slice(hh * HEAD, (hh + 1) * HEAD)
            d = jnp.sum(do_ref[:, sl].astype(F32) * o_ref[:, sl].astype(F32), axis=1, keepdims=True)
            d_ref[:, sl] = jnp.broadcast_to(d, (ts, HEAD))

    return pl.pallas_call(
        body, name="dil_delta", grid=(S // ts,), in_specs=[_row(ts, DIL_O)] * 2, out_specs=_row(ts, DIL_O),
        out_shape=jax.ShapeDtypeStruct((S, DIL_O), F32), compiler_params=_params("parallel"))(do, out)


def _dil_bwd(qkv, do, lse, delta, g):
    _, dil = DIL_PATTERNS[g]
    S = qkv.shape[0]
    L = S // dil
    nb = L // DIL_BLOCK
    slopes = _dil_slopes(g)

    def pair(q, k, v, do_h, lse_h, delta_h, slope, dist, ok):
        s = _dil_s(q, k, slope, dist, ok)
        p = jnp.exp(s - lse_h)
        dp = lax.dot_general(do_h, v, NT, preferred_element_type=F32)
        ds = (p * (dp - delta_h) * DIL_SCALE).astype(BF)
        return p.astype(BF), ds

    def body(cur_ref, prev_ref, next_ref, doc_ref, don_ref, lsec_ref, lsen_ref, dlc_ref, dln_ref, out_ref):
        n = pl.program_id(1)
        ok_c, ok_p0, dist_c, dist_p = _dil_bias(dil)
        ok_a = ok_p0 & (n > 0)
        ok_n = ok_p0 & (n < nb - 1)
        cur, prev, nxt = cur_ref[...], prev_ref[...], next_ref[...]
        for hh in range(DIL_HPG):
            sl = slice(hh * HEAD, (hh + 1) * HEAD)
            q, kc, vc = _dil_head(cur, hh)
            _, kp, vp = _dil_head(prev, hh)
            qn, _, _ = _dil_head(nxt, hh)
            do_c, do_n = doc_ref[:, sl], don_ref[:, sl]
            lse_c, lse_n = lsec_ref[:, sl][:, 0:1], lsen_ref[:, sl][:, 0:1]
            dl_c, dl_n = dlc_ref[:, sl][:, 0:1], dln_ref[:, sl][:, 0:1]
            _, ds_a = pair(q, kp, vp, do_c, lse_c, dl_c, slopes[hh], dist_p, ok_a)
            p_b, ds_b = pair(q, kc, vc, do_c, lse_c, dl_c, slopes[hh], dist_c, ok_c)
            p_n, ds_n = pair(qn, kc, vc, do_n, lse_n, dl_n, slopes[hh], dist_p, ok_n)
            dq = (lax.dot_general(ds_a, kp, NN, preferred_element_type=F32)
                  + lax.dot_general(ds_b, kc, NN, preferred_element_type=F32))
            dk = (lax.dot_general(ds_b, q, TN, preferred_element_type=F32)
                  + lax.dot_general(ds_n, qn, TN, preferred_element_type=F32))
            dv = (lax.dot_general(p_b, do_c, TN, preferred_element_type=F32)
                  + lax.dot_general(p_n, do_n, TN, preferred_element_type=F32))
            out_ref[:, sl] = dq.astype(BF)
            out_ref[:, DIL_O + hh * HEAD:DIL_O + (hh + 1) * HEAD] = dk.astype(BF)
            out_ref[:, 2 * DIL_O + hh * HEAD:2 * DIL_O + (hh + 1) * HEAD] = dv.astype(BF)

    cur_w = pl.BlockSpec((DIL_BLOCK, DIL_W), lambda r, n: (n, r))
    prev_w = pl.BlockSpec((DIL_BLOCK, DIL_W), lambda r, n: (jnp.maximum(n - 1, 0), r))
    next_w = pl.BlockSpec((DIL_BLOCK, DIL_W), lambda r, n: (jnp.minimum(n + 1, nb - 1), r))
    cur_o = pl.BlockSpec((DIL_BLOCK, DIL_O), lambda r, n: (n, r))
    next_o = pl.BlockSpec((DIL_BLOCK, DIL_O), lambda r, n: (jnp.minimum(n + 1, nb - 1), r))
    qv, dov, lsev, dlv = _dil_view(qkv, dil), _dil_view(do, dil), _dil_view(lse, dil), _dil_view(delta, dil)
    out = pl.pallas_call(
        body, name=f"dil_bwd{g}", grid=(dil, nb),
        in_specs=[cur_w, prev_w, next_w, cur_o, next_o, cur_o, next_o, cur_o, next_o],
        out_specs=cur_w, out_shape=jax.ShapeDtypeStruct((L, dil * DIL_W), BF),
        compiler_params=_params("parallel", "parallel"))(qv, qv, qv, dov, dov, lsev, lsev, dlv, dlv)
    return out.reshape(S, DIL_W)


def _adamw(name, w, g, m, v):
    R, C = w.shape
    tr = R if R <= 512 else _tile_rows(R, 256)

    def body(w_ref, g_ref, m_ref, v_ref, go_ref, d_ref, nm_ref, nv_ref):
        gv = g_ref[...]
        go_ref[...] = gv
        nm = ADAM_B1 * m_ref[...] + (1.0 - ADAM_B1) * gv
        nv = ADAM_B2 * v_ref[...] + (1.0 - ADAM_B2) * (gv * gv)
        m_hat = nm / (1.0 - ADAM_B1 ** ADAM_STEP)
        v_hat = nv / (1.0 - ADAM_B2 ** ADAM_STEP)
        d_ref[...] = -ADAM_LR * (m_hat / (jnp.sqrt(v_hat) + ADAM_EPS) + ADAM_WD * w_ref[...])
        nm_ref[...] = nm
        nv_ref[...] = nv

    spec = pl.BlockSpec((tr, C), lambda i: (i, 0))
    shp = jax.ShapeDtypeStruct((R, C), F32)
    return pl.pallas_call(
        body, name=name, grid=(R // tr,), in_specs=[spec] * 4, out_specs=[spec] * 4, out_shape=[shp] * 4,
        compiler_params=_params("parallel"))(w, g, m, v)


def _tile_rows(n, pref, mult=8):
    t = (pref // mult) * mult
    while t >= mult:
        if n % t == 0:
            return t
        t -= mult
    return n


ANY = pl.BlockSpec(memory_space=pl.ANY)


def _place():
    x, y, c = lax.axis_index("x"), lax.axis_index("y"), lax.axis_index("c")
    chips = [(1 - x, y), (x, 1 - y), (1 - x, 1 - y)]
    chip_idx = [2 * cx + cy for cx, cy in chips]
    return x, y, c, 2 * x + y, chips, chip_idx


def _rcopy(src, dst, ssem, rsem, dev):
    return pltpu.make_async_remote_copy(src_ref=src, dst_ref=dst, send_sem=ssem, recv_sem=rsem,
                                        device_id=dev, device_id_type=MESH)


HBM = pl.BlockSpec(memory_space=pltpu.HBM)
SEM = pl.BlockSpec(memory_space=pltpu.SEMAPHORE)
EFFECT = pltpu.SideEffectType.DATAFLOW_SIDE_EFFECTING


def _split_copies(kind, srcs, lands, ssem, rsem):
    _, _, c, me, chips, chip_idx = _place()
    cps = []
    for i in range(len(srcs)):
        for k in range(3):
            if kind == "gather":
                rows = srcs[i].shape[0]
                if rows == lands[i].shape[1]:
                    src, dst = srcs[i], lands[i].at[me]
                else:
                    src, dst = srcs[i], lands[i].at[me, pl.ds(pl.multiple_of(c * rows, 16), rows)]
            else:
                src, dst = srcs[i].at[chip_idx[k]], lands[i].at[k]
            cps.append(_rcopy(src, dst, ssem.at[3 * i + k], rsem.at[3 * i + k], (*chips[k], c)))
    return cps


def _exchange_start(name, kind, srcs, lands, groups):
    n, ng = len(srcs), len(groups)

    def body(*refs):
        src_refs, land_refs = refs[:n], refs[n:2 * n]
        sems = refs[2 * n:2 * n + 2 * ng]
        token = refs[-1]
        for gi, grp in enumerate(groups):
            cps = _split_copies(kind, [src_refs[i] for i in grp], [land_refs[i] for i in grp], sems[2 * gi], sems[2 * gi + 1])
            for cp in cps:
                cp.start()
        token[...] = jnp.zeros_like(token)

    arrays = list(srcs) + list(lands)
    out_shape = []
    for grp in groups:
        out_shape += [pltpu.SemaphoreType.DMA((3 * len(grp),)), pltpu.SemaphoreType.DMA((3 * len(grp),))]
    out_shape += [pltpu.HBM(a.shape, a.dtype) for a in arrays] + [jax.ShapeDtypeStruct((8, LANE), F32)]
    outs = pl.pallas_call(
        body, name=name, out_shape=out_shape, in_specs=[HBM] * (2 * n),
        out_specs=[SEM] * (2 * ng) + [HBM] * (2 * n) + [pl.BlockSpec(memory_space=pltpu.VMEM)],
        input_output_aliases={i: 2 * ng + i for i in range(2 * n)},
        compiler_params=pltpu.CompilerParams(has_side_effects=EFFECT),
    )(*[pltpu.with_memory_space_constraint(a, pltpu.HBM) for a in arrays])
    sems = [(outs[2 * gi], outs[2 * gi + 1]) for gi in range(ng)]
    thru = outs[2 * ng:2 * ng + 2 * n]
    return sems, thru[:n], thru[n:], outs[-1]


def _exchange_wait(name, kind, srcs, lands, sems, after):
    n = len(srcs)

    def body(*refs):
        cps = _split_copies(kind, refs[:n], refs[n:2 * n], refs[2 * n], refs[2 * n + 1])
        for cp in cps:
            cp.wait_send()
            cp.wait_recv()

    arrays = list(srcs) + list(lands)
    outs = pl.pallas_call(
        body, name=name, out_shape=[pltpu.HBM(a.shape, a.dtype) for a in arrays],
        in_specs=[HBM] * (2 * n) + [SEM, SEM, ANY], out_specs=[HBM] * (2 * n),
        input_output_aliases={i: i for i in range(2 * n)},
        compiler_params=pltpu.CompilerParams(has_side_effects=EFFECT),
    )(*arrays, sems[0], sems[1], after)
    return outs[:n], outs[n:]


EXCHANGE_CHUNK_BYTES = 3 * 512 * 1024


def _half_geometry(R, C, axis):
    Rp, Cp = (R // 2, C) if axis == 0 else (R, C // 2)
    tr = _tile_rows(Rp, max(16, EXCHANGE_CHUNK_BYTES // (2 * Cp)), 16)
    return Rp, Cp, tr, Rp // tr


def _pair_sum(name, g, axis):
    G, R, C = g.shape
    Rp, Cp, tr, nb = _half_geometry(R, C, axis)
    steps = G * nb

    def half_block(s, b, h):
        return (s, h * nb + b, 0) if axis == 0 else (s, b, h)

    def body(c_ref, keep_ref, give_ref, out_ref, land, ssem, rsem, credit):
        x, y, c = lax.axis_index("x"), lax.axis_index("y"), lax.axis_index("c")
        sib = (x, y, 1 - c)
        t = pl.program_id(0) * nb + pl.program_id(1)
        slot = t % 2

        @pl.when(t >= 2)
        def _():
            pl.semaphore_wait(credit, 1)

        cp = _rcopy(give_ref.at[0], land.at[slot], ssem.at[slot], rsem.at[slot], sib)
        cp.start()
        cp.wait_recv()
        out_ref[...] = (keep_ref[...].astype(F32) + land[slot].astype(F32)).astype(BF)

        @pl.when(t + 2 < steps)
        def _():
            pl.semaphore_signal(credit, 1, device_id=sib, device_id_type=MESH)

        cp.wait_send()

    blk = (None, tr, Cp)
    grid_spec = pltpu.PrefetchScalarGridSpec(
        num_scalar_prefetch=1, grid=(G, nb),
        in_specs=[pl.BlockSpec(blk, lambda s, b, c_ref: half_block(s, b, c_ref[0])),
                  pl.BlockSpec((1, tr, Cp), lambda s, b, c_ref: half_block(s, b, 1 - c_ref[0]))],
        out_specs=pl.BlockSpec(blk, lambda s, b, c_ref: (s, b, 0)),
        scratch_shapes=[pltpu.VMEM((2, tr, Cp), BF), pltpu.SemaphoreType.DMA((2,)), pltpu.SemaphoreType.DMA((2,)),
                        pltpu.SemaphoreType.REGULAR])
    c_arr = lax.axis_index("c").astype(jnp.int32).reshape(1)
    return pl.pallas_call(
        body, name=name, grid_spec=grid_spec, out_shape=jax.ShapeDtypeStruct((G, Rp, Cp), BF),
        compiler_params=_params("arbitrary", "arbitrary"))(c_arr, g, g)


def _chip_total(name, h, landed):
    G, Rp, Cp = h.shape
    tr = _tile_rows(Rp, max(16, EXCHANGE_CHUNK_BYTES // (2 * Cp)), 16)

    def body(me_ref, own_ref, l0_ref, l1_ref, l2_ref, out_ref):
        acc = own_ref[...].astype(F32)
        for r in (l0_ref, l1_ref, l2_ref):
            acc = acc + r[...].astype(F32)
        out_ref[...] = acc

    blk = (None, tr, Cp)
    grid_spec = pltpu.PrefetchScalarGridSpec(
        num_scalar_prefetch=1, grid=(Rp // tr,),
        in_specs=[pl.BlockSpec(blk, lambda b, me_ref: (me_ref[0], b, 0))]
        + [pl.BlockSpec(blk, functools.partial(lambda b, me_ref, k: (k, b, 0), k=k)) for k in range(3)],
        out_specs=pl.BlockSpec((tr, Cp), lambda b, me_ref: (b, 0)))
    me = (2 * lax.axis_index("x") + lax.axis_index("y")).astype(jnp.int32).reshape(1)
    return pl.pallas_call(
        body, name=name, grid_spec=grid_spec, out_shape=jax.ShapeDtypeStruct((Rp, Cp), F32),
        compiler_params=_params("parallel"))(me, h, landed, landed, landed)


def _pair_join(name, f, axis):
    Rp, Cp = f.shape
    R, C = (2 * Rp, Cp) if axis == 0 else (Rp, 2 * Cp)
    tr = _tile_rows(Rp, max(8, EXCHANGE_CHUNK_BYTES // (4 * Cp)), 8)
    nb = Rp // tr

    def body(f_ref, full, ssem, rsem, lsem):
        x, y, c = lax.axis_index("x"), lax.axis_index("y"), lax.axis_index("c")
        sib = (x, y, 1 - c)
        b = pl.program_id(0)

        def place(h, r0, rows):
            if axis == 0:
                return full.at[pl.ds(pl.multiple_of(h * Rp + r0, 8), rows), :]
            return full.at[pl.ds(pl.multiple_of(r0, 8), rows), pl.ds(pl.multiple_of(h * Cp, LANE), Cp)]

        mine = place(c, b * tr, tr)
        loc = pltpu.make_async_copy(f_ref, mine, lsem)
        rem = _rcopy(f_ref, mine, ssem, rsem, sib)
        loc.start()
        rem.start()
        loc.wait()
        rem.wait_send()

        @pl.when(b == nb - 1)
        def _():
            theirs = place(1 - c, 0, Rp)
            _rcopy(theirs, theirs, ssem, rsem, sib).wait_recv()

    return pl.pallas_call(
        body, name=name, grid=(nb,), in_specs=[pl.BlockSpec((tr, Cp), lambda b: (b, 0))], out_specs=ANY,
        out_shape=jax.ShapeDtypeStruct((R, C), F32),
        scratch_shapes=[pltpu.SemaphoreType.DMA, pltpu.SemaphoreType.DMA, pltpu.SemaphoreType.DMA],
        compiler_params=_params("arbitrary"))(f)


def _pair_share(name, land):
    G, R, C = land.shape
    Rh = R // 2
    tr = _tile_rows(Rh, max(16, EXCHANGE_CHUNK_BYTES // (2 * C)), 16)
    chunks = [(k, b) for k in range(3) for b in range(Rh // tr)]

    def body(src, dst, buf, lsem, ssem, rsem):
        x, y, c, _, _, chip_idx = _place()
        sib = (x, y, 1 - c)

        def region(ref, k, half, r0, rows):
            return ref.at[chip_idx[k], pl.ds(pl.multiple_of(half * Rh + r0, 16), rows)]

        def load(t):
            k, b = chunks[t]
            return pltpu.make_async_copy(region(src, k, c, b * tr, tr), buf.at[t % 2], lsem.at[t % 2])

        def send(t):
            k, b = chunks[t]
            return _rcopy(buf.at[t % 2], region(dst, k, c, b * tr, tr), ssem.at[t % 2], rsem.at[k], sib)

        load(0).start()
        for t in range(len(chunks)):
            load(t).wait()
            if t + 1 < len(chunks):
                if t >= 1:
                    send(t - 1).wait_send()
                load(t + 1).start()
            send(t).start()
        for t in range(max(0, len(chunks) - 2), len(chunks)):
            send(t).wait_send()
        for k in range(3):
            theirs = region(dst, k, 1 - c, 0, Rh)
            _rcopy(theirs, theirs, ssem.at[0], rsem.at[k], sib).wait_recv()

    return pl.pallas_call(
        body, name=name, in_specs=[ANY], out_specs=ANY, out_shape=jax.ShapeDtypeStruct(land.shape, land.dtype),
        input_output_aliases={0: 0},
        scratch_shapes=[pltpu.VMEM((2, tr, C), land.dtype), pltpu.SemaphoreType.DMA((2,)), pltpu.SemaphoreType.DMA((2,)),
                        pltpu.SemaphoreType.DMA((3,))],
    )(land)


def _allreduce_small(v):
    R, K = v.shape
    ndev = 8

    def body(v_ref, o_ref, land, ssem, rsem):
        x, y, c = lax.axis_index("x"), lax.axis_index("y"), lax.axis_index("c")
        me = 4 * x + 2 * y + c
        land[me] = v_ref[...]
        cps = []
        for r in range(1, ndev):
            fx, fy, fc = (r >> 2) & 1, (r >> 1) & 1, r & 1
            peer = (x ^ fx, y ^ fy, c ^ fc)
            cp = _rcopy(v_ref, land.at[me], ssem.at[r - 1], rsem.at[r - 1], peer)
            cp.start()
            cps.append((cp, 4 * peer[0] + 2 * peer[1] + peer[2], r))
        for cp, src, r in cps:
            cp.wait_send()
            _rcopy(v_ref, land.at[src], ssem.at[r - 1], rsem.at[r - 1], (x, y, c)).wait_recv()
        acc = land[0]
        for d in range(1, ndev):
            acc = acc + land[d]
        o_ref[...] = acc

    vm = pl.BlockSpec(memory_space=pltpu.VMEM)
    return pl.pallas_call(
        body, name="allreduce_small", in_specs=[vm], out_specs=vm, out_shape=jax.ShapeDtypeStruct((R, K), F32),
        scratch_shapes=[pltpu.VMEM((ndev, R, K), F32), pltpu.SemaphoreType.DMA((ndev - 1,)), pltpu.SemaphoreType.DMA((ndev - 1,))],
    )(v)


IN_SPLITS = (Q_RANK, KV_RANK, QK_ROPE, DIL_HEADS * HEAD, DIL_HEADS * HEAD, DIL_HEADS * HEAD, D_MODEL, D_MODEL)
IN_OFF = tuple(int(v) for v in np.cumsum((0,) + IN_SPLITS))


def _unshard_cols(g):
    G, K, Ns = g.shape
    return g.transpose(1, 0, 2).reshape(K, G * Ns)


def _shard_cols(w):
    K, N = w.shape
    return w.reshape(K, N_CHIPS, N // N_CHIPS).transpose(1, 0, 2)


def _rope_pad(w):
    half = QK_ROPE // 2
    z = jnp.zeros(w.shape[:-1] + (half,), w.dtype)
    return jnp.concatenate([w[..., :half], z, w[..., half:], z], axis=-1)


def _rope_unpad(w):
    half = QK_ROPE // 2
    return jnp.concatenate([w[..., :half], w[..., 2 * half:3 * half]], axis=-1)


def _split_w_in(w_in_g):
    w = _unshard_cols(w_in_g)
    K = w.shape[0]
    p = [w[:, IN_OFF[i]:IN_OFF[i + 1]] for i in range(8)]
    w_lat = jnp.concatenate([p[0], p[1], _rope_pad(p[2]), jnp.zeros((K, LAT_W - _KPE.stop), w.dtype)], axis=1)
    w_dil = [jnp.concatenate([p[3 + t][:, g * DIL_O:(g + 1) * DIL_O] for t in range(3)], axis=1) for g in range(DIL_GROUPS)]
    w_gate = jnp.concatenate([p[6], p[7]], axis=1)
    return w_lat, w_dil, w_gate


def _merge_dw_in(dw_lat, dw_dil, dw_gate):
    parts = [dw_lat[:, _CQ], dw_lat[:, _CKV], _rope_unpad(dw_lat[:, _KPE])]
    for t in range(3):
        parts += [dw_dil[g][:, t * DIL_O:(t + 1) * DIL_O] for g in range(DIL_GROUPS)]
    parts.append(dw_gate)
    return _shard_cols(jnp.concatenate(parts, axis=1))


def _split_w_uq(w_uq_g):
    w = _unshard_cols(w_uq_g)
    K = w.shape[0]
    w = w.reshape(K, MLA_HEADS, QK_NOPE + QK_ROPE)
    return w[:, :, :QK_NOPE].reshape(K, MLA_HEADS * HEAD), _rope_pad(w[:, :, QK_NOPE:]).reshape(K, MLA_HEADS * HEAD)


def _merge_dw_uq(dw_n, dw_p):
    K = dw_n.shape[0]
    w = jnp.concatenate([dw_n.reshape(K, MLA_HEADS, HEAD), _rope_unpad(dw_p.reshape(K, MLA_HEADS, HEAD))], axis=-1)
    return _shard_cols(w.reshape(K, MLA_HEADS * (QK_NOPE + QK_ROPE)))


def _split_w_ukv(w_ukv_g):
    w = _unshard_cols(w_ukv_g)
    K = w.shape[0]
    w = w.reshape(K, MLA_HEADS, 2 * HEAD)
    return w[:, :, :HEAD].reshape(K, MLA_HEADS * HEAD), w[:, :, HEAD:].reshape(K, MLA_HEADS * HEAD)


def _merge_dw_ukv(dw_k, dw_v):
    K = dw_k.shape[0]
    w = jnp.concatenate([dw_k.reshape(K, MLA_HEADS, HEAD), dw_v.reshape(K, MLA_HEADS, HEAD)], axis=-1)
    return _shard_cols(w.reshape(K, MLA_HEADS * 2 * HEAD))


GATHER_GROUPS = (("w_in",), ("w_uq", "w_ukv", "w_o_mla", "w_o_dil", "w_out"), ("w_up", "w_down", "conv_w"))
SHARED_FETCH = ("w_in",)
REDUCE_GROUPS = (("w_down", "w_up"), ("w_out", "w_o_mla", "w_o_dil"), ("w_uq", "w_ukv", "w_in"))


def _local_step(x, tgt, W, fetch, emit):
    S, D = x.shape
    cos, sin_s = _rope_tables(S)
    w_lat, w_dil, w_gate = _split_w_in(fetch(0, x)["w_in"])

    h = _rmsnorm_fwd("attn_norm", x, W["attn_norm_g"])
    lat = _mm_nn("proj_lat", h, w_lat)
    qkv = [_mm_nn(f"proj_dil{g}", h, w_dil[g], o_dtype=BF) for g in range(DIL_GROUPS)]
    gpre = _mm_nn("proj_gate", h, w_gate)
    WB = fetch(1, gpre)
    w_uqn, w_uqp = _split_w_uq(WB["w_uq"])
    w_k, w_v = _split_w_ukv(WB["w_ukv"])
    w_o_mla, w_o_dil = WB["w_o_mla"], WB["w_o_dil"]
    w_out = WB["w_out"].reshape(D, D)
    qn_, kvn, kpe = _mla_prep(lat, W["q_norm_g"], W["kv_norm_g"], cos, sin_s)
    q_nope = _mm_nn("q_nope", qn_, w_uqn, o_dtype=BF)
    q_pe = _rope("q_rope", _mm_nn("q_pe", qn_, w_uqp), cos, sin_s, False)
    k_nope = _mm_nn("k_nope", kvn, w_k, o_dtype=BF)
    v_mla = _mm_nn("v_mla", kvn, w_v, o_dtype=BF)
    attn_a, lse_a = _mla_fwd(q_nope, q_pe, k_nope, kpe, v_mla)
    dil = [_dil_fwd(qkv[g], g) for g in range(DIL_GROUPS)]
    attn_b, lse_b = _dil_combine([o for o, _ in dil], [l for _, l in dil])
    o_a = _mm_nn("o_mla", attn_a, w_o_mla)
    o_b = _mm_nn("o_dil", attn_b, w_o_dil)
    merge = _merge_fwd(gpre, W["b_gate"], o_a, o_b)
    x1 = _mm_nn("out_proj", merge, w_out, add=x)
    WC = fetch(2, merge)
    w_up = WC["w_up"]
    G4, _, C = w_up.shape
    w_down = WC["w_down"].reshape(G4 // 2, C, D)
    conv_w = WC["conv_w"]
    conv_b = W["conv_b"].reshape(G4, 1, C)
    h2 = _rmsnorm_fwd("ffn_norm", x1, W["ffn_norm_g"])
    u_pre = _up_fwd(h2, w_up)
    act = _ffn_act(u_pre, conv_w, conv_b)
    x2 = _down_fwd(act, w_down, x1)
    dx2, d_final_g, loss8 = _final_loss(x2, tgt, W["final_norm_g"])

    d_act = _down_dgrad(dx2, w_down)
    dw_down = _down_wgrad(act, dx2)
    du = _ffn_act_bwd(u_pre, conv_w, conv_b, d_act)
    du_pre, d_conv_w, d_conv_b = _conv_bwd(du, u_pre, conv_w)
    dh2 = _up_dgrad(du_pre, w_up)
    dw_up = _up_wgrad(h2, du_pre)
    zero = emit(0, {"w_down": dw_down.reshape(N_CHIPS, (G4 // 2) * C // N_CHIPS, D), "w_up": dw_up})
    dx1, d_ffn_g = _rmsnorm_bwd("ffn_norm_bwd", dh2, x1, W["ffn_norm_g"] + zero, dx2)
    d_merge = _mm_nt("out_proj_dgrad", dx1, w_out)
    dw_out = _mm_tn("out_proj_wgrad", merge, dx1)
    d_oa, d_ob, d_gpre, d_b_gate = _merge_bwd(d_merge, gpre, W["b_gate"], o_a, o_b)
    d_attn_a = _mm_nt("o_mla_dgrad", d_oa, w_o_mla, o_dtype=BF)
    dw_o_mla = _mm_tn("o_mla_wgrad", attn_a, d_oa, shards=N_CHIPS)
    d_attn_b = _mm_nt("o_dil_dgrad", d_ob, w_o_dil, o_dtype=BF)
    dw_o_dil = _mm_tn("o_dil_wgrad", attn_b, d_ob, shards=N_CHIPS)
    zero = emit(1, {"w_out": dw_out.reshape(N_CHIPS, D // N_CHIPS, D), "w_o_mla": dw_o_mla, "w_o_dil": dw_o_dil})
    q_norm_g = W["q_norm_g"] + zero
    delta_b = _dil_delta(d_attn_b, attn_b)
    d_qkv = [_dil_bwd(qkv[g], d_attn_b, lse_b, delta_b, g) for g in range(DIL_GROUPS)]
    dq_nope, dq_pe_rot = _mla_bwd_dq(q_nope, q_pe, k_nope, kpe, v_mla, d_attn_a, attn_a, lse_a)
    dk_nope, dv_mla, dkpe_rot = _mla_bwd_dkv(q_nope, q_pe, k_nope, kpe, v_mla, d_attn_a, attn_a, lse_a)
    dq_pe = _rope("q_rope_bwd", dq_pe_rot, cos, sin_s, True)
    d_qn = _mm_nt("q_pe_dgrad", dq_pe, w_uqp, add=_mm_nt("q_nope_dgrad", dq_nope, w_uqn))
    d_kvn = _mm_nt("v_dgrad", dv_mla, w_v, add=_mm_nt("k_nope_dgrad", dk_nope, w_k))
    dw_uq = _merge_dw_uq(_mm_tn("q_nope_wgrad", qn_, dq_nope), _mm_tn("q_pe_wgrad", qn_, dq_pe))
    dw_ukv = _merge_dw_ukv(_mm_tn("k_nope_wgrad", kvn, dk_nope), _mm_tn("v_wgrad", kvn, dv_mla))
    d_lat, d_q_g, d_kv_g = _mla_prep_bwd(lat, q_norm_g, W["kv_norm_g"], cos, sin_s, d_qn, d_kvn, dkpe_rot)
    dw_in = _merge_dw_in(_mm_tn("proj_lat_wgrad", h, d_lat),
                         [_mm_tn(f"proj_dil{g}_wgrad", h, d_qkv[g]) for g in range(DIL_GROUPS)],
                         _mm_tn("proj_gate_wgrad", h, d_gpre))
    zero = emit(2, {"w_uq": dw_uq, "w_ukv": dw_ukv, "w_in": dw_in})
    dh = _mm_nt("proj_lat_dgrad", d_lat, w_lat + zero.astype(BF))
    for g in range(DIL_GROUPS):
        dh = _mm_nt(f"proj_dil{g}_dgrad", d_qkv[g], w_dil[g], add=dh)
    dh = _mm_nt("proj_gate_dgrad", d_gpre, w_gate, add=dh)
    grad_x, d_attn_g = _rmsnorm_bwd("attn_norm_bwd", dh, x, W["attn_norm_g"], dx1)

    small = {"attn_norm_g": d_attn_g, "b_gate": d_b_gate, "q_norm_g": d_q_g, "kv_norm_g": d_kv_g,
             "ffn_norm_g": d_ffn_g, "conv_w": d_conv_w, "conv_b": d_conv_b.reshape(1, G4 * C),
             "final_norm_g": d_final_g}
    return loss8[0, 0], grad_x, small


BIG = ("w_in", "w_uq", "w_ukv", "w_o_mla", "w_o_dil", "w_out", "w_up", "w_down")
SMALL = ("attn_norm_g", "b_gate", "q_norm_g", "kv_norm_g", "ffn_norm_g", "conv_w", "conv_b", "final_norm_g")
WEIGHTS = ("attn_norm_g", "w_in", "b_gate", "q_norm_g", "w_uq", "kv_norm_g", "w_ukv", "w_o_mla", "w_o_dil",
           "w_out", "ffn_norm_g", "w_up", "conv_w", "conv_b", "w_down", "final_norm_g")
SMALL_ROWS = 8
HALF_AXIS = {"w_down": 1}


def _gather_start(shards):
    names = [n for grp in GATHER_GROUPS for n in grp]
    chip = 2 * lax.axis_index("x") + lax.axis_index("y")
    c = lax.axis_index("c")
    srcs, lands = [], []
    for n in names:
        s = shards[n]
        lands.append(lax.dynamic_update_slice(lax.empty((N_CHIPS,) + s.shape, s.dtype), s[None], (chip, 0, 0)))
        if n in SHARED_FETCH:
            s = lax.dynamic_slice_in_dim(s, c * (s.shape[0] // 2), s.shape[0] // 2, 0)
        srcs.append(s)
    groups, at = [], 0
    for grp in GATHER_GROUPS:
        groups.append(list(range(at, at + len(grp))))
        at += len(grp)
    sems, srcs, lands, token = _exchange_start("gather_start", "gather", srcs, lands, groups)

    def fetch(i, after):
        idx = groups[i]
        _, got = _exchange_wait(f"gather_wait{i}", "gather", [srcs[j] for j in idx], [lands[j] for j in idx], sems[i], after)
        return {n: _pair_share(f"pair_share_{n}", g) if n in SHARED_FETCH else g for n, g in zip(GATHER_GROUPS[i], got)}

    return fetch, token[0, 0]


def _reduce_start(i, grads):
    names = REDUCE_GROUPS[i]
    hs = [_pair_sum(f"pair_sum_{n}", grads[n], HALF_AXIS.get(n, 0)) for n in names]
    lands = [lax.empty((3,) + h.shape[1:], h.dtype) for h in hs]
    sems, hs, lands, token = _exchange_start(f"reduce_start{i}", "scatter", hs, lands, [list(range(len(names)))])
    return (sems[0], hs, lands), token[0, 0]


def _reduce_finish(i, pending, after):
    sems, hs, lands = pending
    hs, lands = _exchange_wait(f"reduce_wait{i}", "scatter", hs, lands, sems, after)
    out = {}
    for n, h, landed in zip(REDUCE_GROUPS[i], hs, lands):
        out[n] = _pair_join(f"pair_join_{n}", _chip_total(f"chip_total_{n}", h, landed), HALF_AXIS.get(n, 0))
    return out


def _reduce_small(small):
    flat = [small[n].reshape(-1) for n in SMALL]
    sizes = [f.shape[0] for f in flat]
    total = sum(sizes)
    width = -(-total // (SMALL_ROWS * LANE)) * LANE
    packed = jnp.concatenate(flat + [jnp.zeros((SMALL_ROWS * width - total,), F32)]).reshape(SMALL_ROWS, width)
    red = _allreduce_small(packed).reshape(-1)
    out, off = {}, 0
    for n, s in zip(SMALL, sizes):
        out[n] = red[off:off + s]
        off += s
    return out


def kernel(x, attn_norm_g, w_in, b_gate, q_norm_g, w_uq, kv_norm_g, w_ukv, w_o_mla, w_o_dil, w_out, ffn_norm_g, w_up, conv_w, conv_b, w_down, final_norm_g, loss_target, m_attn_norm_g, m_w_in, m_b_gate, m_q_norm_g, m_w_uq, m_kv_norm_g, m_w_ukv, m_w_o_mla, m_w_o_dil, m_w_out, m_ffn_norm_g, m_w_up, m_conv_w, m_conv_b, m_w_down, m_final_norm_g, v_attn_norm_g, v_w_in, v_b_gate, v_q_norm_g, v_w_uq, v_kv_norm_g, v_w_ukv, v_w_o_mla, v_w_o_dil, v_w_out, v_ffn_norm_g, v_w_up, v_conv_w, v_conv_b, v_w_down, v_final_norm_g):
    given = dict(attn_norm_g=attn_norm_g, w_in=w_in, b_gate=b_gate, q_norm_g=q_norm_g, w_uq=w_uq, kv_norm_g=kv_norm_g,
                 w_ukv=w_ukv, w_o_mla=w_o_mla, w_o_dil=w_o_dil, w_out=w_out, ffn_norm_g=ffn_norm_g, w_up=w_up,
                 conv_w=conv_w, conv_b=conv_b, w_down=w_down, final_norm_g=final_norm_g)
    moments_m = dict(attn_norm_g=m_attn_norm_g, w_in=m_w_in, b_gate=m_b_gate, q_norm_g=m_q_norm_g, w_uq=m_w_uq,
                     kv_norm_g=m_kv_norm_g, w_ukv=m_w_ukv, w_o_mla=m_w_o_mla, w_o_dil=m_w_o_dil, w_out=m_w_out,
                     ffn_norm_g=m_ffn_norm_g, w_up=m_w_up, conv_w=m_conv_w, conv_b=m_conv_b, w_down=m_w_down,
                     final_norm_g=m_final_norm_g)
    moments_v = dict(attn_norm_g=v_attn_norm_g, w_in=v_w_in, b_gate=v_b_gate, q_norm_g=v_q_norm_g, w_uq=v_w_uq,
                     kv_norm_g=v_kv_norm_g, w_ukv=v_w_ukv, w_o_mla=v_w_o_mla, w_o_dil=v_w_o_dil, w_out=v_w_out,
                     ffn_norm_g=v_ffn_norm_g, w_up=v_w_up, conv_w=v_conv_w, conv_b=v_conv_b, w_down=v_w_down,
                     final_norm_g=v_final_norm_g)

    shards = {n: given[n][0].astype(BF) for n in BIG}
    shards["conv_w"] = given["conv_w"][0]
    fetch, zero = _gather_start(shards)
    W = {n: given[n] for n in ("b_gate", "q_norm_g", "kv_norm_g", "ffn_norm_g", "conv_b")}
    W["attn_norm_g"] = given["attn_norm_g"] + zero
    W["final_norm_g"] = given["final_norm_g"].reshape(1, -1)

    pending = {}

    def emit(i, grads):
        pending[i], token = _reduce_start(i, grads)
        return token

    loss_part, grad_x, small = _local_step(x[0], loss_target[0], W, fetch, emit)
    loss = lax.psum(loss_part, ("x", "y", "c"))
    grads, delta, new_m, new_v = {}, {}, {}, {}

    def adamw(n, g):
        shp = given[n].shape
        two_d = (-1, shp[-1]) if len(shp) > 1 else (1, -1)
        go, d, nm, nv = _adamw(f"adamw_{n}", given[n].reshape(two_d), g.reshape(two_d),
                               moments_m[n].reshape(two_d), moments_v[n].reshape(two_d))
        grads[n], delta[n], new_m[n], new_v[n] = go.reshape(shp), d.reshape(shp), nm.reshape(shp), nv.reshape(shp)

    after = grad_x
    for i in range(len(REDUCE_GROUPS)):
        for n, g in _reduce_finish(i, pending[i], after).items():
            adamw(n, g)
        after = delta[REDUCE_GROUPS[i][-1]]
    g_small = _reduce_small(small)
    chip = 2 * lax.axis_index("x") + lax.axis_index("y")
    for n in SMALL:
        if n == "conv_w":
            full = g_small[n].reshape(N_CHIPS, 3, -1)
            adamw(n, lax.dynamic_index_in_dim(full, chip, 0, keepdims=True))
        else:
            adamw(n, g_small[n])

    return (loss, grad_x[None], *[grads[n] for n in WEIGHTS], *[delta[n] for n in WEIGHTS],
            *[new_m[n] for n in WEIGHTS], *[new_v[n] for n in WEIGHTS])
```

```python
import functools
import math

import numpy as np
import jax
import jax.numpy as jnp
from jax import lax
from jax.experimental import pallas as pl
from jax.experimental.pallas import tpu as pltpu

F32 = jnp.float32
BF = jnp.bfloat16
MESH = pl.DeviceIdType.MESH

D_MODEL = 2048
MLA_HEADS = 8
QK_NOPE = 128
QK_ROPE = 64
Q_RANK = 512
KV_RANK = 256
ROPE_THETA = 10000.0
DIL_PATTERNS = ((128, 1), (512, 4), (2048, 16))
DIL_GROUPS = 3
DIL_HPG = 4
DIL_HEADS = 12
HEAD = 128
DIL_BLOCK = 128
ALIBI_MAX_BIAS = 8.0
NORM_EPS = 1e-6
N_CHIPS = 4
ADAM_LR = 0.001
ADAM_B1 = 0.9
ADAM_B2 = 0.999
ADAM_EPS = 1e-08
ADAM_WD = 0.01
ADAM_STEP = 10

LANE = 128
VMEM_LIMIT = 56 * 1024 * 1024
MLA_SCALE = (QK_NOPE + QK_ROPE) ** -0.5
DIL_SCALE = HEAD ** -0.5


def _params(*sem):
    return pltpu.CompilerParams(dimension_semantics=sem, vmem_limit_bytes=VMEM_LIMIT)


def _tile(n, pref):
    t = (pref // LANE) * LANE
    while t >= LANE:
        if n % t == 0:
            return t
        t -= LANE
    return n


NN = (((1,), (0,)), ((), ()))
NT = (((1,), (1,)), ((), ()))
TN = (((0,), (0,)), ((), ()))


def _mm_call(name, a, b, add, *, grid, a_spec, b_spec, add_spec, o_spec, o_shape, o_dtype, acc_shape, dims, nk):
    nax = len(grid)

    def body(*refs):
        if add is None:
            a_ref, b_ref, o_ref = refs[:3]
            c_ref = None
            scr = refs[3:]
        else:
            a_ref, b_ref, c_ref, o_ref = refs[:4]
            scr = refs[4:]
        prod = lax.dot_general(a_ref[...].astype(BF), b_ref[...].astype(BF), dims, preferred_element_type=F32)
        if nk == 1:
            if c_ref is not None:
                prod = prod + c_ref[...]
            o_ref[...] = prod.astype(o_ref.dtype)
        else:
            acc = scr[0]
            k = pl.program_id(nax - 1)

            @pl.when(k == 0)
            def _():
                if c_ref is not None:
                    acc[...] = prod + c_ref[...]
                else:
                    acc[...] = prod

            @pl.when(k > 0)
            def _():
                acc[...] += prod

            @pl.when(k == nk - 1)
            def _():
                o_ref[...] = acc[...].astype(o_ref.dtype)

    ins = [a, b] + ([] if add is None else [add])
    specs = [a_spec, b_spec] + ([] if add is None else [add_spec])
    sem = ("parallel",) * (nax - 1) + ("arbitrary",)
    return pl.pallas_call(
        body, name=name, grid=grid, in_specs=specs, out_specs=o_spec,
        out_shape=jax.ShapeDtypeStruct(o_shape, o_dtype),
        scratch_shapes=[] if nk == 1 else [pltpu.VMEM(acc_shape, F32)],
        compiler_params=_params(*sem),
    )(*ins)


def _mm_nn(name, a, b, *, add=None, o_dtype=F32):
    M, K = a.shape
    sharded = b.ndim == 3
    Ns = b.shape[-1]
    N = Ns * (b.shape[0] if sharded else 1)
    tm, tn, tk = _tile(M, 1024), _tile(Ns, 1024), _tile(K, 2048)
    per = Ns // tn
    nk = K // tk
    if sharded:
        b_spec = pl.BlockSpec((None, tk, tn), lambda i, j, k: (j // per, k, j % per))
    else:
        b_spec = pl.BlockSpec((tk, tn), lambda i, j, k: (k, j))
    return _mm_call(
        name, a, b, add, grid=(M // tm, N // tn, nk),
        a_spec=pl.BlockSpec((tm, tk), lambda i, j, k: (i, k)), b_spec=b_spec,
        add_spec=pl.BlockSpec((tm, tn), lambda i, j, k: (i, j)),
        o_spec=pl.BlockSpec((tm, tn), lambda i, j, k: (i, j)),
        o_shape=(M, N), o_dtype=o_dtype, acc_shape=(tm, tn), dims=NN, nk=nk)


def _mm_nt(name, a, b, *, add=None, o_dtype=F32):
    M, K = a.shape
    sharded = b.ndim == 3
    N, Ks = b.shape[-2], b.shape[-1]
    tm, tn, tk = _tile(M, 1024), _tile(N, 1024), _tile(Ks, 2048)
    per = Ks // tk
    nk = K // tk
    if sharded:
        b_spec = pl.BlockSpec((None, tn, tk), lambda i, j, k: (k // per, j, k % per))
    else:
        b_spec = pl.BlockSpec((tn, tk), lambda i, j, k: (j, k))
    return _mm_call(
        name, a, b, add, grid=(M // tm, N // tn, nk),
        a_spec=pl.BlockSpec((tm, tk), lambda i, j, k: (i, k)), b_spec=b_spec,
        add_spec=pl.BlockSpec((tm, tn), lambda i, j, k: (i, j)),
        o_spec=pl.BlockSpec((tm, tn), lambda i, j, k: (i, j)),
        o_shape=(M, N), o_dtype=o_dtype, acc_shape=(tm, tn), dims=NT, nk=nk)


def _mm_tn(name, a, b, *, shards=1, o_dtype=BF):
    S, M = a.shape
    N = b.shape[1]
    Ns = N // shards
    tm, tn, tk = _tile(M, 1024), _tile(Ns, 1024), _tile(S, 2048)
    per = Ns // tn
    nk = S // tk
    if shards > 1:
        o_spec = pl.BlockSpec((None, tm, tn), lambda i, j, k: (j // per, i, j % per))
        o_shape = (shards, M, Ns)
    else:
        o_spec = pl.BlockSpec((tm, tn), lambda i, j, k: (i, j))
        o_shape = (M, N)
    return _mm_call(
        name, a, b, None, grid=(M // tm, N // tn, nk),
        a_spec=pl.BlockSpec((tk, tm), lambda i, j, k: (k, i)),
        b_spec=pl.BlockSpec((tk, tn), lambda i, j, k: (k, j)),
        add_spec=None, o_spec=o_spec, o_shape=o_shape, o_dtype=o_dtype, acc_shape=(tm, tn), dims=TN, nk=nk)


def _up_fwd(h2, w_up):
    S, D = h2.shape
    G, _, C = w_up.shape
    tm = _tile(S, 512)
    return _mm_call(
        "up_fwd", h2, w_up, None, grid=(G, S // tm, 1),
        a_spec=pl.BlockSpec((tm, D), lambda g, i, k: (i, 0)),
        b_spec=pl.BlockSpec((None, D, C), lambda g, i, k: (g, 0, 0)),
        add_spec=None, o_spec=pl.BlockSpec((None, tm, C), lambda g, i, k: (g, i, 0)),
        o_shape=(G, S, C), o_dtype=BF, acc_shape=None, dims=NN, nk=1)


def _up_dgrad(du_pre, w_up):
    G, S, C = du_pre.shape
    D = w_up.shape[1]
    tm, tn = _tile(S, 1024), _tile(D, 1024)
    return _mm_call(
        "up_dgrad", du_pre, w_up, None, grid=(S // tm, D // tn, G),
        a_spec=pl.BlockSpec((None, tm, C), lambda i, j, g: (g, i, 0)),
        b_spec=pl.BlockSpec((None, tn, C), lambda i, j, g: (g, j, 0)),
        add_spec=None, o_spec=pl.BlockSpec((tm, tn), lambda i, j, g: (i, j)),
        o_shape=(S, D), o_dtype=F32, acc_shape=(tm, tn), dims=NT, nk=G)


def _up_wgrad(h2, du_pre):
    G, S, C = du_pre.shape
    D = h2.shape[1]
    tm, tk = _tile(D, 512), _tile(S, 2048)
    return _mm_call(
        "up_wgrad", h2, du_pre, None, grid=(G, D // tm, S // tk),
        a_spec=pl.BlockSpec((tk, tm), lambda g, i, k: (k, i)),
        b_spec=pl.BlockSpec((None, tk, C), lambda g, i, k: (g, k, 0)),
        add_spec=None, o_spec=pl.BlockSpec((None, tm, C), lambda g, i, k: (g, i, 0)),
        o_shape=(G, D, C), o_dtype=BF, acc_shape=(tm, C), dims=TN, nk=S // tk)


def _down_fwd(act, w_down, x1):
    G, S, C = act.shape
    D = w_down.shape[2]
    tm, tn = _tile(S, 1024), _tile(D, 1024)
    return _mm_call(
        "down_fwd", act, w_down, x1, grid=(S // tm, D // tn, G),
        a_spec=pl.BlockSpec((None, tm, C), lambda i, j, g: (g, i, 0)),
        b_spec=pl.BlockSpec((None, C, tn), lambda i, j, g: (g, 0, j)),
        add_spec=pl.BlockSpec((tm, tn), lambda i, j, g: (i, j)),
        o_spec=pl.BlockSpec((tm, tn), lambda i, j, g: (i, j)),
        o_shape=(S, D), o_dtype=F32, acc_shape=(tm, tn), dims=NN, nk=G)


def _down_dgrad(dx2, w_down):
    S, D = dx2.shape
    G, C, _ = w_down.shape
    tm = _tile(S, 512)
    return _mm_call(
        "down_dgrad", dx2, w_down, None, grid=(G, S // tm, 1),
        a_spec=pl.BlockSpec((tm, D), lambda g, i, k: (i, 0)),
        b_spec=pl.BlockSpec((None, C, D), lambda g, i, k: (g, 0, 0)),
        add_spec=None, o_spec=pl.BlockSpec((None, tm, C), lambda g, i, k: (g, i, 0)),
        o_shape=(G, S, C), o_dtype=BF, acc_shape=None, dims=NT, nk=1)


def _down_wgrad(act, dx2):
    G, S, C = act.shape
    D = dx2.shape[1]
    tn, tk = _tile(D, 512), _tile(S, 1024)
    return _mm_call(
        "down_wgrad", act, dx2, None, grid=(G, D // tn, S // tk),
        a_spec=pl.BlockSpec((None, tk, C), lambda g, j, k: (g, k, 0)),
        b_spec=pl.BlockSpec((tk, tn), lambda g, j, k: (k, j)),
        add_spec=None, o_spec=pl.BlockSpec((None, C, tn), lambda g, j, k: (g, 0, j)),
        o_shape=(G, C, D), o_dtype=BF, acc_shape=(C, tn), dims=TN, nk=S // tk)


def _row(ts, c):
    return pl.BlockSpec((ts, c), lambda i: (i, 0))


def _bcast(r, c):
    return pl.BlockSpec((r, c), lambda i: (0, 0))


def _accumulate(i, ref, val):
    @pl.when(i == 0)
    def _():
        ref[...] = val

    @pl.when(i > 0)
    def _():
        ref[...] += val


def _rstd(xv):
    return lax.rsqrt(jnp.mean(xv * xv, axis=-1, keepdims=True) + NORM_EPS)


def _rmsnorm_fwd(name, x, g):
    S, D = x.shape
    ts = _tile(S, 512)

    def body(x_ref, g_ref, o_ref):
        xv = x_ref[...]
        o_ref[...] = (xv * _rstd(xv) * g_ref[...]).astype(o_ref.dtype)

    return pl.pallas_call(
        body, name=name, grid=(S // ts,), in_specs=[_row(ts, D), _bcast(1, D)], out_specs=_row(ts, D),
        out_shape=jax.ShapeDtypeStruct((S, D), BF), compiler_params=_params("parallel"))(x, g)


def _norm_bwd_rows(dy, xv, g):
    r = _rstd(xv)
    xh = xv * r
    dxh = dy * g
    dx = r * (dxh - xh * jnp.mean(dxh * xh, axis=-1, keepdims=True))
    return dx, jnp.sum(dy * xh, axis=0, keepdims=True)


def _rmsnorm_bwd(name, dy, x, g, res):
    S, D = x.shape
    ts = _tile(S, 512)

    def body(dy_ref, x_ref, g_ref, res_ref, dx_ref, dg_ref):
        dx, dg = _norm_bwd_rows(dy_ref[...], x_ref[...], g_ref[...])
        dx_ref[...] = dx + res_ref[...]
        _accumulate(pl.program_id(0), dg_ref, dg)

    return pl.pallas_call(
        body, name=name, grid=(S // ts,),
        in_specs=[_row(ts, D), _row(ts, D), _bcast(1, D), _row(ts, D)],
        out_specs=[_row(ts, D), _bcast(1, D)],
        out_shape=[jax.ShapeDtypeStruct((S, D), F32), jax.ShapeDtypeStruct((1, D), F32)],
        compiler_params=_params("arbitrary"))(dy, x, g, res)


def _rope_tables(S):
    half = QK_ROPE // 2
    pos = jnp.arange(S, dtype=F32)
    inv_freq = ROPE_THETA ** (-jnp.arange(0, QK_ROPE, 2, dtype=F32) / QK_ROPE)
    ang = pos[:, None] * inv_freq[None, :]
    cos, sin = jnp.cos(ang), jnp.sin(ang)
    z = jnp.zeros((S, half), F32)
    return jnp.concatenate([cos, z, cos, z], axis=1), jnp.concatenate([-sin, z, sin, z], axis=1)


def _rope_lanes(x, cos, sin_signed, inverse):
    if inverse:
        return x * cos + pltpu.roll(x * sin_signed, LANE // 2, 1)
    return x * cos + pltpu.roll(x, LANE // 2, 1) * sin_signed


def _rope(name, x, cos, sin_signed, inverse):
    S, W = x.shape
    ts = _tile(S, 512)

    def body(x_ref, c_ref, s_ref, o_ref):
        c, s = c_ref[...], s_ref[...]
        for h in range(W // LANE):
            sl = slice(h * LANE, (h + 1) * LANE)
            o_ref[:, sl] = _rope_lanes(x_ref[:, sl], c, s, inverse).astype(o_ref.dtype)

    return pl.pallas_call(
        body, name=name, grid=(S // ts,), in_specs=[_row(ts, W), _row(ts, LANE), _row(ts, LANE)],
        out_specs=_row(ts, W), out_shape=jax.ShapeDtypeStruct((S, W), BF),
        compiler_params=_params("parallel"))(x, cos, sin_signed)


LAT_W = 1024
_CQ = slice(0, Q_RANK)
_CKV = slice(Q_RANK, Q_RANK + KV_RANK)
_KPE = slice(Q_RANK + KV_RANK, Q_RANK + KV_RANK + LANE)


def _mla_prep(lat, qg, kvg, cos, sin_signed):
    S = lat.shape[0]
    ts = _tile(S, 512)

    def body(lat_ref, qg_ref, kvg_ref, c_ref, s_ref, qn_ref, kvn_ref, kpe_ref):
        cq = lat_ref[:, _CQ]
        qn_ref[...] = (cq * _rstd(cq) * qg_ref[...]).astype(BF)
        ckv = lat_ref[:, _CKV]
        kvn_ref[...] = (ckv * _rstd(ckv) * kvg_ref[...]).astype(BF)
        kpe_ref[...] = _rope_lanes(lat_ref[:, _KPE], c_ref[...], s_ref[...], False).astype(BF)

    return pl.pallas_call(
        body, name="mla_prep", grid=(S // ts,),
        in_specs=[_row(ts, LAT_W), _bcast(1, Q_RANK), _bcast(1, KV_RANK), _row(ts, LANE), _row(ts, LANE)],
        out_specs=[_row(ts, Q_RANK), _row(ts, KV_RANK), _row(ts, LANE)],
        out_shape=[jax.ShapeDtypeStruct((S, Q_RANK), BF), jax.ShapeDtypeStruct((S, KV_RANK), BF),
                   jax.ShapeDtypeStruct((S, LANE), BF)],
        compiler_params=_params("parallel"))(lat, qg, kvg, cos, sin_signed)


def _mla_prep_bwd(lat, qg, kvg, cos, sin_signed, d_qn, d_kvn, d_kpe):
    S = lat.shape[0]
    ts = _tile(S, 512)

    def body(lat_ref, qg_ref, kvg_ref, c_ref, s_ref, dqn_ref, dkvn_ref, dkpe_ref, dlat_ref, dqg_ref, dkvg_ref):
        i = pl.program_id(0)
        dcq, dqg = _norm_bwd_rows(dqn_ref[...], lat_ref[:, _CQ], qg_ref[...])
        dckv, dkvg = _norm_bwd_rows(dkvn_ref[...], lat_ref[:, _CKV], kvg_ref[...])
        dlat_ref[:, _CQ] = dcq.astype(BF)
        dlat_ref[:, _CKV] = dckv.astype(BF)
        dkpe = dkpe_ref[0]
        for g in range(1, d_kpe.shape[0]):
            dkpe = dkpe + dkpe_ref[g]
        dlat_ref[:, _KPE] = _rope_lanes(dkpe, c_ref[...], s_ref[...], True).astype(BF)
        dlat_ref[:, _KPE.stop:] = jnp.zeros((ts, LAT_W - _KPE.stop), BF)
        _accumulate(i, dqg_ref, dqg)
        _accumulate(i, dkvg_ref, dkvg)

    return pl.pallas_call(
        body, name="mla_prep_bwd", grid=(S // ts,),
        in_specs=[_row(ts, LAT_W), _bcast(1, Q_RANK), _bcast(1, KV_RANK), _row(ts, LANE), _row(ts, LANE),
                  _row(ts, Q_RANK), _row(ts, KV_RANK), pl.BlockSpec((d_kpe.shape[0], ts, LANE), lambda i: (0, i, 0))],
        out_specs=[_row(ts, LAT_W), _bcast(1, Q_RANK), _bcast(1, KV_RANK)],
        out_shape=[jax.ShapeDtypeStruct((S, LAT_W), BF), jax.ShapeDtypeStruct((1, Q_RANK), F32),
                   jax.ShapeDtypeStruct((1, KV_RANK), F32)],
        compiler_params=_params("arbitrary"))(lat, qg, kvg, cos, sin_signed, d_qn, d_kvn, d_kpe)


def _sigmoid(z):
    return 1.0 / (1.0 + jnp.exp(-z))


def _merge_fwd(gpre, b_gate, o_a, o_b):
    S, D = o_a.shape
    ts = _tile(S, 256)

    def body(g_ref, b_ref, oa_ref, ob_ref, m_ref):
        ga = _sigmoid(g_ref[:, :D] + b_ref[:, :D])
        gb = _sigmoid(g_ref[:, D:] + b_ref[:, D:])
        m_ref[...] = (ga * oa_ref[...] + gb * ob_ref[...]).astype(BF)

    return pl.pallas_call(
        body, name="merge_fwd", grid=(S // ts,),
        in_specs=[_row(ts, 2 * D), _bcast(1, 2 * D), _row(ts, D), _row(ts, D)], out_specs=_row(ts, D),
        out_shape=jax.ShapeDtypeStruct((S, D), BF), compiler_params=_params("parallel"))(gpre, b_gate, o_a, o_b)


def _merge_bwd(d_merge, gpre, b_gate, o_a, o_b):
    S, D = o_a.shape
    ts = _tile(S, 256)

    def body(dm_ref, g_ref, b_ref, oa_ref, ob_ref, doa_ref, dob_ref, dg_ref, db_ref):
        dm = dm_ref[...]
        ga = _sigmoid(g_ref[:, :D] + b_ref[:, :D])
        gb = _sigmoid(g_ref[:, D:] + b_ref[:, D:])
        doa_ref[...] = (dm * ga).astype(BF)
        dob_ref[...] = (dm * gb).astype(BF)
        dga = dm * oa_ref[...] * ga * (1.0 - ga)
        dgb = dm * ob_ref[...] * gb * (1.0 - gb)
        dg_ref[:, :D] = dga.astype(BF)
        dg_ref[:, D:] = dgb.astype(BF)
        i = pl.program_id(0)
        part = jnp.concatenate([jnp.sum(dga, axis=0, keepdims=True), jnp.sum(dgb, axis=0, keepdims=True)], axis=1)
        _accumulate(i, db_ref, part)

    return pl.pallas_call(
        body, name="merge_bwd", grid=(S // ts,),
        in_specs=[_row(ts, D), _row(ts, 2 * D), _bcast(1, 2 * D), _row(ts, D), _row(ts, D)],
        out_specs=[_row(ts, D), _row(ts, D), _row(ts, 2 * D), _bcast(1, 2 * D)],
        out_shape=[jax.ShapeDtypeStruct((S, D), BF), jax.ShapeDtypeStruct((S, D), BF),
                   jax.ShapeDtypeStruct((S, 2 * D), BF), jax.ShapeDtypeStruct((1, 2 * D), F32)],
        compiler_params=_params("arbitrary"))(d_merge, gpre, b_gate, o_a, o_b)


def _final_loss(x2, tgt, gf):
    S, D = x2.shape
    ts = _tile(S, 512)

    def body(x_ref, t_ref, g_ref, dx_ref, dg_ref, loss_ref):
        i = pl.program_id(0)
        xv = x_ref[...]
        g = g_ref[...]
        y = xv * _rstd(xv) * g
        err = y - t_ref[...]
        dx, dg = _norm_bwd_rows(err * (1.0 / D), xv, g)
        dx_ref[...] = dx
        _accumulate(i, dg_ref, dg)
        part = 0.5 * jnp.sum(jnp.mean(err * err, axis=-1, keepdims=True), axis=0, keepdims=True)
        _accumulate(i, loss_ref, jnp.broadcast_to(part, (8, LANE)))

    return pl.pallas_call(
        body, name="final_loss", grid=(S // ts,),
        in_specs=[_row(ts, D), _row(ts, D), _bcast(1, D)],
        out_specs=[_row(ts, D), _bcast(1, D), _bcast(8, LANE)],
        out_shape=[jax.ShapeDtypeStruct((S, D), F32), jax.ShapeDtypeStruct((1, D), F32),
                   jax.ShapeDtypeStruct((8, LANE), F32)],
        compiler_params=_params("arbitrary"))(x2, tgt, gf)


HALO = 16


SUB = 8


def _shift_down(cur, prev, k, rows):
    out = pltpu.roll(cur, k, 0)
    head = out[:SUB]
    for j in range(k):
        head = jnp.where(rows == j, prev[HALO - k + j:HALO - k + j + 1, :], head)
    return jnp.concatenate([head, out[SUB:]], axis=0)


def _shift_up(cur, nxt, k, rows, ts):
    out = pltpu.roll(cur, ts - k, 0)
    tail = out[ts - SUB:]
    for j in range(k):
        tail = jnp.where(rows == SUB - k + j, nxt[j:j + 1, :], tail)
    return jnp.concatenate([out[:ts - SUB], tail], axis=0)


def _conv_rows(cur, prev, w, b, rows):
    return b + w[0:1, :] * _shift_down(cur, prev, 2, rows) + w[1:2, :] * _shift_down(cur, prev, 1, rows) + w[2:3, :] * cur


def _conv_specs(ts, C, shard_of):
    nh = ts // HALO
    cur = pl.BlockSpec((None, ts, C), lambda g, i: (shard_of(g), i, 0))
    prev = pl.BlockSpec((None, HALO, C), lambda g, i: (shard_of(g), jnp.maximum(i * nh - 1, 0), 0))
    return cur, prev


def _ffn_act(u_pre, conv_w, conv_b):
    G4, S, C = u_pre.shape
    G = G4 // 2
    ts = _tile(S, 256)

    def body(up_ref, upp_ref, gt_ref, gtp_ref, wu_ref, wg_ref, bu_ref, bg_ref, act_ref):
        first = pl.program_id(1) == 0
        rows = lax.broadcasted_iota(jnp.int32, (SUB, C), 0)
        pu = jnp.where(first, 0.0, upp_ref[...].astype(F32))
        pg = jnp.where(first, 0.0, gtp_ref[...].astype(F32))
        up = _conv_rows(up_ref[...].astype(F32), pu, wu_ref[...], bu_ref[...], rows)
        gate = _conv_rows(gt_ref[...].astype(F32), pg, wg_ref[...], bg_ref[...], rows)
        act_ref[...] = (gate * _sigmoid(gate) * up).astype(BF)

    cur_u, prev_u = _conv_specs(ts, C, lambda g: g)
    cur_g, prev_g = _conv_specs(ts, C, lambda g: g + G)
    w_u = pl.BlockSpec((None, 3, C), lambda g, i: (g, 0, 0))
    w_g = pl.BlockSpec((None, 3, C), lambda g, i: (g + G, 0, 0))
    b_u = pl.BlockSpec((None, 1, C), lambda g, i: (g, 0, 0))
    b_g = pl.BlockSpec((None, 1, C), lambda g, i: (g + G, 0, 0))
    return pl.pallas_call(
        body, name="ffn_act", grid=(G, S // ts),
        in_specs=[cur_u, prev_u, cur_g, prev_g, w_u, w_g, b_u, b_g],
        out_specs=pl.BlockSpec((None, ts, C), lambda g, i: (g, i, 0)),
        out_shape=jax.ShapeDtypeStruct((G, S, C), BF),
        compiler_params=_params("parallel", "parallel"))(u_pre, u_pre, u_pre, u_pre, conv_w, conv_w, conv_b, conv_b)


def _ffn_act_bwd(u_pre, conv_w, conv_b, d_act):
    G4, S, C = u_pre.shape
    G = G4 // 2
    ts = _tile(S, 256)

    def body(up_ref, upp_ref, gt_ref, gtp_ref, wu_ref, wg_ref, bu_ref, bg_ref, da_ref, du_ref):
        first = pl.program_id(1) == 0
        rows = lax.broadcasted_iota(jnp.int32, (SUB, C), 0)
        pu = jnp.where(first, 0.0, upp_ref[...].astype(F32))
        pg = jnp.where(first, 0.0, gtp_ref[...].astype(F32))
        up = _conv_rows(up_ref[...].astype(F32), pu, wu_ref[...], bu_ref[...], rows)
        gate = _conv_rows(gt_ref[...].astype(F32), pg, wg_ref[...], bg_ref[...], rows)
        sg = _sigmoid(gate)
        da = da_ref[...].astype(F32)
        du_ref[0] = (da * (gate * sg)).astype(BF)
        du_ref[1] = (da * up * (sg * (1.0 + gate * (1.0 - sg)))).astype(BF)

    cur_u, prev_u = _conv_specs(ts, C, lambda g: g)
    cur_g, prev_g = _conv_specs(ts, C, lambda g: g + G)
    w_u = pl.BlockSpec((None, 3, C), lambda g, i: (g, 0, 0))
    w_g = pl.BlockSpec((None, 3, C), lambda g, i: (g + G, 0, 0))
    b_u = pl.BlockSpec((None, 1, C), lambda g, i: (g, 0, 0))
    b_g = pl.BlockSpec((None, 1, C), lambda g, i: (g + G, 0, 0))
    blk = pl.BlockSpec((None, ts, C), lambda g, i: (g, i, 0))
    du = pl.pallas_call(
        body, name="ffn_act_bwd", grid=(G, S // ts),
        in_specs=[cur_u, prev_u, cur_g, prev_g, w_u, w_g, b_u, b_g, blk],
        out_specs=pl.BlockSpec((2, None, ts, C), lambda g, i: (0, g, i, 0)),
        out_shape=jax.ShapeDtypeStruct((2, G, S, C), BF),
        compiler_params=_params("parallel", "parallel"))(
            u_pre, u_pre, u_pre, u_pre, conv_w, conv_w, conv_b, conv_b, d_act)
    return du.reshape(G4, S, C)


def _conv_bwd(du, u_pre, conv_w):
    G4, S, C = du.shape
    ts = _tile(S, 256)
    nh = ts // HALO
    last_halo = S // HALO - 1

    def body(du_ref, dun_ref, u_ref, up_ref, w_ref, dpre_ref, dw_ref, db_ref):
        i = pl.program_id(1)
        rows = lax.broadcasted_iota(jnp.int32, (SUB, C), 0)
        du_c = du_ref[...].astype(F32)
        nxt = jnp.where(i == pl.num_programs(1) - 1, 0.0, dun_ref[...].astype(F32))
        prev = jnp.where(i == 0, 0.0, up_ref[...].astype(F32))
        w = w_ref[...]
        dpre = w[2:3, :] * du_c + w[1:2, :] * _shift_up(du_c, nxt, 1, rows, ts) + w[0:1, :] * _shift_up(du_c, nxt, 2, rows, ts)
        dpre_ref[...] = dpre.astype(BF)
        u_c = u_ref[...].astype(F32)
        dw = jnp.concatenate([
            jnp.sum(du_c * _shift_down(u_c, prev, 2, rows), axis=0, keepdims=True),
            jnp.sum(du_c * _shift_down(u_c, prev, 1, rows), axis=0, keepdims=True),
            jnp.sum(du_c * u_c, axis=0, keepdims=True)], axis=0)
        _accumulate(i, dw_ref, dw)
        _accumulate(i, db_ref, jnp.sum(du_c, axis=0, keepdims=True))

    cur = pl.BlockSpec((None, ts, C), lambda g, i: (g, i, 0))
    nxt = pl.BlockSpec((None, HALO, C), lambda g, i: (g, jnp.minimum((i + 1) * nh, last_halo), 0))
    prev = pl.BlockSpec((None, HALO, C), lambda g, i: (g, jnp.maximum(i * nh - 1, 0), 0))
    return pl.pallas_call(
        body, name="conv_bwd", grid=(G4, S // ts),
        in_specs=[cur, nxt, cur, prev, pl.BlockSpec((None, 3, C), lambda g, i: (g, 0, 0))],
        out_specs=[cur, pl.BlockSpec((None, 3, C), lambda g, i: (g, 0, 0)), pl.BlockSpec((None, 1, C), lambda g, i: (g, 0, 0))],
        out_shape=[jax.ShapeDtypeStruct((G4, S, C), BF), jax.ShapeDtypeStruct((G4, 3, C), F32),
                   jax.ShapeDtypeStruct((G4, 1, C), F32)],
        compiler_params=_params("parallel", "arbitrary"))(du, du, u_pre, u_pre, conv_w)


MLA_T = 512
MLA_HB = 4
MLA_HW = MLA_HB * HEAD


def _mla_pairs(n, by_row):
    if by_row:
        pairs = [(i, j) for i in range(n) for j in range(i + 1)]
    else:
        pairs = [(i, j) for j in range(n) for i in range(j, n)]
    return jnp.asarray([p[0] for p in pairs], jnp.int32), jnp.asarray([p[1] for p in pairs], jnp.int32)


def _mla_specs(S):
    q = pl.BlockSpec((MLA_T, MLA_HW), lambda g, t, it, jt: (it[t], g))
    k = pl.BlockSpec((MLA_T, MLA_HW), lambda g, t, it, jt: (jt[t], g))
    kpe = pl.BlockSpec((MLA_T, HEAD), lambda g, t, it, jt: (jt[t], 0))
    lse = pl.BlockSpec((MLA_HB, MLA_T, LANE), lambda g, t, it, jt: (g, it[t], 0))
    return q, k, kpe, lse


def _mla_head(ref, hh):
    return ref[:, hh * HEAD:(hh + 1) * HEAD]


LOG2E = math.log2(math.e)
MLA_EXP2_SCALE = MLA_SCALE * LOG2E


def _mla_scores(qn_ref, qp_ref, kn_ref, kpe, hh, ok):
    q = jnp.concatenate([_mla_head(qn_ref, hh), _mla_head(qp_ref, hh)], axis=1)
    k = jnp.concatenate([_mla_head(kn_ref, hh), kpe], axis=1)
    s = lax.dot_general(q, k, NT, preferred_element_type=F32)
    return q, k, s if ok is None else jnp.where(ok, s, -jnp.inf)


def _mla_diagonal_mask():
    row = lax.broadcasted_iota(jnp.int32, (MLA_T, MLA_T), 0)
    col = lax.broadcasted_iota(jnp.int32, (MLA_T, MLA_T), 1)
    return col <= row


def _mla_step(i, j, step):
    @pl.when(j < i)
    def _():
        step(None)

    @pl.when(j == i)
    def _():
        step(_mla_diagonal_mask())


def _mla_fwd(qn, qp, kn, kpe, v):
    S = qn.shape[0]
    it, jt = _mla_pairs(S // MLA_T, True)

    def body(it_ref, jt_ref, qn_ref, qp_ref, kn_ref, kpe_ref, v_ref, o_ref, lse_ref, m_scr, acc_scr):
        t = pl.program_id(1)
        i, j = it_ref[t], jt_ref[t]

        @pl.when(j == 0)
        def _():
            m_scr[...] = jnp.full(m_scr.shape, -jnp.inf, F32)
            acc_scr[...] = jnp.zeros(acc_scr.shape, F32)

        def step(ok):
            kpe_v = kpe_ref[...]
            ones = jnp.ones((MLA_T, HEAD), BF)
            state = [(m_scr[hh], acc_scr[hh]) for hh in range(MLA_HB)]
            new = []
            for hh in range(MLA_HB):
                m_prev, acc = state[hh]
                _, _, s = _mla_scores(qn_ref, qp_ref, kn_ref, kpe_v, hh, ok)
                m_new = jnp.maximum(m_prev, jnp.max(s, axis=1, keepdims=True))
                p = jnp.exp2((s - m_new) * MLA_EXP2_SCALE).astype(BF)
                v1 = jnp.concatenate([_mla_head(v_ref, hh), ones], axis=1)
                alpha = jnp.exp2((m_prev - m_new) * MLA_EXP2_SCALE)
                new.append((m_new, alpha * acc + lax.dot_general(p, v1, NN, preferred_element_type=F32)))
            for hh in range(MLA_HB):
                m_scr[hh], acc_scr[hh] = new[hh]

        _mla_step(i, j, step)

        @pl.when(j == i)
        def _():
            for hh in range(MLA_HB):
                l = acc_scr[hh, :, HEAD:]
                o_ref[:, hh * HEAD:(hh + 1) * HEAD] = (acc_scr[hh, :, :HEAD] / l).astype(BF)
                lse_ref[hh] = m_scr[hh] * MLA_SCALE + jnp.log(l)

    qspec, kspec, kpespec, lsespec = _mla_specs(S)
    grid_spec = pltpu.PrefetchScalarGridSpec(
        num_scalar_prefetch=2, grid=(MLA_HEADS // MLA_HB, it.shape[0]),
        in_specs=[qspec, qspec, kspec, kpespec, kspec], out_specs=[qspec, lsespec],
        scratch_shapes=[pltpu.VMEM((MLA_HB, MLA_T, 1), F32), pltpu.VMEM((MLA_HB, MLA_T, 2 * HEAD), F32)])
    return pl.pallas_call(
        body, name="mla_fwd", grid_spec=grid_spec,
        out_shape=[jax.ShapeDtypeStruct((S, MLA_HEADS * HEAD), BF), jax.ShapeDtypeStruct((MLA_HEADS, S, LANE), F32)],
        compiler_params=_params("parallel", "arbitrary"))(it, jt, qn, qp, kn, kpe, v)


def _mla_p_ds(qn_ref, qp_ref, kn_ref, kpe, v_ref, do_ref, o_ref, lse_ref, hh, ok):
    q, k, s = _mla_scores(qn_ref, qp_ref, kn_ref, kpe, hh, ok)
    p = jnp.exp2(s * MLA_EXP2_SCALE - lse_ref[hh][:, 0:1] * LOG2E)
    do = _mla_head(do_ref, hh)
    delta = jnp.sum(do.astype(F32) * _mla_head(o_ref, hh).astype(F32), axis=1, keepdims=True)
    dp = lax.dot_general(do, _mla_head(v_ref, hh), NT, preferred_element_type=F32)
    ds = p * (dp - delta) * MLA_SCALE
    return q, k, p, ds, do


def _mla_bwd_dq(qn, qp, kn, kpe, v, do, o, lse):
    S = qn.shape[0]
    it, jt = _mla_pairs(S // MLA_T, True)

    def body(it_ref, jt_ref, qn_ref, qp_ref, kn_ref, kpe_ref, v_ref, do_ref, o_ref, lse_ref, dqn_ref, dqp_ref, acc):
        t = pl.program_id(1)
        i, j = it_ref[t], jt_ref[t]

        @pl.when(j == 0)
        def _():
            acc[...] = jnp.zeros(acc.shape, F32)

        def step(ok):
            kpe_v = kpe_ref[...]
            old = [acc[hh] for hh in range(MLA_HB)]
            for hh in range(MLA_HB):
                _, k, _, ds, _ = _mla_p_ds(qn_ref, qp_ref, kn_ref, kpe_v, v_ref, do_ref, o_ref, lse_ref, hh, ok)
                old[hh] = old[hh] + lax.dot_general(ds.astype(BF), k, NN, preferred_element_type=F32)
            for hh in range(MLA_HB):
                acc[hh] = old[hh]

        _mla_step(i, j, step)

        @pl.when(j == i)
        def _():
            for hh in range(MLA_HB):
                dqn_ref[:, hh * HEAD:(hh + 1) * HEAD] = acc[hh, :, :HEAD].astype(BF)
                dqp_ref[:, hh * HEAD:(hh + 1) * HEAD] = acc[hh, :, HEAD:]

    qspec, kspec, kpespec, lsespec = _mla_specs(S)
    grid_spec = pltpu.PrefetchScalarGridSpec(
        num_scalar_prefetch=2, grid=(MLA_HEADS // MLA_HB, it.shape[0]),
        in_specs=[qspec, qspec, kspec, kpespec, kspec, qspec, qspec, lsespec], out_specs=[qspec, qspec],
        scratch_shapes=[pltpu.VMEM((MLA_HB, MLA_T, 2 * HEAD), F32)])
    return pl.pallas_call(
        body, name="mla_bwd_dq", grid_spec=grid_spec,
        out_shape=[jax.ShapeDtypeStruct((S, MLA_HEADS * HEAD), BF), jax.ShapeDtypeStruct((S, MLA_HEADS * HEAD), F32)],
        compiler_params=_params("parallel", "arbitrary"))(it, jt, qn, qp, kn, kpe, v, do, o, lse)


def _mla_bwd_dkv(qn, qp, kn, kpe, v, do, o, lse):
    S = qn.shape[0]
    nq = S // MLA_T
    it, jt = _mla_pairs(nq, False)

    def body(it_ref, jt_ref, qn_ref, qp_ref, kn_ref, kpe_ref, v_ref, do_ref, o_ref, lse_ref, dkn_ref, dv_ref, dkpe_ref,
             dk_acc, dv_acc):
        t = pl.program_id(1)
        i, j = it_ref[t], jt_ref[t]

        @pl.when(i == j)
        def _():
            dk_acc[...] = jnp.zeros(dk_acc.shape, F32)
            dv_acc[...] = jnp.zeros(dv_acc.shape, F32)

        def step(ok):
            kpe_v = kpe_ref[...]
            for hh in range(MLA_HB):
                q, _, p, ds, do = _mla_p_ds(qn_ref, qp_ref, kn_ref, kpe_v, v_ref, do_ref, o_ref, lse_ref, hh, ok)
                dv_acc[hh] += lax.dot_general(p.astype(BF), do, TN, preferred_element_type=F32)
                dk_acc[hh] += lax.dot_general(ds.astype(BF), q, TN, preferred_element_type=F32)

        _mla_step(i, j, step)

        @pl.when(i == nq - 1)
        def _():
            dkpe = dk_acc[0, :, HEAD:]
            for hh in range(MLA_HB):
                dkn_ref[:, hh * HEAD:(hh + 1) * HEAD] = dk_acc[hh, :, :HEAD].astype(BF)
                dv_ref[:, hh * HEAD:(hh + 1) * HEAD] = dv_acc[hh].astype(BF)
                if hh:
                    dkpe = dkpe + dk_acc[hh, :, HEAD:]
            dkpe_ref[...] = dkpe

    qspec, kspec, kpespec, lsespec = _mla_specs(S)
    dkpespec = pl.BlockSpec((None, MLA_T, HEAD), lambda g, t, it, jt: (g, jt[t], 0))
    grid_spec = pltpu.PrefetchScalarGridSpec(
        num_scalar_prefetch=2, grid=(MLA_HEADS // MLA_HB, it.shape[0]),
        in_specs=[qspec, qspec, kspec, kpespec, kspec, qspec, qspec, lsespec], out_specs=[kspec, kspec, dkpespec],
        scratch_shapes=[pltpu.VMEM((MLA_HB, MLA_T, 2 * HEAD), F32), pltpu.VMEM((MLA_HB, MLA_T, HEAD), F32)])
    return pl.pallas_call(
        body, name="mla_bwd_dkv", grid_spec=grid_spec,
        out_shape=[jax.ShapeDtypeStruct((S, MLA_HEADS * HEAD), BF), jax.ShapeDtypeStruct((S, MLA_HEADS * HEAD), BF),
                   jax.ShapeDtypeStruct((MLA_HEADS // MLA_HB, S, HEAD), F32)],
        compiler_params=_params("parallel", "arbitrary"))(it, jt, qn, qp, kn, kpe, v, do, o, lse)


DIL_W = 3 * DIL_HPG * HEAD
DIL_O = DIL_HPG * HEAD
DIL_STEP_BLOCKS = 4


def _dil_slopes(g):
    return [2.0 ** (-ALIBI_MAX_BIAS * (g * DIL_HPG + hh + 1) / DIL_HEADS) for hh in range(DIL_HPG)]


def _dil_bias(dil):
    p = lax.broadcasted_iota(jnp.int32, (DIL_BLOCK, DIL_BLOCK), 0)
    kk = lax.broadcasted_iota(jnp.int32, (DIL_BLOCK, DIL_BLOCK), 1)
    jc = p - kk
    dist_c = (dil * jc).astype(F32)
    dist_p = (dil * (jc + DIL_BLOCK)).astype(F32)
    return jc >= 0, jc <= 0, dist_c, dist_p


def _dil_head(blk, hh):
    q = blk[:, hh * HEAD:(hh + 1) * HEAD]
    k = blk[:, DIL_O + hh * HEAD:DIL_O + (hh + 1) * HEAD]
    v = blk[:, 2 * DIL_O + hh * HEAD:2 * DIL_O + (hh + 1) * HEAD]
    return q, k, v


def _dil_s(q, k, slope, dist, ok):
    s = lax.dot_general(q, k, NT, preferred_element_type=F32) * DIL_SCALE - slope * dist
    return jnp.where(ok, s, -jnp.inf)


def _dil_view(a, dil):
    S, W = a.shape
    return a.reshape(S // dil, dil * W)


def _dil_fwd(qkv, g):
    _, dil = DIL_PATTERNS[g]
    S = qkv.shape[0]
    L = S // dil
    nb = L // DIL_BLOCK
    slopes = _dil_slopes(g)

    bb = min(DIL_STEP_BLOCKS, nb)
    rows = bb * DIL_BLOCK

    def body(cur_ref, prev_ref, o_ref, lse_ref):
        n = pl.program_id(1)
        ok_c, ok_p, dist_c, dist_p = _dil_bias(dil)
        for b in range(bb):
            cur = cur_ref[b * DIL_BLOCK:(b + 1) * DIL_BLOCK, :]
            prev = prev_ref[...] if b == 0 else cur_ref[(b - 1) * DIL_BLOCK:b * DIL_BLOCK, :]
            ok_b = ok_p & (n > 0) if b == 0 else ok_p
            for hh in range(DIL_HPG):
                q, kc, vc = _dil_head(cur, hh)
                _, kp, vp = _dil_head(prev, hh)
                sc = _dil_s(q, kc, slopes[hh], dist_c, ok_c)
                sp = _dil_s(q, kp, slopes[hh], dist_p, ok_b)
                m = jnp.maximum(jnp.max(sc, axis=1, keepdims=True), jnp.max(sp, axis=1, keepdims=True))
                pc, pp = jnp.exp(sc - m), jnp.exp(sp - m)
                l = jnp.sum(pc, axis=1, keepdims=True) + jnp.sum(pp, axis=1, keepdims=True)
                o = (lax.dot_general(pc.astype(BF), vc, NN, preferred_element_type=F32)
                     + lax.dot_general(pp.astype(BF), vp, NN, preferred_element_type=F32)) / l
                rs, sl = slice(b * DIL_BLOCK, (b + 1) * DIL_BLOCK), slice(hh * HEAD, (hh + 1) * HEAD)
                o_ref[rs, sl] = o
                lse_ref[rs, sl] = jnp.broadcast_to(m + jnp.log(l), (DIL_BLOCK, HEAD))

    ospec = pl.BlockSpec((rows, DIL_O), lambda r, n: (n, r))
    o, lse = pl.pallas_call(
        body, name=f"dil_fwd{g}", grid=(dil, nb // bb),
        in_specs=[pl.BlockSpec((rows, DIL_W), lambda r, n: (n, r)),
                  pl.BlockSpec((DIL_BLOCK, DIL_W), lambda r, n: (jnp.maximum(n * bb - 1, 0), r))],
        out_specs=[ospec, ospec],
        out_shape=[jax.ShapeDtypeStruct((L, dil * DIL_O), F32), jax.ShapeDtypeStruct((L, dil * DIL_O), F32)],
        compiler_params=_params("parallel", "parallel"))(_dil_view(qkv, dil), _dil_view(qkv, dil))
    return o.reshape(S, DIL_O), lse.reshape(S, DIL_O)


def _dil_combine(os_, lses):
    S = os_[0].shape[0]
    ts = _tile(S, 512)

    def body(o0, o1, o2, l0, l1, l2, out_ref, lse_ref):
        a, b, c = l0[...], l1[...], l2[...]
        m = jnp.maximum(jnp.maximum(a, b), c)
        ea, eb, ec = jnp.exp(a - m), jnp.exp(b - m), jnp.exp(c - m)
        tot = ea + eb + ec
        out_ref[...] = ((ea * o0[...] + eb * o1[...] + ec * o2[...]) / tot).astype(BF)
        lse_ref[...] = m + jnp.log(tot)

    return pl.pallas_call(
        body, name="dil_combine", grid=(S // ts,), in_specs=[_row(ts, DIL_O)] * 6,
        out_specs=[_row(ts, DIL_O), _row(ts, DIL_O)],
        out_shape=[jax.ShapeDtypeStruct((S, DIL_O), BF), jax.ShapeDtypeStruct((S, DIL_O), F32)],
        compiler_params=_params("parallel"))(*os_, *lses)


def _dil_delta(do, out):
    S = do.shape[0]
    ts = _tile(S, 512)

    def body(do_ref, o_ref, d_ref):
        for hh in range(DIL_HPG):
            sl = slice(hh * HEAD, (hh + 1) * HEAD)
            d = jnp.sum(do_ref[:, sl].astype(F32) * o_ref[:, sl].astype(F32), axis=1, keepdims=True)
            d_ref[:, sl] = jnp.broadcast_to(d, (ts, HEAD))

    return pl.pallas_call(
        body, name="dil_delta", grid=(S // ts,), in_specs=[_row(ts, DIL_O)] * 2, out_specs=_row(ts, DIL_O),
        out_shape=jax.ShapeDtypeStruct((S, DIL_O), F32), compiler_params=_params("parallel"))(do, out)


def _dil_bwd(qkv, do, lse, delta, g):
    _, dil = DIL_PATTERNS[g]
    S = qkv.shape[0]
    L = S // dil
    nb = L // DIL_BLOCK
    slopes = _dil_slopes(g)

    def pair(q, k, v, do_h, lse_h, delta_h, slope, dist, ok):
        s = _dil_s(q, k, slope, dist, ok)
        p = jnp.exp(s - lse_h)
        dp = lax.dot_general(do_h, v, NT, preferred_element_type=F32)
        ds = (p * (dp - delta_h) * DIL_SCALE).astype(BF)
        return p.astype(BF), ds

    bb = min(DIL_STEP_BLOCKS, nb)
    rows = bb * DIL_BLOCK
    steps = nb // bb

    def body(cur_ref, prev_ref, next_ref, doc_ref, don_ref, lsec_ref, lsen_ref, dlc_ref, dln_ref, out_ref):
        n = pl.program_id(1)
        ok_c, ok_p0, dist_c, dist_p = _dil_bias(dil)
        for b in range(bb):
            rs = slice(b * DIL_BLOCK, (b + 1) * DIL_BLOCK)
            rp = slice((b - 1) * DIL_BLOCK, b * DIL_BLOCK)
            rn = slice((b + 1) * DIL_BLOCK, (b + 2) * DIL_BLOCK)
            first, last = b == 0, b == bb - 1
            cur = cur_ref[rs, :]
            prev = prev_ref[...] if first else cur_ref[rp, :]
            nxt = next_ref[...] if last else cur_ref[rn, :]
            ok_a = ok_p0 & (n > 0) if first else ok_p0
            ok_n = ok_p0 & (n < steps - 1) if last else ok_p0
            for hh in range(DIL_HPG):
                sl = slice(hh * HEAD, (hh + 1) * HEAD)
                q, kc, vc = _dil_head(cur, hh)
                _, kp, vp = _dil_head(prev, hh)
                qn, _, _ = _dil_head(nxt, hh)
                do_c = doc_ref[rs, sl]
                do_n = don_ref[:, sl] if last else doc_ref[rn, sl]
                lse_c = lsec_ref[rs, sl][:, 0:1]
                lse_n = (lsen_ref[:, sl] if last else lsec_ref[rn, sl])[:, 0:1]
                dl_c = dlc_ref[rs, sl][:, 0:1]
                dl_n = (dln_ref[:, sl] if last else dlc_ref[rn, sl])[:, 0:1]
                _, ds_a = pair(q, kp, vp, do_c, lse_c, dl_c, slopes[hh], dist_p, ok_a)
                p_b, ds_b = pair(q, kc, vc, do_c, lse_c, dl_c, slopes[hh], dist_c, ok_c)
                p_n, ds_n = pair(qn, kc, vc, do_n, lse_n, dl_n, slopes[hh], dist_p, ok_n)
                dq = (lax.dot_general(ds_a, kp, NN, preferred_element_type=F32)
                      + lax.dot_general(ds_b, kc, NN, preferred_element_type=F32))
                dk = (lax.dot_general(ds_b, q, TN, preferred_element_type=F32)
                      + lax.dot_general(ds_n, qn, TN, preferred_element_type=F32))
                dv = (lax.dot_general(p_b, do_c, TN, preferred_element_type=F32)
                      + lax.dot_general(p_n, do_n, TN, preferred_element_type=F32))
                out_ref[rs, sl] = dq.astype(BF)
                out_ref[rs, DIL_O + hh * HEAD:DIL_O + (hh + 1) * HEAD] = dk.astype(BF)
                out_ref[rs, 2 * DIL_O + hh * HEAD:2 * DIL_O + (hh + 1) * HEAD] = dv.astype(BF)

    cur_w = pl.BlockSpec((rows, DIL_W), lambda r, n: (n, r))
    prev_w = pl.BlockSpec((DIL_BLOCK, DIL_W), lambda r, n: (jnp.maximum(n * bb - 1, 0), r))
    next_w = pl.BlockSpec((DIL_BLOCK, DIL_W), lambda r, n: (jnp.minimum((n + 1) * bb, nb - 1), r))
    cur_o = pl.BlockSpec((rows, DIL_O), lambda r, n: (n, r))
    next_o = pl.BlockSpec((DIL_BLOCK, DIL_O), lambda r, n: (jnp.minimum((n + 1) * bb, nb - 1), r))
    qv, dov, lsev, dlv = _dil_view(qkv, dil), _dil_view(do, dil), _dil_view(lse, dil), _dil_view(delta, dil)
    out = pl.pallas_call(
        body, name=f"dil_bwd{g}", grid=(dil, steps),
        in_specs=[cur_w, prev_w, next_w, cur_o, next_o, cur_o, next_o, cur_o, next_o],
        out_specs=cur_w, out_shape=jax.ShapeDtypeStruct((L, dil * DIL_W), BF),
        compiler_params=_params("parallel", "parallel"))(qv, qv, qv, dov, dov, lsev, lsev, dlv, dlv)
    return out.reshape(S, DIL_W)


def _adamw(name, w, g, m, v):
    R, C = w.shape
    tr = R if R <= 512 else _tile_rows(R, 256)

    def body(w_ref, g_ref, m_ref, v_ref, go_ref, d_ref, nm_ref, nv_ref):
        gv = g_ref[...]
        go_ref[...] = gv
        nm = ADAM_B1 * m_ref[...] + (1.0 - ADAM_B1) * gv
        nv = ADAM_B2 * v_ref[...] + (1.0 - ADAM_B2) * (gv * gv)
        m_hat = nm / (1.0 - ADAM_B1 ** ADAM_STEP)
        v_hat = nv / (1.0 - ADAM_B2 ** ADAM_STEP)
        d_ref[...] = -ADAM_LR * (m_hat / (jnp.sqrt(v_hat) + ADAM_EPS) + ADAM_WD * w_ref[...])
        nm_ref[...] = nm
        nv_ref[...] = nv

    spec = pl.BlockSpec((tr, C), lambda i: (i, 0))
    shp = jax.ShapeDtypeStruct((R, C), F32)
    return pl.pallas_call(
        body, name=name, grid=(R // tr,), in_specs=[spec] * 4, out_specs=[spec] * 4, out_shape=[shp] * 4,
        compiler_params=_params("parallel"))(w, g, m, v)


def _tile_rows(n, pref, mult=8):
    t = (pref // mult) * mult
    while t >= mult:
        if n % t == 0:
            return t
        t -= mult
    return n


ANY = pl.BlockSpec(memory_space=pl.ANY)


def _place():
    x, y, c = lax.axis_index("x"), lax.axis_index("y"), lax.axis_index("c")
    chips = [(1 - x, y), (x, 1 - y), (1 - x, 1 - y)]
    chip_idx = [2 * cx + cy for cx, cy in chips]
    return x, y, c, 2 * x + y, chips, chip_idx


def _rcopy(src, dst, ssem, rsem, dev):
    return pltpu.make_async_remote_copy(src_ref=src, dst_ref=dst, send_sem=ssem, recv_sem=rsem,
                                        device_id=dev, device_id_type=MESH)


HBM = pl.BlockSpec(memory_space=pltpu.HBM)
SEM = pl.BlockSpec(memory_space=pltpu.SEMAPHORE)
EFFECT = pltpu.SideEffectType.DATAFLOW_SIDE_EFFECTING


def _split_copies(kind, srcs, lands, ssem, rsem):
    _, _, c, me, chips, chip_idx = _place()
    cps = []
    for i in range(len(srcs)):
        for k in range(3):
            if kind == "gather":
                rows = srcs[i].shape[0]
                if rows == lands[i].shape[1]:
                    src, dst = srcs[i], lands[i].at[me]
                else:
                    src, dst = srcs[i], lands[i].at[me, pl.ds(pl.multiple_of(c * rows, 16), rows)]
            else:
                src, dst = srcs[i].at[chip_idx[k]], lands[i].at[k]
            cps.append(_rcopy(src, dst, ssem.at[3 * i + k], rsem.at[3 * i + k], (*chips[k], c)))
    return cps


def _exchange_start(name, kind, srcs, lands, groups):
    n, ng = len(srcs), len(groups)

    def body(*refs):
        src_refs, land_refs = refs[:n], refs[n:2 * n]
        sems = refs[2 * n:2 * n + 2 * ng]
        token = refs[-1]
        for gi, grp in enumerate(groups):
            cps = _split_copies(kind, [src_refs[i] for i in grp], [land_refs[i] for i in grp], sems[2 * gi], sems[2 * gi + 1])
            for cp in cps:
                cp.start()
        token[...] = jnp.zeros_like(token)

    arrays = list(srcs) + list(lands)
    out_shape = []
    for grp in groups:
        out_shape += [pltpu.SemaphoreType.DMA((3 * len(grp),)), pltpu.SemaphoreType.DMA((3 * len(grp),))]
    out_shape += [pltpu.HBM(a.shape, a.dtype) for a in arrays] + [jax.ShapeDtypeStruct((8, LANE), F32)]
    outs = pl.pallas_call(
        body, name=name, out_shape=out_shape, in_specs=[HBM] * (2 * n),
        out_specs=[SEM] * (2 * ng) + [HBM] * (2 * n) + [pl.BlockSpec(memory_space=pltpu.VMEM)],
        input_output_aliases={i: 2 * ng + i for i in range(2 * n)},
        compiler_params=pltpu.CompilerParams(has_side_effects=EFFECT),
    )(*[pltpu.with_memory_space_constraint(a, pltpu.HBM) for a in arrays])
    sems = [(outs[2 * gi], outs[2 * gi + 1]) for gi in range(ng)]
    thru = outs[2 * ng:2 * ng + 2 * n]
    return sems, thru[:n], thru[n:], outs[-1]


def _exchange_wait(name, kind, srcs, lands, sems, after):
    n = len(srcs)

    def body(*refs):
        cps = _split_copies(kind, refs[:n], refs[n:2 * n], refs[2 * n], refs[2 * n + 1])
        for cp in cps:
            cp.wait_send()
            cp.wait_recv()

    arrays = list(srcs) + list(lands)
    outs = pl.pallas_call(
        body, name=name, out_shape=[pltpu.HBM(a.shape, a.dtype) for a in arrays],
        in_specs=[HBM] * (2 * n) + [SEM, SEM, ANY], out_specs=[HBM] * (2 * n),
        input_output_aliases={i: i for i in range(2 * n)},
        compiler_params=pltpu.CompilerParams(has_side_effects=EFFECT),
    )(*arrays, sems[0], sems[1], after)
    return outs[:n], outs[n:]


EXCHANGE_CHUNK_BYTES = 3 * 512 * 1024


def _half_geometry(R, C, axis):
    Rp, Cp = (R // 2, C) if axis == 0 else (R, C // 2)
    tr = _tile_rows(Rp, max(16, EXCHANGE_CHUNK_BYTES // (2 * Cp)), 16)
    return Rp, Cp, tr, Rp // tr


def _pair_sum(name, g, axis):
    G, R, C = g.shape
    Rp, Cp, tr, nb = _half_geometry(R, C, axis)
    steps = G * nb

    def half_block(s, b, h):
        return (s, h * nb + b, 0) if axis == 0 else (s, b, h)

    def body(c_ref, keep_ref, give_ref, out_ref, land, ssem, rsem, credit):
        x, y, c = lax.axis_index("x"), lax.axis_index("y"), lax.axis_index("c")
        sib = (x, y, 1 - c)
        t = pl.program_id(0) * nb + pl.program_id(1)
        slot = t % 2

        @pl.when(t >= 2)
        def _():
            pl.semaphore_wait(credit, 1)

        cp = _rcopy(give_ref.at[0], land.at[slot], ssem.at[slot], rsem.at[slot], sib)
        cp.start()
        cp.wait_recv()
        out_ref[...] = (keep_ref[...].astype(F32) + land[slot].astype(F32)).astype(BF)

        @pl.when(t + 2 < steps)
        def _():
            pl.semaphore_signal(credit, 1, device_id=sib, device_id_type=MESH)

        cp.wait_send()

    blk = (None, tr, Cp)
    grid_spec = pltpu.PrefetchScalarGridSpec(
        num_scalar_prefetch=1, grid=(G, nb),
        in_specs=[pl.BlockSpec(blk, lambda s, b, c_ref: half_block(s, b, c_ref[0])),
                  pl.BlockSpec((1, tr, Cp), lambda s, b, c_ref: half_block(s, b, 1 - c_ref[0]))],
        out_specs=pl.BlockSpec(blk, lambda s, b, c_ref: (s, b, 0)),
        scratch_shapes=[pltpu.VMEM((2, tr, Cp), BF), pltpu.SemaphoreType.DMA((2,)), pltpu.SemaphoreType.DMA((2,)),
                        pltpu.SemaphoreType.REGULAR])
    c_arr = lax.axis_index("c").astype(jnp.int32).reshape(1)
    return pl.pallas_call(
        body, name=name, grid_spec=grid_spec, out_shape=jax.ShapeDtypeStruct((G, Rp, Cp), BF),
        compiler_params=_params("arbitrary", "arbitrary"))(c_arr, g, g)


def _chip_total(name, h, landed):
    G, Rp, Cp = h.shape
    tr = _tile_rows(Rp, max(16, EXCHANGE_CHUNK_BYTES // (2 * Cp)), 16)

    def body(me_ref, own_ref, l0_ref, l1_ref, l2_ref, out_ref):
        acc = own_ref[...].astype(F32)
        for r in (l0_ref, l1_ref, l2_ref):
            acc = acc + r[...].astype(F32)
        out_ref[...] = acc

    blk = (None, tr, Cp)
    grid_spec = pltpu.PrefetchScalarGridSpec(
        num_scalar_prefetch=1, grid=(Rp // tr,),
        in_specs=[pl.BlockSpec(blk, lambda b, me_ref: (me_ref[0], b, 0))]
        + [pl.BlockSpec(blk, functools.partial(lambda b, me_ref, k: (k, b, 0), k=k)) for k in range(3)],
        out_specs=pl.BlockSpec((tr, Cp), lambda b, me_ref: (b, 0)))
    me = (2 * lax.axis_index("x") + lax.axis_index("y")).astype(jnp.int32).reshape(1)
    return pl.pallas_call(
        body, name=name, grid_spec=grid_spec, out_shape=jax.ShapeDtypeStruct((Rp, Cp), F32),
        compiler_params=_params("parallel"))(me, h, landed, landed, landed)


def _pair_join(name, f, axis):
    Rp, Cp = f.shape
    R, C = (2 * Rp, Cp) if axis == 0 else (Rp, 2 * Cp)
    tr = _tile_rows(Rp, max(8, EXCHANGE_CHUNK_BYTES // (4 * Cp)), 8)
    nb = Rp // tr

    def body(f_ref, full, ssem, rsem, lsem):
        x, y, c = lax.axis_index("x"), lax.axis_index("y"), lax.axis_index("c")
        sib = (x, y, 1 - c)
        b = pl.program_id(0)

        def place(h, r0, rows):
            if axis == 0:
                return full.at[pl.ds(pl.multiple_of(h * Rp + r0, 8), rows), :]
            return full.at[pl.ds(pl.multiple_of(r0, 8), rows), pl.ds(pl.multiple_of(h * Cp, LANE), Cp)]

        mine = place(c, b * tr, tr)
        loc = pltpu.make_async_copy(f_ref, mine, lsem)
        rem = _rcopy(f_ref, mine, ssem, rsem, sib)
        loc.start()
        rem.start()
        loc.wait()
        rem.wait_send()

        @pl.when(b == nb - 1)
        def _():
            theirs = place(1 - c, 0, Rp)
            _rcopy(theirs, theirs, ssem, rsem, sib).wait_recv()

    return pl.pallas_call(
        body, name=name, grid=(nb,), in_specs=[pl.BlockSpec((tr, Cp), lambda b: (b, 0))], out_specs=ANY,
        out_shape=jax.ShapeDtypeStruct((R, C), F32),
        scratch_shapes=[pltpu.SemaphoreType.DMA, pltpu.SemaphoreType.DMA, pltpu.SemaphoreType.DMA],
        compiler_params=_params("arbitrary"))(f)


def _pair_share(name, land):
    G, R, C = land.shape
    Rh = R // 2
    tr = _tile_rows(Rh, max(16, EXCHANGE_CHUNK_BYTES // (2 * C)), 16)
    chunks = [(k, b) for k in range(3) for b in range(Rh // tr)]

    def body(src, dst, buf, lsem, ssem, rsem):
        x, y, c, _, _, chip_idx = _place()
        sib = (x, y, 1 - c)

        def region(ref, k, half, r0, rows):
            return ref.at[chip_idx[k], pl.ds(pl.multiple_of(half * Rh + r0, 16), rows)]

        def load(t):
            k, b = chunks[t]
            return pltpu.make_async_copy(region(src, k, c, b * tr, tr), buf.at[t % 2], lsem.at[t % 2])

        def send(t):
            k, b = chunks[t]
            return _rcopy(buf.at[t % 2], region(dst, k, c, b * tr, tr), ssem.at[t % 2], rsem.at[k], sib)

        load(0).start()
        for t in range(len(chunks)):
            load(t).wait()
            if t + 1 < len(chunks):
                if t >= 1:
                    send(t - 1).wait_send()
                load(t + 1).start()
            send(t).start()
        for t in range(max(0, len(chunks) - 2), len(chunks)):
            send(t).wait_send()
        for k in range(3):
            theirs = region(dst, k, 1 - c, 0, Rh)
            _rcopy(theirs, theirs, ssem.at[0], rsem.at[k], sib).wait_recv()

    return pl.pallas_call(
        body, name=name, in_specs=[ANY], out_specs=ANY, out_shape=jax.ShapeDtypeStruct(land.shape, land.dtype),
        input_output_aliases={0: 0},
        scratch_shapes=[pltpu.VMEM((2, tr, C), land.dtype), pltpu.SemaphoreType.DMA((2,)), pltpu.SemaphoreType.DMA((2,)),
                        pltpu.SemaphoreType.DMA((3,))],
    )(land)


def _allreduce_small(v):
    R, K = v.shape
    ndev = 8

    def body(v_ref, o_ref, land, ssem, rsem):
        x, y, c = lax.axis_index("x"), lax.axis_index("y"), lax.axis_index("c")
        me = 4 * x + 2 * y + c
        land[me] = v_ref[...]
        cps = []
        for r in range(1, ndev):
            fx, fy, fc = (r >> 2) & 1, (r >> 1) & 1, r & 1
            peer = (x ^ fx, y ^ fy, c ^ fc)
            cp = _rcopy(v_ref, land.at[me], ssem.at[r - 1], rsem.at[r - 1], peer)
            cp.start()
            cps.append((cp, 4 * peer[0] + 2 * peer[1] + peer[2], r))
        for cp, src, r in cps:
            cp.wait_send()
            _rcopy(v_ref, land.at[src], ssem.at[r - 1], rsem.at[r - 1], (x, y, c)).wait_recv()
        acc = land[0]
        for d in range(1, ndev):
            acc = acc + land[d]
        o_ref[...] = acc

    vm = pl.BlockSpec(memory_space=pltpu.VMEM)
    return pl.pallas_call(
        body, name="allreduce_small", in_specs=[vm], out_specs=vm, out_shape=jax.ShapeDtypeStruct((R, K), F32),
        scratch_shapes=[pltpu.VMEM((ndev, R, K), F32), pltpu.SemaphoreType.DMA((ndev - 1,)), pltpu.SemaphoreType.DMA((ndev - 1,))],
    )(v)


IN_SPLITS = (Q_RANK, KV_RANK, QK_ROPE, DIL_HEADS * HEAD, DIL_HEADS * HEAD, DIL_HEADS * HEAD, D_MODEL, D_MODEL)
IN_OFF = tuple(int(v) for v in np.cumsum((0,) + IN_SPLITS))


def _unshard_cols(g):
    G, K, Ns = g.shape
    return g.transpose(1, 0, 2).reshape(K, G * Ns)


def _shard_cols(w):
    K, N = w.shape
    return w.reshape(K, N_CHIPS, N // N_CHIPS).transpose(1, 0, 2)


def _rope_pad(w):
    half = QK_ROPE // 2
    z = jnp.zeros(w.shape[:-1] + (half,), w.dtype)
    return jnp.concatenate([w[..., :half], z, w[..., half:], z], axis=-1)


def _rope_unpad(w):
    half = QK_ROPE // 2
    return jnp.concatenate([w[..., :half], w[..., 2 * half:3 * half]], axis=-1)


def _split_w_in(w_in_g):
    w = _unshard_cols(w_in_g)
    K = w.shape[0]
    p = [w[:, IN_OFF[i]:IN_OFF[i + 1]] for i in range(8)]
    w_lat = jnp.concatenate([p[0], p[1], _rope_pad(p[2]), jnp.zeros((K, LAT_W - _KPE.stop), w.dtype)], axis=1)
    w_dil = [jnp.concatenate([p[3 + t][:, g * DIL_O:(g + 1) * DIL_O] for t in range(3)], axis=1) for g in range(DIL_GROUPS)]
    w_gate = jnp.concatenate([p[6], p[7]], axis=1)
    return w_lat, w_dil, w_gate


def _merge_dw_in(dw_lat, dw_dil, dw_gate):
    parts = [dw_lat[:, _CQ], dw_lat[:, _CKV], _rope_unpad(dw_lat[:, _KPE])]
    for t in range(3):
        parts += [dw_dil[g][:, t * DIL_O:(t + 1) * DIL_O] for g in range(DIL_GROUPS)]
    parts.append(dw_gate)
    return _shard_cols(jnp.concatenate(parts, axis=1))


def _split_w_uq(w_uq_g):
    w = _unshard_cols(w_uq_g)
    K = w.shape[0]
    w = w.reshape(K, MLA_HEADS, QK_NOPE + QK_ROPE)
    return w[:, :, :QK_NOPE].reshape(K, MLA_HEADS * HEAD), _rope_pad(w[:, :, QK_NOPE:]).reshape(K, MLA_HEADS * HEAD)


def _merge_dw_uq(dw_n, dw_p):
    K = dw_n.shape[0]
    w = jnp.concatenate([dw_n.reshape(K, MLA_HEADS, HEAD), _rope_unpad(dw_p.reshape(K, MLA_HEADS, HEAD))], axis=-1)
    return _shard_cols(w.reshape(K, MLA_HEADS * (QK_NOPE + QK_ROPE)))


def _split_w_ukv(w_ukv_g):
    w = _unshard_cols(w_ukv_g)
    K = w.shape[0]
    w = w.reshape(K, MLA_HEADS, 2 * HEAD)
    return w[:, :, :HEAD].reshape(K, MLA_HEADS * HEAD), w[:, :, HEAD:].reshape(K, MLA_HEADS * HEAD)


def _merge_dw_ukv(dw_k, dw_v):
    K = dw_k.shape[0]
    w = jnp.concatenate([dw_k.reshape(K, MLA_HEADS, HEAD), dw_v.reshape(K, MLA_HEADS, HEAD)], axis=-1)
    return _shard_cols(w.reshape(K, MLA_HEADS * 2 * HEAD))


GATHER_GROUPS = (("w_in",), ("w_uq", "w_ukv", "w_o_mla", "w_o_dil", "w_out"), ("w_up", "w_down", "conv_w"))
SHARED_FETCH = ("w_in",)
REDUCE_GROUPS = (("w_down", "w_up"), ("w_out", "w_o_mla", "w_o_dil"), ("w_uq", "w_ukv", "w_in"))


def _local_step(x, tgt, W, fetch, emit):
    S, D = x.shape
    cos, sin_s = _rope_tables(S)
    w_lat, w_dil, w_gate = _split_w_in(fetch(0, x)["w_in"])

    h = _rmsnorm_fwd("attn_norm", x, W["attn_norm_g"])
    lat = _mm_nn("proj_lat", h, w_lat)
    qkv = [_mm_nn(f"proj_dil{g}", h, w_dil[g], o_dtype=BF) for g in range(DIL_GROUPS)]
    gpre = _mm_nn("proj_gate", h, w_gate)
    WB = fetch(1, gpre)
    w_uqn, w_uqp = _split_w_uq(WB["w_uq"])
    w_k, w_v = _split_w_ukv(WB["w_ukv"])
    w_o_mla, w_o_dil = WB["w_o_mla"], WB["w_o_dil"]
    w_out = WB["w_out"].reshape(D, D)
    qn_, kvn, kpe = _mla_prep(lat, W["q_norm_g"], W["kv_norm_g"], cos, sin_s)
    q_nope = _mm_nn("q_nope", qn_, w_uqn, o_dtype=BF)
    q_pe = _rope("q_rope", _mm_nn("q_pe", qn_, w_uqp), cos, sin_s, False)
    k_nope = _mm_nn("k_nope", kvn, w_k, o_dtype=BF)
    v_mla = _mm_nn("v_mla", kvn, w_v, o_dtype=BF)
    attn_a, lse_a = _mla_fwd(q_nope, q_pe, k_nope, kpe, v_mla)
    dil = [_dil_fwd(qkv[g], g) for g in range(DIL_GROUPS)]
    attn_b, lse_b = _dil_combine([o for o, _ in dil], [l for _, l in dil])
    o_a = _mm_nn("o_mla", attn_a, w_o_mla)
    o_b = _mm_nn("o_dil", attn_b, w_o_dil)
    merge = _merge_fwd(gpre, W["b_gate"], o_a, o_b)
    x1 = _mm_nn("out_proj", merge, w_out, add=x)
    WC = fetch(2, merge)
    w_up = WC["w_up"]
    G4, _, C = w_up.shape
    w_down = WC["w_down"].reshape(G4 // 2, C, D)
    conv_w = WC["conv_w"]
    conv_b = W["conv_b"].reshape(G4, 1, C)
    h2 = _rmsnorm_fwd("ffn_norm", x1, W["ffn_norm_g"])
    u_pre = _up_fwd(h2, w_up)
    act = _ffn_act(u_pre, conv_w, conv_b)
    x2 = _down_fwd(act, w_down, x1)
    dx2, d_final_g, loss8 = _final_loss(x2, tgt, W["final_norm_g"])

    d_act = _down_dgrad(dx2, w_down)
    dw_down = _down_wgrad(act, dx2)
    du = _ffn_act_bwd(u_pre, conv_w, conv_b, d_act)
    du_pre, d_conv_w, d_conv_b = _conv_bwd(du, u_pre, conv_w)
    dh2 = _up_dgrad(du_pre, w_up)
    dw_up = _up_wgrad(h2, du_pre)
    zero = emit(0, {"w_down": dw_down.reshape(N_CHIPS, (G4 // 2) * C // N_CHIPS, D), "w_up": dw_up})
    dx1, d_ffn_g = _rmsnorm_bwd("ffn_norm_bwd", dh2, x1, W["ffn_norm_g"] + zero, dx2)
    d_merge = _mm_nt("out_proj_dgrad", dx1, w_out)
    dw_out = _mm_tn("out_proj_wgrad", merge, dx1)
    d_oa, d_ob, d_gpre, d_b_gate = _merge_bwd(d_merge, gpre, W["b_gate"], o_a, o_b)
    d_attn_a = _mm_nt("o_mla_dgrad", d_oa, w_o_mla, o_dtype=BF)
    dw_o_mla = _mm_tn("o_mla_wgrad", attn_a, d_oa, shards=N_CHIPS)
    d_attn_b = _mm_nt("o_dil_dgrad", d_ob, w_o_dil, o_dtype=BF)
    dw_o_dil = _mm_tn("o_dil_wgrad", attn_b, d_ob, shards=N_CHIPS)
    zero = emit(1, {"w_out": dw_out.reshape(N_CHIPS, D // N_CHIPS, D), "w_o_mla": dw_o_mla, "w_o_dil": dw_o_dil})
    q_norm_g = W["q_norm_g"] + zero
    delta_b = _dil_delta(d_attn_b, attn_b)
    d_qkv = [_dil_bwd(qkv[g], d_attn_b, lse_b, delta_b, g) for g in range(DIL_GROUPS)]
    dq_nope, dq_pe_rot = _mla_bwd_dq(q_nope, q_pe, k_nope, kpe, v_mla, d_attn_a, attn_a, lse_a)
    dk_nope, dv_mla, dkpe_rot = _mla_bwd_dkv(q_nope, q_pe, k_nope, kpe, v_mla, d_attn_a, attn_a, lse_a)
    dq_pe = _rope("q_rope_bwd", dq_pe_rot, cos, sin_s, True)
    d_qn = _mm_nt("q_pe_dgrad", dq_pe, w_uqp, add=_mm_nt("q_nope_dgrad", dq_nope, w_uqn))
    d_kvn = _mm_nt("v_dgrad", dv_mla, w_v, add=_mm_nt("k_nope_dgrad", dk_nope, w_k))
    dw_uq = _merge_dw_uq(_mm_tn("q_nope_wgrad", qn_, dq_nope), _mm_tn("q_pe_wgrad", qn_, dq_pe))
    dw_ukv = _merge_dw_ukv(_mm_tn("k_nope_wgrad", kvn, dk_nope), _mm_tn("v_wgrad", kvn, dv_mla))
    d_lat, d_q_g, d_kv_g = _mla_prep_bwd(lat, q_norm_g, W["kv_norm_g"], cos, sin_s, d_qn, d_kvn, dkpe_rot)
    dw_in = _merge_dw_in(_mm_tn("proj_lat_wgrad", h, d_lat),
                         [_mm_tn(f"proj_dil{g}_wgrad", h, d_qkv[g]) for g in range(DIL_GROUPS)],
                         _mm_tn("proj_gate_wgrad", h, d_gpre))
    zero = emit(2, {"w_uq": dw_uq, "w_ukv": dw_ukv, "w_in": dw_in})
    dh = _mm_nt("proj_lat_dgrad", d_lat, w_lat + zero.astype(BF))
    for g in range(DIL_GROUPS):
        dh = _mm_nt(f"proj_dil{g}_dgrad", d_qkv[g], w_dil[g], add=dh)
    dh = _mm_nt("proj_gate_dgrad", d_gpre, w_gate, add=dh)
    grad_x, d_attn_g = _rmsnorm_bwd("attn_norm_bwd", dh, x, W["attn_norm_g"], dx1)

    small = {"attn_norm_g": d_attn_g, "b_gate": d_b_gate, "q_norm_g": d_q_g, "kv_norm_g": d_kv_g,
             "ffn_norm_g": d_ffn_g, "conv_w": d_conv_w, "conv_b": d_conv_b.reshape(1, G4 * C),
             "final_norm_g": d_final_g}
    return loss8[0, 0], grad_x, small


BIG = ("w_in", "w_uq", "w_ukv", "w_o_mla", "w_o_dil", "w_out", "w_up", "w_down")
SMALL = ("attn_norm_g", "b_gate", "q_norm_g", "kv_norm_g", "ffn_norm_g", "conv_w", "conv_b", "final_norm_g")
WEIGHTS = ("attn_norm_g", "w_in", "b_gate", "q_norm_g", "w_uq", "kv_norm_g", "w_ukv", "w_o_mla", "w_o_dil",
           "w_out", "ffn_norm_g", "w_up", "conv_w", "conv_b", "w_down", "final_norm_g")
SMALL_ROWS = 8
HALF_AXIS = {"w_down": 1}


def _gather_start(shards):
    names = [n for grp in GATHER_GROUPS for n in grp]
    chip = 2 * lax.axis_index("x") + lax.axis_index("y")
    c = lax.axis_index("c")
    srcs, lands = [], []
    for n in names:
        s = shards[n]
        lands.append(lax.dynamic_update_slice(lax.empty((N_CHIPS,) + s.shape, s.dtype), s[None], (chip, 0, 0)))
        if n in SHARED_FETCH:
            s = lax.dynamic_slice_in_dim(s, c * (s.shape[0] // 2), s.shape[0] // 2, 0)
        srcs.append(s)
    groups, at = [], 0
    for grp in GATHER_GROUPS:
        groups.append(list(range(at, at + len(grp))))
        at += len(grp)
    sems, srcs, lands, token = _exchange_start("gather_start", "gather", srcs, lands, groups)

    def fetch(i, after):
        idx = groups[i]
        _, got = _exchange_wait(f"gather_wait{i}", "gather", [srcs[j] for j in idx], [lands[j] for j in idx], sems[i], after)
        return {n: _pair_share(f"pair_share_{n}", g) if n in SHARED_FETCH else g for n, g in zip(GATHER_GROUPS[i], got)}

    return fetch, token[0, 0]


def _reduce_start(i, grads):
    names = REDUCE_GROUPS[i]
    hs = [_pair_sum(f"pair_sum_{n}", grads[n], HALF_AXIS.get(n, 0)) for n in names]
    lands = [lax.empty((3,) + h.shape[1:], h.dtype) for h in hs]
    sems, hs, lands, token = _exchange_start(f"reduce_start{i}", "scatter", hs, lands, [list(range(len(names)))])
    return (sems[0], hs, lands), token[0, 0]


def _reduce_finish(i, pending, after):
    sems, hs, lands = pending
    hs, lands = _exchange_wait(f"reduce_wait{i}", "scatter", hs, lands, sems, after)
    out = {}
    for n, h, landed in zip(REDUCE_GROUPS[i], hs, lands):
        out[n] = _pair_join(f"pair_join_{n}", _chip_total(f"chip_total_{n}", h, landed), HALF_AXIS.get(n, 0))
    return out


def _reduce_small(small):
    flat = [small[n].reshape(-1) for n in SMALL]
    sizes = [f.shape[0] for f in flat]
    total = sum(sizes)
    width = -(-total // (SMALL_ROWS * LANE)) * LANE
    packed = jnp.concatenate(flat + [jnp.zeros((SMALL_ROWS * width - total,), F32)]).reshape(SMALL_ROWS, width)
    red = _allreduce_small(packed).reshape(-1)
    out, off = {}, 0
    for n, s in zip(SMALL, sizes):
        out[n] = red[off:off + s]
        off += s
    return out


def kernel(x, attn_norm_g, w_in, b_gate, q_norm_g, w_uq, kv_norm_g, w_ukv, w_o_mla, w_o_dil, w_out, ffn_norm_g, w_up, conv_w, conv_b, w_down, final_norm_g, loss_target, m_attn_norm_g, m_w_in, m_b_gate, m_q_norm_g, m_w_uq, m_kv_norm_g, m_w_ukv, m_w_o_mla, m_w_o_dil, m_w_out, m_ffn_norm_g, m_w_up, m_conv_w, m_conv_b, m_w_down, m_final_norm_g, v_attn_norm_g, v_w_in, v_b_gate, v_q_norm_g, v_w_uq, v_kv_norm_g, v_w_ukv, v_w_o_mla, v_w_o_dil, v_w_out, v_ffn_norm_g, v_w_up, v_conv_w, v_conv_b, v_w_down, v_final_norm_g):
    given = dict(attn_norm_g=attn_norm_g, w_in=w_in, b_gate=b_gate, q_norm_g=q_norm_g, w_uq=w_uq, kv_norm_g=kv_norm_g,
                 w_ukv=w_ukv, w_o_mla=w_o_mla, w_o_dil=w_o_dil, w_out=w_out, ffn_norm_g=ffn_norm_g, w_up=w_up,
                 conv_w=conv_w, conv_b=conv_b, w_down=w_down, final_norm_g=final_norm_g)
    moments_m = dict(attn_norm_g=m_attn_norm_g, w_in=m_w_in, b_gate=m_b_gate, q_norm_g=m_q_norm_g, w_uq=m_w_uq,
                     kv_norm_g=m_kv_norm_g, w_ukv=m_w_ukv, w_o_mla=m_w_o_mla, w_o_dil=m_w_o_dil, w_out=m_w_out,
                     ffn_norm_g=m_ffn_norm_g, w_up=m_w_up, conv_w=m_conv_w, conv_b=m_conv_b, w_down=m_w_down,
                     final_norm_g=m_final_norm_g)
    moments_v = dict(attn_norm_g=v_attn_norm_g, w_in=v_w_in, b_gate=v_b_gate, q_norm_g=v_q_norm_g, w_uq=v_w_uq,
                     kv_norm_g=v_kv_norm_g, w_ukv=v_w_ukv, w_o_mla=v_w_o_mla, w_o_dil=v_w_o_dil, w_out=v_w_out,
                     ffn_norm_g=v_ffn_norm_g, w_up=v_w_up, conv_w=v_conv_w, conv_b=v_conv_b, w_down=v_w_down,
                     final_norm_g=v_final_norm_g)

    shards = {n: given[n][0].astype(BF) for n in BIG}
    shards["conv_w"] = given["conv_w"][0]
    fetch, zero = _gather_start(shards)
    W = {n: given[n] for n in ("b_gate", "q_norm_g", "kv_norm_g", "ffn_norm_g", "conv_b")}
    W["attn_norm_g"] = given["attn_norm_g"] + zero
    W["final_norm_g"] = given["final_norm_g"].reshape(1, -1)

    pending = {}

    def emit(i, grads):
        pending[i], token = _reduce_start(i, grads)
        return token

    loss_part, grad_x, small = _local_step(x[0], loss_target[0], W, fetch, emit)
    loss = lax.psum(loss_part, ("x", "y", "c"))
    grads, delta, new_m, new_v = {}, {}, {}, {}

    def adamw(n, g):
        shp = given[n].shape
        two_d = (-1, shp[-1]) if len(shp) > 1 else (1, -1)
        go, d, nm, nv = _adamw(f"adamw_{n}", given[n].reshape(two_d), g.reshape(two_d),
                               moments_m[n].reshape(two_d), moments_v[n].reshape(two_d))
        grads[n], delta[n], new_m[n], new_v[n] = go.reshape(shp), d.reshape(shp), nm.reshape(shp), nv.reshape(shp)

    after = grad_x
    for i in range(len(REDUCE_GROUPS)):
        for n, g in _reduce_finish(i, pending[i], after).items():
            adamw(n, g)
        after = delta[REDUCE_GROUPS[i][-1]]
    g_small = _reduce_small(small)
    chip = 2 * lax.axis_index("x") + lax.axis_index("y")
    for n in SMALL:
        if n == "conv_w":
            full = g_small[n].reshape(N_CHIPS, 3, -1)
            adamw(n, lax.dynamic_index_in_dim(full, chip, 0, keepdims=True))
        else:
            adamw(n, g_small[n])

    return (loss, grad_x[None], *[grads[n] for n in WEIGHTS], *[delta[n] for n in WEIGHTS],
            *[new_m[n] for n in WEIGHTS], *[new_v[n] for n in WEIGHTS])
```

```python
import functools
import math

import numpy as np
import jax
import jax.numpy as jnp
from jax import lax
from jax.experimental import pallas as pl
from jax.experimental.pallas import tpu as pltpu

F32 = jnp.float32
BF = jnp.bfloat16
MESH = pl.DeviceIdType.MESH

D_MODEL = 2048
MLA_HEADS = 8
QK_NOPE = 128
QK_ROPE = 64
Q_RANK = 512
KV_RANK = 256
ROPE_THETA = 10000.0
DIL_PATTERNS = ((128, 1), (512, 4), (2048, 16))
DIL_GROUPS = 3
DIL_HPG = 4
DIL_HEADS = 12
HEAD = 128
DIL_BLOCK = 128
ALIBI_MAX_BIAS = 8.0
NORM_EPS = 1e-6
N_CHIPS = 4
ADAM_LR = 0.001
ADAM_B1 = 0.9
ADAM_B2 = 0.999
ADAM_EPS = 1e-08
ADAM_WD = 0.01
ADAM_STEP = 10

LANE = 128
VMEM_LIMIT = 56 * 1024 * 1024
MLA_SCALE = (QK_NOPE + QK_ROPE) ** -0.5
DIL_SCALE = HEAD ** -0.5


def _params(*sem):
    return pltpu.CompilerParams(dimension_semantics=sem, vmem_limit_bytes=VMEM_LIMIT)


def _tile(n, pref):
    t = (pref // LANE) * LANE
    while t >= LANE:
        if n % t == 0:
            return t
        t -= LANE
    return n


NN = (((1,), (0,)), ((), ()))
NT = (((1,), (1,)), ((), ()))
TN = (((0,), (0,)), ((), ()))


def _mm_call(name, a, b, add, *, grid, a_spec, b_spec, add_spec, o_spec, o_shape, o_dtype, acc_shape, dims, nk):
    nax = len(grid)

    def body(*refs):
        if add is None:
            a_ref, b_ref, o_ref = refs[:3]
            c_ref = None
            scr = refs[3:]
        else:
            a_ref, b_ref, c_ref, o_ref = refs[:4]
            scr = refs[4:]
        prod = lax.dot_general(a_ref[...].astype(BF), b_ref[...].astype(BF), dims, preferred_element_type=F32)
        if nk == 1:
            if c_ref is not None:
                prod = prod + c_ref[...]
            o_ref[...] = prod.astype(o_ref.dtype)
        else:
            acc = scr[0]
            k = pl.program_id(nax - 1)

            @pl.when(k == 0)
            def _():
                if c_ref is not None:
                    acc[...] = prod + c_ref[...]
                else:
                    acc[...] = prod

            @pl.when(k > 0)
            def _():
                acc[...] += prod

            @pl.when(k == nk - 1)
            def _():
                o_ref[...] = acc[...].astype(o_ref.dtype)

    ins = [a, b] + ([] if add is None else [add])
    specs = [a_spec, b_spec] + ([] if add is None else [add_spec])
    sem = ("parallel",) * (nax - 1) + ("arbitrary",)
    return pl.pallas_call(
        body, name=name, grid=grid, in_specs=specs, out_specs=o_spec,
        out_shape=jax.ShapeDtypeStruct(o_shape, o_dtype),
        scratch_shapes=[] if nk == 1 else [pltpu.VMEM(acc_shape, F32)],
        compiler_params=_params(*sem),
    )(*ins)


def _mm_nn(name, a, b, *, add=None, o_dtype=F32):
    M, K = a.shape
    sharded = b.ndim == 3
    Ns = b.shape[-1]
    N = Ns * (b.shape[0] if sharded else 1)
    tm, tn, tk = _tile(M, 1024), _tile(Ns, 1024), _tile(K, 2048)
    per = Ns // tn
    nk = K // tk
    if sharded:
        b_spec = pl.BlockSpec((None, tk, tn), lambda i, j, k: (j // per, k, j % per))
    else:
        b_spec = pl.BlockSpec((tk, tn), lambda i, j, k: (k, j))
    return _mm_call(
        name, a, b, add, grid=(M // tm, N // tn, nk),
        a_spec=pl.BlockSpec((tm, tk), lambda i, j, k: (i, k)), b_spec=b_spec,
        add_spec=pl.BlockSpec((tm, tn), lambda i, j, k: (i, j)),
        o_spec=pl.BlockSpec((tm, tn), lambda i, j, k: (i, j)),
        o_shape=(M, N), o_dtype=o_dtype, acc_shape=(tm, tn), dims=NN, nk=nk)


def _mm_nt(name, a, b, *, add=None, o_dtype=F32):
    M, K = a.shape
    sharded = b.ndim == 3
    N, Ks = b.shape[-2], b.shape[-1]
    tm, tn, tk = _tile(M, 1024), _tile(N, 1024), _tile(Ks, 2048)
    per = Ks // tk
    nk = K // tk
    if sharded:
        b_spec = pl.BlockSpec((None, tn, tk), lambda i, j, k: (k // per, j, k % per))
    else:
        b_spec = pl.BlockSpec((tn, tk), lambda i, j, k: (j, k))
    return _mm_call(
        name, a, b, add, grid=(M // tm, N // tn, nk),
        a_spec=pl.BlockSpec((tm, tk), lambda i, j, k: (i, k)), b_spec=b_spec,
        add_spec=pl.BlockSpec((tm, tn), lambda i, j, k: (i, j)),
        o_spec=pl.BlockSpec((tm, tn), lambda i, j, k: (i, j)),
        o_shape=(M, N), o_dtype=o_dtype, acc_shape=(tm, tn), dims=NT, nk=nk)


def _mm_tn(name, a, b, *, shards=1, o_dtype=BF):
    S, M = a.shape
    N = b.shape[1]
    Ns = N // shards
    tm, tn, tk = _tile(M, 1024), _tile(Ns, 1024), _tile(S, 2048)
    per = Ns // tn
    nk = S // tk
    if shards > 1:
        o_spec = pl.BlockSpec((None, tm, tn), lambda i, j, k: (j // per, i, j % per))
        o_shape = (shards, M, Ns)
    else:
        o_spec = pl.BlockSpec((tm, tn), lambda i, j, k: (i, j))
        o_shape = (M, N)
    return _mm_call(
        name, a, b, None, grid=(M // tm, N // tn, nk),
        a_spec=pl.BlockSpec((tk, tm), lambda i, j, k: (k, i)),
        b_spec=pl.BlockSpec((tk, tn), lambda i, j, k: (k, j)),
        add_spec=None, o_spec=o_spec, o_shape=o_shape, o_dtype=o_dtype, acc_shape=(tm, tn), dims=TN, nk=nk)


def _up_fwd(h2, w_up):
    S, D = h2.shape
    G, _, C = w_up.shape
    tm = _tile(S, 512)
    return _mm_call(
        "up_fwd", h2, w_up, None, grid=(G, S // tm, 1),
        a_spec=pl.BlockSpec((tm, D), lambda g, i, k: (i, 0)),
        b_spec=pl.BlockSpec((None, D, C), lambda g, i, k: (g, 0, 0)),
        add_spec=None, o_spec=pl.BlockSpec((None, tm, C), lambda g, i, k: (g, i, 0)),
        o_shape=(G, S, C), o_dtype=BF, acc_shape=None, dims=NN, nk=1)


def _up_dgrad(du_pre, w_up):
    G, S, C = du_pre.shape
    D = w_up.shape[1]
    tm, tn = _tile(S, 1024), _tile(D, 1024)
    return _mm_call(
        "up_dgrad", du_pre, w_up, None, grid=(S // tm, D // tn, G),
        a_spec=pl.BlockSpec((None, tm, C), lambda i, j, g: (g, i, 0)),
        b_spec=pl.BlockSpec((None, tn, C), lambda i, j, g: (g, j, 0)),
        add_spec=None, o_spec=pl.BlockSpec((tm, tn), lambda i, j, g: (i, j)),
        o_shape=(S, D), o_dtype=F32, acc_shape=(tm, tn), dims=NT, nk=G)


def _up_wgrad(h2, du_pre):
    G, S, C = du_pre.shape
    D = h2.shape[1]
    tm, tk = _tile(D, 512), _tile(S, 2048)
    return _mm_call(
        "up_wgrad", h2, du_pre, None, grid=(G, D // tm, S // tk),
        a_spec=pl.BlockSpec((tk, tm), lambda g, i, k: (k, i)),
        b_spec=pl.BlockSpec((None, tk, C), lambda g, i, k: (g, k, 0)),
        add_spec=None, o_spec=pl.BlockSpec((None, tm, C), lambda g, i, k: (g, i, 0)),
        o_shape=(G, D, C), o_dtype=BF, acc_shape=(tm, C), dims=TN, nk=S // tk)


def _down_fwd(act, w_down, x1):
    G, S, C = act.shape
    D = w_down.shape[2]
    tm, tn = _tile(S, 1024), _tile(D, 1024)
    return _mm_call(
        "down_fwd", act, w_down, x1, grid=(S // tm, D // tn, G),
        a_spec=pl.BlockSpec((None, tm, C), lambda i, j, g: (g, i, 0)),
        b_spec=pl.BlockSpec((None, C, tn), lambda i, j, g: (g, 0, j)),
        add_spec=pl.BlockSpec((tm, tn), lambda i, j, g: (i, j)),
        o_spec=pl.BlockSpec((tm, tn), lambda i, j, g: (i, j)),
        o_shape=(S, D), o_dtype=F32, acc_shape=(tm, tn), dims=NN, nk=G)


def _down_dgrad(dx2, w_down):
    S, D = dx2.shape
    G, C, _ = w_down.shape
    tm = _tile(S, 512)
    return _mm_call(
        "down_dgrad", dx2, w_down, None, grid=(G, S // tm, 1),
        a_spec=pl.BlockSpec((tm, D), lambda g, i, k: (i, 0)),
        b_spec=pl.BlockSpec((None, C, D), lambda g, i, k: (g, 0, 0)),
        add_spec=None, o_spec=pl.BlockSpec((None, tm, C), lambda g, i, k: (g, i, 0)),
        o_shape=(G, S, C), o_dtype=BF, acc_shape=None, dims=NT, nk=1)


def _down_wgrad(act, dx2):
    G, S, C = act.shape
    D = dx2.shape[1]
    tn, tk = _tile(D, 512), _tile(S, 1024)
    return _mm_call(
        "down_wgrad", act, dx2, None, grid=(G, D // tn, S // tk),
        a_spec=pl.BlockSpec((None, tk, C), lambda g, j, k: (g, k, 0)),
        b_spec=pl.BlockSpec((tk, tn), lambda g, j, k: (k, j)),
        add_spec=None, o_spec=pl.BlockSpec((None, C, tn), lambda g, j, k: (g, 0, j)),
        o_shape=(G, C, D), o_dtype=BF, acc_shape=(C, tn), dims=TN, nk=S // tk)


def _row(ts, c):
    return pl.BlockSpec((ts, c), lambda i: (i, 0))


def _bcast(r, c):
    return pl.BlockSpec((r, c), lambda i: (0, 0))


def _accumulate(i, ref, val):
    @pl.when(i == 0)
    def _():
        ref[...] = val

    @pl.when(i > 0)
    def _():
        ref[...] += val


def _rstd(xv):
    return lax.rsqrt(jnp.mean(xv * xv, axis=-1, keepdims=True) + NORM_EPS)


def _rmsnorm_fwd(name, x, g):
    S, D = x.shape
    ts = _tile(S, 512)

    def body(x_ref, g_ref, o_ref):
        xv = x_ref[...]
        o_ref[...] = (xv * _rstd(xv) * g_ref[...]).astype(o_ref.dtype)

    return pl.pallas_call(
        body, name=name, grid=(S // ts,), in_specs=[_row(ts, D), _bcast(1, D)], out_specs=_row(ts, D),
        out_shape=jax.ShapeDtypeStruct((S, D), BF), compiler_params=_params("parallel"))(x, g)


def _norm_bwd_rows(dy, xv, g):
    r = _rstd(xv)
    xh = xv * r
    dxh = dy * g
    dx = r * (dxh - xh * jnp.mean(dxh * xh, axis=-1, keepdims=True))
    return dx, jnp.sum(dy * xh, axis=0, keepdims=True)


def _rmsnorm_bwd(name, dy, x, g, res):
    S, D = x.shape
    ts = _tile(S, 512)

    def body(dy_ref, x_ref, g_ref, res_ref, dx_ref, dg_ref):
        dx, dg = _norm_bwd_rows(dy_ref[...], x_ref[...], g_ref[...])
        dx_ref[...] = dx + res_ref[...]
        _accumulate(pl.program_id(0), dg_ref, dg)

    return pl.pallas_call(
        body, name=name, grid=(S // ts,),
        in_specs=[_row(ts, D), _row(ts, D), _bcast(1, D), _row(ts, D)],
        out_specs=[_row(ts, D), _bcast(1, D)],
        out_shape=[jax.ShapeDtypeStruct((S, D), F32), jax.ShapeDtypeStruct((1, D), F32)],
        compiler_params=_params("arbitrary"))(dy, x, g, res)


def _rope_tables(S):
    half = QK_ROPE // 2
    pos = jnp.arange(S, dtype=F32)
    inv_freq = ROPE_THETA ** (-jnp.arange(0, QK_ROPE, 2, dtype=F32) / QK_ROPE)
    ang = pos[:, None] * inv_freq[None, :]
    cos, sin = jnp.cos(ang), jnp.sin(ang)
    z = jnp.zeros((S, half), F32)
    return jnp.concatenate([cos, z, cos, z], axis=1), jnp.concatenate([-sin, z, sin, z], axis=1)


def _rope_lanes(x, cos, sin_signed, inverse):
    if inverse:
        return x * cos + pltpu.roll(x * sin_signed, LANE // 2, 1)
    return x * cos + pltpu.roll(x, LANE // 2, 1) * sin_signed


def _rope(name, x, cos, sin_signed, inverse):
    S, W = x.shape
    ts = _tile(S, 512)

    def body(x_ref, c_ref, s_ref, o_ref):
        c, s = c_ref[...], s_ref[...]
        for h in range(W // LANE):
            sl = slice(h * LANE, (h + 1) * LANE)
            o_ref[:, sl] = _rope_lanes(x_ref[:, sl], c, s, inverse).astype(o_ref.dtype)

    return pl.pallas_call(
        body, name=name, grid=(S // ts,), in_specs=[_row(ts, W), _row(ts, LANE), _row(ts, LANE)],
        out_specs=_row(ts, W), out_shape=jax.ShapeDtypeStruct((S, W), BF),
        compiler_params=_params("parallel"))(x, cos, sin_signed)


LAT_W = 1024
_CQ = slice(0, Q_RANK)
_CKV = slice(Q_RANK, Q_RANK + KV_RANK)
_KPE = slice(Q_RANK + KV_RANK, Q_RANK + KV_RANK + LANE)


def _mla_prep(lat, qg, kvg, cos, sin_signed):
    S = lat.shape[0]
    ts = _tile(S, 512)

    def body(lat_ref, qg_ref, kvg_ref, c_ref, s_ref, qn_ref, kvn_ref, kpe_ref):
        cq = lat_ref[:, _CQ]
        qn_ref[...] = (cq * _rstd(cq) * qg_ref[...]).astype(BF)
        ckv = lat_ref[:, _CKV]
        kvn_ref[...] = (ckv * _rstd(ckv) * kvg_ref[...]).astype(BF)
        kpe_ref[...] = _rope_lanes(lat_ref[:, _KPE], c_ref[...], s_ref[...], False).astype(BF)

    return pl.pallas_call(
        body, name="mla_prep", grid=(S // ts,),
        in_specs=[_row(ts, LAT_W), _bcast(1, Q_RANK), _bcast(1, KV_RANK), _row(ts, LANE), _row(ts, LANE)],
        out_specs=[_row(ts, Q_RANK), _row(ts, KV_RANK), _row(ts, LANE)],
        out_shape=[jax.ShapeDtypeStruct((S, Q_RANK), BF), jax.ShapeDtypeStruct((S, KV_RANK), BF),
                   jax.ShapeDtypeStruct((S, LANE), BF)],
        compiler_params=_params("parallel"))(lat, qg, kvg, cos, sin_signed)


def _mla_prep_bwd(lat, qg, kvg, cos, sin_signed, d_qn, d_kvn, d_kpe):
    S = lat.shape[0]
    ts = _tile(S, 512)

    def body(lat_ref, qg_ref, kvg_ref, c_ref, s_ref, dqn_ref, dkvn_ref, dkpe_ref, dlat_ref, dqg_ref, dkvg_ref):
        i = pl.program_id(0)
        dcq, dqg = _norm_bwd_rows(dqn_ref[...], lat_ref[:, _CQ], qg_ref[...])
        dckv, dkvg = _norm_bwd_rows(dkvn_ref[...], lat_ref[:, _CKV], kvg_ref[...])
        dlat_ref[:, _CQ] = dcq.astype(BF)
        dlat_ref[:, _CKV] = dckv.astype(BF)
        dkpe = dkpe_ref[0]
        for g in range(1, d_kpe.shape[0]):
            dkpe = dkpe + dkpe_ref[g]
        dlat_ref[:, _KPE] = _rope_lanes(dkpe, c_ref[...], s_ref[...], True).astype(BF)
        dlat_ref[:, _KPE.stop:] = jnp.zeros((ts, LAT_W - _KPE.stop), BF)
        _accumulate(i, dqg_ref, dqg)
        _accumulate(i, dkvg_ref, dkvg)

    return pl.pallas_call(
        body, name="mla_prep_bwd", grid=(S // ts,),
        in_specs=[_row(ts, LAT_W), _bcast(1, Q_RANK), _bcast(1, KV_RANK), _row(ts, LANE), _row(ts, LANE),
                  _row(ts, Q_RANK), _row(ts, KV_RANK), pl.BlockSpec((d_kpe.shape[0], ts, LANE), lambda i: (0, i, 0))],
        out_specs=[_row(ts, LAT_W), _bcast(1, Q_RANK), _bcast(1, KV_RANK)],
        out_shape=[jax.ShapeDtypeStruct((S, LAT_W), BF), jax.ShapeDtypeStruct((1, Q_RANK), F32),
                   jax.ShapeDtypeStruct((1, KV_RANK), F32)],
        compiler_params=_params("arbitrary"))(lat, qg, kvg, cos, sin_signed, d_qn, d_kvn, d_kpe)


def _sigmoid(z):
    return 1.0 / (1.0 + jnp.exp(-z))


def _merge_fwd(gpre, b_gate, o_a, o_b):
    S, D = o_a.shape
    ts = _tile(S, 256)

    def body(g_ref, b_ref, oa_ref, ob_ref, m_ref):
        ga = _sigmoid(g_ref[:, :D] + b_ref[:, :D])
        gb = _sigmoid(g_ref[:, D:] + b_ref[:, D:])
        m_ref[...] = (ga * oa_ref[...] + gb * ob_ref[...]).astype(BF)

    return pl.pallas_call(
        body, name="merge_fwd", grid=(S // ts,),
        in_specs=[_row(ts, 2 * D), _bcast(1, 2 * D), _row(ts, D), _row(ts, D)], out_specs=_row(ts, D),
        out_shape=jax.ShapeDtypeStruct((S, D), BF), compiler_params=_params("parallel"))(gpre, b_gate, o_a, o_b)


def _merge_bwd(d_merge, gpre, b_gate, o_a, o_b):
    S, D = o_a.shape
    ts = _tile(S, 256)

    def body(dm_ref, g_ref, b_ref, oa_ref, ob_ref, doa_ref, dob_ref, dg_ref, db_ref):
        dm = dm_ref[...]
        ga = _sigmoid(g_ref[:, :D] + b_ref[:, :D])
        gb = _sigmoid(g_ref[:, D:] + b_ref[:, D:])
        doa_ref[...] = (dm * ga).astype(BF)
        dob_ref[...] = (dm * gb).astype(BF)
        dga = dm * oa_ref[...] * ga * (1.0 - ga)
        dgb = dm * ob_ref[...] * gb * (1.0 - gb)
        dg_ref[:, :D] = dga.astype(BF)
        dg_ref[:, D:] = dgb.astype(BF)
        i = pl.program_id(0)
        part = jnp.concatenate([jnp.sum(dga, axis=0, keepdims=True), jnp.sum(dgb, axis=0, keepdims=True)], axis=1)
        _accumulate(i, db_ref, part)

    return pl.pallas_call(
        body, name="merge_bwd", grid=(S // ts,),
        in_specs=[_row(ts, D), _row(ts, 2 * D), _bcast(1, 2 * D), _row(ts, D), _row(ts, D)],
        out_specs=[_row(ts, D), _row(ts, D), _row(ts, 2 * D), _bcast(1, 2 * D)],
        out_shape=[jax.ShapeDtypeStruct((S, D), BF), jax.ShapeDtypeStruct((S, D), BF),
                   jax.ShapeDtypeStruct((S, 2 * D), BF), jax.ShapeDtypeStruct((1, 2 * D), F32)],
        compiler_params=_params("arbitrary"))(d_merge, gpre, b_gate, o_a, o_b)


def _final_loss(x2, tgt, gf):
    S, D = x2.shape
    ts = _tile(S, 512)

    def body(x_ref, t_ref, g_ref, dx_ref, dg_ref, loss_ref):
        i = pl.program_id(0)
        xv = x_ref[...]
        g = g_ref[...]
        y = xv * _rstd(xv) * g
        err = y - t_ref[...]
        dx, dg = _norm_bwd_rows(err * (1.0 / D), xv, g)
        dx_ref[...] = dx
        _accumulate(i, dg_ref, dg)
        part = 0.5 * jnp.sum(jnp.mean(err * err, axis=-1, keepdims=True), axis=0, keepdims=True)
        _accumulate(i, loss_ref, jnp.broadcast_to(part, (8, LANE)))

    return pl.pallas_call(
        body, name="final_loss", grid=(S // ts,),
        in_specs=[_row(ts, D), _row(ts, D), _bcast(1, D)],
        out_specs=[_row(ts, D), _bcast(1, D), _bcast(8, LANE)],
        out_shape=[jax.ShapeDtypeStruct((S, D), F32), jax.ShapeDtypeStruct((1, D), F32),
                   jax.ShapeDtypeStruct((8, LANE), F32)],
        compiler_params=_params("arbitrary"))(x2, tgt, gf)


HALO = 16


SUB = 8


def _shift_down(cur, prev, k, rows):
    out = pltpu.roll(cur, k, 0)
    head = out[:SUB]
    for j in range(k):
        head = jnp.where(rows == j, prev[HALO - k + j:HALO - k + j + 1, :], head)
    return jnp.concatenate([head, out[SUB:]], axis=0)


def _shift_up(cur, nxt, k, rows, ts):
    out = pltpu.roll(cur, ts - k, 0)
    tail = out[ts - SUB:]
    for j in range(k):
        tail = jnp.where(rows == SUB - k + j, nxt[j:j + 1, :], tail)
    return jnp.concatenate([out[:ts - SUB], tail], axis=0)


def _conv_rows(cur, prev, w, b, rows):
    return b + w[0:1, :] * _shift_down(cur, prev, 2, rows) + w[1:2, :] * _shift_down(cur, prev, 1, rows) + w[2:3, :] * cur


def _conv_specs(ts, C, shard_of):
    nh = ts // HALO
    cur = pl.BlockSpec((None, ts, C), lambda g, i: (shard_of(g), i, 0))
    prev = pl.BlockSpec((None, HALO, C), lambda g, i: (shard_of(g), jnp.maximum(i * nh - 1, 0), 0))
    return cur, prev


def _ffn_act(u_pre, conv_w, conv_b):
    G4, S, C = u_pre.shape
    G = G4 // 2
    ts = _tile(S, 256)

    def body(up_ref, upp_ref, gt_ref, gtp_ref, wu_ref, wg_ref, bu_ref, bg_ref, act_ref):
        first = pl.program_id(1) == 0
        rows = lax.broadcasted_iota(jnp.int32, (SUB, C), 0)
        pu = jnp.where(first, 0.0, upp_ref[...].astype(F32))
        pg = jnp.where(first, 0.0, gtp_ref[...].astype(F32))
        up = _conv_rows(up_ref[...].astype(F32), pu, wu_ref[...], bu_ref[...], rows)
        gate = _conv_rows(gt_ref[...].astype(F32), pg, wg_ref[...], bg_ref[...], rows)
        act_ref[...] = (gate * _sigmoid(gate) * up).astype(BF)

    cur_u, prev_u = _conv_specs(ts, C, lambda g: g)
    cur_g, prev_g = _conv_specs(ts, C, lambda g: g + G)
    w_u = pl.BlockSpec((None, 3, C), lambda g, i: (g, 0, 0))
    w_g = pl.BlockSpec((None, 3, C), lambda g, i: (g + G, 0, 0))
    b_u = pl.BlockSpec((None, 1, C), lambda g, i: (g, 0, 0))
    b_g = pl.BlockSpec((None, 1, C), lambda g, i: (g + G, 0, 0))
    return pl.pallas_call(
        body, name="ffn_act", grid=(G, S // ts),
        in_specs=[cur_u, prev_u, cur_g, prev_g, w_u, w_g, b_u, b_g],
        out_specs=pl.BlockSpec((None, ts, C), lambda g, i: (g, i, 0)),
        out_shape=jax.ShapeDtypeStruct((G, S, C), BF),
        compiler_params=_params("parallel", "parallel"))(u_pre, u_pre, u_pre, u_pre, conv_w, conv_w, conv_b, conv_b)


def _ffn_act_bwd(u_pre, conv_w, conv_b, d_act):
    G4, S, C = u_pre.shape
    G = G4 // 2
    ts = _tile(S, 256)

    def body(up_ref, upp_ref, gt_ref, gtp_ref, wu_ref, wg_ref, bu_ref, bg_ref, da_ref, du_ref):
        first = pl.program_id(1) == 0
        rows = lax.broadcasted_iota(jnp.int32, (SUB, C), 0)
        pu = jnp.where(first, 0.0, upp_ref[...].astype(F32))
        pg = jnp.where(first, 0.0, gtp_ref[...].astype(F32))
        up = _conv_rows(up_ref[...].astype(F32), pu, wu_ref[...], bu_ref[...], rows)
        gate = _conv_rows(gt_ref[...].astype(F32), pg, wg_ref[...], bg_ref[...], rows)
        sg = _sigmoid(gate)
        da = da_ref[...].astype(F32)
        du_ref[0] = (da * (gate * sg)).astype(BF)
        du_ref[1] = (da * up * (sg * (1.0 + gate * (1.0 - sg)))).astype(BF)

    cur_u, prev_u = _conv_specs(ts, C, lambda g: g)
    cur_g, prev_g = _conv_specs(ts, C, lambda g: g + G)
    w_u = pl.BlockSpec((None, 3, C), lambda g, i: (g, 0, 0))
    w_g = pl.BlockSpec((None, 3, C), lambda g, i: (g + G, 0, 0))
    b_u = pl.BlockSpec((None, 1, C), lambda g, i: (g, 0, 0))
    b_g = pl.BlockSpec((None, 1, C), lambda g, i: (g + G, 0, 0))
    blk = pl.BlockSpec((None, ts, C), lambda g, i: (g, i, 0))
    du = pl.pallas_call(
        body, name="ffn_act_bwd", grid=(G, S // ts),
        in_specs=[cur_u, prev_u, cur_g, prev_g, w_u, w_g, b_u, b_g, blk],
        out_specs=pl.BlockSpec((2, None, ts, C), lambda g, i: (0, g, i, 0)),
        out_shape=jax.ShapeDtypeStruct((2, G, S, C), BF),
        compiler_params=_params("parallel", "parallel"))(
            u_pre, u_pre, u_pre, u_pre, conv_w, conv_w, conv_b, conv_b, d_act)
    return du.reshape(G4, S, C)


def _conv_bwd(du, u_pre, conv_w):
    G4, S, C = du.shape
    ts = _tile(S, 256)
    nh = ts // HALO
    last_halo = S // HALO - 1

    def body(du_ref, dun_ref, u_ref, up_ref, w_ref, dpre_ref, dw_ref, db_ref):
        i = pl.program_id(1)
        rows = lax.broadcasted_iota(jnp.int32, (SUB, C), 0)
        du_c = du_ref[...].astype(F32)
        nxt = jnp.where(i == pl.num_programs(1) - 1, 0.0, dun_ref[...].astype(F32))
        prev = jnp.where(i == 0, 0.0, up_ref[...].astype(F32))
        w = w_ref[...]
        dpre = w[2:3, :] * du_c + w[1:2, :] * _shift_up(du_c, nxt, 1, rows, ts) + w[0:1, :] * _shift_up(du_c, nxt, 2, rows, ts)
        dpre_ref[...] = dpre.astype(BF)
        u_c = u_ref[...].astype(F32)
        dw = jnp.concatenate([
            jnp.sum(du_c * _shift_down(u_c, prev, 2, rows), axis=0, keepdims=True),
            jnp.sum(du_c * _shift_down(u_c, prev, 1, rows), axis=0, keepdims=True),
            jnp.sum(du_c * u_c, axis=0, keepdims=True)], axis=0)
        _accumulate(i, dw_ref, dw)
        _accumulate(i, db_ref, jnp.sum(du_c, axis=0, keepdims=True))

    cur = pl.BlockSpec((None, ts, C), lambda g, i: (g, i, 0))
    nxt = pl.BlockSpec((None, HALO, C), lambda g, i: (g, jnp.minimum((i + 1) * nh, last_halo), 0))
    prev = pl.BlockSpec((None, HALO, C), lambda g, i: (g, jnp.maximum(i * nh - 1, 0), 0))
    return pl.pallas_call(
        body, name="conv_bwd", grid=(G4, S // ts),
        in_specs=[cur, nxt, cur, prev, pl.BlockSpec((None, 3, C), lambda g, i: (g, 0, 0))],
        out_specs=[cur, pl.BlockSpec((None, 3, C), lambda g, i: (g, 0, 0)), pl.BlockSpec((None, 1, C), lambda g, i: (g, 0, 0))],
        out_shape=[jax.ShapeDtypeStruct((G4, S, C), BF), jax.ShapeDtypeStruct((G4, 3, C), F32),
                   jax.ShapeDtypeStruct((G4, 1, C), F32)],
        compiler_params=_params("parallel", "arbitrary"))(du, du, u_pre, u_pre, conv_w)


MLA_T = 512
MLA_HB = 4
MLA_HW = MLA_HB * HEAD


def _mla_pairs(n, by_row):
    if by_row:
        pairs = [(i, j) for i in range(n) for j in range(i + 1)]
    else:
        pairs = [(i, j) for j in range(n) for i in range(j, n)]
    return jnp.asarray([p[0] for p in pairs], jnp.int32), jnp.asarray([p[1] for p in pairs], jnp.int32)


def _mla_specs(S):
    q = pl.BlockSpec((MLA_T, MLA_HW), lambda g, t, it, jt: (it[t], g))
    k = pl.BlockSpec((MLA_T, MLA_HW), lambda g, t, it, jt: (jt[t], g))
    kpe = pl.BlockSpec((MLA_T, HEAD), lambda g, t, it, jt: (jt[t], 0))
    lse = pl.BlockSpec((MLA_HB, MLA_T, LANE), lambda g, t, it, jt: (g, it[t], 0))
    return q, k, kpe, lse


def _mla_head(ref, hh):
    return ref[:, hh * HEAD:(hh + 1) * HEAD]


LOG2E = math.log2(math.e)
MLA_EXP2_SCALE = MLA_SCALE * LOG2E


def _mla_scores(qn_ref, qp_ref, kn_ref, kpe, hh, ok):
    q = jnp.concatenate([_mla_head(qn_ref, hh), _mla_head(qp_ref, hh)], axis=1)
    k = jnp.concatenate([_mla_head(kn_ref, hh), kpe], axis=1)
    s = lax.dot_general(q, k, NT, preferred_element_type=F32)
    return q, k, s if ok is None else jnp.where(ok, s, -jnp.inf)


def _mla_diagonal_mask():
    row = lax.broadcasted_iota(jnp.int32, (MLA_T, MLA_T), 0)
    col = lax.broadcasted_iota(jnp.int32, (MLA_T, MLA_T), 1)
    return col <= row


def _mla_step(i, j, step):
    @pl.when(j < i)
    def _():
        step(None)

    @pl.when(j == i)
    def _():
        step(_mla_diagonal_mask())


def _mla_fwd(qn, qp, kn, kpe, v):
    S = qn.shape[0]
    it, jt = _mla_pairs(S // MLA_T, True)

    def body(it_ref, jt_ref, qn_ref, qp_ref, kn_ref, kpe_ref, v_ref, o_ref, lse_ref, m_scr, acc_scr):
        t = pl.program_id(1)
        i, j = it_ref[t], jt_ref[t]

        @pl.when(j == 0)
        def _():
            m_scr[...] = jnp.full(m_scr.shape, -jnp.inf, F32)
            acc_scr[...] = jnp.zeros(acc_scr.shape, F32)

        def step(ok):
            kpe_v = kpe_ref[...]
            ones = jnp.ones((MLA_T, HEAD), BF)
            state = [(m_scr[hh], acc_scr[hh]) for hh in range(MLA_HB)]
            new = []
            for hh in range(MLA_HB):
                m_prev, acc = state[hh]
                _, _, s = _mla_scores(qn_ref, qp_ref, kn_ref, kpe_v, hh, ok)
                m_new = jnp.maximum(m_prev, jnp.max(s, axis=1, keepdims=True))
                p = jnp.exp2((s - m_new) * MLA_EXP2_SCALE).astype(BF)
                v1 = jnp.concatenate([_mla_head(v_ref, hh), ones], axis=1)
                alpha = jnp.exp2((m_prev - m_new) * MLA_EXP2_SCALE)
                new.append((m_new, alpha * acc + lax.dot_general(p, v1, NN, preferred_element_type=F32)))
            for hh in range(MLA_HB):
                m_scr[hh], acc_scr[hh] = new[hh]

        _mla_step(i, j, step)

        @pl.when(j == i)
        def _():
            for hh in range(MLA_HB):
                l = acc_scr[hh, :, HEAD:]
                o_ref[:, hh * HEAD:(hh + 1) * HEAD] = (acc_scr[hh, :, :HEAD] / l).astype(BF)
                lse_ref[hh] = m_scr[hh] * MLA_SCALE + jnp.log(l)

    qspec, kspec, kpespec, lsespec = _mla_specs(S)
    grid_spec = pltpu.PrefetchScalarGridSpec(
        num_scalar_prefetch=2, grid=(MLA_HEADS // MLA_HB, it.shape[0]),
        in_specs=[qspec, qspec, kspec, kpespec, kspec], out_specs=[qspec, lsespec],
        scratch_shapes=[pltpu.VMEM((MLA_HB, MLA_T, 1), F32), pltpu.VMEM((MLA_HB, MLA_T, 2 * HEAD), F32)])
    return pl.pallas_call(
        body, name="mla_fwd", grid_spec=grid_spec,
        out_shape=[jax.ShapeDtypeStruct((S, MLA_HEADS * HEAD), BF), jax.ShapeDtypeStruct((MLA_HEADS, S, LANE), F32)],
        compiler_params=_params("parallel", "arbitrary"))(it, jt, qn, qp, kn, kpe, v)


def _mla_p_ds(qn_ref, qp_ref, kn_ref, kpe, v_ref, do_ref, o_ref, lse_ref, hh, ok):
    q, k, s = _mla_scores(qn_ref, qp_ref, kn_ref, kpe, hh, ok)
    p = jnp.exp2(s * MLA_EXP2_SCALE - lse_ref[hh][:, 0:1] * LOG2E)
    do = _mla_head(do_ref, hh)
    delta = jnp.sum(do.astype(F32) * _mla_head(o_ref, hh).astype(F32), axis=1, keepdims=True)
    dp = lax.dot_general(do, _mla_head(v_ref, hh), NT, preferred_element_type=F32)
    ds = p * (dp - delta) * MLA_SCALE
    return q, k, p, ds, do


def _mla_bwd_dq(qn, qp, kn, kpe, v, do, o, lse):
    S = qn.shape[0]
    it, jt = _mla_pairs(S // MLA_T, True)

    def body(it_ref, jt_ref, qn_ref, qp_ref, kn_ref, kpe_ref, v_ref, do_ref, o_ref, lse_ref, dqn_ref, dqp_ref, acc):
        t = pl.program_id(1)
        i, j = it_ref[t], jt_ref[t]

        @pl.when(j == 0)
        def _():
            acc[...] = jnp.zeros(acc.shape, F32)

        def step(ok):
            kpe_v = kpe_ref[...]
            old = [acc[hh] for hh in range(MLA_HB)]
            for hh in range(MLA_HB):
                _, k, _, ds, _ = _mla_p_ds(qn_ref, qp_ref, kn_ref, kpe_v, v_ref, do_ref, o_ref, lse_ref, hh, ok)
                old[hh] = old[hh] + lax.dot_general(ds.astype(BF), k, NN, preferred_element_type=F32)
            for hh in range(MLA_HB):
                acc[hh] = old[hh]

        _mla_step(i, j, step)

        @pl.when(j == i)
        def _():
            for hh in range(MLA_HB):
                dqn_ref[:, hh * HEAD:(hh + 1) * HEAD] = acc[hh, :, :HEAD].astype(BF)
                dqp_ref[:, hh * HEAD:(hh + 1) * HEAD] = acc[hh, :, HEAD:]

    qspec, kspec, kpespec, lsespec = _mla_specs(S)
    grid_spec = pltpu.PrefetchScalarGridSpec(
        num_scalar_prefetch=2, grid=(MLA_HEADS // MLA_HB, it.shape[0]),
        in_specs=[qspec, qspec, kspec, kpespec, kspec, qspec, qspec, lsespec], out_specs=[qspec, qspec],
        scratch_shapes=[pltpu.VMEM((MLA_HB, MLA_T, 2 * HEAD), F32)])
    return pl.pallas_call(
        body, name="mla_bwd_dq", grid_spec=grid_spec,
        out_shape=[jax.ShapeDtypeStruct((S, MLA_HEADS * HEAD), BF), jax.ShapeDtypeStruct((S, MLA_HEADS * HEAD), F32)],
        compiler_params=_params("parallel", "arbitrary"))(it, jt, qn, qp, kn, kpe, v, do, o, lse)


def _mla_bwd_dkv(qn, qp, kn, kpe, v, do, o, lse):
    S = qn.shape[0]
    nq = S // MLA_T
    it, jt = _mla_pairs(nq, False)

    def body(it_ref, jt_ref, qn_ref, qp_ref, kn_ref, kpe_ref, v_ref, do_ref, o_ref, lse_ref, dkn_ref, dv_ref, dkpe_ref,
             dk_acc, dv_acc):
        t = pl.program_id(1)
        i, j = it_ref[t], jt_ref[t]

        @pl.when(i == j)
        def _():
            dk_acc[...] = jnp.zeros(dk_acc.shape, F32)
            dv_acc[...] = jnp.zeros(dv_acc.shape, F32)

        def step(ok):
            kpe_v = kpe_ref[...]
            for hh in range(MLA_HB):
                q, _, p, ds, do = _mla_p_ds(qn_ref, qp_ref, kn_ref, kpe_v, v_ref, do_ref, o_ref, lse_ref, hh, ok)
                dv_acc[hh] += lax.dot_general(p.astype(BF), do, TN, preferred_element_type=F32)
                dk_acc[hh] += lax.dot_general(ds.astype(BF), q, TN, preferred_element_type=F32)

        _mla_step(i, j, step)

        @pl.when(i == nq - 1)
        def _():
            dkpe = dk_acc[0, :, HEAD:]
            for hh in range(MLA_HB):
                dkn_ref[:, hh * HEAD:(hh + 1) * HEAD] = dk_acc[hh, :, :HEAD].astype(BF)
                dv_ref[:, hh * HEAD:(hh + 1) * HEAD] = dv_acc[hh].astype(BF)
                if hh:
                    dkpe = dkpe + dk_acc[hh, :, HEAD:]
            dkpe_ref[...] = dkpe

    qspec, kspec, kpespec, lsespec = _mla_specs(S)
    dkpespec = pl.BlockSpec((None, MLA_T, HEAD), lambda g, t, it, jt: (g, jt[t], 0))
    grid_spec = pltpu.PrefetchScalarGridSpec(
        num_scalar_prefetch=2, grid=(MLA_HEADS // MLA_HB, it.shape[0]),
        in_specs=[qspec, qspec, kspec, kpespec, kspec, qspec, qspec, lsespec], out_specs=[kspec, kspec, dkpespec],
        scratch_shapes=[pltpu.VMEM((MLA_HB, MLA_T, 2 * HEAD), F32), pltpu.VMEM((MLA_HB, MLA_T, HEAD), F32)])
    return pl.pallas_call(
        body, name="mla_bwd_dkv", grid_spec=grid_spec,
        out_shape=[jax.ShapeDtypeStruct((S, MLA_HEADS * HEAD), BF), jax.ShapeDtypeStruct((S, MLA_HEADS * HEAD), BF),
                   jax.ShapeDtypeStruct((MLA_HEADS // MLA_HB, S, HEAD), F32)],
        compiler_params=_params("parallel", "arbitrary"))(it, jt, qn, qp, kn, kpe, v, do, o, lse)


DIL_W = 3 * DIL_HPG * HEAD
DIL_O = DIL_HPG * HEAD
DIL_STEP_BLOCKS = 4


def _dil_slopes(g):
    return [2.0 ** (-ALIBI_MAX_BIAS * (g * DIL_HPG + hh + 1) / DIL_HEADS) for hh in range(DIL_HPG)]


def _dil_bias(dil):
    p = lax.broadcasted_iota(jnp.int32, (DIL_BLOCK, DIL_BLOCK), 0)
    kk = lax.broadcasted_iota(jnp.int32, (DIL_BLOCK, DIL_BLOCK), 1)
    jc = p - kk
    dist_c = (dil * jc).astype(F32)
    dist_p = (dil * (jc + DIL_BLOCK)).astype(F32)
    return jc >= 0, jc <= 0, dist_c, dist_p


def _dil_head(blk, hh):
    q = blk[:, hh * HEAD:(hh + 1) * HEAD]
    k = blk[:, DIL_O + hh * HEAD:DIL_O + (hh + 1) * HEAD]
    v = blk[:, 2 * DIL_O + hh * HEAD:2 * DIL_O + (hh + 1) * HEAD]
    return q, k, v


def _dil_s(q, k, slope, dist, ok):
    s = lax.dot_general(q, k, NT, preferred_element_type=F32) * DIL_SCALE - slope * dist
    return jnp.where(ok, s, -jnp.inf)


def _dil_view(a, dil):
    S, W = a.shape
    return a.reshape(S // dil, dil * W)


def _dil_fwd(qkv, g):
    _, dil = DIL_PATTERNS[g]
    S = qkv.shape[0]
    L = S // dil
    nb = L // DIL_BLOCK
    slopes = _dil_slopes(g)

    bb = min(DIL_STEP_BLOCKS, nb)
    rows = bb * DIL_BLOCK

    def body(cur_ref, prev_ref, o_ref, lse_ref):
        n = pl.program_id(1)
        ok_c, ok_p, dist_c, dist_p = _dil_bias(dil)
        for b in range(bb):
            cur = cur_ref[b * DIL_BLOCK:(b + 1) * DIL_BLOCK, :]
            prev = prev_ref[...] if b == 0 else cur_ref[(b - 1) * DIL_BLOCK:b * DIL_BLOCK, :]
            ok_b = ok_p & (n > 0) if b == 0 else ok_p
            for hh in range(DIL_HPG):
                q, kc, vc = _dil_head(cur, hh)
                _, kp, vp = _dil_head(prev, hh)
                sc = _dil_s(q, kc, slopes[hh], dist_c, ok_c)
                sp = _dil_s(q, kp, slopes[hh], dist_p, ok_b)
                m = jnp.maximum(jnp.max(sc, axis=1, keepdims=True), jnp.max(sp, axis=1, keepdims=True))
                pc, pp = jnp.exp(sc - m), jnp.exp(sp - m)
                l = jnp.sum(pc, axis=1, keepdims=True) + jnp.sum(pp, axis=1, keepdims=True)
                o = (lax.dot_general(pc.astype(BF), vc, NN, preferred_element_type=F32)
                     + lax.dot_general(pp.astype(BF), vp, NN, preferred_element_type=F32)) / l
                rs, sl = slice(b * DIL_BLOCK, (b + 1) * DIL_BLOCK), slice(hh * HEAD, (hh + 1) * HEAD)
                o_ref[rs, sl] = o
                lse_ref[rs, sl] = jnp.broadcast_to(m + jnp.log(l), (DIL_BLOCK, HEAD))

    ospec = pl.BlockSpec((rows, DIL_O), lambda r, n: (n, r))
    o, lse = pl.pallas_call(
        body, name=f"dil_fwd{g}", grid=(dil, nb // bb),
        in_specs=[pl.BlockSpec((rows, DIL_W), lambda r, n: (n, r)),
                  pl.BlockSpec((DIL_BLOCK, DIL_W), lambda r, n: (jnp.maximum(n * bb - 1, 0), r))],
        out_specs=[ospec, ospec],
        out_shape=[jax.ShapeDtypeStruct((L, dil * DIL_O), F32), jax.ShapeDtypeStruct((L, dil * DIL_O), F32)],
        compiler_params=_params("parallel", "parallel"))(_dil_view(qkv, dil), _dil_view(qkv, dil))
    return o.reshape(S, DIL_O), lse.reshape(S, DIL_O)


def _dil_combine(os_, lses):
    S = os_[0].shape[0]
    ts = _tile(S, 512)

    def body(o0, o1, o2, l0, l1, l2, out_ref, lse_ref):
        a, b, c = l0[...], l1[...], l2[...]
        m = jnp.maximum(jnp.maximum(a, b), c)
        ea, eb, ec = jnp.exp(a - m), jnp.exp(b - m), jnp.exp(c - m)
        tot = ea + eb + ec
        out_ref[...] = ((ea * o0[...] + eb * o1[...] + ec * o2[...]) / tot).astype(BF)
        lse_ref[...] = m + jnp.log(tot)

    return pl.pallas_call(
        body, name="dil_combine", grid=(S // ts,), in_specs=[_row(ts, DIL_O)] * 6,
        out_specs=[_row(ts, DIL_O), _row(ts, DIL_O)],
        out_shape=[jax.ShapeDtypeStruct((S, DIL_O), BF), jax.ShapeDtypeStruct((S, DIL_O), F32)],
        compiler_params=_params("parallel"))(*os_, *lses)


def _dil_delta(do, out):
    S = do.shape[0]
    ts = _tile(S, 512)

    def body(do_ref, o_ref, d_ref):
        for hh in range(DIL_HPG):
            sl = slice(hh * HEAD, (hh + 1) * HEAD)
            d = jnp.sum(do_ref[:, sl].astype(F32) * o_ref[:, sl].astype(F32), axis=1, keepdims=True)
            d_ref[:, sl] = jnp.broadcast_to(d, (ts, HEAD))

    return pl.pallas_call(
        body, name="dil_delta", grid=(S // ts,), in_specs=[_row(ts, DIL_O)] * 2, out_specs=_row(ts, DIL_O),
        out_shape=jax.ShapeDtypeStruct((S, DIL_O), F32), compiler_params=_params("parallel"))(do, out)


def _dil_bwd(qkv, do, lse, delta, g):
    _, dil = DIL_PATTERNS[g]
    S = qkv.shape[0]
    L = S // dil
    nb = L // DIL_BLOCK
    slopes = _dil_slopes(g)

    def pair(q, k, v, do_h, lse_h, delta_h, slope, dist, ok):
        s = _dil_s(q, k, slope, dist, ok)
        p = jnp.exp(s - lse_h)
        dp = lax.dot_general(do_h, v, NT, preferred_element_type=F32)
        ds = (p * (dp - delta_h) * DIL_SCALE).astype(BF)
        return p.astype(BF), ds

    bb = min(DIL_STEP_BLOCKS, nb)
    rows = bb * DIL_BLOCK
    steps = nb // bb

    def body(cur_ref, prev_ref, next_ref, doc_ref, don_ref, lsec_ref, lsen_ref, dlc_ref, dln_ref, out_ref):
        n = pl.program_id(1)
        ok_c, ok_p0, dist_c, dist_p = _dil_bias(dil)
        for b in range(bb):
            rs = slice(b * DIL_BLOCK, (b + 1) * DIL_BLOCK)
            rp = slice((b - 1) * DIL_BLOCK, b * DIL_BLOCK)
            rn = slice((b + 1) * DIL_BLOCK, (b + 2) * DIL_BLOCK)
            first, last = b == 0, b == bb - 1
            cur = cur_ref[rs, :]
            prev = prev_ref[...] if first else cur_ref[rp, :]
            nxt = next_ref[...] if last else cur_ref[rn, :]
            ok_a = ok_p0 & (n > 0) if first else ok_p0
            ok_n = ok_p0 & (n < steps - 1) if last else ok_p0
            for hh in range(DIL_HPG):
                sl = slice(hh * HEAD, (hh + 1) * HEAD)
                q, kc, vc = _dil_head(cur, hh)
                _, kp, vp = _dil_head(prev, hh)
                qn, _, _ = _dil_head(nxt, hh)
                do_c = doc_ref[rs, sl]
                do_n = don_ref[:, sl] if last else doc_ref[rn, sl]
                lse_c = lsec_ref[rs, sl][:, 0:1]
                lse_n = (lsen_ref[:, sl] if last else lsec_ref[rn, sl])[:, 0:1]
                dl_c = dlc_ref[rs, sl][:, 0:1]
                dl_n = (dln_ref[:, sl] if last else dlc_ref[rn, sl])[:, 0:1]
                _, ds_a = pair(q, kp, vp, do_c, lse_c, dl_c, slopes[hh], dist_p, ok_a)
                p_b, ds_b = pair(q, kc, vc, do_c, lse_c, dl_c, slopes[hh], dist_c, ok_c)
                p_n, ds_n = pair(qn, kc, vc, do_n, lse_n, dl_n, slopes[hh], dist_p, ok_n)
                dq = (lax.dot_general(ds_a, kp, NN, preferred_element_type=F32)
                      + lax.dot_general(ds_b, kc, NN, preferred_element_type=F32))
                dk = (lax.dot_general(ds_b, q, TN, preferred_element_type=F32)
                      + lax.dot_general(ds_n, qn, TN, preferred_element_type=F32))
                dv = (lax.dot_general(p_b, do_c, TN, preferred_element_type=F32)
                      + lax.dot_general(p_n, do_n, TN, preferred_element_type=F32))
                out_ref[rs, sl] = dq.astype(BF)
                out_ref[rs, DIL_O + hh * HEAD:DIL_O + (hh + 1) * HEAD] = dk.astype(BF)
                out_ref[rs, 2 * DIL_O + hh * HEAD:2 * DIL_O + (hh + 1) * HEAD] = dv.astype(BF)

    cur_w = pl.BlockSpec((rows, DIL_W), lambda r, n: (n, r))
    prev_w = pl.BlockSpec((DIL_BLOCK, DIL_W), lambda r, n: (jnp.maximum(n * bb - 1, 0), r))
    next_w = pl.BlockSpec((DIL_BLOCK, DIL_W), lambda r, n: (jnp.minimum((n + 1) * bb, nb - 1), r))
    cur_o = pl.BlockSpec((rows, DIL_O), lambda r, n: (n, r))
    next_o = pl.BlockSpec((DIL_BLOCK, DIL_O), lambda r, n: (jnp.minimum((n + 1) * bb, nb - 1), r))
    qv, dov, lsev, dlv = _dil_view(qkv, dil), _dil_view(do, dil), _dil_view(lse, dil), _dil_view(delta, dil)
    out = pl.pallas_call(
        body, name=f"dil_bwd{g}", grid=(dil, steps),
        in_specs=[cur_w, prev_w, next_w, cur_o, next_o, cur_o, next_o, cur_o, next_o],
        out_specs=cur_w, out_shape=jax.ShapeDtypeStruct((L, dil * DIL_W), BF),
        compiler_params=_params("parallel", "parallel"))(qv, qv, qv, dov, dov, lsev, lsev, dlv, dlv)
    return out.reshape(S, DIL_W)


def _adamw(name, w, g, m, v):
    R, C = w.shape
    tr, tc = _adamw_block(R, C)

    def body(w_ref, g_ref, m_ref, v_ref, go_ref, d_ref, nm_ref, nv_ref):
        gv = g_ref[...]
        go_ref[...] = gv
        nm = ADAM_B1 * m_ref[...] + (1.0 - ADAM_B1) * gv
        nv = ADAM_B2 * v_ref[...] + (1.0 - ADAM_B2) * (gv * gv)
        m_hat = nm / (1.0 - ADAM_B1 ** ADAM_STEP)
        v_hat = nv / (1.0 - ADAM_B2 ** ADAM_STEP)
        d_ref[...] = -ADAM_LR * (m_hat / (jnp.sqrt(v_hat) + ADAM_EPS) + ADAM_WD * w_ref[...])
        nm_ref[...] = nm
        nv_ref[...] = nv

    spec = pl.BlockSpec((tr, tc), lambda i, j: (i, j))
    shp = jax.ShapeDtypeStruct((R, C), F32)
    return pl.pallas_call(
        body, name=name, grid=(R // tr, C // tc), in_specs=[spec] * 4, out_specs=[spec] * 4, out_shape=[shp] * 4,
        compiler_params=_params("parallel", "parallel"))(w, g, m, v)


ADAMW_BLOCK_ELEMS = 640 * 1024


def _adamw_block(R, C):
    if R * C <= ADAMW_BLOCK_ELEMS:
        return R, C
    tr = _tile_rows(R, max(8, ADAMW_BLOCK_ELEMS // C))
    if tr * C * 8 >= ADAMW_BLOCK_ELEMS:
        return tr, C
    return R, _tile(C, max(LANE, ADAMW_BLOCK_ELEMS // R))


def _tile_rows(n, pref, mult=8):
    t = (pref // mult) * mult
    while t >= mult:
        if n % t == 0:
            return t
        t -= mult
    return n


ANY = pl.BlockSpec(memory_space=pl.ANY)


def _place():
    x, y, c = lax.axis_index("x"), lax.axis_index("y"), lax.axis_index("c")
    chips = [(1 - x, y), (x, 1 - y), (1 - x, 1 - y)]
    chip_idx = [2 * cx + cy for cx, cy in chips]
    return x, y, c, 2 * x + y, chips, chip_idx


def _rcopy(src, dst, ssem, rsem, dev):
    return pltpu.make_async_remote_copy(src_ref=src, dst_ref=dst, send_sem=ssem, recv_sem=rsem,
                                        device_id=dev, device_id_type=MESH)


HBM = pl.BlockSpec(memory_space=pltpu.HBM)
SEM = pl.BlockSpec(memory_space=pltpu.SEMAPHORE)
EFFECT = pltpu.SideEffectType.DATAFLOW_SIDE_EFFECTING


def _split_copies(kind, srcs, lands, ssem, rsem):
    _, _, c, me, chips, chip_idx = _place()
    cps = []
    for i in range(len(srcs)):
        for k in range(3):
            if kind == "gather":
                rows = srcs[i].shape[0]
                if rows == lands[i].shape[1]:
                    src, dst = srcs[i], lands[i].at[me]
                else:
                    src, dst = srcs[i], lands[i].at[me, pl.ds(pl.multiple_of(c * rows, 16), rows)]
            else:
                src, dst = srcs[i].at[chip_idx[k]], lands[i].at[k]
            cps.append(_rcopy(src, dst, ssem.at[3 * i + k], rsem.at[3 * i + k], (*chips[k], c)))
    return cps


def _exchange_start(name, kind, srcs, lands, groups):
    n, ng = len(srcs), len(groups)

    def body(*refs):
        src_refs, land_refs = refs[:n], refs[n:2 * n]
        sems = refs[2 * n:2 * n + 2 * ng]
        token = refs[-1]
        for gi, grp in enumerate(groups):
            cps = _split_copies(kind, [src_refs[i] for i in grp], [land_refs[i] for i in grp], sems[2 * gi], sems[2 * gi + 1])
            for cp in cps:
                cp.start()
        token[...] = jnp.zeros_like(token)

    arrays = list(srcs) + list(lands)
    out_shape = []
    for grp in groups:
        out_shape += [pltpu.SemaphoreType.DMA((3 * len(grp),)), pltpu.SemaphoreType.DMA((3 * len(grp),))]
    out_shape += [pltpu.HBM(a.shape, a.dtype) for a in arrays] + [jax.ShapeDtypeStruct((8, LANE), F32)]
    outs = pl.pallas_call(
        body, name=name, out_shape=out_shape, in_specs=[HBM] * (2 * n),
        out_specs=[SEM] * (2 * ng) + [HBM] * (2 * n) + [pl.BlockSpec(memory_space=pltpu.VMEM)],
        input_output_aliases={i: 2 * ng + i for i in range(2 * n)},
        compiler_params=pltpu.CompilerParams(has_side_effects=EFFECT),
    )(*[pltpu.with_memory_space_constraint(a, pltpu.HBM) for a in arrays])
    sems = [(outs[2 * gi], outs[2 * gi + 1]) for gi in range(ng)]
    thru = outs[2 * ng:2 * ng + 2 * n]
    return sems, thru[:n], thru[n:], outs[-1]


def _exchange_wait(name, kind, srcs, lands, sems, after):
    n = len(srcs)

    def body(*refs):
        cps = _split_copies(kind, refs[:n], refs[n:2 * n], refs[2 * n], refs[2 * n + 1])
        for cp in cps:
            cp.wait_send()
            cp.wait_recv()

    arrays = list(srcs) + list(lands)
    outs = pl.pallas_call(
        body, name=name, out_shape=[pltpu.HBM(a.shape, a.dtype) for a in arrays],
        in_specs=[HBM] * (2 * n) + [SEM, SEM, ANY], out_specs=[HBM] * (2 * n),
        input_output_aliases={i: i for i in range(2 * n)},
        compiler_params=pltpu.CompilerParams(has_side_effects=EFFECT),
    )(*arrays, sems[0], sems[1], after)
    return outs[:n], outs[n:]


EXCHANGE_CHUNK_BYTES = 3 * 512 * 1024


def _half_geometry(R, C, axis):
    Rp, Cp = (R // 2, C) if axis == 0 else (R, C // 2)
    tr = _tile_rows(Rp, max(16, EXCHANGE_CHUNK_BYTES // (2 * Cp)), 16)
    return Rp, Cp, tr, Rp // tr


def _pair_sum(name, g, axis):
    G, R, C = g.shape
    Rp, Cp, tr, nb = _half_geometry(R, C, axis)
    steps = G * nb

    def half_block(s, b, h):
        return (s, h * nb + b, 0) if axis == 0 else (s, b, h)

    def body(c_ref, keep_ref, give_ref, out_ref, land, ssem, rsem, credit):
        x, y, c = lax.axis_index("x"), lax.axis_index("y"), lax.axis_index("c")
        sib = (x, y, 1 - c)
        t = pl.program_id(0) * nb + pl.program_id(1)
        slot = t % 2

        @pl.when(t >= 2)
        def _():
            pl.semaphore_wait(credit, 1)

        cp = _rcopy(give_ref.at[0], land.at[slot], ssem.at[slot], rsem.at[slot], sib)
        cp.start()
        cp.wait_recv()
        out_ref[...] = (keep_ref[...].astype(F32) + land[slot].astype(F32)).astype(BF)

        @pl.when(t + 2 < steps)
        def _():
            pl.semaphore_signal(credit, 1, device_id=sib, device_id_type=MESH)

        cp.wait_send()

    blk = (None, tr, Cp)
    grid_spec = pltpu.PrefetchScalarGridSpec(
        num_scalar_prefetch=1, grid=(G, nb),
        in_specs=[pl.BlockSpec(blk, lambda s, b, c_ref: half_block(s, b, c_ref[0])),
                  pl.BlockSpec((1, tr, Cp), lambda s, b, c_ref: half_block(s, b, 1 - c_ref[0]))],
        out_specs=pl.BlockSpec(blk, lambda s, b, c_ref: (s, b, 0)),
        scratch_shapes=[pltpu.VMEM((2, tr, Cp), BF), pltpu.SemaphoreType.DMA((2,)), pltpu.SemaphoreType.DMA((2,)),
                        pltpu.SemaphoreType.REGULAR])
    c_arr = lax.axis_index("c").astype(jnp.int32).reshape(1)
    return pl.pallas_call(
        body, name=name, grid_spec=grid_spec, out_shape=jax.ShapeDtypeStruct((G, Rp, Cp), BF),
        compiler_params=_params("arbitrary", "arbitrary"))(c_arr, g, g)


def _chip_total(name, h, landed):
    G, Rp, Cp = h.shape
    tr = _tile_rows(Rp, max(16, EXCHANGE_CHUNK_BYTES // (2 * Cp)), 16)

    def body(me_ref, own_ref, l0_ref, l1_ref, l2_ref, out_ref):
        acc = own_ref[...].astype(F32)
        for r in (l0_ref, l1_ref, l2_ref):
            acc = acc + r[...].astype(F32)
        out_ref[...] = acc

    blk = (None, tr, Cp)
    grid_spec = pltpu.PrefetchScalarGridSpec(
        num_scalar_prefetch=1, grid=(Rp // tr,),
        in_specs=[pl.BlockSpec(blk, lambda b, me_ref: (me_ref[0], b, 0))]
        + [pl.BlockSpec(blk, functools.partial(lambda b, me_ref, k: (k, b, 0), k=k)) for k in range(3)],
        out_specs=pl.BlockSpec((tr, Cp), lambda b, me_ref: (b, 0)))
    me = (2 * lax.axis_index("x") + lax.axis_index("y")).astype(jnp.int32).reshape(1)
    return pl.pallas_call(
        body, name=name, grid_spec=grid_spec, out_shape=jax.ShapeDtypeStruct((Rp, Cp), F32),
        compiler_params=_params("parallel"))(me, h, landed, landed, landed)


def _pair_join(name, f, axis):
    Rp, Cp = f.shape
    R, C = (2 * Rp, Cp) if axis == 0 else (Rp, 2 * Cp)
    tr = _tile_rows(Rp, max(8, EXCHANGE_CHUNK_BYTES // (4 * Cp)), 8)
    nb = Rp // tr

    def body(f_ref, full, ssem, rsem, lsem):
        x, y, c = lax.axis_index("x"), lax.axis_index("y"), lax.axis_index("c")
        sib = (x, y, 1 - c)
        b = pl.program_id(0)

        def place(h, r0, rows):
            if axis == 0:
                return full.at[pl.ds(pl.multiple_of(h * Rp + r0, 8), rows), :]
            return full.at[pl.ds(pl.multiple_of(r0, 8), rows), pl.ds(pl.multiple_of(h * Cp, LANE), Cp)]

        mine = place(c, b * tr, tr)
        loc = pltpu.make_async_copy(f_ref, mine, lsem)
        rem = _rcopy(f_ref, mine, ssem, rsem, sib)
        loc.start()
        rem.start()
        loc.wait()
        rem.wait_send()

        @pl.when(b == nb - 1)
        def _():
            theirs = place(1 - c, 0, Rp)
            _rcopy(theirs, theirs, ssem, rsem, sib).wait_recv()

    return pl.pallas_call(
        body, name=name, grid=(nb,), in_specs=[pl.BlockSpec((tr, Cp), lambda b: (b, 0))], out_specs=ANY,
        out_shape=jax.ShapeDtypeStruct((R, C), F32),
        scratch_shapes=[pltpu.SemaphoreType.DMA, pltpu.SemaphoreType.DMA, pltpu.SemaphoreType.DMA],
        compiler_params=_params("arbitrary"))(f)


def _pair_share(name, land):
    G, R, C = land.shape
    Rh = R // 2
    tr = _tile_rows(Rh, max(16, EXCHANGE_CHUNK_BYTES // (2 * C)), 16)
    chunks = [(k, b) for k in range(3) for b in range(Rh // tr)]

    def body(src, dst, buf, lsem, ssem, rsem):
        x, y, c, _, _, chip_idx = _place()
        sib = (x, y, 1 - c)

        def region(ref, k, half, r0, rows):
            return ref.at[chip_idx[k], pl.ds(pl.multiple_of(half * Rh + r0, 16), rows)]

        def load(t):
            k, b = chunks[t]
            return pltpu.make_async_copy(region(src, k, c, b * tr, tr), buf.at[t % 2], lsem.at[t % 2])

        def send(t):
            k, b = chunks[t]
            return _rcopy(buf.at[t % 2], region(dst, k, c, b * tr, tr), ssem.at[t % 2], rsem.at[k], sib)

        load(0).start()
        for t in range(len(chunks)):
            load(t).wait()
            if t + 1 < len(chunks):
                if t >= 1:
                    send(t - 1).wait_send()
                load(t + 1).start()
            send(t).start()
        for t in range(max(0, len(chunks) - 2), len(chunks)):
            send(t).wait_send()
        for k in range(3):
            theirs = region(dst, k, 1 - c, 0, Rh)
            _rcopy(theirs, theirs, ssem.at[0], rsem.at[k], sib).wait_recv()

    return pl.pallas_call(
        body, name=name, in_specs=[ANY], out_specs=ANY, out_shape=jax.ShapeDtypeStruct(land.shape, land.dtype),
        input_output_aliases={0: 0},
        scratch_shapes=[pltpu.VMEM((2, tr, C), land.dtype), pltpu.SemaphoreType.DMA((2,)), pltpu.SemaphoreType.DMA((2,)),
                        pltpu.SemaphoreType.DMA((3,))],
    )(land)


def _allreduce_small(v):
    R, K = v.shape
    ndev = 8

    def body(v_ref, o_ref, land, ssem, rsem):
        x, y, c = lax.axis_index("x"), lax.axis_index("y"), lax.axis_index("c")
        me = 4 * x + 2 * y + c
        land[me] = v_ref[...]
        cps = []
        for r in range(1, ndev):
            fx, fy, fc = (r >> 2) & 1, (r >> 1) & 1, r & 1
            peer = (x ^ fx, y ^ fy, c ^ fc)
            cp = _rcopy(v_ref, land.at[me], ssem.at[r - 1], rsem.at[r - 1], peer)
            cp.start()
            cps.append((cp, 4 * peer[0] + 2 * peer[1] + peer[2], r))
        for cp, src, r in cps:
            cp.wait_send()
            _rcopy(v_ref, land.at[src], ssem.at[r - 1], rsem.at[r - 1], (x, y, c)).wait_recv()
        acc = land[0]
        for d in range(1, ndev):
            acc = acc + land[d]
        o_ref[...] = acc

    vm = pl.BlockSpec(memory_space=pltpu.VMEM)
    return pl.pallas_call(
        body, name="allreduce_small", in_specs=[vm], out_specs=vm, out_shape=jax.ShapeDtypeStruct((R, K), F32),
        scratch_shapes=[pltpu.VMEM((ndev, R, K), F32), pltpu.SemaphoreType.DMA((ndev - 1,)), pltpu.SemaphoreType.DMA((ndev - 1,))],
    )(v)


IN_SPLITS = (Q_RANK, KV_RANK, QK_ROPE, DIL_HEADS * HEAD, DIL_HEADS * HEAD, DIL_HEADS * HEAD, D_MODEL, D_MODEL)
IN_OFF = tuple(int(v) for v in np.cumsum((0,) + IN_SPLITS))


def _unshard_cols(g):
    G, K, Ns = g.shape
    return g.transpose(1, 0, 2).reshape(K, G * Ns)


def _shard_cols(w):
    K, N = w.shape
    return w.reshape(K, N_CHIPS, N // N_CHIPS).transpose(1, 0, 2)


def _rope_pad(w):
    half = QK_ROPE // 2
    z = jnp.zeros(w.shape[:-1] + (half,), w.dtype)
    return jnp.concatenate([w[..., :half], z, w[..., half:], z], axis=-1)


def _rope_unpad(w):
    half = QK_ROPE // 2
    return jnp.concatenate([w[..., :half], w[..., 2 * half:3 * half]], axis=-1)


def _split_w_in(w_in_g):
    w = _unshard_cols(w_in_g)
    K = w.shape[0]
    p = [w[:, IN_OFF[i]:IN_OFF[i + 1]] for i in range(8)]
    w_lat = jnp.concatenate([p[0], p[1], _rope_pad(p[2]), jnp.zeros((K, LAT_W - _KPE.stop), w.dtype)], axis=1)
    w_dil = [jnp.concatenate([p[3 + t][:, g * DIL_O:(g + 1) * DIL_O] for t in range(3)], axis=1) for g in range(DIL_GROUPS)]
    w_gate = jnp.concatenate([p[6], p[7]], axis=1)
    return w_lat, w_dil, w_gate


def _merge_dw_in(dw_lat, dw_dil, dw_gate):
    parts = [dw_lat[:, _CQ], dw_lat[:, _CKV], _rope_unpad(dw_lat[:, _KPE])]
    for t in range(3):
        parts += [dw_dil[g][:, t * DIL_O:(t + 1) * DIL_O] for g in range(DIL_GROUPS)]
    parts.append(dw_gate)
    return _shard_cols(jnp.concatenate(parts, axis=1))


def _split_w_uq(w_uq_g):
    w = _unshard_cols(w_uq_g)
    K = w.shape[0]
    w = w.reshape(K, MLA_HEADS, QK_NOPE + QK_ROPE)
    return w[:, :, :QK_NOPE].reshape(K, MLA_HEADS * HEAD), _rope_pad(w[:, :, QK_NOPE:]).reshape(K, MLA_HEADS * HEAD)


def _merge_dw_uq(dw_n, dw_p):
    K = dw_n.shape[0]
    w = jnp.concatenate([dw_n.reshape(K, MLA_HEADS, HEAD), _rope_unpad(dw_p.reshape(K, MLA_HEADS, HEAD))], axis=-1)
    return _shard_cols(w.reshape(K, MLA_HEADS * (QK_NOPE + QK_ROPE)))


def _split_w_ukv(w_ukv_g):
    w = _unshard_cols(w_ukv_g)
    K = w.shape[0]
    w = w.reshape(K, MLA_HEADS, 2 * HEAD)
    return w[:, :, :HEAD].reshape(K, MLA_HEADS * HEAD), w[:, :, HEAD:].reshape(K, MLA_HEADS * HEAD)


def _merge_dw_ukv(dw_k, dw_v):
    K = dw_k.shape[0]
    w = jnp.concatenate([dw_k.reshape(K, MLA_HEADS, HEAD), dw_v.reshape(K, MLA_HEADS, HEAD)], axis=-1)
    return _shard_cols(w.reshape(K, MLA_HEADS * 2 * HEAD))


GATHER_GROUPS = (("w_in",), ("w_uq", "w_ukv", "w_o_mla", "w_o_dil", "w_out"), ("w_up", "w_down", "conv_w"))
SHARED_FETCH = ("w_in",)
REDUCE_GROUPS = (("w_down", "w_up"), ("w_out", "w_o_mla", "w_o_dil"), ("w_uq", "w_ukv", "w_in"))


def _local_step(x, tgt, W, fetch, emit):
    S, D = x.shape
    cos, sin_s = _rope_tables(S)
    w_lat, w_dil, w_gate = _split_w_in(fetch(0, x)["w_in"])

    h = _rmsnorm_fwd("attn_norm", x, W["attn_norm_g"])
    lat = _mm_nn("proj_lat", h, w_lat)
    qkv = [_mm_nn(f"proj_dil{g}", h, w_dil[g], o_dtype=BF) for g in range(DIL_GROUPS)]
    gpre = _mm_nn("proj_gate", h, w_gate)
    WB = fetch(1, gpre)
    w_uqn, w_uqp = _split_w_uq(WB["w_uq"])
    w_k, w_v = _split_w_ukv(WB["w_ukv"])
    w_o_mla, w_o_dil = WB["w_o_mla"], WB["w_o_dil"]
    w_out = WB["w_out"].reshape(D, D)
    qn_, kvn, kpe = _mla_prep(lat, W["q_norm_g"], W["kv_norm_g"], cos, sin_s)
    q_nope = _mm_nn("q_nope", qn_, w_uqn, o_dtype=BF)
    q_pe = _rope("q_rope", _mm_nn("q_pe", qn_, w_uqp), cos, sin_s, False)
    k_nope = _mm_nn("k_nope", kvn, w_k, o_dtype=BF)
    v_mla = _mm_nn("v_mla", kvn, w_v, o_dtype=BF)
    attn_a, lse_a = _mla_fwd(q_nope, q_pe, k_nope, kpe, v_mla)
    dil = [_dil_fwd(qkv[g], g) for g in range(DIL_GROUPS)]
    attn_b, lse_b = _dil_combine([o for o, _ in dil], [l for _, l in dil])
    o_a = _mm_nn("o_mla", attn_a, w_o_mla)
    o_b = _mm_nn("o_dil", attn_b, w_o_dil)
    merge = _merge_fwd(gpre, W["b_gate"], o_a, o_b)
    x1 = _mm_nn("out_proj", merge, w_out, add=x)
    WC = fetch(2, merge)
    w_up = WC["w_up"]
    G4, _, C = w_up.shape
    w_down = WC["w_down"].reshape(G4 // 2, C, D)
    conv_w = WC["conv_w"]
    conv_b = W["conv_b"].reshape(G4, 1, C)
    h2 = _rmsnorm_fwd("ffn_norm", x1, W["ffn_norm_g"])
    u_pre = _up_fwd(h2, w_up)
    act = _ffn_act(u_pre, conv_w, conv_b)
    x2 = _down_fwd(act, w_down, x1)
    dx2, d_final_g, loss8 = _final_loss(x2, tgt, W["final_norm_g"])

    d_act = _down_dgrad(dx2, w_down)
    dw_down = _down_wgrad(act, dx2)
    du = _ffn_act_bwd(u_pre, conv_w, conv_b, d_act)
    du_pre, d_conv_w, d_conv_b = _conv_bwd(du, u_pre, conv_w)
    dh2 = _up_dgrad(du_pre, w_up)
    dw_up = _up_wgrad(h2, du_pre)
    zero = emit(0, {"w_down": dw_down.reshape(N_CHIPS, (G4 // 2) * C // N_CHIPS, D), "w_up": dw_up})
    dx1, d_ffn_g = _rmsnorm_bwd("ffn_norm_bwd", dh2, x1, W["ffn_norm_g"] + zero, dx2)
    d_merge = _mm_nt("out_proj_dgrad", dx1, w_out)
    dw_out = _mm_tn("out_proj_wgrad", merge, dx1)
    d_oa, d_ob, d_gpre, d_b_gate = _merge_bwd(d_merge, gpre, W["b_gate"], o_a, o_b)
    d_attn_a = _mm_nt("o_mla_dgrad", d_oa, w_o_mla, o_dtype=BF)
    dw_o_mla = _mm_tn("o_mla_wgrad", attn_a, d_oa, shards=N_CHIPS)
    d_attn_b = _mm_nt("o_dil_dgrad", d_ob, w_o_dil, o_dtype=BF)
    dw_o_dil = _mm_tn("o_dil_wgrad", attn_b, d_ob, shards=N_CHIPS)
    zero = emit(1, {"w_out": dw_out.reshape(N_CHIPS, D // N_CHIPS, D), "w_o_mla": dw_o_mla, "w_o_dil": dw_o_dil})
    q_norm_g = W["q_norm_g"] + zero
    delta_b = _dil_delta(d_attn_b, attn_b)
    d_qkv = [_dil_bwd(qkv[g], d_attn_b, lse_b, delta_b, g) for g in range(DIL_GROUPS)]
    dq_nope, dq_pe_rot = _mla_bwd_dq(q_nope, q_pe, k_nope, kpe, v_mla, d_attn_a, attn_a, lse_a)
    dk_nope, dv_mla, dkpe_rot = _mla_bwd_dkv(q_nope, q_pe, k_nope, kpe, v_mla, d_attn_a, attn_a, lse_a)
    dq_pe = _rope("q_rope_bwd", dq_pe_rot, cos, sin_s, True)
    d_qn = _mm_nt("q_pe_dgrad", dq_pe, w_uqp, add=_mm_nt("q_nope_dgrad", dq_nope, w_uqn))
    d_kvn = _mm_nt("v_dgrad", dv_mla, w_v, add=_mm_nt("k_nope_dgrad", dk_nope, w_k))
    dw_uq = _merge_dw_uq(_mm_tn("q_nope_wgrad", qn_, dq_nope), _mm_tn("q_pe_wgrad", qn_, dq_pe))
    dw_ukv = _merge_dw_ukv(_mm_tn("k_nope_wgrad", kvn, dk_nope), _mm_tn("v_wgrad", kvn, dv_mla))
    d_lat, d_q_g, d_kv_g = _mla_prep_bwd(lat, q_norm_g, W["kv_norm_g"], cos, sin_s, d_qn, d_kvn, dkpe_rot)
    dw_in = _merge_dw_in(_mm_tn("proj_lat_wgrad", h, d_lat),
                         [_mm_tn(f"proj_dil{g}_wgrad", h, d_qkv[g]) for g in range(DIL_GROUPS)],
                         _mm_tn("proj_gate_wgrad", h, d_gpre))
    zero = emit(2, {"w_uq": dw_uq, "w_ukv": dw_ukv, "w_in": dw_in})
    dh = _mm_nt("proj_lat_dgrad", d_lat, w_lat + zero.astype(BF))
    for g in range(DIL_GROUPS):
        dh = _mm_nt(f"proj_dil{g}_dgrad", d_qkv[g], w_dil[g], add=dh)
    dh = _mm_nt("proj_gate_dgrad", d_gpre, w_gate, add=dh)
    grad_x, d_attn_g = _rmsnorm_bwd("attn_norm_bwd", dh, x, W["attn_norm_g"], dx1)

    small = {"attn_norm_g": d_attn_g, "b_gate": d_b_gate, "q_norm_g": d_q_g, "kv_norm_g": d_kv_g,
             "ffn_norm_g": d_ffn_g, "conv_w": d_conv_w, "conv_b": d_conv_b.reshape(1, G4 * C),
             "final_norm_g": d_final_g}
    return loss8[0, 0], grad_x, small


BIG = ("w_in", "w_uq", "w_ukv", "w_o_mla", "w_o_dil", "w_out", "w_up", "w_down")
SMALL = ("attn_norm_g", "b_gate", "q_norm_g", "kv_norm_g", "ffn_norm_g", "conv_w", "conv_b", "final_norm_g")
WEIGHTS = ("attn_norm_g", "w_in", "b_gate", "q_norm_g", "w_uq", "kv_norm_g", "w_ukv", "w_o_mla", "w_o_dil",
           "w_out", "ffn_norm_g", "w_up", "conv_w", "conv_b", "w_down", "final_norm_g")
SMALL_ROWS = 8
COLUMN_MAJOR = ("w_in", "w_up")
HALF_AXIS = {"w_down": 1}


def _gather_start(shards):
    names = [n for grp in GATHER_GROUPS for n in grp]
    chip = 2 * lax.axis_index("x") + lax.axis_index("y")
    c = lax.axis_index("c")
    srcs, lands = [], []
    for n in names:
        s = shards[n]
        lands.append(lax.dynamic_update_slice(lax.empty((N_CHIPS,) + s.shape, s.dtype), s[None], (chip, 0, 0)))
        if n in SHARED_FETCH:
            s = lax.dynamic_slice_in_dim(s, c * (s.shape[0] // 2), s.shape[0] // 2, 0)
        srcs.append(s)
    groups, at = [], 0
    for grp in GATHER_GROUPS:
        groups.append(list(range(at, at + len(grp))))
        at += len(grp)
    sems, srcs, lands, token = _exchange_start("gather_start", "gather", srcs, lands, groups)

    def fetch(i, after):
        idx = groups[i]
        _, got = _exchange_wait(f"gather_wait{i}", "gather", [srcs[j] for j in idx], [lands[j] for j in idx], sems[i], after)
        return {n: _pair_share(f"pair_share_{n}", g) if n in SHARED_FETCH else g for n, g in zip(GATHER_GROUPS[i], got)}

    return fetch, token[0, 0]


def _reduce_start(i, grads):
    names = REDUCE_GROUPS[i]
    hs = [_pair_sum(f"pair_sum_{n}", grads[n], HALF_AXIS.get(n, 0)) for n in names]
    lands = [lax.empty((3,) + h.shape[1:], h.dtype) for h in hs]
    sems, hs, lands, token = _exchange_start(f"reduce_start{i}", "scatter", hs, lands, [list(range(len(names)))])
    return (sems[0], hs, lands), token[0, 0]


def _reduce_finish(i, pending, after):
    sems, hs, lands = pending
    hs, lands = _exchange_wait(f"reduce_wait{i}", "scatter", hs, lands, sems, after)
    out = {}
    for n, h, landed in zip(REDUCE_GROUPS[i], hs, lands):
        out[n] = _pair_join(f"pair_join_{n}", _chip_total(f"chip_total_{n}", h, landed), HALF_AXIS.get(n, 0))
    return out


def _reduce_small(small):
    flat = [small[n].reshape(-1) for n in SMALL]
    sizes = [f.shape[0] for f in flat]
    total = sum(sizes)
    width = -(-total // (SMALL_ROWS * LANE)) * LANE
    packed = jnp.concatenate(flat + [jnp.zeros((SMALL_ROWS * width - total,), F32)]).reshape(SMALL_ROWS, width)
    red = _allreduce_small(packed).reshape(-1)
    out, off = {}, 0
    for n, s in zip(SMALL, sizes):
        out[n] = red[off:off + s]
        off += s
    return out


def kernel(x, attn_norm_g, w_in, b_gate, q_norm_g, w_uq, kv_norm_g, w_ukv, w_o_mla, w_o_dil, w_out, ffn_norm_g, w_up, conv_w, conv_b, w_down, final_norm_g, loss_target, m_attn_norm_g, m_w_in, m_b_gate, m_q_norm_g, m_w_uq, m_kv_norm_g, m_w_ukv, m_w_o_mla, m_w_o_dil, m_w_out, m_ffn_norm_g, m_w_up, m_conv_w, m_conv_b, m_w_down, m_final_norm_g, v_attn_norm_g, v_w_in, v_b_gate, v_q_norm_g, v_w_uq, v_kv_norm_g, v_w_ukv, v_w_o_mla, v_w_o_dil, v_w_out, v_ffn_norm_g, v_w_up, v_conv_w, v_conv_b, v_w_down, v_final_norm_g):
    given = dict(attn_norm_g=attn_norm_g, w_in=w_in, b_gate=b_gate, q_norm_g=q_norm_g, w_uq=w_uq, kv_norm_g=kv_norm_g,
                 w_ukv=w_ukv, w_o_mla=w_o_mla, w_o_dil=w_o_dil, w_out=w_out, ffn_norm_g=ffn_norm_g, w_up=w_up,
                 conv_w=conv_w, conv_b=conv_b, w_down=w_down, final_norm_g=final_norm_g)
    moments_m = dict(attn_norm_g=m_attn_norm_g, w_in=m_w_in, b_gate=m_b_gate, q_norm_g=m_q_norm_g, w_uq=m_w_uq,
                     kv_norm_g=m_kv_norm_g, w_ukv=m_w_ukv, w_o_mla=m_w_o_mla, w_o_dil=m_w_o_dil, w_out=m_w_out,
                     ffn_norm_g=m_ffn_norm_g, w_up=m_w_up, conv_w=m_conv_w, conv_b=m_conv_b, w_down=m_w_down,
                     final_norm_g=m_final_norm_g)
    moments_v = dict(attn_norm_g=v_attn_norm_g, w_in=v_w_in, b_gate=v_b_gate, q_norm_g=v_q_norm_g, w_uq=v_w_uq,
                     kv_norm_g=v_kv_norm_g, w_ukv=v_w_ukv, w_o_mla=v_w_o_mla, w_o_dil=v_w_o_dil, w_out=v_w_out,
                     ffn_norm_g=v_ffn_norm_g, w_up=v_w_up, conv_w=v_conv_w, conv_b=v_conv_b, w_down=v_w_down,
                     final_norm_g=v_final_norm_g)

    shards = {n: given[n][0].astype(BF) for n in BIG}
    shards["conv_w"] = given["conv_w"][0]
    fetch, zero = _gather_start(shards)
    W = {n: given[n] for n in ("b_gate", "q_norm_g", "kv_norm_g", "ffn_norm_g", "conv_b")}
    W["attn_norm_g"] = given["attn_norm_g"] + zero
    W["final_norm_g"] = given["final_norm_g"].reshape(1, -1)

    pending = {}

    def emit(i, grads):
        pending[i], token = _reduce_start(i, grads)
        return token

    loss_part, grad_x, small = _local_step(x[0], loss_target[0], W, fetch, emit)
    loss = lax.psum(loss_part, ("x", "y", "c"))
    grads, delta, new_m, new_v = {}, {}, {}, {}

    def adamw(n, g):
        shp = given[n].shape
        two_d = (-1, shp[-1]) if len(shp) > 1 else (1, -1)
        view = (lambda a: a.reshape(two_d).T) if n in COLUMN_MAJOR else (lambda a: a.reshape(two_d))
        back = (lambda a: a.T.reshape(shp)) if n in COLUMN_MAJOR else (lambda a: a.reshape(shp))
        go, d, nm, nv = _adamw(f"adamw_{n}", view(given[n]), view(g), view(moments_m[n]), view(moments_v[n]))
        grads[n], delta[n], new_m[n], new_v[n] = back(go), back(d), back(nm), back(nv)

    after = grad_x
    for i in range(len(REDUCE_GROUPS)):
        for n, g in _reduce_finish(i, pending[i], after).items():
            adamw(n, g)
        after = delta[REDUCE_GROUPS[i][-1]]
    g_small = _reduce_small(small)
    chip = 2 * lax.axis_index("x") + lax.axis_index("y")
    for n in SMALL:
        if n == "conv_w":
            full = g_small[n].reshape(N_CHIPS, 3, -1)
            adamw(n, lax.dynamic_index_in_dim(full, chip, 0, keepdims=True))
        else:
            adamw(n, g_small[n])

    return (loss, grad_x[None], *[grads[n] for n in WEIGHTS], *[delta[n] for n in WEIGHTS],
            *[new_m[n] for n in WEIGHTS], *[new_v[n] for n in WEIGHTS])
```

```python
import functools
import math

import numpy as np
import jax
import jax.numpy as jnp
from jax import lax
from jax.experimental import pallas as pl
from jax.experimental.pallas import tpu as pltpu

F32 = jnp.float32
BF = jnp.bfloat16
MESH = pl.DeviceIdType.MESH

D_MODEL = 2048
MLA_HEADS = 8
QK_NOPE = 128
QK_ROPE = 64
Q_RANK = 512
KV_RANK = 256
ROPE_THETA = 10000.0
DIL_PATTERNS = ((128, 1), (512, 4), (2048, 16))
DIL_GROUPS = 3
DIL_HPG = 4
DIL_HEADS = 12
HEAD = 128
DIL_BLOCK = 128
ALIBI_MAX_BIAS = 8.0
NORM_EPS = 1e-6
N_CHIPS = 4
ADAM_LR = 0.001
ADAM_B1 = 0.9
ADAM_B2 = 0.999
ADAM_EPS = 1e-08
ADAM_WD = 0.01
ADAM_STEP = 10

LANE = 128
VMEM_LIMIT = 56 * 1024 * 1024
MLA_SCALE = (QK_NOPE + QK_ROPE) ** -0.5
DIL_SCALE = HEAD ** -0.5


def _params(*sem):
    return pltpu.CompilerParams(dimension_semantics=sem, vmem_limit_bytes=VMEM_LIMIT)


def _tile(n, pref):
    t = (pref // LANE) * LANE
    while t >= LANE:
        if n % t == 0:
            return t
        t -= LANE
    return n


NN = (((1,), (0,)), ((), ()))
NT = (((1,), (1,)), ((), ()))
TN = (((0,), (0,)), ((), ()))


def _mm_call(name, a, b, add, *, grid, a_spec, b_spec, add_spec, o_spec, o_shape, o_dtype, acc_shape, dims, nk):
    nax = len(grid)

    def body(*refs):
        if add is None:
            a_ref, b_ref, o_ref = refs[:3]
            c_ref = None
            scr = refs[3:]
        else:
            a_ref, b_ref, c_ref, o_ref = refs[:4]
            scr = refs[4:]
        prod = lax.dot_general(a_ref[...].astype(BF), b_ref[...].astype(BF), dims, preferred_element_type=F32)
        if nk == 1:
            if c_ref is not None:
                prod = prod + c_ref[...]
            o_ref[...] = prod.astype(o_ref.dtype)
        else:
            acc = scr[0]
            k = pl.program_id(nax - 1)

            @pl.when(k == 0)
            def _():
                if c_ref is not None:
                    acc[...] = prod + c_ref[...]
                else:
                    acc[...] = prod

            @pl.when(k > 0)
            def _():
                acc[...] += prod

            @pl.when(k == nk - 1)
            def _():
                o_ref[...] = acc[...].astype(o_ref.dtype)

    ins = [a, b] + ([] if add is None else [add])
    specs = [a_spec, b_spec] + ([] if add is None else [add_spec])
    sem = ("parallel",) * (nax - 1) + ("arbitrary",)
    return pl.pallas_call(
        body, name=name, grid=grid, in_specs=specs, out_specs=o_spec,
        out_shape=jax.ShapeDtypeStruct(o_shape, o_dtype),
        scratch_shapes=[] if nk == 1 else [pltpu.VMEM(acc_shape, F32)],
        compiler_params=_params(*sem),
    )(*ins)


def _mm_nn(name, a, b, *, add=None, o_dtype=F32):
    M, K = a.shape
    sharded = b.ndim == 3
    Ns = b.shape[-1]
    N = Ns * (b.shape[0] if sharded else 1)
    tm, tn, tk = _tile(M, 1024), _tile(Ns, 1024), _tile(K, 2048)
    per = Ns // tn
    nk = K // tk
    if sharded:
        b_spec = pl.BlockSpec((None, tk, tn), lambda i, j, k: (j // per, k, j % per))
    else:
        b_spec = pl.BlockSpec((tk, tn), lambda i, j, k: (k, j))
    return _mm_call(
        name, a, b, add, grid=(M // tm, N // tn, nk),
        a_spec=pl.BlockSpec((tm, tk), lambda i, j, k: (i, k)), b_spec=b_spec,
        add_spec=pl.BlockSpec((tm, tn), lambda i, j, k: (i, j)),
        o_spec=pl.BlockSpec((tm, tn), lambda i, j, k: (i, j)),
        o_shape=(M, N), o_dtype=o_dtype, acc_shape=(tm, tn), dims=NN, nk=nk)


def _mm_nt(name, a, b, *, add=None, o_dtype=F32):
    M, K = a.shape
    sharded = b.ndim == 3
    N, Ks = b.shape[-2], b.shape[-1]
    tm, tn, tk = _tile(M, 1024), _tile(N, 1024), _tile(Ks, 2048)
    per = Ks // tk
    nk = K // tk
    if sharded:
        b_spec = pl.BlockSpec((None, tn, tk), lambda i, j, k: (k // per, j, k % per))
    else:
        b_spec = pl.BlockSpec((tn, tk), lambda i, j, k: (j, k))
    return _mm_call(
        name, a, b, add, grid=(M // tm, N // tn, nk),
        a_spec=pl.BlockSpec((tm, tk), lambda i, j, k: (i, k)), b_spec=b_spec,
        add_spec=pl.BlockSpec((tm, tn), lambda i, j, k: (i, j)),
        o_spec=pl.BlockSpec((tm, tn), lambda i, j, k: (i, j)),
        o_shape=(M, N), o_dtype=o_dtype, acc_shape=(tm, tn), dims=NT, nk=nk)


def _mm_tn(name, a, b, *, shards=1, o_dtype=BF):
    S, M = a.shape
    N = b.shape[1]
    Ns = N // shards
    tm, tn, tk = _tile(M, 1024), _tile(Ns, 1024), _tile(S, 2048)
    per = Ns // tn
    nk = S // tk
    if shards > 1:
        o_spec = pl.BlockSpec((None, tm, tn), lambda i, j, k: (j // per, i, j % per))
        o_shape = (shards, M, Ns)
    else:
        o_spec = pl.BlockSpec((tm, tn), lambda i, j, k: (i, j))
        o_shape = (M, N)
    return _mm_call(
        name, a, b, None, grid=(M // tm, N // tn, nk),
        a_spec=pl.BlockSpec((tk, tm), lambda i, j, k: (k, i)),
        b_spec=pl.BlockSpec((tk, tn), lambda i, j, k: (k, j)),
        add_spec=None, o_spec=o_spec, o_shape=o_shape, o_dtype=o_dtype, acc_shape=(tm, tn), dims=TN, nk=nk)


def _up_fwd(h2, w_up):
    S, D = h2.shape
    G, _, C = w_up.shape
    tm = _tile(S, 512)
    return _mm_call(
        "up_fwd", h2, w_up, None, grid=(G, S // tm, 1),
        a_spec=pl.BlockSpec((tm, D), lambda g, i, k: (i, 0)),
        b_spec=pl.BlockSpec((None, D, C), lambda g, i, k: (g, 0, 0)),
        add_spec=None, o_spec=pl.BlockSpec((None, tm, C), lambda g, i, k: (g, i, 0)),
        o_shape=(G, S, C), o_dtype=BF, acc_shape=None, dims=NN, nk=1)


def _up_dgrad(du_pre, w_up):
    G, S, C = du_pre.shape
    D = w_up.shape[1]
    tm, tn = _tile(S, 1024), _tile(D, 1024)
    return _mm_call(
        "up_dgrad", du_pre, w_up, None, grid=(S // tm, D // tn, G),
        a_spec=pl.BlockSpec((None, tm, C), lambda i, j, g: (g, i, 0)),
        b_spec=pl.BlockSpec((None, tn, C), lambda i, j, g: (g, j, 0)),
        add_spec=None, o_spec=pl.BlockSpec((tm, tn), lambda i, j, g: (i, j)),
        o_shape=(S, D), o_dtype=F32, acc_shape=(tm, tn), dims=NT, nk=G)


def _up_wgrad(h2, du_pre):
    G, S, C = du_pre.shape
    D = h2.shape[1]
    tm, tk = _tile(D, 512), _tile(S, 2048)
    return _mm_call(
        "up_wgrad", h2, du_pre, None, grid=(G, D // tm, S // tk),
        a_spec=pl.BlockSpec((tk, tm), lambda g, i, k: (k, i)),
        b_spec=pl.BlockSpec((None, tk, C), lambda g, i, k: (g, k, 0)),
        add_spec=None, o_spec=pl.BlockSpec((None, tm, C), lambda g, i, k: (g, i, 0)),
        o_shape=(G, D, C), o_dtype=BF, acc_shape=(tm, C), dims=TN, nk=S // tk)


def _down_fwd(act, w_down, x1):
    G, S, C = act.shape
    D = w_down.shape[2]
    tm, tn = _tile(S, 1024), _tile(D, 1024)
    return _mm_call(
        "down_fwd", act, w_down, x1, grid=(S // tm, D // tn, G),
        a_spec=pl.BlockSpec((None, tm, C), lambda i, j, g: (g, i, 0)),
        b_spec=pl.BlockSpec((None, C, tn), lambda i, j, g: (g, 0, j)),
        add_spec=pl.BlockSpec((tm, tn), lambda i, j, g: (i, j)),
        o_spec=pl.BlockSpec((tm, tn), lambda i, j, g: (i, j)),
        o_shape=(S, D), o_dtype=F32, acc_shape=(tm, tn), dims=NN, nk=G)


def _down_dgrad(dx2, w_down):
    S, D = dx2.shape
    G, C, _ = w_down.shape
    tm = _tile(S, 512)
    return _mm_call(
        "down_dgrad", dx2, w_down, None, grid=(G, S // tm, 1),
        a_spec=pl.BlockSpec((tm, D), lambda g, i, k: (i, 0)),
        b_spec=pl.BlockSpec((None, C, D), lambda g, i, k: (g, 0, 0)),
        add_spec=None, o_spec=pl.BlockSpec((None, tm, C), lambda g, i, k: (g, i, 0)),
        o_shape=(G, S, C), o_dtype=BF, acc_shape=None, dims=NT, nk=1)


def _down_wgrad(act, dx2):
    G, S, C = act.shape
    D = dx2.shape[1]
    tn, tk = _tile(D, 512), _tile(S, 1024)
    return _mm_call(
        "down_wgrad", act, dx2, None, grid=(G, D // tn, S // tk),
        a_spec=pl.BlockSpec((None, tk, C), lambda g, j, k: (g, k, 0)),
        b_spec=pl.BlockSpec((tk, tn), lambda g, j, k: (k, j)),
        add_spec=None, o_spec=pl.BlockSpec((None, C, tn), lambda g, j, k: (g, 0, j)),
        o_shape=(G, C, D), o_dtype=BF, acc_shape=(C, tn), dims=TN, nk=S // tk)


def _row(ts, c):
    return pl.BlockSpec((ts, c), lambda i: (i, 0))


def _bcast(r, c):
    return pl.BlockSpec((r, c), lambda i: (0, 0))


def _accumulate(i, ref, val):
    @pl.when(i == 0)
    def _():
        ref[...] = val

    @pl.when(i > 0)
    def _():
        ref[...] += val


def _rstd(xv):
    return lax.rsqrt(jnp.mean(xv * xv, axis=-1, keepdims=True) + NORM_EPS)


def _rmsnorm_fwd(name, x, g):
    S, D = x.shape
    ts = _tile(S, 512)

    def body(x_ref, g_ref, o_ref):
        xv = x_ref[...]
        o_ref[...] = (xv * _rstd(xv) * g_ref[...]).astype(o_ref.dtype)

    return pl.pallas_call(
        body, name=name, grid=(S // ts,), in_specs=[_row(ts, D), _bcast(1, D)], out_specs=_row(ts, D),
        out_shape=jax.ShapeDtypeStruct((S, D), BF), compiler_params=_params("parallel"))(x, g)


def _norm_bwd_rows(dy, xv, g):
    r = _rstd(xv)
    xh = xv * r
    dxh = dy * g
    dx = r * (dxh - xh * jnp.mean(dxh * xh, axis=-1, keepdims=True))
    return dx, jnp.sum(dy * xh, axis=0, keepdims=True)


def _rmsnorm_bwd(name, dy, x, g, res):
    S, D = x.shape
    ts = _tile(S, 512)

    def body(dy_ref, x_ref, g_ref, res_ref, dx_ref, dg_ref):
        dx, dg = _norm_bwd_rows(dy_ref[...], x_ref[...], g_ref[...])
        dx_ref[...] = dx + res_ref[...]
        _accumulate(pl.program_id(0), dg_ref, dg)

    return pl.pallas_call(
        body, name=name, grid=(S // ts,),
        in_specs=[_row(ts, D), _row(ts, D), _bcast(1, D), _row(ts, D)],
        out_specs=[_row(ts, D), _bcast(1, D)],
        out_shape=[jax.ShapeDtypeStruct((S, D), F32), jax.ShapeDtypeStruct((1, D), F32)],
        compiler_params=_params("arbitrary"))(dy, x, g, res)


def _rope_tables(S):
    half = QK_ROPE // 2
    pos = jnp.arange(S, dtype=F32)
    inv_freq = ROPE_THETA ** (-jnp.arange(0, QK_ROPE, 2, dtype=F32) / QK_ROPE)
    ang = pos[:, None] * inv_freq[None, :]
    cos, sin = jnp.cos(ang), jnp.sin(ang)
    z = jnp.zeros((S, half), F32)
    return jnp.concatenate([cos, z, cos, z], axis=1), jnp.concatenate([-sin, z, sin, z], axis=1)


def _rope_lanes(x, cos, sin_signed, inverse):
    if inverse:
        return x * cos + pltpu.roll(x * sin_signed, LANE // 2, 1)
    return x * cos + pltpu.roll(x, LANE // 2, 1) * sin_signed


def _rope(name, x, cos, sin_signed, inverse):
    S, W = x.shape
    ts = _tile(S, 512)

    def body(x_ref, c_ref, s_ref, o_ref):
        c, s = c_ref[...], s_ref[...]
        for h in range(W // LANE):
            sl = slice(h * LANE, (h + 1) * LANE)
            o_ref[:, sl] = _rope_lanes(x_ref[:, sl], c, s, inverse).astype(o_ref.dtype)

    return pl.pallas_call(
        body, name=name, grid=(S // ts,), in_specs=[_row(ts, W), _row(ts, LANE), _row(ts, LANE)],
        out_specs=_row(ts, W), out_shape=jax.ShapeDtypeStruct((S, W), BF),
        compiler_params=_params("parallel"))(x, cos, sin_signed)


LAT_W = 1024
_CQ = slice(0, Q_RANK)
_CKV = slice(Q_RANK, Q_RANK + KV_RANK)
_KPE = slice(Q_RANK + KV_RANK, Q_RANK + KV_RANK + LANE)


def _mla_prep(lat, qg, kvg, cos, sin_signed):
    S = lat.shape[0]
    ts = _tile(S, 512)

    def body(lat_ref, qg_ref, kvg_ref, c_ref, s_ref, qn_ref, kvn_ref, kpe_ref):
        cq = lat_ref[:, _CQ]
        qn_ref[...] = (cq * _rstd(cq) * qg_ref[...]).astype(BF)
        ckv = lat_ref[:, _CKV]
        kvn_ref[...] = (ckv * _rstd(ckv) * kvg_ref[...]).astype(BF)
        kpe_ref[...] = _rope_lanes(lat_ref[:, _KPE], c_ref[...], s_ref[...], False).astype(BF)

    return pl.pallas_call(
        body, name="mla_prep", grid=(S // ts,),
        in_specs=[_row(ts, LAT_W), _bcast(1, Q_RANK), _bcast(1, KV_RANK), _row(ts, LANE), _row(ts, LANE)],
        out_specs=[_row(ts, Q_RANK), _row(ts, KV_RANK), _row(ts, LANE)],
        out_shape=[jax.ShapeDtypeStruct((S, Q_RANK), BF), jax.ShapeDtypeStruct((S, KV_RANK), BF),
                   jax.ShapeDtypeStruct((S, LANE), BF)],
        compiler_params=_params("parallel"))(lat, qg, kvg, cos, sin_signed)


def _mla_prep_bwd(lat, qg, kvg, cos, sin_signed, d_qn, d_kvn, d_kpe):
    S = lat.shape[0]
    ts = _tile(S, 512)

    def body(lat_ref, qg_ref, kvg_ref, c_ref, s_ref, dqn_ref, dkvn_ref, dkpe_ref, dlat_ref, dqg_ref, dkvg_ref):
        i = pl.program_id(0)
        dcq, dqg = _norm_bwd_rows(dqn_ref[...], lat_ref[:, _CQ], qg_ref[...])
        dckv, dkvg = _norm_bwd_rows(dkvn_ref[...], lat_ref[:, _CKV], kvg_ref[...])
        dlat_ref[:, _CQ] = dcq.astype(BF)
        dlat_ref[:, _CKV] = dckv.astype(BF)
        dkpe = dkpe_ref[0]
        for g in range(1, d_kpe.shape[0]):
            dkpe = dkpe + dkpe_ref[g]
        dlat_ref[:, _KPE] = _rope_lanes(dkpe, c_ref[...], s_ref[...], True).astype(BF)
        dlat_ref[:, _KPE.stop:] = jnp.zeros((ts, LAT_W - _KPE.stop), BF)
        _accumulate(i, dqg_ref, dqg)
        _accumulate(i, dkvg_ref, dkvg)

    return pl.pallas_call(
        body, name="mla_prep_bwd", grid=(S // ts,),
        in_specs=[_row(ts, LAT_W), _bcast(1, Q_RANK), _bcast(1, KV_RANK), _row(ts, LANE), _row(ts, LANE),
                  _row(ts, Q_RANK), _row(ts, KV_RANK), pl.BlockSpec((d_kpe.shape[0], ts, LANE), lambda i: (0, i, 0))],
        out_specs=[_row(ts, LAT_W), _bcast(1, Q_RANK), _bcast(1, KV_RANK)],
        out_shape=[jax.ShapeDtypeStruct((S, LAT_W), BF), jax.ShapeDtypeStruct((1, Q_RANK), F32),
                   jax.ShapeDtypeStruct((1, KV_RANK), F32)],
        compiler_params=_params("arbitrary"))(lat, qg, kvg, cos, sin_signed, d_qn, d_kvn, d_kpe)


def _sigmoid(z):
    return 1.0 / (1.0 + jnp.exp(-z))


def _merge_fwd(gpre, b_gate, o_a, o_b):
    S, D = o_a.shape
    ts = _tile(S, 256)

    def body(g_ref, b_ref, oa_ref, ob_ref, m_ref):
        ga = _sigmoid(g_ref[:, :D] + b_ref[:, :D])
        gb = _sigmoid(g_ref[:, D:] + b_ref[:, D:])
        m_ref[...] = (ga * oa_ref[...] + gb * ob_ref[...]).astype(BF)

    return pl.pallas_call(
        body, name="merge_fwd", grid=(S // ts,),
        in_specs=[_row(ts, 2 * D), _bcast(1, 2 * D), _row(ts, D), _row(ts, D)], out_specs=_row(ts, D),
        out_shape=jax.ShapeDtypeStruct((S, D), BF), compiler_params=_params("parallel"))(gpre, b_gate, o_a, o_b)


def _merge_bwd(d_merge, gpre, b_gate, o_a, o_b):
    S, D = o_a.shape
    ts = _tile(S, 256)

    def body(dm_ref, g_ref, b_ref, oa_ref, ob_ref, doa_ref, dob_ref, dg_ref, db_ref):
        dm = dm_ref[...]
        ga = _sigmoid(g_ref[:, :D] + b_ref[:, :D])
        gb = _sigmoid(g_ref[:, D:] + b_ref[:, D:])
        doa_ref[...] = (dm * ga).astype(BF)
        dob_ref[...] = (dm * gb).astype(BF)
        dga = dm * oa_ref[...] * ga * (1.0 - ga)
        dgb = dm * ob_ref[...] * gb * (1.0 - gb)
        dg_ref[:, :D] = dga.astype(BF)
        dg_ref[:, D:] = dgb.astype(BF)
        i = pl.program_id(0)
        part = jnp.concatenate([jnp.sum(dga, axis=0, keepdims=True), jnp.sum(dgb, axis=0, keepdims=True)], axis=1)
        _accumulate(i, db_ref, part)

    return pl.pallas_call(
        body, name="merge_bwd", grid=(S // ts,),
        in_specs=[_row(ts, D), _row(ts, 2 * D), _bcast(1, 2 * D), _row(ts, D), _row(ts, D)],
        out_specs=[_row(ts, D), _row(ts, D), _row(ts, 2 * D), _bcast(1, 2 * D)],
        out_shape=[jax.ShapeDtypeStruct((S, D), BF), jax.ShapeDtypeStruct((S, D), BF),
                   jax.ShapeDtypeStruct((S, 2 * D), BF), jax.ShapeDtypeStruct((1, 2 * D), F32)],
        compiler_params=_params("arbitrary"))(d_merge, gpre, b_gate, o_a, o_b)


def _final_loss(x2, tgt, gf):
    S, D = x2.shape
    ts = _tile(S, 512)

    def body(x_ref, t_ref, g_ref, dx_ref, dg_ref, loss_ref):
        i = pl.program_id(0)
        xv = x_ref[...]
        g = g_ref[...]
        y = xv * _rstd(xv) * g
        err = y - t_ref[...]
        dx, dg = _norm_bwd_rows(err * (1.0 / D), xv, g)
        dx_ref[...] = dx
        _accumulate(i, dg_ref, dg)
        part = 0.5 * jnp.sum(jnp.mean(err * err, axis=-1, keepdims=True), axis=0, keepdims=True)
        _accumulate(i, loss_ref, jnp.broadcast_to(part, (8, LANE)))

    return pl.pallas_call(
        body, name="final_loss", grid=(S // ts,),
        in_specs=[_row(ts, D), _row(ts, D), _bcast(1, D)],
        out_specs=[_row(ts, D), _bcast(1, D), _bcast(8, LANE)],
        out_shape=[jax.ShapeDtypeStruct((S, D), F32), jax.ShapeDtypeStruct((1, D), F32),
                   jax.ShapeDtypeStruct((8, LANE), F32)],
        compiler_params=_params("arbitrary"))(x2, tgt, gf)


HALO = 16


SUB = 8


def _shift_down(cur, prev, k, rows):
    out = pltpu.roll(cur, k, 0)
    head = out[:SUB]
    for j in range(k):
        head = jnp.where(rows == j, prev[HALO - k + j:HALO - k + j + 1, :], head)
    return jnp.concatenate([head, out[SUB:]], axis=0)


def _shift_up(cur, nxt, k, rows, ts):
    out = pltpu.roll(cur, ts - k, 0)
    tail = out[ts - SUB:]
    for j in range(k):
        tail = jnp.where(rows == SUB - k + j, nxt[j:j + 1, :], tail)
    return jnp.concatenate([out[:ts - SUB], tail], axis=0)


def _conv_rows(cur, prev, w, b, rows):
    return b + w[0:1, :] * _shift_down(cur, prev, 2, rows) + w[1:2, :] * _shift_down(cur, prev, 1, rows) + w[2:3, :] * cur


def _conv_specs(ts, C, shard_of):
    nh = ts // HALO
    cur = pl.BlockSpec((None, ts, C), lambda g, i: (shard_of(g), i, 0))
    prev = pl.BlockSpec((None, HALO, C), lambda g, i: (shard_of(g), jnp.maximum(i * nh - 1, 0), 0))
    return cur, prev


def _ffn_act(u_pre, conv_w, conv_b):
    G4, S, C = u_pre.shape
    G = G4 // 2
    ts = _tile(S, 256)

    def body(up_ref, upp_ref, gt_ref, gtp_ref, wu_ref, wg_ref, bu_ref, bg_ref, act_ref, u_ref):
        first = pl.program_id(1) == 0
        rows = lax.broadcasted_iota(jnp.int32, (SUB, C), 0)
        pu = jnp.where(first, 0.0, upp_ref[...].astype(F32))
        pg = jnp.where(first, 0.0, gtp_ref[...].astype(F32))
        up = _conv_rows(up_ref[...].astype(F32), pu, wu_ref[...], bu_ref[...], rows)
        gate = _conv_rows(gt_ref[...].astype(F32), pg, wg_ref[...], bg_ref[...], rows)
        act_ref[...] = (gate * _sigmoid(gate) * up).astype(BF)
        u_ref[0] = up.astype(BF)
        u_ref[1] = gate.astype(BF)

    cur_u, prev_u = _conv_specs(ts, C, lambda g: g)
    cur_g, prev_g = _conv_specs(ts, C, lambda g: g + G)
    w_u = pl.BlockSpec((None, 3, C), lambda g, i: (g, 0, 0))
    w_g = pl.BlockSpec((None, 3, C), lambda g, i: (g + G, 0, 0))
    b_u = pl.BlockSpec((None, 1, C), lambda g, i: (g, 0, 0))
    b_g = pl.BlockSpec((None, 1, C), lambda g, i: (g + G, 0, 0))
    pair = pl.BlockSpec((2, None, ts, C), lambda g, i: (0, g, i, 0))
    act, u = pl.pallas_call(
        body, name="ffn_act", grid=(G, S // ts),
        in_specs=[cur_u, prev_u, cur_g, prev_g, w_u, w_g, b_u, b_g],
        out_specs=[pl.BlockSpec((None, ts, C), lambda g, i: (g, i, 0)), pair],
        out_shape=[jax.ShapeDtypeStruct((G, S, C), BF), jax.ShapeDtypeStruct((2, G, S, C), BF)],
        compiler_params=_params("parallel", "parallel"))(u_pre, u_pre, u_pre, u_pre, conv_w, conv_w, conv_b, conv_b)
    return act, u


def _ffn_act_bwd(u, d_act):
    _, G, S, C = u.shape
    ts = _tile(S, 256)

    def body(u_ref, da_ref, du_ref):
        up, gate = u_ref[0].astype(F32), u_ref[1].astype(F32)
        sg = _sigmoid(gate)
        da = da_ref[...].astype(F32)
        du_ref[0] = (da * (gate * sg)).astype(BF)
        du_ref[1] = (da * up * (sg * (1.0 + gate * (1.0 - sg)))).astype(BF)

    pair = pl.BlockSpec((2, None, ts, C), lambda g, i: (0, g, i, 0))
    du = pl.pallas_call(
        body, name="ffn_act_bwd", grid=(G, S // ts),
        in_specs=[pair, pl.BlockSpec((None, ts, C), lambda g, i: (g, i, 0))], out_specs=pair,
        out_shape=jax.ShapeDtypeStruct((2, G, S, C), BF),
        compiler_params=_params("parallel", "parallel"))(u, d_act)
    return du.reshape(2 * G, S, C)


def _conv_bwd(du, u_pre, conv_w):
    G4, S, C = du.shape
    ts = _tile(S, 256)
    nh = ts // HALO
    last_halo = S // HALO - 1

    def body(du_ref, dun_ref, u_ref, up_ref, w_ref, dpre_ref, dw_ref, db_ref):
        i = pl.program_id(1)
        rows = lax.broadcasted_iota(jnp.int32, (SUB, C), 0)
        du_c = du_ref[...].astype(F32)
        nxt = jnp.where(i == pl.num_programs(1) - 1, 0.0, dun_ref[...].astype(F32))
        prev = jnp.where(i == 0, 0.0, up_ref[...].astype(F32))
        w = w_ref[...]
        dpre = w[2:3, :] * du_c + w[1:2, :] * _shift_up(du_c, nxt, 1, rows, ts) + w[0:1, :] * _shift_up(du_c, nxt, 2, rows, ts)
        dpre_ref[...] = dpre.astype(BF)
        u_c = u_ref[...].astype(F32)
        dw = jnp.concatenate([
            jnp.sum(du_c * _shift_down(u_c, prev, 2, rows), axis=0, keepdims=True),
            jnp.sum(du_c * _shift_down(u_c, prev, 1, rows), axis=0, keepdims=True),
            jnp.sum(du_c * u_c, axis=0, keepdims=True)], axis=0)
        _accumulate(i, dw_ref, dw)
        _accumulate(i, db_ref, jnp.sum(du_c, axis=0, keepdims=True))

    cur = pl.BlockSpec((None, ts, C), lambda g, i: (g, i, 0))
    nxt = pl.BlockSpec((None, HALO, C), lambda g, i: (g, jnp.minimum((i + 1) * nh, last_halo), 0))
    prev = pl.BlockSpec((None, HALO, C), lambda g, i: (g, jnp.maximum(i * nh - 1, 0), 0))
    return pl.pallas_call(
        body, name="conv_bwd", grid=(G4, S // ts),
        in_specs=[cur, nxt, cur, prev, pl.BlockSpec((None, 3, C), lambda g, i: (g, 0, 0))],
        out_specs=[cur, pl.BlockSpec((None, 3, C), lambda g, i: (g, 0, 0)), pl.BlockSpec((None, 1, C), lambda g, i: (g, 0, 0))],
        out_shape=[jax.ShapeDtypeStruct((G4, S, C), BF), jax.ShapeDtypeStruct((G4, 3, C), F32),
                   jax.ShapeDtypeStruct((G4, 1, C), F32)],
        compiler_params=_params("parallel", "arbitrary"))(du, du, u_pre, u_pre, conv_w)


MLA_T = 512
MLA_HB = 4
MLA_HW = MLA_HB * HEAD


def _mla_pairs(n, by_row):
    if by_row:
        pairs = [(i, j) for i in range(n) for j in range(i + 1)]
    else:
        pairs = [(i, j) for j in range(n) for i in range(j, n)]
    return jnp.asarray([p[0] for p in pairs], jnp.int32), jnp.asarray([p[1] for p in pairs], jnp.int32)


def _mla_specs(S):
    q = pl.BlockSpec((MLA_T, MLA_HW), lambda g, t, it, jt: (it[t], g))
    k = pl.BlockSpec((MLA_T, MLA_HW), lambda g, t, it, jt: (jt[t], g))
    kpe = pl.BlockSpec((MLA_T, HEAD), lambda g, t, it, jt: (jt[t], 0))
    lse = pl.BlockSpec((MLA_HB, MLA_T, LANE), lambda g, t, it, jt: (g, it[t], 0))
    return q, k, kpe, lse


def _mla_head(ref, hh):
    return ref[:, hh * HEAD:(hh + 1) * HEAD]


LOG2E = math.log2(math.e)
MLA_EXP2_SCALE = MLA_SCALE * LOG2E


def _mla_scores(qn_ref, qp_ref, kn_ref, kpe, hh, ok):
    q = jnp.concatenate([_mla_head(qn_ref, hh), _mla_head(qp_ref, hh)], axis=1)
    k = jnp.concatenate([_mla_head(kn_ref, hh), kpe], axis=1)
    s = lax.dot_general(q, k, NT, preferred_element_type=F32)
    return q, k, s if ok is None else jnp.where(ok, s, -jnp.inf)


def _mla_diagonal_mask():
    row = lax.broadcasted_iota(jnp.int32, (MLA_T, MLA_T), 0)
    col = lax.broadcasted_iota(jnp.int32, (MLA_T, MLA_T), 1)
    return col <= row


def _mla_step(i, j, step):
    @pl.when(j < i)
    def _():
        step(None)

    @pl.when(j == i)
    def _():
        step(_mla_diagonal_mask())


def _mla_fwd(qn, qp, kn, kpe, v):
    S = qn.shape[0]
    it, jt = _mla_pairs(S // MLA_T, True)

    def body(it_ref, jt_ref, qn_ref, qp_ref, kn_ref, kpe_ref, v_ref, o_ref, lse_ref, m_scr, acc_scr):
        t = pl.program_id(1)
        i, j = it_ref[t], jt_ref[t]

        @pl.when(j == 0)
        def _():
            m_scr[...] = jnp.full(m_scr.shape, -jnp.inf, F32)
            acc_scr[...] = jnp.zeros(acc_scr.shape, F32)

        def step(ok):
            kpe_v = kpe_ref[...]
            ones = jnp.ones((MLA_T, HEAD), BF)
            state = [(m_scr[hh], acc_scr[hh]) for hh in range(MLA_HB)]
            new = []
            for hh in range(MLA_HB):
                m_prev, acc = state[hh]
                _, _, s = _mla_scores(qn_ref, qp_ref, kn_ref, kpe_v, hh, ok)
                m_new = jnp.maximum(m_prev, jnp.max(s, axis=1, keepdims=True))
                p = jnp.exp2((s - m_new) * MLA_EXP2_SCALE).astype(BF)
                v1 = jnp.concatenate([_mla_head(v_ref, hh), ones], axis=1)
                alpha = jnp.exp2((m_prev - m_new) * MLA_EXP2_SCALE)
                new.append((m_new, alpha * acc + lax.dot_general(p, v1, NN, preferred_element_type=F32)))
            for hh in range(MLA_HB):
                m_scr[hh], acc_scr[hh] = new[hh]

        _mla_step(i, j, step)

        @pl.when(j == i)
        def _():
            for hh in range(MLA_HB):
                l = acc_scr[hh, :, HEAD:]
                o_ref[:, hh * HEAD:(hh + 1) * HEAD] = (acc_scr[hh, :, :HEAD] / l).astype(BF)
                lse_ref[hh] = m_scr[hh] * MLA_SCALE + jnp.log(l)

    qspec, kspec, kpespec, lsespec = _mla_specs(S)
    grid_spec = pltpu.PrefetchScalarGridSpec(
        num_scalar_prefetch=2, grid=(MLA_HEADS // MLA_HB, it.shape[0]),
        in_specs=[qspec, qspec, kspec, kpespec, kspec], out_specs=[qspec, lsespec],
        scratch_shapes=[pltpu.VMEM((MLA_HB, MLA_T, 1), F32), pltpu.VMEM((MLA_HB, MLA_T, 2 * HEAD), F32)])
    return pl.pallas_call(
        body, name="mla_fwd", grid_spec=grid_spec,
        out_shape=[jax.ShapeDtypeStruct((S, MLA_HEADS * HEAD), BF), jax.ShapeDtypeStruct((MLA_HEADS, S, LANE), F32)],
        compiler_params=_params("parallel", "arbitrary"))(it, jt, qn, qp, kn, kpe, v)


def _mla_p_ds(qn_ref, qp_ref, kn_ref, kpe, v_ref, do_ref, o_ref, lse_ref, hh, ok):
    q, k, s = _mla_scores(qn_ref, qp_ref, kn_ref, kpe, hh, ok)
    p = jnp.exp2(s * MLA_EXP2_SCALE - lse_ref[hh][:, 0:1] * LOG2E)
    do = _mla_head(do_ref, hh)
    delta = jnp.sum(do.astype(F32) * _mla_head(o_ref, hh).astype(F32), axis=1, keepdims=True)
    dp = lax.dot_general(do, _mla_head(v_ref, hh), NT, preferred_element_type=F32)
    ds = p * (dp - delta) * MLA_SCALE
    return q, k, p, ds, do


def _mla_bwd_dq(qn, qp, kn, kpe, v, do, o, lse):
    S = qn.shape[0]
    it, jt = _mla_pairs(S // MLA_T, True)

    def body(it_ref, jt_ref, qn_ref, qp_ref, kn_ref, kpe_ref, v_ref, do_ref, o_ref, lse_ref, dqn_ref, dqp_ref, acc):
        t = pl.program_id(1)
        i, j = it_ref[t], jt_ref[t]

        @pl.when(j == 0)
        def _():
            acc[...] = jnp.zeros(acc.shape, F32)

        def step(ok):
            kpe_v = kpe_ref[...]
            old = [acc[hh] for hh in range(MLA_HB)]
            for hh in range(MLA_HB):
                _, k, _, ds, _ = _mla_p_ds(qn_ref, qp_ref, kn_ref, kpe_v, v_ref, do_ref, o_ref, lse_ref, hh, ok)
                old[hh] = old[hh] + lax.dot_general(ds.astype(BF), k, NN, preferred_element_type=F32)
            for hh in range(MLA_HB):
                acc[hh] = old[hh]

        _mla_step(i, j, step)

        @pl.when(j == i)
        def _():
            for hh in range(MLA_HB):
                dqn_ref[:, hh * HEAD:(hh + 1) * HEAD] = acc[hh, :, :HEAD].astype(BF)
                dqp_ref[:, hh * HEAD:(hh + 1) * HEAD] = acc[hh, :, HEAD:]

    qspec, kspec, kpespec, lsespec = _mla_specs(S)
    grid_spec = pltpu.PrefetchScalarGridSpec(
        num_scalar_prefetch=2, grid=(MLA_HEADS // MLA_HB, it.shape[0]),
        in_specs=[qspec, qspec, kspec, kpespec, kspec, qspec, qspec, lsespec], out_specs=[qspec, qspec],
        scratch_shapes=[pltpu.VMEM((MLA_HB, MLA_T, 2 * HEAD), F32)])
    return pl.pallas_call(
        body, name="mla_bwd_dq", grid_spec=grid_spec,
        out_shape=[jax.ShapeDtypeStruct((S, MLA_HEADS * HEAD), BF), jax.ShapeDtypeStruct((S, MLA_HEADS * HEAD), F32)],
        compiler_params=_params("parallel", "arbitrary"))(it, jt, qn, qp, kn, kpe, v, do, o, lse)


def _mla_bwd_dkv(qn, qp, kn, kpe, v, do, o, lse):
    S = qn.shape[0]
    nq = S // MLA_T
    it, jt = _mla_pairs(nq, False)

    def body(it_ref, jt_ref, qn_ref, qp_ref, kn_ref, kpe_ref, v_ref, do_ref, o_ref, lse_ref, dkn_ref, dv_ref, dkpe_ref,
             dk_acc, dv_acc):
        t = pl.program_id(1)
        i, j = it_ref[t], jt_ref[t]

        @pl.when(i == j)
        def _():
            dk_acc[...] = jnp.zeros(dk_acc.shape, F32)
            dv_acc[...] = jnp.zeros(dv_acc.shape, F32)

        def step(ok):
            kpe_v = kpe_ref[...]
            for hh in range(MLA_HB):
                q, _, p, ds, do = _mla_p_ds(qn_ref, qp_ref, kn_ref, kpe_v, v_ref, do_ref, o_ref, lse_ref, hh, ok)
                dv_acc[hh] += lax.dot_general(p.astype(BF), do, TN, preferred_element_type=F32)
                dk_acc[hh] += lax.dot_general(ds.astype(BF), q, TN, preferred_element_type=F32)

        _mla_step(i, j, step)

        @pl.when(i == nq - 1)
        def _():
            dkpe = dk_acc[0, :, HEAD:]
            for hh in range(MLA_HB):
                dkn_ref[:, hh * HEAD:(hh + 1) * HEAD] = dk_acc[hh, :, :HEAD].astype(BF)
                dv_ref[:, hh * HEAD:(hh + 1) * HEAD] = dv_acc[hh].astype(BF)
                if hh:
                    dkpe = dkpe + dk_acc[hh, :, HEAD:]
            dkpe_ref[...] = dkpe

    qspec, kspec, kpespec, lsespec = _mla_specs(S)
    dkpespec = pl.BlockSpec((None, MLA_T, HEAD), lambda g, t, it, jt: (g, jt[t], 0))
    grid_spec = pltpu.PrefetchScalarGridSpec(
        num_scalar_prefetch=2, grid=(MLA_HEADS // MLA_HB, it.shape[0]),
        in_specs=[qspec, qspec, kspec, kpespec, kspec, qspec, qspec, lsespec], out_specs=[kspec, kspec, dkpespec],
        scratch_shapes=[pltpu.VMEM((MLA_HB, MLA_T, 2 * HEAD), F32), pltpu.VMEM((MLA_HB, MLA_T, HEAD), F32)])
    return pl.pallas_call(
        body, name="mla_bwd_dkv", grid_spec=grid_spec,
        out_shape=[jax.ShapeDtypeStruct((S, MLA_HEADS * HEAD), BF), jax.ShapeDtypeStruct((S, MLA_HEADS * HEAD), BF),
                   jax.ShapeDtypeStruct((MLA_HEADS // MLA_HB, S, HEAD), F32)],
        compiler_params=_params("parallel", "arbitrary"))(it, jt, qn, qp, kn, kpe, v, do, o, lse)


DIL_W = 3 * DIL_HPG * HEAD
DIL_O = DIL_HPG * HEAD
DIL_STEP_BLOCKS = 4


def _dil_slopes(g):
    return [2.0 ** (-ALIBI_MAX_BIAS * (g * DIL_HPG + hh + 1) / DIL_HEADS) for hh in range(DIL_HPG)]


def _dil_bias(dil):
    p = lax.broadcasted_iota(jnp.int32, (DIL_BLOCK, DIL_BLOCK), 0)
    kk = lax.broadcasted_iota(jnp.int32, (DIL_BLOCK, DIL_BLOCK), 1)
    jc = p - kk
    dist_c = (dil * jc).astype(F32)
    dist_p = (dil * (jc + DIL_BLOCK)).astype(F32)
    return jc >= 0, jc <= 0, dist_c, dist_p


def _dil_head(blk, hh):
    q = blk[:, hh * HEAD:(hh + 1) * HEAD]
    k = blk[:, DIL_O + hh * HEAD:DIL_O + (hh + 1) * HEAD]
    v = blk[:, 2 * DIL_O + hh * HEAD:2 * DIL_O + (hh + 1) * HEAD]
    return q, k, v


def _dil_s(q, k, slope, dist, ok):
    s = lax.dot_general(q, k, NT, preferred_element_type=F32) * DIL_SCALE - slope * dist
    return jnp.where(ok, s, -jnp.inf)


def _dil_view(a, dil):
    S, W = a.shape
    return a.reshape(S // dil, dil * W)


def _dil_fwd(qkv, g):
    _, dil = DIL_PATTERNS[g]
    S = qkv.shape[0]
    L = S // dil
    nb = L // DIL_BLOCK
    slopes = _dil_slopes(g)

    bb = min(DIL_STEP_BLOCKS, nb)
    rows = bb * DIL_BLOCK

    def body(cur_ref, prev_ref, o_ref, lse_ref):
        n = pl.program_id(1)
        ok_c, ok_p, dist_c, dist_p = _dil_bias(dil)
        for b in range(bb):
            cur = cur_ref[b * DIL_BLOCK:(b + 1) * DIL_BLOCK, :]
            prev = prev_ref[...] if b == 0 else cur_ref[(b - 1) * DIL_BLOCK:b * DIL_BLOCK, :]
            ok_b = ok_p & (n > 0) if b == 0 else ok_p
            for hh in range(DIL_HPG):
                q, kc, vc = _dil_head(cur, hh)
                _, kp, vp = _dil_head(prev, hh)
                sc = _dil_s(q, kc, slopes[hh], dist_c, ok_c)
                sp = _dil_s(q, kp, slopes[hh], dist_p, ok_b)
                m = jnp.maximum(jnp.max(sc, axis=1, keepdims=True), jnp.max(sp, axis=1, keepdims=True))
                pc, pp = jnp.exp(sc - m), jnp.exp(sp - m)
                l = jnp.sum(pc, axis=1, keepdims=True) + jnp.sum(pp, axis=1, keepdims=True)
                o = (lax.dot_general(pc.astype(BF), vc, NN, preferred_element_type=F32)
                     + lax.dot_general(pp.astype(BF), vp, NN, preferred_element_type=F32)) / l
                rs, sl = slice(b * DIL_BLOCK, (b + 1) * DIL_BLOCK), slice(hh * HEAD, (hh + 1) * HEAD)
                o_ref[rs, sl] = o
                lse_ref[rs, sl] = jnp.broadcast_to(m + jnp.log(l), (DIL_BLOCK, HEAD))

    ospec = pl.BlockSpec((rows, DIL_O), lambda r, n: (n, r))
    o, lse = pl.pallas_call(
        body, name=f"dil_fwd{g}", grid=(dil, nb // bb),
        in_specs=[pl.BlockSpec((rows, DIL_W), lambda r, n: (n, r)),
                  pl.BlockSpec((DIL_BLOCK, DIL_W), lambda r, n: (jnp.maximum(n * bb - 1, 0), r))],
        out_specs=[ospec, ospec],
        out_shape=[jax.ShapeDtypeStruct((L, dil * DIL_O), F32), jax.ShapeDtypeStruct((L, dil * DIL_O), F32)],
        compiler_params=_params("parallel", "parallel"))(_dil_view(qkv, dil), _dil_view(qkv, dil))
    return o.reshape(S, DIL_O), lse.reshape(S, DIL_O)


def _dil_combine(os_, lses):
    S = os_[0].shape[0]
    ts = _tile(S, 512)

    def body(o0, o1, o2, l0, l1, l2, out_ref, lse_ref):
        a, b, c = l0[...], l1[...], l2[...]
        m = jnp.maximum(jnp.maximum(a, b), c)
        ea, eb, ec = jnp.exp(a - m), jnp.exp(b - m), jnp.exp(c - m)
        tot = ea + eb + ec
        out_ref[...] = ((ea * o0[...] + eb * o1[...] + ec * o2[...]) / tot).astype(BF)
        lse_ref[...] = m + jnp.log(tot)

    return pl.pallas_call(
        body, name="dil_combine", grid=(S // ts,), in_specs=[_row(ts, DIL_O)] * 6,
        out_specs=[_row(ts, DIL_O), _row(ts, DIL_O)],
        out_shape=[jax.ShapeDtypeStruct((S, DIL_O), BF), jax.ShapeDtypeStruct((S, DIL_O), F32)],
        compiler_params=_params("parallel"))(*os_, *lses)


def _dil_delta(do, out):
    S = do.shape[0]
    ts = _tile(S, 512)

    def body(do_ref, o_ref, d_ref):
        for hh in range(DIL_HPG):
            sl = slice(hh * HEAD, (hh + 1) * HEAD)
            d = jnp.sum(do_ref[:, sl].astype(F32) * o_ref[:, sl].astype(F32), axis=1, keepdims=True)
            d_ref[:, sl] = jnp.broadcast_to(d, (ts, HEAD))

    return pl.pallas_call(
        body, name="dil_delta", grid=(S // ts,), in_specs=[_row(ts, DIL_O)] * 2, out_specs=_row(ts, DIL_O),
        out_shape=jax.ShapeDtypeStruct((S, DIL_O), F32), compiler_params=_params("parallel"))(do, out)


def _dil_bwd(qkv, do, lse, delta, g):
    _, dil = DIL_PATTERNS[g]
    S = qkv.shape[0]
    L = S // dil
    nb = L // DIL_BLOCK
    slopes = _dil_slopes(g)

    def pair(q, k, v, do_h, lse_h, delta_h, slope, dist, ok):
        s = _dil_s(q, k, slope, dist, ok)
        p = jnp.exp(s - lse_h)
        dp = lax.dot_general(do_h, v, NT, preferred_element_type=F32)
        ds = (p * (dp - delta_h) * DIL_SCALE).astype(BF)
        return p.astype(BF), ds

    bb = min(DIL_STEP_BLOCKS, nb)
    rows = bb * DIL_BLOCK
    steps = nb // bb

    def body(cur_ref, prev_ref, next_ref, doc_ref, don_ref, lsec_ref, lsen_ref, dlc_ref, dln_ref, out_ref):
        n = pl.program_id(1)
        ok_c, ok_p0, dist_c, dist_p = _dil_bias(dil)
        for b in range(bb):
            rs = slice(b * DIL_BLOCK, (b + 1) * DIL_BLOCK)
            rp = slice((b - 1) * DIL_BLOCK, b * DIL_BLOCK)
            rn = slice((b + 1) * DIL_BLOCK, (b + 2) * DIL_BLOCK)
            first, last = b == 0, b == bb - 1
            cur = cur_ref[rs, :]
            prev = prev_ref[...] if first else cur_ref[rp, :]
            nxt = next_ref[...] if last else cur_ref[rn, :]
            ok_a = ok_p0 & (n > 0) if first else ok_p0
            ok_n = ok_p0 & (n < steps - 1) if last else ok_p0
            for hh in range(DIL_HPG):
                sl = slice(hh * HEAD, (hh + 1) * HEAD)
                q, kc, vc = _dil_head(cur, hh)
                _, kp, vp = _dil_head(prev, hh)
                qn, _, _ = _dil_head(nxt, hh)
                do_c = doc_ref[rs, sl]
                do_n = don_ref[:, sl] if last else doc_ref[rn, sl]
                lse_c = lsec_ref[rs, sl][:, 0:1]
                lse_n = (lsen_ref[:, sl] if last else lsec_ref[rn, sl])[:, 0:1]
                dl_c = dlc_ref[rs, sl][:, 0:1]
                dl_n = (dln_ref[:, sl] if last else dlc_ref[rn, sl])[:, 0:1]
                _, ds_a = pair(q, kp, vp, do_c, lse_c, dl_c, slopes[hh], dist_p, ok_a)
                p_b, ds_b = pair(q, kc, vc, do_c, lse_c, dl_c, slopes[hh], dist_c, ok_c)
                p_n, ds_n = pair(qn, kc, vc, do_n, lse_n, dl_n, slopes[hh], dist_p, ok_n)
                dq = (lax.dot_general(ds_a, kp, NN, preferred_element_type=F32)
                      + lax.dot_general(ds_b, kc, NN, preferred_element_type=F32))
                dk = (lax.dot_general(ds_b, q, TN, preferred_element_type=F32)
                      + lax.dot_general(ds_n, qn, TN, preferred_element_type=F32))
                dv = (lax.dot_general(p_b, do_c, TN, preferred_element_type=F32)
                      + lax.dot_general(p_n, do_n, TN, preferred_element_type=F32))
                out_ref[rs, sl] = dq.astype(BF)
                out_ref[rs, DIL_O + hh * HEAD:DIL_O + (hh + 1) * HEAD] = dk.astype(BF)
                out_ref[rs, 2 * DIL_O + hh * HEAD:2 * DIL_O + (hh + 1) * HEAD] = dv.astype(BF)

    cur_w = pl.BlockSpec((rows, DIL_W), lambda r, n: (n, r))
    prev_w = pl.BlockSpec((DIL_BLOCK, DIL_W), lambda r, n: (jnp.maximum(n * bb - 1, 0), r))
    next_w = pl.BlockSpec((DIL_BLOCK, DIL_W), lambda r, n: (jnp.minimum((n + 1) * bb, nb - 1), r))
    cur_o = pl.BlockSpec((rows, DIL_O), lambda r, n: (n, r))
    next_o = pl.BlockSpec((DIL_BLOCK, DIL_O), lambda r, n: (jnp.minimum((n + 1) * bb, nb - 1), r))
    qv, dov, lsev, dlv = _dil_view(qkv, dil), _dil_view(do, dil), _dil_view(lse, dil), _dil_view(delta, dil)
    out = pl.pallas_call(
        body, name=f"dil_bwd{g}", grid=(dil, steps),
        in_specs=[cur_w, prev_w, next_w, cur_o, next_o, cur_o, next_o, cur_o, next_o],
        out_specs=cur_w, out_shape=jax.ShapeDtypeStruct((L, dil * DIL_W), BF),
        compiler_params=_params("parallel", "parallel"))(qv, qv, qv, dov, dov, lsev, lsev, dlv, dlv)
    return out.reshape(S, DIL_W)


def _adamw(name, w, g, m, v):
    R, C = w.shape
    tr, tc = _adamw_block(R, C)

    def body(w_ref, g_ref, m_ref, v_ref, go_ref, d_ref, nm_ref, nv_ref):
        gv = g_ref[...]
        go_ref[...] = gv
        nm = ADAM_B1 * m_ref[...] + (1.0 - ADAM_B1) * gv
        nv = ADAM_B2 * v_ref[...] + (1.0 - ADAM_B2) * (gv * gv)
        m_hat = nm / (1.0 - ADAM_B1 ** ADAM_STEP)
        v_hat = nv / (1.0 - ADAM_B2 ** ADAM_STEP)
        d_ref[...] = -ADAM_LR * (m_hat / (jnp.sqrt(v_hat) + ADAM_EPS) + ADAM_WD * w_ref[...])
        nm_ref[...] = nm
        nv_ref[...] = nv

    spec = pl.BlockSpec((tr, tc), lambda i, j: (i, j))
    shp = jax.ShapeDtypeStruct((R, C), F32)
    return pl.pallas_call(
        body, name=name, grid=(R // tr, C // tc), in_specs=[spec] * 4, out_specs=[spec] * 4, out_shape=[shp] * 4,
        compiler_params=_params("parallel", "parallel"))(w, g, m, v)


ADAMW_BLOCK_ELEMS = 640 * 1024


def _adamw_block(R, C):
    if R * C <= ADAMW_BLOCK_ELEMS:
        return R, C
    tr = _tile_rows(R, max(8, ADAMW_BLOCK_ELEMS // C))
    tc = _tile(C, max(LANE, ADAMW_BLOCK_ELEMS // R))
    if tr * C >= R * tc or R * tc > ADAMW_BLOCK_ELEMS:
        return tr, C
    return R, tc


def _tile_rows(n, pref, mult=8):
    t = (pref // mult) * mult
    while t >= mult:
        if n % t == 0:
            return t
        t -= mult
    return n


ANY = pl.BlockSpec(memory_space=pl.ANY)


def _place():
    x, y, c = lax.axis_index("x"), lax.axis_index("y"), lax.axis_index("c")
    chips = [(1 - x, y), (x, 1 - y), (1 - x, 1 - y)]
    chip_idx = [2 * cx + cy for cx, cy in chips]
    return x, y, c, 2 * x + y, chips, chip_idx


def _rcopy(src, dst, ssem, rsem, dev):
    return pltpu.make_async_remote_copy(src_ref=src, dst_ref=dst, send_sem=ssem, recv_sem=rsem,
                                        device_id=dev, device_id_type=MESH)


HBM = pl.BlockSpec(memory_space=pltpu.HBM)
SEM = pl.BlockSpec(memory_space=pltpu.SEMAPHORE)
EFFECT = pltpu.SideEffectType.DATAFLOW_SIDE_EFFECTING


def _split_copies(kind, srcs, lands, ssem, rsem):
    _, _, c, me, chips, chip_idx = _place()
    cps = []
    for i in range(len(srcs)):
        for k in range(3):
            if kind == "gather":
                rows = srcs[i].shape[0]
                if rows == lands[i].shape[1]:
                    src, dst = srcs[i], lands[i].at[me]
                else:
                    src, dst = srcs[i], lands[i].at[me, pl.ds(pl.multiple_of(c * rows, 16), rows)]
            else:
                src, dst = srcs[i].at[chip_idx[k]], lands[i].at[k]
            cps.append(_rcopy(src, dst, ssem.at[3 * i + k], rsem.at[3 * i + k], (*chips[k], c)))
    return cps


def _exchange_start(name, kind, srcs, lands, groups):
    n, ng = len(srcs), len(groups)

    def body(*refs):
        src_refs, land_refs = refs[:n], refs[n:2 * n]
        sems = refs[2 * n:2 * n + 2 * ng]
        token = refs[-1]
        for gi, grp in enumerate(groups):
            cps = _split_copies(kind, [src_refs[i] for i in grp], [land_refs[i] for i in grp], sems[2 * gi], sems[2 * gi + 1])
            for cp in cps:
                cp.start()
        token[...] = jnp.zeros_like(token)

    arrays = list(srcs) + list(lands)
    out_shape = []
    for grp in groups:
        out_shape += [pltpu.SemaphoreType.DMA((3 * len(grp),)), pltpu.SemaphoreType.DMA((3 * len(grp),))]
    out_shape += [pltpu.HBM(a.shape, a.dtype) for a in arrays] + [jax.ShapeDtypeStruct((8, LANE), F32)]
    outs = pl.pallas_call(
        body, name=name, out_shape=out_shape, in_specs=[HBM] * (2 * n),
        out_specs=[SEM] * (2 * ng) + [HBM] * (2 * n) + [pl.BlockSpec(memory_space=pltpu.VMEM)],
        input_output_aliases={i: 2 * ng + i for i in range(2 * n)},
        compiler_params=pltpu.CompilerParams(has_side_effects=EFFECT),
    )(*[pltpu.with_memory_space_constraint(a, pltpu.HBM) for a in arrays])
    sems = [(outs[2 * gi], outs[2 * gi + 1]) for gi in range(ng)]
    thru = outs[2 * ng:2 * ng + 2 * n]
    return sems, thru[:n], thru[n:], outs[-1]


def _exchange_wait(name, kind, srcs, lands, sems, after):
    n = len(srcs)

    def body(*refs):
        cps = _split_copies(kind, refs[:n], refs[n:2 * n], refs[2 * n], refs[2 * n + 1])
        for cp in cps:
            cp.wait_send()
            cp.wait_recv()

    arrays = list(srcs) + list(lands)
    outs = pl.pallas_call(
        body, name=name, out_shape=[pltpu.HBM(a.shape, a.dtype) for a in arrays],
        in_specs=[HBM] * (2 * n) + [SEM, SEM, ANY], out_specs=[HBM] * (2 * n),
        input_output_aliases={i: i for i in range(2 * n)},
        compiler_params=pltpu.CompilerParams(has_side_effects=EFFECT),
    )(*arrays, sems[0], sems[1], after)
    return outs[:n], outs[n:]


EXCHANGE_CHUNK_BYTES = 3 * 512 * 1024


def _half_geometry(R, C, axis):
    Rp, Cp = (R // 2, C) if axis == 0 else (R, C // 2)
    tr = _tile_rows(Rp, max(16, EXCHANGE_CHUNK_BYTES // (2 * Cp)), 16)
    return Rp, Cp, tr, Rp // tr


def _pair_sum(name, g, axis):
    G, R, C = g.shape
    Rp, Cp, tr, nb = _half_geometry(R, C, axis)
    steps = G * nb

    def half_block(s, b, h):
        return (s, h * nb + b, 0) if axis == 0 else (s, b, h)

    def body(c_ref, keep_ref, give_ref, out_ref, land, ssem, rsem, credit):
        x, y, c = lax.axis_index("x"), lax.axis_index("y"), lax.axis_index("c")
        sib = (x, y, 1 - c)
        t = pl.program_id(0) * nb + pl.program_id(1)
        slot = t % 2

        @pl.when(t >= 2)
        def _():
            pl.semaphore_wait(credit, 1)

        cp = _rcopy(give_ref.at[0], land.at[slot], ssem.at[slot], rsem.at[slot], sib)
        cp.start()
        cp.wait_recv()
        out_ref[...] = (keep_ref[...].astype(F32) + land[slot].astype(F32)).astype(BF)

        @pl.when(t + 2 < steps)
        def _():
            pl.semaphore_signal(credit, 1, device_id=sib, device_id_type=MESH)

        cp.wait_send()

    blk = (None, tr, Cp)
    grid_spec = pltpu.PrefetchScalarGridSpec(
        num_scalar_prefetch=1, grid=(G, nb),
        in_specs=[pl.BlockSpec(blk, lambda s, b, c_ref: half_block(s, b, c_ref[0])),
                  pl.BlockSpec((1, tr, Cp), lambda s, b, c_ref: half_block(s, b, 1 - c_ref[0]))],
        out_specs=pl.BlockSpec(blk, lambda s, b, c_ref: (s, b, 0)),
        scratch_shapes=[pltpu.VMEM((2, tr, Cp), BF), pltpu.SemaphoreType.DMA((2,)), pltpu.SemaphoreType.DMA((2,)),
                        pltpu.SemaphoreType.REGULAR])
    c_arr = lax.axis_index("c").astype(jnp.int32).reshape(1)
    return pl.pallas_call(
        body, name=name, grid_spec=grid_spec, out_shape=jax.ShapeDtypeStruct((G, Rp, Cp), BF),
        compiler_params=_params("arbitrary", "arbitrary"))(c_arr, g, g)


def _chip_total_join(name, h, landed, axis):
    G, Rp, Cp = h.shape
    R, C = (2 * Rp, Cp) if axis == 0 else (Rp, 2 * Cp)
    tr = _tile_rows(Rp, max(16, EXCHANGE_CHUNK_BYTES // (4 * Cp)), 16)
    nb = Rp // tr

    def body(me_ref, own_ref, l0_ref, l1_ref, l2_ref, full, stage, ssem, rsem, lsem):
        x, y, c = lax.axis_index("x"), lax.axis_index("y"), lax.axis_index("c")
        sib = (x, y, 1 - c)
        b = pl.program_id(0)

        def place(half, r0, rows):
            if axis == 0:
                return full.at[pl.ds(pl.multiple_of(half * Rp + r0, 8), rows), :]
            return full.at[pl.ds(pl.multiple_of(r0, 8), rows), pl.ds(pl.multiple_of(half * Cp, LANE), Cp)]

        def copies(step):
            s = step % 2
            mine = place(c, step * tr, tr)
            return pltpu.make_async_copy(stage.at[s], mine, lsem.at[s]), _rcopy(stage.at[s], mine, ssem.at[s], rsem, sib)

        @pl.when(b >= 2)
        def _():
            loc, rem = copies(b - 2)
            loc.wait()
            rem.wait_send()

        acc = own_ref[...].astype(F32)
        for r in (l0_ref, l1_ref, l2_ref):
            acc = acc + r[...].astype(F32)
        stage[b % 2] = acc
        loc, rem = copies(b)
        loc.start()
        rem.start()

        @pl.when(b == nb - 1)
        def _():
            for step in range(max(0, nb - 2), nb):
                loc, rem = copies(step)
                loc.wait()
                rem.wait_send()
            theirs = place(1 - c, 0, Rp)
            _rcopy(theirs, theirs, ssem.at[0], rsem, sib).wait_recv()

    blk = (None, tr, Cp)
    grid_spec = pltpu.PrefetchScalarGridSpec(
        num_scalar_prefetch=1, grid=(nb,),
        in_specs=[pl.BlockSpec(blk, lambda b, me_ref: (me_ref[0], b, 0))]
        + [pl.BlockSpec(blk, functools.partial(lambda b, me_ref, k: (k, b, 0), k=k)) for k in range(3)],
        out_specs=ANY,
        scratch_shapes=[pltpu.VMEM((2, tr, Cp), F32), pltpu.SemaphoreType.DMA((2,)), pltpu.SemaphoreType.DMA,
                        pltpu.SemaphoreType.DMA((2,))])
    me = (2 * lax.axis_index("x") + lax.axis_index("y")).astype(jnp.int32).reshape(1)
    return pl.pallas_call(
        body, name=name, grid_spec=grid_spec, out_shape=jax.ShapeDtypeStruct((R, C), F32),
        compiler_params=_params("arbitrary"))(me, h, landed, landed, landed)


def _pair_share(name, land):
    G, R, C = land.shape
    Rh = R // 2
    tr = _tile_rows(Rh, max(16, EXCHANGE_CHUNK_BYTES // (2 * C)), 16)
    chunks = [(k, b) for k in range(3) for b in range(Rh // tr)]

    def body(src, dst, buf, lsem, ssem, rsem):
        x, y, c, _, _, chip_idx = _place()
        sib = (x, y, 1 - c)

        def region(ref, k, half, r0, rows):
            return ref.at[chip_idx[k], pl.ds(pl.multiple_of(half * Rh + r0, 16), rows)]

        def load(t):
            k, b = chunks[t]
            return pltpu.make_async_copy(region(src, k, c, b * tr, tr), buf.at[t % 2], lsem.at[t % 2])

        def send(t):
            k, b = chunks[t]
            return _rcopy(buf.at[t % 2], region(dst, k, c, b * tr, tr), ssem.at[t % 2], rsem.at[k], sib)

        load(0).start()
        for t in range(len(chunks)):
            load(t).wait()
            if t + 1 < len(chunks):
                if t >= 1:
                    send(t - 1).wait_send()
                load(t + 1).start()
            send(t).start()
        for t in range(max(0, len(chunks) - 2), len(chunks)):
            send(t).wait_send()
        for k in range(3):
            theirs = region(dst, k, 1 - c, 0, Rh)
            _rcopy(theirs, theirs, ssem.at[0], rsem.at[k], sib).wait_recv()

    return pl.pallas_call(
        body, name=name, in_specs=[ANY], out_specs=ANY, out_shape=jax.ShapeDtypeStruct(land.shape, land.dtype),
        input_output_aliases={0: 0},
        scratch_shapes=[pltpu.VMEM((2, tr, C), land.dtype), pltpu.SemaphoreType.DMA((2,)), pltpu.SemaphoreType.DMA((2,)),
                        pltpu.SemaphoreType.DMA((3,))],
    )(land)


def _allreduce_small(v):
    R, K = v.shape
    ndev = 8

    def body(v_ref, o_ref, land, ssem, rsem):
        x, y, c = lax.axis_index("x"), lax.axis_index("y"), lax.axis_index("c")
        me = 4 * x + 2 * y + c
        land[me] = v_ref[...]
        cps = []
        for r in range(1, ndev):
            fx, fy, fc = (r >> 2) & 1, (r >> 1) & 1, r & 1
            peer = (x ^ fx, y ^ fy, c ^ fc)
            cp = _rcopy(v_ref, land.at[me], ssem.at[r - 1], rsem.at[r - 1], peer)
            cp.start()
            cps.append((cp, 4 * peer[0] + 2 * peer[1] + peer[2], r))
        for cp, src, r in cps:
            cp.wait_send()
            _rcopy(v_ref, land.at[src], ssem.at[r - 1], rsem.at[r - 1], (x, y, c)).wait_recv()
        acc = land[0]
        for d in range(1, ndev):
            acc = acc + land[d]
        o_ref[...] = acc

    vm = pl.BlockSpec(memory_space=pltpu.VMEM)
    return pl.pallas_call(
        body, name="allreduce_small", in_specs=[vm], out_specs=vm, out_shape=jax.ShapeDtypeStruct((R, K), F32),
        scratch_shapes=[pltpu.VMEM((ndev, R, K), F32), pltpu.SemaphoreType.DMA((ndev - 1,)), pltpu.SemaphoreType.DMA((ndev - 1,))],
    )(v)


IN_SPLITS = (Q_RANK, KV_RANK, QK_ROPE, DIL_HEADS * HEAD, DIL_HEADS * HEAD, DIL_HEADS * HEAD, D_MODEL, D_MODEL)
IN_OFF = tuple(int(v) for v in np.cumsum((0,) + IN_SPLITS))


def _unshard_cols(g):
    G, K, Ns = g.shape
    return g.transpose(1, 0, 2).reshape(K, G * Ns)


def _shard_cols(w):
    K, N = w.shape
    return w.reshape(K, N_CHIPS, N // N_CHIPS).transpose(1, 0, 2)


def _rope_pad(w):
    half = QK_ROPE // 2
    z = jnp.zeros(w.shape[:-1] + (half,), w.dtype)
    return jnp.concatenate([w[..., :half], z, w[..., half:], z], axis=-1)


def _rope_unpad(w):
    half = QK_ROPE // 2
    return jnp.concatenate([w[..., :half], w[..., 2 * half:3 * half]], axis=-1)


def _split_w_in(w_in_g):
    w = _unshard_cols(w_in_g)
    K = w.shape[0]
    p = [w[:, IN_OFF[i]:IN_OFF[i + 1]] for i in range(8)]
    w_lat = jnp.concatenate([p[0], p[1], _rope_pad(p[2]), jnp.zeros((K, LAT_W - _KPE.stop), w.dtype)], axis=1)
    w_dil = [jnp.concatenate([p[3 + t][:, g * DIL_O:(g + 1) * DIL_O] for t in range(3)], axis=1) for g in range(DIL_GROUPS)]
    w_gate = jnp.concatenate([p[6], p[7]], axis=1)
    return w_lat, w_dil, w_gate


def _merge_dw_in(dw_lat, dw_dil, dw_gate):
    parts = [dw_lat[:, _CQ], dw_lat[:, _CKV], _rope_unpad(dw_lat[:, _KPE])]
    for t in range(3):
        parts += [dw_dil[g][:, t * DIL_O:(t + 1) * DIL_O] for g in range(DIL_GROUPS)]
    parts.append(dw_gate)
    return _shard_cols(jnp.concatenate(parts, axis=1))


def _split_w_uq(w_uq_g):
    w = _unshard_cols(w_uq_g)
    K = w.shape[0]
    w = w.reshape(K, MLA_HEADS, QK_NOPE + QK_ROPE)
    return w[:, :, :QK_NOPE].reshape(K, MLA_HEADS * HEAD), _rope_pad(w[:, :, QK_NOPE:]).reshape(K, MLA_HEADS * HEAD)


def _merge_dw_uq(dw_n, dw_p):
    K = dw_n.shape[0]
    w = jnp.concatenate([dw_n.reshape(K, MLA_HEADS, HEAD), _rope_unpad(dw_p.reshape(K, MLA_HEADS, HEAD))], axis=-1)
    return _shard_cols(w.reshape(K, MLA_HEADS * (QK_NOPE + QK_ROPE)))


def _split_w_ukv(w_ukv_g):
    w = _unshard_cols(w_ukv_g)
    K = w.shape[0]
    w = w.reshape(K, MLA_HEADS, 2 * HEAD)
    return w[:, :, :HEAD].reshape(K, MLA_HEADS * HEAD), w[:, :, HEAD:].reshape(K, MLA_HEADS * HEAD)


def _merge_dw_ukv(dw_k, dw_v):
    K = dw_k.shape[0]
    w = jnp.concatenate([dw_k.reshape(K, MLA_HEADS, HEAD), dw_v.reshape(K, MLA_HEADS, HEAD)], axis=-1)
    return _shard_cols(w.reshape(K, MLA_HEADS * 2 * HEAD))


GATHER_GROUPS = (("w_in",), ("w_uq", "w_ukv", "w_o_mla", "w_o_dil", "w_out"), ("w_up", "w_down", "conv_w"))
SHARED_FETCH = ("w_in",)
REDUCE_GROUPS = (("w_down", "w_up"), ("w_out", "w_o_mla", "w_o_dil"), ("w_uq", "w_ukv", "w_in"))


def _local_step(x, tgt, W, fetch, emit):
    S, D = x.shape
    cos, sin_s = _rope_tables(S)
    w_lat, w_dil, w_gate = _split_w_in(fetch(0, x)["w_in"])

    h = _rmsnorm_fwd("attn_norm", x, W["attn_norm_g"])
    lat = _mm_nn("proj_lat", h, w_lat)
    qkv = [_mm_nn(f"proj_dil{g}", h, w_dil[g], o_dtype=BF) for g in range(DIL_GROUPS)]
    gpre = _mm_nn("proj_gate", h, w_gate, o_dtype=BF)
    WB = fetch(1, gpre)
    w_uqn, w_uqp = _split_w_uq(WB["w_uq"])
    w_k, w_v = _split_w_ukv(WB["w_ukv"])
    w_o_mla, w_o_dil = WB["w_o_mla"], WB["w_o_dil"]
    w_out = WB["w_out"].reshape(D, D)
    qn_, kvn, kpe = _mla_prep(lat, W["q_norm_g"], W["kv_norm_g"], cos, sin_s)
    q_nope = _mm_nn("q_nope", qn_, w_uqn, o_dtype=BF)
    q_pe = _rope("q_rope", _mm_nn("q_pe", qn_, w_uqp), cos, sin_s, False)
    k_nope = _mm_nn("k_nope", kvn, w_k, o_dtype=BF)
    v_mla = _mm_nn("v_mla", kvn, w_v, o_dtype=BF)
    attn_a, lse_a = _mla_fwd(q_nope, q_pe, k_nope, kpe, v_mla)
    dil = [_dil_fwd(qkv[g], g) for g in range(DIL_GROUPS)]
    attn_b, lse_b = _dil_combine([o for o, _ in dil], [l for _, l in dil])
    o_a = _mm_nn("o_mla", attn_a, w_o_mla, o_dtype=BF)
    o_b = _mm_nn("o_dil", attn_b, w_o_dil, o_dtype=BF)
    merge = _merge_fwd(gpre, W["b_gate"], o_a, o_b)
    x1 = _mm_nn("out_proj", merge, w_out, add=x)
    WC = fetch(2, merge)
    w_up = WC["w_up"]
    G4, _, C = w_up.shape
    w_down = WC["w_down"].reshape(G4 // 2, C, D)
    conv_w = WC["conv_w"]
    conv_b = W["conv_b"].reshape(G4, 1, C)
    h2 = _rmsnorm_fwd("ffn_norm", x1, W["ffn_norm_g"])
    u_pre = _up_fwd(h2, w_up)
    act, u = _ffn_act(u_pre, conv_w, conv_b)
    x2 = _down_fwd(act, w_down, x1)
    dx2, d_final_g, loss8 = _final_loss(x2, tgt, W["final_norm_g"])

    d_act = _down_dgrad(dx2, w_down)
    dw_down = _down_wgrad(act, dx2)
    du = _ffn_act_bwd(u, d_act)
    du_pre, d_conv_w, d_conv_b = _conv_bwd(du, u_pre, conv_w)
    dh2 = _up_dgrad(du_pre, w_up)
    dw_up = _up_wgrad(h2, du_pre)
    zero = emit(0, {"w_down": dw_down.reshape(N_CHIPS, (G4 // 2) * C // N_CHIPS, D), "w_up": dw_up})
    dx1, d_ffn_g = _rmsnorm_bwd("ffn_norm_bwd", dh2, x1, W["ffn_norm_g"] + zero, dx2)
    d_merge = _mm_nt("out_proj_dgrad", dx1, w_out, o_dtype=BF)
    dw_out = _mm_tn("out_proj_wgrad", merge, dx1)
    d_oa, d_ob, d_gpre, d_b_gate = _merge_bwd(d_merge, gpre, W["b_gate"], o_a, o_b)
    d_attn_a = _mm_nt("o_mla_dgrad", d_oa, w_o_mla, o_dtype=BF)
    dw_o_mla = _mm_tn("o_mla_wgrad", attn_a, d_oa, shards=N_CHIPS)
    d_attn_b = _mm_nt("o_dil_dgrad", d_ob, w_o_dil, o_dtype=BF)
    dw_o_dil = _mm_tn("o_dil_wgrad", attn_b, d_ob, shards=N_CHIPS)
    zero = emit(1, {"w_out": dw_out.reshape(N_CHIPS, D // N_CHIPS, D), "w_o_mla": dw_o_mla, "w_o_dil": dw_o_dil})
    q_norm_g = W["q_norm_g"] + zero
    delta_b = _dil_delta(d_attn_b, attn_b)
    d_qkv = [_dil_bwd(qkv[g], d_attn_b, lse_b, delta_b, g) for g in range(DIL_GROUPS)]
    dq_nope, dq_pe_rot = _mla_bwd_dq(q_nope, q_pe, k_nope, kpe, v_mla, d_attn_a, attn_a, lse_a)
    dk_nope, dv_mla, dkpe_rot = _mla_bwd_dkv(q_nope, q_pe, k_nope, kpe, v_mla, d_attn_a, attn_a, lse_a)
    dq_pe = _rope("q_rope_bwd", dq_pe_rot, cos, sin_s, True)
    d_qn = _mm_nt("q_pe_dgrad", dq_pe, w_uqp, add=_mm_nt("q_nope_dgrad", dq_nope, w_uqn))
    d_kvn = _mm_nt("v_dgrad", dv_mla, w_v, add=_mm_nt("k_nope_dgrad", dk_nope, w_k))
    dw_uq = _merge_dw_uq(_mm_tn("q_nope_wgrad", qn_, dq_nope), _mm_tn("q_pe_wgrad", qn_, dq_pe))
    dw_ukv = _merge_dw_ukv(_mm_tn("k_nope_wgrad", kvn, dk_nope), _mm_tn("v_wgrad", kvn, dv_mla))
    d_lat, d_q_g, d_kv_g = _mla_prep_bwd(lat, q_norm_g, W["kv_norm_g"], cos, sin_s, d_qn, d_kvn, dkpe_rot)
    dw_in = _merge_dw_in(_mm_tn("proj_lat_wgrad", h, d_lat),
                         [_mm_tn(f"proj_dil{g}_wgrad", h, d_qkv[g]) for g in range(DIL_GROUPS)],
                         _mm_tn("proj_gate_wgrad", h, d_gpre))
    zero = emit(2, {"w_uq": dw_uq, "w_ukv": dw_ukv, "w_in": dw_in})
    dh = _mm_nt("proj_lat_dgrad", d_lat, w_lat + zero.astype(BF))
    for g in range(DIL_GROUPS):
        dh = _mm_nt(f"proj_dil{g}_dgrad", d_qkv[g], w_dil[g], add=dh)
    dh = _mm_nt("proj_gate_dgrad", d_gpre, w_gate, add=dh)
    grad_x, d_attn_g = _rmsnorm_bwd("attn_norm_bwd", dh, x, W["attn_norm_g"], dx1)

    small = {"attn_norm_g": d_attn_g, "b_gate": d_b_gate, "q_norm_g": d_q_g, "kv_norm_g": d_kv_g,
             "ffn_norm_g": d_ffn_g, "conv_w": d_conv_w, "conv_b": d_conv_b.reshape(1, G4 * C),
             "final_norm_g": d_final_g}
    return loss8[0, 0], grad_x, small


BIG = ("w_in", "w_uq", "w_ukv", "w_o_mla", "w_o_dil", "w_out", "w_up", "w_down")
SMALL = ("attn_norm_g", "b_gate", "q_norm_g", "kv_norm_g", "ffn_norm_g", "conv_w", "conv_b", "final_norm_g")
WEIGHTS = ("attn_norm_g", "w_in", "b_gate", "q_norm_g", "w_uq", "kv_norm_g", "w_ukv", "w_o_mla", "w_o_dil",
           "w_out", "ffn_norm_g", "w_up", "conv_w", "conv_b", "w_down", "final_norm_g")
SMALL_ROWS = 8
COLUMN_MAJOR = ("w_in", "w_up")
HALF_AXIS = {"w_down": 1}


def _gather_start(shards):
    names = [n for grp in GATHER_GROUPS for n in grp]
    chip = 2 * lax.axis_index("x") + lax.axis_index("y")
    c = lax.axis_index("c")
    srcs, lands = [], []
    for n in names:
        s = shards[n]
        lands.append(lax.dynamic_update_slice(lax.empty((N_CHIPS,) + s.shape, s.dtype), s[None], (chip, 0, 0)))
        if n in SHARED_FETCH:
            s = lax.dynamic_slice_in_dim(s, c * (s.shape[0] // 2), s.shape[0] // 2, 0)
        srcs.append(s)
    groups, at = [], 0
    for grp in GATHER_GROUPS:
        groups.append(list(range(at, at + len(grp))))
        at += len(grp)
    sems, srcs, lands, token = _exchange_start("gather_start", "gather", srcs, lands, groups)

    def fetch(i, after):
        idx = groups[i]
        _, got = _exchange_wait(f"gather_wait{i}", "gather", [srcs[j] for j in idx], [lands[j] for j in idx], sems[i], after)
        return {n: _pair_share(f"pair_share_{n}", g) if n in SHARED_FETCH else g for n, g in zip(GATHER_GROUPS[i], got)}

    return fetch, token[0, 0]


def _reduce_start(i, grads):
    names = REDUCE_GROUPS[i]
    hs = [_pair_sum(f"pair_sum_{n}", grads[n], HALF_AXIS.get(n, 0)) for n in names]
    lands = [lax.empty((3,) + h.shape[1:], h.dtype) for h in hs]
    sems, hs, lands, token = _exchange_start(f"reduce_start{i}", "scatter", hs, lands, [list(range(len(names)))])
    return (sems[0], hs, lands), token[0, 0]


def _reduce_finish(i, pending, after):
    sems, hs, lands = pending
    hs, lands = _exchange_wait(f"reduce_wait{i}", "scatter", hs, lands, sems, after)
    out = {}
    for n, h, landed in zip(REDUCE_GROUPS[i], hs, lands):
        out[n] = _chip_total_join(f"chip_total_{n}", h, landed, HALF_AXIS.get(n, 0))
    return out


def _reduce_small(small):
    flat = [small[n].reshape(-1) for n in SMALL]
    sizes = [f.shape[0] for f in flat]
    total = sum(sizes)
    width = -(-total // (SMALL_ROWS * LANE)) * LANE
    packed = jnp.concatenate(flat + [jnp.zeros((SMALL_ROWS * width - total,), F32)]).reshape(SMALL_ROWS, width)
    red = _allreduce_small(packed).reshape(-1)
    out, off = {}, 0
    for n, s in zip(SMALL, sizes):
        out[n] = red[off:off + s]
        off += s
    return out


def kernel(x, attn_norm_g, w_in, b_gate, q_norm_g, w_uq, kv_norm_g, w_ukv, w_o_mla, w_o_dil, w_out, ffn_norm_g, w_up, conv_w, conv_b, w_down, final_norm_g, loss_target, m_attn_norm_g, m_w_in, m_b_gate, m_q_norm_g, m_w_uq, m_kv_norm_g, m_w_ukv, m_w_o_mla, m_w_o_dil, m_w_out, m_ffn_norm_g, m_w_up, m_conv_w, m_conv_b, m_w_down, m_final_norm_g, v_attn_norm_g, v_w_in, v_b_gate, v_q_norm_g, v_w_uq, v_kv_norm_g, v_w_ukv, v_w_o_mla, v_w_o_dil, v_w_out, v_ffn_norm_g, v_w_up, v_conv_w, v_conv_b, v_w_down, v_final_norm_g):
    given = dict(attn_norm_g=attn_norm_g, w_in=w_in, b_gate=b_gate, q_norm_g=q_norm_g, w_uq=w_uq, kv_norm_g=kv_norm_g,
                 w_ukv=w_ukv, w_o_mla=w_o_mla, w_o_dil=w_o_dil, w_out=w_out, ffn_norm_g=ffn_norm_g, w_up=w_up,
                 conv_w=conv_w, conv_b=conv_b, w_down=w_down, final_norm_g=final_norm_g)
    moments_m = dict(attn_norm_g=m_attn_norm_g, w_in=m_w_in, b_gate=m_b_gate, q_norm_g=m_q_norm_g, w_uq=m_w_uq,
                     kv_norm_g=m_kv_norm_g, w_ukv=m_w_ukv, w_o_mla=m_w_o_mla, w_o_dil=m_w_o_dil, w_out=m_w_out,
                     ffn_norm_g=m_ffn_norm_g, w_up=m_w_up, conv_w=m_conv_w, conv_b=m_conv_b, w_down=m_w_down,
                     final_norm_g=m_final_norm_g)
    moments_v = dict(attn_norm_g=v_attn_norm_g, w_in=v_w_in, b_gate=v_b_gate, q_norm_g=v_q_norm_g, w_uq=v_w_uq,
                     kv_norm_g=v_kv_norm_g, w_ukv=v_w_ukv, w_o_mla=v_w_o_mla, w_o_dil=v_w_o_dil, w_out=v_w_out,
                     ffn_norm_g=v_ffn_norm_g, w_up=v_w_up, conv_w=v_conv_w, conv_b=v_conv_b, w_down=v_w_down,
                     final_norm_g=v_final_norm_g)

    shards = {n: given[n][0].astype(BF) for n in BIG}
    shards["conv_w"] = given["conv_w"][0]
    fetch, zero = _gather_start(shards)
    W = {n: given[n] for n in ("b_gate", "q_norm_g", "kv_norm_g", "ffn_norm_g", "conv_b")}
    W["attn_norm_g"] = given["attn_norm_g"] + zero
    W["final_norm_g"] = given["final_norm_g"].reshape(1, -1)

    pending = {}

    def emit(i, grads):
        pending[i], token = _reduce_start(i, grads)
        return token

    loss_part, grad_x, small = _local_step(x[0], loss_target[0], W, fetch, emit)
    loss = lax.psum(loss_part, ("x", "y", "c"))
    grads, delta, new_m, new_v = {}, {}, {}, {}

    def adamw(n, g):
        shp = given[n].shape
        two_d = (-1, shp[-1]) if len(shp) > 1 else (1, -1)
        view = (lambda a: a.reshape(two_d).T) if n in COLUMN_MAJOR else (lambda a: a.reshape(two_d))
        back = (lambda a: a.T.reshape(shp)) if n in COLUMN_MAJOR else (lambda a: a.reshape(shp))
        go, d, nm, nv = _adamw(f"adamw_{n}", view(given[n]), view(g), view(moments_m[n]), view(moments_v[n]))
        grads[n], delta[n], new_m[n], new_v[n] = back(go), back(d), back(nm), back(nv)

    after = grad_x
    for i in range(len(REDUCE_GROUPS)):
        for n, g in _reduce_finish(i, pending[i], after).items():
            adamw(n, g)
        after = delta[REDUCE_GROUPS[i][-1]]
    g_small = _reduce_small(small)
    chip = 2 * lax.axis_index("x") + lax.axis_index("y")
    for n in SMALL:
        if n == "conv_w":
            full = g_small[n].reshape(N_CHIPS, 3, -1)
            adamw(n, lax.dynamic_index_in_dim(full, chip, 0, keepdims=True))
        else:
            adamw(n, g_small[n])

    return (loss, grad_x[None], *[grads[n] for n in WEIGHTS], *[delta[n] for n in WEIGHTS],
            *[new_m[n] for n in WEIGHTS], *[new_v[n] for n in WEIGHTS])
```

```python
import functools
import math

import numpy as np
import jax
import jax.numpy as jnp
from jax import lax
from jax.experimental import pallas as pl
from jax.experimental.pallas import tpu as pltpu

F32 = jnp.float32
BF = jnp.bfloat16
MESH = pl.DeviceIdType.MESH

D_MODEL = 2048
MLA_HEADS = 8
QK_NOPE = 128
QK_ROPE = 64
Q_RANK = 512
KV_RANK = 256
ROPE_THETA = 10000.0
DIL_PATTERNS = ((128, 1), (512, 4), (2048, 16))
DIL_GROUPS = 3
DIL_HPG = 4
DIL_HEADS = 12
HEAD = 128
DIL_BLOCK = 128
ALIBI_MAX_BIAS = 8.0
NORM_EPS = 1e-6
N_CHIPS = 4
ADAM_LR = 0.001
ADAM_B1 = 0.9
ADAM_B2 = 0.999
ADAM_EPS = 1e-08
ADAM_WD = 0.01
ADAM_STEP = 10

LANE = 128
VMEM_LIMIT = 56 * 1024 * 1024
MLA_SCALE = (QK_NOPE + QK_ROPE) ** -0.5
DIL_SCALE = HEAD ** -0.5


def _params(*sem):
    return pltpu.CompilerParams(dimension_semantics=sem, vmem_limit_bytes=VMEM_LIMIT)


def _tile(n, pref):
    t = (pref // LANE) * LANE
    while t >= LANE:
        if n % t == 0:
            return t
        t -= LANE
    return n


NN = (((1,), (0,)), ((), ()))
NT = (((1,), (1,)), ((), ()))
TN = (((0,), (0,)), ((), ()))


def _mm_call(name, a, b, add, *, grid, a_spec, b_spec, add_spec, o_spec, o_shape, o_dtype, acc_shape, dims, nk):
    nax = len(grid)

    def body(*refs):
        if add is None:
            a_ref, b_ref, o_ref = refs[:3]
            c_ref = None
            scr = refs[3:]
        else:
            a_ref, b_ref, c_ref, o_ref = refs[:4]
            scr = refs[4:]
        prod = lax.dot_general(a_ref[...].astype(BF), b_ref[...].astype(BF), dims, preferred_element_type=F32)
        if nk == 1:
            if c_ref is not None:
                prod = prod + c_ref[...]
            o_ref[...] = prod.astype(o_ref.dtype)
        else:
            acc = scr[0]
            k = pl.program_id(nax - 1)

            @pl.when(k == 0)
            def _():
                if c_ref is not None:
                    acc[...] = prod + c_ref[...]
                else:
                    acc[...] = prod

            @pl.when(k > 0)
            def _():
                acc[...] += prod

            @pl.when(k == nk - 1)
            def _():
                o_ref[...] = acc[...].astype(o_ref.dtype)

    ins = [a, b] + ([] if add is None else [add])
    specs = [a_spec, b_spec] + ([] if add is None else [add_spec])
    sem = ("parallel",) * (nax - 1) + ("arbitrary",)
    return pl.pallas_call(
        body, name=name, grid=grid, in_specs=specs, out_specs=o_spec,
        out_shape=jax.ShapeDtypeStruct(o_shape, o_dtype),
        scratch_shapes=[] if nk == 1 else [pltpu.VMEM(acc_shape, F32)],
        compiler_params=_params(*sem),
    )(*ins)


def _mm_nn(name, a, b, *, add=None, o_dtype=F32):
    M, K = a.shape
    sharded = b.ndim == 3
    Ns = b.shape[-1]
    N = Ns * (b.shape[0] if sharded else 1)
    tm, tn, tk = _tile(M, 1024), _tile(Ns, 1024), _tile(K, 2048)
    per = Ns // tn
    nk = K // tk
    if sharded:
        b_spec = pl.BlockSpec((None, tk, tn), lambda i, j, k: (j // per, k, j % per))
    else:
        b_spec = pl.BlockSpec((tk, tn), lambda i, j, k: (k, j))
    return _mm_call(
        name, a, b, add, grid=(M // tm, N // tn, nk),
        a_spec=pl.BlockSpec((tm, tk), lambda i, j, k: (i, k)), b_spec=b_spec,
        add_spec=pl.BlockSpec((tm, tn), lambda i, j, k: (i, j)),
        o_spec=pl.BlockSpec((tm, tn), lambda i, j, k: (i, j)),
        o_shape=(M, N), o_dtype=o_dtype, acc_shape=(tm, tn), dims=NN, nk=nk)


def _mm_nt(name, a, b, *, add=None, o_dtype=F32):
    M, K = a.shape
    sharded = b.ndim == 3
    N, Ks = b.shape[-2], b.shape[-1]
    tm, tn, tk = _tile(M, 1024), _tile(N, 1024), _tile(Ks, 2048)
    per = Ks // tk
    nk = K // tk
    if sharded:
        b_spec = pl.BlockSpec((None, tn, tk), lambda i, j, k: (k // per, j, k % per))
    else:
        b_spec = pl.BlockSpec((tn, tk), lambda i, j, k: (j, k))
    return _mm_call(
        name, a, b, add, grid=(M // tm, N // tn, nk),
        a_spec=pl.BlockSpec((tm, tk), lambda i, j, k: (i, k)), b_spec=b_spec,
        add_spec=pl.BlockSpec((tm, tn), lambda i, j, k: (i, j)),
        o_spec=pl.BlockSpec((tm, tn), lambda i, j, k: (i, j)),
        o_shape=(M, N), o_dtype=o_dtype, acc_shape=(tm, tn), dims=NT, nk=nk)


def _mm_tn(name, a, b, *, shards=1, o_dtype=BF):
    S, M = a.shape
    N = b.shape[1]
    Ns = N // shards
    tm, tn, tk = _tile(M, 1024), _tile(Ns, 1024), _tile(S, 2048)
    per = Ns // tn
    nk = S // tk
    if shards > 1:
        o_spec = pl.BlockSpec((None, tm, tn), lambda i, j, k: (j // per, i, j % per))
        o_shape = (shards, M, Ns)
    else:
        o_spec = pl.BlockSpec((tm, tn), lambda i, j, k: (i, j))
        o_shape = (M, N)
    return _mm_call(
        name, a, b, None, grid=(M // tm, N // tn, nk),
        a_spec=pl.BlockSpec((tk, tm), lambda i, j, k: (k, i)),
        b_spec=pl.BlockSpec((tk, tn), lambda i, j, k: (k, j)),
        add_spec=None, o_spec=o_spec, o_shape=o_shape, o_dtype=o_dtype, acc_shape=(tm, tn), dims=TN, nk=nk)


def _up_fwd(h2, w_up):
    S, D = h2.shape
    G, _, C = w_up.shape
    tm = _tile(S, 512)
    return _mm_call(
        "up_fwd", h2, w_up, None, grid=(G, S // tm, 1),
        a_spec=pl.BlockSpec((tm, D), lambda g, i, k: (i, 0)),
        b_spec=pl.BlockSpec((None, D, C), lambda g, i, k: (g, 0, 0)),
        add_spec=None, o_spec=pl.BlockSpec((None, tm, C), lambda g, i, k: (g, i, 0)),
        o_shape=(G, S, C), o_dtype=BF, acc_shape=None, dims=NN, nk=1)


def _up_dgrad(du_pre, w_up):
    G, S, C = du_pre.shape
    D = w_up.shape[1]
    tm, tn = _tile(S, 1024), _tile(D, 1024)
    return _mm_call(
        "up_dgrad", du_pre, w_up, None, grid=(S // tm, D // tn, G),
        a_spec=pl.BlockSpec((None, tm, C), lambda i, j, g: (g, i, 0)),
        b_spec=pl.BlockSpec((None, tn, C), lambda i, j, g: (g, j, 0)),
        add_spec=None, o_spec=pl.BlockSpec((tm, tn), lambda i, j, g: (i, j)),
        o_shape=(S, D), o_dtype=F32, acc_shape=(tm, tn), dims=NT, nk=G)


def _up_wgrad(h2, du_pre):
    G, S, C = du_pre.shape
    D = h2.shape[1]
    tm, tk = _tile(D, 512), _tile(S, 2048)
    return _mm_call(
        "up_wgrad", h2, du_pre, None, grid=(G, D // tm, S // tk),
        a_spec=pl.BlockSpec((tk, tm), lambda g, i, k: (k, i)),
        b_spec=pl.BlockSpec((None, tk, C), lambda g, i, k: (g, k, 0)),
        add_spec=None, o_spec=pl.BlockSpec((None, tm, C), lambda g, i, k: (g, i, 0)),
        o_shape=(G, D, C), o_dtype=BF, acc_shape=(tm, C), dims=TN, nk=S // tk)


def _down_fwd(act, w_down, x1):
    G, S, C = act.shape
    D = w_down.shape[2]
    tm, tn = _tile(S, 1024), _tile(D, 1024)
    return _mm_call(
        "down_fwd", act, w_down, x1, grid=(S // tm, D // tn, G),
        a_spec=pl.BlockSpec((None, tm, C), lambda i, j, g: (g, i, 0)),
        b_spec=pl.BlockSpec((None, C, tn), lambda i, j, g: (g, 0, j)),
        add_spec=pl.BlockSpec((tm, tn), lambda i, j, g: (i, j)),
        o_spec=pl.BlockSpec((tm, tn), lambda i, j, g: (i, j)),
        o_shape=(S, D), o_dtype=F32, acc_shape=(tm, tn), dims=NN, nk=G)


def _down_dgrad(dx2, w_down):
    S, D = dx2.shape
    G, C, _ = w_down.shape
    tm = _tile(S, 512)
    return _mm_call(
        "down_dgrad", dx2, w_down, None, grid=(G, S // tm, 1),
        a_spec=pl.BlockSpec((tm, D), lambda g, i, k: (i, 0)),
        b_spec=pl.BlockSpec((None, C, D), lambda g, i, k: (g, 0, 0)),
        add_spec=None, o_spec=pl.BlockSpec((None, tm, C), lambda g, i, k: (g, i, 0)),
        o_shape=(G, S, C), o_dtype=BF, acc_shape=None, dims=NT, nk=1)


def _down_wgrad(act, dx2):
    G, S, C = act.shape
    D = dx2.shape[1]
    tn, tk = _tile(D, 512), _tile(S, 1024)
    return _mm_call(
        "down_wgrad", act, dx2, None, grid=(G, D // tn, S // tk),
        a_spec=pl.BlockSpec((None, tk, C), lambda g, j, k: (g, k, 0)),
        b_spec=pl.BlockSpec((tk, tn), lambda g, j, k: (k, j)),
        add_spec=None, o_spec=pl.BlockSpec((None, C, tn), lambda g, j, k: (g, 0, j)),
        o_shape=(G, C, D), o_dtype=BF, acc_shape=(C, tn), dims=TN, nk=S // tk)


def _row(ts, c):
    return pl.BlockSpec((ts, c), lambda i: (i, 0))


def _bcast(r, c):
    return pl.BlockSpec((r, c), lambda i: (0, 0))


def _accumulate(i, ref, val):
    @pl.when(i == 0)
    def _():
        ref[...] = val

    @pl.when(i > 0)
    def _():
        ref[...] += val


def _rstd(xv):
    return lax.rsqrt(jnp.mean(xv * xv, axis=-1, keepdims=True) + NORM_EPS)


def _rmsnorm_fwd(name, x, g):
    S, D = x.shape
    ts = _tile(S, 512)

    def body(x_ref, g_ref, o_ref):
        xv = x_ref[...]
        o_ref[...] = (xv * _rstd(xv) * g_ref[...]).astype(o_ref.dtype)

    return pl.pallas_call(
        body, name=name, grid=(S // ts,), in_specs=[_row(ts, D), _bcast(1, D)], out_specs=_row(ts, D),
        out_shape=jax.ShapeDtypeStruct((S, D), BF), compiler_params=_params("parallel"))(x, g)


def _norm_bwd_rows(dy, xv, g):
    r = _rstd(xv)
    xh = xv * r
    dxh = dy * g
    dx = r * (dxh - xh * jnp.mean(dxh * xh, axis=-1, keepdims=True))
    return dx, jnp.sum(dy * xh, axis=0, keepdims=True)


def _rmsnorm_bwd(name, dy, x, g, res):
    S, D = x.shape
    ts = _tile(S, 512)

    def body(dy_ref, x_ref, g_ref, res_ref, dx_ref, dg_ref):
        dx, dg = _norm_bwd_rows(dy_ref[...], x_ref[...], g_ref[...])
        dx_ref[...] = dx + res_ref[...]
        _accumulate(pl.program_id(0), dg_ref, dg)

    return pl.pallas_call(
        body, name=name, grid=(S // ts,),
        in_specs=[_row(ts, D), _row(ts, D), _bcast(1, D), _row(ts, D)],
        out_specs=[_row(ts, D), _bcast(1, D)],
        out_shape=[jax.ShapeDtypeStruct((S, D), F32), jax.ShapeDtypeStruct((1, D), F32)],
        compiler_params=_params("arbitrary"))(dy, x, g, res)


def _rope_tables(S):
    half = QK_ROPE // 2
    pos = jnp.arange(S, dtype=F32)
    inv_freq = ROPE_THETA ** (-jnp.arange(0, QK_ROPE, 2, dtype=F32) / QK_ROPE)
    ang = pos[:, None] * inv_freq[None, :]
    cos, sin = jnp.cos(ang), jnp.sin(ang)
    z = jnp.zeros((S, half), F32)
    return jnp.concatenate([cos, z, cos, z], axis=1), jnp.concatenate([-sin, z, sin, z], axis=1)


def _rope_lanes(x, cos, sin_signed, inverse):
    if inverse:
        return x * cos + pltpu.roll(x * sin_signed, LANE // 2, 1)
    return x * cos + pltpu.roll(x, LANE // 2, 1) * sin_signed


def _rope(name, x, cos, sin_signed, inverse):
    S, W = x.shape
    ts = _tile(S, 512)

    def body(x_ref, c_ref, s_ref, o_ref):
        c, s = c_ref[...], s_ref[...]
        for h in range(W // LANE):
            sl = slice(h * LANE, (h + 1) * LANE)
            o_ref[:, sl] = _rope_lanes(x_ref[:, sl], c, s, inverse).astype(o_ref.dtype)

    return pl.pallas_call(
        body, name=name, grid=(S // ts,), in_specs=[_row(ts, W), _row(ts, LANE), _row(ts, LANE)],
        out_specs=_row(ts, W), out_shape=jax.ShapeDtypeStruct((S, W), BF),
        compiler_params=_params("parallel"))(x, cos, sin_signed)


LAT_W = 1024
_CQ = slice(0, Q_RANK)
_CKV = slice(Q_RANK, Q_RANK + KV_RANK)
_KPE = slice(Q_RANK + KV_RANK, Q_RANK + KV_RANK + LANE)


def _mla_prep(lat, qg, kvg, cos, sin_signed):
    S = lat.shape[0]
    ts = _tile(S, 512)

    def body(lat_ref, qg_ref, kvg_ref, c_ref, s_ref, qn_ref, kvn_ref, kpe_ref):
        cq = lat_ref[:, _CQ]
        qn_ref[...] = (cq * _rstd(cq) * qg_ref[...]).astype(BF)
        ckv = lat_ref[:, _CKV]
        kvn_ref[...] = (ckv * _rstd(ckv) * kvg_ref[...]).astype(BF)
        kpe_ref[...] = _rope_lanes(lat_ref[:, _KPE], c_ref[...], s_ref[...], False).astype(BF)

    return pl.pallas_call(
        body, name="mla_prep", grid=(S // ts,),
        in_specs=[_row(ts, LAT_W), _bcast(1, Q_RANK), _bcast(1, KV_RANK), _row(ts, LANE), _row(ts, LANE)],
        out_specs=[_row(ts, Q_RANK), _row(ts, KV_RANK), _row(ts, LANE)],
        out_shape=[jax.ShapeDtypeStruct((S, Q_RANK), BF), jax.ShapeDtypeStruct((S, KV_RANK), BF),
                   jax.ShapeDtypeStruct((S, LANE), BF)],
        compiler_params=_params("parallel"))(lat, qg, kvg, cos, sin_signed)


def _mla_prep_bwd(lat, qg, kvg, cos, sin_signed, d_qn, d_kvn, d_kpe):
    S = lat.shape[0]
    ts = _tile(S, 512)

    def body(lat_ref, qg_ref, kvg_ref, c_ref, s_ref, dqn_ref, dkvn_ref, dkpe_ref, dlat_ref, dqg_ref, dkvg_ref):
        i = pl.program_id(0)
        dcq, dqg = _norm_bwd_rows(dqn_ref[...], lat_ref[:, _CQ], qg_ref[...])
        dckv, dkvg = _norm_bwd_rows(dkvn_ref[...], lat_ref[:, _CKV], kvg_ref[...])
        dlat_ref[:, _CQ] = dcq.astype(BF)
        dlat_ref[:, _CKV] = dckv.astype(BF)
        dkpe = dkpe_ref[0]
        for g in range(1, d_kpe.shape[0]):
            dkpe = dkpe + dkpe_ref[g]
        dlat_ref[:, _KPE] = _rope_lanes(dkpe, c_ref[...], s_ref[...], True).astype(BF)
        dlat_ref[:, _KPE.stop:] = jnp.zeros((ts, LAT_W - _KPE.stop), BF)
        _accumulate(i, dqg_ref, dqg)
        _accumulate(i, dkvg_ref, dkvg)

    return pl.pallas_call(
        body, name="mla_prep_bwd", grid=(S // ts,),
        in_specs=[_row(ts, LAT_W), _bcast(1, Q_RANK), _bcast(1, KV_RANK), _row(ts, LANE), _row(ts, LANE),
                  _row(ts, Q_RANK), _row(ts, KV_RANK), pl.BlockSpec((d_kpe.shape[0], ts, LANE), lambda i: (0, i, 0))],
        out_specs=[_row(ts, LAT_W), _bcast(1, Q_RANK), _bcast(1, KV_RANK)],
        out_shape=[jax.ShapeDtypeStruct((S, LAT_W), BF), jax.ShapeDtypeStruct((1, Q_RANK), F32),
                   jax.ShapeDtypeStruct((1, KV_RANK), F32)],
        compiler_params=_params("arbitrary"))(lat, qg, kvg, cos, sin_signed, d_qn, d_kvn, d_kpe)


def _sigmoid(z):
    return 1.0 / (1.0 + jnp.exp(-z))


def _merge_fwd(gpre, b_gate, o_a, o_b):
    S, D = o_a.shape
    ts = _tile(S, 256)

    def body(g_ref, b_ref, oa_ref, ob_ref, m_ref):
        ga = _sigmoid(g_ref[:, :D] + b_ref[:, :D])
        gb = _sigmoid(g_ref[:, D:] + b_ref[:, D:])
        m_ref[...] = (ga * oa_ref[...] + gb * ob_ref[...]).astype(BF)

    return pl.pallas_call(
        body, name="merge_fwd", grid=(S // ts,),
        in_specs=[_row(ts, 2 * D), _bcast(1, 2 * D), _row(ts, D), _row(ts, D)], out_specs=_row(ts, D),
        out_shape=jax.ShapeDtypeStruct((S, D), BF), compiler_params=_params("parallel"))(gpre, b_gate, o_a, o_b)


def _merge_bwd(d_merge, gpre, b_gate, o_a, o_b):
    S, D = o_a.shape
    ts = _tile(S, 256)

    def body(dm_ref, g_ref, b_ref, oa_ref, ob_ref, doa_ref, dob_ref, dg_ref, db_ref):
        dm = dm_ref[...]
        ga = _sigmoid(g_ref[:, :D] + b_ref[:, :D])
        gb = _sigmoid(g_ref[:, D:] + b_ref[:, D:])
        doa_ref[...] = (dm * ga).astype(BF)
        dob_ref[...] = (dm * gb).astype(BF)
        dga = dm * oa_ref[...] * ga * (1.0 - ga)
        dgb = dm * ob_ref[...] * gb * (1.0 - gb)
        dg_ref[:, :D] = dga.astype(BF)
        dg_ref[:, D:] = dgb.astype(BF)
        i = pl.program_id(0)
        part = jnp.concatenate([jnp.sum(dga, axis=0, keepdims=True), jnp.sum(dgb, axis=0, keepdims=True)], axis=1)
        _accumulate(i, db_ref, part)

    return pl.pallas_call(
        body, name="merge_bwd", grid=(S // ts,),
        in_specs=[_row(ts, D), _row(ts, 2 * D), _bcast(1, 2 * D), _row(ts, D), _row(ts, D)],
        out_specs=[_row(ts, D), _row(ts, D), _row(ts, 2 * D), _bcast(1, 2 * D)],
        out_shape=[jax.ShapeDtypeStruct((S, D), BF), jax.ShapeDtypeStruct((S, D), BF),
                   jax.ShapeDtypeStruct((S, 2 * D), BF), jax.ShapeDtypeStruct((1, 2 * D), F32)],
        compiler_params=_params("arbitrary"))(d_merge, gpre, b_gate, o_a, o_b)


def _final_loss(x2, tgt, gf):
    S, D = x2.shape
    ts = _tile(S, 512)

    def body(x_ref, t_ref, g_ref, dx_ref, dg_ref, loss_ref):
        i = pl.program_id(0)
        xv = x_ref[...]
        g = g_ref[...]
        y = xv * _rstd(xv) * g
        err = y - t_ref[...]
        dx, dg = _norm_bwd_rows(err * (1.0 / D), xv, g)
        dx_ref[...] = dx
        _accumulate(i, dg_ref, dg)
        part = 0.5 * jnp.sum(jnp.mean(err * err, axis=-1, keepdims=True), axis=0, keepdims=True)
        _accumulate(i, loss_ref, jnp.broadcast_to(part, (8, LANE)))

    return pl.pallas_call(
        body, name="final_loss", grid=(S // ts,),
        in_specs=[_row(ts, D), _row(ts, D), _bcast(1, D)],
        out_specs=[_row(ts, D), _bcast(1, D), _bcast(8, LANE)],
        out_shape=[jax.ShapeDtypeStruct((S, D), F32), jax.ShapeDtypeStruct((1, D), F32),
                   jax.ShapeDtypeStruct((8, LANE), F32)],
        compiler_params=_params("arbitrary"))(x2, tgt, gf)


HALO = 16


SUB = 8


def _shift_down(cur, prev, k, rows):
    out = pltpu.roll(cur, k, 0)
    head = out[:SUB]
    for j in range(k):
        head = jnp.where(rows == j, prev[HALO - k + j:HALO - k + j + 1, :], head)
    return jnp.concatenate([head, out[SUB:]], axis=0)


def _shift_up(cur, nxt, k, rows, ts):
    out = pltpu.roll(cur, ts - k, 0)
    tail = out[ts - SUB:]
    for j in range(k):
        tail = jnp.where(rows == SUB - k + j, nxt[j:j + 1, :], tail)
    return jnp.concatenate([out[:ts - SUB], tail], axis=0)


def _conv_rows(cur, prev, w, b, rows):
    return b + w[0:1, :] * _shift_down(cur, prev, 2, rows) + w[1:2, :] * _shift_down(cur, prev, 1, rows) + w[2:3, :] * cur


def _conv_specs(ts, C, shard_of):
    nh = ts // HALO
    cur = pl.BlockSpec((None, ts, C), lambda g, i: (shard_of(g), i, 0))
    prev = pl.BlockSpec((None, HALO, C), lambda g, i: (shard_of(g), jnp.maximum(i * nh - 1, 0), 0))
    return cur, prev


def _ffn_act(u_pre, conv_w, conv_b):
    G4, S, C = u_pre.shape
    G = G4 // 2
    ts = _tile(S, 256)

    def body(up_ref, upp_ref, gt_ref, gtp_ref, wu_ref, wg_ref, bu_ref, bg_ref, act_ref, u_ref):
        first = pl.program_id(1) == 0
        rows = lax.broadcasted_iota(jnp.int32, (SUB, C), 0)
        pu = jnp.where(first, 0.0, upp_ref[...].astype(F32))
        pg = jnp.where(first, 0.0, gtp_ref[...].astype(F32))
        up = _conv_rows(up_ref[...].astype(F32), pu, wu_ref[...], bu_ref[...], rows)
        gate = _conv_rows(gt_ref[...].astype(F32), pg, wg_ref[...], bg_ref[...], rows)
        act_ref[...] = (gate * _sigmoid(gate) * up).astype(BF)
        u_ref[0] = up.astype(BF)
        u_ref[1] = gate.astype(BF)

    cur_u, prev_u = _conv_specs(ts, C, lambda g: g)
    cur_g, prev_g = _conv_specs(ts, C, lambda g: g + G)
    w_u = pl.BlockSpec((None, 3, C), lambda g, i: (g, 0, 0))
    w_g = pl.BlockSpec((None, 3, C), lambda g, i: (g + G, 0, 0))
    b_u = pl.BlockSpec((None, 1, C), lambda g, i: (g, 0, 0))
    b_g = pl.BlockSpec((None, 1, C), lambda g, i: (g + G, 0, 0))
    pair = pl.BlockSpec((2, None, ts, C), lambda g, i: (0, g, i, 0))
    act, u = pl.pallas_call(
        body, name="ffn_act", grid=(G, S // ts),
        in_specs=[cur_u, prev_u, cur_g, prev_g, w_u, w_g, b_u, b_g],
        out_specs=[pl.BlockSpec((None, ts, C), lambda g, i: (g, i, 0)), pair],
        out_shape=[jax.ShapeDtypeStruct((G, S, C), BF), jax.ShapeDtypeStruct((2, G, S, C), BF)],
        compiler_params=_params("parallel", "parallel"))(u_pre, u_pre, u_pre, u_pre, conv_w, conv_w, conv_b, conv_b)
    return act, u


def _ffn_act_bwd(u, d_act):
    _, G, S, C = u.shape
    ts = _tile(S, 256)

    def body(u_ref, da_ref, du_ref):
        up, gate = u_ref[0].astype(F32), u_ref[1].astype(F32)
        sg = _sigmoid(gate)
        da = da_ref[...].astype(F32)
        du_ref[0] = (da * (gate * sg)).astype(BF)
        du_ref[1] = (da * up * (sg * (1.0 + gate * (1.0 - sg)))).astype(BF)

    pair = pl.BlockSpec((2, None, ts, C), lambda g, i: (0, g, i, 0))
    du = pl.pallas_call(
        body, name="ffn_act_bwd", grid=(G, S // ts),
        in_specs=[pair, pl.BlockSpec((None, ts, C), lambda g, i: (g, i, 0))], out_specs=pair,
        out_shape=jax.ShapeDtypeStruct((2, G, S, C), BF),
        compiler_params=_params("parallel", "parallel"))(u, d_act)
    return du.reshape(2 * G, S, C)


def _conv_bwd(du, u_pre, conv_w):
    G4, S, C = du.shape
    ts = _tile(S, 256)
    nh = ts // HALO
    last_halo = S // HALO - 1

    def body(du_ref, dun_ref, u_ref, up_ref, w_ref, dpre_ref, dw_ref, db_ref):
        i = pl.program_id(1)
        rows = lax.broadcasted_iota(jnp.int32, (SUB, C), 0)
        du_c = du_ref[...].astype(F32)
        nxt = jnp.where(i == pl.num_programs(1) - 1, 0.0, dun_ref[...].astype(F32))
        prev = jnp.where(i == 0, 0.0, up_ref[...].astype(F32))
        w = w_ref[...]
        dpre = w[2:3, :] * du_c + w[1:2, :] * _shift_up(du_c, nxt, 1, rows, ts) + w[0:1, :] * _shift_up(du_c, nxt, 2, rows, ts)
        dpre_ref[...] = dpre.astype(BF)
        u_c = u_ref[...].astype(F32)
        dw = jnp.concatenate([
            jnp.sum(du_c * _shift_down(u_c, prev, 2, rows), axis=0, keepdims=True),
            jnp.sum(du_c * _shift_down(u_c, prev, 1, rows), axis=0, keepdims=True),
            jnp.sum(du_c * u_c, axis=0, keepdims=True)], axis=0)
        _accumulate(i, dw_ref, dw)
        _accumulate(i, db_ref, jnp.sum(du_c, axis=0, keepdims=True))

    cur = pl.BlockSpec((None, ts, C), lambda g, i: (g, i, 0))
    nxt = pl.BlockSpec((None, HALO, C), lambda g, i: (g, jnp.minimum((i + 1) * nh, last_halo), 0))
    prev = pl.BlockSpec((None, HALO, C), lambda g, i: (g, jnp.maximum(i * nh - 1, 0), 0))
    return pl.pallas_call(
        body, name="conv_bwd", grid=(G4, S // ts),
        in_specs=[cur, nxt, cur, prev, pl.BlockSpec((None, 3, C), lambda g, i: (g, 0, 0))],
        out_specs=[cur, pl.BlockSpec((None, 3, C), lambda g, i: (g, 0, 0)), pl.BlockSpec((None, 1, C), lambda g, i: (g, 0, 0))],
        out_shape=[jax.ShapeDtypeStruct((G4, S, C), BF), jax.ShapeDtypeStruct((G4, 3, C), F32),
                   jax.ShapeDtypeStruct((G4, 1, C), F32)],
        compiler_params=_params("parallel", "arbitrary"))(du, du, u_pre, u_pre, conv_w)


MLA_T = 1024
MLA_HB = 4
MLA_HW = MLA_HB * HEAD


def _mla_pairs(n, by_row):
    if by_row:
        pairs = [(i, j) for i in range(n) for j in range(i + 1)]
    else:
        pairs = [(i, j) for j in range(n) for i in range(j, n)]
    return jnp.asarray([p[0] for p in pairs], jnp.int32), jnp.asarray([p[1] for p in pairs], jnp.int32)


def _mla_specs(S):
    q = pl.BlockSpec((MLA_T, MLA_HW), lambda g, t, it, jt: (it[t], g))
    k = pl.BlockSpec((MLA_T, MLA_HW), lambda g, t, it, jt: (jt[t], g))
    kpe = pl.BlockSpec((MLA_T, HEAD), lambda g, t, it, jt: (jt[t], 0))
    lse = pl.BlockSpec((MLA_HB, MLA_T, LANE), lambda g, t, it, jt: (g, it[t], 0))
    return q, k, kpe, lse


def _mla_head(ref, hh):
    return ref[:, hh * HEAD:(hh + 1) * HEAD]


LOG2E = math.log2(math.e)
MLA_EXP2_SCALE = MLA_SCALE * LOG2E


def _mla_scores(qn_ref, qp_ref, kn_ref, kpe, hh, ok):
    q = jnp.concatenate([_mla_head(qn_ref, hh), _mla_head(qp_ref, hh)], axis=1)
    k = jnp.concatenate([_mla_head(kn_ref, hh), kpe], axis=1)
    s = lax.dot_general(q, k, NT, preferred_element_type=F32)
    return q, k, s if ok is None else jnp.where(ok, s, -jnp.inf)


def _mla_diagonal_mask():
    row = lax.broadcasted_iota(jnp.int32, (MLA_T, MLA_T), 0)
    col = lax.broadcasted_iota(jnp.int32, (MLA_T, MLA_T), 1)
    return col <= row


def _mla_step(i, j, step):
    @pl.when(j < i)
    def _():
        step(None)

    @pl.when(j == i)
    def _():
        step(_mla_diagonal_mask())


def _mla_fwd(qn, qp, kn, kpe, v):
    S = qn.shape[0]
    it, jt = _mla_pairs(S // MLA_T, True)

    def body(it_ref, jt_ref, qn_ref, qp_ref, kn_ref, kpe_ref, v_ref, o_ref, lse_ref, m_scr, acc_scr):
        t = pl.program_id(1)
        i, j = it_ref[t], jt_ref[t]

        @pl.when(j == 0)
        def _():
            m_scr[...] = jnp.full(m_scr.shape, -jnp.inf, F32)
            acc_scr[...] = jnp.zeros(acc_scr.shape, F32)

        def step(ok):
            kpe_v = kpe_ref[...]
            ones = jnp.ones((MLA_T, HEAD), BF)
            state = [(m_scr[hh], acc_scr[hh]) for hh in range(MLA_HB)]
            new = []
            for hh in range(MLA_HB):
                m_prev, acc = state[hh]
                _, _, s = _mla_scores(qn_ref, qp_ref, kn_ref, kpe_v, hh, ok)
                m_new = jnp.maximum(m_prev, jnp.max(s, axis=1, keepdims=True))
                p = jnp.exp2((s - m_new) * MLA_EXP2_SCALE).astype(BF)
                v1 = jnp.concatenate([_mla_head(v_ref, hh), ones], axis=1)
                alpha = jnp.exp2((m_prev - m_new) * MLA_EXP2_SCALE)
                new.append((m_new, alpha * acc + lax.dot_general(p, v1, NN, preferred_element_type=F32)))
            for hh in range(MLA_HB):
                m_scr[hh], acc_scr[hh] = new[hh]

        _mla_step(i, j, step)

        @pl.when(j == i)
        def _():
            for hh in range(MLA_HB):
                l = acc_scr[hh, :, HEAD:]
                o_ref[:, hh * HEAD:(hh + 1) * HEAD] = (acc_scr[hh, :, :HEAD] / l).astype(BF)
                lse_ref[hh] = m_scr[hh] * MLA_SCALE + jnp.log(l)

    qspec, kspec, kpespec, lsespec = _mla_specs(S)
    grid_spec = pltpu.PrefetchScalarGridSpec(
        num_scalar_prefetch=2, grid=(MLA_HEADS // MLA_HB, it.shape[0]),
        in_specs=[qspec, qspec, kspec, kpespec, kspec], out_specs=[qspec, lsespec],
        scratch_shapes=[pltpu.VMEM((MLA_HB, MLA_T, 1), F32), pltpu.VMEM((MLA_HB, MLA_T, 2 * HEAD), F32)])
    return pl.pallas_call(
        body, name="mla_fwd", grid_spec=grid_spec,
        out_shape=[jax.ShapeDtypeStruct((S, MLA_HEADS * HEAD), BF), jax.ShapeDtypeStruct((MLA_HEADS, S, LANE), F32)],
        compiler_params=_params("parallel", "arbitrary"))(it, jt, qn, qp, kn, kpe, v)


def _mla_p_ds(qn_ref, qp_ref, kn_ref, kpe, v_ref, do_ref, o_ref, lse_ref, hh, ok):
    q, k, s = _mla_scores(qn_ref, qp_ref, kn_ref, kpe, hh, ok)
    p = jnp.exp2(s * MLA_EXP2_SCALE - lse_ref[hh][:, 0:1] * LOG2E)
    do = _mla_head(do_ref, hh)
    delta = jnp.sum(do.astype(F32) * _mla_head(o_ref, hh).astype(F32), axis=1, keepdims=True)
    dp = lax.dot_general(do, _mla_head(v_ref, hh), NT, preferred_element_type=F32)
    ds = p * (dp - delta) * MLA_SCALE
    return q, k, p, ds, do


def _mla_bwd_dq(qn, qp, kn, kpe, v, do, o, lse):
    S = qn.shape[0]
    it, jt = _mla_pairs(S // MLA_T, True)

    def body(it_ref, jt_ref, qn_ref, qp_ref, kn_ref, kpe_ref, v_ref, do_ref, o_ref, lse_ref, dqn_ref, dqp_ref, acc):
        t = pl.program_id(1)
        i, j = it_ref[t], jt_ref[t]

        @pl.when(j == 0)
        def _():
            acc[...] = jnp.zeros(acc.shape, F32)

        def step(ok):
            kpe_v = kpe_ref[...]
            old = [acc[hh] for hh in range(MLA_HB)]
            for hh in range(MLA_HB):
                _, k, _, ds, _ = _mla_p_ds(qn_ref, qp_ref, kn_ref, kpe_v, v_ref, do_ref, o_ref, lse_ref, hh, ok)
                old[hh] = old[hh] + lax.dot_general(ds.astype(BF), k, NN, preferred_element_type=F32)
            for hh in range(MLA_HB):
                acc[hh] = old[hh]

        _mla_step(i, j, step)

        @pl.when(j == i)
        def _():
            for hh in range(MLA_HB):
                dqn_ref[:, hh * HEAD:(hh + 1) * HEAD] = acc[hh, :, :HEAD].astype(BF)
                dqp_ref[:, hh * HEAD:(hh + 1) * HEAD] = acc[hh, :, HEAD:]

    qspec, kspec, kpespec, lsespec = _mla_specs(S)
    grid_spec = pltpu.PrefetchScalarGridSpec(
        num_scalar_prefetch=2, grid=(MLA_HEADS // MLA_HB, it.shape[0]),
        in_specs=[qspec, qspec, kspec, kpespec, kspec, qspec, qspec, lsespec], out_specs=[qspec, qspec],
        scratch_shapes=[pltpu.VMEM((MLA_HB, MLA_T, 2 * HEAD), F32)])
    return pl.pallas_call(
        body, name="mla_bwd_dq", grid_spec=grid_spec,
        out_shape=[jax.ShapeDtypeStruct((S, MLA_HEADS * HEAD), BF), jax.ShapeDtypeStruct((S, MLA_HEADS * HEAD), F32)],
        compiler_params=_params("parallel", "arbitrary"))(it, jt, qn, qp, kn, kpe, v, do, o, lse)


def _mla_bwd_dkv(qn, qp, kn, kpe, v, do, o, lse):
    S = qn.shape[0]
    nq = S // MLA_T
    it, jt = _mla_pairs(nq, False)

    def body(it_ref, jt_ref, qn_ref, qp_ref, kn_ref, kpe_ref, v_ref, do_ref, o_ref, lse_ref, dkn_ref, dv_ref, dkpe_ref,
             dk_acc, dv_acc):
        t = pl.program_id(1)
        i, j = it_ref[t], jt_ref[t]

        @pl.when(i == j)
        def _():
            dk_acc[...] = jnp.zeros(dk_acc.shape, F32)
            dv_acc[...] = jnp.zeros(dv_acc.shape, F32)

        def step(ok):
            kpe_v = kpe_ref[...]
            for hh in range(MLA_HB):
                q, _, p, ds, do = _mla_p_ds(qn_ref, qp_ref, kn_ref, kpe_v, v_ref, do_ref, o_ref, lse_ref, hh, ok)
                dv_acc[hh] += lax.dot_general(p.astype(BF), do, TN, preferred_element_type=F32)
                dk_acc[hh] += lax.dot_general(ds.astype(BF), q, TN, preferred_element_type=F32)

        _mla_step(i, j, step)

        @pl.when(i == nq - 1)
        def _():
            dkpe = dk_acc[0, :, HEAD:]
            for hh in range(MLA_HB):
                dkn_ref[:, hh * HEAD:(hh + 1) * HEAD] = dk_acc[hh, :, :HEAD].astype(BF)
                dv_ref[:, hh * HEAD:(hh + 1) * HEAD] = dv_acc[hh].astype(BF)
                if hh:
                    dkpe = dkpe + dk_acc[hh, :, HEAD:]
            dkpe_ref[...] = dkpe

    qspec, kspec, kpespec, lsespec = _mla_specs(S)
    dkpespec = pl.BlockSpec((None, MLA_T, HEAD), lambda g, t, it, jt: (g, jt[t], 0))
    grid_spec = pltpu.PrefetchScalarGridSpec(
        num_scalar_prefetch=2, grid=(MLA_HEADS // MLA_HB, it.shape[0]),
        in_specs=[qspec, qspec, kspec, kpespec, kspec, qspec, qspec, lsespec], out_specs=[kspec, kspec, dkpespec],
        scratch_shapes=[pltpu.VMEM((MLA_HB, MLA_T, 2 * HEAD), F32), pltpu.VMEM((MLA_HB, MLA_T, HEAD), F32)])
    return pl.pallas_call(
        body, name="mla_bwd_dkv", grid_spec=grid_spec,
        out_shape=[jax.ShapeDtypeStruct((S, MLA_HEADS * HEAD), BF), jax.ShapeDtypeStruct((S, MLA_HEADS * HEAD), BF),
                   jax.ShapeDtypeStruct((MLA_HEADS // MLA_HB, S, HEAD), F32)],
        compiler_params=_params("parallel", "arbitrary"))(it, jt, qn, qp, kn, kpe, v, do, o, lse)


DIL_W = 3 * DIL_HPG * HEAD
DIL_O = DIL_HPG * HEAD
DIL_STEP_BLOCKS = 4


def _dil_slopes(g):
    return [2.0 ** (-ALIBI_MAX_BIAS * (g * DIL_HPG + hh + 1) / DIL_HEADS) for hh in range(DIL_HPG)]


def _dil_bias(dil):
    p = lax.broadcasted_iota(jnp.int32, (DIL_BLOCK, DIL_BLOCK), 0)
    kk = lax.broadcasted_iota(jnp.int32, (DIL_BLOCK, DIL_BLOCK), 1)
    jc = p - kk
    dist_c = (dil * jc).astype(F32)
    dist_p = (dil * (jc + DIL_BLOCK)).astype(F32)
    return jc >= 0, jc <= 0, dist_c, dist_p


def _dil_head(blk, hh):
    q = blk[:, hh * HEAD:(hh + 1) * HEAD]
    k = blk[:, DIL_O + hh * HEAD:DIL_O + (hh + 1) * HEAD]
    v = blk[:, 2 * DIL_O + hh * HEAD:2 * DIL_O + (hh + 1) * HEAD]
    return q, k, v


def _dil_s(q, k, slope, dist, ok):
    s = lax.dot_general(q, k, NT, preferred_element_type=F32) * DIL_SCALE - slope * dist
    return jnp.where(ok, s, -jnp.inf)


def _dil_view(a, dil):
    S, W = a.shape
    return a.reshape(S // dil, dil * W)


def _dil_fwd(qkv, g):
    _, dil = DIL_PATTERNS[g]
    S = qkv.shape[0]
    L = S // dil
    nb = L // DIL_BLOCK
    slopes = _dil_slopes(g)

    bb = min(DIL_STEP_BLOCKS, nb)
    rows = bb * DIL_BLOCK

    def body(cur_ref, prev_ref, o_ref, lse_ref):
        n = pl.program_id(1)
        ok_c, ok_p, dist_c, dist_p = _dil_bias(dil)
        for b in range(bb):
            cur = cur_ref[b * DIL_BLOCK:(b + 1) * DIL_BLOCK, :]
            prev = prev_ref[...] if b == 0 else cur_ref[(b - 1) * DIL_BLOCK:b * DIL_BLOCK, :]
            ok_b = ok_p & (n > 0) if b == 0 else ok_p
            for hh in range(DIL_HPG):
                q, kc, vc = _dil_head(cur, hh)
                _, kp, vp = _dil_head(prev, hh)
                sc = _dil_s(q, kc, slopes[hh], dist_c, ok_c)
                sp = _dil_s(q, kp, slopes[hh], dist_p, ok_b)
                m = jnp.maximum(jnp.max(sc, axis=1, keepdims=True), jnp.max(sp, axis=1, keepdims=True))
                pc, pp = jnp.exp(sc - m), jnp.exp(sp - m)
                l = jnp.sum(pc, axis=1, keepdims=True) + jnp.sum(pp, axis=1, keepdims=True)
                o = (lax.dot_general(pc.astype(BF), vc, NN, preferred_element_type=F32)
                     + lax.dot_general(pp.astype(BF), vp, NN, preferred_element_type=F32)) / l
                rs, sl = slice(b * DIL_BLOCK, (b + 1) * DIL_BLOCK), slice(hh * HEAD, (hh + 1) * HEAD)
                o_ref[rs, sl] = o
                lse_ref[rs, sl] = jnp.broadcast_to(m + jnp.log(l), (DIL_BLOCK, HEAD))

    ospec = pl.BlockSpec((rows, DIL_O), lambda r, n: (n, r))
    o, lse = pl.pallas_call(
        body, name=f"dil_fwd{g}", grid=(dil, nb // bb),
        in_specs=[pl.BlockSpec((rows, DIL_W), lambda r, n: (n, r)),
                  pl.BlockSpec((DIL_BLOCK, DIL_W), lambda r, n: (jnp.maximum(n * bb - 1, 0), r))],
        out_specs=[ospec, ospec],
        out_shape=[jax.ShapeDtypeStruct((L, dil * DIL_O), F32), jax.ShapeDtypeStruct((L, dil * DIL_O), F32)],
        compiler_params=_params("parallel", "parallel"))(_dil_view(qkv, dil), _dil_view(qkv, dil))
    return o.reshape(S, DIL_O), lse.reshape(S, DIL_O)


def _dil_combine(os_, lses):
    S = os_[0].shape[0]
    ts = _tile(S, 512)

    def body(o0, o1, o2, l0, l1, l2, out_ref, lse_ref):
        a, b, c = l0[...], l1[...], l2[...]
        m = jnp.maximum(jnp.maximum(a, b), c)
        ea, eb, ec = jnp.exp(a - m), jnp.exp(b - m), jnp.exp(c - m)
        tot = ea + eb + ec
        out_ref[...] = ((ea * o0[...] + eb * o1[...] + ec * o2[...]) / tot).astype(BF)
        lse_ref[...] = m + jnp.log(tot)

    return pl.pallas_call(
        body, name="dil_combine", grid=(S // ts,), in_specs=[_row(ts, DIL_O)] * 6,
        out_specs=[_row(ts, DIL_O), _row(ts, DIL_O)],
        out_shape=[jax.ShapeDtypeStruct((S, DIL_O), BF), jax.ShapeDtypeStruct((S, DIL_O), F32)],
        compiler_params=_params("parallel"))(*os_, *lses)


def _dil_delta(do, out):
    S = do.shape[0]
    ts = _tile(S, 512)

    def body(do_ref, o_ref, d_ref):
        for hh in range(DIL_HPG):
            sl = slice(hh * HEAD, (hh + 1) * HEAD)
            d = jnp.sum(do_ref[:, sl].astype(F32) * o_ref[:, sl].astype(F32), axis=1, keepdims=True)
            d_ref[:, sl] = jnp.broadcast_to(d, (ts, HEAD))

    return pl.pallas_call(
        body, name="dil_delta", grid=(S // ts,), in_specs=[_row(ts, DIL_O)] * 2, out_specs=_row(ts, DIL_O),
        out_shape=jax.ShapeDtypeStruct((S, DIL_O), F32), compiler_params=_params("parallel"))(do, out)


def _dil_bwd(qkv, do, lse, delta, g):
    _, dil = DIL_PATTERNS[g]
    S = qkv.shape[0]
    L = S // dil
    nb = L // DIL_BLOCK
    slopes = _dil_slopes(g)

    def pair(q, k, v, do_h, lse_h, delta_h, slope, dist, ok):
        s = _dil_s(q, k, slope, dist, ok)
        p = jnp.exp(s - lse_h)
        dp = lax.dot_general(do_h, v, NT, preferred_element_type=F32)
        ds = (p * (dp - delta_h) * DIL_SCALE).astype(BF)
        return p.astype(BF), ds

    bb = min(DIL_STEP_BLOCKS, nb)
    rows = bb * DIL_BLOCK
    steps = nb // bb

    def body(cur_ref, prev_ref, next_ref, doc_ref, don_ref, lsec_ref, lsen_ref, dlc_ref, dln_ref, out_ref):
        n = pl.program_id(1)
        ok_c, ok_p0, dist_c, dist_p = _dil_bias(dil)
        for b in range(bb):
            rs = slice(b * DIL_BLOCK, (b + 1) * DIL_BLOCK)
            rp = slice((b - 1) * DIL_BLOCK, b * DIL_BLOCK)
            rn = slice((b + 1) * DIL_BLOCK, (b + 2) * DIL_BLOCK)
            first, last = b == 0, b == bb - 1
            cur = cur_ref[rs, :]
            prev = prev_ref[...] if first else cur_ref[rp, :]
            nxt = next_ref[...] if last else cur_ref[rn, :]
            ok_a = ok_p0 & (n > 0) if first else ok_p0
            ok_n = ok_p0 & (n < steps - 1) if last else ok_p0
            for hh in range(DIL_HPG):
                sl = slice(hh * HEAD, (hh + 1) * HEAD)
                q, kc, vc = _dil_head(cur, hh)
                _, kp, vp = _dil_head(prev, hh)
                qn, _, _ = _dil_head(nxt, hh)
                do_c = doc_ref[rs, sl]
                do_n = don_ref[:, sl] if last else doc_ref[rn, sl]
                lse_c = lsec_ref[rs, sl][:, 0:1]
                lse_n = (lsen_ref[:, sl] if last else lsec_ref[rn, sl])[:, 0:1]
                dl_c = dlc_ref[rs, sl][:, 0:1]
                dl_n = (dln_ref[:, sl] if last else dlc_ref[rn, sl])[:, 0:1]
                _, ds_a = pair(q, kp, vp, do_c, lse_c, dl_c, slopes[hh], dist_p, ok_a)
                p_b, ds_b = pair(q, kc, vc, do_c, lse_c, dl_c, slopes[hh], dist_c, ok_c)
                p_n, ds_n = pair(qn, kc, vc, do_n, lse_n, dl_n, slopes[hh], dist_p, ok_n)
                dq = (lax.dot_general(ds_a, kp, NN, preferred_element_type=F32)
                      + lax.dot_general(ds_b, kc, NN, preferred_element_type=F32))
                dk = (lax.dot_general(ds_b, q, TN, preferred_element_type=F32)
                      + lax.dot_general(ds_n, qn, TN, preferred_element_type=F32))
                dv = (lax.dot_general(p_b, do_c, TN, preferred_element_type=F32)
                      + lax.dot_general(p_n, do_n, TN, preferred_element_type=F32))
                out_ref[rs, sl] = dq.astype(BF)
                out_ref[rs, DIL_O + hh * HEAD:DIL_O + (hh + 1) * HEAD] = dk.astype(BF)
                out_ref[rs, 2 * DIL_O + hh * HEAD:2 * DIL_O + (hh + 1) * HEAD] = dv.astype(BF)

    cur_w = pl.BlockSpec((rows, DIL_W), lambda r, n: (n, r))
    prev_w = pl.BlockSpec((DIL_BLOCK, DIL_W), lambda r, n: (jnp.maximum(n * bb - 1, 0), r))
    next_w = pl.BlockSpec((DIL_BLOCK, DIL_W), lambda r, n: (jnp.minimum((n + 1) * bb, nb - 1), r))
    cur_o = pl.BlockSpec((rows, DIL_O), lambda r, n: (n, r))
    next_o = pl.BlockSpec((DIL_BLOCK, DIL_O), lambda r, n: (jnp.minimum((n + 1) * bb, nb - 1), r))
    qv, dov, lsev, dlv = _dil_view(qkv, dil), _dil_view(do, dil), _dil_view(lse, dil), _dil_view(delta, dil)
    out = pl.pallas_call(
        body, name=f"dil_bwd{g}", grid=(dil, steps),
        in_specs=[cur_w, prev_w, next_w, cur_o, next_o, cur_o, next_o, cur_o, next_o],
        out_specs=cur_w, out_shape=jax.ShapeDtypeStruct((L, dil * DIL_W), BF),
        compiler_params=_params("parallel", "parallel"))(qv, qv, qv, dov, dov, lsev, lsev, dlv, dlv)
    return out.reshape(S, DIL_W)


def _adamw(name, w, g, m, v):
    R, C = w.shape
    tr, tc = _adamw_block(R, C)

    def body(w_ref, g_ref, m_ref, v_ref, go_ref, d_ref, nm_ref, nv_ref):
        gv = g_ref[...]
        go_ref[...] = gv
        nm = ADAM_B1 * m_ref[...] + (1.0 - ADAM_B1) * gv
        nv = ADAM_B2 * v_ref[...] + (1.0 - ADAM_B2) * (gv * gv)
        m_hat = nm / (1.0 - ADAM_B1 ** ADAM_STEP)
        v_hat = nv / (1.0 - ADAM_B2 ** ADAM_STEP)
        d_ref[...] = -ADAM_LR * (m_hat / (jnp.sqrt(v_hat) + ADAM_EPS) + ADAM_WD * w_ref[...])
        nm_ref[...] = nm
        nv_ref[...] = nv

    spec = pl.BlockSpec((tr, tc), lambda i, j: (i, j))
    shp = jax.ShapeDtypeStruct((R, C), F32)
    return pl.pallas_call(
        body, name=name, grid=(R // tr, C // tc), in_specs=[spec] * 4, out_specs=[spec] * 4, out_shape=[shp] * 4,
        compiler_params=_params("parallel", "parallel"))(w, g, m, v)


ADAMW_BLOCK_ELEMS = 640 * 1024


def _adamw_block(R, C):
    if R * C <= ADAMW_BLOCK_ELEMS:
        return R, C
    tr = _tile_rows(R, max(8, ADAMW_BLOCK_ELEMS // C))
    tc = _tile(C, max(LANE, ADAMW_BLOCK_ELEMS // R))
    if tr * C >= R * tc or R * tc > ADAMW_BLOCK_ELEMS:
        return tr, C
    return R, tc


def _tile_rows(n, pref, mult=8):
    t = (pref // mult) * mult
    while t >= mult:
        if n % t == 0:
            return t
        t -= mult
    return n


ANY = pl.BlockSpec(memory_space=pl.ANY)


def _place():
    x, y, c = lax.axis_index("x"), lax.axis_index("y"), lax.axis_index("c")
    chips = [(1 - x, y), (x, 1 - y), (1 - x, 1 - y)]
    chip_idx = [2 * cx + cy for cx, cy in chips]
    return x, y, c, 2 * x + y, chips, chip_idx


def _rcopy(src, dst, ssem, rsem, dev):
    return pltpu.make_async_remote_copy(src_ref=src, dst_ref=dst, send_sem=ssem, recv_sem=rsem,
                                        device_id=dev, device_id_type=MESH)


HBM = pl.BlockSpec(memory_space=pltpu.HBM)
SEM = pl.BlockSpec(memory_space=pltpu.SEMAPHORE)
EFFECT = pltpu.SideEffectType.DATAFLOW_SIDE_EFFECTING


def _split_copies(kind, srcs, lands, ssem, rsem):
    _, _, c, me, chips, chip_idx = _place()
    cps = []
    for i in range(len(srcs)):
        for k in range(3):
            if kind == "gather":
                rows = srcs[i].shape[0]
                if rows == lands[i].shape[1]:
                    src, dst = srcs[i], lands[i].at[me]
                else:
                    src, dst = srcs[i], lands[i].at[me, pl.ds(pl.multiple_of(c * rows, 16), rows)]
            else:
                src, dst = srcs[i].at[chip_idx[k]], lands[i].at[k]
            cps.append(_rcopy(src, dst, ssem.at[3 * i + k], rsem.at[3 * i + k], (*chips[k], c)))
    return cps


def _exchange_start(name, kind, srcs, lands, groups):
    n, ng = len(srcs), len(groups)

    def body(*refs):
        src_refs, land_refs = refs[:n], refs[n:2 * n]
        sems = refs[2 * n:2 * n + 2 * ng]
        token = refs[-1]
        for gi, grp in enumerate(groups):
            cps = _split_copies(kind, [src_refs[i] for i in grp], [land_refs[i] for i in grp], sems[2 * gi], sems[2 * gi + 1])
            for cp in cps:
                cp.start()
        token[...] = jnp.zeros_like(token)

    arrays = list(srcs) + list(lands)
    out_shape = []
    for grp in groups:
        out_shape += [pltpu.SemaphoreType.DMA((3 * len(grp),)), pltpu.SemaphoreType.DMA((3 * len(grp),))]
    out_shape += [pltpu.HBM(a.shape, a.dtype) for a in arrays] + [jax.ShapeDtypeStruct((8, LANE), F32)]
    outs = pl.pallas_call(
        body, name=name, out_shape=out_shape, in_specs=[HBM] * (2 * n),
        out_specs=[SEM] * (2 * ng) + [HBM] * (2 * n) + [pl.BlockSpec(memory_space=pltpu.VMEM)],
        input_output_aliases={i: 2 * ng + i for i in range(2 * n)},
        compiler_params=pltpu.CompilerParams(has_side_effects=EFFECT),
    )(*[pltpu.with_memory_space_constraint(a, pltpu.HBM) for a in arrays])
    sems = [(outs[2 * gi], outs[2 * gi + 1]) for gi in range(ng)]
    thru = outs[2 * ng:2 * ng + 2 * n]
    return sems, thru[:n], thru[n:], outs[-1]


def _exchange_wait(name, kind, srcs, lands, sems, after):
    n = len(srcs)

    def body(*refs):
        cps = _split_copies(kind, refs[:n], refs[n:2 * n], refs[2 * n], refs[2 * n + 1])
        for cp in cps:
            cp.wait_send()
            cp.wait_recv()

    arrays = list(srcs) + list(lands)
    outs = pl.pallas_call(
        body, name=name, out_shape=[pltpu.HBM(a.shape, a.dtype) for a in arrays],
        in_specs=[HBM] * (2 * n) + [SEM, SEM, ANY], out_specs=[HBM] * (2 * n),
        input_output_aliases={i: i for i in range(2 * n)},
        compiler_params=pltpu.CompilerParams(has_side_effects=EFFECT),
    )(*arrays, sems[0], sems[1], after)
    return outs[:n], outs[n:]


EXCHANGE_CHUNK_BYTES = 3 * 1024 * 1024


def _half_geometry(R, C, axis):
    Rp, Cp = (R // 2, C) if axis == 0 else (R, C // 2)
    tr = _tile_rows(Rp, max(16, EXCHANGE_CHUNK_BYTES // (2 * Cp)), 16)
    return Rp, Cp, tr, Rp // tr


def _pair_sum(name, g, axis):
    G, R, C = g.shape
    Rp, Cp, tr, nb = _half_geometry(R, C, axis)
    steps = G * nb

    def half_block(s, b, h):
        return (s, h * nb + b, 0) if axis == 0 else (s, b, h)

    def body(c_ref, keep_ref, give_ref, out_ref, land, ssem, rsem, credit):
        x, y, c = lax.axis_index("x"), lax.axis_index("y"), lax.axis_index("c")
        sib = (x, y, 1 - c)
        t = pl.program_id(0) * nb + pl.program_id(1)
        slot = t % 2

        @pl.when(t >= 2)
        def _():
            pl.semaphore_wait(credit, 1)

        cp = _rcopy(give_ref.at[0], land.at[slot], ssem.at[slot], rsem.at[slot], sib)
        cp.start()
        cp.wait_recv()
        out_ref[...] = (keep_ref[...].astype(F32) + land[slot].astype(F32)).astype(BF)

        @pl.when(t + 2 < steps)
        def _():
            pl.semaphore_signal(credit, 1, device_id=sib, device_id_type=MESH)

        cp.wait_send()

    blk = (None, tr, Cp)
    grid_spec = pltpu.PrefetchScalarGridSpec(
        num_scalar_prefetch=1, grid=(G, nb),
        in_specs=[pl.BlockSpec(blk, lambda s, b, c_ref: half_block(s, b, c_ref[0])),
                  pl.BlockSpec((1, tr, Cp), lambda s, b, c_ref: half_block(s, b, 1 - c_ref[0]))],
        out_specs=pl.BlockSpec(blk, lambda s, b, c_ref: (s, b, 0)),
        scratch_shapes=[pltpu.VMEM((2, tr, Cp), BF), pltpu.SemaphoreType.DMA((2,)), pltpu.SemaphoreType.DMA((2,)),
                        pltpu.SemaphoreType.REGULAR])
    c_arr = lax.axis_index("c").astype(jnp.int32).reshape(1)
    return pl.pallas_call(
        body, name=name, grid_spec=grid_spec, out_shape=jax.ShapeDtypeStruct((G, Rp, Cp), BF),
        compiler_params=_params("arbitrary", "arbitrary"))(c_arr, g, g)


def _chip_total_join(name, h, landed, axis):
    G, Rp, Cp = h.shape
    R, C = (2 * Rp, Cp) if axis == 0 else (Rp, 2 * Cp)
    tr = _tile_rows(Rp, max(16, EXCHANGE_CHUNK_BYTES // (4 * Cp)), 16)
    nb = Rp // tr

    def body(me_ref, own_ref, l0_ref, l1_ref, l2_ref, full, stage, ssem, rsem, lsem):
        x, y, c = lax.axis_index("x"), lax.axis_index("y"), lax.axis_index("c")
        sib = (x, y, 1 - c)
        b = pl.program_id(0)

        def place(half, r0, rows):
            if axis == 0:
                return full.at[pl.ds(pl.multiple_of(half * Rp + r0, 8), rows), :]
            return full.at[pl.ds(pl.multiple_of(r0, 8), rows), pl.ds(pl.multiple_of(half * Cp, LANE), Cp)]

        def copies(step):
            s = step % 2
            mine = place(c, step * tr, tr)
            return pltpu.make_async_copy(stage.at[s], mine, lsem.at[s]), _rcopy(stage.at[s], mine, ssem.at[s], rsem, sib)

        @pl.when(b >= 2)
        def _():
            loc, rem = copies(b - 2)
            loc.wait()
            rem.wait_send()

        acc = own_ref[...].astype(F32)
        for r in (l0_ref, l1_ref, l2_ref):
            acc = acc + r[...].astype(F32)
        stage[b % 2] = acc
        loc, rem = copies(b)
        loc.start()
        rem.start()

        @pl.when(b == nb - 1)
        def _():
            for step in range(max(0, nb - 2), nb):
                loc, rem = copies(step)
                loc.wait()
                rem.wait_send()
            theirs = place(1 - c, 0, Rp)
            _rcopy(theirs, theirs, ssem.at[0], rsem, sib).wait_recv()

    blk = (None, tr, Cp)
    grid_spec = pltpu.PrefetchScalarGridSpec(
        num_scalar_prefetch=1, grid=(nb,),
        in_specs=[pl.BlockSpec(blk, lambda b, me_ref: (me_ref[0], b, 0))]
        + [pl.BlockSpec(blk, functools.partial(lambda b, me_ref, k: (k, b, 0), k=k)) for k in range(3)],
        out_specs=ANY,
        scratch_shapes=[pltpu.VMEM((2, tr, Cp), F32), pltpu.SemaphoreType.DMA((2,)), pltpu.SemaphoreType.DMA,
                        pltpu.SemaphoreType.DMA((2,))])
    me = (2 * lax.axis_index("x") + lax.axis_index("y")).astype(jnp.int32).reshape(1)
    return pl.pallas_call(
        body, name=name, grid_spec=grid_spec, out_shape=jax.ShapeDtypeStruct((R, C), F32),
        compiler_params=_params("arbitrary"))(me, h, landed, landed, landed)


def _pair_share(name, land):
    G, R, C = land.shape
    Rh = R // 2
    tr = _tile_rows(Rh, max(16, EXCHANGE_CHUNK_BYTES // (2 * C)), 16)
    chunks = [(k, b) for k in range(3) for b in range(Rh // tr)]

    def body(src, dst, buf, lsem, ssem, rsem):
        x, y, c, _, _, chip_idx = _place()
        sib = (x, y, 1 - c)

        def region(ref, k, half, r0, rows):
            return ref.at[chip_idx[k], pl.ds(pl.multiple_of(half * Rh + r0, 16), rows)]

        def load(t):
            k, b = chunks[t]
            return pltpu.make_async_copy(region(src, k, c, b * tr, tr), buf.at[t % 2], lsem.at[t % 2])

        def send(t):
            k, b = chunks[t]
            return _rcopy(buf.at[t % 2], region(dst, k, c, b * tr, tr), ssem.at[t % 2], rsem.at[k], sib)

        load(0).start()
        for t in range(len(chunks)):
            load(t).wait()
            if t + 1 < len(chunks):
                if t >= 1:
                    send(t - 1).wait_send()
                load(t + 1).start()
            send(t).start()
        for t in range(max(0, len(chunks) - 2), len(chunks)):
            send(t).wait_send()
        for k in range(3):
            theirs = region(dst, k, 1 - c, 0, Rh)
            _rcopy(theirs, theirs, ssem.at[0], rsem.at[k], sib).wait_recv()

    return pl.pallas_call(
        body, name=name, in_specs=[ANY], out_specs=ANY, out_shape=jax.ShapeDtypeStruct(land.shape, land.dtype),
        input_output_aliases={0: 0},
        scratch_shapes=[pltpu.VMEM((2, tr, C), land.dtype), pltpu.SemaphoreType.DMA((2,)), pltpu.SemaphoreType.DMA((2,)),
                        pltpu.SemaphoreType.DMA((3,))],
    )(land)


def _allreduce_small(v):
    R, K = v.shape
    ndev = 8

    def body(v_ref, o_ref, land, ssem, rsem):
        x, y, c = lax.axis_index("x"), lax.axis_index("y"), lax.axis_index("c")
        me = 4 * x + 2 * y + c
        land[me] = v_ref[...]
        cps = []
        for r in range(1, ndev):
            fx, fy, fc = (r >> 2) & 1, (r >> 1) & 1, r & 1
            peer = (x ^ fx, y ^ fy, c ^ fc)
            cp = _rcopy(v_ref, land.at[me], ssem.at[r - 1], rsem.at[r - 1], peer)
            cp.start()
            cps.append((cp, 4 * peer[0] + 2 * peer[1] + peer[2], r))
        for cp, src, r in cps:
            cp.wait_send()
            _rcopy(v_ref, land.at[src], ssem.at[r - 1], rsem.at[r - 1], (x, y, c)).wait_recv()
        acc = land[0]
        for d in range(1, ndev):
            acc = acc + land[d]
        o_ref[...] = acc

    vm = pl.BlockSpec(memory_space=pltpu.VMEM)
    return pl.pallas_call(
        body, name="allreduce_small", in_specs=[vm], out_specs=vm, out_shape=jax.ShapeDtypeStruct((R, K), F32),
        scratch_shapes=[pltpu.VMEM((ndev, R, K), F32), pltpu.SemaphoreType.DMA((ndev - 1,)), pltpu.SemaphoreType.DMA((ndev - 1,))],
    )(v)


IN_SPLITS = (Q_RANK, KV_RANK, QK_ROPE, DIL_HEADS * HEAD, DIL_HEADS * HEAD, DIL_HEADS * HEAD, D_MODEL, D_MODEL)
IN_OFF = tuple(int(v) for v in np.cumsum((0,) + IN_SPLITS))


def _unshard_cols(g):
    G, K, Ns = g.shape
    return g.transpose(1, 0, 2).reshape(K, G * Ns)


def _shard_cols(w):
    K, N = w.shape
    return w.reshape(K, N_CHIPS, N // N_CHIPS).transpose(1, 0, 2)


def _rope_pad(w):
    half = QK_ROPE // 2
    z = jnp.zeros(w.shape[:-1] + (half,), w.dtype)
    return jnp.concatenate([w[..., :half], z, w[..., half:], z], axis=-1)


def _rope_unpad(w):
    half = QK_ROPE // 2
    return jnp.concatenate([w[..., :half], w[..., 2 * half:3 * half]], axis=-1)


def _split_w_in(w_in_g):
    w = _unshard_cols(w_in_g)
    K = w.shape[0]
    p = [w[:, IN_OFF[i]:IN_OFF[i + 1]] for i in range(8)]
    w_lat = jnp.concatenate([p[0], p[1], _rope_pad(p[2]), jnp.zeros((K, LAT_W - _KPE.stop), w.dtype)], axis=1)
    w_dil = [jnp.concatenate([p[3 + t][:, g * DIL_O:(g + 1) * DIL_O] for t in range(3)], axis=1) for g in range(DIL_GROUPS)]
    w_gate = jnp.concatenate([p[6], p[7]], axis=1)
    return w_lat, w_dil, w_gate


def _merge_dw_in(dw_lat, dw_dil, dw_gate):
    parts = [dw_lat[:, _CQ], dw_lat[:, _CKV], _rope_unpad(dw_lat[:, _KPE])]
    for t in range(3):
        parts += [dw_dil[g][:, t * DIL_O:(t + 1) * DIL_O] for g in range(DIL_GROUPS)]
    parts.append(dw_gate)
    return _shard_cols(jnp.concatenate(parts, axis=1))


def _split_w_uq(w_uq_g):
    w = _unshard_cols(w_uq_g)
    K = w.shape[0]
    w = w.reshape(K, MLA_HEADS, QK_NOPE + QK_ROPE)
    return w[:, :, :QK_NOPE].reshape(K, MLA_HEADS * HEAD), _rope_pad(w[:, :, QK_NOPE:]).reshape(K, MLA_HEADS * HEAD)


def _merge_dw_uq(dw_n, dw_p):
    K = dw_n.shape[0]
    w = jnp.concatenate([dw_n.reshape(K, MLA_HEADS, HEAD), _rope_unpad(dw_p.reshape(K, MLA_HEADS, HEAD))], axis=-1)
    return _shard_cols(w.reshape(K, MLA_HEADS * (QK_NOPE + QK_ROPE)))


def _split_w_ukv(w_ukv_g):
    w = _unshard_cols(w_ukv_g)
    K = w.shape[0]
    w = w.reshape(K, MLA_HEADS, 2 * HEAD)
    return w[:, :, :HEAD].reshape(K, MLA_HEADS * HEAD), w[:, :, HEAD:].reshape(K, MLA_HEADS * HEAD)


def _merge_dw_ukv(dw_k, dw_v):
    K = dw_k.shape[0]
    w = jnp.concatenate([dw_k.reshape(K, MLA_HEADS, HEAD), dw_v.reshape(K, MLA_HEADS, HEAD)], axis=-1)
    return _shard_cols(w.reshape(K, MLA_HEADS * 2 * HEAD))


GATHER_GROUPS = (("w_in",), ("w_uq", "w_ukv", "w_o_mla", "w_o_dil", "w_out"), ("w_up", "w_down", "conv_w"))
SHARED_FETCH = ("w_in",)
REDUCE_GROUPS = (("w_down", "w_up"), ("w_out", "w_o_mla", "w_o_dil"), ("w_uq", "w_ukv", "w_in"))


def _local_step(x, tgt, W, fetch, emit):
    S, D = x.shape
    cos, sin_s = _rope_tables(S)
    w_lat, w_dil, w_gate = _split_w_in(fetch(0, x)["w_in"])

    h = _rmsnorm_fwd("attn_norm", x, W["attn_norm_g"])
    lat = _mm_nn("proj_lat", h, w_lat)
    qkv = [_mm_nn(f"proj_dil{g}", h, w_dil[g], o_dtype=BF) for g in range(DIL_GROUPS)]
    gpre = _mm_nn("proj_gate", h, w_gate, o_dtype=BF)
    WB = fetch(1, gpre)
    w_uqn, w_uqp = _split_w_uq(WB["w_uq"])
    w_k, w_v = _split_w_ukv(WB["w_ukv"])
    w_o_mla, w_o_dil = WB["w_o_mla"], WB["w_o_dil"]
    w_out = WB["w_out"].reshape(D, D)
    qn_, kvn, kpe = _mla_prep(lat, W["q_norm_g"], W["kv_norm_g"], cos, sin_s)
    q_nope = _mm_nn("q_nope", qn_, w_uqn, o_dtype=BF)
    q_pe = _rope("q_rope", _mm_nn("q_pe", qn_, w_uqp), cos, sin_s, False)
    k_nope = _mm_nn("k_nope", kvn, w_k, o_dtype=BF)
    v_mla = _mm_nn("v_mla", kvn, w_v, o_dtype=BF)
    attn_a, lse_a = _mla_fwd(q_nope, q_pe, k_nope, kpe, v_mla)
    dil = [_dil_fwd(qkv[g], g) for g in range(DIL_GROUPS)]
    attn_b, lse_b = _dil_combine([o for o, _ in dil], [l for _, l in dil])
    o_a = _mm_nn("o_mla", attn_a, w_o_mla, o_dtype=BF)
    o_b = _mm_nn("o_dil", attn_b, w_o_dil, o_dtype=BF)
    merge = _merge_fwd(gpre, W["b_gate"], o_a, o_b)
    x1 = _mm_nn("out_proj", merge, w_out, add=x)
    WC = fetch(2, merge)
    w_up = WC["w_up"]
    G4, _, C = w_up.shape
    w_down = WC["w_down"].reshape(G4 // 2, C, D)
    conv_w = WC["conv_w"]
    conv_b = W["conv_b"].reshape(G4, 1, C)
    h2 = _rmsnorm_fwd("ffn_norm", x1, W["ffn_norm_g"])
    u_pre = _up_fwd(h2, w_up)
    act, u = _ffn_act(u_pre, conv_w, conv_b)
    x2 = _down_fwd(act, w_down, x1)
    dx2, d_final_g, loss8 = _final_loss(x2, tgt, W["final_norm_g"])

    d_act = _down_dgrad(dx2, w_down)
    dw_down = _down_wgrad(act, dx2)
    du = _ffn_act_bwd(u, d_act)
    du_pre, d_conv_w, d_conv_b = _conv_bwd(du, u_pre, conv_w)
    dh2 = _up_dgrad(du_pre, w_up)
    dw_up = _up_wgrad(h2, du_pre)
    zero = emit(0, {"w_down": dw_down.reshape(N_CHIPS, (G4 // 2) * C // N_CHIPS, D), "w_up": dw_up})
    dx1, d_ffn_g = _rmsnorm_bwd("ffn_norm_bwd", dh2, x1, W["ffn_norm_g"] + zero, dx2)
    d_merge = _mm_nt("out_proj_dgrad", dx1, w_out, o_dtype=BF)
    dw_out = _mm_tn("out_proj_wgrad", merge, dx1)
    d_oa, d_ob, d_gpre, d_b_gate = _merge_bwd(d_merge, gpre, W["b_gate"], o_a, o_b)
    d_attn_a = _mm_nt("o_mla_dgrad", d_oa, w_o_mla, o_dtype=BF)
    dw_o_mla = _mm_tn("o_mla_wgrad", attn_a, d_oa, shards=N_CHIPS)
    d_attn_b = _mm_nt("o_dil_dgrad", d_ob, w_o_dil, o_dtype=BF)
    dw_o_dil = _mm_tn("o_dil_wgrad", attn_b, d_ob, shards=N_CHIPS)
    zero = emit(1, {"w_out": dw_out.reshape(N_CHIPS, D // N_CHIPS, D), "w_o_mla": dw_o_mla, "w_o_dil": dw_o_dil})
    q_norm_g = W["q_norm_g"] + zero
    delta_b = _dil_delta(d_attn_b, attn_b)
    d_qkv = [_dil_bwd(qkv[g], d_attn_b, lse_b, delta_b, g) for g in range(DIL_GROUPS)]
    dq_nope, dq_pe_rot = _mla_bwd_dq(q_nope, q_pe, k_nope, kpe, v_mla, d_attn_a, attn_a, lse_a)
    dk_nope, dv_mla, dkpe_rot = _mla_bwd_dkv(q_nope, q_pe, k_nope, kpe, v_mla, d_attn_a, attn_a, lse_a)
    dq_pe = _rope("q_rope_bwd", dq_pe_rot, cos, sin_s, True)
    d_qn = _mm_nt("q_pe_dgrad", dq_pe, w_uqp, add=_mm_nt("q_nope_dgrad", dq_nope, w_uqn))
    d_kvn = _mm_nt("v_dgrad", dv_mla, w_v, add=_mm_nt("k_nope_dgrad", dk_nope, w_k))
    dw_uq = _merge_dw_uq(_mm_tn("q_nope_wgrad", qn_, dq_nope), _mm_tn("q_pe_wgrad", qn_, dq_pe))
    dw_ukv = _merge_dw_ukv(_mm_tn("k_nope_wgrad", kvn, dk_nope), _mm_tn("v_wgrad", kvn, dv_mla))
    d_lat, d_q_g, d_kv_g = _mla_prep_bwd(lat, q_norm_g, W["kv_norm_g"], cos, sin_s, d_qn, d_kvn, dkpe_rot)
    dw_in = _merge_dw_in(_mm_tn("proj_lat_wgrad", h, d_lat),
                         [_mm_tn(f"proj_dil{g}_wgrad", h, d_qkv[g]) for g in range(DIL_GROUPS)],
                         _mm_tn("proj_gate_wgrad", h, d_gpre))
    zero = emit(2, {"w_uq": dw_uq, "w_ukv": dw_ukv, "w_in": dw_in})
    dh = _mm_nt("proj_lat_dgrad", d_lat, w_lat + zero.astype(BF))
    for g in range(DIL_GROUPS):
        dh = _mm_nt(f"proj_dil{g}_dgrad", d_qkv[g], w_dil[g], add=dh)
    dh = _mm_nt("proj_gate_dgrad", d_gpre, w_gate, add=dh)
    grad_x, d_attn_g = _rmsnorm_bwd("attn_norm_bwd", dh, x, W["attn_norm_g"], dx1)

    small = {"attn_norm_g": d_attn_g, "b_gate": d_b_gate, "q_norm_g": d_q_g, "kv_norm_g": d_kv_g,
             "ffn_norm_g": d_ffn_g, "conv_w": d_conv_w, "conv_b": d_conv_b.reshape(1, G4 * C),
             "final_norm_g": d_final_g}
    return loss8[0, 0], grad_x, small


BIG = ("w_in", "w_uq", "w_ukv", "w_o_mla", "w_o_dil", "w_out", "w_up", "w_down")
SMALL = ("attn_norm_g", "b_gate", "q_norm_g", "kv_norm_g", "ffn_norm_g", "conv_w", "conv_b", "final_norm_g")
WEIGHTS = ("attn_norm_g", "w_in", "b_gate", "q_norm_g", "w_uq", "kv_norm_g", "w_ukv", "w_o_mla", "w_o_dil",
           "w_out", "ffn_norm_g", "w_up", "conv_w", "conv_b", "w_down", "final_norm_g")
SMALL_ROWS = 8
COLUMN_MAJOR = ("w_in", "w_up")
HALF_AXIS = {"w_down": 1}


def _gather_start(shards):
    names = [n for grp in GATHER_GROUPS for n in grp]
    chip = 2 * lax.axis_index("x") + lax.axis_index("y")
    c = lax.axis_index("c")
    srcs, lands = [], []
    for n in names:
        s = shards[n]
        lands.append(lax.dynamic_update_slice(lax.empty((N_CHIPS,) + s.shape, s.dtype), s[None], (chip, 0, 0)))
        if n in SHARED_FETCH:
            s = lax.dynamic_slice_in_dim(s, c * (s.shape[0] // 2), s.shape[0] // 2, 0)
        srcs.append(s)
    groups, at = [], 0
    for grp in GATHER_GROUPS:
        groups.append(list(range(at, at + len(grp))))
        at += len(grp)
    sems, srcs, lands, token = _exchange_start("gather_start", "gather", srcs, lands, groups)

    def fetch(i, after):
        idx = groups[i]
        _, got = _exchange_wait(f"gather_wait{i}", "gather", [srcs[j] for j in idx], [lands[j] for j in idx], sems[i], after)
        return {n: _pair_share(f"pair_share_{n}", g) if n in SHARED_FETCH else g for n, g in zip(GATHER_GROUPS[i], got)}

    return fetch, token[0, 0]


def _reduce_start(i, grads):
    names = REDUCE_GROUPS[i]
    hs = [_pair_sum(f"pair_sum_{n}", grads[n], HALF_AXIS.get(n, 0)) for n in names]
    lands = [lax.empty((3,) + h.shape[1:], h.dtype) for h in hs]
    sems, hs, lands, token = _exchange_start(f"reduce_start{i}", "scatter", hs, lands, [list(range(len(names)))])
    return (sems[0], hs, lands), token[0, 0]


def _reduce_finish(i, pending, after):
    sems, hs, lands = pending
    hs, lands = _exchange_wait(f"reduce_wait{i}", "scatter", hs, lands, sems, after)
    out = {}
    for n, h, landed in zip(REDUCE_GROUPS[i], hs, lands):
        out[n] = _chip_total_join(f"chip_total_{n}", h, landed, HALF_AXIS.get(n, 0))
    return out


def _reduce_small(small):
    flat = [small[n].reshape(-1) for n in SMALL]
    sizes = [f.shape[0] for f in flat]
    total = sum(sizes)
    width = -(-total // (SMALL_ROWS * LANE)) * LANE
    packed = jnp.concatenate(flat + [jnp.zeros((SMALL_ROWS * width - total,), F32)]).reshape(SMALL_ROWS, width)
    red = _allreduce_small(packed).reshape(-1)
    out, off = {}, 0
    for n, s in zip(SMALL, sizes):
        out[n] = red[off:off + s]
        off += s
    return out


def kernel(x, attn_norm_g, w_in, b_gate, q_norm_g, w_uq, kv_norm_g, w_ukv, w_o_mla, w_o_dil, w_out, ffn_norm_g, w_up, conv_w, conv_b, w_down, final_norm_g, loss_target, m_attn_norm_g, m_w_in, m_b_gate, m_q_norm_g, m_w_uq, m_kv_norm_g, m_w_ukv, m_w_o_mla, m_w_o_dil, m_w_out, m_ffn_norm_g, m_w_up, m_conv_w, m_conv_b, m_w_down, m_final_norm_g, v_attn_norm_g, v_w_in, v_b_gate, v_q_norm_g, v_w_uq, v_kv_norm_g, v_w_ukv, v_w_o_mla, v_w_o_dil, v_w_out, v_ffn_norm_g, v_w_up, v_conv_w, v_conv_b, v_w_down, v_final_norm_g):
    given = dict(attn_norm_g=attn_norm_g, w_in=w_in, b_gate=b_gate, q_norm_g=q_norm_g, w_uq=w_uq, kv_norm_g=kv_norm_g,
                 w_ukv=w_ukv, w_o_mla=w_o_mla, w_o_dil=w_o_dil, w_out=w_out, ffn_norm_g=ffn_norm_g, w_up=w_up,
                 conv_w=conv_w, conv_b=conv_b, w_down=w_down, final_norm_g=final_norm_g)
    moments_m = dict(attn_norm_g=m_attn_norm_g, w_in=m_w_in, b_gate=m_b_gate, q_norm_g=m_q_norm_g, w_uq=m_w_uq,
                     kv_norm_g=m_kv_norm_g, w_ukv=m_w_ukv, w_o_mla=m_w_o_mla, w_o_dil=m_w_o_dil, w_out=m_w_out,
                     ffn_norm_g=m_ffn_norm_g, w_up=m_w_up, conv_w=m_conv_w, conv_b=m_conv_b, w_down=m_w_down,
                     final_norm_g=m_final_norm_g)
    moments_v = dict(attn_norm_g=v_attn_norm_g, w_in=v_w_in, b_gate=v_b_gate, q_norm_g=v_q_norm_g, w_uq=v_w_uq,
                     kv_norm_g=v_kv_norm_g, w_ukv=v_w_ukv, w_o_mla=v_w_o_mla, w_o_dil=v_w_o_dil, w_out=v_w_out,
                     ffn_norm_g=v_ffn_norm_g, w_up=v_w_up, conv_w=v_conv_w, conv_b=v_conv_b, w_down=v_w_down,
                     final_norm_g=v_final_norm_g)

    shards = {n: given[n][0].astype(BF) for n in BIG}
    shards["conv_w"] = given["conv_w"][0]
    fetch, zero = _gather_start(shards)
    W = {n: given[n] for n in ("b_gate", "q_norm_g", "kv_norm_g", "ffn_norm_g", "conv_b")}
    W["attn_norm_g"] = given["attn_norm_g"] + zero
    W["final_norm_g"] = given["final_norm_g"].reshape(1, -1)

    pending = {}

    def emit(i, grads):
        pending[i], token = _reduce_start(i, grads)
        return token

    loss_part, grad_x, small = _local_step(x[0], loss_target[0], W, fetch, emit)
    loss = lax.psum(loss_part, ("x", "y", "c"))
    grads, delta, new_m, new_v = {}, {}, {}, {}

    def adamw(n, g):
        shp = given[n].shape
        two_d = (-1, shp[-1]) if len(shp) > 1 else (1, -1)
        view = (lambda a: a.reshape(two_d).T) if n in COLUMN_MAJOR else (lambda a: a.reshape(two_d))
        back = (lambda a: a.T.reshape(shp)) if n in COLUMN_MAJOR else (lambda a: a.reshape(shp))
        go, d, nm, nv = _adamw(f"adamw_{n}", view(given[n]), view(g), view(moments_m[n]), view(moments_v[n]))
        grads[n], delta[n], new_m[n], new_v[n] = back(go), back(d), back(nm), back(nv)

    after = grad_x
    for i in range(len(REDUCE_GROUPS)):
        for n, g in _reduce_finish(i, pending[i], after).items():
            adamw(n, g)
        after = delta[REDUCE_GROUPS[i][-1]]
    g_small = _reduce_small(small)
    chip = 2 * lax.axis_index("x") + lax.axis_index("y")
    for n in SMALL:
        if n == "conv_w":
            full = g_small[n].reshape(N_CHIPS, 3, -1)
            adamw(n, lax.dynamic_index_in_dim(full, chip, 0, keepdims=True))
        else:
            adamw(n, g_small[n])

    return (loss, grad_x[None], *[grads[n] for n in WEIGHTS], *[delta[n] for n in WEIGHTS],
            *[new_m[n] for n in WEIGHTS], *[new_v[n] for n in WEIGHTS])
```

```python
import functools
import math

import numpy as np
import jax
import jax.numpy as jnp
from jax import lax
from jax.experimental import pallas as pl
from jax.experimental.pallas import tpu as pltpu

F32 = jnp.float32
BF = jnp.bfloat16
MESH = pl.DeviceIdType.MESH

D_MODEL = 2048
MLA_HEADS = 8
QK_NOPE = 128
QK_ROPE = 64
Q_RANK = 512
KV_RANK = 256
ROPE_THETA = 10000.0
DIL_PATTERNS = ((128, 1), (512, 4), (2048, 16))
DIL_GROUPS = 3
DIL_HPG = 4
DIL_HEADS = 12
HEAD = 128
DIL_BLOCK = 128
ALIBI_MAX_BIAS = 8.0
NORM_EPS = 1e-6
N_CHIPS = 4
ADAM_LR = 0.001
ADAM_B1 = 0.9
ADAM_B2 = 0.999
ADAM_EPS = 1e-08
ADAM_WD = 0.01
ADAM_STEP = 10

LANE = 128
VMEM_LIMIT = 56 * 1024 * 1024
MLA_SCALE = (QK_NOPE + QK_ROPE) ** -0.5
DIL_SCALE = HEAD ** -0.5


def _params(*sem):
    return pltpu.CompilerParams(dimension_semantics=sem, vmem_limit_bytes=VMEM_LIMIT)


def _tile(n, pref):
    t = (pref // LANE) * LANE
    while t >= LANE:
        if n % t == 0:
            return t
        t -= LANE
    return n


NN = (((1,), (0,)), ((), ()))
NT = (((1,), (1,)), ((), ()))
TN = (((0,), (0,)), ((), ()))


def _mm_call(name, a, b, add, *, grid, a_spec, b_spec, add_spec, o_spec, o_shape, o_dtype, acc_shape, dims, nk):
    nax = len(grid)

    def body(*refs):
        if add is None:
            a_ref, b_ref, o_ref = refs[:3]
            c_ref = None
            scr = refs[3:]
        else:
            a_ref, b_ref, c_ref, o_ref = refs[:4]
            scr = refs[4:]
        prod = lax.dot_general(a_ref[...].astype(BF), b_ref[...].astype(BF), dims, preferred_element_type=F32)
        if nk == 1:
            if c_ref is not None:
                prod = prod + c_ref[...]
            o_ref[...] = prod.astype(o_ref.dtype)
        else:
            acc = scr[0]
            k = pl.program_id(nax - 1)

            @pl.when(k == 0)
            def _():
                if c_ref is not None:
                    acc[...] = prod + c_ref[...]
                else:
                    acc[...] = prod

            @pl.when(k > 0)
            def _():
                acc[...] += prod

            @pl.when(k == nk - 1)
            def _():
                o_ref[...] = acc[...].astype(o_ref.dtype)

    ins = [a, b] + ([] if add is None else [add])
    specs = [a_spec, b_spec] + ([] if add is None else [add_spec])
    sem = ("parallel",) * (nax - 1) + ("arbitrary",)
    return pl.pallas_call(
        body, name=name, grid=grid, in_specs=specs, out_specs=o_spec,
        out_shape=jax.ShapeDtypeStruct(o_shape, o_dtype),
        scratch_shapes=[] if nk == 1 else [pltpu.VMEM(acc_shape, F32)],
        compiler_params=_params(*sem),
    )(*ins)


def _mm_nn(name, a, b, *, add=None, o_dtype=F32):
    M, K = a.shape
    sharded = b.ndim == 3
    Ns = b.shape[-1]
    N = Ns * (b.shape[0] if sharded else 1)
    tm, tn, tk = _tile(M, 1024), _tile(Ns, 1024), _tile(K, 2048)
    per = Ns // tn
    nk = K // tk
    if sharded:
        b_spec = pl.BlockSpec((None, tk, tn), lambda i, j, k: (j // per, k, j % per))
    else:
        b_spec = pl.BlockSpec((tk, tn), lambda i, j, k: (k, j))
    return _mm_call(
        name, a, b, add, grid=(M // tm, N // tn, nk),
        a_spec=pl.BlockSpec((tm, tk), lambda i, j, k: (i, k)), b_spec=b_spec,
        add_spec=pl.BlockSpec((tm, tn), lambda i, j, k: (i, j)),
        o_spec=pl.BlockSpec((tm, tn), lambda i, j, k: (i, j)),
        o_shape=(M, N), o_dtype=o_dtype, acc_shape=(tm, tn), dims=NN, nk=nk)


def _mm_nt(name, a, b, *, add=None, o_dtype=F32):
    M, K = a.shape
    sharded = b.ndim == 3
    N, Ks = b.shape[-2], b.shape[-1]
    tm, tn, tk = _tile(M, 1024), _tile(N, 1024), _tile(Ks, 2048)
    per = Ks // tk
    nk = K // tk
    if sharded:
        b_spec = pl.BlockSpec((None, tn, tk), lambda i, j, k: (k // per, j, k % per))
    else:
        b_spec = pl.BlockSpec((tn, tk), lambda i, j, k: (j, k))
    return _mm_call(
        name, a, b, add, grid=(M // tm, N // tn, nk),
        a_spec=pl.BlockSpec((tm, tk), lambda i, j, k: (i, k)), b_spec=b_spec,
        add_spec=pl.BlockSpec((tm, tn), lambda i, j, k: (i, j)),
        o_spec=pl.BlockSpec((tm, tn), lambda i, j, k: (i, j)),
        o_shape=(M, N), o_dtype=o_dtype, acc_shape=(tm, tn), dims=NT, nk=nk)


def _mm_tn(name, a, b, *, shards=1, o_dtype=BF):
    S, M = a.shape
    N = b.shape[1]
    Ns = N // shards
    tm, tn, tk = _tile(M, 1024), _tile(Ns, 1024), _tile(S, 2048)
    per = Ns // tn
    nk = S // tk
    if shards > 1:
        o_spec = pl.BlockSpec((None, tm, tn), lambda i, j, k: (j // per, i, j % per))
        o_shape = (shards, M, Ns)
    else:
        o_spec = pl.BlockSpec((tm, tn), lambda i, j, k: (i, j))
        o_shape = (M, N)
    return _mm_call(
        name, a, b, None, grid=(M // tm, N // tn, nk),
        a_spec=pl.BlockSpec((tk, tm), lambda i, j, k: (k, i)),
        b_spec=pl.BlockSpec((tk, tn), lambda i, j, k: (k, j)),
        add_spec=None, o_spec=o_spec, o_shape=o_shape, o_dtype=o_dtype, acc_shape=(tm, tn), dims=TN, nk=nk)


def _up_fwd(h2, w_up):
    S, D = h2.shape
    G, _, C = w_up.shape
    tm = _tile(S, 512)
    return _mm_call(
        "up_fwd", h2, w_up, None, grid=(G, S // tm, 1),
        a_spec=pl.BlockSpec((tm, D), lambda g, i, k: (i, 0)),
        b_spec=pl.BlockSpec((None, D, C), lambda g, i, k: (g, 0, 0)),
        add_spec=None, o_spec=pl.BlockSpec((None, tm, C), lambda g, i, k: (g, i, 0)),
        o_shape=(G, S, C), o_dtype=BF, acc_shape=None, dims=NN, nk=1)


def _up_dgrad(du_pre, w_up):
    G, S, C = du_pre.shape
    D = w_up.shape[1]
    tm, tn = _tile(S, 1024), _tile(D, 1024)
    return _mm_call(
        "up_dgrad", du_pre, w_up, None, grid=(S // tm, D // tn, G),
        a_spec=pl.BlockSpec((None, tm, C), lambda i, j, g: (g, i, 0)),
        b_spec=pl.BlockSpec((None, tn, C), lambda i, j, g: (g, j, 0)),
        add_spec=None, o_spec=pl.BlockSpec((tm, tn), lambda i, j, g: (i, j)),
        o_shape=(S, D), o_dtype=F32, acc_shape=(tm, tn), dims=NT, nk=G)


def _up_wgrad(h2, du_pre):
    G, S, C = du_pre.shape
    D = h2.shape[1]
    tm, tk = _tile(D, 512), _tile(S, 2048)
    return _mm_call(
        "up_wgrad", h2, du_pre, None, grid=(G, D // tm, S // tk),
        a_spec=pl.BlockSpec((tk, tm), lambda g, i, k: (k, i)),
        b_spec=pl.BlockSpec((None, tk, C), lambda g, i, k: (g, k, 0)),
        add_spec=None, o_spec=pl.BlockSpec((None, tm, C), lambda g, i, k: (g, i, 0)),
        o_shape=(G, D, C), o_dtype=BF, acc_shape=(tm, C), dims=TN, nk=S // tk)


def _down_fwd(act, w_down, x1):
    G, S, C = act.shape
    D = w_down.shape[2]
    tm, tn = _tile(S, 1024), _tile(D, 1024)
    return _mm_call(
        "down_fwd", act, w_down, x1, grid=(S // tm, D // tn, G),
        a_spec=pl.BlockSpec((None, tm, C), lambda i, j, g: (g, i, 0)),
        b_spec=pl.BlockSpec((None, C, tn), lambda i, j, g: (g, 0, j)),
        add_spec=pl.BlockSpec((tm, tn), lambda i, j, g: (i, j)),
        o_spec=pl.BlockSpec((tm, tn), lambda i, j, g: (i, j)),
        o_shape=(S, D), o_dtype=F32, acc_shape=(tm, tn), dims=NN, nk=G)


def _down_dgrad(dx2, w_down):
    S, D = dx2.shape
    G, C, _ = w_down.shape
    tm = _tile(S, 512)
    return _mm_call(
        "down_dgrad", dx2, w_down, None, grid=(G, S // tm, 1),
        a_spec=pl.BlockSpec((tm, D), lambda g, i, k: (i, 0)),
        b_spec=pl.BlockSpec((None, C, D), lambda g, i, k: (g, 0, 0)),
        add_spec=None, o_spec=pl.BlockSpec((None, tm, C), lambda g, i, k: (g, i, 0)),
        o_shape=(G, S, C), o_dtype=BF, acc_shape=None, dims=NT, nk=1)


def _down_wgrad(act, dx2):
    G, S, C = act.shape
    D = dx2.shape[1]
    tn, tk = _tile(D, 512), _tile(S, 1024)
    return _mm_call(
        "down_wgrad", act, dx2, None, grid=(G, D // tn, S // tk),
        a_spec=pl.BlockSpec((None, tk, C), lambda g, j, k: (g, k, 0)),
        b_spec=pl.BlockSpec((tk, tn), lambda g, j, k: (k, j)),
        add_spec=None, o_spec=pl.BlockSpec((None, C, tn), lambda g, j, k: (g, 0, j)),
        o_shape=(G, C, D), o_dtype=BF, acc_shape=(C, tn), dims=TN, nk=S // tk)


def _row(ts, c):
    return pl.BlockSpec((ts, c), lambda i: (i, 0))


def _bcast(r, c):
    return pl.BlockSpec((r, c), lambda i: (0, 0))


def _accumulate(i, ref, val):
    @pl.when(i == 0)
    def _():
        ref[...] = val

    @pl.when(i > 0)
    def _():
        ref[...] += val


def _rstd(xv):
    return lax.rsqrt(jnp.mean(xv * xv, axis=-1, keepdims=True) + NORM_EPS)


def _rmsnorm_fwd(name, x, g):
    S, D = x.shape
    ts = _tile(S, 512)

    def body(x_ref, g_ref, o_ref):
        xv = x_ref[...]
        o_ref[...] = (xv * _rstd(xv) * g_ref[...]).astype(o_ref.dtype)

    return pl.pallas_call(
        body, name=name, grid=(S // ts,), in_specs=[_row(ts, D), _bcast(1, D)], out_specs=_row(ts, D),
        out_shape=jax.ShapeDtypeStruct((S, D), BF), compiler_params=_params("parallel"))(x, g)


def _norm_bwd_rows(dy, xv, g):
    r = _rstd(xv)
    xh = xv * r
    dxh = dy * g
    dx = r * (dxh - xh * jnp.mean(dxh * xh, axis=-1, keepdims=True))
    return dx, jnp.sum(dy * xh, axis=0, keepdims=True)


def _rmsnorm_bwd(name, dy, x, g, res):
    S, D = x.shape
    ts = _tile(S, 512)

    def body(dy_ref, x_ref, g_ref, res_ref, dx_ref, dg_ref):
        dx, dg = _norm_bwd_rows(dy_ref[...], x_ref[...], g_ref[...])
        dx_ref[...] = dx + res_ref[...]
        _accumulate(pl.program_id(0), dg_ref, dg)

    return pl.pallas_call(
        body, name=name, grid=(S // ts,),
        in_specs=[_row(ts, D), _row(ts, D), _bcast(1, D), _row(ts, D)],
        out_specs=[_row(ts, D), _bcast(1, D)],
        out_shape=[jax.ShapeDtypeStruct((S, D), F32), jax.ShapeDtypeStruct((1, D), F32)],
        compiler_params=_params("arbitrary"))(dy, x, g, res)


def _rope_tables(S):
    half = QK_ROPE // 2
    pos = jnp.arange(S, dtype=F32)
    inv_freq = ROPE_THETA ** (-jnp.arange(0, QK_ROPE, 2, dtype=F32) / QK_ROPE)
    ang = pos[:, None] * inv_freq[None, :]
    cos, sin = jnp.cos(ang), jnp.sin(ang)
    z = jnp.zeros((S, half), F32)
    return jnp.concatenate([cos, z, cos, z], axis=1), jnp.concatenate([-sin, z, sin, z], axis=1)


def _rope_lanes(x, cos, sin_signed, inverse):
    if inverse:
        return x * cos + pltpu.roll(x * sin_signed, LANE // 2, 1)
    return x * cos + pltpu.roll(x, LANE // 2, 1) * sin_signed


def _rope(name, x, cos, sin_signed, inverse):
    S, W = x.shape
    ts = _tile(S, 512)

    def body(x_ref, c_ref, s_ref, o_ref):
        c, s = c_ref[...], s_ref[...]
        for h in range(W // LANE):
            sl = slice(h * LANE, (h + 1) * LANE)
            o_ref[:, sl] = _rope_lanes(x_ref[:, sl], c, s, inverse).astype(o_ref.dtype)

    return pl.pallas_call(
        body, name=name, grid=(S // ts,), in_specs=[_row(ts, W), _row(ts, LANE), _row(ts, LANE)],
        out_specs=_row(ts, W), out_shape=jax.ShapeDtypeStruct((S, W), BF),
        compiler_params=_params("parallel"))(x, cos, sin_signed)


LAT_W = 1024
_CQ = slice(0, Q_RANK)
_CKV = slice(Q_RANK, Q_RANK + KV_RANK)
_KPE = slice(Q_RANK + KV_RANK, Q_RANK + KV_RANK + LANE)


def _mla_prep(lat, qg, kvg, cos, sin_signed):
    S = lat.shape[0]
    ts = _tile(S, 512)

    def body(lat_ref, qg_ref, kvg_ref, c_ref, s_ref, qn_ref, kvn_ref, kpe_ref):
        cq = lat_ref[:, _CQ]
        qn_ref[...] = (cq * _rstd(cq) * qg_ref[...]).astype(BF)
        ckv = lat_ref[:, _CKV]
        kvn_ref[...] = (ckv * _rstd(ckv) * kvg_ref[...]).astype(BF)
        kpe_ref[...] = _rope_lanes(lat_ref[:, _KPE], c_ref[...], s_ref[...], False).astype(BF)

    return pl.pallas_call(
        body, name="mla_prep", grid=(S // ts,),
        in_specs=[_row(ts, LAT_W), _bcast(1, Q_RANK), _bcast(1, KV_RANK), _row(ts, LANE), _row(ts, LANE)],
        out_specs=[_row(ts, Q_RANK), _row(ts, KV_RANK), _row(ts, LANE)],
        out_shape=[jax.ShapeDtypeStruct((S, Q_RANK), BF), jax.ShapeDtypeStruct((S, KV_RANK), BF),
                   jax.ShapeDtypeStruct((S, LANE), BF)],
        compiler_params=_params("parallel"))(lat, qg, kvg, cos, sin_signed)


def _mla_prep_bwd(lat, qg, kvg, cos, sin_signed, d_qn, d_kvn, d_kpe):
    S = lat.shape[0]
    ts = _tile(S, 512)

    def body(lat_ref, qg_ref, kvg_ref, c_ref, s_ref, dqn_ref, dkvn_ref, dkpe_ref, dlat_ref, dqg_ref, dkvg_ref):
        i = pl.program_id(0)
        dcq, dqg = _norm_bwd_rows(dqn_ref[...], lat_ref[:, _CQ], qg_ref[...])
        dckv, dkvg = _norm_bwd_rows(dkvn_ref[...], lat_ref[:, _CKV], kvg_ref[...])
        dlat_ref[:, _CQ] = dcq.astype(BF)
        dlat_ref[:, _CKV] = dckv.astype(BF)
        dkpe = dkpe_ref[0]
        for g in range(1, d_kpe.shape[0]):
            dkpe = dkpe + dkpe_ref[g]
        dlat_ref[:, _KPE] = _rope_lanes(dkpe, c_ref[...], s_ref[...], True).astype(BF)
        dlat_ref[:, _KPE.stop:] = jnp.zeros((ts, LAT_W - _KPE.stop), BF)
        _accumulate(i, dqg_ref, dqg)
        _accumulate(i, dkvg_ref, dkvg)

    return pl.pallas_call(
        body, name="mla_prep_bwd", grid=(S // ts,),
        in_specs=[_row(ts, LAT_W), _bcast(1, Q_RANK), _bcast(1, KV_RANK), _row(ts, LANE), _row(ts, LANE),
                  _row(ts, Q_RANK), _row(ts, KV_RANK), pl.BlockSpec((d_kpe.shape[0], ts, LANE), lambda i: (0, i, 0))],
        out_specs=[_row(ts, LAT_W), _bcast(1, Q_RANK), _bcast(1, KV_RANK)],
        out_shape=[jax.ShapeDtypeStruct((S, LAT_W), BF), jax.ShapeDtypeStruct((1, Q_RANK), F32),
                   jax.ShapeDtypeStruct((1, KV_RANK), F32)],
        compiler_params=_params("arbitrary"))(lat, qg, kvg, cos, sin_signed, d_qn, d_kvn, d_kpe)


def _sigmoid(z):
    return 1.0 / (1.0 + jnp.exp(-z))


def _merge_fwd(gpre, b_gate, o_a, o_b):
    S, D = o_a.shape
    ts = _tile(S, 256)

    def body(g_ref, b_ref, oa_ref, ob_ref, m_ref):
        ga = _sigmoid(g_ref[:, :D] + b_ref[:, :D])
        gb = _sigmoid(g_ref[:, D:] + b_ref[:, D:])
        m_ref[...] = (ga * oa_ref[...] + gb * ob_ref[...]).astype(BF)

    return pl.pallas_call(
        body, name="merge_fwd", grid=(S // ts,),
        in_specs=[_row(ts, 2 * D), _bcast(1, 2 * D), _row(ts, D), _row(ts, D)], out_specs=_row(ts, D),
        out_shape=jax.ShapeDtypeStruct((S, D), BF), compiler_params=_params("parallel"))(gpre, b_gate, o_a, o_b)


def _merge_bwd(d_merge, gpre, b_gate, o_a, o_b):
    S, D = o_a.shape
    ts = _tile(S, 256)

    def body(dm_ref, g_ref, b_ref, oa_ref, ob_ref, doa_ref, dob_ref, dg_ref, db_ref):
        dm = dm_ref[...]
        ga = _sigmoid(g_ref[:, :D] + b_ref[:, :D])
        gb = _sigmoid(g_ref[:, D:] + b_ref[:, D:])
        doa_ref[...] = (dm * ga).astype(BF)
        dob_ref[...] = (dm * gb).astype(BF)
        dga = dm * oa_ref[...] * ga * (1.0 - ga)
        dgb = dm * ob_ref[...] * gb * (1.0 - gb)
        dg_ref[:, :D] = dga.astype(BF)
        dg_ref[:, D:] = dgb.astype(BF)
        i = pl.program_id(0)
        part = jnp.concatenate([jnp.sum(dga, axis=0, keepdims=True), jnp.sum(dgb, axis=0, keepdims=True)], axis=1)
        _accumulate(i, db_ref, part)

    return pl.pallas_call(
        body, name="merge_bwd", grid=(S // ts,),
        in_specs=[_row(ts, D), _row(ts, 2 * D), _bcast(1, 2 * D), _row(ts, D), _row(ts, D)],
        out_specs=[_row(ts, D), _row(ts, D), _row(ts, 2 * D), _bcast(1, 2 * D)],
        out_shape=[jax.ShapeDtypeStruct((S, D), BF), jax.ShapeDtypeStruct((S, D), BF),
                   jax.ShapeDtypeStruct((S, 2 * D), BF), jax.ShapeDtypeStruct((1, 2 * D), F32)],
        compiler_params=_params("arbitrary"))(d_merge, gpre, b_gate, o_a, o_b)


def _final_loss(x2, tgt, gf):
    S, D = x2.shape
    ts = _tile(S, 512)

    def body(x_ref, t_ref, g_ref, dx_ref, dg_ref, loss_ref):
        i = pl.program_id(0)
        xv = x_ref[...]
        g = g_ref[...]
        y = xv * _rstd(xv) * g
        err = y - t_ref[...]
        dx, dg = _norm_bwd_rows(err * (1.0 / D), xv, g)
        dx_ref[...] = dx
        _accumulate(i, dg_ref, dg)
        part = 0.5 * jnp.sum(jnp.mean(err * err, axis=-1, keepdims=True), axis=0, keepdims=True)
        _accumulate(i, loss_ref, jnp.broadcast_to(part, (8, LANE)))

    return pl.pallas_call(
        body, name="final_loss", grid=(S // ts,),
        in_specs=[_row(ts, D), _row(ts, D), _bcast(1, D)],
        out_specs=[_row(ts, D), _bcast(1, D), _bcast(8, LANE)],
        out_shape=[jax.ShapeDtypeStruct((S, D), F32), jax.ShapeDtypeStruct((1, D), F32),
                   jax.ShapeDtypeStruct((8, LANE), F32)],
        compiler_params=_params("arbitrary"))(x2, tgt, gf)


HALO = 16


SUB = 8


def _shift_down(cur, prev, k, rows):
    out = pltpu.roll(cur, k, 0)
    head = out[:SUB]
    for j in range(k):
        head = jnp.where(rows == j, prev[HALO - k + j:HALO - k + j + 1, :], head)
    return jnp.concatenate([head, out[SUB:]], axis=0)


def _shift_up(cur, nxt, k, rows, ts):
    out = pltpu.roll(cur, ts - k, 0)
    tail = out[ts - SUB:]
    for j in range(k):
        tail = jnp.where(rows == SUB - k + j, nxt[j:j + 1, :], tail)
    return jnp.concatenate([out[:ts - SUB], tail], axis=0)


def _conv_rows(cur, prev, w, b, rows):
    return b + w[0:1, :] * _shift_down(cur, prev, 2, rows) + w[1:2, :] * _shift_down(cur, prev, 1, rows) + w[2:3, :] * cur


def _conv_specs(ts, C, shard_of):
    nh = ts // HALO
    cur = pl.BlockSpec((None, ts, C), lambda g, i: (shard_of(g), i, 0))
    prev = pl.BlockSpec((None, HALO, C), lambda g, i: (shard_of(g), jnp.maximum(i * nh - 1, 0), 0))
    return cur, prev


def _ffn_act(u_pre, conv_w, conv_b):
    G4, S, C = u_pre.shape
    G = G4 // 2
    ts = _tile(S, 256)

    def body(up_ref, upp_ref, gt_ref, gtp_ref, wu_ref, wg_ref, bu_ref, bg_ref, act_ref, u_ref):
        first = pl.program_id(1) == 0
        rows = lax.broadcasted_iota(jnp.int32, (SUB, C), 0)
        pu = jnp.where(first, 0.0, upp_ref[...].astype(F32))
        pg = jnp.where(first, 0.0, gtp_ref[...].astype(F32))
        up = _conv_rows(up_ref[...].astype(F32), pu, wu_ref[...], bu_ref[...], rows)
        gate = _conv_rows(gt_ref[...].astype(F32), pg, wg_ref[...], bg_ref[...], rows)
        act_ref[...] = (gate * _sigmoid(gate) * up).astype(BF)
        u_ref[0] = up.astype(BF)
        u_ref[1] = gate.astype(BF)

    cur_u, prev_u = _conv_specs(ts, C, lambda g: g)
    cur_g, prev_g = _conv_specs(ts, C, lambda g: g + G)
    w_u = pl.BlockSpec((None, 3, C), lambda g, i: (g, 0, 0))
    w_g = pl.BlockSpec((None, 3, C), lambda g, i: (g + G, 0, 0))
    b_u = pl.BlockSpec((None, 1, C), lambda g, i: (g, 0, 0))
    b_g = pl.BlockSpec((None, 1, C), lambda g, i: (g + G, 0, 0))
    pair = pl.BlockSpec((2, None, ts, C), lambda g, i: (0, g, i, 0))
    act, u = pl.pallas_call(
        body, name="ffn_act", grid=(G, S // ts),
        in_specs=[cur_u, prev_u, cur_g, prev_g, w_u, w_g, b_u, b_g],
        out_specs=[pl.BlockSpec((None, ts, C), lambda g, i: (g, i, 0)), pair],
        out_shape=[jax.ShapeDtypeStruct((G, S, C), BF), jax.ShapeDtypeStruct((2, G, S, C), BF)],
        compiler_params=_params("parallel", "parallel"))(u_pre, u_pre, u_pre, u_pre, conv_w, conv_w, conv_b, conv_b)
    return act, u


def _ffn_act_bwd(u, d_act):
    _, G, S, C = u.shape
    ts = _tile(S, 256)

    def body(u_ref, da_ref, du_ref):
        up, gate = u_ref[0].astype(F32), u_ref[1].astype(F32)
        sg = _sigmoid(gate)
        da = da_ref[...].astype(F32)
        du_ref[0] = (da * (gate * sg)).astype(BF)
        du_ref[1] = (da * up * (sg * (1.0 + gate * (1.0 - sg)))).astype(BF)

    pair = pl.BlockSpec((2, None, ts, C), lambda g, i: (0, g, i, 0))
    du = pl.pallas_call(
        body, name="ffn_act_bwd", grid=(G, S // ts),
        in_specs=[pair, pl.BlockSpec((None, ts, C), lambda g, i: (g, i, 0))], out_specs=pair,
        out_shape=jax.ShapeDtypeStruct((2, G, S, C), BF),
        compiler_params=_params("parallel", "parallel"))(u, d_act)
    return du.reshape(2 * G, S, C)


def _conv_bwd(du, u_pre, conv_w):
    G4, S, C = du.shape
    ts = _tile(S, 256)
    nh = ts // HALO
    last_halo = S // HALO - 1

    def body(du_ref, dun_ref, u_ref, up_ref, w_ref, dpre_ref, dw_ref, db_ref):
        i = pl.program_id(1)
        rows = lax.broadcasted_iota(jnp.int32, (SUB, C), 0)
        du_c = du_ref[...].astype(F32)
        nxt = jnp.where(i == pl.num_programs(1) - 1, 0.0, dun_ref[...].astype(F32))
        prev = jnp.where(i == 0, 0.0, up_ref[...].astype(F32))
        w = w_ref[...]
        dpre = w[2:3, :] * du_c + w[1:2, :] * _shift_up(du_c, nxt, 1, rows, ts) + w[0:1, :] * _shift_up(du_c, nxt, 2, rows, ts)
        dpre_ref[...] = dpre.astype(BF)
        u_c = u_ref[...].astype(F32)
        dw = jnp.concatenate([
            jnp.sum(du_c * _shift_down(u_c, prev, 2, rows), axis=0, keepdims=True),
            jnp.sum(du_c * _shift_down(u_c, prev, 1, rows), axis=0, keepdims=True),
            jnp.sum(du_c * u_c, axis=0, keepdims=True)], axis=0)
        _accumulate(i, dw_ref, dw)
        _accumulate(i, db_ref, jnp.sum(du_c, axis=0, keepdims=True))

    cur = pl.BlockSpec((None, ts, C), lambda g, i: (g, i, 0))
    nxt = pl.BlockSpec((None, HALO, C), lambda g, i: (g, jnp.minimum((i + 1) * nh, last_halo), 0))
    prev = pl.BlockSpec((None, HALO, C), lambda g, i: (g, jnp.maximum(i * nh - 1, 0), 0))
    return pl.pallas_call(
        body, name="conv_bwd", grid=(G4, S // ts),
        in_specs=[cur, nxt, cur, prev, pl.BlockSpec((None, 3, C), lambda g, i: (g, 0, 0))],
        out_specs=[cur, pl.BlockSpec((None, 3, C), lambda g, i: (g, 0, 0)), pl.BlockSpec((None, 1, C), lambda g, i: (g, 0, 0))],
        out_shape=[jax.ShapeDtypeStruct((G4, S, C), BF), jax.ShapeDtypeStruct((G4, 3, C), F32),
                   jax.ShapeDtypeStruct((G4, 1, C), F32)],
        compiler_params=_params("parallel", "arbitrary"))(du, du, u_pre, u_pre, conv_w)


MLA_T = 1024
MLA_HB = 4
MLA_BWD_HB = 2


def _mla_pairs(n, by_row):
    if by_row:
        pairs = [(i, j) for i in range(n) for j in range(i + 1)]
    else:
        pairs = [(i, j) for j in range(n) for i in range(j, n)]
    return jnp.asarray([p[0] for p in pairs], jnp.int32), jnp.asarray([p[1] for p in pairs], jnp.int32)


def _mla_specs(hb):
    q = pl.BlockSpec((MLA_T, hb * HEAD), lambda g, t, it, jt: (it[t], g))
    k = pl.BlockSpec((MLA_T, hb * HEAD), lambda g, t, it, jt: (jt[t], g))
    kpe = pl.BlockSpec((MLA_T, HEAD), lambda g, t, it, jt: (jt[t], 0))
    lse = pl.BlockSpec((hb, MLA_T, LANE), lambda g, t, it, jt: (g, it[t], 0))
    return q, k, kpe, lse


def _mla_head(ref, hh):
    return ref[:, hh * HEAD:(hh + 1) * HEAD]


LOG2E = math.log2(math.e)
MLA_EXP2_SCALE = MLA_SCALE * LOG2E


def _mla_scores(qn_ref, qp_ref, kn_ref, kpe, hh, ok):
    q = jnp.concatenate([_mla_head(qn_ref, hh), _mla_head(qp_ref, hh)], axis=1)
    k = jnp.concatenate([_mla_head(kn_ref, hh), kpe], axis=1)
    s = lax.dot_general(q, k, NT, preferred_element_type=F32)
    return q, k, s if ok is None else jnp.where(ok, s, -jnp.inf)


def _mla_diagonal_mask():
    row = lax.broadcasted_iota(jnp.int32, (MLA_T, MLA_T), 0)
    col = lax.broadcasted_iota(jnp.int32, (MLA_T, MLA_T), 1)
    return col <= row


def _mla_step(i, j, step):
    @pl.when(j < i)
    def _():
        step(None)

    @pl.when(j == i)
    def _():
        step(_mla_diagonal_mask())


def _mla_fwd(qn, qp, kn, kpe, v):
    S = qn.shape[0]
    it, jt = _mla_pairs(S // MLA_T, True)

    def body(it_ref, jt_ref, qn_ref, qp_ref, kn_ref, kpe_ref, v_ref, o_ref, lse_ref, m_scr, acc_scr):
        t = pl.program_id(1)
        i, j = it_ref[t], jt_ref[t]

        @pl.when(j == 0)
        def _():
            m_scr[...] = jnp.full(m_scr.shape, -jnp.inf, F32)
            acc_scr[...] = jnp.zeros(acc_scr.shape, F32)

        def step(ok):
            kpe_v = kpe_ref[...]
            ones = jnp.ones((MLA_T, HEAD), BF)
            state = [(m_scr[hh], acc_scr[hh]) for hh in range(MLA_HB)]
            new = []
            for hh in range(MLA_HB):
                m_prev, acc = state[hh]
                _, _, s = _mla_scores(qn_ref, qp_ref, kn_ref, kpe_v, hh, ok)
                m_new = jnp.maximum(m_prev, jnp.max(s, axis=1, keepdims=True))
                p = jnp.exp2((s - m_new) * MLA_EXP2_SCALE).astype(BF)
                v1 = jnp.concatenate([_mla_head(v_ref, hh), ones], axis=1)
                alpha = jnp.exp2((m_prev - m_new) * MLA_EXP2_SCALE)
                new.append((m_new, alpha * acc + lax.dot_general(p, v1, NN, preferred_element_type=F32)))
            for hh in range(MLA_HB):
                m_scr[hh], acc_scr[hh] = new[hh]

        _mla_step(i, j, step)

        @pl.when(j == i)
        def _():
            for hh in range(MLA_HB):
                l = acc_scr[hh, :, HEAD:]
                o_ref[:, hh * HEAD:(hh + 1) * HEAD] = (acc_scr[hh, :, :HEAD] / l).astype(BF)
                lse_ref[hh] = m_scr[hh] * MLA_SCALE + jnp.log(l)

    qspec, kspec, kpespec, lsespec = _mla_specs(MLA_HB)
    grid_spec = pltpu.PrefetchScalarGridSpec(
        num_scalar_prefetch=2, grid=(MLA_HEADS // MLA_HB, it.shape[0]),
        in_specs=[qspec, qspec, kspec, kpespec, kspec], out_specs=[qspec, lsespec],
        scratch_shapes=[pltpu.VMEM((MLA_HB, MLA_T, 1), F32), pltpu.VMEM((MLA_HB, MLA_T, 2 * HEAD), F32)])
    return pl.pallas_call(
        body, name="mla_fwd", grid_spec=grid_spec,
        out_shape=[jax.ShapeDtypeStruct((S, MLA_HEADS * HEAD), BF), jax.ShapeDtypeStruct((MLA_HEADS, S, LANE), F32)],
        compiler_params=_params("parallel", "arbitrary"))(it, jt, qn, qp, kn, kpe, v)


def _mla_p_ds(qn_ref, qp_ref, kn_ref, kpe, v_ref, do_ref, o_ref, lse_ref, hh, ok):
    q, k, s = _mla_scores(qn_ref, qp_ref, kn_ref, kpe, hh, ok)
    p = jnp.exp2(s * MLA_EXP2_SCALE - lse_ref[hh][:, 0:1] * LOG2E)
    do = _mla_head(do_ref, hh)
    delta = jnp.sum(do.astype(F32) * _mla_head(o_ref, hh).astype(F32), axis=1, keepdims=True)
    dp = lax.dot_general(do, _mla_head(v_ref, hh), NT, preferred_element_type=F32)
    ds = p * (dp - delta) * MLA_SCALE
    return q, k, p, ds, do


def _mla_bwd(qn, qp, kn, kpe, v, do, o, lse):
    S = qn.shape[0]
    nq = S // MLA_T
    hb = MLA_BWD_HB
    it, jt = _mla_pairs(nq, False)

    def body(it_ref, jt_ref, qn_ref, qp_ref, kn_ref, kpe_ref, v_ref, do_ref, o_ref, lse_ref,
             dqn_ref, dqp_ref, dkn_ref, dv_ref, dkpe_ref, dq_acc, dk_acc, dv_acc, stage_n, stage_p, osem):
        t = pl.program_id(1)
        i, j = it_ref[t], jt_ref[t]

        @pl.when(t == 0)
        def _():
            dq_acc[...] = jnp.zeros(dq_acc.shape, F32)

        @pl.when(i == j)
        def _():
            dk_acc[...] = jnp.zeros(dk_acc.shape, F32)
            dv_acc[...] = jnp.zeros(dv_acc.shape, F32)

        def step(ok):
            kpe_v = kpe_ref[...]
            for hh in range(hb):
                q, k, p, ds, do_h = _mla_p_ds(qn_ref, qp_ref, kn_ref, kpe_v, v_ref, do_ref, o_ref, lse_ref, hh, ok)
                ds = ds.astype(BF)
                dv_acc[hh] += lax.dot_general(p.astype(BF), do_h, TN, preferred_element_type=F32)
                dk_acc[hh] += lax.dot_general(ds, q, TN, preferred_element_type=F32)
                dq_acc[i, hh] += lax.dot_general(ds, k, NN, preferred_element_type=F32)

        _mla_step(i, j, step)

        @pl.when(i == j)
        def _():
            for hh in range(hb):
                stage_n[:, hh * HEAD:(hh + 1) * HEAD] = dq_acc[i, hh, :, :HEAD].astype(BF)
                stage_p[:, hh * HEAD:(hh + 1) * HEAD] = dq_acc[i, hh, :, HEAD:]
            rows = pl.ds(pl.multiple_of(i * MLA_T, MLA_T), MLA_T)
            cols = pl.ds(pl.multiple_of(pl.program_id(0) * hb * HEAD, LANE), hb * HEAD)
            out_n = pltpu.make_async_copy(stage_n, dqn_ref.at[rows, cols], osem.at[0])
            out_p = pltpu.make_async_copy(stage_p, dqp_ref.at[rows, cols], osem.at[1])
            out_n.start()
            out_p.start()
            out_n.wait()
            out_p.wait()

        @pl.when(i == nq - 1)
        def _():
            dkpe = dk_acc[0, :, HEAD:]
            for hh in range(hb):
                dkn_ref[:, hh * HEAD:(hh + 1) * HEAD] = dk_acc[hh, :, :HEAD].astype(BF)
                dv_ref[:, hh * HEAD:(hh + 1) * HEAD] = dv_acc[hh].astype(BF)
                if hh:
                    dkpe = dkpe + dk_acc[hh, :, HEAD:]
            dkpe_ref[...] = dkpe

    qspec, kspec, kpespec, lsespec = _mla_specs(hb)
    dkpespec = pl.BlockSpec((None, MLA_T, HEAD), lambda g, t, it, jt: (g, jt[t], 0))
    grid_spec = pltpu.PrefetchScalarGridSpec(
        num_scalar_prefetch=2, grid=(MLA_HEADS // hb, it.shape[0]),
        in_specs=[qspec, qspec, kspec, kpespec, kspec, qspec, qspec, lsespec],
        out_specs=[ANY, ANY, kspec, kspec, dkpespec],
        scratch_shapes=[pltpu.VMEM((nq, hb, MLA_T, 2 * HEAD), F32), pltpu.VMEM((hb, MLA_T, 2 * HEAD), F32),
                        pltpu.VMEM((hb, MLA_T, HEAD), F32), pltpu.VMEM((MLA_T, hb * HEAD), BF),
                        pltpu.VMEM((MLA_T, hb * HEAD), F32), pltpu.SemaphoreType.DMA((2,))])
    wide = jax.ShapeDtypeStruct((S, MLA_HEADS * HEAD), BF)
    return pl.pallas_call(
        body, name="mla_bwd", grid_spec=grid_spec,
        out_shape=[wide, jax.ShapeDtypeStruct((S, MLA_HEADS * HEAD), F32), wide, wide,
                   jax.ShapeDtypeStruct((MLA_HEADS // hb, S, HEAD), F32)],
        compiler_params=_params("parallel", "arbitrary"))(it, jt, qn, qp, kn, kpe, v, do, o, lse)


DIL_W = 3 * DIL_HPG * HEAD
DIL_O = DIL_HPG * HEAD
DIL_STEP_BLOCKS = 4


def _dil_slopes(g):
    return [2.0 ** (-ALIBI_MAX_BIAS * (g * DIL_HPG + hh + 1) / DIL_HEADS) for hh in range(DIL_HPG)]


def _dil_bias(dil):
    p = lax.broadcasted_iota(jnp.int32, (DIL_BLOCK, DIL_BLOCK), 0)
    kk = lax.broadcasted_iota(jnp.int32, (DIL_BLOCK, DIL_BLOCK), 1)
    jc = p - kk
    dist_c = (dil * jc).astype(F32)
    dist_p = (dil * (jc + DIL_BLOCK)).astype(F32)
    return jc >= 0, jc <= 0, dist_c, dist_p


def _dil_head(blk, hh):
    q = blk[:, hh * HEAD:(hh + 1) * HEAD]
    k = blk[:, DIL_O + hh * HEAD:DIL_O + (hh + 1) * HEAD]
    v = blk[:, 2 * DIL_O + hh * HEAD:2 * DIL_O + (hh + 1) * HEAD]
    return q, k, v


def _dil_s(q, k, slope, dist, ok):
    s = lax.dot_general(q, k, NT, preferred_element_type=F32) * DIL_SCALE - slope * dist
    return jnp.where(ok, s, -jnp.inf)


def _dil_view(a, dil):
    S, W = a.shape
    return a.reshape(S // dil, dil * W)


def _dil_fwd(qkv, g):
    _, dil = DIL_PATTERNS[g]
    S = qkv.shape[0]
    L = S // dil
    nb = L // DIL_BLOCK
    slopes = _dil_slopes(g)

    bb = min(DIL_STEP_BLOCKS, nb)
    rows = bb * DIL_BLOCK

    def body(cur_ref, prev_ref, o_ref, lse_ref):
        n = pl.program_id(1)
        ok_c, ok_p, dist_c, dist_p = _dil_bias(dil)
        for b in range(bb):
            cur = cur_ref[b * DIL_BLOCK:(b + 1) * DIL_BLOCK, :]
            prev = prev_ref[...] if b == 0 else cur_ref[(b - 1) * DIL_BLOCK:b * DIL_BLOCK, :]
            ok_b = ok_p & (n > 0) if b == 0 else ok_p
            for hh in range(DIL_HPG):
                q, kc, vc = _dil_head(cur, hh)
                _, kp, vp = _dil_head(prev, hh)
                sc = _dil_s(q, kc, slopes[hh], dist_c, ok_c)
                sp = _dil_s(q, kp, slopes[hh], dist_p, ok_b)
                m = jnp.maximum(jnp.max(sc, axis=1, keepdims=True), jnp.max(sp, axis=1, keepdims=True))
                pc, pp = jnp.exp(sc - m), jnp.exp(sp - m)
                l = jnp.sum(pc, axis=1, keepdims=True) + jnp.sum(pp, axis=1, keepdims=True)
                o = (lax.dot_general(pc.astype(BF), vc, NN, preferred_element_type=F32)
                     + lax.dot_general(pp.astype(BF), vp, NN, preferred_element_type=F32)) / l
                rs, sl = slice(b * DIL_BLOCK, (b + 1) * DIL_BLOCK), slice(hh * HEAD, (hh + 1) * HEAD)
                o_ref[rs, sl] = o
                lse_ref[rs, sl] = jnp.broadcast_to(m + jnp.log(l), (DIL_BLOCK, HEAD))

    ospec = pl.BlockSpec((rows, DIL_O), lambda r, n: (n, r))
    o, lse = pl.pallas_call(
        body, name=f"dil_fwd{g}", grid=(dil, nb // bb),
        in_specs=[pl.BlockSpec((rows, DIL_W), lambda r, n: (n, r)),
                  pl.BlockSpec((DIL_BLOCK, DIL_W), lambda r, n: (jnp.maximum(n * bb - 1, 0), r))],
        out_specs=[ospec, ospec],
        out_shape=[jax.ShapeDtypeStruct((L, dil * DIL_O), F32), jax.ShapeDtypeStruct((L, dil * DIL_O), F32)],
        compiler_params=_params("parallel", "parallel"))(_dil_view(qkv, dil), _dil_view(qkv, dil))
    return o.reshape(S, DIL_O), lse.reshape(S, DIL_O)


def _dil_combine(os_, lses):
    S = os_[0].shape[0]
    ts = _tile(S, 512)

    def body(o0, o1, o2, l0, l1, l2, out_ref, lse_ref):
        a, b, c = l0[...], l1[...], l2[...]
        m = jnp.maximum(jnp.maximum(a, b), c)
        ea, eb, ec = jnp.exp(a - m), jnp.exp(b - m), jnp.exp(c - m)
        tot = ea + eb + ec
        out_ref[...] = ((ea * o0[...] + eb * o1[...] + ec * o2[...]) / tot).astype(BF)
        lse_ref[...] = m + jnp.log(tot)

    return pl.pallas_call(
        body, name="dil_combine", grid=(S // ts,), in_specs=[_row(ts, DIL_O)] * 6,
        out_specs=[_row(ts, DIL_O), _row(ts, DIL_O)],
        out_shape=[jax.ShapeDtypeStruct((S, DIL_O), BF), jax.ShapeDtypeStruct((S, DIL_O), F32)],
        compiler_params=_params("parallel"))(*os_, *lses)


def _dil_delta(do, out):
    S = do.shape[0]
    ts = _tile(S, 512)

    def body(do_ref, o_ref, d_ref):
        for hh in range(DIL_HPG):
            sl = slice(hh * HEAD, (hh + 1) * HEAD)
            d = jnp.sum(do_ref[:, sl].astype(F32) * o_ref[:, sl].astype(F32), axis=1, keepdims=True)
            d_ref[:, sl] = jnp.broadcast_to(d, (ts, HEAD))

    return pl.pallas_call(
        body, name="dil_delta", grid=(S // ts,), in_specs=[_row(ts, DIL_O)] * 2, out_specs=_row(ts, DIL_O),
        out_shape=jax.ShapeDtypeStruct((S, DIL_O), F32), compiler_params=_params("parallel"))(do, out)


def _dil_bwd(qkv, do, lse, delta, g):
    _, dil = DIL_PATTERNS[g]
    S = qkv.shape[0]
    L = S // dil
    nb = L // DIL_BLOCK
    slopes = _dil_slopes(g)

    def pair(q, k, v, do_h, lse_h, delta_h, slope, dist, ok):
        s = _dil_s(q, k, slope, dist, ok)
        p = jnp.exp(s - lse_h)
        dp = lax.dot_general(do_h, v, NT, preferred_element_type=F32)
        ds = (p * (dp - delta_h) * DIL_SCALE).astype(BF)
        return p.astype(BF), ds

    bb = min(DIL_STEP_BLOCKS, nb)
    rows = bb * DIL_BLOCK
    steps = nb // bb

    def body(cur_ref, prev_ref, next_ref, doc_ref, don_ref, lsec_ref, lsen_ref, dlc_ref, dln_ref, out_ref):
        n = pl.program_id(1)
        ok_c, ok_p0, dist_c, dist_p = _dil_bias(dil)
        for b in range(bb):
            rs = slice(b * DIL_BLOCK, (b + 1) * DIL_BLOCK)
            rp = slice((b - 1) * DIL_BLOCK, b * DIL_BLOCK)
            rn = slice((b + 1) * DIL_BLOCK, (b + 2) * DIL_BLOCK)
            first, last = b == 0, b == bb - 1
            cur = cur_ref[rs, :]
            prev = prev_ref[...] if first else cur_ref[rp, :]
            nxt = next_ref[...] if last else cur_ref[rn, :]
            ok_a = ok_p0 & (n > 0) if first else ok_p0
            ok_n = ok_p0 & (n < steps - 1) if last else ok_p0
            for hh in range(DIL_HPG):
                sl = slice(hh * HEAD, (hh + 1) * HEAD)
                q, kc, vc = _dil_head(cur, hh)
                _, kp, vp = _dil_head(prev, hh)
                qn, _, _ = _dil_head(nxt, hh)
                do_c = doc_ref[rs, sl]
                do_n = don_ref[:, sl] if last else doc_ref[rn, sl]
                lse_c = lsec_ref[rs, sl][:, 0:1]
                lse_n = (lsen_ref[:, sl] if last else lsec_ref[rn, sl])[:, 0:1]
                dl_c = dlc_ref[rs, sl][:, 0:1]
                dl_n = (dln_ref[:, sl] if last else dlc_ref[rn, sl])[:, 0:1]
                _, ds_a = pair(q, kp, vp, do_c, lse_c, dl_c, slopes[hh], dist_p, ok_a)
                p_b, ds_b = pair(q, kc, vc, do_c, lse_c, dl_c, slopes[hh], dist_c, ok_c)
                p_n, ds_n = pair(qn, kc, vc, do_n, lse_n, dl_n, slopes[hh], dist_p, ok_n)
                dq = (lax.dot_general(ds_a, kp, NN, preferred_element_type=F32)
                      + lax.dot_general(ds_b, kc, NN, preferred_element_type=F32))
                dk = (lax.dot_general(ds_b, q, TN, preferred_element_type=F32)
                      + lax.dot_general(ds_n, qn, TN, preferred_element_type=F32))
                dv = (lax.dot_general(p_b, do_c, TN, preferred_element_type=F32)
                      + lax.dot_general(p_n, do_n, TN, preferred_element_type=F32))
                out_ref[rs, sl] = dq.astype(BF)
                out_ref[rs, DIL_O + hh * HEAD:DIL_O + (hh + 1) * HEAD] = dk.astype(BF)
                out_ref[rs, 2 * DIL_O + hh * HEAD:2 * DIL_O + (hh + 1) * HEAD] = dv.astype(BF)

    cur_w = pl.BlockSpec((rows, DIL_W), lambda r, n: (n, r))
    prev_w = pl.BlockSpec((DIL_BLOCK, DIL_W), lambda r, n: (jnp.maximum(n * bb - 1, 0), r))
    next_w = pl.BlockSpec((DIL_BLOCK, DIL_W), lambda r, n: (jnp.minimum((n + 1) * bb, nb - 1), r))
    cur_o = pl.BlockSpec((rows, DIL_O), lambda r, n: (n, r))
    next_o = pl.BlockSpec((DIL_BLOCK, DIL_O), lambda r, n: (jnp.minimum((n + 1) * bb, nb - 1), r))
    qv, dov, lsev, dlv = _dil_view(qkv, dil), _dil_view(do, dil), _dil_view(lse, dil), _dil_view(delta, dil)
    out = pl.pallas_call(
        body, name=f"dil_bwd{g}", grid=(dil, steps),
        in_specs=[cur_w, prev_w, next_w, cur_o, next_o, cur_o, next_o, cur_o, next_o],
        out_specs=cur_w, out_shape=jax.ShapeDtypeStruct((L, dil * DIL_W), BF),
        compiler_params=_params("parallel", "parallel"))(qv, qv, qv, dov, dov, lsev, lsev, dlv, dlv)
    return out.reshape(S, DIL_W)


def _adamw(name, w, g, m, v):
    R, C = w.shape
    tr, tc = _adamw_block(R, C)

    def body(w_ref, g_ref, m_ref, v_ref, go_ref, d_ref, nm_ref, nv_ref):
        gv = g_ref[...]
        go_ref[...] = gv
        nm = ADAM_B1 * m_ref[...] + (1.0 - ADAM_B1) * gv
        nv = ADAM_B2 * v_ref[...] + (1.0 - ADAM_B2) * (gv * gv)
        m_hat = nm / (1.0 - ADAM_B1 ** ADAM_STEP)
        v_hat = nv / (1.0 - ADAM_B2 ** ADAM_STEP)
        d_ref[...] = -ADAM_LR * (m_hat / (jnp.sqrt(v_hat) + ADAM_EPS) + ADAM_WD * w_ref[...])
        nm_ref[...] = nm
        nv_ref[...] = nv

    spec = pl.BlockSpec((tr, tc), lambda i, j: (i, j))
    shp = jax.ShapeDtypeStruct((R, C), F32)
    return pl.pallas_call(
        body, name=name, grid=(R // tr, C // tc), in_specs=[spec] * 4, out_specs=[spec] * 4, out_shape=[shp] * 4,
        compiler_params=_params("parallel", "parallel"))(w, g, m, v)


ADAMW_BLOCK_ELEMS = 640 * 1024


def _adamw_block(R, C):
    if R * C <= ADAMW_BLOCK_ELEMS:
        return R, C
    tr = _tile_rows(R, max(8, ADAMW_BLOCK_ELEMS // C))
    tc = _tile(C, max(LANE, ADAMW_BLOCK_ELEMS // R))
    if tr * C >= R * tc or R * tc > ADAMW_BLOCK_ELEMS:
        return tr, C
    return R, tc


def _tile_rows(n, pref, mult=8):
    t = (pref // mult) * mult
    while t >= mult:
        if n % t == 0:
            return t
        t -= mult
    return n


ANY = pl.BlockSpec(memory_space=pl.ANY)


def _place():
    x, y, c = lax.axis_index("x"), lax.axis_index("y"), lax.axis_index("c")
    chips = [(1 - x, y), (x, 1 - y), (1 - x, 1 - y)]
    chip_idx = [2 * cx + cy for cx, cy in chips]
    return x, y, c, 2 * x + y, chips, chip_idx


def _rcopy(src, dst, ssem, rsem, dev):
    return pltpu.make_async_remote_copy(src_ref=src, dst_ref=dst, send_sem=ssem, recv_sem=rsem,
                                        device_id=dev, device_id_type=MESH)


HBM = pl.BlockSpec(memory_space=pltpu.HBM)
SEM = pl.BlockSpec(memory_space=pltpu.SEMAPHORE)
EFFECT = pltpu.SideEffectType.DATAFLOW_SIDE_EFFECTING


def _split_copies(kind, srcs, lands, ssem, rsem):
    _, _, c, me, chips, chip_idx = _place()
    cps = []
    for i in range(len(srcs)):
        for k in range(3):
            if kind == "gather":
                rows = srcs[i].shape[0]
                if rows == lands[i].shape[1]:
                    src, dst = srcs[i], lands[i].at[me]
                else:
                    src, dst = srcs[i], lands[i].at[me, pl.ds(pl.multiple_of(c * rows, 16), rows)]
            else:
                src, dst = srcs[i].at[chip_idx[k]], lands[i].at[k]
            cps.append(_rcopy(src, dst, ssem.at[3 * i + k], rsem.at[3 * i + k], (*chips[k], c)))
    return cps


def _exchange_start(name, kind, srcs, lands, groups):
    n, ng = len(srcs), len(groups)

    def body(*refs):
        src_refs, land_refs = refs[:n], refs[n:2 * n]
        sems = refs[2 * n:2 * n + 2 * ng]
        token = refs[-1]
        for gi, grp in enumerate(groups):
            cps = _split_copies(kind, [src_refs[i] for i in grp], [land_refs[i] for i in grp], sems[2 * gi], sems[2 * gi + 1])
            for cp in cps:
                cp.start()
        token[...] = jnp.zeros_like(token)

    arrays = list(srcs) + list(lands)
    out_shape = []
    for grp in groups:
        out_shape += [pltpu.SemaphoreType.DMA((3 * len(grp),)), pltpu.SemaphoreType.DMA((3 * len(grp),))]
    out_shape += [pltpu.HBM(a.shape, a.dtype) for a in arrays] + [jax.ShapeDtypeStruct((8, LANE), F32)]
    outs = pl.pallas_call(
        body, name=name, out_shape=out_shape, in_specs=[HBM] * (2 * n),
        out_specs=[SEM] * (2 * ng) + [HBM] * (2 * n) + [pl.BlockSpec(memory_space=pltpu.VMEM)],
        input_output_aliases={i: 2 * ng + i for i in range(2 * n)},
        compiler_params=pltpu.CompilerParams(has_side_effects=EFFECT),
    )(*[pltpu.with_memory_space_constraint(a, pltpu.HBM) for a in arrays])
    sems = [(outs[2 * gi], outs[2 * gi + 1]) for gi in range(ng)]
    thru = outs[2 * ng:2 * ng + 2 * n]
    return sems, thru[:n], thru[n:], outs[-1]


def _exchange_wait(name, kind, srcs, lands, sems, after):
    n = len(srcs)

    def body(*refs):
        cps = _split_copies(kind, refs[:n], refs[n:2 * n], refs[2 * n], refs[2 * n + 1])
        for cp in cps:
            cp.wait_send()
            cp.wait_recv()

    arrays = list(srcs) + list(lands)
    outs = pl.pallas_call(
        body, name=name, out_shape=[pltpu.HBM(a.shape, a.dtype) for a in arrays],
        in_specs=[HBM] * (2 * n) + [SEM, SEM, ANY], out_specs=[HBM] * (2 * n),
        input_output_aliases={i: i for i in range(2 * n)},
        compiler_params=pltpu.CompilerParams(has_side_effects=EFFECT),
    )(*arrays, sems[0], sems[1], after)
    return outs[:n], outs[n:]


EXCHANGE_CHUNK_BYTES = 3 * 1024 * 1024


def _half_geometry(R, C, axis):
    Rp, Cp = (R // 2, C) if axis == 0 else (R, C // 2)
    tr = _tile_rows(Rp, max(16, EXCHANGE_CHUNK_BYTES // (2 * Cp)), 16)
    return Rp, Cp, tr, Rp // tr


def _pair_sum(name, g, axis):
    G, R, C = g.shape
    Rp, Cp, tr, nb = _half_geometry(R, C, axis)
    steps = G * nb

    def half_block(s, b, h):
        return (s, h * nb + b, 0) if axis == 0 else (s, b, h)

    def body(c_ref, keep_ref, give_ref, out_ref, land, ssem, rsem, credit):
        x, y, c = lax.axis_index("x"), lax.axis_index("y"), lax.axis_index("c")
        sib = (x, y, 1 - c)
        t = pl.program_id(0) * nb + pl.program_id(1)
        slot = t % 2

        @pl.when(t >= 2)
        def _():
            pl.semaphore_wait(credit, 1)

        cp = _rcopy(give_ref.at[0], land.at[slot], ssem.at[slot], rsem.at[slot], sib)
        cp.start()
        cp.wait_recv()
        out_ref[...] = (keep_ref[...].astype(F32) + land[slot].astype(F32)).astype(BF)

        @pl.when(t + 2 < steps)
        def _():
            pl.semaphore_signal(credit, 1, device_id=sib, device_id_type=MESH)

        cp.wait_send()

    blk = (None, tr, Cp)
    grid_spec = pltpu.PrefetchScalarGridSpec(
        num_scalar_prefetch=1, grid=(G, nb),
        in_specs=[pl.BlockSpec(blk, lambda s, b, c_ref: half_block(s, b, c_ref[0])),
                  pl.BlockSpec((1, tr, Cp), lambda s, b, c_ref: half_block(s, b, 1 - c_ref[0]))],
        out_specs=pl.BlockSpec(blk, lambda s, b, c_ref: (s, b, 0)),
        scratch_shapes=[pltpu.VMEM((2, tr, Cp), BF), pltpu.SemaphoreType.DMA((2,)), pltpu.SemaphoreType.DMA((2,)),
                        pltpu.SemaphoreType.REGULAR])
    c_arr = lax.axis_index("c").astype(jnp.int32).reshape(1)
    return pl.pallas_call(
        body, name=name, grid_spec=grid_spec, out_shape=jax.ShapeDtypeStruct((G, Rp, Cp), BF),
        compiler_params=_params("arbitrary", "arbitrary"))(c_arr, g, g)


def _chip_total_join(name, h, landed, axis):
    G, Rp, Cp = h.shape
    R, C = (2 * Rp, Cp) if axis == 0 else (Rp, 2 * Cp)
    tr = _tile_rows(Rp, max(16, EXCHANGE_CHUNK_BYTES // (4 * Cp)), 16)
    nb = Rp // tr

    def body(me_ref, own_ref, l0_ref, l1_ref, l2_ref, full, stage, ssem, rsem, lsem):
        x, y, c = lax.axis_index("x"), lax.axis_index("y"), lax.axis_index("c")
        sib = (x, y, 1 - c)
        b = pl.program_id(0)

        def place(half, r0, rows):
            if axis == 0:
                return full.at[pl.ds(pl.multiple_of(half * Rp + r0, 8), rows), :]
            return full.at[pl.ds(pl.multiple_of(r0, 8), rows), pl.ds(pl.multiple_of(half * Cp, LANE), Cp)]

        def copies(step):
            s = step % 2
            mine = place(c, step * tr, tr)
            return pltpu.make_async_copy(stage.at[s], mine, lsem.at[s]), _rcopy(stage.at[s], mine, ssem.at[s], rsem, sib)

        @pl.when(b >= 2)
        def _():
            loc, rem = copies(b - 2)
            loc.wait()
            rem.wait_send()

        acc = own_ref[...].astype(F32)
        for r in (l0_ref, l1_ref, l2_ref):
            acc = acc + r[...].astype(F32)
        stage[b % 2] = acc
        loc, rem = copies(b)
        loc.start()
        rem.start()

        @pl.when(b == nb - 1)
        def _():
            for step in range(max(0, nb - 2), nb):
                loc, rem = copies(step)
                loc.wait()
                rem.wait_send()
            theirs = place(1 - c, 0, Rp)
            _rcopy(theirs, theirs, ssem.at[0], rsem, sib).wait_recv()

    blk = (None, tr, Cp)
    grid_spec = pltpu.PrefetchScalarGridSpec(
        num_scalar_prefetch=1, grid=(nb,),
        in_specs=[pl.BlockSpec(blk, lambda b, me_ref: (me_ref[0], b, 0))]
        + [pl.BlockSpec(blk, functools.partial(lambda b, me_ref, k: (k, b, 0), k=k)) for k in range(3)],
        out_specs=ANY,
        scratch_shapes=[pltpu.VMEM((2, tr, Cp), F32), pltpu.SemaphoreType.DMA((2,)), pltpu.SemaphoreType.DMA,
                        pltpu.SemaphoreType.DMA((2,))])
    me = (2 * lax.axis_index("x") + lax.axis_index("y")).astype(jnp.int32).reshape(1)
    return pl.pallas_call(
        body, name=name, grid_spec=grid_spec, out_shape=jax.ShapeDtypeStruct((R, C), F32),
        compiler_params=_params("arbitrary"))(me, h, landed, landed, landed)


def _pair_share(name, land):
    G, R, C = land.shape
    Rh = R // 2
    tr = _tile_rows(Rh, max(16, EXCHANGE_CHUNK_BYTES // (2 * C)), 16)
    chunks = [(k, b) for k in range(3) for b in range(Rh // tr)]

    def body(src, dst, buf, lsem, ssem, rsem):
        x, y, c, _, _, chip_idx = _place()
        sib = (x, y, 1 - c)

        def region(ref, k, half, r0, rows):
            return ref.at[chip_idx[k], pl.ds(pl.multiple_of(half * Rh + r0, 16), rows)]

        def load(t):
            k, b = chunks[t]
            return pltpu.make_async_copy(region(src, k, c, b * tr, tr), buf.at[t % 2], lsem.at[t % 2])

        def send(t):
            k, b = chunks[t]
            return _rcopy(buf.at[t % 2], region(dst, k, c, b * tr, tr), ssem.at[t % 2], rsem.at[k], sib)

        load(0).start()
        for t in range(len(chunks)):
            load(t).wait()
            if t + 1 < len(chunks):
                if t >= 1:
                    send(t - 1).wait_send()
                load(t + 1).start()
            send(t).start()
        for t in range(max(0, len(chunks) - 2), len(chunks)):
            send(t).wait_send()
        for k in range(3):
            theirs = region(dst, k, 1 - c, 0, Rh)
            _rcopy(theirs, theirs, ssem.at[0], rsem.at[k], sib).wait_recv()

    return pl.pallas_call(
        body, name=name, in_specs=[ANY], out_specs=ANY, out_shape=jax.ShapeDtypeStruct(land.shape, land.dtype),
        input_output_aliases={0: 0},
        scratch_shapes=[pltpu.VMEM((2, tr, C), land.dtype), pltpu.SemaphoreType.DMA((2,)), pltpu.SemaphoreType.DMA((2,)),
                        pltpu.SemaphoreType.DMA((3,))],
    )(land)


def _allreduce_small(v):
    R, K = v.shape
    ndev = 8

    def body(v_ref, o_ref, land, ssem, rsem):
        x, y, c = lax.axis_index("x"), lax.axis_index("y"), lax.axis_index("c")
        me = 4 * x + 2 * y + c
        land[me] = v_ref[...]
        cps = []
        for r in range(1, ndev):
            fx, fy, fc = (r >> 2) & 1, (r >> 1) & 1, r & 1
            peer = (x ^ fx, y ^ fy, c ^ fc)
            cp = _rcopy(v_ref, land.at[me], ssem.at[r - 1], rsem.at[r - 1], peer)
            cp.start()
            cps.append((cp, 4 * peer[0] + 2 * peer[1] + peer[2], r))
        for cp, src, r in cps:
            cp.wait_send()
            _rcopy(v_ref, land.at[src], ssem.at[r - 1], rsem.at[r - 1], (x, y, c)).wait_recv()
        acc = land[0]
        for d in range(1, ndev):
            acc = acc + land[d]
        o_ref[...] = acc

    vm = pl.BlockSpec(memory_space=pltpu.VMEM)
    return pl.pallas_call(
        body, name="allreduce_small", in_specs=[vm], out_specs=vm, out_shape=jax.ShapeDtypeStruct((R, K), F32),
        scratch_shapes=[pltpu.VMEM((ndev, R, K), F32), pltpu.SemaphoreType.DMA((ndev - 1,)), pltpu.SemaphoreType.DMA((ndev - 1,))],
    )(v)


IN_SPLITS = (Q_RANK, KV_RANK, QK_ROPE, DIL_HEADS * HEAD, DIL_HEADS * HEAD, DIL_HEADS * HEAD, D_MODEL, D_MODEL)
IN_OFF = tuple(int(v) for v in np.cumsum((0,) + IN_SPLITS))


def _unshard_cols(g):
    G, K, Ns = g.shape
    return g.transpose(1, 0, 2).reshape(K, G * Ns)


def _shard_cols(w):
    K, N = w.shape
    return w.reshape(K, N_CHIPS, N // N_CHIPS).transpose(1, 0, 2)


def _rope_pad(w):
    half = QK_ROPE // 2
    z = jnp.zeros(w.shape[:-1] + (half,), w.dtype)
    return jnp.concatenate([w[..., :half], z, w[..., half:], z], axis=-1)


def _rope_unpad(w):
    half = QK_ROPE // 2
    return jnp.concatenate([w[..., :half], w[..., 2 * half:3 * half]], axis=-1)


def _split_w_in(w_in_g):
    w = _unshard_cols(w_in_g)
    K = w.shape[0]
    p = [w[:, IN_OFF[i]:IN_OFF[i + 1]] for i in range(8)]
    w_lat = jnp.concatenate([p[0], p[1], _rope_pad(p[2]), jnp.zeros((K, LAT_W - _KPE.stop), w.dtype)], axis=1)
    w_dil = [jnp.concatenate([p[3 + t][:, g * DIL_O:(g + 1) * DIL_O] for t in range(3)], axis=1) for g in range(DIL_GROUPS)]
    w_gate = jnp.concatenate([p[6], p[7]], axis=1)
    return w_lat, w_dil, w_gate


def _merge_dw_in(dw_lat, dw_dil, dw_gate):
    parts = [dw_lat[:, _CQ], dw_lat[:, _CKV], _rope_unpad(dw_lat[:, _KPE])]
    for t in range(3):
        parts += [dw_dil[g][:, t * DIL_O:(t + 1) * DIL_O] for g in range(DIL_GROUPS)]
    parts.append(dw_gate)
    return _shard_cols(jnp.concatenate(parts, axis=1))


def _split_w_uq(w_uq_g):
    w = _unshard_cols(w_uq_g)
    K = w.shape[0]
    w = w.reshape(K, MLA_HEADS, QK_NOPE + QK_ROPE)
    return w[:, :, :QK_NOPE].reshape(K, MLA_HEADS * HEAD), _rope_pad(w[:, :, QK_NOPE:]).reshape(K, MLA_HEADS * HEAD)


def _merge_dw_uq(dw_n, dw_p):
    K = dw_n.shape[0]
    w = jnp.concatenate([dw_n.reshape(K, MLA_HEADS, HEAD), _rope_unpad(dw_p.reshape(K, MLA_HEADS, HEAD))], axis=-1)
    return _shard_cols(w.reshape(K, MLA_HEADS * (QK_NOPE + QK_ROPE)))


def _split_w_ukv(w_ukv_g):
    w = _unshard_cols(w_ukv_g)
    K = w.shape[0]
    w = w.reshape(K, MLA_HEADS, 2 * HEAD)
    return w[:, :, :HEAD].reshape(K, MLA_HEADS * HEAD), w[:, :, HEAD:].reshape(K, MLA_HEADS * HEAD)


def _merge_dw_ukv(dw_k, dw_v):
    K = dw_k.shape[0]
    w = jnp.concatenate([dw_k.reshape(K, MLA_HEADS, HEAD), dw_v.reshape(K, MLA_HEADS, HEAD)], axis=-1)
    return _shard_cols(w.reshape(K, MLA_HEADS * 2 * HEAD))


GATHER_GROUPS = (("w_in",), ("w_uq", "w_ukv", "w_o_mla", "w_o_dil", "w_out"), ("w_up", "w_down", "conv_w"))
SHARED_FETCH = ("w_in",)
REDUCE_GROUPS = (("w_down", "w_up"), ("w_out", "w_o_mla", "w_o_dil"), ("w_uq", "w_ukv", "w_in"))


def _local_step(x, tgt, W, fetch, emit):
    S, D = x.shape
    cos, sin_s = _rope_tables(S)
    w_lat, w_dil, w_gate = _split_w_in(fetch(0, x)["w_in"])

    h = _rmsnorm_fwd("attn_norm", x, W["attn_norm_g"])
    lat = _mm_nn("proj_lat", h, w_lat)
    qkv = [_mm_nn(f"proj_dil{g}", h, w_dil[g], o_dtype=BF) for g in range(DIL_GROUPS)]
    gpre = _mm_nn("proj_gate", h, w_gate, o_dtype=BF)
    WB = fetch(1, gpre)
    w_uqn, w_uqp = _split_w_uq(WB["w_uq"])
    w_k, w_v = _split_w_ukv(WB["w_ukv"])
    w_o_mla, w_o_dil = WB["w_o_mla"], WB["w_o_dil"]
    w_out = WB["w_out"].reshape(D, D)
    qn_, kvn, kpe = _mla_prep(lat, W["q_norm_g"], W["kv_norm_g"], cos, sin_s)
    q_nope = _mm_nn("q_nope", qn_, w_uqn, o_dtype=BF)
    q_pe = _rope("q_rope", _mm_nn("q_pe", qn_, w_uqp), cos, sin_s, False)
    k_nope = _mm_nn("k_nope", kvn, w_k, o_dtype=BF)
    v_mla = _mm_nn("v_mla", kvn, w_v, o_dtype=BF)
    attn_a, lse_a = _mla_fwd(q_nope, q_pe, k_nope, kpe, v_mla)
    dil = [_dil_fwd(qkv[g], g) for g in range(DIL_GROUPS)]
    attn_b, lse_b = _dil_combine([o for o, _ in dil], [l for _, l in dil])
    o_a = _mm_nn("o_mla", attn_a, w_o_mla, o_dtype=BF)
    o_b = _mm_nn("o_dil", attn_b, w_o_dil, o_dtype=BF)
    merge = _merge_fwd(gpre, W["b_gate"], o_a, o_b)
    x1 = _mm_nn("out_proj", merge, w_out, add=x)
    WC = fetch(2, merge)
    w_up = WC["w_up"]
    G4, _, C = w_up.shape
    w_down = WC["w_down"].reshape(G4 // 2, C, D)
    conv_w = WC["conv_w"]
    conv_b = W["conv_b"].reshape(G4, 1, C)
    h2 = _rmsnorm_fwd("ffn_norm", x1, W["ffn_norm_g"])
    u_pre = _up_fwd(h2, w_up)
    act, u = _ffn_act(u_pre, conv_w, conv_b)
    x2 = _down_fwd(act, w_down, x1)
    dx2, d_final_g, loss8 = _final_loss(x2, tgt, W["final_norm_g"])

    d_act = _down_dgrad(dx2, w_down)
    dw_down = _down_wgrad(act, dx2)
    du = _ffn_act_bwd(u, d_act)
    du_pre, d_conv_w, d_conv_b = _conv_bwd(du, u_pre, conv_w)
    dh2 = _up_dgrad(du_pre, w_up)
    dw_up = _up_wgrad(h2, du_pre)
    zero = emit(0, {"w_down": dw_down.reshape(N_CHIPS, (G4 // 2) * C // N_CHIPS, D), "w_up": dw_up})
    dx1, d_ffn_g = _rmsnorm_bwd("ffn_norm_bwd", dh2, x1, W["ffn_norm_g"] + zero, dx2)
    d_merge = _mm_nt("out_proj_dgrad", dx1, w_out, o_dtype=BF)
    dw_out = _mm_tn("out_proj_wgrad", merge, dx1)
    d_oa, d_ob, d_gpre, d_b_gate = _merge_bwd(d_merge, gpre, W["b_gate"], o_a, o_b)
    d_attn_a = _mm_nt("o_mla_dgrad", d_oa, w_o_mla, o_dtype=BF)
    dw_o_mla = _mm_tn("o_mla_wgrad", attn_a, d_oa, shards=N_CHIPS)
    d_attn_b = _mm_nt("o_dil_dgrad", d_ob, w_o_dil, o_dtype=BF)
    dw_o_dil = _mm_tn("o_dil_wgrad", attn_b, d_ob, shards=N_CHIPS)
    zero = emit(1, {"w_out": dw_out.reshape(N_CHIPS, D // N_CHIPS, D), "w_o_mla": dw_o_mla, "w_o_dil": dw_o_dil})
    q_norm_g = W["q_norm_g"] + zero
    delta_b = _dil_delta(d_attn_b, attn_b)
    d_qkv = [_dil_bwd(qkv[g], d_attn_b, lse_b, delta_b, g) for g in range(DIL_GROUPS)]
    dq_nope, dq_pe_rot, dk_nope, dv_mla, dkpe_rot = _mla_bwd(q_nope, q_pe, k_nope, kpe, v_mla, d_attn_a, attn_a, lse_a)
    dq_pe = _rope("q_rope_bwd", dq_pe_rot, cos, sin_s, True)
    d_qn = _mm_nt("q_pe_dgrad", dq_pe, w_uqp, add=_mm_nt("q_nope_dgrad", dq_nope, w_uqn))
    d_kvn = _mm_nt("v_dgrad", dv_mla, w_v, add=_mm_nt("k_nope_dgrad", dk_nope, w_k))
    dw_uq = _merge_dw_uq(_mm_tn("q_nope_wgrad", qn_, dq_nope), _mm_tn("q_pe_wgrad", qn_, dq_pe))
    dw_ukv = _merge_dw_ukv(_mm_tn("k_nope_wgrad", kvn, dk_nope), _mm_tn("v_wgrad", kvn, dv_mla))
    d_lat, d_q_g, d_kv_g = _mla_prep_bwd(lat, q_norm_g, W["kv_norm_g"], cos, sin_s, d_qn, d_kvn, dkpe_rot)
    dw_in = _merge_dw_in(_mm_tn("proj_lat_wgrad", h, d_lat),
                         [_mm_tn(f"proj_dil{g}_wgrad", h, d_qkv[g]) for g in range(DIL_GROUPS)],
                         _mm_tn("proj_gate_wgrad", h, d_gpre))
    zero = emit(2, {"w_uq": dw_uq, "w_ukv": dw_ukv, "w_in": dw_in})
    dh = _mm_nt("proj_lat_dgrad", d_lat, w_lat + zero.astype(BF))
    for g in range(DIL_GROUPS):
        dh = _mm_nt(f"proj_dil{g}_dgrad", d_qkv[g], w_dil[g], add=dh)
    dh = _mm_nt("proj_gate_dgrad", d_gpre, w_gate, add=dh)
    grad_x, d_attn_g = _rmsnorm_bwd("attn_norm_bwd", dh, x, W["attn_norm_g"], dx1)

    small = {"attn_norm_g": d_attn_g, "b_gate": d_b_gate, "q_norm_g": d_q_g, "kv_norm_g": d_kv_g,
             "ffn_norm_g": d_ffn_g, "conv_w": d_conv_w, "conv_b": d_conv_b.reshape(1, G4 * C),
             "final_norm_g": d_final_g}
    return loss8[0, 0], grad_x, small


BIG = ("w_in", "w_uq", "w_ukv", "w_o_mla", "w_o_dil", "w_out", "w_up", "w_down")
SMALL = ("attn_norm_g", "b_gate", "q_norm_g", "kv_norm_g", "ffn_norm_g", "conv_w", "conv_b", "final_norm_g")
WEIGHTS = ("attn_norm_g", "w_in", "b_gate", "q_norm_g", "w_uq", "kv_norm_g", "w_ukv", "w_o_mla", "w_o_dil",
           "w_out", "ffn_norm_g", "w_up", "conv_w", "conv_b", "w_down", "final_norm_g")
SMALL_ROWS = 8
COLUMN_MAJOR = ("w_in", "w_up")
HALF_AXIS = {"w_down": 1}


def _gather_start(shards):
    names = [n for grp in GATHER_GROUPS for n in grp]
    chip = 2 * lax.axis_index("x") + lax.axis_index("y")
    c = lax.axis_index("c")
    srcs, lands = [], []
    for n in names:
        s = shards[n]
        lands.append(lax.dynamic_update_slice(lax.empty((N_CHIPS,) + s.shape, s.dtype), s[None], (chip, 0, 0)))
        if n in SHARED_FETCH:
            s = lax.dynamic_slice_in_dim(s, c * (s.shape[0] // 2), s.shape[0] // 2, 0)
        srcs.append(s)
    groups, at = [], 0
    for grp in GATHER_GROUPS:
        groups.append(list(range(at, at + len(grp))))
        at += len(grp)
    sems, srcs, lands, token = _exchange_start("gather_start", "gather", srcs, lands, groups)

    def fetch(i, after):
        idx = groups[i]
        _, got = _exchange_wait(f"gather_wait{i}", "gather", [srcs[j] for j in idx], [lands[j] for j in idx], sems[i], after)
        return {n: _pair_share(f"pair_share_{n}", g) if n in SHARED_FETCH else g for n, g in zip(GATHER_GROUPS[i], got)}

    return fetch, token[0, 0]


def _reduce_start(i, grads):
    names = REDUCE_GROUPS[i]
    hs = [_pair_sum(f"pair_sum_{n}", grads[n], HALF_AXIS.get(n, 0)) for n in names]
    lands = [lax.empty((3,) + h.shape[1:], h.dtype) for h in hs]
    sems, hs, lands, token = _exchange_start(f"reduce_start{i}", "scatter", hs, lands, [list(range(len(names)))])
    return (sems[0], hs, lands), token[0, 0]


def _reduce_finish(i, pending, after):
    sems, hs, lands = pending
    hs, lands = _exchange_wait(f"reduce_wait{i}", "scatter", hs, lands, sems, after)
    out = {}
    for n, h, landed in zip(REDUCE_GROUPS[i], hs, lands):
        out[n] = _chip_total_join(f"chip_total_{n}", h, landed, HALF_AXIS.get(n, 0))
    return out


def _reduce_small(small):
    flat = [small[n].reshape(-1) for n in SMALL]
    sizes = [f.shape[0] for f in flat]
    total = sum(sizes)
    width = -(-total // (SMALL_ROWS * LANE)) * LANE
    packed = jnp.concatenate(flat + [jnp.zeros((SMALL_ROWS * width - total,), F32)]).reshape(SMALL_ROWS, width)
    red = _allreduce_small(packed).reshape(-1)
    out, off = {}, 0
    for n, s in zip(SMALL, sizes):
        out[n] = red[off:off + s]
        off += s
    return out


def kernel(x, attn_norm_g, w_in, b_gate, q_norm_g, w_uq, kv_norm_g, w_ukv, w_o_mla, w_o_dil, w_out, ffn_norm_g, w_up, conv_w, conv_b, w_down, final_norm_g, loss_target, m_attn_norm_g, m_w_in, m_b_gate, m_q_norm_g, m_w_uq, m_kv_norm_g, m_w_ukv, m_w_o_mla, m_w_o_dil, m_w_out, m_ffn_norm_g, m_w_up, m_conv_w, m_conv_b, m_w_down, m_final_norm_g, v_attn_norm_g, v_w_in, v_b_gate, v_q_norm_g, v_w_uq, v_kv_norm_g, v_w_ukv, v_w_o_mla, v_w_o_dil, v_w_out, v_ffn_norm_g, v_w_up, v_conv_w, v_conv_b, v_w_down, v_final_norm_g):
    given = dict(attn_norm_g=attn_norm_g, w_in=w_in, b_gate=b_gate, q_norm_g=q_norm_g, w_uq=w_uq, kv_norm_g=kv_norm_g,
                 w_ukv=w_ukv, w_o_mla=w_o_mla, w_o_dil=w_o_dil, w_out=w_out, ffn_norm_g=ffn_norm_g, w_up=w_up,
                 conv_w=conv_w, conv_b=conv_b, w_down=w_down, final_norm_g=final_norm_g)
    moments_m = dict(attn_norm_g=m_attn_norm_g, w_in=m_w_in, b_gate=m_b_gate, q_norm_g=m_q_norm_g, w_uq=m_w_uq,
                     kv_norm_g=m_kv_norm_g, w_ukv=m_w_ukv, w_o_mla=m_w_o_mla, w_o_dil=m_w_o_dil, w_out=m_w_out,
                     ffn_norm_g=m_ffn_norm_g, w_up=m_w_up, conv_w=m_conv_w, conv_b=m_conv_b, w_down=m_w_down,
                     final_norm_g=m_final_norm_g)
    moments_v = dict(attn_norm_g=v_attn_norm_g, w_in=v_w_in, b_gate=v_b_gate, q_norm_g=v_q_norm_g, w_uq=v_w_uq,
                     kv_norm_g=v_kv_norm_g, w_ukv=v_w_ukv, w_o_mla=v_w_o_mla, w_o_dil=v_w_o_dil, w_out=v_w_out,
                     ffn_norm_g=v_ffn_norm_g, w_up=v_w_up, conv_w=v_conv_w, conv_b=v_conv_b, w_down=v_w_down,
                     final_norm_g=v_final_norm_g)

    shards = {n: given[n][0].astype(BF) for n in BIG}
    shards["conv_w"] = given["conv_w"][0]
    fetch, zero = _gather_start(shards)
    W = {n: given[n] for n in ("b_gate", "q_norm_g", "kv_norm_g", "ffn_norm_g", "conv_b")}
    W["attn_norm_g"] = given["attn_norm_g"] + zero
    W["final_norm_g"] = given["final_norm_g"].reshape(1, -1)

    pending = {}

    def emit(i, grads):
        pending[i], token = _reduce_start(i, grads)
        return token

    loss_part, grad_x, small = _local_step(x[0], loss_target[0], W, fetch, emit)
    loss = lax.psum(loss_part, ("x", "y", "c"))
    grads, delta, new_m, new_v = {}, {}, {}, {}

    def adamw(n, g):
        shp = given[n].shape
        two_d = (-1, shp[-1]) if len(shp) > 1 else (1, -1)
        view = (lambda a: a.reshape(two_d).T) if n in COLUMN_MAJOR else (lambda a: a.reshape(two_d))
        back = (lambda a: a.T.reshape(shp)) if n in COLUMN_MAJOR else (lambda a: a.reshape(shp))
        go, d, nm, nv = _adamw(f"adamw_{n}", view(given[n]), view(g), view(moments_m[n]), view(moments_v[n]))
        grads[n], delta[n], new_m[n], new_v[n] = back(go), back(d), back(nm), back(nv)

    after = grad_x
    for i in range(len(REDUCE_GROUPS)):
        for n, g in _reduce_finish(i, pending[i], after).items():
            adamw(n, g)
        after = delta[REDUCE_GROUPS[i][-1]]
    g_small = _reduce_small(small)
    chip = 2 * lax.axis_index("x") + lax.axis_index("y")
    for n in SMALL:
        if n == "conv_w":
            full = g_small[n].reshape(N_CHIPS, 3, -1)
            adamw(n, lax.dynamic_index_in_dim(full, chip, 0, keepdims=True))
        else:
            adamw(n, g_small[n])

    return (loss, grad_x[None], *[grads[n] for n in WEIGHTS], *[delta[n] for n in WEIGHTS],
            *[new_m[n] for n in WEIGHTS], *[new_v[n] for n in WEIGHTS])
```

```python
import functools
import math

import numpy as np
import jax
import jax.numpy as jnp
from jax import lax
from jax.experimental import pallas as pl
from jax.experimental.pallas import tpu as pltpu

F32 = jnp.float32
BF = jnp.bfloat16
MESH = pl.DeviceIdType.MESH

D_MODEL = 2048
MLA_HEADS = 8
QK_NOPE = 128
QK_ROPE = 64
Q_RANK = 512
KV_RANK = 256
ROPE_THETA = 10000.0
DIL_PATTERNS = ((128, 1), (512, 4), (2048, 16))
DIL_GROUPS = 3
DIL_HPG = 4
DIL_HEADS = 12
HEAD = 128
DIL_BLOCK = 128
ALIBI_MAX_BIAS = 8.0
NORM_EPS = 1e-6
N_CHIPS = 4
ADAM_LR = 0.001
ADAM_B1 = 0.9
ADAM_B2 = 0.999
ADAM_EPS = 1e-08
ADAM_WD = 0.01
ADAM_STEP = 10

LANE = 128
VMEM_LIMIT = 56 * 1024 * 1024
MLA_SCALE = (QK_NOPE + QK_ROPE) ** -0.5
DIL_SCALE = HEAD ** -0.5


def _params(*sem):
    return pltpu.CompilerParams(dimension_semantics=sem, vmem_limit_bytes=VMEM_LIMIT)


def _tile(n, pref):
    t = (pref // LANE) * LANE
    while t >= LANE:
        if n % t == 0:
            return t
        t -= LANE
    return n


NN = (((1,), (0,)), ((), ()))
NT = (((1,), (1,)), ((), ()))
TN = (((0,), (0,)), ((), ()))


def _mm_call(name, a, b, add, *, grid, a_spec, b_spec, add_spec, o_spec, o_shape, o_dtype, acc_shape, dims, nk):
    nax = len(grid)

    def body(*refs):
        if add is None:
            a_ref, b_ref, o_ref = refs[:3]
            c_ref = None
            scr = refs[3:]
        else:
            a_ref, b_ref, c_ref, o_ref = refs[:4]
            scr = refs[4:]
        prod = lax.dot_general(a_ref[...].astype(BF), b_ref[...].astype(BF), dims, preferred_element_type=F32)
        if nk == 1:
            if c_ref is not None:
                prod = prod + c_ref[...]
            o_ref[...] = prod.astype(o_ref.dtype)
        else:
            acc = scr[0]
            k = pl.program_id(nax - 1)

            @pl.when(k == 0)
            def _():
                if c_ref is not None:
                    acc[...] = prod + c_ref[...]
                else:
                    acc[...] = prod

            @pl.when(k > 0)
            def _():
                acc[...] += prod

            @pl.when(k == nk - 1)
            def _():
                o_ref[...] = acc[...].astype(o_ref.dtype)

    ins = [a, b] + ([] if add is None else [add])
    specs = [a_spec, b_spec] + ([] if add is None else [add_spec])
    sem = ("parallel",) * (nax - 1) + ("arbitrary",)
    return pl.pallas_call(
        body, name=name, grid=grid, in_specs=specs, out_specs=o_spec,
        out_shape=jax.ShapeDtypeStruct(o_shape, o_dtype),
        scratch_shapes=[] if nk == 1 else [pltpu.VMEM(acc_shape, F32)],
        compiler_params=_params(*sem),
    )(*ins)


def _mm_nn(name, a, b, *, add=None, o_dtype=F32):
    M, K = a.shape
    sharded = b.ndim == 3
    Ns = b.shape[-1]
    N = Ns * (b.shape[0] if sharded else 1)
    tm, tn, tk = _tile(M, 1024), _tile(Ns, 1024), _tile(K, 2048)
    per = Ns // tn
    nk = K // tk
    if sharded:
        b_spec = pl.BlockSpec((None, tk, tn), lambda i, j, k: (j // per, k, j % per))
    else:
        b_spec = pl.BlockSpec((tk, tn), lambda i, j, k: (k, j))
    return _mm_call(
        name, a, b, add, grid=(M // tm, N // tn, nk),
        a_spec=pl.BlockSpec((tm, tk), lambda i, j, k: (i, k)), b_spec=b_spec,
        add_spec=pl.BlockSpec((tm, tn), lambda i, j, k: (i, j)),
        o_spec=pl.BlockSpec((tm, tn), lambda i, j, k: (i, j)),
        o_shape=(M, N), o_dtype=o_dtype, acc_shape=(tm, tn), dims=NN, nk=nk)


def _mm_nt(name, a, b, *, add=None, o_dtype=F32):
    M, K = a.shape
    sharded = b.ndim == 3
    N, Ks = b.shape[-2], b.shape[-1]
    tm, tn, tk = _tile(M, 1024), _tile(N, 1024), _tile(Ks, 2048)
    per = Ks // tk
    nk = K // tk
    if sharded:
        b_spec = pl.BlockSpec((None, tn, tk), lambda i, j, k: (k // per, j, k % per))
    else:
        b_spec = pl.BlockSpec((tn, tk), lambda i, j, k: (j, k))
    return _mm_call(
        name, a, b, add, grid=(M // tm, N // tn, nk),
        a_spec=pl.BlockSpec((tm, tk), lambda i, j, k: (i, k)), b_spec=b_spec,
        add_spec=pl.BlockSpec((tm, tn), lambda i, j, k: (i, j)),
        o_spec=pl.BlockSpec((tm, tn), lambda i, j, k: (i, j)),
        o_shape=(M, N), o_dtype=o_dtype, acc_shape=(tm, tn), dims=NT, nk=nk)


def _mm_tn(name, a, b, *, shards=1, o_dtype=BF):
    S, M = a.shape
    N = b.shape[1]
    Ns = N // shards
    tm, tn, tk = _tile(M, 1024), _tile(Ns, 1024), _tile(S, 2048)
    per = Ns // tn
    nk = S // tk
    if shards > 1:
        o_spec = pl.BlockSpec((None, tm, tn), lambda i, j, k: (j // per, i, j % per))
        o_shape = (shards, M, Ns)
    else:
        o_spec = pl.BlockSpec((tm, tn), lambda i, j, k: (i, j))
        o_shape = (M, N)
    return _mm_call(
        name, a, b, None, grid=(M // tm, N // tn, nk),
        a_spec=pl.BlockSpec((tk, tm), lambda i, j, k: (k, i)),
        b_spec=pl.BlockSpec((tk, tn), lambda i, j, k: (k, j)),
        add_spec=None, o_spec=o_spec, o_shape=o_shape, o_dtype=o_dtype, acc_shape=(tm, tn), dims=TN, nk=nk)


def _up_fwd(h2, w_up):
    S, D = h2.shape
    G, _, C = w_up.shape
    tm = _tile(S, 512)
    return _mm_call(
        "up_fwd", h2, w_up, None, grid=(G, S // tm, 1),
        a_spec=pl.BlockSpec((tm, D), lambda g, i, k: (i, 0)),
        b_spec=pl.BlockSpec((None, D, C), lambda g, i, k: (g, 0, 0)),
        add_spec=None, o_spec=pl.BlockSpec((None, tm, C), lambda g, i, k: (g, i, 0)),
        o_shape=(G, S, C), o_dtype=BF, acc_shape=None, dims=NN, nk=1)


def _up_dgrad(du_pre, w_up):
    G, S, C = du_pre.shape
    D = w_up.shape[1]
    tm, tn = _tile(S, 1024), _tile(D, 1024)
    return _mm_call(
        "up_dgrad", du_pre, w_up, None, grid=(S // tm, D // tn, G),
        a_spec=pl.BlockSpec((None, tm, C), lambda i, j, g: (g, i, 0)),
        b_spec=pl.BlockSpec((None, tn, C), lambda i, j, g: (g, j, 0)),
        add_spec=None, o_spec=pl.BlockSpec((tm, tn), lambda i, j, g: (i, j)),
        o_shape=(S, D), o_dtype=F32, acc_shape=(tm, tn), dims=NT, nk=G)


def _up_wgrad(h2, du_pre):
    G, S, C = du_pre.shape
    D = h2.shape[1]
    tm, tk = _tile(D, 512), _tile(S, 2048)
    return _mm_call(
        "up_wgrad", h2, du_pre, None, grid=(G, D // tm, S // tk),
        a_spec=pl.BlockSpec((tk, tm), lambda g, i, k: (k, i)),
        b_spec=pl.BlockSpec((None, tk, C), lambda g, i, k: (g, k, 0)),
        add_spec=None, o_spec=pl.BlockSpec((None, tm, C), lambda g, i, k: (g, i, 0)),
        o_shape=(G, D, C), o_dtype=BF, acc_shape=(tm, C), dims=TN, nk=S // tk)


def _down_fwd(act, w_down, x1):
    G, S, C = act.shape
    D = w_down.shape[2]
    tm, tn = _tile(S, 1024), _tile(D, 1024)
    return _mm_call(
        "down_fwd", act, w_down, x1, grid=(S // tm, D // tn, G),
        a_spec=pl.BlockSpec((None, tm, C), lambda i, j, g: (g, i, 0)),
        b_spec=pl.BlockSpec((None, C, tn), lambda i, j, g: (g, 0, j)),
        add_spec=pl.BlockSpec((tm, tn), lambda i, j, g: (i, j)),
        o_spec=pl.BlockSpec((tm, tn), lambda i, j, g: (i, j)),
        o_shape=(S, D), o_dtype=F32, acc_shape=(tm, tn), dims=NN, nk=G)


def _down_dgrad(dx2, w_down):
    S, D = dx2.shape
    G, C, _ = w_down.shape
    tm = _tile(S, 512)
    return _mm_call(
        "down_dgrad", dx2, w_down, None, grid=(G, S // tm, 1),
        a_spec=pl.BlockSpec((tm, D), lambda g, i, k: (i, 0)),
        b_spec=pl.BlockSpec((None, C, D), lambda g, i, k: (g, 0, 0)),
        add_spec=None, o_spec=pl.BlockSpec((None, tm, C), lambda g, i, k: (g, i, 0)),
        o_shape=(G, S, C), o_dtype=BF, acc_shape=None, dims=NT, nk=1)


def _down_wgrad(act, dx2):
    G, S, C = act.shape
    D = dx2.shape[1]
    tn, tk = _tile(D, 512), _tile(S, 1024)
    return _mm_call(
        "down_wgrad", act, dx2, None, grid=(G, D // tn, S // tk),
        a_spec=pl.BlockSpec((None, tk, C), lambda g, j, k: (g, k, 0)),
        b_spec=pl.BlockSpec((tk, tn), lambda g, j, k: (k, j)),
        add_spec=None, o_spec=pl.BlockSpec((None, C, tn), lambda g, j, k: (g, 0, j)),
        o_shape=(G, C, D), o_dtype=BF, acc_shape=(C, tn), dims=TN, nk=S // tk)


def _row(ts, c):
    return pl.BlockSpec((ts, c), lambda i: (i, 0))


def _bcast(r, c):
    return pl.BlockSpec((r, c), lambda i: (0, 0))


def _accumulate(i, ref, val):
    @pl.when(i == 0)
    def _():
        ref[...] = val

    @pl.when(i > 0)
    def _():
        ref[...] += val


def _rstd(xv):
    return lax.rsqrt(jnp.mean(xv * xv, axis=-1, keepdims=True) + NORM_EPS)


def _rmsnorm_fwd(name, x, g):
    S, D = x.shape
    ts = _tile(S, 512)

    def body(x_ref, g_ref, o_ref):
        xv = x_ref[...]
        o_ref[...] = (xv * _rstd(xv) * g_ref[...]).astype(o_ref.dtype)

    return pl.pallas_call(
        body, name=name, grid=(S // ts,), in_specs=[_row(ts, D), _bcast(1, D)], out_specs=_row(ts, D),
        out_shape=jax.ShapeDtypeStruct((S, D), BF), compiler_params=_params("parallel"))(x, g)


def _norm_bwd_rows(dy, xv, g):
    r = _rstd(xv)
    xh = xv * r
    dxh = dy * g
    dx = r * (dxh - xh * jnp.mean(dxh * xh, axis=-1, keepdims=True))
    return dx, jnp.sum(dy * xh, axis=0, keepdims=True)


def _rmsnorm_bwd(name, dy, x, g, res):
    S, D = x.shape
    ts = _tile(S, 512)

    def body(dy_ref, x_ref, g_ref, res_ref, dx_ref, dg_ref):
        dx, dg = _norm_bwd_rows(dy_ref[...], x_ref[...], g_ref[...])
        dx_ref[...] = dx + res_ref[...]
        _accumulate(pl.program_id(0), dg_ref, dg)

    return pl.pallas_call(
        body, name=name, grid=(S // ts,),
        in_specs=[_row(ts, D), _row(ts, D), _bcast(1, D), _row(ts, D)],
        out_specs=[_row(ts, D), _bcast(1, D)],
        out_shape=[jax.ShapeDtypeStruct((S, D), F32), jax.ShapeDtypeStruct((1, D), F32)],
        compiler_params=_params("arbitrary"))(dy, x, g, res)


def _rope_tables(S):
    half = QK_ROPE // 2
    pos = jnp.arange(S, dtype=F32)
    inv_freq = ROPE_THETA ** (-jnp.arange(0, QK_ROPE, 2, dtype=F32) / QK_ROPE)
    ang = pos[:, None] * inv_freq[None, :]
    cos, sin = jnp.cos(ang), jnp.sin(ang)
    z = jnp.zeros((S, half), F32)
    return jnp.concatenate([cos, z, cos, z], axis=1), jnp.concatenate([-sin, z, sin, z], axis=1)


def _rope_lanes(x, cos, sin_signed, inverse):
    if inverse:
        return x * cos + pltpu.roll(x * sin_signed, LANE // 2, 1)
    return x * cos + pltpu.roll(x, LANE // 2, 1) * sin_signed


def _rope(name, x, cos, sin_signed, inverse):
    S, W = x.shape
    ts = _tile(S, 512)

    def body(x_ref, c_ref, s_ref, o_ref):
        c, s = c_ref[...], s_ref[...]
        for h in range(W // LANE):
            sl = slice(h * LANE, (h + 1) * LANE)
            o_ref[:, sl] = _rope_lanes(x_ref[:, sl], c, s, inverse).astype(o_ref.dtype)

    return pl.pallas_call(
        body, name=name, grid=(S // ts,), in_specs=[_row(ts, W), _row(ts, LANE), _row(ts, LANE)],
        out_specs=_row(ts, W), out_shape=jax.ShapeDtypeStruct((S, W), BF),
        compiler_params=_params("parallel"))(x, cos, sin_signed)


LAT_W = 1024
_CQ = slice(0, Q_RANK)
_CKV = slice(Q_RANK, Q_RANK + KV_RANK)
_KPE = slice(Q_RANK + KV_RANK, Q_RANK + KV_RANK + LANE)


def _mla_prep(lat, qg, kvg, cos, sin_signed):
    S = lat.shape[0]
    ts = _tile(S, 512)

    def body(lat_ref, qg_ref, kvg_ref, c_ref, s_ref, qn_ref, kvn_ref, kpe_ref):
        cq = lat_ref[:, _CQ]
        qn_ref[...] = (cq * _rstd(cq) * qg_ref[...]).astype(BF)
        ckv = lat_ref[:, _CKV]
        kvn_ref[...] = (ckv * _rstd(ckv) * kvg_ref[...]).astype(BF)
        kpe_ref[...] = _rope_lanes(lat_ref[:, _KPE], c_ref[...], s_ref[...], False).astype(BF)

    return pl.pallas_call(
        body, name="mla_prep", grid=(S // ts,),
        in_specs=[_row(ts, LAT_W), _bcast(1, Q_RANK), _bcast(1, KV_RANK), _row(ts, LANE), _row(ts, LANE)],
        out_specs=[_row(ts, Q_RANK), _row(ts, KV_RANK), _row(ts, LANE)],
        out_shape=[jax.ShapeDtypeStruct((S, Q_RANK), BF), jax.ShapeDtypeStruct((S, KV_RANK), BF),
                   jax.ShapeDtypeStruct((S, LANE), BF)],
        compiler_params=_params("parallel"))(lat, qg, kvg, cos, sin_signed)


def _mla_prep_bwd(lat, qg, kvg, cos, sin_signed, d_qn, d_kvn, d_kpe):
    S = lat.shape[0]
    ts = _tile(S, 512)

    def body(lat_ref, qg_ref, kvg_ref, c_ref, s_ref, dqn_ref, dkvn_ref, dkpe_ref, dlat_ref, dqg_ref, dkvg_ref):
        i = pl.program_id(0)
        dcq, dqg = _norm_bwd_rows(dqn_ref[...], lat_ref[:, _CQ], qg_ref[...])
        dckv, dkvg = _norm_bwd_rows(dkvn_ref[...], lat_ref[:, _CKV], kvg_ref[...])
        dlat_ref[:, _CQ] = dcq.astype(BF)
        dlat_ref[:, _CKV] = dckv.astype(BF)
        dkpe = dkpe_ref[0]
        for g in range(1, d_kpe.shape[0]):
            dkpe = dkpe + dkpe_ref[g]
        dlat_ref[:, _KPE] = _rope_lanes(dkpe, c_ref[...], s_ref[...], True).astype(BF)
        dlat_ref[:, _KPE.stop:] = jnp.zeros((ts, LAT_W - _KPE.stop), BF)
        _accumulate(i, dqg_ref, dqg)
        _accumulate(i, dkvg_ref, dkvg)

    return pl.pallas_call(
        body, name="mla_prep_bwd", grid=(S // ts,),
        in_specs=[_row(ts, LAT_W), _bcast(1, Q_RANK), _bcast(1, KV_RANK), _row(ts, LANE), _row(ts, LANE),
                  _row(ts, Q_RANK), _row(ts, KV_RANK), pl.BlockSpec((d_kpe.shape[0], ts, LANE), lambda i: (0, i, 0))],
        out_specs=[_row(ts, LAT_W), _bcast(1, Q_RANK), _bcast(1, KV_RANK)],
        out_shape=[jax.ShapeDtypeStruct((S, LAT_W), BF), jax.ShapeDtypeStruct((1, Q_RANK), F32),
                   jax.ShapeDtypeStruct((1, KV_RANK), F32)],
        compiler_params=_params("arbitrary"))(lat, qg, kvg, cos, sin_signed, d_qn, d_kvn, d_kpe)


def _sigmoid(z):
    return 1.0 / (1.0 + jnp.exp(-z))


def _merge_fwd(gpre, b_gate, o_a, o_b):
    S, D = o_a.shape
    ts = _tile(S, 256)

    def body(g_ref, b_ref, oa_ref, ob_ref, m_ref):
        ga = _sigmoid(g_ref[:, :D] + b_ref[:, :D])
        gb = _sigmoid(g_ref[:, D:] + b_ref[:, D:])
        m_ref[...] = (ga * oa_ref[...] + gb * ob_ref[...]).astype(BF)

    return pl.pallas_call(
        body, name="merge_fwd", grid=(S // ts,),
        in_specs=[_row(ts, 2 * D), _bcast(1, 2 * D), _row(ts, D), _row(ts, D)], out_specs=_row(ts, D),
        out_shape=jax.ShapeDtypeStruct((S, D), BF), compiler_params=_params("parallel"))(gpre, b_gate, o_a, o_b)


def _merge_bwd(d_merge, gpre, b_gate, o_a, o_b):
    S, D = o_a.shape
    ts = _tile(S, 256)

    def body(dm_ref, g_ref, b_ref, oa_ref, ob_ref, doa_ref, dob_ref, dg_ref, db_ref):
        dm = dm_ref[...]
        ga = _sigmoid(g_ref[:, :D] + b_ref[:, :D])
        gb = _sigmoid(g_ref[:, D:] + b_ref[:, D:])
        doa_ref[...] = (dm * ga).astype(BF)
        dob_ref[...] = (dm * gb).astype(BF)
        dga = dm * oa_ref[...] * ga * (1.0 - ga)
        dgb = dm * ob_ref[...] * gb * (1.0 - gb)
        dg_ref[:, :D] = dga.astype(BF)
        dg_ref[:, D:] = dgb.astype(BF)
        i = pl.program_id(0)
        part = jnp.concatenate([jnp.sum(dga, axis=0, keepdims=True), jnp.sum(dgb, axis=0, keepdims=True)], axis=1)
        _accumulate(i, db_ref, part)

    return pl.pallas_call(
        body, name="merge_bwd", grid=(S // ts,),
        in_specs=[_row(ts, D), _row(ts, 2 * D), _bcast(1, 2 * D), _row(ts, D), _row(ts, D)],
        out_specs=[_row(ts, D), _row(ts, D), _row(ts, 2 * D), _bcast(1, 2 * D)],
        out_shape=[jax.ShapeDtypeStruct((S, D), BF), jax.ShapeDtypeStruct((S, D), BF),
                   jax.ShapeDtypeStruct((S, 2 * D), BF), jax.ShapeDtypeStruct((1, 2 * D), F32)],
        compiler_params=_params("arbitrary"))(d_merge, gpre, b_gate, o_a, o_b)


def _final_loss(x2, tgt, gf):
    S, D = x2.shape
    ts = _tile(S, 512)

    def body(x_ref, t_ref, g_ref, dx_ref, dg_ref, loss_ref):
        i = pl.program_id(0)
        xv = x_ref[...]
        g = g_ref[...]
        y = xv * _rstd(xv) * g
        err = y - t_ref[...]
        dx, dg = _norm_bwd_rows(err * (1.0 / D), xv, g)
        dx_ref[...] = dx
        _accumulate(i, dg_ref, dg)
        part = 0.5 * jnp.sum(jnp.mean(err * err, axis=-1, keepdims=True), axis=0, keepdims=True)
        _accumulate(i, loss_ref, jnp.broadcast_to(part, (8, LANE)))

    return pl.pallas_call(
        body, name="final_loss", grid=(S // ts,),
        in_specs=[_row(ts, D), _row(ts, D), _bcast(1, D)],
        out_specs=[_row(ts, D), _bcast(1, D), _bcast(8, LANE)],
        out_shape=[jax.ShapeDtypeStruct((S, D), F32), jax.ShapeDtypeStruct((1, D), F32),
                   jax.ShapeDtypeStruct((8, LANE), F32)],
        compiler_params=_params("arbitrary"))(x2, tgt, gf)


HALO = 16


SUB = 8


def _shift_down(cur, prev, k, rows):
    out = pltpu.roll(cur, k, 0)
    head = out[:SUB]
    for j in range(k):
        head = jnp.where(rows == j, prev[HALO - k + j:HALO - k + j + 1, :], head)
    return jnp.concatenate([head, out[SUB:]], axis=0)


def _shift_up(cur, nxt, k, rows, ts):
    out = pltpu.roll(cur, ts - k, 0)
    tail = out[ts - SUB:]
    for j in range(k):
        tail = jnp.where(rows == SUB - k + j, nxt[j:j + 1, :], tail)
    return jnp.concatenate([out[:ts - SUB], tail], axis=0)


def _conv_rows(cur, prev, w, b, rows):
    return b + w[0:1, :] * _shift_down(cur, prev, 2, rows) + w[1:2, :] * _shift_down(cur, prev, 1, rows) + w[2:3, :] * cur


def _conv_specs(ts, C, shard_of):
    nh = ts // HALO
    cur = pl.BlockSpec((None, ts, C), lambda g, i: (shard_of(g), i, 0))
    prev = pl.BlockSpec((None, HALO, C), lambda g, i: (shard_of(g), jnp.maximum(i * nh - 1, 0), 0))
    return cur, prev


def _ffn_act(u_pre, conv_w, conv_b):
    G4, S, C = u_pre.shape
    G = G4 // 2
    ts = _tile(S, 256)

    def body(up_ref, upp_ref, gt_ref, gtp_ref, wu_ref, wg_ref, bu_ref, bg_ref, act_ref, u_ref):
        first = pl.program_id(1) == 0
        rows = lax.broadcasted_iota(jnp.int32, (SUB, C), 0)
        pu = jnp.where(first, 0.0, upp_ref[...].astype(F32))
        pg = jnp.where(first, 0.0, gtp_ref[...].astype(F32))
        up = _conv_rows(up_ref[...].astype(F32), pu, wu_ref[...], bu_ref[...], rows)
        gate = _conv_rows(gt_ref[...].astype(F32), pg, wg_ref[...], bg_ref[...], rows)
        act_ref[...] = (gate * _sigmoid(gate) * up).astype(BF)
        u_ref[0] = up.astype(BF)
        u_ref[1] = gate.astype(BF)

    cur_u, prev_u = _conv_specs(ts, C, lambda g: g)
    cur_g, prev_g = _conv_specs(ts, C, lambda g: g + G)
    w_u = pl.BlockSpec((None, 3, C), lambda g, i: (g, 0, 0))
    w_g = pl.BlockSpec((None, 3, C), lambda g, i: (g + G, 0, 0))
    b_u = pl.BlockSpec((None, 1, C), lambda g, i: (g, 0, 0))
    b_g = pl.BlockSpec((None, 1, C), lambda g, i: (g + G, 0, 0))
    pair = pl.BlockSpec((2, None, ts, C), lambda g, i: (0, g, i, 0))
    act, u = pl.pallas_call(
        body, name="ffn_act", grid=(G, S // ts),
        in_specs=[cur_u, prev_u, cur_g, prev_g, w_u, w_g, b_u, b_g],
        out_specs=[pl.BlockSpec((None, ts, C), lambda g, i: (g, i, 0)), pair],
        out_shape=[jax.ShapeDtypeStruct((G, S, C), BF), jax.ShapeDtypeStruct((2, G, S, C), BF)],
        compiler_params=_params("parallel", "parallel"))(u_pre, u_pre, u_pre, u_pre, conv_w, conv_w, conv_b, conv_b)
    return act, u


def _ffn_act_bwd(u, d_act):
    _, G, S, C = u.shape
    ts = _tile(S, 256)

    def body(u_ref, da_ref, du_ref):
        up, gate = u_ref[0].astype(F32), u_ref[1].astype(F32)
        sg = _sigmoid(gate)
        da = da_ref[...].astype(F32)
        du_ref[0] = (da * (gate * sg)).astype(BF)
        du_ref[1] = (da * up * (sg * (1.0 + gate * (1.0 - sg)))).astype(BF)

    pair = pl.BlockSpec((2, None, ts, C), lambda g, i: (0, g, i, 0))
    du = pl.pallas_call(
        body, name="ffn_act_bwd", grid=(G, S // ts),
        in_specs=[pair, pl.BlockSpec((None, ts, C), lambda g, i: (g, i, 0))], out_specs=pair,
        out_shape=jax.ShapeDtypeStruct((2, G, S, C), BF),
        compiler_params=_params("parallel", "parallel"))(u, d_act)
    return du.reshape(2 * G, S, C)


def _conv_bwd(du, u_pre, conv_w):
    G4, S, C = du.shape
    ts = _tile(S, 256)
    nh = ts // HALO
    last_halo = S // HALO - 1

    def body(du_ref, dun_ref, u_ref, w_ref, dpre_ref, dw_ref, db_ref):
        i = pl.program_id(1)
        rows = lax.broadcasted_iota(jnp.int32, (SUB, C), 0)
        du_c = du_ref[...].astype(F32)
        nxt = jnp.where(i == pl.num_programs(1) - 1, 0.0, dun_ref[...].astype(F32))
        up1, up2 = _shift_up(du_c, nxt, 1, rows, ts), _shift_up(du_c, nxt, 2, rows, ts)
        w = w_ref[...]
        dpre_ref[...] = (w[2:3, :] * du_c + w[1:2, :] * up1 + w[0:1, :] * up2).astype(BF)
        u_c = u_ref[...].astype(F32)
        dw = jnp.concatenate([
            jnp.sum(up2 * u_c, axis=0, keepdims=True),
            jnp.sum(up1 * u_c, axis=0, keepdims=True),
            jnp.sum(du_c * u_c, axis=0, keepdims=True)], axis=0)
        _accumulate(i, dw_ref, dw)
        _accumulate(i, db_ref, jnp.sum(du_c, axis=0, keepdims=True))

    cur = pl.BlockSpec((None, ts, C), lambda g, i: (g, i, 0))
    nxt = pl.BlockSpec((None, HALO, C), lambda g, i: (g, jnp.minimum((i + 1) * nh, last_halo), 0))
    return pl.pallas_call(
        body, name="conv_bwd", grid=(G4, S // ts),
        in_specs=[cur, nxt, cur, pl.BlockSpec((None, 3, C), lambda g, i: (g, 0, 0))],
        out_specs=[cur, pl.BlockSpec((None, 3, C), lambda g, i: (g, 0, 0)), pl.BlockSpec((None, 1, C), lambda g, i: (g, 0, 0))],
        out_shape=[jax.ShapeDtypeStruct((G4, S, C), BF), jax.ShapeDtypeStruct((G4, 3, C), F32),
                   jax.ShapeDtypeStruct((G4, 1, C), F32)],
        compiler_params=_params("parallel", "arbitrary"))(du, du, u_pre, conv_w)


MLA_T = 1024
MLA_HB = 4
MLA_BWD_HB = 2


def _mla_pairs(n, by_row):
    if by_row:
        pairs = [(i, j) for i in range(n) for j in range(i + 1)]
    else:
        pairs = [(i, j) for j in range(n) for i in range(j, n)]
    return jnp.asarray([p[0] for p in pairs], jnp.int32), jnp.asarray([p[1] for p in pairs], jnp.int32)


def _mla_specs(hb):
    q = pl.BlockSpec((MLA_T, hb * HEAD), lambda g, t, it, jt: (it[t], g))
    k = pl.BlockSpec((MLA_T, hb * HEAD), lambda g, t, it, jt: (jt[t], g))
    kpe = pl.BlockSpec((MLA_T, HEAD), lambda g, t, it, jt: (jt[t], 0))
    lse = pl.BlockSpec((hb, MLA_T, LANE), lambda g, t, it, jt: (g, it[t], 0))
    return q, k, kpe, lse


def _mla_head(ref, hh):
    return ref[:, hh * HEAD:(hh + 1) * HEAD]


LOG2E = math.log2(math.e)
MLA_EXP2_SCALE = MLA_SCALE * LOG2E


def _mla_scores(qn_ref, qp_ref, kn_ref, kpe, hh, ok):
    q = jnp.concatenate([_mla_head(qn_ref, hh), _mla_head(qp_ref, hh)], axis=1)
    k = jnp.concatenate([_mla_head(kn_ref, hh), kpe], axis=1)
    s = lax.dot_general(q, k, NT, preferred_element_type=F32)
    return q, k, s if ok is None else jnp.where(ok, s, -jnp.inf)


def _mla_diagonal_mask():
    row = lax.broadcasted_iota(jnp.int32, (MLA_T, MLA_T), 0)
    col = lax.broadcasted_iota(jnp.int32, (MLA_T, MLA_T), 1)
    return col <= row


def _mla_step(i, j, step):
    @pl.when(j < i)
    def _():
        step(None)

    @pl.when(j == i)
    def _():
        step(_mla_diagonal_mask())


def _mla_fwd(qn, qp, kn, kpe, v):
    S = qn.shape[0]
    it, jt = _mla_pairs(S // MLA_T, True)

    def body(it_ref, jt_ref, qn_ref, qp_ref, kn_ref, kpe_ref, v_ref, o_ref, lse_ref, m_scr, acc_scr):
        t = pl.program_id(1)
        i, j = it_ref[t], jt_ref[t]

        @pl.when(j == 0)
        def _():
            m_scr[...] = jnp.full(m_scr.shape, -jnp.inf, F32)
            acc_scr[...] = jnp.zeros(acc_scr.shape, F32)

        def step(ok):
            kpe_v = kpe_ref[...]
            ones = jnp.ones((MLA_T, HEAD), BF)
            state = [(m_scr[hh], acc_scr[hh]) for hh in range(MLA_HB)]
            new = []
            for hh in range(MLA_HB):
                m_prev, acc = state[hh]
                _, _, s = _mla_scores(qn_ref, qp_ref, kn_ref, kpe_v, hh, ok)
                m_new = jnp.maximum(m_prev, jnp.max(s, axis=1, keepdims=True))
                p = jnp.exp2((s - m_new) * MLA_EXP2_SCALE).astype(BF)
                v1 = jnp.concatenate([_mla_head(v_ref, hh), ones], axis=1)
                alpha = jnp.exp2((m_prev - m_new) * MLA_EXP2_SCALE)
                new.append((m_new, alpha * acc + lax.dot_general(p, v1, NN, preferred_element_type=F32)))
            for hh in range(MLA_HB):
                m_scr[hh], acc_scr[hh] = new[hh]

        _mla_step(i, j, step)

        @pl.when(j == i)
        def _():
            for hh in range(MLA_HB):
                l = acc_scr[hh, :, HEAD:]
                o_ref[:, hh * HEAD:(hh + 1) * HEAD] = (acc_scr[hh, :, :HEAD] / l).astype(BF)
                lse_ref[hh] = m_scr[hh] * MLA_SCALE + jnp.log(l)

    qspec, kspec, kpespec, lsespec = _mla_specs(MLA_HB)
    grid_spec = pltpu.PrefetchScalarGridSpec(
        num_scalar_prefetch=2, grid=(MLA_HEADS // MLA_HB, it.shape[0]),
        in_specs=[qspec, qspec, kspec, kpespec, kspec], out_specs=[qspec, lsespec],
        scratch_shapes=[pltpu.VMEM((MLA_HB, MLA_T, 1), F32), pltpu.VMEM((MLA_HB, MLA_T, 2 * HEAD), F32)])
    return pl.pallas_call(
        body, name="mla_fwd", grid_spec=grid_spec,
        out_shape=[jax.ShapeDtypeStruct((S, MLA_HEADS * HEAD), BF), jax.ShapeDtypeStruct((MLA_HEADS, S, LANE), F32)],
        compiler_params=_params("parallel", "arbitrary"))(it, jt, qn, qp, kn, kpe, v)


def _mla_p_ds(qn_ref, qp_ref, kn_ref, kpe, v_ref, do_ref, o_ref, lse_ref, hh, ok):
    q, k, s = _mla_scores(qn_ref, qp_ref, kn_ref, kpe, hh, ok)
    p = jnp.exp2(s * MLA_EXP2_SCALE - lse_ref[hh][:, 0:1] * LOG2E)
    do = _mla_head(do_ref, hh)
    delta = jnp.sum(do.astype(F32) * _mla_head(o_ref, hh).astype(F32), axis=1, keepdims=True)
    dp = lax.dot_general(do, _mla_head(v_ref, hh), NT, preferred_element_type=F32)
    ds = p * (dp - delta) * MLA_SCALE
    return q, k, p, ds, do


def _mla_bwd(qn, qp, kn, kpe, v, do, o, lse):
    S = qn.shape[0]
    nq = S // MLA_T
    hb = MLA_BWD_HB
    it, jt = _mla_pairs(nq, False)

    def body(it_ref, jt_ref, qn_ref, qp_ref, kn_ref, kpe_ref, v_ref, do_ref, o_ref, lse_ref,
             dqn_ref, dqp_ref, dkn_ref, dv_ref, dkpe_ref, dq_acc, dk_acc, dv_acc, stage_n, stage_p, osem):
        t = pl.program_id(1)
        i, j = it_ref[t], jt_ref[t]

        @pl.when(t == 0)
        def _():
            dq_acc[...] = jnp.zeros(dq_acc.shape, F32)

        @pl.when(i == j)
        def _():
            dk_acc[...] = jnp.zeros(dk_acc.shape, F32)
            dv_acc[...] = jnp.zeros(dv_acc.shape, F32)

        def step(ok):
            kpe_v = kpe_ref[...]
            for hh in range(hb):
                q, k, p, ds, do_h = _mla_p_ds(qn_ref, qp_ref, kn_ref, kpe_v, v_ref, do_ref, o_ref, lse_ref, hh, ok)
                ds = ds.astype(BF)
                dv_acc[hh] += lax.dot_general(p.astype(BF), do_h, TN, preferred_element_type=F32)
                dk_acc[hh] += lax.dot_general(ds, q, TN, preferred_element_type=F32)
                dq_acc[i, hh] += lax.dot_general(ds, k, NN, preferred_element_type=F32)

        _mla_step(i, j, step)

        @pl.when(i == j)
        def _():
            for hh in range(hb):
                stage_n[:, hh * HEAD:(hh + 1) * HEAD] = dq_acc[i, hh, :, :HEAD].astype(BF)
                stage_p[:, hh * HEAD:(hh + 1) * HEAD] = dq_acc[i, hh, :, HEAD:]
            rows = pl.ds(pl.multiple_of(i * MLA_T, MLA_T), MLA_T)
            cols = pl.ds(pl.multiple_of(pl.program_id(0) * hb * HEAD, LANE), hb * HEAD)
            out_n = pltpu.make_async_copy(stage_n, dqn_ref.at[rows, cols], osem.at[0])
            out_p = pltpu.make_async_copy(stage_p, dqp_ref.at[rows, cols], osem.at[1])
            out_n.start()
            out_p.start()
            out_n.wait()
            out_p.wait()

        @pl.when(i == nq - 1)
        def _():
            dkpe = dk_acc[0, :, HEAD:]
            for hh in range(hb):
                dkn_ref[:, hh * HEAD:(hh + 1) * HEAD] = dk_acc[hh, :, :HEAD].astype(BF)
                dv_ref[:, hh * HEAD:(hh + 1) * HEAD] = dv_acc[hh].astype(BF)
                if hh:
                    dkpe = dkpe + dk_acc[hh, :, HEAD:]
            dkpe_ref[...] = dkpe

    qspec, kspec, kpespec, lsespec = _mla_specs(hb)
    dkpespec = pl.BlockSpec((None, MLA_T, HEAD), lambda g, t, it, jt: (g, jt[t], 0))
    grid_spec = pltpu.PrefetchScalarGridSpec(
        num_scalar_prefetch=2, grid=(MLA_HEADS // hb, it.shape[0]),
        in_specs=[qspec, qspec, kspec, kpespec, kspec, qspec, qspec, lsespec],
        out_specs=[ANY, ANY, kspec, kspec, dkpespec],
        scratch_shapes=[pltpu.VMEM((nq, hb, MLA_T, 2 * HEAD), F32), pltpu.VMEM((hb, MLA_T, 2 * HEAD), F32),
                        pltpu.VMEM((hb, MLA_T, HEAD), F32), pltpu.VMEM((MLA_T, hb * HEAD), BF),
                        pltpu.VMEM((MLA_T, hb * HEAD), F32), pltpu.SemaphoreType.DMA((2,))])
    wide = jax.ShapeDtypeStruct((S, MLA_HEADS * HEAD), BF)
    return pl.pallas_call(
        body, name="mla_bwd", grid_spec=grid_spec,
        out_shape=[wide, jax.ShapeDtypeStruct((S, MLA_HEADS * HEAD), F32), wide, wide,
                   jax.ShapeDtypeStruct((MLA_HEADS // hb, S, HEAD), F32)],
        compiler_params=_params("parallel", "arbitrary"))(it, jt, qn, qp, kn, kpe, v, do, o, lse)


DIL_W = 3 * DIL_HPG * HEAD
DIL_O = DIL_HPG * HEAD
DIL_STEP_BLOCKS = 4


def _dil_slopes(g):
    return [2.0 ** (-ALIBI_MAX_BIAS * (g * DIL_HPG + hh + 1) / DIL_HEADS) for hh in range(DIL_HPG)]


def _dil_bias(dil):
    p = lax.broadcasted_iota(jnp.int32, (DIL_BLOCK, DIL_BLOCK), 0)
    kk = lax.broadcasted_iota(jnp.int32, (DIL_BLOCK, DIL_BLOCK), 1)
    jc = p - kk
    dist_c = (dil * jc).astype(F32)
    dist_p = (dil * (jc + DIL_BLOCK)).astype(F32)
    return jc >= 0, jc <= 0, dist_c, dist_p


def _dil_head(blk, hh):
    q = blk[:, hh * HEAD:(hh + 1) * HEAD]
    k = blk[:, DIL_O + hh * HEAD:DIL_O + (hh + 1) * HEAD]
    v = blk[:, 2 * DIL_O + hh * HEAD:2 * DIL_O + (hh + 1) * HEAD]
    return q, k, v


def _dil_s(q, k, slope, dist, ok):
    s = lax.dot_general(q, k, NT, preferred_element_type=F32) * DIL_SCALE - slope * dist
    return jnp.where(ok, s, -jnp.inf)


def _dil_view(a, dil):
    S, W = a.shape
    return a.reshape(S // dil, dil * W)


def _dil_fwd(qkv, g):
    _, dil = DIL_PATTERNS[g]
    S = qkv.shape[0]
    L = S // dil
    nb = L // DIL_BLOCK
    slopes = _dil_slopes(g)

    bb = min(DIL_STEP_BLOCKS, nb)
    rows = bb * DIL_BLOCK

    def body(cur_ref, prev_ref, o_ref, lse_ref):
        n = pl.program_id(1)
        ok_c, ok_p, dist_c, dist_p = _dil_bias(dil)
        for b in range(bb):
            cur = cur_ref[b * DIL_BLOCK:(b + 1) * DIL_BLOCK, :]
            prev = prev_ref[...] if b == 0 else cur_ref[(b - 1) * DIL_BLOCK:b * DIL_BLOCK, :]
            ok_b = ok_p & (n > 0) if b == 0 else ok_p
            for hh in range(DIL_HPG):
                q, kc, vc = _dil_head(cur, hh)
                _, kp, vp = _dil_head(prev, hh)
                sc = _dil_s(q, kc, slopes[hh], dist_c, ok_c)
                sp = _dil_s(q, kp, slopes[hh], dist_p, ok_b)
                m = jnp.maximum(jnp.max(sc, axis=1, keepdims=True), jnp.max(sp, axis=1, keepdims=True))
                pc, pp = jnp.exp(sc - m), jnp.exp(sp - m)
                l = jnp.sum(pc, axis=1, keepdims=True) + jnp.sum(pp, axis=1, keepdims=True)
                o = (lax.dot_general(pc.astype(BF), vc, NN, preferred_element_type=F32)
                     + lax.dot_general(pp.astype(BF), vp, NN, preferred_element_type=F32)) / l
                rs, sl = slice(b * DIL_BLOCK, (b + 1) * DIL_BLOCK), slice(hh * HEAD, (hh + 1) * HEAD)
                o_ref[rs, sl] = o
                lse_ref[rs, sl] = jnp.broadcast_to(m + jnp.log(l), (DIL_BLOCK, HEAD))

    ospec = pl.BlockSpec((rows, DIL_O), lambda r, n: (n, r))
    o, lse = pl.pallas_call(
        body, name=f"dil_fwd{g}", grid=(dil, nb // bb),
        in_specs=[pl.BlockSpec((rows, DIL_W), lambda r, n: (n, r)),
                  pl.BlockSpec((DIL_BLOCK, DIL_W), lambda r, n: (jnp.maximum(n * bb - 1, 0), r))],
        out_specs=[ospec, ospec],
        out_shape=[jax.ShapeDtypeStruct((L, dil * DIL_O), F32), jax.ShapeDtypeStruct((L, dil * DIL_O), F32)],
        compiler_params=_params("parallel", "parallel"))(_dil_view(qkv, dil), _dil_view(qkv, dil))
    return o.reshape(S, DIL_O), lse.reshape(S, DIL_O)


def _dil_combine(os_, lses):
    S = os_[0].shape[0]
    ts = _tile(S, 512)

    def body(o0, o1, o2, l0, l1, l2, out_ref, lse_ref):
        a, b, c = l0[...], l1[...], l2[...]
        m = jnp.maximum(jnp.maximum(a, b), c)
        ea, eb, ec = jnp.exp(a - m), jnp.exp(b - m), jnp.exp(c - m)
        tot = ea + eb + ec
        out_ref[...] = ((ea * o0[...] + eb * o1[...] + ec * o2[...]) / tot).astype(BF)
        lse_ref[...] = m + jnp.log(tot)

    return pl.pallas_call(
        body, name="dil_combine", grid=(S // ts,), in_specs=[_row(ts, DIL_O)] * 6,
        out_specs=[_row(ts, DIL_O), _row(ts, DIL_O)],
        out_shape=[jax.ShapeDtypeStruct((S, DIL_O), BF), jax.ShapeDtypeStruct((S, DIL_O), F32)],
        compiler_params=_params("parallel"))(*os_, *lses)


def _dil_delta(do, out):
    S = do.shape[0]
    ts = _tile(S, 512)

    def body(do_ref, o_ref, d_ref):
        for hh in range(DIL_HPG):
            sl = slice(hh * HEAD, (hh + 1) * HEAD)
            d = jnp.sum(do_ref[:, sl].astype(F32) * o_ref[:, sl].astype(F32), axis=1, keepdims=True)
            d_ref[:, sl] = jnp.broadcast_to(d, (ts, HEAD))

    return pl.pallas_call(
        body, name="dil_delta", grid=(S // ts,), in_specs=[_row(ts, DIL_O)] * 2, out_specs=_row(ts, DIL_O),
        out_shape=jax.ShapeDtypeStruct((S, DIL_O), F32), compiler_params=_params("parallel"))(do, out)


def _dil_bwd(qkv, do, lse, delta, g):
    _, dil = DIL_PATTERNS[g]
    S = qkv.shape[0]
    L = S // dil
    nb = L // DIL_BLOCK
    slopes = _dil_slopes(g)

    def pair(q, k, v, do_h, lse_h, delta_h, slope, dist, ok):
        s = _dil_s(q, k, slope, dist, ok)
        p = jnp.exp(s - lse_h)
        dp = lax.dot_general(do_h, v, NT, preferred_element_type=F32)
        ds = (p * (dp - delta_h) * DIL_SCALE).astype(BF)
        return p.astype(BF), ds

    bb = min(DIL_STEP_BLOCKS, nb)
    rows = bb * DIL_BLOCK
    steps = nb // bb

    def body(cur_ref, prev_ref, next_ref, doc_ref, don_ref, lsec_ref, lsen_ref, dlc_ref, dln_ref, out_ref):
        n = pl.program_id(1)
        ok_c, ok_p0, dist_c, dist_p = _dil_bias(dil)
        for b in range(bb):
            rs = slice(b * DIL_BLOCK, (b + 1) * DIL_BLOCK)
            rp = slice((b - 1) * DIL_BLOCK, b * DIL_BLOCK)
            rn = slice((b + 1) * DIL_BLOCK, (b + 2) * DIL_BLOCK)
            first, last = b == 0, b == bb - 1
            cur = cur_ref[rs, :]
            prev = prev_ref[...] if first else cur_ref[rp, :]
            nxt = next_ref[...] if last else cur_ref[rn, :]
            ok_a = ok_p0 & (n > 0) if first else ok_p0
            ok_n = ok_p0 & (n < steps - 1) if last else ok_p0
            for hh in range(DIL_HPG):
                sl = slice(hh * HEAD, (hh + 1) * HEAD)
                q, kc, vc = _dil_head(cur, hh)
                _, kp, vp = _dil_head(prev, hh)
                qn, _, _ = _dil_head(nxt, hh)
                do_c = doc_ref[rs, sl]
                do_n = don_ref[:, sl] if last else doc_ref[rn, sl]
                lse_c = lsec_ref[rs, sl][:, 0:1]
                lse_n = (lsen_ref[:, sl] if last else lsec_ref[rn, sl])[:, 0:1]
                dl_c = dlc_ref[rs, sl][:, 0:1]
                dl_n = (dln_ref[:, sl] if last else dlc_ref[rn, sl])[:, 0:1]
                _, ds_a = pair(q, kp, vp, do_c, lse_c, dl_c, slopes[hh], dist_p, ok_a)
                p_b, ds_b = pair(q, kc, vc, do_c, lse_c, dl_c, slopes[hh], dist_c, ok_c)
                p_n, ds_n = pair(qn, kc, vc, do_n, lse_n, dl_n, slopes[hh], dist_p, ok_n)
                dq = (lax.dot_general(ds_a, kp, NN, preferred_element_type=F32)
                      + lax.dot_general(ds_b, kc, NN, preferred_element_type=F32))
                dk = (lax.dot_general(ds_b, q, TN, preferred_element_type=F32)
                      + lax.dot_general(ds_n, qn, TN, preferred_element_type=F32))
                dv = (lax.dot_general(p_b, do_c, TN, preferred_element_type=F32)
                      + lax.dot_general(p_n, do_n, TN, preferred_element_type=F32))
                out_ref[rs, sl] = dq.astype(BF)
                out_ref[rs, DIL_O + hh * HEAD:DIL_O + (hh + 1) * HEAD] = dk.astype(BF)
                out_ref[rs, 2 * DIL_O + hh * HEAD:2 * DIL_O + (hh + 1) * HEAD] = dv.astype(BF)

    cur_w = pl.BlockSpec((rows, DIL_W), lambda r, n: (n, r))
    prev_w = pl.BlockSpec((DIL_BLOCK, DIL_W), lambda r, n: (jnp.maximum(n * bb - 1, 0), r))
    next_w = pl.BlockSpec((DIL_BLOCK, DIL_W), lambda r, n: (jnp.minimum((n + 1) * bb, nb - 1), r))
    cur_o = pl.BlockSpec((rows, DIL_O), lambda r, n: (n, r))
    next_o = pl.BlockSpec((DIL_BLOCK, DIL_O), lambda r, n: (jnp.minimum((n + 1) * bb, nb - 1), r))
    qv, dov, lsev, dlv = _dil_view(qkv, dil), _dil_view(do, dil), _dil_view(lse, dil), _dil_view(delta, dil)
    out = pl.pallas_call(
        body, name=f"dil_bwd{g}", grid=(dil, steps),
        in_specs=[cur_w, prev_w, next_w, cur_o, next_o, cur_o, next_o, cur_o, next_o],
        out_specs=cur_w, out_shape=jax.ShapeDtypeStruct((L, dil * DIL_W), BF),
        compiler_params=_params("parallel", "parallel"))(qv, qv, qv, dov, dov, lsev, lsev, dlv, dlv)
    return out.reshape(S, DIL_W)


def _adamw(name, w, g, m, v):
    R, C = w.shape
    tr, tc = _adamw_block(R, C)

    def body(w_ref, g_ref, m_ref, v_ref, go_ref, d_ref, nm_ref, nv_ref):
        gv = g_ref[...]
        go_ref[...] = gv
        nm = ADAM_B1 * m_ref[...] + (1.0 - ADAM_B1) * gv
        nv = ADAM_B2 * v_ref[...] + (1.0 - ADAM_B2) * (gv * gv)
        m_hat = nm / (1.0 - ADAM_B1 ** ADAM_STEP)
        v_hat = nv / (1.0 - ADAM_B2 ** ADAM_STEP)
        d_ref[...] = -ADAM_LR * (m_hat / (jnp.sqrt(v_hat) + ADAM_EPS) + ADAM_WD * w_ref[...])
        nm_ref[...] = nm
        nv_ref[...] = nv

    spec = pl.BlockSpec((tr, tc), lambda i, j: (i, j))
    shp = jax.ShapeDtypeStruct((R, C), F32)
    return pl.pallas_call(
        body, name=name, grid=(R // tr, C // tc), in_specs=[spec] * 4, out_specs=[spec] * 4, out_shape=[shp] * 4,
        compiler_params=_params("parallel", "parallel"))(w, g, m, v)


ADAMW_BLOCK_ELEMS = 640 * 1024


def _adamw_block(R, C):
    if R * C <= ADAMW_BLOCK_ELEMS:
        return R, C
    tr = _tile_rows(R, max(8, ADAMW_BLOCK_ELEMS // C))
    tc = _tile(C, max(LANE, ADAMW_BLOCK_ELEMS // R))
    if tr * C >= R * tc or R * tc > ADAMW_BLOCK_ELEMS:
        return tr, C
    return R, tc


def _tile_rows(n, pref, mult=8):
    t = (pref // mult) * mult
    while t >= mult:
        if n % t == 0:
            return t
        t -= mult
    return n


ANY = pl.BlockSpec(memory_space=pl.ANY)


def _place():
    x, y, c = lax.axis_index("x"), lax.axis_index("y"), lax.axis_index("c")
    chips = [(1 - x, y), (x, 1 - y), (1 - x, 1 - y)]
    chip_idx = [2 * cx + cy for cx, cy in chips]
    return x, y, c, 2 * x + y, chips, chip_idx


def _rcopy(src, dst, ssem, rsem, dev):
    return pltpu.make_async_remote_copy(src_ref=src, dst_ref=dst, send_sem=ssem, recv_sem=rsem,
                                        device_id=dev, device_id_type=MESH)


HBM = pl.BlockSpec(memory_space=pltpu.HBM)
SEM = pl.BlockSpec(memory_space=pltpu.SEMAPHORE)
EFFECT = pltpu.SideEffectType.DATAFLOW_SIDE_EFFECTING


def _split_copies(kind, srcs, lands, ssem, rsem):
    _, _, c, me, chips, chip_idx = _place()
    cps = []
    for i in range(len(srcs)):
        for k in range(3):
            if kind == "gather":
                rows = srcs[i].shape[0]
                if rows == lands[i].shape[1]:
                    src, dst = srcs[i], lands[i].at[me]
                else:
                    src, dst = srcs[i], lands[i].at[me, pl.ds(pl.multiple_of(c * rows, 16), rows)]
            else:
                src, dst = srcs[i].at[chip_idx[k]], lands[i].at[k]
            cps.append(_rcopy(src, dst, ssem.at[3 * i + k], rsem.at[3 * i + k], (*chips[k], c)))
    return cps


def _exchange_start(name, kind, srcs, lands, groups):
    n, ng = len(srcs), len(groups)

    def body(*refs):
        src_refs, land_refs = refs[:n], refs[n:2 * n]
        sems = refs[2 * n:2 * n + 2 * ng]
        token = refs[-1]
        for gi, grp in enumerate(groups):
            cps = _split_copies(kind, [src_refs[i] for i in grp], [land_refs[i] for i in grp], sems[2 * gi], sems[2 * gi + 1])
            for cp in cps:
                cp.start()
        token[...] = jnp.zeros_like(token)

    arrays = list(srcs) + list(lands)
    out_shape = []
    for grp in groups:
        out_shape += [pltpu.SemaphoreType.DMA((3 * len(grp),)), pltpu.SemaphoreType.DMA((3 * len(grp),))]
    out_shape += [pltpu.HBM(a.shape, a.dtype) for a in arrays] + [jax.ShapeDtypeStruct((8, LANE), F32)]
    outs = pl.pallas_call(
        body, name=name, out_shape=out_shape, in_specs=[HBM] * (2 * n),
        out_specs=[SEM] * (2 * ng) + [HBM] * (2 * n) + [pl.BlockSpec(memory_space=pltpu.VMEM)],
        input_output_aliases={i: 2 * ng + i for i in range(2 * n)},
        compiler_params=pltpu.CompilerParams(has_side_effects=EFFECT),
    )(*[pltpu.with_memory_space_constraint(a, pltpu.HBM) for a in arrays])
    sems = [(outs[2 * gi], outs[2 * gi + 1]) for gi in range(ng)]
    thru = outs[2 * ng:2 * ng + 2 * n]
    return sems, thru[:n], thru[n:], outs[-1]


def _exchange_wait(name, kind, srcs, lands, sems, after):
    n = len(srcs)

    def body(*refs):
        cps = _split_copies(kind, refs[:n], refs[n:2 * n], refs[2 * n], refs[2 * n + 1])
        for cp in cps:
            cp.wait_send()
            cp.wait_recv()

    arrays = list(srcs) + list(lands)
    outs = pl.pallas_call(
        body, name=name, out_shape=[pltpu.HBM(a.shape, a.dtype) for a in arrays],
        in_specs=[HBM] * (2 * n) + [SEM, SEM, ANY], out_specs=[HBM] * (2 * n),
        input_output_aliases={i: i for i in range(2 * n)},
        compiler_params=pltpu.CompilerParams(has_side_effects=EFFECT),
    )(*arrays, sems[0], sems[1], after)
    return outs[:n], outs[n:]


EXCHANGE_CHUNK_BYTES = 3 * 1024 * 1024


def _half_geometry(R, C, axis):
    Rp, Cp = (R // 2, C) if axis == 0 else (R, C // 2)
    tr = _tile_rows(Rp, max(16, EXCHANGE_CHUNK_BYTES // (2 * Cp)), 16)
    return Rp, Cp, tr, Rp // tr


def _pair_sum(name, g, axis):
    G, R, C = g.shape
    Rp, Cp, tr, nb = _half_geometry(R, C, axis)
    steps = G * nb

    def half_block(s, b, h):
        return (s, h * nb + b, 0) if axis == 0 else (s, b, h)

    def body(c_ref, keep_ref, give_ref, out_ref, land, ssem, rsem, credit):
        x, y, c = lax.axis_index("x"), lax.axis_index("y"), lax.axis_index("c")
        sib = (x, y, 1 - c)
        t = pl.program_id(0) * nb + pl.program_id(1)
        slot = t % 2

        @pl.when(t >= 2)
        def _():
            pl.semaphore_wait(credit, 1)

        cp = _rcopy(give_ref.at[0], land.at[slot], ssem.at[slot], rsem.at[slot], sib)
        cp.start()
        cp.wait_recv()
        out_ref[...] = (keep_ref[...].astype(F32) + land[slot].astype(F32)).astype(BF)

        @pl.when(t + 2 < steps)
        def _():
            pl.semaphore_signal(credit, 1, device_id=sib, device_id_type=MESH)

        cp.wait_send()

    blk = (None, tr, Cp)
    grid_spec = pltpu.PrefetchScalarGridSpec(
        num_scalar_prefetch=1, grid=(G, nb),
        in_specs=[pl.BlockSpec(blk, lambda s, b, c_ref: half_block(s, b, c_ref[0])),
                  pl.BlockSpec((1, tr, Cp), lambda s, b, c_ref: half_block(s, b, 1 - c_ref[0]))],
        out_specs=pl.BlockSpec(blk, lambda s, b, c_ref: (s, b, 0)),
        scratch_shapes=[pltpu.VMEM((2, tr, Cp), BF), pltpu.SemaphoreType.DMA((2,)), pltpu.SemaphoreType.DMA((2,)),
                        pltpu.SemaphoreType.REGULAR])
    c_arr = lax.axis_index("c").astype(jnp.int32).reshape(1)
    return pl.pallas_call(
        body, name=name, grid_spec=grid_spec, out_shape=jax.ShapeDtypeStruct((G, Rp, Cp), BF),
        compiler_params=_params("arbitrary", "arbitrary"))(c_arr, g, g)


def _chip_total_join(name, h, landed, axis):
    G, Rp, Cp = h.shape
    R, C = (2 * Rp, Cp) if axis == 0 else (Rp, 2 * Cp)
    tr = _tile_rows(Rp, max(16, EXCHANGE_CHUNK_BYTES // (4 * Cp)), 16)
    nb = Rp // tr

    def body(me_ref, own_ref, l0_ref, l1_ref, l2_ref, full, stage, ssem, rsem, lsem):
        x, y, c = lax.axis_index("x"), lax.axis_index("y"), lax.axis_index("c")
        sib = (x, y, 1 - c)
        b = pl.program_id(0)

        def place(half, r0, rows):
            if axis == 0:
                return full.at[pl.ds(pl.multiple_of(half * Rp + r0, 8), rows), :]
            return full.at[pl.ds(pl.multiple_of(r0, 8), rows), pl.ds(pl.multiple_of(half * Cp, LANE), Cp)]

        def copies(step):
            s = step % 2
            mine = place(c, step * tr, tr)
            return pltpu.make_async_copy(stage.at[s], mine, lsem.at[s]), _rcopy(stage.at[s], mine, ssem.at[s], rsem, sib)

        @pl.when(b >= 2)
        def _():
            loc, rem = copies(b - 2)
            loc.wait()
            rem.wait_send()

        acc = own_ref[...].astype(F32)
        for r in (l0_ref, l1_ref, l2_ref):
            acc = acc + r[...].astype(F32)
        stage[b % 2] = acc
        loc, rem = copies(b)
        loc.start()
        rem.start()

        @pl.when(b == nb - 1)
        def _():
            for step in range(max(0, nb - 2), nb):
                loc, rem = copies(step)
                loc.wait()
                rem.wait_send()
            theirs = place(1 - c, 0, Rp)
            _rcopy(theirs, theirs, ssem.at[0], rsem, sib).wait_recv()

    blk = (None, tr, Cp)
    grid_spec = pltpu.PrefetchScalarGridSpec(
        num_scalar_prefetch=1, grid=(nb,),
        in_specs=[pl.BlockSpec(blk, lambda b, me_ref: (me_ref[0], b, 0))]
        + [pl.BlockSpec(blk, functools.partial(lambda b, me_ref, k: (k, b, 0), k=k)) for k in range(3)],
        out_specs=ANY,
        scratch_shapes=[pltpu.VMEM((2, tr, Cp), F32), pltpu.SemaphoreType.DMA((2,)), pltpu.SemaphoreType.DMA,
                        pltpu.SemaphoreType.DMA((2,))])
    me = (2 * lax.axis_index("x") + lax.axis_index("y")).astype(jnp.int32).reshape(1)
    return pl.pallas_call(
        body, name=name, grid_spec=grid_spec, out_shape=jax.ShapeDtypeStruct((R, C), F32),
        compiler_params=_params("arbitrary"))(me, h, landed, landed, landed)


def _pair_share(name, land):
    G, R, C = land.shape
    Rh = R // 2
    tr = _tile_rows(Rh, max(16, EXCHANGE_CHUNK_BYTES // (2 * C)), 16)
    chunks = [(k, b) for k in range(3) for b in range(Rh // tr)]

    def body(src, dst, buf, lsem, ssem, rsem):
        x, y, c, _, _, chip_idx = _place()
        sib = (x, y, 1 - c)

        def region(ref, k, half, r0, rows):
            return ref.at[chip_idx[k], pl.ds(pl.multiple_of(half * Rh + r0, 16), rows)]

        def load(t):
            k, b = chunks[t]
            return pltpu.make_async_copy(region(src, k, c, b * tr, tr), buf.at[t % 2], lsem.at[t % 2])

        def send(t):
            k, b = chunks[t]
            return _rcopy(buf.at[t % 2], region(dst, k, c, b * tr, tr), ssem.at[t % 2], rsem.at[k], sib)

        load(0).start()
        for t in range(len(chunks)):
            load(t).wait()
            if t + 1 < len(chunks):
                if t >= 1:
                    send(t - 1).wait_send()
                load(t + 1).start()
            send(t).start()
        for t in range(max(0, len(chunks) - 2), len(chunks)):
            send(t).wait_send()
        for k in range(3):
            theirs = region(dst, k, 1 - c, 0, Rh)
            _rcopy(theirs, theirs, ssem.at[0], rsem.at[k], sib).wait_recv()

    return pl.pallas_call(
        body, name=name, in_specs=[ANY], out_specs=ANY, out_shape=jax.ShapeDtypeStruct(land.shape, land.dtype),
        input_output_aliases={0: 0},
        scratch_shapes=[pltpu.VMEM((2, tr, C), land.dtype), pltpu.SemaphoreType.DMA((2,)), pltpu.SemaphoreType.DMA((2,)),
                        pltpu.SemaphoreType.DMA((3,))],
    )(land)


def _allreduce_small(v):
    R, K = v.shape
    ndev = 8

    def body(v_ref, o_ref, land, ssem, rsem):
        x, y, c = lax.axis_index("x"), lax.axis_index("y"), lax.axis_index("c")
        me = 4 * x + 2 * y + c
        land[me] = v_ref[...]
        cps = []
        for r in range(1, ndev):
            fx, fy, fc = (r >> 2) & 1, (r >> 1) & 1, r & 1
            peer = (x ^ fx, y ^ fy, c ^ fc)
            cp = _rcopy(v_ref, land.at[me], ssem.at[r - 1], rsem.at[r - 1], peer)
            cp.start()
            cps.append((cp, 4 * peer[0] + 2 * peer[1] + peer[2], r))
        for cp, src, r in cps:
            cp.wait_send()
            _rcopy(v_ref, land.at[src], ssem.at[r - 1], rsem.at[r - 1], (x, y, c)).wait_recv()
        acc = land[0]
        for d in range(1, ndev):
            acc = acc + land[d]
        o_ref[...] = acc

    vm = pl.BlockSpec(memory_space=pltpu.VMEM)
    return pl.pallas_call(
        body, name="allreduce_small", in_specs=[vm], out_specs=vm, out_shape=jax.ShapeDtypeStruct((R, K), F32),
        scratch_shapes=[pltpu.VMEM((ndev, R, K), F32), pltpu.SemaphoreType.DMA((ndev - 1,)), pltpu.SemaphoreType.DMA((ndev - 1,))],
    )(v)


IN_SPLITS = (Q_RANK, KV_RANK, QK_ROPE, DIL_HEADS * HEAD, DIL_HEADS * HEAD, DIL_HEADS * HEAD, D_MODEL, D_MODEL)
IN_OFF = tuple(int(v) for v in np.cumsum((0,) + IN_SPLITS))


def _unshard_cols(g):
    G, K, Ns = g.shape
    return g.transpose(1, 0, 2).reshape(K, G * Ns)


def _shard_cols(w):
    K, N = w.shape
    return w.reshape(K, N_CHIPS, N // N_CHIPS).transpose(1, 0, 2)


def _rope_pad(w):
    half = QK_ROPE // 2
    z = jnp.zeros(w.shape[:-1] + (half,), w.dtype)
    return jnp.concatenate([w[..., :half], z, w[..., half:], z], axis=-1)


def _rope_unpad(w):
    half = QK_ROPE // 2
    return jnp.concatenate([w[..., :half], w[..., 2 * half:3 * half]], axis=-1)


def _split_w_in(w_in_g):
    w = _unshard_cols(w_in_g)
    K = w.shape[0]
    p = [w[:, IN_OFF[i]:IN_OFF[i + 1]] for i in range(8)]
    w_lat = jnp.concatenate([p[0], p[1], _rope_pad(p[2]), jnp.zeros((K, LAT_W - _KPE.stop), w.dtype)], axis=1)
    w_dil = [jnp.concatenate([p[3 + t][:, g * DIL_O:(g + 1) * DIL_O] for t in range(3)], axis=1) for g in range(DIL_GROUPS)]
    w_gate = jnp.concatenate([p[6], p[7]], axis=1)
    return w_lat, w_dil, w_gate


def _merge_dw_in(dw_lat, dw_dil, dw_gate):
    parts = [dw_lat[:, _CQ], dw_lat[:, _CKV], _rope_unpad(dw_lat[:, _KPE])]
    for t in range(3):
        parts += [dw_dil[g][:, t * DIL_O:(t + 1) * DIL_O] for g in range(DIL_GROUPS)]
    parts.append(dw_gate)
    return _shard_cols(jnp.concatenate(parts, axis=1))


def _split_w_uq(w_uq_g):
    w = _unshard_cols(w_uq_g)
    K = w.shape[0]
    w = w.reshape(K, MLA_HEADS, QK_NOPE + QK_ROPE)
    return w[:, :, :QK_NOPE].reshape(K, MLA_HEADS * HEAD), _rope_pad(w[:, :, QK_NOPE:]).reshape(K, MLA_HEADS * HEAD)


def _merge_dw_uq(dw_n, dw_p):
    K = dw_n.shape[0]
    w = jnp.concatenate([dw_n.reshape(K, MLA_HEADS, HEAD), _rope_unpad(dw_p.reshape(K, MLA_HEADS, HEAD))], axis=-1)
    return _shard_cols(w.reshape(K, MLA_HEADS * (QK_NOPE + QK_ROPE)))


def _split_w_ukv(w_ukv_g):
    w = _unshard_cols(w_ukv_g)
    K = w.shape[0]
    w = w.reshape(K, MLA_HEADS, 2 * HEAD)
    return w[:, :, :HEAD].reshape(K, MLA_HEADS * HEAD), w[:, :, HEAD:].reshape(K, MLA_HEADS * HEAD)


def _merge_dw_ukv(dw_k, dw_v):
    K = dw_k.shape[0]
    w = jnp.concatenate([dw_k.reshape(K, MLA_HEADS, HEAD), dw_v.reshape(K, MLA_HEADS, HEAD)], axis=-1)
    return _shard_cols(w.reshape(K, MLA_HEADS * 2 * HEAD))


GATHER_GROUPS = (("w_in",), ("w_uq", "w_ukv", "w_o_mla", "w_o_dil", "w_out"), ("w_up", "w_down", "conv_w"))
SHARED_FETCH = ("w_in",)
REDUCE_GROUPS = (("w_down", "w_up"), ("w_out", "w_o_mla", "w_o_dil"), ("w_uq", "w_ukv", "w_in"))


def _local_step(x, tgt, W, fetch, emit):
    S, D = x.shape
    cos, sin_s = _rope_tables(S)
    w_lat, w_dil, w_gate = _split_w_in(fetch(0, x)["w_in"])

    h = _rmsnorm_fwd("attn_norm", x, W["attn_norm_g"])
    lat = _mm_nn("proj_lat", h, w_lat)
    qkv = [_mm_nn(f"proj_dil{g}", h, w_dil[g], o_dtype=BF) for g in range(DIL_GROUPS)]
    gpre = _mm_nn("proj_gate", h, w_gate, o_dtype=BF)
    WB = fetch(1, gpre)
    w_uqn, w_uqp = _split_w_uq(WB["w_uq"])
    w_k, w_v = _split_w_ukv(WB["w_ukv"])
    w_o_mla, w_o_dil = WB["w_o_mla"], WB["w_o_dil"]
    w_out = WB["w_out"].reshape(D, D)
    qn_, kvn, kpe = _mla_prep(lat, W["q_norm_g"], W["kv_norm_g"], cos, sin_s)
    q_nope = _mm_nn("q_nope", qn_, w_uqn, o_dtype=BF)
    q_pe = _rope("q_rope", _mm_nn("q_pe", qn_, w_uqp), cos, sin_s, False)
    k_nope = _mm_nn("k_nope", kvn, w_k, o_dtype=BF)
    v_mla = _mm_nn("v_mla", kvn, w_v, o_dtype=BF)
    attn_a, lse_a = _mla_fwd(q_nope, q_pe, k_nope, kpe, v_mla)
    dil = [_dil_fwd(qkv[g], g) for g in range(DIL_GROUPS)]
    attn_b, lse_b = _dil_combine([o for o, _ in dil], [l for _, l in dil])
    o_a = _mm_nn("o_mla", attn_a, w_o_mla, o_dtype=BF)
    o_b = _mm_nn("o_dil", attn_b, w_o_dil, o_dtype=BF)
    merge = _merge_fwd(gpre, W["b_gate"], o_a, o_b)
    x1 = _mm_nn("out_proj", merge, w_out, add=x)
    WC = fetch(2, merge)
    w_up = WC["w_up"]
    G4, _, C = w_up.shape
    w_down = WC["w_down"].reshape(G4 // 2, C, D)
    conv_w = WC["conv_w"]
    conv_b = W["conv_b"].reshape(G4, 1, C)
    h2 = _rmsnorm_fwd("ffn_norm", x1, W["ffn_norm_g"])
    u_pre = _up_fwd(h2, w_up)
    act, u = _ffn_act(u_pre, conv_w, conv_b)
    x2 = _down_fwd(act, w_down, x1)
    dx2, d_final_g, loss8 = _final_loss(x2, tgt, W["final_norm_g"])

    d_act = _down_dgrad(dx2, w_down)
    dw_down = _down_wgrad(act, dx2)
    du = _ffn_act_bwd(u, d_act)
    du_pre, d_conv_w, d_conv_b = _conv_bwd(du, u_pre, conv_w)
    dh2 = _up_dgrad(du_pre, w_up)
    dw_up = _up_wgrad(h2, du_pre)
    zero = emit(0, {"w_down": dw_down.reshape(N_CHIPS, (G4 // 2) * C // N_CHIPS, D), "w_up": dw_up})
    dx1, d_ffn_g = _rmsnorm_bwd("ffn_norm_bwd", dh2, x1, W["ffn_norm_g"] + zero, dx2)
    d_merge = _mm_nt("out_proj_dgrad", dx1, w_out, o_dtype=BF)
    dw_out = _mm_tn("out_proj_wgrad", merge, dx1)
    d_oa, d_ob, d_gpre, d_b_gate = _merge_bwd(d_merge, gpre, W["b_gate"], o_a, o_b)
    d_attn_a = _mm_nt("o_mla_dgrad", d_oa, w_o_mla, o_dtype=BF)
    dw_o_mla = _mm_tn("o_mla_wgrad", attn_a, d_oa, shards=N_CHIPS)
    d_attn_b = _mm_nt("o_dil_dgrad", d_ob, w_o_dil, o_dtype=BF)
    dw_o_dil = _mm_tn("o_dil_wgrad", attn_b, d_ob, shards=N_CHIPS)
    zero = emit(1, {"w_out": dw_out.reshape(N_CHIPS, D // N_CHIPS, D), "w_o_mla": dw_o_mla, "w_o_dil": dw_o_dil})
    q_norm_g = W["q_norm_g"] + zero
    delta_b = _dil_delta(d_attn_b, attn_b)
    d_qkv = [_dil_bwd(qkv[g], d_attn_b, lse_b, delta_b, g) for g in range(DIL_GROUPS)]
    dq_nope, dq_pe_rot, dk_nope, dv_mla, dkpe_rot = _mla_bwd(q_nope, q_pe, k_nope, kpe, v_mla, d_attn_a, attn_a, lse_a)
    dq_pe = _rope("q_rope_bwd", dq_pe_rot, cos, sin_s, True)
    d_qn = _mm_nt("q_pe_dgrad", dq_pe, w_uqp, add=_mm_nt("q_nope_dgrad", dq_nope, w_uqn))
    d_kvn = _mm_nt("v_dgrad", dv_mla, w_v, add=_mm_nt("k_nope_dgrad", dk_nope, w_k))
    dw_uq = _merge_dw_uq(_mm_tn("q_nope_wgrad", qn_, dq_nope), _mm_tn("q_pe_wgrad", qn_, dq_pe))
    dw_ukv = _merge_dw_ukv(_mm_tn("k_nope_wgrad", kvn, dk_nope), _mm_tn("v_wgrad", kvn, dv_mla))
    d_lat, d_q_g, d_kv_g = _mla_prep_bwd(lat, q_norm_g, W["kv_norm_g"], cos, sin_s, d_qn, d_kvn, dkpe_rot)
    dw_in = _merge_dw_in(_mm_tn("proj_lat_wgrad", h, d_lat),
                         [_mm_tn(f"proj_dil{g}_wgrad", h, d_qkv[g]) for g in range(DIL_GROUPS)],
                         _mm_tn("proj_gate_wgrad", h, d_gpre))
    zero = emit(2, {"w_uq": dw_uq, "w_ukv": dw_ukv, "w_in": dw_in})
    dh = _mm_nt("proj_lat_dgrad", d_lat, w_lat + zero.astype(BF))
    for g in range(DIL_GROUPS):
        dh = _mm_nt(f"proj_dil{g}_dgrad", d_qkv[g], w_dil[g], add=dh)
    dh = _mm_nt("proj_gate_dgrad", d_gpre, w_gate, add=dh)
    grad_x, d_attn_g = _rmsnorm_bwd("attn_norm_bwd", dh, x, W["attn_norm_g"], dx1)

    small = {"attn_norm_g": d_attn_g, "b_gate": d_b_gate, "q_norm_g": d_q_g, "kv_norm_g": d_kv_g,
             "ffn_norm_g": d_ffn_g, "conv_w": d_conv_w, "conv_b": d_conv_b.reshape(1, G4 * C),
             "final_norm_g": d_final_g}
    return loss8[0, 0], grad_x, small


BIG = ("w_in", "w_uq", "w_ukv", "w_o_mla", "w_o_dil", "w_out", "w_up", "w_down")
SMALL = ("attn_norm_g", "b_gate", "q_norm_g", "kv_norm_g", "ffn_norm_g", "conv_w", "conv_b", "final_norm_g")
WEIGHTS = ("attn_norm_g", "w_in", "b_gate", "q_norm_g", "w_uq", "kv_norm_g", "w_ukv", "w_o_mla", "w_o_dil",
           "w_out", "ffn_norm_g", "w_up", "conv_w", "conv_b", "w_down", "final_norm_g")
SMALL_ROWS = 8
COLUMN_MAJOR = ("w_in", "w_up")
HALF_AXIS = {"w_down": 1}


def _gather_start(shards):
    chip = 2 * lax.axis_index("x") + lax.axis_index("y")
    c = lax.axis_index("c")

    def prepare(names, zero):
        srcs, lands = [], []
        for n in names:
            s = shards[n] + zero
            s = s if n == "conv_w" else s.astype(BF)
            lands.append(lax.dynamic_update_slice(lax.empty((N_CHIPS,) + s.shape, s.dtype), s[None], (chip, 0, 0)))
            if n in SHARED_FETCH:
                s = lax.dynamic_slice_in_dim(s, c * (s.shape[0] // 2), s.shape[0] // 2, 0)
            srcs.append(s)
        return srcs, lands

    n0 = len(GATHER_GROUPS[0])
    srcs0, lands0 = prepare(GATHER_GROUPS[0], 0.0)
    sems0, srcs0, lands0, token = _exchange_start("gather_start0", "gather", srcs0, lands0, [list(range(n0))])
    srcs, lands = prepare([n for grp in GATHER_GROUPS[1:] for n in grp], token[0, 0])
    groups, at = [], 0
    for grp in GATHER_GROUPS[1:]:
        groups.append(list(range(at, at + len(grp))))
        at += len(grp)
    sems, srcs, lands, token1 = _exchange_start("gather_start1", "gather", srcs, lands, groups)

    def fetch(i, after):
        if i == 0:
            _, got = _exchange_wait("gather_wait0", "gather", srcs0, lands0, sems0[0], token1)
        else:
            idx = groups[i - 1]
            _, got = _exchange_wait(f"gather_wait{i}", "gather", [srcs[j] for j in idx], [lands[j] for j in idx],
                                    sems[i - 1], after)
        return {n: _pair_share(f"pair_share_{n}", g) if n in SHARED_FETCH else g for n, g in zip(GATHER_GROUPS[i], got)}

    return fetch, token[0, 0]


def _reduce_start(i, grads):
    names = REDUCE_GROUPS[i]
    hs = [_pair_sum(f"pair_sum_{n}", grads[n], HALF_AXIS.get(n, 0)) for n in names]
    lands = [lax.empty((3,) + h.shape[1:], h.dtype) for h in hs]
    sems, hs, lands, token = _exchange_start(f"reduce_start{i}", "scatter", hs, lands, [list(range(len(names)))])
    return (sems[0], hs, lands), token[0, 0]


def _reduce_finish(i, pending, after):
    sems, hs, lands = pending
    hs, lands = _exchange_wait(f"reduce_wait{i}", "scatter", hs, lands, sems, after)
    out = {}
    for n, h, landed in zip(REDUCE_GROUPS[i], hs, lands):
        out[n] = _chip_total_join(f"chip_total_{n}", h, landed, HALF_AXIS.get(n, 0))
    return out


def _reduce_small(small):
    flat = [small[n].reshape(-1) for n in SMALL]
    sizes = [f.shape[0] for f in flat]
    total = sum(sizes)
    width = -(-total // (SMALL_ROWS * LANE)) * LANE
    packed = jnp.concatenate(flat + [jnp.zeros((SMALL_ROWS * width - total,), F32)]).reshape(SMALL_ROWS, width)
    red = _allreduce_small(packed).reshape(-1)
    out, off = {}, 0
    for n, s in zip(SMALL, sizes):
        out[n] = red[off:off + s]
        off += s
    return out


def kernel(x, attn_norm_g, w_in, b_gate, q_norm_g, w_uq, kv_norm_g, w_ukv, w_o_mla, w_o_dil, w_out, ffn_norm_g, w_up, conv_w, conv_b, w_down, final_norm_g, loss_target, m_attn_norm_g, m_w_in, m_b_gate, m_q_norm_g, m_w_uq, m_kv_norm_g, m_w_ukv, m_w_o_mla, m_w_o_dil, m_w_out, m_ffn_norm_g, m_w_up, m_conv_w, m_conv_b, m_w_down, m_final_norm_g, v_attn_norm_g, v_w_in, v_b_gate, v_q_norm_g, v_w_uq, v_kv_norm_g, v_w_ukv, v_w_o_mla, v_w_o_dil, v_w_out, v_ffn_norm_g, v_w_up, v_conv_w, v_conv_b, v_w_down, v_final_norm_g):
    given = dict(attn_norm_g=attn_norm_g, w_in=w_in, b_gate=b_gate, q_norm_g=q_norm_g, w_uq=w_uq, kv_norm_g=kv_norm_g,
                 w_ukv=w_ukv, w_o_mla=w_o_mla, w_o_dil=w_o_dil, w_out=w_out, ffn_norm_g=ffn_norm_g, w_up=w_up,
                 conv_w=conv_w, conv_b=conv_b, w_down=w_down, final_norm_g=final_norm_g)
    moments_m = dict(attn_norm_g=m_attn_norm_g, w_in=m_w_in, b_gate=m_b_gate, q_norm_g=m_q_norm_g, w_uq=m_w_uq,
                     kv_norm_g=m_kv_norm_g, w_ukv=m_w_ukv, w_o_mla=m_w_o_mla, w_o_dil=m_w_o_dil, w_out=m_w_out,
                     ffn_norm_g=m_ffn_norm_g, w_up=m_w_up, conv_w=m_conv_w, conv_b=m_conv_b, w_down=m_w_down,
                     final_norm_g=m_final_norm_g)
    moments_v = dict(attn_norm_g=v_attn_norm_g, w_in=v_w_in, b_gate=v_b_gate, q_norm_g=v_q_norm_g, w_uq=v_w_uq,
                     kv_norm_g=v_kv_norm_g, w_ukv=v_w_ukv, w_o_mla=v_w_o_mla, w_o_dil=v_w_o_dil, w_out=v_w_out,
                     ffn_norm_g=v_ffn_norm_g, w_up=v_w_up, conv_w=v_conv_w, conv_b=v_conv_b, w_down=v_w_down,
                     final_norm_g=v_final_norm_g)

    fetch, zero = _gather_start({n: given[n][0] for n in BIG + ("conv_w",)})
    W = {n: given[n] for n in ("b_gate", "q_norm_g", "kv_norm_g", "ffn_norm_g", "conv_b")}
    W["attn_norm_g"] = given["attn_norm_g"] + zero
    W["final_norm_g"] = given["final_norm_g"].reshape(1, -1)

    pending = {}

    def emit(i, grads):
        pending[i], token = _reduce_start(i, grads)
        return token

    loss_part, grad_x, small = _local_step(x[0], loss_target[0], W, fetch, emit)
    loss = lax.psum(loss_part, ("x", "y", "c"))
    grads, delta, new_m, new_v = {}, {}, {}, {}

    def adamw(n, g):
        shp = given[n].shape
        two_d = (-1, shp[-1]) if len(shp) > 1 else (1, -1)
        view = (lambda a: a.reshape(two_d).T) if n in COLUMN_MAJOR else (lambda a: a.reshape(two_d))
        back = (lambda a: a.T.reshape(shp)) if n in COLUMN_MAJOR else (lambda a: a.reshape(shp))
        go, d, nm, nv = _adamw(f"adamw_{n}", view(given[n]), view(g), view(moments_m[n]), view(moments_v[n]))
        grads[n], delta[n], new_m[n], new_v[n] = back(go), back(d), back(nm), back(nv)

    after = grad_x
    for i in range(len(REDUCE_GROUPS)):
        for n, g in _reduce_finish(i, pending[i], after).items():
            adamw(n, g)
        after = delta[REDUCE_GROUPS[i][-1]]
    g_small = _reduce_small(small)
    chip = 2 * lax.axis_index("x") + lax.axis_index("y")
    for n in SMALL:
        if n == "conv_w":
            full = g_small[n].reshape(N_CHIPS, 3, -1)
            adamw(n, lax.dynamic_index_in_dim(full, chip, 0, keepdims=True))
        else:
            adamw(n, g_small[n])

    return (loss, grad_x[None], *[grads[n] for n in WEIGHTS], *[delta[n] for n in WEIGHTS],
            *[new_m[n] for n in WEIGHTS], *[new_v[n] for n in WEIGHTS])
```

```python
import functools
import math

import numpy as np
import jax
import jax.numpy as jnp
from jax import lax
from jax.experimental import pallas as pl
from jax.experimental.pallas import tpu as pltpu

F32 = jnp.float32
BF = jnp.bfloat16
MESH = pl.DeviceIdType.MESH

D_MODEL = 2048
MLA_HEADS = 8
QK_NOPE = 128
QK_ROPE = 64
Q_RANK = 512
KV_RANK = 256
ROPE_THETA = 10000.0
DIL_PATTERNS = ((128, 1), (512, 4), (2048, 16))
DIL_GROUPS = 3
DIL_HPG = 4
DIL_HEADS = 12
HEAD = 128
DIL_BLOCK = 128
ALIBI_MAX_BIAS = 8.0
NORM_EPS = 1e-6
N_CHIPS = 4
ADAM_LR = 0.001
ADAM_B1 = 0.9
ADAM_B2 = 0.999
ADAM_EPS = 1e-08
ADAM_WD = 0.01
ADAM_STEP = 10

LANE = 128
VMEM_LIMIT = 56 * 1024 * 1024
MLA_SCALE = (QK_NOPE + QK_ROPE) ** -0.5
DIL_SCALE = HEAD ** -0.5


def _params(*sem):
    return pltpu.CompilerParams(dimension_semantics=sem, vmem_limit_bytes=VMEM_LIMIT)


def _tile(n, pref):
    t = (pref // LANE) * LANE
    while t >= LANE:
        if n % t == 0:
            return t
        t -= LANE
    return n


NN = (((1,), (0,)), ((), ()))
NT = (((1,), (1,)), ((), ()))
TN = (((0,), (0,)), ((), ()))


def _mm_call(name, a, b, add, *, grid, a_spec, b_spec, add_spec, o_spec, o_shape, o_dtype, acc_shape, dims, nk):
    nax = len(grid)

    def body(*refs):
        if add is None:
            a_ref, b_ref, o_ref = refs[:3]
            c_ref = None
            scr = refs[3:]
        else:
            a_ref, b_ref, c_ref, o_ref = refs[:4]
            scr = refs[4:]
        prod = lax.dot_general(a_ref[...].astype(BF), b_ref[...].astype(BF), dims, preferred_element_type=F32)
        if nk == 1:
            if c_ref is not None:
                prod = prod + c_ref[...]
            o_ref[...] = prod.astype(o_ref.dtype)
        else:
            acc = scr[0]
            k = pl.program_id(nax - 1)

            @pl.when(k == 0)
            def _():
                if c_ref is not None:
                    acc[...] = prod + c_ref[...]
                else:
                    acc[...] = prod

            @pl.when(k > 0)
            def _():
                acc[...] += prod

            @pl.when(k == nk - 1)
            def _():
                o_ref[...] = acc[...].astype(o_ref.dtype)

    ins = [a, b] + ([] if add is None else [add])
    specs = [a_spec, b_spec] + ([] if add is None else [add_spec])
    sem = ("parallel",) * (nax - 1) + ("arbitrary",)
    return pl.pallas_call(
        body, name=name, grid=grid, in_specs=specs, out_specs=o_spec,
        out_shape=jax.ShapeDtypeStruct(o_shape, o_dtype),
        scratch_shapes=[] if nk == 1 else [pltpu.VMEM(acc_shape, F32)],
        compiler_params=_params(*sem),
    )(*ins)


def _mm_nn(name, a, b, *, add=None, o_dtype=F32):
    M, K = a.shape
    sharded = b.ndim == 3
    Ns = b.shape[-1]
    N = Ns * (b.shape[0] if sharded else 1)
    tm, tn, tk = _tile(M, 1024), _tile(Ns, 1024), _tile(K, 2048)
    per = Ns // tn
    nk = K // tk
    if sharded:
        b_spec = pl.BlockSpec((None, tk, tn), lambda i, j, k: (j // per, k, j % per))
    else:
        b_spec = pl.BlockSpec((tk, tn), lambda i, j, k: (k, j))
    return _mm_call(
        name, a, b, add, grid=(M // tm, N // tn, nk),
        a_spec=pl.BlockSpec((tm, tk), lambda i, j, k: (i, k)), b_spec=b_spec,
        add_spec=pl.BlockSpec((tm, tn), lambda i, j, k: (i, j)),
        o_spec=pl.BlockSpec((tm, tn), lambda i, j, k: (i, j)),
        o_shape=(M, N), o_dtype=o_dtype, acc_shape=(tm, tn), dims=NN, nk=nk)


def _mm_nt(name, a, b, *, add=None, o_dtype=F32):
    M, K = a.shape
    sharded = b.ndim == 3
    N, Ks = b.shape[-2], b.shape[-1]
    tm, tn, tk = _tile(M, 1024), _tile(N, 1024), _tile(Ks, 2048)
    per = Ks // tk
    nk = K // tk
    if sharded:
        b_spec = pl.BlockSpec((None, tn, tk), lambda i, j, k: (k // per, j, k % per))
    else:
        b_spec = pl.BlockSpec((tn, tk), lambda i, j, k: (j, k))
    return _mm_call(
        name, a, b, add, grid=(M // tm, N // tn, nk),
        a_spec=pl.BlockSpec((tm, tk), lambda i, j, k: (i, k)), b_spec=b_spec,
        add_spec=pl.BlockSpec((tm, tn), lambda i, j, k: (i, j)),
        o_spec=pl.BlockSpec((tm, tn), lambda i, j, k: (i, j)),
        o_shape=(M, N), o_dtype=o_dtype, acc_shape=(tm, tn), dims=NT, nk=nk)


def _mm_tn(name, a, b, *, shards=1, o_dtype=BF):
    S, M = a.shape
    N = b.shape[1]
    Ns = N // shards
    tm, tn, tk = _tile(M, 1024), _tile(Ns, 1024), _tile(S, 2048)
    per = Ns // tn
    nk = S // tk
    if shards > 1:
        o_spec = pl.BlockSpec((None, tm, tn), lambda i, j, k: (j // per, i, j % per))
        o_shape = (shards, M, Ns)
    else:
        o_spec = pl.BlockSpec((tm, tn), lambda i, j, k: (i, j))
        o_shape = (M, N)
    return _mm_call(
        name, a, b, None, grid=(M // tm, N // tn, nk),
        a_spec=pl.BlockSpec((tk, tm), lambda i, j, k: (k, i)),
        b_spec=pl.BlockSpec((tk, tn), lambda i, j, k: (k, j)),
        add_spec=None, o_spec=o_spec, o_shape=o_shape, o_dtype=o_dtype, acc_shape=(tm, tn), dims=TN, nk=nk)


def _up_fwd(h2, w_up):
    S, D = h2.shape
    G, _, C = w_up.shape
    tm = _tile(S, 512)
    return _mm_call(
        "up_fwd", h2, w_up, None, grid=(G, S // tm, 1),
        a_spec=pl.BlockSpec((tm, D), lambda g, i, k: (i, 0)),
        b_spec=pl.BlockSpec((None, D, C), lambda g, i, k: (g, 0, 0)),
        add_spec=None, o_spec=pl.BlockSpec((None, tm, C), lambda g, i, k: (g, i, 0)),
        o_shape=(G, S, C), o_dtype=BF, acc_shape=None, dims=NN, nk=1)


def _up_dgrad(du_pre, w_up):
    G, S, C = du_pre.shape
    D = w_up.shape[1]
    tm, tn = _tile(S, 1024), _tile(D, 1024)
    return _mm_call(
        "up_dgrad", du_pre, w_up, None, grid=(S // tm, D // tn, G),
        a_spec=pl.BlockSpec((None, tm, C), lambda i, j, g: (g, i, 0)),
        b_spec=pl.BlockSpec((None, tn, C), lambda i, j, g: (g, j, 0)),
        add_spec=None, o_spec=pl.BlockSpec((tm, tn), lambda i, j, g: (i, j)),
        o_shape=(S, D), o_dtype=F32, acc_shape=(tm, tn), dims=NT, nk=G)


def _up_wgrad(h2, du_pre):
    G, S, C = du_pre.shape
    D = h2.shape[1]
    tm, tk = _tile(D, 512), _tile(S, 2048)
    return _mm_call(
        "up_wgrad", h2, du_pre, None, grid=(G, D // tm, S // tk),
        a_spec=pl.BlockSpec((tk, tm), lambda g, i, k: (k, i)),
        b_spec=pl.BlockSpec((None, tk, C), lambda g, i, k: (g, k, 0)),
        add_spec=None, o_spec=pl.BlockSpec((None, tm, C), lambda g, i, k: (g, i, 0)),
        o_shape=(G, D, C), o_dtype=BF, acc_shape=(tm, C), dims=TN, nk=S // tk)


def _down_fwd(act, w_down, x1):
    G, S, C = act.shape
    D = w_down.shape[2]
    tm, tn = _tile(S, 1024), _tile(D, 1024)
    return _mm_call(
        "down_fwd", act, w_down, x1, grid=(S // tm, D // tn, G),
        a_spec=pl.BlockSpec((None, tm, C), lambda i, j, g: (g, i, 0)),
        b_spec=pl.BlockSpec((None, C, tn), lambda i, j, g: (g, 0, j)),
        add_spec=pl.BlockSpec((tm, tn), lambda i, j, g: (i, j)),
        o_spec=pl.BlockSpec((tm, tn), lambda i, j, g: (i, j)),
        o_shape=(S, D), o_dtype=F32, acc_shape=(tm, tn), dims=NN, nk=G)


def _down_dgrad(dx2, w_down):
    S, D = dx2.shape
    G, C, _ = w_down.shape
    tm = _tile(S, 512)
    return _mm_call(
        "down_dgrad", dx2, w_down, None, grid=(G, S // tm, 1),
        a_spec=pl.BlockSpec((tm, D), lambda g, i, k: (i, 0)),
        b_spec=pl.BlockSpec((None, C, D), lambda g, i, k: (g, 0, 0)),
        add_spec=None, o_spec=pl.BlockSpec((None, tm, C), lambda g, i, k: (g, i, 0)),
        o_shape=(G, S, C), o_dtype=BF, acc_shape=None, dims=NT, nk=1)


def _down_wgrad(act, dx2):
    G, S, C = act.shape
    D = dx2.shape[1]
    tn, tk = _tile(D, 512), _tile(S, 1024)
    return _mm_call(
        "down_wgrad", act, dx2, None, grid=(G, D // tn, S // tk),
        a_spec=pl.BlockSpec((None, tk, C), lambda g, j, k: (g, k, 0)),
        b_spec=pl.BlockSpec((tk, tn), lambda g, j, k: (k, j)),
        add_spec=None, o_spec=pl.BlockSpec((None, C, tn), lambda g, j, k: (g, 0, j)),
        o_shape=(G, C, D), o_dtype=BF, acc_shape=(C, tn), dims=TN, nk=S // tk)


def _row(ts, c):
    return pl.BlockSpec((ts, c), lambda i: (i, 0))


def _bcast(r, c):
    return pl.BlockSpec((r, c), lambda i: (0, 0))


def _accumulate(i, ref, val):
    @pl.when(i == 0)
    def _():
        ref[...] = val

    @pl.when(i > 0)
    def _():
        ref[...] += val


def _rstd(xv):
    return lax.rsqrt(jnp.mean(xv * xv, axis=-1, keepdims=True) + NORM_EPS)


def _rmsnorm_fwd(name, x, g):
    S, D = x.shape
    ts = _tile(S, 512)

    def body(x_ref, g_ref, o_ref):
        xv = x_ref[...]
        o_ref[...] = (xv * _rstd(xv) * g_ref[...]).astype(o_ref.dtype)

    return pl.pallas_call(
        body, name=name, grid=(S // ts,), in_specs=[_row(ts, D), _bcast(1, D)], out_specs=_row(ts, D),
        out_shape=jax.ShapeDtypeStruct((S, D), BF), compiler_params=_params("parallel"))(x, g)


def _norm_bwd_rows(dy, xv, g):
    r = _rstd(xv)
    xh = xv * r
    dxh = dy * g
    dx = r * (dxh - xh * jnp.mean(dxh * xh, axis=-1, keepdims=True))
    return dx, jnp.sum(dy * xh, axis=0, keepdims=True)


def _rmsnorm_bwd(name, dy, x, g, res):
    S, D = x.shape
    ts = _tile(S, 512)

    def body(dy_ref, x_ref, g_ref, res_ref, dx_ref, dg_ref):
        dx, dg = _norm_bwd_rows(dy_ref[...], x_ref[...], g_ref[...])
        dx_ref[...] = dx + res_ref[...]
        _accumulate(pl.program_id(0), dg_ref, dg)

    return pl.pallas_call(
        body, name=name, grid=(S // ts,),
        in_specs=[_row(ts, D), _row(ts, D), _bcast(1, D), _row(ts, D)],
        out_specs=[_row(ts, D), _bcast(1, D)],
        out_shape=[jax.ShapeDtypeStruct((S, D), F32), jax.ShapeDtypeStruct((1, D), F32)],
        compiler_params=_params("arbitrary"))(dy, x, g, res)


def _rope_tables(S):
    half = QK_ROPE // 2
    pos = jnp.arange(S, dtype=F32)
    inv_freq = ROPE_THETA ** (-jnp.arange(0, QK_ROPE, 2, dtype=F32) / QK_ROPE)
    ang = pos[:, None] * inv_freq[None, :]
    cos, sin = jnp.cos(ang), jnp.sin(ang)
    z = jnp.zeros((S, half), F32)
    return jnp.concatenate([cos, z, cos, z], axis=1), jnp.concatenate([-sin, z, sin, z], axis=1)


def _rope_lanes(x, cos, sin_signed, inverse):
    if inverse:
        return x * cos + pltpu.roll(x * sin_signed, LANE // 2, 1)
    return x * cos + pltpu.roll(x, LANE // 2, 1) * sin_signed


def _rope(name, x, cos, sin_signed, inverse):
    S, W = x.shape
    ts = _tile(S, 512)

    def body(x_ref, c_ref, s_ref, o_ref):
        c, s = c_ref[...], s_ref[...]
        for h in range(W // LANE):
            sl = slice(h * LANE, (h + 1) * LANE)
            o_ref[:, sl] = _rope_lanes(x_ref[:, sl], c, s, inverse).astype(o_ref.dtype)

    return pl.pallas_call(
        body, name=name, grid=(S // ts,), in_specs=[_row(ts, W), _row(ts, LANE), _row(ts, LANE)],
        out_specs=_row(ts, W), out_shape=jax.ShapeDtypeStruct((S, W), BF),
        compiler_params=_params("parallel"))(x, cos, sin_signed)


LAT_W = 1024
_CQ = slice(0, Q_RANK)
_CKV = slice(Q_RANK, Q_RANK + KV_RANK)
_KPE = slice(Q_RANK + KV_RANK, Q_RANK + KV_RANK + LANE)


def _mla_prep(lat, qg, kvg, cos, sin_signed):
    S = lat.shape[0]
    ts = _tile(S, 512)

    def body(lat_ref, qg_ref, kvg_ref, c_ref, s_ref, qn_ref, kvn_ref, kpe_ref):
        cq = lat_ref[:, _CQ]
        qn_ref[...] = (cq * _rstd(cq) * qg_ref[...]).astype(BF)
        ckv = lat_ref[:, _CKV]
        kvn_ref[...] = (ckv * _rstd(ckv) * kvg_ref[...]).astype(BF)
        kpe_ref[...] = _rope_lanes(lat_ref[:, _KPE], c_ref[...], s_ref[...], False).astype(BF)

    return pl.pallas_call(
        body, name="mla_prep", grid=(S // ts,),
        in_specs=[_row(ts, LAT_W), _bcast(1, Q_RANK), _bcast(1, KV_RANK), _row(ts, LANE), _row(ts, LANE)],
        out_specs=[_row(ts, Q_RANK), _row(ts, KV_RANK), _row(ts, LANE)],
        out_shape=[jax.ShapeDtypeStruct((S, Q_RANK), BF), jax.ShapeDtypeStruct((S, KV_RANK), BF),
                   jax.ShapeDtypeStruct((S, LANE), BF)],
        compiler_params=_params("parallel"))(lat, qg, kvg, cos, sin_signed)


def _mla_prep_bwd(lat, qg, kvg, cos, sin_signed, d_qn, d_kvn, d_kpe):
    S = lat.shape[0]
    ts = _tile(S, 512)

    def body(lat_ref, qg_ref, kvg_ref, c_ref, s_ref, dqn_ref, dkvn_ref, dkpe_ref, dlat_ref, dqg_ref, dkvg_ref):
        i = pl.program_id(0)
        dcq, dqg = _norm_bwd_rows(dqn_ref[...], lat_ref[:, _CQ], qg_ref[...])
        dckv, dkvg = _norm_bwd_rows(dkvn_ref[...], lat_ref[:, _CKV], kvg_ref[...])
        dlat_ref[:, _CQ] = dcq.astype(BF)
        dlat_ref[:, _CKV] = dckv.astype(BF)
        dkpe = dkpe_ref[0]
        for g in range(1, d_kpe.shape[0]):
            dkpe = dkpe + dkpe_ref[g]
        dlat_ref[:, _KPE] = _rope_lanes(dkpe, c_ref[...], s_ref[...], True).astype(BF)
        dlat_ref[:, _KPE.stop:] = jnp.zeros((ts, LAT_W - _KPE.stop), BF)
        _accumulate(i, dqg_ref, dqg)
        _accumulate(i, dkvg_ref, dkvg)

    return pl.pallas_call(
        body, name="mla_prep_bwd", grid=(S // ts,),
        in_specs=[_row(ts, LAT_W), _bcast(1, Q_RANK), _bcast(1, KV_RANK), _row(ts, LANE), _row(ts, LANE),
                  _row(ts, Q_RANK), _row(ts, KV_RANK), pl.BlockSpec((d_kpe.shape[0], ts, LANE), lambda i: (0, i, 0))],
        out_specs=[_row(ts, LAT_W), _bcast(1, Q_RANK), _bcast(1, KV_RANK)],
        out_shape=[jax.ShapeDtypeStruct((S, LAT_W), BF), jax.ShapeDtypeStruct((1, Q_RANK), F32),
                   jax.ShapeDtypeStruct((1, KV_RANK), F32)],
        compiler_params=_params("arbitrary"))(lat, qg, kvg, cos, sin_signed, d_qn, d_kvn, d_kpe)


def _sigmoid(z):
    return 1.0 / (1.0 + jnp.exp(-z))


def _merge_fwd(gpre, b_gate, o_a, o_b):
    S, D = o_a.shape
    ts = _tile(S, 256)

    def body(g_ref, b_ref, oa_ref, ob_ref, m_ref):
        ga = _sigmoid(g_ref[:, :D] + b_ref[:, :D])
        gb = _sigmoid(g_ref[:, D:] + b_ref[:, D:])
        m_ref[...] = (ga * oa_ref[...] + gb * ob_ref[...]).astype(BF)

    return pl.pallas_call(
        body, name="merge_fwd", grid=(S // ts,),
        in_specs=[_row(ts, 2 * D), _bcast(1, 2 * D), _row(ts, D), _row(ts, D)], out_specs=_row(ts, D),
        out_shape=jax.ShapeDtypeStruct((S, D), BF), compiler_params=_params("parallel"))(gpre, b_gate, o_a, o_b)


def _merge_bwd(d_merge, gpre, b_gate, o_a, o_b):
    S, D = o_a.shape
    ts = _tile(S, 256)

    def body(dm_ref, g_ref, b_ref, oa_ref, ob_ref, doa_ref, dob_ref, dg_ref, db_ref):
        dm = dm_ref[...]
        ga = _sigmoid(g_ref[:, :D] + b_ref[:, :D])
        gb = _sigmoid(g_ref[:, D:] + b_ref[:, D:])
        doa_ref[...] = (dm * ga).astype(BF)
        dob_ref[...] = (dm * gb).astype(BF)
        dga = dm * oa_ref[...] * ga * (1.0 - ga)
        dgb = dm * ob_ref[...] * gb * (1.0 - gb)
        dg_ref[:, :D] = dga.astype(BF)
        dg_ref[:, D:] = dgb.astype(BF)
        i = pl.program_id(0)
        part = jnp.concatenate([jnp.sum(dga, axis=0, keepdims=True), jnp.sum(dgb, axis=0, keepdims=True)], axis=1)
        _accumulate(i, db_ref, part)

    return pl.pallas_call(
        body, name="merge_bwd", grid=(S // ts,),
        in_specs=[_row(ts, D), _row(ts, 2 * D), _bcast(1, 2 * D), _row(ts, D), _row(ts, D)],
        out_specs=[_row(ts, D), _row(ts, D), _row(ts, 2 * D), _bcast(1, 2 * D)],
        out_shape=[jax.ShapeDtypeStruct((S, D), BF), jax.ShapeDtypeStruct((S, D), BF),
                   jax.ShapeDtypeStruct((S, 2 * D), BF), jax.ShapeDtypeStruct((1, 2 * D), F32)],
        compiler_params=_params("arbitrary"))(d_merge, gpre, b_gate, o_a, o_b)


def _final_loss(x2, tgt, gf):
    S, D = x2.shape
    ts = _tile(S, 512)

    def body(x_ref, t_ref, g_ref, dx_ref, dg_ref, loss_ref):
        i = pl.program_id(0)
        xv = x_ref[...]
        g = g_ref[...]
        y = xv * _rstd(xv) * g
        err = y - t_ref[...]
        dx, dg = _norm_bwd_rows(err * (1.0 / D), xv, g)
        dx_ref[...] = dx
        _accumulate(i, dg_ref, dg)
        part = 0.5 * jnp.sum(jnp.mean(err * err, axis=-1, keepdims=True), axis=0, keepdims=True)
        _accumulate(i, loss_ref, jnp.broadcast_to(part, (8, LANE)))

    return pl.pallas_call(
        body, name="final_loss", grid=(S // ts,),
        in_specs=[_row(ts, D), _row(ts, D), _bcast(1, D)],
        out_specs=[_row(ts, D), _bcast(1, D), _bcast(8, LANE)],
        out_shape=[jax.ShapeDtypeStruct((S, D), F32), jax.ShapeDtypeStruct((1, D), F32),
                   jax.ShapeDtypeStruct((8, LANE), F32)],
        compiler_params=_params("arbitrary"))(x2, tgt, gf)


HALO = 16


SUB = 8


def _shift_down(cur, prev, k, rows):
    out = pltpu.roll(cur, k, 0)
    head = out[:SUB]
    for j in range(k):
        head = jnp.where(rows == j, prev[HALO - k + j:HALO - k + j + 1, :], head)
    return jnp.concatenate([head, out[SUB:]], axis=0)


def _shift_up(cur, nxt, k, rows, ts):
    out = pltpu.roll(cur, ts - k, 0)
    tail = out[ts - SUB:]
    for j in range(k):
        tail = jnp.where(rows == SUB - k + j, nxt[j:j + 1, :], tail)
    return jnp.concatenate([out[:ts - SUB], tail], axis=0)


def _conv_rows(cur, prev, w, b, rows):
    return b + w[0:1, :] * _shift_down(cur, prev, 2, rows) + w[1:2, :] * _shift_down(cur, prev, 1, rows) + w[2:3, :] * cur


def _conv_specs(ts, C, shard_of):
    nh = ts // HALO
    cur = pl.BlockSpec((None, ts, C), lambda g, i: (shard_of(g), i, 0))
    prev = pl.BlockSpec((None, HALO, C), lambda g, i: (shard_of(g), jnp.maximum(i * nh - 1, 0), 0))
    return cur, prev


def _ffn_act(u_pre, conv_w, conv_b):
    G4, S, C = u_pre.shape
    G = G4 // 2
    ts = _tile(S, 256)

    def body(up_ref, upp_ref, gt_ref, gtp_ref, wu_ref, wg_ref, bu_ref, bg_ref, act_ref, u_ref):
        first = pl.program_id(1) == 0
        rows = lax.broadcasted_iota(jnp.int32, (SUB, C), 0)
        pu = jnp.where(first, 0.0, upp_ref[...].astype(F32))
        pg = jnp.where(first, 0.0, gtp_ref[...].astype(F32))
        up = _conv_rows(up_ref[...].astype(F32), pu, wu_ref[...], bu_ref[...], rows)
        gate = _conv_rows(gt_ref[...].astype(F32), pg, wg_ref[...], bg_ref[...], rows)
        act_ref[...] = (gate * _sigmoid(gate) * up).astype(BF)
        u_ref[0] = up.astype(BF)
        u_ref[1] = gate.astype(BF)

    cur_u, prev_u = _conv_specs(ts, C, lambda g: g)
    cur_g, prev_g = _conv_specs(ts, C, lambda g: g + G)
    w_u = pl.BlockSpec((None, 3, C), lambda g, i: (g, 0, 0))
    w_g = pl.BlockSpec((None, 3, C), lambda g, i: (g + G, 0, 0))
    b_u = pl.BlockSpec((None, 1, C), lambda g, i: (g, 0, 0))
    b_g = pl.BlockSpec((None, 1, C), lambda g, i: (g + G, 0, 0))
    pair = pl.BlockSpec((2, None, ts, C), lambda g, i: (0, g, i, 0))
    act, u = pl.pallas_call(
        body, name="ffn_act", grid=(G, S // ts),
        in_specs=[cur_u, prev_u, cur_g, prev_g, w_u, w_g, b_u, b_g],
        out_specs=[pl.BlockSpec((None, ts, C), lambda g, i: (g, i, 0)), pair],
        out_shape=[jax.ShapeDtypeStruct((G, S, C), BF), jax.ShapeDtypeStruct((2, G, S, C), BF)],
        compiler_params=_params("parallel", "parallel"))(u_pre, u_pre, u_pre, u_pre, conv_w, conv_w, conv_b, conv_b)
    return act, u


def _ffn_act_bwd(u, d_act):
    _, G, S, C = u.shape
    ts = _tile(S, 256)

    def body(u_ref, da_ref, du_ref):
        up, gate = u_ref[0].astype(F32), u_ref[1].astype(F32)
        sg = _sigmoid(gate)
        da = da_ref[...].astype(F32)
        du_ref[0] = (da * (gate * sg)).astype(BF)
        du_ref[1] = (da * up * (sg * (1.0 + gate * (1.0 - sg)))).astype(BF)

    pair = pl.BlockSpec((2, None, ts, C), lambda g, i: (0, g, i, 0))
    du = pl.pallas_call(
        body, name="ffn_act_bwd", grid=(G, S // ts),
        in_specs=[pair, pl.BlockSpec((None, ts, C), lambda g, i: (g, i, 0))], out_specs=pair,
        out_shape=jax.ShapeDtypeStruct((2, G, S, C), BF),
        compiler_params=_params("parallel", "parallel"))(u, d_act)
    return du.reshape(2 * G, S, C)


def _conv_bwd(du, u_pre, conv_w):
    G4, S, C = du.shape
    ts = _tile(S, 256)
    nh = ts // HALO
    last_halo = S // HALO - 1

    def body(du_ref, dun_ref, u_ref, w_ref, dpre_ref, dw_ref, db_ref):
        i = pl.program_id(1)
        rows = lax.broadcasted_iota(jnp.int32, (SUB, C), 0)
        du_c = du_ref[...].astype(F32)
        nxt = jnp.where(i == pl.num_programs(1) - 1, 0.0, dun_ref[...].astype(F32))
        up1, up2 = _shift_up(du_c, nxt, 1, rows, ts), _shift_up(du_c, nxt, 2, rows, ts)
        w = w_ref[...]
        dpre_ref[...] = (w[2:3, :] * du_c + w[1:2, :] * up1 + w[0:1, :] * up2).astype(BF)
        u_c = u_ref[...].astype(F32)
        dw = jnp.concatenate([
            jnp.sum(up2 * u_c, axis=0, keepdims=True),
            jnp.sum(up1 * u_c, axis=0, keepdims=True),
            jnp.sum(du_c * u_c, axis=0, keepdims=True)], axis=0)
        _accumulate(i, dw_ref, dw)
        _accumulate(i, db_ref, jnp.sum(du_c, axis=0, keepdims=True))

    cur = pl.BlockSpec((None, ts, C), lambda g, i: (g, i, 0))
    nxt = pl.BlockSpec((None, HALO, C), lambda g, i: (g, jnp.minimum((i + 1) * nh, last_halo), 0))
    return pl.pallas_call(
        body, name="conv_bwd", grid=(G4, S // ts),
        in_specs=[cur, nxt, cur, pl.BlockSpec((None, 3, C), lambda g, i: (g, 0, 0))],
        out_specs=[cur, pl.BlockSpec((None, 3, C), lambda g, i: (g, 0, 0)), pl.BlockSpec((None, 1, C), lambda g, i: (g, 0, 0))],
        out_shape=[jax.ShapeDtypeStruct((G4, S, C), BF), jax.ShapeDtypeStruct((G4, 3, C), F32),
                   jax.ShapeDtypeStruct((G4, 1, C), F32)],
        compiler_params=_params("parallel", "arbitrary"))(du, du, u_pre, conv_w)


MLA_T = 1024
MLA_HB = 4
MLA_BWD_HB = 2


def _mla_pairs(n, by_row):
    if by_row:
        pairs = [(i, j) for i in range(n) for j in range(i + 1)]
    else:
        pairs = [(i, j) for j in range(n) for i in range(j, n)]
    return jnp.asarray([p[0] for p in pairs], jnp.int32), jnp.asarray([p[1] for p in pairs], jnp.int32)


def _mla_specs(hb):
    q = pl.BlockSpec((MLA_T, hb * HEAD), lambda g, t, it, jt: (it[t], g))
    k = pl.BlockSpec((MLA_T, hb * HEAD), lambda g, t, it, jt: (jt[t], g))
    kpe = pl.BlockSpec((MLA_T, HEAD), lambda g, t, it, jt: (jt[t], 0))
    lse = pl.BlockSpec((hb, MLA_T, LANE), lambda g, t, it, jt: (g, it[t], 0))
    return q, k, kpe, lse


def _mla_head(ref, hh):
    return ref[:, hh * HEAD:(hh + 1) * HEAD]


LOG2E = math.log2(math.e)
MLA_EXP2_SCALE = MLA_SCALE * LOG2E


def _mla_scores(qn_ref, qp_ref, kn_ref, kpe, hh, ok):
    q = jnp.concatenate([_mla_head(qn_ref, hh), _mla_head(qp_ref, hh)], axis=1)
    k = jnp.concatenate([_mla_head(kn_ref, hh), kpe], axis=1)
    s = lax.dot_general(q, k, NT, preferred_element_type=F32)
    return q, k, s if ok is None else jnp.where(ok, s, -jnp.inf)


def _mla_diagonal_mask():
    row = lax.broadcasted_iota(jnp.int32, (MLA_T, MLA_T), 0)
    col = lax.broadcasted_iota(jnp.int32, (MLA_T, MLA_T), 1)
    return col <= row


def _mla_step(i, j, step):
    @pl.when(j < i)
    def _():
        step(None)

    @pl.when(j == i)
    def _():
        step(_mla_diagonal_mask())


def _mla_fwd(qn, qp, kn, kpe, v):
    S = qn.shape[0]
    it, jt = _mla_pairs(S // MLA_T, True)

    def body(it_ref, jt_ref, qn_ref, qp_ref, kn_ref, kpe_ref, v_ref, o_ref, lse_ref, m_scr, acc_scr):
        t = pl.program_id(1)
        i, j = it_ref[t], jt_ref[t]

        @pl.when(j == 0)
        def _():
            m_scr[...] = jnp.full(m_scr.shape, -jnp.inf, F32)
            acc_scr[...] = jnp.zeros(acc_scr.shape, F32)

        def step(ok):
            kpe_v = kpe_ref[...]
            ones = jnp.ones((MLA_T, HEAD), BF)
            state = [(m_scr[hh], acc_scr[hh]) for hh in range(MLA_HB)]
            new = []
            for hh in range(MLA_HB):
                m_prev, acc = state[hh]
                _, _, s = _mla_scores(qn_ref, qp_ref, kn_ref, kpe_v, hh, ok)
                m_new = jnp.maximum(m_prev, jnp.max(s, axis=1, keepdims=True))
                p = jnp.exp2((s - m_new) * MLA_EXP2_SCALE).astype(BF)
                v1 = jnp.concatenate([_mla_head(v_ref, hh), ones], axis=1)
                alpha = jnp.exp2((m_prev - m_new) * MLA_EXP2_SCALE)
                new.append((m_new, alpha * acc + lax.dot_general(p, v1, NN, preferred_element_type=F32)))
            for hh in range(MLA_HB):
                m_scr[hh], acc_scr[hh] = new[hh]

        _mla_step(i, j, step)

        @pl.when(j == i)
        def _():
            for hh in range(MLA_HB):
                l = acc_scr[hh, :, HEAD:]
                o_ref[:, hh * HEAD:(hh + 1) * HEAD] = (acc_scr[hh, :, :HEAD] / l).astype(BF)
                lse_ref[hh] = m_scr[hh] * MLA_SCALE + jnp.log(l)

    qspec, kspec, kpespec, lsespec = _mla_specs(MLA_HB)
    grid_spec = pltpu.PrefetchScalarGridSpec(
        num_scalar_prefetch=2, grid=(MLA_HEADS // MLA_HB, it.shape[0]),
        in_specs=[qspec, qspec, kspec, kpespec, kspec], out_specs=[qspec, lsespec],
        scratch_shapes=[pltpu.VMEM((MLA_HB, MLA_T, 1), F32), pltpu.VMEM((MLA_HB, MLA_T, 2 * HEAD), F32)])
    return pl.pallas_call(
        body, name="mla_fwd", grid_spec=grid_spec,
        out_shape=[jax.ShapeDtypeStruct((S, MLA_HEADS * HEAD), BF), jax.ShapeDtypeStruct((MLA_HEADS, S, LANE), F32)],
        compiler_params=_params("parallel", "arbitrary"))(it, jt, qn, qp, kn, kpe, v)


def _mla_p_ds(qn_ref, qp_ref, kn_ref, kpe, v_ref, do_ref, o_ref, lse_ref, hh, ok):
    q, k, s = _mla_scores(qn_ref, qp_ref, kn_ref, kpe, hh, ok)
    p = jnp.exp2(s * MLA_EXP2_SCALE - lse_ref[hh][:, 0:1] * LOG2E)
    do = _mla_head(do_ref, hh)
    delta = jnp.sum(do.astype(F32) * _mla_head(o_ref, hh).astype(F32), axis=1, keepdims=True)
    dp = lax.dot_general(do, _mla_head(v_ref, hh), NT, preferred_element_type=F32)
    ds = p * (dp - delta) * MLA_SCALE
    return q, k, p, ds, do


def _mla_bwd(qn, qp, kn, kpe, v, do, o, lse):
    S = qn.shape[0]
    nq = S // MLA_T
    hb = MLA_BWD_HB
    it, jt = _mla_pairs(nq, False)

    def body(it_ref, jt_ref, qn_ref, qp_ref, kn_ref, kpe_ref, v_ref, do_ref, o_ref, lse_ref,
             dqn_ref, dqp_ref, dkn_ref, dv_ref, dkpe_ref, dq_acc, dk_acc, dv_acc, stage_n, stage_p, osem):
        t = pl.program_id(1)
        i, j = it_ref[t], jt_ref[t]

        @pl.when(t == 0)
        def _():
            dq_acc[...] = jnp.zeros(dq_acc.shape, F32)

        @pl.when(i == j)
        def _():
            dk_acc[...] = jnp.zeros(dk_acc.shape, F32)
            dv_acc[...] = jnp.zeros(dv_acc.shape, F32)

        def step(ok):
            kpe_v = kpe_ref[...]
            for hh in range(hb):
                q, k, p, ds, do_h = _mla_p_ds(qn_ref, qp_ref, kn_ref, kpe_v, v_ref, do_ref, o_ref, lse_ref, hh, ok)
                ds = ds.astype(BF)
                dv_acc[hh] += lax.dot_general(p.astype(BF), do_h, TN, preferred_element_type=F32)
                dk_acc[hh] += lax.dot_general(ds, q, TN, preferred_element_type=F32)
                dq_acc[i, hh] += lax.dot_general(ds, k, NN, preferred_element_type=F32)

        _mla_step(i, j, step)

        @pl.when(i == j)
        def _():
            for hh in range(hb):
                stage_n[:, hh * HEAD:(hh + 1) * HEAD] = dq_acc[i, hh, :, :HEAD].astype(BF)
                stage_p[:, hh * HEAD:(hh + 1) * HEAD] = dq_acc[i, hh, :, HEAD:]
            rows = pl.ds(pl.multiple_of(i * MLA_T, MLA_T), MLA_T)
            cols = pl.ds(pl.multiple_of(pl.program_id(0) * hb * HEAD, LANE), hb * HEAD)
            out_n = pltpu.make_async_copy(stage_n, dqn_ref.at[rows, cols], osem.at[0])
            out_p = pltpu.make_async_copy(stage_p, dqp_ref.at[rows, cols], osem.at[1])
            out_n.start()
            out_p.start()
            out_n.wait()
            out_p.wait()

        @pl.when(i == nq - 1)
        def _():
            dkpe = dk_acc[0, :, HEAD:]
            for hh in range(hb):
                dkn_ref[:, hh * HEAD:(hh + 1) * HEAD] = dk_acc[hh, :, :HEAD].astype(BF)
                dv_ref[:, hh * HEAD:(hh + 1) * HEAD] = dv_acc[hh].astype(BF)
                if hh:
                    dkpe = dkpe + dk_acc[hh, :, HEAD:]
            dkpe_ref[...] = dkpe

    qspec, kspec, kpespec, lsespec = _mla_specs(hb)
    dkpespec = pl.BlockSpec((None, MLA_T, HEAD), lambda g, t, it, jt: (g, jt[t], 0))
    grid_spec = pltpu.PrefetchScalarGridSpec(
        num_scalar_prefetch=2, grid=(MLA_HEADS // hb, it.shape[0]),
        in_specs=[qspec, qspec, kspec, kpespec, kspec, qspec, qspec, lsespec],
        out_specs=[ANY, ANY, kspec, kspec, dkpespec],
        scratch_shapes=[pltpu.VMEM((nq, hb, MLA_T, 2 * HEAD), F32), pltpu.VMEM((hb, MLA_T, 2 * HEAD), F32),
                        pltpu.VMEM((hb, MLA_T, HEAD), F32), pltpu.VMEM((MLA_T, hb * HEAD), BF),
                        pltpu.VMEM((MLA_T, hb * HEAD), F32), pltpu.SemaphoreType.DMA((2,))])
    wide = jax.ShapeDtypeStruct((S, MLA_HEADS * HEAD), BF)
    return pl.pallas_call(
        body, name="mla_bwd", grid_spec=grid_spec,
        out_shape=[wide, jax.ShapeDtypeStruct((S, MLA_HEADS * HEAD), F32), wide, wide,
                   jax.ShapeDtypeStruct((MLA_HEADS // hb, S, HEAD), F32)],
        compiler_params=_params("parallel", "arbitrary"))(it, jt, qn, qp, kn, kpe, v, do, o, lse)


DIL_W = 3 * DIL_HPG * HEAD
DIL_O = DIL_HPG * HEAD
DIL_STEP_BLOCKS = 4


def _dil_slopes(g):
    return [2.0 ** (-ALIBI_MAX_BIAS * (g * DIL_HPG + hh + 1) / DIL_HEADS) for hh in range(DIL_HPG)]


def _dil_bias(dil):
    p = lax.broadcasted_iota(jnp.int32, (DIL_BLOCK, DIL_BLOCK), 0)
    kk = lax.broadcasted_iota(jnp.int32, (DIL_BLOCK, DIL_BLOCK), 1)
    jc = p - kk
    dist_c = (dil * jc).astype(F32)
    dist_p = (dil * (jc + DIL_BLOCK)).astype(F32)
    return jc >= 0, jc <= 0, dist_c, dist_p


def _dil_bias2(dil):
    p = lax.broadcasted_iota(jnp.int32, (DIL_BLOCK, 2 * DIL_BLOCK), 0)
    kk = lax.broadcasted_iota(jnp.int32, (DIL_BLOCK, 2 * DIL_BLOCK), 1)
    j = p + DIL_BLOCK - kk
    return (j >= 0) & (j <= DIL_BLOCK), kk < DIL_BLOCK, (dil * j).astype(F32)


def _dil_head(blk, hh):
    q = blk[:, hh * HEAD:(hh + 1) * HEAD]
    k = blk[:, DIL_O + hh * HEAD:DIL_O + (hh + 1) * HEAD]
    v = blk[:, 2 * DIL_O + hh * HEAD:2 * DIL_O + (hh + 1) * HEAD]
    return q, k, v


def _dil_s(q, k, slope, dist, ok):
    s = lax.dot_general(q, k, NT, preferred_element_type=F32) * DIL_SCALE - slope * dist
    return jnp.where(ok, s, -jnp.inf)


def _dil_view(a, dil):
    S, W = a.shape
    return a.reshape(S // dil, dil * W)


def _dil_fwd(qkv, g):
    _, dil = DIL_PATTERNS[g]
    S = qkv.shape[0]
    L = S // dil
    nb = L // DIL_BLOCK
    slopes = _dil_slopes(g)

    bb = min(DIL_STEP_BLOCKS, nb)
    rows = bb * DIL_BLOCK

    def body(cur_ref, prev_ref, o_ref, lse_ref):
        n = pl.program_id(1)
        ok, in_prev, dist = _dil_bias2(dil)
        for b in range(bb):
            if b == 0:
                both = jnp.concatenate([prev_ref[...], cur_ref[0:DIL_BLOCK, :]], axis=0)
                ok_b = ok & (~in_prev | (n > 0))
            else:
                both = cur_ref[(b - 1) * DIL_BLOCK:(b + 1) * DIL_BLOCK, :]
                ok_b = ok
            for hh in range(DIL_HPG):
                q, _, _ = _dil_head(both[DIL_BLOCK:], hh)
                _, k2, v2 = _dil_head(both, hh)
                s = _dil_s(q, k2, slopes[hh], dist, ok_b)
                m = jnp.max(s, axis=1, keepdims=True)
                p = jnp.exp(s - m)
                l = jnp.sum(p, axis=1, keepdims=True)
                o = lax.dot_general(p.astype(BF), v2, NN, preferred_element_type=F32) / l
                rs, sl = slice(b * DIL_BLOCK, (b + 1) * DIL_BLOCK), slice(hh * HEAD, (hh + 1) * HEAD)
                o_ref[rs, sl] = o
                lse_ref[rs, sl] = jnp.broadcast_to(m + jnp.log(l), (DIL_BLOCK, HEAD))

    ospec = pl.BlockSpec((rows, DIL_O), lambda r, n: (n, r))
    o, lse = pl.pallas_call(
        body, name=f"dil_fwd{g}", grid=(dil, nb // bb),
        in_specs=[pl.BlockSpec((rows, DIL_W), lambda r, n: (n, r)),
                  pl.BlockSpec((DIL_BLOCK, DIL_W), lambda r, n: (jnp.maximum(n * bb - 1, 0), r))],
        out_specs=[ospec, ospec],
        out_shape=[jax.ShapeDtypeStruct((L, dil * DIL_O), F32), jax.ShapeDtypeStruct((L, dil * DIL_O), F32)],
        compiler_params=_params("parallel", "parallel"))(_dil_view(qkv, dil), _dil_view(qkv, dil))
    return o.reshape(S, DIL_O), lse.reshape(S, DIL_O)


def _dil_combine(os_, lses):
    S = os_[0].shape[0]
    ts = _tile(S, 512)

    def body(o0, o1, o2, l0, l1, l2, out_ref, lse_ref):
        a, b, c = l0[...], l1[...], l2[...]
        m = jnp.maximum(jnp.maximum(a, b), c)
        ea, eb, ec = jnp.exp(a - m), jnp.exp(b - m), jnp.exp(c - m)
        tot = ea + eb + ec
        out_ref[...] = ((ea * o0[...] + eb * o1[...] + ec * o2[...]) / tot).astype(BF)
        lse_ref[...] = m + jnp.log(tot)

    return pl.pallas_call(
        body, name="dil_combine", grid=(S // ts,), in_specs=[_row(ts, DIL_O)] * 6,
        out_specs=[_row(ts, DIL_O), _row(ts, DIL_O)],
        out_shape=[jax.ShapeDtypeStruct((S, DIL_O), BF), jax.ShapeDtypeStruct((S, DIL_O), F32)],
        compiler_params=_params("parallel"))(*os_, *lses)


def _dil_delta(do, out):
    S = do.shape[0]
    ts = _tile(S, 512)

    def body(do_ref, o_ref, d_ref):
        for hh in range(DIL_HPG):
            sl = slice(hh * HEAD, (hh + 1) * HEAD)
            d = jnp.sum(do_ref[:, sl].astype(F32) * o_ref[:, sl].astype(F32), axis=1, keepdims=True)
            d_ref[:, sl] = jnp.broadcast_to(d, (ts, HEAD))

    return pl.pallas_call(
        body, name="dil_delta", grid=(S // ts,), in_specs=[_row(ts, DIL_O)] * 2, out_specs=_row(ts, DIL_O),
        out_shape=jax.ShapeDtypeStruct((S, DIL_O), F32), compiler_params=_params("parallel"))(do, out)


def _dil_bwd(qkv, do, lse, delta, g):
    _, dil = DIL_PATTERNS[g]
    S = qkv.shape[0]
    L = S // dil
    nb = L // DIL_BLOCK
    slopes = _dil_slopes(g)

    def pair(q, k, v, do_h, lse_h, delta_h, slope, dist, ok):
        s = _dil_s(q, k, slope, dist, ok)
        p = jnp.exp(s - lse_h)
        dp = lax.dot_general(do_h, v, NT, preferred_element_type=F32)
        ds = (p * (dp - delta_h) * DIL_SCALE).astype(BF)
        return p.astype(BF), ds

    bb = min(DIL_STEP_BLOCKS, nb)
    rows = bb * DIL_BLOCK
    steps = nb // bb

    def body(cur_ref, prev_ref, next_ref, doc_ref, don_ref, lsec_ref, lsen_ref, dlc_ref, dln_ref, out_ref):
        n = pl.program_id(1)
        ok2, in_prev, dist2 = _dil_bias2(dil)
        _, ok_p0, _, dist_p = _dil_bias(dil)
        for b in range(bb):
            rs = slice(b * DIL_BLOCK, (b + 1) * DIL_BLOCK)
            rn = slice((b + 1) * DIL_BLOCK, (b + 2) * DIL_BLOCK)
            two = slice(b * DIL_BLOCK, (b + 2) * DIL_BLOCK)
            first, last = b == 0, b == bb - 1
            if first:
                keys = jnp.concatenate([prev_ref[...], cur_ref[rs, :]], axis=0)
                ok_ab = ok2 & (~in_prev | (n > 0))
            else:
                keys = cur_ref[(b - 1) * DIL_BLOCK:(b + 1) * DIL_BLOCK, :]
                ok_ab = ok2
            qrows = jnp.concatenate([cur_ref[rs, :], next_ref[...]], axis=0) if last else cur_ref[two, :]
            ok_n = ok_p0 & (n < steps - 1) if last else ok_p0
            for hh in range(DIL_HPG):
                sl = slice(hh * HEAD, (hh + 1) * HEAD)
                q2, _, _ = _dil_head(qrows, hh)
                _, k2, v2 = _dil_head(keys, hh)
                q, qn, kc, vc = q2[:DIL_BLOCK], q2[DIL_BLOCK:], k2[DIL_BLOCK:], v2[DIL_BLOCK:]
                do2 = jnp.concatenate([doc_ref[rs, sl], don_ref[:, sl]], axis=0) if last else doc_ref[two, sl]
                do_c, do_n = do2[:DIL_BLOCK], do2[DIL_BLOCK:]
                lse_c = lsec_ref[rs, sl][:, 0:1]
                lse_n = (lsen_ref[:, sl] if last else lsec_ref[rn, sl])[:, 0:1]
                dl_c = dlc_ref[rs, sl][:, 0:1]
                dl_n = (dln_ref[:, sl] if last else dlc_ref[rn, sl])[:, 0:1]
                p_ab, ds_ab = pair(q, k2, v2, do_c, lse_c, dl_c, slopes[hh], dist2, ok_ab)
                p_n, ds_n = pair(qn, kc, vc, do_n, lse_n, dl_n, slopes[hh], dist_p, ok_n)
                dq = lax.dot_general(ds_ab, k2, NN, preferred_element_type=F32)
                dk = lax.dot_general(jnp.concatenate([ds_ab[:, DIL_BLOCK:], ds_n], axis=0), q2, TN, preferred_element_type=F32)
                dv = lax.dot_general(jnp.concatenate([p_ab[:, DIL_BLOCK:], p_n], axis=0), do2, TN, preferred_element_type=F32)
                out_ref[rs, sl] = dq.astype(BF)
                out_ref[rs, DIL_O + hh * HEAD:DIL_O + (hh + 1) * HEAD] = dk.astype(BF)
                out_ref[rs, 2 * DIL_O + hh * HEAD:2 * DIL_O + (hh + 1) * HEAD] = dv.astype(BF)

    cur_w = pl.BlockSpec((rows, DIL_W), lambda r, n: (n, r))
    prev_w = pl.BlockSpec((DIL_BLOCK, DIL_W), lambda r, n: (jnp.maximum(n * bb - 1, 0), r))
    next_w = pl.BlockSpec((DIL_BLOCK, DIL_W), lambda r, n: (jnp.minimum((n + 1) * bb, nb - 1), r))
    cur_o = pl.BlockSpec((rows, DIL_O), lambda r, n: (n, r))
    next_o = pl.BlockSpec((DIL_BLOCK, DIL_O), lambda r, n: (jnp.minimum((n + 1) * bb, nb - 1), r))
    qv, dov, lsev, dlv = _dil_view(qkv, dil), _dil_view(do, dil), _dil_view(lse, dil), _dil_view(delta, dil)
    out = pl.pallas_call(
        body, name=f"dil_bwd{g}", grid=(dil, steps),
        in_specs=[cur_w, prev_w, next_w, cur_o, next_o, cur_o, next_o, cur_o, next_o],
        out_specs=cur_w, out_shape=jax.ShapeDtypeStruct((L, dil * DIL_W), BF),
        compiler_params=_params("parallel", "parallel"))(qv, qv, qv, dov, dov, lsev, lsev, dlv, dlv)
    return out.reshape(S, DIL_W)


def _adamw(name, w, g, m, v):
    R, C = w.shape
    tr, tc = _adamw_block(R, C)

    def body(w_ref, g_ref, m_ref, v_ref, go_ref, d_ref, nm_ref, nv_ref):
        gv = g_ref[...]
        go_ref[...] = gv
        nm = ADAM_B1 * m_ref[...] + (1.0 - ADAM_B1) * gv
        nv = ADAM_B2 * v_ref[...] + (1.0 - ADAM_B2) * (gv * gv)
        m_hat = nm / (1.0 - ADAM_B1 ** ADAM_STEP)
        v_hat = nv / (1.0 - ADAM_B2 ** ADAM_STEP)
        d_ref[...] = -ADAM_LR * (m_hat / (jnp.sqrt(v_hat) + ADAM_EPS) + ADAM_WD * w_ref[...])
        nm_ref[...] = nm
        nv_ref[...] = nv

    spec = pl.BlockSpec((tr, tc), lambda i, j: (i, j))
    shp = jax.ShapeDtypeStruct((R, C), F32)
    return pl.pallas_call(
        body, name=name, grid=(R // tr, C // tc), in_specs=[spec] * 4, out_specs=[spec] * 4, out_shape=[shp] * 4,
        compiler_params=_params("parallel", "parallel"))(w, g, m, v)


ADAMW_BLOCK_ELEMS = 640 * 1024


def _adamw_block(R, C):
    if R * C <= ADAMW_BLOCK_ELEMS:
        return R, C
    tr = _tile_rows(R, max(8, ADAMW_BLOCK_ELEMS // C))
    tc = _tile(C, max(LANE, ADAMW_BLOCK_ELEMS // R))
    if tr * C >= R * tc or R * tc > ADAMW_BLOCK_ELEMS:
        return tr, C
    return R, tc


def _tile_rows(n, pref, mult=8):
    t = (pref // mult) * mult
    while t >= mult:
        if n % t == 0:
            return t
        t -= mult
    return n


ANY = pl.BlockSpec(memory_space=pl.ANY)


def _place():
    x, y, c = lax.axis_index("x"), lax.axis_index("y"), lax.axis_index("c")
    chips = [(1 - x, y), (x, 1 - y), (1 - x, 1 - y)]
    chip_idx = [2 * cx + cy for cx, cy in chips]
    return x, y, c, 2 * x + y, chips, chip_idx


def _rcopy(src, dst, ssem, rsem, dev):
    return pltpu.make_async_remote_copy(src_ref=src, dst_ref=dst, send_sem=ssem, recv_sem=rsem,
                                        device_id=dev, device_id_type=MESH)


HBM = pl.BlockSpec(memory_space=pltpu.HBM)
SEM = pl.BlockSpec(memory_space=pltpu.SEMAPHORE)
EFFECT = pltpu.SideEffectType.DATAFLOW_SIDE_EFFECTING


def _split_copies(kind, srcs, lands, ssem, rsem):
    _, _, c, me, chips, chip_idx = _place()
    cps = []
    for i in range(len(srcs)):
        for k in range(3):
            if kind == "gather":
                rows = srcs[i].shape[0]
                if rows == lands[i].shape[1]:
                    src, dst = srcs[i], lands[i].at[me]
                else:
                    src, dst = srcs[i], lands[i].at[me, pl.ds(pl.multiple_of(c * rows, 16), rows)]
            else:
                src, dst = srcs[i].at[chip_idx[k]], lands[i].at[k]
            cps.append(_rcopy(src, dst, ssem.at[3 * i + k], rsem.at[3 * i + k], (*chips[k], c)))
    return cps


def _exchange_start(name, kind, srcs, lands, groups):
    n, ng = len(srcs), len(groups)

    def body(*refs):
        src_refs, land_refs = refs[:n], refs[n:2 * n]
        sems = refs[2 * n:2 * n + 2 * ng]
        token = refs[-1]
        for gi, grp in enumerate(groups):
            cps = _split_copies(kind, [src_refs[i] for i in grp], [land_refs[i] for i in grp], sems[2 * gi], sems[2 * gi + 1])
            for cp in cps:
                cp.start()
        token[...] = jnp.zeros_like(token)

    arrays = list(srcs) + list(lands)
    out_shape = []
    for grp in groups:
        out_shape += [pltpu.SemaphoreType.DMA((3 * len(grp),)), pltpu.SemaphoreType.DMA((3 * len(grp),))]
    out_shape += [pltpu.HBM(a.shape, a.dtype) for a in arrays] + [jax.ShapeDtypeStruct((8, LANE), F32)]
    outs = pl.pallas_call(
        body, name=name, out_shape=out_shape, in_specs=[HBM] * (2 * n),
        out_specs=[SEM] * (2 * ng) + [HBM] * (2 * n) + [pl.BlockSpec(memory_space=pltpu.VMEM)],
        input_output_aliases={i: 2 * ng + i for i in range(2 * n)},
        compiler_params=pltpu.CompilerParams(has_side_effects=EFFECT),
    )(*[pltpu.with_memory_space_constraint(a, pltpu.HBM) for a in arrays])
    sems = [(outs[2 * gi], outs[2 * gi + 1]) for gi in range(ng)]
    thru = outs[2 * ng:2 * ng + 2 * n]
    return sems, thru[:n], thru[n:], outs[-1]


def _exchange_wait(name, kind, srcs, lands, sems, after):
    n = len(srcs)

    def body(*refs):
        cps = _split_copies(kind, refs[:n], refs[n:2 * n], refs[2 * n], refs[2 * n + 1])
        for cp in cps:
            cp.wait_send()
            cp.wait_recv()

    arrays = list(srcs) + list(lands)
    outs = pl.pallas_call(
        body, name=name, out_shape=[pltpu.HBM(a.shape, a.dtype) for a in arrays],
        in_specs=[HBM] * (2 * n) + [SEM, SEM, ANY], out_specs=[HBM] * (2 * n),
        input_output_aliases={i: i for i in range(2 * n)},
        compiler_params=pltpu.CompilerParams(has_side_effects=EFFECT),
    )(*arrays, sems[0], sems[1], after)
    return outs[:n], outs[n:]


EXCHANGE_CHUNK_BYTES = 3 * 1024 * 1024


def _half_geometry(R, C, axis):
    Rp, Cp = (R // 2, C) if axis == 0 else (R, C // 2)
    tr = _tile_rows(Rp, max(16, EXCHANGE_CHUNK_BYTES // (2 * Cp)), 16)
    return Rp, Cp, tr, Rp // tr


def _pair_sum(name, g, axis):
    G, R, C = g.shape
    Rp, Cp, tr, nb = _half_geometry(R, C, axis)
    steps = G * nb

    def half_block(s, b, h):
        return (s, h * nb + b, 0) if axis == 0 else (s, b, h)

    def body(c_ref, keep_ref, give_ref, out_ref, land, ssem, rsem, credit):
        x, y, c = lax.axis_index("x"), lax.axis_index("y"), lax.axis_index("c")
        sib = (x, y, 1 - c)
        t = pl.program_id(0) * nb + pl.program_id(1)
        slot = t % 2

        @pl.when(t >= 2)
        def _():
            pl.semaphore_wait(credit, 1)

        cp = _rcopy(give_ref.at[0], land.at[slot], ssem.at[slot], rsem.at[slot], sib)
        cp.start()
        cp.wait_recv()
        out_ref[...] = (keep_ref[...].astype(F32) + land[slot].astype(F32)).astype(BF)

        @pl.when(t + 2 < steps)
        def _():
            pl.semaphore_signal(credit, 1, device_id=sib, device_id_type=MESH)

        cp.wait_send()

    blk = (None, tr, Cp)
    grid_spec = pltpu.PrefetchScalarGridSpec(
        num_scalar_prefetch=1, grid=(G, nb),
        in_specs=[pl.BlockSpec(blk, lambda s, b, c_ref: half_block(s, b, c_ref[0])),
                  pl.BlockSpec((1, tr, Cp), lambda s, b, c_ref: half_block(s, b, 1 - c_ref[0]))],
        out_specs=pl.BlockSpec(blk, lambda s, b, c_ref: (s, b, 0)),
        scratch_shapes=[pltpu.VMEM((2, tr, Cp), BF), pltpu.SemaphoreType.DMA((2,)), pltpu.SemaphoreType.DMA((2,)),
                        pltpu.SemaphoreType.REGULAR])
    c_arr = lax.axis_index("c").astype(jnp.int32).reshape(1)
    return pl.pallas_call(
        body, name=name, grid_spec=grid_spec, out_shape=jax.ShapeDtypeStruct((G, Rp, Cp), BF),
        compiler_params=_params("arbitrary", "arbitrary"))(c_arr, g, g)


def _chip_total_join(name, h, landed, axis):
    G, Rp, Cp = h.shape
    R, C = (2 * Rp, Cp) if axis == 0 else (Rp, 2 * Cp)
    tr = _tile_rows(Rp, max(16, EXCHANGE_CHUNK_BYTES // (4 * Cp)), 16)
    nb = Rp // tr

    def body(me_ref, own_ref, l0_ref, l1_ref, l2_ref, full, stage, ssem, rsem, lsem):
        x, y, c = lax.axis_index("x"), lax.axis_index("y"), lax.axis_index("c")
        sib = (x, y, 1 - c)
        b = pl.program_id(0)

        def place(half, r0, rows):
            if axis == 0:
                return full.at[pl.ds(pl.multiple_of(half * Rp + r0, 8), rows), :]
            return full.at[pl.ds(pl.multiple_of(r0, 8), rows), pl.ds(pl.multiple_of(half * Cp, LANE), Cp)]

        def copies(step):
            s = step % 2
            mine = place(c, step * tr, tr)
            return pltpu.make_async_copy(stage.at[s], mine, lsem.at[s]), _rcopy(stage.at[s], mine, ssem.at[s], rsem, sib)

        @pl.when(b >= 2)
        def _():
            loc, rem = copies(b - 2)
            loc.wait()
            rem.wait_send()

        acc = own_ref[...].astype(F32)
        for r in (l0_ref, l1_ref, l2_ref):
            acc = acc + r[...].astype(F32)
        stage[b % 2] = acc
        loc, rem = copies(b)
        loc.start()
        rem.start()

        @pl.when(b == nb - 1)
        def _():
            for step in range(max(0, nb - 2), nb):
                loc, rem = copies(step)
                loc.wait()
                rem.wait_send()
            theirs = place(1 - c, 0, Rp)
            _rcopy(theirs, theirs, ssem.at[0], rsem, sib).wait_recv()

    blk = (None, tr, Cp)
    grid_spec = pltpu.PrefetchScalarGridSpec(
        num_scalar_prefetch=1, grid=(nb,),
        in_specs=[pl.BlockSpec(blk, lambda b, me_ref: (me_ref[0], b, 0))]
        + [pl.BlockSpec(blk, functools.partial(lambda b, me_ref, k: (k, b, 0), k=k)) for k in range(3)],
        out_specs=ANY,
        scratch_shapes=[pltpu.VMEM((2, tr, Cp), F32), pltpu.SemaphoreType.DMA((2,)), pltpu.SemaphoreType.DMA,
                        pltpu.SemaphoreType.DMA((2,))])
    me = (2 * lax.axis_index("x") + lax.axis_index("y")).astype(jnp.int32).reshape(1)
    return pl.pallas_call(
        body, name=name, grid_spec=grid_spec, out_shape=jax.ShapeDtypeStruct((R, C), F32),
        compiler_params=_params("arbitrary"))(me, h, landed, landed, landed)


def _pair_share(name, land):
    G, R, C = land.shape
    Rh = R // 2
    tr = _tile_rows(Rh, max(16, EXCHANGE_CHUNK_BYTES // (2 * C)), 16)
    chunks = [(k, b) for k in range(3) for b in range(Rh // tr)]

    def body(src, dst, buf, lsem, ssem, rsem):
        x, y, c, _, _, chip_idx = _place()
        sib = (x, y, 1 - c)

        def region(ref, k, half, r0, rows):
            return ref.at[chip_idx[k], pl.ds(pl.multiple_of(half * Rh + r0, 16), rows)]

        def load(t):
            k, b = chunks[t]
            return pltpu.make_async_copy(region(src, k, c, b * tr, tr), buf.at[t % 2], lsem.at[t % 2])

        def send(t):
            k, b = chunks[t]
            return _rcopy(buf.at[t % 2], region(dst, k, c, b * tr, tr), ssem.at[t % 2], rsem.at[k], sib)

        load(0).start()
        for t in range(len(chunks)):
            load(t).wait()
            if t + 1 < len(chunks):
                if t >= 1:
                    send(t - 1).wait_send()
                load(t + 1).start()
            send(t).start()
        for t in range(max(0, len(chunks) - 2), len(chunks)):
            send(t).wait_send()
        for k in range(3):
            theirs = region(dst, k, 1 - c, 0, Rh)
            _rcopy(theirs, theirs, ssem.at[0], rsem.at[k], sib).wait_recv()

    return pl.pallas_call(
        body, name=name, in_specs=[ANY], out_specs=ANY, out_shape=jax.ShapeDtypeStruct(land.shape, land.dtype),
        input_output_aliases={0: 0},
        scratch_shapes=[pltpu.VMEM((2, tr, C), land.dtype), pltpu.SemaphoreType.DMA((2,)), pltpu.SemaphoreType.DMA((2,)),
                        pltpu.SemaphoreType.DMA((3,))],
    )(land)


def _allreduce_small(v):
    R, K = v.shape
    ndev = 8

    def body(v_ref, o_ref, land, ssem, rsem):
        x, y, c = lax.axis_index("x"), lax.axis_index("y"), lax.axis_index("c")
        me = 4 * x + 2 * y + c
        land[me] = v_ref[...]
        cps = []
        for r in range(1, ndev):
            fx, fy, fc = (r >> 2) & 1, (r >> 1) & 1, r & 1
            peer = (x ^ fx, y ^ fy, c ^ fc)
            cp = _rcopy(v_ref, land.at[me], ssem.at[r - 1], rsem.at[r - 1], peer)
            cp.start()
            cps.append((cp, 4 * peer[0] + 2 * peer[1] + peer[2], r))
        for cp, src, r in cps:
            cp.wait_send()
            _rcopy(v_ref, land.at[src], ssem.at[r - 1], rsem.at[r - 1], (x, y, c)).wait_recv()
        acc = land[0]
        for d in range(1, ndev):
            acc = acc + land[d]
        o_ref[...] = acc

    vm = pl.BlockSpec(memory_space=pltpu.VMEM)
    return pl.pallas_call(
        body, name="allreduce_small", in_specs=[vm], out_specs=vm, out_shape=jax.ShapeDtypeStruct((R, K), F32),
        scratch_shapes=[pltpu.VMEM((ndev, R, K), F32), pltpu.SemaphoreType.DMA((ndev - 1,)), pltpu.SemaphoreType.DMA((ndev - 1,))],
    )(v)


IN_SPLITS = (Q_RANK, KV_RANK, QK_ROPE, DIL_HEADS * HEAD, DIL_HEADS * HEAD, DIL_HEADS * HEAD, D_MODEL, D_MODEL)
IN_OFF = tuple(int(v) for v in np.cumsum((0,) + IN_SPLITS))


def _unshard_cols(g):
    G, K, Ns = g.shape
    return g.transpose(1, 0, 2).reshape(K, G * Ns)


def _shard_cols(w):
    K, N = w.shape
    return w.reshape(K, N_CHIPS, N // N_CHIPS).transpose(1, 0, 2)


def _rope_pad(w):
    half = QK_ROPE // 2
    z = jnp.zeros(w.shape[:-1] + (half,), w.dtype)
    return jnp.concatenate([w[..., :half], z, w[..., half:], z], axis=-1)


def _rope_unpad(w):
    half = QK_ROPE // 2
    return jnp.concatenate([w[..., :half], w[..., 2 * half:3 * half]], axis=-1)


def _split_w_in(w_in_g):
    w = _unshard_cols(w_in_g)
    K = w.shape[0]
    p = [w[:, IN_OFF[i]:IN_OFF[i + 1]] for i in range(8)]
    w_lat = jnp.concatenate([p[0], p[1], _rope_pad(p[2]), jnp.zeros((K, LAT_W - _KPE.stop), w.dtype)], axis=1)
    w_dil = [jnp.concatenate([p[3 + t][:, g * DIL_O:(g + 1) * DIL_O] for t in range(3)], axis=1) for g in range(DIL_GROUPS)]
    w_gate = jnp.concatenate([p[6], p[7]], axis=1)
    return w_lat, w_dil, w_gate


def _merge_dw_in(dw_lat, dw_dil, dw_gate):
    parts = [dw_lat[:, _CQ], dw_lat[:, _CKV], _rope_unpad(dw_lat[:, _KPE])]
    for t in range(3):
        parts += [dw_dil[g][:, t * DIL_O:(t + 1) * DIL_O] for g in range(DIL_GROUPS)]
    parts.append(dw_gate)
    return _shard_cols(jnp.concatenate(parts, axis=1))


def _split_w_uq(w_uq_g):
    w = _unshard_cols(w_uq_g)
    K = w.shape[0]
    w = w.reshape(K, MLA_HEADS, QK_NOPE + QK_ROPE)
    return w[:, :, :QK_NOPE].reshape(K, MLA_HEADS * HEAD), _rope_pad(w[:, :, QK_NOPE:]).reshape(K, MLA_HEADS * HEAD)


def _merge_dw_uq(dw_n, dw_p):
    K = dw_n.shape[0]
    w = jnp.concatenate([dw_n.reshape(K, MLA_HEADS, HEAD), _rope_unpad(dw_p.reshape(K, MLA_HEADS, HEAD))], axis=-1)
    return _shard_cols(w.reshape(K, MLA_HEADS * (QK_NOPE + QK_ROPE)))


def _split_w_ukv(w_ukv_g):
    w = _unshard_cols(w_ukv_g)
    K = w.shape[0]
    w = w.reshape(K, MLA_HEADS, 2 * HEAD)
    return w[:, :, :HEAD].reshape(K, MLA_HEADS * HEAD), w[:, :, HEAD:].reshape(K, MLA_HEADS * HEAD)


def _merge_dw_ukv(dw_k, dw_v):
    K = dw_k.shape[0]
    w = jnp.concatenate([dw_k.reshape(K, MLA_HEADS, HEAD), dw_v.reshape(K, MLA_HEADS, HEAD)], axis=-1)
    return _shard_cols(w.reshape(K, MLA_HEADS * 2 * HEAD))


GATHER_GROUPS = (("w_in",), ("w_uq", "w_ukv", "w_o_mla", "w_o_dil", "w_out"), ("w_up", "w_down", "conv_w"))
SHARED_FETCH = ("w_in",)
REDUCE_GROUPS = (("w_down", "w_up"), ("w_out", "w_o_mla", "w_o_dil"), ("w_uq", "w_ukv", "w_in"))


def _local_step(x, tgt, W, fetch, emit):
    S, D = x.shape
    cos, sin_s = _rope_tables(S)
    w_lat, w_dil, w_gate = _split_w_in(fetch(0, x)["w_in"])

    h = _rmsnorm_fwd("attn_norm", x, W["attn_norm_g"])
    lat = _mm_nn("proj_lat", h, w_lat)
    qkv = [_mm_nn(f"proj_dil{g}", h, w_dil[g], o_dtype=BF) for g in range(DIL_GROUPS)]
    gpre = _mm_nn("proj_gate", h, w_gate, o_dtype=BF)
    WB = fetch(1, gpre)
    w_uqn, w_uqp = _split_w_uq(WB["w_uq"])
    w_k, w_v = _split_w_ukv(WB["w_ukv"])
    w_o_mla, w_o_dil = WB["w_o_mla"], WB["w_o_dil"]
    w_out = WB["w_out"].reshape(D, D)
    qn_, kvn, kpe = _mla_prep(lat, W["q_norm_g"], W["kv_norm_g"], cos, sin_s)
    q_nope = _mm_nn("q_nope", qn_, w_uqn, o_dtype=BF)
    q_pe = _rope("q_rope", _mm_nn("q_pe", qn_, w_uqp), cos, sin_s, False)
    k_nope = _mm_nn("k_nope", kvn, w_k, o_dtype=BF)
    v_mla = _mm_nn("v_mla", kvn, w_v, o_dtype=BF)
    attn_a, lse_a = _mla_fwd(q_nope, q_pe, k_nope, kpe, v_mla)
    dil = [_dil_fwd(qkv[g], g) for g in range(DIL_GROUPS)]
    attn_b, lse_b = _dil_combine([o for o, _ in dil], [l for _, l in dil])
    o_a = _mm_nn("o_mla", attn_a, w_o_mla, o_dtype=BF)
    o_b = _mm_nn("o_dil", attn_b, w_o_dil, o_dtype=BF)
    merge = _merge_fwd(gpre, W["b_gate"], o_a, o_b)
    x1 = _mm_nn("out_proj", merge, w_out, add=x)
    WC = fetch(2, merge)
    w_up = WC["w_up"]
    G4, _, C = w_up.shape
    w_down = WC["w_down"].reshape(G4 // 2, C, D)
    conv_w = WC["conv_w"]
    conv_b = W["conv_b"].reshape(G4, 1, C)
    h2 = _rmsnorm_fwd("ffn_norm", x1, W["ffn_norm_g"])
    u_pre = _up_fwd(h2, w_up)
    act, u = _ffn_act(u_pre, conv_w, conv_b)
    x2 = _down_fwd(act, w_down, x1)
    dx2, d_final_g, loss8 = _final_loss(x2, tgt, W["final_norm_g"])

    d_act = _down_dgrad(dx2, w_down)
    dw_down = _down_wgrad(act, dx2)
    du = _ffn_act_bwd(u, d_act)
    du_pre, d_conv_w, d_conv_b = _conv_bwd(du, u_pre, conv_w)
    dh2 = _up_dgrad(du_pre, w_up)
    dw_up = _up_wgrad(h2, du_pre)
    zero = emit(0, {"w_down": dw_down.reshape(N_CHIPS, (G4 // 2) * C // N_CHIPS, D), "w_up": dw_up})
    dx1, d_ffn_g = _rmsnorm_bwd("ffn_norm_bwd", dh2, x1, W["ffn_norm_g"] + zero, dx2)
    d_merge = _mm_nt("out_proj_dgrad", dx1, w_out, o_dtype=BF)
    dw_out = _mm_tn("out_proj_wgrad", merge, dx1)
    d_oa, d_ob, d_gpre, d_b_gate = _merge_bwd(d_merge, gpre, W["b_gate"], o_a, o_b)
    d_attn_a = _mm_nt("o_mla_dgrad", d_oa, w_o_mla, o_dtype=BF)
    dw_o_mla = _mm_tn("o_mla_wgrad", attn_a, d_oa, shards=N_CHIPS)
    d_attn_b = _mm_nt("o_dil_dgrad", d_ob, w_o_dil, o_dtype=BF)
    dw_o_dil = _mm_tn("o_dil_wgrad", attn_b, d_ob, shards=N_CHIPS)
    zero = emit(1, {"w_out": dw_out.reshape(N_CHIPS, D // N_CHIPS, D), "w_o_mla": dw_o_mla, "w_o_dil": dw_o_dil})
    q_norm_g = W["q_norm_g"] + zero
    delta_b = _dil_delta(d_attn_b, attn_b)
    d_qkv = [_dil_bwd(qkv[g], d_attn_b, lse_b, delta_b, g) for g in range(DIL_GROUPS)]
    dq_nope, dq_pe_rot, dk_nope, dv_mla, dkpe_rot = _mla_bwd(q_nope, q_pe, k_nope, kpe, v_mla, d_attn_a, attn_a, lse_a)
    dq_pe = _rope("q_rope_bwd", dq_pe_rot, cos, sin_s, True)
    d_qn = _mm_nt("q_pe_dgrad", dq_pe, w_uqp, add=_mm_nt("q_nope_dgrad", dq_nope, w_uqn))
    d_kvn = _mm_nt("v_dgrad", dv_mla, w_v, add=_mm_nt("k_nope_dgrad", dk_nope, w_k))
    dw_uq = _merge_dw_uq(_mm_tn("q_nope_wgrad", qn_, dq_nope), _mm_tn("q_pe_wgrad", qn_, dq_pe))
    dw_ukv = _merge_dw_ukv(_mm_tn("k_nope_wgrad", kvn, dk_nope), _mm_tn("v_wgrad", kvn, dv_mla))
    d_lat, d_q_g, d_kv_g = _mla_prep_bwd(lat, q_norm_g, W["kv_norm_g"], cos, sin_s, d_qn, d_kvn, dkpe_rot)
    dw_in = _merge_dw_in(_mm_tn("proj_lat_wgrad", h, d_lat),
                         [_mm_tn(f"proj_dil{g}_wgrad", h, d_qkv[g]) for g in range(DIL_GROUPS)],
                         _mm_tn("proj_gate_wgrad", h, d_gpre))
    zero = emit(2, {"w_uq": dw_uq, "w_ukv": dw_ukv, "w_in": dw_in})
    dh = _mm_nt("proj_lat_dgrad", d_lat, w_lat + zero.astype(BF))
    for g in range(DIL_GROUPS):
        dh = _mm_nt(f"proj_dil{g}_dgrad", d_qkv[g], w_dil[g], add=dh)
    dh = _mm_nt("proj_gate_dgrad", d_gpre, w_gate, add=dh)
    grad_x, d_attn_g = _rmsnorm_bwd("attn_norm_bwd", dh, x, W["attn_norm_g"], dx1)

    small = {"attn_norm_g": d_attn_g, "b_gate": d_b_gate, "q_norm_g": d_q_g, "kv_norm_g": d_kv_g,
             "ffn_norm_g": d_ffn_g, "conv_w": d_conv_w, "conv_b": d_conv_b.reshape(1, G4 * C),
             "final_norm_g": d_final_g}
    return loss8[0, 0], grad_x, small


BIG = ("w_in", "w_uq", "w_ukv", "w_o_mla", "w_o_dil", "w_out", "w_up", "w_down")
SMALL = ("attn_norm_g", "b_gate", "q_norm_g", "kv_norm_g", "ffn_norm_g", "conv_w", "conv_b", "final_norm_g")
WEIGHTS = ("attn_norm_g", "w_in", "b_gate", "q_norm_g", "w_uq", "kv_norm_g", "w_ukv", "w_o_mla", "w_o_dil",
           "w_out", "ffn_norm_g", "w_up", "conv_w", "conv_b", "w_down", "final_norm_g")
SMALL_ROWS = 8
COLUMN_MAJOR = ("w_in", "w_up")
HALF_AXIS = {"w_down": 1}


def _gather_start(shards):
    chip = 2 * lax.axis_index("x") + lax.axis_index("y")
    c = lax.axis_index("c")

    def prepare(names, zero):
        srcs, lands = [], []
        for n in names:
            s = shards[n] + zero
            s = s if n == "conv_w" else s.astype(BF)
            lands.append(lax.dynamic_update_slice(lax.empty((N_CHIPS,) + s.shape, s.dtype), s[None], (chip, 0, 0)))
            if n in SHARED_FETCH:
                s = lax.dynamic_slice_in_dim(s, c * (s.shape[0] // 2), s.shape[0] // 2, 0)
            srcs.append(s)
        return srcs, lands

    n0 = len(GATHER_GROUPS[0])
    srcs0, lands0 = prepare(GATHER_GROUPS[0], 0.0)
    sems0, srcs0, lands0, token = _exchange_start("gather_start0", "gather", srcs0, lands0, [list(range(n0))])
    srcs, lands = prepare([n for grp in GATHER_GROUPS[1:] for n in grp], token[0, 0])
    groups, at = [], 0
    for grp in GATHER_GROUPS[1:]:
        groups.append(list(range(at, at + len(grp))))
        at += len(grp)
    sems, srcs, lands, token1 = _exchange_start("gather_start1", "gather", srcs, lands, groups)

    def fetch(i, after):
        if i == 0:
            _, got = _exchange_wait("gather_wait0", "gather", srcs0, lands0, sems0[0], token1)
        else:
            idx = groups[i - 1]
            _, got = _exchange_wait(f"gather_wait{i}", "gather", [srcs[j] for j in idx], [lands[j] for j in idx],
                                    sems[i - 1], after)
        return {n: _pair_share(f"pair_share_{n}", g) if n in SHARED_FETCH else g for n, g in zip(GATHER_GROUPS[i], got)}

    return fetch, token[0, 0]


def _reduce_start(i, grads):
    names = REDUCE_GROUPS[i]
    hs = [_pair_sum(f"pair_sum_{n}", grads[n], HALF_AXIS.get(n, 0)) for n in names]
    lands = [lax.empty((3,) + h.shape[1:], h.dtype) for h in hs]
    sems, hs, lands, token = _exchange_start(f"reduce_start{i}", "scatter", hs, lands, [list(range(len(names)))])
    return (sems[0], hs, lands), token[0, 0]


def _reduce_finish(i, pending, after):
    sems, hs, lands = pending
    hs, lands = _exchange_wait(f"reduce_wait{i}", "scatter", hs, lands, sems, after)
    out = {}
    for n, h, landed in zip(REDUCE_GROUPS[i], hs, lands):
        out[n] = _chip_total_join(f"chip_total_{n}", h, landed, HALF_AXIS.get(n, 0))
    return out


def _reduce_small(small):
    flat = [small[n].reshape(-1) for n in SMALL]
    sizes = [f.shape[0] for f in flat]
    total = sum(sizes)
    width = -(-total // (SMALL_ROWS * LANE)) * LANE
    packed = jnp.concatenate(flat + [jnp.zeros((SMALL_ROWS * width - total,), F32)]).reshape(SMALL_ROWS, width)
    red = _allreduce_small(packed).reshape(-1)
    out, off = {}, 0
    for n, s in zip(SMALL, sizes):
        out[n] = red[off:off + s]
        off += s
    return out


def kernel(x, attn_norm_g, w_in, b_gate, q_norm_g, w_uq, kv_norm_g, w_ukv, w_o_mla, w_o_dil, w_out, ffn_norm_g, w_up, conv_w, conv_b, w_down, final_norm_g, loss_target, m_attn_norm_g, m_w_in, m_b_gate, m_q_norm_g, m_w_uq, m_kv_norm_g, m_w_ukv, m_w_o_mla, m_w_o_dil, m_w_out, m_ffn_norm_g, m_w_up, m_conv_w, m_conv_b, m_w_down, m_final_norm_g, v_attn_norm_g, v_w_in, v_b_gate, v_q_norm_g, v_w_uq, v_kv_norm_g, v_w_ukv, v_w_o_mla, v_w_o_dil, v_w_out, v_ffn_norm_g, v_w_up, v_conv_w, v_conv_b, v_w_down, v_final_norm_g):
    given = dict(attn_norm_g=attn_norm_g, w_in=w_in, b_gate=b_gate, q_norm_g=q_norm_g, w_uq=w_uq, kv_norm_g=kv_norm_g,
                 w_ukv=w_ukv, w_o_mla=w_o_mla, w_o_dil=w_o_dil, w_out=w_out, ffn_norm_g=ffn_norm_g, w_up=w_up,
                 conv_w=conv_w, conv_b=conv_b, w_down=w_down, final_norm_g=final_norm_g)
    moments_m = dict(attn_norm_g=m_attn_norm_g, w_in=m_w_in, b_gate=m_b_gate, q_norm_g=m_q_norm_g, w_uq=m_w_uq,
                     kv_norm_g=m_kv_norm_g, w_ukv=m_w_ukv, w_o_mla=m_w_o_mla, w_o_dil=m_w_o_dil, w_out=m_w_out,
                     ffn_norm_g=m_ffn_norm_g, w_up=m_w_up, conv_w=m_conv_w, conv_b=m_conv_b, w_down=m_w_down,
                     final_norm_g=m_final_norm_g)
    moments_v = dict(attn_norm_g=v_attn_norm_g, w_in=v_w_in, b_gate=v_b_gate, q_norm_g=v_q_norm_g, w_uq=v_w_uq,
                     kv_norm_g=v_kv_norm_g, w_ukv=v_w_ukv, w_o_mla=v_w_o_mla, w_o_dil=v_w_o_dil, w_out=v_w_out,
                     ffn_norm_g=v_ffn_norm_g, w_up=v_w_up, conv_w=v_conv_w, conv_b=v_conv_b, w_down=v_w_down,
                     final_norm_g=v_final_norm_g)

    fetch, zero = _gather_start({n: given[n][0] for n in BIG + ("conv_w",)})
    W = {n: given[n] for n in ("b_gate", "q_norm_g", "kv_norm_g", "ffn_norm_g", "conv_b")}
    W["attn_norm_g"] = given["attn_norm_g"] + zero
    W["final_norm_g"] = given["final_norm_g"].reshape(1, -1)

    pending = {}

    def emit(i, grads):
        pending[i], token = _reduce_start(i, grads)
        return token

    loss_part, grad_x, small = _local_step(x[0], loss_target[0], W, fetch, emit)
    loss = lax.psum(loss_part, ("x", "y", "c"))
    grads, delta, new_m, new_v = {}, {}, {}, {}

    def adamw(n, g):
        shp = given[n].shape
        two_d = (-1, shp[-1]) if len(shp) > 1 else (1, -1)
        view = (lambda a: a.reshape(two_d).T) if n in COLUMN_MAJOR else (lambda a: a.reshape(two_d))
        back = (lambda a: a.T.reshape(shp)) if n in COLUMN_MAJOR else (lambda a: a.reshape(shp))
        go, d, nm, nv = _adamw(f"adamw_{n}", view(given[n]), view(g), view(moments_m[n]), view(moments_v[n]))
        grads[n], delta[n], new_m[n], new_v[n] = back(go), back(d), back(nm), back(nv)

    after = grad_x
    for i in range(len(REDUCE_GROUPS)):
        for n, g in _reduce_finish(i, pending[i], after).items():
            adamw(n, g)
        after = delta[REDUCE_GROUPS[i][-1]]
    g_small = _reduce_small(small)
    chip = 2 * lax.axis_index("x") + lax.axis_index("y")
    for n in SMALL:
        if n == "conv_w":
            full = g_small[n].reshape(N_CHIPS, 3, -1)
            adamw(n, lax.dynamic_index_in_dim(full, chip, 0, keepdims=True))
        else:
            adamw(n, g_small[n])

    return (loss, grad_x[None], *[grads[n] for n in WEIGHTS], *[delta[n] for n in WEIGHTS],
            *[new_m[n] for n in WEIGHTS], *[new_v[n] for n in WEIGHTS])
```

```python
import functools
import math

import numpy as np
import jax
import jax.numpy as jnp
from jax import lax
from jax.experimental import pallas as pl
from jax.experimental.pallas import tpu as pltpu

F32 = jnp.float32
BF = jnp.bfloat16
MESH = pl.DeviceIdType.MESH

D_MODEL = 2048
MLA_HEADS = 8
QK_NOPE = 128
QK_ROPE = 64
Q_RANK = 512
KV_RANK = 256
ROPE_THETA = 10000.0
DIL_PATTERNS = ((128, 1), (512, 4), (2048, 16))
DIL_GROUPS = 3
DIL_HPG = 4
DIL_HEADS = 12
HEAD = 128
DIL_BLOCK = 128
ALIBI_MAX_BIAS = 8.0
NORM_EPS = 1e-6
N_CHIPS = 4
ADAM_LR = 0.001
ADAM_B1 = 0.9
ADAM_B2 = 0.999
ADAM_EPS = 1e-08
ADAM_WD = 0.01
ADAM_STEP = 10

LANE = 128
VMEM_LIMIT = 56 * 1024 * 1024
MLA_SCALE = (QK_NOPE + QK_ROPE) ** -0.5
DIL_SCALE = HEAD ** -0.5


def _params(*sem):
    return pltpu.CompilerParams(dimension_semantics=sem, vmem_limit_bytes=VMEM_LIMIT)


def _tile(n, pref):
    t = (pref // LANE) * LANE
    while t >= LANE:
        if n % t == 0:
            return t
        t -= LANE
    return n


NN = (((1,), (0,)), ((), ()))
NT = (((1,), (1,)), ((), ()))
TN = (((0,), (0,)), ((), ()))


def _mm_call(name, a, b, add, *, grid, a_spec, b_spec, add_spec, o_spec, o_shape, o_dtype, acc_shape, dims, nk):
    nax = len(grid)

    def body(*refs):
        if add is None:
            a_ref, b_ref, o_ref = refs[:3]
            c_ref = None
            scr = refs[3:]
        else:
            a_ref, b_ref, c_ref, o_ref = refs[:4]
            scr = refs[4:]
        prod = lax.dot_general(a_ref[...].astype(BF), b_ref[...].astype(BF), dims, preferred_element_type=F32)
        if nk == 1:
            if c_ref is not None:
                prod = prod + c_ref[...]
            o_ref[...] = prod.astype(o_ref.dtype)
        else:
            acc = scr[0]
            k = pl.program_id(nax - 1)

            @pl.when(k == 0)
            def _():
                if c_ref is not None:
                    acc[...] = prod + c_ref[...]
                else:
                    acc[...] = prod

            @pl.when(k > 0)
            def _():
                acc[...] += prod

            @pl.when(k == nk - 1)
            def _():
                o_ref[...] = acc[...].astype(o_ref.dtype)

    ins = [a, b] + ([] if add is None else [add])
    specs = [a_spec, b_spec] + ([] if add is None else [add_spec])
    sem = ("parallel",) * (nax - 1) + ("arbitrary",)
    return pl.pallas_call(
        body, name=name, grid=grid, in_specs=specs, out_specs=o_spec,
        out_shape=jax.ShapeDtypeStruct(o_shape, o_dtype),
        scratch_shapes=[] if nk == 1 else [pltpu.VMEM(acc_shape, F32)],
        compiler_params=_params(*sem),
    )(*ins)


def _mm_nn(name, a, b, *, add=None, o_dtype=F32):
    M, K = a.shape
    sharded = b.ndim == 3
    Ns = b.shape[-1]
    N = Ns * (b.shape[0] if sharded else 1)
    tm, tn, tk = _tile(M, 1024), _tile(Ns, 1024), _tile(K, 2048)
    per = Ns // tn
    nk = K // tk
    if sharded:
        b_spec = pl.BlockSpec((None, tk, tn), lambda i, j, k: (j // per, k, j % per))
    else:
        b_spec = pl.BlockSpec((tk, tn), lambda i, j, k: (k, j))
    return _mm_call(
        name, a, b, add, grid=(M // tm, N // tn, nk),
        a_spec=pl.BlockSpec((tm, tk), lambda i, j, k: (i, k)), b_spec=b_spec,
        add_spec=pl.BlockSpec((tm, tn), lambda i, j, k: (i, j)),
        o_spec=pl.BlockSpec((tm, tn), lambda i, j, k: (i, j)),
        o_shape=(M, N), o_dtype=o_dtype, acc_shape=(tm, tn), dims=NN, nk=nk)


def _mm_nt(name, a, b, *, add=None, o_dtype=F32):
    M, K = a.shape
    sharded = b.ndim == 3
    N, Ks = b.shape[-2], b.shape[-1]
    tm, tn, tk = _tile(M, 1024), _tile(N, 1024), _tile(Ks, 2048)
    per = Ks // tk
    nk = K // tk
    if sharded:
        b_spec = pl.BlockSpec((None, tn, tk), lambda i, j, k: (k // per, j, k % per))
    else:
        b_spec = pl.BlockSpec((tn, tk), lambda i, j, k: (j, k))
    return _mm_call(
        name, a, b, add, grid=(M // tm, N // tn, nk),
        a_spec=pl.BlockSpec((tm, tk), lambda i, j, k: (i, k)), b_spec=b_spec,
        add_spec=pl.BlockSpec((tm, tn), lambda i, j, k: (i, j)),
        o_spec=pl.BlockSpec((tm, tn), lambda i, j, k: (i, j)),
        o_shape=(M, N), o_dtype=o_dtype, acc_shape=(tm, tn), dims=NT, nk=nk)


def _mm_tn(name, a, b, *, shards=1, o_dtype=BF):
    S, M = a.shape
    N = b.shape[1]
    Ns = N // shards
    tm, tn, tk = _tile(M, 1024), _tile(Ns, 1024), _tile(S, 2048)
    per = Ns // tn
    nk = S // tk
    if shards > 1:
        o_spec = pl.BlockSpec((None, tm, tn), lambda i, j, k: (j // per, i, j % per))
        o_shape = (shards, M, Ns)
    else:
        o_spec = pl.BlockSpec((tm, tn), lambda i, j, k: (i, j))
        o_shape = (M, N)
    return _mm_call(
        name, a, b, None, grid=(M // tm, N // tn, nk),
        a_spec=pl.BlockSpec((tk, tm), lambda i, j, k: (k, i)),
        b_spec=pl.BlockSpec((tk, tn), lambda i, j, k: (k, j)),
        add_spec=None, o_spec=o_spec, o_shape=o_shape, o_dtype=o_dtype, acc_shape=(tm, tn), dims=TN, nk=nk)


def _up_fwd(h2, w_up):
    S, D = h2.shape
    G, _, C = w_up.shape
    tm = _tile(S, 512)
    return _mm_call(
        "up_fwd", h2, w_up, None, grid=(G, S // tm, 1),
        a_spec=pl.BlockSpec((tm, D), lambda g, i, k: (i, 0)),
        b_spec=pl.BlockSpec((None, D, C), lambda g, i, k: (g, 0, 0)),
        add_spec=None, o_spec=pl.BlockSpec((None, tm, C), lambda g, i, k: (g, i, 0)),
        o_shape=(G, S, C), o_dtype=BF, acc_shape=None, dims=NN, nk=1)


def _up_dgrad(du_pre, w_up):
    G, S, C = du_pre.shape
    D = w_up.shape[1]
    tm, tn = _tile(S, 1024), _tile(D, 1024)
    return _mm_call(
        "up_dgrad", du_pre, w_up, None, grid=(S // tm, D // tn, G),
        a_spec=pl.BlockSpec((None, tm, C), lambda i, j, g: (g, i, 0)),
        b_spec=pl.BlockSpec((None, tn, C), lambda i, j, g: (g, j, 0)),
        add_spec=None, o_spec=pl.BlockSpec((tm, tn), lambda i, j, g: (i, j)),
        o_shape=(S, D), o_dtype=F32, acc_shape=(tm, tn), dims=NT, nk=G)


def _up_wgrad(h2, du_pre):
    G, S, C = du_pre.shape
    D = h2.shape[1]
    tm, tk = _tile(D, 512), _tile(S, 2048)
    return _mm_call(
        "up_wgrad", h2, du_pre, None, grid=(G, D // tm, S // tk),
        a_spec=pl.BlockSpec((tk, tm), lambda g, i, k: (k, i)),
        b_spec=pl.BlockSpec((None, tk, C), lambda g, i, k: (g, k, 0)),
        add_spec=None, o_spec=pl.BlockSpec((None, tm, C), lambda g, i, k: (g, i, 0)),
        o_shape=(G, D, C), o_dtype=BF, acc_shape=(tm, C), dims=TN, nk=S // tk)


def _down_fwd(act, w_down, x1):
    G, S, C = act.shape
    D = w_down.shape[2]
    tm, tn = _tile(S, 1024), _tile(D, 1024)
    return _mm_call(
        "down_fwd", act, w_down, x1, grid=(S // tm, D // tn, G),
        a_spec=pl.BlockSpec((None, tm, C), lambda i, j, g: (g, i, 0)),
        b_spec=pl.BlockSpec((None, C, tn), lambda i, j, g: (g, 0, j)),
        add_spec=pl.BlockSpec((tm, tn), lambda i, j, g: (i, j)),
        o_spec=pl.BlockSpec((tm, tn), lambda i, j, g: (i, j)),
        o_shape=(S, D), o_dtype=F32, acc_shape=(tm, tn), dims=NN, nk=G)


def _down_dgrad(dx2, w_down):
    S, D = dx2.shape
    G, C, _ = w_down.shape
    tm = _tile(S, 512)
    return _mm_call(
        "down_dgrad", dx2, w_down, None, grid=(G, S // tm, 1),
        a_spec=pl.BlockSpec((tm, D), lambda g, i, k: (i, 0)),
        b_spec=pl.BlockSpec((None, C, D), lambda g, i, k: (g, 0, 0)),
        add_spec=None, o_spec=pl.BlockSpec((None, tm, C), lambda g, i, k: (g, i, 0)),
        o_shape=(G, S, C), o_dtype=BF, acc_shape=None, dims=NT, nk=1)


def _down_wgrad(act, dx2):
    G, S, C = act.shape
    D = dx2.shape[1]
    tn, tk = _tile(D, 512), _tile(S, 1024)
    return _mm_call(
        "down_wgrad", act, dx2, None, grid=(G, D // tn, S // tk),
        a_spec=pl.BlockSpec((None, tk, C), lambda g, j, k: (g, k, 0)),
        b_spec=pl.BlockSpec((tk, tn), lambda g, j, k: (k, j)),
        add_spec=None, o_spec=pl.BlockSpec((None, C, tn), lambda g, j, k: (g, 0, j)),
        o_shape=(G, C, D), o_dtype=BF, acc_shape=(C, tn), dims=TN, nk=S // tk)


def _row(ts, c):
    return pl.BlockSpec((ts, c), lambda i: (i, 0))


def _bcast(r, c):
    return pl.BlockSpec((r, c), lambda i: (0, 0))


def _accumulate(i, ref, val):
    @pl.when(i == 0)
    def _():
        ref[...] = val

    @pl.when(i > 0)
    def _():
        ref[...] += val


def _rstd(xv):
    return lax.rsqrt(jnp.mean(xv * xv, axis=-1, keepdims=True) + NORM_EPS)


def _rmsnorm_fwd(name, x, g):
    S, D = x.shape
    ts = _tile(S, 512)

    def body(x_ref, g_ref, o_ref):
        xv = x_ref[...]
        o_ref[...] = (xv * _rstd(xv) * g_ref[...]).astype(o_ref.dtype)

    return pl.pallas_call(
        body, name=name, grid=(S // ts,), in_specs=[_row(ts, D), _bcast(1, D)], out_specs=_row(ts, D),
        out_shape=jax.ShapeDtypeStruct((S, D), BF), compiler_params=_params("parallel"))(x, g)


def _norm_bwd_rows(dy, xv, g):
    r = _rstd(xv)
    xh = xv * r
    dxh = dy * g
    dx = r * (dxh - xh * jnp.mean(dxh * xh, axis=-1, keepdims=True))
    return dx, jnp.sum(dy * xh, axis=0, keepdims=True)


def _rmsnorm_bwd(name, dy, x, g, res):
    S, D = x.shape
    ts = _tile(S, 512)

    def body(dy_ref, x_ref, g_ref, res_ref, dx_ref, dg_ref):
        dx, dg = _norm_bwd_rows(dy_ref[...], x_ref[...], g_ref[...])
        dx_ref[...] = dx + res_ref[...]
        _accumulate(pl.program_id(0), dg_ref, dg)

    return pl.pallas_call(
        body, name=name, grid=(S // ts,),
        in_specs=[_row(ts, D), _row(ts, D), _bcast(1, D), _row(ts, D)],
        out_specs=[_row(ts, D), _bcast(1, D)],
        out_shape=[jax.ShapeDtypeStruct((S, D), F32), jax.ShapeDtypeStruct((1, D), F32)],
        compiler_params=_params("arbitrary"))(dy, x, g, res)


def _rope_tables(S):
    half = QK_ROPE // 2
    pos = jnp.arange(S, dtype=F32)
    inv_freq = ROPE_THETA ** (-jnp.arange(0, QK_ROPE, 2, dtype=F32) / QK_ROPE)
    ang = pos[:, None] * inv_freq[None, :]
    cos, sin = jnp.cos(ang), jnp.sin(ang)
    z = jnp.zeros((S, half), F32)
    return jnp.concatenate([cos, z, cos, z], axis=1), jnp.concatenate([-sin, z, sin, z], axis=1)


def _rope_lanes(x, cos, sin_signed, inverse):
    if inverse:
        return x * cos + pltpu.roll(x * sin_signed, LANE // 2, 1)
    return x * cos + pltpu.roll(x, LANE // 2, 1) * sin_signed


def _rope(name, x, cos, sin_signed, inverse):
    S, W = x.shape
    ts = _tile(S, 512)

    def body(x_ref, c_ref, s_ref, o_ref):
        c, s = c_ref[...], s_ref[...]
        for h in range(W // LANE):
            sl = slice(h * LANE, (h + 1) * LANE)
            o_ref[:, sl] = _rope_lanes(x_ref[:, sl], c, s, inverse).astype(o_ref.dtype)

    return pl.pallas_call(
        body, name=name, grid=(S // ts,), in_specs=[_row(ts, W), _row(ts, LANE), _row(ts, LANE)],
        out_specs=_row(ts, W), out_shape=jax.ShapeDtypeStruct((S, W), BF),
        compiler_params=_params("parallel"))(x, cos, sin_signed)


LAT_W = 1024
_CQ = slice(0, Q_RANK)
_CKV = slice(Q_RANK, Q_RANK + KV_RANK)
_KPE = slice(Q_RANK + KV_RANK, Q_RANK + KV_RANK + LANE)


def _mla_prep(lat, qg, kvg, cos, sin_signed):
    S = lat.shape[0]
    ts = _tile(S, 512)

    def body(lat_ref, qg_ref, kvg_ref, c_ref, s_ref, qn_ref, kvn_ref, kpe_ref):
        cq = lat_ref[:, _CQ]
        qn_ref[...] = (cq * _rstd(cq) * qg_ref[...]).astype(BF)
        ckv = lat_ref[:, _CKV]
        kvn_ref[...] = (ckv * _rstd(ckv) * kvg_ref[...]).astype(BF)
        kpe_ref[...] = _rope_lanes(lat_ref[:, _KPE], c_ref[...], s_ref[...], False).astype(BF)

    return pl.pallas_call(
        body, name="mla_prep", grid=(S // ts,),
        in_specs=[_row(ts, LAT_W), _bcast(1, Q_RANK), _bcast(1, KV_RANK), _row(ts, LANE), _row(ts, LANE)],
        out_specs=[_row(ts, Q_RANK), _row(ts, KV_RANK), _row(ts, LANE)],
        out_shape=[jax.ShapeDtypeStruct((S, Q_RANK), BF), jax.ShapeDtypeStruct((S, KV_RANK), BF),
                   jax.ShapeDtypeStruct((S, LANE), BF)],
        compiler_params=_params("parallel"))(lat, qg, kvg, cos, sin_signed)


def _mla_prep_bwd(lat, qg, kvg, cos, sin_signed, d_qn, d_kvn, d_kpe):
    S = lat.shape[0]
    ts = _tile(S, 512)

    def body(lat_ref, qg_ref, kvg_ref, c_ref, s_ref, dqn_ref, dkvn_ref, dkpe_ref, dlat_ref, dqg_ref, dkvg_ref):
        i = pl.program_id(0)
        dcq, dqg = _norm_bwd_rows(dqn_ref[...], lat_ref[:, _CQ], qg_ref[...])
        dckv, dkvg = _norm_bwd_rows(dkvn_ref[...], lat_ref[:, _CKV], kvg_ref[...])
        dlat_ref[:, _CQ] = dcq.astype(BF)
        dlat_ref[:, _CKV] = dckv.astype(BF)
        dkpe = dkpe_ref[0]
        for g in range(1, d_kpe.shape[0]):
            dkpe = dkpe + dkpe_ref[g]
        dlat_ref[:, _KPE] = _rope_lanes(dkpe, c_ref[...], s_ref[...], True).astype(BF)
        dlat_ref[:, _KPE.stop:] = jnp.zeros((ts, LAT_W - _KPE.stop), BF)
        _accumulate(i, dqg_ref, dqg)
        _accumulate(i, dkvg_ref, dkvg)

    return pl.pallas_call(
        body, name="mla_prep_bwd", grid=(S // ts,),
        in_specs=[_row(ts, LAT_W), _bcast(1, Q_RANK), _bcast(1, KV_RANK), _row(ts, LANE), _row(ts, LANE),
                  _row(ts, Q_RANK), _row(ts, KV_RANK), pl.BlockSpec((d_kpe.shape[0], ts, LANE), lambda i: (0, i, 0))],
        out_specs=[_row(ts, LAT_W), _bcast(1, Q_RANK), _bcast(1, KV_RANK)],
        out_shape=[jax.ShapeDtypeStruct((S, LAT_W), BF), jax.ShapeDtypeStruct((1, Q_RANK), F32),
                   jax.ShapeDtypeStruct((1, KV_RANK), F32)],
        compiler_params=_params("arbitrary"))(lat, qg, kvg, cos, sin_signed, d_qn, d_kvn, d_kpe)


def _sigmoid(z):
    return 1.0 / (1.0 + jnp.exp(-z))


def _merge_fwd(gpre, b_gate, o_a, o_b):
    S, D = o_a.shape
    ts = _tile(S, 256)

    def body(g_ref, b_ref, oa_ref, ob_ref, m_ref):
        ga = _sigmoid(g_ref[:, :D] + b_ref[:, :D])
        gb = _sigmoid(g_ref[:, D:] + b_ref[:, D:])
        m_ref[...] = (ga * oa_ref[...] + gb * ob_ref[...]).astype(BF)

    return pl.pallas_call(
        body, name="merge_fwd", grid=(S // ts,),
        in_specs=[_row(ts, 2 * D), _bcast(1, 2 * D), _row(ts, D), _row(ts, D)], out_specs=_row(ts, D),
        out_shape=jax.ShapeDtypeStruct((S, D), BF), compiler_params=_params("parallel"))(gpre, b_gate, o_a, o_b)


def _merge_bwd(d_merge, gpre, b_gate, o_a, o_b):
    S, D = o_a.shape
    ts = _tile(S, 256)

    def body(dm_ref, g_ref, b_ref, oa_ref, ob_ref, doa_ref, dob_ref, dg_ref, db_ref):
        dm = dm_ref[...]
        ga = _sigmoid(g_ref[:, :D] + b_ref[:, :D])
        gb = _sigmoid(g_ref[:, D:] + b_ref[:, D:])
        doa_ref[...] = (dm * ga).astype(BF)
        dob_ref[...] = (dm * gb).astype(BF)
        dga = dm * oa_ref[...] * ga * (1.0 - ga)
        dgb = dm * ob_ref[...] * gb * (1.0 - gb)
        dg_ref[:, :D] = dga.astype(BF)
        dg_ref[:, D:] = dgb.astype(BF)
        i = pl.program_id(0)
        part = jnp.concatenate([jnp.sum(dga, axis=0, keepdims=True), jnp.sum(dgb, axis=0, keepdims=True)], axis=1)
        _accumulate(i, db_ref, part)

    return pl.pallas_call(
        body, name="merge_bwd", grid=(S // ts,),
        in_specs=[_row(ts, D), _row(ts, 2 * D), _bcast(1, 2 * D), _row(ts, D), _row(ts, D)],
        out_specs=[_row(ts, D), _row(ts, D), _row(ts, 2 * D), _bcast(1, 2 * D)],
        out_shape=[jax.ShapeDtypeStruct((S, D), BF), jax.ShapeDtypeStruct((S, D), BF),
                   jax.ShapeDtypeStruct((S, 2 * D), BF), jax.ShapeDtypeStruct((1, 2 * D), F32)],
        compiler_params=_params("arbitrary"))(d_merge, gpre, b_gate, o_a, o_b)


def _final_loss(x2, tgt, gf):
    S, D = x2.shape
    ts = _tile(S, 512)

    def body(x_ref, t_ref, g_ref, dx_ref, dg_ref, loss_ref):
        i = pl.program_id(0)
        xv = x_ref[...]
        g = g_ref[...]
        y = xv * _rstd(xv) * g
        err = y - t_ref[...]
        dx, dg = _norm_bwd_rows(err * (1.0 / D), xv, g)
        dx_ref[...] = dx
        _accumulate(i, dg_ref, dg)
        part = 0.5 * jnp.sum(jnp.mean(err * err, axis=-1, keepdims=True), axis=0, keepdims=True)
        _accumulate(i, loss_ref, jnp.broadcast_to(part, (8, LANE)))

    return pl.pallas_call(
        body, name="final_loss", grid=(S // ts,),
        in_specs=[_row(ts, D), _row(ts, D), _bcast(1, D)],
        out_specs=[_row(ts, D), _bcast(1, D), _bcast(8, LANE)],
        out_shape=[jax.ShapeDtypeStruct((S, D), F32), jax.ShapeDtypeStruct((1, D), F32),
                   jax.ShapeDtypeStruct((8, LANE), F32)],
        compiler_params=_params("arbitrary"))(x2, tgt, gf)


HALO = 16


SUB = 8


def _shift_down(cur, prev, k, rows):
    out = pltpu.roll(cur, k, 0)
    head = out[:SUB]
    for j in range(k):
        head = jnp.where(rows == j, prev[HALO - k + j:HALO - k + j + 1, :], head)
    return jnp.concatenate([head, out[SUB:]], axis=0)


def _shift_up(cur, nxt, k, rows, ts):
    out = pltpu.roll(cur, ts - k, 0)
    tail = out[ts - SUB:]
    for j in range(k):
        tail = jnp.where(rows == SUB - k + j, nxt[j:j + 1, :], tail)
    return jnp.concatenate([out[:ts - SUB], tail], axis=0)


def _conv_rows(cur, prev, w, b, rows):
    return b + w[0:1, :] * _shift_down(cur, prev, 2, rows) + w[1:2, :] * _shift_down(cur, prev, 1, rows) + w[2:3, :] * cur


def _conv_specs(ts, C, shard_of):
    nh = ts // HALO
    cur = pl.BlockSpec((None, ts, C), lambda g, i: (shard_of(g), i, 0))
    prev = pl.BlockSpec((None, HALO, C), lambda g, i: (shard_of(g), jnp.maximum(i * nh - 1, 0), 0))
    return cur, prev


def _ffn_act(u_pre, conv_w, conv_b):
    G4, S, C = u_pre.shape
    G = G4 // 2
    ts = _tile(S, 256)

    def body(up_ref, upp_ref, gt_ref, gtp_ref, wu_ref, wg_ref, bu_ref, bg_ref, act_ref, u_ref):
        first = pl.program_id(1) == 0
        rows = lax.broadcasted_iota(jnp.int32, (SUB, C), 0)
        pu = jnp.where(first, 0.0, upp_ref[...].astype(F32))
        pg = jnp.where(first, 0.0, gtp_ref[...].astype(F32))
        up = _conv_rows(up_ref[...].astype(F32), pu, wu_ref[...], bu_ref[...], rows)
        gate = _conv_rows(gt_ref[...].astype(F32), pg, wg_ref[...], bg_ref[...], rows)
        act_ref[...] = (gate * _sigmoid(gate) * up).astype(BF)
        u_ref[0] = up.astype(BF)
        u_ref[1] = gate.astype(BF)

    cur_u, prev_u = _conv_specs(ts, C, lambda g: g)
    cur_g, prev_g = _conv_specs(ts, C, lambda g: g + G)
    w_u = pl.BlockSpec((None, 3, C), lambda g, i: (g, 0, 0))
    w_g = pl.BlockSpec((None, 3, C), lambda g, i: (g + G, 0, 0))
    b_u = pl.BlockSpec((None, 1, C), lambda g, i: (g, 0, 0))
    b_g = pl.BlockSpec((None, 1, C), lambda g, i: (g + G, 0, 0))
    pair = pl.BlockSpec((2, None, ts, C), lambda g, i: (0, g, i, 0))
    act, u = pl.pallas_call(
        body, name="ffn_act", grid=(G, S // ts),
        in_specs=[cur_u, prev_u, cur_g, prev_g, w_u, w_g, b_u, b_g],
        out_specs=[pl.BlockSpec((None, ts, C), lambda g, i: (g, i, 0)), pair],
        out_shape=[jax.ShapeDtypeStruct((G, S, C), BF), jax.ShapeDtypeStruct((2, G, S, C), BF)],
        compiler_params=_params("parallel", "parallel"))(u_pre, u_pre, u_pre, u_pre, conv_w, conv_w, conv_b, conv_b)
    return act, u


def _ffn_act_bwd(u, d_act):
    _, G, S, C = u.shape
    ts = _tile(S, 256)

    def body(u_ref, da_ref, du_ref):
        up, gate = u_ref[0].astype(F32), u_ref[1].astype(F32)
        sg = _sigmoid(gate)
        da = da_ref[...].astype(F32)
        du_ref[0] = (da * (gate * sg)).astype(BF)
        du_ref[1] = (da * up * (sg * (1.0 + gate * (1.0 - sg)))).astype(BF)

    pair = pl.BlockSpec((2, None, ts, C), lambda g, i: (0, g, i, 0))
    du = pl.pallas_call(
        body, name="ffn_act_bwd", grid=(G, S // ts),
        in_specs=[pair, pl.BlockSpec((None, ts, C), lambda g, i: (g, i, 0))], out_specs=pair,
        out_shape=jax.ShapeDtypeStruct((2, G, S, C), BF),
        compiler_params=_params("parallel", "parallel"))(u, d_act)
    return du.reshape(2 * G, S, C)


def _conv_bwd(du, u_pre, conv_w):
    G4, S, C = du.shape
    ts = _tile(S, 256)
    nh = ts // HALO
    last_halo = S // HALO - 1

    def body(du_ref, dun_ref, u_ref, w_ref, dpre_ref, dw_ref, db_ref):
        i = pl.program_id(1)
        rows = lax.broadcasted_iota(jnp.int32, (SUB, C), 0)
        du_c = du_ref[...].astype(F32)
        nxt = jnp.where(i == pl.num_programs(1) - 1, 0.0, dun_ref[...].astype(F32))
        up1, up2 = _shift_up(du_c, nxt, 1, rows, ts), _shift_up(du_c, nxt, 2, rows, ts)
        w = w_ref[...]
        dpre_ref[...] = (w[2:3, :] * du_c + w[1:2, :] * up1 + w[0:1, :] * up2).astype(BF)
        u_c = u_ref[...].astype(F32)
        dw = jnp.concatenate([
            jnp.sum(up2 * u_c, axis=0, keepdims=True),
            jnp.sum(up1 * u_c, axis=0, keepdims=True),
            jnp.sum(du_c * u_c, axis=0, keepdims=True)], axis=0)
        _accumulate(i, dw_ref, dw)
        _accumulate(i, db_ref, jnp.sum(du_c, axis=0, keepdims=True))

    cur = pl.BlockSpec((None, ts, C), lambda g, i: (g, i, 0))
    nxt = pl.BlockSpec((None, HALO, C), lambda g, i: (g, jnp.minimum((i + 1) * nh, last_halo), 0))
    return pl.pallas_call(
        body, name="conv_bwd", grid=(G4, S // ts),
        in_specs=[cur, nxt, cur, pl.BlockSpec((None, 3, C), lambda g, i: (g, 0, 0))],
        out_specs=[cur, pl.BlockSpec((None, 3, C), lambda g, i: (g, 0, 0)), pl.BlockSpec((None, 1, C), lambda g, i: (g, 0, 0))],
        out_shape=[jax.ShapeDtypeStruct((G4, S, C), BF), jax.ShapeDtypeStruct((G4, 3, C), F32),
                   jax.ShapeDtypeStruct((G4, 1, C), F32)],
        compiler_params=_params("parallel", "arbitrary"))(du, du, u_pre, conv_w)


MLA_T = 1024
MLA_HB = 4
MLA_BWD_HB = 2


def _mla_pairs(n, by_row):
    if by_row:
        pairs = [(i, j) for i in range(n) for j in range(i + 1)]
    else:
        pairs = [(i, j) for j in range(n) for i in range(j, n)]
    return jnp.asarray([p[0] for p in pairs], jnp.int32), jnp.asarray([p[1] for p in pairs], jnp.int32)


def _mla_specs(hb):
    q = pl.BlockSpec((MLA_T, hb * HEAD), lambda g, t, it, jt: (it[t], g))
    k = pl.BlockSpec((MLA_T, hb * HEAD), lambda g, t, it, jt: (jt[t], g))
    kpe = pl.BlockSpec((MLA_T, HEAD), lambda g, t, it, jt: (jt[t], 0))
    lse = pl.BlockSpec((hb, MLA_T, LANE), lambda g, t, it, jt: (g, it[t], 0))
    return q, k, kpe, lse


def _mla_head(ref, hh):
    return ref[:, hh * HEAD:(hh + 1) * HEAD]


LOG2E = math.log2(math.e)
MLA_EXP2_SCALE = MLA_SCALE * LOG2E


def _mla_scores(qn_ref, qp_ref, kn_ref, kpe, hh, ok):
    q = jnp.concatenate([_mla_head(qn_ref, hh), _mla_head(qp_ref, hh)], axis=1)
    k = jnp.concatenate([_mla_head(kn_ref, hh), kpe], axis=1)
    s = lax.dot_general(q, k, NT, preferred_element_type=F32)
    return q, k, s if ok is None else jnp.where(ok, s, -jnp.inf)


def _mla_diagonal_mask():
    row = lax.broadcasted_iota(jnp.int32, (MLA_T, MLA_T), 0)
    col = lax.broadcasted_iota(jnp.int32, (MLA_T, MLA_T), 1)
    return col <= row


def _mla_step(i, j, step):
    @pl.when(j < i)
    def _():
        step(None)

    @pl.when(j == i)
    def _():
        step(_mla_diagonal_mask())


def _mla_fwd(qn, qp, kn, kpe, v):
    S = qn.shape[0]
    it, jt = _mla_pairs(S // MLA_T, True)

    def body(it_ref, jt_ref, qn_ref, qp_ref, kn_ref, kpe_ref, v_ref, o_ref, lse_ref, m_scr, acc_scr):
        t = pl.program_id(1)
        i, j = it_ref[t], jt_ref[t]

        @pl.when(j == 0)
        def _():
            m_scr[...] = jnp.full(m_scr.shape, -jnp.inf, F32)
            acc_scr[...] = jnp.zeros(acc_scr.shape, F32)

        def step(ok):
            kpe_v = kpe_ref[...]
            ones = jnp.ones((MLA_T, HEAD), BF)
            state = [(m_scr[hh], acc_scr[hh]) for hh in range(MLA_HB)]
            new = []
            for hh in range(MLA_HB):
                m_prev, acc = state[hh]
                _, _, s = _mla_scores(qn_ref, qp_ref, kn_ref, kpe_v, hh, ok)
                m_new = jnp.maximum(m_prev, jnp.max(s, axis=1, keepdims=True))
                p = jnp.exp2((s - m_new) * MLA_EXP2_SCALE).astype(BF)
                v1 = jnp.concatenate([_mla_head(v_ref, hh), ones], axis=1)
                alpha = jnp.exp2((m_prev - m_new) * MLA_EXP2_SCALE)
                new.append((m_new, alpha * acc + lax.dot_general(p, v1, NN, preferred_element_type=F32)))
            for hh in range(MLA_HB):
                m_scr[hh], acc_scr[hh] = new[hh]

        _mla_step(i, j, step)

        @pl.when(j == i)
        def _():
            for hh in range(MLA_HB):
                l = acc_scr[hh, :, HEAD:]
                o_ref[:, hh * HEAD:(hh + 1) * HEAD] = (acc_scr[hh, :, :HEAD] / l).astype(BF)
                lse_ref[hh] = m_scr[hh] * MLA_SCALE + jnp.log(l)

    qspec, kspec, kpespec, lsespec = _mla_specs(MLA_HB)
    grid_spec = pltpu.PrefetchScalarGridSpec(
        num_scalar_prefetch=2, grid=(MLA_HEADS // MLA_HB, it.shape[0]),
        in_specs=[qspec, qspec, kspec, kpespec, kspec], out_specs=[qspec, lsespec],
        scratch_shapes=[pltpu.VMEM((MLA_HB, MLA_T, 1), F32), pltpu.VMEM((MLA_HB, MLA_T, 2 * HEAD), F32)])
    return pl.pallas_call(
        body, name="mla_fwd", grid_spec=grid_spec,
        out_shape=[jax.ShapeDtypeStruct((S, MLA_HEADS * HEAD), BF), jax.ShapeDtypeStruct((MLA_HEADS, S, LANE), F32)],
        compiler_params=_params("parallel", "arbitrary"))(it, jt, qn, qp, kn, kpe, v)


def _mla_p_ds(qn_ref, qp_ref, kn_ref, kpe, v_ref, do_ref, o_ref, lse_ref, hh, ok):
    q, k, s = _mla_scores(qn_ref, qp_ref, kn_ref, kpe, hh, ok)
    p = jnp.exp2(s * MLA_EXP2_SCALE - lse_ref[hh][:, 0:1] * LOG2E)
    do = _mla_head(do_ref, hh)
    delta = jnp.sum(do.astype(F32) * _mla_head(o_ref, hh).astype(F32), axis=1, keepdims=True)
    dp = lax.dot_general(do, _mla_head(v_ref, hh), NT, preferred_element_type=F32)
    ds = p * (dp - delta) * MLA_SCALE
    return q, k, p, ds, do


def _mla_bwd(qn, qp, kn, kpe, v, do, o, lse):
    S = qn.shape[0]
    nq = S // MLA_T
    hb = MLA_BWD_HB
    it, jt = _mla_pairs(nq, False)

    def body(it_ref, jt_ref, qn_ref, qp_ref, kn_ref, kpe_ref, v_ref, do_ref, o_ref, lse_ref,
             dqn_ref, dqp_ref, dkn_ref, dv_ref, dkpe_ref, dq_acc, dk_acc, dv_acc, stage_n, stage_p, osem):
        t = pl.program_id(1)
        i, j = it_ref[t], jt_ref[t]

        @pl.when(t == 0)
        def _():
            dq_acc[...] = jnp.zeros(dq_acc.shape, F32)

        @pl.when(i == j)
        def _():
            dk_acc[...] = jnp.zeros(dk_acc.shape, F32)
            dv_acc[...] = jnp.zeros(dv_acc.shape, F32)

        def step(ok):
            kpe_v = kpe_ref[...]
            for hh in range(hb):
                q, k, p, ds, do_h = _mla_p_ds(qn_ref, qp_ref, kn_ref, kpe_v, v_ref, do_ref, o_ref, lse_ref, hh, ok)
                ds = ds.astype(BF)
                dv_acc[hh] += lax.dot_general(p.astype(BF), do_h, TN, preferred_element_type=F32)
                dk_acc[hh] += lax.dot_general(ds, q, TN, preferred_element_type=F32)
                dq_acc[i, hh] += lax.dot_general(ds, k, NN, preferred_element_type=F32)

        _mla_step(i, j, step)

        @pl.when(i == j)
        def _():
            for hh in range(hb):
                stage_n[:, hh * HEAD:(hh + 1) * HEAD] = dq_acc[i, hh, :, :HEAD].astype(BF)
                stage_p[:, hh * HEAD:(hh + 1) * HEAD] = dq_acc[i, hh, :, HEAD:]
            rows = pl.ds(pl.multiple_of(i * MLA_T, MLA_T), MLA_T)
            cols = pl.ds(pl.multiple_of(pl.program_id(0) * hb * HEAD, LANE), hb * HEAD)
            out_n = pltpu.make_async_copy(stage_n, dqn_ref.at[rows, cols], osem.at[0])
            out_p = pltpu.make_async_copy(stage_p, dqp_ref.at[rows, cols], osem.at[1])
            out_n.start()
            out_p.start()
            out_n.wait()
            out_p.wait()

        @pl.when(i == nq - 1)
        def _():
            dkpe = dk_acc[0, :, HEAD:]
            for hh in range(hb):
                dkn_ref[:, hh * HEAD:(hh + 1) * HEAD] = dk_acc[hh, :, :HEAD].astype(BF)
                dv_ref[:, hh * HEAD:(hh + 1) * HEAD] = dv_acc[hh].astype(BF)
                if hh:
                    dkpe = dkpe + dk_acc[hh, :, HEAD:]
            dkpe_ref[...] = dkpe

    qspec, kspec, kpespec, lsespec = _mla_specs(hb)
    dkpespec = pl.BlockSpec((None, MLA_T, HEAD), lambda g, t, it, jt: (g, jt[t], 0))
    grid_spec = pltpu.PrefetchScalarGridSpec(
        num_scalar_prefetch=2, grid=(MLA_HEADS // hb, it.shape[0]),
        in_specs=[qspec, qspec, kspec, kpespec, kspec, qspec, qspec, lsespec],
        out_specs=[ANY, ANY, kspec, kspec, dkpespec],
        scratch_shapes=[pltpu.VMEM((nq, hb, MLA_T, 2 * HEAD), F32), pltpu.VMEM((hb, MLA_T, 2 * HEAD), F32),
                        pltpu.VMEM((hb, MLA_T, HEAD), F32), pltpu.VMEM((MLA_T, hb * HEAD), BF),
                        pltpu.VMEM((MLA_T, hb * HEAD), F32), pltpu.SemaphoreType.DMA((2,))])
    wide = jax.ShapeDtypeStruct((S, MLA_HEADS * HEAD), BF)
    return pl.pallas_call(
        body, name="mla_bwd", grid_spec=grid_spec,
        out_shape=[wide, jax.ShapeDtypeStruct((S, MLA_HEADS * HEAD), F32), wide, wide,
                   jax.ShapeDtypeStruct((MLA_HEADS // hb, S, HEAD), F32)],
        compiler_params=_params("parallel", "arbitrary"))(it, jt, qn, qp, kn, kpe, v, do, o, lse)


DIL_W = 3 * DIL_HPG * HEAD
DIL_O = DIL_HPG * HEAD
DIL_STEP_BLOCKS = 4


def _dil_slopes(g):
    return [2.0 ** (-ALIBI_MAX_BIAS * (g * DIL_HPG + hh + 1) / DIL_HEADS) for hh in range(DIL_HPG)]


def _dil_bias(dil):
    p = lax.broadcasted_iota(jnp.int32, (DIL_BLOCK, DIL_BLOCK), 0)
    kk = lax.broadcasted_iota(jnp.int32, (DIL_BLOCK, DIL_BLOCK), 1)
    jc = p - kk
    dist_c = (dil * jc).astype(F32)
    dist_p = (dil * (jc + DIL_BLOCK)).astype(F32)
    return jc >= 0, jc <= 0, dist_c, dist_p


def _dil_bias2(dil):
    p = lax.broadcasted_iota(jnp.int32, (DIL_BLOCK, 2 * DIL_BLOCK), 0)
    kk = lax.broadcasted_iota(jnp.int32, (DIL_BLOCK, 2 * DIL_BLOCK), 1)
    j = p + DIL_BLOCK - kk
    return (j >= 0) & (j <= DIL_BLOCK), kk < DIL_BLOCK, (dil * j).astype(F32)


def _dil_head(blk, hh):
    q = blk[:, hh * HEAD:(hh + 1) * HEAD]
    k = blk[:, DIL_O + hh * HEAD:DIL_O + (hh + 1) * HEAD]
    v = blk[:, 2 * DIL_O + hh * HEAD:2 * DIL_O + (hh + 1) * HEAD]
    return q, k, v


def _dil_s(q, k, slope, dist, ok):
    s = lax.dot_general(q, k, NT, preferred_element_type=F32) * DIL_SCALE - slope * dist
    return jnp.where(ok, s, -jnp.inf)


def _dil_view(a, dil):
    S, W = a.shape
    return a.reshape(S // dil, dil * W)


def _dil_fwd(qkv, g):
    _, dil = DIL_PATTERNS[g]
    S = qkv.shape[0]
    L = S // dil
    nb = L // DIL_BLOCK
    slopes = _dil_slopes(g)

    bb = min(DIL_STEP_BLOCKS, nb)
    rows = bb * DIL_BLOCK

    def body(cur_ref, prev_ref, o_ref, lse_ref):
        n = pl.program_id(1)
        ok, in_prev, dist = _dil_bias2(dil)
        for b in range(bb):
            if b == 0:
                both = jnp.concatenate([prev_ref[...], cur_ref[0:DIL_BLOCK, :]], axis=0)
                ok_b = ok & (~in_prev | (n > 0))
            else:
                both = cur_ref[(b - 1) * DIL_BLOCK:(b + 1) * DIL_BLOCK, :]
                ok_b = ok
            for hh in range(DIL_HPG):
                q, _, _ = _dil_head(both[DIL_BLOCK:], hh)
                _, k2, v2 = _dil_head(both, hh)
                s = _dil_s(q, k2, slopes[hh], dist, ok_b)
                m = jnp.max(s, axis=1, keepdims=True)
                p = jnp.exp(s - m)
                l = jnp.sum(p, axis=1, keepdims=True)
                o = lax.dot_general(p.astype(BF), v2, NN, preferred_element_type=F32) / l
                rs, sl = slice(b * DIL_BLOCK, (b + 1) * DIL_BLOCK), slice(hh * HEAD, (hh + 1) * HEAD)
                o_ref[rs, sl] = o
                lse_ref[rs, sl] = jnp.broadcast_to(m + jnp.log(l), (DIL_BLOCK, HEAD))

    ospec = pl.BlockSpec((rows, DIL_O), lambda r, n: (n, r))
    o, lse = pl.pallas_call(
        body, name=f"dil_fwd{g}", grid=(dil, nb // bb),
        in_specs=[pl.BlockSpec((rows, DIL_W), lambda r, n: (n, r)),
                  pl.BlockSpec((DIL_BLOCK, DIL_W), lambda r, n: (jnp.maximum(n * bb - 1, 0), r))],
        out_specs=[ospec, ospec],
        out_shape=[jax.ShapeDtypeStruct((L, dil * DIL_O), F32), jax.ShapeDtypeStruct((L, dil * DIL_O), F32)],
        compiler_params=_params("parallel", "parallel"))(_dil_view(qkv, dil), _dil_view(qkv, dil))
    return o.reshape(S, DIL_O), lse.reshape(S, DIL_O)


def _dil_combine(os_, lses):
    S = os_[0].shape[0]
    ts = _tile(S, 512)

    def body(o0, o1, o2, l0, l1, l2, out_ref, lse_ref):
        a, b, c = l0[...], l1[...], l2[...]
        m = jnp.maximum(jnp.maximum(a, b), c)
        ea, eb, ec = jnp.exp(a - m), jnp.exp(b - m), jnp.exp(c - m)
        tot = ea + eb + ec
        out_ref[...] = ((ea * o0[...] + eb * o1[...] + ec * o2[...]) / tot).astype(BF)
        lse_ref[...] = m + jnp.log(tot)

    return pl.pallas_call(
        body, name="dil_combine", grid=(S // ts,), in_specs=[_row(ts, DIL_O)] * 6,
        out_specs=[_row(ts, DIL_O), _row(ts, DIL_O)],
        out_shape=[jax.ShapeDtypeStruct((S, DIL_O), BF), jax.ShapeDtypeStruct((S, DIL_O), F32)],
        compiler_params=_params("parallel"))(*os_, *lses)


def _dil_delta(do, out):
    S = do.shape[0]
    ts = _tile(S, 512)

    def body(do_ref, o_ref, d_ref):
        for hh in range(DIL_HPG):
            sl = slice(hh * HEAD, (hh + 1) * HEAD)
            d = jnp.sum(do_ref[:, sl].astype(F32) * o_ref[:, sl].astype(F32), axis=1, keepdims=True)
            d_ref[:, sl] = jnp.broadcast_to(d, (ts, HEAD))

    return pl.pallas_call(
        body, name="dil_delta", grid=(S // ts,), in_specs=[_row(ts, DIL_O)] * 2, out_specs=_row(ts, DIL_O),
        out_shape=jax.ShapeDtypeStruct((S, DIL_O), F32), compiler_params=_params("parallel"))(do, out)


def _dil_bwd(qkv, do, lse, delta, g):
    _, dil = DIL_PATTERNS[g]
    S = qkv.shape[0]
    L = S // dil
    nb = L // DIL_BLOCK
    slopes = _dil_slopes(g)

    def pair(q, k, v, do_h, lse_h, delta_h, slope, dist, ok):
        s = _dil_s(q, k, slope, dist, ok)
        p = jnp.exp(s - lse_h)
        dp = lax.dot_general(do_h, v, NT, preferred_element_type=F32)
        ds = (p * (dp - delta_h) * DIL_SCALE).astype(BF)
        return p.astype(BF), ds

    bb = min(DIL_STEP_BLOCKS, nb)
    rows = bb * DIL_BLOCK
    steps = nb // bb

    def body(cur_ref, prev_ref, next_ref, doc_ref, don_ref, lsec_ref, lsen_ref, dlc_ref, dln_ref, out_ref):
        n = pl.program_id(1)
        ok2, in_prev, dist2 = _dil_bias2(dil)
        _, ok_p0, _, dist_p = _dil_bias(dil)
        for b in range(bb):
            rs = slice(b * DIL_BLOCK, (b + 1) * DIL_BLOCK)
            rn = slice((b + 1) * DIL_BLOCK, (b + 2) * DIL_BLOCK)
            two = slice(b * DIL_BLOCK, (b + 2) * DIL_BLOCK)
            first, last = b == 0, b == bb - 1
            if first:
                keys = jnp.concatenate([prev_ref[...], cur_ref[rs, :]], axis=0)
                ok_ab = ok2 & (~in_prev | (n > 0))
            else:
                keys = cur_ref[(b - 1) * DIL_BLOCK:(b + 1) * DIL_BLOCK, :]
                ok_ab = ok2
            qrows = jnp.concatenate([cur_ref[rs, :], next_ref[...]], axis=0) if last else cur_ref[two, :]
            ok_n = ok_p0 & (n < steps - 1) if last else ok_p0
            for hh in range(DIL_HPG):
                sl = slice(hh * HEAD, (hh + 1) * HEAD)
                q2, _, _ = _dil_head(qrows, hh)
                _, k2, v2 = _dil_head(keys, hh)
                q, qn, kc, vc = q2[:DIL_BLOCK], q2[DIL_BLOCK:], k2[DIL_BLOCK:], v2[DIL_BLOCK:]
                do2 = jnp.concatenate([doc_ref[rs, sl], don_ref[:, sl]], axis=0) if last else doc_ref[two, sl]
                do_c, do_n = do2[:DIL_BLOCK], do2[DIL_BLOCK:]
                lse_c = lsec_ref[rs, sl][:, 0:1]
                lse_n = (lsen_ref[:, sl] if last else lsec_ref[rn, sl])[:, 0:1]
                dl_c = dlc_ref[rs, sl][:, 0:1]
                dl_n = (dln_ref[:, sl] if last else dlc_ref[rn, sl])[:, 0:1]
                p_ab, ds_ab = pair(q, k2, v2, do_c, lse_c, dl_c, slopes[hh], dist2, ok_ab)
                p_n, ds_n = pair(qn, kc, vc, do_n, lse_n, dl_n, slopes[hh], dist_p, ok_n)
                dq = lax.dot_general(ds_ab, k2, NN, preferred_element_type=F32)
                dk = lax.dot_general(jnp.concatenate([ds_ab[:, DIL_BLOCK:], ds_n], axis=0), q2, TN, preferred_element_type=F32)
                dv = lax.dot_general(jnp.concatenate([p_ab[:, DIL_BLOCK:], p_n], axis=0), do2, TN, preferred_element_type=F32)
                out_ref[rs, sl] = dq.astype(BF)
                out_ref[rs, DIL_O + hh * HEAD:DIL_O + (hh + 1) * HEAD] = dk.astype(BF)
                out_ref[rs, 2 * DIL_O + hh * HEAD:2 * DIL_O + (hh + 1) * HEAD] = dv.astype(BF)

    cur_w = pl.BlockSpec((rows, DIL_W), lambda r, n: (n, r))
    prev_w = pl.BlockSpec((DIL_BLOCK, DIL_W), lambda r, n: (jnp.maximum(n * bb - 1, 0), r))
    next_w = pl.BlockSpec((DIL_BLOCK, DIL_W), lambda r, n: (jnp.minimum((n + 1) * bb, nb - 1), r))
    cur_o = pl.BlockSpec((rows, DIL_O), lambda r, n: (n, r))
    next_o = pl.BlockSpec((DIL_BLOCK, DIL_O), lambda r, n: (jnp.minimum((n + 1) * bb, nb - 1), r))
    qv, dov, lsev, dlv = _dil_view(qkv, dil), _dil_view(do, dil), _dil_view(lse, dil), _dil_view(delta, dil)
    out = pl.pallas_call(
        body, name=f"dil_bwd{g}", grid=(dil, steps),
        in_specs=[cur_w, prev_w, next_w, cur_o, next_o, cur_o, next_o, cur_o, next_o],
        out_specs=cur_w, out_shape=jax.ShapeDtypeStruct((L, dil * DIL_W), BF),
        compiler_params=_params("parallel", "parallel"))(qv, qv, qv, dov, dov, lsev, lsev, dlv, dlv)
    return out.reshape(S, DIL_W)


def _adamw(name, w, g, m, v):
    R, C = w.shape
    tr, tc = _adamw_block(R, C)

    def body(w_ref, g_ref, m_ref, v_ref, go_ref, d_ref, nm_ref, nv_ref):
        gv = g_ref[...]
        go_ref[...] = gv
        nm = ADAM_B1 * m_ref[...] + (1.0 - ADAM_B1) * gv
        nv = ADAM_B2 * v_ref[...] + (1.0 - ADAM_B2) * (gv * gv)
        m_hat = nm / (1.0 - ADAM_B1 ** ADAM_STEP)
        v_hat = nv / (1.0 - ADAM_B2 ** ADAM_STEP)
        d_ref[...] = -ADAM_LR * (m_hat / (jnp.sqrt(v_hat) + ADAM_EPS) + ADAM_WD * w_ref[...])
        nm_ref[...] = nm
        nv_ref[...] = nv

    spec = pl.BlockSpec((tr, tc), lambda i, j: (i, j))
    shp = jax.ShapeDtypeStruct((R, C), F32)
    return pl.pallas_call(
        body, name=name, grid=(R // tr, C // tc), in_specs=[spec] * 4, out_specs=[spec] * 4, out_shape=[shp] * 4,
        compiler_params=_params("parallel", "parallel"))(w, g, m, v)


ADAMW_BLOCK_ELEMS = 640 * 1024


def _adamw_block(R, C):
    if R * C <= ADAMW_BLOCK_ELEMS:
        return R, C
    tr = _tile_rows(R, max(8, ADAMW_BLOCK_ELEMS // C))
    tc = _tile(C, max(LANE, ADAMW_BLOCK_ELEMS // R))
    if tr * C >= R * tc or R * tc > ADAMW_BLOCK_ELEMS:
        return tr, C
    return R, tc


def _tile_rows(n, pref, mult=8):
    t = (pref // mult) * mult
    while t >= mult:
        if n % t == 0:
            return t
        t -= mult
    return n


ANY = pl.BlockSpec(memory_space=pl.ANY)


def _place():
    x, y, c = lax.axis_index("x"), lax.axis_index("y"), lax.axis_index("c")
    chips = [(1 - x, y), (x, 1 - y), (1 - x, 1 - y)]
    chip_idx = [2 * cx + cy for cx, cy in chips]
    return x, y, c, 2 * x + y, chips, chip_idx


def _rcopy(src, dst, ssem, rsem, dev):
    return pltpu.make_async_remote_copy(src_ref=src, dst_ref=dst, send_sem=ssem, recv_sem=rsem,
                                        device_id=dev, device_id_type=MESH)


HBM = pl.BlockSpec(memory_space=pltpu.HBM)
SEM = pl.BlockSpec(memory_space=pltpu.SEMAPHORE)
EFFECT = pltpu.SideEffectType.DATAFLOW_SIDE_EFFECTING


def _split_copies(kind, srcs, lands, ssem, rsem):
    _, _, c, me, chips, chip_idx = _place()
    cps = []
    for i in range(len(srcs)):
        for k in range(3):
            if kind == "gather":
                rows = srcs[i].shape[0]
                if rows == lands[i].shape[1]:
                    src, dst = srcs[i], lands[i].at[me]
                else:
                    src, dst = srcs[i], lands[i].at[me, pl.ds(pl.multiple_of(c * rows, 16), rows)]
            else:
                src, dst = srcs[i].at[chip_idx[k]], lands[i].at[k]
            cps.append(_rcopy(src, dst, ssem.at[3 * i + k], rsem.at[3 * i + k], (*chips[k], c)))
    return cps


def _exchange_start(name, kind, srcs, lands, groups):
    n, ng = len(srcs), len(groups)

    def body(*refs):
        src_refs, land_refs = refs[:n], refs[n:2 * n]
        sems = refs[2 * n:2 * n + 2 * ng]
        token = refs[-1]
        for gi, grp in enumerate(groups):
            cps = _split_copies(kind, [src_refs[i] for i in grp], [land_refs[i] for i in grp], sems[2 * gi], sems[2 * gi + 1])
            for cp in cps:
                cp.start()
        token[...] = jnp.zeros_like(token)

    arrays = list(srcs) + list(lands)
    out_shape = []
    for grp in groups:
        out_shape += [pltpu.SemaphoreType.DMA((3 * len(grp),)), pltpu.SemaphoreType.DMA((3 * len(grp),))]
    out_shape += [pltpu.HBM(a.shape, a.dtype) for a in arrays] + [jax.ShapeDtypeStruct((8, LANE), F32)]
    outs = pl.pallas_call(
        body, name=name, out_shape=out_shape, in_specs=[HBM] * (2 * n),
        out_specs=[SEM] * (2 * ng) + [HBM] * (2 * n) + [pl.BlockSpec(memory_space=pltpu.VMEM)],
        input_output_aliases={i: 2 * ng + i for i in range(2 * n)},
        compiler_params=pltpu.CompilerParams(has_side_effects=EFFECT),
    )(*[pltpu.with_memory_space_constraint(a, pltpu.HBM) for a in arrays])
    sems = [(outs[2 * gi], outs[2 * gi + 1]) for gi in range(ng)]
    thru = outs[2 * ng:2 * ng + 2 * n]
    return sems, thru[:n], thru[n:], outs[-1]


def _exchange_wait(name, kind, srcs, lands, sems, after):
    n = len(srcs)

    def body(*refs):
        cps = _split_copies(kind, refs[:n], refs[n:2 * n], refs[2 * n], refs[2 * n + 1])
        for cp in cps:
            cp.wait_send()
            cp.wait_recv()

    arrays = list(srcs) + list(lands)
    outs = pl.pallas_call(
        body, name=name, out_shape=[pltpu.HBM(a.shape, a.dtype) for a in arrays],
        in_specs=[HBM] * (2 * n) + [SEM, SEM, ANY], out_specs=[HBM] * (2 * n),
        input_output_aliases={i: i for i in range(2 * n)},
        compiler_params=pltpu.CompilerParams(has_side_effects=EFFECT),
    )(*arrays, sems[0], sems[1], after)
    return outs[:n], outs[n:]


EXCHANGE_CHUNK_BYTES = 3 * 1024 * 1024


def _half_geometry(R, C, axis):
    Rp, Cp = (R // 2, C) if axis == 0 else (R, C // 2)
    tr = _tile_rows(Rp, max(16, EXCHANGE_CHUNK_BYTES // (2 * Cp)), 16)
    return Rp, Cp, tr, Rp // tr


def _pair_sum(name, g, axis):
    G, R, C = g.shape
    Rp, Cp, tr, nb = _half_geometry(R, C, axis)
    steps = G * nb

    def half_block(s, b, h):
        return (s, h * nb + b, 0) if axis == 0 else (s, b, h)

    def body(c_ref, keep_ref, give_ref, out_ref, land, ssem, rsem, credit):
        x, y, c = lax.axis_index("x"), lax.axis_index("y"), lax.axis_index("c")
        sib = (x, y, 1 - c)
        t = pl.program_id(0) * nb + pl.program_id(1)
        slot = t % 2

        @pl.when(t >= 2)
        def _():
            pl.semaphore_wait(credit, 1)

        cp = _rcopy(give_ref.at[0], land.at[slot], ssem.at[slot], rsem.at[slot], sib)
        cp.start()
        cp.wait_recv()
        out_ref[...] = (keep_ref[...].astype(F32) + land[slot].astype(F32)).astype(BF)

        @pl.when(t + 2 < steps)
        def _():
            pl.semaphore_signal(credit, 1, device_id=sib, device_id_type=MESH)

        cp.wait_send()

    blk = (None, tr, Cp)
    grid_spec = pltpu.PrefetchScalarGridSpec(
        num_scalar_prefetch=1, grid=(G, nb),
        in_specs=[pl.BlockSpec(blk, lambda s, b, c_ref: half_block(s, b, c_ref[0])),
                  pl.BlockSpec((1, tr, Cp), lambda s, b, c_ref: half_block(s, b, 1 - c_ref[0]))],
        out_specs=pl.BlockSpec(blk, lambda s, b, c_ref: (s, b, 0)),
        scratch_shapes=[pltpu.VMEM((2, tr, Cp), BF), pltpu.SemaphoreType.DMA((2,)), pltpu.SemaphoreType.DMA((2,)),
                        pltpu.SemaphoreType.REGULAR])
    c_arr = lax.axis_index("c").astype(jnp.int32).reshape(1)
    return pl.pallas_call(
        body, name=name, grid_spec=grid_spec, out_shape=jax.ShapeDtypeStruct((G, Rp, Cp), BF),
        compiler_params=_params("arbitrary", "arbitrary"))(c_arr, g, g)


def _chip_total_join(name, h, landed, axis):
    G, Rp, Cp = h.shape
    R, C = (2 * Rp, Cp) if axis == 0 else (Rp, 2 * Cp)
    tr = _tile_rows(Rp, max(16, EXCHANGE_CHUNK_BYTES // (4 * Cp)), 16)
    nb = Rp // tr

    def body(me_ref, own_ref, l0_ref, l1_ref, l2_ref, full, stage, ssem, rsem, lsem):
        x, y, c = lax.axis_index("x"), lax.axis_index("y"), lax.axis_index("c")
        sib = (x, y, 1 - c)
        b = pl.program_id(0)

        def place(half, r0, rows):
            if axis == 0:
                return full.at[pl.ds(pl.multiple_of(half * Rp + r0, 8), rows), :]
            return full.at[pl.ds(pl.multiple_of(r0, 8), rows), pl.ds(pl.multiple_of(half * Cp, LANE), Cp)]

        def copies(step):
            s = step % 2
            mine = place(c, step * tr, tr)
            return pltpu.make_async_copy(stage.at[s], mine, lsem.at[s]), _rcopy(stage.at[s], mine, ssem.at[s], rsem, sib)

        @pl.when(b >= 2)
        def _():
            loc, rem = copies(b - 2)
            loc.wait()
            rem.wait_send()

        acc = own_ref[...].astype(F32)
        for r in (l0_ref, l1_ref, l2_ref):
            acc = acc + r[...].astype(F32)
        stage[b % 2] = acc
        loc, rem = copies(b)
        loc.start()
        rem.start()

        @pl.when(b == nb - 1)
        def _():
            for step in range(max(0, nb - 2), nb):
                loc, rem = copies(step)
                loc.wait()
                rem.wait_send()
            theirs = place(1 - c, 0, Rp)
            _rcopy(theirs, theirs, ssem.at[0], rsem, sib).wait_recv()

    blk = (None, tr, Cp)
    grid_spec = pltpu.PrefetchScalarGridSpec(
        num_scalar_prefetch=1, grid=(nb,),
        in_specs=[pl.BlockSpec(blk, lambda b, me_ref: (me_ref[0], b, 0))]
        + [pl.BlockSpec(blk, functools.partial(lambda b, me_ref, k: (k, b, 0), k=k)) for k in range(3)],
        out_specs=ANY,
        scratch_shapes=[pltpu.VMEM((2, tr, Cp), F32), pltpu.SemaphoreType.DMA((2,)), pltpu.SemaphoreType.DMA,
                        pltpu.SemaphoreType.DMA((2,))])
    me = (2 * lax.axis_index("x") + lax.axis_index("y")).astype(jnp.int32).reshape(1)
    return pl.pallas_call(
        body, name=name, grid_spec=grid_spec, out_shape=jax.ShapeDtypeStruct((R, C), F32),
        compiler_params=_params("arbitrary"))(me, h, landed, landed, landed)


def _pair_share(name, land):
    G, R, C = land.shape
    Rh = R // 2
    tr = _tile_rows(Rh, max(16, EXCHANGE_CHUNK_BYTES // (2 * C)), 16)
    chunks = [(k, b) for k in range(3) for b in range(Rh // tr)]

    def body(src, dst, buf, lsem, ssem, rsem):
        x, y, c, _, _, chip_idx = _place()
        sib = (x, y, 1 - c)

        def region(ref, k, half, r0, rows):
            return ref.at[chip_idx[k], pl.ds(pl.multiple_of(half * Rh + r0, 16), rows)]

        def load(t):
            k, b = chunks[t]
            return pltpu.make_async_copy(region(src, k, c, b * tr, tr), buf.at[t % 2], lsem.at[t % 2])

        def send(t):
            k, b = chunks[t]
            return _rcopy(buf.at[t % 2], region(dst, k, c, b * tr, tr), ssem.at[t % 2], rsem.at[k], sib)

        load(0).start()
        for t in range(len(chunks)):
            load(t).wait()
            if t + 1 < len(chunks):
                if t >= 1:
                    send(t - 1).wait_send()
                load(t + 1).start()
            send(t).start()
        for t in range(max(0, len(chunks) - 2), len(chunks)):
            send(t).wait_send()
        for k in range(3):
            theirs = region(dst, k, 1 - c, 0, Rh)
            _rcopy(theirs, theirs, ssem.at[0], rsem.at[k], sib).wait_recv()

    return pl.pallas_call(
        body, name=name, in_specs=[ANY], out_specs=ANY, out_shape=jax.ShapeDtypeStruct(land.shape, land.dtype),
        input_output_aliases={0: 0},
        scratch_shapes=[pltpu.VMEM((2, tr, C), land.dtype), pltpu.SemaphoreType.DMA((2,)), pltpu.SemaphoreType.DMA((2,)),
                        pltpu.SemaphoreType.DMA((3,))],
    )(land)


def _allreduce_small(v):
    R, K = v.shape
    ndev = 8

    def body(v_ref, o_ref, land, ssem, rsem):
        x, y, c = lax.axis_index("x"), lax.axis_index("y"), lax.axis_index("c")
        me = 4 * x + 2 * y + c
        land[me] = v_ref[...]
        cps = []
        for r in range(1, ndev):
            fx, fy, fc = (r >> 2) & 1, (r >> 1) & 1, r & 1
            peer = (x ^ fx, y ^ fy, c ^ fc)
            cp = _rcopy(v_ref, land.at[me], ssem.at[r - 1], rsem.at[r - 1], peer)
            cp.start()
            cps.append((cp, 4 * peer[0] + 2 * peer[1] + peer[2], r))
        for cp, src, r in cps:
            cp.wait_send()
            _rcopy(v_ref, land.at[src], ssem.at[r - 1], rsem.at[r - 1], (x, y, c)).wait_recv()
        acc = land[0]
        for d in range(1, ndev):
            acc = acc + land[d]
        o_ref[...] = acc

    vm = pl.BlockSpec(memory_space=pltpu.VMEM)
    return pl.pallas_call(
        body, name="allreduce_small", in_specs=[vm], out_specs=vm, out_shape=jax.ShapeDtypeStruct((R, K), F32),
        scratch_shapes=[pltpu.VMEM((ndev, R, K), F32), pltpu.SemaphoreType.DMA((ndev - 1,)), pltpu.SemaphoreType.DMA((ndev - 1,))],
    )(v)


IN_SPLITS = (Q_RANK, KV_RANK, QK_ROPE, DIL_HEADS * HEAD, DIL_HEADS * HEAD, DIL_HEADS * HEAD, D_MODEL, D_MODEL)
IN_OFF = tuple(int(v) for v in np.cumsum((0,) + IN_SPLITS))


def _unshard_cols(g):
    G, K, Ns = g.shape
    return g.transpose(1, 0, 2).reshape(K, G * Ns)


def _shard_cols(w):
    K, N = w.shape
    return w.reshape(K, N_CHIPS, N // N_CHIPS).transpose(1, 0, 2)


def _rope_pad(w):
    half = QK_ROPE // 2
    z = jnp.zeros(w.shape[:-1] + (half,), w.dtype)
    return jnp.concatenate([w[..., :half], z, w[..., half:], z], axis=-1)


def _rope_unpad(w):
    half = QK_ROPE // 2
    return jnp.concatenate([w[..., :half], w[..., 2 * half:3 * half]], axis=-1)


def _split_w_in(w_in_g):
    G, K, Ns = w_in_g.shape

    def cols(lo, hi):
        pieces = [w_in_g[k][:, max(lo, k * Ns) - k * Ns:min(hi, (k + 1) * Ns) - k * Ns]
                  for k in range(G) if max(lo, k * Ns) < min(hi, (k + 1) * Ns)]
        return pieces[0] if len(pieces) == 1 else jnp.concatenate(pieces, axis=1)

    p = [(IN_OFF[i], IN_OFF[i + 1]) for i in range(8)]
    w_lat = jnp.concatenate([cols(*p[0]), cols(*p[1]), _rope_pad(cols(*p[2])),
                             jnp.zeros((K, LAT_W - _KPE.stop), w_in_g.dtype)], axis=1)
    w_dil = [jnp.concatenate([cols(p[3 + t][0] + g * DIL_O, p[3 + t][0] + (g + 1) * DIL_O) for t in range(3)], axis=1)
             for g in range(DIL_GROUPS)]
    w_gate = cols(p[6][0], p[7][1])
    return w_lat, w_dil, w_gate


def _merge_dw_in(dw_lat, dw_dil, dw_gate):
    parts = [dw_lat[:, _CQ], dw_lat[:, _CKV], _rope_unpad(dw_lat[:, _KPE])]
    for t in range(3):
        parts += [dw_dil[g][:, t * DIL_O:(t + 1) * DIL_O] for g in range(DIL_GROUPS)]
    parts.append(dw_gate)
    width = sum(p.shape[1] for p in parts) // N_CHIPS
    shards = []
    for k in range(N_CHIPS):
        pieces, at = [], 0
        for p in parts:
            lo, hi = max(k * width, at), min((k + 1) * width, at + p.shape[1])
            if lo < hi:
                pieces.append(p[:, lo - at:hi - at])
            at += p.shape[1]
        shards.append(jnp.concatenate(pieces, axis=1))
    return jnp.stack(shards)


def _split_w_uq(w_uq_g):
    w = _unshard_cols(w_uq_g)
    K = w.shape[0]
    w = w.reshape(K, MLA_HEADS, QK_NOPE + QK_ROPE)
    return w[:, :, :QK_NOPE].reshape(K, MLA_HEADS * HEAD), _rope_pad(w[:, :, QK_NOPE:]).reshape(K, MLA_HEADS * HEAD)


def _merge_dw_uq(dw_n, dw_p):
    K = dw_n.shape[0]
    w = jnp.concatenate([dw_n.reshape(K, MLA_HEADS, HEAD), _rope_unpad(dw_p.reshape(K, MLA_HEADS, HEAD))], axis=-1)
    return _shard_cols(w.reshape(K, MLA_HEADS * (QK_NOPE + QK_ROPE)))


def _split_w_ukv(w_ukv_g):
    w = _unshard_cols(w_ukv_g)
    K = w.shape[0]
    w = w.reshape(K, MLA_HEADS, 2 * HEAD)
    return w[:, :, :HEAD].reshape(K, MLA_HEADS * HEAD), w[:, :, HEAD:].reshape(K, MLA_HEADS * HEAD)


def _merge_dw_ukv(dw_k, dw_v):
    K = dw_k.shape[0]
    w = jnp.concatenate([dw_k.reshape(K, MLA_HEADS, HEAD), dw_v.reshape(K, MLA_HEADS, HEAD)], axis=-1)
    return _shard_cols(w.reshape(K, MLA_HEADS * 2 * HEAD))


GATHER_GROUPS = (("w_in",), ("w_uq", "w_ukv", "w_o_mla", "w_o_dil", "w_out"), ("w_up", "w_down", "conv_w"))
SHARED_FETCH = ("w_in",)
REDUCE_GROUPS = (("w_down", "w_up"), ("w_out", "w_o_mla", "w_o_dil"), ("w_uq", "w_ukv", "w_in"))


def _local_step(x, tgt, W, fetch, emit):
    S, D = x.shape
    cos, sin_s = _rope_tables(S)
    w_lat, w_dil, w_gate = _split_w_in(fetch(0, x)["w_in"])

    h = _rmsnorm_fwd("attn_norm", x, W["attn_norm_g"])
    lat = _mm_nn("proj_lat", h, w_lat)
    qkv = [_mm_nn(f"proj_dil{g}", h, w_dil[g], o_dtype=BF) for g in range(DIL_GROUPS)]
    gpre = _mm_nn("proj_gate", h, w_gate, o_dtype=BF)
    WB = fetch(1, gpre)
    w_uqn, w_uqp = _split_w_uq(WB["w_uq"])
    w_k, w_v = _split_w_ukv(WB["w_ukv"])
    w_o_mla, w_o_dil = WB["w_o_mla"], WB["w_o_dil"]
    w_out = WB["w_out"].reshape(D, D)
    qn_, kvn, kpe = _mla_prep(lat, W["q_norm_g"], W["kv_norm_g"], cos, sin_s)
    q_nope = _mm_nn("q_nope", qn_, w_uqn, o_dtype=BF)
    q_pe = _rope("q_rope", _mm_nn("q_pe", qn_, w_uqp), cos, sin_s, False)
    k_nope = _mm_nn("k_nope", kvn, w_k, o_dtype=BF)
    v_mla = _mm_nn("v_mla", kvn, w_v, o_dtype=BF)
    attn_a, lse_a = _mla_fwd(q_nope, q_pe, k_nope, kpe, v_mla)
    dil = [_dil_fwd(qkv[g], g) for g in range(DIL_GROUPS)]
    attn_b, lse_b = _dil_combine([o for o, _ in dil], [l for _, l in dil])
    o_a = _mm_nn("o_mla", attn_a, w_o_mla, o_dtype=BF)
    o_b = _mm_nn("o_dil", attn_b, w_o_dil, o_dtype=BF)
    merge = _merge_fwd(gpre, W["b_gate"], o_a, o_b)
    x1 = _mm_nn("out_proj", merge, w_out, add=x)
    WC = fetch(2, merge)
    w_up = WC["w_up"]
    G4, _, C = w_up.shape
    w_down = WC["w_down"].reshape(G4 // 2, C, D)
    conv_w = WC["conv_w"]
    conv_b = W["conv_b"].reshape(G4, 1, C)
    h2 = _rmsnorm_fwd("ffn_norm", x1, W["ffn_norm_g"])
    u_pre = _up_fwd(h2, w_up)
    act, u = _ffn_act(u_pre, conv_w, conv_b)
    x2 = _down_fwd(act, w_down, x1)
    dx2, d_final_g, loss8 = _final_loss(x2, tgt, W["final_norm_g"])

    d_act = _down_dgrad(dx2, w_down)
    dw_down = _down_wgrad(act, dx2)
    du = _ffn_act_bwd(u, d_act)
    du_pre, d_conv_w, d_conv_b = _conv_bwd(du, u_pre, conv_w)
    dh2 = _up_dgrad(du_pre, w_up)
    dw_up = _up_wgrad(h2, du_pre)
    zero = emit(0, {"w_down": dw_down.reshape(N_CHIPS, (G4 // 2) * C // N_CHIPS, D), "w_up": dw_up})
    dx1, d_ffn_g = _rmsnorm_bwd("ffn_norm_bwd", dh2, x1, W["ffn_norm_g"] + zero, dx2)
    d_merge = _mm_nt("out_proj_dgrad", dx1, w_out, o_dtype=BF)
    dw_out = _mm_tn("out_proj_wgrad", merge, dx1)
    d_oa, d_ob, d_gpre, d_b_gate = _merge_bwd(d_merge, gpre, W["b_gate"], o_a, o_b)
    d_attn_a = _mm_nt("o_mla_dgrad", d_oa, w_o_mla, o_dtype=BF)
    dw_o_mla = _mm_tn("o_mla_wgrad", attn_a, d_oa, shards=N_CHIPS)
    d_attn_b = _mm_nt("o_dil_dgrad", d_ob, w_o_dil, o_dtype=BF)
    dw_o_dil = _mm_tn("o_dil_wgrad", attn_b, d_ob, shards=N_CHIPS)
    zero = emit(1, {"w_out": dw_out.reshape(N_CHIPS, D // N_CHIPS, D), "w_o_mla": dw_o_mla, "w_o_dil": dw_o_dil})
    q_norm_g = W["q_norm_g"] + zero
    delta_b = _dil_delta(d_attn_b, attn_b)
    d_qkv = [_dil_bwd(qkv[g], d_attn_b, lse_b, delta_b, g) for g in range(DIL_GROUPS)]
    dq_nope, dq_pe_rot, dk_nope, dv_mla, dkpe_rot = _mla_bwd(q_nope, q_pe, k_nope, kpe, v_mla, d_attn_a, attn_a, lse_a)
    dq_pe = _rope("q_rope_bwd", dq_pe_rot, cos, sin_s, True)
    d_qn = _mm_nt("q_pe_dgrad", dq_pe, w_uqp, add=_mm_nt("q_nope_dgrad", dq_nope, w_uqn))
    d_kvn = _mm_nt("v_dgrad", dv_mla, w_v, add=_mm_nt("k_nope_dgrad", dk_nope, w_k))
    dw_uq = _merge_dw_uq(_mm_tn("q_nope_wgrad", qn_, dq_nope), _mm_tn("q_pe_wgrad", qn_, dq_pe))
    dw_ukv = _merge_dw_ukv(_mm_tn("k_nope_wgrad", kvn, dk_nope), _mm_tn("v_wgrad", kvn, dv_mla))
    d_lat, d_q_g, d_kv_g = _mla_prep_bwd(lat, q_norm_g, W["kv_norm_g"], cos, sin_s, d_qn, d_kvn, dkpe_rot)
    dw_in = _merge_dw_in(_mm_tn("proj_lat_wgrad", h, d_lat),
                         [_mm_tn(f"proj_dil{g}_wgrad", h, d_qkv[g]) for g in range(DIL_GROUPS)],
                         _mm_tn("proj_gate_wgrad", h, d_gpre))
    zero = emit(2, {"w_uq": dw_uq, "w_ukv": dw_ukv, "w_in": dw_in})
    dh = _mm_nt("proj_lat_dgrad", d_lat, w_lat + zero.astype(BF))
    for g in range(DIL_GROUPS):
        dh = _mm_nt(f"proj_dil{g}_dgrad", d_qkv[g], w_dil[g], add=dh)
    dh = _mm_nt("proj_gate_dgrad", d_gpre, w_gate, add=dh)
    grad_x, d_attn_g = _rmsnorm_bwd("attn_norm_bwd", dh, x, W["attn_norm_g"], dx1)

    small = {"attn_norm_g": d_attn_g, "b_gate": d_b_gate, "q_norm_g": d_q_g, "kv_norm_g": d_kv_g,
             "ffn_norm_g": d_ffn_g, "conv_w": d_conv_w, "conv_b": d_conv_b.reshape(1, G4 * C),
             "final_norm_g": d_final_g}
    return loss8[0, 0], grad_x, small


BIG = ("w_in", "w_uq", "w_ukv", "w_o_mla", "w_o_dil", "w_out", "w_up", "w_down")
SMALL = ("attn_norm_g", "b_gate", "q_norm_g", "kv_norm_g", "ffn_norm_g", "conv_w", "conv_b", "final_norm_g")
WEIGHTS = ("attn_norm_g", "w_in", "b_gate", "q_norm_g", "w_uq", "kv_norm_g", "w_ukv", "w_o_mla", "w_o_dil",
           "w_out", "ffn_norm_g", "w_up", "conv_w", "conv_b", "w_down", "final_norm_g")
SMALL_ROWS = 8
COLUMN_MAJOR = ("w_in", "w_up")
HALF_AXIS = {"w_down": 1}


def _gather_start(shards):
    chip = 2 * lax.axis_index("x") + lax.axis_index("y")
    c = lax.axis_index("c")

    def prepare(names, zero):
        srcs, lands = [], []
        for n in names:
            s = shards[n] + zero
            s = s if n == "conv_w" else s.astype(BF)
            lands.append(lax.dynamic_update_slice(lax.empty((N_CHIPS,) + s.shape, s.dtype), s[None], (chip, 0, 0)))
            if n in SHARED_FETCH:
                s = lax.dynamic_slice_in_dim(s, c * (s.shape[0] // 2), s.shape[0] // 2, 0)
            srcs.append(s)
        return srcs, lands

    n0 = len(GATHER_GROUPS[0])
    srcs0, lands0 = prepare(GATHER_GROUPS[0], 0.0)
    sems0, srcs0, lands0, token = _exchange_start("gather_start0", "gather", srcs0, lands0, [list(range(n0))])
    srcs, lands = prepare([n for grp in GATHER_GROUPS[1:] for n in grp], token[0, 0])
    groups, at = [], 0
    for grp in GATHER_GROUPS[1:]:
        groups.append(list(range(at, at + len(grp))))
        at += len(grp)
    sems, srcs, lands, token1 = _exchange_start("gather_start1", "gather", srcs, lands, groups)

    def fetch(i, after):
        if i == 0:
            _, got = _exchange_wait("gather_wait0", "gather", srcs0, lands0, sems0[0], token1)
        else:
            idx = groups[i - 1]
            _, got = _exchange_wait(f"gather_wait{i}", "gather", [srcs[j] for j in idx], [lands[j] for j in idx],
                                    sems[i - 1], after)
        return {n: _pair_share(f"pair_share_{n}", g) if n in SHARED_FETCH else g for n, g in zip(GATHER_GROUPS[i], got)}

    return fetch, token[0, 0]


def _reduce_start(i, grads):
    names = REDUCE_GROUPS[i]
    hs = [_pair_sum(f"pair_sum_{n}", grads[n], HALF_AXIS.get(n, 0)) for n in names]
    lands = [lax.empty((3,) + h.shape[1:], h.dtype) for h in hs]
    sems, hs, lands, token = _exchange_start(f"reduce_start{i}", "scatter", hs, lands, [list(range(len(names)))])
    return (sems[0], hs, lands), token[0, 0]


def _reduce_finish(i, pending, after):
    sems, hs, lands = pending
    hs, lands = _exchange_wait(f"reduce_wait{i}", "scatter", hs, lands, sems, after)
    out = {}
    for n, h, landed in zip(REDUCE_GROUPS[i], hs, lands):
        out[n] = _chip_total_join(f"chip_total_{n}", h, landed, HALF_AXIS.get(n, 0))
    return out


def _reduce_small(small):
    flat = [small[n].reshape(-1) for n in SMALL]
    sizes = [f.shape[0] for f in flat]
    total = sum(sizes)
    width = -(-total // (SMALL_ROWS * LANE)) * LANE
    packed = jnp.concatenate(flat + [jnp.zeros((SMALL_ROWS * width - total,), F32)]).reshape(SMALL_ROWS, width)
    red = _allreduce_small(packed).reshape(-1)
    out, off = {}, 0
    for n, s in zip(SMALL, sizes):
        out[n] = red[off:off + s]
        off += s
    return out


def kernel(x, attn_norm_g, w_in, b_gate, q_norm_g, w_uq, kv_norm_g, w_ukv, w_o_mla, w_o_dil, w_out, ffn_norm_g, w_up, conv_w, conv_b, w_down, final_norm_g, loss_target, m_attn_norm_g, m_w_in, m_b_gate, m_q_norm_g, m_w_uq, m_kv_norm_g, m_w_ukv, m_w_o_mla, m_w_o_dil, m_w_out, m_ffn_norm_g, m_w_up, m_conv_w, m_conv_b, m_w_down, m_final_norm_g, v_attn_norm_g, v_w_in, v_b_gate, v_q_norm_g, v_w_uq, v_kv_norm_g, v_w_ukv, v_w_o_mla, v_w_o_dil, v_w_out, v_ffn_norm_g, v_w_up, v_conv_w, v_conv_b, v_w_down, v_final_norm_g):
    given = dict(attn_norm_g=attn_norm_g, w_in=w_in, b_gate=b_gate, q_norm_g=q_norm_g, w_uq=w_uq, kv_norm_g=kv_norm_g,
                 w_ukv=w_ukv, w_o_mla=w_o_mla, w_o_dil=w_o_dil, w_out=w_out, ffn_norm_g=ffn_norm_g, w_up=w_up,
                 conv_w=conv_w, conv_b=conv_b, w_down=w_down, final_norm_g=final_norm_g)
    moments_m = dict(attn_norm_g=m_attn_norm_g, w_in=m_w_in, b_gate=m_b_gate, q_norm_g=m_q_norm_g, w_uq=m_w_uq,
                     kv_norm_g=m_kv_norm_g, w_ukv=m_w_ukv, w_o_mla=m_w_o_mla, w_o_dil=m_w_o_dil, w_out=m_w_out,
                     ffn_norm_g=m_ffn_norm_g, w_up=m_w_up, conv_w=m_conv_w, conv_b=m_conv_b, w_down=m_w_down,
                     final_norm_g=m_final_norm_g)
    moments_v = dict(attn_norm_g=v_attn_norm_g, w_in=v_w_in, b_gate=v_b_gate, q_norm_g=v_q_norm_g, w_uq=v_w_uq,
                     kv_norm_g=v_kv_norm_g, w_ukv=v_w_ukv, w_o_mla=v_w_o_mla, w_o_dil=v_w_o_dil, w_out=v_w_out,
                     ffn_norm_g=v_ffn_norm_g, w_up=v_w_up, conv_w=v_conv_w, conv_b=v_conv_b, w_down=v_w_down,
                     final_norm_g=v_final_norm_g)

    fetch, zero = _gather_start({n: given[n][0] for n in BIG + ("conv_w",)})
    W = {n: given[n] for n in ("b_gate", "q_norm_g", "kv_norm_g", "ffn_norm_g", "conv_b")}
    W["attn_norm_g"] = given["attn_norm_g"] + zero
    W["final_norm_g"] = given["final_norm_g"].reshape(1, -1)

    pending = {}

    def emit(i, grads):
        pending[i], token = _reduce_start(i, grads)
        return token

    loss_part, grad_x, small = _local_step(x[0], loss_target[0], W, fetch, emit)
    loss = lax.psum(loss_part, ("x", "y", "c"))
    grads, delta, new_m, new_v = {}, {}, {}, {}

    def adamw(n, g):
        shp = given[n].shape
        two_d = (-1, shp[-1]) if len(shp) > 1 else (1, -1)
        view = (lambda a: a.reshape(two_d).T) if n in COLUMN_MAJOR else (lambda a: a.reshape(two_d))
        back = (lambda a: a.T.reshape(shp)) if n in COLUMN_MAJOR else (lambda a: a.reshape(shp))
        go, d, nm, nv = _adamw(f"adamw_{n}", view(given[n]), view(g), view(moments_m[n]), view(moments_v[n]))
        grads[n], delta[n], new_m[n], new_v[n] = back(go), back(d), back(nm), back(nv)

    after = grad_x
    for i in range(len(REDUCE_GROUPS)):
        for n, g in _reduce_finish(i, pending[i], after).items():
            adamw(n, g)
        after = delta[REDUCE_GROUPS[i][-1]]
    g_small = _reduce_small(small)
    chip = 2 * lax.axis_index("x") + lax.axis_index("y")
    for n in SMALL:
        if n == "conv_w":
            full = g_small[n].reshape(N_CHIPS, 3, -1)
            adamw(n, lax.dynamic_index_in_dim(full, chip, 0, keepdims=True))
        else:
            adamw(n, g_small[n])

    return (loss, grad_x[None], *[grads[n] for n in WEIGHTS], *[delta[n] for n in WEIGHTS],
            *[new_m[n] for n in WEIGHTS], *[new_v[n] for n in WEIGHTS])
```

```python
import functools
import math

import numpy as np
import jax
import jax.numpy as jnp
from jax import lax
from jax.experimental import pallas as pl
from jax.experimental.pallas import tpu as pltpu

F32 = jnp.float32
BF = jnp.bfloat16
MESH = pl.DeviceIdType.MESH

D_MODEL = 2048
MLA_HEADS = 8
QK_NOPE = 128
QK_ROPE = 64
Q_RANK = 512
KV_RANK = 256
ROPE_THETA = 10000.0
DIL_PATTERNS = ((128, 1), (512, 4), (2048, 16))
DIL_GROUPS = 3
DIL_HPG = 4
DIL_HEADS = 12
HEAD = 128
DIL_BLOCK = 128
ALIBI_MAX_BIAS = 8.0
NORM_EPS = 1e-6
N_CHIPS = 4
ADAM_LR = 0.001
ADAM_B1 = 0.9
ADAM_B2 = 0.999
ADAM_EPS = 1e-08
ADAM_WD = 0.01
ADAM_STEP = 10

LANE = 128
VMEM_LIMIT = 56 * 1024 * 1024
MLA_SCALE = (QK_NOPE + QK_ROPE) ** -0.5
DIL_SCALE = HEAD ** -0.5


def _params(*sem):
    return pltpu.CompilerParams(dimension_semantics=sem, vmem_limit_bytes=VMEM_LIMIT)


def _tile(n, pref):
    t = (pref // LANE) * LANE
    while t >= LANE:
        if n % t == 0:
            return t
        t -= LANE
    return n


NN = (((1,), (0,)), ((), ()))
NT = (((1,), (1,)), ((), ()))
TN = (((0,), (0,)), ((), ()))


def _mm_call(name, a, b, add, *, grid, a_spec, b_spec, add_spec, o_spec, o_shape, o_dtype, acc_shape, dims, nk):
    nax = len(grid)

    def body(*refs):
        if add is None:
            a_ref, b_ref, o_ref = refs[:3]
            c_ref = None
            scr = refs[3:]
        else:
            a_ref, b_ref, c_ref, o_ref = refs[:4]
            scr = refs[4:]
        prod = lax.dot_general(a_ref[...].astype(BF), b_ref[...].astype(BF), dims, preferred_element_type=F32)
        if nk == 1:
            if c_ref is not None:
                prod = prod + c_ref[...]
            o_ref[...] = prod.astype(o_ref.dtype)
        else:
            acc = scr[0]
            k = pl.program_id(nax - 1)

            @pl.when(k == 0)
            def _():
                if c_ref is not None:
                    acc[...] = prod + c_ref[...]
                else:
                    acc[...] = prod

            @pl.when(k > 0)
            def _():
                acc[...] += prod

            @pl.when(k == nk - 1)
            def _():
                o_ref[...] = acc[...].astype(o_ref.dtype)

    ins = [a, b] + ([] if add is None else [add])
    specs = [a_spec, b_spec] + ([] if add is None else [add_spec])
    sem = ("parallel",) * (nax - 1) + ("arbitrary",)
    return pl.pallas_call(
        body, name=name, grid=grid, in_specs=specs, out_specs=o_spec,
        out_shape=jax.ShapeDtypeStruct(o_shape, o_dtype),
        scratch_shapes=[] if nk == 1 else [pltpu.VMEM(acc_shape, F32)],
        compiler_params=_params(*sem),
    )(*ins)


def _mm_nn(name, a, b, *, add=None, o_dtype=F32):
    M, K = a.shape
    sharded = b.ndim == 3
    Ns = b.shape[-1]
    N = Ns * (b.shape[0] if sharded else 1)
    tm, tn, tk = _tile(M, 1024), _tile(Ns, 1024), _tile(K, 2048)
    per = Ns // tn
    nk = K // tk
    if sharded:
        b_spec = pl.BlockSpec((None, tk, tn), lambda i, j, k: (j // per, k, j % per))
    else:
        b_spec = pl.BlockSpec((tk, tn), lambda i, j, k: (k, j))
    return _mm_call(
        name, a, b, add, grid=(M // tm, N // tn, nk),
        a_spec=pl.BlockSpec((tm, tk), lambda i, j, k: (i, k)), b_spec=b_spec,
        add_spec=pl.BlockSpec((tm, tn), lambda i, j, k: (i, j)),
        o_spec=pl.BlockSpec((tm, tn), lambda i, j, k: (i, j)),
        o_shape=(M, N), o_dtype=o_dtype, acc_shape=(tm, tn), dims=NN, nk=nk)


def _mm_nt(name, a, b, *, add=None, o_dtype=F32):
    M, K = a.shape
    sharded = b.ndim == 3
    N, Ks = b.shape[-2], b.shape[-1]
    tm, tn, tk = _tile(M, 1024), _tile(N, 1024), _tile(Ks, 2048)
    per = Ks // tk
    nk = K // tk
    if sharded:
        b_spec = pl.BlockSpec((None, tn, tk), lambda i, j, k: (k // per, j, k % per))
    else:
        b_spec = pl.BlockSpec((tn, tk), lambda i, j, k: (j, k))
    return _mm_call(
        name, a, b, add, grid=(M // tm, N // tn, nk),
        a_spec=pl.BlockSpec((tm, tk), lambda i, j, k: (i, k)), b_spec=b_spec,
        add_spec=pl.BlockSpec((tm, tn), lambda i, j, k: (i, j)),
        o_spec=pl.BlockSpec((tm, tn), lambda i, j, k: (i, j)),
        o_shape=(M, N), o_dtype=o_dtype, acc_shape=(tm, tn), dims=NT, nk=nk)


def _mm_tn(name, a, b, *, shards=1, o_dtype=BF):
    S, M = a.shape
    N = b.shape[1]
    Ns = N // shards
    tm, tn, tk = _tile(M, 1024), _tile(Ns, 1024), _tile(S, 2048)
    per = Ns // tn
    nk = S // tk
    if shards > 1:
        o_spec = pl.BlockSpec((None, tm, tn), lambda i, j, k: (j // per, i, j % per))
        o_shape = (shards, M, Ns)
    else:
        o_spec = pl.BlockSpec((tm, tn), lambda i, j, k: (i, j))
        o_shape = (M, N)
    return _mm_call(
        name, a, b, None, grid=(M // tm, N // tn, nk),
        a_spec=pl.BlockSpec((tk, tm), lambda i, j, k: (k, i)),
        b_spec=pl.BlockSpec((tk, tn), lambda i, j, k: (k, j)),
        add_spec=None, o_spec=o_spec, o_shape=o_shape, o_dtype=o_dtype, acc_shape=(tm, tn), dims=TN, nk=nk)


def _up_fwd(h2, w_up):
    S, D = h2.shape
    G, _, C = w_up.shape
    tm = _tile(S, 512)
    return _mm_call(
        "up_fwd", h2, w_up, None, grid=(G, S // tm, 1),
        a_spec=pl.BlockSpec((tm, D), lambda g, i, k: (i, 0)),
        b_spec=pl.BlockSpec((None, D, C), lambda g, i, k: (g, 0, 0)),
        add_spec=None, o_spec=pl.BlockSpec((None, tm, C), lambda g, i, k: (g, i, 0)),
        o_shape=(G, S, C), o_dtype=BF, acc_shape=None, dims=NN, nk=1)


def _up_dgrad(du_pre, w_up):
    G, S, C = du_pre.shape
    D = w_up.shape[1]
    tm, tn = _tile(S, 1024), _tile(D, 1024)
    return _mm_call(
        "up_dgrad", du_pre, w_up, None, grid=(S // tm, D // tn, G),
        a_spec=pl.BlockSpec((None, tm, C), lambda i, j, g: (g, i, 0)),
        b_spec=pl.BlockSpec((None, tn, C), lambda i, j, g: (g, j, 0)),
        add_spec=None, o_spec=pl.BlockSpec((tm, tn), lambda i, j, g: (i, j)),
        o_shape=(S, D), o_dtype=F32, acc_shape=(tm, tn), dims=NT, nk=G)


def _up_wgrad(h2, du_pre):
    G, S, C = du_pre.shape
    D = h2.shape[1]
    tm, tk = _tile(D, 512), _tile(S, 2048)
    return _mm_call(
        "up_wgrad", h2, du_pre, None, grid=(G, D // tm, S // tk),
        a_spec=pl.BlockSpec((tk, tm), lambda g, i, k: (k, i)),
        b_spec=pl.BlockSpec((None, tk, C), lambda g, i, k: (g, k, 0)),
        add_spec=None, o_spec=pl.BlockSpec((None, tm, C), lambda g, i, k: (g, i, 0)),
        o_shape=(G, D, C), o_dtype=BF, acc_shape=(tm, C), dims=TN, nk=S // tk)


def _down_fwd(act, w_down, x1):
    G, S, C = act.shape
    D = w_down.shape[2]
    tm, tn = _tile(S, 1024), _tile(D, 1024)
    return _mm_call(
        "down_fwd", act, w_down, x1, grid=(S // tm, D // tn, G),
        a_spec=pl.BlockSpec((None, tm, C), lambda i, j, g: (g, i, 0)),
        b_spec=pl.BlockSpec((None, C, tn), lambda i, j, g: (g, 0, j)),
        add_spec=pl.BlockSpec((tm, tn), lambda i, j, g: (i, j)),
        o_spec=pl.BlockSpec((tm, tn), lambda i, j, g: (i, j)),
        o_shape=(S, D), o_dtype=F32, acc_shape=(tm, tn), dims=NN, nk=G)


def _down_dgrad(dx2, w_down):
    S, D = dx2.shape
    G, C, _ = w_down.shape
    tm = _tile(S, 512)
    return _mm_call(
        "down_dgrad", dx2, w_down, None, grid=(G, S // tm, 1),
        a_spec=pl.BlockSpec((tm, D), lambda g, i, k: (i, 0)),
        b_spec=pl.BlockSpec((None, C, D), lambda g, i, k: (g, 0, 0)),
        add_spec=None, o_spec=pl.BlockSpec((None, tm, C), lambda g, i, k: (g, i, 0)),
        o_shape=(G, S, C), o_dtype=BF, acc_shape=None, dims=NT, nk=1)


def _down_wgrad(act, dx2):
    G, S, C = act.shape
    D = dx2.shape[1]
    tn, tk = _tile(D, 512), _tile(S, 1024)
    return _mm_call(
        "down_wgrad", act, dx2, None, grid=(G, D // tn, S // tk),
        a_spec=pl.BlockSpec((None, tk, C), lambda g, j, k: (g, k, 0)),
        b_spec=pl.BlockSpec((tk, tn), lambda g, j, k: (k, j)),
        add_spec=None, o_spec=pl.BlockSpec((None, C, tn), lambda g, j, k: (g, 0, j)),
        o_shape=(G, C, D), o_dtype=BF, acc_shape=(C, tn), dims=TN, nk=S // tk)


def _row(ts, c):
    return pl.BlockSpec((ts, c), lambda i: (i, 0))


def _bcast(r, c):
    return pl.BlockSpec((r, c), lambda i: (0, 0))


def _accumulate(i, ref, val):
    @pl.when(i == 0)
    def _():
        ref[...] = val

    @pl.when(i > 0)
    def _():
        ref[...] += val


def _rstd(xv):
    return lax.rsqrt(jnp.mean(xv * xv, axis=-1, keepdims=True) + NORM_EPS)


def _rmsnorm_fwd(name, x, g):
    S, D = x.shape
    ts = _tile(S, 512)

    def body(x_ref, g_ref, o_ref):
        xv = x_ref[...]
        o_ref[...] = (xv * _rstd(xv) * g_ref[...]).astype(o_ref.dtype)

    return pl.pallas_call(
        body, name=name, grid=(S // ts,), in_specs=[_row(ts, D), _bcast(1, D)], out_specs=_row(ts, D),
        out_shape=jax.ShapeDtypeStruct((S, D), BF), compiler_params=_params("parallel"))(x, g)


def _norm_bwd_rows(dy, xv, g):
    r = _rstd(xv)
    xh = xv * r
    dxh = dy * g
    dx = r * (dxh - xh * jnp.mean(dxh * xh, axis=-1, keepdims=True))
    return dx, jnp.sum(dy * xh, axis=0, keepdims=True)


def _rmsnorm_bwd(name, dy, x, g, res):
    S, D = x.shape
    ts = _tile(S, 512)

    def body(dy_ref, x_ref, g_ref, res_ref, dx_ref, dg_ref):
        dx, dg = _norm_bwd_rows(dy_ref[...], x_ref[...], g_ref[...])
        dx_ref[...] = dx + res_ref[...]
        _accumulate(pl.program_id(0), dg_ref, dg)

    return pl.pallas_call(
        body, name=name, grid=(S // ts,),
        in_specs=[_row(ts, D), _row(ts, D), _bcast(1, D), _row(ts, D)],
        out_specs=[_row(ts, D), _bcast(1, D)],
        out_shape=[jax.ShapeDtypeStruct((S, D), F32), jax.ShapeDtypeStruct((1, D), F32)],
        compiler_params=_params("arbitrary"))(dy, x, g, res)


def _rope_tables(S):
    half = QK_ROPE // 2
    pos = jnp.arange(S, dtype=F32)
    inv_freq = ROPE_THETA ** (-jnp.arange(0, QK_ROPE, 2, dtype=F32) / QK_ROPE)
    ang = pos[:, None] * inv_freq[None, :]
    cos, sin = jnp.cos(ang), jnp.sin(ang)
    z = jnp.zeros((S, half), F32)
    return jnp.concatenate([cos, z, cos, z], axis=1), jnp.concatenate([-sin, z, sin, z], axis=1)


def _rope_lanes(x, cos, sin_signed, inverse):
    if inverse:
        return x * cos + pltpu.roll(x * sin_signed, LANE // 2, 1)
    return x * cos + pltpu.roll(x, LANE // 2, 1) * sin_signed


def _rope(name, x, cos, sin_signed, inverse):
    S, W = x.shape
    ts = _tile(S, 512)

    def body(x_ref, c_ref, s_ref, o_ref):
        c, s = c_ref[...], s_ref[...]
        for h in range(W // LANE):
            sl = slice(h * LANE, (h + 1) * LANE)
            o_ref[:, sl] = _rope_lanes(x_ref[:, sl], c, s, inverse).astype(o_ref.dtype)

    return pl.pallas_call(
        body, name=name, grid=(S // ts,), in_specs=[_row(ts, W), _row(ts, LANE), _row(ts, LANE)],
        out_specs=_row(ts, W), out_shape=jax.ShapeDtypeStruct((S, W), BF),
        compiler_params=_params("parallel"))(x, cos, sin_signed)


LAT_W = 1024
_CQ = slice(0, Q_RANK)
_CKV = slice(Q_RANK, Q_RANK + KV_RANK)
_KPE = slice(Q_RANK + KV_RANK, Q_RANK + KV_RANK + LANE)


def _mla_prep(lat, qg, kvg, cos, sin_signed):
    S = lat.shape[0]
    ts = _tile(S, 512)

    def body(lat_ref, qg_ref, kvg_ref, c_ref, s_ref, qn_ref, kvn_ref, kpe_ref):
        cq = lat_ref[:, _CQ]
        qn_ref[...] = (cq * _rstd(cq) * qg_ref[...]).astype(BF)
        ckv = lat_ref[:, _CKV]
        kvn_ref[...] = (ckv * _rstd(ckv) * kvg_ref[...]).astype(BF)
        kpe_ref[...] = _rope_lanes(lat_ref[:, _KPE], c_ref[...], s_ref[...], False).astype(BF)

    return pl.pallas_call(
        body, name="mla_prep", grid=(S // ts,),
        in_specs=[_row(ts, LAT_W), _bcast(1, Q_RANK), _bcast(1, KV_RANK), _row(ts, LANE), _row(ts, LANE)],
        out_specs=[_row(ts, Q_RANK), _row(ts, KV_RANK), _row(ts, LANE)],
        out_shape=[jax.ShapeDtypeStruct((S, Q_RANK), BF), jax.ShapeDtypeStruct((S, KV_RANK), BF),
                   jax.ShapeDtypeStruct((S, LANE), BF)],
        compiler_params=_params("parallel"))(lat, qg, kvg, cos, sin_signed)


def _mla_prep_bwd(lat, qg, kvg, cos, sin_signed, d_qn, d_kvn, d_kpe):
    S = lat.shape[0]
    ts = _tile(S, 512)

    def body(lat_ref, qg_ref, kvg_ref, c_ref, s_ref, dqn_ref, dkvn_ref, dkpe_ref, dlat_ref, dqg_ref, dkvg_ref):
        i = pl.program_id(0)
        dcq, dqg = _norm_bwd_rows(dqn_ref[...], lat_ref[:, _CQ], qg_ref[...])
        dckv, dkvg = _norm_bwd_rows(dkvn_ref[...], lat_ref[:, _CKV], kvg_ref[...])
        dlat_ref[:, _CQ] = dcq.astype(BF)
        dlat_ref[:, _CKV] = dckv.astype(BF)
        dkpe = dkpe_ref[0]
        for g in range(1, d_kpe.shape[0]):
            dkpe = dkpe + dkpe_ref[g]
        dlat_ref[:, _KPE] = _rope_lanes(dkpe, c_ref[...], s_ref[...], True).astype(BF)
        dlat_ref[:, _KPE.stop:] = jnp.zeros((ts, LAT_W - _KPE.stop), BF)
        _accumulate(i, dqg_ref, dqg)
        _accumulate(i, dkvg_ref, dkvg)

    return pl.pallas_call(
        body, name="mla_prep_bwd", grid=(S // ts,),
        in_specs=[_row(ts, LAT_W), _bcast(1, Q_RANK), _bcast(1, KV_RANK), _row(ts, LANE), _row(ts, LANE),
                  _row(ts, Q_RANK), _row(ts, KV_RANK), pl.BlockSpec((d_kpe.shape[0], ts, LANE), lambda i: (0, i, 0))],
        out_specs=[_row(ts, LAT_W), _bcast(1, Q_RANK), _bcast(1, KV_RANK)],
        out_shape=[jax.ShapeDtypeStruct((S, LAT_W), BF), jax.ShapeDtypeStruct((1, Q_RANK), F32),
                   jax.ShapeDtypeStruct((1, KV_RANK), F32)],
        compiler_params=_params("arbitrary"))(lat, qg, kvg, cos, sin_signed, d_qn, d_kvn, d_kpe)


def _sigmoid(z):
    return 1.0 / (1.0 + jnp.exp(-z))


def _merge_fwd(gpre, b_gate, o_a, o_b):
    S, D = o_a.shape
    ts = _tile(S, 256)

    def body(g_ref, b_ref, oa_ref, ob_ref, m_ref):
        ga = _sigmoid(g_ref[:, :D] + b_ref[:, :D])
        gb = _sigmoid(g_ref[:, D:] + b_ref[:, D:])
        m_ref[...] = (ga * oa_ref[...] + gb * ob_ref[...]).astype(BF)

    return pl.pallas_call(
        body, name="merge_fwd", grid=(S // ts,),
        in_specs=[_row(ts, 2 * D), _bcast(1, 2 * D), _row(ts, D), _row(ts, D)], out_specs=_row(ts, D),
        out_shape=jax.ShapeDtypeStruct((S, D), BF), compiler_params=_params("parallel"))(gpre, b_gate, o_a, o_b)


def _merge_bwd(d_merge, gpre, b_gate, o_a, o_b):
    S, D = o_a.shape
    ts = _tile(S, 256)

    def body(dm_ref, g_ref, b_ref, oa_ref, ob_ref, doa_ref, dob_ref, dg_ref, db_ref):
        dm = dm_ref[...]
        ga = _sigmoid(g_ref[:, :D] + b_ref[:, :D])
        gb = _sigmoid(g_ref[:, D:] + b_ref[:, D:])
        doa_ref[...] = (dm * ga).astype(BF)
        dob_ref[...] = (dm * gb).astype(BF)
        dga = dm * oa_ref[...] * ga * (1.0 - ga)
        dgb = dm * ob_ref[...] * gb * (1.0 - gb)
        dg_ref[:, :D] = dga.astype(BF)
        dg_ref[:, D:] = dgb.astype(BF)
        i = pl.program_id(0)
        part = jnp.concatenate([jnp.sum(dga, axis=0, keepdims=True), jnp.sum(dgb, axis=0, keepdims=True)], axis=1)
        _accumulate(i, db_ref, part)

    return pl.pallas_call(
        body, name="merge_bwd", grid=(S // ts,),
        in_specs=[_row(ts, D), _row(ts, 2 * D), _bcast(1, 2 * D), _row(ts, D), _row(ts, D)],
        out_specs=[_row(ts, D), _row(ts, D), _row(ts, 2 * D), _bcast(1, 2 * D)],
        out_shape=[jax.ShapeDtypeStruct((S, D), BF), jax.ShapeDtypeStruct((S, D), BF),
                   jax.ShapeDtypeStruct((S, 2 * D), BF), jax.ShapeDtypeStruct((1, 2 * D), F32)],
        compiler_params=_params("arbitrary"))(d_merge, gpre, b_gate, o_a, o_b)


def _final_loss(x2, tgt, gf):
    S, D = x2.shape
    ts = _tile(S, 512)

    def body(x_ref, t_ref, g_ref, dx_ref, dg_ref, loss_ref):
        i = pl.program_id(0)
        xv = x_ref[...]
        g = g_ref[...]
        y = xv * _rstd(xv) * g
        err = y - t_ref[...]
        dx, dg = _norm_bwd_rows(err * (1.0 / D), xv, g)
        dx_ref[...] = dx
        _accumulate(i, dg_ref, dg)
        part = 0.5 * jnp.sum(jnp.mean(err * err, axis=-1, keepdims=True), axis=0, keepdims=True)
        _accumulate(i, loss_ref, jnp.broadcast_to(part, (8, LANE)))

    return pl.pallas_call(
        body, name="final_loss", grid=(S // ts,),
        in_specs=[_row(ts, D), _row(ts, D), _bcast(1, D)],
        out_specs=[_row(ts, D), _bcast(1, D), _bcast(8, LANE)],
        out_shape=[jax.ShapeDtypeStruct((S, D), F32), jax.ShapeDtypeStruct((1, D), F32),
                   jax.ShapeDtypeStruct((8, LANE), F32)],
        compiler_params=_params("arbitrary"))(x2, tgt, gf)


HALO = 16


SUB = 8


def _shift_down(cur, prev, k, rows):
    out = pltpu.roll(cur, k, 0)
    head = out[:SUB]
    for j in range(k):
        head = jnp.where(rows == j, prev[HALO - k + j:HALO - k + j + 1, :], head)
    return jnp.concatenate([head, out[SUB:]], axis=0)


def _shift_up(cur, nxt, k, rows, ts):
    out = pltpu.roll(cur, ts - k, 0)
    tail = out[ts - SUB:]
    for j in range(k):
        tail = jnp.where(rows == SUB - k + j, nxt[j:j + 1, :], tail)
    return jnp.concatenate([out[:ts - SUB], tail], axis=0)


def _conv_rows(cur, prev, w, b, rows):
    return b + w[0:1, :] * _shift_down(cur, prev, 2, rows) + w[1:2, :] * _shift_down(cur, prev, 1, rows) + w[2:3, :] * cur


def _conv_specs(ts, C, shard_of):
    nh = ts // HALO
    cur = pl.BlockSpec((None, ts, C), lambda g, i: (shard_of(g), i, 0))
    prev = pl.BlockSpec((None, HALO, C), lambda g, i: (shard_of(g), jnp.maximum(i * nh - 1, 0), 0))
    return cur, prev


def _ffn_act(u_pre, conv_w, conv_b):
    G4, S, C = u_pre.shape
    G = G4 // 2
    ts = _tile(S, 256)

    def body(up_ref, upp_ref, gt_ref, gtp_ref, wu_ref, wg_ref, bu_ref, bg_ref, act_ref, u_ref):
        first = pl.program_id(1) == 0
        rows = lax.broadcasted_iota(jnp.int32, (SUB, C), 0)
        pu = jnp.where(first, 0.0, upp_ref[...].astype(F32))
        pg = jnp.where(first, 0.0, gtp_ref[...].astype(F32))
        up = _conv_rows(up_ref[...].astype(F32), pu, wu_ref[...], bu_ref[...], rows)
        gate = _conv_rows(gt_ref[...].astype(F32), pg, wg_ref[...], bg_ref[...], rows)
        act_ref[...] = (gate * _sigmoid(gate) * up).astype(BF)
        u_ref[0] = up.astype(BF)
        u_ref[1] = gate.astype(BF)

    cur_u, prev_u = _conv_specs(ts, C, lambda g: g)
    cur_g, prev_g = _conv_specs(ts, C, lambda g: g + G)
    w_u = pl.BlockSpec((None, 3, C), lambda g, i: (g, 0, 0))
    w_g = pl.BlockSpec((None, 3, C), lambda g, i: (g + G, 0, 0))
    b_u = pl.BlockSpec((None, 1, C), lambda g, i: (g, 0, 0))
    b_g = pl.BlockSpec((None, 1, C), lambda g, i: (g + G, 0, 0))
    pair = pl.BlockSpec((2, None, ts, C), lambda g, i: (0, g, i, 0))
    act, u = pl.pallas_call(
        body, name="ffn_act", grid=(G, S // ts),
        in_specs=[cur_u, prev_u, cur_g, prev_g, w_u, w_g, b_u, b_g],
        out_specs=[pl.BlockSpec((None, ts, C), lambda g, i: (g, i, 0)), pair],
        out_shape=[jax.ShapeDtypeStruct((G, S, C), BF), jax.ShapeDtypeStruct((2, G, S, C), BF)],
        compiler_params=_params("parallel", "parallel"))(u_pre, u_pre, u_pre, u_pre, conv_w, conv_w, conv_b, conv_b)
    return act, u


def _ffn_act_bwd(u, d_act):
    _, G, S, C = u.shape
    ts = _tile(S, 256)

    def body(u_ref, da_ref, du_ref):
        up, gate = u_ref[0].astype(F32), u_ref[1].astype(F32)
        sg = _sigmoid(gate)
        da = da_ref[...].astype(F32)
        du_ref[0] = (da * (gate * sg)).astype(BF)
        du_ref[1] = (da * up * (sg * (1.0 + gate * (1.0 - sg)))).astype(BF)

    pair = pl.BlockSpec((2, None, ts, C), lambda g, i: (0, g, i, 0))
    du = pl.pallas_call(
        body, name="ffn_act_bwd", grid=(G, S // ts),
        in_specs=[pair, pl.BlockSpec((None, ts, C), lambda g, i: (g, i, 0))], out_specs=pair,
        out_shape=jax.ShapeDtypeStruct((2, G, S, C), BF),
        compiler_params=_params("parallel", "parallel"))(u, d_act)
    return du.reshape(2 * G, S, C)


def _conv_bwd(du, u_pre, conv_w):
    G4, S, C = du.shape
    ts = _tile(S, 256)
    nh = ts // HALO
    last_halo = S // HALO - 1

    def body(du_ref, dun_ref, u_ref, w_ref, dpre_ref, dw_ref, db_ref):
        i = pl.program_id(1)
        rows = lax.broadcasted_iota(jnp.int32, (SUB, C), 0)
        du_c = du_ref[...].astype(F32)
        nxt = jnp.where(i == pl.num_programs(1) - 1, 0.0, dun_ref[...].astype(F32))
        up1, up2 = _shift_up(du_c, nxt, 1, rows, ts), _shift_up(du_c, nxt, 2, rows, ts)
        w = w_ref[...]
        dpre_ref[...] = (w[2:3, :] * du_c + w[1:2, :] * up1 + w[0:1, :] * up2).astype(BF)
        u_c = u_ref[...].astype(F32)
        dw = jnp.concatenate([
            jnp.sum(up2 * u_c, axis=0, keepdims=True),
            jnp.sum(up1 * u_c, axis=0, keepdims=True),
            jnp.sum(du_c * u_c, axis=0, keepdims=True)], axis=0)
        _accumulate(i, dw_ref, dw)
        _accumulate(i, db_ref, jnp.sum(du_c, axis=0, keepdims=True))

    cur = pl.BlockSpec((None, ts, C), lambda g, i: (g, i, 0))
    nxt = pl.BlockSpec((None, HALO, C), lambda g, i: (g, jnp.minimum((i + 1) * nh, last_halo), 0))
    return pl.pallas_call(
        body, name="conv_bwd", grid=(G4, S // ts),
        in_specs=[cur, nxt, cur, pl.BlockSpec((None, 3, C), lambda g, i: (g, 0, 0))],
        out_specs=[cur, pl.BlockSpec((None, 3, C), lambda g, i: (g, 0, 0)), pl.BlockSpec((None, 1, C), lambda g, i: (g, 0, 0))],
        out_shape=[jax.ShapeDtypeStruct((G4, S, C), BF), jax.ShapeDtypeStruct((G4, 3, C), F32),
                   jax.ShapeDtypeStruct((G4, 1, C), F32)],
        compiler_params=_params("parallel", "arbitrary"))(du, du, u_pre, conv_w)


MLA_T = 1024
MLA_HB = 4
MLA_BWD_HB = 2


def _mla_pairs(n, by_row):
    if by_row:
        pairs = [(i, j) for i in range(n) for j in range(i + 1)]
    else:
        pairs = [(i, j) for j in range(n) for i in range(j, n)]
    return jnp.asarray([p[0] for p in pairs], jnp.int32), jnp.asarray([p[1] for p in pairs], jnp.int32)


def _mla_specs(hb):
    q = pl.BlockSpec((MLA_T, hb * HEAD), lambda g, t, it, jt: (it[t], g))
    k = pl.BlockSpec((MLA_T, hb * HEAD), lambda g, t, it, jt: (jt[t], g))
    kpe = pl.BlockSpec((MLA_T, HEAD), lambda g, t, it, jt: (jt[t], 0))
    lse = pl.BlockSpec((hb, MLA_T, LANE), lambda g, t, it, jt: (g, it[t], 0))
    return q, k, kpe, lse


def _mla_head(ref, hh):
    return ref[:, hh * HEAD:(hh + 1) * HEAD]


LOG2E = math.log2(math.e)
MLA_EXP2_SCALE = MLA_SCALE * LOG2E


def _mla_scores(qn_ref, qp_ref, kn_ref, kpe, hh, ok):
    q = jnp.concatenate([_mla_head(qn_ref, hh), _mla_head(qp_ref, hh)], axis=1)
    k = jnp.concatenate([_mla_head(kn_ref, hh), kpe], axis=1)
    s = lax.dot_general(q, k, NT, preferred_element_type=F32)
    return q, k, s if ok is None else jnp.where(ok, s, -jnp.inf)


def _mla_diagonal_mask():
    row = lax.broadcasted_iota(jnp.int32, (MLA_T, MLA_T), 0)
    col = lax.broadcasted_iota(jnp.int32, (MLA_T, MLA_T), 1)
    return col <= row


def _mla_step(i, j, step):
    @pl.when(j < i)
    def _():
        step(None)

    @pl.when(j == i)
    def _():
        step(_mla_diagonal_mask())


def _mla_fwd(qn, qp, kn, kpe, v):
    S = qn.shape[0]
    it, jt = _mla_pairs(S // MLA_T, True)

    def body(it_ref, jt_ref, qn_ref, qp_ref, kn_ref, kpe_ref, v_ref, o_ref, lse_ref, m_scr, acc_scr):
        t = pl.program_id(1)
        i, j = it_ref[t], jt_ref[t]

        @pl.when(j == 0)
        def _():
            m_scr[...] = jnp.full(m_scr.shape, -jnp.inf, F32)
            acc_scr[...] = jnp.zeros(acc_scr.shape, F32)

        def step(ok):
            kpe_v = kpe_ref[...]
            ones = jnp.ones((MLA_T, HEAD), BF)
            state = [(m_scr[hh], acc_scr[hh]) for hh in range(MLA_HB)]
            new = []
            for hh in range(MLA_HB):
                m_prev, acc = state[hh]
                _, _, s = _mla_scores(qn_ref, qp_ref, kn_ref, kpe_v, hh, ok)
                m_new = jnp.maximum(m_prev, jnp.max(s, axis=1, keepdims=True))
                p = jnp.exp2((s - m_new) * MLA_EXP2_SCALE).astype(BF)
                v1 = jnp.concatenate([_mla_head(v_ref, hh), ones], axis=1)
                alpha = jnp.exp2((m_prev - m_new) * MLA_EXP2_SCALE)
                new.append((m_new, alpha * acc + lax.dot_general(p, v1, NN, preferred_element_type=F32)))
            for hh in range(MLA_HB):
                m_scr[hh], acc_scr[hh] = new[hh]

        _mla_step(i, j, step)

        @pl.when(j == i)
        def _():
            for hh in range(MLA_HB):
                l = acc_scr[hh, :, HEAD:]
                o_ref[:, hh * HEAD:(hh + 1) * HEAD] = (acc_scr[hh, :, :HEAD] / l).astype(BF)
                lse_ref[hh] = m_scr[hh] * MLA_SCALE + jnp.log(l)

    qspec, kspec, kpespec, lsespec = _mla_specs(MLA_HB)
    grid_spec = pltpu.PrefetchScalarGridSpec(
        num_scalar_prefetch=2, grid=(MLA_HEADS // MLA_HB, it.shape[0]),
        in_specs=[qspec, qspec, kspec, kpespec, kspec], out_specs=[qspec, lsespec],
        scratch_shapes=[pltpu.VMEM((MLA_HB, MLA_T, 1), F32), pltpu.VMEM((MLA_HB, MLA_T, 2 * HEAD), F32)])
    return pl.pallas_call(
        body, name="mla_fwd", grid_spec=grid_spec,
        out_shape=[jax.ShapeDtypeStruct((S, MLA_HEADS * HEAD), BF), jax.ShapeDtypeStruct((MLA_HEADS, S, LANE), F32)],
        compiler_params=_params("parallel", "arbitrary"))(it, jt, qn, qp, kn, kpe, v)


def _mla_p_ds(qn_ref, qp_ref, kn_ref, kpe, v_ref, do_ref, o_ref, lse_ref, hh, ok):
    q, k, s = _mla_scores(qn_ref, qp_ref, kn_ref, kpe, hh, ok)
    p = jnp.exp2(s * MLA_EXP2_SCALE - lse_ref[hh][:, 0:1] * LOG2E)
    do = _mla_head(do_ref, hh)
    delta = jnp.sum(do.astype(F32) * _mla_head(o_ref, hh).astype(F32), axis=1, keepdims=True)
    dp = lax.dot_general(do, _mla_head(v_ref, hh), NT, preferred_element_type=F32)
    ds = p * (dp - delta) * MLA_SCALE
    return q, k, p, ds, do


def _mla_bwd(qn, qp, kn, kpe, v, do, o, lse):
    S = qn.shape[0]
    nq = S // MLA_T
    hb = MLA_BWD_HB
    it, jt = _mla_pairs(nq, False)

    def body(it_ref, jt_ref, qn_ref, qp_ref, kn_ref, kpe_ref, v_ref, do_ref, o_ref, lse_ref,
             dqn_ref, dqp_ref, dkn_ref, dv_ref, dkpe_ref, dq_acc, dk_acc, dv_acc, stage_n, stage_p, osem):
        t = pl.program_id(1)
        i, j = it_ref[t], jt_ref[t]

        @pl.when(t == 0)
        def _():
            dq_acc[...] = jnp.zeros(dq_acc.shape, F32)

        @pl.when(i == j)
        def _():
            dk_acc[...] = jnp.zeros(dk_acc.shape, F32)
            dv_acc[...] = jnp.zeros(dv_acc.shape, F32)

        def step(ok):
            kpe_v = kpe_ref[...]
            for hh in range(hb):
                q, k, p, ds, do_h = _mla_p_ds(qn_ref, qp_ref, kn_ref, kpe_v, v_ref, do_ref, o_ref, lse_ref, hh, ok)
                ds = ds.astype(BF)
                dv_acc[hh] += lax.dot_general(p.astype(BF), do_h, TN, preferred_element_type=F32)
                dk_acc[hh] += lax.dot_general(ds, q, TN, preferred_element_type=F32)
                dq_acc[i, hh] += lax.dot_general(ds, k, NN, preferred_element_type=F32)

        _mla_step(i, j, step)

        @pl.when(i == j)
        def _():
            for hh in range(hb):
                stage_n[:, hh * HEAD:(hh + 1) * HEAD] = dq_acc[i, hh, :, :HEAD].astype(BF)
                stage_p[:, hh * HEAD:(hh + 1) * HEAD] = dq_acc[i, hh, :, HEAD:]
            rows = pl.ds(pl.multiple_of(i * MLA_T, MLA_T), MLA_T)
            cols = pl.ds(pl.multiple_of(pl.program_id(0) * hb * HEAD, LANE), hb * HEAD)
            out_n = pltpu.make_async_copy(stage_n, dqn_ref.at[rows, cols], osem.at[0])
            out_p = pltpu.make_async_copy(stage_p, dqp_ref.at[rows, cols], osem.at[1])
            out_n.start()
            out_p.start()
            out_n.wait()
            out_p.wait()

        @pl.when(i == nq - 1)
        def _():
            dkpe = dk_acc[0, :, HEAD:]
            for hh in range(hb):
                dkn_ref[:, hh * HEAD:(hh + 1) * HEAD] = dk_acc[hh, :, :HEAD].astype(BF)
                dv_ref[:, hh * HEAD:(hh + 1) * HEAD] = dv_acc[hh].astype(BF)
                if hh:
                    dkpe = dkpe + dk_acc[hh, :, HEAD:]
            dkpe_ref[...] = dkpe

    qspec, kspec, kpespec, lsespec = _mla_specs(hb)
    dkpespec = pl.BlockSpec((None, MLA_T, HEAD), lambda g, t, it, jt: (g, jt[t], 0))
    grid_spec = pltpu.PrefetchScalarGridSpec(
        num_scalar_prefetch=2, grid=(MLA_HEADS // hb, it.shape[0]),
        in_specs=[qspec, qspec, kspec, kpespec, kspec, qspec, qspec, lsespec],
        out_specs=[ANY, ANY, kspec, kspec, dkpespec],
        scratch_shapes=[pltpu.VMEM((nq, hb, MLA_T, 2 * HEAD), F32), pltpu.VMEM((hb, MLA_T, 2 * HEAD), F32),
                        pltpu.VMEM((hb, MLA_T, HEAD), F32), pltpu.VMEM((MLA_T, hb * HEAD), BF),
                        pltpu.VMEM((MLA_T, hb * HEAD), F32), pltpu.SemaphoreType.DMA((2,))])
    wide = jax.ShapeDtypeStruct((S, MLA_HEADS * HEAD), BF)
    return pl.pallas_call(
        body, name="mla_bwd", grid_spec=grid_spec,
        out_shape=[wide, jax.ShapeDtypeStruct((S, MLA_HEADS * HEAD), F32), wide, wide,
                   jax.ShapeDtypeStruct((MLA_HEADS // hb, S, HEAD), F32)],
        compiler_params=_params("parallel", "arbitrary"))(it, jt, qn, qp, kn, kpe, v, do, o, lse)


DIL_W = 3 * DIL_HPG * HEAD
DIL_O = DIL_HPG * HEAD
DIL_STEP_BLOCKS = 4


def _dil_slopes(g):
    return [2.0 ** (-ALIBI_MAX_BIAS * (g * DIL_HPG + hh + 1) / DIL_HEADS) for hh in range(DIL_HPG)]


def _dil_bias(dil):
    p = lax.broadcasted_iota(jnp.int32, (DIL_BLOCK, DIL_BLOCK), 0)
    kk = lax.broadcasted_iota(jnp.int32, (DIL_BLOCK, DIL_BLOCK), 1)
    jc = p - kk
    dist_c = (dil * jc).astype(F32)
    dist_p = (dil * (jc + DIL_BLOCK)).astype(F32)
    return jc >= 0, jc <= 0, dist_c, dist_p


def _dil_bias2(dil):
    p = lax.broadcasted_iota(jnp.int32, (DIL_BLOCK, 2 * DIL_BLOCK), 0)
    kk = lax.broadcasted_iota(jnp.int32, (DIL_BLOCK, 2 * DIL_BLOCK), 1)
    j = p + DIL_BLOCK - kk
    return (j >= 0) & (j <= DIL_BLOCK), kk < DIL_BLOCK, (dil * j).astype(F32)


def _dil_head(blk, hh):
    q = blk[:, hh * HEAD:(hh + 1) * HEAD]
    k = blk[:, DIL_O + hh * HEAD:DIL_O + (hh + 1) * HEAD]
    v = blk[:, 2 * DIL_O + hh * HEAD:2 * DIL_O + (hh + 1) * HEAD]
    return q, k, v


def _dil_s(q, k, slope, dist, ok):
    s = lax.dot_general(q, k, NT, preferred_element_type=F32) * DIL_SCALE - slope * dist
    return jnp.where(ok, s, -jnp.inf)


def _dil_view(a, dil):
    S, W = a.shape
    return a.reshape(S // dil, dil * W)


def _dil_fwd(qkv, g):
    _, dil = DIL_PATTERNS[g]
    S = qkv.shape[0]
    L = S // dil
    nb = L // DIL_BLOCK
    slopes = _dil_slopes(g)

    bb = min(DIL_STEP_BLOCKS, nb)
    rows = bb * DIL_BLOCK

    def body(cur_ref, prev_ref, o_ref, lse_ref):
        n = pl.program_id(1)
        ok, in_prev, dist = _dil_bias2(dil)
        for b in range(bb):
            if b == 0:
                both = jnp.concatenate([prev_ref[...], cur_ref[0:DIL_BLOCK, :]], axis=0)
                ok_b = ok & (~in_prev | (n > 0))
            else:
                both = cur_ref[(b - 1) * DIL_BLOCK:(b + 1) * DIL_BLOCK, :]
                ok_b = ok
            for hh in range(DIL_HPG):
                q, _, _ = _dil_head(both[DIL_BLOCK:], hh)
                _, k2, v2 = _dil_head(both, hh)
                s = _dil_s(q, k2, slopes[hh], dist, ok_b)
                m = jnp.max(s, axis=1, keepdims=True)
                p = jnp.exp(s - m)
                l = jnp.sum(p, axis=1, keepdims=True)
                o = lax.dot_general(p.astype(BF), v2, NN, preferred_element_type=F32) / l
                rs, sl = slice(b * DIL_BLOCK, (b + 1) * DIL_BLOCK), slice(hh * HEAD, (hh + 1) * HEAD)
                o_ref[rs, sl] = o
                lse_ref[rs, sl] = jnp.broadcast_to(m + jnp.log(l), (DIL_BLOCK, HEAD))

    ospec = pl.BlockSpec((rows, DIL_O), lambda r, n: (n, r))
    o, lse = pl.pallas_call(
        body, name=f"dil_fwd{g}", grid=(dil, nb // bb),
        in_specs=[pl.BlockSpec((rows, DIL_W), lambda r, n: (n, r)),
                  pl.BlockSpec((DIL_BLOCK, DIL_W), lambda r, n: (jnp.maximum(n * bb - 1, 0), r))],
        out_specs=[ospec, ospec],
        out_shape=[jax.ShapeDtypeStruct((L, dil * DIL_O), F32), jax.ShapeDtypeStruct((L, dil * DIL_O), F32)],
        compiler_params=_params("parallel", "parallel"))(_dil_view(qkv, dil), _dil_view(qkv, dil))
    return o.reshape(S, DIL_O), lse.reshape(S, DIL_O)


def _dil_combine(os_, lses):
    S = os_[0].shape[0]
    ts = _tile(S, 512)

    def body(o0, o1, o2, l0, l1, l2, out_ref, lse_ref):
        a, b, c = l0[...], l1[...], l2[...]
        m = jnp.maximum(jnp.maximum(a, b), c)
        ea, eb, ec = jnp.exp(a - m), jnp.exp(b - m), jnp.exp(c - m)
        tot = ea + eb + ec
        out_ref[...] = ((ea * o0[...] + eb * o1[...] + ec * o2[...]) / tot).astype(BF)
        lse_ref[...] = m + jnp.log(tot)

    return pl.pallas_call(
        body, name="dil_combine", grid=(S // ts,), in_specs=[_row(ts, DIL_O)] * 6,
        out_specs=[_row(ts, DIL_O), _row(ts, DIL_O)],
        out_shape=[jax.ShapeDtypeStruct((S, DIL_O), BF), jax.ShapeDtypeStruct((S, DIL_O), F32)],
        compiler_params=_params("parallel"))(*os_, *lses)


def _dil_delta(do, out):
    S = do.shape[0]
    ts = _tile(S, 512)

    def body(do_ref, o_ref, d_ref):
        for hh in range(DIL_HPG):
            sl = slice(hh * HEAD, (hh + 1) * HEAD)
            d = jnp.sum(do_ref[:, sl].astype(F32) * o_ref[:, sl].astype(F32), axis=1, keepdims=True)
            d_ref[:, sl] = jnp.broadcast_to(d, (ts, HEAD))

    return pl.pallas_call(
        body, name="dil_delta", grid=(S // ts,), in_specs=[_row(ts, DIL_O)] * 2, out_specs=_row(ts, DIL_O),
        out_shape=jax.ShapeDtypeStruct((S, DIL_O), F32), compiler_params=_params("parallel"))(do, out)


def _dil_bwd(qkv, do, lse, delta, g):
    _, dil = DIL_PATTERNS[g]
    S = qkv.shape[0]
    L = S // dil
    nb = L // DIL_BLOCK
    slopes = _dil_slopes(g)

    def pair(q, k, v, do_h, lse_h, delta_h, slope, dist, ok):
        s = _dil_s(q, k, slope, dist, ok)
        p = jnp.exp(s - lse_h)
        dp = lax.dot_general(do_h, v, NT, preferred_element_type=F32)
        ds = (p * (dp - delta_h) * DIL_SCALE).astype(BF)
        return p.astype(BF), ds

    bb = min(DIL_STEP_BLOCKS, nb)
    rows = bb * DIL_BLOCK
    steps = nb // bb

    def body(cur_ref, prev_ref, next_ref, doc_ref, don_ref, lsec_ref, lsen_ref, dlc_ref, dln_ref, out_ref):
        n = pl.program_id(1)
        ok2, in_prev, dist2 = _dil_bias2(dil)
        _, ok_p0, _, dist_p = _dil_bias(dil)
        for b in range(bb):
            rs = slice(b * DIL_BLOCK, (b + 1) * DIL_BLOCK)
            rn = slice((b + 1) * DIL_BLOCK, (b + 2) * DIL_BLOCK)
            two = slice(b * DIL_BLOCK, (b + 2) * DIL_BLOCK)
            first, last = b == 0, b == bb - 1
            if first:
                keys = jnp.concatenate([prev_ref[...], cur_ref[rs, :]], axis=0)
                ok_ab = ok2 & (~in_prev | (n > 0))
            else:
                keys = cur_ref[(b - 1) * DIL_BLOCK:(b + 1) * DIL_BLOCK, :]
                ok_ab = ok2
            qrows = jnp.concatenate([cur_ref[rs, :], next_ref[...]], axis=0) if last else cur_ref[two, :]
            ok_n = ok_p0 & (n < steps - 1) if last else ok_p0
            for hh in range(DIL_HPG):
                sl = slice(hh * HEAD, (hh + 1) * HEAD)
                q2, _, _ = _dil_head(qrows, hh)
                _, k2, v2 = _dil_head(keys, hh)
                q, qn, kc, vc = q2[:DIL_BLOCK], q2[DIL_BLOCK:], k2[DIL_BLOCK:], v2[DIL_BLOCK:]
                do2 = jnp.concatenate([doc_ref[rs, sl], don_ref[:, sl]], axis=0) if last else doc_ref[two, sl]
                do_c, do_n = do2[:DIL_BLOCK], do2[DIL_BLOCK:]
                lse_c = lsec_ref[rs, sl][:, 0:1]
                lse_n = (lsen_ref[:, sl] if last else lsec_ref[rn, sl])[:, 0:1]
                dl_c = dlc_ref[rs, sl][:, 0:1]
                dl_n = (dln_ref[:, sl] if last else dlc_ref[rn, sl])[:, 0:1]
                p_ab, ds_ab = pair(q, k2, v2, do_c, lse_c, dl_c, slopes[hh], dist2, ok_ab)
                p_n, ds_n = pair(qn, kc, vc, do_n, lse_n, dl_n, slopes[hh], dist_p, ok_n)
                dq = lax.dot_general(ds_ab, k2, NN, preferred_element_type=F32)
                dk = lax.dot_general(jnp.concatenate([ds_ab[:, DIL_BLOCK:], ds_n], axis=0), q2, TN, preferred_element_type=F32)
                dv = lax.dot_general(jnp.concatenate([p_ab[:, DIL_BLOCK:], p_n], axis=0), do2, TN, preferred_element_type=F32)
                out_ref[rs, sl] = dq.astype(BF)
                out_ref[rs, DIL_O + hh * HEAD:DIL_O + (hh + 1) * HEAD] = dk.astype(BF)
                out_ref[rs, 2 * DIL_O + hh * HEAD:2 * DIL_O + (hh + 1) * HEAD] = dv.astype(BF)

    cur_w = pl.BlockSpec((rows, DIL_W), lambda r, n: (n, r))
    prev_w = pl.BlockSpec((DIL_BLOCK, DIL_W), lambda r, n: (jnp.maximum(n * bb - 1, 0), r))
    next_w = pl.BlockSpec((DIL_BLOCK, DIL_W), lambda r, n: (jnp.minimum((n + 1) * bb, nb - 1), r))
    cur_o = pl.BlockSpec((rows, DIL_O), lambda r, n: (n, r))
    next_o = pl.BlockSpec((DIL_BLOCK, DIL_O), lambda r, n: (jnp.minimum((n + 1) * bb, nb - 1), r))
    qv, dov, lsev, dlv = _dil_view(qkv, dil), _dil_view(do, dil), _dil_view(lse, dil), _dil_view(delta, dil)
    out = pl.pallas_call(
        body, name=f"dil_bwd{g}", grid=(dil, steps),
        in_specs=[cur_w, prev_w, next_w, cur_o, next_o, cur_o, next_o, cur_o, next_o],
        out_specs=cur_w, out_shape=jax.ShapeDtypeStruct((L, dil * DIL_W), BF),
        compiler_params=_params("parallel", "parallel"))(qv, qv, qv, dov, dov, lsev, lsev, dlv, dlv)
    return out.reshape(S, DIL_W)


def _adamw(name, w, g, m, v):
    R, C = w.shape
    tr, tc = _adamw_block(R, C)

    def body(w_ref, g_ref, m_ref, v_ref, go_ref, d_ref, nm_ref, nv_ref):
        gv = g_ref[...]
        go_ref[...] = gv
        nm = ADAM_B1 * m_ref[...] + (1.0 - ADAM_B1) * gv
        nv = ADAM_B2 * v_ref[...] + (1.0 - ADAM_B2) * (gv * gv)
        m_hat = nm / (1.0 - ADAM_B1 ** ADAM_STEP)
        v_hat = nv / (1.0 - ADAM_B2 ** ADAM_STEP)
        d_ref[...] = -ADAM_LR * (m_hat / (jnp.sqrt(v_hat) + ADAM_EPS) + ADAM_WD * w_ref[...])
        nm_ref[...] = nm
        nv_ref[...] = nv

    spec = pl.BlockSpec((tr, tc), lambda i, j: (i, j))
    shp = jax.ShapeDtypeStruct((R, C), F32)
    return pl.pallas_call(
        body, name=name, grid=(R // tr, C // tc), in_specs=[spec] * 4, out_specs=[spec] * 4, out_shape=[shp] * 4,
        compiler_params=_params("parallel", "parallel"))(w, g, m, v)


ADAMW_BLOCK_ELEMS = 640 * 1024


def _adamw_block(R, C):
    if R * C <= ADAMW_BLOCK_ELEMS:
        return R, C
    tr = _tile_rows(R, max(8, ADAMW_BLOCK_ELEMS // C))
    tc = _tile(C, max(LANE, ADAMW_BLOCK_ELEMS // R))
    if tr * C >= R * tc or R * tc > ADAMW_BLOCK_ELEMS:
        return tr, C
    return R, tc


def _tile_rows(n, pref, mult=8):
    t = (pref // mult) * mult
    while t >= mult:
        if n % t == 0:
            return t
        t -= mult
    return n


ANY = pl.BlockSpec(memory_space=pl.ANY)


def _place():
    x, y, c = lax.axis_index("x"), lax.axis_index("y"), lax.axis_index("c")
    chips = [(1 - x, y), (x, 1 - y), (1 - x, 1 - y)]
    chip_idx = [2 * cx + cy for cx, cy in chips]
    return x, y, c, 2 * x + y, chips, chip_idx


def _rcopy(src, dst, ssem, rsem, dev):
    return pltpu.make_async_remote_copy(src_ref=src, dst_ref=dst, send_sem=ssem, recv_sem=rsem,
                                        device_id=dev, device_id_type=MESH)


HBM = pl.BlockSpec(memory_space=pltpu.HBM)
SEM = pl.BlockSpec(memory_space=pltpu.SEMAPHORE)
EFFECT = pltpu.SideEffectType.DATAFLOW_SIDE_EFFECTING


def _split_copies(kind, srcs, lands, ssem, rsem):
    _, _, c, me, chips, chip_idx = _place()
    cps = []
    for i in range(len(srcs)):
        for k in range(3):
            if kind == "gather":
                rows = srcs[i].shape[0]
                if rows == lands[i].shape[1]:
                    src, dst = srcs[i], lands[i].at[me]
                else:
                    src, dst = srcs[i], lands[i].at[me, pl.ds(pl.multiple_of(c * rows, 16), rows)]
            else:
                src, dst = srcs[i].at[chip_idx[k]], lands[i].at[k]
            cps.append(_rcopy(src, dst, ssem.at[3 * i + k], rsem.at[3 * i + k], (*chips[k], c)))
    return cps


def _exchange_start(name, kind, srcs, lands, groups):
    n, ng = len(srcs), len(groups)

    def body(*refs):
        src_refs, land_refs = refs[:n], refs[n:2 * n]
        sems = refs[2 * n:2 * n + 2 * ng]
        token = refs[-1]
        for gi, grp in enumerate(groups):
            cps = _split_copies(kind, [src_refs[i] for i in grp], [land_refs[i] for i in grp], sems[2 * gi], sems[2 * gi + 1])
            for cp in cps:
                cp.start()
        token[...] = jnp.zeros_like(token)

    arrays = list(srcs) + list(lands)
    out_shape = []
    for grp in groups:
        out_shape += [pltpu.SemaphoreType.DMA((3 * len(grp),)), pltpu.SemaphoreType.DMA((3 * len(grp),))]
    out_shape += [pltpu.HBM(a.shape, a.dtype) for a in arrays] + [jax.ShapeDtypeStruct((8, LANE), F32)]
    outs = pl.pallas_call(
        body, name=name, out_shape=out_shape, in_specs=[HBM] * (2 * n),
        out_specs=[SEM] * (2 * ng) + [HBM] * (2 * n) + [pl.BlockSpec(memory_space=pltpu.VMEM)],
        input_output_aliases={i: 2 * ng + i for i in range(2 * n)},
        compiler_params=pltpu.CompilerParams(has_side_effects=EFFECT),
    )(*[pltpu.with_memory_space_constraint(a, pltpu.HBM) for a in arrays])
    sems = [(outs[2 * gi], outs[2 * gi + 1]) for gi in range(ng)]
    thru = outs[2 * ng:2 * ng + 2 * n]
    return sems, thru[:n], thru[n:], outs[-1]


def _exchange_wait(name, kind, srcs, lands, sems, after):
    n = len(srcs)

    def body(*refs):
        cps = _split_copies(kind, refs[:n], refs[n:2 * n], refs[2 * n], refs[2 * n + 1])
        for cp in cps:
            cp.wait_send()
            cp.wait_recv()

    arrays = list(srcs) + list(lands)
    outs = pl.pallas_call(
        body, name=name, out_shape=[pltpu.HBM(a.shape, a.dtype) for a in arrays],
        in_specs=[HBM] * (2 * n) + [SEM, SEM, ANY], out_specs=[HBM] * (2 * n),
        input_output_aliases={i: i for i in range(2 * n)},
        compiler_params=pltpu.CompilerParams(has_side_effects=EFFECT),
    )(*arrays, sems[0], sems[1], after)
    return outs[:n], outs[n:]


EXCHANGE_CHUNK_BYTES = 3 * 1024 * 1024


def _half_geometry(R, C, axis):
    Rp, Cp = (R // 2, C) if axis == 0 else (R, C // 2)
    tr = _tile_rows(Rp, max(16, EXCHANGE_CHUNK_BYTES // (2 * Cp)), 16)
    return Rp, Cp, tr, Rp // tr


def _pair_sum(name, g, axis):
    G, R, C = g.shape
    Rp, Cp, tr, nb = _half_geometry(R, C, axis)
    steps = G * nb

    def half_block(s, b, h):
        return (s, h * nb + b, 0) if axis == 0 else (s, b, h)

    def body(c_ref, keep_ref, give_ref, out_ref, land, ssem, rsem, credit):
        x, y, c = lax.axis_index("x"), lax.axis_index("y"), lax.axis_index("c")
        sib = (x, y, 1 - c)
        t = pl.program_id(0) * nb + pl.program_id(1)
        slot = t % 2

        @pl.when(t >= 2)
        def _():
            pl.semaphore_wait(credit, 1)

        cp = _rcopy(give_ref.at[0], land.at[slot], ssem.at[slot], rsem.at[slot], sib)
        cp.start()
        cp.wait_recv()
        out_ref[...] = (keep_ref[...].astype(F32) + land[slot].astype(F32)).astype(BF)

        @pl.when(t + 2 < steps)
        def _():
            pl.semaphore_signal(credit, 1, device_id=sib, device_id_type=MESH)

        cp.wait_send()

    blk = (None, tr, Cp)
    grid_spec = pltpu.PrefetchScalarGridSpec(
        num_scalar_prefetch=1, grid=(G, nb),
        in_specs=[pl.BlockSpec(blk, lambda s, b, c_ref: half_block(s, b, c_ref[0])),
                  pl.BlockSpec((1, tr, Cp), lambda s, b, c_ref: half_block(s, b, 1 - c_ref[0]))],
        out_specs=pl.BlockSpec(blk, lambda s, b, c_ref: (s, b, 0)),
        scratch_shapes=[pltpu.VMEM((2, tr, Cp), BF), pltpu.SemaphoreType.DMA((2,)), pltpu.SemaphoreType.DMA((2,)),
                        pltpu.SemaphoreType.REGULAR])
    c_arr = lax.axis_index("c").astype(jnp.int32).reshape(1)
    return pl.pallas_call(
        body, name=name, grid_spec=grid_spec, out_shape=jax.ShapeDtypeStruct((G, Rp, Cp), BF),
        compiler_params=_params("arbitrary", "arbitrary"))(c_arr, g, g)


def _chip_total_join(name, h, landed, axis):
    G, Rp, Cp = h.shape
    R, C = (2 * Rp, Cp) if axis == 0 else (Rp, 2 * Cp)
    tr = _tile_rows(Rp, max(16, EXCHANGE_CHUNK_BYTES // (4 * Cp)), 16)
    nb = Rp // tr

    def body(me_ref, own_ref, l0_ref, l1_ref, l2_ref, full, stage, ssem, rsem, lsem):
        x, y, c = lax.axis_index("x"), lax.axis_index("y"), lax.axis_index("c")
        sib = (x, y, 1 - c)
        b = pl.program_id(0)

        def place(half, r0, rows):
            if axis == 0:
                return full.at[pl.ds(pl.multiple_of(half * Rp + r0, 8), rows), :]
            return full.at[pl.ds(pl.multiple_of(r0, 8), rows), pl.ds(pl.multiple_of(half * Cp, LANE), Cp)]

        def copies(step):
            s = step % 2
            mine = place(c, step * tr, tr)
            return pltpu.make_async_copy(stage.at[s], mine, lsem.at[s]), _rcopy(stage.at[s], mine, ssem.at[s], rsem, sib)

        @pl.when(b >= 2)
        def _():
            loc, rem = copies(b - 2)
            loc.wait()
            rem.wait_send()

        acc = own_ref[...].astype(F32)
        for r in (l0_ref, l1_ref, l2_ref):
            acc = acc + r[...].astype(F32)
        stage[b % 2] = acc
        loc, rem = copies(b)
        loc.start()
        rem.start()

        @pl.when(b == nb - 1)
        def _():
            for step in range(max(0, nb - 2), nb):
                loc, rem = copies(step)
                loc.wait()
                rem.wait_send()
            theirs = place(1 - c, 0, Rp)
            _rcopy(theirs, theirs, ssem.at[0], rsem, sib).wait_recv()

    blk = (None, tr, Cp)
    grid_spec = pltpu.PrefetchScalarGridSpec(
        num_scalar_prefetch=1, grid=(nb,),
        in_specs=[pl.BlockSpec(blk, lambda b, me_ref: (me_ref[0], b, 0))]
        + [pl.BlockSpec(blk, functools.partial(lambda b, me_ref, k: (k, b, 0), k=k)) for k in range(3)],
        out_specs=ANY,
        scratch_shapes=[pltpu.VMEM((2, tr, Cp), F32), pltpu.SemaphoreType.DMA((2,)), pltpu.SemaphoreType.DMA,
                        pltpu.SemaphoreType.DMA((2,))])
    me = (2 * lax.axis_index("x") + lax.axis_index("y")).astype(jnp.int32).reshape(1)
    return pl.pallas_call(
        body, name=name, grid_spec=grid_spec, out_shape=jax.ShapeDtypeStruct((R, C), F32),
        compiler_params=_params("arbitrary"))(me, h, landed, landed, landed)


def _pair_share(name, land):
    G, R, C = land.shape
    Rh = R // 2
    tr = _tile_rows(Rh, max(16, EXCHANGE_CHUNK_BYTES // (2 * C)), 16)
    chunks = [(k, b) for k in range(3) for b in range(Rh // tr)]

    def body(src, dst, buf, lsem, ssem, rsem):
        x, y, c, _, _, chip_idx = _place()
        sib = (x, y, 1 - c)

        def region(ref, k, half, r0, rows):
            return ref.at[chip_idx[k], pl.ds(pl.multiple_of(half * Rh + r0, 16), rows)]

        def load(t):
            k, b = chunks[t]
            return pltpu.make_async_copy(region(src, k, c, b * tr, tr), buf.at[t % 2], lsem.at[t % 2])

        def send(t):
            k, b = chunks[t]
            return _rcopy(buf.at[t % 2], region(dst, k, c, b * tr, tr), ssem.at[t % 2], rsem.at[k], sib)

        load(0).start()
        for t in range(len(chunks)):
            load(t).wait()
            if t + 1 < len(chunks):
                if t >= 1:
                    send(t - 1).wait_send()
                load(t + 1).start()
            send(t).start()
        for t in range(max(0, len(chunks) - 2), len(chunks)):
            send(t).wait_send()
        for k in range(3):
            theirs = region(dst, k, 1 - c, 0, Rh)
            _rcopy(theirs, theirs, ssem.at[0], rsem.at[k], sib).wait_recv()

    return pl.pallas_call(
        body, name=name, in_specs=[ANY], out_specs=ANY, out_shape=jax.ShapeDtypeStruct(land.shape, land.dtype),
        input_output_aliases={0: 0},
        scratch_shapes=[pltpu.VMEM((2, tr, C), land.dtype), pltpu.SemaphoreType.DMA((2,)), pltpu.SemaphoreType.DMA((2,)),
                        pltpu.SemaphoreType.DMA((3,))],
    )(land)


def _allreduce_small(v):
    R, K = v.shape
    ndev = 8

    def body(v_ref, o_ref, land, ssem, rsem):
        x, y, c = lax.axis_index("x"), lax.axis_index("y"), lax.axis_index("c")
        me = 4 * x + 2 * y + c
        land[me] = v_ref[...]
        cps = []
        for r in range(1, ndev):
            fx, fy, fc = (r >> 2) & 1, (r >> 1) & 1, r & 1
            peer = (x ^ fx, y ^ fy, c ^ fc)
            cp = _rcopy(v_ref, land.at[me], ssem.at[r - 1], rsem.at[r - 1], peer)
            cp.start()
            cps.append((cp, 4 * peer[0] + 2 * peer[1] + peer[2], r))
        for cp, src, r in cps:
            cp.wait_send()
            _rcopy(v_ref, land.at[src], ssem.at[r - 1], rsem.at[r - 1], (x, y, c)).wait_recv()
        acc = land[0]
        for d in range(1, ndev):
            acc = acc + land[d]
        o_ref[...] = acc

    vm = pl.BlockSpec(memory_space=pltpu.VMEM)
    return pl.pallas_call(
        body, name="allreduce_small", in_specs=[vm], out_specs=vm, out_shape=jax.ShapeDtypeStruct((R, K), F32),
        scratch_shapes=[pltpu.VMEM((ndev, R, K), F32), pltpu.SemaphoreType.DMA((ndev - 1,)), pltpu.SemaphoreType.DMA((ndev - 1,))],
    )(v)


IN_SPLITS = (Q_RANK, KV_RANK, QK_ROPE, DIL_HEADS * HEAD, DIL_HEADS * HEAD, DIL_HEADS * HEAD, D_MODEL, D_MODEL)
IN_OFF = tuple(int(v) for v in np.cumsum((0,) + IN_SPLITS))


def _unshard_cols(g):
    G, K, Ns = g.shape
    return g.transpose(1, 0, 2).reshape(K, G * Ns)


def _shard_cols(w):
    K, N = w.shape
    return w.reshape(K, N_CHIPS, N // N_CHIPS).transpose(1, 0, 2)


def _rope_pad(w):
    half = QK_ROPE // 2
    z = jnp.zeros(w.shape[:-1] + (half,), w.dtype)
    return jnp.concatenate([w[..., :half], z, w[..., half:], z], axis=-1)


def _rope_unpad(w):
    half = QK_ROPE // 2
    return jnp.concatenate([w[..., :half], w[..., 2 * half:3 * half]], axis=-1)


def _split_w_in(w_in_g):
    G, K, Ns = w_in_g.shape

    def cols(lo, hi):
        pieces = [w_in_g[k][:, max(lo, k * Ns) - k * Ns:min(hi, (k + 1) * Ns) - k * Ns]
                  for k in range(G) if max(lo, k * Ns) < min(hi, (k + 1) * Ns)]
        return pieces[0] if len(pieces) == 1 else jnp.concatenate(pieces, axis=1)

    p = [(IN_OFF[i], IN_OFF[i + 1]) for i in range(8)]
    w_lat = jnp.concatenate([cols(*p[0]), cols(*p[1]), _rope_pad(cols(*p[2])),
                             jnp.zeros((K, LAT_W - _KPE.stop), w_in_g.dtype)], axis=1)
    w_dil = [jnp.concatenate([cols(p[3 + t][0] + g * DIL_O, p[3 + t][0] + (g + 1) * DIL_O) for t in range(3)], axis=1)
             for g in range(DIL_GROUPS)]
    w_gate = cols(p[6][0], p[7][1])
    return w_lat, w_dil, w_gate


def _merge_dw_in(dw_lat, dw_dil, dw_gate):
    parts = [dw_lat[:, _CQ], dw_lat[:, _CKV], _rope_unpad(dw_lat[:, _KPE])]
    for t in range(3):
        parts += [dw_dil[g][:, t * DIL_O:(t + 1) * DIL_O] for g in range(DIL_GROUPS)]
    parts.append(dw_gate)
    width = sum(p.shape[1] for p in parts) // N_CHIPS
    shards = []
    for k in range(N_CHIPS):
        pieces, at = [], 0
        for p in parts:
            lo, hi = max(k * width, at), min((k + 1) * width, at + p.shape[1])
            if lo < hi:
                pieces.append(p[:, lo - at:hi - at])
            at += p.shape[1]
        shards.append(jnp.concatenate(pieces, axis=1))
    return jnp.stack(shards)


def _split_w_uq(w_uq_g):
    w = _unshard_cols(w_uq_g)
    K = w.shape[0]
    w = w.reshape(K, MLA_HEADS, QK_NOPE + QK_ROPE)
    return w[:, :, :QK_NOPE].reshape(K, MLA_HEADS * HEAD), _rope_pad(w[:, :, QK_NOPE:]).reshape(K, MLA_HEADS * HEAD)


def _merge_dw_uq(dw_n, dw_p):
    K = dw_n.shape[0]
    w = jnp.concatenate([dw_n.reshape(K, MLA_HEADS, HEAD), _rope_unpad(dw_p.reshape(K, MLA_HEADS, HEAD))], axis=-1)
    return _shard_cols(w.reshape(K, MLA_HEADS * (QK_NOPE + QK_ROPE)))


def _split_w_ukv(w_ukv_g):
    w = _unshard_cols(w_ukv_g)
    K = w.shape[0]
    w = w.reshape(K, MLA_HEADS, 2 * HEAD)
    return w[:, :, :HEAD].reshape(K, MLA_HEADS * HEAD), w[:, :, HEAD:].reshape(K, MLA_HEADS * HEAD)


def _merge_dw_ukv(dw_k, dw_v):
    K = dw_k.shape[0]
    w = jnp.concatenate([dw_k.reshape(K, MLA_HEADS, HEAD), dw_v.reshape(K, MLA_HEADS, HEAD)], axis=-1)
    return _shard_cols(w.reshape(K, MLA_HEADS * 2 * HEAD))


GATHER_GROUPS = (("w_in",), ("w_uq", "w_ukv", "w_o_mla", "w_o_dil", "w_out"), ("w_up", "w_down", "conv_w"))
SHARED_FETCH = ("w_in",)
REDUCE_GROUPS = (("w_down", "w_up"), ("w_out", "w_o_mla", "w_o_dil"), ("w_uq", "w_ukv", "w_in"))


def _local_step(x, tgt, W, fetch, emit):
    S, D = x.shape
    cos, sin_s = _rope_tables(S)
    w_lat, w_dil, w_gate = _split_w_in(fetch(0, x)["w_in"])

    h = _rmsnorm_fwd("attn_norm", x, W["attn_norm_g"])
    lat = _mm_nn("proj_lat", h, w_lat)
    qkv = [_mm_nn(f"proj_dil{g}", h, w_dil[g], o_dtype=BF) for g in range(DIL_GROUPS)]
    gpre = _mm_nn("proj_gate", h, w_gate, o_dtype=BF)
    WB = fetch(1, gpre)
    w_uqn, w_uqp = _split_w_uq(WB["w_uq"])
    w_k, w_v = _split_w_ukv(WB["w_ukv"])
    w_o_mla, w_o_dil = WB["w_o_mla"], WB["w_o_dil"]
    w_out = WB["w_out"].reshape(D, D)
    qn_, kvn, kpe = _mla_prep(lat, W["q_norm_g"], W["kv_norm_g"], cos, sin_s)
    q_nope = _mm_nn("q_nope", qn_, w_uqn, o_dtype=BF)
    q_pe = _rope("q_rope", _mm_nn("q_pe", qn_, w_uqp), cos, sin_s, False)
    k_nope = _mm_nn("k_nope", kvn, w_k, o_dtype=BF)
    v_mla = _mm_nn("v_mla", kvn, w_v, o_dtype=BF)
    attn_a, lse_a = _mla_fwd(q_nope, q_pe, k_nope, kpe, v_mla)
    dil = [_dil_fwd(qkv[g], g) for g in range(DIL_GROUPS)]
    attn_b, lse_b = _dil_combine([o for o, _ in dil], [l for _, l in dil])
    o_a = _mm_nn("o_mla", attn_a, w_o_mla, o_dtype=BF)
    o_b = _mm_nn("o_dil", attn_b, w_o_dil, o_dtype=BF)
    merge = _merge_fwd(gpre, W["b_gate"], o_a, o_b)
    x1 = _mm_nn("out_proj", merge, w_out, add=x)
    WC = fetch(2, merge)
    w_up = WC["w_up"]
    G4, _, C = w_up.shape
    w_down = WC["w_down"].reshape(G4 // 2, C, D)
    conv_w = WC["conv_w"]
    conv_b = W["conv_b"].reshape(G4, 1, C)
    h2 = _rmsnorm_fwd("ffn_norm", x1, W["ffn_norm_g"])
    u_pre = _up_fwd(h2, w_up)
    act, u = _ffn_act(u_pre, conv_w, conv_b)
    x2 = _down_fwd(act, w_down, x1)
    dx2, d_final_g, loss8 = _final_loss(x2, tgt, W["final_norm_g"])

    d_act = _down_dgrad(dx2, w_down)
    dw_down = _down_wgrad(act, dx2)
    du = _ffn_act_bwd(u, d_act)
    du_pre, d_conv_w, d_conv_b = _conv_bwd(du, u_pre, conv_w)
    dh2 = _up_dgrad(du_pre, w_up)
    dw_up = _up_wgrad(h2, du_pre)
    zero = emit(0, {"w_down": dw_down.reshape(N_CHIPS, (G4 // 2) * C // N_CHIPS, D), "w_up": dw_up})
    dx1, d_ffn_g = _rmsnorm_bwd("ffn_norm_bwd", dh2, x1, W["ffn_norm_g"] + zero, dx2)
    d_merge = _mm_nt("out_proj_dgrad", dx1, w_out, o_dtype=BF)
    dw_out = _mm_tn("out_proj_wgrad", merge, dx1)
    d_oa, d_ob, d_gpre, d_b_gate = _merge_bwd(d_merge, gpre, W["b_gate"], o_a, o_b)
    d_attn_a = _mm_nt("o_mla_dgrad", d_oa, w_o_mla, o_dtype=BF)
    dw_o_mla = _mm_tn("o_mla_wgrad", attn_a, d_oa, shards=N_CHIPS)
    d_attn_b = _mm_nt("o_dil_dgrad", d_ob, w_o_dil, o_dtype=BF)
    dw_o_dil = _mm_tn("o_dil_wgrad", attn_b, d_ob, shards=N_CHIPS)
    zero = emit(1, {"w_out": dw_out.reshape(N_CHIPS, D // N_CHIPS, D), "w_o_mla": dw_o_mla, "w_o_dil": dw_o_dil})
    q_norm_g = W["q_norm_g"] + zero
    delta_b = _dil_delta(d_attn_b, attn_b)
    d_qkv = [_dil_bwd(qkv[g], d_attn_b, lse_b, delta_b, g) for g in range(DIL_GROUPS)]
    dq_nope, dq_pe_rot, dk_nope, dv_mla, dkpe_rot = _mla_bwd(q_nope, q_pe, k_nope, kpe, v_mla, d_attn_a, attn_a, lse_a)
    dq_pe = _rope("q_rope_bwd", dq_pe_rot, cos, sin_s, True)
    d_qn = _mm_nt("q_pe_dgrad", dq_pe, w_uqp, add=_mm_nt("q_nope_dgrad", dq_nope, w_uqn))
    d_kvn = _mm_nt("v_dgrad", dv_mla, w_v, add=_mm_nt("k_nope_dgrad", dk_nope, w_k))
    dw_uq = _merge_dw_uq(_mm_tn("q_nope_wgrad", qn_, dq_nope), _mm_tn("q_pe_wgrad", qn_, dq_pe))
    dw_ukv = _merge_dw_ukv(_mm_tn("k_nope_wgrad", kvn, dk_nope), _mm_tn("v_wgrad", kvn, dv_mla))
    d_lat, d_q_g, d_kv_g = _mla_prep_bwd(lat, q_norm_g, W["kv_norm_g"], cos, sin_s, d_qn, d_kvn, dkpe_rot)
    dw_in = _merge_dw_in(_mm_tn("proj_lat_wgrad", h, d_lat),
                         [_mm_tn(f"proj_dil{g}_wgrad", h, d_qkv[g]) for g in range(DIL_GROUPS)],
                         _mm_tn("proj_gate_wgrad", h, d_gpre))
    zero = emit(2, {"w_uq": dw_uq, "w_ukv": dw_ukv, "w_in": dw_in})
    dh = _mm_nt("proj_lat_dgrad", d_lat, w_lat + zero.astype(BF))
    for g in range(DIL_GROUPS):
        dh = _mm_nt(f"proj_dil{g}_dgrad", d_qkv[g], w_dil[g], add=dh)
    dh = _mm_nt("proj_gate_dgrad", d_gpre, w_gate, add=dh)
    grad_x, d_attn_g = _rmsnorm_bwd("attn_norm_bwd", dh, x, W["attn_norm_g"], dx1)

    small = {"attn_norm_g": d_attn_g, "b_gate": d_b_gate, "q_norm_g": d_q_g, "kv_norm_g": d_kv_g,
             "ffn_norm_g": d_ffn_g, "conv_w": d_conv_w, "conv_b": d_conv_b.reshape(1, G4 * C),
             "final_norm_g": d_final_g}
    return loss8[0, 0], grad_x, small


BIG = ("w_in", "w_uq", "w_ukv", "w_o_mla", "w_o_dil", "w_out", "w_up", "w_down")
SMALL = ("attn_norm_g", "b_gate", "q_norm_g", "kv_norm_g", "ffn_norm_g", "conv_w", "conv_b", "final_norm_g")
WEIGHTS = ("attn_norm_g", "w_in", "b_gate", "q_norm_g", "w_uq", "kv_norm_g", "w_ukv", "w_o_mla", "w_o_dil",
           "w_out", "ffn_norm_g", "w_up", "conv_w", "conv_b", "w_down", "final_norm_g")
SMALL_ROWS = 8
COLUMN_MAJOR = ("w_in", "w_up")
HALF_AXIS = {"w_down": 1}


def _gather_start(shards):
    chip = 2 * lax.axis_index("x") + lax.axis_index("y")
    c = lax.axis_index("c")

    def prepare(names, zero):
        srcs, lands = [], []
        for n in names:
            s = shards[n] + zero
            s = s if n == "conv_w" else s.astype(BF)
            lands.append(lax.dynamic_update_slice(lax.empty((N_CHIPS,) + s.shape, s.dtype), s[None], (chip, 0, 0)))
            if n in SHARED_FETCH:
                s = lax.dynamic_slice_in_dim(s, c * (s.shape[0] // 2), s.shape[0] // 2, 0)
            srcs.append(s)
        return srcs, lands

    n0 = len(GATHER_GROUPS[0])
    srcs0, lands0 = prepare(GATHER_GROUPS[0], 0.0)
    sems0, srcs0, lands0, token = _exchange_start("gather_start0", "gather", srcs0, lands0, [list(range(n0))])
    srcs, lands = prepare([n for grp in GATHER_GROUPS[1:] for n in grp], token[0, 0])
    groups, at = [], 0
    for grp in GATHER_GROUPS[1:]:
        groups.append(list(range(at, at + len(grp))))
        at += len(grp)
    sems, srcs, lands, token1 = _exchange_start("gather_start1", "gather", srcs, lands, groups)

    def fetch(i, after):
        if i == 0:
            _, got = _exchange_wait("gather_wait0", "gather", srcs0, lands0, sems0[0], token1)
        else:
            idx = groups[i - 1]
            _, got = _exchange_wait(f"gather_wait{i}", "gather", [srcs[j] for j in idx], [lands[j] for j in idx],
                                    sems[i - 1], after)
        return {n: _pair_share(f"pair_share_{n}", g) if n in SHARED_FETCH else g for n, g in zip(GATHER_GROUPS[i], got)}

    return fetch, token[0, 0]


def _reduce_start(i, grads):
    names = REDUCE_GROUPS[i]
    hs = [_pair_sum(f"pair_sum_{n}", grads[n], HALF_AXIS.get(n, 0)) for n in names]
    lands = [lax.empty((3,) + h.shape[1:], h.dtype) for h in hs]
    sems, hs, lands, token = _exchange_start(f"reduce_start{i}", "scatter", hs, lands, [list(range(len(names)))])
    return (sems[0], hs, lands), token[0, 0]


def _reduce_finish(i, pending, after):
    sems, hs, lands = pending
    hs, lands = _exchange_wait(f"reduce_wait{i}", "scatter", hs, lands, sems, after)
    out = {}
    for n, h, landed in zip(REDUCE_GROUPS[i], hs, lands):
        out[n] = _chip_total_join(f"chip_total_{n}", h, landed, HALF_AXIS.get(n, 0))
    return out


def _reduce_small(small):
    names = tuple(small)
    flat = [small[n].reshape(-1) for n in names]
    sizes = [f.shape[0] for f in flat]
    total = sum(sizes)
    width = -(-total // (SMALL_ROWS * LANE)) * LANE
    packed = jnp.concatenate(flat + [jnp.zeros((SMALL_ROWS * width - total,), F32)]).reshape(SMALL_ROWS, width)
    red = _allreduce_small(packed).reshape(-1)
    out, off = {}, 0
    for n, s in zip(names, sizes):
        out[n] = red[off:off + s]
        off += s
    return out


def kernel(x, attn_norm_g, w_in, b_gate, q_norm_g, w_uq, kv_norm_g, w_ukv, w_o_mla, w_o_dil, w_out, ffn_norm_g, w_up, conv_w, conv_b, w_down, final_norm_g, loss_target, m_attn_norm_g, m_w_in, m_b_gate, m_q_norm_g, m_w_uq, m_kv_norm_g, m_w_ukv, m_w_o_mla, m_w_o_dil, m_w_out, m_ffn_norm_g, m_w_up, m_conv_w, m_conv_b, m_w_down, m_final_norm_g, v_attn_norm_g, v_w_in, v_b_gate, v_q_norm_g, v_w_uq, v_kv_norm_g, v_w_ukv, v_w_o_mla, v_w_o_dil, v_w_out, v_ffn_norm_g, v_w_up, v_conv_w, v_conv_b, v_w_down, v_final_norm_g):
    given = dict(attn_norm_g=attn_norm_g, w_in=w_in, b_gate=b_gate, q_norm_g=q_norm_g, w_uq=w_uq, kv_norm_g=kv_norm_g,
                 w_ukv=w_ukv, w_o_mla=w_o_mla, w_o_dil=w_o_dil, w_out=w_out, ffn_norm_g=ffn_norm_g, w_up=w_up,
                 conv_w=conv_w, conv_b=conv_b, w_down=w_down, final_norm_g=final_norm_g)
    moments_m = dict(attn_norm_g=m_attn_norm_g, w_in=m_w_in, b_gate=m_b_gate, q_norm_g=m_q_norm_g, w_uq=m_w_uq,
                     kv_norm_g=m_kv_norm_g, w_ukv=m_w_ukv, w_o_mla=m_w_o_mla, w_o_dil=m_w_o_dil, w_out=m_w_out,
                     ffn_norm_g=m_ffn_norm_g, w_up=m_w_up, conv_w=m_conv_w, conv_b=m_conv_b, w_down=m_w_down,
                     final_norm_g=m_final_norm_g)
    moments_v = dict(attn_norm_g=v_attn_norm_g, w_in=v_w_in, b_gate=v_b_gate, q_norm_g=v_q_norm_g, w_uq=v_w_uq,
                     kv_norm_g=v_kv_norm_g, w_ukv=v_w_ukv, w_o_mla=v_w_o_mla, w_o_dil=v_w_o_dil, w_out=v_w_out,
                     ffn_norm_g=v_ffn_norm_g, w_up=v_w_up, conv_w=v_conv_w, conv_b=v_conv_b, w_down=v_w_down,
                     final_norm_g=v_final_norm_g)

    fetch, zero = _gather_start({n: given[n][0] for n in BIG + ("conv_w",)})
    W = {n: given[n] for n in ("b_gate", "q_norm_g", "kv_norm_g", "ffn_norm_g", "conv_b")}
    W["attn_norm_g"] = given["attn_norm_g"] + zero
    W["final_norm_g"] = given["final_norm_g"].reshape(1, -1)

    pending = {}

    def emit(i, grads):
        pending[i], token = _reduce_start(i, grads)
        return token

    loss_part, grad_x, small = _local_step(x[0], loss_target[0], W, fetch, emit)
    small["loss"] = loss_part
    grads, delta, new_m, new_v = {}, {}, {}, {}

    def adamw(n, g):
        shp = given[n].shape
        two_d = (-1, shp[-1]) if len(shp) > 1 else (1, -1)
        view = (lambda a: a.reshape(two_d).T) if n in COLUMN_MAJOR else (lambda a: a.reshape(two_d))
        back = (lambda a: a.T.reshape(shp)) if n in COLUMN_MAJOR else (lambda a: a.reshape(shp))
        go, d, nm, nv = _adamw(f"adamw_{n}", view(given[n]), view(g), view(moments_m[n]), view(moments_v[n]))
        grads[n], delta[n], new_m[n], new_v[n] = back(go), back(d), back(nm), back(nv)

    after = grad_x
    for i in range(len(REDUCE_GROUPS)):
        for n, g in _reduce_finish(i, pending[i], after).items():
            adamw(n, g)
        after = delta[REDUCE_GROUPS[i][-1]]
    g_small = _reduce_small(small)
    loss = g_small["loss"][0]
    chip = 2 * lax.axis_index("x") + lax.axis_index("y")
    for n in SMALL:
        if n == "conv_w":
            full = g_small[n].reshape(N_CHIPS, 3, -1)
            adamw(n, lax.dynamic_index_in_dim(full, chip, 0, keepdims=True))
        else:
            adamw(n, g_small[n])

    return (loss, grad_x[None], *[grads[n] for n in WEIGHTS], *[delta[n] for n in WEIGHTS],
            *[new_m[n] for n in WEIGHTS], *[new_v[n] for n in WEIGHTS])
```

```python
import functools
import math

import numpy as np
import jax
import jax.numpy as jnp
from jax import lax
from jax.experimental import pallas as pl
from jax.experimental.pallas import tpu as pltpu

F32 = jnp.float32
BF = jnp.bfloat16
MESH = pl.DeviceIdType.MESH

D_MODEL = 2048
MLA_HEADS = 8
QK_NOPE = 128
QK_ROPE = 64
Q_RANK = 512
KV_RANK = 256
ROPE_THETA = 10000.0
DIL_PATTERNS = ((128, 1), (512, 4), (2048, 16))
DIL_GROUPS = 3
DIL_HPG = 4
DIL_HEADS = 12
HEAD = 128
DIL_BLOCK = 128
ALIBI_MAX_BIAS = 8.0
NORM_EPS = 1e-6
N_CHIPS = 4
ADAM_LR = 0.001
ADAM_B1 = 0.9
ADAM_B2 = 0.999
ADAM_EPS = 1e-08
ADAM_WD = 0.01
ADAM_STEP = 10

LANE = 128
VMEM_LIMIT = 56 * 1024 * 1024
MLA_SCALE = (QK_NOPE + QK_ROPE) ** -0.5
DIL_SCALE = HEAD ** -0.5


def _params(*sem):
    return pltpu.CompilerParams(dimension_semantics=sem, vmem_limit_bytes=VMEM_LIMIT)


def _tile(n, pref):
    t = (pref // LANE) * LANE
    while t >= LANE:
        if n % t == 0:
            return t
        t -= LANE
    return n


NN = (((1,), (0,)), ((), ()))
NT = (((1,), (1,)), ((), ()))
TN = (((0,), (0,)), ((), ()))


def _mm_call(name, a, b, add, *, grid, a_spec, b_spec, add_spec, o_spec, o_shape, o_dtype, acc_shape, dims, nk):
    nax = len(grid)

    def body(*refs):
        if add is None:
            a_ref, b_ref, o_ref = refs[:3]
            c_ref = None
            scr = refs[3:]
        else:
            a_ref, b_ref, c_ref, o_ref = refs[:4]
            scr = refs[4:]
        prod = lax.dot_general(a_ref[...].astype(BF), b_ref[...].astype(BF), dims, preferred_element_type=F32)
        if nk == 1:
            if c_ref is not None:
                prod = prod + c_ref[...]
            o_ref[...] = prod.astype(o_ref.dtype)
        else:
            acc = scr[0]
            k = pl.program_id(nax - 1)

            @pl.when(k == 0)
            def _():
                if c_ref is not None:
                    acc[...] = prod + c_ref[...]
                else:
                    acc[...] = prod

            @pl.when(k > 0)
            def _():
                acc[...] += prod

            @pl.when(k == nk - 1)
            def _():
                o_ref[...] = acc[...].astype(o_ref.dtype)

    ins = [a, b] + ([] if add is None else [add])
    specs = [a_spec, b_spec] + ([] if add is None else [add_spec])
    sem = ("parallel",) * (nax - 1) + ("arbitrary",)
    return pl.pallas_call(
        body, name=name, grid=grid, in_specs=specs, out_specs=o_spec,
        out_shape=jax.ShapeDtypeStruct(o_shape, o_dtype),
        scratch_shapes=[] if nk == 1 else [pltpu.VMEM(acc_shape, F32)],
        compiler_params=_params(*sem),
    )(*ins)


def _mm_nn(name, a, b, *, add=None, o_dtype=F32):
    M, K = a.shape
    sharded = b.ndim == 3
    Ns = b.shape[-1]
    N = Ns * (b.shape[0] if sharded else 1)
    tm, tn, tk = _tile(M, 1024), _tile(Ns, 1024), _tile(K, 2048)
    per = Ns // tn
    nk = K // tk
    if sharded:
        b_spec = pl.BlockSpec((None, tk, tn), lambda i, j, k: (j // per, k, j % per))
    else:
        b_spec = pl.BlockSpec((tk, tn), lambda i, j, k: (k, j))
    return _mm_call(
        name, a, b, add, grid=(M // tm, N // tn, nk),
        a_spec=pl.BlockSpec((tm, tk), lambda i, j, k: (i, k)), b_spec=b_spec,
        add_spec=pl.BlockSpec((tm, tn), lambda i, j, k: (i, j)),
        o_spec=pl.BlockSpec((tm, tn), lambda i, j, k: (i, j)),
        o_shape=(M, N), o_dtype=o_dtype, acc_shape=(tm, tn), dims=NN, nk=nk)


def _mm_nt(name, a, b, *, add=None, o_dtype=F32):
    M, K = a.shape
    sharded = b.ndim == 3
    N, Ks = b.shape[-2], b.shape[-1]
    tm, tn, tk = _tile(M, 1024), _tile(N, 1024), _tile(Ks, 2048)
    per = Ks // tk
    nk = K // tk
    if sharded:
        b_spec = pl.BlockSpec((None, tn, tk), lambda i, j, k: (k // per, j, k % per))
    else:
        b_spec = pl.BlockSpec((tn, tk), lambda i, j, k: (j, k))
    return _mm_call(
        name, a, b, add, grid=(M // tm, N // tn, nk),
        a_spec=pl.BlockSpec((tm, tk), lambda i, j, k: (i, k)), b_spec=b_spec,
        add_spec=pl.BlockSpec((tm, tn), lambda i, j, k: (i, j)),
        o_spec=pl.BlockSpec((tm, tn), lambda i, j, k: (i, j)),
        o_shape=(M, N), o_dtype=o_dtype, acc_shape=(tm, tn), dims=NT, nk=nk)


def _mm_tn(name, a, b, *, shards=1, o_dtype=BF):
    S, M = a.shape
    N = b.shape[1]
    Ns = N // shards
    tm, tn, tk = _tile(M, 1024), _tile(Ns, 1024), _tile(S, 2048)
    per = Ns // tn
    nk = S // tk
    if shards > 1:
        o_spec = pl.BlockSpec((None, tm, tn), lambda i, j, k: (j // per, i, j % per))
        o_shape = (shards, M, Ns)
    else:
        o_spec = pl.BlockSpec((tm, tn), lambda i, j, k: (i, j))
        o_shape = (M, N)
    return _mm_call(
        name, a, b, None, grid=(M // tm, N // tn, nk),
        a_spec=pl.BlockSpec((tk, tm), lambda i, j, k: (k, i)),
        b_spec=pl.BlockSpec((tk, tn), lambda i, j, k: (k, j)),
        add_spec=None, o_spec=o_spec, o_shape=o_shape, o_dtype=o_dtype, acc_shape=(tm, tn), dims=TN, nk=nk)


def _up_fwd(h2, w_up):
    S, D = h2.shape
    G, _, C = w_up.shape
    tm = _tile(S, 512)
    return _mm_call(
        "up_fwd", h2, w_up, None, grid=(G, S // tm, 1),
        a_spec=pl.BlockSpec((tm, D), lambda g, i, k: (i, 0)),
        b_spec=pl.BlockSpec((None, D, C), lambda g, i, k: (g, 0, 0)),
        add_spec=None, o_spec=pl.BlockSpec((None, tm, C), lambda g, i, k: (g, i, 0)),
        o_shape=(G, S, C), o_dtype=BF, acc_shape=None, dims=NN, nk=1)


def _up_dgrad(du_pre, w_up):
    G, S, C = du_pre.shape
    D = w_up.shape[1]
    tm, tn = _tile(S, 1024), _tile(D, 1024)
    return _mm_call(
        "up_dgrad", du_pre, w_up, None, grid=(S // tm, D // tn, G),
        a_spec=pl.BlockSpec((None, tm, C), lambda i, j, g: (g, i, 0)),
        b_spec=pl.BlockSpec((None, tn, C), lambda i, j, g: (g, j, 0)),
        add_spec=None, o_spec=pl.BlockSpec((tm, tn), lambda i, j, g: (i, j)),
        o_shape=(S, D), o_dtype=F32, acc_shape=(tm, tn), dims=NT, nk=G)


def _up_wgrad(h2, du_pre):
    G, S, C = du_pre.shape
    D = h2.shape[1]
    tm, tk = _tile(D, 512), _tile(S, 2048)
    return _mm_call(
        "up_wgrad", h2, du_pre, None, grid=(G, D // tm, S // tk),
        a_spec=pl.BlockSpec((tk, tm), lambda g, i, k: (k, i)),
        b_spec=pl.BlockSpec((None, tk, C), lambda g, i, k: (g, k, 0)),
        add_spec=None, o_spec=pl.BlockSpec((None, tm, C), lambda g, i, k: (g, i, 0)),
        o_shape=(G, D, C), o_dtype=BF, acc_shape=(tm, C), dims=TN, nk=S // tk)


def _down_fwd(act, w_down, x1):
    G, S, C = act.shape
    D = w_down.shape[2]
    tm, tn = _tile(S, 1024), _tile(D, 1024)
    return _mm_call(
        "down_fwd", act, w_down, x1, grid=(S // tm, D // tn, G),
        a_spec=pl.BlockSpec((None, tm, C), lambda i, j, g: (g, i, 0)),
        b_spec=pl.BlockSpec((None, C, tn), lambda i, j, g: (g, 0, j)),
        add_spec=pl.BlockSpec((tm, tn), lambda i, j, g: (i, j)),
        o_spec=pl.BlockSpec((tm, tn), lambda i, j, g: (i, j)),
        o_shape=(S, D), o_dtype=F32, acc_shape=(tm, tn), dims=NN, nk=G)


def _down_dgrad(dx2, w_down):
    S, D = dx2.shape
    G, C, _ = w_down.shape
    tm = _tile(S, 512)
    return _mm_call(
        "down_dgrad", dx2, w_down, None, grid=(G, S // tm, 1),
        a_spec=pl.BlockSpec((tm, D), lambda g, i, k: (i, 0)),
        b_spec=pl.BlockSpec((None, C, D), lambda g, i, k: (g, 0, 0)),
        add_spec=None, o_spec=pl.BlockSpec((None, tm, C), lambda g, i, k: (g, i, 0)),
        o_shape=(G, S, C), o_dtype=BF, acc_shape=None, dims=NT, nk=1)


def _down_wgrad(act, dx2):
    G, S, C = act.shape
    D = dx2.shape[1]
    tn, tk = _tile(D, 512), _tile(S, 1024)
    return _mm_call(
        "down_wgrad", act, dx2, None, grid=(G, D // tn, S // tk),
        a_spec=pl.BlockSpec((None, tk, C), lambda g, j, k: (g, k, 0)),
        b_spec=pl.BlockSpec((tk, tn), lambda g, j, k: (k, j)),
        add_spec=None, o_spec=pl.BlockSpec((None, C, tn), lambda g, j, k: (g, 0, j)),
        o_shape=(G, C, D), o_dtype=BF, acc_shape=(C, tn), dims=TN, nk=S // tk)


def _row(ts, c):
    return pl.BlockSpec((ts, c), lambda i: (i, 0))


def _bcast(r, c):
    return pl.BlockSpec((r, c), lambda i: (0, 0))


def _accumulate(i, ref, val):
    @pl.when(i == 0)
    def _():
        ref[...] = val

    @pl.when(i > 0)
    def _():
        ref[...] += val


def _rstd(xv):
    return lax.rsqrt(jnp.mean(xv * xv, axis=-1, keepdims=True) + NORM_EPS)


def _rmsnorm_fwd(name, x, g):
    S, D = x.shape
    ts = _tile(S, 512)

    def body(x_ref, g_ref, o_ref):
        xv = x_ref[...]
        o_ref[...] = (xv * _rstd(xv) * g_ref[...]).astype(o_ref.dtype)

    return pl.pallas_call(
        body, name=name, grid=(S // ts,), in_specs=[_row(ts, D), _bcast(1, D)], out_specs=_row(ts, D),
        out_shape=jax.ShapeDtypeStruct((S, D), BF), compiler_params=_params("parallel"))(x, g)


def _norm_bwd_rows(dy, xv, g):
    r = _rstd(xv)
    xh = xv * r
    dxh = dy * g
    dx = r * (dxh - xh * jnp.mean(dxh * xh, axis=-1, keepdims=True))
    return dx, jnp.sum(dy * xh, axis=0, keepdims=True)


def _rmsnorm_bwd(name, dy, x, g, res):
    S, D = x.shape
    ts = _tile(S, 512)

    def body(dy_ref, x_ref, g_ref, res_ref, dx_ref, dg_ref):
        dx, dg = _norm_bwd_rows(dy_ref[...], x_ref[...], g_ref[...])
        dx_ref[...] = dx + res_ref[...]
        _accumulate(pl.program_id(0), dg_ref, dg)

    return pl.pallas_call(
        body, name=name, grid=(S // ts,),
        in_specs=[_row(ts, D), _row(ts, D), _bcast(1, D), _row(ts, D)],
        out_specs=[_row(ts, D), _bcast(1, D)],
        out_shape=[jax.ShapeDtypeStruct((S, D), F32), jax.ShapeDtypeStruct((1, D), F32)],
        compiler_params=_params("arbitrary"))(dy, x, g, res)


def _rope_tables(S):
    half = QK_ROPE // 2
    pos = jnp.arange(S, dtype=F32)
    inv_freq = ROPE_THETA ** (-jnp.arange(0, QK_ROPE, 2, dtype=F32) / QK_ROPE)
    ang = pos[:, None] * inv_freq[None, :]
    cos, sin = jnp.cos(ang), jnp.sin(ang)
    z = jnp.zeros((S, half), F32)
    return jnp.concatenate([cos, z, cos, z], axis=1), jnp.concatenate([-sin, z, sin, z], axis=1)


def _rope_lanes(x, cos, sin_signed, inverse):
    if inverse:
        return x * cos + pltpu.roll(x * sin_signed, LANE // 2, 1)
    return x * cos + pltpu.roll(x, LANE // 2, 1) * sin_signed


def _rope(name, x, cos, sin_signed, inverse):
    S, W = x.shape
    ts = _tile(S, 512)

    def body(x_ref, c_ref, s_ref, o_ref):
        c, s = c_ref[...], s_ref[...]
        for h in range(W // LANE):
            sl = slice(h * LANE, (h + 1) * LANE)
            o_ref[:, sl] = _rope_lanes(x_ref[:, sl], c, s, inverse).astype(o_ref.dtype)

    return pl.pallas_call(
        body, name=name, grid=(S // ts,), in_specs=[_row(ts, W), _row(ts, LANE), _row(ts, LANE)],
        out_specs=_row(ts, W), out_shape=jax.ShapeDtypeStruct((S, W), BF),
        compiler_params=_params("parallel"))(x, cos, sin_signed)


LAT_W = 1024
_CQ = slice(0, Q_RANK)
_CKV = slice(Q_RANK, Q_RANK + KV_RANK)
_KPE = slice(Q_RANK + KV_RANK, Q_RANK + KV_RANK + LANE)


def _mla_prep(lat, qg, kvg, cos, sin_signed):
    S = lat.shape[0]
    ts = _tile(S, 512)

    def body(lat_ref, qg_ref, kvg_ref, c_ref, s_ref, qn_ref, kvn_ref, kpe_ref):
        cq = lat_ref[:, _CQ]
        qn_ref[...] = (cq * _rstd(cq) * qg_ref[...]).astype(BF)
        ckv = lat_ref[:, _CKV]
        kvn_ref[...] = (ckv * _rstd(ckv) * kvg_ref[...]).astype(BF)
        kpe_ref[...] = _rope_lanes(lat_ref[:, _KPE], c_ref[...], s_ref[...], False).astype(BF)

    return pl.pallas_call(
        body, name="mla_prep", grid=(S // ts,),
        in_specs=[_row(ts, LAT_W), _bcast(1, Q_RANK), _bcast(1, KV_RANK), _row(ts, LANE), _row(ts, LANE)],
        out_specs=[_row(ts, Q_RANK), _row(ts, KV_RANK), _row(ts, LANE)],
        out_shape=[jax.ShapeDtypeStruct((S, Q_RANK), BF), jax.ShapeDtypeStruct((S, KV_RANK), BF),
                   jax.ShapeDtypeStruct((S, LANE), BF)],
        compiler_params=_params("parallel"))(lat, qg, kvg, cos, sin_signed)


def _mla_prep_bwd(lat, qg, kvg, cos, sin_signed, d_qn, d_kvn, d_kpe):
    S = lat.shape[0]
    ts = _tile(S, 512)

    def body(lat_ref, qg_ref, kvg_ref, c_ref, s_ref, dqn_ref, dkvn_ref, dkpe_ref, dlat_ref, dqg_ref, dkvg_ref):
        i = pl.program_id(0)
        dcq, dqg = _norm_bwd_rows(dqn_ref[...], lat_ref[:, _CQ], qg_ref[...])
        dckv, dkvg = _norm_bwd_rows(dkvn_ref[...], lat_ref[:, _CKV], kvg_ref[...])
        dlat_ref[:, _CQ] = dcq.astype(BF)
        dlat_ref[:, _CKV] = dckv.astype(BF)
        dkpe = dkpe_ref[0]
        for g in range(1, d_kpe.shape[0]):
            dkpe = dkpe + dkpe_ref[g]
        dlat_ref[:, _KPE] = _rope_lanes(dkpe, c_ref[...], s_ref[...], True).astype(BF)
        dlat_ref[:, _KPE.stop:] = jnp.zeros((ts, LAT_W - _KPE.stop), BF)
        _accumulate(i, dqg_ref, dqg)
        _accumulate(i, dkvg_ref, dkvg)

    return pl.pallas_call(
        body, name="mla_prep_bwd", grid=(S // ts,),
        in_specs=[_row(ts, LAT_W), _bcast(1, Q_RANK), _bcast(1, KV_RANK), _row(ts, LANE), _row(ts, LANE),
                  _row(ts, Q_RANK), _row(ts, KV_RANK), pl.BlockSpec((d_kpe.shape[0], ts, LANE), lambda i: (0, i, 0))],
        out_specs=[_row(ts, LAT_W), _bcast(1, Q_RANK), _bcast(1, KV_RANK)],
        out_shape=[jax.ShapeDtypeStruct((S, LAT_W), BF), jax.ShapeDtypeStruct((1, Q_RANK), F32),
                   jax.ShapeDtypeStruct((1, KV_RANK), F32)],
        compiler_params=_params("arbitrary"))(lat, qg, kvg, cos, sin_signed, d_qn, d_kvn, d_kpe)


def _sigmoid(z):
    return 1.0 / (1.0 + jnp.exp(-z))


def _merge_fwd(gpre, b_gate, o_a, o_b):
    S, D = o_a.shape
    ts = _tile(S, 256)

    def body(g_ref, b_ref, oa_ref, ob_ref, m_ref):
        ga = _sigmoid(g_ref[:, :D] + b_ref[:, :D])
        gb = _sigmoid(g_ref[:, D:] + b_ref[:, D:])
        m_ref[...] = (ga * oa_ref[...] + gb * ob_ref[...]).astype(BF)

    return pl.pallas_call(
        body, name="merge_fwd", grid=(S // ts,),
        in_specs=[_row(ts, 2 * D), _bcast(1, 2 * D), _row(ts, D), _row(ts, D)], out_specs=_row(ts, D),
        out_shape=jax.ShapeDtypeStruct((S, D), BF), compiler_params=_params("parallel"))(gpre, b_gate, o_a, o_b)


def _merge_bwd(d_merge, gpre, b_gate, o_a, o_b):
    S, D = o_a.shape
    ts = _tile(S, 256)

    def body(dm_ref, g_ref, b_ref, oa_ref, ob_ref, doa_ref, dob_ref, dg_ref, db_ref):
        dm = dm_ref[...]
        ga = _sigmoid(g_ref[:, :D] + b_ref[:, :D])
        gb = _sigmoid(g_ref[:, D:] + b_ref[:, D:])
        doa_ref[...] = (dm * ga).astype(BF)
        dob_ref[...] = (dm * gb).astype(BF)
        dga = dm * oa_ref[...] * ga * (1.0 - ga)
        dgb = dm * ob_ref[...] * gb * (1.0 - gb)
        dg_ref[:, :D] = dga.astype(BF)
        dg_ref[:, D:] = dgb.astype(BF)
        i = pl.program_id(0)
        part = jnp.concatenate([jnp.sum(dga, axis=0, keepdims=True), jnp.sum(dgb, axis=0, keepdims=True)], axis=1)
        _accumulate(i, db_ref, part)

    return pl.pallas_call(
        body, name="merge_bwd", grid=(S // ts,),
        in_specs=[_row(ts, D), _row(ts, 2 * D), _bcast(1, 2 * D), _row(ts, D), _row(ts, D)],
        out_specs=[_row(ts, D), _row(ts, D), _row(ts, 2 * D), _bcast(1, 2 * D)],
        out_shape=[jax.ShapeDtypeStruct((S, D), BF), jax.ShapeDtypeStruct((S, D), BF),
                   jax.ShapeDtypeStruct((S, 2 * D), BF), jax.ShapeDtypeStruct((1, 2 * D), F32)],
        compiler_params=_params("arbitrary"))(d_merge, gpre, b_gate, o_a, o_b)


def _final_loss(x2, tgt, gf):
    S, D = x2.shape
    ts = _tile(S, 512)

    def body(x_ref, t_ref, g_ref, dx_ref, dg_ref, loss_ref):
        i = pl.program_id(0)
        xv = x_ref[...]
        g = g_ref[...]
        y = xv * _rstd(xv) * g
        err = y - t_ref[...]
        dx, dg = _norm_bwd_rows(err * (1.0 / D), xv, g)
        dx_ref[...] = dx
        _accumulate(i, dg_ref, dg)
        part = 0.5 * jnp.sum(jnp.mean(err * err, axis=-1, keepdims=True), axis=0, keepdims=True)
        _accumulate(i, loss_ref, jnp.broadcast_to(part, (8, LANE)))

    return pl.pallas_call(
        body, name="final_loss", grid=(S // ts,),
        in_specs=[_row(ts, D), _row(ts, D), _bcast(1, D)],
        out_specs=[_row(ts, D), _bcast(1, D), _bcast(8, LANE)],
        out_shape=[jax.ShapeDtypeStruct((S, D), F32), jax.ShapeDtypeStruct((1, D), F32),
                   jax.ShapeDtypeStruct((8, LANE), F32)],
        compiler_params=_params("arbitrary"))(x2, tgt, gf)


HALO = 16


SUB = 8


def _shift_down(cur, prev, k, rows):
    out = pltpu.roll(cur, k, 0)
    head = out[:SUB]
    for j in range(k):
        head = jnp.where(rows == j, prev[HALO - k + j:HALO - k + j + 1, :], head)
    return jnp.concatenate([head, out[SUB:]], axis=0)


def _shift_up(cur, nxt, k, rows, ts):
    out = pltpu.roll(cur, ts - k, 0)
    tail = out[ts - SUB:]
    for j in range(k):
        tail = jnp.where(rows == SUB - k + j, nxt[j:j + 1, :], tail)
    return jnp.concatenate([out[:ts - SUB], tail], axis=0)


def _conv_rows(cur, prev, w, b, rows):
    return b + w[0:1, :] * _shift_down(cur, prev, 2, rows) + w[1:2, :] * _shift_down(cur, prev, 1, rows) + w[2:3, :] * cur


def _conv_specs(ts, C, shard_of):
    nh = ts // HALO
    cur = pl.BlockSpec((None, ts, C), lambda g, i: (shard_of(g), i, 0))
    prev = pl.BlockSpec((None, HALO, C), lambda g, i: (shard_of(g), jnp.maximum(i * nh - 1, 0), 0))
    return cur, prev


def _ffn_act(u_pre, conv_w, conv_b):
    G4, S, C = u_pre.shape
    G = G4 // 2
    ts = _tile(S, 256)

    def body(up_ref, upp_ref, gt_ref, gtp_ref, wu_ref, wg_ref, bu_ref, bg_ref, act_ref, u_ref):
        first = pl.program_id(1) == 0
        rows = lax.broadcasted_iota(jnp.int32, (SUB, C), 0)
        pu = jnp.where(first, 0.0, upp_ref[...].astype(F32))
        pg = jnp.where(first, 0.0, gtp_ref[...].astype(F32))
        up = _conv_rows(up_ref[...].astype(F32), pu, wu_ref[...], bu_ref[...], rows)
        gate = _conv_rows(gt_ref[...].astype(F32), pg, wg_ref[...], bg_ref[...], rows)
        act_ref[...] = (gate * _sigmoid(gate) * up).astype(BF)
        u_ref[0] = up.astype(BF)
        u_ref[1] = gate.astype(BF)

    cur_u, prev_u = _conv_specs(ts, C, lambda g: g)
    cur_g, prev_g = _conv_specs(ts, C, lambda g: g + G)
    w_u = pl.BlockSpec((None, 3, C), lambda g, i: (g, 0, 0))
    w_g = pl.BlockSpec((None, 3, C), lambda g, i: (g + G, 0, 0))
    b_u = pl.BlockSpec((None, 1, C), lambda g, i: (g, 0, 0))
    b_g = pl.BlockSpec((None, 1, C), lambda g, i: (g + G, 0, 0))
    pair = pl.BlockSpec((2, None, ts, C), lambda g, i: (0, g, i, 0))
    act, u = pl.pallas_call(
        body, name="ffn_act", grid=(G, S // ts),
        in_specs=[cur_u, prev_u, cur_g, prev_g, w_u, w_g, b_u, b_g],
        out_specs=[pl.BlockSpec((None, ts, C), lambda g, i: (g, i, 0)), pair],
        out_shape=[jax.ShapeDtypeStruct((G, S, C), BF), jax.ShapeDtypeStruct((2, G, S, C), BF)],
        compiler_params=_params("parallel", "parallel"))(u_pre, u_pre, u_pre, u_pre, conv_w, conv_w, conv_b, conv_b)
    return act, u


def _ffn_act_conv_bwd(u, d_act, u_pre, conv_w):
    _, G, S, C = u.shape
    ts = _tile(S, 256)
    nh = ts // HALO
    last_halo = S // HALO - 1

    def d_outputs(up, gate, da):
        sg = _sigmoid(gate)
        return da * (gate * sg), da * up * (sg * (1.0 + gate * (1.0 - sg)))

    def body(u_ref, un_ref, da_ref, dan_ref, x_ref, w_ref, dpre_ref, dw_ref, db_ref):
        i = pl.program_id(1)
        rows = lax.broadcasted_iota(jnp.int32, (SUB, C), 0)
        da_n = jnp.where(i == pl.num_programs(1) - 1, 0.0, dan_ref[...].astype(F32))
        du = d_outputs(u_ref[0].astype(F32), u_ref[1].astype(F32), da_ref[...].astype(F32))
        du_n = d_outputs(un_ref[0].astype(F32), un_ref[1].astype(F32), da_n)
        for kind in range(2):
            du_c, w = du[kind], w_ref[kind]
            up1, up2 = _shift_up(du_c, du_n[kind], 1, rows, ts), _shift_up(du_c, du_n[kind], 2, rows, ts)
            dpre_ref[kind] = (w[2:3, :] * du_c + w[1:2, :] * up1 + w[0:1, :] * up2).astype(BF)
            x_c = x_ref[kind].astype(F32)
            dw = jnp.concatenate([
                jnp.sum(up2 * x_c, axis=0, keepdims=True),
                jnp.sum(up1 * x_c, axis=0, keepdims=True),
                jnp.sum(du_c * x_c, axis=0, keepdims=True)], axis=0)
            db = jnp.sum(du_c, axis=0, keepdims=True)

            @pl.when(i == 0)
            def _():
                dw_ref[kind] = dw
                db_ref[kind] = db

            @pl.when(i > 0)
            def _():
                dw_ref[kind] += dw
                db_ref[kind] += db

    def nxt_row(i):
        return jnp.minimum((i + 1) * nh, last_halo)

    pair = pl.BlockSpec((2, None, ts, C), lambda g, i: (0, g, i, 0))
    pair_n = pl.BlockSpec((2, None, HALO, C), lambda g, i: (0, g, nxt_row(i), 0))
    one = pl.BlockSpec((None, ts, C), lambda g, i: (g, i, 0))
    one_n = pl.BlockSpec((None, HALO, C), lambda g, i: (g, nxt_row(i), 0))
    w_spec = pl.BlockSpec((2, None, 3, C), lambda g, i: (0, g, 0, 0))
    b_spec = pl.BlockSpec((2, None, 1, C), lambda g, i: (0, g, 0, 0))
    dpre, dw, db = pl.pallas_call(
        body, name="ffn_act_conv_bwd", grid=(G, S // ts),
        in_specs=[pair, pair_n, one, one_n, pair, w_spec], out_specs=[pair, w_spec, b_spec],
        out_shape=[jax.ShapeDtypeStruct((2, G, S, C), BF), jax.ShapeDtypeStruct((2, G, 3, C), F32),
                   jax.ShapeDtypeStruct((2, G, 1, C), F32)],
        compiler_params=_params("parallel", "arbitrary"))(
            u, u, d_act, d_act, u_pre.reshape(2, G, S, C), conv_w.reshape(2, G, 3, C))
    return dpre.reshape(2 * G, S, C), dw.reshape(2 * G, 3, C), db.reshape(2 * G, 1, C)


MLA_T = 1024
MLA_HB = 4
MLA_BWD_HB = 2


def _mla_pairs(n, by_row):
    if by_row:
        pairs = [(i, j) for i in range(n) for j in range(i + 1)]
    else:
        pairs = [(i, j) for j in range(n) for i in range(j, n)]
    return jnp.asarray([p[0] for p in pairs], jnp.int32), jnp.asarray([p[1] for p in pairs], jnp.int32)


def _mla_specs(hb):
    q = pl.BlockSpec((MLA_T, hb * HEAD), lambda g, t, it, jt: (it[t], g))
    k = pl.BlockSpec((MLA_T, hb * HEAD), lambda g, t, it, jt: (jt[t], g))
    kpe = pl.BlockSpec((MLA_T, HEAD), lambda g, t, it, jt: (jt[t], 0))
    lse = pl.BlockSpec((hb, MLA_T, LANE), lambda g, t, it, jt: (g, it[t], 0))
    return q, k, kpe, lse


def _mla_head(ref, hh):
    return ref[:, hh * HEAD:(hh + 1) * HEAD]


LOG2E = math.log2(math.e)
MLA_EXP2_SCALE = MLA_SCALE * LOG2E


def _mla_scores(qn_ref, qp_ref, kn_ref, kpe, hh, ok):
    q = jnp.concatenate([_mla_head(qn_ref, hh), _mla_head(qp_ref, hh)], axis=1)
    k = jnp.concatenate([_mla_head(kn_ref, hh), kpe], axis=1)
    s = lax.dot_general(q, k, NT, preferred_element_type=F32)
    return q, k, s if ok is None else jnp.where(ok, s, -jnp.inf)


def _mla_diagonal_mask():
    row = lax.broadcasted_iota(jnp.int32, (MLA_T, MLA_T), 0)
    col = lax.broadcasted_iota(jnp.int32, (MLA_T, MLA_T), 1)
    return col <= row


def _mla_step(i, j, step):
    @pl.when(j < i)
    def _():
        step(None)

    @pl.when(j == i)
    def _():
        step(_mla_diagonal_mask())


def _mla_fwd(qn, qp, kn, kpe, v):
    S = qn.shape[0]
    it, jt = _mla_pairs(S // MLA_T, True)

    def body(it_ref, jt_ref, qn_ref, qp_ref, kn_ref, kpe_ref, v_ref, o_ref, lse_ref, m_scr, acc_scr):
        t = pl.program_id(1)
        i, j = it_ref[t], jt_ref[t]

        @pl.when(j == 0)
        def _():
            m_scr[...] = jnp.full(m_scr.shape, -jnp.inf, F32)
            acc_scr[...] = jnp.zeros(acc_scr.shape, F32)

        def step(ok):
            kpe_v = kpe_ref[...]
            ones = jnp.ones((MLA_T, HEAD), BF)
            state = [(m_scr[hh], acc_scr[hh]) for hh in range(MLA_HB)]
            new = []
            for hh in range(MLA_HB):
                m_prev, acc = state[hh]
                _, _, s = _mla_scores(qn_ref, qp_ref, kn_ref, kpe_v, hh, ok)
                m_new = jnp.maximum(m_prev, jnp.max(s, axis=1, keepdims=True))
                p = jnp.exp2((s - m_new) * MLA_EXP2_SCALE).astype(BF)
                v1 = jnp.concatenate([_mla_head(v_ref, hh), ones], axis=1)
                alpha = jnp.exp2((m_prev - m_new) * MLA_EXP2_SCALE)
                new.append((m_new, alpha * acc + lax.dot_general(p, v1, NN, preferred_element_type=F32)))
            for hh in range(MLA_HB):
                m_scr[hh], acc_scr[hh] = new[hh]

        _mla_step(i, j, step)

        @pl.when(j == i)
        def _():
            for hh in range(MLA_HB):
                l = acc_scr[hh, :, HEAD:]
                o_ref[:, hh * HEAD:(hh + 1) * HEAD] = (acc_scr[hh, :, :HEAD] / l).astype(BF)
                lse_ref[hh] = m_scr[hh] * MLA_SCALE + jnp.log(l)

    qspec, kspec, kpespec, lsespec = _mla_specs(MLA_HB)
    grid_spec = pltpu.PrefetchScalarGridSpec(
        num_scalar_prefetch=2, grid=(MLA_HEADS // MLA_HB, it.shape[0]),
        in_specs=[qspec, qspec, kspec, kpespec, kspec], out_specs=[qspec, lsespec],
        scratch_shapes=[pltpu.VMEM((MLA_HB, MLA_T, 1), F32), pltpu.VMEM((MLA_HB, MLA_T, 2 * HEAD), F32)])
    return pl.pallas_call(
        body, name="mla_fwd", grid_spec=grid_spec,
        out_shape=[jax.ShapeDtypeStruct((S, MLA_HEADS * HEAD), BF), jax.ShapeDtypeStruct((MLA_HEADS, S, LANE), F32)],
        compiler_params=_params("parallel", "arbitrary"))(it, jt, qn, qp, kn, kpe, v)


def _mla_p_ds(qn_ref, qp_ref, kn_ref, kpe, v_ref, do_ref, o_ref, lse_ref, hh, ok):
    q, k, s = _mla_scores(qn_ref, qp_ref, kn_ref, kpe, hh, ok)
    p = jnp.exp2(s * MLA_EXP2_SCALE - lse_ref[hh][:, 0:1] * LOG2E)
    do = _mla_head(do_ref, hh)
    delta = jnp.sum(do.astype(F32) * _mla_head(o_ref, hh).astype(F32), axis=1, keepdims=True)
    dp = lax.dot_general(do, _mla_head(v_ref, hh), NT, preferred_element_type=F32)
    ds = p * (dp - delta) * MLA_SCALE
    return q, k, p, ds, do


def _mla_bwd(qn, qp, kn, kpe, v, do, o, lse):
    S = qn.shape[0]
    nq = S // MLA_T
    hb = MLA_BWD_HB
    it, jt = _mla_pairs(nq, False)

    def body(it_ref, jt_ref, qn_ref, qp_ref, kn_ref, kpe_ref, v_ref, do_ref, o_ref, lse_ref,
             dqn_ref, dqp_ref, dkn_ref, dv_ref, dkpe_ref, dq_acc, dk_acc, dv_acc, stage_n, stage_p, osem):
        t = pl.program_id(1)
        i, j = it_ref[t], jt_ref[t]

        @pl.when(t == 0)
        def _():
            dq_acc[...] = jnp.zeros(dq_acc.shape, F32)

        @pl.when(i == j)
        def _():
            dk_acc[...] = jnp.zeros(dk_acc.shape, F32)
            dv_acc[...] = jnp.zeros(dv_acc.shape, F32)

        def step(ok):
            kpe_v = kpe_ref[...]
            for hh in range(hb):
                q, k, p, ds, do_h = _mla_p_ds(qn_ref, qp_ref, kn_ref, kpe_v, v_ref, do_ref, o_ref, lse_ref, hh, ok)
                ds = ds.astype(BF)
                dv_acc[hh] += lax.dot_general(p.astype(BF), do_h, TN, preferred_element_type=F32)
                dk_acc[hh] += lax.dot_general(ds, q, TN, preferred_element_type=F32)
                dq_acc[i, hh] += lax.dot_general(ds, k, NN, preferred_element_type=F32)

        _mla_step(i, j, step)

        @pl.when(i == j)
        def _():
            for hh in range(hb):
                stage_n[:, hh * HEAD:(hh + 1) * HEAD] = dq_acc[i, hh, :, :HEAD].astype(BF)
                stage_p[:, hh * HEAD:(hh + 1) * HEAD] = dq_acc[i, hh, :, HEAD:]
            rows = pl.ds(pl.multiple_of(i * MLA_T, MLA_T), MLA_T)
            cols = pl.ds(pl.multiple_of(pl.program_id(0) * hb * HEAD, LANE), hb * HEAD)
            out_n = pltpu.make_async_copy(stage_n, dqn_ref.at[rows, cols], osem.at[0])
            out_p = pltpu.make_async_copy(stage_p, dqp_ref.at[rows, cols], osem.at[1])
            out_n.start()
            out_p.start()
            out_n.wait()
            out_p.wait()

        @pl.when(i == nq - 1)
        def _():
            dkpe = dk_acc[0, :, HEAD:]
            for hh in range(hb):
                dkn_ref[:, hh * HEAD:(hh + 1) * HEAD] = dk_acc[hh, :, :HEAD].astype(BF)
                dv_ref[:, hh * HEAD:(hh + 1) * HEAD] = dv_acc[hh].astype(BF)
                if hh:
                    dkpe = dkpe + dk_acc[hh, :, HEAD:]
            dkpe_ref[...] = dkpe

    qspec, kspec, kpespec, lsespec = _mla_specs(hb)
    dkpespec = pl.BlockSpec((None, MLA_T, HEAD), lambda g, t, it, jt: (g, jt[t], 0))
    grid_spec = pltpu.PrefetchScalarGridSpec(
        num_scalar_prefetch=2, grid=(MLA_HEADS // hb, it.shape[0]),
        in_specs=[qspec, qspec, kspec, kpespec, kspec, qspec, qspec, lsespec],
        out_specs=[ANY, ANY, kspec, kspec, dkpespec],
        scratch_shapes=[pltpu.VMEM((nq, hb, MLA_T, 2 * HEAD), F32), pltpu.VMEM((hb, MLA_T, 2 * HEAD), F32),
                        pltpu.VMEM((hb, MLA_T, HEAD), F32), pltpu.VMEM((MLA_T, hb * HEAD), BF),
                        pltpu.VMEM((MLA_T, hb * HEAD), F32), pltpu.SemaphoreType.DMA((2,))])
    wide = jax.ShapeDtypeStruct((S, MLA_HEADS * HEAD), BF)
    return pl.pallas_call(
        body, name="mla_bwd", grid_spec=grid_spec,
        out_shape=[wide, jax.ShapeDtypeStruct((S, MLA_HEADS * HEAD), F32), wide, wide,
                   jax.ShapeDtypeStruct((MLA_HEADS // hb, S, HEAD), F32)],
        compiler_params=_params("parallel", "arbitrary"))(it, jt, qn, qp, kn, kpe, v, do, o, lse)


DIL_W = 3 * DIL_HPG * HEAD
DIL_O = DIL_HPG * HEAD
DIL_STEP_BLOCKS = 4


def _dil_slopes(g):
    return [2.0 ** (-ALIBI_MAX_BIAS * (g * DIL_HPG + hh + 1) / DIL_HEADS) for hh in range(DIL_HPG)]


def _dil_bias(dil):
    p = lax.broadcasted_iota(jnp.int32, (DIL_BLOCK, DIL_BLOCK), 0)
    kk = lax.broadcasted_iota(jnp.int32, (DIL_BLOCK, DIL_BLOCK), 1)
    jc = p - kk
    dist_c = (dil * jc).astype(F32)
    dist_p = (dil * (jc + DIL_BLOCK)).astype(F32)
    return jc >= 0, jc <= 0, dist_c, dist_p


def _dil_bias2(dil):
    p = lax.broadcasted_iota(jnp.int32, (DIL_BLOCK, 2 * DIL_BLOCK), 0)
    kk = lax.broadcasted_iota(jnp.int32, (DIL_BLOCK, 2 * DIL_BLOCK), 1)
    j = p + DIL_BLOCK - kk
    return (j >= 0) & (j <= DIL_BLOCK), kk < DIL_BLOCK, (dil * j).astype(F32)


def _dil_head(blk, hh):
    q = blk[:, hh * HEAD:(hh + 1) * HEAD]
    k = blk[:, DIL_O + hh * HEAD:DIL_O + (hh + 1) * HEAD]
    v = blk[:, 2 * DIL_O + hh * HEAD:2 * DIL_O + (hh + 1) * HEAD]
    return q, k, v


def _dil_s(q, k, slope, dist, ok):
    s = lax.dot_general(q, k, NT, preferred_element_type=F32) * DIL_SCALE - slope * dist
    return jnp.where(ok, s, -jnp.inf)


def _dil_view(a, dil):
    S, W = a.shape
    return a.reshape(S // dil, dil * W)


def _dil_fwd(qkv, g):
    _, dil = DIL_PATTERNS[g]
    S = qkv.shape[0]
    L = S // dil
    nb = L // DIL_BLOCK
    slopes = _dil_slopes(g)

    bb = min(DIL_STEP_BLOCKS, nb)
    rows = bb * DIL_BLOCK

    def body(cur_ref, prev_ref, o_ref, lse_ref):
        n = pl.program_id(1)
        ok, in_prev, dist = _dil_bias2(dil)
        for b in range(bb):
            if b == 0:
                both = jnp.concatenate([prev_ref[...], cur_ref[0:DIL_BLOCK, :]], axis=0)
                ok_b = ok & (~in_prev | (n > 0))
            else:
                both = cur_ref[(b - 1) * DIL_BLOCK:(b + 1) * DIL_BLOCK, :]
                ok_b = ok
            for hh in range(DIL_HPG):
                q, _, _ = _dil_head(both[DIL_BLOCK:], hh)
                _, k2, v2 = _dil_head(both, hh)
                s = _dil_s(q, k2, slopes[hh], dist, ok_b)
                m = jnp.max(s, axis=1, keepdims=True)
                p = jnp.exp(s - m)
                l = jnp.sum(p, axis=1, keepdims=True)
                o = lax.dot_general(p.astype(BF), v2, NN, preferred_element_type=F32) / l
                rs, sl = slice(b * DIL_BLOCK, (b + 1) * DIL_BLOCK), slice(hh * HEAD, (hh + 1) * HEAD)
                o_ref[rs, sl] = o
                lse_ref[rs, sl] = jnp.broadcast_to(m + jnp.log(l), (DIL_BLOCK, HEAD))

    ospec = pl.BlockSpec((rows, DIL_O), lambda r, n: (n, r))
    o, lse = pl.pallas_call(
        body, name=f"dil_fwd{g}", grid=(dil, nb // bb),
        in_specs=[pl.BlockSpec((rows, DIL_W), lambda r, n: (n, r)),
                  pl.BlockSpec((DIL_BLOCK, DIL_W), lambda r, n: (jnp.maximum(n * bb - 1, 0), r))],
        out_specs=[ospec, ospec],
        out_shape=[jax.ShapeDtypeStruct((L, dil * DIL_O), F32), jax.ShapeDtypeStruct((L, dil * DIL_O), F32)],
        compiler_params=_params("parallel", "parallel"))(_dil_view(qkv, dil), _dil_view(qkv, dil))
    return o.reshape(S, DIL_O), lse.reshape(S, DIL_O)


def _dil_combine(os_, lses):
    S = os_[0].shape[0]
    ts = _tile(S, 512)

    def body(o0, o1, o2, l0, l1, l2, out_ref, lse_ref):
        a, b, c = l0[...], l1[...], l2[...]
        m = jnp.maximum(jnp.maximum(a, b), c)
        ea, eb, ec = jnp.exp(a - m), jnp.exp(b - m), jnp.exp(c - m)
        tot = ea + eb + ec
        out_ref[...] = ((ea * o0[...] + eb * o1[...] + ec * o2[...]) / tot).astype(BF)
        lse_ref[...] = m + jnp.log(tot)

    return pl.pallas_call(
        body, name="dil_combine", grid=(S // ts,), in_specs=[_row(ts, DIL_O)] * 6,
        out_specs=[_row(ts, DIL_O), _row(ts, DIL_O)],
        out_shape=[jax.ShapeDtypeStruct((S, DIL_O), BF), jax.ShapeDtypeStruct((S, DIL_O), F32)],
        compiler_params=_params("parallel"))(*os_, *lses)


def _dil_delta(do, out):
    S = do.shape[0]
    ts = _tile(S, 512)

    def body(do_ref, o_ref, d_ref):
        for hh in range(DIL_HPG):
            sl = slice(hh * HEAD, (hh + 1) * HEAD)
            d = jnp.sum(do_ref[:, sl].astype(F32) * o_ref[:, sl].astype(F32), axis=1, keepdims=True)
            d_ref[:, sl] = jnp.broadcast_to(d, (ts, HEAD))

    return pl.pallas_call(
        body, name="dil_delta", grid=(S // ts,), in_specs=[_row(ts, DIL_O)] * 2, out_specs=_row(ts, DIL_O),
        out_shape=jax.ShapeDtypeStruct((S, DIL_O), F32), compiler_params=_params("parallel"))(do, out)


def _dil_bwd(qkv, do, lse, delta, g):
    _, dil = DIL_PATTERNS[g]
    S = qkv.shape[0]
    L = S // dil
    nb = L // DIL_BLOCK
    slopes = _dil_slopes(g)

    def pair(q, k, v, do_h, lse_h, delta_h, slope, dist, ok):
        s = _dil_s(q, k, slope, dist, ok)
        p = jnp.exp(s - lse_h)
        dp = lax.dot_general(do_h, v, NT, preferred_element_type=F32)
        ds = (p * (dp - delta_h) * DIL_SCALE).astype(BF)
        return p.astype(BF), ds

    bb = min(DIL_STEP_BLOCKS, nb)
    rows = bb * DIL_BLOCK
    steps = nb // bb

    def body(cur_ref, prev_ref, next_ref, doc_ref, don_ref, lsec_ref, lsen_ref, dlc_ref, dln_ref, out_ref):
        n = pl.program_id(1)
        ok2, in_prev, dist2 = _dil_bias2(dil)
        _, ok_p0, _, dist_p = _dil_bias(dil)
        for b in range(bb):
            rs = slice(b * DIL_BLOCK, (b + 1) * DIL_BLOCK)
            rn = slice((b + 1) * DIL_BLOCK, (b + 2) * DIL_BLOCK)
            two = slice(b * DIL_BLOCK, (b + 2) * DIL_BLOCK)
            first, last = b == 0, b == bb - 1
            if first:
                keys = jnp.concatenate([prev_ref[...], cur_ref[rs, :]], axis=0)
                ok_ab = ok2 & (~in_prev | (n > 0))
            else:
                keys = cur_ref[(b - 1) * DIL_BLOCK:(b + 1) * DIL_BLOCK, :]
                ok_ab = ok2
            qrows = jnp.concatenate([cur_ref[rs, :], next_ref[...]], axis=0) if last else cur_ref[two, :]
            ok_n = ok_p0 & (n < steps - 1) if last else ok_p0
            for hh in range(DIL_HPG):
                sl = slice(hh * HEAD, (hh + 1) * HEAD)
                q2, _, _ = _dil_head(qrows, hh)
                _, k2, v2 = _dil_head(keys, hh)
                q, qn, kc, vc = q2[:DIL_BLOCK], q2[DIL_BLOCK:], k2[DIL_BLOCK:], v2[DIL_BLOCK:]
                do2 = jnp.concatenate([doc_ref[rs, sl], don_ref[:, sl]], axis=0) if last else doc_ref[two, sl]
                do_c, do_n = do2[:DIL_BLOCK], do2[DIL_BLOCK:]
                lse_c = lsec_ref[rs, sl][:, 0:1]
                lse_n = (lsen_ref[:, sl] if last else lsec_ref[rn, sl])[:, 0:1]
                dl_c = dlc_ref[rs, sl][:, 0:1]
                dl_n = (dln_ref[:, sl] if last else dlc_ref[rn, sl])[:, 0:1]
                p_ab, ds_ab = pair(q, k2, v2, do_c, lse_c, dl_c, slopes[hh], dist2, ok_ab)
                p_n, ds_n = pair(qn, kc, vc, do_n, lse_n, dl_n, slopes[hh], dist_p, ok_n)
                dq = lax.dot_general(ds_ab, k2, NN, preferred_element_type=F32)
                dk = lax.dot_general(jnp.concatenate([ds_ab[:, DIL_BLOCK:], ds_n], axis=0), q2, TN, preferred_element_type=F32)
                dv = lax.dot_general(jnp.concatenate([p_ab[:, DIL_BLOCK:], p_n], axis=0), do2, TN, preferred_element_type=F32)
                out_ref[rs, sl] = dq.astype(BF)
                out_ref[rs, DIL_O + hh * HEAD:DIL_O + (hh + 1) * HEAD] = dk.astype(BF)
                out_ref[rs, 2 * DIL_O + hh * HEAD:2 * DIL_O + (hh + 1) * HEAD] = dv.astype(BF)

    cur_w = pl.BlockSpec((rows, DIL_W), lambda r, n: (n, r))
    prev_w = pl.BlockSpec((DIL_BLOCK, DIL_W), lambda r, n: (jnp.maximum(n * bb - 1, 0), r))
    next_w = pl.BlockSpec((DIL_BLOCK, DIL_W), lambda r, n: (jnp.minimum((n + 1) * bb, nb - 1), r))
    cur_o = pl.BlockSpec((rows, DIL_O), lambda r, n: (n, r))
    next_o = pl.BlockSpec((DIL_BLOCK, DIL_O), lambda r, n: (jnp.minimum((n + 1) * bb, nb - 1), r))
    qv, dov, lsev, dlv = _dil_view(qkv, dil), _dil_view(do, dil), _dil_view(lse, dil), _dil_view(delta, dil)
    out = pl.pallas_call(
        body, name=f"dil_bwd{g}", grid=(dil, steps),
        in_specs=[cur_w, prev_w, next_w, cur_o, next_o, cur_o, next_o, cur_o, next_o],
        out_specs=cur_w, out_shape=jax.ShapeDtypeStruct((L, dil * DIL_W), BF),
        compiler_params=_params("parallel", "parallel"))(qv, qv, qv, dov, dov, lsev, lsev, dlv, dlv)
    return out.reshape(S, DIL_W)


def _adamw(name, w, g, m, v):
    R, C = w.shape
    tr, tc = _adamw_block(R, C)

    def body(w_ref, g_ref, m_ref, v_ref, go_ref, d_ref, nm_ref, nv_ref):
        gv = g_ref[...]
        go_ref[...] = gv
        nm = ADAM_B1 * m_ref[...] + (1.0 - ADAM_B1) * gv
        nv = ADAM_B2 * v_ref[...] + (1.0 - ADAM_B2) * (gv * gv)
        m_hat = nm / (1.0 - ADAM_B1 ** ADAM_STEP)
        v_hat = nv / (1.0 - ADAM_B2 ** ADAM_STEP)
        d_ref[...] = -ADAM_LR * (m_hat / (jnp.sqrt(v_hat) + ADAM_EPS) + ADAM_WD * w_ref[...])
        nm_ref[...] = nm
        nv_ref[...] = nv

    spec = pl.BlockSpec((tr, tc), lambda i, j: (i, j))
    shp = jax.ShapeDtypeStruct((R, C), F32)
    return pl.pallas_call(
        body, name=name, grid=(R // tr, C // tc), in_specs=[spec] * 4, out_specs=[spec] * 4, out_shape=[shp] * 4,
        compiler_params=_params("parallel", "parallel"))(w, g, m, v)


ADAMW_BLOCK_ELEMS = 640 * 1024


def _adamw_block(R, C):
    if R * C <= ADAMW_BLOCK_ELEMS:
        return R, C
    tr = _tile_rows(R, max(8, ADAMW_BLOCK_ELEMS // C))
    tc = _tile(C, max(LANE, ADAMW_BLOCK_ELEMS // R))
    if tr * C >= R * tc or R * tc > ADAMW_BLOCK_ELEMS:
        return tr, C
    return R, tc


def _tile_rows(n, pref, mult=8):
    t = (pref // mult) * mult
    while t >= mult:
        if n % t == 0:
            return t
        t -= mult
    return n


ANY = pl.BlockSpec(memory_space=pl.ANY)


def _place():
    x, y, c = lax.axis_index("x"), lax.axis_index("y"), lax.axis_index("c")
    chips = [(1 - x, y), (x, 1 - y), (1 - x, 1 - y)]
    chip_idx = [2 * cx + cy for cx, cy in chips]
    return x, y, c, 2 * x + y, chips, chip_idx


def _rcopy(src, dst, ssem, rsem, dev):
    return pltpu.make_async_remote_copy(src_ref=src, dst_ref=dst, send_sem=ssem, recv_sem=rsem,
                                        device_id=dev, device_id_type=MESH)


HBM = pl.BlockSpec(memory_space=pltpu.HBM)
SEM = pl.BlockSpec(memory_space=pltpu.SEMAPHORE)
EFFECT = pltpu.SideEffectType.DATAFLOW_SIDE_EFFECTING


def _split_copies(kind, srcs, lands, ssem, rsem):
    _, _, c, me, chips, chip_idx = _place()
    cps = []
    for i in range(len(srcs)):
        for k in range(3):
            if kind == "gather":
                rows = srcs[i].shape[0]
                if rows == lands[i].shape[1]:
                    src, dst = srcs[i], lands[i].at[me]
                else:
                    src, dst = srcs[i], lands[i].at[me, pl.ds(pl.multiple_of(c * rows, 16), rows)]
            else:
                src, dst = srcs[i].at[chip_idx[k]], lands[i].at[k]
            cps.append(_rcopy(src, dst, ssem.at[3 * i + k], rsem.at[3 * i + k], (*chips[k], c)))
    return cps


def _exchange_start(name, kind, srcs, lands, groups):
    n, ng = len(srcs), len(groups)

    def body(*refs):
        src_refs, land_refs = refs[:n], refs[n:2 * n]
        sems = refs[2 * n:2 * n + 2 * ng]
        token = refs[-1]
        for gi, grp in enumerate(groups):
            cps = _split_copies(kind, [src_refs[i] for i in grp], [land_refs[i] for i in grp], sems[2 * gi], sems[2 * gi + 1])
            for cp in cps:
                cp.start()
        token[...] = jnp.zeros_like(token)

    arrays = list(srcs) + list(lands)
    out_shape = []
    for grp in groups:
        out_shape += [pltpu.SemaphoreType.DMA((3 * len(grp),)), pltpu.SemaphoreType.DMA((3 * len(grp),))]
    out_shape += [pltpu.HBM(a.shape, a.dtype) for a in arrays] + [jax.ShapeDtypeStruct((8, LANE), F32)]
    outs = pl.pallas_call(
        body, name=name, out_shape=out_shape, in_specs=[HBM] * (2 * n),
        out_specs=[SEM] * (2 * ng) + [HBM] * (2 * n) + [pl.BlockSpec(memory_space=pltpu.VMEM)],
        input_output_aliases={i: 2 * ng + i for i in range(2 * n)},
        compiler_params=pltpu.CompilerParams(has_side_effects=EFFECT),
    )(*[pltpu.with_memory_space_constraint(a, pltpu.HBM) for a in arrays])
    sems = [(outs[2 * gi], outs[2 * gi + 1]) for gi in range(ng)]
    thru = outs[2 * ng:2 * ng + 2 * n]
    return sems, thru[:n], thru[n:], outs[-1]


def _exchange_wait(name, kind, srcs, lands, sems, after):
    n = len(srcs)

    def body(*refs):
        cps = _split_copies(kind, refs[:n], refs[n:2 * n], refs[2 * n], refs[2 * n + 1])
        for cp in cps:
            cp.wait_send()
            cp.wait_recv()

    arrays = list(srcs) + list(lands)
    outs = pl.pallas_call(
        body, name=name, out_shape=[pltpu.HBM(a.shape, a.dtype) for a in arrays],
        in_specs=[HBM] * (2 * n) + [SEM, SEM, ANY], out_specs=[HBM] * (2 * n),
        input_output_aliases={i: i for i in range(2 * n)},
        compiler_params=pltpu.CompilerParams(has_side_effects=EFFECT),
    )(*arrays, sems[0], sems[1], after)
    return outs[:n], outs[n:]


EXCHANGE_CHUNK_BYTES = 3 * 1024 * 1024


def _half_geometry(R, C, axis):
    Rp, Cp = (R // 2, C) if axis == 0 else (R, C // 2)
    tr = _tile_rows(Rp, max(16, EXCHANGE_CHUNK_BYTES // (2 * Cp)), 16)
    return Rp, Cp, tr, Rp // tr


def _pair_sum(name, g, axis):
    G, R, C = g.shape
    Rp, Cp, tr, nb = _half_geometry(R, C, axis)
    steps = G * nb

    def half_block(s, b, h):
        return (s, h * nb + b, 0) if axis == 0 else (s, b, h)

    def body(c_ref, keep_ref, give_ref, out_ref, land, ssem, rsem, credit):
        x, y, c = lax.axis_index("x"), lax.axis_index("y"), lax.axis_index("c")
        sib = (x, y, 1 - c)
        t = pl.program_id(0) * nb + pl.program_id(1)
        slot = t % 2

        @pl.when(t >= 2)
        def _():
            pl.semaphore_wait(credit, 1)

        cp = _rcopy(give_ref.at[0], land.at[slot], ssem.at[slot], rsem.at[slot], sib)
        cp.start()
        cp.wait_recv()
        out_ref[...] = (keep_ref[...].astype(F32) + land[slot].astype(F32)).astype(BF)

        @pl.when(t + 2 < steps)
        def _():
            pl.semaphore_signal(credit, 1, device_id=sib, device_id_type=MESH)

        cp.wait_send()

    blk = (None, tr, Cp)
    grid_spec = pltpu.PrefetchScalarGridSpec(
        num_scalar_prefetch=1, grid=(G, nb),
        in_specs=[pl.BlockSpec(blk, lambda s, b, c_ref: half_block(s, b, c_ref[0])),
                  pl.BlockSpec((1, tr, Cp), lambda s, b, c_ref: half_block(s, b, 1 - c_ref[0]))],
        out_specs=pl.BlockSpec(blk, lambda s, b, c_ref: (s, b, 0)),
        scratch_shapes=[pltpu.VMEM((2, tr, Cp), BF), pltpu.SemaphoreType.DMA((2,)), pltpu.SemaphoreType.DMA((2,)),
                        pltpu.SemaphoreType.REGULAR])
    c_arr = lax.axis_index("c").astype(jnp.int32).reshape(1)
    return pl.pallas_call(
        body, name=name, grid_spec=grid_spec, out_shape=jax.ShapeDtypeStruct((G, Rp, Cp), BF),
        compiler_params=_params("arbitrary", "arbitrary"))(c_arr, g, g)


def _chip_total_join(name, h, landed, axis):
    G, Rp, Cp = h.shape
    R, C = (2 * Rp, Cp) if axis == 0 else (Rp, 2 * Cp)
    tr = _tile_rows(Rp, max(16, EXCHANGE_CHUNK_BYTES // (4 * Cp)), 16)
    nb = Rp // tr

    def body(me_ref, own_ref, l0_ref, l1_ref, l2_ref, full, stage, ssem, rsem, lsem):
        x, y, c = lax.axis_index("x"), lax.axis_index("y"), lax.axis_index("c")
        sib = (x, y, 1 - c)
        b = pl.program_id(0)

        def place(half, r0, rows):
            if axis == 0:
                return full.at[pl.ds(pl.multiple_of(half * Rp + r0, 8), rows), :]
            return full.at[pl.ds(pl.multiple_of(r0, 8), rows), pl.ds(pl.multiple_of(half * Cp, LANE), Cp)]

        def copies(step):
            s = step % 2
            mine = place(c, step * tr, tr)
            return pltpu.make_async_copy(stage.at[s], mine, lsem.at[s]), _rcopy(stage.at[s], mine, ssem.at[s], rsem, sib)

        @pl.when(b >= 2)
        def _():
            loc, rem = copies(b - 2)
            loc.wait()
            rem.wait_send()

        acc = own_ref[...].astype(F32)
        for r in (l0_ref, l1_ref, l2_ref):
            acc = acc + r[...].astype(F32)
        stage[b % 2] = acc
        loc, rem = copies(b)
        loc.start()
        rem.start()

        @pl.when(b == nb - 1)
        def _():
            for step in range(max(0, nb - 2), nb):
                loc, rem = copies(step)
                loc.wait()
                rem.wait_send()
            theirs = place(1 - c, 0, Rp)
            _rcopy(theirs, theirs, ssem.at[0], rsem, sib).wait_recv()

    blk = (None, tr, Cp)
    grid_spec = pltpu.PrefetchScalarGridSpec(
        num_scalar_prefetch=1, grid=(nb,),
        in_specs=[pl.BlockSpec(blk, lambda b, me_ref: (me_ref[0], b, 0))]
        + [pl.BlockSpec(blk, functools.partial(lambda b, me_ref, k: (k, b, 0), k=k)) for k in range(3)],
        out_specs=ANY,
        scratch_shapes=[pltpu.VMEM((2, tr, Cp), F32), pltpu.SemaphoreType.DMA((2,)), pltpu.SemaphoreType.DMA,
                        pltpu.SemaphoreType.DMA((2,))])
    me = (2 * lax.axis_index("x") + lax.axis_index("y")).astype(jnp.int32).reshape(1)
    return pl.pallas_call(
        body, name=name, grid_spec=grid_spec, out_shape=jax.ShapeDtypeStruct((R, C), F32),
        compiler_params=_params("arbitrary"))(me, h, landed, landed, landed)


def _pair_share(name, land):
    G, R, C = land.shape
    Rh = R // 2
    tr = _tile_rows(Rh, max(16, EXCHANGE_CHUNK_BYTES // (2 * C)), 16)
    chunks = [(k, b) for k in range(3) for b in range(Rh // tr)]

    def body(src, dst, buf, lsem, ssem, rsem):
        x, y, c, _, _, chip_idx = _place()
        sib = (x, y, 1 - c)

        def region(ref, k, half, r0, rows):
            return ref.at[chip_idx[k], pl.ds(pl.multiple_of(half * Rh + r0, 16), rows)]

        def load(t):
            k, b = chunks[t]
            return pltpu.make_async_copy(region(src, k, c, b * tr, tr), buf.at[t % 2], lsem.at[t % 2])

        def send(t):
            k, b = chunks[t]
            return _rcopy(buf.at[t % 2], region(dst, k, c, b * tr, tr), ssem.at[t % 2], rsem.at[k], sib)

        load(0).start()
        for t in range(len(chunks)):
            load(t).wait()
            if t + 1 < len(chunks):
                if t >= 1:
                    send(t - 1).wait_send()
                load(t + 1).start()
            send(t).start()
        for t in range(max(0, len(chunks) - 2), len(chunks)):
            send(t).wait_send()
        for k in range(3):
            theirs = region(dst, k, 1 - c, 0, Rh)
            _rcopy(theirs, theirs, ssem.at[0], rsem.at[k], sib).wait_recv()

    return pl.pallas_call(
        body, name=name, in_specs=[ANY], out_specs=ANY, out_shape=jax.ShapeDtypeStruct(land.shape, land.dtype),
        input_output_aliases={0: 0},
        scratch_shapes=[pltpu.VMEM((2, tr, C), land.dtype), pltpu.SemaphoreType.DMA((2,)), pltpu.SemaphoreType.DMA((2,)),
                        pltpu.SemaphoreType.DMA((3,))],
    )(land)


def _allreduce_small(v):
    R, K = v.shape
    ndev = 8

    def body(v_ref, o_ref, land, ssem, rsem):
        x, y, c = lax.axis_index("x"), lax.axis_index("y"), lax.axis_index("c")
        me = 4 * x + 2 * y + c
        land[me] = v_ref[...]
        cps = []
        for r in range(1, ndev):
            fx, fy, fc = (r >> 2) & 1, (r >> 1) & 1, r & 1
            peer = (x ^ fx, y ^ fy, c ^ fc)
            cp = _rcopy(v_ref, land.at[me], ssem.at[r - 1], rsem.at[r - 1], peer)
            cp.start()
            cps.append((cp, 4 * peer[0] + 2 * peer[1] + peer[2], r))
        for cp, src, r in cps:
            cp.wait_send()
            _rcopy(v_ref, land.at[src], ssem.at[r - 1], rsem.at[r - 1], (x, y, c)).wait_recv()
        acc = land[0]
        for d in range(1, ndev):
            acc = acc + land[d]
        o_ref[...] = acc

    vm = pl.BlockSpec(memory_space=pltpu.VMEM)
    return pl.pallas_call(
        body, name="allreduce_small", in_specs=[vm], out_specs=vm, out_shape=jax.ShapeDtypeStruct((R, K), F32),
        scratch_shapes=[pltpu.VMEM((ndev, R, K), F32), pltpu.SemaphoreType.DMA((ndev - 1,)), pltpu.SemaphoreType.DMA((ndev - 1,))],
    )(v)


IN_SPLITS = (Q_RANK, KV_RANK, QK_ROPE, DIL_HEADS * HEAD, DIL_HEADS * HEAD, DIL_HEADS * HEAD, D_MODEL, D_MODEL)
IN_OFF = tuple(int(v) for v in np.cumsum((0,) + IN_SPLITS))


def _unshard_cols(g):
    G, K, Ns = g.shape
    return g.transpose(1, 0, 2).reshape(K, G * Ns)


def _shard_cols(w):
    K, N = w.shape
    return w.reshape(K, N_CHIPS, N // N_CHIPS).transpose(1, 0, 2)


def _rope_pad(w):
    half = QK_ROPE // 2
    z = jnp.zeros(w.shape[:-1] + (half,), w.dtype)
    return jnp.concatenate([w[..., :half], z, w[..., half:], z], axis=-1)


def _rope_unpad(w):
    half = QK_ROPE // 2
    return jnp.concatenate([w[..., :half], w[..., 2 * half:3 * half]], axis=-1)


def _split_w_in(w_in_g):
    G, K, Ns = w_in_g.shape

    def cols(lo, hi):
        pieces = [w_in_g[k][:, max(lo, k * Ns) - k * Ns:min(hi, (k + 1) * Ns) - k * Ns]
                  for k in range(G) if max(lo, k * Ns) < min(hi, (k + 1) * Ns)]
        return pieces[0] if len(pieces) == 1 else jnp.concatenate(pieces, axis=1)

    p = [(IN_OFF[i], IN_OFF[i + 1]) for i in range(8)]
    w_lat = jnp.concatenate([cols(*p[0]), cols(*p[1]), _rope_pad(cols(*p[2])),
                             jnp.zeros((K, LAT_W - _KPE.stop), w_in_g.dtype)], axis=1)
    w_dil = [jnp.concatenate([cols(p[3 + t][0] + g * DIL_O, p[3 + t][0] + (g + 1) * DIL_O) for t in range(3)], axis=1)
             for g in range(DIL_GROUPS)]
    w_gate = cols(p[6][0], p[7][1])
    return w_lat, w_dil, w_gate


def _merge_dw_in(dw_lat, dw_dil, dw_gate):
    parts = [dw_lat[:, _CQ], dw_lat[:, _CKV], _rope_unpad(dw_lat[:, _KPE])]
    for t in range(3):
        parts += [dw_dil[g][:, t * DIL_O:(t + 1) * DIL_O] for g in range(DIL_GROUPS)]
    parts.append(dw_gate)
    width = sum(p.shape[1] for p in parts) // N_CHIPS
    shards = []
    for k in range(N_CHIPS):
        pieces, at = [], 0
        for p in parts:
            lo, hi = max(k * width, at), min((k + 1) * width, at + p.shape[1])
            if lo < hi:
                pieces.append(p[:, lo - at:hi - at])
            at += p.shape[1]
        shards.append(jnp.concatenate(pieces, axis=1))
    return jnp.stack(shards)


def _split_w_uq(w_uq_g):
    w = _unshard_cols(w_uq_g)
    K = w.shape[0]
    w = w.reshape(K, MLA_HEADS, QK_NOPE + QK_ROPE)
    return w[:, :, :QK_NOPE].reshape(K, MLA_HEADS * HEAD), _rope_pad(w[:, :, QK_NOPE:]).reshape(K, MLA_HEADS * HEAD)


def _merge_dw_uq(dw_n, dw_p):
    K = dw_n.shape[0]
    w = jnp.concatenate([dw_n.reshape(K, MLA_HEADS, HEAD), _rope_unpad(dw_p.reshape(K, MLA_HEADS, HEAD))], axis=-1)
    return _shard_cols(w.reshape(K, MLA_HEADS * (QK_NOPE + QK_ROPE)))


def _split_w_ukv(w_ukv_g):
    w = _unshard_cols(w_ukv_g)
    K = w.shape[0]
    w = w.reshape(K, MLA_HEADS, 2 * HEAD)
    return w[:, :, :HEAD].reshape(K, MLA_HEADS * HEAD), w[:, :, HEAD:].reshape(K, MLA_HEADS * HEAD)


def _merge_dw_ukv(dw_k, dw_v):
    K = dw_k.shape[0]
    w = jnp.concatenate([dw_k.reshape(K, MLA_HEADS, HEAD), dw_v.reshape(K, MLA_HEADS, HEAD)], axis=-1)
    return _shard_cols(w.reshape(K, MLA_HEADS * 2 * HEAD))


GATHER_GROUPS = (("w_in",), ("w_uq", "w_ukv", "w_o_mla", "w_o_dil", "w_out"), ("w_up", "w_down", "conv_w"))
SHARED_FETCH = ("w_in",)
REDUCE_GROUPS = (("w_down", "w_up"), ("w_out", "w_o_mla", "w_o_dil"), ("w_uq", "w_ukv", "w_in"))


def _local_step(x, tgt, W, fetch, emit):
    S, D = x.shape
    cos, sin_s = _rope_tables(S)
    w_lat, w_dil, w_gate = _split_w_in(fetch(0, x)["w_in"])

    h = _rmsnorm_fwd("attn_norm", x, W["attn_norm_g"])
    lat = _mm_nn("proj_lat", h, w_lat)
    qkv = [_mm_nn(f"proj_dil{g}", h, w_dil[g], o_dtype=BF) for g in range(DIL_GROUPS)]
    gpre = _mm_nn("proj_gate", h, w_gate, o_dtype=BF)
    WB = fetch(1, gpre)
    w_uqn, w_uqp = _split_w_uq(WB["w_uq"])
    w_k, w_v = _split_w_ukv(WB["w_ukv"])
    w_o_mla, w_o_dil = WB["w_o_mla"], WB["w_o_dil"]
    w_out = WB["w_out"].reshape(D, D)
    qn_, kvn, kpe = _mla_prep(lat, W["q_norm_g"], W["kv_norm_g"], cos, sin_s)
    q_nope = _mm_nn("q_nope", qn_, w_uqn, o_dtype=BF)
    q_pe = _rope("q_rope", _mm_nn("q_pe", qn_, w_uqp), cos, sin_s, False)
    k_nope = _mm_nn("k_nope", kvn, w_k, o_dtype=BF)
    v_mla = _mm_nn("v_mla", kvn, w_v, o_dtype=BF)
    attn_a, lse_a = _mla_fwd(q_nope, q_pe, k_nope, kpe, v_mla)
    dil = [_dil_fwd(qkv[g], g) for g in range(DIL_GROUPS)]
    attn_b, lse_b = _dil_combine([o for o, _ in dil], [l for _, l in dil])
    o_a = _mm_nn("o_mla", attn_a, w_o_mla, o_dtype=BF)
    o_b = _mm_nn("o_dil", attn_b, w_o_dil, o_dtype=BF)
    merge = _merge_fwd(gpre, W["b_gate"], o_a, o_b)
    x1 = _mm_nn("out_proj", merge, w_out, add=x)
    WC = fetch(2, merge)
    w_up = WC["w_up"]
    G4, _, C = w_up.shape
    w_down = WC["w_down"].reshape(G4 // 2, C, D)
    conv_w = WC["conv_w"]
    conv_b = W["conv_b"].reshape(G4, 1, C)
    h2 = _rmsnorm_fwd("ffn_norm", x1, W["ffn_norm_g"])
    u_pre = _up_fwd(h2, w_up)
    act, u = _ffn_act(u_pre, conv_w, conv_b)
    x2 = _down_fwd(act, w_down, x1)
    dx2, d_final_g, loss8 = _final_loss(x2, tgt, W["final_norm_g"])

    d_act = _down_dgrad(dx2, w_down)
    dw_down = _down_wgrad(act, dx2)
    du_pre, d_conv_w, d_conv_b = _ffn_act_conv_bwd(u, d_act, u_pre, conv_w)
    dh2 = _up_dgrad(du_pre, w_up)
    dw_up = _up_wgrad(h2, du_pre)
    zero = emit(0, {"w_down": dw_down.reshape(N_CHIPS, (G4 // 2) * C // N_CHIPS, D), "w_up": dw_up})
    dx1, d_ffn_g = _rmsnorm_bwd("ffn_norm_bwd", dh2, x1, W["ffn_norm_g"] + zero, dx2)
    d_merge = _mm_nt("out_proj_dgrad", dx1, w_out, o_dtype=BF)
    dw_out = _mm_tn("out_proj_wgrad", merge, dx1)
    d_oa, d_ob, d_gpre, d_b_gate = _merge_bwd(d_merge, gpre, W["b_gate"], o_a, o_b)
    d_attn_a = _mm_nt("o_mla_dgrad", d_oa, w_o_mla, o_dtype=BF)
    dw_o_mla = _mm_tn("o_mla_wgrad", attn_a, d_oa, shards=N_CHIPS)
    d_attn_b = _mm_nt("o_dil_dgrad", d_ob, w_o_dil, o_dtype=BF)
    dw_o_dil = _mm_tn("o_dil_wgrad", attn_b, d_ob, shards=N_CHIPS)
    zero = emit(1, {"w_out": dw_out.reshape(N_CHIPS, D // N_CHIPS, D), "w_o_mla": dw_o_mla, "w_o_dil": dw_o_dil})
    q_norm_g = W["q_norm_g"] + zero
    delta_b = _dil_delta(d_attn_b, attn_b)
    d_qkv = [_dil_bwd(qkv[g], d_attn_b, lse_b, delta_b, g) for g in range(DIL_GROUPS)]
    dq_nope, dq_pe_rot, dk_nope, dv_mla, dkpe_rot = _mla_bwd(q_nope, q_pe, k_nope, kpe, v_mla, d_attn_a, attn_a, lse_a)
    dq_pe = _rope("q_rope_bwd", dq_pe_rot, cos, sin_s, True)
    d_qn = _mm_nt("q_pe_dgrad", dq_pe, w_uqp, add=_mm_nt("q_nope_dgrad", dq_nope, w_uqn))
    d_kvn = _mm_nt("v_dgrad", dv_mla, w_v, add=_mm_nt("k_nope_dgrad", dk_nope, w_k))
    dw_uq = _merge_dw_uq(_mm_tn("q_nope_wgrad", qn_, dq_nope), _mm_tn("q_pe_wgrad", qn_, dq_pe))
    dw_ukv = _merge_dw_ukv(_mm_tn("k_nope_wgrad", kvn, dk_nope), _mm_tn("v_wgrad", kvn, dv_mla))
    d_lat, d_q_g, d_kv_g = _mla_prep_bwd(lat, q_norm_g, W["kv_norm_g"], cos, sin_s, d_qn, d_kvn, dkpe_rot)
    dw_in = _merge_dw_in(_mm_tn("proj_lat_wgrad", h, d_lat),
                         [_mm_tn(f"proj_dil{g}_wgrad", h, d_qkv[g]) for g in range(DIL_GROUPS)],
                         _mm_tn("proj_gate_wgrad", h, d_gpre))
    zero = emit(2, {"w_uq": dw_uq, "w_ukv": dw_ukv, "w_in": dw_in})
    dh = _mm_nt("proj_lat_dgrad", d_lat, w_lat + zero.astype(BF))
    for g in range(DIL_GROUPS):
        dh = _mm_nt(f"proj_dil{g}_dgrad", d_qkv[g], w_dil[g], add=dh)
    dh = _mm_nt("proj_gate_dgrad", d_gpre, w_gate, add=dh)
    grad_x, d_attn_g = _rmsnorm_bwd("attn_norm_bwd", dh, x, W["attn_norm_g"], dx1)

    small = {"attn_norm_g": d_attn_g, "b_gate": d_b_gate, "q_norm_g": d_q_g, "kv_norm_g": d_kv_g,
             "ffn_norm_g": d_ffn_g, "conv_w": d_conv_w, "conv_b": d_conv_b.reshape(1, G4 * C),
             "final_norm_g": d_final_g}
    return loss8[0, 0], grad_x, small


BIG = ("w_in", "w_uq", "w_ukv", "w_o_mla", "w_o_dil", "w_out", "w_up", "w_down")
SMALL = ("attn_norm_g", "b_gate", "q_norm_g", "kv_norm_g", "ffn_norm_g", "conv_w", "conv_b", "final_norm_g")
WEIGHTS = ("attn_norm_g", "w_in", "b_gate", "q_norm_g", "w_uq", "kv_norm_g", "w_ukv", "w_o_mla", "w_o_dil",
           "w_out", "ffn_norm_g", "w_up", "conv_w", "conv_b", "w_down", "final_norm_g")
SMALL_ROWS = 8
COLUMN_MAJOR = ("w_in", "w_up")
HALF_AXIS = {"w_down": 1}


def _gather_start(shards):
    chip = 2 * lax.axis_index("x") + lax.axis_index("y")
    c = lax.axis_index("c")

    def prepare(names, zero):
        srcs, lands = [], []
        for n in names:
            s = shards[n] + zero
            s = s if n == "conv_w" else s.astype(BF)
            lands.append(lax.dynamic_update_slice(lax.empty((N_CHIPS,) + s.shape, s.dtype), s[None], (chip, 0, 0)))
            if n in SHARED_FETCH:
                s = lax.dynamic_slice_in_dim(s, c * (s.shape[0] // 2), s.shape[0] // 2, 0)
            srcs.append(s)
        return srcs, lands

    n0 = len(GATHER_GROUPS[0])
    srcs0, lands0 = prepare(GATHER_GROUPS[0], 0.0)
    sems0, srcs0, lands0, token = _exchange_start("gather_start0", "gather", srcs0, lands0, [list(range(n0))])
    srcs, lands = prepare([n for grp in GATHER_GROUPS[1:] for n in grp], token[0, 0])
    groups, at = [], 0
    for grp in GATHER_GROUPS[1:]:
        groups.append(list(range(at, at + len(grp))))
        at += len(grp)
    sems, srcs, lands, token1 = _exchange_start("gather_start1", "gather", srcs, lands, groups)

    def fetch(i, after):
        if i == 0:
            _, got = _exchange_wait("gather_wait0", "gather", srcs0, lands0, sems0[0], token1)
        else:
            idx = groups[i - 1]
            _, got = _exchange_wait(f"gather_wait{i}", "gather", [srcs[j] for j in idx], [lands[j] for j in idx],
                                    sems[i - 1], after)
        return {n: _pair_share(f"pair_share_{n}", g) if n in SHARED_FETCH else g for n, g in zip(GATHER_GROUPS[i], got)}

    return fetch, token[0, 0]


def _reduce_start(i, grads):
    names = REDUCE_GROUPS[i]
    hs = [_pair_sum(f"pair_sum_{n}", grads[n], HALF_AXIS.get(n, 0)) for n in names]
    lands = [lax.empty((3,) + h.shape[1:], h.dtype) for h in hs]
    sems, hs, lands, token = _exchange_start(f"reduce_start{i}", "scatter", hs, lands, [list(range(len(names)))])
    return (sems[0], hs, lands), token[0, 0]


def _reduce_finish(i, pending, after):
    sems, hs, lands = pending
    hs, lands = _exchange_wait(f"reduce_wait{i}", "scatter", hs, lands, sems, after)
    out = {}
    for n, h, landed in zip(REDUCE_GROUPS[i], hs, lands):
        out[n] = _chip_total_join(f"chip_total_{n}", h, landed, HALF_AXIS.get(n, 0))
    return out


def _reduce_small(small):
    names = tuple(small)
    flat = [small[n].reshape(-1) for n in names]
    sizes = [f.shape[0] for f in flat]
    total = sum(sizes)
    width = -(-total // (SMALL_ROWS * LANE)) * LANE
    packed = jnp.concatenate(flat + [jnp.zeros((SMALL_ROWS * width - total,), F32)]).reshape(SMALL_ROWS, width)
    red = _allreduce_small(packed).reshape(-1)
    out, off = {}, 0
    for n, s in zip(names, sizes):
        out[n] = red[off:off + s]
        off += s
    return out


def kernel(x, attn_norm_g, w_in, b_gate, q_norm_g, w_uq, kv_norm_g, w_ukv, w_o_mla, w_o_dil, w_out, ffn_norm_g, w_up, conv_w, conv_b, w_down, final_norm_g, loss_target, m_attn_norm_g, m_w_in, m_b_gate, m_q_norm_g, m_w_uq, m_kv_norm_g, m_w_ukv, m_w_o_mla, m_w_o_dil, m_w_out, m_ffn_norm_g, m_w_up, m_conv_w, m_conv_b, m_w_down, m_final_norm_g, v_attn_norm_g, v_w_in, v_b_gate, v_q_norm_g, v_w_uq, v_kv_norm_g, v_w_ukv, v_w_o_mla, v_w_o_dil, v_w_out, v_ffn_norm_g, v_w_up, v_conv_w, v_conv_b, v_w_down, v_final_norm_g):
    given = dict(attn_norm_g=attn_norm_g, w_in=w_in, b_gate=b_gate, q_norm_g=q_norm_g, w_uq=w_uq, kv_norm_g=kv_norm_g,
                 w_ukv=w_ukv, w_o_mla=w_o_mla, w_o_dil=w_o_dil, w_out=w_out, ffn_norm_g=ffn_norm_g, w_up=w_up,
                 conv_w=conv_w, conv_b=conv_b, w_down=w_down, final_norm_g=final_norm_g)
    moments_m = dict(attn_norm_g=m_attn_norm_g, w_in=m_w_in, b_gate=m_b_gate, q_norm_g=m_q_norm_g, w_uq=m_w_uq,
                     kv_norm_g=m_kv_norm_g, w_ukv=m_w_ukv, w_o_mla=m_w_o_mla, w_o_dil=m_w_o_dil, w_out=m_w_out,
                     ffn_norm_g=m_ffn_norm_g, w_up=m_w_up, conv_w=m_conv_w, conv_b=m_conv_b, w_down=m_w_down,
                     final_norm_g=m_final_norm_g)
    moments_v = dict(attn_norm_g=v_attn_norm_g, w_in=v_w_in, b_gate=v_b_gate, q_norm_g=v_q_norm_g, w_uq=v_w_uq,
                     kv_norm_g=v_kv_norm_g, w_ukv=v_w_ukv, w_o_mla=v_w_o_mla, w_o_dil=v_w_o_dil, w_out=v_w_out,
                     ffn_norm_g=v_ffn_norm_g, w_up=v_w_up, conv_w=v_conv_w, conv_b=v_conv_b, w_down=v_w_down,
                     final_norm_g=v_final_norm_g)

    fetch, zero = _gather_start({n: given[n][0] for n in BIG + ("conv_w",)})
    W = {n: given[n] for n in ("b_gate", "q_norm_g", "kv_norm_g", "ffn_norm_g", "conv_b")}
    W["attn_norm_g"] = given["attn_norm_g"] + zero
    W["final_norm_g"] = given["final_norm_g"].reshape(1, -1)

    pending = {}

    def emit(i, grads):
        pending[i], token = _reduce_start(i, grads)
        return token

    loss_part, grad_x, small = _local_step(x[0], loss_target[0], W, fetch, emit)
    small["loss"] = loss_part
    grads, delta, new_m, new_v = {}, {}, {}, {}

    def adamw(n, g):
        shp = given[n].shape
        two_d = (-1, shp[-1]) if len(shp) > 1 else (1, -1)
        view = (lambda a: a.reshape(two_d).T) if n in COLUMN_MAJOR else (lambda a: a.reshape(two_d))
        back = (lambda a: a.T.reshape(shp)) if n in COLUMN_MAJOR else (lambda a: a.reshape(shp))
        go, d, nm, nv = _adamw(f"adamw_{n}", view(given[n]), view(g), view(moments_m[n]), view(moments_v[n]))
        grads[n], delta[n], new_m[n], new_v[n] = back(go), back(d), back(nm), back(nv)

    after = grad_x
    for i in range(len(REDUCE_GROUPS)):
        for n, g in _reduce_finish(i, pending[i], after).items():
            adamw(n, g)
        after = delta[REDUCE_GROUPS[i][-1]]
    g_small = _reduce_small(small)
    loss = g_small["loss"][0]
    chip = 2 * lax.axis_index("x") + lax.axis_index("y")
    for n in SMALL:
        if n == "conv_w":
            full = g_small[n].reshape(N_CHIPS, 3, -1)
            adamw(n, lax.dynamic_index_in_dim(full, chip, 0, keepdims=True))
        else:
            adamw(n, g_small[n])

    return (loss, grad_x[None], *[grads[n] for n in WEIGHTS], *[delta[n] for n in WEIGHTS],
            *[new_m[n] for n in WEIGHTS], *[new_v[n] for n in WEIGHTS])
```

```python
import functools
import math

import numpy as np
import jax
import jax.numpy as jnp
from jax import lax
from jax.experimental import pallas as pl
from jax.experimental.pallas import tpu as pltpu

F32 = jnp.float32
BF = jnp.bfloat16
MESH = pl.DeviceIdType.MESH

D_MODEL = 2048
MLA_HEADS = 8
QK_NOPE = 128
QK_ROPE = 64
Q_RANK = 512
KV_RANK = 256
ROPE_THETA = 10000.0
DIL_PATTERNS = ((128, 1), (512, 4), (2048, 16))
DIL_GROUPS = 3
DIL_HPG = 4
DIL_HEADS = 12
HEAD = 128
DIL_BLOCK = 128
ALIBI_MAX_BIAS = 8.0
NORM_EPS = 1e-6
N_CHIPS = 4
ADAM_LR = 0.001
ADAM_B1 = 0.9
ADAM_B2 = 0.999
ADAM_EPS = 1e-08
ADAM_WD = 0.01
ADAM_STEP = 10

LANE = 128
VMEM_LIMIT = 56 * 1024 * 1024
MLA_SCALE = (QK_NOPE + QK_ROPE) ** -0.5
DIL_SCALE = HEAD ** -0.5


def _params(*sem):
    return pltpu.CompilerParams(dimension_semantics=sem, vmem_limit_bytes=VMEM_LIMIT)


def _tile(n, pref):
    t = (pref // LANE) * LANE
    while t >= LANE:
        if n % t == 0:
            return t
        t -= LANE
    return n


NN = (((1,), (0,)), ((), ()))
NT = (((1,), (1,)), ((), ()))
TN = (((0,), (0,)), ((), ()))


def _mm_call(name, a, b, add, *, grid, a_spec, b_spec, add_spec, o_spec, o_shape, o_dtype, acc_shape, dims, nk):
    nax = len(grid)

    def body(*refs):
        if add is None:
            a_ref, b_ref, o_ref = refs[:3]
            c_ref = None
            scr = refs[3:]
        else:
            a_ref, b_ref, c_ref, o_ref = refs[:4]
            scr = refs[4:]
        prod = lax.dot_general(a_ref[...].astype(BF), b_ref[...].astype(BF), dims, preferred_element_type=F32)
        if nk == 1:
            if c_ref is not None:
                prod = prod + c_ref[...]
            o_ref[...] = prod.astype(o_ref.dtype)
        else:
            acc = scr[0]
            k = pl.program_id(nax - 1)

            @pl.when(k == 0)
            def _():
                if c_ref is not None:
                    acc[...] = prod + c_ref[...]
                else:
                    acc[...] = prod

            @pl.when(k > 0)
            def _():
                acc[...] += prod

            @pl.when(k == nk - 1)
            def _():
                o_ref[...] = acc[...].astype(o_ref.dtype)

    ins = [a, b] + ([] if add is None else [add])
    specs = [a_spec, b_spec] + ([] if add is None else [add_spec])
    sem = ("parallel",) * (nax - 1) + ("arbitrary",)
    return pl.pallas_call(
        body, name=name, grid=grid, in_specs=specs, out_specs=o_spec,
        out_shape=jax.ShapeDtypeStruct(o_shape, o_dtype),
        scratch_shapes=[] if nk == 1 else [pltpu.VMEM(acc_shape, F32)],
        compiler_params=_params(*sem),
    )(*ins)


def _mm_nn(name, a, b, *, add=None, o_dtype=F32):
    M, K = a.shape
    sharded = b.ndim == 3
    Ns = b.shape[-1]
    N = Ns * (b.shape[0] if sharded else 1)
    tm, tn, tk = _tile(M, 1024), _tile(Ns, 1024), _tile(K, 2048)
    per = Ns // tn
    nk = K // tk
    if sharded:
        b_spec = pl.BlockSpec((None, tk, tn), lambda i, j, k: (j // per, k, j % per))
    else:
        b_spec = pl.BlockSpec((tk, tn), lambda i, j, k: (k, j))
    return _mm_call(
        name, a, b, add, grid=(M // tm, N // tn, nk),
        a_spec=pl.BlockSpec((tm, tk), lambda i, j, k: (i, k)), b_spec=b_spec,
        add_spec=pl.BlockSpec((tm, tn), lambda i, j, k: (i, j)),
        o_spec=pl.BlockSpec((tm, tn), lambda i, j, k: (i, j)),
        o_shape=(M, N), o_dtype=o_dtype, acc_shape=(tm, tn), dims=NN, nk=nk)


def _mm_nt(name, a, b, *, add=None, o_dtype=F32):
    M, K = a.shape
    sharded = b.ndim == 3
    N, Ks = b.shape[-2], b.shape[-1]
    tm, tn, tk = _tile(M, 1024), _tile(N, 1024), _tile(Ks, 2048)
    per = Ks // tk
    nk = K // tk
    if sharded:
        b_spec = pl.BlockSpec((None, tn, tk), lambda i, j, k: (k // per, j, k % per))
    else:
        b_spec = pl.BlockSpec((tn, tk), lambda i, j, k: (j, k))
    return _mm_call(
        name, a, b, add, grid=(M // tm, N // tn, nk),
        a_spec=pl.BlockSpec((tm, tk), lambda i, j, k: (i, k)), b_spec=b_spec,
        add_spec=pl.BlockSpec((tm, tn), lambda i, j, k: (i, j)),
        o_spec=pl.BlockSpec((tm, tn), lambda i, j, k: (i, j)),
        o_shape=(M, N), o_dtype=o_dtype, acc_shape=(tm, tn), dims=NT, nk=nk)


def _mm_nt_sum(name, a_s, b_s):
    n = len(a_s)
    M, N = a_s[0].shape[0], b_s[0].shape[0]
    tm, tn = _tile(M, 512), _tile(N, 1024)

    def body(*refs):
        acc = None
        for i in range(n):
            prod = lax.dot_general(refs[i][...].astype(BF), refs[n + i][...].astype(BF), NT, preferred_element_type=F32)
            acc = prod if acc is None else acc + prod
        refs[2 * n][...] = acc

    return pl.pallas_call(
        body, name=name, grid=(N // tn, M // tm),
        in_specs=[pl.BlockSpec((tm, a.shape[1]), lambda j, i: (i, 0)) for a in a_s]
        + [pl.BlockSpec((tn, b.shape[1]), lambda j, i: (j, 0)) for b in b_s],
        out_specs=pl.BlockSpec((tm, tn), lambda j, i: (i, j)),
        out_shape=jax.ShapeDtypeStruct((M, N), F32),
        compiler_params=_params("parallel", "parallel"))(*a_s, *b_s)


def _mm_tn(name, a, b, *, shards=1, o_dtype=BF):
    S, M = a.shape
    N = b.shape[1]
    Ns = N // shards
    tm, tn, tk = _tile(M, 1024), _tile(Ns, 1024), _tile(S, 2048)
    per = Ns // tn
    nk = S // tk
    if shards > 1:
        o_spec = pl.BlockSpec((None, tm, tn), lambda i, j, k: (j // per, i, j % per))
        o_shape = (shards, M, Ns)
    else:
        o_spec = pl.BlockSpec((tm, tn), lambda i, j, k: (i, j))
        o_shape = (M, N)
    return _mm_call(
        name, a, b, None, grid=(M // tm, N // tn, nk),
        a_spec=pl.BlockSpec((tk, tm), lambda i, j, k: (k, i)),
        b_spec=pl.BlockSpec((tk, tn), lambda i, j, k: (k, j)),
        add_spec=None, o_spec=o_spec, o_shape=o_shape, o_dtype=o_dtype, acc_shape=(tm, tn), dims=TN, nk=nk)


def _up_fwd(h2, w_up):
    S, D = h2.shape
    G, _, C = w_up.shape
    tm = _tile(S, 512)
    return _mm_call(
        "up_fwd", h2, w_up, None, grid=(G, S // tm, 1),
        a_spec=pl.BlockSpec((tm, D), lambda g, i, k: (i, 0)),
        b_spec=pl.BlockSpec((None, D, C), lambda g, i, k: (g, 0, 0)),
        add_spec=None, o_spec=pl.BlockSpec((None, tm, C), lambda g, i, k: (g, i, 0)),
        o_shape=(G, S, C), o_dtype=BF, acc_shape=None, dims=NN, nk=1)


def _up_dgrad(du_pre, w_up):
    G, S, C = du_pre.shape
    D = w_up.shape[1]
    tm, tn = _tile(S, 1024), _tile(D, 1024)
    return _mm_call(
        "up_dgrad", du_pre, w_up, None, grid=(S // tm, D // tn, G),
        a_spec=pl.BlockSpec((None, tm, C), lambda i, j, g: (g, i, 0)),
        b_spec=pl.BlockSpec((None, tn, C), lambda i, j, g: (g, j, 0)),
        add_spec=None, o_spec=pl.BlockSpec((tm, tn), lambda i, j, g: (i, j)),
        o_shape=(S, D), o_dtype=F32, acc_shape=(tm, tn), dims=NT, nk=G)


def _up_wgrad(h2, du_pre):
    G, S, C = du_pre.shape
    D = h2.shape[1]
    tm, tk = _tile(D, 512), _tile(S, 2048)
    return _mm_call(
        "up_wgrad", h2, du_pre, None, grid=(G, D // tm, S // tk),
        a_spec=pl.BlockSpec((tk, tm), lambda g, i, k: (k, i)),
        b_spec=pl.BlockSpec((None, tk, C), lambda g, i, k: (g, k, 0)),
        add_spec=None, o_spec=pl.BlockSpec((None, tm, C), lambda g, i, k: (g, i, 0)),
        o_shape=(G, D, C), o_dtype=BF, acc_shape=(tm, C), dims=TN, nk=S // tk)


def _down_fwd(act, w_down, x1):
    G, S, C = act.shape
    D = w_down.shape[2]
    tm, tn = _tile(S, 1024), _tile(D, 1024)
    return _mm_call(
        "down_fwd", act, w_down, x1, grid=(S // tm, D // tn, G),
        a_spec=pl.BlockSpec((None, tm, C), lambda i, j, g: (g, i, 0)),
        b_spec=pl.BlockSpec((None, C, tn), lambda i, j, g: (g, 0, j)),
        add_spec=pl.BlockSpec((tm, tn), lambda i, j, g: (i, j)),
        o_spec=pl.BlockSpec((tm, tn), lambda i, j, g: (i, j)),
        o_shape=(S, D), o_dtype=F32, acc_shape=(tm, tn), dims=NN, nk=G)


def _down_dgrad(dx2, w_down):
    S, D = dx2.shape
    G, C, _ = w_down.shape
    tm = _tile(S, 512)
    return _mm_call(
        "down_dgrad", dx2, w_down, None, grid=(G, S // tm, 1),
        a_spec=pl.BlockSpec((tm, D), lambda g, i, k: (i, 0)),
        b_spec=pl.BlockSpec((None, C, D), lambda g, i, k: (g, 0, 0)),
        add_spec=None, o_spec=pl.BlockSpec((None, tm, C), lambda g, i, k: (g, i, 0)),
        o_shape=(G, S, C), o_dtype=BF, acc_shape=None, dims=NT, nk=1)


def _down_wgrad(act, dx2):
    G, S, C = act.shape
    D = dx2.shape[1]
    tn, tk = _tile(D, 512), _tile(S, 1024)
    return _mm_call(
        "down_wgrad", act, dx2, None, grid=(G, D // tn, S // tk),
        a_spec=pl.BlockSpec((None, tk, C), lambda g, j, k: (g, k, 0)),
        b_spec=pl.BlockSpec((tk, tn), lambda g, j, k: (k, j)),
        add_spec=None, o_spec=pl.BlockSpec((None, C, tn), lambda g, j, k: (g, 0, j)),
        o_shape=(G, C, D), o_dtype=BF, acc_shape=(C, tn), dims=TN, nk=S // tk)


def _row(ts, c):
    return pl.BlockSpec((ts, c), lambda i: (i, 0))


def _bcast(r, c):
    return pl.BlockSpec((r, c), lambda i: (0, 0))


def _accumulate(i, ref, val):
    @pl.when(i == 0)
    def _():
        ref[...] = val

    @pl.when(i > 0)
    def _():
        ref[...] += val


def _rstd(xv):
    return lax.rsqrt(jnp.mean(xv * xv, axis=-1, keepdims=True) + NORM_EPS)


def _rmsnorm_fwd(name, x, g):
    S, D = x.shape
    ts = _tile(S, 512)

    def body(x_ref, g_ref, o_ref):
        xv = x_ref[...]
        o_ref[...] = (xv * _rstd(xv) * g_ref[...]).astype(o_ref.dtype)

    return pl.pallas_call(
        body, name=name, grid=(S // ts,), in_specs=[_row(ts, D), _bcast(1, D)], out_specs=_row(ts, D),
        out_shape=jax.ShapeDtypeStruct((S, D), BF), compiler_params=_params("parallel"))(x, g)


def _norm_bwd_rows(dy, xv, g):
    r = _rstd(xv)
    xh = xv * r
    dxh = dy * g
    dx = r * (dxh - xh * jnp.mean(dxh * xh, axis=-1, keepdims=True))
    return dx, jnp.sum(dy * xh, axis=0, keepdims=True)


def _rmsnorm_bwd(name, dy, x, g, res):
    S, D = x.shape
    ts = _tile(S, 512)

    def body(dy_ref, x_ref, g_ref, res_ref, dx_ref, dg_ref):
        dx, dg = _norm_bwd_rows(dy_ref[...], x_ref[...], g_ref[...])
        dx_ref[...] = dx + res_ref[...]
        _accumulate(pl.program_id(0), dg_ref, dg)

    return pl.pallas_call(
        body, name=name, grid=(S // ts,),
        in_specs=[_row(ts, D), _row(ts, D), _bcast(1, D), _row(ts, D)],
        out_specs=[_row(ts, D), _bcast(1, D)],
        out_shape=[jax.ShapeDtypeStruct((S, D), F32), jax.ShapeDtypeStruct((1, D), F32)],
        compiler_params=_params("arbitrary"))(dy, x, g, res)


def _rope_tables(S):
    half = QK_ROPE // 2
    pos = jnp.arange(S, dtype=F32)
    inv_freq = ROPE_THETA ** (-jnp.arange(0, QK_ROPE, 2, dtype=F32) / QK_ROPE)
    ang = pos[:, None] * inv_freq[None, :]
    cos, sin = jnp.cos(ang), jnp.sin(ang)
    z = jnp.zeros((S, half), F32)
    return jnp.concatenate([cos, z, cos, z], axis=1), jnp.concatenate([-sin, z, sin, z], axis=1)


def _rope_lanes(x, cos, sin_signed, inverse):
    if inverse:
        return x * cos + pltpu.roll(x * sin_signed, LANE // 2, 1)
    return x * cos + pltpu.roll(x, LANE // 2, 1) * sin_signed


def _rope(name, x, cos, sin_signed, inverse):
    S, W = x.shape
    ts = _tile(S, 512)

    def body(x_ref, c_ref, s_ref, o_ref):
        c, s = c_ref[...], s_ref[...]
        for h in range(W // LANE):
            sl = slice(h * LANE, (h + 1) * LANE)
            o_ref[:, sl] = _rope_lanes(x_ref[:, sl], c, s, inverse).astype(o_ref.dtype)

    return pl.pallas_call(
        body, name=name, grid=(S // ts,), in_specs=[_row(ts, W), _row(ts, LANE), _row(ts, LANE)],
        out_specs=_row(ts, W), out_shape=jax.ShapeDtypeStruct((S, W), BF),
        compiler_params=_params("parallel"))(x, cos, sin_signed)


LAT_W = 1024
_CQ = slice(0, Q_RANK)
_CKV = slice(Q_RANK, Q_RANK + KV_RANK)
_KPE = slice(Q_RANK + KV_RANK, Q_RANK + KV_RANK + LANE)


def _mla_prep(lat, qg, kvg, cos, sin_signed):
    S = lat.shape[0]
    ts = _tile(S, 512)

    def body(lat_ref, qg_ref, kvg_ref, c_ref, s_ref, qn_ref, kvn_ref, kpe_ref):
        cq = lat_ref[:, _CQ]
        qn_ref[...] = (cq * _rstd(cq) * qg_ref[...]).astype(BF)
        ckv = lat_ref[:, _CKV]
        kvn_ref[...] = (ckv * _rstd(ckv) * kvg_ref[...]).astype(BF)
        kpe_ref[...] = _rope_lanes(lat_ref[:, _KPE], c_ref[...], s_ref[...], False).astype(BF)

    return pl.pallas_call(
        body, name="mla_prep", grid=(S // ts,),
        in_specs=[_row(ts, LAT_W), _bcast(1, Q_RANK), _bcast(1, KV_RANK), _row(ts, LANE), _row(ts, LANE)],
        out_specs=[_row(ts, Q_RANK), _row(ts, KV_RANK), _row(ts, LANE)],
        out_shape=[jax.ShapeDtypeStruct((S, Q_RANK), BF), jax.ShapeDtypeStruct((S, KV_RANK), BF),
                   jax.ShapeDtypeStruct((S, LANE), BF)],
        compiler_params=_params("parallel"))(lat, qg, kvg, cos, sin_signed)


def _mla_prep_bwd(lat, qg, kvg, cos, sin_signed, d_qn, d_kvn, d_kpe):
    S = lat.shape[0]
    ts = _tile(S, 512)

    def body(lat_ref, qg_ref, kvg_ref, c_ref, s_ref, dqn_ref, dkvn_ref, dkpe_ref, dlat_ref, dqg_ref, dkvg_ref):
        i = pl.program_id(0)
        dcq, dqg = _norm_bwd_rows(dqn_ref[...], lat_ref[:, _CQ], qg_ref[...])
        dckv, dkvg = _norm_bwd_rows(dkvn_ref[...], lat_ref[:, _CKV], kvg_ref[...])
        dlat_ref[:, _CQ] = dcq.astype(BF)
        dlat_ref[:, _CKV] = dckv.astype(BF)
        dkpe = dkpe_ref[0]
        for g in range(1, d_kpe.shape[0]):
            dkpe = dkpe + dkpe_ref[g]
        dlat_ref[:, _KPE] = _rope_lanes(dkpe, c_ref[...], s_ref[...], True).astype(BF)
        dlat_ref[:, _KPE.stop:] = jnp.zeros((ts, LAT_W - _KPE.stop), BF)
        _accumulate(i, dqg_ref, dqg)
        _accumulate(i, dkvg_ref, dkvg)

    return pl.pallas_call(
        body, name="mla_prep_bwd", grid=(S // ts,),
        in_specs=[_row(ts, LAT_W), _bcast(1, Q_RANK), _bcast(1, KV_RANK), _row(ts, LANE), _row(ts, LANE),
                  _row(ts, Q_RANK), _row(ts, KV_RANK), pl.BlockSpec((d_kpe.shape[0], ts, LANE), lambda i: (0, i, 0))],
        out_specs=[_row(ts, LAT_W), _bcast(1, Q_RANK), _bcast(1, KV_RANK)],
        out_shape=[jax.ShapeDtypeStruct((S, LAT_W), BF), jax.ShapeDtypeStruct((1, Q_RANK), F32),
                   jax.ShapeDtypeStruct((1, KV_RANK), F32)],
        compiler_params=_params("arbitrary"))(lat, qg, kvg, cos, sin_signed, d_qn, d_kvn, d_kpe)


def _sigmoid(z):
    return 1.0 / (1.0 + jnp.exp(-z))


def _merge_fwd(gpre, b_gate, o_a, o_b):
    S, D = o_a.shape
    ts = _tile(S, 256)

    def body(g_ref, b_ref, oa_ref, ob_ref, m_ref):
        ga = _sigmoid(g_ref[:, :D] + b_ref[:, :D])
        gb = _sigmoid(g_ref[:, D:] + b_ref[:, D:])
        m_ref[...] = (ga * oa_ref[...] + gb * ob_ref[...]).astype(BF)

    return pl.pallas_call(
        body, name="merge_fwd", grid=(S // ts,),
        in_specs=[_row(ts, 2 * D), _bcast(1, 2 * D), _row(ts, D), _row(ts, D)], out_specs=_row(ts, D),
        out_shape=jax.ShapeDtypeStruct((S, D), BF), compiler_params=_params("parallel"))(gpre, b_gate, o_a, o_b)


def _merge_bwd(d_merge, gpre, b_gate, o_a, o_b):
    S, D = o_a.shape
    ts = _tile(S, 256)

    def body(dm_ref, g_ref, b_ref, oa_ref, ob_ref, doa_ref, dob_ref, dg_ref, db_ref):
        dm = dm_ref[...]
        ga = _sigmoid(g_ref[:, :D] + b_ref[:, :D])
        gb = _sigmoid(g_ref[:, D:] + b_ref[:, D:])
        doa_ref[...] = (dm * ga).astype(BF)
        dob_ref[...] = (dm * gb).astype(BF)
        dga = dm * oa_ref[...] * ga * (1.0 - ga)
        dgb = dm * ob_ref[...] * gb * (1.0 - gb)
        dg_ref[:, :D] = dga.astype(BF)
        dg_ref[:, D:] = dgb.astype(BF)
        i = pl.program_id(0)
        part = jnp.concatenate([jnp.sum(dga, axis=0, keepdims=True), jnp.sum(dgb, axis=0, keepdims=True)], axis=1)
        _accumulate(i, db_ref, part)

    return pl.pallas_call(
        body, name="merge_bwd", grid=(S // ts,),
        in_specs=[_row(ts, D), _row(ts, 2 * D), _bcast(1, 2 * D), _row(ts, D), _row(ts, D)],
        out_specs=[_row(ts, D), _row(ts, D), _row(ts, 2 * D), _bcast(1, 2 * D)],
        out_shape=[jax.ShapeDtypeStruct((S, D), BF), jax.ShapeDtypeStruct((S, D), BF),
                   jax.ShapeDtypeStruct((S, 2 * D), BF), jax.ShapeDtypeStruct((1, 2 * D), F32)],
        compiler_params=_params("arbitrary"))(d_merge, gpre, b_gate, o_a, o_b)


def _final_loss(x2, tgt, gf):
    S, D = x2.shape
    ts = _tile(S, 512)

    def body(x_ref, t_ref, g_ref, dx_ref, dg_ref, loss_ref):
        i = pl.program_id(0)
        xv = x_ref[...]
        g = g_ref[...]
        y = xv * _rstd(xv) * g
        err = y - t_ref[...]
        dx, dg = _norm_bwd_rows(err * (1.0 / D), xv, g)
        dx_ref[...] = dx
        _accumulate(i, dg_ref, dg)
        part = 0.5 * jnp.sum(jnp.mean(err * err, axis=-1, keepdims=True), axis=0, keepdims=True)
        _accumulate(i, loss_ref, jnp.broadcast_to(part, (8, LANE)))

    return pl.pallas_call(
        body, name="final_loss", grid=(S // ts,),
        in_specs=[_row(ts, D), _row(ts, D), _bcast(1, D)],
        out_specs=[_row(ts, D), _bcast(1, D), _bcast(8, LANE)],
        out_shape=[jax.ShapeDtypeStruct((S, D), F32), jax.ShapeDtypeStruct((1, D), F32),
                   jax.ShapeDtypeStruct((8, LANE), F32)],
        compiler_params=_params("arbitrary"))(x2, tgt, gf)


HALO = 16


SUB = 8


def _shift_down(cur, prev, k, rows):
    out = pltpu.roll(cur, k, 0)
    head = out[:SUB]
    for j in range(k):
        head = jnp.where(rows == j, prev[HALO - k + j:HALO - k + j + 1, :], head)
    return jnp.concatenate([head, out[SUB:]], axis=0)


def _shift_up(cur, nxt, k, rows, ts):
    out = pltpu.roll(cur, ts - k, 0)
    tail = out[ts - SUB:]
    for j in range(k):
        tail = jnp.where(rows == SUB - k + j, nxt[j:j + 1, :], tail)
    return jnp.concatenate([out[:ts - SUB], tail], axis=0)


def _conv_rows(cur, prev, w, b, rows):
    return b + w[0:1, :] * _shift_down(cur, prev, 2, rows) + w[1:2, :] * _shift_down(cur, prev, 1, rows) + w[2:3, :] * cur


def _conv_specs(ts, C, shard_of):
    nh = ts // HALO
    cur = pl.BlockSpec((None, ts, C), lambda g, i: (shard_of(g), i, 0))
    prev = pl.BlockSpec((None, HALO, C), lambda g, i: (shard_of(g), jnp.maximum(i * nh - 1, 0), 0))
    return cur, prev


def _ffn_act(u_pre, conv_w, conv_b):
    G4, S, C = u_pre.shape
    G = G4 // 2
    ts = _tile(S, 256)

    def body(up_ref, upp_ref, gt_ref, gtp_ref, wu_ref, wg_ref, bu_ref, bg_ref, act_ref, u_ref):
        first = pl.program_id(1) == 0
        rows = lax.broadcasted_iota(jnp.int32, (SUB, C), 0)
        pu = jnp.where(first, 0.0, upp_ref[...].astype(F32))
        pg = jnp.where(first, 0.0, gtp_ref[...].astype(F32))
        up = _conv_rows(up_ref[...].astype(F32), pu, wu_ref[...], bu_ref[...], rows)
        gate = _conv_rows(gt_ref[...].astype(F32), pg, wg_ref[...], bg_ref[...], rows)
        act_ref[...] = (gate * _sigmoid(gate) * up).astype(BF)
        u_ref[0] = up.astype(BF)
        u_ref[1] = gate.astype(BF)

    cur_u, prev_u = _conv_specs(ts, C, lambda g: g)
    cur_g, prev_g = _conv_specs(ts, C, lambda g: g + G)
    w_u = pl.BlockSpec((None, 3, C), lambda g, i: (g, 0, 0))
    w_g = pl.BlockSpec((None, 3, C), lambda g, i: (g + G, 0, 0))
    b_u = pl.BlockSpec((None, 1, C), lambda g, i: (g, 0, 0))
    b_g = pl.BlockSpec((None, 1, C), lambda g, i: (g + G, 0, 0))
    pair = pl.BlockSpec((2, None, ts, C), lambda g, i: (0, g, i, 0))
    act, u = pl.pallas_call(
        body, name="ffn_act", grid=(G, S // ts),
        in_specs=[cur_u, prev_u, cur_g, prev_g, w_u, w_g, b_u, b_g],
        out_specs=[pl.BlockSpec((None, ts, C), lambda g, i: (g, i, 0)), pair],
        out_shape=[jax.ShapeDtypeStruct((G, S, C), BF), jax.ShapeDtypeStruct((2, G, S, C), BF)],
        compiler_params=_params("parallel", "parallel"))(u_pre, u_pre, u_pre, u_pre, conv_w, conv_w, conv_b, conv_b)
    return act, u


def _ffn_act_conv_bwd(u, d_act, u_pre, conv_w):
    _, G, S, C = u.shape
    ts = _tile(S, 256)
    nh = ts // HALO
    last_halo = S // HALO - 1

    def d_outputs(up, gate, da):
        sg = _sigmoid(gate)
        return da * (gate * sg), da * up * (sg * (1.0 + gate * (1.0 - sg)))

    def body(u_ref, un_ref, da_ref, dan_ref, x_ref, w_ref, dpre_ref, dw_ref, db_ref):
        i = pl.program_id(1)
        rows = lax.broadcasted_iota(jnp.int32, (SUB, C), 0)
        da_n = jnp.where(i == pl.num_programs(1) - 1, 0.0, dan_ref[...].astype(F32))
        du = d_outputs(u_ref[0].astype(F32), u_ref[1].astype(F32), da_ref[...].astype(F32))
        du_n = d_outputs(un_ref[0].astype(F32), un_ref[1].astype(F32), da_n)
        for kind in range(2):
            du_c, w = du[kind], w_ref[kind]
            up1, up2 = _shift_up(du_c, du_n[kind], 1, rows, ts), _shift_up(du_c, du_n[kind], 2, rows, ts)
            dpre_ref[kind] = (w[2:3, :] * du_c + w[1:2, :] * up1 + w[0:1, :] * up2).astype(BF)
            x_c = x_ref[kind].astype(F32)
            dw = jnp.concatenate([
                jnp.sum(up2 * x_c, axis=0, keepdims=True),
                jnp.sum(up1 * x_c, axis=0, keepdims=True),
                jnp.sum(du_c * x_c, axis=0, keepdims=True)], axis=0)
            db = jnp.sum(du_c, axis=0, keepdims=True)

            @pl.when(i == 0)
            def _():
                dw_ref[kind] = dw
                db_ref[kind] = db

            @pl.when(i > 0)
            def _():
                dw_ref[kind] += dw
                db_ref[kind] += db

    def nxt_row(i):
        return jnp.minimum((i + 1) * nh, last_halo)

    pair = pl.BlockSpec((2, None, ts, C), lambda g, i: (0, g, i, 0))
    pair_n = pl.BlockSpec((2, None, HALO, C), lambda g, i: (0, g, nxt_row(i), 0))
    one = pl.BlockSpec((None, ts, C), lambda g, i: (g, i, 0))
    one_n = pl.BlockSpec((None, HALO, C), lambda g, i: (g, nxt_row(i), 0))
    w_spec = pl.BlockSpec((2, None, 3, C), lambda g, i: (0, g, 0, 0))
    b_spec = pl.BlockSpec((2, None, 1, C), lambda g, i: (0, g, 0, 0))
    dpre, dw, db = pl.pallas_call(
        body, name="ffn_act_conv_bwd", grid=(G, S // ts),
        in_specs=[pair, pair_n, one, one_n, pair, w_spec], out_specs=[pair, w_spec, b_spec],
        out_shape=[jax.ShapeDtypeStruct((2, G, S, C), BF), jax.ShapeDtypeStruct((2, G, 3, C), F32),
                   jax.ShapeDtypeStruct((2, G, 1, C), F32)],
        compiler_params=_params("parallel", "arbitrary"))(
            u, u, d_act, d_act, u_pre.reshape(2, G, S, C), conv_w.reshape(2, G, 3, C))
    return dpre.reshape(2 * G, S, C), dw.reshape(2 * G, 3, C), db.reshape(2 * G, 1, C)


MLA_T = 1024
MLA_HB = 4
MLA_BWD_HB = 2


def _mla_pairs(n, by_row):
    if by_row:
        pairs = [(i, j) for i in range(n) for j in range(i + 1)]
    else:
        pairs = [(i, j) for j in range(n) for i in range(j, n)]
    return jnp.asarray([p[0] for p in pairs], jnp.int32), jnp.asarray([p[1] for p in pairs], jnp.int32)


def _mla_specs(hb):
    q = pl.BlockSpec((MLA_T, hb * HEAD), lambda g, t, it, jt: (it[t], g))
    k = pl.BlockSpec((MLA_T, hb * HEAD), lambda g, t, it, jt: (jt[t], g))
    kpe = pl.BlockSpec((MLA_T, HEAD), lambda g, t, it, jt: (jt[t], 0))
    lse = pl.BlockSpec((hb, MLA_T, LANE), lambda g, t, it, jt: (g, it[t], 0))
    return q, k, kpe, lse


def _mla_head(ref, hh):
    return ref[:, hh * HEAD:(hh + 1) * HEAD]


LOG2E = math.log2(math.e)
MLA_EXP2_SCALE = MLA_SCALE * LOG2E


def _mla_scores(qn_ref, qp_ref, kn_ref, kpe, hh, ok):
    q = jnp.concatenate([_mla_head(qn_ref, hh), _mla_head(qp_ref, hh)], axis=1)
    k = jnp.concatenate([_mla_head(kn_ref, hh), kpe], axis=1)
    s = lax.dot_general(q, k, NT, preferred_element_type=F32)
    return q, k, s if ok is None else jnp.where(ok, s, -jnp.inf)


def _mla_diagonal_mask():
    row = lax.broadcasted_iota(jnp.int32, (MLA_T, MLA_T), 0)
    col = lax.broadcasted_iota(jnp.int32, (MLA_T, MLA_T), 1)
    return col <= row


def _mla_step(i, j, step):
    @pl.when(j < i)
    def _():
        step(None)

    @pl.when(j == i)
    def _():
        step(_mla_diagonal_mask())


def _mla_fwd(qn, qp, kn, kpe, v):
    S = qn.shape[0]
    it, jt = _mla_pairs(S // MLA_T, True)

    def body(it_ref, jt_ref, qn_ref, qp_ref, kn_ref, kpe_ref, v_ref, o_ref, lse_ref, m_scr, acc_scr):
        t = pl.program_id(1)
        i, j = it_ref[t], jt_ref[t]

        @pl.when(j == 0)
        def _():
            m_scr[...] = jnp.full(m_scr.shape, -jnp.inf, F32)
            acc_scr[...] = jnp.zeros(acc_scr.shape, F32)

        def step(ok):
            kpe_v = kpe_ref[...]
            ones = jnp.ones((MLA_T, HEAD), BF)
            state = [(m_scr[hh], acc_scr[hh]) for hh in range(MLA_HB)]
            new = []
            for hh in range(MLA_HB):
                m_prev, acc = state[hh]
                _, _, s = _mla_scores(qn_ref, qp_ref, kn_ref, kpe_v, hh, ok)
                m_new = jnp.maximum(m_prev, jnp.max(s, axis=1, keepdims=True))
                p = jnp.exp2((s - m_new) * MLA_EXP2_SCALE).astype(BF)
                v1 = jnp.concatenate([_mla_head(v_ref, hh), ones], axis=1)
                alpha = jnp.exp2((m_prev - m_new) * MLA_EXP2_SCALE)
                new.append((m_new, alpha * acc + lax.dot_general(p, v1, NN, preferred_element_type=F32)))
            for hh in range(MLA_HB):
                m_scr[hh], acc_scr[hh] = new[hh]

        _mla_step(i, j, step)

        @pl.when(j == i)
        def _():
            for hh in range(MLA_HB):
                l = acc_scr[hh, :, HEAD:]
                o_ref[:, hh * HEAD:(hh + 1) * HEAD] = (acc_scr[hh, :, :HEAD] / l).astype(BF)
                lse_ref[hh] = m_scr[hh] * MLA_SCALE + jnp.log(l)

    qspec, kspec, kpespec, lsespec = _mla_specs(MLA_HB)
    grid_spec = pltpu.PrefetchScalarGridSpec(
        num_scalar_prefetch=2, grid=(MLA_HEADS // MLA_HB, it.shape[0]),
        in_specs=[qspec, qspec, kspec, kpespec, kspec], out_specs=[qspec, lsespec],
        scratch_shapes=[pltpu.VMEM((MLA_HB, MLA_T, 1), F32), pltpu.VMEM((MLA_HB, MLA_T, 2 * HEAD), F32)])
    return pl.pallas_call(
        body, name="mla_fwd", grid_spec=grid_spec,
        out_shape=[jax.ShapeDtypeStruct((S, MLA_HEADS * HEAD), BF), jax.ShapeDtypeStruct((MLA_HEADS, S, LANE), F32)],
        compiler_params=_params("parallel", "arbitrary"))(it, jt, qn, qp, kn, kpe, v)


def _mla_p_ds(qn_ref, qp_ref, kn_ref, kpe, v_ref, do_ref, o_ref, lse_ref, hh, ok):
    q, k, s = _mla_scores(qn_ref, qp_ref, kn_ref, kpe, hh, ok)
    p = jnp.exp2(s * MLA_EXP2_SCALE - lse_ref[hh][:, 0:1] * LOG2E)
    do = _mla_head(do_ref, hh)
    delta = jnp.sum(do.astype(F32) * _mla_head(o_ref, hh).astype(F32), axis=1, keepdims=True)
    dp = lax.dot_general(do, _mla_head(v_ref, hh), NT, preferred_element_type=F32)
    ds = p * (dp - delta) * MLA_SCALE
    return q, k, p, ds, do


def _mla_bwd(qn, qp, kn, kpe, v, do, o, lse):
    S = qn.shape[0]
    nq = S // MLA_T
    hb = MLA_BWD_HB
    it, jt = _mla_pairs(nq, False)

    def body(it_ref, jt_ref, qn_ref, qp_ref, kn_ref, kpe_ref, v_ref, do_ref, o_ref, lse_ref,
             dqn_ref, dqp_ref, dkn_ref, dv_ref, dkpe_ref, dq_acc, dk_acc, dv_acc, stage_n, stage_p, osem):
        t = pl.program_id(1)
        i, j = it_ref[t], jt_ref[t]

        @pl.when(t == 0)
        def _():
            dq_acc[...] = jnp.zeros(dq_acc.shape, F32)

        @pl.when(i == j)
        def _():
            dk_acc[...] = jnp.zeros(dk_acc.shape, F32)
            dv_acc[...] = jnp.zeros(dv_acc.shape, F32)

        def step(ok):
            kpe_v = kpe_ref[...]
            for hh in range(hb):
                q, k, p, ds, do_h = _mla_p_ds(qn_ref, qp_ref, kn_ref, kpe_v, v_ref, do_ref, o_ref, lse_ref, hh, ok)
                ds = ds.astype(BF)
                dv_acc[hh] += lax.dot_general(p.astype(BF), do_h, TN, preferred_element_type=F32)
                dk_acc[hh] += lax.dot_general(ds, q, TN, preferred_element_type=F32)
                dq_acc[i, hh] += lax.dot_general(ds, k, NN, preferred_element_type=F32)

        _mla_step(i, j, step)

        @pl.when(i == j)
        def _():
            for hh in range(hb):
                stage_n[:, hh * HEAD:(hh + 1) * HEAD] = dq_acc[i, hh, :, :HEAD].astype(BF)
                stage_p[:, hh * HEAD:(hh + 1) * HEAD] = dq_acc[i, hh, :, HEAD:]
            rows = pl.ds(pl.multiple_of(i * MLA_T, MLA_T), MLA_T)
            cols = pl.ds(pl.multiple_of(pl.program_id(0) * hb * HEAD, LANE), hb * HEAD)
            out_n = pltpu.make_async_copy(stage_n, dqn_ref.at[rows, cols], osem.at[0])
            out_p = pltpu.make_async_copy(stage_p, dqp_ref.at[rows, cols], osem.at[1])
            out_n.start()
            out_p.start()
            out_n.wait()
            out_p.wait()

        @pl.when(i == nq - 1)
        def _():
            dkpe = dk_acc[0, :, HEAD:]
            for hh in range(hb):
                dkn_ref[:, hh * HEAD:(hh + 1) * HEAD] = dk_acc[hh, :, :HEAD].astype(BF)
                dv_ref[:, hh * HEAD:(hh + 1) * HEAD] = dv_acc[hh].astype(BF)
                if hh:
                    dkpe = dkpe + dk_acc[hh, :, HEAD:]
            dkpe_ref[...] = dkpe

    qspec, kspec, kpespec, lsespec = _mla_specs(hb)
    dkpespec = pl.BlockSpec((None, MLA_T, HEAD), lambda g, t, it, jt: (g, jt[t], 0))
    grid_spec = pltpu.PrefetchScalarGridSpec(
        num_scalar_prefetch=2, grid=(MLA_HEADS // hb, it.shape[0]),
        in_specs=[qspec, qspec, kspec, kpespec, kspec, qspec, qspec, lsespec],
        out_specs=[ANY, ANY, kspec, kspec, dkpespec],
        scratch_shapes=[pltpu.VMEM((nq, hb, MLA_T, 2 * HEAD), F32), pltpu.VMEM((hb, MLA_T, 2 * HEAD), F32),
                        pltpu.VMEM((hb, MLA_T, HEAD), F32), pltpu.VMEM((MLA_T, hb * HEAD), BF),
                        pltpu.VMEM((MLA_T, hb * HEAD), F32), pltpu.SemaphoreType.DMA((2,))])
    wide = jax.ShapeDtypeStruct((S, MLA_HEADS * HEAD), BF)
    return pl.pallas_call(
        body, name="mla_bwd", grid_spec=grid_spec,
        out_shape=[wide, jax.ShapeDtypeStruct((S, MLA_HEADS * HEAD), F32), wide, wide,
                   jax.ShapeDtypeStruct((MLA_HEADS // hb, S, HEAD), F32)],
        compiler_params=_params("parallel", "arbitrary"))(it, jt, qn, qp, kn, kpe, v, do, o, lse)


DIL_W = 3 * DIL_HPG * HEAD
DIL_O = DIL_HPG * HEAD
DIL_STEP_BLOCKS = 4


def _dil_slopes(g):
    return [2.0 ** (-ALIBI_MAX_BIAS * (g * DIL_HPG + hh + 1) / DIL_HEADS) for hh in range(DIL_HPG)]


def _dil_bias(dil):
    p = lax.broadcasted_iota(jnp.int32, (DIL_BLOCK, DIL_BLOCK), 0)
    kk = lax.broadcasted_iota(jnp.int32, (DIL_BLOCK, DIL_BLOCK), 1)
    jc = p - kk
    dist_c = (dil * jc).astype(F32)
    dist_p = (dil * (jc + DIL_BLOCK)).astype(F32)
    return jc >= 0, jc <= 0, dist_c, dist_p


def _dil_bias2(dil):
    p = lax.broadcasted_iota(jnp.int32, (DIL_BLOCK, 2 * DIL_BLOCK), 0)
    kk = lax.broadcasted_iota(jnp.int32, (DIL_BLOCK, 2 * DIL_BLOCK), 1)
    j = p + DIL_BLOCK - kk
    return (j >= 0) & (j <= DIL_BLOCK), kk < DIL_BLOCK, (dil * j).astype(F32)


def _dil_head(blk, hh):
    q = blk[:, hh * HEAD:(hh + 1) * HEAD]
    k = blk[:, DIL_O + hh * HEAD:DIL_O + (hh + 1) * HEAD]
    v = blk[:, 2 * DIL_O + hh * HEAD:2 * DIL_O + (hh + 1) * HEAD]
    return q, k, v


def _dil_s(q, k, slope, dist, ok):
    s = lax.dot_general(q, k, NT, preferred_element_type=F32) * DIL_SCALE - slope * dist
    return jnp.where(ok, s, -jnp.inf)


def _dil_view(a, dil):
    S, W = a.shape
    return a.reshape(S // dil, dil * W)


def _dil_fwd(qkv, g):
    _, dil = DIL_PATTERNS[g]
    S = qkv.shape[0]
    L = S // dil
    nb = L // DIL_BLOCK
    slopes = _dil_slopes(g)

    bb = min(DIL_STEP_BLOCKS, nb)
    rows = bb * DIL_BLOCK

    def body(cur_ref, prev_ref, o_ref, lse_ref):
        n = pl.program_id(1)
        ok, in_prev, dist = _dil_bias2(dil)
        for b in range(bb):
            if b == 0:
                both = jnp.concatenate([prev_ref[...], cur_ref[0:DIL_BLOCK, :]], axis=0)
                ok_b = ok & (~in_prev | (n > 0))
            else:
                both = cur_ref[(b - 1) * DIL_BLOCK:(b + 1) * DIL_BLOCK, :]
                ok_b = ok
            for hh in range(DIL_HPG):
                q, _, _ = _dil_head(both[DIL_BLOCK:], hh)
                _, k2, v2 = _dil_head(both, hh)
                s = _dil_s(q, k2, slopes[hh], dist, ok_b)
                m = jnp.max(s, axis=1, keepdims=True)
                p = jnp.exp(s - m)
                l = jnp.sum(p, axis=1, keepdims=True)
                o = lax.dot_general(p.astype(BF), v2, NN, preferred_element_type=F32) / l
                rs, sl = slice(b * DIL_BLOCK, (b + 1) * DIL_BLOCK), slice(hh * HEAD, (hh + 1) * HEAD)
                o_ref[rs, sl] = o
                lse_ref[rs, sl] = jnp.broadcast_to(m + jnp.log(l), (DIL_BLOCK, HEAD))

    ospec = pl.BlockSpec((rows, DIL_O), lambda r, n: (n, r))
    o, lse = pl.pallas_call(
        body, name=f"dil_fwd{g}", grid=(dil, nb // bb),
        in_specs=[pl.BlockSpec((rows, DIL_W), lambda r, n: (n, r)),
                  pl.BlockSpec((DIL_BLOCK, DIL_W), lambda r, n: (jnp.maximum(n * bb - 1, 0), r))],
        out_specs=[ospec, ospec],
        out_shape=[jax.ShapeDtypeStruct((L, dil * DIL_O), F32), jax.ShapeDtypeStruct((L, dil * DIL_O), F32)],
        compiler_params=_params("parallel", "parallel"))(_dil_view(qkv, dil), _dil_view(qkv, dil))
    return o.reshape(S, DIL_O), lse.reshape(S, DIL_O)


def _dil_combine(os_, lses):
    S = os_[0].shape[0]
    ts = _tile(S, 512)

    def body(o0, o1, o2, l0, l1, l2, out_ref, lse_ref):
        a, b, c = l0[...], l1[...], l2[...]
        m = jnp.maximum(jnp.maximum(a, b), c)
        ea, eb, ec = jnp.exp(a - m), jnp.exp(b - m), jnp.exp(c - m)
        tot = ea + eb + ec
        out_ref[...] = ((ea * o0[...] + eb * o1[...] + ec * o2[...]) / tot).astype(BF)
        lse_ref[...] = m + jnp.log(tot)

    return pl.pallas_call(
        body, name="dil_combine", grid=(S // ts,), in_specs=[_row(ts, DIL_O)] * 6,
        out_specs=[_row(ts, DIL_O), _row(ts, DIL_O)],
        out_shape=[jax.ShapeDtypeStruct((S, DIL_O), BF), jax.ShapeDtypeStruct((S, DIL_O), F32)],
        compiler_params=_params("parallel"))(*os_, *lses)


def _dil_delta(do, out):
    S = do.shape[0]
    ts = _tile(S, 512)

    def body(do_ref, o_ref, d_ref):
        for hh in range(DIL_HPG):
            sl = slice(hh * HEAD, (hh + 1) * HEAD)
            d = jnp.sum(do_ref[:, sl].astype(F32) * o_ref[:, sl].astype(F32), axis=1, keepdims=True)
            d_ref[:, sl] = jnp.broadcast_to(d, (ts, HEAD))

    return pl.pallas_call(
        body, name="dil_delta", grid=(S // ts,), in_specs=[_row(ts, DIL_O)] * 2, out_specs=_row(ts, DIL_O),
        out_shape=jax.ShapeDtypeStruct((S, DIL_O), F32), compiler_params=_params("parallel"))(do, out)


def _dil_bwd(qkv, do, lse, delta, g):
    _, dil = DIL_PATTERNS[g]
    S = qkv.shape[0]
    L = S // dil
    nb = L // DIL_BLOCK
    slopes = _dil_slopes(g)

    def pair(q, k, v, do_h, lse_h, delta_h, slope, dist, ok):
        s = _dil_s(q, k, slope, dist, ok)
        p = jnp.exp(s - lse_h)
        dp = lax.dot_general(do_h, v, NT, preferred_element_type=F32)
        ds = (p * (dp - delta_h) * DIL_SCALE).astype(BF)
        return p.astype(BF), ds

    bb = min(DIL_STEP_BLOCKS, nb)
    rows = bb * DIL_BLOCK
    steps = nb // bb

    def body(cur_ref, prev_ref, next_ref, doc_ref, don_ref, lsec_ref, lsen_ref, dlc_ref, dln_ref, out_ref):
        n = pl.program_id(1)
        ok2, in_prev, dist2 = _dil_bias2(dil)
        _, ok_p0, _, dist_p = _dil_bias(dil)
        for b in range(bb):
            rs = slice(b * DIL_BLOCK, (b + 1) * DIL_BLOCK)
            rn = slice((b + 1) * DIL_BLOCK, (b + 2) * DIL_BLOCK)
            two = slice(b * DIL_BLOCK, (b + 2) * DIL_BLOCK)
            first, last = b == 0, b == bb - 1
            if first:
                keys = jnp.concatenate([prev_ref[...], cur_ref[rs, :]], axis=0)
                ok_ab = ok2 & (~in_prev | (n > 0))
            else:
                keys = cur_ref[(b - 1) * DIL_BLOCK:(b + 1) * DIL_BLOCK, :]
                ok_ab = ok2
            qrows = jnp.concatenate([cur_ref[rs, :], next_ref[...]], axis=0) if last else cur_ref[two, :]
            ok_n = ok_p0 & (n < steps - 1) if last else ok_p0
            for hh in range(DIL_HPG):
                sl = slice(hh * HEAD, (hh + 1) * HEAD)
                q2, _, _ = _dil_head(qrows, hh)
                _, k2, v2 = _dil_head(keys, hh)
                q, qn, kc, vc = q2[:DIL_BLOCK], q2[DIL_BLOCK:], k2[DIL_BLOCK:], v2[DIL_BLOCK:]
                do2 = jnp.concatenate([doc_ref[rs, sl], don_ref[:, sl]], axis=0) if last else doc_ref[two, sl]
                do_c, do_n = do2[:DIL_BLOCK], do2[DIL_BLOCK:]
                lse_c = lsec_ref[rs, sl][:, 0:1]
                lse_n = (lsen_ref[:, sl] if last else lsec_ref[rn, sl])[:, 0:1]
                dl_c = dlc_ref[rs, sl][:, 0:1]
                dl_n = (dln_ref[:, sl] if last else dlc_ref[rn, sl])[:, 0:1]
                p_ab, ds_ab = pair(q, k2, v2, do_c, lse_c, dl_c, slopes[hh], dist2, ok_ab)
                p_n, ds_n = pair(qn, kc, vc, do_n, lse_n, dl_n, slopes[hh], dist_p, ok_n)
                dq = lax.dot_general(ds_ab, k2, NN, preferred_element_type=F32)
                dk = lax.dot_general(jnp.concatenate([ds_ab[:, DIL_BLOCK:], ds_n], axis=0), q2, TN, preferred_element_type=F32)
                dv = lax.dot_general(jnp.concatenate([p_ab[:, DIL_BLOCK:], p_n], axis=0), do2, TN, preferred_element_type=F32)
                out_ref[rs, sl] = dq.astype(BF)
                out_ref[rs, DIL_O + hh * HEAD:DIL_O + (hh + 1) * HEAD] = dk.astype(BF)
                out_ref[rs, 2 * DIL_O + hh * HEAD:2 * DIL_O + (hh + 1) * HEAD] = dv.astype(BF)

    cur_w = pl.BlockSpec((rows, DIL_W), lambda r, n: (n, r))
    prev_w = pl.BlockSpec((DIL_BLOCK, DIL_W), lambda r, n: (jnp.maximum(n * bb - 1, 0), r))
    next_w = pl.BlockSpec((DIL_BLOCK, DIL_W), lambda r, n: (jnp.minimum((n + 1) * bb, nb - 1), r))
    cur_o = pl.BlockSpec((rows, DIL_O), lambda r, n: (n, r))
    next_o = pl.BlockSpec((DIL_BLOCK, DIL_O), lambda r, n: (jnp.minimum((n + 1) * bb, nb - 1), r))
    qv, dov, lsev, dlv = _dil_view(qkv, dil), _dil_view(do, dil), _dil_view(lse, dil), _dil_view(delta, dil)
    out = pl.pallas_call(
        body, name=f"dil_bwd{g}", grid=(dil, steps),
        in_specs=[cur_w, prev_w, next_w, cur_o, next_o, cur_o, next_o, cur_o, next_o],
        out_specs=cur_w, out_shape=jax.ShapeDtypeStruct((L, dil * DIL_W), BF),
        compiler_params=_params("parallel", "parallel"))(qv, qv, qv, dov, dov, lsev, lsev, dlv, dlv)
    return out.reshape(S, DIL_W)


def _adamw(name, w, g, m, v):
    R, C = w.shape
    tr, tc = _adamw_block(R, C)

    def body(w_ref, g_ref, m_ref, v_ref, go_ref, d_ref, nm_ref, nv_ref):
        gv = g_ref[...]
        go_ref[...] = gv
        nm = ADAM_B1 * m_ref[...] + (1.0 - ADAM_B1) * gv
        nv = ADAM_B2 * v_ref[...] + (1.0 - ADAM_B2) * (gv * gv)
        m_hat = nm / (1.0 - ADAM_B1 ** ADAM_STEP)
        v_hat = nv / (1.0 - ADAM_B2 ** ADAM_STEP)
        d_ref[...] = -ADAM_LR * (m_hat / (jnp.sqrt(v_hat) + ADAM_EPS) + ADAM_WD * w_ref[...])
        nm_ref[...] = nm
        nv_ref[...] = nv

    spec = pl.BlockSpec((tr, tc), lambda i, j: (i, j))
    shp = jax.ShapeDtypeStruct((R, C), F32)
    return pl.pallas_call(
        body, name=name, grid=(R // tr, C // tc), in_specs=[spec] * 4, out_specs=[spec] * 4, out_shape=[shp] * 4,
        compiler_params=_params("parallel", "parallel"))(w, g, m, v)


ADAMW_BLOCK_ELEMS = 640 * 1024


def _adamw_block(R, C):
    if R * C <= ADAMW_BLOCK_ELEMS:
        return R, C
    tr = _tile_rows(R, max(8, ADAMW_BLOCK_ELEMS // C))
    tc = _tile(C, max(LANE, ADAMW_BLOCK_ELEMS // R))
    if tr * C >= R * tc or R * tc > ADAMW_BLOCK_ELEMS:
        return tr, C
    return R, tc


def _tile_rows(n, pref, mult=8):
    t = (pref // mult) * mult
    while t >= mult:
        if n % t == 0:
            return t
        t -= mult
    return n


ANY = pl.BlockSpec(memory_space=pl.ANY)


def _place():
    x, y, c = lax.axis_index("x"), lax.axis_index("y"), lax.axis_index("c")
    chips = [(1 - x, y), (x, 1 - y), (1 - x, 1 - y)]
    chip_idx = [2 * cx + cy for cx, cy in chips]
    return x, y, c, 2 * x + y, chips, chip_idx


def _rcopy(src, dst, ssem, rsem, dev):
    return pltpu.make_async_remote_copy(src_ref=src, dst_ref=dst, send_sem=ssem, recv_sem=rsem,
                                        device_id=dev, device_id_type=MESH)


HBM = pl.BlockSpec(memory_space=pltpu.HBM)
SEM = pl.BlockSpec(memory_space=pltpu.SEMAPHORE)
EFFECT = pltpu.SideEffectType.DATAFLOW_SIDE_EFFECTING


def _split_copies(kind, srcs, lands, ssem, rsem):
    _, _, c, me, chips, chip_idx = _place()
    cps = []
    for i in range(len(srcs)):
        for k in range(3):
            if kind == "gather":
                rows = srcs[i].shape[0]
                if rows == lands[i].shape[1]:
                    src, dst = srcs[i], lands[i].at[me]
                else:
                    src, dst = srcs[i], lands[i].at[me, pl.ds(pl.multiple_of(c * rows, 16), rows)]
            else:
                src, dst = srcs[i].at[chip_idx[k]], lands[i].at[k]
            cps.append(_rcopy(src, dst, ssem.at[3 * i + k], rsem.at[3 * i + k], (*chips[k], c)))
    return cps


def _exchange_start(name, kind, srcs, lands, groups):
    n, ng = len(srcs), len(groups)

    def body(*refs):
        src_refs, land_refs = refs[:n], refs[n:2 * n]
        sems = refs[2 * n:2 * n + 2 * ng]
        token = refs[-1]
        for gi, grp in enumerate(groups):
            cps = _split_copies(kind, [src_refs[i] for i in grp], [land_refs[i] for i in grp], sems[2 * gi], sems[2 * gi + 1])
            for cp in cps:
                cp.start()
        token[...] = jnp.zeros_like(token)

    arrays = list(srcs) + list(lands)
    out_shape = []
    for grp in groups:
        out_shape += [pltpu.SemaphoreType.DMA((3 * len(grp),)), pltpu.SemaphoreType.DMA((3 * len(grp),))]
    out_shape += [pltpu.HBM(a.shape, a.dtype) for a in arrays] + [jax.ShapeDtypeStruct((8, LANE), F32)]
    outs = pl.pallas_call(
        body, name=name, out_shape=out_shape, in_specs=[HBM] * (2 * n),
        out_specs=[SEM] * (2 * ng) + [HBM] * (2 * n) + [pl.BlockSpec(memory_space=pltpu.VMEM)],
        input_output_aliases={i: 2 * ng + i for i in range(2 * n)},
        compiler_params=pltpu.CompilerParams(has_side_effects=EFFECT),
    )(*[pltpu.with_memory_space_constraint(a, pltpu.HBM) for a in arrays])
    sems = [(outs[2 * gi], outs[2 * gi + 1]) for gi in range(ng)]
    thru = outs[2 * ng:2 * ng + 2 * n]
    return sems, thru[:n], thru[n:], outs[-1]


def _exchange_wait(name, kind, srcs, lands, sems, after):
    n = len(srcs)

    def body(*refs):
        cps = _split_copies(kind, refs[:n], refs[n:2 * n], refs[2 * n], refs[2 * n + 1])
        for cp in cps:
            cp.wait_send()
            cp.wait_recv()

    arrays = list(srcs) + list(lands)
    outs = pl.pallas_call(
        body, name=name, out_shape=[pltpu.HBM(a.shape, a.dtype) for a in arrays],
        in_specs=[HBM] * (2 * n) + [SEM, SEM, ANY], out_specs=[HBM] * (2 * n),
        input_output_aliases={i: i for i in range(2 * n)},
        compiler_params=pltpu.CompilerParams(has_side_effects=EFFECT),
    )(*arrays, sems[0], sems[1], after)
    return outs[:n], outs[n:]


EXCHANGE_CHUNK_BYTES = 3 * 1024 * 1024


def _half_geometry(R, C, axis):
    Rp, Cp = (R // 2, C) if axis == 0 else (R, C // 2)
    tr = _tile_rows(Rp, max(16, EXCHANGE_CHUNK_BYTES // (2 * Cp)), 16)
    return Rp, Cp, tr, Rp // tr


def _pair_sum(name, g, axis):
    G, R, C = g.shape
    Rp, Cp, tr, nb = _half_geometry(R, C, axis)
    steps = G * nb

    def half_block(s, b, h):
        return (s, h * nb + b, 0) if axis == 0 else (s, b, h)

    def body(c_ref, keep_ref, give_ref, out_ref, land, ssem, rsem, credit):
        x, y, c = lax.axis_index("x"), lax.axis_index("y"), lax.axis_index("c")
        sib = (x, y, 1 - c)
        t = pl.program_id(0) * nb + pl.program_id(1)
        slot = t % 2

        @pl.when(t >= 2)
        def _():
            pl.semaphore_wait(credit, 1)

        cp = _rcopy(give_ref.at[0], land.at[slot], ssem.at[slot], rsem.at[slot], sib)
        cp.start()
        cp.wait_recv()
        out_ref[...] = (keep_ref[...].astype(F32) + land[slot].astype(F32)).astype(BF)

        @pl.when(t + 2 < steps)
        def _():
            pl.semaphore_signal(credit, 1, device_id=sib, device_id_type=MESH)

        cp.wait_send()

    blk = (None, tr, Cp)
    grid_spec = pltpu.PrefetchScalarGridSpec(
        num_scalar_prefetch=1, grid=(G, nb),
        in_specs=[pl.BlockSpec(blk, lambda s, b, c_ref: half_block(s, b, c_ref[0])),
                  pl.BlockSpec((1, tr, Cp), lambda s, b, c_ref: half_block(s, b, 1 - c_ref[0]))],
        out_specs=pl.BlockSpec(blk, lambda s, b, c_ref: (s, b, 0)),
        scratch_shapes=[pltpu.VMEM((2, tr, Cp), BF), pltpu.SemaphoreType.DMA((2,)), pltpu.SemaphoreType.DMA((2,)),
                        pltpu.SemaphoreType.REGULAR])
    c_arr = lax.axis_index("c").astype(jnp.int32).reshape(1)
    return pl.pallas_call(
        body, name=name, grid_spec=grid_spec, out_shape=jax.ShapeDtypeStruct((G, Rp, Cp), BF),
        compiler_params=_params("arbitrary", "arbitrary"))(c_arr, g, g)


def _chip_total_join(name, h, landed, axis):
    G, Rp, Cp = h.shape
    R, C = (2 * Rp, Cp) if axis == 0 else (Rp, 2 * Cp)
    tr = _tile_rows(Rp, max(16, EXCHANGE_CHUNK_BYTES // (4 * Cp)), 16)
    nb = Rp // tr

    def body(me_ref, own_ref, l0_ref, l1_ref, l2_ref, full, stage, ssem, rsem, lsem):
        x, y, c = lax.axis_index("x"), lax.axis_index("y"), lax.axis_index("c")
        sib = (x, y, 1 - c)
        b = pl.program_id(0)

        def place(half, r0, rows):
            if axis == 0:
                return full.at[pl.ds(pl.multiple_of(half * Rp + r0, 8), rows), :]
            return full.at[pl.ds(pl.multiple_of(r0, 8), rows), pl.ds(pl.multiple_of(half * Cp, LANE), Cp)]

        def copies(step):
            s = step % 2
            mine = place(c, step * tr, tr)
            return pltpu.make_async_copy(stage.at[s], mine, lsem.at[s]), _rcopy(stage.at[s], mine, ssem.at[s], rsem, sib)

        @pl.when(b >= 2)
        def _():
            loc, rem = copies(b - 2)
            loc.wait()
            rem.wait_send()

        acc = own_ref[...].astype(F32)
        for r in (l0_ref, l1_ref, l2_ref):
            acc = acc + r[...].astype(F32)
        stage[b % 2] = acc
        loc, rem = copies(b)
        loc.start()
        rem.start()

        @pl.when(b == nb - 1)
        def _():
            for step in range(max(0, nb - 2), nb):
                loc, rem = copies(step)
                loc.wait()
                rem.wait_send()
            theirs = place(1 - c, 0, Rp)
            _rcopy(theirs, theirs, ssem.at[0], rsem, sib).wait_recv()

    blk = (None, tr, Cp)
    grid_spec = pltpu.PrefetchScalarGridSpec(
        num_scalar_prefetch=1, grid=(nb,),
        in_specs=[pl.BlockSpec(blk, lambda b, me_ref: (me_ref[0], b, 0))]
        + [pl.BlockSpec(blk, functools.partial(lambda b, me_ref, k: (k, b, 0), k=k)) for k in range(3)],
        out_specs=ANY,
        scratch_shapes=[pltpu.VMEM((2, tr, Cp), F32), pltpu.SemaphoreType.DMA((2,)), pltpu.SemaphoreType.DMA,
                        pltpu.SemaphoreType.DMA((2,))])
    me = (2 * lax.axis_index("x") + lax.axis_index("y")).astype(jnp.int32).reshape(1)
    return pl.pallas_call(
        body, name=name, grid_spec=grid_spec, out_shape=jax.ShapeDtypeStruct((R, C), F32),
        compiler_params=_params("arbitrary"))(me, h, landed, landed, landed)


def _pair_share(name, land):
    G, R, C = land.shape
    Rh = R // 2
    tr = _tile_rows(Rh, max(16, EXCHANGE_CHUNK_BYTES // (2 * C)), 16)
    chunks = [(k, b) for k in range(3) for b in range(Rh // tr)]

    def body(src, dst, buf, lsem, ssem, rsem):
        x, y, c, _, _, chip_idx = _place()
        sib = (x, y, 1 - c)

        def region(ref, k, half, r0, rows):
            return ref.at[chip_idx[k], pl.ds(pl.multiple_of(half * Rh + r0, 16), rows)]

        def load(t):
            k, b = chunks[t]
            return pltpu.make_async_copy(region(src, k, c, b * tr, tr), buf.at[t % 2], lsem.at[t % 2])

        def send(t):
            k, b = chunks[t]
            return _rcopy(buf.at[t % 2], region(dst, k, c, b * tr, tr), ssem.at[t % 2], rsem.at[k], sib)

        load(0).start()
        for t in range(len(chunks)):
            load(t).wait()
            if t + 1 < len(chunks):
                if t >= 1:
                    send(t - 1).wait_send()
                load(t + 1).start()
            send(t).start()
        for t in range(max(0, len(chunks) - 2), len(chunks)):
            send(t).wait_send()
        for k in range(3):
            theirs = region(dst, k, 1 - c, 0, Rh)
            _rcopy(theirs, theirs, ssem.at[0], rsem.at[k], sib).wait_recv()

    return pl.pallas_call(
        body, name=name, in_specs=[ANY], out_specs=ANY, out_shape=jax.ShapeDtypeStruct(land.shape, land.dtype),
        input_output_aliases={0: 0},
        scratch_shapes=[pltpu.VMEM((2, tr, C), land.dtype), pltpu.SemaphoreType.DMA((2,)), pltpu.SemaphoreType.DMA((2,)),
                        pltpu.SemaphoreType.DMA((3,))],
    )(land)


def _allreduce_small(v):
    R, K = v.shape
    ndev = 8

    def body(v_ref, o_ref, land, ssem, rsem):
        x, y, c = lax.axis_index("x"), lax.axis_index("y"), lax.axis_index("c")
        me = 4 * x + 2 * y + c
        land[me] = v_ref[...]
        cps = []
        for r in range(1, ndev):
            fx, fy, fc = (r >> 2) & 1, (r >> 1) & 1, r & 1
            peer = (x ^ fx, y ^ fy, c ^ fc)
            cp = _rcopy(v_ref, land.at[me], ssem.at[r - 1], rsem.at[r - 1], peer)
            cp.start()
            cps.append((cp, 4 * peer[0] + 2 * peer[1] + peer[2], r))
        for cp, src, r in cps:
            cp.wait_send()
            _rcopy(v_ref, land.at[src], ssem.at[r - 1], rsem.at[r - 1], (x, y, c)).wait_recv()
        acc = land[0]
        for d in range(1, ndev):
            acc = acc + land[d]
        o_ref[...] = acc

    vm = pl.BlockSpec(memory_space=pltpu.VMEM)
    return pl.pallas_call(
        body, name="allreduce_small", in_specs=[vm], out_specs=vm, out_shape=jax.ShapeDtypeStruct((R, K), F32),
        scratch_shapes=[pltpu.VMEM((ndev, R, K), F32), pltpu.SemaphoreType.DMA((ndev - 1,)), pltpu.SemaphoreType.DMA((ndev - 1,))],
    )(v)


IN_SPLITS = (Q_RANK, KV_RANK, QK_ROPE, DIL_HEADS * HEAD, DIL_HEADS * HEAD, DIL_HEADS * HEAD, D_MODEL, D_MODEL)
IN_OFF = tuple(int(v) for v in np.cumsum((0,) + IN_SPLITS))


def _unshard_cols(g):
    G, K, Ns = g.shape
    return g.transpose(1, 0, 2).reshape(K, G * Ns)


def _shard_cols(w):
    K, N = w.shape
    return w.reshape(K, N_CHIPS, N // N_CHIPS).transpose(1, 0, 2)


def _rope_pad(w):
    half = QK_ROPE // 2
    z = jnp.zeros(w.shape[:-1] + (half,), w.dtype)
    return jnp.concatenate([w[..., :half], z, w[..., half:], z], axis=-1)


def _rope_unpad(w):
    half = QK_ROPE // 2
    return jnp.concatenate([w[..., :half], w[..., 2 * half:3 * half]], axis=-1)


def _split_w_in(w_in_g):
    G, K, Ns = w_in_g.shape

    def cols(lo, hi):
        pieces = [w_in_g[k][:, max(lo, k * Ns) - k * Ns:min(hi, (k + 1) * Ns) - k * Ns]
                  for k in range(G) if max(lo, k * Ns) < min(hi, (k + 1) * Ns)]
        return pieces[0] if len(pieces) == 1 else jnp.concatenate(pieces, axis=1)

    p = [(IN_OFF[i], IN_OFF[i + 1]) for i in range(8)]
    w_lat = jnp.concatenate([cols(*p[0]), cols(*p[1]), _rope_pad(cols(*p[2])),
                             jnp.zeros((K, LAT_W - _KPE.stop), w_in_g.dtype)], axis=1)
    w_dil = [jnp.concatenate([cols(p[3 + t][0] + g * DIL_O, p[3 + t][0] + (g + 1) * DIL_O) for t in range(3)], axis=1)
             for g in range(DIL_GROUPS)]
    w_gate = cols(p[6][0], p[7][1])
    return w_lat, w_dil, w_gate


def _merge_dw_in(dw_lat, dw_dil, dw_gate):
    parts = [dw_lat[:, _CQ], dw_lat[:, _CKV], _rope_unpad(dw_lat[:, _KPE])]
    for t in range(3):
        parts += [dw_dil[g][:, t * DIL_O:(t + 1) * DIL_O] for g in range(DIL_GROUPS)]
    parts.append(dw_gate)
    width = sum(p.shape[1] for p in parts) // N_CHIPS
    shards = []
    for k in range(N_CHIPS):
        pieces, at = [], 0
        for p in parts:
            lo, hi = max(k * width, at), min((k + 1) * width, at + p.shape[1])
            if lo < hi:
                pieces.append(p[:, lo - at:hi - at])
            at += p.shape[1]
        shards.append(jnp.concatenate(pieces, axis=1))
    return jnp.stack(shards)


def _split_w_uq(w_uq_g):
    w = _unshard_cols(w_uq_g)
    K = w.shape[0]
    w = w.reshape(K, MLA_HEADS, QK_NOPE + QK_ROPE)
    return w[:, :, :QK_NOPE].reshape(K, MLA_HEADS * HEAD), _rope_pad(w[:, :, QK_NOPE:]).reshape(K, MLA_HEADS * HEAD)


def _merge_dw_uq(dw_n, dw_p):
    K = dw_n.shape[0]
    w = jnp.concatenate([dw_n.reshape(K, MLA_HEADS, HEAD), _rope_unpad(dw_p.reshape(K, MLA_HEADS, HEAD))], axis=-1)
    return _shard_cols(w.reshape(K, MLA_HEADS * (QK_NOPE + QK_ROPE)))


def _split_w_ukv(w_ukv_g):
    w = _unshard_cols(w_ukv_g)
    K = w.shape[0]
    w = w.reshape(K, MLA_HEADS, 2 * HEAD)
    return w[:, :, :HEAD].reshape(K, MLA_HEADS * HEAD), w[:, :, HEAD:].reshape(K, MLA_HEADS * HEAD)


def _merge_dw_ukv(dw_k, dw_v):
    K = dw_k.shape[0]
    w = jnp.concatenate([dw_k.reshape(K, MLA_HEADS, HEAD), dw_v.reshape(K, MLA_HEADS, HEAD)], axis=-1)
    return _shard_cols(w.reshape(K, MLA_HEADS * 2 * HEAD))


GATHER_GROUPS = (("w_in",), ("w_uq", "w_ukv", "w_o_mla", "w_o_dil", "w_out"), ("w_up", "w_down", "conv_w"))
SHARED_FETCH = ("w_in",)
REDUCE_GROUPS = (("w_down", "w_up"), ("w_out", "w_o_mla", "w_o_dil"), ("w_uq", "w_ukv", "w_in"))


def _local_step(x, tgt, W, fetch, emit):
    S, D = x.shape
    cos, sin_s = _rope_tables(S)
    w_lat, w_dil, w_gate = _split_w_in(fetch(0, x)["w_in"])

    h = _rmsnorm_fwd("attn_norm", x, W["attn_norm_g"])
    lat = _mm_nn("proj_lat", h, w_lat)
    qkv = [_mm_nn(f"proj_dil{g}", h, w_dil[g], o_dtype=BF) for g in range(DIL_GROUPS)]
    gpre = _mm_nn("proj_gate", h, w_gate, o_dtype=BF)
    WB = fetch(1, gpre)
    w_uqn, w_uqp = _split_w_uq(WB["w_uq"])
    w_k, w_v = _split_w_ukv(WB["w_ukv"])
    w_o_mla, w_o_dil = WB["w_o_mla"], WB["w_o_dil"]
    w_out = WB["w_out"].reshape(D, D)
    qn_, kvn, kpe = _mla_prep(lat, W["q_norm_g"], W["kv_norm_g"], cos, sin_s)
    q_nope = _mm_nn("q_nope", qn_, w_uqn, o_dtype=BF)
    q_pe = _rope("q_rope", _mm_nn("q_pe", qn_, w_uqp), cos, sin_s, False)
    k_nope = _mm_nn("k_nope", kvn, w_k, o_dtype=BF)
    v_mla = _mm_nn("v_mla", kvn, w_v, o_dtype=BF)
    attn_a, lse_a = _mla_fwd(q_nope, q_pe, k_nope, kpe, v_mla)
    dil = [_dil_fwd(qkv[g], g) for g in range(DIL_GROUPS)]
    attn_b, lse_b = _dil_combine([o for o, _ in dil], [l for _, l in dil])
    o_a = _mm_nn("o_mla", attn_a, w_o_mla, o_dtype=BF)
    o_b = _mm_nn("o_dil", attn_b, w_o_dil, o_dtype=BF)
    merge = _merge_fwd(gpre, W["b_gate"], o_a, o_b)
    x1 = _mm_nn("out_proj", merge, w_out, add=x)
    WC = fetch(2, merge)
    w_up = WC["w_up"]
    G4, _, C = w_up.shape
    w_down = WC["w_down"].reshape(G4 // 2, C, D)
    conv_w = WC["conv_w"]
    conv_b = W["conv_b"].reshape(G4, 1, C)
    h2 = _rmsnorm_fwd("ffn_norm", x1, W["ffn_norm_g"])
    u_pre = _up_fwd(h2, w_up)
    act, u = _ffn_act(u_pre, conv_w, conv_b)
    x2 = _down_fwd(act, w_down, x1)
    dx2, d_final_g, loss8 = _final_loss(x2, tgt, W["final_norm_g"])

    d_act = _down_dgrad(dx2, w_down)
    dw_down = _down_wgrad(act, dx2)
    du_pre, d_conv_w, d_conv_b = _ffn_act_conv_bwd(u, d_act, u_pre, conv_w)
    dh2 = _up_dgrad(du_pre, w_up)
    dw_up = _up_wgrad(h2, du_pre)
    zero = emit(0, {"w_down": dw_down.reshape(N_CHIPS, (G4 // 2) * C // N_CHIPS, D), "w_up": dw_up})
    dx1, d_ffn_g = _rmsnorm_bwd("ffn_norm_bwd", dh2, x1, W["ffn_norm_g"] + zero, dx2)
    d_merge = _mm_nt("out_proj_dgrad", dx1, w_out, o_dtype=BF)
    dw_out = _mm_tn("out_proj_wgrad", merge, dx1)
    d_oa, d_ob, d_gpre, d_b_gate = _merge_bwd(d_merge, gpre, W["b_gate"], o_a, o_b)
    d_attn_a = _mm_nt("o_mla_dgrad", d_oa, w_o_mla, o_dtype=BF)
    dw_o_mla = _mm_tn("o_mla_wgrad", attn_a, d_oa, shards=N_CHIPS)
    d_attn_b = _mm_nt("o_dil_dgrad", d_ob, w_o_dil, o_dtype=BF)
    dw_o_dil = _mm_tn("o_dil_wgrad", attn_b, d_ob, shards=N_CHIPS)
    zero = emit(1, {"w_out": dw_out.reshape(N_CHIPS, D // N_CHIPS, D), "w_o_mla": dw_o_mla, "w_o_dil": dw_o_dil})
    q_norm_g = W["q_norm_g"] + zero
    delta_b = _dil_delta(d_attn_b, attn_b)
    d_qkv = [_dil_bwd(qkv[g], d_attn_b, lse_b, delta_b, g) for g in range(DIL_GROUPS)]
    dq_nope, dq_pe_rot, dk_nope, dv_mla, dkpe_rot = _mla_bwd(q_nope, q_pe, k_nope, kpe, v_mla, d_attn_a, attn_a, lse_a)
    dq_pe = _rope("q_rope_bwd", dq_pe_rot, cos, sin_s, True)
    d_qn = _mm_nt("q_pe_dgrad", dq_pe, w_uqp, add=_mm_nt("q_nope_dgrad", dq_nope, w_uqn))
    d_kvn = _mm_nt("v_dgrad", dv_mla, w_v, add=_mm_nt("k_nope_dgrad", dk_nope, w_k))
    dw_uq = _merge_dw_uq(_mm_tn("q_nope_wgrad", qn_, dq_nope), _mm_tn("q_pe_wgrad", qn_, dq_pe))
    dw_ukv = _merge_dw_ukv(_mm_tn("k_nope_wgrad", kvn, dk_nope), _mm_tn("v_wgrad", kvn, dv_mla))
    d_lat, d_q_g, d_kv_g = _mla_prep_bwd(lat, q_norm_g, W["kv_norm_g"], cos, sin_s, d_qn, d_kvn, dkpe_rot)
    dw_in = _merge_dw_in(_mm_tn("proj_lat_wgrad", h, d_lat),
                         [_mm_tn(f"proj_dil{g}_wgrad", h, d_qkv[g]) for g in range(DIL_GROUPS)],
                         _mm_tn("proj_gate_wgrad", h, d_gpre))
    zero = emit(2, {"w_uq": dw_uq, "w_ukv": dw_ukv, "w_in": dw_in})
    dh = _mm_nt_sum("proj_lat_dil_dgrad", [d_lat] + d_qkv, [w_lat + zero.astype(BF)] + w_dil)
    dh = _mm_nt("proj_gate_dgrad", d_gpre, w_gate, add=dh)
    grad_x, d_attn_g = _rmsnorm_bwd("attn_norm_bwd", dh, x, W["attn_norm_g"], dx1)

    small = {"attn_norm_g": d_attn_g, "b_gate": d_b_gate, "q_norm_g": d_q_g, "kv_norm_g": d_kv_g,
             "ffn_norm_g": d_ffn_g, "conv_w": d_conv_w, "conv_b": d_conv_b.reshape(1, G4 * C),
             "final_norm_g": d_final_g}
    return loss8[0, 0], grad_x, small


BIG = ("w_in", "w_uq", "w_ukv", "w_o_mla", "w_o_dil", "w_out", "w_up", "w_down")
SMALL = ("attn_norm_g", "b_gate", "q_norm_g", "kv_norm_g", "ffn_norm_g", "conv_w", "conv_b", "final_norm_g")
WEIGHTS = ("attn_norm_g", "w_in", "b_gate", "q_norm_g", "w_uq", "kv_norm_g", "w_ukv", "w_o_mla", "w_o_dil",
           "w_out", "ffn_norm_g", "w_up", "conv_w", "conv_b", "w_down", "final_norm_g")
SMALL_ROWS = 8
COLUMN_MAJOR = ("w_in", "w_up")
HALF_AXIS = {"w_down": 1}


def _gather_start(shards):
    chip = 2 * lax.axis_index("x") + lax.axis_index("y")
    c = lax.axis_index("c")

    def prepare(names, zero):
        srcs, lands = [], []
        for n in names:
            s = shards[n] + zero
            s = s if n == "conv_w" else s.astype(BF)
            lands.append(lax.dynamic_update_slice(lax.empty((N_CHIPS,) + s.shape, s.dtype), s[None], (chip, 0, 0)))
            if n in SHARED_FETCH:
                s = lax.dynamic_slice_in_dim(s, c * (s.shape[0] // 2), s.shape[0] // 2, 0)
            srcs.append(s)
        return srcs, lands

    n0 = len(GATHER_GROUPS[0])
    srcs0, lands0 = prepare(GATHER_GROUPS[0], 0.0)
    sems0, srcs0, lands0, token = _exchange_start("gather_start0", "gather", srcs0, lands0, [list(range(n0))])
    srcs, lands = prepare([n for grp in GATHER_GROUPS[1:] for n in grp], token[0, 0])
    groups, at = [], 0
    for grp in GATHER_GROUPS[1:]:
        groups.append(list(range(at, at + len(grp))))
        at += len(grp)
    sems, srcs, lands, token1 = _exchange_start("gather_start1", "gather", srcs, lands, groups)

    def fetch(i, after):
        if i == 0:
            _, got = _exchange_wait("gather_wait0", "gather", srcs0, lands0, sems0[0], token1)
        else:
            idx = groups[i - 1]
            _, got = _exchange_wait(f"gather_wait{i}", "gather", [srcs[j] for j in idx], [lands[j] for j in idx],
                                    sems[i - 1], after)
        return {n: _pair_share(f"pair_share_{n}", g) if n in SHARED_FETCH else g for n, g in zip(GATHER_GROUPS[i], got)}

    return fetch, token[0, 0]


def _reduce_start(i, grads):
    names = REDUCE_GROUPS[i]
    hs = [_pair_sum(f"pair_sum_{n}", grads[n], HALF_AXIS.get(n, 0)) for n in names]
    lands = [lax.empty((3,) + h.shape[1:], h.dtype) for h in hs]
    sems, hs, lands, token = _exchange_start(f"reduce_start{i}", "scatter", hs, lands, [list(range(len(names)))])
    return (sems[0], hs, lands), token[0, 0]


def _reduce_finish(i, pending, after):
    sems, hs, lands = pending
    hs, lands = _exchange_wait(f"reduce_wait{i}", "scatter", hs, lands, sems, after)
    out = {}
    for n, h, landed in zip(REDUCE_GROUPS[i], hs, lands):
        out[n] = _chip_total_join(f"chip_total_{n}", h, landed, HALF_AXIS.get(n, 0))
    return out


def _reduce_small(small):
    names = tuple(small)
    flat = [small[n].reshape(-1) for n in names]
    sizes = [f.shape[0] for f in flat]
    total = sum(sizes)
    width = -(-total // (SMALL_ROWS * LANE)) * LANE
    packed = jnp.concatenate(flat + [jnp.zeros((SMALL_ROWS * width - total,), F32)]).reshape(SMALL_ROWS, width)
    red = _allreduce_small(packed).reshape(-1)
    out, off = {}, 0
    for n, s in zip(names, sizes):
        out[n] = red[off:off + s]
        off += s
    return out


def kernel(x, attn_norm_g, w_in, b_gate, q_norm_g, w_uq, kv_norm_g, w_ukv, w_o_mla, w_o_dil, w_out, ffn_norm_g, w_up, conv_w, conv_b, w_down, final_norm_g, loss_target, m_attn_norm_g, m_w_in, m_b_gate, m_q_norm_g, m_w_uq, m_kv_norm_g, m_w_ukv, m_w_o_mla, m_w_o_dil, m_w_out, m_ffn_norm_g, m_w_up, m_conv_w, m_conv_b, m_w_down, m_final_norm_g, v_attn_norm_g, v_w_in, v_b_gate, v_q_norm_g, v_w_uq, v_kv_norm_g, v_w_ukv, v_w_o_mla, v_w_o_dil, v_w_out, v_ffn_norm_g, v_w_up, v_conv_w, v_conv_b, v_w_down, v_final_norm_g):
    given = dict(attn_norm_g=attn_norm_g, w_in=w_in, b_gate=b_gate, q_norm_g=q_norm_g, w_uq=w_uq, kv_norm_g=kv_norm_g,
                 w_ukv=w_ukv, w_o_mla=w_o_mla, w_o_dil=w_o_dil, w_out=w_out, ffn_norm_g=ffn_norm_g, w_up=w_up,
                 conv_w=conv_w, conv_b=conv_b, w_down=w_down, final_norm_g=final_norm_g)
    moments_m = dict(attn_norm_g=m_attn_norm_g, w_in=m_w_in, b_gate=m_b_gate, q_norm_g=m_q_norm_g, w_uq=m_w_uq,
                     kv_norm_g=m_kv_norm_g, w_ukv=m_w_ukv, w_o_mla=m_w_o_mla, w_o_dil=m_w_o_dil, w_out=m_w_out,
                     ffn_norm_g=m_ffn_norm_g, w_up=m_w_up, conv_w=m_conv_w, conv_b=m_conv_b, w_down=m_w_down,
                     final_norm_g=m_final_norm_g)
    moments_v = dict(attn_norm_g=v_attn_norm_g, w_in=v_w_in, b_gate=v_b_gate, q_norm_g=v_q_norm_g, w_uq=v_w_uq,
                     kv_norm_g=v_kv_norm_g, w_ukv=v_w_ukv, w_o_mla=v_w_o_mla, w_o_dil=v_w_o_dil, w_out=v_w_out,
                     ffn_norm_g=v_ffn_norm_g, w_up=v_w_up, conv_w=v_conv_w, conv_b=v_conv_b, w_down=v_w_down,
                     final_norm_g=v_final_norm_g)

    fetch, zero = _gather_start({n: given[n][0] for n in BIG + ("conv_w",)})
    W = {n: given[n] for n in ("b_gate", "q_norm_g", "kv_norm_g", "ffn_norm_g", "conv_b")}
    W["attn_norm_g"] = given["attn_norm_g"] + zero
    W["final_norm_g"] = given["final_norm_g"].reshape(1, -1)

    pending = {}

    def emit(i, grads):
        pending[i], token = _reduce_start(i, grads)
        return token

    loss_part, grad_x, small = _local_step(x[0], loss_target[0], W, fetch, emit)
    small["loss"] = loss_part
    grads, delta, new_m, new_v = {}, {}, {}, {}

    def adamw(n, g):
        shp = given[n].shape
        two_d = (-1, shp[-1]) if len(shp) > 1 else (1, -1)
        view = (lambda a: a.reshape(two_d).T) if n in COLUMN_MAJOR else (lambda a: a.reshape(two_d))
        back = (lambda a: a.T.reshape(shp)) if n in COLUMN_MAJOR else (lambda a: a.reshape(shp))
        go, d, nm, nv = _adamw(f"adamw_{n}", view(given[n]), view(g), view(moments_m[n]), view(moments_v[n]))
        grads[n], delta[n], new_m[n], new_v[n] = back(go), back(d), back(nm), back(nv)

    after = grad_x
    for i in range(len(REDUCE_GROUPS)):
        for n, g in _reduce_finish(i, pending[i], after).items():
            adamw(n, g)
        after = delta[REDUCE_GROUPS[i][-1]]
    g_small = _reduce_small(small)
    loss = g_small["loss"][0]
    chip = 2 * lax.axis_index("x") + lax.axis_index("y")
    for n in SMALL:
        if n == "conv_w":
            full = g_small[n].reshape(N_CHIPS, 3, -1)
            adamw(n, lax.dynamic_index_in_dim(full, chip, 0, keepdims=True))
        else:
            adamw(n, g_small[n])

    return (loss, grad_x[None], *[grads[n] for n in WEIGHTS], *[delta[n] for n in WEIGHTS],
            *[new_m[n] for n in WEIGHTS], *[new_v[n] for n in WEIGHTS])
```

```python
import functools
import math

import numpy as np
import jax
import jax.numpy as jnp
from jax import lax
from jax.experimental import pallas as pl
from jax.experimental.pallas import tpu as pltpu

F32 = jnp.float32
BF = jnp.bfloat16
MESH = pl.DeviceIdType.MESH

D_MODEL = 2048
MLA_HEADS = 8
QK_NOPE = 128
QK_ROPE = 64
Q_RANK = 512
KV_RANK = 256
ROPE_THETA = 10000.0
DIL_PATTERNS = ((128, 1), (512, 4), (2048, 16))
DIL_GROUPS = 3
DIL_HPG = 4
DIL_HEADS = 12
HEAD = 128
DIL_BLOCK = 128
ALIBI_MAX_BIAS = 8.0
NORM_EPS = 1e-6
N_CHIPS = 4
ADAM_LR = 0.001
ADAM_B1 = 0.9
ADAM_B2 = 0.999
ADAM_EPS = 1e-08
ADAM_WD = 0.01
ADAM_STEP = 10

LANE = 128
VMEM_LIMIT = 56 * 1024 * 1024
MLA_SCALE = (QK_NOPE + QK_ROPE) ** -0.5
DIL_SCALE = HEAD ** -0.5


def _params(*sem):
    return pltpu.CompilerParams(dimension_semantics=sem, vmem_limit_bytes=VMEM_LIMIT)


def _tile(n, pref):
    t = (pref // LANE) * LANE
    while t >= LANE:
        if n % t == 0:
            return t
        t -= LANE
    return n


NN = (((1,), (0,)), ((), ()))
NT = (((1,), (1,)), ((), ()))
TN = (((0,), (0,)), ((), ()))


def _mm_call(name, a, b, add, *, grid, a_spec, b_spec, add_spec, o_spec, o_shape, o_dtype, acc_shape, dims, nk):
    nax = len(grid)

    def body(*refs):
        if add is None:
            a_ref, b_ref, o_ref = refs[:3]
            c_ref = None
            scr = refs[3:]
        else:
            a_ref, b_ref, c_ref, o_ref = refs[:4]
            scr = refs[4:]
        prod = lax.dot_general(a_ref[...].astype(BF), b_ref[...].astype(BF), dims, preferred_element_type=F32)
        if nk == 1:
            if c_ref is not None:
                prod = prod + c_ref[...]
            o_ref[...] = prod.astype(o_ref.dtype)
        else:
            acc = scr[0]
            k = pl.program_id(nax - 1)

            @pl.when(k == 0)
            def _():
                if c_ref is not None:
                    acc[...] = prod + c_ref[...]
                else:
                    acc[...] = prod

            @pl.when(k > 0)
            def _():
                acc[...] += prod

            @pl.when(k == nk - 1)
            def _():
                o_ref[...] = acc[...].astype(o_ref.dtype)

    ins = [a, b] + ([] if add is None else [add])
    specs = [a_spec, b_spec] + ([] if add is None else [add_spec])
    sem = ("parallel",) * (nax - 1) + ("arbitrary",)
    return pl.pallas_call(
        body, name=name, grid=grid, in_specs=specs, out_specs=o_spec,
        out_shape=jax.ShapeDtypeStruct(o_shape, o_dtype),
        scratch_shapes=[] if nk == 1 else [pltpu.VMEM(acc_shape, F32)],
        compiler_params=_params(*sem),
    )(*ins)


def _mm_nn(name, a, b, *, add=None, o_dtype=F32):
    M, K = a.shape
    sharded = b.ndim == 3
    Ns = b.shape[-1]
    N = Ns * (b.shape[0] if sharded else 1)
    tm, tn, tk = _tile(M, 1024), _tile(Ns, 1024), _tile(K, 2048)
    per = Ns // tn
    nk = K // tk
    if sharded:
        b_spec = pl.BlockSpec((None, tk, tn), lambda i, j, k: (j // per, k, j % per))
    else:
        b_spec = pl.BlockSpec((tk, tn), lambda i, j, k: (k, j))
    return _mm_call(
        name, a, b, add, grid=(M // tm, N // tn, nk),
        a_spec=pl.BlockSpec((tm, tk), lambda i, j, k: (i, k)), b_spec=b_spec,
        add_spec=pl.BlockSpec((tm, tn), lambda i, j, k: (i, j)),
        o_spec=pl.BlockSpec((tm, tn), lambda i, j, k: (i, j)),
        o_shape=(M, N), o_dtype=o_dtype, acc_shape=(tm, tn), dims=NN, nk=nk)


def _mm_nt(name, a, b, *, add=None, o_dtype=F32):
    M, K = a.shape
    sharded = b.ndim == 3
    N, Ks = b.shape[-2], b.shape[-1]
    tm, tn, tk = _tile(M, 1024), _tile(N, 1024), _tile(Ks, 2048)
    per = Ks // tk
    nk = K // tk
    if sharded:
        b_spec = pl.BlockSpec((None, tn, tk), lambda i, j, k: (k // per, j, k % per))
    else:
        b_spec = pl.BlockSpec((tn, tk), lambda i, j, k: (j, k))
    return _mm_call(
        name, a, b, add, grid=(M // tm, N // tn, nk),
        a_spec=pl.BlockSpec((tm, tk), lambda i, j, k: (i, k)), b_spec=b_spec,
        add_spec=pl.BlockSpec((tm, tn), lambda i, j, k: (i, j)),
        o_spec=pl.BlockSpec((tm, tn), lambda i, j, k: (i, j)),
        o_shape=(M, N), o_dtype=o_dtype, acc_shape=(tm, tn), dims=NT, nk=nk)


def _mm_nt_sum(name, a_s, b_s):
    n = len(a_s)
    M, N = a_s[0].shape[0], b_s[0].shape[0]
    tm, tn = _tile(M, 512), _tile(N, 1024)

    def body(*refs):
        acc = None
        for i in range(n):
            prod = lax.dot_general(refs[i][...].astype(BF), refs[n + i][...].astype(BF), NT, preferred_element_type=F32)
            acc = prod if acc is None else acc + prod
        refs[2 * n][...] = acc

    return pl.pallas_call(
        body, name=name, grid=(N // tn, M // tm),
        in_specs=[pl.BlockSpec((tm, a.shape[1]), lambda j, i: (i, 0)) for a in a_s]
        + [pl.BlockSpec((tn, b.shape[1]), lambda j, i: (j, 0)) for b in b_s],
        out_specs=pl.BlockSpec((tm, tn), lambda j, i: (i, j)),
        out_shape=jax.ShapeDtypeStruct((M, N), F32),
        compiler_params=_params("parallel", "parallel"))(*a_s, *b_s)


def _mm_tn(name, a, b, *, shards=1, o_dtype=BF):
    S, M = a.shape
    N = b.shape[1]
    Ns = N // shards
    tm, tn, tk = _tile(M, 1024), _tile(Ns, 1024), _tile(S, 2048)
    per = Ns // tn
    nk = S // tk
    if shards > 1:
        o_spec = pl.BlockSpec((None, tm, tn), lambda i, j, k: (j // per, i, j % per))
        o_shape = (shards, M, Ns)
    else:
        o_spec = pl.BlockSpec((tm, tn), lambda i, j, k: (i, j))
        o_shape = (M, N)
    return _mm_call(
        name, a, b, None, grid=(M // tm, N // tn, nk),
        a_spec=pl.BlockSpec((tk, tm), lambda i, j, k: (k, i)),
        b_spec=pl.BlockSpec((tk, tn), lambda i, j, k: (k, j)),
        add_spec=None, o_spec=o_spec, o_shape=o_shape, o_dtype=o_dtype, acc_shape=(tm, tn), dims=TN, nk=nk)


def _up_fwd(h2, w_up):
    S, D = h2.shape
    G, _, C = w_up.shape
    tm = _tile(S, 512)
    return _mm_call(
        "up_fwd", h2, w_up, None, grid=(G, S // tm, 1),
        a_spec=pl.BlockSpec((tm, D), lambda g, i, k: (i, 0)),
        b_spec=pl.BlockSpec((None, D, C), lambda g, i, k: (g, 0, 0)),
        add_spec=None, o_spec=pl.BlockSpec((None, tm, C), lambda g, i, k: (g, i, 0)),
        o_shape=(G, S, C), o_dtype=BF, acc_shape=None, dims=NN, nk=1)


def _up_dgrad(du_pre, w_up):
    G, S, C = du_pre.shape
    D = w_up.shape[1]
    tm, tn = _tile(S, 1024), _tile(D, 1024)
    return _mm_call(
        "up_dgrad", du_pre, w_up, None, grid=(S // tm, D // tn, G),
        a_spec=pl.BlockSpec((None, tm, C), lambda i, j, g: (g, i, 0)),
        b_spec=pl.BlockSpec((None, tn, C), lambda i, j, g: (g, j, 0)),
        add_spec=None, o_spec=pl.BlockSpec((tm, tn), lambda i, j, g: (i, j)),
        o_shape=(S, D), o_dtype=F32, acc_shape=(tm, tn), dims=NT, nk=G)


def _up_wgrad(h2, du_pre):
    G, S, C = du_pre.shape
    D = h2.shape[1]
    tm, tk = _tile(D, 512), _tile(S, 2048)
    return _mm_call(
        "up_wgrad", h2, du_pre, None, grid=(G, D // tm, S // tk),
        a_spec=pl.BlockSpec((tk, tm), lambda g, i, k: (k, i)),
        b_spec=pl.BlockSpec((None, tk, C), lambda g, i, k: (g, k, 0)),
        add_spec=None, o_spec=pl.BlockSpec((None, tm, C), lambda g, i, k: (g, i, 0)),
        o_shape=(G, D, C), o_dtype=BF, acc_shape=(tm, C), dims=TN, nk=S // tk)


def _down_fwd(act, w_down, x1):
    G, S, C = act.shape
    D = w_down.shape[2]
    tm, tn = _tile(S, 1024), _tile(D, 1024)
    return _mm_call(
        "down_fwd", act, w_down, x1, grid=(S // tm, D // tn, G),
        a_spec=pl.BlockSpec((None, tm, C), lambda i, j, g: (g, i, 0)),
        b_spec=pl.BlockSpec((None, C, tn), lambda i, j, g: (g, 0, j)),
        add_spec=pl.BlockSpec((tm, tn), lambda i, j, g: (i, j)),
        o_spec=pl.BlockSpec((tm, tn), lambda i, j, g: (i, j)),
        o_shape=(S, D), o_dtype=F32, acc_shape=(tm, tn), dims=NN, nk=G)


def _down_dgrad(dx2, w_down):
    S, D = dx2.shape
    G, C, _ = w_down.shape
    tm = _tile(S, 512)
    return _mm_call(
        "down_dgrad", dx2, w_down, None, grid=(G, S // tm, 1),
        a_spec=pl.BlockSpec((tm, D), lambda g, i, k: (i, 0)),
        b_spec=pl.BlockSpec((None, C, D), lambda g, i, k: (g, 0, 0)),
        add_spec=None, o_spec=pl.BlockSpec((None, tm, C), lambda g, i, k: (g, i, 0)),
        o_shape=(G, S, C), o_dtype=BF, acc_shape=None, dims=NT, nk=1)


def _down_wgrad(act, dx2):
    G, S, C = act.shape
    D = dx2.shape[1]
    tn, tk = _tile(D, 512), _tile(S, 1024)
    return _mm_call(
        "down_wgrad", act, dx2, None, grid=(G, D // tn, S // tk),
        a_spec=pl.BlockSpec((None, tk, C), lambda g, j, k: (g, k, 0)),
        b_spec=pl.BlockSpec((tk, tn), lambda g, j, k: (k, j)),
        add_spec=None, o_spec=pl.BlockSpec((None, C, tn), lambda g, j, k: (g, 0, j)),
        o_shape=(G, C, D), o_dtype=BF, acc_shape=(C, tn), dims=TN, nk=S // tk)


def _row(ts, c):
    return pl.BlockSpec((ts, c), lambda i: (i, 0))


def _bcast(r, c):
    return pl.BlockSpec((r, c), lambda i: (0, 0))


def _accumulate(i, ref, val):
    @pl.when(i == 0)
    def _():
        ref[...] = val

    @pl.when(i > 0)
    def _():
        ref[...] += val


def _rstd(xv):
    return lax.rsqrt(jnp.mean(xv * xv, axis=-1, keepdims=True) + NORM_EPS)


def _rmsnorm_fwd(name, x, g):
    S, D = x.shape
    ts = _tile(S, 512)

    def body(x_ref, g_ref, o_ref):
        xv = x_ref[...]
        o_ref[...] = (xv * _rstd(xv) * g_ref[...]).astype(o_ref.dtype)

    return pl.pallas_call(
        body, name=name, grid=(S // ts,), in_specs=[_row(ts, D), _bcast(1, D)], out_specs=_row(ts, D),
        out_shape=jax.ShapeDtypeStruct((S, D), BF), compiler_params=_params("parallel"))(x, g)


def _norm_bwd_rows(dy, xv, g):
    r = _rstd(xv)
    xh = xv * r
    dxh = dy * g
    dx = r * (dxh - xh * jnp.mean(dxh * xh, axis=-1, keepdims=True))
    return dx, jnp.sum(dy * xh, axis=0, keepdims=True)


def _rmsnorm_bwd(name, dy, x, g, res):
    S, D = x.shape
    ts = _tile(S, 512)

    def body(dy_ref, x_ref, g_ref, res_ref, dx_ref, dg_ref):
        dx, dg = _norm_bwd_rows(dy_ref[...], x_ref[...], g_ref[...])
        dx_ref[...] = dx + res_ref[...]
        _accumulate(pl.program_id(0), dg_ref, dg)

    return pl.pallas_call(
        body, name=name, grid=(S // ts,),
        in_specs=[_row(ts, D), _row(ts, D), _bcast(1, D), _row(ts, D)],
        out_specs=[_row(ts, D), _bcast(1, D)],
        out_shape=[jax.ShapeDtypeStruct((S, D), F32), jax.ShapeDtypeStruct((1, D), F32)],
        compiler_params=_params("arbitrary"))(dy, x, g, res)


def _rope_tables(S):
    half = QK_ROPE // 2
    pos = jnp.arange(S, dtype=F32)
    inv_freq = ROPE_THETA ** (-jnp.arange(0, QK_ROPE, 2, dtype=F32) / QK_ROPE)
    ang = pos[:, None] * inv_freq[None, :]
    cos, sin = jnp.cos(ang), jnp.sin(ang)
    z = jnp.zeros((S, half), F32)
    return jnp.concatenate([cos, z, cos, z], axis=1), jnp.concatenate([-sin, z, sin, z], axis=1)


def _rope_lanes(x, cos, sin_signed, inverse):
    if inverse:
        return x * cos + pltpu.roll(x * sin_signed, LANE // 2, 1)
    return x * cos + pltpu.roll(x, LANE // 2, 1) * sin_signed


def _rope(name, x, cos, sin_signed, inverse):
    S, W = x.shape
    ts = _tile(S, 512)

    def body(x_ref, c_ref, s_ref, o_ref):
        c, s = c_ref[...], s_ref[...]
        for h in range(W // LANE):
            sl = slice(h * LANE, (h + 1) * LANE)
            o_ref[:, sl] = _rope_lanes(x_ref[:, sl], c, s, inverse).astype(o_ref.dtype)

    return pl.pallas_call(
        body, name=name, grid=(S // ts,), in_specs=[_row(ts, W), _row(ts, LANE), _row(ts, LANE)],
        out_specs=_row(ts, W), out_shape=jax.ShapeDtypeStruct((S, W), BF),
        compiler_params=_params("parallel"))(x, cos, sin_signed)


LAT_W = 1024
_CQ = slice(0, Q_RANK)
_CKV = slice(Q_RANK, Q_RANK + KV_RANK)
_KPE = slice(Q_RANK + KV_RANK, Q_RANK + KV_RANK + LANE)


def _mla_prep(lat, qg, kvg, cos, sin_signed):
    S = lat.shape[0]
    ts = _tile(S, 512)

    def body(lat_ref, qg_ref, kvg_ref, c_ref, s_ref, qn_ref, kvn_ref, kpe_ref):
        cq = lat_ref[:, _CQ]
        qn_ref[...] = (cq * _rstd(cq) * qg_ref[...]).astype(BF)
        ckv = lat_ref[:, _CKV]
        kvn_ref[...] = (ckv * _rstd(ckv) * kvg_ref[...]).astype(BF)
        kpe_ref[...] = _rope_lanes(lat_ref[:, _KPE], c_ref[...], s_ref[...], False).astype(BF)

    return pl.pallas_call(
        body, name="mla_prep", grid=(S // ts,),
        in_specs=[_row(ts, LAT_W), _bcast(1, Q_RANK), _bcast(1, KV_RANK), _row(ts, LANE), _row(ts, LANE)],
        out_specs=[_row(ts, Q_RANK), _row(ts, KV_RANK), _row(ts, LANE)],
        out_shape=[jax.ShapeDtypeStruct((S, Q_RANK), BF), jax.ShapeDtypeStruct((S, KV_RANK), BF),
                   jax.ShapeDtypeStruct((S, LANE), BF)],
        compiler_params=_params("parallel"))(lat, qg, kvg, cos, sin_signed)


def _mla_prep_bwd(lat, qg, kvg, cos, sin_signed, d_qn, d_kvn, d_kpe):
    S = lat.shape[0]
    ts = _tile(S, 512)

    def body(lat_ref, qg_ref, kvg_ref, c_ref, s_ref, dqn_ref, dkvn_ref, dkpe_ref, dlat_ref, dqg_ref, dkvg_ref):
        i = pl.program_id(0)
        dcq, dqg = _norm_bwd_rows(dqn_ref[...], lat_ref[:, _CQ], qg_ref[...])
        dckv, dkvg = _norm_bwd_rows(dkvn_ref[...], lat_ref[:, _CKV], kvg_ref[...])
        dlat_ref[:, _CQ] = dcq.astype(BF)
        dlat_ref[:, _CKV] = dckv.astype(BF)
        dkpe = dkpe_ref[0]
        for g in range(1, d_kpe.shape[0]):
            dkpe = dkpe + dkpe_ref[g]
        dlat_ref[:, _KPE] = _rope_lanes(dkpe, c_ref[...], s_ref[...], True).astype(BF)
        dlat_ref[:, _KPE.stop:] = jnp.zeros((ts, LAT_W - _KPE.stop), BF)
        _accumulate(i, dqg_ref, dqg)
        _accumulate(i, dkvg_ref, dkvg)

    return pl.pallas_call(
        body, name="mla_prep_bwd", grid=(S // ts,),
        in_specs=[_row(ts, LAT_W), _bcast(1, Q_RANK), _bcast(1, KV_RANK), _row(ts, LANE), _row(ts, LANE),
                  _row(ts, Q_RANK), _row(ts, KV_RANK), pl.BlockSpec((d_kpe.shape[0], ts, LANE), lambda i: (0, i, 0))],
        out_specs=[_row(ts, LAT_W), _bcast(1, Q_RANK), _bcast(1, KV_RANK)],
        out_shape=[jax.ShapeDtypeStruct((S, LAT_W), BF), jax.ShapeDtypeStruct((1, Q_RANK), F32),
                   jax.ShapeDtypeStruct((1, KV_RANK), F32)],
        compiler_params=_params("arbitrary"))(lat, qg, kvg, cos, sin_signed, d_qn, d_kvn, d_kpe)


def _sigmoid(z):
    return 1.0 / (1.0 + jnp.exp(-z))


def _merge_fwd(gpre, b_gate, o_a, o_b):
    S, D = o_a.shape
    ts = _tile(S, 256)

    def body(g_ref, b_ref, oa_ref, ob_ref, m_ref):
        ga = _sigmoid(g_ref[:, :D] + b_ref[:, :D])
        gb = _sigmoid(g_ref[:, D:] + b_ref[:, D:])
        m_ref[...] = (ga * oa_ref[...] + gb * ob_ref[...]).astype(BF)

    return pl.pallas_call(
        body, name="merge_fwd", grid=(S // ts,),
        in_specs=[_row(ts, 2 * D), _bcast(1, 2 * D), _row(ts, D), _row(ts, D)], out_specs=_row(ts, D),
        out_shape=jax.ShapeDtypeStruct((S, D), BF), compiler_params=_params("parallel"))(gpre, b_gate, o_a, o_b)


def _merge_bwd(d_merge, gpre, b_gate, o_a, o_b):
    S, D = o_a.shape
    ts = _tile(S, 256)

    def body(dm_ref, g_ref, b_ref, oa_ref, ob_ref, doa_ref, dob_ref, dg_ref, db_ref):
        dm = dm_ref[...]
        ga = _sigmoid(g_ref[:, :D] + b_ref[:, :D])
        gb = _sigmoid(g_ref[:, D:] + b_ref[:, D:])
        doa_ref[...] = (dm * ga).astype(BF)
        dob_ref[...] = (dm * gb).astype(BF)
        dga = dm * oa_ref[...] * ga * (1.0 - ga)
        dgb = dm * ob_ref[...] * gb * (1.0 - gb)
        dg_ref[:, :D] = dga.astype(BF)
        dg_ref[:, D:] = dgb.astype(BF)
        i = pl.program_id(0)
        part = jnp.concatenate([jnp.sum(dga, axis=0, keepdims=True), jnp.sum(dgb, axis=0, keepdims=True)], axis=1)
        _accumulate(i, db_ref, part)

    return pl.pallas_call(
        body, name="merge_bwd", grid=(S // ts,),
        in_specs=[_row(ts, D), _row(ts, 2 * D), _bcast(1, 2 * D), _row(ts, D), _row(ts, D)],
        out_specs=[_row(ts, D), _row(ts, D), _row(ts, 2 * D), _bcast(1, 2 * D)],
        out_shape=[jax.ShapeDtypeStruct((S, D), BF), jax.ShapeDtypeStruct((S, D), BF),
                   jax.ShapeDtypeStruct((S, 2 * D), BF), jax.ShapeDtypeStruct((1, 2 * D), F32)],
        compiler_params=_params("arbitrary"))(d_merge, gpre, b_gate, o_a, o_b)


def _final_loss(x2, tgt, gf):
    S, D = x2.shape
    ts = _tile(S, 512)

    def body(x_ref, t_ref, g_ref, dx_ref, dg_ref, loss_ref):
        i = pl.program_id(0)
        xv = x_ref[...]
        g = g_ref[...]
        y = xv * _rstd(xv) * g
        err = y - t_ref[...]
        dx, dg = _norm_bwd_rows(err * (1.0 / D), xv, g)
        dx_ref[...] = dx
        _accumulate(i, dg_ref, dg)
        part = 0.5 * jnp.sum(jnp.mean(err * err, axis=-1, keepdims=True), axis=0, keepdims=True)
        _accumulate(i, loss_ref, jnp.broadcast_to(part, (8, LANE)))

    return pl.pallas_call(
        body, name="final_loss", grid=(S // ts,),
        in_specs=[_row(ts, D), _row(ts, D), _bcast(1, D)],
        out_specs=[_row(ts, D), _bcast(1, D), _bcast(8, LANE)],
        out_shape=[jax.ShapeDtypeStruct((S, D), F32), jax.ShapeDtypeStruct((1, D), F32),
                   jax.ShapeDtypeStruct((8, LANE), F32)],
        compiler_params=_params("arbitrary"))(x2, tgt, gf)


HALO = 16


SUB = 8


def _shift_down(cur, prev, k, rows):
    out = pltpu.roll(cur, k, 0)
    head = out[:SUB]
    for j in range(k):
        head = jnp.where(rows == j, prev[HALO - k + j:HALO - k + j + 1, :], head)
    return jnp.concatenate([head, out[SUB:]], axis=0)


def _shift_up(cur, nxt, k, rows, ts):
    out = pltpu.roll(cur, ts - k, 0)
    tail = out[ts - SUB:]
    for j in range(k):
        tail = jnp.where(rows == SUB - k + j, nxt[j:j + 1, :], tail)
    return jnp.concatenate([out[:ts - SUB], tail], axis=0)


def _conv_rows(cur, prev, w, b, rows):
    return b + w[0:1, :] * _shift_down(cur, prev, 2, rows) + w[1:2, :] * _shift_down(cur, prev, 1, rows) + w[2:3, :] * cur


def _conv_specs(ts, C, shard_of):
    nh = ts // HALO
    cur = pl.BlockSpec((None, ts, C), lambda g, i: (shard_of(g), i, 0))
    prev = pl.BlockSpec((None, HALO, C), lambda g, i: (shard_of(g), jnp.maximum(i * nh - 1, 0), 0))
    return cur, prev


def _ffn_act(u_pre, conv_w, conv_b):
    G4, S, C = u_pre.shape
    G = G4 // 2
    ts = _tile(S, 256)

    def body(up_ref, upp_ref, gt_ref, gtp_ref, wu_ref, wg_ref, bu_ref, bg_ref, act_ref, u_ref):
        first = pl.program_id(1) == 0
        rows = lax.broadcasted_iota(jnp.int32, (SUB, C), 0)
        pu = jnp.where(first, 0.0, upp_ref[...].astype(F32))
        pg = jnp.where(first, 0.0, gtp_ref[...].astype(F32))
        up = _conv_rows(up_ref[...].astype(F32), pu, wu_ref[...], bu_ref[...], rows)
        gate = _conv_rows(gt_ref[...].astype(F32), pg, wg_ref[...], bg_ref[...], rows)
        act_ref[...] = (gate * _sigmoid(gate) * up).astype(BF)
        u_ref[0] = up.astype(BF)
        u_ref[1] = gate.astype(BF)

    cur_u, prev_u = _conv_specs(ts, C, lambda g: g)
    cur_g, prev_g = _conv_specs(ts, C, lambda g: g + G)
    w_u = pl.BlockSpec((None, 3, C), lambda g, i: (g, 0, 0))
    w_g = pl.BlockSpec((None, 3, C), lambda g, i: (g + G, 0, 0))
    b_u = pl.BlockSpec((None, 1, C), lambda g, i: (g, 0, 0))
    b_g = pl.BlockSpec((None, 1, C), lambda g, i: (g + G, 0, 0))
    pair = pl.BlockSpec((2, None, ts, C), lambda g, i: (0, g, i, 0))
    act, u = pl.pallas_call(
        body, name="ffn_act", grid=(G, S // ts),
        in_specs=[cur_u, prev_u, cur_g, prev_g, w_u, w_g, b_u, b_g],
        out_specs=[pl.BlockSpec((None, ts, C), lambda g, i: (g, i, 0)), pair],
        out_shape=[jax.ShapeDtypeStruct((G, S, C), BF), jax.ShapeDtypeStruct((2, G, S, C), BF)],
        compiler_params=_params("parallel", "parallel"))(u_pre, u_pre, u_pre, u_pre, conv_w, conv_w, conv_b, conv_b)
    return act, u


def _ffn_act_conv_bwd(u, d_act, u_pre, conv_w):
    _, G, S, C = u.shape
    ts = _tile(S, 256)
    nh = ts // HALO
    last_halo = S // HALO - 1

    def d_outputs(up, gate, da):
        sg = _sigmoid(gate)
        return da * (gate * sg), da * up * (sg * (1.0 + gate * (1.0 - sg)))

    def body(u_ref, un_ref, da_ref, dan_ref, x_ref, w_ref, dpre_ref, dw_ref, db_ref):
        i = pl.program_id(1)
        rows = lax.broadcasted_iota(jnp.int32, (SUB, C), 0)
        da_n = jnp.where(i == pl.num_programs(1) - 1, 0.0, dan_ref[...].astype(F32))
        du = d_outputs(u_ref[0].astype(F32), u_ref[1].astype(F32), da_ref[...].astype(F32))
        du_n = d_outputs(un_ref[0].astype(F32), un_ref[1].astype(F32), da_n)
        for kind in range(2):
            du_c, w = du[kind], w_ref[kind]
            up1, up2 = _shift_up(du_c, du_n[kind], 1, rows, ts), _shift_up(du_c, du_n[kind], 2, rows, ts)
            dpre_ref[kind] = (w[2:3, :] * du_c + w[1:2, :] * up1 + w[0:1, :] * up2).astype(BF)
            x_c = x_ref[kind].astype(F32)
            dw = jnp.concatenate([
                jnp.sum(up2 * x_c, axis=0, keepdims=True),
                jnp.sum(up1 * x_c, axis=0, keepdims=True),
                jnp.sum(du_c * x_c, axis=0, keepdims=True)], axis=0)
            db = jnp.sum(du_c, axis=0, keepdims=True)

            @pl.when(i == 0)
            def _():
                dw_ref[kind] = dw
                db_ref[kind] = db

            @pl.when(i > 0)
            def _():
                dw_ref[kind] += dw
                db_ref[kind] += db

    def nxt_row(i):
        return jnp.minimum((i + 1) * nh, last_halo)

    pair = pl.BlockSpec((2, None, ts, C), lambda g, i: (0, g, i, 0))
    pair_n = pl.BlockSpec((2, None, HALO, C), lambda g, i: (0, g, nxt_row(i), 0))
    one = pl.BlockSpec((None, ts, C), lambda g, i: (g, i, 0))
    one_n = pl.BlockSpec((None, HALO, C), lambda g, i: (g, nxt_row(i), 0))
    w_spec = pl.BlockSpec((2, None, 3, C), lambda g, i: (0, g, 0, 0))
    b_spec = pl.BlockSpec((2, None, 1, C), lambda g, i: (0, g, 0, 0))
    dpre, dw, db = pl.pallas_call(
        body, name="ffn_act_conv_bwd", grid=(G, S // ts),
        in_specs=[pair, pair_n, one, one_n, pair, w_spec], out_specs=[pair, w_spec, b_spec],
        out_shape=[jax.ShapeDtypeStruct((2, G, S, C), BF), jax.ShapeDtypeStruct((2, G, 3, C), F32),
                   jax.ShapeDtypeStruct((2, G, 1, C), F32)],
        compiler_params=_params("parallel", "arbitrary"))(
            u, u, d_act, d_act, u_pre.reshape(2, G, S, C), conv_w.reshape(2, G, 3, C))
    return dpre.reshape(2 * G, S, C), dw.reshape(2 * G, 3, C), db.reshape(2 * G, 1, C)


MLA_T = 1024
MLA_HB = 4
MLA_BWD_HB = 2


def _mla_pairs(n, by_row):
    if by_row:
        pairs = [(i, j) for i in range(n) for j in range(i + 1)]
    else:
        pairs = [(i, j) for j in range(n) for i in range(j, n)]
    return jnp.asarray([p[0] for p in pairs], jnp.int32), jnp.asarray([p[1] for p in pairs], jnp.int32)


def _mla_specs(hb):
    q = pl.BlockSpec((MLA_T, hb * HEAD), lambda g, t, it, jt: (it[t], g))
    k = pl.BlockSpec((MLA_T, hb * HEAD), lambda g, t, it, jt: (jt[t], g))
    kpe = pl.BlockSpec((MLA_T, HEAD), lambda g, t, it, jt: (jt[t], 0))
    lse = pl.BlockSpec((hb, MLA_T, LANE), lambda g, t, it, jt: (g, it[t], 0))
    return q, k, kpe, lse


def _mla_head(ref, hh):
    return ref[:, hh * HEAD:(hh + 1) * HEAD]


LOG2E = math.log2(math.e)
MLA_EXP2_SCALE = MLA_SCALE * LOG2E


def _mla_scores(qn_ref, qp_ref, kn_ref, kpe, hh, ok):
    q = jnp.concatenate([_mla_head(qn_ref, hh), _mla_head(qp_ref, hh)], axis=1)
    k = jnp.concatenate([_mla_head(kn_ref, hh), kpe], axis=1)
    s = lax.dot_general(q, k, NT, preferred_element_type=F32)
    return q, k, s if ok is None else jnp.where(ok, s, -jnp.inf)


def _mla_diagonal_mask():
    row = lax.broadcasted_iota(jnp.int32, (MLA_T, MLA_T), 0)
    col = lax.broadcasted_iota(jnp.int32, (MLA_T, MLA_T), 1)
    return col <= row


def _mla_step(i, j, step):
    @pl.when(j < i)
    def _():
        step(None)

    @pl.when(j == i)
    def _():
        step(_mla_diagonal_mask())


def _mla_fwd(qn, qp, kn, kpe, v):
    S = qn.shape[0]
    it, jt = _mla_pairs(S // MLA_T, True)

    def body(it_ref, jt_ref, qn_ref, qp_ref, kn_ref, kpe_ref, v_ref, o_ref, lse_ref, m_scr, acc_scr):
        t = pl.program_id(1)
        i, j = it_ref[t], jt_ref[t]

        @pl.when(j == 0)
        def _():
            m_scr[...] = jnp.full(m_scr.shape, -jnp.inf, F32)
            acc_scr[...] = jnp.zeros(acc_scr.shape, F32)

        def step(ok):
            kpe_v = kpe_ref[...]
            ones = jnp.ones((MLA_T, HEAD), BF)
            state = [(m_scr[hh], acc_scr[hh]) for hh in range(MLA_HB)]
            new = []
            for hh in range(MLA_HB):
                m_prev, acc = state[hh]
                _, _, s = _mla_scores(qn_ref, qp_ref, kn_ref, kpe_v, hh, ok)
                m_new = jnp.maximum(m_prev, jnp.max(s, axis=1, keepdims=True))
                p = jnp.exp2((s - m_new) * MLA_EXP2_SCALE).astype(BF)
                v1 = jnp.concatenate([_mla_head(v_ref, hh), ones], axis=1)
                alpha = jnp.exp2((m_prev - m_new) * MLA_EXP2_SCALE)
                new.append((m_new, alpha * acc + lax.dot_general(p, v1, NN, preferred_element_type=F32)))
            for hh in range(MLA_HB):
                m_scr[hh], acc_scr[hh] = new[hh]

        _mla_step(i, j, step)

        @pl.when(j == i)
        def _():
            for hh in range(MLA_HB):
                l = acc_scr[hh, :, HEAD:]
                o_ref[:, hh * HEAD:(hh + 1) * HEAD] = (acc_scr[hh, :, :HEAD] / l).astype(BF)
                lse_ref[hh] = m_scr[hh] * MLA_SCALE + jnp.log(l)

    qspec, kspec, kpespec, lsespec = _mla_specs(MLA_HB)
    grid_spec = pltpu.PrefetchScalarGridSpec(
        num_scalar_prefetch=2, grid=(MLA_HEADS // MLA_HB, it.shape[0]),
        in_specs=[qspec, qspec, kspec, kpespec, kspec], out_specs=[qspec, lsespec],
        scratch_shapes=[pltpu.VMEM((MLA_HB, MLA_T, 1), F32), pltpu.VMEM((MLA_HB, MLA_T, 2 * HEAD), F32)])
    return pl.pallas_call(
        body, name="mla_fwd", grid_spec=grid_spec,
        out_shape=[jax.ShapeDtypeStruct((S, MLA_HEADS * HEAD), BF), jax.ShapeDtypeStruct((MLA_HEADS, S, LANE), F32)],
        compiler_params=_params("parallel", "arbitrary"))(it, jt, qn, qp, kn, kpe, v)


def _mla_p_ds(qn_ref, qp_ref, kn_ref, kpe, v_ref, do_ref, o_ref, lse_ref, hh, ok):
    q, k, s = _mla_scores(qn_ref, qp_ref, kn_ref, kpe, hh, ok)
    p = jnp.exp2(s * MLA_EXP2_SCALE - lse_ref[hh][:, 0:1] * LOG2E)
    do = _mla_head(do_ref, hh)
    delta = jnp.sum(do.astype(F32) * _mla_head(o_ref, hh).astype(F32), axis=1, keepdims=True)
    dp = lax.dot_general(do, _mla_head(v_ref, hh), NT, preferred_element_type=F32)
    ds = p * (dp - delta) * MLA_SCALE
    return q, k, p, ds, do


def _mla_bwd(qn, qp, kn, kpe, v, do, o, lse):
    S = qn.shape[0]
    nq = S // MLA_T
    hb = MLA_BWD_HB
    it, jt = _mla_pairs(nq, False)

    def body(it_ref, jt_ref, qn_ref, qp_ref, kn_ref, kpe_ref, v_ref, do_ref, o_ref, lse_ref,
             dqn_ref, dqp_ref, dkn_ref, dv_ref, dkpe_ref, dq_acc, dk_acc, dv_acc, stage_n, stage_p, osem):
        t = pl.program_id(1)
        i, j = it_ref[t], jt_ref[t]

        @pl.when(t == 0)
        def _():
            dq_acc[...] = jnp.zeros(dq_acc.shape, F32)

        @pl.when(i == j)
        def _():
            dk_acc[...] = jnp.zeros(dk_acc.shape, F32)
            dv_acc[...] = jnp.zeros(dv_acc.shape, F32)

        def step(ok):
            kpe_v = kpe_ref[...]
            for hh in range(hb):
                q, k, p, ds, do_h = _mla_p_ds(qn_ref, qp_ref, kn_ref, kpe_v, v_ref, do_ref, o_ref, lse_ref, hh, ok)
                ds = ds.astype(BF)
                dv_acc[hh] += lax.dot_general(p.astype(BF), do_h, TN, preferred_element_type=F32)
                dk_acc[hh] += lax.dot_general(ds, q, TN, preferred_element_type=F32)
                dq_acc[i, hh] += lax.dot_general(ds, k, NN, preferred_element_type=F32)

        _mla_step(i, j, step)

        @pl.when(i == j)
        def _():
            for hh in range(hb):
                stage_n[:, hh * HEAD:(hh + 1) * HEAD] = dq_acc[i, hh, :, :HEAD].astype(BF)
                stage_p[:, hh * HEAD:(hh + 1) * HEAD] = dq_acc[i, hh, :, HEAD:]
            rows = pl.ds(pl.multiple_of(i * MLA_T, MLA_T), MLA_T)
            cols = pl.ds(pl.multiple_of(pl.program_id(0) * hb * HEAD, LANE), hb * HEAD)
            out_n = pltpu.make_async_copy(stage_n, dqn_ref.at[rows, cols], osem.at[0])
            out_p = pltpu.make_async_copy(stage_p, dqp_ref.at[rows, cols], osem.at[1])
            out_n.start()
            out_p.start()
            out_n.wait()
            out_p.wait()

        @pl.when(i == nq - 1)
        def _():
            dkpe = dk_acc[0, :, HEAD:]
            for hh in range(hb):
                dkn_ref[:, hh * HEAD:(hh + 1) * HEAD] = dk_acc[hh, :, :HEAD].astype(BF)
                dv_ref[:, hh * HEAD:(hh + 1) * HEAD] = dv_acc[hh].astype(BF)
                if hh:
                    dkpe = dkpe + dk_acc[hh, :, HEAD:]
            dkpe_ref[...] = dkpe

    qspec, kspec, kpespec, lsespec = _mla_specs(hb)
    dkpespec = pl.BlockSpec((None, MLA_T, HEAD), lambda g, t, it, jt: (g, jt[t], 0))
    grid_spec = pltpu.PrefetchScalarGridSpec(
        num_scalar_prefetch=2, grid=(MLA_HEADS // hb, it.shape[0]),
        in_specs=[qspec, qspec, kspec, kpespec, kspec, qspec, qspec, lsespec],
        out_specs=[ANY, ANY, kspec, kspec, dkpespec],
        scratch_shapes=[pltpu.VMEM((nq, hb, MLA_T, 2 * HEAD), F32), pltpu.VMEM((hb, MLA_T, 2 * HEAD), F32),
                        pltpu.VMEM((hb, MLA_T, HEAD), F32), pltpu.VMEM((MLA_T, hb * HEAD), BF),
                        pltpu.VMEM((MLA_T, hb * HEAD), F32), pltpu.SemaphoreType.DMA((2,))])
    wide = jax.ShapeDtypeStruct((S, MLA_HEADS * HEAD), BF)
    return pl.pallas_call(
        body, name="mla_bwd", grid_spec=grid_spec,
        out_shape=[wide, jax.ShapeDtypeStruct((S, MLA_HEADS * HEAD), F32), wide, wide,
                   jax.ShapeDtypeStruct((MLA_HEADS // hb, S, HEAD), F32)],
        compiler_params=_params("parallel", "arbitrary"))(it, jt, qn, qp, kn, kpe, v, do, o, lse)


DIL_W = 3 * DIL_HPG * HEAD
DIL_O = DIL_HPG * HEAD
DIL_STEP_BLOCKS = 4


def _dil_slopes(g):
    return [2.0 ** (-ALIBI_MAX_BIAS * (g * DIL_HPG + hh + 1) / DIL_HEADS) for hh in range(DIL_HPG)]


def _dil_bias(dil):
    p = lax.broadcasted_iota(jnp.int32, (DIL_BLOCK, DIL_BLOCK), 0)
    kk = lax.broadcasted_iota(jnp.int32, (DIL_BLOCK, DIL_BLOCK), 1)
    jc = p - kk
    dist_c = (dil * jc).astype(F32)
    dist_p = (dil * (jc + DIL_BLOCK)).astype(F32)
    return jc >= 0, jc <= 0, dist_c, dist_p


def _dil_bias2(dil):
    p = lax.broadcasted_iota(jnp.int32, (DIL_BLOCK, 2 * DIL_BLOCK), 0)
    kk = lax.broadcasted_iota(jnp.int32, (DIL_BLOCK, 2 * DIL_BLOCK), 1)
    j = p + DIL_BLOCK - kk
    return (j >= 0) & (j <= DIL_BLOCK), kk < DIL_BLOCK, (dil * j).astype(F32)


def _dil_head(blk, hh):
    q = blk[:, hh * HEAD:(hh + 1) * HEAD]
    k = blk[:, DIL_O + hh * HEAD:DIL_O + (hh + 1) * HEAD]
    v = blk[:, 2 * DIL_O + hh * HEAD:2 * DIL_O + (hh + 1) * HEAD]
    return q, k, v


def _dil_s(q, k, slope, dist, ok):
    s = lax.dot_general(q, k, NT, preferred_element_type=F32) * DIL_SCALE - slope * dist
    return jnp.where(ok, s, -jnp.inf)


def _dil_view(a, dil):
    S, W = a.shape
    return a.reshape(S // dil, dil * W)


def _dil_fwd(qkv, g):
    _, dil = DIL_PATTERNS[g]
    S = qkv.shape[0]
    L = S // dil
    nb = L // DIL_BLOCK
    slopes = _dil_slopes(g)

    bb = min(DIL_STEP_BLOCKS, nb)
    rows = bb * DIL_BLOCK

    def body(cur_ref, prev_ref, o_ref, lse_ref):
        n = pl.program_id(1)
        ok, in_prev, dist = _dil_bias2(dil)
        for b in range(bb):
            if b == 0:
                both = jnp.concatenate([prev_ref[...], cur_ref[0:DIL_BLOCK, :]], axis=0)
                ok_b = ok & (~in_prev | (n > 0))
            else:
                both = cur_ref[(b - 1) * DIL_BLOCK:(b + 1) * DIL_BLOCK, :]
                ok_b = ok
            for hh in range(DIL_HPG):
                q, _, _ = _dil_head(both[DIL_BLOCK:], hh)
                _, k2, v2 = _dil_head(both, hh)
                s = _dil_s(q, k2, slopes[hh], dist, ok_b)
                m = jnp.max(s, axis=1, keepdims=True)
                p = jnp.exp(s - m)
                l = jnp.sum(p, axis=1, keepdims=True)
                o = lax.dot_general(p.astype(BF), v2, NN, preferred_element_type=F32) / l
                rs, sl = slice(b * DIL_BLOCK, (b + 1) * DIL_BLOCK), slice(hh * HEAD, (hh + 1) * HEAD)
                o_ref[rs, sl] = o
                lse_ref[rs, sl] = jnp.broadcast_to(m + jnp.log(l), (DIL_BLOCK, HEAD))

    ospec = pl.BlockSpec((rows, DIL_O), lambda r, n: (n, r))
    o, lse = pl.pallas_call(
        body, name=f"dil_fwd{g}", grid=(dil, nb // bb),
        in_specs=[pl.BlockSpec((rows, DIL_W), lambda r, n: (n, r)),
                  pl.BlockSpec((DIL_BLOCK, DIL_W), lambda r, n: (jnp.maximum(n * bb - 1, 0), r))],
        out_specs=[ospec, ospec],
        out_shape=[jax.ShapeDtypeStruct((L, dil * DIL_O), F32), jax.ShapeDtypeStruct((L, dil * DIL_O), F32)],
        compiler_params=_params("parallel", "parallel"))(_dil_view(qkv, dil), _dil_view(qkv, dil))
    return o.reshape(S, DIL_O), lse.reshape(S, DIL_O)


def _dil_combine(os_, lses):
    S = os_[0].shape[0]
    ts = _tile(S, 512)

    def body(o0, o1, o2, l0, l1, l2, out_ref, lse_ref):
        a, b, c = l0[...], l1[...], l2[...]
        m = jnp.maximum(jnp.maximum(a, b), c)
        ea, eb, ec = jnp.exp(a - m), jnp.exp(b - m), jnp.exp(c - m)
        tot = ea + eb + ec
        out_ref[...] = ((ea * o0[...] + eb * o1[...] + ec * o2[...]) / tot).astype(BF)
        lse_ref[...] = m + jnp.log(tot)

    return pl.pallas_call(
        body, name="dil_combine", grid=(S // ts,), in_specs=[_row(ts, DIL_O)] * 6,
        out_specs=[_row(ts, DIL_O), _row(ts, DIL_O)],
        out_shape=[jax.ShapeDtypeStruct((S, DIL_O), BF), jax.ShapeDtypeStruct((S, DIL_O), F32)],
        compiler_params=_params("parallel"))(*os_, *lses)


def _dil_delta(do, out):
    S = do.shape[0]
    ts = _tile(S, 512)

    def body(do_ref, o_ref, d_ref):
        for hh in range(DIL_HPG):
            sl = slice(hh * HEAD, (hh + 1) * HEAD)
            d = jnp.sum(do_ref[:, sl].astype(F32) * o_ref[:, sl].astype(F32), axis=1, keepdims=True)
            d_ref[:, sl] = jnp.broadcast_to(d, (ts, HEAD))

    return pl.pallas_call(
        body, name="dil_delta", grid=(S // ts,), in_specs=[_row(ts, DIL_O)] * 2, out_specs=_row(ts, DIL_O),
        out_shape=jax.ShapeDtypeStruct((S, DIL_O), F32), compiler_params=_params("parallel"))(do, out)


def _dil_bwd(qkv, do, lse, delta, g):
    _, dil = DIL_PATTERNS[g]
    S = qkv.shape[0]
    L = S // dil
    nb = L // DIL_BLOCK
    slopes = _dil_slopes(g)

    def pair(q, k, v, do_h, lse_h, delta_h, slope, dist, ok):
        s = _dil_s(q, k, slope, dist, ok)
        p = jnp.exp(s - lse_h)
        dp = lax.dot_general(do_h, v, NT, preferred_element_type=F32)
        ds = (p * (dp - delta_h) * DIL_SCALE).astype(BF)
        return p.astype(BF), ds

    bb = min(DIL_STEP_BLOCKS, nb)
    rows = bb * DIL_BLOCK
    steps = nb // bb

    def body(cur_ref, prev_ref, next_ref, doc_ref, don_ref, lsec_ref, lsen_ref, dlc_ref, dln_ref, out_ref):
        n = pl.program_id(1)
        ok2, in_prev, dist2 = _dil_bias2(dil)
        _, ok_p0, _, dist_p = _dil_bias(dil)
        for b in range(bb):
            rs = slice(b * DIL_BLOCK, (b + 1) * DIL_BLOCK)
            rn = slice((b + 1) * DIL_BLOCK, (b + 2) * DIL_BLOCK)
            two = slice(b * DIL_BLOCK, (b + 2) * DIL_BLOCK)
            first, last = b == 0, b == bb - 1
            if first:
                keys = jnp.concatenate([prev_ref[...], cur_ref[rs, :]], axis=0)
                ok_ab = ok2 & (~in_prev | (n > 0))
            else:
                keys = cur_ref[(b - 1) * DIL_BLOCK:(b + 1) * DIL_BLOCK, :]
                ok_ab = ok2
            qrows = jnp.concatenate([cur_ref[rs, :], next_ref[...]], axis=0) if last else cur_ref[two, :]
            ok_n = ok_p0 & (n < steps - 1) if last else ok_p0
            for hh in range(DIL_HPG):
                sl = slice(hh * HEAD, (hh + 1) * HEAD)
                q2, _, _ = _dil_head(qrows, hh)
                _, k2, v2 = _dil_head(keys, hh)
                q, qn, kc, vc = q2[:DIL_BLOCK], q2[DIL_BLOCK:], k2[DIL_BLOCK:], v2[DIL_BLOCK:]
                do2 = jnp.concatenate([doc_ref[rs, sl], don_ref[:, sl]], axis=0) if last else doc_ref[two, sl]
                do_c, do_n = do2[:DIL_BLOCK], do2[DIL_BLOCK:]
                lse_c = lsec_ref[rs, sl][:, 0:1]
                lse_n = (lsen_ref[:, sl] if last else lsec_ref[rn, sl])[:, 0:1]
                dl_c = dlc_ref[rs, sl][:, 0:1]
                dl_n = (dln_ref[:, sl] if last else dlc_ref[rn, sl])[:, 0:1]
                p_ab, ds_ab = pair(q, k2, v2, do_c, lse_c, dl_c, slopes[hh], dist2, ok_ab)
                p_n, ds_n = pair(qn, kc, vc, do_n, lse_n, dl_n, slopes[hh], dist_p, ok_n)
                dq = lax.dot_general(ds_ab, k2, NN, preferred_element_type=F32)
                dk = lax.dot_general(jnp.concatenate([ds_ab[:, DIL_BLOCK:], ds_n], axis=0), q2, TN, preferred_element_type=F32)
                dv = lax.dot_general(jnp.concatenate([p_ab[:, DIL_BLOCK:], p_n], axis=0), do2, TN, preferred_element_type=F32)
                out_ref[rs, sl] = dq.astype(BF)
                out_ref[rs, DIL_O + hh * HEAD:DIL_O + (hh + 1) * HEAD] = dk.astype(BF)
                out_ref[rs, 2 * DIL_O + hh * HEAD:2 * DIL_O + (hh + 1) * HEAD] = dv.astype(BF)

    cur_w = pl.BlockSpec((rows, DIL_W), lambda r, n: (n, r))
    prev_w = pl.BlockSpec((DIL_BLOCK, DIL_W), lambda r, n: (jnp.maximum(n * bb - 1, 0), r))
    next_w = pl.BlockSpec((DIL_BLOCK, DIL_W), lambda r, n: (jnp.minimum((n + 1) * bb, nb - 1), r))
    cur_o = pl.BlockSpec((rows, DIL_O), lambda r, n: (n, r))
    next_o = pl.BlockSpec((DIL_BLOCK, DIL_O), lambda r, n: (jnp.minimum((n + 1) * bb, nb - 1), r))
    qv, dov, lsev, dlv = _dil_view(qkv, dil), _dil_view(do, dil), _dil_view(lse, dil), _dil_view(delta, dil)
    out = pl.pallas_call(
        body, name=f"dil_bwd{g}", grid=(dil, steps),
        in_specs=[cur_w, prev_w, next_w, cur_o, next_o, cur_o, next_o, cur_o, next_o],
        out_specs=cur_w, out_shape=jax.ShapeDtypeStruct((L, dil * DIL_W), BF),
        compiler_params=_params("parallel", "parallel"))(qv, qv, qv, dov, dov, lsev, lsev, dlv, dlv)
    return out.reshape(S, DIL_W)


def _adamw(name, w, g, m, v):
    R, C = w.shape
    tr, tc = _adamw_block(R, C)

    def body(w_ref, g_ref, m_ref, v_ref, go_ref, d_ref, nm_ref, nv_ref):
        gv = g_ref[...]
        go_ref[...] = gv
        nm = ADAM_B1 * m_ref[...] + (1.0 - ADAM_B1) * gv
        nv = ADAM_B2 * v_ref[...] + (1.0 - ADAM_B2) * (gv * gv)
        m_hat = nm / (1.0 - ADAM_B1 ** ADAM_STEP)
        v_hat = nv / (1.0 - ADAM_B2 ** ADAM_STEP)
        d_ref[...] = -ADAM_LR * (m_hat / (jnp.sqrt(v_hat) + ADAM_EPS) + ADAM_WD * w_ref[...])
        nm_ref[...] = nm
        nv_ref[...] = nv

    spec = pl.BlockSpec((tr, tc), lambda i, j: (i, j))
    shp = jax.ShapeDtypeStruct((R, C), F32)
    return pl.pallas_call(
        body, name=name, grid=(R // tr, C // tc), in_specs=[spec] * 4, out_specs=[spec] * 4, out_shape=[shp] * 4,
        compiler_params=_params("parallel", "parallel"))(w, g, m, v)


ADAMW_BLOCK_ELEMS = 640 * 1024


def _adamw_block(R, C):
    if R * C <= ADAMW_BLOCK_ELEMS:
        return R, C
    tr = _tile_rows(R, max(8, ADAMW_BLOCK_ELEMS // C))
    tc = _tile(C, max(LANE, ADAMW_BLOCK_ELEMS // R))
    if tr * C >= R * tc or R * tc > ADAMW_BLOCK_ELEMS:
        return tr, C
    return R, tc


def _tile_rows(n, pref, mult=8):
    t = (pref // mult) * mult
    while t >= mult:
        if n % t == 0:
            return t
        t -= mult
    return n


ANY = pl.BlockSpec(memory_space=pl.ANY)


def _place():
    x, y, c = lax.axis_index("x"), lax.axis_index("y"), lax.axis_index("c")
    chips = [(1 - x, y), (x, 1 - y), (1 - x, 1 - y)]
    chip_idx = [2 * cx + cy for cx, cy in chips]
    return x, y, c, 2 * x + y, chips, chip_idx


def _rcopy(src, dst, ssem, rsem, dev):
    return pltpu.make_async_remote_copy(src_ref=src, dst_ref=dst, send_sem=ssem, recv_sem=rsem,
                                        device_id=dev, device_id_type=MESH)


HBM = pl.BlockSpec(memory_space=pltpu.HBM)
SEM = pl.BlockSpec(memory_space=pltpu.SEMAPHORE)
EFFECT = pltpu.SideEffectType.DATAFLOW_SIDE_EFFECTING


def _split_copies(kind, srcs, lands, ssem, rsem):
    _, _, c, me, chips, chip_idx = _place()
    cps = []
    for i in range(len(srcs)):
        for k in range(3):
            if kind == "gather":
                rows = srcs[i].shape[0]
                if rows == lands[i].shape[1]:
                    src, dst = srcs[i], lands[i].at[me]
                else:
                    src, dst = srcs[i], lands[i].at[me, pl.ds(pl.multiple_of(c * rows, 16), rows)]
            else:
                src, dst = srcs[i].at[chip_idx[k]], lands[i].at[k]
            cps.append(_rcopy(src, dst, ssem.at[3 * i + k], rsem.at[3 * i + k], (*chips[k], c)))
    return cps


def _exchange_start(name, kind, srcs, lands, groups):
    n, ng = len(srcs), len(groups)

    def body(*refs):
        src_refs, land_refs = refs[:n], refs[n:2 * n]
        sems = refs[2 * n:2 * n + 2 * ng]
        token = refs[-1]
        for gi, grp in enumerate(groups):
            cps = _split_copies(kind, [src_refs[i] for i in grp], [land_refs[i] for i in grp], sems[2 * gi], sems[2 * gi + 1])
            for cp in cps:
                cp.start()
        token[...] = jnp.zeros_like(token)

    arrays = list(srcs) + list(lands)
    out_shape = []
    for grp in groups:
        out_shape += [pltpu.SemaphoreType.DMA((3 * len(grp),)), pltpu.SemaphoreType.DMA((3 * len(grp),))]
    out_shape += [pltpu.HBM(a.shape, a.dtype) for a in arrays] + [jax.ShapeDtypeStruct((8, LANE), F32)]
    outs = pl.pallas_call(
        body, name=name, out_shape=out_shape, in_specs=[HBM] * (2 * n),
        out_specs=[SEM] * (2 * ng) + [HBM] * (2 * n) + [pl.BlockSpec(memory_space=pltpu.VMEM)],
        input_output_aliases={i: 2 * ng + i for i in range(2 * n)},
        compiler_params=pltpu.CompilerParams(has_side_effects=EFFECT),
    )(*[pltpu.with_memory_space_constraint(a, pltpu.HBM) for a in arrays])
    sems = [(outs[2 * gi], outs[2 * gi + 1]) for gi in range(ng)]
    thru = outs[2 * ng:2 * ng + 2 * n]
    return sems, thru[:n], thru[n:], outs[-1]


def _exchange_wait(name, kind, srcs, lands, sems, after):
    n = len(srcs)

    def body(*refs):
        cps = _split_copies(kind, refs[:n], refs[n:2 * n], refs[2 * n], refs[2 * n + 1])
        for cp in cps:
            cp.wait_send()
            cp.wait_recv()

    arrays = list(srcs) + list(lands)
    outs = pl.pallas_call(
        body, name=name, out_shape=[pltpu.HBM(a.shape, a.dtype) for a in arrays],
        in_specs=[HBM] * (2 * n) + [SEM, SEM, ANY], out_specs=[HBM] * (2 * n),
        input_output_aliases={i: i for i in range(2 * n)},
        compiler_params=pltpu.CompilerParams(has_side_effects=EFFECT),
    )(*arrays, sems[0], sems[1], after)
    return outs[:n], outs[n:]


EXCHANGE_CHUNK_BYTES = 3 * 1024 * 1024


def _half_geometry(R, C, axis):
    Rp, Cp = (R // 2, C) if axis == 0 else (R, C // 2)
    tr = _tile_rows(Rp, max(16, EXCHANGE_CHUNK_BYTES // (2 * Cp)), 16)
    return Rp, Cp, tr, Rp // tr


def _pair_sum(name, g, axis):
    G, R, C = g.shape
    Rp, Cp, tr, nb = _half_geometry(R, C, axis)
    steps = G * nb

    def half_block(t, h):
        s, b = t // nb, t % nb
        return (s, h * nb + b, 0) if axis == 0 else (s, b, h)

    def body(c_ref, keep_ref, give_ref, out_ref, land, ssem, rsem, credit):
        x, y, c = lax.axis_index("x"), lax.axis_index("y"), lax.axis_index("c")
        sib = (x, y, 1 - c)
        t = pl.program_id(0)

        def copy(chunk):
            return _rcopy(give_ref.at[0], land.at[chunk % 2], ssem.at[chunk % 2], rsem.at[chunk % 2], sib)

        @pl.when((t >= 2) & (t < steps))
        def _():
            pl.semaphore_wait(credit, 1)

        @pl.when(t < steps)
        def _():
            copy(t).start()

        @pl.when(t >= 1)
        def _():
            copy(t - 1).wait_recv()
            out_ref[...] = (keep_ref[...].astype(F32) + land[(t - 1) % 2].astype(F32)).astype(BF)

            @pl.when(t + 1 < steps)
            def _():
                pl.semaphore_signal(credit, 1, device_id=sib, device_id_type=MESH)

        @pl.when(t < steps)
        def _():
            copy(t).wait_send()

    blk = (None, tr, Cp)
    grid_spec = pltpu.PrefetchScalarGridSpec(
        num_scalar_prefetch=1, grid=(steps + 1,),
        in_specs=[pl.BlockSpec(blk, lambda t, c_ref: half_block(jnp.maximum(t - 1, 0), c_ref[0])),
                  pl.BlockSpec((1, tr, Cp), lambda t, c_ref: half_block(jnp.minimum(t, steps - 1), 1 - c_ref[0]))],
        out_specs=pl.BlockSpec(blk, lambda t, c_ref: (jnp.maximum(t - 1, 0) // nb, jnp.maximum(t - 1, 0) % nb, 0)),
        scratch_shapes=[pltpu.VMEM((2, tr, Cp), BF), pltpu.SemaphoreType.DMA((2,)), pltpu.SemaphoreType.DMA((2,)),
                        pltpu.SemaphoreType.REGULAR])
    c_arr = lax.axis_index("c").astype(jnp.int32).reshape(1)
    return pl.pallas_call(
        body, name=name, grid_spec=grid_spec, out_shape=jax.ShapeDtypeStruct((G, Rp, Cp), BF),
        compiler_params=_params("arbitrary"))(c_arr, g, g)


def _chip_total_join(name, h, landed, axis):
    G, Rp, Cp = h.shape
    R, C = (2 * Rp, Cp) if axis == 0 else (Rp, 2 * Cp)
    tr = _tile_rows(Rp, max(16, EXCHANGE_CHUNK_BYTES // (4 * Cp)), 16)
    nb = Rp // tr

    def body(me_ref, own_ref, l0_ref, l1_ref, l2_ref, full, stage, ssem, rsem, lsem):
        x, y, c = lax.axis_index("x"), lax.axis_index("y"), lax.axis_index("c")
        sib = (x, y, 1 - c)
        b = pl.program_id(0)

        def place(half, r0, rows):
            if axis == 0:
                return full.at[pl.ds(pl.multiple_of(half * Rp + r0, 8), rows), :]
            return full.at[pl.ds(pl.multiple_of(r0, 8), rows), pl.ds(pl.multiple_of(half * Cp, LANE), Cp)]

        def copies(step):
            s = step % 2
            mine = place(c, step * tr, tr)
            return pltpu.make_async_copy(stage.at[s], mine, lsem.at[s]), _rcopy(stage.at[s], mine, ssem.at[s], rsem, sib)

        @pl.when(b >= 2)
        def _():
            loc, rem = copies(b - 2)
            loc.wait()
            rem.wait_send()

        acc = own_ref[...].astype(F32)
        for r in (l0_ref, l1_ref, l2_ref):
            acc = acc + r[...].astype(F32)
        stage[b % 2] = acc
        loc, rem = copies(b)
        loc.start()
        rem.start()

        @pl.when(b == nb - 1)
        def _():
            for step in range(max(0, nb - 2), nb):
                loc, rem = copies(step)
                loc.wait()
                rem.wait_send()
            theirs = place(1 - c, 0, Rp)
            _rcopy(theirs, theirs, ssem.at[0], rsem, sib).wait_recv()

    blk = (None, tr, Cp)
    grid_spec = pltpu.PrefetchScalarGridSpec(
        num_scalar_prefetch=1, grid=(nb,),
        in_specs=[pl.BlockSpec(blk, lambda b, me_ref: (me_ref[0], b, 0))]
        + [pl.BlockSpec(blk, functools.partial(lambda b, me_ref, k: (k, b, 0), k=k)) for k in range(3)],
        out_specs=ANY,
        scratch_shapes=[pltpu.VMEM((2, tr, Cp), F32), pltpu.SemaphoreType.DMA((2,)), pltpu.SemaphoreType.DMA,
                        pltpu.SemaphoreType.DMA((2,))])
    me = (2 * lax.axis_index("x") + lax.axis_index("y")).astype(jnp.int32).reshape(1)
    return pl.pallas_call(
        body, name=name, grid_spec=grid_spec, out_shape=jax.ShapeDtypeStruct((R, C), F32),
        compiler_params=_params("arbitrary"))(me, h, landed, landed, landed)


def _pair_share(name, land):
    G, R, C = land.shape
    Rh = R // 2
    tr = _tile_rows(Rh, max(16, EXCHANGE_CHUNK_BYTES // (2 * C)), 16)
    chunks = [(k, b) for k in range(3) for b in range(Rh // tr)]

    def body(src, dst, buf, lsem, ssem, rsem):
        x, y, c, _, _, chip_idx = _place()
        sib = (x, y, 1 - c)

        def region(ref, k, half, r0, rows):
            return ref.at[chip_idx[k], pl.ds(pl.multiple_of(half * Rh + r0, 16), rows)]

        def load(t):
            k, b = chunks[t]
            return pltpu.make_async_copy(region(src, k, c, b * tr, tr), buf.at[t % 2], lsem.at[t % 2])

        def send(t):
            k, b = chunks[t]
            return _rcopy(buf.at[t % 2], region(dst, k, c, b * tr, tr), ssem.at[t % 2], rsem.at[k], sib)

        load(0).start()
        for t in range(len(chunks)):
            load(t).wait()
            if t + 1 < len(chunks):
                if t >= 1:
                    send(t - 1).wait_send()
                load(t + 1).start()
            send(t).start()
        for t in range(max(0, len(chunks) - 2), len(chunks)):
            send(t).wait_send()
        for k in range(3):
            theirs = region(dst, k, 1 - c, 0, Rh)
            _rcopy(theirs, theirs, ssem.at[0], rsem.at[k], sib).wait_recv()

    return pl.pallas_call(
        body, name=name, in_specs=[ANY], out_specs=ANY, out_shape=jax.ShapeDtypeStruct(land.shape, land.dtype),
        input_output_aliases={0: 0},
        scratch_shapes=[pltpu.VMEM((2, tr, C), land.dtype), pltpu.SemaphoreType.DMA((2,)), pltpu.SemaphoreType.DMA((2,)),
                        pltpu.SemaphoreType.DMA((3,))],
    )(land)


def _allreduce_small(v):
    R, K = v.shape
    ndev = 8

    def body(v_ref, o_ref, land, ssem, rsem):
        x, y, c = lax.axis_index("x"), lax.axis_index("y"), lax.axis_index("c")
        me = 4 * x + 2 * y + c
        land[me] = v_ref[...]
        cps = []
        for r in range(1, ndev):
            fx, fy, fc = (r >> 2) & 1, (r >> 1) & 1, r & 1
            peer = (x ^ fx, y ^ fy, c ^ fc)
            cp = _rcopy(v_ref, land.at[me], ssem.at[r - 1], rsem.at[r - 1], peer)
            cp.start()
            cps.append((cp, 4 * peer[0] + 2 * peer[1] + peer[2], r))
        for cp, src, r in cps:
            cp.wait_send()
            _rcopy(v_ref, land.at[src], ssem.at[r - 1], rsem.at[r - 1], (x, y, c)).wait_recv()
        acc = land[0]
        for d in range(1, ndev):
            acc = acc + land[d]
        o_ref[...] = acc

    vm = pl.BlockSpec(memory_space=pltpu.VMEM)
    return pl.pallas_call(
        body, name="allreduce_small", in_specs=[vm], out_specs=vm, out_shape=jax.ShapeDtypeStruct((R, K), F32),
        scratch_shapes=[pltpu.VMEM((ndev, R, K), F32), pltpu.SemaphoreType.DMA((ndev - 1,)), pltpu.SemaphoreType.DMA((ndev - 1,))],
    )(v)


IN_SPLITS = (Q_RANK, KV_RANK, QK_ROPE, DIL_HEADS * HEAD, DIL_HEADS * HEAD, DIL_HEADS * HEAD, D_MODEL, D_MODEL)
IN_OFF = tuple(int(v) for v in np.cumsum((0,) + IN_SPLITS))


def _unshard_cols(g):
    G, K, Ns = g.shape
    return g.transpose(1, 0, 2).reshape(K, G * Ns)


def _shard_cols(w):
    K, N = w.shape
    return w.reshape(K, N_CHIPS, N // N_CHIPS).transpose(1, 0, 2)


def _rope_pad(w):
    half = QK_ROPE // 2
    z = jnp.zeros(w.shape[:-1] + (half,), w.dtype)
    return jnp.concatenate([w[..., :half], z, w[..., half:], z], axis=-1)


def _rope_unpad(w):
    half = QK_ROPE // 2
    return jnp.concatenate([w[..., :half], w[..., 2 * half:3 * half]], axis=-1)


def _split_w_in(w_in_g):
    G, K, Ns = w_in_g.shape

    def cols(lo, hi):
        pieces = [w_in_g[k][:, max(lo, k * Ns) - k * Ns:min(hi, (k + 1) * Ns) - k * Ns]
                  for k in range(G) if max(lo, k * Ns) < min(hi, (k + 1) * Ns)]
        return pieces[0] if len(pieces) == 1 else jnp.concatenate(pieces, axis=1)

    p = [(IN_OFF[i], IN_OFF[i + 1]) for i in range(8)]
    w_lat = jnp.concatenate([cols(*p[0]), cols(*p[1]), _rope_pad(cols(*p[2])),
                             jnp.zeros((K, LAT_W - _KPE.stop), w_in_g.dtype)], axis=1)
    w_dil = [jnp.concatenate([cols(p[3 + t][0] + g * DIL_O, p[3 + t][0] + (g + 1) * DIL_O) for t in range(3)], axis=1)
             for g in range(DIL_GROUPS)]
    w_gate = cols(p[6][0], p[7][1])
    return w_lat, w_dil, w_gate


def _merge_dw_in(dw_lat, dw_dil, dw_gate):
    parts = [dw_lat[:, _CQ], dw_lat[:, _CKV], _rope_unpad(dw_lat[:, _KPE])]
    for t in range(3):
        parts += [dw_dil[g][:, t * DIL_O:(t + 1) * DIL_O] for g in range(DIL_GROUPS)]
    parts.append(dw_gate)
    width = sum(p.shape[1] for p in parts) // N_CHIPS
    shards = []
    for k in range(N_CHIPS):
        pieces, at = [], 0
        for p in parts:
            lo, hi = max(k * width, at), min((k + 1) * width, at + p.shape[1])
            if lo < hi:
                pieces.append(p[:, lo - at:hi - at])
            at += p.shape[1]
        shards.append(jnp.concatenate(pieces, axis=1))
    return jnp.stack(shards)


def _split_w_uq(w_uq_g):
    w = _unshard_cols(w_uq_g)
    K = w.shape[0]
    w = w.reshape(K, MLA_HEADS, QK_NOPE + QK_ROPE)
    return w[:, :, :QK_NOPE].reshape(K, MLA_HEADS * HEAD), _rope_pad(w[:, :, QK_NOPE:]).reshape(K, MLA_HEADS * HEAD)


def _merge_dw_uq(dw_n, dw_p):
    K = dw_n.shape[0]
    w = jnp.concatenate([dw_n.reshape(K, MLA_HEADS, HEAD), _rope_unpad(dw_p.reshape(K, MLA_HEADS, HEAD))], axis=-1)
    return _shard_cols(w.reshape(K, MLA_HEADS * (QK_NOPE + QK_ROPE)))


def _split_w_ukv(w_ukv_g):
    w = _unshard_cols(w_ukv_g)
    K = w.shape[0]
    w = w.reshape(K, MLA_HEADS, 2 * HEAD)
    return w[:, :, :HEAD].reshape(K, MLA_HEADS * HEAD), w[:, :, HEAD:].reshape(K, MLA_HEADS * HEAD)


def _merge_dw_ukv(dw_k, dw_v):
    K = dw_k.shape[0]
    w = jnp.concatenate([dw_k.reshape(K, MLA_HEADS, HEAD), dw_v.reshape(K, MLA_HEADS, HEAD)], axis=-1)
    return _shard_cols(w.reshape(K, MLA_HEADS * 2 * HEAD))


GATHER_GROUPS = (("w_in",), ("w_uq", "w_ukv", "w_o_mla", "w_o_dil", "w_out"), ("w_up", "w_down", "conv_w"))
SHARED_FETCH = ("w_in",)
REDUCE_GROUPS = (("w_down", "w_up"), ("w_out", "w_o_mla", "w_o_dil"), ("w_uq", "w_ukv", "w_in"))


def _local_step(x, tgt, W, fetch, emit):
    S, D = x.shape
    cos, sin_s = _rope_tables(S)
    w_lat, w_dil, w_gate = _split_w_in(fetch(0, x)["w_in"])

    h = _rmsnorm_fwd("attn_norm", x, W["attn_norm_g"])
    lat = _mm_nn("proj_lat", h, w_lat)
    qkv = [_mm_nn(f"proj_dil{g}", h, w_dil[g], o_dtype=BF) for g in range(DIL_GROUPS)]
    gpre = _mm_nn("proj_gate", h, w_gate, o_dtype=BF)
    WB = fetch(1, gpre)
    w_uqn, w_uqp = _split_w_uq(WB["w_uq"])
    w_k, w_v = _split_w_ukv(WB["w_ukv"])
    w_o_mla, w_o_dil = WB["w_o_mla"], WB["w_o_dil"]
    w_out = WB["w_out"].reshape(D, D)
    qn_, kvn, kpe = _mla_prep(lat, W["q_norm_g"], W["kv_norm_g"], cos, sin_s)
    q_nope = _mm_nn("q_nope", qn_, w_uqn, o_dtype=BF)
    q_pe = _rope("q_rope", _mm_nn("q_pe", qn_, w_uqp), cos, sin_s, False)
    k_nope = _mm_nn("k_nope", kvn, w_k, o_dtype=BF)
    v_mla = _mm_nn("v_mla", kvn, w_v, o_dtype=BF)
    attn_a, lse_a = _mla_fwd(q_nope, q_pe, k_nope, kpe, v_mla)
    dil = [_dil_fwd(qkv[g], g) for g in range(DIL_GROUPS)]
    attn_b, lse_b = _dil_combine([o for o, _ in dil], [l for _, l in dil])
    o_a = _mm_nn("o_mla", attn_a, w_o_mla, o_dtype=BF)
    o_b = _mm_nn("o_dil", attn_b, w_o_dil, o_dtype=BF)
    merge = _merge_fwd(gpre, W["b_gate"], o_a, o_b)
    x1 = _mm_nn("out_proj", merge, w_out, add=x)
    WC = fetch(2, merge)
    w_up = WC["w_up"]
    G4, _, C = w_up.shape
    w_down = WC["w_down"].reshape(G4 // 2, C, D)
    conv_w = WC["conv_w"]
    conv_b = W["conv_b"].reshape(G4, 1, C)
    h2 = _rmsnorm_fwd("ffn_norm", x1, W["ffn_norm_g"])
    u_pre = _up_fwd(h2, w_up)
    act, u = _ffn_act(u_pre, conv_w, conv_b)
    x2 = _down_fwd(act, w_down, x1)
    dx2, d_final_g, loss8 = _final_loss(x2, tgt, W["final_norm_g"])

    d_act = _down_dgrad(dx2, w_down)
    dw_down = _down_wgrad(act, dx2)
    du_pre, d_conv_w, d_conv_b = _ffn_act_conv_bwd(u, d_act, u_pre, conv_w)
    dh2 = _up_dgrad(du_pre, w_up)
    dw_up = _up_wgrad(h2, du_pre)
    zero = emit(0, {"w_down": dw_down.reshape(N_CHIPS, (G4 // 2) * C // N_CHIPS, D), "w_up": dw_up})
    dx1, d_ffn_g = _rmsnorm_bwd("ffn_norm_bwd", dh2, x1, W["ffn_norm_g"] + zero, dx2)
    d_merge = _mm_nt("out_proj_dgrad", dx1, w_out, o_dtype=BF)
    dw_out = _mm_tn("out_proj_wgrad", merge, dx1)
    d_oa, d_ob, d_gpre, d_b_gate = _merge_bwd(d_merge, gpre, W["b_gate"], o_a, o_b)
    d_attn_a = _mm_nt("o_mla_dgrad", d_oa, w_o_mla, o_dtype=BF)
    dw_o_mla = _mm_tn("o_mla_wgrad", attn_a, d_oa, shards=N_CHIPS)
    d_attn_b = _mm_nt("o_dil_dgrad", d_ob, w_o_dil, o_dtype=BF)
    dw_o_dil = _mm_tn("o_dil_wgrad", attn_b, d_ob, shards=N_CHIPS)
    zero = emit(1, {"w_out": dw_out.reshape(N_CHIPS, D // N_CHIPS, D), "w_o_mla": dw_o_mla, "w_o_dil": dw_o_dil})
    q_norm_g = W["q_norm_g"] + zero
    delta_b = _dil_delta(d_attn_b, attn_b)
    d_qkv = [_dil_bwd(qkv[g], d_attn_b, lse_b, delta_b, g) for g in range(DIL_GROUPS)]
    dq_nope, dq_pe_rot, dk_nope, dv_mla, dkpe_rot = _mla_bwd(q_nope, q_pe, k_nope, kpe, v_mla, d_attn_a, attn_a, lse_a)
    dq_pe = _rope("q_rope_bwd", dq_pe_rot, cos, sin_s, True)
    d_qn = _mm_nt_sum("q_dgrad", [dq_nope, dq_pe], [w_uqn, w_uqp])
    d_kvn = _mm_nt_sum("kv_dgrad", [dk_nope, dv_mla], [w_k, w_v])
    dw_uq = _merge_dw_uq(_mm_tn("q_nope_wgrad", qn_, dq_nope), _mm_tn("q_pe_wgrad", qn_, dq_pe))
    dw_ukv = _merge_dw_ukv(_mm_tn("k_nope_wgrad", kvn, dk_nope), _mm_tn("v_wgrad", kvn, dv_mla))
    d_lat, d_q_g, d_kv_g = _mla_prep_bwd(lat, q_norm_g, W["kv_norm_g"], cos, sin_s, d_qn, d_kvn, dkpe_rot)
    dw_in = _merge_dw_in(_mm_tn("proj_lat_wgrad", h, d_lat),
                         [_mm_tn(f"proj_dil{g}_wgrad", h, d_qkv[g]) for g in range(DIL_GROUPS)],
                         _mm_tn("proj_gate_wgrad", h, d_gpre))
    zero = emit(2, {"w_uq": dw_uq, "w_ukv": dw_ukv, "w_in": dw_in})
    dh = _mm_nt_sum("proj_lat_dil_dgrad", [d_lat] + d_qkv, [w_lat + zero.astype(BF)] + w_dil)
    dh = _mm_nt("proj_gate_dgrad", d_gpre, w_gate, add=dh)
    grad_x, d_attn_g = _rmsnorm_bwd("attn_norm_bwd", dh, x, W["attn_norm_g"], dx1)

    small = {"attn_norm_g": d_attn_g, "b_gate": d_b_gate, "q_norm_g": d_q_g, "kv_norm_g": d_kv_g,
             "ffn_norm_g": d_ffn_g, "conv_w": d_conv_w, "conv_b": d_conv_b.reshape(1, G4 * C),
             "final_norm_g": d_final_g}
    return loss8[0, 0], grad_x, small


BIG = ("w_in", "w_uq", "w_ukv", "w_o_mla", "w_o_dil", "w_out", "w_up", "w_down")
SMALL = ("attn_norm_g", "b_gate", "q_norm_g", "kv_norm_g", "ffn_norm_g", "conv_w", "conv_b", "final_norm_g")
WEIGHTS = ("attn_norm_g", "w_in", "b_gate", "q_norm_g", "w_uq", "kv_norm_g", "w_ukv", "w_o_mla", "w_o_dil",
           "w_out", "ffn_norm_g", "w_up", "conv_w", "conv_b", "w_down", "final_norm_g")
SMALL_ROWS = 8
COLUMN_MAJOR = ("w_in", "w_up")
HALF_AXIS = {"w_down": 1}


def _gather_start(shards):
    chip = 2 * lax.axis_index("x") + lax.axis_index("y")
    c = lax.axis_index("c")

    def prepare(names, zero):
        srcs, lands = [], []
        for n in names:
            s = shards[n] + zero
            s = s if n == "conv_w" else s.astype(BF)
            lands.append(lax.dynamic_update_slice(lax.empty((N_CHIPS,) + s.shape, s.dtype), s[None], (chip, 0, 0)))
            if n in SHARED_FETCH:
                s = lax.dynamic_slice_in_dim(s, c * (s.shape[0] // 2), s.shape[0] // 2, 0)
            srcs.append(s)
        return srcs, lands

    n0 = len(GATHER_GROUPS[0])
    srcs0, lands0 = prepare(GATHER_GROUPS[0], 0.0)
    sems0, srcs0, lands0, token = _exchange_start("gather_start0", "gather", srcs0, lands0, [list(range(n0))])
    srcs, lands = prepare([n for grp in GATHER_GROUPS[1:] for n in grp], token[0, 0])
    groups, at = [], 0
    for grp in GATHER_GROUPS[1:]:
        groups.append(list(range(at, at + len(grp))))
        at += len(grp)
    sems, srcs, lands, token1 = _exchange_start("gather_start1", "gather", srcs, lands, groups)

    def fetch(i, after):
        if i == 0:
            _, got = _exchange_wait("gather_wait0", "gather", srcs0, lands0, sems0[0], token1)
        else:
            idx = groups[i - 1]
            _, got = _exchange_wait(f"gather_wait{i}", "gather", [srcs[j] for j in idx], [lands[j] for j in idx],
                                    sems[i - 1], after)
        return {n: _pair_share(f"pair_share_{n}", g) if n in SHARED_FETCH else g for n, g in zip(GATHER_GROUPS[i], got)}

    return fetch, token[0, 0]


def _reduce_start(i, grads):
    names = REDUCE_GROUPS[i]
    hs = [_pair_sum(f"pair_sum_{n}", grads[n], HALF_AXIS.get(n, 0)) for n in names]
    lands = [lax.empty((3,) + h.shape[1:], h.dtype) for h in hs]
    sems, hs, lands, token = _exchange_start(f"reduce_start{i}", "scatter", hs, lands, [list(range(len(names)))])
    return (sems[0], hs, lands), token[0, 0]


def _reduce_finish(i, pending, after):
    sems, hs, lands = pending
    hs, lands = _exchange_wait(f"reduce_wait{i}", "scatter", hs, lands, sems, after)
    out = {}
    for n, h, landed in zip(REDUCE_GROUPS[i], hs, lands):
        out[n] = _chip_total_join(f"chip_total_{n}", h, landed, HALF_AXIS.get(n, 0))
    return out


def _reduce_small(small):
    names = tuple(small)
    flat = [small[n].reshape(-1) for n in names]
    sizes = [f.shape[0] for f in flat]
    total = sum(sizes)
    width = -(-total // (SMALL_ROWS * LANE)) * LANE
    packed = jnp.concatenate(flat + [jnp.zeros((SMALL_ROWS * width - total,), F32)]).reshape(SMALL_ROWS, width)
    red = _allreduce_small(packed).reshape(-1)
    out, off = {}, 0
    for n, s in zip(names, sizes):
        out[n] = red[off:off + s]
        off += s
    return out


def kernel(x, attn_norm_g, w_in, b_gate, q_norm_g, w_uq, kv_norm_g, w_ukv, w_o_mla, w_o_dil, w_out, ffn_norm_g, w_up, conv_w, conv_b, w_down, final_norm_g, loss_target, m_attn_norm_g, m_w_in, m_b_gate, m_q_norm_g, m_w_uq, m_kv_norm_g, m_w_ukv, m_w_o_mla, m_w_o_dil, m_w_out, m_ffn_norm_g, m_w_up, m_conv_w, m_conv_b, m_w_down, m_final_norm_g, v_attn_norm_g, v_w_in, v_b_gate, v_q_norm_g, v_w_uq, v_kv_norm_g, v_w_ukv, v_w_o_mla, v_w_o_dil, v_w_out, v_ffn_norm_g, v_w_up, v_conv_w, v_conv_b, v_w_down, v_final_norm_g):
    given = dict(attn_norm_g=attn_norm_g, w_in=w_in, b_gate=b_gate, q_norm_g=q_norm_g, w_uq=w_uq, kv_norm_g=kv_norm_g,
                 w_ukv=w_ukv, w_o_mla=w_o_mla, w_o_dil=w_o_dil, w_out=w_out, ffn_norm_g=ffn_norm_g, w_up=w_up,
                 conv_w=conv_w, conv_b=conv_b, w_down=w_down, final_norm_g=final_norm_g)
    moments_m = dict(attn_norm_g=m_attn_norm_g, w_in=m_w_in, b_gate=m_b_gate, q_norm_g=m_q_norm_g, w_uq=m_w_uq,
                     kv_norm_g=m_kv_norm_g, w_ukv=m_w_ukv, w_o_mla=m_w_o_mla, w_o_dil=m_w_o_dil, w_out=m_w_out,
                     ffn_norm_g=m_ffn_norm_g, w_up=m_w_up, conv_w=m_conv_w, conv_b=m_conv_b, w_down=m_w_down,
                     final_norm_g=m_final_norm_g)
    moments_v = dict(attn_norm_g=v_attn_norm_g, w_in=v_w_in, b_gate=v_b_gate, q_norm_g=v_q_norm_g, w_uq=v_w_uq,
                     kv_norm_g=v_kv_norm_g, w_ukv=v_w_ukv, w_o_mla=v_w_o_mla, w_o_dil=v_w_o_dil, w_out=v_w_out,
                     ffn_norm_g=v_ffn_norm_g, w_up=v_w_up, conv_w=v_conv_w, conv_b=v_conv_b, w_down=v_w_down,
                     final_norm_g=v_final_norm_g)

    fetch, zero = _gather_start({n: given[n][0] for n in BIG + ("conv_w",)})
    W = {n: given[n] for n in ("b_gate", "q_norm_g", "kv_norm_g", "ffn_norm_g", "conv_b")}
    W["attn_norm_g"] = given["attn_norm_g"] + zero
    W["final_norm_g"] = given["final_norm_g"].reshape(1, -1)

    pending = {}

    def emit(i, grads):
        pending[i], token = _reduce_start(i, grads)
        return token

    loss_part, grad_x, small = _local_step(x[0], loss_target[0], W, fetch, emit)
    small["loss"] = loss_part
    grads, delta, new_m, new_v = {}, {}, {}, {}

    def adamw(n, g):
        shp = given[n].shape
        two_d = (-1, shp[-1]) if len(shp) > 1 else (1, -1)
        view = (lambda a: a.reshape(two_d).T) if n in COLUMN_MAJOR else (lambda a: a.reshape(two_d))
        back = (lambda a: a.T.reshape(shp)) if n in COLUMN_MAJOR else (lambda a: a.reshape(shp))
        go, d, nm, nv = _adamw(f"adamw_{n}", view(given[n]), view(g), view(moments_m[n]), view(moments_v[n]))
        grads[n], delta[n], new_m[n], new_v[n] = back(go), back(d), back(nm), back(nv)

    after = grad_x
    for i in range(len(REDUCE_GROUPS)):
        for n, g in _reduce_finish(i, pending[i], after).items():
            adamw(n, g)
        after = delta[REDUCE_GROUPS[i][-1]]
    g_small = _reduce_small(small)
    loss = g_small["loss"][0]
    chip = 2 * lax.axis_index("x") + lax.axis_index("y")
    for n in SMALL:
        if n == "conv_w":
            full = g_small[n].reshape(N_CHIPS, 3, -1)
            adamw(n, lax.dynamic_index_in_dim(full, chip, 0, keepdims=True))
        else:
            adamw(n, g_small[n])

    return (loss, grad_x[None], *[grads[n] for n in WEIGHTS], *[delta[n] for n in WEIGHTS],
            *[new_m[n] for n in WEIGHTS], *[new_v[n] for n in WEIGHTS])
```

```python
import functools
import math

import numpy as np
import jax
import jax.numpy as jnp
from jax import lax
from jax.experimental import pallas as pl
from jax.experimental.pallas import tpu as pltpu

F32 = jnp.float32
BF = jnp.bfloat16
MESH = pl.DeviceIdType.MESH

D_MODEL = 2048
MLA_HEADS = 8
QK_NOPE = 128
QK_ROPE = 64
Q_RANK = 512
KV_RANK = 256
ROPE_THETA = 10000.0
DIL_PATTERNS = ((128, 1), (512, 4), (2048, 16))
DIL_GROUPS = 3
DIL_HPG = 4
DIL_HEADS = 12
HEAD = 128
DIL_BLOCK = 128
ALIBI_MAX_BIAS = 8.0
NORM_EPS = 1e-6
N_CHIPS = 4
ADAM_LR = 0.001
ADAM_B1 = 0.9
ADAM_B2 = 0.999
ADAM_EPS = 1e-08
ADAM_WD = 0.01
ADAM_STEP = 10

LANE = 128
VMEM_LIMIT = 56 * 1024 * 1024
MLA_SCALE = (QK_NOPE + QK_ROPE) ** -0.5
DIL_SCALE = HEAD ** -0.5


def _params(*sem):
    return pltpu.CompilerParams(dimension_semantics=sem, vmem_limit_bytes=VMEM_LIMIT)


def _tile(n, pref):
    t = (pref // LANE) * LANE
    while t >= LANE:
        if n % t == 0:
            return t
        t -= LANE
    return n


NN = (((1,), (0,)), ((), ()))
NT = (((1,), (1,)), ((), ()))
TN = (((0,), (0,)), ((), ()))


def _mm_call(name, a, b, add, *, grid, a_spec, b_spec, add_spec, o_spec, o_shape, o_dtype, acc_shape, dims, nk):
    nax = len(grid)

    def body(*refs):
        if add is None:
            a_ref, b_ref, o_ref = refs[:3]
            c_ref = None
            scr = refs[3:]
        else:
            a_ref, b_ref, c_ref, o_ref = refs[:4]
            scr = refs[4:]
        prod = lax.dot_general(a_ref[...].astype(BF), b_ref[...].astype(BF), dims, preferred_element_type=F32)
        if nk == 1:
            if c_ref is not None:
                prod = prod + c_ref[...]
            o_ref[...] = prod.astype(o_ref.dtype)
        else:
            acc = scr[0]
            k = pl.program_id(nax - 1)

            @pl.when(k == 0)
            def _():
                if c_ref is not None:
                    acc[...] = prod + c_ref[...]
                else:
                    acc[...] = prod

            @pl.when(k > 0)
            def _():
                acc[...] += prod

            @pl.when(k == nk - 1)
            def _():
                o_ref[...] = acc[...].astype(o_ref.dtype)

    ins = [a, b] + ([] if add is None else [add])
    specs = [a_spec, b_spec] + ([] if add is None else [add_spec])
    sem = ("parallel",) * (nax - 1) + ("arbitrary",)
    return pl.pallas_call(
        body, name=name, grid=grid, in_specs=specs, out_specs=o_spec,
        out_shape=jax.ShapeDtypeStruct(o_shape, o_dtype),
        scratch_shapes=[] if nk == 1 else [pltpu.VMEM(acc_shape, F32)],
        compiler_params=_params(*sem),
    )(*ins)


def _mm_nn(name, a, b, *, add=None, o_dtype=F32):
    M, K = a.shape
    sharded = b.ndim == 3
    Ns = b.shape[-1]
    N = Ns * (b.shape[0] if sharded else 1)
    tm, tn, tk = _tile(M, 1024), _tile(Ns, 1024), _tile(K, 2048)
    per = Ns // tn
    nk = K // tk
    if sharded:
        b_spec = pl.BlockSpec((None, tk, tn), lambda i, j, k: (j // per, k, j % per))
    else:
        b_spec = pl.BlockSpec((tk, tn), lambda i, j, k: (k, j))
    return _mm_call(
        name, a, b, add, grid=(M // tm, N // tn, nk),
        a_spec=pl.BlockSpec((tm, tk), lambda i, j, k: (i, k)), b_spec=b_spec,
        add_spec=pl.BlockSpec((tm, tn), lambda i, j, k: (i, j)),
        o_spec=pl.BlockSpec((tm, tn), lambda i, j, k: (i, j)),
        o_shape=(M, N), o_dtype=o_dtype, acc_shape=(tm, tn), dims=NN, nk=nk)


def _mm_nt(name, a, b, *, add=None, o_dtype=F32):
    M, K = a.shape
    sharded = b.ndim == 3
    N, Ks = b.shape[-2], b.shape[-1]
    tm, tn, tk = _tile(M, 1024), _tile(N, 1024), _tile(Ks, 2048)
    per = Ks // tk
    nk = K // tk
    if sharded:
        b_spec = pl.BlockSpec((None, tn, tk), lambda i, j, k: (k // per, j, k % per))
    else:
        b_spec = pl.BlockSpec((tn, tk), lambda i, j, k: (j, k))
    return _mm_call(
        name, a, b, add, grid=(M // tm, N // tn, nk),
        a_spec=pl.BlockSpec((tm, tk), lambda i, j, k: (i, k)), b_spec=b_spec,
        add_spec=pl.BlockSpec((tm, tn), lambda i, j, k: (i, j)),
        o_spec=pl.BlockSpec((tm, tn), lambda i, j, k: (i, j)),
        o_shape=(M, N), o_dtype=o_dtype, acc_shape=(tm, tn), dims=NT, nk=nk)


def _mm_nt_sum(name, a_s, b_s):
    n = len(a_s)
    M, N = a_s[0].shape[0], b_s[0].shape[0]
    tm, tn = _tile(M, 512), _tile(N, 1024)

    def body(*refs):
        acc = None
        for i in range(n):
            prod = lax.dot_general(refs[i][...].astype(BF), refs[n + i][...].astype(BF), NT, preferred_element_type=F32)
            acc = prod if acc is None else acc + prod
        refs[2 * n][...] = acc

    return pl.pallas_call(
        body, name=name, grid=(N // tn, M // tm),
        in_specs=[pl.BlockSpec((tm, a.shape[1]), lambda j, i: (i, 0)) for a in a_s]
        + [pl.BlockSpec((tn, b.shape[1]), lambda j, i: (j, 0)) for b in b_s],
        out_specs=pl.BlockSpec((tm, tn), lambda j, i: (i, j)),
        out_shape=jax.ShapeDtypeStruct((M, N), F32),
        compiler_params=_params("parallel", "parallel"))(*a_s, *b_s)


def _mm_tn(name, a, b, *, shards=1, o_dtype=BF):
    S, M = a.shape
    N = b.shape[1]
    Ns = N // shards
    tm, tn, tk = _tile(M, 1024), _tile(Ns, 1024), _tile(S, 2048)
    per = Ns // tn
    nk = S // tk
    if shards > 1:
        o_spec = pl.BlockSpec((None, tm, tn), lambda i, j, k: (j // per, i, j % per))
        o_shape = (shards, M, Ns)
    else:
        o_spec = pl.BlockSpec((tm, tn), lambda i, j, k: (i, j))
        o_shape = (M, N)
    return _mm_call(
        name, a, b, None, grid=(M // tm, N // tn, nk),
        a_spec=pl.BlockSpec((tk, tm), lambda i, j, k: (k, i)),
        b_spec=pl.BlockSpec((tk, tn), lambda i, j, k: (k, j)),
        add_spec=None, o_spec=o_spec, o_shape=o_shape, o_dtype=o_dtype, acc_shape=(tm, tn), dims=TN, nk=nk)


def _up_fwd(h2, w_up):
    S, D = h2.shape
    G, _, C = w_up.shape
    tm = _tile(S, 512)
    return _mm_call(
        "up_fwd", h2, w_up, None, grid=(G, S // tm, 1),
        a_spec=pl.BlockSpec((tm, D), lambda g, i, k: (i, 0)),
        b_spec=pl.BlockSpec((None, D, C), lambda g, i, k: (g, 0, 0)),
        add_spec=None, o_spec=pl.BlockSpec((None, tm, C), lambda g, i, k: (g, i, 0)),
        o_shape=(G, S, C), o_dtype=BF, acc_shape=None, dims=NN, nk=1)


def _up_dgrad(du_pre, w_up):
    G, S, C = du_pre.shape
    D = w_up.shape[1]
    tm, tn = _tile(S, 1024), _tile(D, 1024)
    return _mm_call(
        "up_dgrad", du_pre, w_up, None, grid=(S // tm, D // tn, G),
        a_spec=pl.BlockSpec((None, tm, C), lambda i, j, g: (g, i, 0)),
        b_spec=pl.BlockSpec((None, tn, C), lambda i, j, g: (g, j, 0)),
        add_spec=None, o_spec=pl.BlockSpec((tm, tn), lambda i, j, g: (i, j)),
        o_shape=(S, D), o_dtype=F32, acc_shape=(tm, tn), dims=NT, nk=G)


def _up_wgrad(h2, du_pre):
    G, S, C = du_pre.shape
    D = h2.shape[1]
    tm, tk = _tile(D, 512), _tile(S, 2048)
    return _mm_call(
        "up_wgrad", h2, du_pre, None, grid=(G, D // tm, S // tk),
        a_spec=pl.BlockSpec((tk, tm), lambda g, i, k: (k, i)),
        b_spec=pl.BlockSpec((None, tk, C), lambda g, i, k: (g, k, 0)),
        add_spec=None, o_spec=pl.BlockSpec((None, tm, C), lambda g, i, k: (g, i, 0)),
        o_shape=(G, D, C), o_dtype=BF, acc_shape=(tm, C), dims=TN, nk=S // tk)


def _down_fwd(act, w_down, x1):
    G, S, C = act.shape
    D = w_down.shape[2]
    tm, tn = _tile(S, 1024), _tile(D, 1024)
    return _mm_call(
        "down_fwd", act, w_down, x1, grid=(S // tm, D // tn, G),
        a_spec=pl.BlockSpec((None, tm, C), lambda i, j, g: (g, i, 0)),
        b_spec=pl.BlockSpec((None, C, tn), lambda i, j, g: (g, 0, j)),
        add_spec=pl.BlockSpec((tm, tn), lambda i, j, g: (i, j)),
        o_spec=pl.BlockSpec((tm, tn), lambda i, j, g: (i, j)),
        o_shape=(S, D), o_dtype=F32, acc_shape=(tm, tn), dims=NN, nk=G)


def _down_dgrad(dx2, w_down):
    S, D = dx2.shape
    G, C, _ = w_down.shape
    tm = _tile(S, 512)
    return _mm_call(
        "down_dgrad", dx2, w_down, None, grid=(G, S // tm, 1),
        a_spec=pl.BlockSpec((tm, D), lambda g, i, k: (i, 0)),
        b_spec=pl.BlockSpec((None, C, D), lambda g, i, k: (g, 0, 0)),
        add_spec=None, o_spec=pl.BlockSpec((None, tm, C), lambda g, i, k: (g, i, 0)),
        o_shape=(G, S, C), o_dtype=BF, acc_shape=None, dims=NT, nk=1)


def _down_wgrad(act, dx2):
    G, S, C = act.shape
    D = dx2.shape[1]
    tn, tk = _tile(D, 512), _tile(S, 1024)
    return _mm_call(
        "down_wgrad", act, dx2, None, grid=(G, D // tn, S // tk),
        a_spec=pl.BlockSpec((None, tk, C), lambda g, j, k: (g, k, 0)),
        b_spec=pl.BlockSpec((tk, tn), lambda g, j, k: (k, j)),
        add_spec=None, o_spec=pl.BlockSpec((None, C, tn), lambda g, j, k: (g, 0, j)),
        o_shape=(G, C, D), o_dtype=BF, acc_shape=(C, tn), dims=TN, nk=S // tk)


def _row(ts, c):
    return pl.BlockSpec((ts, c), lambda i: (i, 0))


def _bcast(r, c):
    return pl.BlockSpec((r, c), lambda i: (0, 0))


def _accumulate(i, ref, val):
    @pl.when(i == 0)
    def _():
        ref[...] = val

    @pl.when(i > 0)
    def _():
        ref[...] += val


def _rstd(xv):
    return lax.rsqrt(jnp.mean(xv * xv, axis=-1, keepdims=True) + NORM_EPS)


def _rmsnorm_fwd(name, x, g):
    S, D = x.shape
    ts = _tile(S, 512)

    def body(x_ref, g_ref, o_ref):
        xv = x_ref[...]
        o_ref[...] = (xv * _rstd(xv) * g_ref[...]).astype(o_ref.dtype)

    return pl.pallas_call(
        body, name=name, grid=(S // ts,), in_specs=[_row(ts, D), _bcast(1, D)], out_specs=_row(ts, D),
        out_shape=jax.ShapeDtypeStruct((S, D), BF), compiler_params=_params("parallel"))(x, g)


def _norm_bwd_rows(dy, xv, g):
    r = _rstd(xv)
    xh = xv * r
    dxh = dy * g
    dx = r * (dxh - xh * jnp.mean(dxh * xh, axis=-1, keepdims=True))
    return dx, jnp.sum(dy * xh, axis=0, keepdims=True)


def _rmsnorm_bwd(name, dy, x, g, res):
    S, D = x.shape
    ts = _tile(S, 512)

    def body(dy_ref, x_ref, g_ref, res_ref, dx_ref, dg_ref):
        dx, dg = _norm_bwd_rows(dy_ref[...], x_ref[...], g_ref[...])
        dx_ref[...] = dx + res_ref[...]
        _accumulate(pl.program_id(0), dg_ref, dg)

    return pl.pallas_call(
        body, name=name, grid=(S // ts,),
        in_specs=[_row(ts, D), _row(ts, D), _bcast(1, D), _row(ts, D)],
        out_specs=[_row(ts, D), _bcast(1, D)],
        out_shape=[jax.ShapeDtypeStruct((S, D), F32), jax.ShapeDtypeStruct((1, D), F32)],
        compiler_params=_params("arbitrary"))(dy, x, g, res)


def _rope_tables(S):
    half = QK_ROPE // 2
    pos = jnp.arange(S, dtype=F32)
    inv_freq = ROPE_THETA ** (-jnp.arange(0, QK_ROPE, 2, dtype=F32) / QK_ROPE)
    ang = pos[:, None] * inv_freq[None, :]
    cos, sin = jnp.cos(ang), jnp.sin(ang)
    z = jnp.zeros((S, half), F32)
    return jnp.concatenate([cos, z, cos, z], axis=1), jnp.concatenate([-sin, z, sin, z], axis=1)


def _rope_lanes(x, cos, sin_signed, inverse):
    if inverse:
        return x * cos + pltpu.roll(x * sin_signed, LANE // 2, 1)
    return x * cos + pltpu.roll(x, LANE // 2, 1) * sin_signed


def _rope(name, x, cos, sin_signed, inverse):
    S, W = x.shape
    ts = _tile(S, 512)

    def body(x_ref, c_ref, s_ref, o_ref):
        c, s = c_ref[...], s_ref[...]
        for h in range(W // LANE):
            sl = slice(h * LANE, (h + 1) * LANE)
            o_ref[:, sl] = _rope_lanes(x_ref[:, sl], c, s, inverse).astype(o_ref.dtype)

    return pl.pallas_call(
        body, name=name, grid=(S // ts,), in_specs=[_row(ts, W), _row(ts, LANE), _row(ts, LANE)],
        out_specs=_row(ts, W), out_shape=jax.ShapeDtypeStruct((S, W), BF),
        compiler_params=_params("parallel"))(x, cos, sin_signed)


LAT_W = 1024
_CQ = slice(0, Q_RANK)
_CKV = slice(Q_RANK, Q_RANK + KV_RANK)
_KPE = slice(Q_RANK + KV_RANK, Q_RANK + KV_RANK + LANE)


def _mla_prep(lat, qg, kvg, cos, sin_signed):
    S = lat.shape[0]
    ts = _tile(S, 512)

    def body(lat_ref, qg_ref, kvg_ref, c_ref, s_ref, qn_ref, kvn_ref, kpe_ref):
        cq = lat_ref[:, _CQ]
        qn_ref[...] = (cq * _rstd(cq) * qg_ref[...]).astype(BF)
        ckv = lat_ref[:, _CKV]
        kvn_ref[...] = (ckv * _rstd(ckv) * kvg_ref[...]).astype(BF)
        kpe_ref[...] = _rope_lanes(lat_ref[:, _KPE], c_ref[...], s_ref[...], False).astype(BF)

    return pl.pallas_call(
        body, name="mla_prep", grid=(S // ts,),
        in_specs=[_row(ts, LAT_W), _bcast(1, Q_RANK), _bcast(1, KV_RANK), _row(ts, LANE), _row(ts, LANE)],
        out_specs=[_row(ts, Q_RANK), _row(ts, KV_RANK), _row(ts, LANE)],
        out_shape=[jax.ShapeDtypeStruct((S, Q_RANK), BF), jax.ShapeDtypeStruct((S, KV_RANK), BF),
                   jax.ShapeDtypeStruct((S, LANE), BF)],
        compiler_params=_params("parallel"))(lat, qg, kvg, cos, sin_signed)


def _mla_prep_bwd(lat, qg, kvg, cos, sin_signed, d_qn, d_kvn, d_kpe):
    S = lat.shape[0]
    ts = _tile(S, 512)

    def body(lat_ref, qg_ref, kvg_ref, c_ref, s_ref, dqn_ref, dkvn_ref, dkpe_ref, dlat_ref, dqg_ref, dkvg_ref):
        i = pl.program_id(0)
        dcq, dqg = _norm_bwd_rows(dqn_ref[...], lat_ref[:, _CQ], qg_ref[...])
        dckv, dkvg = _norm_bwd_rows(dkvn_ref[...], lat_ref[:, _CKV], kvg_ref[...])
        dlat_ref[:, _CQ] = dcq.astype(BF)
        dlat_ref[:, _CKV] = dckv.astype(BF)
        dkpe = dkpe_ref[0]
        for g in range(1, d_kpe.shape[0]):
            dkpe = dkpe + dkpe_ref[g]
        dlat_ref[:, _KPE] = _rope_lanes(dkpe, c_ref[...], s_ref[...], True).astype(BF)
        dlat_ref[:, _KPE.stop:] = jnp.zeros((ts, LAT_W - _KPE.stop), BF)
        _accumulate(i, dqg_ref, dqg)
        _accumulate(i, dkvg_ref, dkvg)

    return pl.pallas_call(
        body, name="mla_prep_bwd", grid=(S // ts,),
        in_specs=[_row(ts, LAT_W), _bcast(1, Q_RANK), _bcast(1, KV_RANK), _row(ts, LANE), _row(ts, LANE),
                  _row(ts, Q_RANK), _row(ts, KV_RANK), pl.BlockSpec((d_kpe.shape[0], ts, LANE), lambda i: (0, i, 0))],
        out_specs=[_row(ts, LAT_W), _bcast(1, Q_RANK), _bcast(1, KV_RANK)],
        out_shape=[jax.ShapeDtypeStruct((S, LAT_W), BF), jax.ShapeDtypeStruct((1, Q_RANK), F32),
                   jax.ShapeDtypeStruct((1, KV_RANK), F32)],
        compiler_params=_params("arbitrary"))(lat, qg, kvg, cos, sin_signed, d_qn, d_kvn, d_kpe)


def _sigmoid(z):
    return 1.0 / (1.0 + jnp.exp(-z))


def _merge_fwd(gpre, b_gate, o_a, o_b):
    S, D = o_a.shape
    ts = _tile(S, 256)

    def body(g_ref, b_ref, oa_ref, ob_ref, m_ref):
        ga = _sigmoid(g_ref[:, :D] + b_ref[:, :D])
        gb = _sigmoid(g_ref[:, D:] + b_ref[:, D:])
        m_ref[...] = (ga * oa_ref[...] + gb * ob_ref[...]).astype(BF)

    return pl.pallas_call(
        body, name="merge_fwd", grid=(S // ts,),
        in_specs=[_row(ts, 2 * D), _bcast(1, 2 * D), _row(ts, D), _row(ts, D)], out_specs=_row(ts, D),
        out_shape=jax.ShapeDtypeStruct((S, D), BF), compiler_params=_params("parallel"))(gpre, b_gate, o_a, o_b)


def _merge_bwd(d_merge, gpre, b_gate, o_a, o_b):
    S, D = o_a.shape
    ts = _tile(S, 256)

    def body(dm_ref, g_ref, b_ref, oa_ref, ob_ref, doa_ref, dob_ref, dg_ref, db_ref):
        dm = dm_ref[...]
        ga = _sigmoid(g_ref[:, :D] + b_ref[:, :D])
        gb = _sigmoid(g_ref[:, D:] + b_ref[:, D:])
        doa_ref[...] = (dm * ga).astype(BF)
        dob_ref[...] = (dm * gb).astype(BF)
        dga = dm * oa_ref[...] * ga * (1.0 - ga)
        dgb = dm * ob_ref[...] * gb * (1.0 - gb)
        dg_ref[:, :D] = dga.astype(BF)
        dg_ref[:, D:] = dgb.astype(BF)
        i = pl.program_id(0)
        part = jnp.concatenate([jnp.sum(dga, axis=0, keepdims=True), jnp.sum(dgb, axis=0, keepdims=True)], axis=1)
        _accumulate(i, db_ref, part)

    return pl.pallas_call(
        body, name="merge_bwd", grid=(S // ts,),
        in_specs=[_row(ts, D), _row(ts, 2 * D), _bcast(1, 2 * D), _row(ts, D), _row(ts, D)],
        out_specs=[_row(ts, D), _row(ts, D), _row(ts, 2 * D), _bcast(1, 2 * D)],
        out_shape=[jax.ShapeDtypeStruct((S, D), BF), jax.ShapeDtypeStruct((S, D), BF),
                   jax.ShapeDtypeStruct((S, 2 * D), BF), jax.ShapeDtypeStruct((1, 2 * D), F32)],
        compiler_params=_params("arbitrary"))(d_merge, gpre, b_gate, o_a, o_b)


def _final_loss(x2, tgt, gf):
    S, D = x2.shape
    ts = _tile(S, 512)

    def body(x_ref, t_ref, g_ref, dx_ref, dg_ref, loss_ref):
        i = pl.program_id(0)
        xv = x_ref[...]
        g = g_ref[...]
        y = xv * _rstd(xv) * g
        err = y - t_ref[...]
        dx, dg = _norm_bwd_rows(err * (1.0 / D), xv, g)
        dx_ref[...] = dx
        _accumulate(i, dg_ref, dg)
        part = 0.5 * jnp.sum(jnp.mean(err * err, axis=-1, keepdims=True), axis=0, keepdims=True)
        _accumulate(i, loss_ref, jnp.broadcast_to(part, (8, LANE)))

    return pl.pallas_call(
        body, name="final_loss", grid=(S // ts,),
        in_specs=[_row(ts, D), _row(ts, D), _bcast(1, D)],
        out_specs=[_row(ts, D), _bcast(1, D), _bcast(8, LANE)],
        out_shape=[jax.ShapeDtypeStruct((S, D), F32), jax.ShapeDtypeStruct((1, D), F32),
                   jax.ShapeDtypeStruct((8, LANE), F32)],
        compiler_params=_params("arbitrary"))(x2, tgt, gf)


HALO = 16


SUB = 8


def _shift_down(cur, prev, k, rows):
    out = pltpu.roll(cur, k, 0)
    head = out[:SUB]
    for j in range(k):
        head = jnp.where(rows == j, prev[HALO - k + j:HALO - k + j + 1, :], head)
    return jnp.concatenate([head, out[SUB:]], axis=0)


def _shift_up(cur, nxt, k, rows, ts):
    out = pltpu.roll(cur, ts - k, 0)
    tail = out[ts - SUB:]
    for j in range(k):
        tail = jnp.where(rows == SUB - k + j, nxt[j:j + 1, :], tail)
    return jnp.concatenate([out[:ts - SUB], tail], axis=0)


def _conv_rows(cur, prev, w, b, rows):
    return b + w[0:1, :] * _shift_down(cur, prev, 2, rows) + w[1:2, :] * _shift_down(cur, prev, 1, rows) + w[2:3, :] * cur


def _conv_specs(ts, C, shard_of):
    nh = ts // HALO
    cur = pl.BlockSpec((None, ts, C), lambda g, i: (shard_of(g), i, 0))
    prev = pl.BlockSpec((None, HALO, C), lambda g, i: (shard_of(g), jnp.maximum(i * nh - 1, 0), 0))
    return cur, prev


def _ffn_act(u_pre, conv_w, conv_b):
    G4, S, C = u_pre.shape
    G = G4 // 2
    ts = _tile(S, 256)

    def body(up_ref, upp_ref, gt_ref, gtp_ref, wu_ref, wg_ref, bu_ref, bg_ref, act_ref, u_ref):
        first = pl.program_id(1) == 0
        rows = lax.broadcasted_iota(jnp.int32, (SUB, C), 0)
        pu = jnp.where(first, 0.0, upp_ref[...].astype(F32))
        pg = jnp.where(first, 0.0, gtp_ref[...].astype(F32))
        up = _conv_rows(up_ref[...].astype(F32), pu, wu_ref[...], bu_ref[...], rows)
        gate = _conv_rows(gt_ref[...].astype(F32), pg, wg_ref[...], bg_ref[...], rows)
        act_ref[...] = (gate * _sigmoid(gate) * up).astype(BF)
        u_ref[0] = up.astype(BF)
        u_ref[1] = gate.astype(BF)

    cur_u, prev_u = _conv_specs(ts, C, lambda g: g)
    cur_g, prev_g = _conv_specs(ts, C, lambda g: g + G)
    w_u = pl.BlockSpec((None, 3, C), lambda g, i: (g, 0, 0))
    w_g = pl.BlockSpec((None, 3, C), lambda g, i: (g + G, 0, 0))
    b_u = pl.BlockSpec((None, 1, C), lambda g, i: (g, 0, 0))
    b_g = pl.BlockSpec((None, 1, C), lambda g, i: (g + G, 0, 0))
    pair = pl.BlockSpec((2, None, ts, C), lambda g, i: (0, g, i, 0))
    act, u = pl.pallas_call(
        body, name="ffn_act", grid=(G, S // ts),
        in_specs=[cur_u, prev_u, cur_g, prev_g, w_u, w_g, b_u, b_g],
        out_specs=[pl.BlockSpec((None, ts, C), lambda g, i: (g, i, 0)), pair],
        out_shape=[jax.ShapeDtypeStruct((G, S, C), BF), jax.ShapeDtypeStruct((2, G, S, C), BF)],
        compiler_params=_params("parallel", "parallel"))(u_pre, u_pre, u_pre, u_pre, conv_w, conv_w, conv_b, conv_b)
    return act, u


def _ffn_act_conv_bwd(u, d_act, u_pre, conv_w):
    _, G, S, C = u.shape
    ts = _tile(S, 256)
    nh = ts // HALO
    last_halo = S // HALO - 1

    def d_outputs(up, gate, da):
        sg = _sigmoid(gate)
        return da * (gate * sg), da * up * (sg * (1.0 + gate * (1.0 - sg)))

    def body(u_ref, un_ref, da_ref, dan_ref, x_ref, w_ref, dpre_ref, dw_ref, db_ref):
        i = pl.program_id(1)
        rows = lax.broadcasted_iota(jnp.int32, (SUB, C), 0)
        da_n = jnp.where(i == pl.num_programs(1) - 1, 0.0, dan_ref[...].astype(F32))
        du = d_outputs(u_ref[0].astype(F32), u_ref[1].astype(F32), da_ref[...].astype(F32))
        du_n = d_outputs(un_ref[0].astype(F32), un_ref[1].astype(F32), da_n)
        for kind in range(2):
            du_c, w = du[kind], w_ref[kind]
            up1, up2 = _shift_up(du_c, du_n[kind], 1, rows, ts), _shift_up(du_c, du_n[kind], 2, rows, ts)
            dpre_ref[kind] = (w[2:3, :] * du_c + w[1:2, :] * up1 + w[0:1, :] * up2).astype(BF)
            x_c = x_ref[kind].astype(F32)
            dw = jnp.concatenate([
                jnp.sum(up2 * x_c, axis=0, keepdims=True),
                jnp.sum(up1 * x_c, axis=0, keepdims=True),
                jnp.sum(du_c * x_c, axis=0, keepdims=True)], axis=0)
            db = jnp.sum(du_c, axis=0, keepdims=True)

            @pl.when(i == 0)
            def _():
                dw_ref[kind] = dw
                db_ref[kind] = db

            @pl.when(i > 0)
            def _():
                dw_ref[kind] += dw
                db_ref[kind] += db

    def nxt_row(i):
        return jnp.minimum((i + 1) * nh, last_halo)

    pair = pl.BlockSpec((2, None, ts, C), lambda g, i: (0, g, i, 0))
    pair_n = pl.BlockSpec((2, None, HALO, C), lambda g, i: (0, g, nxt_row(i), 0))
    one = pl.BlockSpec((None, ts, C), lambda g, i: (g, i, 0))
    one_n = pl.BlockSpec((None, HALO, C), lambda g, i: (g, nxt_row(i), 0))
    w_spec = pl.BlockSpec((2, None, 3, C), lambda g, i: (0, g, 0, 0))
    b_spec = pl.BlockSpec((2, None, 1, C), lambda g, i: (0, g, 0, 0))
    dpre, dw, db = pl.pallas_call(
        body, name="ffn_act_conv_bwd", grid=(G, S // ts),
        in_specs=[pair, pair_n, one, one_n, pair, w_spec], out_specs=[pair, w_spec, b_spec],
        out_shape=[jax.ShapeDtypeStruct((2, G, S, C), BF), jax.ShapeDtypeStruct((2, G, 3, C), F32),
                   jax.ShapeDtypeStruct((2, G, 1, C), F32)],
        compiler_params=_params("parallel", "arbitrary"))(
            u, u, d_act, d_act, u_pre.reshape(2, G, S, C), conv_w.reshape(2, G, 3, C))
    return dpre.reshape(2 * G, S, C), dw.reshape(2 * G, 3, C), db.reshape(2 * G, 1, C)


MLA_T = 1024
MLA_HB = 4
MLA_BWD_HB = 2


def _mla_pairs(n, by_row):
    if by_row:
        pairs = [(i, j) for i in range(n) for j in range(i + 1)]
    else:
        pairs = [(i, j) for j in range(n) for i in range(j, n)]
    return jnp.asarray([p[0] for p in pairs], jnp.int32), jnp.asarray([p[1] for p in pairs], jnp.int32)


def _mla_specs(hb):
    q = pl.BlockSpec((MLA_T, hb * HEAD), lambda g, t, it, jt: (it[t], g))
    k = pl.BlockSpec((MLA_T, hb * HEAD), lambda g, t, it, jt: (jt[t], g))
    kpe = pl.BlockSpec((MLA_T, HEAD), lambda g, t, it, jt: (jt[t], 0))
    lse = pl.BlockSpec((hb, MLA_T, LANE), lambda g, t, it, jt: (g, it[t], 0))
    return q, k, kpe, lse


def _mla_head(ref, hh):
    return ref[:, hh * HEAD:(hh + 1) * HEAD]


LOG2E = math.log2(math.e)
MLA_EXP2_SCALE = MLA_SCALE * LOG2E


def _mla_scores(qn_ref, qp_ref, kn_ref, kpe, hh, ok):
    q = jnp.concatenate([_mla_head(qn_ref, hh), _mla_head(qp_ref, hh)], axis=1)
    k = jnp.concatenate([_mla_head(kn_ref, hh), kpe], axis=1)
    s = lax.dot_general(q, k, NT, preferred_element_type=F32)
    return q, k, s if ok is None else jnp.where(ok, s, -jnp.inf)


def _mla_diagonal_mask():
    row = lax.broadcasted_iota(jnp.int32, (MLA_T, MLA_T), 0)
    col = lax.broadcasted_iota(jnp.int32, (MLA_T, MLA_T), 1)
    return col <= row


def _mla_step(i, j, step):
    @pl.when(j < i)
    def _():
        step(None)

    @pl.when(j == i)
    def _():
        step(_mla_diagonal_mask())


def _mla_fwd(qn, qp, kn, kpe, v):
    S = qn.shape[0]
    it, jt = _mla_pairs(S // MLA_T, True)

    def body(it_ref, jt_ref, qn_ref, qp_ref, kn_ref, kpe_ref, v_ref, o_ref, lse_ref, m_scr, acc_scr):
        t = pl.program_id(1)
        i, j = it_ref[t], jt_ref[t]

        @pl.when(j == 0)
        def _():
            m_scr[...] = jnp.full(m_scr.shape, -jnp.inf, F32)
            acc_scr[...] = jnp.zeros(acc_scr.shape, F32)

        def step(ok):
            kpe_v = kpe_ref[...]
            ones = jnp.ones((MLA_T, HEAD), BF)
            state = [(m_scr[hh], acc_scr[hh]) for hh in range(MLA_HB)]
            new = []
            for hh in range(MLA_HB):
                m_prev, acc = state[hh]
                _, _, s = _mla_scores(qn_ref, qp_ref, kn_ref, kpe_v, hh, ok)
                m_new = jnp.maximum(m_prev, jnp.max(s, axis=1, keepdims=True))
                p = jnp.exp2((s - m_new) * MLA_EXP2_SCALE).astype(BF)
                v1 = jnp.concatenate([_mla_head(v_ref, hh), ones], axis=1)
                alpha = jnp.exp2((m_prev - m_new) * MLA_EXP2_SCALE)
                new.append((m_new, alpha * acc + lax.dot_general(p, v1, NN, preferred_element_type=F32)))
            for hh in range(MLA_HB):
                m_scr[hh], acc_scr[hh] = new[hh]

        _mla_step(i, j, step)

        @pl.when(j == i)
        def _():
            for hh in range(MLA_HB):
                l = acc_scr[hh, :, HEAD:]
                o_ref[:, hh * HEAD:(hh + 1) * HEAD] = (acc_scr[hh, :, :HEAD] / l).astype(BF)
                lse_ref[hh] = m_scr[hh] * MLA_SCALE + jnp.log(l)

    qspec, kspec, kpespec, lsespec = _mla_specs(MLA_HB)
    grid_spec = pltpu.PrefetchScalarGridSpec(
        num_scalar_prefetch=2, grid=(MLA_HEADS // MLA_HB, it.shape[0]),
        in_specs=[qspec, qspec, kspec, kpespec, kspec], out_specs=[qspec, lsespec],
        scratch_shapes=[pltpu.VMEM((MLA_HB, MLA_T, 1), F32), pltpu.VMEM((MLA_HB, MLA_T, 2 * HEAD), F32)])
    return pl.pallas_call(
        body, name="mla_fwd", grid_spec=grid_spec,
        out_shape=[jax.ShapeDtypeStruct((S, MLA_HEADS * HEAD), BF), jax.ShapeDtypeStruct((MLA_HEADS, S, LANE), F32)],
        compiler_params=_params("parallel", "arbitrary"))(it, jt, qn, qp, kn, kpe, v)


def _mla_p_ds(qn_ref, qp_ref, kn_ref, kpe, v_ref, do_ref, o_ref, lse_ref, hh, ok):
    q, k, s = _mla_scores(qn_ref, qp_ref, kn_ref, kpe, hh, ok)
    p = jnp.exp2(s * MLA_EXP2_SCALE - lse_ref[hh][:, 0:1] * LOG2E)
    do = _mla_head(do_ref, hh)
    delta = jnp.sum(do.astype(F32) * _mla_head(o_ref, hh).astype(F32), axis=1, keepdims=True)
    dp = lax.dot_general(do, _mla_head(v_ref, hh), NT, preferred_element_type=F32)
    ds = p * (dp - delta) * MLA_SCALE
    return q, k, p, ds, do


def _mla_bwd(qn, qp, kn, kpe, v, do, o, lse):
    S = qn.shape[0]
    nq = S // MLA_T
    hb = MLA_BWD_HB
    it, jt = _mla_pairs(nq, False)

    def body(it_ref, jt_ref, qn_ref, qp_ref, kn_ref, kpe_ref, v_ref, do_ref, o_ref, lse_ref,
             dqn_ref, dqp_ref, dkn_ref, dv_ref, dkpe_ref, dq_acc, dk_acc, dv_acc, stage_n, stage_p, osem):
        t = pl.program_id(1)
        i, j = it_ref[t], jt_ref[t]

        @pl.when(t == 0)
        def _():
            dq_acc[...] = jnp.zeros(dq_acc.shape, F32)

        @pl.when(i == j)
        def _():
            dk_acc[...] = jnp.zeros(dk_acc.shape, F32)
            dv_acc[...] = jnp.zeros(dv_acc.shape, F32)

        def step(ok):
            kpe_v = kpe_ref[...]
            for hh in range(hb):
                q, k, p, ds, do_h = _mla_p_ds(qn_ref, qp_ref, kn_ref, kpe_v, v_ref, do_ref, o_ref, lse_ref, hh, ok)
                ds = ds.astype(BF)
                dv_acc[hh] += lax.dot_general(p.astype(BF), do_h, TN, preferred_element_type=F32)
                dk_acc[hh] += lax.dot_general(ds, q, TN, preferred_element_type=F32)
                dq_acc[i, hh] += lax.dot_general(ds, k, NN, preferred_element_type=F32)

        _mla_step(i, j, step)

        @pl.when(i == j)
        def _():
            for hh in range(hb):
                stage_n[:, hh * HEAD:(hh + 1) * HEAD] = dq_acc[i, hh, :, :HEAD].astype(BF)
                stage_p[:, hh * HEAD:(hh + 1) * HEAD] = dq_acc[i, hh, :, HEAD:]
            rows = pl.ds(pl.multiple_of(i * MLA_T, MLA_T), MLA_T)
            cols = pl.ds(pl.multiple_of(pl.program_id(0) * hb * HEAD, LANE), hb * HEAD)
            out_n = pltpu.make_async_copy(stage_n, dqn_ref.at[rows, cols], osem.at[0])
            out_p = pltpu.make_async_copy(stage_p, dqp_ref.at[rows, cols], osem.at[1])
            out_n.start()
            out_p.start()
            out_n.wait()
            out_p.wait()

        @pl.when(i == nq - 1)
        def _():
            dkpe = dk_acc[0, :, HEAD:]
            for hh in range(hb):
                dkn_ref[:, hh * HEAD:(hh + 1) * HEAD] = dk_acc[hh, :, :HEAD].astype(BF)
                dv_ref[:, hh * HEAD:(hh + 1) * HEAD] = dv_acc[hh].astype(BF)
                if hh:
                    dkpe = dkpe + dk_acc[hh, :, HEAD:]
            dkpe_ref[...] = dkpe

    qspec, kspec, kpespec, lsespec = _mla_specs(hb)
    dkpespec = pl.BlockSpec((None, MLA_T, HEAD), lambda g, t, it, jt: (g, jt[t], 0))
    grid_spec = pltpu.PrefetchScalarGridSpec(
        num_scalar_prefetch=2, grid=(MLA_HEADS // hb, it.shape[0]),
        in_specs=[qspec, qspec, kspec, kpespec, kspec, qspec, qspec, lsespec],
        out_specs=[ANY, ANY, kspec, kspec, dkpespec],
        scratch_shapes=[pltpu.VMEM((nq, hb, MLA_T, 2 * HEAD), F32), pltpu.VMEM((hb, MLA_T, 2 * HEAD), F32),
                        pltpu.VMEM((hb, MLA_T, HEAD), F32), pltpu.VMEM((MLA_T, hb * HEAD), BF),
                        pltpu.VMEM((MLA_T, hb * HEAD), F32), pltpu.SemaphoreType.DMA((2,))])
    wide = jax.ShapeDtypeStruct((S, MLA_HEADS * HEAD), BF)
    return pl.pallas_call(
        body, name="mla_bwd", grid_spec=grid_spec,
        out_shape=[wide, jax.ShapeDtypeStruct((S, MLA_HEADS * HEAD), F32), wide, wide,
                   jax.ShapeDtypeStruct((MLA_HEADS // hb, S, HEAD), F32)],
        compiler_params=_params("parallel", "arbitrary"))(it, jt, qn, qp, kn, kpe, v, do, o, lse)


DIL_W = 3 * DIL_HPG * HEAD
DIL_O = DIL_HPG * HEAD
DIL_STEP_BLOCKS = 4


def _dil_slopes(g):
    return [2.0 ** (-ALIBI_MAX_BIAS * (g * DIL_HPG + hh + 1) / DIL_HEADS) for hh in range(DIL_HPG)]


def _dil_bias(dil):
    p = lax.broadcasted_iota(jnp.int32, (DIL_BLOCK, DIL_BLOCK), 0)
    kk = lax.broadcasted_iota(jnp.int32, (DIL_BLOCK, DIL_BLOCK), 1)
    jc = p - kk
    dist_c = (dil * jc).astype(F32)
    dist_p = (dil * (jc + DIL_BLOCK)).astype(F32)
    return jc >= 0, jc <= 0, dist_c, dist_p


def _dil_bias2(dil):
    p = lax.broadcasted_iota(jnp.int32, (DIL_BLOCK, 2 * DIL_BLOCK), 0)
    kk = lax.broadcasted_iota(jnp.int32, (DIL_BLOCK, 2 * DIL_BLOCK), 1)
    j = p + DIL_BLOCK - kk
    return (j >= 0) & (j <= DIL_BLOCK), kk < DIL_BLOCK, (dil * j).astype(F32)


def _dil_head(blk, hh):
    q = blk[:, hh * HEAD:(hh + 1) * HEAD]
    k = blk[:, DIL_O + hh * HEAD:DIL_O + (hh + 1) * HEAD]
    v = blk[:, 2 * DIL_O + hh * HEAD:2 * DIL_O + (hh + 1) * HEAD]
    return q, k, v


def _dil_s(q, k, slope, dist, ok):
    s = lax.dot_general(q, k, NT, preferred_element_type=F32) * DIL_SCALE - slope * dist
    return jnp.where(ok, s, -jnp.inf)


def _dil_view(a, dil):
    S, W = a.shape
    return a.reshape(S // dil, dil * W)


def _dil_fwd(qkv, g):
    _, dil = DIL_PATTERNS[g]
    S = qkv.shape[0]
    L = S // dil
    nb = L // DIL_BLOCK
    slopes = _dil_slopes(g)

    bb = min(DIL_STEP_BLOCKS, nb)
    rows = bb * DIL_BLOCK

    def body(cur_ref, prev_ref, o_ref, lse_ref):
        n = pl.program_id(1)
        ok, in_prev, dist = _dil_bias2(dil)
        for b in range(bb):
            if b == 0:
                both = jnp.concatenate([prev_ref[...], cur_ref[0:DIL_BLOCK, :]], axis=0)
                ok_b = ok & (~in_prev | (n > 0))
            else:
                both = cur_ref[(b - 1) * DIL_BLOCK:(b + 1) * DIL_BLOCK, :]
                ok_b = ok
            for hh in range(DIL_HPG):
                q, _, _ = _dil_head(both[DIL_BLOCK:], hh)
                _, k2, v2 = _dil_head(both, hh)
                s = _dil_s(q, k2, slopes[hh], dist, ok_b)
                m = jnp.max(s, axis=1, keepdims=True)
                p = jnp.exp(s - m)
                l = jnp.sum(p, axis=1, keepdims=True)
                o = lax.dot_general(p.astype(BF), v2, NN, preferred_element_type=F32) / l
                rs, sl = slice(b * DIL_BLOCK, (b + 1) * DIL_BLOCK), slice(hh * HEAD, (hh + 1) * HEAD)
                o_ref[rs, sl] = o
                lse_ref[rs, sl] = jnp.broadcast_to(m + jnp.log(l), (DIL_BLOCK, HEAD))

    ospec = pl.BlockSpec((rows, DIL_O), lambda r, n: (n, r))
    o, lse = pl.pallas_call(
        body, name=f"dil_fwd{g}", grid=(dil, nb // bb),
        in_specs=[pl.BlockSpec((rows, DIL_W), lambda r, n: (n, r)),
                  pl.BlockSpec((DIL_BLOCK, DIL_W), lambda r, n: (jnp.maximum(n * bb - 1, 0), r))],
        out_specs=[ospec, ospec],
        out_shape=[jax.ShapeDtypeStruct((L, dil * DIL_O), F32), jax.ShapeDtypeStruct((L, dil * DIL_O), F32)],
        compiler_params=_params("parallel", "parallel"))(_dil_view(qkv, dil), _dil_view(qkv, dil))
    return o.reshape(S, DIL_O), lse.reshape(S, DIL_O)


def _dil_combine(os_, lses):
    S = os_[0].shape[0]
    ts = _tile(S, 512)

    def body(o0, o1, o2, l0, l1, l2, out_ref, lse_ref):
        a, b, c = l0[...], l1[...], l2[...]
        m = jnp.maximum(jnp.maximum(a, b), c)
        ea, eb, ec = jnp.exp(a - m), jnp.exp(b - m), jnp.exp(c - m)
        tot = ea + eb + ec
        out_ref[...] = ((ea * o0[...] + eb * o1[...] + ec * o2[...]) / tot).astype(BF)
        lse_ref[...] = m + jnp.log(tot)

    return pl.pallas_call(
        body, name="dil_combine", grid=(S // ts,), in_specs=[_row(ts, DIL_O)] * 6,
        out_specs=[_row(ts, DIL_O), _row(ts, DIL_O)],
        out_shape=[jax.ShapeDtypeStruct((S, DIL_O), BF), jax.ShapeDtypeStruct((S, DIL_O), F32)],
        compiler_params=_params("parallel"))(*os_, *lses)


def _dil_delta(do, out):
    S = do.shape[0]
    ts = _tile(S, 512)

    def body(do_ref, o_ref, d_ref):
        for hh in range(DIL_HPG):
            sl = slice(hh * HEAD, (hh + 1) * HEAD)
            d = jnp.sum(do_ref[:, sl].astype(F32) * o_ref[:, sl].astype(F32), axis=1, keepdims=True)
            d_ref[:, sl] = jnp.broadcast_to(d, (ts, HEAD))

    return pl.pallas_call(
        body, name="dil_delta", grid=(S // ts,), in_specs=[_row(ts, DIL_O)] * 2, out_specs=_row(ts, DIL_O),
        out_shape=jax.ShapeDtypeStruct((S, DIL_O), F32), compiler_params=_params("parallel"))(do, out)


def _dil_bwd(qkv, do, lse, delta, g):
    _, dil = DIL_PATTERNS[g]
    S = qkv.shape[0]
    L = S // dil
    nb = L // DIL_BLOCK
    slopes = _dil_slopes(g)

    def pair(q, k, v, do_h, lse_h, delta_h, slope, dist, ok):
        s = _dil_s(q, k, slope, dist, ok)
        p = jnp.exp(s - lse_h)
        dp = lax.dot_general(do_h, v, NT, preferred_element_type=F32)
        ds = (p * (dp - delta_h) * DIL_SCALE).astype(BF)
        return p.astype(BF), ds

    bb = min(DIL_STEP_BLOCKS, nb)
    rows = bb * DIL_BLOCK
    steps = nb // bb

    def body(cur_ref, prev_ref, next_ref, doc_ref, don_ref, lsec_ref, lsen_ref, dlc_ref, dln_ref, out_ref):
        n = pl.program_id(1)
        ok2, in_prev, dist2 = _dil_bias2(dil)
        _, ok_p0, _, dist_p = _dil_bias(dil)
        for b in range(bb):
            rs = slice(b * DIL_BLOCK, (b + 1) * DIL_BLOCK)
            rn = slice((b + 1) * DIL_BLOCK, (b + 2) * DIL_BLOCK)
            two = slice(b * DIL_BLOCK, (b + 2) * DIL_BLOCK)
            first, last = b == 0, b == bb - 1
            if first:
                keys = jnp.concatenate([prev_ref[...], cur_ref[rs, :]], axis=0)
                ok_ab = ok2 & (~in_prev | (n > 0))
            else:
                keys = cur_ref[(b - 1) * DIL_BLOCK:(b + 1) * DIL_BLOCK, :]
                ok_ab = ok2
            qrows = jnp.concatenate([cur_ref[rs, :], next_ref[...]], axis=0) if last else cur_ref[two, :]
            ok_n = ok_p0 & (n < steps - 1) if last else ok_p0
            for hh in range(DIL_HPG):
                sl = slice(hh * HEAD, (hh + 1) * HEAD)
                q2, _, _ = _dil_head(qrows, hh)
                _, k2, v2 = _dil_head(keys, hh)
                q, qn, kc, vc = q2[:DIL_BLOCK], q2[DIL_BLOCK:], k2[DIL_BLOCK:], v2[DIL_BLOCK:]
                do2 = jnp.concatenate([doc_ref[rs, sl], don_ref[:, sl]], axis=0) if last else doc_ref[two, sl]
                do_c, do_n = do2[:DIL_BLOCK], do2[DIL_BLOCK:]
                lse_c = lsec_ref[rs, sl][:, 0:1]
                lse_n = (lsen_ref[:, sl] if last else lsec_ref[rn, sl])[:, 0:1]
                dl_c = dlc_ref[rs, sl][:, 0:1]
                dl_n = (dln_ref[:, sl] if last else dlc_ref[rn, sl])[:, 0:1]
                p_ab, ds_ab = pair(q, k2, v2, do_c, lse_c, dl_c, slopes[hh], dist2, ok_ab)
                p_n, ds_n = pair(qn, kc, vc, do_n, lse_n, dl_n, slopes[hh], dist_p, ok_n)
                dq = lax.dot_general(ds_ab, k2, NN, preferred_element_type=F32)
                dk = lax.dot_general(jnp.concatenate([ds_ab[:, DIL_BLOCK:], ds_n], axis=0), q2, TN, preferred_element_type=F32)
                dv = lax.dot_general(jnp.concatenate([p_ab[:, DIL_BLOCK:], p_n], axis=0), do2, TN, preferred_element_type=F32)
                out_ref[rs, sl] = dq.astype(BF)
                out_ref[rs, DIL_O + hh * HEAD:DIL_O + (hh + 1) * HEAD] = dk.astype(BF)
                out_ref[rs, 2 * DIL_O + hh * HEAD:2 * DIL_O + (hh + 1) * HEAD] = dv.astype(BF)

    cur_w = pl.BlockSpec((rows, DIL_W), lambda r, n: (n, r))
    prev_w = pl.BlockSpec((DIL_BLOCK, DIL_W), lambda r, n: (jnp.maximum(n * bb - 1, 0), r))
    next_w = pl.BlockSpec((DIL_BLOCK, DIL_W), lambda r, n: (jnp.minimum((n + 1) * bb, nb - 1), r))
    cur_o = pl.BlockSpec((rows, DIL_O), lambda r, n: (n, r))
    next_o = pl.BlockSpec((DIL_BLOCK, DIL_O), lambda r, n: (jnp.minimum((n + 1) * bb, nb - 1), r))
    qv, dov, lsev, dlv = _dil_view(qkv, dil), _dil_view(do, dil), _dil_view(lse, dil), _dil_view(delta, dil)
    out = pl.pallas_call(
        body, name=f"dil_bwd{g}", grid=(dil, steps),
        in_specs=[cur_w, prev_w, next_w, cur_o, next_o, cur_o, next_o, cur_o, next_o],
        out_specs=cur_w, out_shape=jax.ShapeDtypeStruct((L, dil * DIL_W), BF),
        compiler_params=_params("parallel", "parallel"))(qv, qv, qv, dov, dov, lsev, lsev, dlv, dlv)
    return out.reshape(S, DIL_W)


def _adamw(name, w, g, m, v):
    R, C = w.shape
    tr, tc = _adamw_block(R, C)

    def body(w_ref, g_ref, m_ref, v_ref, go_ref, d_ref, nm_ref, nv_ref):
        gv = g_ref[...]
        go_ref[...] = gv
        nm = ADAM_B1 * m_ref[...] + (1.0 - ADAM_B1) * gv
        nv = ADAM_B2 * v_ref[...] + (1.0 - ADAM_B2) * (gv * gv)
        m_hat = nm / (1.0 - ADAM_B1 ** ADAM_STEP)
        v_hat = nv / (1.0 - ADAM_B2 ** ADAM_STEP)
        d_ref[...] = -ADAM_LR * (m_hat / (jnp.sqrt(v_hat) + ADAM_EPS) + ADAM_WD * w_ref[...])
        nm_ref[...] = nm
        nv_ref[...] = nv

    spec = pl.BlockSpec((tr, tc), lambda i, j: (i, j))
    shp = jax.ShapeDtypeStruct((R, C), F32)
    return pl.pallas_call(
        body, name=name, grid=(R // tr, C // tc), in_specs=[spec] * 4, out_specs=[spec] * 4, out_shape=[shp] * 4,
        compiler_params=_params("parallel", "parallel"))(w, g, m, v)


ADAMW_BLOCK_ELEMS = 640 * 1024


def _adamw_block(R, C):
    if R * C <= ADAMW_BLOCK_ELEMS:
        return R, C
    tr = _tile_rows(R, max(8, ADAMW_BLOCK_ELEMS // C))
    tc = _tile(C, max(LANE, ADAMW_BLOCK_ELEMS // R))
    if tr * C >= R * tc or R * tc > ADAMW_BLOCK_ELEMS:
        return tr, C
    return R, tc


def _tile_rows(n, pref, mult=8):
    t = (pref // mult) * mult
    while t >= mult:
        if n % t == 0:
            return t
        t -= mult
    return n


ANY = pl.BlockSpec(memory_space=pl.ANY)


def _place():
    x, y, c = lax.axis_index("x"), lax.axis_index("y"), lax.axis_index("c")
    chips = [(1 - x, y), (x, 1 - y), (1 - x, 1 - y)]
    chip_idx = [2 * cx + cy for cx, cy in chips]
    return x, y, c, 2 * x + y, chips, chip_idx


def _rcopy(src, dst, ssem, rsem, dev):
    return pltpu.make_async_remote_copy(src_ref=src, dst_ref=dst, send_sem=ssem, recv_sem=rsem,
                                        device_id=dev, device_id_type=MESH)


HBM = pl.BlockSpec(memory_space=pltpu.HBM)
SEM = pl.BlockSpec(memory_space=pltpu.SEMAPHORE)
EFFECT = pltpu.SideEffectType.DATAFLOW_SIDE_EFFECTING


def _split_copies(kind, srcs, lands, ssem, rsem):
    _, _, c, me, chips, chip_idx = _place()
    cps = []
    for i in range(len(srcs)):
        for k in range(3):
            if kind == "gather":
                rows = srcs[i].shape[0]
                if rows == lands[i].shape[1]:
                    src, dst = srcs[i], lands[i].at[me]
                else:
                    src, dst = srcs[i], lands[i].at[me, pl.ds(pl.multiple_of(c * rows, 16), rows)]
            else:
                src, dst = srcs[i].at[chip_idx[k]], lands[i].at[k]
            cps.append(_rcopy(src, dst, ssem.at[3 * i + k], rsem.at[3 * i + k], (*chips[k], c)))
    return cps


def _exchange_start(name, kind, srcs, lands, groups):
    n, ng = len(srcs), len(groups)

    def body(*refs):
        src_refs, land_refs = refs[:n], refs[n:2 * n]
        sems = refs[2 * n:2 * n + 2 * ng]
        token = refs[-1]
        for gi, grp in enumerate(groups):
            cps = _split_copies(kind, [src_refs[i] for i in grp], [land_refs[i] for i in grp], sems[2 * gi], sems[2 * gi + 1])
            for cp in cps:
                cp.start()
        token[...] = jnp.zeros_like(token)

    arrays = list(srcs) + list(lands)
    out_shape = []
    for grp in groups:
        out_shape += [pltpu.SemaphoreType.DMA((3 * len(grp),)), pltpu.SemaphoreType.DMA((3 * len(grp),))]
    out_shape += [pltpu.HBM(a.shape, a.dtype) for a in arrays] + [jax.ShapeDtypeStruct((8, LANE), F32)]
    outs = pl.pallas_call(
        body, name=name, out_shape=out_shape, in_specs=[HBM] * (2 * n),
        out_specs=[SEM] * (2 * ng) + [HBM] * (2 * n) + [pl.BlockSpec(memory_space=pltpu.VMEM)],
        input_output_aliases={i: 2 * ng + i for i in range(2 * n)},
        compiler_params=pltpu.CompilerParams(has_side_effects=EFFECT),
    )(*[pltpu.with_memory_space_constraint(a, pltpu.HBM) for a in arrays])
    sems = [(outs[2 * gi], outs[2 * gi + 1]) for gi in range(ng)]
    thru = outs[2 * ng:2 * ng + 2 * n]
    return sems, thru[:n], thru[n:], outs[-1]


def _exchange_wait(name, kind, srcs, lands, sems, after):
    n = len(srcs)

    def body(*refs):
        cps = _split_copies(kind, refs[:n], refs[n:2 * n], refs[2 * n], refs[2 * n + 1])
        for cp in cps:
            cp.wait_send()
            cp.wait_recv()

    arrays = list(srcs) + list(lands)
    outs = pl.pallas_call(
        body, name=name, out_shape=[pltpu.HBM(a.shape, a.dtype) for a in arrays],
        in_specs=[HBM] * (2 * n) + [SEM, SEM, ANY], out_specs=[HBM] * (2 * n),
        input_output_aliases={i: i for i in range(2 * n)},
        compiler_params=pltpu.CompilerParams(has_side_effects=EFFECT),
    )(*arrays, sems[0], sems[1], after)
    return outs[:n], outs[n:]


EXCHANGE_CHUNK_BYTES = 3 * 1024 * 1024


def _half_geometry(R, C, axis):
    Rp, Cp = (R // 2, C) if axis == 0 else (R, C // 2)
    tr = _tile_rows(Rp, max(16, EXCHANGE_CHUNK_BYTES // (2 * Cp)), 16)
    return Rp, Cp, tr, Rp // tr


def _pair_sum(name, g, axis):
    G, R, C = g.shape
    Rp, Cp, tr, nb = _half_geometry(R, C, axis)
    steps = G * nb

    def half_block(t, h):
        s, b = t // nb, t % nb
        return (s, h * nb + b, 0) if axis == 0 else (s, b, h)

    def body(c_ref, keep_ref, give_ref, out_ref, land, ssem, rsem, credit):
        x, y, c = lax.axis_index("x"), lax.axis_index("y"), lax.axis_index("c")
        sib = (x, y, 1 - c)
        t = pl.program_id(0)

        def copy(chunk):
            return _rcopy(give_ref.at[0], land.at[chunk % 2], ssem.at[chunk % 2], rsem.at[chunk % 2], sib)

        @pl.when((t >= 2) & (t < steps))
        def _():
            pl.semaphore_wait(credit, 1)

        @pl.when(t < steps)
        def _():
            copy(t).start()

        @pl.when(t >= 1)
        def _():
            copy(t - 1).wait_recv()
            out_ref[...] = (keep_ref[...].astype(F32) + land[(t - 1) % 2].astype(F32)).astype(BF)

            @pl.when(t + 1 < steps)
            def _():
                pl.semaphore_signal(credit, 1, device_id=sib, device_id_type=MESH)

        @pl.when(t < steps)
        def _():
            copy(t).wait_send()

    blk = (None, tr, Cp)
    grid_spec = pltpu.PrefetchScalarGridSpec(
        num_scalar_prefetch=1, grid=(steps + 1,),
        in_specs=[pl.BlockSpec(blk, lambda t, c_ref: half_block(jnp.maximum(t - 1, 0), c_ref[0])),
                  pl.BlockSpec((1, tr, Cp), lambda t, c_ref: half_block(jnp.minimum(t, steps - 1), 1 - c_ref[0]))],
        out_specs=pl.BlockSpec(blk, lambda t, c_ref: (jnp.maximum(t - 1, 0) // nb, jnp.maximum(t - 1, 0) % nb, 0)),
        scratch_shapes=[pltpu.VMEM((2, tr, Cp), BF), pltpu.SemaphoreType.DMA((2,)), pltpu.SemaphoreType.DMA((2,)),
                        pltpu.SemaphoreType.REGULAR])
    c_arr = lax.axis_index("c").astype(jnp.int32).reshape(1)
    return pl.pallas_call(
        body, name=name, grid_spec=grid_spec, out_shape=jax.ShapeDtypeStruct((G, Rp, Cp), BF),
        compiler_params=_params("arbitrary"))(c_arr, g, g)


def _chip_total_join(name, h, landed, axis):
    G, Rp, Cp = h.shape
    R, C = (2 * Rp, Cp) if axis == 0 else (Rp, 2 * Cp)
    tr = _tile_rows(Rp, max(16, EXCHANGE_CHUNK_BYTES // (4 * Cp)), 16)
    nb = Rp // tr

    def body(me_ref, own_ref, l0_ref, l1_ref, l2_ref, full, stage, ssem, rsem, lsem):
        x, y, c = lax.axis_index("x"), lax.axis_index("y"), lax.axis_index("c")
        sib = (x, y, 1 - c)
        b = pl.program_id(0)

        def place(half, r0, rows):
            if axis == 0:
                return full.at[pl.ds(pl.multiple_of(half * Rp + r0, 8), rows), :]
            return full.at[pl.ds(pl.multiple_of(r0, 8), rows), pl.ds(pl.multiple_of(half * Cp, LANE), Cp)]

        def copies(step):
            s = step % 2
            mine = place(c, step * tr, tr)
            return pltpu.make_async_copy(stage.at[s], mine, lsem.at[s]), _rcopy(stage.at[s], mine, ssem.at[s], rsem, sib)

        @pl.when(b >= 2)
        def _():
            loc, rem = copies(b - 2)
            loc.wait()
            rem.wait_send()

        acc = own_ref[...].astype(F32)
        for r in (l0_ref, l1_ref, l2_ref):
            acc = acc + r[...].astype(F32)
        stage[b % 2] = acc
        loc, rem = copies(b)
        loc.start()
        rem.start()

        @pl.when(b == nb - 1)
        def _():
            for step in range(max(0, nb - 2), nb):
                loc, rem = copies(step)
                loc.wait()
                rem.wait_send()
            theirs = place(1 - c, 0, Rp)
            _rcopy(theirs, theirs, ssem.at[0], rsem, sib).wait_recv()

    blk = (None, tr, Cp)
    grid_spec = pltpu.PrefetchScalarGridSpec(
        num_scalar_prefetch=1, grid=(nb,),
        in_specs=[pl.BlockSpec(blk, lambda b, me_ref: (me_ref[0], b, 0))]
        + [pl.BlockSpec(blk, functools.partial(lambda b, me_ref, k: (k, b, 0), k=k)) for k in range(3)],
        out_specs=ANY,
        scratch_shapes=[pltpu.VMEM((2, tr, Cp), F32), pltpu.SemaphoreType.DMA((2,)), pltpu.SemaphoreType.DMA,
                        pltpu.SemaphoreType.DMA((2,))])
    me = (2 * lax.axis_index("x") + lax.axis_index("y")).astype(jnp.int32).reshape(1)
    return pl.pallas_call(
        body, name=name, grid_spec=grid_spec, out_shape=jax.ShapeDtypeStruct((R, C), F32),
        compiler_params=_params("arbitrary"))(me, h, landed, landed, landed)


def _pair_share(name, land):
    G, R, C = land.shape
    Rh = R // 2
    tr = _tile_rows(Rh, max(16, EXCHANGE_CHUNK_BYTES // (2 * C)), 16)
    chunks = [(k, b) for k in range(3) for b in range(Rh // tr)]

    def body(src, dst, buf, lsem, ssem, rsem):
        x, y, c, _, _, chip_idx = _place()
        sib = (x, y, 1 - c)

        def region(ref, k, half, r0, rows):
            return ref.at[chip_idx[k], pl.ds(pl.multiple_of(half * Rh + r0, 16), rows)]

        def load(t):
            k, b = chunks[t]
            return pltpu.make_async_copy(region(src, k, c, b * tr, tr), buf.at[t % 2], lsem.at[t % 2])

        def send(t):
            k, b = chunks[t]
            return _rcopy(buf.at[t % 2], region(dst, k, c, b * tr, tr), ssem.at[t % 2], rsem.at[k], sib)

        load(0).start()
        for t in range(len(chunks)):
            load(t).wait()
            if t + 1 < len(chunks):
                if t >= 1:
                    send(t - 1).wait_send()
                load(t + 1).start()
            send(t).start()
        for t in range(max(0, len(chunks) - 2), len(chunks)):
            send(t).wait_send()
        for k in range(3):
            theirs = region(dst, k, 1 - c, 0, Rh)
            _rcopy(theirs, theirs, ssem.at[0], rsem.at[k], sib).wait_recv()

    return pl.pallas_call(
        body, name=name, in_specs=[ANY], out_specs=ANY, out_shape=jax.ShapeDtypeStruct(land.shape, land.dtype),
        input_output_aliases={0: 0},
        scratch_shapes=[pltpu.VMEM((2, tr, C), land.dtype), pltpu.SemaphoreType.DMA((2,)), pltpu.SemaphoreType.DMA((2,)),
                        pltpu.SemaphoreType.DMA((3,))],
    )(land)


def _allreduce_small(v):
    R, K = v.shape
    ndev = 8

    def body(v_ref, o_ref, land, ssem, rsem):
        x, y, c = lax.axis_index("x"), lax.axis_index("y"), lax.axis_index("c")
        me = 4 * x + 2 * y + c
        land[me] = v_ref[...]
        cps = []
        for r in range(1, ndev):
            fx, fy, fc = (r >> 2) & 1, (r >> 1) & 1, r & 1
            peer = (x ^ fx, y ^ fy, c ^ fc)
            cp = _rcopy(v_ref, land.at[me], ssem.at[r - 1], rsem.at[r - 1], peer)
            cp.start()
            cps.append((cp, 4 * peer[0] + 2 * peer[1] + peer[2], r))
        for cp, src, r in cps:
            cp.wait_send()
            _rcopy(v_ref, land.at[src], ssem.at[r - 1], rsem.at[r - 1], (x, y, c)).wait_recv()
        acc = land[0]
        for d in range(1, ndev):
            acc = acc + land[d]
        o_ref[...] = acc

    vm = pl.BlockSpec(memory_space=pltpu.VMEM)
    return pl.pallas_call(
        body, name="allreduce_small", in_specs=[vm], out_specs=vm, out_shape=jax.ShapeDtypeStruct((R, K), F32),
        scratch_shapes=[pltpu.VMEM((ndev, R, K), F32), pltpu.SemaphoreType.DMA((ndev - 1,)), pltpu.SemaphoreType.DMA((ndev - 1,))],
    )(v)


IN_SPLITS = (Q_RANK, KV_RANK, QK_ROPE, DIL_HEADS * HEAD, DIL_HEADS * HEAD, DIL_HEADS * HEAD, D_MODEL, D_MODEL)
IN_OFF = tuple(int(v) for v in np.cumsum((0,) + IN_SPLITS))


def _unshard_cols(g):
    G, K, Ns = g.shape
    return g.transpose(1, 0, 2).reshape(K, G * Ns)


def _shard_cols(w):
    K, N = w.shape
    return w.reshape(K, N_CHIPS, N // N_CHIPS).transpose(1, 0, 2)


def _rope_pad(w):
    half = QK_ROPE // 2
    z = jnp.zeros(w.shape[:-1] + (half,), w.dtype)
    return jnp.concatenate([w[..., :half], z, w[..., half:], z], axis=-1)


def _rope_unpad(w):
    half = QK_ROPE // 2
    return jnp.concatenate([w[..., :half], w[..., 2 * half:3 * half]], axis=-1)


def _split_w_in(w_in_g):
    G, K, Ns = w_in_g.shape

    def cols(lo, hi):
        pieces = [w_in_g[k][:, max(lo, k * Ns) - k * Ns:min(hi, (k + 1) * Ns) - k * Ns]
                  for k in range(G) if max(lo, k * Ns) < min(hi, (k + 1) * Ns)]
        return pieces[0] if len(pieces) == 1 else jnp.concatenate(pieces, axis=1)

    p = [(IN_OFF[i], IN_OFF[i + 1]) for i in range(8)]
    w_lat = jnp.concatenate([cols(*p[0]), cols(*p[1]), _rope_pad(cols(*p[2])),
                             jnp.zeros((K, LAT_W - _KPE.stop), w_in_g.dtype)], axis=1)
    w_dil = [jnp.concatenate([cols(p[3 + t][0] + g * DIL_O, p[3 + t][0] + (g + 1) * DIL_O) for t in range(3)], axis=1)
             for g in range(DIL_GROUPS)]
    w_gate = cols(p[6][0], p[7][1])
    return w_lat, w_dil, w_gate


def _merge_dw_in(dw_lat, dw_dil, dw_gate):
    parts = [dw_lat[:, _CQ], dw_lat[:, _CKV], _rope_unpad(dw_lat[:, _KPE])]
    for t in range(3):
        parts += [dw_dil[g][:, t * DIL_O:(t + 1) * DIL_O] for g in range(DIL_GROUPS)]
    parts.append(dw_gate)
    width = sum(p.shape[1] for p in parts) // N_CHIPS
    shards = []
    for k in range(N_CHIPS):
        pieces, at = [], 0
        for p in parts:
            lo, hi = max(k * width, at), min((k + 1) * width, at + p.shape[1])
            if lo < hi:
                pieces.append(p[:, lo - at:hi - at])
            at += p.shape[1]
        shards.append(jnp.concatenate(pieces, axis=1))
    return jnp.stack(shards)


def _split_w_uq(w_uq_g):
    w = _unshard_cols(w_uq_g)
    K = w.shape[0]
    w = w.reshape(K, MLA_HEADS, QK_NOPE + QK_ROPE)
    return w[:, :, :QK_NOPE].reshape(K, MLA_HEADS * HEAD), _rope_pad(w[:, :, QK_NOPE:]).reshape(K, MLA_HEADS * HEAD)


def _merge_dw_uq(dw_n, dw_p):
    K = dw_n.shape[0]
    w = jnp.concatenate([dw_n.reshape(K, MLA_HEADS, HEAD), _rope_unpad(dw_p.reshape(K, MLA_HEADS, HEAD))], axis=-1)
    return _shard_cols(w.reshape(K, MLA_HEADS * (QK_NOPE + QK_ROPE)))


def _split_w_ukv(w_ukv_g):
    w = _unshard_cols(w_ukv_g)
    K = w.shape[0]
    w = w.reshape(K, MLA_HEADS, 2 * HEAD)
    return w[:, :, :HEAD].reshape(K, MLA_HEADS * HEAD), w[:, :, HEAD:].reshape(K, MLA_HEADS * HEAD)


def _merge_dw_ukv(dw_k, dw_v):
    K = dw_k.shape[0]
    w = jnp.concatenate([dw_k.reshape(K, MLA_HEADS, HEAD), dw_v.reshape(K, MLA_HEADS, HEAD)], axis=-1)
    return _shard_cols(w.reshape(K, MLA_HEADS * 2 * HEAD))


GATHER_GROUPS = (("w_in",), ("w_uq", "w_ukv", "w_o_mla", "w_o_dil", "w_out"), ("w_up", "w_down", "conv_w"))
SHARED_FETCH = ("w_in",)
REDUCE_GROUPS = (("w_down", "w_up"), ("w_out", "w_o_mla", "w_o_dil"), ("w_uq", "w_ukv", "w_in"))


def _local_step(x, tgt, W, fetch, emit):
    S, D = x.shape
    cos, sin_s = _rope_tables(S)
    w_lat, w_dil, w_gate = _split_w_in(fetch(0, x)["w_in"])

    h = _rmsnorm_fwd("attn_norm", x, W["attn_norm_g"])
    lat = _mm_nn("proj_lat", h, w_lat)
    qkv = [_mm_nn(f"proj_dil{g}", h, w_dil[g], o_dtype=BF) for g in range(DIL_GROUPS)]
    gpre = _mm_nn("proj_gate", h, w_gate, o_dtype=BF)
    WB = fetch(1, gpre)
    w_uqn, w_uqp = _split_w_uq(WB["w_uq"])
    w_k, w_v = _split_w_ukv(WB["w_ukv"])
    w_o_mla, w_o_dil = WB["w_o_mla"], WB["w_o_dil"]
    w_out = WB["w_out"].reshape(D, D)
    qn_, kvn, kpe = _mla_prep(lat, W["q_norm_g"], W["kv_norm_g"], cos, sin_s)
    q_nope = _mm_nn("q_nope", qn_, w_uqn, o_dtype=BF)
    q_pe = _rope("q_rope", _mm_nn("q_pe", qn_, w_uqp), cos, sin_s, False)
    k_nope = _mm_nn("k_nope", kvn, w_k, o_dtype=BF)
    v_mla = _mm_nn("v_mla", kvn, w_v, o_dtype=BF)
    attn_a, lse_a = _mla_fwd(q_nope, q_pe, k_nope, kpe, v_mla)
    dil = [_dil_fwd(qkv[g], g) for g in range(DIL_GROUPS)]
    attn_b, lse_b = _dil_combine([o for o, _ in dil], [l for _, l in dil])
    o_a = _mm_nn("o_mla", attn_a, w_o_mla, o_dtype=BF)
    o_b = _mm_nn("o_dil", attn_b, w_o_dil, o_dtype=BF)
    merge = _merge_fwd(gpre, W["b_gate"], o_a, o_b)
    x1 = _mm_nn("out_proj", merge, w_out, add=x)
    WC = fetch(2, merge)
    w_up = WC["w_up"]
    G4, _, C = w_up.shape
    w_down = WC["w_down"].reshape(G4 // 2, C, D)
    conv_w = WC["conv_w"]
    conv_b = W["conv_b"].reshape(G4, 1, C)
    h2 = _rmsnorm_fwd("ffn_norm", x1, W["ffn_norm_g"])
    u_pre = _up_fwd(h2, w_up)
    act, u = _ffn_act(u_pre, conv_w, conv_b)
    x2 = _down_fwd(act, w_down, x1)
    dx2, d_final_g, loss8 = _final_loss(x2, tgt, W["final_norm_g"])

    d_act = _down_dgrad(dx2, w_down)
    dw_down = _down_wgrad(act, dx2)
    du_pre, d_conv_w, d_conv_b = _ffn_act_conv_bwd(u, d_act, u_pre, conv_w)
    dh2 = _up_dgrad(du_pre, w_up)
    dw_up = _up_wgrad(h2, du_pre)
    zero = emit(0, {"w_down": dw_down.reshape(N_CHIPS, (G4 // 2) * C // N_CHIPS, D), "w_up": dw_up})
    dx1, d_ffn_g = _rmsnorm_bwd("ffn_norm_bwd", dh2, x1, W["ffn_norm_g"] + zero, dx2)
    d_merge = _mm_nt("out_proj_dgrad", dx1, w_out, o_dtype=BF)
    dw_out = _mm_tn("out_proj_wgrad", merge, dx1)
    d_oa, d_ob, d_gpre, d_b_gate = _merge_bwd(d_merge, gpre, W["b_gate"], o_a, o_b)
    d_attn_a = _mm_nt("o_mla_dgrad", d_oa, w_o_mla, o_dtype=BF)
    dw_o_mla = _mm_tn("o_mla_wgrad", attn_a, d_oa, shards=N_CHIPS)
    d_attn_b = _mm_nt("o_dil_dgrad", d_ob, w_o_dil, o_dtype=BF)
    dw_o_dil = _mm_tn("o_dil_wgrad", attn_b, d_ob, shards=N_CHIPS)
    zero = emit(1, {"w_out": dw_out.reshape(N_CHIPS, D // N_CHIPS, D), "w_o_mla": dw_o_mla, "w_o_dil": dw_o_dil})
    q_norm_g = W["q_norm_g"] + zero
    delta_b = _dil_delta(d_attn_b, attn_b)
    d_qkv = [_dil_bwd(qkv[g], d_attn_b, lse_b, delta_b, g) for g in range(DIL_GROUPS)]
    dq_nope, dq_pe_rot, dk_nope, dv_mla, dkpe_rot = _mla_bwd(q_nope, q_pe, k_nope, kpe, v_mla, d_attn_a, attn_a, lse_a)
    dq_pe = _rope("q_rope_bwd", dq_pe_rot, cos, sin_s, True)
    d_qn = _mm_nt_sum("q_dgrad", [dq_nope, dq_pe], [w_uqn, w_uqp])
    d_kvn = _mm_nt_sum("kv_dgrad", [dk_nope, dv_mla], [w_k, w_v])
    dw_uq = _merge_dw_uq(_mm_tn("q_nope_wgrad", qn_, dq_nope), _mm_tn("q_pe_wgrad", qn_, dq_pe))
    dw_ukv = _merge_dw_ukv(_mm_tn("k_nope_wgrad", kvn, dk_nope), _mm_tn("v_wgrad", kvn, dv_mla))
    d_lat, d_q_g, d_kv_g = _mla_prep_bwd(lat, q_norm_g, W["kv_norm_g"], cos, sin_s, d_qn, d_kvn, dkpe_rot)
    dw_in = _merge_dw_in(_mm_tn("proj_lat_wgrad", h, d_lat),
                         [_mm_tn(f"proj_dil{g}_wgrad", h, d_qkv[g]) for g in range(DIL_GROUPS)],
                         _mm_tn("proj_gate_wgrad", h, d_gpre))
    zero = emit(2, {"w_uq": dw_uq, "w_ukv": dw_ukv, "w_in": dw_in})
    dh = _mm_nt_sum("proj_lat_dil_dgrad", [d_lat] + d_qkv, [w_lat + zero.astype(BF)] + w_dil)
    dh = _mm_nt("proj_gate_dgrad", d_gpre, w_gate, add=dh)
    grad_x, d_attn_g = _rmsnorm_bwd("attn_norm_bwd", dh, x, W["attn_norm_g"], dx1)

    small = {"attn_norm_g": d_attn_g, "b_gate": d_b_gate, "q_norm_g": d_q_g, "kv_norm_g": d_kv_g,
             "ffn_norm_g": d_ffn_g, "conv_w": d_conv_w, "conv_b": d_conv_b.reshape(1, G4 * C),
             "final_norm_g": d_final_g}
    return loss8[0, 0], grad_x, small


BIG = ("w_in", "w_uq", "w_ukv", "w_o_mla", "w_o_dil", "w_out", "w_up", "w_down")
SMALL = ("attn_norm_g", "b_gate", "q_norm_g", "kv_norm_g", "ffn_norm_g", "conv_w", "conv_b", "final_norm_g")
WEIGHTS = ("attn_norm_g", "w_in", "b_gate", "q_norm_g", "w_uq", "kv_norm_g", "w_ukv", "w_o_mla", "w_o_dil",
           "w_out", "ffn_norm_g", "w_up", "conv_w", "conv_b", "w_down", "final_norm_g")
SMALL_ROWS = 8
COLUMN_MAJOR = ("w_in", "w_up")
HALF_AXIS = {"w_down": 1}


def _gather_start(shards):
    chip = 2 * lax.axis_index("x") + lax.axis_index("y")
    c = lax.axis_index("c")

    def prepare(names, zero):
        srcs, lands = [], []
        for n in names:
            if n in COLUMN_MAJOR:
                s = lax.optimization_barrier((shards[n].T + zero).astype(BF).T)
            else:
                s = shards[n] + zero
                s = s if n == "conv_w" else s.astype(BF)
            lands.append(lax.dynamic_update_slice(lax.empty((N_CHIPS,) + s.shape, s.dtype), s[None], (chip, 0, 0)))
            if n in SHARED_FETCH:
                s = lax.dynamic_slice_in_dim(s, c * (s.shape[0] // 2), s.shape[0] // 2, 0)
            srcs.append(s)
        return srcs, lands

    n0 = len(GATHER_GROUPS[0])
    srcs0, lands0 = prepare(GATHER_GROUPS[0], 0.0)
    sems0, srcs0, lands0, token = _exchange_start("gather_start0", "gather", srcs0, lands0, [list(range(n0))])
    srcs, lands = prepare([n for grp in GATHER_GROUPS[1:] for n in grp], token[0, 0])
    groups, at = [], 0
    for grp in GATHER_GROUPS[1:]:
        groups.append(list(range(at, at + len(grp))))
        at += len(grp)
    sems, srcs, lands, token1 = _exchange_start("gather_start1", "gather", srcs, lands, groups)

    def fetch(i, after):
        if i == 0:
            _, got = _exchange_wait("gather_wait0", "gather", srcs0, lands0, sems0[0], token1)
        else:
            idx = groups[i - 1]
            _, got = _exchange_wait(f"gather_wait{i}", "gather", [srcs[j] for j in idx], [lands[j] for j in idx],
                                    sems[i - 1], after)
        return {n: _pair_share(f"pair_share_{n}", g) if n in SHARED_FETCH else g for n, g in zip(GATHER_GROUPS[i], got)}

    return fetch, token[0, 0]


def _reduce_start(i, grads):
    names = REDUCE_GROUPS[i]
    hs = [_pair_sum(f"pair_sum_{n}", grads[n], HALF_AXIS.get(n, 0)) for n in names]
    lands = [lax.empty((3,) + h.shape[1:], h.dtype) for h in hs]
    sems, hs, lands, token = _exchange_start(f"reduce_start{i}", "scatter", hs, lands, [list(range(len(names)))])
    return (sems[0], hs, lands), token[0, 0]


def _reduce_finish(i, pending, after):
    sems, hs, lands = pending
    hs, lands = _exchange_wait(f"reduce_wait{i}", "scatter", hs, lands, sems, after)
    out = {}
    for n, h, landed in zip(REDUCE_GROUPS[i], hs, lands):
        out[n] = _chip_total_join(f"chip_total_{n}", h, landed, HALF_AXIS.get(n, 0))
    return out


def _reduce_small(small):
    names = tuple(small)
    flat = [small[n].reshape(-1) for n in names]
    sizes = [f.shape[0] for f in flat]
    total = sum(sizes)
    width = -(-total // (SMALL_ROWS * LANE)) * LANE
    packed = jnp.concatenate(flat + [jnp.zeros((SMALL_ROWS * width - total,), F32)]).reshape(SMALL_ROWS, width)
    red = _allreduce_small(packed).reshape(-1)
    out, off = {}, 0
    for n, s in zip(names, sizes):
        out[n] = red[off:off + s]
        off += s
    return out


def kernel(x, attn_norm_g, w_in, b_gate, q_norm_g, w_uq, kv_norm_g, w_ukv, w_o_mla, w_o_dil, w_out, ffn_norm_g, w_up, conv_w, conv_b, w_down, final_norm_g, loss_target, m_attn_norm_g, m_w_in, m_b_gate, m_q_norm_g, m_w_uq, m_kv_norm_g, m_w_ukv, m_w_o_mla, m_w_o_dil, m_w_out, m_ffn_norm_g, m_w_up, m_conv_w, m_conv_b, m_w_down, m_final_norm_g, v_attn_norm_g, v_w_in, v_b_gate, v_q_norm_g, v_w_uq, v_kv_norm_g, v_w_ukv, v_w_o_mla, v_w_o_dil, v_w_out, v_ffn_norm_g, v_w_up, v_conv_w, v_conv_b, v_w_down, v_final_norm_g):
    given = dict(attn_norm_g=attn_norm_g, w_in=w_in, b_gate=b_gate, q_norm_g=q_norm_g, w_uq=w_uq, kv_norm_g=kv_norm_g,
                 w_ukv=w_ukv, w_o_mla=w_o_mla, w_o_dil=w_o_dil, w_out=w_out, ffn_norm_g=ffn_norm_g, w_up=w_up,
                 conv_w=conv_w, conv_b=conv_b, w_down=w_down, final_norm_g=final_norm_g)
    moments_m = dict(attn_norm_g=m_attn_norm_g, w_in=m_w_in, b_gate=m_b_gate, q_norm_g=m_q_norm_g, w_uq=m_w_uq,
                     kv_norm_g=m_kv_norm_g, w_ukv=m_w_ukv, w_o_mla=m_w_o_mla, w_o_dil=m_w_o_dil, w_out=m_w_out,
                     ffn_norm_g=m_ffn_norm_g, w_up=m_w_up, conv_w=m_conv_w, conv_b=m_conv_b, w_down=m_w_down,
                     final_norm_g=m_final_norm_g)
    moments_v = dict(attn_norm_g=v_attn_norm_g, w_in=v_w_in, b_gate=v_b_gate, q_norm_g=v_q_norm_g, w_uq=v_w_uq,
                     kv_norm_g=v_kv_norm_g, w_ukv=v_w_ukv, w_o_mla=v_w_o_mla, w_o_dil=v_w_o_dil, w_out=v_w_out,
                     ffn_norm_g=v_ffn_norm_g, w_up=v_w_up, conv_w=v_conv_w, conv_b=v_conv_b, w_down=v_w_down,
                     final_norm_g=v_final_norm_g)

    fetch, zero = _gather_start({n: given[n][0] for n in BIG + ("conv_w",)})
    W = {n: given[n] for n in ("b_gate", "q_norm_g", "kv_norm_g", "ffn_norm_g", "conv_b")}
    W["attn_norm_g"] = given["attn_norm_g"] + zero
    W["final_norm_g"] = given["final_norm_g"].reshape(1, -1)

    pending = {}

    def emit(i, grads):
        pending[i], token = _reduce_start(i, grads)
        return token

    loss_part, grad_x, small = _local_step(x[0], loss_target[0], W, fetch, emit)
    small["loss"] = loss_part
    grads, delta, new_m, new_v = {}, {}, {}, {}

    def adamw(n, g):
        shp = given[n].shape
        two_d = (-1, shp[-1]) if len(shp) > 1 else (1, -1)
        view = (lambda a: a.reshape(two_d).T) if n in COLUMN_MAJOR else (lambda a: a.reshape(two_d))
        back = (lambda a: a.T.reshape(shp)) if n in COLUMN_MAJOR else (lambda a: a.reshape(shp))
        go, d, nm, nv = _adamw(f"adamw_{n}", view(given[n]), view(g), view(moments_m[n]), view(moments_v[n]))
        grads[n], delta[n], new_m[n], new_v[n] = back(go), back(d), back(nm), back(nv)

    after = grad_x
    for i in range(len(REDUCE_GROUPS)):
        for n, g in _reduce_finish(i, pending[i], after).items():
            adamw(n, g)
        after = delta[REDUCE_GROUPS[i][-1]]
    g_small = _reduce_small(small)
    loss = g_small["loss"][0]
    chip = 2 * lax.axis_index("x") + lax.axis_index("y")
    for n in SMALL:
        if n == "conv_w":
            full = g_small[n].reshape(N_CHIPS, 3, -1)
            adamw(n, lax.dynamic_index_in_dim(full, chip, 0, keepdims=True))
        else:
            adamw(n, g_small[n])

    return (loss, grad_x[None], *[grads[n] for n in WEIGHTS], *[delta[n] for n in WEIGHTS],
            *[new_m[n] for n in WEIGHTS], *[new_v[n] for n in WEIGHTS])
```

```python
import functools
import math

import numpy as np
import jax
import jax.numpy as jnp
from jax import lax
from jax.experimental import pallas as pl
from jax.experimental.pallas import tpu as pltpu

F32 = jnp.float32
BF = jnp.bfloat16
MESH = pl.DeviceIdType.MESH

D_MODEL = 2048
MLA_HEADS = 8
QK_NOPE = 128
QK_ROPE = 64
Q_RANK = 512
KV_RANK = 256
ROPE_THETA = 10000.0
DIL_PATTERNS = ((128, 1), (512, 4), (2048, 16))
DIL_GROUPS = 3
DIL_HPG = 4
DIL_HEADS = 12
HEAD = 128
DIL_BLOCK = 128
ALIBI_MAX_BIAS = 8.0
NORM_EPS = 1e-6
N_CHIPS = 4
ADAM_LR = 0.001
ADAM_B1 = 0.9
ADAM_B2 = 0.999
ADAM_EPS = 1e-08
ADAM_WD = 0.01
ADAM_STEP = 10

LANE = 128
VMEM_LIMIT = 56 * 1024 * 1024
MLA_SCALE = (QK_NOPE + QK_ROPE) ** -0.5
DIL_SCALE = HEAD ** -0.5


def _params(*sem):
    return pltpu.CompilerParams(dimension_semantics=sem, vmem_limit_bytes=VMEM_LIMIT)


def _tile(n, pref):
    t = (pref // LANE) * LANE
    while t >= LANE:
        if n % t == 0:
            return t
        t -= LANE
    return n


NN = (((1,), (0,)), ((), ()))
NT = (((1,), (1,)), ((), ()))
TN = (((0,), (0,)), ((), ()))


def _mm_call(name, a, b, add, *, grid, a_spec, b_spec, add_spec, o_spec, o_shape, o_dtype, acc_shape, dims, nk):
    nax = len(grid)

    def body(*refs):
        if add is None:
            a_ref, b_ref, o_ref = refs[:3]
            c_ref = None
            scr = refs[3:]
        else:
            a_ref, b_ref, c_ref, o_ref = refs[:4]
            scr = refs[4:]
        prod = lax.dot_general(a_ref[...].astype(BF), b_ref[...].astype(BF), dims, preferred_element_type=F32)
        if nk == 1:
            if c_ref is not None:
                prod = prod + c_ref[...]
            o_ref[...] = prod.astype(o_ref.dtype)
        else:
            acc = scr[0]
            k = pl.program_id(nax - 1)

            @pl.when(k == 0)
            def _():
                if c_ref is not None:
                    acc[...] = prod + c_ref[...]
                else:
                    acc[...] = prod

            @pl.when(k > 0)
            def _():
                acc[...] += prod

            @pl.when(k == nk - 1)
            def _():
                o_ref[...] = acc[...].astype(o_ref.dtype)

    ins = [a, b] + ([] if add is None else [add])
    specs = [a_spec, b_spec] + ([] if add is None else [add_spec])
    sem = ("parallel",) * (nax - 1) + ("arbitrary",)
    return pl.pallas_call(
        body, name=name, grid=grid, in_specs=specs, out_specs=o_spec,
        out_shape=jax.ShapeDtypeStruct(o_shape, o_dtype),
        scratch_shapes=[] if nk == 1 else [pltpu.VMEM(acc_shape, F32)],
        compiler_params=_params(*sem),
    )(*ins)


def _mm_nn(name, a, b, *, add=None, o_dtype=F32):
    M, K = a.shape
    sharded = b.ndim == 3
    Ns = b.shape[-1]
    N = Ns * (b.shape[0] if sharded else 1)
    tm, tn, tk = _tile(M, 1024), _tile(Ns, 1024), _tile(K, 2048)
    per = Ns // tn
    nk = K // tk
    if sharded:
        b_spec = pl.BlockSpec((None, tk, tn), lambda i, j, k: (j // per, k, j % per))
    else:
        b_spec = pl.BlockSpec((tk, tn), lambda i, j, k: (k, j))
    return _mm_call(
        name, a, b, add, grid=(M // tm, N // tn, nk),
        a_spec=pl.BlockSpec((tm, tk), lambda i, j, k: (i, k)), b_spec=b_spec,
        add_spec=pl.BlockSpec((tm, tn), lambda i, j, k: (i, j)),
        o_spec=pl.BlockSpec((tm, tn), lambda i, j, k: (i, j)),
        o_shape=(M, N), o_dtype=o_dtype, acc_shape=(tm, tn), dims=NN, nk=nk)


def _mm_nt(name, a, b, *, add=None, o_dtype=F32):
    M, K = a.shape
    sharded = b.ndim == 3
    N, Ks = b.shape[-2], b.shape[-1]
    tm, tn, tk = _tile(M, 1024), _tile(N, 1024), _tile(Ks, 2048)
    per = Ks // tk
    nk = K // tk
    if sharded:
        b_spec = pl.BlockSpec((None, tn, tk), lambda i, j, k: (k // per, j, k % per))
    else:
        b_spec = pl.BlockSpec((tn, tk), lambda i, j, k: (j, k))
    return _mm_call(
        name, a, b, add, grid=(M // tm, N // tn, nk),
        a_spec=pl.BlockSpec((tm, tk), lambda i, j, k: (i, k)), b_spec=b_spec,
        add_spec=pl.BlockSpec((tm, tn), lambda i, j, k: (i, j)),
        o_spec=pl.BlockSpec((tm, tn), lambda i, j, k: (i, j)),
        o_shape=(M, N), o_dtype=o_dtype, acc_shape=(tm, tn), dims=NT, nk=nk)


def _mm_nt_sum(name, a_s, b_s):
    n = len(a_s)
    M, N = a_s[0].shape[0], b_s[0].shape[0]
    tm, tn = _tile(M, 512), _tile(N, 1024)

    def body(*refs):
        acc = None
        for i in range(n):
            prod = lax.dot_general(refs[i][...].astype(BF), refs[n + i][...].astype(BF), NT, preferred_element_type=F32)
            acc = prod if acc is None else acc + prod
        refs[2 * n][...] = acc

    return pl.pallas_call(
        body, name=name, grid=(N // tn, M // tm),
        in_specs=[pl.BlockSpec((tm, a.shape[1]), lambda j, i: (i, 0)) for a in a_s]
        + [pl.BlockSpec((tn, b.shape[1]), lambda j, i: (j, 0)) for b in b_s],
        out_specs=pl.BlockSpec((tm, tn), lambda j, i: (i, j)),
        out_shape=jax.ShapeDtypeStruct((M, N), F32),
        compiler_params=_params("parallel", "parallel"))(*a_s, *b_s)


def _mm_tn(name, a, b, *, shards=1, o_dtype=BF):
    S, M = a.shape
    N = b.shape[1]
    Ns = N // shards
    tm, tn, tk = _tile(M, 1024), _tile(Ns, 1024), _tile(S, 2048)
    per = Ns // tn
    nk = S // tk
    if shards > 1:
        o_spec = pl.BlockSpec((None, tm, tn), lambda i, j, k: (j // per, i, j % per))
        o_shape = (shards, M, Ns)
    else:
        o_spec = pl.BlockSpec((tm, tn), lambda i, j, k: (i, j))
        o_shape = (M, N)
    return _mm_call(
        name, a, b, None, grid=(M // tm, N // tn, nk),
        a_spec=pl.BlockSpec((tk, tm), lambda i, j, k: (k, i)),
        b_spec=pl.BlockSpec((tk, tn), lambda i, j, k: (k, j)),
        add_spec=None, o_spec=o_spec, o_shape=o_shape, o_dtype=o_dtype, acc_shape=(tm, tn), dims=TN, nk=nk)


def _up_fwd(h2, w_up):
    S, D = h2.shape
    G, _, C = w_up.shape
    tm = _tile(S, 512)
    return _mm_call(
        "up_fwd", h2, w_up, None, grid=(G, S // tm, 1),
        a_spec=pl.BlockSpec((tm, D), lambda g, i, k: (i, 0)),
        b_spec=pl.BlockSpec((None, D, C), lambda g, i, k: (g, 0, 0)),
        add_spec=None, o_spec=pl.BlockSpec((None, tm, C), lambda g, i, k: (g, i, 0)),
        o_shape=(G, S, C), o_dtype=BF, acc_shape=None, dims=NN, nk=1)


def _up_dgrad(du_pre, w_up):
    G, S, C = du_pre.shape
    D = w_up.shape[1]
    tm, tn = _tile(S, 1024), _tile(D, 1024)
    return _mm_call(
        "up_dgrad", du_pre, w_up, None, grid=(S // tm, D // tn, G),
        a_spec=pl.BlockSpec((None, tm, C), lambda i, j, g: (g, i, 0)),
        b_spec=pl.BlockSpec((None, tn, C), lambda i, j, g: (g, j, 0)),
        add_spec=None, o_spec=pl.BlockSpec((tm, tn), lambda i, j, g: (i, j)),
        o_shape=(S, D), o_dtype=F32, acc_shape=(tm, tn), dims=NT, nk=G)


def _up_wgrad(h2, du_pre):
    G, S, C = du_pre.shape
    D = h2.shape[1]
    tm, tk = _tile(D, 512), _tile(S, 2048)
    return _mm_call(
        "up_wgrad", h2, du_pre, None, grid=(G, D // tm, S // tk),
        a_spec=pl.BlockSpec((tk, tm), lambda g, i, k: (k, i)),
        b_spec=pl.BlockSpec((None, tk, C), lambda g, i, k: (g, k, 0)),
        add_spec=None, o_spec=pl.BlockSpec((None, tm, C), lambda g, i, k: (g, i, 0)),
        o_shape=(G, D, C), o_dtype=BF, acc_shape=(tm, C), dims=TN, nk=S // tk)


def _down_fwd(act, w_down, x1):
    G, S, C = act.shape
    D = w_down.shape[2]
    tm, tn = _tile(S, 1024), _tile(D, 1024)
    return _mm_call(
        "down_fwd", act, w_down, x1, grid=(S // tm, D // tn, G),
        a_spec=pl.BlockSpec((None, tm, C), lambda i, j, g: (g, i, 0)),
        b_spec=pl.BlockSpec((None, C, tn), lambda i, j, g: (g, 0, j)),
        add_spec=pl.BlockSpec((tm, tn), lambda i, j, g: (i, j)),
        o_spec=pl.BlockSpec((tm, tn), lambda i, j, g: (i, j)),
        o_shape=(S, D), o_dtype=F32, acc_shape=(tm, tn), dims=NN, nk=G)


def _down_dgrad(dx2, w_down):
    S, D = dx2.shape
    G, C, _ = w_down.shape
    tm = _tile(S, 512)
    return _mm_call(
        "down_dgrad", dx2, w_down, None, grid=(G, S // tm, 1),
        a_spec=pl.BlockSpec((tm, D), lambda g, i, k: (i, 0)),
        b_spec=pl.BlockSpec((None, C, D), lambda g, i, k: (g, 0, 0)),
        add_spec=None, o_spec=pl.BlockSpec((None, tm, C), lambda g, i, k: (g, i, 0)),
        o_shape=(G, S, C), o_dtype=BF, acc_shape=None, dims=NT, nk=1)


def _down_wgrad(act, dx2):
    G, S, C = act.shape
    D = dx2.shape[1]
    tn, tk = _tile(D, 512), _tile(S, 1024)
    return _mm_call(
        "down_wgrad", act, dx2, None, grid=(G, D // tn, S // tk),
        a_spec=pl.BlockSpec((None, tk, C), lambda g, j, k: (g, k, 0)),
        b_spec=pl.BlockSpec((tk, tn), lambda g, j, k: (k, j)),
        add_spec=None, o_spec=pl.BlockSpec((None, C, tn), lambda g, j, k: (g, 0, j)),
        o_shape=(G, C, D), o_dtype=BF, acc_shape=(C, tn), dims=TN, nk=S // tk)


def _row(ts, c):
    return pl.BlockSpec((ts, c), lambda i: (i, 0))


def _bcast(r, c):
    return pl.BlockSpec((r, c), lambda i: (0, 0))


def _accumulate(i, ref, val):
    @pl.when(i == 0)
    def _():
        ref[...] = val

    @pl.when(i > 0)
    def _():
        ref[...] += val


def _rstd(xv):
    return lax.rsqrt(jnp.mean(xv * xv, axis=-1, keepdims=True) + NORM_EPS)


def _rmsnorm_fwd(name, x, g):
    S, D = x.shape
    ts = _tile(S, 512)

    def body(x_ref, g_ref, o_ref):
        xv = x_ref[...]
        o_ref[...] = (xv * _rstd(xv) * g_ref[...]).astype(o_ref.dtype)

    return pl.pallas_call(
        body, name=name, grid=(S // ts,), in_specs=[_row(ts, D), _bcast(1, D)], out_specs=_row(ts, D),
        out_shape=jax.ShapeDtypeStruct((S, D), BF), compiler_params=_params("parallel"))(x, g)


def _norm_bwd_rows(dy, xv, g):
    r = _rstd(xv)
    xh = xv * r
    dxh = dy * g
    dx = r * (dxh - xh * jnp.mean(dxh * xh, axis=-1, keepdims=True))
    return dx, jnp.sum(dy * xh, axis=0, keepdims=True)


def _rmsnorm_bwd(name, dy, x, g, res):
    S, D = x.shape
    ts = _tile(S, 512)

    def body(dy_ref, x_ref, g_ref, res_ref, dx_ref, dg_ref):
        dx, dg = _norm_bwd_rows(dy_ref[...], x_ref[...], g_ref[...])
        dx_ref[...] = dx + res_ref[...]
        _accumulate(pl.program_id(0), dg_ref, dg)

    return pl.pallas_call(
        body, name=name, grid=(S // ts,),
        in_specs=[_row(ts, D), _row(ts, D), _bcast(1, D), _row(ts, D)],
        out_specs=[_row(ts, D), _bcast(1, D)],
        out_shape=[jax.ShapeDtypeStruct((S, D), F32), jax.ShapeDtypeStruct((1, D), F32)],
        compiler_params=_params("arbitrary"))(dy, x, g, res)


def _rope_tables(S):
    half = QK_ROPE // 2
    pos = jnp.arange(S, dtype=F32)
    inv_freq = ROPE_THETA ** (-jnp.arange(0, QK_ROPE, 2, dtype=F32) / QK_ROPE)
    ang = pos[:, None] * inv_freq[None, :]
    cos, sin = jnp.cos(ang), jnp.sin(ang)
    z = jnp.zeros((S, half), F32)
    return jnp.concatenate([cos, z, cos, z], axis=1), jnp.concatenate([-sin, z, sin, z], axis=1)


def _rope_lanes(x, cos, sin_signed, inverse):
    if inverse:
        return x * cos + pltpu.roll(x * sin_signed, LANE // 2, 1)
    return x * cos + pltpu.roll(x, LANE // 2, 1) * sin_signed


def _rope(name, x, cos, sin_signed, inverse):
    S, W = x.shape
    ts = _tile(S, 512)

    def body(x_ref, c_ref, s_ref, o_ref):
        c, s = c_ref[...], s_ref[...]
        for h in range(W // LANE):
            sl = slice(h * LANE, (h + 1) * LANE)
            o_ref[:, sl] = _rope_lanes(x_ref[:, sl], c, s, inverse).astype(o_ref.dtype)

    return pl.pallas_call(
        body, name=name, grid=(S // ts,), in_specs=[_row(ts, W), _row(ts, LANE), _row(ts, LANE)],
        out_specs=_row(ts, W), out_shape=jax.ShapeDtypeStruct((S, W), BF),
        compiler_params=_params("parallel"))(x, cos, sin_signed)


LAT_W = 1024
_CQ = slice(0, Q_RANK)
_CKV = slice(Q_RANK, Q_RANK + KV_RANK)
_KPE = slice(Q_RANK + KV_RANK, Q_RANK + KV_RANK + LANE)


def _mla_prep(lat, qg, kvg, cos, sin_signed):
    S = lat.shape[0]
    ts = _tile(S, 512)

    def body(lat_ref, qg_ref, kvg_ref, c_ref, s_ref, qn_ref, kvn_ref, kpe_ref):
        cq = lat_ref[:, _CQ]
        qn_ref[...] = (cq * _rstd(cq) * qg_ref[...]).astype(BF)
        ckv = lat_ref[:, _CKV]
        kvn_ref[...] = (ckv * _rstd(ckv) * kvg_ref[...]).astype(BF)
        kpe_ref[...] = _rope_lanes(lat_ref[:, _KPE], c_ref[...], s_ref[...], False).astype(BF)

    return pl.pallas_call(
        body, name="mla_prep", grid=(S // ts,),
        in_specs=[_row(ts, LAT_W), _bcast(1, Q_RANK), _bcast(1, KV_RANK), _row(ts, LANE), _row(ts, LANE)],
        out_specs=[_row(ts, Q_RANK), _row(ts, KV_RANK), _row(ts, LANE)],
        out_shape=[jax.ShapeDtypeStruct((S, Q_RANK), BF), jax.ShapeDtypeStruct((S, KV_RANK), BF),
                   jax.ShapeDtypeStruct((S, LANE), BF)],
        compiler_params=_params("parallel"))(lat, qg, kvg, cos, sin_signed)


def _mla_prep_bwd(lat, qg, kvg, cos, sin_signed, d_qn, d_kvn, d_kpe):
    S = lat.shape[0]
    ts = _tile(S, 512)

    def body(lat_ref, qg_ref, kvg_ref, c_ref, s_ref, dqn_ref, dkvn_ref, dkpe_ref, dlat_ref, dqg_ref, dkvg_ref):
        i = pl.program_id(0)
        dcq, dqg = _norm_bwd_rows(dqn_ref[...], lat_ref[:, _CQ], qg_ref[...])
        dckv, dkvg = _norm_bwd_rows(dkvn_ref[...], lat_ref[:, _CKV], kvg_ref[...])
        dlat_ref[:, _CQ] = dcq.astype(BF)
        dlat_ref[:, _CKV] = dckv.astype(BF)
        dkpe = dkpe_ref[0]
        for g in range(1, d_kpe.shape[0]):
            dkpe = dkpe + dkpe_ref[g]
        dlat_ref[:, _KPE] = _rope_lanes(dkpe, c_ref[...], s_ref[...], True).astype(BF)
        dlat_ref[:, _KPE.stop:] = jnp.zeros((ts, LAT_W - _KPE.stop), BF)
        _accumulate(i, dqg_ref, dqg)
        _accumulate(i, dkvg_ref, dkvg)

    return pl.pallas_call(
        body, name="mla_prep_bwd", grid=(S // ts,),
        in_specs=[_row(ts, LAT_W), _bcast(1, Q_RANK), _bcast(1, KV_RANK), _row(ts, LANE), _row(ts, LANE),
                  _row(ts, Q_RANK), _row(ts, KV_RANK), pl.BlockSpec((d_kpe.shape[0], ts, LANE), lambda i: (0, i, 0))],
        out_specs=[_row(ts, LAT_W), _bcast(1, Q_RANK), _bcast(1, KV_RANK)],
        out_shape=[jax.ShapeDtypeStruct((S, LAT_W), BF), jax.ShapeDtypeStruct((1, Q_RANK), F32),
                   jax.ShapeDtypeStruct((1, KV_RANK), F32)],
        compiler_params=_params("arbitrary"))(lat, qg, kvg, cos, sin_signed, d_qn, d_kvn, d_kpe)


def _sigmoid(z):
    return 1.0 / (1.0 + jnp.exp(-z))


def _merge_fwd(gpre, b_gate, o_a, o_b):
    S, D = o_a.shape
    ts = _tile(S, 256)

    def body(g_ref, b_ref, oa_ref, ob_ref, m_ref):
        ga = _sigmoid(g_ref[:, :D] + b_ref[:, :D])
        gb = _sigmoid(g_ref[:, D:] + b_ref[:, D:])
        m_ref[...] = (ga * oa_ref[...] + gb * ob_ref[...]).astype(BF)

    return pl.pallas_call(
        body, name="merge_fwd", grid=(S // ts,),
        in_specs=[_row(ts, 2 * D), _bcast(1, 2 * D), _row(ts, D), _row(ts, D)], out_specs=_row(ts, D),
        out_shape=jax.ShapeDtypeStruct((S, D), BF), compiler_params=_params("parallel"))(gpre, b_gate, o_a, o_b)


def _merge_bwd(d_merge, gpre, b_gate, o_a, o_b):
    S, D = o_a.shape
    ts = _tile(S, 256)

    def body(dm_ref, g_ref, b_ref, oa_ref, ob_ref, doa_ref, dob_ref, dg_ref, db_ref):
        dm = dm_ref[...]
        ga = _sigmoid(g_ref[:, :D] + b_ref[:, :D])
        gb = _sigmoid(g_ref[:, D:] + b_ref[:, D:])
        doa_ref[...] = (dm * ga).astype(BF)
        dob_ref[...] = (dm * gb).astype(BF)
        dga = dm * oa_ref[...] * ga * (1.0 - ga)
        dgb = dm * ob_ref[...] * gb * (1.0 - gb)
        dg_ref[:, :D] = dga.astype(BF)
        dg_ref[:, D:] = dgb.astype(BF)
        i = pl.program_id(0)
        part = jnp.concatenate([jnp.sum(dga, axis=0, keepdims=True), jnp.sum(dgb, axis=0, keepdims=True)], axis=1)
        _accumulate(i, db_ref, part)

    return pl.pallas_call(
        body, name="merge_bwd", grid=(S // ts,),
        in_specs=[_row(ts, D), _row(ts, 2 * D), _bcast(1, 2 * D), _row(ts, D), _row(ts, D)],
        out_specs=[_row(ts, D), _row(ts, D), _row(ts, 2 * D), _bcast(1, 2 * D)],
        out_shape=[jax.ShapeDtypeStruct((S, D), BF), jax.ShapeDtypeStruct((S, D), BF),
                   jax.ShapeDtypeStruct((S, 2 * D), BF), jax.ShapeDtypeStruct((1, 2 * D), F32)],
        compiler_params=_params("arbitrary"))(d_merge, gpre, b_gate, o_a, o_b)


def _final_loss(x2, tgt, gf):
    S, D = x2.shape
    ts = _tile(S, 512)

    def body(x_ref, t_ref, g_ref, dx_ref, dg_ref, loss_ref):
        i = pl.program_id(0)
        xv = x_ref[...]
        g = g_ref[...]
        y = xv * _rstd(xv) * g
        err = y - t_ref[...]
        dx, dg = _norm_bwd_rows(err * (1.0 / D), xv, g)
        dx_ref[...] = dx
        _accumulate(i, dg_ref, dg)
        part = 0.5 * jnp.sum(jnp.mean(err * err, axis=-1, keepdims=True), axis=0, keepdims=True)
        _accumulate(i, loss_ref, jnp.broadcast_to(part, (8, LANE)))

    return pl.pallas_call(
        body, name="final_loss", grid=(S // ts,),
        in_specs=[_row(ts, D), _row(ts, D), _bcast(1, D)],
        out_specs=[_row(ts, D), _bcast(1, D), _bcast(8, LANE)],
        out_shape=[jax.ShapeDtypeStruct((S, D), F32), jax.ShapeDtypeStruct((1, D), F32),
                   jax.ShapeDtypeStruct((8, LANE), F32)],
        compiler_params=_params("arbitrary"))(x2, tgt, gf)


HALO = 16


SUB = 8


def _shift_down(cur, prev, k, rows):
    out = pltpu.roll(cur, k, 0)
    head = out[:SUB]
    for j in range(k):
        head = jnp.where(rows == j, prev[HALO - k + j:HALO - k + j + 1, :], head)
    return jnp.concatenate([head, out[SUB:]], axis=0)


def _shift_up(cur, nxt, k, rows, ts):
    out = pltpu.roll(cur, ts - k, 0)
    tail = out[ts - SUB:]
    for j in range(k):
        tail = jnp.where(rows == SUB - k + j, nxt[j:j + 1, :], tail)
    return jnp.concatenate([out[:ts - SUB], tail], axis=0)


def _conv_rows(cur, prev, w, b, rows):
    return b + w[0:1, :] * _shift_down(cur, prev, 2, rows) + w[1:2, :] * _shift_down(cur, prev, 1, rows) + w[2:3, :] * cur


def _conv_specs(ts, C, shard_of):
    nh = ts // HALO
    cur = pl.BlockSpec((None, ts, C), lambda g, i: (shard_of(g), i, 0))
    prev = pl.BlockSpec((None, HALO, C), lambda g, i: (shard_of(g), jnp.maximum(i * nh - 1, 0), 0))
    return cur, prev


def _ffn_act(u_pre, conv_w, conv_b):
    G4, S, C = u_pre.shape
    G = G4 // 2
    ts = _tile(S, 256)

    def body(up_ref, upp_ref, gt_ref, gtp_ref, wu_ref, wg_ref, bu_ref, bg_ref, act_ref, u_ref):
        first = pl.program_id(1) == 0
        rows = lax.broadcasted_iota(jnp.int32, (SUB, C), 0)
        pu = jnp.where(first, 0.0, upp_ref[...].astype(F32))
        pg = jnp.where(first, 0.0, gtp_ref[...].astype(F32))
        up = _conv_rows(up_ref[...].astype(F32), pu, wu_ref[...], bu_ref[...], rows)
        gate = _conv_rows(gt_ref[...].astype(F32), pg, wg_ref[...], bg_ref[...], rows)
        act_ref[...] = (gate * _sigmoid(gate) * up).astype(BF)
        u_ref[0] = up.astype(BF)
        u_ref[1] = gate.astype(BF)

    cur_u, prev_u = _conv_specs(ts, C, lambda g: g)
    cur_g, prev_g = _conv_specs(ts, C, lambda g: g + G)
    w_u = pl.BlockSpec((None, 3, C), lambda g, i: (g, 0, 0))
    w_g = pl.BlockSpec((None, 3, C), lambda g, i: (g + G, 0, 0))
    b_u = pl.BlockSpec((None, 1, C), lambda g, i: (g, 0, 0))
    b_g = pl.BlockSpec((None, 1, C), lambda g, i: (g + G, 0, 0))
    pair = pl.BlockSpec((2, None, ts, C), lambda g, i: (0, g, i, 0))
    act, u = pl.pallas_call(
        body, name="ffn_act", grid=(G, S // ts),
        in_specs=[cur_u, prev_u, cur_g, prev_g, w_u, w_g, b_u, b_g],
        out_specs=[pl.BlockSpec((None, ts, C), lambda g, i: (g, i, 0)), pair],
        out_shape=[jax.ShapeDtypeStruct((G, S, C), BF), jax.ShapeDtypeStruct((2, G, S, C), BF)],
        compiler_params=_params("parallel", "parallel"))(u_pre, u_pre, u_pre, u_pre, conv_w, conv_w, conv_b, conv_b)
    return act, u


def _ffn_act_conv_bwd(u, d_act, u_pre, conv_w):
    _, G, S, C = u.shape
    ts = _tile(S, 256)
    nh = ts // HALO
    last_halo = S // HALO - 1

    def d_outputs(up, gate, da):
        sg = _sigmoid(gate)
        return da * (gate * sg), da * up * (sg * (1.0 + gate * (1.0 - sg)))

    def body(u_ref, un_ref, da_ref, dan_ref, x_ref, w_ref, dpre_ref, dw_ref, db_ref):
        i = pl.program_id(1)
        rows = lax.broadcasted_iota(jnp.int32, (SUB, C), 0)
        da_n = jnp.where(i == pl.num_programs(1) - 1, 0.0, dan_ref[...].astype(F32))
        du = d_outputs(u_ref[0].astype(F32), u_ref[1].astype(F32), da_ref[...].astype(F32))
        du_n = d_outputs(un_ref[0].astype(F32), un_ref[1].astype(F32), da_n)
        for kind in range(2):
            du_c, w = du[kind], w_ref[kind]
            up1, up2 = _shift_up(du_c, du_n[kind], 1, rows, ts), _shift_up(du_c, du_n[kind], 2, rows, ts)
            dpre_ref[kind] = (w[2:3, :] * du_c + w[1:2, :] * up1 + w[0:1, :] * up2).astype(BF)
            x_c = x_ref[kind].astype(F32)
            dw = jnp.concatenate([
                jnp.sum(up2 * x_c, axis=0, keepdims=True),
                jnp.sum(up1 * x_c, axis=0, keepdims=True),
                jnp.sum(du_c * x_c, axis=0, keepdims=True)], axis=0)
            db = jnp.sum(du_c, axis=0, keepdims=True)

            @pl.when(i == 0)
            def _():
                dw_ref[kind] = dw
                db_ref[kind] = db

            @pl.when(i > 0)
            def _():
                dw_ref[kind] += dw
                db_ref[kind] += db

    def nxt_row(i):
        return jnp.minimum((i + 1) * nh, last_halo)

    pair = pl.BlockSpec((2, None, ts, C), lambda g, i: (0, g, i, 0))
    pair_n = pl.BlockSpec((2, None, HALO, C), lambda g, i: (0, g, nxt_row(i), 0))
    one = pl.BlockSpec((None, ts, C), lambda g, i: (g, i, 0))
    one_n = pl.BlockSpec((None, HALO, C), lambda g, i: (g, nxt_row(i), 0))
    w_spec = pl.BlockSpec((2, None, 3, C), lambda g, i: (0, g, 0, 0))
    b_spec = pl.BlockSpec((2, None, 1, C), lambda g, i: (0, g, 0, 0))
    dpre, dw, db = pl.pallas_call(
        body, name="ffn_act_conv_bwd", grid=(G, S // ts),
        in_specs=[pair, pair_n, one, one_n, pair, w_spec], out_specs=[pair, w_spec, b_spec],
        out_shape=[jax.ShapeDtypeStruct((2, G, S, C), BF), jax.ShapeDtypeStruct((2, G, 3, C), F32),
                   jax.ShapeDtypeStruct((2, G, 1, C), F32)],
        compiler_params=_params("parallel", "arbitrary"))(
            u, u, d_act, d_act, u_pre.reshape(2, G, S, C), conv_w.reshape(2, G, 3, C))
    return dpre.reshape(2 * G, S, C), dw.reshape(2 * G, 3, C), db.reshape(2 * G, 1, C)


MLA_T = 1024
MLA_HB = 4
MLA_BWD_HB = 2


def _mla_pairs(n, by_row):
    if by_row:
        pairs = [(i, j) for i in range(n) for j in range(i + 1)]
    else:
        pairs = [(i, j) for j in range(n) for i in range(j, n)]
    return jnp.asarray([p[0] for p in pairs], jnp.int32), jnp.asarray([p[1] for p in pairs], jnp.int32)


def _mla_specs(hb):
    q = pl.BlockSpec((MLA_T, hb * HEAD), lambda g, t, it, jt: (it[t], g))
    k = pl.BlockSpec((MLA_T, hb * HEAD), lambda g, t, it, jt: (jt[t], g))
    kpe = pl.BlockSpec((MLA_T, HEAD), lambda g, t, it, jt: (jt[t], 0))
    lse = pl.BlockSpec((hb, MLA_T, LANE), lambda g, t, it, jt: (g, it[t], 0))
    return q, k, kpe, lse


def _mla_head(ref, hh):
    return ref[:, hh * HEAD:(hh + 1) * HEAD]


LOG2E = math.log2(math.e)
MLA_EXP2_SCALE = MLA_SCALE * LOG2E


def _mla_scores(qn_ref, qp_ref, kn_ref, kpe, hh, ok):
    q = jnp.concatenate([_mla_head(qn_ref, hh), _mla_head(qp_ref, hh)], axis=1)
    k = jnp.concatenate([_mla_head(kn_ref, hh), kpe], axis=1)
    s = lax.dot_general(q, k, NT, preferred_element_type=F32)
    return q, k, s if ok is None else jnp.where(ok, s, -jnp.inf)


def _mla_diagonal_mask():
    row = lax.broadcasted_iota(jnp.int32, (MLA_T, MLA_T), 0)
    col = lax.broadcasted_iota(jnp.int32, (MLA_T, MLA_T), 1)
    return col <= row


def _mla_step(i, j, step):
    @pl.when(j < i)
    def _():
        step(None)

    @pl.when(j == i)
    def _():
        step(_mla_diagonal_mask())


def _mla_fwd(qn, qp, kn, kpe, v):
    S = qn.shape[0]
    it, jt = _mla_pairs(S // MLA_T, True)

    def body(it_ref, jt_ref, qn_ref, qp_ref, kn_ref, kpe_ref, v_ref, o_ref, lse_ref, m_scr, acc_scr):
        t = pl.program_id(1)
        i, j = it_ref[t], jt_ref[t]

        @pl.when(j == 0)
        def _():
            m_scr[...] = jnp.full(m_scr.shape, -jnp.inf, F32)
            acc_scr[...] = jnp.zeros(acc_scr.shape, F32)

        def step(ok):
            kpe_v = kpe_ref[...]
            ones = jnp.ones((MLA_T, HEAD), BF)
            state = [(m_scr[hh], acc_scr[hh]) for hh in range(MLA_HB)]
            new = []
            for hh in range(MLA_HB):
                m_prev, acc = state[hh]
                _, _, s = _mla_scores(qn_ref, qp_ref, kn_ref, kpe_v, hh, ok)
                m_new = jnp.maximum(m_prev, jnp.max(s, axis=1, keepdims=True))
                p = jnp.exp2((s - m_new) * MLA_EXP2_SCALE).astype(BF)
                v1 = jnp.concatenate([_mla_head(v_ref, hh), ones], axis=1)
                alpha = jnp.exp2((m_prev - m_new) * MLA_EXP2_SCALE)
                new.append((m_new, alpha * acc + lax.dot_general(p, v1, NN, preferred_element_type=F32)))
            for hh in range(MLA_HB):
                m_scr[hh], acc_scr[hh] = new[hh]

        _mla_step(i, j, step)

        @pl.when(j == i)
        def _():
            for hh in range(MLA_HB):
                l = acc_scr[hh, :, HEAD:]
                o_ref[:, hh * HEAD:(hh + 1) * HEAD] = (acc_scr[hh, :, :HEAD] / l).astype(BF)
                lse_ref[hh] = m_scr[hh] * MLA_SCALE + jnp.log(l)

    qspec, kspec, kpespec, lsespec = _mla_specs(MLA_HB)
    grid_spec = pltpu.PrefetchScalarGridSpec(
        num_scalar_prefetch=2, grid=(MLA_HEADS // MLA_HB, it.shape[0]),
        in_specs=[qspec, qspec, kspec, kpespec, kspec], out_specs=[qspec, lsespec],
        scratch_shapes=[pltpu.VMEM((MLA_HB, MLA_T, 1), F32), pltpu.VMEM((MLA_HB, MLA_T, 2 * HEAD), F32)])
    return pl.pallas_call(
        body, name="mla_fwd", grid_spec=grid_spec,
        out_shape=[jax.ShapeDtypeStruct((S, MLA_HEADS * HEAD), BF), jax.ShapeDtypeStruct((MLA_HEADS, S, LANE), F32)],
        compiler_params=_params("parallel", "arbitrary"))(it, jt, qn, qp, kn, kpe, v)


def _mla_p_ds(qn_ref, qp_ref, kn_ref, kpe, v_ref, do_ref, o_ref, lse_ref, hh, ok):
    q, k, s = _mla_scores(qn_ref, qp_ref, kn_ref, kpe, hh, ok)
    p = jnp.exp2(s * MLA_EXP2_SCALE - lse_ref[hh][:, 0:1] * LOG2E)
    do = _mla_head(do_ref, hh)
    delta = jnp.sum(do.astype(F32) * _mla_head(o_ref, hh).astype(F32), axis=1, keepdims=True)
    dp = lax.dot_general(do, _mla_head(v_ref, hh), NT, preferred_element_type=F32)
    ds = p * (dp - delta) * MLA_SCALE
    return q, k, p, ds, do


def _mla_bwd(qn, qp, kn, kpe, v, do, o, lse):
    S = qn.shape[0]
    nq = S // MLA_T
    hb = MLA_BWD_HB
    it, jt = _mla_pairs(nq, False)

    def body(it_ref, jt_ref, qn_ref, qp_ref, kn_ref, kpe_ref, v_ref, do_ref, o_ref, lse_ref,
             dqn_ref, dqp_ref, dkn_ref, dv_ref, dkpe_ref, dq_acc, dk_acc, dv_acc, stage_n, stage_p, osem):
        t = pl.program_id(1)
        i, j = it_ref[t], jt_ref[t]

        @pl.when(t == 0)
        def _():
            dq_acc[...] = jnp.zeros(dq_acc.shape, F32)

        @pl.when(i == j)
        def _():
            dk_acc[...] = jnp.zeros(dk_acc.shape, F32)
            dv_acc[...] = jnp.zeros(dv_acc.shape, F32)

        def step(ok):
            kpe_v = kpe_ref[...]
            for hh in range(hb):
                q, k, p, ds, do_h = _mla_p_ds(qn_ref, qp_ref, kn_ref, kpe_v, v_ref, do_ref, o_ref, lse_ref, hh, ok)
                ds = ds.astype(BF)
                dv_acc[hh] += lax.dot_general(p.astype(BF), do_h, TN, preferred_element_type=F32)
                dk_acc[hh] += lax.dot_general(ds, q, TN, preferred_element_type=F32)
                dq_acc[i, hh] += lax.dot_general(ds, k, NN, preferred_element_type=F32)

        _mla_step(i, j, step)

        @pl.when(i == j)
        def _():
            for hh in range(hb):
                stage_n[:, hh * HEAD:(hh + 1) * HEAD] = dq_acc[i, hh, :, :HEAD].astype(BF)
                stage_p[:, hh * HEAD:(hh + 1) * HEAD] = dq_acc[i, hh, :, HEAD:]
            rows = pl.ds(pl.multiple_of(i * MLA_T, MLA_T), MLA_T)
            cols = pl.ds(pl.multiple_of(pl.program_id(0) * hb * HEAD, LANE), hb * HEAD)
            out_n = pltpu.make_async_copy(stage_n, dqn_ref.at[rows, cols], osem.at[0])
            out_p = pltpu.make_async_copy(stage_p, dqp_ref.at[rows, cols], osem.at[1])
            out_n.start()
            out_p.start()
            out_n.wait()
            out_p.wait()

        @pl.when(i == nq - 1)
        def _():
            dkpe = dk_acc[0, :, HEAD:]
            for hh in range(hb):
                dkn_ref[:, hh * HEAD:(hh + 1) * HEAD] = dk_acc[hh, :, :HEAD].astype(BF)
                dv_ref[:, hh * HEAD:(hh + 1) * HEAD] = dv_acc[hh].astype(BF)
                if hh:
                    dkpe = dkpe + dk_acc[hh, :, HEAD:]
            dkpe_ref[...] = dkpe

    qspec, kspec, kpespec, lsespec = _mla_specs(hb)
    dkpespec = pl.BlockSpec((None, MLA_T, HEAD), lambda g, t, it, jt: (g, jt[t], 0))
    grid_spec = pltpu.PrefetchScalarGridSpec(
        num_scalar_prefetch=2, grid=(MLA_HEADS // hb, it.shape[0]),
        in_specs=[qspec, qspec, kspec, kpespec, kspec, qspec, qspec, lsespec],
        out_specs=[ANY, ANY, kspec, kspec, dkpespec],
        scratch_shapes=[pltpu.VMEM((nq, hb, MLA_T, 2 * HEAD), F32), pltpu.VMEM((hb, MLA_T, 2 * HEAD), F32),
                        pltpu.VMEM((hb, MLA_T, HEAD), F32), pltpu.VMEM((MLA_T, hb * HEAD), BF),
                        pltpu.VMEM((MLA_T, hb * HEAD), F32), pltpu.SemaphoreType.DMA((2,))])
    wide = jax.ShapeDtypeStruct((S, MLA_HEADS * HEAD), BF)
    return pl.pallas_call(
        body, name="mla_bwd", grid_spec=grid_spec,
        out_shape=[wide, jax.ShapeDtypeStruct((S, MLA_HEADS * HEAD), F32), wide, wide,
                   jax.ShapeDtypeStruct((MLA_HEADS // hb, S, HEAD), F32)],
        compiler_params=_params("parallel", "arbitrary"))(it, jt, qn, qp, kn, kpe, v, do, o, lse)


DIL_W = 3 * DIL_HPG * HEAD
DIL_O = DIL_HPG * HEAD
DIL_STEP_BLOCKS = 4


def _dil_slopes(g):
    return [2.0 ** (-ALIBI_MAX_BIAS * (g * DIL_HPG + hh + 1) / DIL_HEADS) for hh in range(DIL_HPG)]


def _dil_bias(dil):
    p = lax.broadcasted_iota(jnp.int32, (DIL_BLOCK, DIL_BLOCK), 0)
    kk = lax.broadcasted_iota(jnp.int32, (DIL_BLOCK, DIL_BLOCK), 1)
    jc = p - kk
    dist_c = (dil * jc).astype(F32)
    dist_p = (dil * (jc + DIL_BLOCK)).astype(F32)
    return jc >= 0, jc <= 0, dist_c, dist_p


def _dil_bias2(dil):
    p = lax.broadcasted_iota(jnp.int32, (DIL_BLOCK, 2 * DIL_BLOCK), 0)
    kk = lax.broadcasted_iota(jnp.int32, (DIL_BLOCK, 2 * DIL_BLOCK), 1)
    j = p + DIL_BLOCK - kk
    return (j >= 0) & (j <= DIL_BLOCK), kk < DIL_BLOCK, (dil * j).astype(F32)


def _dil_head(blk, hh):
    q = blk[:, hh * HEAD:(hh + 1) * HEAD]
    k = blk[:, DIL_O + hh * HEAD:DIL_O + (hh + 1) * HEAD]
    v = blk[:, 2 * DIL_O + hh * HEAD:2 * DIL_O + (hh + 1) * HEAD]
    return q, k, v


def _dil_s(q, k, slope, dist, ok):
    s = lax.dot_general(q, k, NT, preferred_element_type=F32) * DIL_SCALE - slope * dist
    return jnp.where(ok, s, -jnp.inf)


def _dil_view(a, dil):
    S, W = a.shape
    return a.reshape(S // dil, dil * W)


def _dil_fwd(qkv, g):
    _, dil = DIL_PATTERNS[g]
    S = qkv.shape[0]
    L = S // dil
    nb = L // DIL_BLOCK
    slopes = _dil_slopes(g)

    bb = min(DIL_STEP_BLOCKS, nb)
    rows = bb * DIL_BLOCK

    def body(cur_ref, prev_ref, o_ref, lse_ref):
        n = pl.program_id(1)
        ok, in_prev, dist = _dil_bias2(dil)
        for b in range(bb):
            if b == 0:
                both = jnp.concatenate([prev_ref[...], cur_ref[0:DIL_BLOCK, :]], axis=0)
                ok_b = ok & (~in_prev | (n > 0))
            else:
                both = cur_ref[(b - 1) * DIL_BLOCK:(b + 1) * DIL_BLOCK, :]
                ok_b = ok
            for hh in range(DIL_HPG):
                q, _, _ = _dil_head(both[DIL_BLOCK:], hh)
                _, k2, v2 = _dil_head(both, hh)
                s = _dil_s(q, k2, slopes[hh], dist, ok_b)
                m = jnp.max(s, axis=1, keepdims=True)
                p = jnp.exp(s - m)
                l = jnp.sum(p, axis=1, keepdims=True)
                o = lax.dot_general(p.astype(BF), v2, NN, preferred_element_type=F32) / l
                rs, sl = slice(b * DIL_BLOCK, (b + 1) * DIL_BLOCK), slice(hh * HEAD, (hh + 1) * HEAD)
                o_ref[rs, sl] = o
                lse_ref[rs, sl] = jnp.broadcast_to(m + jnp.log(l), (DIL_BLOCK, HEAD))

    ospec = pl.BlockSpec((rows, DIL_O), lambda r, n: (n, r))
    o, lse = pl.pallas_call(
        body, name=f"dil_fwd{g}", grid=(dil, nb // bb),
        in_specs=[pl.BlockSpec((rows, DIL_W), lambda r, n: (n, r)),
                  pl.BlockSpec((DIL_BLOCK, DIL_W), lambda r, n: (jnp.maximum(n * bb - 1, 0), r))],
        out_specs=[ospec, ospec],
        out_shape=[jax.ShapeDtypeStruct((L, dil * DIL_O), F32), jax.ShapeDtypeStruct((L, dil * DIL_O), F32)],
        compiler_params=_params("parallel", "parallel"))(_dil_view(qkv, dil), _dil_view(qkv, dil))
    return o.reshape(S, DIL_O), lse.reshape(S, DIL_O)


def _dil_combine(os_, lses):
    S = os_[0].shape[0]
    ts = _tile(S, 512)

    def body(o0, o1, o2, l0, l1, l2, out_ref, lse_ref):
        a, b, c = l0[...], l1[...], l2[...]
        m = jnp.maximum(jnp.maximum(a, b), c)
        ea, eb, ec = jnp.exp(a - m), jnp.exp(b - m), jnp.exp(c - m)
        tot = ea + eb + ec
        out_ref[...] = ((ea * o0[...] + eb * o1[...] + ec * o2[...]) / tot).astype(BF)
        lse_ref[...] = m + jnp.log(tot)

    return pl.pallas_call(
        body, name="dil_combine", grid=(S // ts,), in_specs=[_row(ts, DIL_O)] * 6,
        out_specs=[_row(ts, DIL_O), _row(ts, DIL_O)],
        out_shape=[jax.ShapeDtypeStruct((S, DIL_O), BF), jax.ShapeDtypeStruct((S, DIL_O), F32)],
        compiler_params=_params("parallel"))(*os_, *lses)


def _dil_delta(do, out):
    S = do.shape[0]
    ts = _tile(S, 512)

    def body(do_ref, o_ref, d_ref):
        for hh in range(DIL_HPG):
            sl = slice(hh * HEAD, (hh + 1) * HEAD)
            d = jnp.sum(do_ref[:, sl].astype(F32) * o_ref[:, sl].astype(F32), axis=1, keepdims=True)
            d_ref[:, sl] = jnp.broadcast_to(d, (ts, HEAD))

    return pl.pallas_call(
        body, name="dil_delta", grid=(S // ts,), in_specs=[_row(ts, DIL_O)] * 2, out_specs=_row(ts, DIL_O),
        out_shape=jax.ShapeDtypeStruct((S, DIL_O), F32), compiler_params=_params("parallel"))(do, out)


def _dil_bwd(qkv, do, lse, delta, g):
    _, dil = DIL_PATTERNS[g]
    S = qkv.shape[0]
    L = S // dil
    nb = L // DIL_BLOCK
    slopes = _dil_slopes(g)

    def pair(q, k, v, do_h, lse_h, delta_h, slope, dist, ok):
        s = _dil_s(q, k, slope, dist, ok)
        p = jnp.exp(s - lse_h)
        dp = lax.dot_general(do_h, v, NT, preferred_element_type=F32)
        ds = (p * (dp - delta_h) * DIL_SCALE).astype(BF)
        return p.astype(BF), ds

    bb = min(DIL_STEP_BLOCKS, nb)
    rows = bb * DIL_BLOCK
    steps = nb // bb

    def body(cur_ref, prev_ref, next_ref, doc_ref, don_ref, lsec_ref, lsen_ref, dlc_ref, dln_ref, out_ref):
        n = pl.program_id(1)
        ok2, in_prev, dist2 = _dil_bias2(dil)
        _, ok_p0, _, dist_p = _dil_bias(dil)
        for b in range(bb):
            rs = slice(b * DIL_BLOCK, (b + 1) * DIL_BLOCK)
            rn = slice((b + 1) * DIL_BLOCK, (b + 2) * DIL_BLOCK)
            two = slice(b * DIL_BLOCK, (b + 2) * DIL_BLOCK)
            first, last = b == 0, b == bb - 1
            if first:
                keys = jnp.concatenate([prev_ref[...], cur_ref[rs, :]], axis=0)
                ok_ab = ok2 & (~in_prev | (n > 0))
            else:
                keys = cur_ref[(b - 1) * DIL_BLOCK:(b + 1) * DIL_BLOCK, :]
                ok_ab = ok2
            qrows = jnp.concatenate([cur_ref[rs, :], next_ref[...]], axis=0) if last else cur_ref[two, :]
            ok_n = ok_p0 & (n < steps - 1) if last else ok_p0
            for hh in range(DIL_HPG):
                sl = slice(hh * HEAD, (hh + 1) * HEAD)
                q2, _, _ = _dil_head(qrows, hh)
                _, k2, v2 = _dil_head(keys, hh)
                q, qn, kc, vc = q2[:DIL_BLOCK], q2[DIL_BLOCK:], k2[DIL_BLOCK:], v2[DIL_BLOCK:]
                do2 = jnp.concatenate([doc_ref[rs, sl], don_ref[:, sl]], axis=0) if last else doc_ref[two, sl]
                do_c, do_n = do2[:DIL_BLOCK], do2[DIL_BLOCK:]
                lse_c = lsec_ref[rs, sl][:, 0:1]
                lse_n = (lsen_ref[:, sl] if last else lsec_ref[rn, sl])[:, 0:1]
                dl_c = dlc_ref[rs, sl][:, 0:1]
                dl_n = (dln_ref[:, sl] if last else dlc_ref[rn, sl])[:, 0:1]
                p_ab, ds_ab = pair(q, k2, v2, do_c, lse_c, dl_c, slopes[hh], dist2, ok_ab)
                p_n, ds_n = pair(qn, kc, vc, do_n, lse_n, dl_n, slopes[hh], dist_p, ok_n)
                dq = lax.dot_general(ds_ab, k2, NN, preferred_element_type=F32)
                dk = lax.dot_general(jnp.concatenate([ds_ab[:, DIL_BLOCK:], ds_n], axis=0), q2, TN, preferred_element_type=F32)
                dv = lax.dot_general(jnp.concatenate([p_ab[:, DIL_BLOCK:], p_n], axis=0), do2, TN, preferred_element_type=F32)
                out_ref[rs, sl] = dq.astype(BF)
                out_ref[rs, DIL_O + hh * HEAD:DIL_O + (hh + 1) * HEAD] = dk.astype(BF)
                out_ref[rs, 2 * DIL_O + hh * HEAD:2 * DIL_O + (hh + 1) * HEAD] = dv.astype(BF)

    cur_w = pl.BlockSpec((rows, DIL_W), lambda r, n: (n, r))
    prev_w = pl.BlockSpec((DIL_BLOCK, DIL_W), lambda r, n: (jnp.maximum(n * bb - 1, 0), r))
    next_w = pl.BlockSpec((DIL_BLOCK, DIL_W), lambda r, n: (jnp.minimum((n + 1) * bb, nb - 1), r))
    cur_o = pl.BlockSpec((rows, DIL_O), lambda r, n: (n, r))
    next_o = pl.BlockSpec((DIL_BLOCK, DIL_O), lambda r, n: (jnp.minimum((n + 1) * bb, nb - 1), r))
    qv, dov, lsev, dlv = _dil_view(qkv, dil), _dil_view(do, dil), _dil_view(lse, dil), _dil_view(delta, dil)
    out = pl.pallas_call(
        body, name=f"dil_bwd{g}", grid=(dil, steps),
        in_specs=[cur_w, prev_w, next_w, cur_o, next_o, cur_o, next_o, cur_o, next_o],
        out_specs=cur_w, out_shape=jax.ShapeDtypeStruct((L, dil * DIL_W), BF),
        compiler_params=_params("parallel", "parallel"))(qv, qv, qv, dov, dov, lsev, lsev, dlv, dlv)
    return out.reshape(S, DIL_W)


def _adamw(name, w, g, m, v):
    R, C = w.shape
    tr, tc = _adamw_block(R, C)

    def body(w_ref, g_ref, m_ref, v_ref, go_ref, d_ref, nm_ref, nv_ref):
        gv = g_ref[...]
        go_ref[...] = gv
        nm = ADAM_B1 * m_ref[...] + (1.0 - ADAM_B1) * gv
        nv = ADAM_B2 * v_ref[...] + (1.0 - ADAM_B2) * (gv * gv)
        m_hat = nm / (1.0 - ADAM_B1 ** ADAM_STEP)
        v_hat = nv / (1.0 - ADAM_B2 ** ADAM_STEP)
        d_ref[...] = -ADAM_LR * (m_hat / (jnp.sqrt(v_hat) + ADAM_EPS) + ADAM_WD * w_ref[...])
        nm_ref[...] = nm
        nv_ref[...] = nv

    spec = pl.BlockSpec((tr, tc), lambda i, j: (i, j))
    shp = jax.ShapeDtypeStruct((R, C), F32)
    return pl.pallas_call(
        body, name=name, grid=(R // tr, C // tc), in_specs=[spec] * 4, out_specs=[spec] * 4, out_shape=[shp] * 4,
        compiler_params=_params("parallel", "parallel"))(w, g, m, v)


ADAMW_BLOCK_ELEMS = 704 * 1024


def _adamw_block(R, C):
    if R * C <= ADAMW_BLOCK_ELEMS:
        return R, C
    tr = _tile_rows(R, max(8, ADAMW_BLOCK_ELEMS // C))
    tc = _tile(C, max(LANE, ADAMW_BLOCK_ELEMS // R))
    if tr * C >= R * tc or R * tc > ADAMW_BLOCK_ELEMS:
        return tr, C
    return R, tc


def _tile_rows(n, pref, mult=8):
    t = (pref // mult) * mult
    while t >= mult:
        if n % t == 0:
            return t
        t -= mult
    return n


ANY = pl.BlockSpec(memory_space=pl.ANY)


def _place():
    x, y, c = lax.axis_index("x"), lax.axis_index("y"), lax.axis_index("c")
    chips = [(1 - x, y), (x, 1 - y), (1 - x, 1 - y)]
    chip_idx = [2 * cx + cy for cx, cy in chips]
    return x, y, c, 2 * x + y, chips, chip_idx


def _rcopy(src, dst, ssem, rsem, dev):
    return pltpu.make_async_remote_copy(src_ref=src, dst_ref=dst, send_sem=ssem, recv_sem=rsem,
                                        device_id=dev, device_id_type=MESH)


HBM = pl.BlockSpec(memory_space=pltpu.HBM)
SEM = pl.BlockSpec(memory_space=pltpu.SEMAPHORE)
EFFECT = pltpu.SideEffectType.DATAFLOW_SIDE_EFFECTING


def _split_copies(kind, srcs, lands, ssem, rsem):
    _, _, c, me, chips, chip_idx = _place()
    cps = []
    for i in range(len(srcs)):
        for k in range(3):
            if kind == "gather":
                rows = srcs[i].shape[0]
                if rows == lands[i].shape[1]:
                    src, dst = srcs[i], lands[i].at[me]
                else:
                    src, dst = srcs[i], lands[i].at[me, pl.ds(pl.multiple_of(c * rows, 16), rows)]
            else:
                src, dst = srcs[i].at[chip_idx[k]], lands[i].at[k]
            cps.append(_rcopy(src, dst, ssem.at[3 * i + k], rsem.at[3 * i + k], (*chips[k], c)))
    return cps


def _exchange_start(name, kind, srcs, lands, groups):
    n, ng = len(srcs), len(groups)

    def body(*refs):
        src_refs, land_refs = refs[:n], refs[n:2 * n]
        sems = refs[2 * n:2 * n + 2 * ng]
        token = refs[-1]
        for gi, grp in enumerate(groups):
            cps = _split_copies(kind, [src_refs[i] for i in grp], [land_refs[i] for i in grp], sems[2 * gi], sems[2 * gi + 1])
            for cp in cps:
                cp.start()
        token[...] = jnp.zeros_like(token)

    arrays = list(srcs) + list(lands)
    out_shape = []
    for grp in groups:
        out_shape += [pltpu.SemaphoreType.DMA((3 * len(grp),)), pltpu.SemaphoreType.DMA((3 * len(grp),))]
    out_shape += [pltpu.HBM(a.shape, a.dtype) for a in arrays] + [jax.ShapeDtypeStruct((8, LANE), F32)]
    outs = pl.pallas_call(
        body, name=name, out_shape=out_shape, in_specs=[HBM] * (2 * n),
        out_specs=[SEM] * (2 * ng) + [HBM] * (2 * n) + [pl.BlockSpec(memory_space=pltpu.VMEM)],
        input_output_aliases={i: 2 * ng + i for i in range(2 * n)},
        compiler_params=pltpu.CompilerParams(has_side_effects=EFFECT),
    )(*[pltpu.with_memory_space_constraint(a, pltpu.HBM) for a in arrays])
    sems = [(outs[2 * gi], outs[2 * gi + 1]) for gi in range(ng)]
    thru = outs[2 * ng:2 * ng + 2 * n]
    return sems, thru[:n], thru[n:], outs[-1]


def _exchange_wait(name, kind, srcs, lands, sems, after):
    n = len(srcs)

    def body(*refs):
        cps = _split_copies(kind, refs[:n], refs[n:2 * n], refs[2 * n], refs[2 * n + 1])
        for cp in cps:
            cp.wait_send()
            cp.wait_recv()

    arrays = list(srcs) + list(lands)
    outs = pl.pallas_call(
        body, name=name, out_shape=[pltpu.HBM(a.shape, a.dtype) for a in arrays],
        in_specs=[HBM] * (2 * n) + [SEM, SEM, ANY], out_specs=[HBM] * (2 * n),
        input_output_aliases={i: i for i in range(2 * n)},
        compiler_params=pltpu.CompilerParams(has_side_effects=EFFECT),
    )(*arrays, sems[0], sems[1], after)
    return outs[:n], outs[n:]


EXCHANGE_CHUNK_BYTES = 3 * 1024 * 1024


def _half_geometry(R, C, axis):
    Rp, Cp = (R // 2, C) if axis == 0 else (R, C // 2)
    tr = _tile_rows(Rp, max(16, EXCHANGE_CHUNK_BYTES // (2 * Cp)), 16)
    return Rp, Cp, tr, Rp // tr


def _pair_sum(name, g, axis):
    G, R, C = g.shape
    Rp, Cp, tr, nb = _half_geometry(R, C, axis)
    steps = G * nb

    def half_block(t, h):
        s, b = t // nb, t % nb
        return (s, h * nb + b, 0) if axis == 0 else (s, b, h)

    def body(c_ref, keep_ref, give_ref, out_ref, land, ssem, rsem, credit):
        x, y, c = lax.axis_index("x"), lax.axis_index("y"), lax.axis_index("c")
        sib = (x, y, 1 - c)
        t = pl.program_id(0)

        def copy(chunk):
            return _rcopy(give_ref.at[0], land.at[chunk % 2], ssem.at[chunk % 2], rsem.at[chunk % 2], sib)

        @pl.when((t >= 2) & (t < steps))
        def _():
            pl.semaphore_wait(credit, 1)

        @pl.when(t < steps)
        def _():
            copy(t).start()

        @pl.when(t >= 1)
        def _():
            copy(t - 1).wait_recv()
            out_ref[...] = (keep_ref[...].astype(F32) + land[(t - 1) % 2].astype(F32)).astype(BF)

            @pl.when(t + 1 < steps)
            def _():
                pl.semaphore_signal(credit, 1, device_id=sib, device_id_type=MESH)

        @pl.when(t < steps)
        def _():
            copy(t).wait_send()

    blk = (None, tr, Cp)
    grid_spec = pltpu.PrefetchScalarGridSpec(
        num_scalar_prefetch=1, grid=(steps + 1,),
        in_specs=[pl.BlockSpec(blk, lambda t, c_ref: half_block(jnp.maximum(t - 1, 0), c_ref[0])),
                  pl.BlockSpec((1, tr, Cp), lambda t, c_ref: half_block(jnp.minimum(t, steps - 1), 1 - c_ref[0]))],
        out_specs=pl.BlockSpec(blk, lambda t, c_ref: (jnp.maximum(t - 1, 0) // nb, jnp.maximum(t - 1, 0) % nb, 0)),
        scratch_shapes=[pltpu.VMEM((2, tr, Cp), BF), pltpu.SemaphoreType.DMA((2,)), pltpu.SemaphoreType.DMA((2,)),
                        pltpu.SemaphoreType.REGULAR])
    c_arr = lax.axis_index("c").astype(jnp.int32).reshape(1)
    return pl.pallas_call(
        body, name=name, grid_spec=grid_spec, out_shape=jax.ShapeDtypeStruct((G, Rp, Cp), BF),
        compiler_params=_params("arbitrary"))(c_arr, g, g)


def _chip_total_join(name, h, landed, axis):
    G, Rp, Cp = h.shape
    R, C = (2 * Rp, Cp) if axis == 0 else (Rp, 2 * Cp)
    tr = _tile_rows(Rp, max(16, EXCHANGE_CHUNK_BYTES // (4 * Cp)), 16)
    nb = Rp // tr

    def body(me_ref, own_ref, l0_ref, l1_ref, l2_ref, full, stage, ssem, rsem, lsem):
        x, y, c = lax.axis_index("x"), lax.axis_index("y"), lax.axis_index("c")
        sib = (x, y, 1 - c)
        b = pl.program_id(0)

        def place(half, r0, rows):
            if axis == 0:
                return full.at[pl.ds(pl.multiple_of(half * Rp + r0, 8), rows), :]
            return full.at[pl.ds(pl.multiple_of(r0, 8), rows), pl.ds(pl.multiple_of(half * Cp, LANE), Cp)]

        def copies(step):
            s = step % 2
            mine = place(c, step * tr, tr)
            return pltpu.make_async_copy(stage.at[s], mine, lsem.at[s]), _rcopy(stage.at[s], mine, ssem.at[s], rsem, sib)

        @pl.when(b >= 2)
        def _():
            loc, rem = copies(b - 2)
            loc.wait()
            rem.wait_send()

        acc = own_ref[...].astype(F32)
        for r in (l0_ref, l1_ref, l2_ref):
            acc = acc + r[...].astype(F32)
        stage[b % 2] = acc
        loc, rem = copies(b)
        loc.start()
        rem.start()

        @pl.when(b == nb - 1)
        def _():
            for step in range(max(0, nb - 2), nb):
                loc, rem = copies(step)
                loc.wait()
                rem.wait_send()
            theirs = place(1 - c, 0, Rp)
            _rcopy(theirs, theirs, ssem.at[0], rsem, sib).wait_recv()

    blk = (None, tr, Cp)
    grid_spec = pltpu.PrefetchScalarGridSpec(
        num_scalar_prefetch=1, grid=(nb,),
        in_specs=[pl.BlockSpec(blk, lambda b, me_ref: (me_ref[0], b, 0))]
        + [pl.BlockSpec(blk, functools.partial(lambda b, me_ref, k: (k, b, 0), k=k)) for k in range(3)],
        out_specs=ANY,
        scratch_shapes=[pltpu.VMEM((2, tr, Cp), F32), pltpu.SemaphoreType.DMA((2,)), pltpu.SemaphoreType.DMA,
                        pltpu.SemaphoreType.DMA((2,))])
    me = (2 * lax.axis_index("x") + lax.axis_index("y")).astype(jnp.int32).reshape(1)
    return pl.pallas_call(
        body, name=name, grid_spec=grid_spec, out_shape=jax.ShapeDtypeStruct((R, C), F32),
        compiler_params=_params("arbitrary"))(me, h, landed, landed, landed)


def _pair_share(name, land):
    G, R, C = land.shape
    Rh = R // 2
    tr = _tile_rows(Rh, max(16, EXCHANGE_CHUNK_BYTES // (2 * C)), 16)
    chunks = [(k, b) for k in range(3) for b in range(Rh // tr)]

    def body(src, dst, buf, lsem, ssem, rsem):
        x, y, c, _, _, chip_idx = _place()
        sib = (x, y, 1 - c)

        def region(ref, k, half, r0, rows):
            return ref.at[chip_idx[k], pl.ds(pl.multiple_of(half * Rh + r0, 16), rows)]

        def load(t):
            k, b = chunks[t]
            return pltpu.make_async_copy(region(src, k, c, b * tr, tr), buf.at[t % 2], lsem.at[t % 2])

        def send(t):
            k, b = chunks[t]
            return _rcopy(buf.at[t % 2], region(dst, k, c, b * tr, tr), ssem.at[t % 2], rsem.at[k], sib)

        load(0).start()
        for t in range(len(chunks)):
            load(t).wait()
            if t + 1 < len(chunks):
                if t >= 1:
                    send(t - 1).wait_send()
                load(t + 1).start()
            send(t).start()
        for t in range(max(0, len(chunks) - 2), len(chunks)):
            send(t).wait_send()
        for k in range(3):
            theirs = region(dst, k, 1 - c, 0, Rh)
            _rcopy(theirs, theirs, ssem.at[0], rsem.at[k], sib).wait_recv()

    return pl.pallas_call(
        body, name=name, in_specs=[ANY], out_specs=ANY, out_shape=jax.ShapeDtypeStruct(land.shape, land.dtype),
        input_output_aliases={0: 0},
        scratch_shapes=[pltpu.VMEM((2, tr, C), land.dtype), pltpu.SemaphoreType.DMA((2,)), pltpu.SemaphoreType.DMA((2,)),
                        pltpu.SemaphoreType.DMA((3,))],
    )(land)


def _allreduce_small(v):
    R, K = v.shape
    ndev = 8

    def body(v_ref, o_ref, land, ssem, rsem):
        x, y, c = lax.axis_index("x"), lax.axis_index("y"), lax.axis_index("c")
        me = 4 * x + 2 * y + c
        land[me] = v_ref[...]
        cps = []
        for r in range(1, ndev):
            fx, fy, fc = (r >> 2) & 1, (r >> 1) & 1, r & 1
            peer = (x ^ fx, y ^ fy, c ^ fc)
            cp = _rcopy(v_ref, land.at[me], ssem.at[r - 1], rsem.at[r - 1], peer)
            cp.start()
            cps.append((cp, 4 * peer[0] + 2 * peer[1] + peer[2], r))
        for cp, src, r in cps:
            cp.wait_send()
            _rcopy(v_ref, land.at[src], ssem.at[r - 1], rsem.at[r - 1], (x, y, c)).wait_recv()
        acc = land[0]
        for d in range(1, ndev):
            acc = acc + land[d]
        o_ref[...] = acc

    vm = pl.BlockSpec(memory_space=pltpu.VMEM)
    return pl.pallas_call(
        body, name="allreduce_small", in_specs=[vm], out_specs=vm, out_shape=jax.ShapeDtypeStruct((R, K), F32),
        scratch_shapes=[pltpu.VMEM((ndev, R, K), F32), pltpu.SemaphoreType.DMA((ndev - 1,)), pltpu.SemaphoreType.DMA((ndev - 1,))],
    )(v)


IN_SPLITS = (Q_RANK, KV_RANK, QK_ROPE, DIL_HEADS * HEAD, DIL_HEADS * HEAD, DIL_HEADS * HEAD, D_MODEL, D_MODEL)
IN_OFF = tuple(int(v) for v in np.cumsum((0,) + IN_SPLITS))


def _unshard_cols(g):
    G, K, Ns = g.shape
    return g.transpose(1, 0, 2).reshape(K, G * Ns)


def _shard_cols(w):
    K, N = w.shape
    return w.reshape(K, N_CHIPS, N // N_CHIPS).transpose(1, 0, 2)


def _rope_pad(w):
    half = QK_ROPE // 2
    z = jnp.zeros(w.shape[:-1] + (half,), w.dtype)
    return jnp.concatenate([w[..., :half], z, w[..., half:], z], axis=-1)


def _rope_unpad(w):
    half = QK_ROPE // 2
    return jnp.concatenate([w[..., :half], w[..., 2 * half:3 * half]], axis=-1)


def _split_w_in(w_in_g):
    G, K, Ns = w_in_g.shape

    def cols(lo, hi):
        pieces = [w_in_g[k][:, max(lo, k * Ns) - k * Ns:min(hi, (k + 1) * Ns) - k * Ns]
                  for k in range(G) if max(lo, k * Ns) < min(hi, (k + 1) * Ns)]
        return pieces[0] if len(pieces) == 1 else jnp.concatenate(pieces, axis=1)

    p = [(IN_OFF[i], IN_OFF[i + 1]) for i in range(8)]
    w_lat = jnp.concatenate([cols(*p[0]), cols(*p[1]), _rope_pad(cols(*p[2])),
                             jnp.zeros((K, LAT_W - _KPE.stop), w_in_g.dtype)], axis=1)
    w_dil = [jnp.concatenate([cols(p[3 + t][0] + g * DIL_O, p[3 + t][0] + (g + 1) * DIL_O) for t in range(3)], axis=1)
             for g in range(DIL_GROUPS)]
    w_gate = cols(p[6][0], p[7][1])
    return w_lat, w_dil, w_gate


def _merge_dw_in(dw_lat, dw_dil, dw_gate):
    parts = [dw_lat[:, _CQ], dw_lat[:, _CKV], _rope_unpad(dw_lat[:, _KPE])]
    for t in range(3):
        parts += [dw_dil[g][:, t * DIL_O:(t + 1) * DIL_O] for g in range(DIL_GROUPS)]
    parts.append(dw_gate)
    width = sum(p.shape[1] for p in parts) // N_CHIPS
    shards = []
    for k in range(N_CHIPS):
        pieces, at = [], 0
        for p in parts:
            lo, hi = max(k * width, at), min((k + 1) * width, at + p.shape[1])
            if lo < hi:
                pieces.append(p[:, lo - at:hi - at])
            at += p.shape[1]
        shards.append(jnp.concatenate(pieces, axis=1))
    return jnp.stack(shards)


def _split_w_uq(w_uq_g):
    w = _unshard_cols(w_uq_g)
    K = w.shape[0]
    w = w.reshape(K, MLA_HEADS, QK_NOPE + QK_ROPE)
    return w[:, :, :QK_NOPE].reshape(K, MLA_HEADS * HEAD), _rope_pad(w[:, :, QK_NOPE:]).reshape(K, MLA_HEADS * HEAD)


def _merge_dw_uq(dw_n, dw_p):
    K = dw_n.shape[0]
    w = jnp.concatenate([dw_n.reshape(K, MLA_HEADS, HEAD), _rope_unpad(dw_p.reshape(K, MLA_HEADS, HEAD))], axis=-1)
    return _shard_cols(w.reshape(K, MLA_HEADS * (QK_NOPE + QK_ROPE)))


def _split_w_ukv(w_ukv_g):
    w = _unshard_cols(w_ukv_g)
    K = w.shape[0]
    w = w.reshape(K, MLA_HEADS, 2 * HEAD)
    return w[:, :, :HEAD].reshape(K, MLA_HEADS * HEAD), w[:, :, HEAD:].reshape(K, MLA_HEADS * HEAD)


def _merge_dw_ukv(dw_k, dw_v):
    K = dw_k.shape[0]
    w = jnp.concatenate([dw_k.reshape(K, MLA_HEADS, HEAD), dw_v.reshape(K, MLA_HEADS, HEAD)], axis=-1)
    return _shard_cols(w.reshape(K, MLA_HEADS * 2 * HEAD))


GATHER_GROUPS = (("w_in",), ("w_uq", "w_ukv", "w_o_mla", "w_o_dil", "w_out"), ("w_up", "w_down", "conv_w"))
SHARED_FETCH = ("w_in",)
REDUCE_GROUPS = (("w_down", "w_up"), ("w_out", "w_o_mla", "w_o_dil"), ("w_uq", "w_ukv", "w_in"))


def _local_step(x, tgt, W, fetch, emit):
    S, D = x.shape
    cos, sin_s = _rope_tables(S)
    w_lat, w_dil, w_gate = _split_w_in(fetch(0, x)["w_in"])

    h = _rmsnorm_fwd("attn_norm", x, W["attn_norm_g"])
    lat = _mm_nn("proj_lat", h, w_lat)
    qkv = [_mm_nn(f"proj_dil{g}", h, w_dil[g], o_dtype=BF) for g in range(DIL_GROUPS)]
    gpre = _mm_nn("proj_gate", h, w_gate, o_dtype=BF)
    WB = fetch(1, gpre)
    w_uqn, w_uqp = _split_w_uq(WB["w_uq"])
    w_k, w_v = _split_w_ukv(WB["w_ukv"])
    w_o_mla, w_o_dil = WB["w_o_mla"], WB["w_o_dil"]
    w_out = WB["w_out"].reshape(D, D)
    qn_, kvn, kpe = _mla_prep(lat, W["q_norm_g"], W["kv_norm_g"], cos, sin_s)
    q_nope = _mm_nn("q_nope", qn_, w_uqn, o_dtype=BF)
    q_pe = _rope("q_rope", _mm_nn("q_pe", qn_, w_uqp), cos, sin_s, False)
    k_nope = _mm_nn("k_nope", kvn, w_k, o_dtype=BF)
    v_mla = _mm_nn("v_mla", kvn, w_v, o_dtype=BF)
    attn_a, lse_a = _mla_fwd(q_nope, q_pe, k_nope, kpe, v_mla)
    dil = [_dil_fwd(qkv[g], g) for g in range(DIL_GROUPS)]
    attn_b, lse_b = _dil_combine([o for o, _ in dil], [l for _, l in dil])
    o_a = _mm_nn("o_mla", attn_a, w_o_mla, o_dtype=BF)
    o_b = _mm_nn("o_dil", attn_b, w_o_dil, o_dtype=BF)
    merge = _merge_fwd(gpre, W["b_gate"], o_a, o_b)
    x1 = _mm_nn("out_proj", merge, w_out, add=x)
    WC = fetch(2, merge)
    w_up = WC["w_up"]
    G4, _, C = w_up.shape
    w_down = WC["w_down"].reshape(G4 // 2, C, D)
    conv_w = WC["conv_w"]
    conv_b = W["conv_b"].reshape(G4, 1, C)
    h2 = _rmsnorm_fwd("ffn_norm", x1, W["ffn_norm_g"])
    u_pre = _up_fwd(h2, w_up)
    act, u = _ffn_act(u_pre, conv_w, conv_b)
    x2 = _down_fwd(act, w_down, x1)
    dx2, d_final_g, loss8 = _final_loss(x2, tgt, W["final_norm_g"])

    d_act = _down_dgrad(dx2, w_down)
    dw_down = _down_wgrad(act, dx2)
    du_pre, d_conv_w, d_conv_b = _ffn_act_conv_bwd(u, d_act, u_pre, conv_w)
    dh2 = _up_dgrad(du_pre, w_up)
    dw_up = _up_wgrad(h2, du_pre)
    zero = emit(0, {"w_down": dw_down.reshape(N_CHIPS, (G4 // 2) * C // N_CHIPS, D), "w_up": dw_up})
    dx1, d_ffn_g = _rmsnorm_bwd("ffn_norm_bwd", dh2, x1, W["ffn_norm_g"] + zero, dx2)
    d_merge = _mm_nt("out_proj_dgrad", dx1, w_out, o_dtype=BF)
    dw_out = _mm_tn("out_proj_wgrad", merge, dx1)
    d_oa, d_ob, d_gpre, d_b_gate = _merge_bwd(d_merge, gpre, W["b_gate"], o_a, o_b)
    d_attn_a = _mm_nt("o_mla_dgrad", d_oa, w_o_mla, o_dtype=BF)
    dw_o_mla = _mm_tn("o_mla_wgrad", attn_a, d_oa, shards=N_CHIPS)
    d_attn_b = _mm_nt("o_dil_dgrad", d_ob, w_o_dil, o_dtype=BF)
    dw_o_dil = _mm_tn("o_dil_wgrad", attn_b, d_ob, shards=N_CHIPS)
    zero = emit(1, {"w_out": dw_out.reshape(N_CHIPS, D // N_CHIPS, D), "w_o_mla": dw_o_mla, "w_o_dil": dw_o_dil})
    q_norm_g = W["q_norm_g"] + zero
    delta_b = _dil_delta(d_attn_b, attn_b)
    d_qkv = [_dil_bwd(qkv[g], d_attn_b, lse_b, delta_b, g) for g in range(DIL_GROUPS)]
    dq_nope, dq_pe_rot, dk_nope, dv_mla, dkpe_rot = _mla_bwd(q_nope, q_pe, k_nope, kpe, v_mla, d_attn_a, attn_a, lse_a)
    dq_pe = _rope("q_rope_bwd", dq_pe_rot, cos, sin_s, True)
    d_qn = _mm_nt_sum("q_dgrad", [dq_nope, dq_pe], [w_uqn, w_uqp])
    d_kvn = _mm_nt_sum("kv_dgrad", [dk_nope, dv_mla], [w_k, w_v])
    dw_uq = _merge_dw_uq(_mm_tn("q_nope_wgrad", qn_, dq_nope), _mm_tn("q_pe_wgrad", qn_, dq_pe))
    dw_ukv = _merge_dw_ukv(_mm_tn("k_nope_wgrad", kvn, dk_nope), _mm_tn("v_wgrad", kvn, dv_mla))
    d_lat, d_q_g, d_kv_g = _mla_prep_bwd(lat, q_norm_g, W["kv_norm_g"], cos, sin_s, d_qn, d_kvn, dkpe_rot)
    dw_in = _merge_dw_in(_mm_tn("proj_lat_wgrad", h, d_lat),
                         [_mm_tn(f"proj_dil{g}_wgrad", h, d_qkv[g]) for g in range(DIL_GROUPS)],
                         _mm_tn("proj_gate_wgrad", h, d_gpre))
    zero = emit(2, {"w_uq": dw_uq, "w_ukv": dw_ukv, "w_in": dw_in})
    dh = _mm_nt_sum("proj_lat_dil_dgrad", [d_lat] + d_qkv, [w_lat + zero.astype(BF)] + w_dil)
    dh = _mm_nt("proj_gate_dgrad", d_gpre, w_gate, add=dh)
    grad_x, d_attn_g = _rmsnorm_bwd("attn_norm_bwd", dh, x, W["attn_norm_g"], dx1)

    small = {"attn_norm_g": d_attn_g, "b_gate": d_b_gate, "q_norm_g": d_q_g, "kv_norm_g": d_kv_g,
             "ffn_norm_g": d_ffn_g, "conv_w": d_conv_w, "conv_b": d_conv_b.reshape(1, G4 * C),
             "final_norm_g": d_final_g}
    return loss8[0, 0], grad_x, small


BIG = ("w_in", "w_uq", "w_ukv", "w_o_mla", "w_o_dil", "w_out", "w_up", "w_down")
SMALL = ("attn_norm_g", "b_gate", "q_norm_g", "kv_norm_g", "ffn_norm_g", "conv_w", "conv_b", "final_norm_g")
WEIGHTS = ("attn_norm_g", "w_in", "b_gate", "q_norm_g", "w_uq", "kv_norm_g", "w_ukv", "w_o_mla", "w_o_dil",
           "w_out", "ffn_norm_g", "w_up", "conv_w", "conv_b", "w_down", "final_norm_g")
SMALL_ROWS = 8
COLUMN_MAJOR = ("w_in", "w_up")
HALF_AXIS = {"w_down": 1}


def _gather_start(shards):
    chip = 2 * lax.axis_index("x") + lax.axis_index("y")
    c = lax.axis_index("c")

    def prepare(names, zero):
        srcs, lands = [], []
        for n in names:
            if n in COLUMN_MAJOR:
                s = lax.optimization_barrier((shards[n].T + zero).astype(BF).T)
            else:
                s = shards[n] + zero
                s = s if n == "conv_w" else s.astype(BF)
            lands.append(lax.dynamic_update_slice(lax.empty((N_CHIPS,) + s.shape, s.dtype), s[None], (chip, 0, 0)))
            if n in SHARED_FETCH:
                s = lax.dynamic_slice_in_dim(s, c * (s.shape[0] // 2), s.shape[0] // 2, 0)
            srcs.append(s)
        return srcs, lands

    n0 = len(GATHER_GROUPS[0])
    srcs0, lands0 = prepare(GATHER_GROUPS[0], 0.0)
    sems0, srcs0, lands0, token = _exchange_start("gather_start0", "gather", srcs0, lands0, [list(range(n0))])
    srcs, lands = prepare([n for grp in GATHER_GROUPS[1:] for n in grp], token[0, 0])
    groups, at = [], 0
    for grp in GATHER_GROUPS[1:]:
        groups.append(list(range(at, at + len(grp))))
        at += len(grp)
    sems, srcs, lands, token1 = _exchange_start("gather_start1", "gather", srcs, lands, groups)

    def fetch(i, after):
        if i == 0:
            _, got = _exchange_wait("gather_wait0", "gather", srcs0, lands0, sems0[0], token1)
        else:
            idx = groups[i - 1]
            _, got = _exchange_wait(f"gather_wait{i}", "gather", [srcs[j] for j in idx], [lands[j] for j in idx],
                                    sems[i - 1], after)
        return {n: _pair_share(f"pair_share_{n}", g) if n in SHARED_FETCH else g for n, g in zip(GATHER_GROUPS[i], got)}

    return fetch, token[0, 0]


def _reduce_start(i, grads):
    names = REDUCE_GROUPS[i]
    hs = [_pair_sum(f"pair_sum_{n}", grads[n], HALF_AXIS.get(n, 0)) for n in names]
    lands = [lax.empty((3,) + h.shape[1:], h.dtype) for h in hs]
    sems, hs, lands, token = _exchange_start(f"reduce_start{i}", "scatter", hs, lands, [list(range(len(names)))])
    return (sems[0], hs, lands), token[0, 0]


def _reduce_finish(i, pending, after):
    sems, hs, lands = pending
    hs, lands = _exchange_wait(f"reduce_wait{i}", "scatter", hs, lands, sems, after)
    out = {}
    for n, h, landed in zip(REDUCE_GROUPS[i], hs, lands):
        out[n] = _chip_total_join(f"chip_total_{n}", h, landed, HALF_AXIS.get(n, 0))
    return out


def _reduce_small(small):
    names = tuple(small)
    flat = [small[n].reshape(-1) for n in names]
    sizes = [f.shape[0] for f in flat]
    total = sum(sizes)
    width = -(-total // (SMALL_ROWS * LANE)) * LANE
    packed = jnp.concatenate(flat + [jnp.zeros((SMALL_ROWS * width - total,), F32)]).reshape(SMALL_ROWS, width)
    red = _allreduce_small(packed).reshape(-1)
    out, off = {}, 0
    for n, s in zip(names, sizes):
        out[n] = red[off:off + s]
        off += s
    return out


def kernel(x, attn_norm_g, w_in, b_gate, q_norm_g, w_uq, kv_norm_g, w_ukv, w_o_mla, w_o_dil, w_out, ffn_norm_g, w_up, conv_w, conv_b, w_down, final_norm_g, loss_target, m_attn_norm_g, m_w_in, m_b_gate, m_q_norm_g, m_w_uq, m_kv_norm_g, m_w_ukv, m_w_o_mla, m_w_o_dil, m_w_out, m_ffn_norm_g, m_w_up, m_conv_w, m_conv_b, m_w_down, m_final_norm_g, v_attn_norm_g, v_w_in, v_b_gate, v_q_norm_g, v_w_uq, v_kv_norm_g, v_w_ukv, v_w_o_mla, v_w_o_dil, v_w_out, v_ffn_norm_g, v_w_up, v_conv_w, v_conv_b, v_w_down, v_final_norm_g):
    given = dict(attn_norm_g=attn_norm_g, w_in=w_in, b_gate=b_gate, q_norm_g=q_norm_g, w_uq=w_uq, kv_norm_g=kv_norm_g,
                 w_ukv=w_ukv, w_o_mla=w_o_mla, w_o_dil=w_o_dil, w_out=w_out, ffn_norm_g=ffn_norm_g, w_up=w_up,
                 conv_w=conv_w, conv_b=conv_b, w_down=w_down, final_norm_g=final_norm_g)
    moments_m = dict(attn_norm_g=m_attn_norm_g, w_in=m_w_in, b_gate=m_b_gate, q_norm_g=m_q_norm_g, w_uq=m_w_uq,
                     kv_norm_g=m_kv_norm_g, w_ukv=m_w_ukv, w_o_mla=m_w_o_mla, w_o_dil=m_w_o_dil, w_out=m_w_out,
                     ffn_norm_g=m_ffn_norm_g, w_up=m_w_up, conv_w=m_conv_w, conv_b=m_conv_b, w_down=m_w_down,
                     final_norm_g=m_final_norm_g)
    moments_v = dict(attn_norm_g=v_attn_norm_g, w_in=v_w_in, b_gate=v_b_gate, q_norm_g=v_q_norm_g, w_uq=v_w_uq,
                     kv_norm_g=v_kv_norm_g, w_ukv=v_w_ukv, w_o_mla=v_w_o_mla, w_o_dil=v_w_o_dil, w_out=v_w_out,
                     ffn_norm_g=v_ffn_norm_g, w_up=v_w_up, conv_w=v_conv_w, conv_b=v_conv_b, w_down=v_w_down,
                     final_norm_g=v_final_norm_g)

    fetch, zero = _gather_start({n: given[n][0] for n in BIG + ("conv_w",)})
    W = {n: given[n] for n in ("b_gate", "q_norm_g", "kv_norm_g", "ffn_norm_g", "conv_b")}
    W["attn_norm_g"] = given["attn_norm_g"] + zero
    W["final_norm_g"] = given["final_norm_g"].reshape(1, -1)

    pending = {}

    def emit(i, grads):
        pending[i], token = _reduce_start(i, grads)
        return token

    loss_part, grad_x, small = _local_step(x[0], loss_target[0], W, fetch, emit)
    small["loss"] = loss_part
    grads, delta, new_m, new_v = {}, {}, {}, {}

    def adamw(n, g):
        shp = given[n].shape
        two_d = (-1, shp[-1]) if len(shp) > 1 else (1, -1)
        view = (lambda a: a.reshape(two_d).T) if n in COLUMN_MAJOR else (lambda a: a.reshape(two_d))
        back = (lambda a: a.T.reshape(shp)) if n in COLUMN_MAJOR else (lambda a: a.reshape(shp))
        go, d, nm, nv = _adamw(f"adamw_{n}", view(given[n]), view(g), view(moments_m[n]), view(moments_v[n]))
        grads[n], delta[n], new_m[n], new_v[n] = back(go), back(d), back(nm), back(nv)

    after = grad_x
    for i in range(len(REDUCE_GROUPS)):
        for n, g in _reduce_finish(i, pending[i], after).items():
            adamw(n, g)
        after = delta[REDUCE_GROUPS[i][-1]]
    g_small = _reduce_small(small)
    loss = g_small["loss"][0]
    chip = 2 * lax.axis_index("x") + lax.axis_index("y")
    for n in SMALL:
        if n == "conv_w":
            full = g_small[n].reshape(N_CHIPS, 3, -1)
            adamw(n, lax.dynamic_index_in_dim(full, chip, 0, keepdims=True))
        else:
            adamw(n, g_small[n])

    return (loss, grad_x[None], *[grads[n] for n in WEIGHTS], *[delta[n] for n in WEIGHTS],
            *[new_m[n] for n in WEIGHTS], *[new_v[n] for n in WEIGHTS])
```

```python
import functools
import math

import numpy as np
import jax
import jax.numpy as jnp
from jax import lax
from jax.experimental import pallas as pl
from jax.experimental.pallas import tpu as pltpu

F32 = jnp.float32
BF = jnp.bfloat16
MESH = pl.DeviceIdType.MESH

D_MODEL = 2048
MLA_HEADS = 8
QK_NOPE = 128
QK_ROPE = 64
Q_RANK = 512
KV_RANK = 256
ROPE_THETA = 10000.0
DIL_PATTERNS = ((128, 1), (512, 4), (2048, 16))
DIL_GROUPS = 3
DIL_HPG = 4
DIL_HEADS = 12
HEAD = 128
DIL_BLOCK = 128
ALIBI_MAX_BIAS = 8.0
NORM_EPS = 1e-6
N_CHIPS = 4
ADAM_LR = 0.001
ADAM_B1 = 0.9
ADAM_B2 = 0.999
ADAM_EPS = 1e-08
ADAM_WD = 0.01
ADAM_STEP = 10

LANE = 128
VMEM_LIMIT = 56 * 1024 * 1024
MLA_SCALE = (QK_NOPE + QK_ROPE) ** -0.5
DIL_SCALE = HEAD ** -0.5


def _params(*sem):
    return pltpu.CompilerParams(dimension_semantics=sem, vmem_limit_bytes=VMEM_LIMIT)


def _tile(n, pref):
    t = (pref // LANE) * LANE
    while t >= LANE:
        if n % t == 0:
            return t
        t -= LANE
    return n


NN = (((1,), (0,)), ((), ()))
NT = (((1,), (1,)), ((), ()))
TN = (((0,), (0,)), ((), ()))


def _mm_call(name, a, b, add, *, grid, a_spec, b_spec, add_spec, o_spec, o_shape, o_dtype, acc_shape, dims, nk):
    nax = len(grid)

    def body(*refs):
        if add is None:
            a_ref, b_ref, o_ref = refs[:3]
            c_ref = None
            scr = refs[3:]
        else:
            a_ref, b_ref, c_ref, o_ref = refs[:4]
            scr = refs[4:]
        prod = lax.dot_general(a_ref[...].astype(BF), b_ref[...].astype(BF), dims, preferred_element_type=F32)
        if nk == 1:
            if c_ref is not None:
                prod = prod + c_ref[...]
            o_ref[...] = prod.astype(o_ref.dtype)
        else:
            acc = scr[0]
            k = pl.program_id(nax - 1)

            @pl.when(k == 0)
            def _():
                if c_ref is not None:
                    acc[...] = prod + c_ref[...]
                else:
                    acc[...] = prod

            @pl.when(k > 0)
            def _():
                acc[...] += prod

            @pl.when(k == nk - 1)
            def _():
                o_ref[...] = acc[...].astype(o_ref.dtype)

    ins = [a, b] + ([] if add is None else [add])
    specs = [a_spec, b_spec] + ([] if add is None else [add_spec])
    sem = ("parallel",) * (nax - 1) + ("arbitrary",)
    return pl.pallas_call(
        body, name=name, grid=grid, in_specs=specs, out_specs=o_spec,
        out_shape=jax.ShapeDtypeStruct(o_shape, o_dtype),
        scratch_shapes=[] if nk == 1 else [pltpu.VMEM(acc_shape, F32)],
        compiler_params=_params(*sem),
    )(*ins)


def _mm_nn(name, a, b, *, add=None, o_dtype=F32):
    M, K = a.shape
    sharded = b.ndim == 3
    Ns = b.shape[-1]
    N = Ns * (b.shape[0] if sharded else 1)
    tm, tn, tk = _tile(M, 1024), _tile(Ns, 1024), _tile(K, 2048)
    per = Ns // tn
    nk = K // tk
    if sharded:
        b_spec = pl.BlockSpec((None, tk, tn), lambda i, j, k: (j // per, k, j % per))
    else:
        b_spec = pl.BlockSpec((tk, tn), lambda i, j, k: (k, j))
    return _mm_call(
        name, a, b, add, grid=(M // tm, N // tn, nk),
        a_spec=pl.BlockSpec((tm, tk), lambda i, j, k: (i, k)), b_spec=b_spec,
        add_spec=pl.BlockSpec((tm, tn), lambda i, j, k: (i, j)),
        o_spec=pl.BlockSpec((tm, tn), lambda i, j, k: (i, j)),
        o_shape=(M, N), o_dtype=o_dtype, acc_shape=(tm, tn), dims=NN, nk=nk)


def _mm_nt(name, a, b, *, add=None, o_dtype=F32):
    M, K = a.shape
    sharded = b.ndim == 3
    N, Ks = b.shape[-2], b.shape[-1]
    tm, tn, tk = _tile(M, 1024), _tile(N, 1024), _tile(Ks, 2048)
    per = Ks // tk
    nk = K // tk
    if sharded:
        b_spec = pl.BlockSpec((None, tn, tk), lambda i, j, k: (k // per, j, k % per))
    else:
        b_spec = pl.BlockSpec((tn, tk), lambda i, j, k: (j, k))
    return _mm_call(
        name, a, b, add, grid=(M // tm, N // tn, nk),
        a_spec=pl.BlockSpec((tm, tk), lambda i, j, k: (i, k)), b_spec=b_spec,
        add_spec=pl.BlockSpec((tm, tn), lambda i, j, k: (i, j)),
        o_spec=pl.BlockSpec((tm, tn), lambda i, j, k: (i, j)),
        o_shape=(M, N), o_dtype=o_dtype, acc_shape=(tm, tn), dims=NT, nk=nk)


def _mm_nt_sum(name, a_s, b_s):
    n = len(a_s)
    M, N = a_s[0].shape[0], b_s[0].shape[0]
    tm, tn = _tile(M, 512), _tile(N, 1024)

    def body(*refs):
        acc = None
        for i in range(n):
            prod = lax.dot_general(refs[i][...].astype(BF), refs[n + i][...].astype(BF), NT, preferred_element_type=F32)
            acc = prod if acc is None else acc + prod
        refs[2 * n][...] = acc

    return pl.pallas_call(
        body, name=name, grid=(N // tn, M // tm),
        in_specs=[pl.BlockSpec((tm, a.shape[1]), lambda j, i: (i, 0)) for a in a_s]
        + [pl.BlockSpec((tn, b.shape[1]), lambda j, i: (j, 0)) for b in b_s],
        out_specs=pl.BlockSpec((tm, tn), lambda j, i: (i, j)),
        out_shape=jax.ShapeDtypeStruct((M, N), F32),
        compiler_params=_params("parallel", "parallel"))(*a_s, *b_s)


def _mm_tn(name, a, b, *, shards=1, o_dtype=BF):
    S, M = a.shape
    N = b.shape[1]
    Ns = N // shards
    tm, tn, tk = _tile(M, 1024), _tile(Ns, 1024), _tile(S, 2048)
    per = Ns // tn
    nk = S // tk
    if shards > 1:
        o_spec = pl.BlockSpec((None, tm, tn), lambda i, j, k: (j // per, i, j % per))
        o_shape = (shards, M, Ns)
    else:
        o_spec = pl.BlockSpec((tm, tn), lambda i, j, k: (i, j))
        o_shape = (M, N)
    return _mm_call(
        name, a, b, None, grid=(M // tm, N // tn, nk),
        a_spec=pl.BlockSpec((tk, tm), lambda i, j, k: (k, i)),
        b_spec=pl.BlockSpec((tk, tn), lambda i, j, k: (k, j)),
        add_spec=None, o_spec=o_spec, o_shape=o_shape, o_dtype=o_dtype, acc_shape=(tm, tn), dims=TN, nk=nk)


def _up_fwd(h2, w_up):
    S, D = h2.shape
    G, _, C = w_up.shape
    tm = _tile(S, 512)
    return _mm_call(
        "up_fwd", h2, w_up, None, grid=(G, S // tm, 1),
        a_spec=pl.BlockSpec((tm, D), lambda g, i, k: (i, 0)),
        b_spec=pl.BlockSpec((None, D, C), lambda g, i, k: (g, 0, 0)),
        add_spec=None, o_spec=pl.BlockSpec((None, tm, C), lambda g, i, k: (g, i, 0)),
        o_shape=(G, S, C), o_dtype=BF, acc_shape=None, dims=NN, nk=1)


def _up_dgrad(du_pre, w_up):
    G, S, C = du_pre.shape
    D = w_up.shape[1]
    tm, tn = _tile(S, 1024), _tile(D, 1024)
    return _mm_call(
        "up_dgrad", du_pre, w_up, None, grid=(S // tm, D // tn, G),
        a_spec=pl.BlockSpec((None, tm, C), lambda i, j, g: (g, i, 0)),
        b_spec=pl.BlockSpec((None, tn, C), lambda i, j, g: (g, j, 0)),
        add_spec=None, o_spec=pl.BlockSpec((tm, tn), lambda i, j, g: (i, j)),
        o_shape=(S, D), o_dtype=F32, acc_shape=(tm, tn), dims=NT, nk=G)


def _up_wgrad(h2, du_pre):
    G, S, C = du_pre.shape
    D = h2.shape[1]
    tm, tk = _tile(D, 512), _tile(S, 2048)
    return _mm_call(
        "up_wgrad", h2, du_pre, None, grid=(G, D // tm, S // tk),
        a_spec=pl.BlockSpec((tk, tm), lambda g, i, k: (k, i)),
        b_spec=pl.BlockSpec((None, tk, C), lambda g, i, k: (g, k, 0)),
        add_spec=None, o_spec=pl.BlockSpec((None, tm, C), lambda g, i, k: (g, i, 0)),
        o_shape=(G, D, C), o_dtype=BF, acc_shape=(tm, C), dims=TN, nk=S // tk)


def _down_fwd(act, w_down, x1):
    G, S, C = act.shape
    D = w_down.shape[2]
    tm, tn = _tile(S, 1024), _tile(D, 1024)
    return _mm_call(
        "down_fwd", act, w_down, x1, grid=(S // tm, D // tn, G),
        a_spec=pl.BlockSpec((None, tm, C), lambda i, j, g: (g, i, 0)),
        b_spec=pl.BlockSpec((None, C, tn), lambda i, j, g: (g, 0, j)),
        add_spec=pl.BlockSpec((tm, tn), lambda i, j, g: (i, j)),
        o_spec=pl.BlockSpec((tm, tn), lambda i, j, g: (i, j)),
        o_shape=(S, D), o_dtype=F32, acc_shape=(tm, tn), dims=NN, nk=G)


def _down_dgrad(dx2, w_down):
    S, D = dx2.shape
    G, C, _ = w_down.shape
    tm = _tile(S, 512)
    return _mm_call(
        "down_dgrad", dx2, w_down, None, grid=(G, S // tm, 1),
        a_spec=pl.BlockSpec((tm, D), lambda g, i, k: (i, 0)),
        b_spec=pl.BlockSpec((None, C, D), lambda g, i, k: (g, 0, 0)),
        add_spec=None, o_spec=pl.BlockSpec((None, tm, C), lambda g, i, k: (g, i, 0)),
        o_shape=(G, S, C), o_dtype=BF, acc_shape=None, dims=NT, nk=1)


def _down_wgrad(act, dx2):
    G, S, C = act.shape
    D = dx2.shape[1]
    tn, tk = _tile(D, 512), _tile(S, 1024)
    return _mm_call(
        "down_wgrad", act, dx2, None, grid=(G, D // tn, S // tk),
        a_spec=pl.BlockSpec((None, tk, C), lambda g, j, k: (g, k, 0)),
        b_spec=pl.BlockSpec((tk, tn), lambda g, j, k: (k, j)),
        add_spec=None, o_spec=pl.BlockSpec((None, C, tn), lambda g, j, k: (g, 0, j)),
        o_shape=(G, C, D), o_dtype=BF, acc_shape=(C, tn), dims=TN, nk=S // tk)


def _row(ts, c):
    return pl.BlockSpec((ts, c), lambda i: (i, 0))


def _bcast(r, c):
    return pl.BlockSpec((r, c), lambda i: (0, 0))


def _accumulate(i, ref, val):
    @pl.when(i == 0)
    def _():
        ref[...] = val

    @pl.when(i > 0)
    def _():
        ref[...] += val


def _rstd(xv):
    return lax.rsqrt(jnp.mean(xv * xv, axis=-1, keepdims=True) + NORM_EPS)


def _rmsnorm_fwd(name, x, g):
    S, D = x.shape
    ts = _tile(S, 512)

    def body(x_ref, g_ref, o_ref):
        xv = x_ref[...]
        o_ref[...] = (xv * _rstd(xv) * g_ref[...]).astype(o_ref.dtype)

    return pl.pallas_call(
        body, name=name, grid=(S // ts,), in_specs=[_row(ts, D), _bcast(1, D)], out_specs=_row(ts, D),
        out_shape=jax.ShapeDtypeStruct((S, D), BF), compiler_params=_params("parallel"))(x, g)


def _norm_bwd_rows(dy, xv, g):
    r = _rstd(xv)
    xh = xv * r
    dxh = dy * g
    dx = r * (dxh - xh * jnp.mean(dxh * xh, axis=-1, keepdims=True))
    return dx, jnp.sum(dy * xh, axis=0, keepdims=True)


def _rmsnorm_bwd(name, dy, x, g, res):
    S, D = x.shape
    ts = _tile(S, 512)

    def body(dy_ref, x_ref, g_ref, res_ref, dx_ref, dg_ref):
        dx, dg = _norm_bwd_rows(dy_ref[...], x_ref[...], g_ref[...])
        dx_ref[...] = dx + res_ref[...]
        _accumulate(pl.program_id(0), dg_ref, dg)

    return pl.pallas_call(
        body, name=name, grid=(S // ts,),
        in_specs=[_row(ts, D), _row(ts, D), _bcast(1, D), _row(ts, D)],
        out_specs=[_row(ts, D), _bcast(1, D)],
        out_shape=[jax.ShapeDtypeStruct((S, D), F32), jax.ShapeDtypeStruct((1, D), F32)],
        compiler_params=_params("arbitrary"))(dy, x, g, res)


def _rope_tables(S):
    half = QK_ROPE // 2
    pos = jnp.arange(S, dtype=F32)
    inv_freq = ROPE_THETA ** (-jnp.arange(0, QK_ROPE, 2, dtype=F32) / QK_ROPE)
    ang = pos[:, None] * inv_freq[None, :]
    cos, sin = jnp.cos(ang), jnp.sin(ang)
    z = jnp.zeros((S, half), F32)
    return jnp.concatenate([cos, z, cos, z], axis=1), jnp.concatenate([-sin, z, sin, z], axis=1)


def _rope_lanes(x, cos, sin_signed, inverse):
    if inverse:
        return x * cos + pltpu.roll(x * sin_signed, LANE // 2, 1)
    return x * cos + pltpu.roll(x, LANE // 2, 1) * sin_signed


def _rope(name, x, cos, sin_signed, inverse):
    S, W = x.shape
    ts = _tile(S, 512)

    def body(x_ref, c_ref, s_ref, o_ref):
        c, s = c_ref[...], s_ref[...]
        for h in range(W // LANE):
            sl = slice(h * LANE, (h + 1) * LANE)
            o_ref[:, sl] = _rope_lanes(x_ref[:, sl], c, s, inverse).astype(o_ref.dtype)

    return pl.pallas_call(
        body, name=name, grid=(S // ts,), in_specs=[_row(ts, W), _row(ts, LANE), _row(ts, LANE)],
        out_specs=_row(ts, W), out_shape=jax.ShapeDtypeStruct((S, W), BF),
        compiler_params=_params("parallel"))(x, cos, sin_signed)


LAT_W = 1024
_CQ = slice(0, Q_RANK)
_CKV = slice(Q_RANK, Q_RANK + KV_RANK)
_KPE = slice(Q_RANK + KV_RANK, Q_RANK + KV_RANK + LANE)


def _mla_prep(lat, qg, kvg, cos, sin_signed):
    S = lat.shape[0]
    ts = _tile(S, 512)

    def body(lat_ref, qg_ref, kvg_ref, c_ref, s_ref, qn_ref, kvn_ref, kpe_ref):
        cq = lat_ref[:, _CQ]
        qn_ref[...] = (cq * _rstd(cq) * qg_ref[...]).astype(BF)
        ckv = lat_ref[:, _CKV]
        kvn_ref[...] = (ckv * _rstd(ckv) * kvg_ref[...]).astype(BF)
        kpe_ref[...] = _rope_lanes(lat_ref[:, _KPE], c_ref[...], s_ref[...], False).astype(BF)

    return pl.pallas_call(
        body, name="mla_prep", grid=(S // ts,),
        in_specs=[_row(ts, LAT_W), _bcast(1, Q_RANK), _bcast(1, KV_RANK), _row(ts, LANE), _row(ts, LANE)],
        out_specs=[_row(ts, Q_RANK), _row(ts, KV_RANK), _row(ts, LANE)],
        out_shape=[jax.ShapeDtypeStruct((S, Q_RANK), BF), jax.ShapeDtypeStruct((S, KV_RANK), BF),
                   jax.ShapeDtypeStruct((S, LANE), BF)],
        compiler_params=_params("parallel"))(lat, qg, kvg, cos, sin_signed)


def _mla_prep_bwd(lat, qg, kvg, cos, sin_signed, d_qn, d_kvn, d_kpe):
    S = lat.shape[0]
    ts = _tile(S, 512)

    def body(lat_ref, qg_ref, kvg_ref, c_ref, s_ref, dqn_ref, dkvn_ref, dkpe_ref, dlat_ref, dqg_ref, dkvg_ref):
        i = pl.program_id(0)
        dcq, dqg = _norm_bwd_rows(dqn_ref[...], lat_ref[:, _CQ], qg_ref[...])
        dckv, dkvg = _norm_bwd_rows(dkvn_ref[...], lat_ref[:, _CKV], kvg_ref[...])
        dlat_ref[:, _CQ] = dcq.astype(BF)
        dlat_ref[:, _CKV] = dckv.astype(BF)
        dkpe = dkpe_ref[0]
        for g in range(1, d_kpe.shape[0]):
            dkpe = dkpe + dkpe_ref[g]
        dlat_ref[:, _KPE] = _rope_lanes(dkpe, c_ref[...], s_ref[...], True).astype(BF)
        dlat_ref[:, _KPE.stop:] = jnp.zeros((ts, LAT_W - _KPE.stop), BF)
        _accumulate(i, dqg_ref, dqg)
        _accumulate(i, dkvg_ref, dkvg)

    return pl.pallas_call(
        body, name="mla_prep_bwd", grid=(S // ts,),
        in_specs=[_row(ts, LAT_W), _bcast(1, Q_RANK), _bcast(1, KV_RANK), _row(ts, LANE), _row(ts, LANE),
                  _row(ts, Q_RANK), _row(ts, KV_RANK), pl.BlockSpec((d_kpe.shape[0], ts, LANE), lambda i: (0, i, 0))],
        out_specs=[_row(ts, LAT_W), _bcast(1, Q_RANK), _bcast(1, KV_RANK)],
        out_shape=[jax.ShapeDtypeStruct((S, LAT_W), BF), jax.ShapeDtypeStruct((1, Q_RANK), F32),
                   jax.ShapeDtypeStruct((1, KV_RANK), F32)],
        compiler_params=_params("arbitrary"))(lat, qg, kvg, cos, sin_signed, d_qn, d_kvn, d_kpe)


def _sigmoid(z):
    return 1.0 / (1.0 + jnp.exp(-z))


def _merge_fwd(gpre, b_gate, o_a, o_b):
    S, D = o_a.shape
    ts = _tile(S, 256)

    def body(g_ref, b_ref, oa_ref, ob_ref, m_ref):
        ga = _sigmoid(g_ref[:, :D] + b_ref[:, :D])
        gb = _sigmoid(g_ref[:, D:] + b_ref[:, D:])
        m_ref[...] = (ga * oa_ref[...] + gb * ob_ref[...]).astype(BF)

    return pl.pallas_call(
        body, name="merge_fwd", grid=(S // ts,),
        in_specs=[_row(ts, 2 * D), _bcast(1, 2 * D), _row(ts, D), _row(ts, D)], out_specs=_row(ts, D),
        out_shape=jax.ShapeDtypeStruct((S, D), BF), compiler_params=_params("parallel"))(gpre, b_gate, o_a, o_b)


def _merge_bwd(d_merge, gpre, b_gate, o_a, o_b):
    S, D = o_a.shape
    ts = _tile(S, 256)

    def body(dm_ref, g_ref, b_ref, oa_ref, ob_ref, doa_ref, dob_ref, dg_ref, db_ref):
        dm = dm_ref[...]
        ga = _sigmoid(g_ref[:, :D] + b_ref[:, :D])
        gb = _sigmoid(g_ref[:, D:] + b_ref[:, D:])
        doa_ref[...] = (dm * ga).astype(BF)
        dob_ref[...] = (dm * gb).astype(BF)
        dga = dm * oa_ref[...] * ga * (1.0 - ga)
        dgb = dm * ob_ref[...] * gb * (1.0 - gb)
        dg_ref[:, :D] = dga.astype(BF)
        dg_ref[:, D:] = dgb.astype(BF)
        i = pl.program_id(0)
        part = jnp.concatenate([jnp.sum(dga, axis=0, keepdims=True), jnp.sum(dgb, axis=0, keepdims=True)], axis=1)
        _accumulate(i, db_ref, part)

    return pl.pallas_call(
        body, name="merge_bwd", grid=(S // ts,),
        in_specs=[_row(ts, D), _row(ts, 2 * D), _bcast(1, 2 * D), _row(ts, D), _row(ts, D)],
        out_specs=[_row(ts, D), _row(ts, D), _row(ts, 2 * D), _bcast(1, 2 * D)],
        out_shape=[jax.ShapeDtypeStruct((S, D), BF), jax.ShapeDtypeStruct((S, D), BF),
                   jax.ShapeDtypeStruct((S, 2 * D), BF), jax.ShapeDtypeStruct((1, 2 * D), F32)],
        compiler_params=_params("arbitrary"))(d_merge, gpre, b_gate, o_a, o_b)


def _final_loss(x2, tgt, gf):
    S, D = x2.shape
    ts = _tile(S, 512)

    def body(x_ref, t_ref, g_ref, dx_ref, dg_ref, loss_ref):
        i = pl.program_id(0)
        xv = x_ref[...]
        g = g_ref[...]
        y = xv * _rstd(xv) * g
        err = y - t_ref[...]
        dx, dg = _norm_bwd_rows(err * (1.0 / D), xv, g)
        dx_ref[...] = dx
        _accumulate(i, dg_ref, dg)
        part = 0.5 * jnp.sum(jnp.mean(err * err, axis=-1, keepdims=True), axis=0, keepdims=True)
        _accumulate(i, loss_ref, jnp.broadcast_to(part, (8, LANE)))

    return pl.pallas_call(
        body, name="final_loss", grid=(S // ts,),
        in_specs=[_row(ts, D), _row(ts, D), _bcast(1, D)],
        out_specs=[_row(ts, D), _bcast(1, D), _bcast(8, LANE)],
        out_shape=[jax.ShapeDtypeStruct((S, D), F32), jax.ShapeDtypeStruct((1, D), F32),
                   jax.ShapeDtypeStruct((8, LANE), F32)],
        compiler_params=_params("arbitrary"))(x2, tgt, gf)


HALO = 16


SUB = 8


def _shift_down(cur, prev, k, rows):
    out = pltpu.roll(cur, k, 0)
    head = out[:SUB]
    for j in range(k):
        head = jnp.where(rows == j, prev[HALO - k + j:HALO - k + j + 1, :], head)
    return jnp.concatenate([head, out[SUB:]], axis=0)


def _shift_up(cur, nxt, k, rows, ts):
    out = pltpu.roll(cur, ts - k, 0)
    tail = out[ts - SUB:]
    for j in range(k):
        tail = jnp.where(rows == SUB - k + j, nxt[j:j + 1, :], tail)
    return jnp.concatenate([out[:ts - SUB], tail], axis=0)


def _conv_rows(cur, prev, w, b, rows):
    return b + w[0:1, :] * _shift_down(cur, prev, 2, rows) + w[1:2, :] * _shift_down(cur, prev, 1, rows) + w[2:3, :] * cur


def _conv_specs(ts, C, shard_of):
    nh = ts // HALO
    cur = pl.BlockSpec((None, ts, C), lambda g, i: (shard_of(g), i, 0))
    prev = pl.BlockSpec((None, HALO, C), lambda g, i: (shard_of(g), jnp.maximum(i * nh - 1, 0), 0))
    return cur, prev


def _ffn_act(u_pre, conv_w, conv_b):
    G4, S, C = u_pre.shape
    G = G4 // 2
    ts = _tile(S, 256)

    def body(up_ref, upp_ref, gt_ref, gtp_ref, wu_ref, wg_ref, bu_ref, bg_ref, act_ref, u_ref):
        first = pl.program_id(1) == 0
        rows = lax.broadcasted_iota(jnp.int32, (SUB, C), 0)
        pu = jnp.where(first, 0.0, upp_ref[...].astype(F32))
        pg = jnp.where(first, 0.0, gtp_ref[...].astype(F32))
        up = _conv_rows(up_ref[...].astype(F32), pu, wu_ref[...], bu_ref[...], rows)
        gate = _conv_rows(gt_ref[...].astype(F32), pg, wg_ref[...], bg_ref[...], rows)
        act_ref[...] = (gate * _sigmoid(gate) * up).astype(BF)
        u_ref[0] = up.astype(BF)
        u_ref[1] = gate.astype(BF)

    cur_u, prev_u = _conv_specs(ts, C, lambda g: g)
    cur_g, prev_g = _conv_specs(ts, C, lambda g: g + G)
    w_u = pl.BlockSpec((None, 3, C), lambda g, i: (g, 0, 0))
    w_g = pl.BlockSpec((None, 3, C), lambda g, i: (g + G, 0, 0))
    b_u = pl.BlockSpec((None, 1, C), lambda g, i: (g, 0, 0))
    b_g = pl.BlockSpec((None, 1, C), lambda g, i: (g + G, 0, 0))
    pair = pl.BlockSpec((2, None, ts, C), lambda g, i: (0, g, i, 0))
    act, u = pl.pallas_call(
        body, name="ffn_act", grid=(G, S // ts),
        in_specs=[cur_u, prev_u, cur_g, prev_g, w_u, w_g, b_u, b_g],
        out_specs=[pl.BlockSpec((None, ts, C), lambda g, i: (g, i, 0)), pair],
        out_shape=[jax.ShapeDtypeStruct((G, S, C), BF), jax.ShapeDtypeStruct((2, G, S, C), BF)],
        compiler_params=_params("parallel", "parallel"))(u_pre, u_pre, u_pre, u_pre, conv_w, conv_w, conv_b, conv_b)
    return act, u


def _ffn_act_conv_bwd(u, d_act, u_pre, conv_w):
    _, G, S, C = u.shape
    ts = _tile(S, 256)
    nh = ts // HALO
    last_halo = S // HALO - 1

    def d_outputs(up, gate, da):
        sg = _sigmoid(gate)
        return da * (gate * sg), da * up * (sg * (1.0 + gate * (1.0 - sg)))

    def body(u_ref, un_ref, da_ref, dan_ref, x_ref, w_ref, dpre_ref, dw_ref, db_ref):
        i = pl.program_id(1)
        rows = lax.broadcasted_iota(jnp.int32, (SUB, C), 0)
        da_n = jnp.where(i == pl.num_programs(1) - 1, 0.0, dan_ref[...].astype(F32))
        du = d_outputs(u_ref[0].astype(F32), u_ref[1].astype(F32), da_ref[...].astype(F32))
        du_n = d_outputs(un_ref[0].astype(F32), un_ref[1].astype(F32), da_n)
        for kind in range(2):
            du_c, w = du[kind], w_ref[kind]
            up1, up2 = _shift_up(du_c, du_n[kind], 1, rows, ts), _shift_up(du_c, du_n[kind], 2, rows, ts)
            dpre_ref[kind] = (w[2:3, :] * du_c + w[1:2, :] * up1 + w[0:1, :] * up2).astype(BF)
            x_c = x_ref[kind].astype(F32)
            dw = jnp.concatenate([
                jnp.sum(up2 * x_c, axis=0, keepdims=True),
                jnp.sum(up1 * x_c, axis=0, keepdims=True),
                jnp.sum(du_c * x_c, axis=0, keepdims=True)], axis=0)
            db = jnp.sum(du_c, axis=0, keepdims=True)

            @pl.when(i == 0)
            def _():
                dw_ref[kind] = dw
                db_ref[kind] = db

            @pl.when(i > 0)
            def _():
                dw_ref[kind] += dw
                db_ref[kind] += db

    def nxt_row(i):
        return jnp.minimum((i + 1) * nh, last_halo)

    pair = pl.BlockSpec((2, None, ts, C), lambda g, i: (0, g, i, 0))
    pair_n = pl.BlockSpec((2, None, HALO, C), lambda g, i: (0, g, nxt_row(i), 0))
    one = pl.BlockSpec((None, ts, C), lambda g, i: (g, i, 0))
    one_n = pl.BlockSpec((None, HALO, C), lambda g, i: (g, nxt_row(i), 0))
    w_spec = pl.BlockSpec((2, None, 3, C), lambda g, i: (0, g, 0, 0))
    b_spec = pl.BlockSpec((2, None, 1, C), lambda g, i: (0, g, 0, 0))
    dpre, dw, db = pl.pallas_call(
        body, name="ffn_act_conv_bwd", grid=(G, S // ts),
        in_specs=[pair, pair_n, one, one_n, pair, w_spec], out_specs=[pair, w_spec, b_spec],
        out_shape=[jax.ShapeDtypeStruct((2, G, S, C), BF), jax.ShapeDtypeStruct((2, G, 3, C), F32),
                   jax.ShapeDtypeStruct((2, G, 1, C), F32)],
        compiler_params=_params("parallel", "arbitrary"))(
            u, u, d_act, d_act, u_pre.reshape(2, G, S, C), conv_w.reshape(2, G, 3, C))
    return dpre.reshape(2 * G, S, C), dw.reshape(2 * G, 3, C), db.reshape(2 * G, 1, C)


MLA_T = 1024
MLA_HB = 4
MLA_BWD_HB = 2


def _mla_pairs(n, by_row):
    if by_row:
        pairs = [(i, j) for i in range(n) for j in range(i + 1)]
    else:
        pairs = [(i, j) for j in range(n) for i in range(j, n)]
    return jnp.asarray([p[0] for p in pairs], jnp.int32), jnp.asarray([p[1] for p in pairs], jnp.int32)


def _mla_specs(hb):
    q = pl.BlockSpec((MLA_T, hb * HEAD), lambda g, t, it, jt: (it[t], g))
    k = pl.BlockSpec((MLA_T, hb * HEAD), lambda g, t, it, jt: (jt[t], g))
    kpe = pl.BlockSpec((MLA_T, HEAD), lambda g, t, it, jt: (jt[t], 0))
    lse = pl.BlockSpec((hb, MLA_T, LANE), lambda g, t, it, jt: (g, it[t], 0))
    return q, k, kpe, lse


def _mla_head(ref, hh):
    return ref[:, hh * HEAD:(hh + 1) * HEAD]


LOG2E = math.log2(math.e)
MLA_EXP2_SCALE = MLA_SCALE * LOG2E


def _mla_scores(qn_ref, qp_ref, kn_ref, kpe, hh, ok):
    q = jnp.concatenate([_mla_head(qn_ref, hh), _mla_head(qp_ref, hh)], axis=1)
    k = jnp.concatenate([_mla_head(kn_ref, hh), kpe], axis=1)
    s = lax.dot_general(q, k, NT, preferred_element_type=F32)
    return q, k, s if ok is None else jnp.where(ok, s, -jnp.inf)


def _mla_diagonal_mask():
    row = lax.broadcasted_iota(jnp.int32, (MLA_T, MLA_T), 0)
    col = lax.broadcasted_iota(jnp.int32, (MLA_T, MLA_T), 1)
    return col <= row


def _mla_step(i, j, step):
    @pl.when(j < i)
    def _():
        step(None)

    @pl.when(j == i)
    def _():
        step(_mla_diagonal_mask())


def _mla_fwd(qn, qp, kn, kpe, v):
    S = qn.shape[0]
    it, jt = _mla_pairs(S // MLA_T, True)

    def body(it_ref, jt_ref, qn_ref, qp_ref, kn_ref, kpe_ref, v_ref, o_ref, lse_ref, m_scr, acc_scr):
        t = pl.program_id(1)
        i, j = it_ref[t], jt_ref[t]

        @pl.when(j == 0)
        def _():
            m_scr[...] = jnp.full(m_scr.shape, -jnp.inf, F32)
            acc_scr[...] = jnp.zeros(acc_scr.shape, F32)

        def step(ok):
            kpe_v = kpe_ref[...]
            ones = jnp.ones((MLA_T, HEAD), BF)
            state = [(m_scr[hh], acc_scr[hh]) for hh in range(MLA_HB)]
            new = []
            for hh in range(MLA_HB):
                m_prev, acc = state[hh]
                _, _, s = _mla_scores(qn_ref, qp_ref, kn_ref, kpe_v, hh, ok)
                m_new = jnp.maximum(m_prev, jnp.max(s, axis=1, keepdims=True))
                p = jnp.exp2((s - m_new) * MLA_EXP2_SCALE).astype(BF)
                v1 = jnp.concatenate([_mla_head(v_ref, hh), ones], axis=1)
                alpha = jnp.exp2((m_prev - m_new) * MLA_EXP2_SCALE)
                new.append((m_new, alpha * acc + lax.dot_general(p, v1, NN, preferred_element_type=F32)))
            for hh in range(MLA_HB):
                m_scr[hh], acc_scr[hh] = new[hh]

        _mla_step(i, j, step)

        @pl.when(j == i)
        def _():
            for hh in range(MLA_HB):
                l = acc_scr[hh, :, HEAD:]
                o_ref[:, hh * HEAD:(hh + 1) * HEAD] = (acc_scr[hh, :, :HEAD] / l).astype(BF)
                lse_ref[hh] = m_scr[hh] * MLA_SCALE + jnp.log(l)

    qspec, kspec, kpespec, lsespec = _mla_specs(MLA_HB)
    grid_spec = pltpu.PrefetchScalarGridSpec(
        num_scalar_prefetch=2, grid=(MLA_HEADS // MLA_HB, it.shape[0]),
        in_specs=[qspec, qspec, kspec, kpespec, kspec], out_specs=[qspec, lsespec],
        scratch_shapes=[pltpu.VMEM((MLA_HB, MLA_T, 1), F32), pltpu.VMEM((MLA_HB, MLA_T, 2 * HEAD), F32)])
    return pl.pallas_call(
        body, name="mla_fwd", grid_spec=grid_spec,
        out_shape=[jax.ShapeDtypeStruct((S, MLA_HEADS * HEAD), BF), jax.ShapeDtypeStruct((MLA_HEADS, S, LANE), F32)],
        compiler_params=_params("parallel", "arbitrary"))(it, jt, qn, qp, kn, kpe, v)


def _mla_p_ds(qn_ref, qp_ref, kn_ref, kpe, v_ref, do_ref, o_ref, lse_ref, hh, ok):
    q, k, s = _mla_scores(qn_ref, qp_ref, kn_ref, kpe, hh, ok)
    p = jnp.exp2(s * MLA_EXP2_SCALE - lse_ref[hh][:, 0:1] * LOG2E)
    do = _mla_head(do_ref, hh)
    delta = jnp.sum(do.astype(F32) * _mla_head(o_ref, hh).astype(F32), axis=1, keepdims=True)
    dp = lax.dot_general(do, _mla_head(v_ref, hh), NT, preferred_element_type=F32)
    ds = p * (dp - delta) * MLA_SCALE
    return q, k, p, ds, do


def _mla_bwd(qn, qp, kn, kpe, v, do, o, lse):
    S = qn.shape[0]
    nq = S // MLA_T
    hb = MLA_BWD_HB
    it, jt = _mla_pairs(nq, False)

    def body(it_ref, jt_ref, qn_ref, qp_ref, kn_ref, kpe_ref, v_ref, do_ref, o_ref, lse_ref,
             dqn_ref, dqp_ref, dkn_ref, dv_ref, dkpe_ref, dq_acc, dk_acc, dv_acc, stage_n, stage_p, osem):
        t = pl.program_id(1)
        i, j = it_ref[t], jt_ref[t]

        @pl.when(t == 0)
        def _():
            dq_acc[...] = jnp.zeros(dq_acc.shape, F32)

        @pl.when(i == j)
        def _():
            dk_acc[...] = jnp.zeros(dk_acc.shape, F32)
            dv_acc[...] = jnp.zeros(dv_acc.shape, F32)

        def step(ok):
            kpe_v = kpe_ref[...]
            for hh in range(hb):
                q, k, p, ds, do_h = _mla_p_ds(qn_ref, qp_ref, kn_ref, kpe_v, v_ref, do_ref, o_ref, lse_ref, hh, ok)
                ds = ds.astype(BF)
                dv_acc[hh] += lax.dot_general(p.astype(BF), do_h, TN, preferred_element_type=F32)
                dk_acc[hh] += lax.dot_general(ds, q, TN, preferred_element_type=F32)
                dq_acc[i, hh] += lax.dot_general(ds, k, NN, preferred_element_type=F32)

        _mla_step(i, j, step)

        @pl.when(i == j)
        def _():
            for hh in range(hb):
                stage_n[:, hh * HEAD:(hh + 1) * HEAD] = dq_acc[i, hh, :, :HEAD].astype(BF)
                stage_p[:, hh * HEAD:(hh + 1) * HEAD] = dq_acc[i, hh, :, HEAD:]
            rows = pl.ds(pl.multiple_of(i * MLA_T, MLA_T), MLA_T)
            cols = pl.ds(pl.multiple_of(pl.program_id(0) * hb * HEAD, LANE), hb * HEAD)
            out_n = pltpu.make_async_copy(stage_n, dqn_ref.at[rows, cols], osem.at[0])
            out_p = pltpu.make_async_copy(stage_p, dqp_ref.at[rows, cols], osem.at[1])
            out_n.start()
            out_p.start()
            out_n.wait()
            out_p.wait()

        @pl.when(i == nq - 1)
        def _():
            dkpe = dk_acc[0, :, HEAD:]
            for hh in range(hb):
                dkn_ref[:, hh * HEAD:(hh + 1) * HEAD] = dk_acc[hh, :, :HEAD].astype(BF)
                dv_ref[:, hh * HEAD:(hh + 1) * HEAD] = dv_acc[hh].astype(BF)
                if hh:
                    dkpe = dkpe + dk_acc[hh, :, HEAD:]
            dkpe_ref[...] = dkpe

    qspec, kspec, kpespec, lsespec = _mla_specs(hb)
    dkpespec = pl.BlockSpec((None, MLA_T, HEAD), lambda g, t, it, jt: (g, jt[t], 0))
    grid_spec = pltpu.PrefetchScalarGridSpec(
        num_scalar_prefetch=2, grid=(MLA_HEADS // hb, it.shape[0]),
        in_specs=[qspec, qspec, kspec, kpespec, kspec, qspec, qspec, lsespec],
        out_specs=[ANY, ANY, kspec, kspec, dkpespec],
        scratch_shapes=[pltpu.VMEM((nq, hb, MLA_T, 2 * HEAD), F32), pltpu.VMEM((hb, MLA_T, 2 * HEAD), F32),
                        pltpu.VMEM((hb, MLA_T, HEAD), F32), pltpu.VMEM((MLA_T, hb * HEAD), BF),
                        pltpu.VMEM((MLA_T, hb * HEAD), F32), pltpu.SemaphoreType.DMA((2,))])
    wide = jax.ShapeDtypeStruct((S, MLA_HEADS * HEAD), BF)
    return pl.pallas_call(
        body, name="mla_bwd", grid_spec=grid_spec,
        out_shape=[wide, jax.ShapeDtypeStruct((S, MLA_HEADS * HEAD), F32), wide, wide,
                   jax.ShapeDtypeStruct((MLA_HEADS // hb, S, HEAD), F32)],
        compiler_params=_params("parallel", "arbitrary"))(it, jt, qn, qp, kn, kpe, v, do, o, lse)


DIL_W = 3 * DIL_HPG * HEAD
DIL_O = DIL_HPG * HEAD
DIL_STEP_BLOCKS = 4


def _dil_slopes(g):
    return [2.0 ** (-ALIBI_MAX_BIAS * (g * DIL_HPG + hh + 1) / DIL_HEADS) for hh in range(DIL_HPG)]


def _dil_bias(dil):
    p = lax.broadcasted_iota(jnp.int32, (DIL_BLOCK, DIL_BLOCK), 0)
    kk = lax.broadcasted_iota(jnp.int32, (DIL_BLOCK, DIL_BLOCK), 1)
    jc = p - kk
    dist_c = (dil * jc).astype(F32)
    dist_p = (dil * (jc + DIL_BLOCK)).astype(F32)
    return jc >= 0, jc <= 0, dist_c, dist_p


def _dil_bias2(dil):
    p = lax.broadcasted_iota(jnp.int32, (DIL_BLOCK, 2 * DIL_BLOCK), 0)
    kk = lax.broadcasted_iota(jnp.int32, (DIL_BLOCK, 2 * DIL_BLOCK), 1)
    j = p + DIL_BLOCK - kk
    return (j >= 0) & (j <= DIL_BLOCK), kk < DIL_BLOCK, (dil * j).astype(F32)


def _dil_head(blk, hh):
    q = blk[:, hh * HEAD:(hh + 1) * HEAD]
    k = blk[:, DIL_O + hh * HEAD:DIL_O + (hh + 1) * HEAD]
    v = blk[:, 2 * DIL_O + hh * HEAD:2 * DIL_O + (hh + 1) * HEAD]
    return q, k, v


def _dil_s(q, k, slope, dist, ok):
    s = lax.dot_general(q, k, NT, preferred_element_type=F32) * DIL_SCALE - slope * dist
    return jnp.where(ok, s, -jnp.inf)


def _dil_view(a, dil):
    S, W = a.shape
    return a.reshape(S // dil, dil * W)


def _dil_fwd(qkv, g):
    _, dil = DIL_PATTERNS[g]
    S = qkv.shape[0]
    L = S // dil
    nb = L // DIL_BLOCK
    slopes = _dil_slopes(g)

    bb = min(DIL_STEP_BLOCKS, nb)
    rows = bb * DIL_BLOCK

    def body(cur_ref, prev_ref, o_ref, lse_ref):
        n = pl.program_id(1)
        ok, in_prev, dist = _dil_bias2(dil)
        for b in range(bb):
            if b == 0:
                both = jnp.concatenate([prev_ref[...], cur_ref[0:DIL_BLOCK, :]], axis=0)
                ok_b = ok & (~in_prev | (n > 0))
            else:
                both = cur_ref[(b - 1) * DIL_BLOCK:(b + 1) * DIL_BLOCK, :]
                ok_b = ok
            for hh in range(DIL_HPG):
                q, _, _ = _dil_head(both[DIL_BLOCK:], hh)
                _, k2, v2 = _dil_head(both, hh)
                s = _dil_s(q, k2, slopes[hh], dist, ok_b)
                m = jnp.max(s, axis=1, keepdims=True)
                p = jnp.exp(s - m)
                l = jnp.sum(p, axis=1, keepdims=True)
                o = lax.dot_general(p.astype(BF), v2, NN, preferred_element_type=F32) / l
                rs, sl = slice(b * DIL_BLOCK, (b + 1) * DIL_BLOCK), slice(hh * HEAD, (hh + 1) * HEAD)
                o_ref[rs, sl] = o
                lse_ref[rs, sl] = jnp.broadcast_to(m + jnp.log(l), (DIL_BLOCK, HEAD))

    ospec = pl.BlockSpec((rows, DIL_O), lambda r, n: (n, r))
    o, lse = pl.pallas_call(
        body, name=f"dil_fwd{g}", grid=(dil, nb // bb),
        in_specs=[pl.BlockSpec((rows, DIL_W), lambda r, n: (n, r)),
                  pl.BlockSpec((DIL_BLOCK, DIL_W), lambda r, n: (jnp.maximum(n * bb - 1, 0), r))],
        out_specs=[ospec, ospec],
        out_shape=[jax.ShapeDtypeStruct((L, dil * DIL_O), F32), jax.ShapeDtypeStruct((L, dil * DIL_O), F32)],
        compiler_params=_params("parallel", "parallel"))(_dil_view(qkv, dil), _dil_view(qkv, dil))
    return o.reshape(S, DIL_O), lse.reshape(S, DIL_O)


def _dil_combine(os_, lses):
    S = os_[0].shape[0]
    ts = _tile(S, 512)

    def body(o0, o1, o2, l0, l1, l2, out_ref, lse_ref):
        a, b, c = l0[...], l1[...], l2[...]
        m = jnp.maximum(jnp.maximum(a, b), c)
        ea, eb, ec = jnp.exp(a - m), jnp.exp(b - m), jnp.exp(c - m)
        tot = ea + eb + ec
        out_ref[...] = ((ea * o0[...] + eb * o1[...] + ec * o2[...]) / tot).astype(BF)
        lse_ref[...] = m + jnp.log(tot)

    return pl.pallas_call(
        body, name="dil_combine", grid=(S // ts,), in_specs=[_row(ts, DIL_O)] * 6,
        out_specs=[_row(ts, DIL_O), _row(ts, DIL_O)],
        out_shape=[jax.ShapeDtypeStruct((S, DIL_O), BF), jax.ShapeDtypeStruct((S, DIL_O), F32)],
        compiler_params=_params("parallel"))(*os_, *lses)


def _dil_delta(do, out):
    S = do.shape[0]
    ts = _tile(S, 512)

    def body(do_ref, o_ref, d_ref):
        for hh in range(DIL_HPG):
            sl = slice(hh * HEAD, (hh + 1) * HEAD)
            d = jnp.sum(do_ref[:, sl].astype(F32) * o_ref[:, sl].astype(F32), axis=1, keepdims=True)
            d_ref[:, sl] = jnp.broadcast_to(d, (ts, HEAD))

    return pl.pallas_call(
        body, name="dil_delta", grid=(S // ts,), in_specs=[_row(ts, DIL_O)] * 2, out_specs=_row(ts, DIL_O),
        out_shape=jax.ShapeDtypeStruct((S, DIL_O), F32), compiler_params=_params("parallel"))(do, out)


def _dil_bwd(qkv, do, lse, delta, g):
    _, dil = DIL_PATTERNS[g]
    S = qkv.shape[0]
    L = S // dil
    nb = L // DIL_BLOCK
    slopes = _dil_slopes(g)

    def pair(q, k, v, do_h, lse_h, delta_h, slope, dist, ok):
        s = _dil_s(q, k, slope, dist, ok)
        p = jnp.exp(s - lse_h)
        dp = lax.dot_general(do_h, v, NT, preferred_element_type=F32)
        ds = (p * (dp - delta_h) * DIL_SCALE).astype(BF)
        return p.astype(BF), ds

    bb = min(DIL_STEP_BLOCKS, nb)
    rows = bb * DIL_BLOCK
    steps = nb // bb

    def body(cur_ref, prev_ref, next_ref, doc_ref, don_ref, lsec_ref, lsen_ref, dlc_ref, dln_ref, out_ref):
        n = pl.program_id(1)
        ok2, in_prev, dist2 = _dil_bias2(dil)
        _, ok_p0, _, dist_p = _dil_bias(dil)
        for b in range(bb):
            rs = slice(b * DIL_BLOCK, (b + 1) * DIL_BLOCK)
            rn = slice((b + 1) * DIL_BLOCK, (b + 2) * DIL_BLOCK)
            two = slice(b * DIL_BLOCK, (b + 2) * DIL_BLOCK)
            first, last = b == 0, b == bb - 1
            if first:
                keys = jnp.concatenate([prev_ref[...], cur_ref[rs, :]], axis=0)
                ok_ab = ok2 & (~in_prev | (n > 0))
            else:
                keys = cur_ref[(b - 1) * DIL_BLOCK:(b + 1) * DIL_BLOCK, :]
                ok_ab = ok2
            qrows = jnp.concatenate([cur_ref[rs, :], next_ref[...]], axis=0) if last else cur_ref[two, :]
            ok_n = ok_p0 & (n < steps - 1) if last else ok_p0
            for hh in range(DIL_HPG):
                sl = slice(hh * HEAD, (hh + 1) * HEAD)
                q2, _, _ = _dil_head(qrows, hh)
                _, k2, v2 = _dil_head(keys, hh)
                q, qn, kc, vc = q2[:DIL_BLOCK], q2[DIL_BLOCK:], k2[DIL_BLOCK:], v2[DIL_BLOCK:]
                do2 = jnp.concatenate([doc_ref[rs, sl], don_ref[:, sl]], axis=0) if last else doc_ref[two, sl]
                do_c, do_n = do2[:DIL_BLOCK], do2[DIL_BLOCK:]
                lse_c = lsec_ref[rs, sl][:, 0:1]
                lse_n = (lsen_ref[:, sl] if last else lsec_ref[rn, sl])[:, 0:1]
                dl_c = dlc_ref[rs, sl][:, 0:1]
                dl_n = (dln_ref[:, sl] if last else dlc_ref[rn, sl])[:, 0:1]
                p_ab, ds_ab = pair(q, k2, v2, do_c, lse_c, dl_c, slopes[hh], dist2, ok_ab)
                p_n, ds_n = pair(qn, kc, vc, do_n, lse_n, dl_n, slopes[hh], dist_p, ok_n)
                dq = lax.dot_general(ds_ab, k2, NN, preferred_element_type=F32)
                dk = lax.dot_general(jnp.concatenate([ds_ab[:, DIL_BLOCK:], ds_n], axis=0), q2, TN, preferred_element_type=F32)
                dv = lax.dot_general(jnp.concatenate([p_ab[:, DIL_BLOCK:], p_n], axis=0), do2, TN, preferred_element_type=F32)
                out_ref[rs, sl] = dq.astype(BF)
                out_ref[rs, DIL_O + hh * HEAD:DIL_O + (hh + 1) * HEAD] = dk.astype(BF)
                out_ref[rs, 2 * DIL_O + hh * HEAD:2 * DIL_O + (hh + 1) * HEAD] = dv.astype(BF)

    cur_w = pl.BlockSpec((rows, DIL_W), lambda r, n: (n, r))
    prev_w = pl.BlockSpec((DIL_BLOCK, DIL_W), lambda r, n: (jnp.maximum(n * bb - 1, 0), r))
    next_w = pl.BlockSpec((DIL_BLOCK, DIL_W), lambda r, n: (jnp.minimum((n + 1) * bb, nb - 1), r))
    cur_o = pl.BlockSpec((rows, DIL_O), lambda r, n: (n, r))
    next_o = pl.BlockSpec((DIL_BLOCK, DIL_O), lambda r, n: (jnp.minimum((n + 1) * bb, nb - 1), r))
    qv, dov, lsev, dlv = _dil_view(qkv, dil), _dil_view(do, dil), _dil_view(lse, dil), _dil_view(delta, dil)
    out = pl.pallas_call(
        body, name=f"dil_bwd{g}", grid=(dil, steps),
        in_specs=[cur_w, prev_w, next_w, cur_o, next_o, cur_o, next_o, cur_o, next_o],
        out_specs=cur_w, out_shape=jax.ShapeDtypeStruct((L, dil * DIL_W), BF),
        compiler_params=_params("parallel", "parallel"))(qv, qv, qv, dov, dov, lsev, lsev, dlv, dlv)
    return out.reshape(S, DIL_W)


def _adamw(name, w, g, m, v):
    R, C = w.shape
    tr, tc = _adamw_block(R, C)

    def body(w_ref, g_ref, m_ref, v_ref, go_ref, d_ref, nm_ref, nv_ref):
        gv = g_ref[...]
        go_ref[...] = gv
        nm = ADAM_B1 * m_ref[...] + (1.0 - ADAM_B1) * gv
        nv = ADAM_B2 * v_ref[...] + (1.0 - ADAM_B2) * (gv * gv)
        m_hat = nm / (1.0 - ADAM_B1 ** ADAM_STEP)
        v_hat = nv / (1.0 - ADAM_B2 ** ADAM_STEP)
        d_ref[...] = -ADAM_LR * (m_hat / (jnp.sqrt(v_hat) + ADAM_EPS) + ADAM_WD * w_ref[...])
        nm_ref[...] = nm
        nv_ref[...] = nv

    spec = pl.BlockSpec((tr, tc), lambda i, j: (i, j))
    shp = jax.ShapeDtypeStruct((R, C), F32)
    return pl.pallas_call(
        body, name=name, grid=(R // tr, C // tc), in_specs=[spec] * 4, out_specs=[spec] * 4, out_shape=[shp] * 4,
        compiler_params=_params("parallel", "parallel"))(w, g, m, v)


ADAMW_BLOCK_ELEMS = 704 * 1024


def _adamw_block(R, C):
    if R * C <= ADAMW_BLOCK_ELEMS:
        return R, C
    tr = _tile_rows(R, max(8, ADAMW_BLOCK_ELEMS // C))
    tc = _tile(C, max(LANE, ADAMW_BLOCK_ELEMS // R))
    if tr * C >= R * tc or R * tc > ADAMW_BLOCK_ELEMS:
        return tr, C
    return R, tc


def _tile_rows(n, pref, mult=8):
    t = (pref // mult) * mult
    while t >= mult:
        if n % t == 0:
            return t
        t -= mult
    return n


ANY = pl.BlockSpec(memory_space=pl.ANY)


def _place():
    x, y, c = lax.axis_index("x"), lax.axis_index("y"), lax.axis_index("c")
    chips = [(1 - x, y), (x, 1 - y), (1 - x, 1 - y)]
    chip_idx = [2 * cx + cy for cx, cy in chips]
    return x, y, c, 2 * x + y, chips, chip_idx


def _rcopy(src, dst, ssem, rsem, dev):
    return pltpu.make_async_remote_copy(src_ref=src, dst_ref=dst, send_sem=ssem, recv_sem=rsem,
                                        device_id=dev, device_id_type=MESH)


HBM = pl.BlockSpec(memory_space=pltpu.HBM)
SEM = pl.BlockSpec(memory_space=pltpu.SEMAPHORE)
EFFECT = pltpu.SideEffectType.DATAFLOW_SIDE_EFFECTING


def _split_copies(kind, srcs, lands, ssem, rsem):
    _, _, c, me, chips, chip_idx = _place()
    cps = []
    for i in range(len(srcs)):
        for k in range(3):
            if kind == "gather":
                src, dst = srcs[i], lands[i].at[me]
            elif kind == "gather_half":
                rows = srcs[i].shape[0] // 2
                half = pl.ds(pl.multiple_of(c * rows, 16), rows)
                src, dst = srcs[i].at[half], lands[i].at[me, half]
            else:
                src, dst = srcs[i].at[chip_idx[k]], lands[i].at[k]
            cps.append(_rcopy(src, dst, ssem.at[3 * i + k], rsem.at[3 * i + k], (*chips[k], c)))
    return cps


def _exchange_start(name, kind, srcs, lands, groups):
    n, ng = len(srcs), len(groups)

    def body(*refs):
        src_refs, land_refs = refs[:n], refs[n:2 * n]
        sems = refs[2 * n:2 * n + 2 * ng]
        token = refs[-1]
        for gi, grp in enumerate(groups):
            cps = _split_copies(kind, [src_refs[i] for i in grp], [land_refs[i] for i in grp], sems[2 * gi], sems[2 * gi + 1])
            for cp in cps:
                cp.start()
        token[...] = jnp.zeros_like(token)

    arrays = list(srcs) + list(lands)
    out_shape = []
    for grp in groups:
        out_shape += [pltpu.SemaphoreType.DMA((3 * len(grp),)), pltpu.SemaphoreType.DMA((3 * len(grp),))]
    out_shape += [pltpu.HBM(a.shape, a.dtype) for a in arrays] + [jax.ShapeDtypeStruct((8, LANE), F32)]
    outs = pl.pallas_call(
        body, name=name, out_shape=out_shape, in_specs=[HBM] * (2 * n),
        out_specs=[SEM] * (2 * ng) + [HBM] * (2 * n) + [pl.BlockSpec(memory_space=pltpu.VMEM)],
        input_output_aliases={i: 2 * ng + i for i in range(2 * n)},
        compiler_params=pltpu.CompilerParams(has_side_effects=EFFECT),
    )(*[pltpu.with_memory_space_constraint(a, pltpu.HBM) for a in arrays])
    sems = [(outs[2 * gi], outs[2 * gi + 1]) for gi in range(ng)]
    thru = outs[2 * ng:2 * ng + 2 * n]
    return sems, thru[:n], thru[n:], outs[-1]


def _exchange_wait(name, kind, srcs, lands, sems, after):
    n = len(srcs)

    def body(*refs):
        cps = _split_copies(kind, refs[:n], refs[n:2 * n], refs[2 * n], refs[2 * n + 1])
        for cp in cps:
            cp.wait_send()
            cp.wait_recv()

    arrays = list(srcs) + list(lands)
    outs = pl.pallas_call(
        body, name=name, out_shape=[pltpu.HBM(a.shape, a.dtype) for a in arrays],
        in_specs=[HBM] * (2 * n) + [SEM, SEM, ANY], out_specs=[HBM] * (2 * n),
        input_output_aliases={i: i for i in range(2 * n)},
        compiler_params=pltpu.CompilerParams(has_side_effects=EFFECT),
    )(*arrays, sems[0], sems[1], after)
    return outs[:n], outs[n:]


EXCHANGE_CHUNK_BYTES = 3 * 1024 * 1024


def _half_geometry(R, C, axis):
    Rp, Cp = (R // 2, C) if axis == 0 else (R, C // 2)
    tr = _tile_rows(Rp, max(16, EXCHANGE_CHUNK_BYTES // (2 * Cp)), 16)
    return Rp, Cp, tr, Rp // tr


def _pair_sum(name, g, axis):
    G, R, C = g.shape
    Rp, Cp, tr, nb = _half_geometry(R, C, axis)
    steps = G * nb

    def half_block(t, h):
        s, b = t // nb, t % nb
        return (s, h * nb + b, 0) if axis == 0 else (s, b, h)

    def body(c_ref, keep_ref, give_ref, out_ref, land, ssem, rsem, credit):
        x, y, c = lax.axis_index("x"), lax.axis_index("y"), lax.axis_index("c")
        sib = (x, y, 1 - c)
        t = pl.program_id(0)

        def copy(chunk):
            return _rcopy(give_ref.at[0], land.at[chunk % 2], ssem.at[chunk % 2], rsem.at[chunk % 2], sib)

        @pl.when((t >= 2) & (t < steps))
        def _():
            pl.semaphore_wait(credit, 1)

        @pl.when(t < steps)
        def _():
            copy(t).start()

        @pl.when(t >= 1)
        def _():
            copy(t - 1).wait_recv()
            out_ref[...] = (keep_ref[...].astype(F32) + land[(t - 1) % 2].astype(F32)).astype(BF)

            @pl.when(t + 1 < steps)
            def _():
                pl.semaphore_signal(credit, 1, device_id=sib, device_id_type=MESH)

        @pl.when(t < steps)
        def _():
            copy(t).wait_send()

    blk = (None, tr, Cp)
    grid_spec = pltpu.PrefetchScalarGridSpec(
        num_scalar_prefetch=1, grid=(steps + 1,),
        in_specs=[pl.BlockSpec(blk, lambda t, c_ref: half_block(jnp.maximum(t - 1, 0), c_ref[0])),
                  pl.BlockSpec((1, tr, Cp), lambda t, c_ref: half_block(jnp.minimum(t, steps - 1), 1 - c_ref[0]))],
        out_specs=pl.BlockSpec(blk, lambda t, c_ref: (jnp.maximum(t - 1, 0) // nb, jnp.maximum(t - 1, 0) % nb, 0)),
        scratch_shapes=[pltpu.VMEM((2, tr, Cp), BF), pltpu.SemaphoreType.DMA((2,)), pltpu.SemaphoreType.DMA((2,)),
                        pltpu.SemaphoreType.REGULAR])
    c_arr = lax.axis_index("c").astype(jnp.int32).reshape(1)
    return pl.pallas_call(
        body, name=name, grid_spec=grid_spec, out_shape=jax.ShapeDtypeStruct((G, Rp, Cp), BF),
        compiler_params=_params("arbitrary"))(c_arr, g, g)


def _chip_total_join(name, h, landed, axis):
    G, Rp, Cp = h.shape
    R, C = (2 * Rp, Cp) if axis == 0 else (Rp, 2 * Cp)
    tr = _tile_rows(Rp, max(16, EXCHANGE_CHUNK_BYTES // (4 * Cp)), 16)
    nb = Rp // tr

    def body(me_ref, own_ref, l0_ref, l1_ref, l2_ref, full, stage, ssem, rsem, lsem):
        x, y, c = lax.axis_index("x"), lax.axis_index("y"), lax.axis_index("c")
        sib = (x, y, 1 - c)
        b = pl.program_id(0)

        def place(half, r0, rows):
            if axis == 0:
                return full.at[pl.ds(pl.multiple_of(half * Rp + r0, 8), rows), :]
            return full.at[pl.ds(pl.multiple_of(r0, 8), rows), pl.ds(pl.multiple_of(half * Cp, LANE), Cp)]

        def copies(step):
            s = step % 2
            mine = place(c, step * tr, tr)
            return pltpu.make_async_copy(stage.at[s], mine, lsem.at[s]), _rcopy(stage.at[s], mine, ssem.at[s], rsem, sib)

        @pl.when(b >= 2)
        def _():
            loc, rem = copies(b - 2)
            loc.wait()
            rem.wait_send()

        acc = own_ref[...].astype(F32)
        for r in (l0_ref, l1_ref, l2_ref):
            acc = acc + r[...].astype(F32)
        stage[b % 2] = acc
        loc, rem = copies(b)
        loc.start()
        rem.start()

        @pl.when(b == nb - 1)
        def _():
            for step in range(max(0, nb - 2), nb):
                loc, rem = copies(step)
                loc.wait()
                rem.wait_send()
            theirs = place(1 - c, 0, Rp)
            _rcopy(theirs, theirs, ssem.at[0], rsem, sib).wait_recv()

    blk = (None, tr, Cp)
    grid_spec = pltpu.PrefetchScalarGridSpec(
        num_scalar_prefetch=1, grid=(nb,),
        in_specs=[pl.BlockSpec(blk, lambda b, me_ref: (me_ref[0], b, 0))]
        + [pl.BlockSpec(blk, functools.partial(lambda b, me_ref, k: (k, b, 0), k=k)) for k in range(3)],
        out_specs=ANY,
        scratch_shapes=[pltpu.VMEM((2, tr, Cp), F32), pltpu.SemaphoreType.DMA((2,)), pltpu.SemaphoreType.DMA,
                        pltpu.SemaphoreType.DMA((2,))])
    me = (2 * lax.axis_index("x") + lax.axis_index("y")).astype(jnp.int32).reshape(1)
    return pl.pallas_call(
        body, name=name, grid_spec=grid_spec, out_shape=jax.ShapeDtypeStruct((R, C), F32),
        compiler_params=_params("arbitrary"))(me, h, landed, landed, landed)


def _pair_share(name, land):
    G, R, C = land.shape
    Rh = R // 2
    tr = _tile_rows(Rh, max(16, EXCHANGE_CHUNK_BYTES // (2 * C)), 16)
    chunks = [(k, b) for k in range(3) for b in range(Rh // tr)]

    def body(src, dst, buf, lsem, ssem, rsem):
        x, y, c, _, _, chip_idx = _place()
        sib = (x, y, 1 - c)

        def region(ref, k, half, r0, rows):
            return ref.at[chip_idx[k], pl.ds(pl.multiple_of(half * Rh + r0, 16), rows)]

        def load(t):
            k, b = chunks[t]
            return pltpu.make_async_copy(region(src, k, c, b * tr, tr), buf.at[t % 2], lsem.at[t % 2])

        def send(t):
            k, b = chunks[t]
            return _rcopy(buf.at[t % 2], region(dst, k, c, b * tr, tr), ssem.at[t % 2], rsem.at[k], sib)

        load(0).start()
        for t in range(len(chunks)):
            load(t).wait()
            if t + 1 < len(chunks):
                if t >= 1:
                    send(t - 1).wait_send()
                load(t + 1).start()
            send(t).start()
        for t in range(max(0, len(chunks) - 2), len(chunks)):
            send(t).wait_send()
        for k in range(3):
            theirs = region(dst, k, 1 - c, 0, Rh)
            _rcopy(theirs, theirs, ssem.at[0], rsem.at[k], sib).wait_recv()

    return pl.pallas_call(
        body, name=name, in_specs=[ANY], out_specs=ANY, out_shape=jax.ShapeDtypeStruct(land.shape, land.dtype),
        input_output_aliases={0: 0},
        scratch_shapes=[pltpu.VMEM((2, tr, C), land.dtype), pltpu.SemaphoreType.DMA((2,)), pltpu.SemaphoreType.DMA((2,)),
                        pltpu.SemaphoreType.DMA((3,))],
    )(land)


def _allreduce_small(v):
    R, K = v.shape
    ndev = 8

    def body(v_ref, o_ref, land, ssem, rsem):
        x, y, c = lax.axis_index("x"), lax.axis_index("y"), lax.axis_index("c")
        me = 4 * x + 2 * y + c
        land[me] = v_ref[...]
        cps = []
        for r in range(1, ndev):
            fx, fy, fc = (r >> 2) & 1, (r >> 1) & 1, r & 1
            peer = (x ^ fx, y ^ fy, c ^ fc)
            cp = _rcopy(v_ref, land.at[me], ssem.at[r - 1], rsem.at[r - 1], peer)
            cp.start()
            cps.append((cp, 4 * peer[0] + 2 * peer[1] + peer[2], r))
        for cp, src, r in cps:
            cp.wait_send()
            _rcopy(v_ref, land.at[src], ssem.at[r - 1], rsem.at[r - 1], (x, y, c)).wait_recv()
        acc = land[0]
        for d in range(1, ndev):
            acc = acc + land[d]
        o_ref[...] = acc

    vm = pl.BlockSpec(memory_space=pltpu.VMEM)
    return pl.pallas_call(
        body, name="allreduce_small", in_specs=[vm], out_specs=vm, out_shape=jax.ShapeDtypeStruct((R, K), F32),
        scratch_shapes=[pltpu.VMEM((ndev, R, K), F32), pltpu.SemaphoreType.DMA((ndev - 1,)), pltpu.SemaphoreType.DMA((ndev - 1,))],
    )(v)


IN_SPLITS = (Q_RANK, KV_RANK, QK_ROPE, DIL_HEADS * HEAD, DIL_HEADS * HEAD, DIL_HEADS * HEAD, D_MODEL, D_MODEL)
IN_OFF = tuple(int(v) for v in np.cumsum((0,) + IN_SPLITS))


def _unshard_cols(g):
    G, K, Ns = g.shape
    return g.transpose(1, 0, 2).reshape(K, G * Ns)


def _shard_cols(w):
    K, N = w.shape
    return w.reshape(K, N_CHIPS, N // N_CHIPS).transpose(1, 0, 2)


def _rope_pad(w):
    half = QK_ROPE // 2
    z = jnp.zeros(w.shape[:-1] + (half,), w.dtype)
    return jnp.concatenate([w[..., :half], z, w[..., half:], z], axis=-1)


def _rope_unpad(w):
    half = QK_ROPE // 2
    return jnp.concatenate([w[..., :half], w[..., 2 * half:3 * half]], axis=-1)


def _split_w_in(w_in_g):
    G, K, Ns = w_in_g.shape

    def cols(lo, hi):
        pieces = [w_in_g[k][:, max(lo, k * Ns) - k * Ns:min(hi, (k + 1) * Ns) - k * Ns]
                  for k in range(G) if max(lo, k * Ns) < min(hi, (k + 1) * Ns)]
        return pieces[0] if len(pieces) == 1 else jnp.concatenate(pieces, axis=1)

    p = [(IN_OFF[i], IN_OFF[i + 1]) for i in range(8)]
    w_lat = jnp.concatenate([cols(*p[0]), cols(*p[1]), _rope_pad(cols(*p[2])),
                             jnp.zeros((K, LAT_W - _KPE.stop), w_in_g.dtype)], axis=1)
    w_dil = [jnp.concatenate([cols(p[3 + t][0] + g * DIL_O, p[3 + t][0] + (g + 1) * DIL_O) for t in range(3)], axis=1)
             for g in range(DIL_GROUPS)]
    w_gate = cols(p[6][0], p[7][1])
    return w_lat, w_dil, w_gate


def _merge_dw_in(dw_lat, dw_dil, dw_gate):
    parts = [dw_lat[:, _CQ], dw_lat[:, _CKV], _rope_unpad(dw_lat[:, _KPE])]
    for t in range(3):
        parts += [dw_dil[g][:, t * DIL_O:(t + 1) * DIL_O] for g in range(DIL_GROUPS)]
    parts.append(dw_gate)
    width = sum(p.shape[1] for p in parts) // N_CHIPS
    shards = []
    for k in range(N_CHIPS):
        pieces, at = [], 0
        for p in parts:
            lo, hi = max(k * width, at), min((k + 1) * width, at + p.shape[1])
            if lo < hi:
                pieces.append(p[:, lo - at:hi - at])
            at += p.shape[1]
        shards.append(jnp.concatenate(pieces, axis=1))
    return jnp.stack(shards)


def _split_w_uq(w_uq_g):
    w = _unshard_cols(w_uq_g)
    K = w.shape[0]
    w = w.reshape(K, MLA_HEADS, QK_NOPE + QK_ROPE)
    return w[:, :, :QK_NOPE].reshape(K, MLA_HEADS * HEAD), _rope_pad(w[:, :, QK_NOPE:]).reshape(K, MLA_HEADS * HEAD)


def _merge_dw_uq(dw_n, dw_p):
    K = dw_n.shape[0]
    w = jnp.concatenate([dw_n.reshape(K, MLA_HEADS, HEAD), _rope_unpad(dw_p.reshape(K, MLA_HEADS, HEAD))], axis=-1)
    return _shard_cols(w.reshape(K, MLA_HEADS * (QK_NOPE + QK_ROPE)))


def _split_w_ukv(w_ukv_g):
    w = _unshard_cols(w_ukv_g)
    K = w.shape[0]
    w = w.reshape(K, MLA_HEADS, 2 * HEAD)
    return w[:, :, :HEAD].reshape(K, MLA_HEADS * HEAD), w[:, :, HEAD:].reshape(K, MLA_HEADS * HEAD)


def _merge_dw_ukv(dw_k, dw_v):
    K = dw_k.shape[0]
    w = jnp.concatenate([dw_k.reshape(K, MLA_HEADS, HEAD), dw_v.reshape(K, MLA_HEADS, HEAD)], axis=-1)
    return _shard_cols(w.reshape(K, MLA_HEADS * 2 * HEAD))


GATHER_GROUPS = (("w_in",), ("w_uq", "w_ukv", "w_o_mla", "w_o_dil", "w_out"), ("w_up", "w_down", "conv_w"))
SHARED_FETCH = ("w_in",)
REDUCE_GROUPS = (("w_down", "w_up"), ("w_out", "w_o_mla", "w_o_dil"), ("w_uq", "w_ukv", "w_in"))


def _local_step(x, tgt, W, fetch, emit):
    S, D = x.shape
    cos, sin_s = _rope_tables(S)
    w_lat, w_dil, w_gate = _split_w_in(fetch(0, x)["w_in"])

    h = _rmsnorm_fwd("attn_norm", x, W["attn_norm_g"])
    lat = _mm_nn("proj_lat", h, w_lat)
    qkv = [_mm_nn(f"proj_dil{g}", h, w_dil[g], o_dtype=BF) for g in range(DIL_GROUPS)]
    gpre = _mm_nn("proj_gate", h, w_gate, o_dtype=BF)
    WB = fetch(1, gpre)
    w_uqn, w_uqp = _split_w_uq(WB["w_uq"])
    w_k, w_v = _split_w_ukv(WB["w_ukv"])
    w_o_mla, w_o_dil = WB["w_o_mla"], WB["w_o_dil"]
    w_out = WB["w_out"].reshape(D, D)
    qn_, kvn, kpe = _mla_prep(lat, W["q_norm_g"], W["kv_norm_g"], cos, sin_s)
    q_nope = _mm_nn("q_nope", qn_, w_uqn, o_dtype=BF)
    q_pe = _rope("q_rope", _mm_nn("q_pe", qn_, w_uqp), cos, sin_s, False)
    k_nope = _mm_nn("k_nope", kvn, w_k, o_dtype=BF)
    v_mla = _mm_nn("v_mla", kvn, w_v, o_dtype=BF)
    attn_a, lse_a = _mla_fwd(q_nope, q_pe, k_nope, kpe, v_mla)
    dil = [_dil_fwd(qkv[g], g) for g in range(DIL_GROUPS)]
    attn_b, lse_b = _dil_combine([o for o, _ in dil], [l for _, l in dil])
    o_a = _mm_nn("o_mla", attn_a, w_o_mla, o_dtype=BF)
    o_b = _mm_nn("o_dil", attn_b, w_o_dil, o_dtype=BF)
    merge = _merge_fwd(gpre, W["b_gate"], o_a, o_b)
    x1 = _mm_nn("out_proj", merge, w_out, add=x)
    WC = fetch(2, merge)
    w_up = WC["w_up"]
    G4, _, C = w_up.shape
    w_down = WC["w_down"].reshape(G4 // 2, C, D)
    conv_w = WC["conv_w"]
    conv_b = W["conv_b"].reshape(G4, 1, C)
    h2 = _rmsnorm_fwd("ffn_norm", x1, W["ffn_norm_g"])
    u_pre = _up_fwd(h2, w_up)
    act, u = _ffn_act(u_pre, conv_w, conv_b)
    x2 = _down_fwd(act, w_down, x1)
    dx2, d_final_g, loss8 = _final_loss(x2, tgt, W["final_norm_g"])

    d_act = _down_dgrad(dx2, w_down)
    dw_down = _down_wgrad(act, dx2)
    du_pre, d_conv_w, d_conv_b = _ffn_act_conv_bwd(u, d_act, u_pre, conv_w)
    dh2 = _up_dgrad(du_pre, w_up)
    dw_up = _up_wgrad(h2, du_pre)
    zero = emit(0, {"w_down": dw_down.reshape(N_CHIPS, (G4 // 2) * C // N_CHIPS, D), "w_up": dw_up})
    dx1, d_ffn_g = _rmsnorm_bwd("ffn_norm_bwd", dh2, x1, W["ffn_norm_g"] + zero, dx2)
    d_merge = _mm_nt("out_proj_dgrad", dx1, w_out, o_dtype=BF)
    dw_out = _mm_tn("out_proj_wgrad", merge, dx1)
    d_oa, d_ob, d_gpre, d_b_gate = _merge_bwd(d_merge, gpre, W["b_gate"], o_a, o_b)
    d_attn_a = _mm_nt("o_mla_dgrad", d_oa, w_o_mla, o_dtype=BF)
    dw_o_mla = _mm_tn("o_mla_wgrad", attn_a, d_oa, shards=N_CHIPS)
    d_attn_b = _mm_nt("o_dil_dgrad", d_ob, w_o_dil, o_dtype=BF)
    dw_o_dil = _mm_tn("o_dil_wgrad", attn_b, d_ob, shards=N_CHIPS)
    zero = emit(1, {"w_out": dw_out.reshape(N_CHIPS, D // N_CHIPS, D), "w_o_mla": dw_o_mla, "w_o_dil": dw_o_dil})
    q_norm_g = W["q_norm_g"] + zero
    delta_b = _dil_delta(d_attn_b, attn_b)
    d_qkv = [_dil_bwd(qkv[g], d_attn_b, lse_b, delta_b, g) for g in range(DIL_GROUPS)]
    dq_nope, dq_pe_rot, dk_nope, dv_mla, dkpe_rot = _mla_bwd(q_nope, q_pe, k_nope, kpe, v_mla, d_attn_a, attn_a, lse_a)
    dq_pe = _rope("q_rope_bwd", dq_pe_rot, cos, sin_s, True)
    d_qn = _mm_nt_sum("q_dgrad", [dq_nope, dq_pe], [w_uqn, w_uqp])
    d_kvn = _mm_nt_sum("kv_dgrad", [dk_nope, dv_mla], [w_k, w_v])
    dw_uq = _merge_dw_uq(_mm_tn("q_nope_wgrad", qn_, dq_nope), _mm_tn("q_pe_wgrad", qn_, dq_pe))
    dw_ukv = _merge_dw_ukv(_mm_tn("k_nope_wgrad", kvn, dk_nope), _mm_tn("v_wgrad", kvn, dv_mla))
    d_lat, d_q_g, d_kv_g = _mla_prep_bwd(lat, q_norm_g, W["kv_norm_g"], cos, sin_s, d_qn, d_kvn, dkpe_rot)
    dw_in = _merge_dw_in(_mm_tn("proj_lat_wgrad", h, d_lat),
                         [_mm_tn(f"proj_dil{g}_wgrad", h, d_qkv[g]) for g in range(DIL_GROUPS)],
                         _mm_tn("proj_gate_wgrad", h, d_gpre))
    zero = emit(2, {"w_uq": dw_uq, "w_ukv": dw_ukv, "w_in": dw_in})
    dh = _mm_nt_sum("proj_lat_dil_dgrad", [d_lat] + d_qkv, [w_lat + zero.astype(BF)] + w_dil)
    dh = _mm_nt("proj_gate_dgrad", d_gpre, w_gate, add=dh)
    grad_x, d_attn_g = _rmsnorm_bwd("attn_norm_bwd", dh, x, W["attn_norm_g"], dx1)

    small = {"attn_norm_g": d_attn_g, "b_gate": d_b_gate, "q_norm_g": d_q_g, "kv_norm_g": d_kv_g,
             "ffn_norm_g": d_ffn_g, "conv_w": d_conv_w, "conv_b": d_conv_b.reshape(1, G4 * C),
             "final_norm_g": d_final_g}
    return loss8[0, 0], grad_x, small


BIG = ("w_in", "w_uq", "w_ukv", "w_o_mla", "w_o_dil", "w_out", "w_up", "w_down")
SMALL = ("attn_norm_g", "b_gate", "q_norm_g", "kv_norm_g", "ffn_norm_g", "conv_w", "conv_b", "final_norm_g")
WEIGHTS = ("attn_norm_g", "w_in", "b_gate", "q_norm_g", "w_uq", "kv_norm_g", "w_ukv", "w_o_mla", "w_o_dil",
           "w_out", "ffn_norm_g", "w_up", "conv_w", "conv_b", "w_down", "final_norm_g")
SMALL_ROWS = 8
COLUMN_MAJOR = ("w_in", "w_up")
HALF_AXIS = {"w_down": 1}


def _gather_start(shards):
    chip = 2 * lax.axis_index("x") + lax.axis_index("y")

    def prepare(names, zero):
        srcs, lands = [], []
        for n in names:
            if n in COLUMN_MAJOR:
                s = lax.optimization_barrier((shards[n].T + zero).astype(BF).T)
            else:
                s = shards[n] + zero
                s = s if n == "conv_w" else s.astype(BF)
            lands.append(lax.dynamic_update_slice(lax.empty((N_CHIPS,) + s.shape, s.dtype), s[None], (chip, 0, 0)))
            srcs.append(s)
        return srcs, lands

    assert all(n in SHARED_FETCH for n in GATHER_GROUPS[0]) and not any(n in SHARED_FETCH for g in GATHER_GROUPS[1:] for n in g)
    n0 = len(GATHER_GROUPS[0])
    srcs0, lands0 = prepare(GATHER_GROUPS[0], 0.0)
    sems0, srcs0, lands0, token = _exchange_start("gather_start0", "gather_half", srcs0, lands0, [list(range(n0))])
    srcs, lands = prepare([n for grp in GATHER_GROUPS[1:] for n in grp], token[0, 0])
    groups, at = [], 0
    for grp in GATHER_GROUPS[1:]:
        groups.append(list(range(at, at + len(grp))))
        at += len(grp)
    sems, srcs, lands, token1 = _exchange_start("gather_start1", "gather", srcs, lands, groups)

    def fetch(i, after):
        if i == 0:
            _, got = _exchange_wait("gather_wait0", "gather_half", srcs0, lands0, sems0[0], token1)
        else:
            idx = groups[i - 1]
            _, got = _exchange_wait(f"gather_wait{i}", "gather", [srcs[j] for j in idx], [lands[j] for j in idx],
                                    sems[i - 1], after)
        return {n: _pair_share(f"pair_share_{n}", g) if n in SHARED_FETCH else g for n, g in zip(GATHER_GROUPS[i], got)}

    return fetch, token[0, 0]


def _reduce_start(i, grads):
    names = REDUCE_GROUPS[i]
    hs = [_pair_sum(f"pair_sum_{n}", grads[n], HALF_AXIS.get(n, 0)) for n in names]
    lands = [lax.empty((3,) + h.shape[1:], h.dtype) for h in hs]
    sems, hs, lands, token = _exchange_start(f"reduce_start{i}", "scatter", hs, lands, [list(range(len(names)))])
    return (sems[0], hs, lands), token[0, 0]


def _reduce_finish(i, pending, after):
    sems, hs, lands = pending
    hs, lands = _exchange_wait(f"reduce_wait{i}", "scatter", hs, lands, sems, after)
    out = {}
    for n, h, landed in zip(REDUCE_GROUPS[i], hs, lands):
        out[n] = _chip_total_join(f"chip_total_{n}", h, landed, HALF_AXIS.get(n, 0))
    return out


def _reduce_small(small):
    names = tuple(small)
    flat = [small[n].reshape(-1) for n in names]
    sizes = [f.shape[0] for f in flat]
    total = sum(sizes)
    width = -(-total // (SMALL_ROWS * LANE)) * LANE
    packed = jnp.concatenate(flat + [jnp.zeros((SMALL_ROWS * width - total,), F32)]).reshape(SMALL_ROWS, width)
    red = _allreduce_small(packed).reshape(-1)
    out, off = {}, 0
    for n, s in zip(names, sizes):
        out[n] = red[off:off + s]
        off += s
    return out


def kernel(x, attn_norm_g, w_in, b_gate, q_norm_g, w_uq, kv_norm_g, w_ukv, w_o_mla, w_o_dil, w_out, ffn_norm_g, w_up, conv_w, conv_b, w_down, final_norm_g, loss_target, m_attn_norm_g, m_w_in, m_b_gate, m_q_norm_g, m_w_uq, m_kv_norm_g, m_w_ukv, m_w_o_mla, m_w_o_dil, m_w_out, m_ffn_norm_g, m_w_up, m_conv_w, m_conv_b, m_w_down, m_final_norm_g, v_attn_norm_g, v_w_in, v_b_gate, v_q_norm_g, v_w_uq, v_kv_norm_g, v_w_ukv, v_w_o_mla, v_w_o_dil, v_w_out, v_ffn_norm_g, v_w_up, v_conv_w, v_conv_b, v_w_down, v_final_norm_g):
    given = dict(attn_norm_g=attn_norm_g, w_in=w_in, b_gate=b_gate, q_norm_g=q_norm_g, w_uq=w_uq, kv_norm_g=kv_norm_g,
                 w_ukv=w_ukv, w_o_mla=w_o_mla, w_o_dil=w_o_dil, w_out=w_out, ffn_norm_g=ffn_norm_g, w_up=w_up,
                 conv_w=conv_w, conv_b=conv_b, w_down=w_down, final_norm_g=final_norm_g)
    moments_m = dict(attn_norm_g=m_attn_norm_g, w_in=m_w_in, b_gate=m_b_gate, q_norm_g=m_q_norm_g, w_uq=m_w_uq,
                     kv_norm_g=m_kv_norm_g, w_ukv=m_w_ukv, w_o_mla=m_w_o_mla, w_o_dil=m_w_o_dil, w_out=m_w_out,
                     ffn_norm_g=m_ffn_norm_g, w_up=m_w_up, conv_w=m_conv_w, conv_b=m_conv_b, w_down=m_w_down,
                     final_norm_g=m_final_norm_g)
    moments_v = dict(attn_norm_g=v_attn_norm_g, w_in=v_w_in, b_gate=v_b_gate, q_norm_g=v_q_norm_g, w_uq=v_w_uq,
                     kv_norm_g=v_kv_norm_g, w_ukv=v_w_ukv, w_o_mla=v_w_o_mla, w_o_dil=v_w_o_dil, w_out=v_w_out,
                     ffn_norm_g=v_ffn_norm_g, w_up=v_w_up, conv_w=v_conv_w, conv_b=v_conv_b, w_down=v_w_down,
                     final_norm_g=v_final_norm_g)

    fetch, zero = _gather_start({n: given[n][0] for n in BIG + ("conv_w",)})
    W = {n: given[n] for n in ("b_gate", "q_norm_g", "kv_norm_g", "ffn_norm_g", "conv_b")}
    W["attn_norm_g"] = given["attn_norm_g"] + zero
    W["final_norm_g"] = given["final_norm_g"].reshape(1, -1)

    pending = {}

    def emit(i, grads):
        pending[i], token = _reduce_start(i, grads)
        return token

    loss_part, grad_x, small = _local_step(x[0], loss_target[0], W, fetch, emit)
    small["loss"] = loss_part
    grads, delta, new_m, new_v = {}, {}, {}, {}

    def adamw(n, g):
        shp = given[n].shape
        two_d = (-1, shp[-1]) if len(shp) > 1 else (1, -1)
        view = (lambda a: a.reshape(two_d).T) if n in COLUMN_MAJOR else (lambda a: a.reshape(two_d))
        back = (lambda a: a.T.reshape(shp)) if n in COLUMN_MAJOR else (lambda a: a.reshape(shp))
        go, d, nm, nv = _adamw(f"adamw_{n}", view(given[n]), view(g), view(moments_m[n]), view(moments_v[n]))
        grads[n], delta[n], new_m[n], new_v[n] = back(go), back(d), back(nm), back(nv)

    after = grad_x
    for i in range(len(REDUCE_GROUPS)):
        for n, g in _reduce_finish(i, pending[i], after).items():
            adamw(n, g)
        after = delta[REDUCE_GROUPS[i][-1]]
    g_small = _reduce_small(small)
    loss = g_small["loss"][0]
    chip = 2 * lax.axis_index("x") + lax.axis_index("y")
    for n in SMALL:
        if n == "conv_w":
            full = g_small[n].reshape(N_CHIPS, 3, -1)
            adamw(n, lax.dynamic_index_in_dim(full, chip, 0, keepdims=True))
        else:
            adamw(n, g_small[n])

    return (loss, grad_x[None], *[grads[n] for n in WEIGHTS], *[delta[n] for n in WEIGHTS],
            *[new_m[n] for n in WEIGHTS], *[new_v[n] for n in WEIGHTS])
```

```python
import functools
import math

import numpy as np
import jax
import jax.numpy as jnp
from jax import lax
from jax.experimental import pallas as pl
from jax.experimental.pallas import tpu as pltpu

F32 = jnp.float32
BF = jnp.bfloat16
MESH = pl.DeviceIdType.MESH

D_MODEL = 2048
MLA_HEADS = 8
QK_NOPE = 128
QK_ROPE = 64
Q_RANK = 512
KV_RANK = 256
ROPE_THETA = 10000.0
DIL_PATTERNS = ((128, 1), (512, 4), (2048, 16))
DIL_GROUPS = 3
DIL_HPG = 4
DIL_HEADS = 12
HEAD = 128
DIL_BLOCK = 128
ALIBI_MAX_BIAS = 8.0
NORM_EPS = 1e-6
N_CHIPS = 4
ADAM_LR = 0.001
ADAM_B1 = 0.9
ADAM_B2 = 0.999
ADAM_EPS = 1e-08
ADAM_WD = 0.01
ADAM_STEP = 10

LANE = 128
VMEM_LIMIT = 56 * 1024 * 1024
MLA_SCALE = (QK_NOPE + QK_ROPE) ** -0.5
DIL_SCALE = HEAD ** -0.5


def _params(*sem):
    return pltpu.CompilerParams(dimension_semantics=sem, vmem_limit_bytes=VMEM_LIMIT)


def _tile(n, pref):
    t = (pref // LANE) * LANE
    while t >= LANE:
        if n % t == 0:
            return t
        t -= LANE
    return n


NN = (((1,), (0,)), ((), ()))
NT = (((1,), (1,)), ((), ()))
TN = (((0,), (0,)), ((), ()))


def _mm_call(name, a, b, add, *, grid, a_spec, b_spec, add_spec, o_spec, o_shape, o_dtype, acc_shape, dims, nk):
    nax = len(grid)

    def body(*refs):
        if add is None:
            a_ref, b_ref, o_ref = refs[:3]
            c_ref = None
            scr = refs[3:]
        else:
            a_ref, b_ref, c_ref, o_ref = refs[:4]
            scr = refs[4:]
        prod = lax.dot_general(a_ref[...].astype(BF), b_ref[...].astype(BF), dims, preferred_element_type=F32)
        if nk == 1:
            if c_ref is not None:
                prod = prod + c_ref[...]
            o_ref[...] = prod.astype(o_ref.dtype)
        else:
            acc = scr[0]
            k = pl.program_id(nax - 1)

            @pl.when(k == 0)
            def _():
                if c_ref is not None:
                    acc[...] = prod + c_ref[...]
                else:
                    acc[...] = prod

            @pl.when(k > 0)
            def _():
                acc[...] += prod

            @pl.when(k == nk - 1)
            def _():
                o_ref[...] = acc[...].astype(o_ref.dtype)

    ins = [a, b] + ([] if add is None else [add])
    specs = [a_spec, b_spec] + ([] if add is None else [add_spec])
    sem = ("parallel",) * (nax - 1) + ("arbitrary",)
    return pl.pallas_call(
        body, name=name, grid=grid, in_specs=specs, out_specs=o_spec,
        out_shape=jax.ShapeDtypeStruct(o_shape, o_dtype),
        scratch_shapes=[] if nk == 1 else [pltpu.VMEM(acc_shape, F32)],
        compiler_params=_params(*sem),
    )(*ins)


def _mm_nn(name, a, b, *, add=None, o_dtype=F32):
    M, K = a.shape
    sharded = b.ndim == 3
    Ns = b.shape[-1]
    N = Ns * (b.shape[0] if sharded else 1)
    tm, tn, tk = _tile(M, 1024), _tile(Ns, 1024), _tile(K, 2048)
    per = Ns // tn
    nk = K // tk
    if sharded:
        b_spec = pl.BlockSpec((None, tk, tn), lambda i, j, k: (j // per, k, j % per))
    else:
        b_spec = pl.BlockSpec((tk, tn), lambda i, j, k: (k, j))
    return _mm_call(
        name, a, b, add, grid=(M // tm, N // tn, nk),
        a_spec=pl.BlockSpec((tm, tk), lambda i, j, k: (i, k)), b_spec=b_spec,
        add_spec=pl.BlockSpec((tm, tn), lambda i, j, k: (i, j)),
        o_spec=pl.BlockSpec((tm, tn), lambda i, j, k: (i, j)),
        o_shape=(M, N), o_dtype=o_dtype, acc_shape=(tm, tn), dims=NN, nk=nk)


def _mm_nt(name, a, b, *, add=None, o_dtype=F32):
    M, K = a.shape
    sharded = b.ndim == 3
    N, Ks = b.shape[-2], b.shape[-1]
    tm, tn, tk = _tile(M, 1024), _tile(N, 1024), _tile(Ks, 2048)
    per = Ks // tk
    nk = K // tk
    if sharded:
        b_spec = pl.BlockSpec((None, tn, tk), lambda i, j, k: (k // per, j, k % per))
    else:
        b_spec = pl.BlockSpec((tn, tk), lambda i, j, k: (j, k))
    return _mm_call(
        name, a, b, add, grid=(M // tm, N // tn, nk),
        a_spec=pl.BlockSpec((tm, tk), lambda i, j, k: (i, k)), b_spec=b_spec,
        add_spec=pl.BlockSpec((tm, tn), lambda i, j, k: (i, j)),
        o_spec=pl.BlockSpec((tm, tn), lambda i, j, k: (i, j)),
        o_shape=(M, N), o_dtype=o_dtype, acc_shape=(tm, tn), dims=NT, nk=nk)


def _mm_nt_sum(name, a_s, b_s):
    n = len(a_s)
    M, N = a_s[0].shape[0], b_s[0].shape[0]
    tm, tn = _tile(M, 512), _tile(N, 1024)

    def body(*refs):
        acc = None
        for i in range(n):
            prod = lax.dot_general(refs[i][...].astype(BF), refs[n + i][...].astype(BF), NT, preferred_element_type=F32)
            acc = prod if acc is None else acc + prod
        refs[2 * n][...] = acc

    return pl.pallas_call(
        body, name=name, grid=(N // tn, M // tm),
        in_specs=[pl.BlockSpec((tm, a.shape[1]), lambda j, i: (i, 0)) for a in a_s]
        + [pl.BlockSpec((tn, b.shape[1]), lambda j, i: (j, 0)) for b in b_s],
        out_specs=pl.BlockSpec((tm, tn), lambda j, i: (i, j)),
        out_shape=jax.ShapeDtypeStruct((M, N), F32),
        compiler_params=_params("parallel", "parallel"))(*a_s, *b_s)


def _mm_tn(name, a, b, *, shards=1, o_dtype=BF):
    S, M = a.shape
    N = b.shape[1]
    Ns = N // shards
    tm, tn, tk = _tile(M, 1024), _tile(Ns, 1024), _tile(S, 2048)
    per = Ns // tn
    nk = S // tk
    if shards > 1:
        o_spec = pl.BlockSpec((None, tm, tn), lambda i, j, k: (j // per, i, j % per))
        o_shape = (shards, M, Ns)
    else:
        o_spec = pl.BlockSpec((tm, tn), lambda i, j, k: (i, j))
        o_shape = (M, N)
    return _mm_call(
        name, a, b, None, grid=(M // tm, N // tn, nk),
        a_spec=pl.BlockSpec((tk, tm), lambda i, j, k: (k, i)),
        b_spec=pl.BlockSpec((tk, tn), lambda i, j, k: (k, j)),
        add_spec=None, o_spec=o_spec, o_shape=o_shape, o_dtype=o_dtype, acc_shape=(tm, tn), dims=TN, nk=nk)


def _up_fwd(h2, w_up):
    S, D = h2.shape
    G, _, C = w_up.shape
    tm = _tile(S, 512)
    return _mm_call(
        "up_fwd", h2, w_up, None, grid=(G, S // tm, 1),
        a_spec=pl.BlockSpec((tm, D), lambda g, i, k: (i, 0)),
        b_spec=pl.BlockSpec((None, D, C), lambda g, i, k: (g, 0, 0)),
        add_spec=None, o_spec=pl.BlockSpec((None, tm, C), lambda g, i, k: (g, i, 0)),
        o_shape=(G, S, C), o_dtype=BF, acc_shape=None, dims=NN, nk=1)


def _up_dgrad(du_pre, w_up):
    G, S, C = du_pre.shape
    D = w_up.shape[1]
    tm, tn = _tile(S, 1024), _tile(D, 1024)
    return _mm_call(
        "up_dgrad", du_pre, w_up, None, grid=(S // tm, D // tn, G),
        a_spec=pl.BlockSpec((None, tm, C), lambda i, j, g: (g, i, 0)),
        b_spec=pl.BlockSpec((None, tn, C), lambda i, j, g: (g, j, 0)),
        add_spec=None, o_spec=pl.BlockSpec((tm, tn), lambda i, j, g: (i, j)),
        o_shape=(S, D), o_dtype=F32, acc_shape=(tm, tn), dims=NT, nk=G)


def _up_wgrad(h2, du_pre):
    G, S, C = du_pre.shape
    D = h2.shape[1]
    tm, tk = _tile(D, 512), _tile(S, 2048)
    return _mm_call(
        "up_wgrad", h2, du_pre, None, grid=(G, D // tm, S // tk),
        a_spec=pl.BlockSpec((tk, tm), lambda g, i, k: (k, i)),
        b_spec=pl.BlockSpec((None, tk, C), lambda g, i, k: (g, k, 0)),
        add_spec=None, o_spec=pl.BlockSpec((None, tm, C), lambda g, i, k: (g, i, 0)),
        o_shape=(G, D, C), o_dtype=BF, acc_shape=(tm, C), dims=TN, nk=S // tk)


def _down_fwd(act, w_down, x1):
    G, S, C = act.shape
    D = w_down.shape[2]
    tm, tn = _tile(S, 1024), _tile(D, 1024)
    return _mm_call(
        "down_fwd", act, w_down, x1, grid=(S // tm, D // tn, G),
        a_spec=pl.BlockSpec((None, tm, C), lambda i, j, g: (g, i, 0)),
        b_spec=pl.BlockSpec((None, C, tn), lambda i, j, g: (g, 0, j)),
        add_spec=pl.BlockSpec((tm, tn), lambda i, j, g: (i, j)),
        o_spec=pl.BlockSpec((tm, tn), lambda i, j, g: (i, j)),
        o_shape=(S, D), o_dtype=F32, acc_shape=(tm, tn), dims=NN, nk=G)


def _down_dgrad(dx2, w_down):
    S, D = dx2.shape
    G, C, _ = w_down.shape
    tm = _tile(S, 512)
    return _mm_call(
        "down_dgrad", dx2, w_down, None, grid=(G, S // tm, 1),
        a_spec=pl.BlockSpec((tm, D), lambda g, i, k: (i, 0)),
        b_spec=pl.BlockSpec((None, C, D), lambda g, i, k: (g, 0, 0)),
        add_spec=None, o_spec=pl.BlockSpec((None, tm, C), lambda g, i, k: (g, i, 0)),
        o_shape=(G, S, C), o_dtype=BF, acc_shape=None, dims=NT, nk=1)


def _down_wgrad(act, dx2):
    G, S, C = act.shape
    D = dx2.shape[1]
    tn, tk = _tile(D, 512), _tile(S, 1024)
    return _mm_call(
        "down_wgrad", act, dx2, None, grid=(G, D // tn, S // tk),
        a_spec=pl.BlockSpec((None, tk, C), lambda g, j, k: (g, k, 0)),
        b_spec=pl.BlockSpec((tk, tn), lambda g, j, k: (k, j)),
        add_spec=None, o_spec=pl.BlockSpec((None, C, tn), lambda g, j, k: (g, 0, j)),
        o_shape=(G, C, D), o_dtype=BF, acc_shape=(C, tn), dims=TN, nk=S // tk)


def _row(ts, c):
    return pl.BlockSpec((ts, c), lambda i: (i, 0))


def _bcast(r, c):
    return pl.BlockSpec((r, c), lambda i: (0, 0))


def _accumulate(i, ref, val):
    @pl.when(i == 0)
    def _():
        ref[...] = val

    @pl.when(i > 0)
    def _():
        ref[...] += val


def _rstd(xv):
    return lax.rsqrt(jnp.mean(xv * xv, axis=-1, keepdims=True) + NORM_EPS)


def _rmsnorm_fwd(name, x, g):
    S, D = x.shape
    ts = _tile(S, 512)

    def body(x_ref, g_ref, o_ref):
        xv = x_ref[...]
        o_ref[...] = (xv * _rstd(xv) * g_ref[...]).astype(o_ref.dtype)

    return pl.pallas_call(
        body, name=name, grid=(S // ts,), in_specs=[_row(ts, D), _bcast(1, D)], out_specs=_row(ts, D),
        out_shape=jax.ShapeDtypeStruct((S, D), BF), compiler_params=_params("parallel"))(x, g)


def _norm_bwd_rows(dy, xv, g):
    r = _rstd(xv)
    xh = xv * r
    dxh = dy * g
    dx = r * (dxh - xh * jnp.mean(dxh * xh, axis=-1, keepdims=True))
    return dx, jnp.sum(dy * xh, axis=0, keepdims=True)


def _rmsnorm_bwd(name, dy, x, g, res):
    S, D = x.shape
    ts = _tile(S, 512)

    def body(dy_ref, x_ref, g_ref, res_ref, dx_ref, dxb_ref, dg_ref):
        dx, dg = _norm_bwd_rows(dy_ref[...], x_ref[...], g_ref[...])
        dx = dx + res_ref[...]
        dx_ref[...] = dx
        dxb_ref[...] = dx.astype(BF)
        _accumulate(pl.program_id(0), dg_ref, dg)

    return pl.pallas_call(
        body, name=name, grid=(S // ts,),
        in_specs=[_row(ts, D), _row(ts, D), _bcast(1, D), _row(ts, D)],
        out_specs=[_row(ts, D), _row(ts, D), _bcast(1, D)],
        out_shape=[jax.ShapeDtypeStruct((S, D), F32), jax.ShapeDtypeStruct((S, D), BF), jax.ShapeDtypeStruct((1, D), F32)],
        compiler_params=_params("arbitrary"))(dy, x, g, res)


def _rope_tables(S):
    half = QK_ROPE // 2
    pos = jnp.arange(S, dtype=F32)
    inv_freq = ROPE_THETA ** (-jnp.arange(0, QK_ROPE, 2, dtype=F32) / QK_ROPE)
    ang = pos[:, None] * inv_freq[None, :]
    cos, sin = jnp.cos(ang), jnp.sin(ang)
    z = jnp.zeros((S, half), F32)
    return jnp.concatenate([cos, z, cos, z], axis=1), jnp.concatenate([-sin, z, sin, z], axis=1)


def _rope_lanes(x, cos, sin_signed, inverse):
    if inverse:
        return x * cos + pltpu.roll(x * sin_signed, LANE // 2, 1)
    return x * cos + pltpu.roll(x, LANE // 2, 1) * sin_signed


def _rope(name, x, cos, sin_signed, inverse):
    S, W = x.shape
    ts = _tile(S, 512)

    def body(x_ref, c_ref, s_ref, o_ref):
        c, s = c_ref[...], s_ref[...]
        for h in range(W // LANE):
            sl = slice(h * LANE, (h + 1) * LANE)
            o_ref[:, sl] = _rope_lanes(x_ref[:, sl], c, s, inverse).astype(o_ref.dtype)

    return pl.pallas_call(
        body, name=name, grid=(S // ts,), in_specs=[_row(ts, W), _row(ts, LANE), _row(ts, LANE)],
        out_specs=_row(ts, W), out_shape=jax.ShapeDtypeStruct((S, W), BF),
        compiler_params=_params("parallel"))(x, cos, sin_signed)


LAT_W = 1024
_CQ = slice(0, Q_RANK)
_CKV = slice(Q_RANK, Q_RANK + KV_RANK)
_KPE = slice(Q_RANK + KV_RANK, Q_RANK + KV_RANK + LANE)


def _mla_prep(lat, qg, kvg, cos, sin_signed):
    S = lat.shape[0]
    ts = _tile(S, 512)

    def body(lat_ref, qg_ref, kvg_ref, c_ref, s_ref, qn_ref, kvn_ref, kpe_ref):
        cq = lat_ref[:, _CQ]
        qn_ref[...] = (cq * _rstd(cq) * qg_ref[...]).astype(BF)
        ckv = lat_ref[:, _CKV]
        kvn_ref[...] = (ckv * _rstd(ckv) * kvg_ref[...]).astype(BF)
        kpe_ref[...] = _rope_lanes(lat_ref[:, _KPE], c_ref[...], s_ref[...], False).astype(BF)

    return pl.pallas_call(
        body, name="mla_prep", grid=(S // ts,),
        in_specs=[_row(ts, LAT_W), _bcast(1, Q_RANK), _bcast(1, KV_RANK), _row(ts, LANE), _row(ts, LANE)],
        out_specs=[_row(ts, Q_RANK), _row(ts, KV_RANK), _row(ts, LANE)],
        out_shape=[jax.ShapeDtypeStruct((S, Q_RANK), BF), jax.ShapeDtypeStruct((S, KV_RANK), BF),
                   jax.ShapeDtypeStruct((S, LANE), BF)],
        compiler_params=_params("parallel"))(lat, qg, kvg, cos, sin_signed)


def _mla_prep_bwd(lat, qg, kvg, cos, sin_signed, d_qn, d_kvn, d_kpe):
    S = lat.shape[0]
    ts = _tile(S, 512)

    def body(lat_ref, qg_ref, kvg_ref, c_ref, s_ref, dqn_ref, dkvn_ref, dkpe_ref, dlat_ref, dqg_ref, dkvg_ref):
        i = pl.program_id(0)
        dcq, dqg = _norm_bwd_rows(dqn_ref[...], lat_ref[:, _CQ], qg_ref[...])
        dckv, dkvg = _norm_bwd_rows(dkvn_ref[...], lat_ref[:, _CKV], kvg_ref[...])
        dlat_ref[:, _CQ] = dcq.astype(BF)
        dlat_ref[:, _CKV] = dckv.astype(BF)
        dkpe = dkpe_ref[0]
        for g in range(1, d_kpe.shape[0]):
            dkpe = dkpe + dkpe_ref[g]
        dlat_ref[:, _KPE] = _rope_lanes(dkpe, c_ref[...], s_ref[...], True).astype(BF)
        dlat_ref[:, _KPE.stop:] = jnp.zeros((ts, LAT_W - _KPE.stop), BF)
        _accumulate(i, dqg_ref, dqg)
        _accumulate(i, dkvg_ref, dkvg)

    return pl.pallas_call(
        body, name="mla_prep_bwd", grid=(S // ts,),
        in_specs=[_row(ts, LAT_W), _bcast(1, Q_RANK), _bcast(1, KV_RANK), _row(ts, LANE), _row(ts, LANE),
                  _row(ts, Q_RANK), _row(ts, KV_RANK), pl.BlockSpec((d_kpe.shape[0], ts, LANE), lambda i: (0, i, 0))],
        out_specs=[_row(ts, LAT_W), _bcast(1, Q_RANK), _bcast(1, KV_RANK)],
        out_shape=[jax.ShapeDtypeStruct((S, LAT_W), BF), jax.ShapeDtypeStruct((1, Q_RANK), F32),
                   jax.ShapeDtypeStruct((1, KV_RANK), F32)],
        compiler_params=_params("arbitrary"))(lat, qg, kvg, cos, sin_signed, d_qn, d_kvn, d_kpe)


def _sigmoid(z):
    return 1.0 / (1.0 + jnp.exp(-z))


def _merge_fwd(gpre, b_gate, o_a, o_b):
    S, D = o_a.shape
    ts = _tile(S, 256)

    def body(g_ref, b_ref, oa_ref, ob_ref, m_ref):
        ga = _sigmoid(g_ref[:, :D] + b_ref[:, :D])
        gb = _sigmoid(g_ref[:, D:] + b_ref[:, D:])
        m_ref[...] = (ga * oa_ref[...] + gb * ob_ref[...]).astype(BF)

    return pl.pallas_call(
        body, name="merge_fwd", grid=(S // ts,),
        in_specs=[_row(ts, 2 * D), _bcast(1, 2 * D), _row(ts, D), _row(ts, D)], out_specs=_row(ts, D),
        out_shape=jax.ShapeDtypeStruct((S, D), BF), compiler_params=_params("parallel"))(gpre, b_gate, o_a, o_b)


def _merge_bwd(d_merge, gpre, b_gate, o_a, o_b):
    S, D = o_a.shape
    ts = _tile(S, 256)

    def body(dm_ref, g_ref, b_ref, oa_ref, ob_ref, doa_ref, dob_ref, dg_ref, db_ref):
        dm = dm_ref[...]
        ga = _sigmoid(g_ref[:, :D] + b_ref[:, :D])
        gb = _sigmoid(g_ref[:, D:] + b_ref[:, D:])
        doa_ref[...] = (dm * ga).astype(BF)
        dob_ref[...] = (dm * gb).astype(BF)
        dga = dm * oa_ref[...] * ga * (1.0 - ga)
        dgb = dm * ob_ref[...] * gb * (1.0 - gb)
        dg_ref[:, :D] = dga.astype(BF)
        dg_ref[:, D:] = dgb.astype(BF)
        i = pl.program_id(0)
        part = jnp.concatenate([jnp.sum(dga, axis=0, keepdims=True), jnp.sum(dgb, axis=0, keepdims=True)], axis=1)
        _accumulate(i, db_ref, part)

    return pl.pallas_call(
        body, name="merge_bwd", grid=(S // ts,),
        in_specs=[_row(ts, D), _row(ts, 2 * D), _bcast(1, 2 * D), _row(ts, D), _row(ts, D)],
        out_specs=[_row(ts, D), _row(ts, D), _row(ts, 2 * D), _bcast(1, 2 * D)],
        out_shape=[jax.ShapeDtypeStruct((S, D), BF), jax.ShapeDtypeStruct((S, D), BF),
                   jax.ShapeDtypeStruct((S, 2 * D), BF), jax.ShapeDtypeStruct((1, 2 * D), F32)],
        compiler_params=_params("arbitrary"))(d_merge, gpre, b_gate, o_a, o_b)


def _final_loss(x2, tgt, gf):
    S, D = x2.shape
    ts = _tile(S, 512)

    def body(x_ref, t_ref, g_ref, dx_ref, dxb_ref, dg_ref, loss_ref):
        i = pl.program_id(0)
        xv = x_ref[...]
        g = g_ref[...]
        y = xv * _rstd(xv) * g
        err = y - t_ref[...]
        dx, dg = _norm_bwd_rows(err * (1.0 / D), xv, g)
        dx_ref[...] = dx
        dxb_ref[...] = dx.astype(BF)
        _accumulate(i, dg_ref, dg)
        part = 0.5 * jnp.sum(jnp.mean(err * err, axis=-1, keepdims=True), axis=0, keepdims=True)
        _accumulate(i, loss_ref, jnp.broadcast_to(part, (8, LANE)))

    return pl.pallas_call(
        body, name="final_loss", grid=(S // ts,),
        in_specs=[_row(ts, D), _row(ts, D), _bcast(1, D)],
        out_specs=[_row(ts, D), _row(ts, D), _bcast(1, D), _bcast(8, LANE)],
        out_shape=[jax.ShapeDtypeStruct((S, D), F32), jax.ShapeDtypeStruct((S, D), BF), jax.ShapeDtypeStruct((1, D), F32),
                   jax.ShapeDtypeStruct((8, LANE), F32)],
        compiler_params=_params("arbitrary"))(x2, tgt, gf)


HALO = 16


SUB = 8


def _shift_down(cur, prev, k, rows):
    out = pltpu.roll(cur, k, 0)
    head = out[:SUB]
    for j in range(k):
        head = jnp.where(rows == j, prev[HALO - k + j:HALO - k + j + 1, :], head)
    return jnp.concatenate([head, out[SUB:]], axis=0)


def _shift_up(cur, nxt, k, rows, ts):
    out = pltpu.roll(cur, ts - k, 0)
    tail = out[ts - SUB:]
    for j in range(k):
        tail = jnp.where(rows == SUB - k + j, nxt[j:j + 1, :], tail)
    return jnp.concatenate([out[:ts - SUB], tail], axis=0)


def _conv_rows(cur, prev, w, b, rows):
    return b + w[0:1, :] * _shift_down(cur, prev, 2, rows) + w[1:2, :] * _shift_down(cur, prev, 1, rows) + w[2:3, :] * cur


def _conv_specs(ts, C, shard_of):
    nh = ts // HALO
    cur = pl.BlockSpec((None, ts, C), lambda g, i: (shard_of(g), i, 0))
    prev = pl.BlockSpec((None, HALO, C), lambda g, i: (shard_of(g), jnp.maximum(i * nh - 1, 0), 0))
    return cur, prev


def _ffn_act(u_pre, conv_w, conv_b):
    G4, S, C = u_pre.shape
    G = G4 // 2
    ts = _tile(S, 256)

    def body(up_ref, upp_ref, gt_ref, gtp_ref, wu_ref, wg_ref, bu_ref, bg_ref, act_ref, u_ref):
        first = pl.program_id(1) == 0
        rows = lax.broadcasted_iota(jnp.int32, (SUB, C), 0)
        pu = jnp.where(first, 0.0, upp_ref[...].astype(F32))
        pg = jnp.where(first, 0.0, gtp_ref[...].astype(F32))
        up = _conv_rows(up_ref[...].astype(F32), pu, wu_ref[...], bu_ref[...], rows)
        gate = _conv_rows(gt_ref[...].astype(F32), pg, wg_ref[...], bg_ref[...], rows)
        act_ref[...] = (gate * _sigmoid(gate) * up).astype(BF)
        u_ref[0] = up.astype(BF)
        u_ref[1] = gate.astype(BF)

    cur_u, prev_u = _conv_specs(ts, C, lambda g: g)
    cur_g, prev_g = _conv_specs(ts, C, lambda g: g + G)
    w_u = pl.BlockSpec((None, 3, C), lambda g, i: (g, 0, 0))
    w_g = pl.BlockSpec((None, 3, C), lambda g, i: (g + G, 0, 0))
    b_u = pl.BlockSpec((None, 1, C), lambda g, i: (g, 0, 0))
    b_g = pl.BlockSpec((None, 1, C), lambda g, i: (g + G, 0, 0))
    pair = pl.BlockSpec((2, None, ts, C), lambda g, i: (0, g, i, 0))
    act, u = pl.pallas_call(
        body, name="ffn_act", grid=(G, S // ts),
        in_specs=[cur_u, prev_u, cur_g, prev_g, w_u, w_g, b_u, b_g],
        out_specs=[pl.BlockSpec((None, ts, C), lambda g, i: (g, i, 0)), pair],
        out_shape=[jax.ShapeDtypeStruct((G, S, C), BF), jax.ShapeDtypeStruct((2, G, S, C), BF)],
        compiler_params=_params("parallel", "parallel"))(u_pre, u_pre, u_pre, u_pre, conv_w, conv_w, conv_b, conv_b)
    return act, u


def _ffn_act_conv_bwd(u, d_act, u_pre, conv_w):
    _, G, S, C = u.shape
    ts = _tile(S, 256)
    nh = ts // HALO
    last_halo = S // HALO - 1

    def d_outputs(up, gate, da):
        sg = _sigmoid(gate)
        return da * (gate * sg), da * up * (sg * (1.0 + gate * (1.0 - sg)))

    def body(u_ref, un_ref, da_ref, dan_ref, x_ref, w_ref, dpre_ref, dw_ref, db_ref):
        i = pl.program_id(1)
        rows = lax.broadcasted_iota(jnp.int32, (SUB, C), 0)
        da_n = jnp.where(i == pl.num_programs(1) - 1, 0.0, dan_ref[...].astype(F32))
        du = d_outputs(u_ref[0].astype(F32), u_ref[1].astype(F32), da_ref[...].astype(F32))
        du_n = d_outputs(un_ref[0].astype(F32), un_ref[1].astype(F32), da_n)
        for kind in range(2):
            du_c, w = du[kind], w_ref[kind]
            up1, up2 = _shift_up(du_c, du_n[kind], 1, rows, ts), _shift_up(du_c, du_n[kind], 2, rows, ts)
            dpre_ref[kind] = (w[2:3, :] * du_c + w[1:2, :] * up1 + w[0:1, :] * up2).astype(BF)
            x_c = x_ref[kind].astype(F32)
            dw = jnp.concatenate([
                jnp.sum(up2 * x_c, axis=0, keepdims=True),
                jnp.sum(up1 * x_c, axis=0, keepdims=True),
                jnp.sum(du_c * x_c, axis=0, keepdims=True)], axis=0)
            db = jnp.sum(du_c, axis=0, keepdims=True)

            @pl.when(i == 0)
            def _():
                dw_ref[kind] = dw
                db_ref[kind] = db

            @pl.when(i > 0)
            def _():
                dw_ref[kind] += dw
                db_ref[kind] += db

    def nxt_row(i):
        return jnp.minimum((i + 1) * nh, last_halo)

    pair = pl.BlockSpec((2, None, ts, C), lambda g, i: (0, g, i, 0))
    pair_n = pl.BlockSpec((2, None, HALO, C), lambda g, i: (0, g, nxt_row(i), 0))
    one = pl.BlockSpec((None, ts, C), lambda g, i: (g, i, 0))
    one_n = pl.BlockSpec((None, HALO, C), lambda g, i: (g, nxt_row(i), 0))
    w_spec = pl.BlockSpec((2, None, 3, C), lambda g, i: (0, g, 0, 0))
    b_spec = pl.BlockSpec((2, None, 1, C), lambda g, i: (0, g, 0, 0))
    dpre, dw, db = pl.pallas_call(
        body, name="ffn_act_conv_bwd", grid=(G, S // ts),
        in_specs=[pair, pair_n, one, one_n, pair, w_spec], out_specs=[pair, w_spec, b_spec],
        out_shape=[jax.ShapeDtypeStruct((2, G, S, C), BF), jax.ShapeDtypeStruct((2, G, 3, C), F32),
                   jax.ShapeDtypeStruct((2, G, 1, C), F32)],
        compiler_params=_params("parallel", "arbitrary"))(
            u, u, d_act, d_act, u_pre.reshape(2, G, S, C), conv_w.reshape(2, G, 3, C))
    return dpre.reshape(2 * G, S, C), dw.reshape(2 * G, 3, C), db.reshape(2 * G, 1, C)


MLA_T = 1024
MLA_HB = 4
MLA_BWD_HB = 2


def _mla_pairs(n, by_row):
    if by_row:
        pairs = [(i, j) for i in range(n) for j in range(i + 1)]
    else:
        pairs = [(i, j) for j in range(n) for i in range(j, n)]
    return jnp.asarray([p[0] for p in pairs], jnp.int32), jnp.asarray([p[1] for p in pairs], jnp.int32)


def _mla_specs(hb):
    q = pl.BlockSpec((MLA_T, hb * HEAD), lambda g, t, it, jt: (it[t], g))
    k = pl.BlockSpec((MLA_T, hb * HEAD), lambda g, t, it, jt: (jt[t], g))
    kpe = pl.BlockSpec((MLA_T, HEAD), lambda g, t, it, jt: (jt[t], 0))
    lse = pl.BlockSpec((hb, MLA_T, LANE), lambda g, t, it, jt: (g, it[t], 0))
    return q, k, kpe, lse


def _mla_head(ref, hh):
    return ref[:, hh * HEAD:(hh + 1) * HEAD]


LOG2E = math.log2(math.e)
MLA_EXP2_SCALE = MLA_SCALE * LOG2E


def _mla_scores(qn_ref, qp_ref, kn_ref, kpe, hh, ok):
    q = jnp.concatenate([_mla_head(qn_ref, hh), _mla_head(qp_ref, hh)], axis=1)
    k = jnp.concatenate([_mla_head(kn_ref, hh), kpe], axis=1)
    s = lax.dot_general(q, k, NT, preferred_element_type=F32)
    return q, k, s if ok is None else jnp.where(ok, s, -jnp.inf)


def _mla_diagonal_mask():
    row = lax.broadcasted_iota(jnp.int32, (MLA_T, MLA_T), 0)
    col = lax.broadcasted_iota(jnp.int32, (MLA_T, MLA_T), 1)
    return col <= row


def _mla_step(i, j, step):
    @pl.when(j < i)
    def _():
        step(None)

    @pl.when(j == i)
    def _():
        step(_mla_diagonal_mask())


def _mla_fwd(qn, qp, kn, kpe, v):
    S = qn.shape[0]
    it, jt = _mla_pairs(S // MLA_T, True)

    def body(it_ref, jt_ref, qn_ref, qp_ref, kn_ref, kpe_ref, v_ref, o_ref, lse_ref, m_scr, acc_scr):
        t = pl.program_id(1)
        i, j = it_ref[t], jt_ref[t]

        @pl.when(j == 0)
        def _():
            m_scr[...] = jnp.full(m_scr.shape, -jnp.inf, F32)
            acc_scr[...] = jnp.zeros(acc_scr.shape, F32)

        def step(ok):
            kpe_v = kpe_ref[...]
            ones = jnp.ones((MLA_T, HEAD), BF)
            state = [(m_scr[hh], acc_scr[hh]) for hh in range(MLA_HB)]
            new = []
            for hh in range(MLA_HB):
                m_prev, acc = state[hh]
                _, _, s = _mla_scores(qn_ref, qp_ref, kn_ref, kpe_v, hh, ok)
                m_new = jnp.maximum(m_prev, jnp.max(s, axis=1, keepdims=True))
                p = jnp.exp2((s - m_new) * MLA_EXP2_SCALE).astype(BF)
                v1 = jnp.concatenate([_mla_head(v_ref, hh), ones], axis=1)
                alpha = jnp.exp2((m_prev - m_new) * MLA_EXP2_SCALE)
                new.append((m_new, alpha * acc + lax.dot_general(p, v1, NN, preferred_element_type=F32)))
            for hh in range(MLA_HB):
                m_scr[hh], acc_scr[hh] = new[hh]

        _mla_step(i, j, step)

        @pl.when(j == i)
        def _():
            for hh in range(MLA_HB):
                l = acc_scr[hh, :, HEAD:]
                o_ref[:, hh * HEAD:(hh + 1) * HEAD] = (acc_scr[hh, :, :HEAD] / l).astype(BF)
                lse_ref[hh] = m_scr[hh] * MLA_SCALE + jnp.log(l)

    qspec, kspec, kpespec, lsespec = _mla_specs(MLA_HB)
    grid_spec = pltpu.PrefetchScalarGridSpec(
        num_scalar_prefetch=2, grid=(MLA_HEADS // MLA_HB, it.shape[0]),
        in_specs=[qspec, qspec, kspec, kpespec, kspec], out_specs=[qspec, lsespec],
        scratch_shapes=[pltpu.VMEM((MLA_HB, MLA_T, 1), F32), pltpu.VMEM((MLA_HB, MLA_T, 2 * HEAD), F32)])
    return pl.pallas_call(
        body, name="mla_fwd", grid_spec=grid_spec,
        out_shape=[jax.ShapeDtypeStruct((S, MLA_HEADS * HEAD), BF), jax.ShapeDtypeStruct((MLA_HEADS, S, LANE), F32)],
        compiler_params=_params("parallel", "arbitrary"))(it, jt, qn, qp, kn, kpe, v)


def _mla_p_ds(qn_ref, qp_ref, kn_ref, kpe, v_ref, do_ref, o_ref, lse_ref, hh, ok):
    q, k, s = _mla_scores(qn_ref, qp_ref, kn_ref, kpe, hh, ok)
    p = jnp.exp2(s * MLA_EXP2_SCALE - lse_ref[hh][:, 0:1] * LOG2E)
    do = _mla_head(do_ref, hh)
    delta = jnp.sum(do.astype(F32) * _mla_head(o_ref, hh).astype(F32), axis=1, keepdims=True)
    dp = lax.dot_general(do, _mla_head(v_ref, hh), NT, preferred_element_type=F32)
    ds = p * (dp - delta) * MLA_SCALE
    return q, k, p, ds, do


def _mla_bwd(qn, qp, kn, kpe, v, do, o, lse):
    S = qn.shape[0]
    nq = S // MLA_T
    hb = MLA_BWD_HB
    it, jt = _mla_pairs(nq, False)

    def body(it_ref, jt_ref, qn_ref, qp_ref, kn_ref, kpe_ref, v_ref, do_ref, o_ref, lse_ref,
             dqn_ref, dqp_ref, dkn_ref, dv_ref, dkpe_ref, dq_acc, dk_acc, dv_acc, stage_n, stage_p, osem):
        t = pl.program_id(1)
        i, j = it_ref[t], jt_ref[t]

        @pl.when(t == 0)
        def _():
            dq_acc[...] = jnp.zeros(dq_acc.shape, F32)

        @pl.when(i == j)
        def _():
            dk_acc[...] = jnp.zeros(dk_acc.shape, F32)
            dv_acc[...] = jnp.zeros(dv_acc.shape, F32)

        def step(ok):
            kpe_v = kpe_ref[...]
            for hh in range(hb):
                q, k, p, ds, do_h = _mla_p_ds(qn_ref, qp_ref, kn_ref, kpe_v, v_ref, do_ref, o_ref, lse_ref, hh, ok)
                ds = ds.astype(BF)
                dv_acc[hh] += lax.dot_general(p.astype(BF), do_h, TN, preferred_element_type=F32)
                dk_acc[hh] += lax.dot_general(ds, q, TN, preferred_element_type=F32)
                dq_acc[i, hh] += lax.dot_general(ds, k, NN, preferred_element_type=F32)

        _mla_step(i, j, step)

        @pl.when(i == j)
        def _():
            for hh in range(hb):
                stage_n[:, hh * HEAD:(hh + 1) * HEAD] = dq_acc[i, hh, :, :HEAD].astype(BF)
                stage_p[:, hh * HEAD:(hh + 1) * HEAD] = dq_acc[i, hh, :, HEAD:]
            rows = pl.ds(pl.multiple_of(i * MLA_T, MLA_T), MLA_T)
            cols = pl.ds(pl.multiple_of(pl.program_id(0) * hb * HEAD, LANE), hb * HEAD)
            out_n = pltpu.make_async_copy(stage_n, dqn_ref.at[rows, cols], osem.at[0])
            out_p = pltpu.make_async_copy(stage_p, dqp_ref.at[rows, cols], osem.at[1])
            out_n.start()
            out_p.start()
            out_n.wait()
            out_p.wait()

        @pl.when(i == nq - 1)
        def _():
            dkpe = dk_acc[0, :, HEAD:]
            for hh in range(hb):
                dkn_ref[:, hh * HEAD:(hh + 1) * HEAD] = dk_acc[hh, :, :HEAD].astype(BF)
                dv_ref[:, hh * HEAD:(hh + 1) * HEAD] = dv_acc[hh].astype(BF)
                if hh:
                    dkpe = dkpe + dk_acc[hh, :, HEAD:]
            dkpe_ref[...] = dkpe

    qspec, kspec, kpespec, lsespec = _mla_specs(hb)
    dkpespec = pl.BlockSpec((None, MLA_T, HEAD), lambda g, t, it, jt: (g, jt[t], 0))
    grid_spec = pltpu.PrefetchScalarGridSpec(
        num_scalar_prefetch=2, grid=(MLA_HEADS // hb, it.shape[0]),
        in_specs=[qspec, qspec, kspec, kpespec, kspec, qspec, qspec, lsespec],
        out_specs=[ANY, ANY, kspec, kspec, dkpespec],
        scratch_shapes=[pltpu.VMEM((nq, hb, MLA_T, 2 * HEAD), F32), pltpu.VMEM((hb, MLA_T, 2 * HEAD), F32),
                        pltpu.VMEM((hb, MLA_T, HEAD), F32), pltpu.VMEM((MLA_T, hb * HEAD), BF),
                        pltpu.VMEM((MLA_T, hb * HEAD), F32), pltpu.SemaphoreType.DMA((2,))])
    wide = jax.ShapeDtypeStruct((S, MLA_HEADS * HEAD), BF)
    return pl.pallas_call(
        body, name="mla_bwd", grid_spec=grid_spec,
        out_shape=[wide, jax.ShapeDtypeStruct((S, MLA_HEADS * HEAD), F32), wide, wide,
                   jax.ShapeDtypeStruct((MLA_HEADS // hb, S, HEAD), F32)],
        compiler_params=_params("parallel", "arbitrary"))(it, jt, qn, qp, kn, kpe, v, do, o, lse)


DIL_W = 3 * DIL_HPG * HEAD
DIL_O = DIL_HPG * HEAD
DIL_STEP_BLOCKS = 4


def _dil_slopes(g):
    return [2.0 ** (-ALIBI_MAX_BIAS * (g * DIL_HPG + hh + 1) / DIL_HEADS) for hh in range(DIL_HPG)]


def _dil_bias(dil):
    p = lax.broadcasted_iota(jnp.int32, (DIL_BLOCK, DIL_BLOCK), 0)
    kk = lax.broadcasted_iota(jnp.int32, (DIL_BLOCK, DIL_BLOCK), 1)
    jc = p - kk
    dist_c = (dil * jc).astype(F32)
    dist_p = (dil * (jc + DIL_BLOCK)).astype(F32)
    return jc >= 0, jc <= 0, dist_c, dist_p


def _dil_bias2(dil):
    p = lax.broadcasted_iota(jnp.int32, (DIL_BLOCK, 2 * DIL_BLOCK), 0)
    kk = lax.broadcasted_iota(jnp.int32, (DIL_BLOCK, 2 * DIL_BLOCK), 1)
    j = p + DIL_BLOCK - kk
    return (j >= 0) & (j <= DIL_BLOCK), kk < DIL_BLOCK, (dil * j).astype(F32)


def _dil_head(blk, hh):
    q = blk[:, hh * HEAD:(hh + 1) * HEAD]
    k = blk[:, DIL_O + hh * HEAD:DIL_O + (hh + 1) * HEAD]
    v = blk[:, 2 * DIL_O + hh * HEAD:2 * DIL_O + (hh + 1) * HEAD]
    return q, k, v


def _dil_s(q, k, slope, dist, ok):
    s = lax.dot_general(q, k, NT, preferred_element_type=F32) * DIL_SCALE - slope * dist
    return jnp.where(ok, s, -jnp.inf)


def _dil_view(a, dil):
    S, W = a.shape
    return a.reshape(S // dil, dil * W)


def _dil_fwd(qkv, g):
    _, dil = DIL_PATTERNS[g]
    S = qkv.shape[0]
    L = S // dil
    nb = L // DIL_BLOCK
    slopes = _dil_slopes(g)

    bb = min(DIL_STEP_BLOCKS, nb)
    rows = bb * DIL_BLOCK

    def body(cur_ref, prev_ref, o_ref, lse_ref):
        n = pl.program_id(1)
        ok, in_prev, dist = _dil_bias2(dil)
        for b in range(bb):
            if b == 0:
                both = jnp.concatenate([prev_ref[...], cur_ref[0:DIL_BLOCK, :]], axis=0)
                ok_b = ok & (~in_prev | (n > 0))
            else:
                both = cur_ref[(b - 1) * DIL_BLOCK:(b + 1) * DIL_BLOCK, :]
                ok_b = ok
            for hh in range(DIL_HPG):
                q, _, _ = _dil_head(both[DIL_BLOCK:], hh)
                _, k2, v2 = _dil_head(both, hh)
                s = _dil_s(q, k2, slopes[hh], dist, ok_b)
                m = jnp.max(s, axis=1, keepdims=True)
                p = jnp.exp(s - m)
                l = jnp.sum(p, axis=1, keepdims=True)
                o = lax.dot_general(p.astype(BF), v2, NN, preferred_element_type=F32) / l
                rs, sl = slice(b * DIL_BLOCK, (b + 1) * DIL_BLOCK), slice(hh * HEAD, (hh + 1) * HEAD)
                o_ref[rs, sl] = o
                lse_ref[rs, sl] = jnp.broadcast_to(m + jnp.log(l), (DIL_BLOCK, HEAD))

    ospec = pl.BlockSpec((rows, DIL_O), lambda r, n: (n, r))
    o, lse = pl.pallas_call(
        body, name=f"dil_fwd{g}", grid=(dil, nb // bb),
        in_specs=[pl.BlockSpec((rows, DIL_W), lambda r, n: (n, r)),
                  pl.BlockSpec((DIL_BLOCK, DIL_W), lambda r, n: (jnp.maximum(n * bb - 1, 0), r))],
        out_specs=[ospec, ospec],
        out_shape=[jax.ShapeDtypeStruct((L, dil * DIL_O), F32), jax.ShapeDtypeStruct((L, dil * DIL_O), F32)],
        compiler_params=_params("parallel", "parallel"))(_dil_view(qkv, dil), _dil_view(qkv, dil))
    return o.reshape(S, DIL_O), lse.reshape(S, DIL_O)


def _dil_combine(os_, lses):
    S = os_[0].shape[0]
    ts = _tile(S, 512)

    def body(o0, o1, o2, l0, l1, l2, out_ref, lse_ref):
        a, b, c = l0[...], l1[...], l2[...]
        m = jnp.maximum(jnp.maximum(a, b), c)
        ea, eb, ec = jnp.exp(a - m), jnp.exp(b - m), jnp.exp(c - m)
        tot = ea + eb + ec
        out_ref[...] = ((ea * o0[...] + eb * o1[...] + ec * o2[...]) / tot).astype(BF)
        lse_ref[...] = m + jnp.log(tot)

    return pl.pallas_call(
        body, name="dil_combine", grid=(S // ts,), in_specs=[_row(ts, DIL_O)] * 6,
        out_specs=[_row(ts, DIL_O), _row(ts, DIL_O)],
        out_shape=[jax.ShapeDtypeStruct((S, DIL_O), BF), jax.ShapeDtypeStruct((S, DIL_O), F32)],
        compiler_params=_params("parallel"))(*os_, *lses)


def _dil_delta(do, out):
    S = do.shape[0]
    ts = _tile(S, 512)

    def body(do_ref, o_ref, d_ref):
        for hh in range(DIL_HPG):
            sl = slice(hh * HEAD, (hh + 1) * HEAD)
            d = jnp.sum(do_ref[:, sl].astype(F32) * o_ref[:, sl].astype(F32), axis=1, keepdims=True)
            d_ref[:, sl] = jnp.broadcast_to(d, (ts, HEAD))

    return pl.pallas_call(
        body, name="dil_delta", grid=(S // ts,), in_specs=[_row(ts, DIL_O)] * 2, out_specs=_row(ts, DIL_O),
        out_shape=jax.ShapeDtypeStruct((S, DIL_O), F32), compiler_params=_params("parallel"))(do, out)


def _dil_bwd(qkv, do, lse, delta, g):
    _, dil = DIL_PATTERNS[g]
    S = qkv.shape[0]
    L = S // dil
    nb = L // DIL_BLOCK
    slopes = _dil_slopes(g)

    def pair(q, k, v, do_h, lse_h, delta_h, slope, dist, ok):
        s = _dil_s(q, k, slope, dist, ok)
        p = jnp.exp(s - lse_h)
        dp = lax.dot_general(do_h, v, NT, preferred_element_type=F32)
        ds = (p * (dp - delta_h) * DIL_SCALE).astype(BF)
        return p.astype(BF), ds

    bb = min(DIL_STEP_BLOCKS, nb)
    rows = bb * DIL_BLOCK
    steps = nb // bb

    def body(cur_ref, prev_ref, next_ref, doc_ref, don_ref, lsec_ref, lsen_ref, dlc_ref, dln_ref, out_ref):
        n = pl.program_id(1)
        ok2, in_prev, dist2 = _dil_bias2(dil)
        _, ok_p0, _, dist_p = _dil_bias(dil)
        for b in range(bb):
            rs = slice(b * DIL_BLOCK, (b + 1) * DIL_BLOCK)
            rn = slice((b + 1) * DIL_BLOCK, (b + 2) * DIL_BLOCK)
            two = slice(b * DIL_BLOCK, (b + 2) * DIL_BLOCK)
            first, last = b == 0, b == bb - 1
            if first:
                keys = jnp.concatenate([prev_ref[...], cur_ref[rs, :]], axis=0)
                ok_ab = ok2 & (~in_prev | (n > 0))
            else:
                keys = cur_ref[(b - 1) * DIL_BLOCK:(b + 1) * DIL_BLOCK, :]
                ok_ab = ok2
            qrows = jnp.concatenate([cur_ref[rs, :], next_ref[...]], axis=0) if last else cur_ref[two, :]
            ok_n = ok_p0 & (n < steps - 1) if last else ok_p0
            for hh in range(DIL_HPG):
                sl = slice(hh * HEAD, (hh + 1) * HEAD)
                q2, _, _ = _dil_head(qrows, hh)
                _, k2, v2 = _dil_head(keys, hh)
                q, qn, kc, vc = q2[:DIL_BLOCK], q2[DIL_BLOCK:], k2[DIL_BLOCK:], v2[DIL_BLOCK:]
                do2 = jnp.concatenate([doc_ref[rs, sl], don_ref[:, sl]], axis=0) if last else doc_ref[two, sl]
                do_c, do_n = do2[:DIL_BLOCK], do2[DIL_BLOCK:]
                lse_c = lsec_ref[rs, sl][:, 0:1]
                lse_n = (lsen_ref[:, sl] if last else lsec_ref[rn, sl])[:, 0:1]
                dl_c = dlc_ref[rs, sl][:, 0:1]
                dl_n = (dln_ref[:, sl] if last else dlc_ref[rn, sl])[:, 0:1]
                p_ab, ds_ab = pair(q, k2, v2, do_c, lse_c, dl_c, slopes[hh], dist2, ok_ab)
                p_n, ds_n = pair(qn, kc, vc, do_n, lse_n, dl_n, slopes[hh], dist_p, ok_n)
                dq = lax.dot_general(ds_ab, k2, NN, preferred_element_type=F32)
                dk = lax.dot_general(jnp.concatenate([ds_ab[:, DIL_BLOCK:], ds_n], axis=0), q2, TN, preferred_element_type=F32)
                dv = lax.dot_general(jnp.concatenate([p_ab[:, DIL_BLOCK:], p_n], axis=0), do2, TN, preferred_element_type=F32)
                out_ref[rs, sl] = dq.astype(BF)
                out_ref[rs, DIL_O + hh * HEAD:DIL_O + (hh + 1) * HEAD] = dk.astype(BF)
                out_ref[rs, 2 * DIL_O + hh * HEAD:2 * DIL_O + (hh + 1) * HEAD] = dv.astype(BF)

    cur_w = pl.BlockSpec((rows, DIL_W), lambda r, n: (n, r))
    prev_w = pl.BlockSpec((DIL_BLOCK, DIL_W), lambda r, n: (jnp.maximum(n * bb - 1, 0), r))
    next_w = pl.BlockSpec((DIL_BLOCK, DIL_W), lambda r, n: (jnp.minimum((n + 1) * bb, nb - 1), r))
    cur_o = pl.BlockSpec((rows, DIL_O), lambda r, n: (n, r))
    next_o = pl.BlockSpec((DIL_BLOCK, DIL_O), lambda r, n: (jnp.minimum((n + 1) * bb, nb - 1), r))
    qv, dov, lsev, dlv = _dil_view(qkv, dil), _dil_view(do, dil), _dil_view(lse, dil), _dil_view(delta, dil)
    out = pl.pallas_call(
        body, name=f"dil_bwd{g}", grid=(dil, steps),
        in_specs=[cur_w, prev_w, next_w, cur_o, next_o, cur_o, next_o, cur_o, next_o],
        out_specs=cur_w, out_shape=jax.ShapeDtypeStruct((L, dil * DIL_W), BF),
        compiler_params=_params("parallel", "parallel"))(qv, qv, qv, dov, dov, lsev, lsev, dlv, dlv)
    return out.reshape(S, DIL_W)


def _adamw(name, w, g, m, v):
    R, C = w.shape
    tr, tc = _adamw_block(R, C)

    def body(w_ref, g_ref, m_ref, v_ref, go_ref, d_ref, nm_ref, nv_ref):
        gv = g_ref[...]
        go_ref[...] = gv
        nm = ADAM_B1 * m_ref[...] + (1.0 - ADAM_B1) * gv
        nv = ADAM_B2 * v_ref[...] + (1.0 - ADAM_B2) * (gv * gv)
        m_hat = nm / (1.0 - ADAM_B1 ** ADAM_STEP)
        v_hat = nv / (1.0 - ADAM_B2 ** ADAM_STEP)
        d_ref[...] = -ADAM_LR * (m_hat / (jnp.sqrt(v_hat) + ADAM_EPS) + ADAM_WD * w_ref[...])
        nm_ref[...] = nm
        nv_ref[...] = nv

    spec = pl.BlockSpec((tr, tc), lambda i, j: (i, j))
    shp = jax.ShapeDtypeStruct((R, C), F32)
    return pl.pallas_call(
        body, name=name, grid=(R // tr, C // tc), in_specs=[spec] * 4, out_specs=[spec] * 4, out_shape=[shp] * 4,
        compiler_params=_params("parallel", "parallel"))(w, g, m, v)


ADAMW_BLOCK_ELEMS = 704 * 1024


def _adamw_block(R, C):
    if R * C <= ADAMW_BLOCK_ELEMS:
        return R, C
    tr = _tile_rows(R, max(8, ADAMW_BLOCK_ELEMS // C))
    tc = _tile(C, max(LANE, ADAMW_BLOCK_ELEMS // R))
    if tr * C >= R * tc or R * tc > ADAMW_BLOCK_ELEMS:
        return tr, C
    return R, tc


def _tile_rows(n, pref, mult=8):
    t = (pref // mult) * mult
    while t >= mult:
        if n % t == 0:
            return t
        t -= mult
    return n


ANY = pl.BlockSpec(memory_space=pl.ANY)


def _place():
    x, y, c = lax.axis_index("x"), lax.axis_index("y"), lax.axis_index("c")
    chips = [(1 - x, y), (x, 1 - y), (1 - x, 1 - y)]
    chip_idx = [2 * cx + cy for cx, cy in chips]
    return x, y, c, 2 * x + y, chips, chip_idx


def _rcopy(src, dst, ssem, rsem, dev):
    return pltpu.make_async_remote_copy(src_ref=src, dst_ref=dst, send_sem=ssem, recv_sem=rsem,
                                        device_id=dev, device_id_type=MESH)


HBM = pl.BlockSpec(memory_space=pltpu.HBM)
SEM = pl.BlockSpec(memory_space=pltpu.SEMAPHORE)
EFFECT = pltpu.SideEffectType.DATAFLOW_SIDE_EFFECTING


def _split_copies(kind, srcs, lands, ssem, rsem):
    _, _, c, me, chips, chip_idx = _place()
    cps = []
    for i in range(len(srcs)):
        for k in range(3):
            if kind == "gather":
                rows = srcs[i].shape[0]
                if rows == lands[i].shape[1]:
                    src, dst = srcs[i], lands[i].at[me]
                else:
                    src, dst = srcs[i], lands[i].at[me, pl.ds(pl.multiple_of(c * rows, 16), rows)]
            else:
                src, dst = srcs[i].at[chip_idx[k]], lands[i].at[k]
            cps.append(_rcopy(src, dst, ssem.at[3 * i + k], rsem.at[3 * i + k], (*chips[k], c)))
    return cps


def _exchange_start(name, kind, srcs, lands, groups):
    n, ng = len(srcs), len(groups)

    def body(*refs):
        src_refs, land_refs = refs[:n], refs[n:2 * n]
        sems = refs[2 * n:2 * n + 2 * ng]
        token = refs[-1]
        for gi, grp in enumerate(groups):
            cps = _split_copies(kind, [src_refs[i] for i in grp], [land_refs[i] for i in grp], sems[2 * gi], sems[2 * gi + 1])
            for cp in cps:
                cp.start()
        token[...] = jnp.zeros_like(token)

    arrays = list(srcs) + list(lands)
    out_shape = []
    for grp in groups:
        out_shape += [pltpu.SemaphoreType.DMA((3 * len(grp),)), pltpu.SemaphoreType.DMA((3 * len(grp),))]
    out_shape += [pltpu.HBM(a.shape, a.dtype) for a in arrays] + [jax.ShapeDtypeStruct((8, LANE), F32)]
    outs = pl.pallas_call(
        body, name=name, out_shape=out_shape, in_specs=[HBM] * (2 * n),
        out_specs=[SEM] * (2 * ng) + [HBM] * (2 * n) + [pl.BlockSpec(memory_space=pltpu.VMEM)],
        input_output_aliases={i: 2 * ng + i for i in range(2 * n)},
        compiler_params=pltpu.CompilerParams(has_side_effects=EFFECT),
    )(*[pltpu.with_memory_space_constraint(a, pltpu.HBM) for a in arrays])
    sems = [(outs[2 * gi], outs[2 * gi + 1]) for gi in range(ng)]
    thru = outs[2 * ng:2 * ng + 2 * n]
    return sems, thru[:n], thru[n:], outs[-1]


def _exchange_wait(name, kind, srcs, lands, sems, after):
    n = len(srcs)

    def body(*refs):
        cps = _split_copies(kind, refs[:n], refs[n:2 * n], refs[2 * n], refs[2 * n + 1])
        for cp in cps:
            cp.wait_send()
            cp.wait_recv()

    arrays = list(srcs) + list(lands)
    outs = pl.pallas_call(
        body, name=name, out_shape=[pltpu.HBM(a.shape, a.dtype) for a in arrays],
        in_specs=[HBM] * (2 * n) + [SEM, SEM, ANY], out_specs=[HBM] * (2 * n),
        input_output_aliases={i: i for i in range(2 * n)},
        compiler_params=pltpu.CompilerParams(has_side_effects=EFFECT),
    )(*arrays, sems[0], sems[1], after)
    return outs[:n], outs[n:]


EXCHANGE_CHUNK_BYTES = 3 * 1024 * 1024


def _half_geometry(R, C, axis):
    Rp, Cp = (R // 2, C) if axis == 0 else (R, C // 2)
    tr = _tile_rows(Rp, max(16, EXCHANGE_CHUNK_BYTES // (2 * Cp)), 16)
    return Rp, Cp, tr, Rp // tr


def _pair_sum(name, g, axis):
    G, R, C = g.shape
    Rp, Cp, tr, nb = _half_geometry(R, C, axis)
    steps = G * nb

    def half_block(t, h):
        s, b = t // nb, t % nb
        return (s, h * nb + b, 0) if axis == 0 else (s, b, h)

    def body(c_ref, keep_ref, give_ref, out_ref, land, ssem, rsem, credit):
        x, y, c = lax.axis_index("x"), lax.axis_index("y"), lax.axis_index("c")
        sib = (x, y, 1 - c)
        t = pl.program_id(0)

        def copy(chunk):
            return _rcopy(give_ref.at[0], land.at[chunk % 2], ssem.at[chunk % 2], rsem.at[chunk % 2], sib)

        @pl.when((t >= 2) & (t < steps))
        def _():
            pl.semaphore_wait(credit, 1)

        @pl.when(t < steps)
        def _():
            copy(t).start()

        @pl.when(t >= 1)
        def _():
            copy(t - 1).wait_recv()
            out_ref[...] = (keep_ref[...].astype(F32) + land[(t - 1) % 2].astype(F32)).astype(BF)

            @pl.when(t + 1 < steps)
            def _():
                pl.semaphore_signal(credit, 1, device_id=sib, device_id_type=MESH)

        @pl.when(t < steps)
        def _():
            copy(t).wait_send()

    blk = (None, tr, Cp)
    grid_spec = pltpu.PrefetchScalarGridSpec(
        num_scalar_prefetch=1, grid=(steps + 1,),
        in_specs=[pl.BlockSpec(blk, lambda t, c_ref: half_block(jnp.maximum(t - 1, 0), c_ref[0])),
                  pl.BlockSpec((1, tr, Cp), lambda t, c_ref: half_block(jnp.minimum(t, steps - 1), 1 - c_ref[0]))],
        out_specs=pl.BlockSpec(blk, lambda t, c_ref: (jnp.maximum(t - 1, 0) // nb, jnp.maximum(t - 1, 0) % nb, 0)),
        scratch_shapes=[pltpu.VMEM((2, tr, Cp), BF), pltpu.SemaphoreType.DMA((2,)), pltpu.SemaphoreType.DMA((2,)),
                        pltpu.SemaphoreType.REGULAR])
    c_arr = lax.axis_index("c").astype(jnp.int32).reshape(1)
    return pl.pallas_call(
        body, name=name, grid_spec=grid_spec, out_shape=jax.ShapeDtypeStruct((G, Rp, Cp), BF),
        compiler_params=_params("arbitrary"))(c_arr, g, g)


def _chip_total_join(name, h, landed, axis):
    G, Rp, Cp = h.shape
    R, C = (2 * Rp, Cp) if axis == 0 else (Rp, 2 * Cp)
    tr = _tile_rows(Rp, max(16, EXCHANGE_CHUNK_BYTES // (4 * Cp)), 16)
    nb = Rp // tr

    def body(me_ref, own_ref, l0_ref, l1_ref, l2_ref, full, stage, ssem, rsem, lsem):
        x, y, c = lax.axis_index("x"), lax.axis_index("y"), lax.axis_index("c")
        sib = (x, y, 1 - c)
        b = pl.program_id(0)

        def place(half, r0, rows):
            if axis == 0:
                return full.at[pl.ds(pl.multiple_of(half * Rp + r0, 8), rows), :]
            return full.at[pl.ds(pl.multiple_of(r0, 8), rows), pl.ds(pl.multiple_of(half * Cp, LANE), Cp)]

        def copies(step):
            s = step % 2
            mine = place(c, step * tr, tr)
            return pltpu.make_async_copy(stage.at[s], mine, lsem.at[s]), _rcopy(stage.at[s], mine, ssem.at[s], rsem, sib)

        @pl.when(b >= 2)
        def _():
            loc, rem = copies(b - 2)
            loc.wait()
            rem.wait_send()

        acc = own_ref[...].astype(F32)
        for r in (l0_ref, l1_ref, l2_ref):
            acc = acc + r[...].astype(F32)
        stage[b % 2] = acc
        loc, rem = copies(b)
        loc.start()
        rem.start()

        @pl.when(b == nb - 1)
        def _():
            for step in range(max(0, nb - 2), nb):
                loc, rem = copies(step)
                loc.wait()
                rem.wait_send()
            theirs = place(1 - c, 0, Rp)
            _rcopy(theirs, theirs, ssem.at[0], rsem, sib).wait_recv()

    blk = (None, tr, Cp)
    grid_spec = pltpu.PrefetchScalarGridSpec(
        num_scalar_prefetch=1, grid=(nb,),
        in_specs=[pl.BlockSpec(blk, lambda b, me_ref: (me_ref[0], b, 0))]
        + [pl.BlockSpec(blk, functools.partial(lambda b, me_ref, k: (k, b, 0), k=k)) for k in range(3)],
        out_specs=ANY,
        scratch_shapes=[pltpu.VMEM((2, tr, Cp), F32), pltpu.SemaphoreType.DMA((2,)), pltpu.SemaphoreType.DMA,
                        pltpu.SemaphoreType.DMA((2,))])
    me = (2 * lax.axis_index("x") + lax.axis_index("y")).astype(jnp.int32).reshape(1)
    return pl.pallas_call(
        body, name=name, grid_spec=grid_spec, out_shape=jax.ShapeDtypeStruct((R, C), F32),
        compiler_params=_params("arbitrary"))(me, h, landed, landed, landed)


def _pair_share(name, land):
    G, R, C = land.shape
    Rh = R // 2
    tr = _tile_rows(Rh, max(16, EXCHANGE_CHUNK_BYTES // (2 * C)), 16)
    chunks = [(k, b) for k in range(3) for b in range(Rh // tr)]

    def body(src, dst, buf, lsem, ssem, rsem):
        x, y, c, _, _, chip_idx = _place()
        sib = (x, y, 1 - c)

        def region(ref, k, half, r0, rows):
            return ref.at[chip_idx[k], pl.ds(pl.multiple_of(half * Rh + r0, 16), rows)]

        def load(t):
            k, b = chunks[t]
            return pltpu.make_async_copy(region(src, k, c, b * tr, tr), buf.at[t % 2], lsem.at[t % 2])

        def send(t):
            k, b = chunks[t]
            return _rcopy(buf.at[t % 2], region(dst, k, c, b * tr, tr), ssem.at[t % 2], rsem.at[k], sib)

        load(0).start()
        for t in range(len(chunks)):
            load(t).wait()
            if t + 1 < len(chunks):
                if t >= 1:
                    send(t - 1).wait_send()
                load(t + 1).start()
            send(t).start()
        for t in range(max(0, len(chunks) - 2), len(chunks)):
            send(t).wait_send()
        for k in range(3):
            theirs = region(dst, k, 1 - c, 0, Rh)
            _rcopy(theirs, theirs, ssem.at[0], rsem.at[k], sib).wait_recv()

    return pl.pallas_call(
        body, name=name, in_specs=[ANY], out_specs=ANY, out_shape=jax.ShapeDtypeStruct(land.shape, land.dtype),
        input_output_aliases={0: 0},
        scratch_shapes=[pltpu.VMEM((2, tr, C), land.dtype), pltpu.SemaphoreType.DMA((2,)), pltpu.SemaphoreType.DMA((2,)),
                        pltpu.SemaphoreType.DMA((3,))],
    )(land)


def _allreduce_small(v):
    R, K = v.shape
    ndev = 8

    def body(v_ref, o_ref, land, ssem, rsem):
        x, y, c = lax.axis_index("x"), lax.axis_index("y"), lax.axis_index("c")
        me = 4 * x + 2 * y + c
        land[me] = v_ref[...]
        cps = []
        for r in range(1, ndev):
            fx, fy, fc = (r >> 2) & 1, (r >> 1) & 1, r & 1
            peer = (x ^ fx, y ^ fy, c ^ fc)
            cp = _rcopy(v_ref, land.at[me], ssem.at[r - 1], rsem.at[r - 1], peer)
            cp.start()
            cps.append((cp, 4 * peer[0] + 2 * peer[1] + peer[2], r))
        for cp, src, r in cps:
            cp.wait_send()
            _rcopy(v_ref, land.at[src], ssem.at[r - 1], rsem.at[r - 1], (x, y, c)).wait_recv()
        acc = land[0]
        for d in range(1, ndev):
            acc = acc + land[d]
        o_ref[...] = acc

    vm = pl.BlockSpec(memory_space=pltpu.VMEM)
    return pl.pallas_call(
        body, name="allreduce_small", in_specs=[vm], out_specs=vm, out_shape=jax.ShapeDtypeStruct((R, K), F32),
        scratch_shapes=[pltpu.VMEM((ndev, R, K), F32), pltpu.SemaphoreType.DMA((ndev - 1,)), pltpu.SemaphoreType.DMA((ndev - 1,))],
    )(v)


IN_SPLITS = (Q_RANK, KV_RANK, QK_ROPE, DIL_HEADS * HEAD, DIL_HEADS * HEAD, DIL_HEADS * HEAD, D_MODEL, D_MODEL)
IN_OFF = tuple(int(v) for v in np.cumsum((0,) + IN_SPLITS))


def _unshard_cols(g):
    G, K, Ns = g.shape
    return g.transpose(1, 0, 2).reshape(K, G * Ns)


def _shard_cols(w):
    K, N = w.shape
    return w.reshape(K, N_CHIPS, N // N_CHIPS).transpose(1, 0, 2)


def _rope_pad(w):
    half = QK_ROPE // 2
    z = jnp.zeros(w.shape[:-1] + (half,), w.dtype)
    return jnp.concatenate([w[..., :half], z, w[..., half:], z], axis=-1)


def _rope_unpad(w):
    half = QK_ROPE // 2
    return jnp.concatenate([w[..., :half], w[..., 2 * half:3 * half]], axis=-1)


def _split_w_in(w_in_g):
    G, K, Ns = w_in_g.shape

    def cols(lo, hi):
        pieces = [w_in_g[k][:, max(lo, k * Ns) - k * Ns:min(hi, (k + 1) * Ns) - k * Ns]
                  for k in range(G) if max(lo, k * Ns) < min(hi, (k + 1) * Ns)]
        return pieces[0] if len(pieces) == 1 else jnp.concatenate(pieces, axis=1)

    p = [(IN_OFF[i], IN_OFF[i + 1]) for i in range(8)]
    w_lat = jnp.concatenate([cols(*p[0]), cols(*p[1]), _rope_pad(cols(*p[2])),
                             jnp.zeros((K, LAT_W - _KPE.stop), w_in_g.dtype)], axis=1)
    w_dil = [jnp.concatenate([cols(p[3 + t][0] + g * DIL_O, p[3 + t][0] + (g + 1) * DIL_O) for t in range(3)], axis=1)
             for g in range(DIL_GROUPS)]
    w_gate = cols(p[6][0], p[7][1])
    return w_lat, w_dil, w_gate


def _merge_dw_in(dw_lat, dw_dil, dw_gate):
    parts = [dw_lat[:, _CQ], dw_lat[:, _CKV], _rope_unpad(dw_lat[:, _KPE])]
    for t in range(3):
        parts += [dw_dil[g][:, t * DIL_O:(t + 1) * DIL_O] for g in range(DIL_GROUPS)]
    parts.append(dw_gate)
    width = sum(p.shape[1] for p in parts) // N_CHIPS
    shards = []
    for k in range(N_CHIPS):
        pieces, at = [], 0
        for p in parts:
            lo, hi = max(k * width, at), min((k + 1) * width, at + p.shape[1])
            if lo < hi:
                pieces.append(p[:, lo - at:hi - at])
            at += p.shape[1]
        shards.append(jnp.concatenate(pieces, axis=1))
    return jnp.stack(shards)


def _split_w_uq(w_uq_g):
    w = _unshard_cols(w_uq_g)
    K = w.shape[0]
    w = w.reshape(K, MLA_HEADS, QK_NOPE + QK_ROPE)
    return w[:, :, :QK_NOPE].reshape(K, MLA_HEADS * HEAD), _rope_pad(w[:, :, QK_NOPE:]).reshape(K, MLA_HEADS * HEAD)


def _merge_dw_uq(dw_n, dw_p):
    K = dw_n.shape[0]
    w = jnp.concatenate([dw_n.reshape(K, MLA_HEADS, HEAD), _rope_unpad(dw_p.reshape(K, MLA_HEADS, HEAD))], axis=-1)
    return _shard_cols(w.reshape(K, MLA_HEADS * (QK_NOPE + QK_ROPE)))


def _split_w_ukv(w_ukv_g):
    w = _unshard_cols(w_ukv_g)
    K = w.shape[0]
    w = w.reshape(K, MLA_HEADS, 2 * HEAD)
    return w[:, :, :HEAD].reshape(K, MLA_HEADS * HEAD), w[:, :, HEAD:].reshape(K, MLA_HEADS * HEAD)


def _merge_dw_ukv(dw_k, dw_v):
    K = dw_k.shape[0]
    w = jnp.concatenate([dw_k.reshape(K, MLA_HEADS, HEAD), dw_v.reshape(K, MLA_HEADS, HEAD)], axis=-1)
    return _shard_cols(w.reshape(K, MLA_HEADS * 2 * HEAD))


GATHER_GROUPS = (("w_in",), ("w_uq", "w_ukv", "w_o_mla", "w_o_dil", "w_out"), ("w_up", "w_down", "conv_w"))
SHARED_FETCH = ("w_in",)
REDUCE_GROUPS = (("w_down", "w_up"), ("w_out", "w_o_mla", "w_o_dil"), ("w_uq", "w_ukv", "w_in"))


def _local_step(x, tgt, W, fetch, emit):
    S, D = x.shape
    cos, sin_s = _rope_tables(S)
    w_lat, w_dil, w_gate = _split_w_in(fetch(0, x)["w_in"])

    h = _rmsnorm_fwd("attn_norm", x, W["attn_norm_g"])
    lat = _mm_nn("proj_lat", h, w_lat)
    qkv = [_mm_nn(f"proj_dil{g}", h, w_dil[g], o_dtype=BF) for g in range(DIL_GROUPS)]
    gpre = _mm_nn("proj_gate", h, w_gate, o_dtype=BF)
    WB = fetch(1, gpre)
    w_uqn, w_uqp = _split_w_uq(WB["w_uq"])
    w_k, w_v = _split_w_ukv(WB["w_ukv"])
    w_o_mla, w_o_dil = WB["w_o_mla"], WB["w_o_dil"]
    w_out = WB["w_out"].reshape(D, D)
    qn_, kvn, kpe = _mla_prep(lat, W["q_norm_g"], W["kv_norm_g"], cos, sin_s)
    q_nope = _mm_nn("q_nope", qn_, w_uqn, o_dtype=BF)
    q_pe = _rope("q_rope", _mm_nn("q_pe", qn_, w_uqp), cos, sin_s, False)
    k_nope = _mm_nn("k_nope", kvn, w_k, o_dtype=BF)
    v_mla = _mm_nn("v_mla", kvn, w_v, o_dtype=BF)
    attn_a, lse_a = _mla_fwd(q_nope, q_pe, k_nope, kpe, v_mla)
    dil = [_dil_fwd(qkv[g], g) for g in range(DIL_GROUPS)]
    attn_b, lse_b = _dil_combine([o for o, _ in dil], [l for _, l in dil])
    o_a = _mm_nn("o_mla", attn_a, w_o_mla, o_dtype=BF)
    o_b = _mm_nn("o_dil", attn_b, w_o_dil, o_dtype=BF)
    merge = _merge_fwd(gpre, W["b_gate"], o_a, o_b)
    x1 = _mm_nn("out_proj", merge, w_out, add=x)
    WC = fetch(2, merge)
    w_up = WC["w_up"]
    G4, _, C = w_up.shape
    w_down = WC["w_down"].reshape(G4 // 2, C, D)
    conv_w = WC["conv_w"]
    conv_b = W["conv_b"].reshape(G4, 1, C)
    h2 = _rmsnorm_fwd("ffn_norm", x1, W["ffn_norm_g"])
    u_pre = _up_fwd(h2, w_up)
    act, u = _ffn_act(u_pre, conv_w, conv_b)
    x2 = _down_fwd(act, w_down, x1)
    dx2, dx2_b, d_final_g, loss8 = _final_loss(x2, tgt, W["final_norm_g"])

    d_act = _down_dgrad(dx2_b, w_down)
    dw_down = _down_wgrad(act, dx2_b)
    du_pre, d_conv_w, d_conv_b = _ffn_act_conv_bwd(u, d_act, u_pre, conv_w)
    dh2 = _up_dgrad(du_pre, w_up)
    dw_up = _up_wgrad(h2, du_pre)
    zero = emit(0, {"w_down": dw_down.reshape(N_CHIPS, (G4 // 2) * C // N_CHIPS, D), "w_up": dw_up})
    dx1, dx1_b, d_ffn_g = _rmsnorm_bwd("ffn_norm_bwd", dh2, x1, W["ffn_norm_g"] + zero, dx2)
    d_merge = _mm_nt("out_proj_dgrad", dx1_b, w_out, o_dtype=BF)
    dw_out = _mm_tn("out_proj_wgrad", merge, dx1_b)
    d_oa, d_ob, d_gpre, d_b_gate = _merge_bwd(d_merge, gpre, W["b_gate"], o_a, o_b)
    d_attn_a = _mm_nt("o_mla_dgrad", d_oa, w_o_mla, o_dtype=BF)
    dw_o_mla = _mm_tn("o_mla_wgrad", attn_a, d_oa, shards=N_CHIPS)
    d_attn_b = _mm_nt("o_dil_dgrad", d_ob, w_o_dil, o_dtype=BF)
    dw_o_dil = _mm_tn("o_dil_wgrad", attn_b, d_ob, shards=N_CHIPS)
    zero = emit(1, {"w_out": dw_out.reshape(N_CHIPS, D // N_CHIPS, D), "w_o_mla": dw_o_mla, "w_o_dil": dw_o_dil})
    q_norm_g = W["q_norm_g"] + zero
    delta_b = _dil_delta(d_attn_b, attn_b)
    d_qkv = [_dil_bwd(qkv[g], d_attn_b, lse_b, delta_b, g) for g in range(DIL_GROUPS)]
    dq_nope, dq_pe_rot, dk_nope, dv_mla, dkpe_rot = _mla_bwd(q_nope, q_pe, k_nope, kpe, v_mla, d_attn_a, attn_a, lse_a)
    dq_pe = _rope("q_rope_bwd", dq_pe_rot, cos, sin_s, True)
    d_qn = _mm_nt_sum("q_dgrad", [dq_nope, dq_pe], [w_uqn, w_uqp])
    d_kvn = _mm_nt_sum("kv_dgrad", [dk_nope, dv_mla], [w_k, w_v])
    dw_uq = _merge_dw_uq(_mm_tn("q_nope_wgrad", qn_, dq_nope), _mm_tn("q_pe_wgrad", qn_, dq_pe))
    dw_ukv = _merge_dw_ukv(_mm_tn("k_nope_wgrad", kvn, dk_nope), _mm_tn("v_wgrad", kvn, dv_mla))
    d_lat, d_q_g, d_kv_g = _mla_prep_bwd(lat, q_norm_g, W["kv_norm_g"], cos, sin_s, d_qn, d_kvn, dkpe_rot)
    dw_in = _merge_dw_in(_mm_tn("proj_lat_wgrad", h, d_lat),
                         [_mm_tn(f"proj_dil{g}_wgrad", h, d_qkv[g]) for g in range(DIL_GROUPS)],
                         _mm_tn("proj_gate_wgrad", h, d_gpre))
    zero = emit(2, {"w_uq": dw_uq, "w_ukv": dw_ukv, "w_in": dw_in})
    dh = _mm_nt_sum("proj_lat_dil_dgrad", [d_lat] + d_qkv, [w_lat + zero.astype(BF)] + w_dil)
    dh = _mm_nt("proj_gate_dgrad", d_gpre, w_gate, add=dh)
    grad_x, _, d_attn_g = _rmsnorm_bwd("attn_norm_bwd", dh, x, W["attn_norm_g"], dx1)

    small = {"attn_norm_g": d_attn_g, "b_gate": d_b_gate, "q_norm_g": d_q_g, "kv_norm_g": d_kv_g,
             "ffn_norm_g": d_ffn_g, "conv_w": d_conv_w, "conv_b": d_conv_b.reshape(1, G4 * C),
             "final_norm_g": d_final_g}
    return loss8[0, 0], grad_x, small


BIG = ("w_in", "w_uq", "w_ukv", "w_o_mla", "w_o_dil", "w_out", "w_up", "w_down")
SMALL = ("attn_norm_g", "b_gate", "q_norm_g", "kv_norm_g", "ffn_norm_g", "conv_w", "conv_b", "final_norm_g")
WEIGHTS = ("attn_norm_g", "w_in", "b_gate", "q_norm_g", "w_uq", "kv_norm_g", "w_ukv", "w_o_mla", "w_o_dil",
           "w_out", "ffn_norm_g", "w_up", "conv_w", "conv_b", "w_down", "final_norm_g")
SMALL_ROWS = 8
COLUMN_MAJOR = ("w_in", "w_up")
HALF_AXIS = {"w_down": 1}


def _gather_start(shards):
    chip = 2 * lax.axis_index("x") + lax.axis_index("y")
    c = lax.axis_index("c")

    def prepare(names, zero):
        srcs, lands = [], []
        for n in names:
            if n in COLUMN_MAJOR:
                s = lax.optimization_barrier((shards[n].T + zero).astype(BF).T)
            else:
                s = shards[n] + zero
                s = s if n == "conv_w" else s.astype(BF)
            lands.append(lax.dynamic_update_slice(lax.empty((N_CHIPS,) + s.shape, s.dtype), s[None], (chip, 0, 0)))
            if n in SHARED_FETCH:
                s = lax.dynamic_slice_in_dim(s, c * (s.shape[0] // 2), s.shape[0] // 2, 0)
            srcs.append(s)
        return srcs, lands

    n0 = len(GATHER_GROUPS[0])
    srcs0, lands0 = prepare(GATHER_GROUPS[0], 0.0)
    sems0, srcs0, lands0, token = _exchange_start("gather_start0", "gather", srcs0, lands0, [list(range(n0))])
    srcs, lands = prepare([n for grp in GATHER_GROUPS[1:] for n in grp], token[0, 0])
    groups, at = [], 0
    for grp in GATHER_GROUPS[1:]:
        groups.append(list(range(at, at + len(grp))))
        at += len(grp)
    sems, srcs, lands, token1 = _exchange_start("gather_start1", "gather", srcs, lands, groups)

    def fetch(i, after):
        if i == 0:
            _, got = _exchange_wait("gather_wait0", "gather", srcs0, lands0, sems0[0], token1)
        else:
            idx = groups[i - 1]
            _, got = _exchange_wait(f"gather_wait{i}", "gather", [srcs[j] for j in idx], [lands[j] for j in idx],
                                    sems[i - 1], after)
        return {n: _pair_share(f"pair_share_{n}", g) if n in SHARED_FETCH else g for n, g in zip(GATHER_GROUPS[i], got)}

    return fetch, token[0, 0]


def _reduce_start(i, grads):
    names = REDUCE_GROUPS[i]
    hs = [_pair_sum(f"pair_sum_{n}", grads[n], HALF_AXIS.get(n, 0)) for n in names]
    lands = [lax.empty((3,) + h.shape[1:], h.dtype) for h in hs]
    sems, hs, lands, token = _exchange_start(f"reduce_start{i}", "scatter", hs, lands, [list(range(len(names)))])
    return (sems[0], hs, lands), token[0, 0]


def _reduce_finish(i, pending, after):
    sems, hs, lands = pending
    hs, lands = _exchange_wait(f"reduce_wait{i}", "scatter", hs, lands, sems, after)
    out = {}
    for n, h, landed in zip(REDUCE_GROUPS[i], hs, lands):
        out[n] = _chip_total_join(f"chip_total_{n}", h, landed, HALF_AXIS.get(n, 0))
    return out


def _reduce_small(small):
    names = tuple(small)
    flat = [small[n].reshape(-1) for n in names]
    sizes = [f.shape[0] for f in flat]
    total = sum(sizes)
    width = -(-total // (SMALL_ROWS * LANE)) * LANE
    packed = jnp.concatenate(flat + [jnp.zeros((SMALL_ROWS * width - total,), F32)]).reshape(SMALL_ROWS, width)
    red = _allreduce_small(packed).reshape(-1)
    out, off = {}, 0
    for n, s in zip(names, sizes):
        out[n] = red[off:off + s]
        off += s
    return out


def kernel(x, attn_norm_g, w_in, b_gate, q_norm_g, w_uq, kv_norm_g, w_ukv, w_o_mla, w_o_dil, w_out, ffn_norm_g, w_up, conv_w, conv_b, w_down, final_norm_g, loss_target, m_attn_norm_g, m_w_in, m_b_gate, m_q_norm_g, m_w_uq, m_kv_norm_g, m_w_ukv, m_w_o_mla, m_w_o_dil, m_w_out, m_ffn_norm_g, m_w_up, m_conv_w, m_conv_b, m_w_down, m_final_norm_g, v_attn_norm_g, v_w_in, v_b_gate, v_q_norm_g, v_w_uq, v_kv_norm_g, v_w_ukv, v_w_o_mla, v_w_o_dil, v_w_out, v_ffn_norm_g, v_w_up, v_conv_w, v_conv_b, v_w_down, v_final_norm_g):
    given = dict(attn_norm_g=attn_norm_g, w_in=w_in, b_gate=b_gate, q_norm_g=q_norm_g, w_uq=w_uq, kv_norm_g=kv_norm_g,
                 w_ukv=w_ukv, w_o_mla=w_o_mla, w_o_dil=w_o_dil, w_out=w_out, ffn_norm_g=ffn_norm_g, w_up=w_up,
                 conv_w=conv_w, conv_b=conv_b, w_down=w_down, final_norm_g=final_norm_g)
    moments_m = dict(attn_norm_g=m_attn_norm_g, w_in=m_w_in, b_gate=m_b_gate, q_norm_g=m_q_norm_g, w_uq=m_w_uq,
                     kv_norm_g=m_kv_norm_g, w_ukv=m_w_ukv, w_o_mla=m_w_o_mla, w_o_dil=m_w_o_dil, w_out=m_w_out,
                     ffn_norm_g=m_ffn_norm_g, w_up=m_w_up, conv_w=m_conv_w, conv_b=m_conv_b, w_down=m_w_down,
                     final_norm_g=m_final_norm_g)
    moments_v = dict(attn_norm_g=v_attn_norm_g, w_in=v_w_in, b_gate=v_b_gate, q_norm_g=v_q_norm_g, w_uq=v_w_uq,
                     kv_norm_g=v_kv_norm_g, w_ukv=v_w_ukv, w_o_mla=v_w_o_mla, w_o_dil=v_w_o_dil, w_out=v_w_out,
                     ffn_norm_g=v_ffn_norm_g, w_up=v_w_up, conv_w=v_conv_w, conv_b=v_conv_b, w_down=v_w_down,
                     final_norm_g=v_final_norm_g)

    fetch, zero = _gather_start({n: given[n][0] for n in BIG + ("conv_w",)})
    W = {n: given[n] for n in ("b_gate", "q_norm_g", "kv_norm_g", "ffn_norm_g", "conv_b")}
    W["attn_norm_g"] = given["attn_norm_g"] + zero
    W["final_norm_g"] = given["final_norm_g"].reshape(1, -1)

    pending = {}

    def emit(i, grads):
        pending[i], token = _reduce_start(i, grads)
        return token

    loss_part, grad_x, small = _local_step(x[0], loss_target[0], W, fetch, emit)
    small["loss"] = loss_part
    grads, delta, new_m, new_v = {}, {}, {}, {}

    def adamw(n, g):
        shp = given[n].shape
        two_d = (-1, shp[-1]) if len(shp) > 1 else (1, -1)
        view = (lambda a: a.reshape(two_d).T) if n in COLUMN_MAJOR else (lambda a: a.reshape(two_d))
        back = (lambda a: a.T.reshape(shp)) if n in COLUMN_MAJOR else (lambda a: a.reshape(shp))
        go, d, nm, nv = _adamw(f"adamw_{n}", view(given[n]), view(g), view(moments_m[n]), view(moments_v[n]))
        grads[n], delta[n], new_m[n], new_v[n] = back(go), back(d), back(nm), back(nv)

    after = grad_x
    for i in range(len(REDUCE_GROUPS)):
        for n, g in _reduce_finish(i, pending[i], after).items():
            adamw(n, g)
        after = delta[REDUCE_GROUPS[i][-1]]
    g_small = _reduce_small(small)
    loss = g_small["loss"][0]
    chip = 2 * lax.axis_index("x") + lax.axis_index("y")
    for n in SMALL:
        if n == "conv_w":
            full = g_small[n].reshape(N_CHIPS, 3, -1)
            adamw(n, lax.dynamic_index_in_dim(full, chip, 0, keepdims=True))
        else:
            adamw(n, g_small[n])

    return (loss, grad_x[None], *[grads[n] for n in WEIGHTS], *[delta[n] for n in WEIGHTS],
            *[new_m[n] for n in WEIGHTS], *[new_v[n] for n in WEIGHTS])
```
